```python
import math
import jax, jax.numpy as jnp
from jax import lax
import numpy as np

D_MODEL = 1024
BATCH = 8
SEQ = 2048
DEPTH = 1

RET_HEADS = 4
RET_DK = 256
RET_DV = 512
RET_CHUNK = 128
RET_QK_W = RET_HEADS * RET_DK
RET_V_W = RET_HEADS * RET_DV
ATT_GROUPS = ((128, 1), (512, 4), (2048, 16))
N_ATT_GROUPS = len(ATT_GROUPS)
ATT_HEADS_PER_GROUP = 4
ATT_HEAD_DIM = 128
ATT_GROUP_W = ATT_HEADS_PER_GROUP * ATT_HEAD_DIM
N_ATT_HEADS = N_ATT_GROUPS * ATT_HEADS_PER_GROUP
REL_BUCKETS = 32
REL_MAX_DIST = 2048
D_FF = 4 * D_MODEL
N_BRANCHES = 2
RMS_EPS = 1e-6
GN_EPS = 1e-5
ROPE_BASE = 10000.0

IN_SIZES = ([RET_QK_W, RET_QK_W, RET_V_W, RET_V_W]
            + [ATT_GROUP_W] * (3 * N_ATT_GROUPS)
            + [D_MODEL] * N_BRANCHES)
IN_COLS = sum(IN_SIZES)
IN_OFFSETS = [sum(IN_SIZES[:i + 1]) for i in range(len(IN_SIZES) - 1)]

kernel_name = "hybrid_retention_dilated_attn_block"


def rms_norm(x, g):
    xf = x.astype(jnp.float32)
    y = xf * lax.rsqrt(jnp.mean(xf * xf, axis=-1, keepdims=True) + RMS_EPS)
    return (y * g.astype(jnp.float32)).astype(x.dtype)


def modulate(h, shift, scale):
    return h * (1 + scale[:, None, :]) + shift[:, None, :]


def t5_bucket(dist):
    max_exact = REL_BUCKETS // 2
    d_f = jnp.maximum(dist, 1).astype(jnp.float32)
    large = max_exact + (jnp.log(d_f / max_exact) / math.log(REL_MAX_DIST / max_exact)
                         * (REL_BUCKETS - max_exact)).astype(jnp.int32)
    large = jnp.minimum(large, REL_BUCKETS - 1)
    return jnp.where(dist < max_exact, dist, large)


def rotary(x, pos):
    half = x.shape[-1] // 2
    inv = ROPE_BASE ** (-jnp.arange(half, dtype=jnp.float32) / half)
    ang = pos.astype(jnp.float32)[:, None] * inv[None, :]
    cos, sin = jnp.cos(ang).astype(x.dtype), jnp.sin(ang).astype(x.dtype)
    x1, x2 = x[..., :half], x[..., half:]
    return jnp.concatenate([x1 * cos - x2 * sin, x1 * sin + x2 * cos], axis=-1)


def retention(q, k, v):
    B, H, S, dk = q.shape
    dv = v.shape[-1]
    C = RET_CHUNK
    nc = S // C
    log_g = jnp.log1p(-(2.0 ** (-5.0 - jnp.arange(H, dtype=jnp.float32))))
    idx = jnp.arange(C, dtype=jnp.float32)
    rel = idx[:, None] - idx[None, :]
    inner_decay = jnp.where(rel >= 0, jnp.exp(log_g[:, None, None] * jnp.maximum(rel, 0.0)), 0.0)
    q_decay = jnp.exp(log_g[:, None] * (idx + 1.0))
    k_decay = jnp.exp(log_g[:, None] * (C - 1.0 - idx))
    chunk_decay = jnp.exp(log_g * C)

    def to_chunks(t):
        return jnp.moveaxis(t.astype(jnp.float32).reshape(B, H, nc, C, t.shape[-1]), 2, 0)

    qc, kc, vc = to_chunks(q), to_chunks(k), to_chunks(v)

    def step(state, inp):
        qi, ki, vi = inp
        s = jnp.einsum('bhid,bhjd->bhij', qi, ki) * inner_decay[None]
        o = (jnp.einsum('bhij,bhje->bhie', s, vi)
             + jnp.einsum('bhid,bhde->bhie', qi, state) * q_decay[None, :, :, None])
        state = (state * chunk_decay[None, :, None, None]
                 + jnp.einsum('bhjd,bhje->bhde', ki * k_decay[None, :, :, None], vi))
        return state, o

    state0 = jnp.zeros((B, H, dk, dv), jnp.float32)
    _, o = lax.scan(step, state0, (qc, kc, vc))
    return jnp.moveaxis(o, 0, 2).reshape(B, H, S, dv)


def dilated_group(q, k, v, bias_tab, window, dilation):
    B, H, S, dh = q.shape
    w = window // dilation
    blk = w
    span = dilation * blk
    Sp = -(-S // span) * span
    L = Sp // dilation
    nb = L // blk

    def split(t):
        t = jnp.pad(t, ((0, 0), (0, 0), (0, Sp - S), (0, 0)))
        t = t.reshape(B, H, L, dilation, dh).transpose(0, 1, 3, 2, 4)
        return t.reshape(B, H, dilation, nb, blk, dh)

    qs, ks, vs = split(q), split(k), split(v)

    def with_prev(t):
        prev = jnp.pad(t[:, :, :, :-1], ((0, 0), (0, 0), (0, 0), (1, 0), (0, 0), (0, 0)))
        return jnp.concatenate([prev, t], axis=4)

    kb, vb = with_prev(ks), with_prev(vs)
    qi = jnp.arange(blk)[:, None]
    kj = jnp.arange(2 * blk)[None, :]
    m = blk + qi - kj
    band = (m >= 0) & (m <= w)
    first_ok = kj >= blk
    valid = band[None] & ((jnp.arange(nb)[:, None, None] > 0) | first_ok[None])
    bias = bias_tab[t5_bucket(jnp.clip(m, 0, w) * dilation)]
    bias = jnp.moveaxis(bias, -1, 0).astype(jnp.float32)

    s = (jnp.einsum('bhrnid,bhrnjd->bhrnij', qs, kb).astype(jnp.float32) * (dh ** -0.5)
         + bias[None, :, None, None])
    s = jnp.where(valid[None, None, None], s, -1e30)
    mx = jnp.max(s, axis=-1, keepdims=True)
    e = jnp.exp(s - mx)
    den = jnp.sum(e, axis=-1, keepdims=True)
    p = (e / den).astype(v.dtype)
    lse = (mx + jnp.log(den))[..., 0]
    o = jnp.einsum('bhrnij,bhrnjd->bhrnid', p, vb)

    def merge(t):
        t = t.reshape(B, H, dilation, L, *t.shape[5:])
        t = jnp.swapaxes(t, 2, 3)
        t = t.reshape(B, H, Sp, *t.shape[4:])
        return t[:, :, :S]

    return merge(o), merge(lse)


def token_mixer(h, w_in, rel_bias, gn_g, gn_b, w_ret_out, w_att_out, w_o):
    B, S, _ = h.shape
    proj = h @ w_in
    parts = jnp.split(proj, IN_OFFSETS, axis=-1)

    def heads(t, n):
        return t.reshape(B, S, n, -1).transpose(0, 2, 1, 3)

    pos = jnp.arange(S)
    rq, rk, rv, rg = parts[0], parts[1], parts[2], parts[3]
    rq = rotary(heads(rq, RET_HEADS), pos)
    rk = rotary(heads(rk, RET_HEADS), pos) * (RET_DK ** -0.5)
    ro = retention(rq, rk, heads(rv, RET_HEADS))
    mu = jnp.mean(ro, axis=-1, keepdims=True)
    var = jnp.mean(jnp.square(ro - mu), axis=-1, keepdims=True)
    ro = ((ro - mu) * lax.rsqrt(var + GN_EPS)).transpose(0, 2, 1, 3).reshape(B, S, RET_V_W)
    ro = (ro * gn_g.astype(jnp.float32) + gn_b.astype(jnp.float32)).astype(h.dtype)
    ret_out = (jax.nn.silu(rg) * ro) @ w_ret_out

    outs, lses = [], []
    for gi, (win, dil) in enumerate(ATT_GROUPS):
        aq, ak, av = parts[4 + 3 * gi], parts[5 + 3 * gi], parts[6 + 3 * gi]
        tab = rel_bias[:, gi * ATT_HEADS_PER_GROUP:(gi + 1) * ATT_HEADS_PER_GROUP]
        o, lse = dilated_group(heads(aq, ATT_HEADS_PER_GROUP), heads(ak, ATT_HEADS_PER_GROUP),
                               heads(av, ATT_HEADS_PER_GROUP), tab, win, dil)
        outs.append(o)
        lses.append(lse)
    o_all = jnp.stack(outs, axis=0)
    wts = jax.nn.softmax(jnp.stack(lses, axis=0), axis=0)
    att = jnp.einsum('gbhs,gbhsd->bshd', wts.astype(o_all.dtype), o_all).reshape(B, S, ATT_GROUP_W)
    att_out = att @ w_att_out

    gate_a, gate_b = parts[-2], parts[-1]
    merged = jax.nn.sigmoid(gate_a) * ret_out + jax.nn.sigmoid(gate_b) * att_out
    return merged @ w_o


def squared_relu_mlp(h, w1, w2):
    return jnp.square(jax.nn.relu(h @ w1)) @ w2


def _fwd_setup_inputs(seed: int = 0) -> dict:
    key = jax.random.key(seed)
    ks = jax.random.split(key, 18)
    nrm = jax.random.normal
    f32 = jnp.float32
    return {
        "x": nrm(ks[0], (BATCH, SEQ, D_MODEL), f32),
        "c": nrm(ks[1], (BATCH, D_MODEL), f32),
        "w_ada": nrm(ks[2], (DEPTH, D_MODEL, 6 * D_MODEL), f32) * D_MODEL ** -0.5,
        "b_ada": nrm(ks[3], (DEPTH, 6 * D_MODEL), f32) * 0.02,
        "norm1_g": 1.0 + 0.02 * nrm(ks[4], (DEPTH, D_MODEL), f32),
        "w_in": nrm(ks[5], (DEPTH, D_MODEL, IN_COLS), f32) * D_MODEL ** -0.5,
        "rel_bias": nrm(ks[6], (REL_BUCKETS, N_ATT_HEADS), f32) * 0.5,
        "ret_gn_g": 1.0 + 0.02 * nrm(ks[7], (DEPTH, RET_V_W), f32),
        "ret_gn_b": 0.02 * nrm(ks[8], (DEPTH, RET_V_W), f32),
        "w_ret_out": nrm(ks[9], (DEPTH, RET_V_W, D_MODEL), f32) * RET_V_W ** -0.5,
        "w_att_out": nrm(ks[10], (DEPTH, ATT_GROUP_W, D_MODEL), f32) * ATT_GROUP_W ** -0.5,
        "w_o": nrm(ks[11], (DEPTH, D_MODEL, D_MODEL), f32) * D_MODEL ** -0.5,
        "norm2_g": 1.0 + 0.02 * nrm(ks[12], (DEPTH, D_MODEL), f32),
        "w_ff1": nrm(ks[13], (DEPTH, D_MODEL, D_FF), f32) * D_MODEL ** -0.5,
        "w_ff2": nrm(ks[14], (DEPTH, D_FF, D_MODEL), f32) * D_FF ** -0.5,
        "norm_f_g": 1.0 + 0.02 * nrm(ks[15], (D_MODEL,), f32),
    }


def _fwd_reference(x, c, w_ada, b_ada, norm1_g, w_in, rel_bias, ret_gn_g, ret_gn_b,
              w_ret_out, w_att_out, w_o, norm2_g, w_ff1, w_ff2, norm_f_g):
    for l in range(DEPTH):
        mod = jax.nn.silu(c) @ w_ada[l] + b_ada[l]
        sh1, sc1, g1, sh2, sc2, g2 = jnp.split(mod, 6, axis=-1)
        h = modulate(rms_norm(x, norm1_g[l]), sh1, sc1)
        x = x + g1[:, None, :] * token_mixer(h, w_in[l], rel_bias, ret_gn_g[l], ret_gn_b[l],
                                             w_ret_out[l], w_att_out[l], w_o[l])
        h = modulate(rms_norm(x, norm2_g[l]), sh2, sc2)
        x = x + g2[:, None, :] * squared_relu_mlp(h, w_ff1[l], w_ff2[l])
    return rms_norm(x, norm_f_g)


import jax as _jax
import jax.numpy as _jnp

TWIN_FORMAT = 'train_step'
FWD_PARAMS = ['x', 'c', 'w_ada', 'b_ada', 'norm1_g', 'w_in', 'rel_bias', 'ret_gn_g', 'ret_gn_b', 'w_ret_out', 'w_att_out', 'w_o', 'norm2_g', 'w_ff1', 'w_ff2', 'norm_f_g']
TWIN_WEIGHTS = ['w_ada', 'b_ada', 'norm1_g', 'w_in', 'rel_bias', 'ret_gn_g', 'ret_gn_b', 'w_ret_out', 'w_att_out', 'w_o', 'norm2_g', 'w_ff1', 'w_ff2', 'norm_f_g']
TWIN_DIFF_INPUT = 'x'
TWIN_INPUTS = ['x', 'c', 'w_ada', 'b_ada', 'norm1_g', 'w_in', 'rel_bias', 'ret_gn_g', 'ret_gn_b', 'w_ret_out', 'w_att_out', 'w_o', 'norm2_g', 'w_ff1', 'w_ff2', 'norm_f_g', 'loss_target', 'm_w_ada', 'm_b_ada', 'm_norm1_g', 'm_w_in', 'm_rel_bias', 'm_ret_gn_g', 'm_ret_gn_b', 'm_w_ret_out', 'm_w_att_out', 'm_w_o', 'm_norm2_g', 'm_w_ff1', 'm_w_ff2', 'm_norm_f_g', 'v_w_ada', 'v_b_ada', 'v_norm1_g', 'v_w_in', 'v_rel_bias', 'v_ret_gn_g', 'v_ret_gn_b', 'v_w_ret_out', 'v_w_att_out', 'v_w_o', 'v_norm2_g', 'v_w_ff1', 'v_w_ff2', 'v_norm_f_g']
TWIN_OUTPUTS = ['loss', 'grad_x', 'grad_w_ada', 'grad_b_ada', 'grad_norm1_g', 'grad_w_in', 'grad_rel_bias', 'grad_ret_gn_g', 'grad_ret_gn_b', 'grad_w_ret_out', 'grad_w_att_out', 'grad_w_o', 'grad_norm2_g', 'grad_w_ff1', 'grad_w_ff2', 'grad_norm_f_g', 'delta_w_ada', 'delta_b_ada', 'delta_norm1_g', 'delta_w_in', 'delta_rel_bias', 'delta_ret_gn_g', 'delta_ret_gn_b', 'delta_w_ret_out', 'delta_w_att_out', 'delta_w_o', 'delta_norm2_g', 'delta_w_ff1', 'delta_w_ff2', 'delta_norm_f_g', 'new_m_w_ada', 'new_m_b_ada', 'new_m_norm1_g', 'new_m_w_in', 'new_m_rel_bias', 'new_m_ret_gn_g', 'new_m_ret_gn_b', 'new_m_w_ret_out', 'new_m_w_att_out', 'new_m_w_o', 'new_m_norm2_g', 'new_m_w_ff1', 'new_m_w_ff2', 'new_m_norm_f_g', 'new_v_w_ada', 'new_v_b_ada', 'new_v_norm1_g', 'new_v_w_in', 'new_v_rel_bias', 'new_v_ret_gn_g', 'new_v_ret_gn_b', 'new_v_w_ret_out', 'new_v_w_att_out', 'new_v_w_o', 'new_v_norm2_g', 'new_v_w_ff1', 'new_v_w_ff2', 'new_v_norm_f_g']
TWIN_LEAF_KINDS = {'loss': 'loss', 'grad_x': 'grad_x', 'grad_w_ada': 'grad_w', 'grad_b_ada': 'grad_w', 'grad_norm1_g': 'grad_w', 'grad_w_in': 'grad_w', 'grad_rel_bias': 'grad_w', 'grad_ret_gn_g': 'grad_w', 'grad_ret_gn_b': 'grad_w', 'grad_w_ret_out': 'grad_w', 'grad_w_att_out': 'grad_w', 'grad_w_o': 'grad_w', 'grad_norm2_g': 'grad_w', 'grad_w_ff1': 'grad_w', 'grad_w_ff2': 'grad_w', 'grad_norm_f_g': 'grad_w', 'delta_w_ada': 'delta_w', 'delta_b_ada': 'delta_w', 'delta_norm1_g': 'delta_w', 'delta_w_in': 'delta_w', 'delta_rel_bias': 'delta_w', 'delta_ret_gn_g': 'delta_w', 'delta_ret_gn_b': 'delta_w', 'delta_w_ret_out': 'delta_w', 'delta_w_att_out': 'delta_w', 'delta_w_o': 'delta_w', 'delta_norm2_g': 'delta_w', 'delta_w_ff1': 'delta_w', 'delta_w_ff2': 'delta_w', 'delta_norm_f_g': 'delta_w', 'new_m_w_ada': 'new_m', 'new_m_b_ada': 'new_m', 'new_m_norm1_g': 'new_m', 'new_m_w_in': 'new_m', 'new_m_rel_bias': 'new_m', 'new_m_ret_gn_g': 'new_m', 'new_m_ret_gn_b': 'new_m', 'new_m_w_ret_out': 'new_m', 'new_m_w_att_out': 'new_m', 'new_m_w_o': 'new_m', 'new_m_norm2_g': 'new_m', 'new_m_w_ff1': 'new_m', 'new_m_w_ff2': 'new_m', 'new_m_norm_f_g': 'new_m', 'new_v_w_ada': 'new_v', 'new_v_b_ada': 'new_v', 'new_v_norm1_g': 'new_v', 'new_v_w_in': 'new_v', 'new_v_rel_bias': 'new_v', 'new_v_ret_gn_g': 'new_v', 'new_v_ret_gn_b': 'new_v', 'new_v_w_ret_out': 'new_v', 'new_v_w_att_out': 'new_v', 'new_v_w_o': 'new_v', 'new_v_norm2_g': 'new_v', 'new_v_w_ff1': 'new_v', 'new_v_w_ff2': 'new_v', 'new_v_norm_f_g': 'new_v'}


def _forward(args):
    return _fwd_reference(*[args[k] for k in FWD_PARAMS])


def _output_shape():
    out = _jax.eval_shape(lambda: _forward(_fwd_setup_inputs(0)))
    return out.shape, out.dtype

N_MICROBATCH = 1
ADAM_LR = 0.001
ADAM_B1 = 0.9
ADAM_B2 = 0.999
ADAM_EPS = 1e-08
ADAM_WD = 0.01
ADAM_STEP = 10
PER_EXAMPLE_BATCH_AXIS = {'x': 0, 'c': 0, 'loss_target': 0}
SHARED_INPUTS = []
_WEIGHT_DTYPES = {'w_ada': _jnp.float32, 'b_ada': _jnp.float32, 'norm1_g': _jnp.float32, 'w_in': _jnp.float32, 'rel_bias': _jnp.float32, 'ret_gn_g': _jnp.float32, 'ret_gn_b': _jnp.float32, 'w_ret_out': _jnp.float32, 'w_att_out': _jnp.float32, 'w_o': _jnp.float32, 'norm2_g': _jnp.float32, 'w_ff1': _jnp.float32, 'w_ff2': _jnp.float32, 'norm_f_g': _jnp.float32}
MOMENT_SCALE = {'w_ada': 9.700326e-02, 'b_ada': 1.776570e-01, 'norm1_g': 7.433816e-02, 'w_in': 2.761292e-02, 'rel_bias': 2.126658e-02, 'ret_gn_g': 2.579356e-02, 'ret_gn_b': 2.705460e-02, 'w_ret_out': 3.523191e-02, 'w_att_out': 2.565821e-02, 'w_o': 4.327829e-02, 'norm2_g': 1.025007e-01, 'w_ff1': 6.234813e-02, 'w_ff2': 1.348164e-01, 'norm_f_g': 1.784190e+01}


def _to_microbatches(a, axis):
    t = _jnp.moveaxis(a, axis, 0)
    t = t.reshape((N_MICROBATCH, t.shape[0] // N_MICROBATCH) + t.shape[1:])
    return _jnp.moveaxis(t, 1, axis + 1)


def setup_inputs(seed: int = 0) -> dict:
    inp = _fwd_setup_inputs(seed)
    key = _jax.random.fold_in(_jax.random.key(seed), 7919)
    shape, _ = _output_shape()
    out = dict(inp)
    out["loss_target"] = _jax.random.normal(_jax.random.fold_in(key, 0), shape, _jnp.float32)
    for i, name in enumerate(TWIN_WEIGHTS):
        w = inp[name].astype(_jnp.float32)
        if MOMENT_SCALE is None:
            s = _jnp.sqrt(_jnp.mean(_jnp.square(w)) + 1e-30)
        else:
            s = MOMENT_SCALE[name]
        km, kv = _jax.random.split(_jax.random.fold_in(key, i + 1))
        out[name] = w
        out["m_" + name] = s * _jax.random.normal(km, w.shape, _jnp.float32)
        out["v_" + name] = (s * s) * _jax.random.uniform(kv, w.shape, _jnp.float32, 0.5, 1.5)
    if N_MICROBATCH > 1:
        for name, axis in PER_EXAMPLE_BATCH_AXIS.items():
            out[name] = _to_microbatches(out[name], axis)
    return {'x': out['x'], 'c': out['c'], 'w_ada': out['w_ada'], 'b_ada': out['b_ada'], 'norm1_g': out['norm1_g'], 'w_in': out['w_in'], 'rel_bias': out['rel_bias'], 'ret_gn_g': out['ret_gn_g'], 'ret_gn_b': out['ret_gn_b'], 'w_ret_out': out['w_ret_out'], 'w_att_out': out['w_att_out'], 'w_o': out['w_o'], 'norm2_g': out['norm2_g'], 'w_ff1': out['w_ff1'], 'w_ff2': out['w_ff2'], 'norm_f_g': out['norm_f_g'], 'loss_target': out['loss_target'], 'm_w_ada': out['m_w_ada'], 'm_b_ada': out['m_b_ada'], 'm_norm1_g': out['m_norm1_g'], 'm_w_in': out['m_w_in'], 'm_rel_bias': out['m_rel_bias'], 'm_ret_gn_g': out['m_ret_gn_g'], 'm_ret_gn_b': out['m_ret_gn_b'], 'm_w_ret_out': out['m_w_ret_out'], 'm_w_att_out': out['m_w_att_out'], 'm_w_o': out['m_w_o'], 'm_norm2_g': out['m_norm2_g'], 'm_w_ff1': out['m_w_ff1'], 'm_w_ff2': out['m_w_ff2'], 'm_norm_f_g': out['m_norm_f_g'], 'v_w_ada': out['v_w_ada'], 'v_b_ada': out['v_b_ada'], 'v_norm1_g': out['v_norm1_g'], 'v_w_in': out['v_w_in'], 'v_rel_bias': out['v_rel_bias'], 'v_ret_gn_g': out['v_ret_gn_g'], 'v_ret_gn_b': out['v_ret_gn_b'], 'v_w_ret_out': out['v_w_ret_out'], 'v_w_att_out': out['v_w_att_out'], 'v_w_o': out['v_w_o'], 'v_norm2_g': out['v_norm2_g'], 'v_w_ff1': out['v_w_ff1'], 'v_w_ff2': out['v_w_ff2'], 'v_norm_f_g': out['v_norm_f_g']}


def _loss(weights, diff, rest, loss_target):
    with _jax.named_scope("forward"):
        args = {**rest, TWIN_DIFF_INPUT: diff, **{k: w.astype(_WEIGHT_DTYPES[k]) for k, w in weights.items()}}
        y = _forward(args)
    with _jax.named_scope("loss_head"):
        err = _jnp.square(y.astype(_jnp.float32) - loss_target)
        return 0.5 * _jnp.sum(_jnp.mean(err, axis=-1)) if err.ndim else 0.5 * err


def _adamw(w, g, m, v):
    m = ADAM_B1 * m + (1.0 - ADAM_B1) * g
    v = ADAM_B2 * v + (1.0 - ADAM_B2) * _jnp.square(g)
    m_hat = m / (1.0 - ADAM_B1 ** ADAM_STEP)
    v_hat = v / (1.0 - ADAM_B2 ** ADAM_STEP)
    delta = -ADAM_LR * (m_hat / (_jnp.sqrt(v_hat) + ADAM_EPS) + ADAM_WD * w)
    return delta, m, v


def reference(x, c, w_ada, b_ada, norm1_g, w_in, rel_bias, ret_gn_g, ret_gn_b, w_ret_out, w_att_out, w_o, norm2_g, w_ff1, w_ff2, norm_f_g, loss_target, m_w_ada, m_b_ada, m_norm1_g, m_w_in, m_rel_bias, m_ret_gn_g, m_ret_gn_b, m_w_ret_out, m_w_att_out, m_w_o, m_norm2_g, m_w_ff1, m_w_ff2, m_norm_f_g, v_w_ada, v_b_ada, v_norm1_g, v_w_in, v_rel_bias, v_ret_gn_g, v_ret_gn_b, v_w_ret_out, v_w_att_out, v_w_o, v_norm2_g, v_w_ff1, v_w_ff2, v_norm_f_g):
    given = dict(x=x, c=c, w_ada=w_ada, b_ada=b_ada, norm1_g=norm1_g, w_in=w_in, rel_bias=rel_bias, ret_gn_g=ret_gn_g, ret_gn_b=ret_gn_b, w_ret_out=w_ret_out, w_att_out=w_att_out, w_o=w_o, norm2_g=norm2_g, w_ff1=w_ff1, w_ff2=w_ff2, norm_f_g=norm_f_g, loss_target=loss_target, m_w_ada=m_w_ada, m_b_ada=m_b_ada, m_norm1_g=m_norm1_g, m_w_in=m_w_in, m_rel_bias=m_rel_bias, m_ret_gn_g=m_ret_gn_g, m_ret_gn_b=m_ret_gn_b, m_w_ret_out=m_w_ret_out, m_w_att_out=m_w_att_out, m_w_o=m_w_o, m_norm2_g=m_norm2_g, m_w_ff1=m_w_ff1, m_w_ff2=m_w_ff2, m_norm_f_g=m_norm_f_g, v_w_ada=v_w_ada, v_b_ada=v_b_ada, v_norm1_g=v_norm1_g, v_w_in=v_w_in, v_rel_bias=v_rel_bias, v_ret_gn_g=v_ret_gn_g, v_ret_gn_b=v_ret_gn_b, v_w_ret_out=v_w_ret_out, v_w_att_out=v_w_att_out, v_w_o=v_w_o, v_norm2_g=v_norm2_g, v_w_ff1=v_w_ff1, v_w_ff2=v_w_ff2, v_norm_f_g=v_norm_f_g)
    weights = {n: given[n] for n in TWIN_WEIGHTS}
    shared = {n: given[n] for n in SHARED_INPUTS}
    per_example = {n: given[n] for n in ['x', 'c']}
    grad_fn = _jax.value_and_grad(_loss, argnums=(0, 1))

    def one_microbatch(ex, loss_target):
        ex = dict(ex)
        diff = ex.pop(TWIN_DIFF_INPUT)
        return grad_fn(weights, diff, {**shared, **ex}, loss_target)

    if N_MICROBATCH == 1:
        loss, (grad_w, grad_x) = one_microbatch(per_example, given["loss_target"])
    else:
        def body(carry, xs):
            loss_sum, grad_sum = carry
            l_k, (gw_k, gx_k) = one_microbatch(xs[0], xs[1])
            with _jax.named_scope("update"):
                return (loss_sum + l_k, _jax.tree.map(_jnp.add, grad_sum, gw_k)), gx_k

        init = (_jnp.zeros((), _jnp.float32), _jax.tree.map(_jnp.zeros_like, weights))
        (loss, grad_w), grad_x = _jax.lax.scan(body, init, (per_example, given["loss_target"]))
    with _jax.named_scope("update"):
        delta_w, new_m, new_v = {}, {}, {}
        for n in TWIN_WEIGHTS:
            delta_w[n], new_m[n], new_v[n] = _adamw(weights[n], grad_w[n], given["m_" + n], given["v_" + n])
    return (loss, grad_x, *[grad_w[n] for n in TWIN_WEIGHTS], *[delta_w[n] for n in TWIN_WEIGHTS],
            *[new_m[n] for n in TWIN_WEIGHTS], *[new_v[n] for n in TWIN_WEIGHTS])
```

```python
import functools
import math

import numpy as np
import jax
import jax.numpy as jnp
from jax import lax
from jax.experimental import pallas as pl
from jax.experimental.pallas import tpu as pltpu

F32 = jnp.float32
BF16 = jnp.bfloat16
MESH = pl.DeviceIdType.MESH

N_DEV = 8
S = 2048
D = 1024
RET_HEADS = 4
RET_DK = 256
RET_DV = 512
CHUNK = 128
N_CHUNK = S // CHUNK
ATT_GROUPS = ((128, 1), (512, 4), (2048, 16))
ATT_HG = 4
ATT_DH = 128
ATT_BLK = 128
N_BUCKETS = 32
MAX_DIST = 2048
D_FF = 4096
IN_COLS = 12800
OFF_RQ, OFF_RK, OFF_RV, OFF_RG, OFF_ATT, OFF_GA, OFF_GB = 0, 1024, 2048, 4096, 6144, 10752, 11776
RMS_EPS = 1e-6
GN_EPS = 1e-5
ADAM_LR, ADAM_B1, ADAM_B2, ADAM_EPS, ADAM_WD, ADAM_STEP = 0.001, 0.9, 0.999, 1e-08, 0.01, 10
VMEM_LIMIT = 48 * 1024 * 1024


def _pcall(body, **kw):
    return pl.pallas_call(body, **kw)


def _params(sem=None):
    return pltpu.CompilerParams(dimension_semantics=sem, vmem_limit_bytes=VMEM_LIMIT)


def _dot(a, b, dn):
    return lax.dot_general(a.astype(BF16), b.astype(BF16), (dn, ((), ())), preferred_element_type=F32)


NN = ((1,), (0,))
NT = ((1,), (1,))
TN = ((0,), (0,))


def _mm(a, b, mode, *, tm, tn, tk, name, out_dtype=F32, res=None, gvec=None):
    if mode == 'nn':
        (M, K), (_, N) = a.shape, b.shape
        a_spec = pl.BlockSpec((tm, tk), lambda i, j, k: (i, k))
        b_spec = pl.BlockSpec((tk, tn), lambda i, j, k: (k, j))
        dn = NN
    elif mode == 'nt':
        (M, K), (N, _) = a.shape, b.shape
        a_spec = pl.BlockSpec((tm, tk), lambda i, j, k: (i, k))
        b_spec = pl.BlockSpec((tn, tk), lambda i, j, k: (j, k))
        dn = NT
    else:
        (K, M), (_, N) = a.shape, b.shape
        a_spec = pl.BlockSpec((tk, tm), lambda i, j, k: (k, i))
        b_spec = pl.BlockSpec((tk, tn), lambda i, j, k: (k, j))
        dn = TN
    assert M % tm == 0 and N % tn == 0 and K % tk == 0, (name, M, N, K)
    nk = K // tk
    fused = res is not None
    o_spec = pl.BlockSpec((tm, tn), lambda i, j, k: (i, j))

    def body(a_ref, b_ref, *rest):
        if fused:
            res_ref, g_ref, o_ref, x_ref, acc_ref = rest
        else:
            o_ref, acc_ref = rest

        def finish(acc):
            o_ref[...] = acc.astype(o_ref.dtype)
            if fused:
                x_ref[...] = res_ref[...] + g_ref[...] * acc

        p = _dot(a_ref[...], b_ref[...], dn)
        if nk == 1:
            finish(p)
        else:
            k = pl.program_id(2)

            @pl.when(k == 0)
            def _():
                acc_ref[...] = p

            @pl.when(k > 0)
            def _():
                acc_ref[...] += p

            @pl.when(k == nk - 1)
            def _():
                finish(acc_ref[...])

    in_specs = [a_spec, b_spec]
    args = [a, b]
    out_shape = jax.ShapeDtypeStruct((M, N), out_dtype)
    out_specs = o_spec
    if fused:
        in_specs += [pl.BlockSpec((tm, tn), lambda i, j, k: (i, j)), pl.BlockSpec((1, tn), lambda i, j, k: (0, j))]
        args += [res, gvec]
        out_shape = (out_shape, jax.ShapeDtypeStruct((M, N), F32))
        out_specs = (o_spec, pl.BlockSpec((tm, tn), lambda i, j, k: (i, j)))
    return _pcall(body, name=name, grid=(M // tm, N // tn, nk), in_specs=in_specs, out_specs=out_specs,
                  out_shape=out_shape, scratch_shapes=[pltpu.VMEM((tm, tn), F32)],
                  compiler_params=_params(("parallel", "parallel", "arbitrary")))(*args)


TR = 256


def _row_spec(w=D):
    return pl.BlockSpec((TR, w), lambda i: (i, 0))


def _vec_spec(w=D):
    return pl.BlockSpec((1, w), lambda i: (0, 0))


def _norm_mod_fwd(x, g, sh, sc, name):
    def body(x_ref, g_ref, sh_ref, sc_ref, o_ref):
        xv = x_ref[...]
        rstd = lax.rsqrt(jnp.mean(xv * xv, axis=-1, keepdims=True) + RMS_EPS)
        n = xv * rstd * g_ref[...]
        o_ref[...] = (n * (1.0 + sc_ref[...]) + sh_ref[...]).astype(BF16)

    return _pcall(body, name=name, grid=(S // TR,), in_specs=[_row_spec(), _vec_spec(), _vec_spec(), _vec_spec()],
                  out_specs=_row_spec(), out_shape=jax.ShapeDtypeStruct((S, D), BF16),
                  compiler_params=_params(("parallel",)))(x, g, sh, sc)


def _norm_mod_bwd(x, g, sc, dh, dres, name):
    def body(x_ref, g_ref, sc_ref, dh_ref, dres_ref, dx_ref, dsc_ref, dsh_ref, dg_ref):
        i = pl.program_id(0)
        xv = x_ref[...]
        dh = dh_ref[...]
        rstd = lax.rsqrt(jnp.mean(xv * xv, axis=-1, keepdims=True) + RMS_EPS)
        xhat = xv * rstd
        gv = g_ref[...]
        dn = dh * (1.0 + sc_ref[...])
        dxhat = dn * gv
        dx_ref[...] = dres_ref[...] + rstd * (dxhat - xhat * jnp.mean(dxhat * xhat, axis=-1, keepdims=True))
        p_sc = jnp.sum(dh * (xhat * gv), axis=0, keepdims=True)
        p_sh = jnp.sum(dh, axis=0, keepdims=True)
        p_g = jnp.sum(dn * xhat, axis=0, keepdims=True)

        @pl.when(i == 0)
        def _():
            dsc_ref[...] = p_sc
            dsh_ref[...] = p_sh
            dg_ref[...] = p_g

        @pl.when(i > 0)
        def _():
            dsc_ref[...] += p_sc
            dsh_ref[...] += p_sh
            dg_ref[...] += p_g

    vec = jax.ShapeDtypeStruct((1, D), F32)
    return _pcall(body, name=name, grid=(S // TR,),
                  in_specs=[_row_spec(), _vec_spec(), _vec_spec(), _row_spec(), _row_spec()],
                  out_specs=(_row_spec(), _vec_spec(), _vec_spec(), _vec_spec()),
                  out_shape=(jax.ShapeDtypeStruct((S, D), F32), vec, vec, vec),
                  compiler_params=_params(("arbitrary",)))(x, g, sc, dh, dres)


def _gate_bwd(dxr, fval, gvec, name):
    def body(dx_ref, f_ref, g_ref, dz_ref, dg_ref):
        i = pl.program_id(0)
        dx = dx_ref[...]
        dz_ref[...] = (dx * g_ref[...]).astype(BF16)
        p = jnp.sum(dx * f_ref[...], axis=0, keepdims=True)

        @pl.when(i == 0)
        def _():
            dg_ref[...] = p

        @pl.when(i > 0)
        def _():
            dg_ref[...] += p

    return _pcall(body, name=name, grid=(S // TR,), in_specs=[_row_spec(), _row_spec(), _vec_spec()],
                  out_specs=(_row_spec(), _vec_spec()),
                  out_shape=(jax.ShapeDtypeStruct((S, D), BF16), jax.ShapeDtypeStruct((1, D), F32)),
                  compiler_params=_params(("arbitrary",)))(dxr, fval, gvec)


def _relu2_fwd(u):
    def body(u_ref, a_ref):
        r = jnp.maximum(u_ref[...], 0.0)
        a_ref[...] = (r * r).astype(BF16)

    return _pcall(body, name="relu2_fwd", grid=(S // TR,), in_specs=[_row_spec(D_FF)], out_specs=_row_spec(D_FF),
                  out_shape=jax.ShapeDtypeStruct((S, D_FF), BF16), compiler_params=_params(("parallel",)))(u)


def _relu2_bwd(da, u):
    def body(da_ref, u_ref, du_ref):
        du_ref[...] = (da_ref[...] * (2.0 * jnp.maximum(u_ref[...], 0.0))).astype(BF16)

    return _pcall(body, name="relu2_bwd", grid=(S // TR,), in_specs=[_row_spec(D_FF), _row_spec(D_FF)],
                  out_specs=_row_spec(D_FF), out_shape=jax.ShapeDtypeStruct((S, D_FF), BF16),
                  compiler_params=_params(("parallel",)))(da, u)


def _final_loss(x2, tgt, g):
    def body(x_ref, t_ref, g_ref, loss_ref, dx_ref, dg_ref):
        i = pl.program_id(0)
        xv = x_ref[...]
        gv = g_ref[...]
        rstd = lax.rsqrt(jnp.mean(xv * xv, axis=-1, keepdims=True) + RMS_EPS)
        xhat = xv * rstd
        err = xhat * gv - t_ref[...]
        dy = err * (1.0 / D)
        dxhat = dy * gv
        dx_ref[...] = rstd * (dxhat - xhat * jnp.mean(dxhat * xhat, axis=-1, keepdims=True))
        p_g = jnp.sum(dy * xhat, axis=0, keepdims=True)
        p_l = jnp.zeros((1, 128), F32) + 0.5 * jnp.sum(jnp.mean(err * err, axis=-1, keepdims=True))

        @pl.when(i == 0)
        def _():
            dg_ref[...] = p_g
            loss_ref[...] = p_l

        @pl.when(i > 0)
        def _():
            dg_ref[...] += p_g
            loss_ref[...] += p_l

    return _pcall(body, name="final_loss", grid=(S // TR,), in_specs=[_row_spec(), _row_spec(), _vec_spec()],
                  out_specs=(_vec_spec(128), _row_spec(), _vec_spec()),
                  out_shape=(jax.ShapeDtypeStruct((1, 128), F32), jax.ShapeDtypeStruct((S, D), F32),
                             jax.ShapeDtypeStruct((1, D), F32)),
                  compiler_params=_params(("arbitrary",)))(x2, tgt, g)


HALF = 512


def _merge_fwd(proj, ret_out, att_out):
    def body(ga_ref, gb_ref, r_ref, a_ref, o_ref):
        o_ref[...] = (jax.nn.sigmoid(ga_ref[...]) * r_ref[...] + jax.nn.sigmoid(gb_ref[...]) * a_ref[...]).astype(BF16)

    blk = lambda off: pl.BlockSpec((TR, HALF), lambda i, j: (i, off // HALF + j))
    return _pcall(body, name="merge_fwd", grid=(S // TR, D // HALF),
                  in_specs=[blk(OFF_GA), blk(OFF_GB), blk(0), blk(0)], out_specs=blk(0),
                  out_shape=jax.ShapeDtypeStruct((S, D), BF16),
                  compiler_params=_params(("parallel", "parallel")))(proj, proj, ret_out, att_out)


def _merge_bwd(proj, ret_out, att_out, dmerged):
    def body(ga_ref, gb_ref, r_ref, a_ref, dm_ref, dr_ref, da_ref, dga_ref, dgb_ref):
        sa = jax.nn.sigmoid(ga_ref[...])
        sb = jax.nn.sigmoid(gb_ref[...])
        dm = dm_ref[...]
        dr_ref[...] = (dm * sa).astype(BF16)
        da_ref[...] = (dm * sb).astype(BF16)
        dga_ref[...] = (dm * r_ref[...] * (sa * (1.0 - sa))).astype(BF16)
        dgb_ref[...] = (dm * a_ref[...] * (sb * (1.0 - sb))).astype(BF16)

    blk = lambda off: pl.BlockSpec((TR, HALF), lambda i, j: (i, off // HALF + j))
    o = jax.ShapeDtypeStruct((S, D), BF16)
    return _pcall(body, name="merge_bwd", grid=(S // TR, D // HALF),
                  in_specs=[blk(OFF_GA), blk(OFF_GB), blk(0), blk(0), blk(0)], out_specs=(blk(0),) * 4,
                  out_shape=(o, o, o, o),
                  compiler_params=_params(("parallel", "parallel")))(proj, proj, ret_out, att_out, dmerged)


def _ret_tables():
    H, C = RET_HEADS, CHUNK
    log_g = jnp.log1p(-(2.0 ** (-5.0 - jnp.arange(H, dtype=F32))))
    idx = jnp.arange(C, dtype=F32)
    rel = idx[:, None] - idx[None, :]
    inner = jnp.where(rel >= 0, jnp.exp(log_g[:, None, None] * jnp.maximum(rel, 0.0)), 0.0)
    qd = jnp.exp(log_g[:, None] * (idx + 1.0))[:, :, None]
    kd = jnp.exp(log_g[:, None] * (C - 1.0 - idx))[:, :, None]
    cd = jnp.broadcast_to(jnp.exp(log_g * C)[:, None, None], (H, 1, 128))
    half = RET_DK // 2
    inv = 10000.0 ** (-jnp.arange(half, dtype=F32) / half)
    ang = jnp.arange(S, dtype=F32)[:, None] * inv[None, :]
    return inner, qd, kd, cd, jnp.cos(ang), jnp.sin(ang)


def _rot(x, cos, sin):
    x1, x2 = x[:, :128], x[:, 128:]
    return jnp.concatenate([x1 * cos - x2 * sin, x1 * sin + x2 * cos], axis=1)


def _rot_t(d, cos, sin):
    d1, d2 = d[:, :128], d[:, 128:]
    return jnp.concatenate([d1 * cos + d2 * sin, d2 * cos - d1 * sin], axis=1)


def _ret_in_specs(chunk_of):
    ci = chunk_of
    return [
        pl.BlockSpec((CHUNK, RET_DK), lambda h, t: (ci(t), OFF_RQ // RET_DK + h)),
        pl.BlockSpec((CHUNK, RET_DK), lambda h, t: (ci(t), OFF_RK // RET_DK + h)),
        pl.BlockSpec((CHUNK, RET_DV), lambda h, t: (ci(t), OFF_RV // RET_DV + h)),
        pl.BlockSpec((CHUNK, RET_DV), lambda h, t: (ci(t), OFF_RG // RET_DV + h)),
        pl.BlockSpec((CHUNK, 128), lambda h, t: (ci(t), 0)),
        pl.BlockSpec((CHUNK, 128), lambda h, t: (ci(t), 0)),
        pl.BlockSpec((None, CHUNK, CHUNK), lambda h, t: (h, 0, 0)),
        pl.BlockSpec((None, CHUNK, 1), lambda h, t: (h, 0, 0)),
        pl.BlockSpec((None, CHUNK, 1), lambda h, t: (h, 0, 0)),
        pl.BlockSpec((None, 1, 128), lambda h, t: (h, 0, 0)),
    ]


def _ret_fwd(proj, tables, gn_g, gn_b):
    inner, qd, kd, cd, cos, sin = tables

    def body(q_ref, k_ref, v_ref, rg_ref, cos_ref, sin_ref, in_ref, qd_ref, kd_ref, cd_ref, g_ref, b_ref,
             gated_ref, ro_ref, st_ref, s_scr):
        i = pl.program_id(1)

        @pl.when(i == 0)
        def _():
            s_scr[...] = jnp.zeros_like(s_scr)

        cosv, sinv = cos_ref[...], sin_ref[...]
        q = _rot(q_ref[...], cosv, sinv)
        k = _rot(k_ref[...], cosv, sinv) * (RET_DK ** -0.5)
        v = v_ref[...]
        st = s_scr[...]
        st_ref[...] = st
        s = _dot(q, k, NT) * in_ref[...]
        o = _dot(s, v, NN) + _dot(q, st, NN) * qd_ref[...]
        s_scr[...] = st * cd_ref[:, :1] + _dot(k * kd_ref[...], v, TN)
        ro_ref[...] = o
        mu = jnp.mean(o, axis=-1, keepdims=True)
        oc = o - mu
        var = jnp.mean(oc * oc, axis=-1, keepdims=True)
        rn = oc * lax.rsqrt(var + GN_EPS) * g_ref[...] + b_ref[...]
        rg = rg_ref[...]
        gated_ref[...] = (rg * jax.nn.sigmoid(rg) * rn).astype(BF16)

    vspec = pl.BlockSpec((1, RET_DV), lambda h, t: (0, h))
    ospec = pl.BlockSpec((CHUNK, RET_DV), lambda h, t: (t, h))
    return _pcall(body, name="ret_fwd", grid=(RET_HEADS, N_CHUNK),
                  in_specs=_ret_in_specs(lambda t: t) + [vspec, vspec],
                  out_specs=(ospec, ospec, pl.BlockSpec((None, None, RET_DK, RET_DV), lambda h, t: (h, t, 0, 0))),
                  out_shape=(jax.ShapeDtypeStruct((S, RET_HEADS * RET_DV), BF16),
                             jax.ShapeDtypeStruct((S, RET_HEADS * RET_DV), F32),
                             jax.ShapeDtypeStruct((RET_HEADS, N_CHUNK, RET_DK, RET_DV), F32)),
                  scratch_shapes=[pltpu.VMEM((RET_DK, RET_DV), F32)],
                  compiler_params=_params(("parallel", "arbitrary")))(
        proj, proj, proj, proj, cos, sin, inner, qd, kd, cd, gn_g, gn_b)


def _ret_bwd(proj, tables, gn_g, gn_b, ro, states, dgated):
    inner, qd, kd, cd, cos, sin = tables
    last = N_CHUNK - 1

    def body(q_ref, k_ref, v_ref, rg_ref, cos_ref, sin_ref, in_ref, qd_ref, kd_ref, cd_ref, g_ref, b_ref,
             ro_ref, st_ref, dg_ref, dq_ref, dk_ref, dv_ref, drg_ref, gg_ref, gb_ref, gs_scr):
        t = pl.program_id(1)
        cosv, sinv = cos_ref[...], sin_ref[...]
        q = _rot(q_ref[...], cosv, sinv)
        k = _rot(k_ref[...], cosv, sinv) * (RET_DK ** -0.5)
        v = v_ref[...]
        qdv, kdv, dm = qd_ref[...], kd_ref[...], in_ref[...]
        st = st_ref[...]
        o = ro_ref[...]
        gv = g_ref[...]
        mu = jnp.mean(o, axis=-1, keepdims=True)
        oc = o - mu
        rstd = lax.rsqrt(jnp.mean(oc * oc, axis=-1, keepdims=True) + GN_EPS)
        ohat = oc * rstd
        rn = ohat * gv + b_ref[...]
        rg = rg_ref[...]
        sg = jax.nn.sigmoid(rg)
        dgt = dg_ref[...]
        drn = dgt * (rg * sg)
        drg_ref[...] = (dgt * rn * (sg * (1.0 + rg * (1.0 - sg)))).astype(BF16)
        p_g = jnp.sum(drn * ohat, axis=0, keepdims=True)
        p_b = jnp.sum(drn, axis=0, keepdims=True)

        @pl.when(t == 0)
        def _():
            gs_scr[...] = jnp.zeros_like(gs_scr)
            gg_ref[...] = p_g
            gb_ref[...] = p_b

        @pl.when(t > 0)
        def _():
            gg_ref[...] += p_g
            gb_ref[...] += p_b

        dohat = drn * gv
        do = rstd * (dohat - jnp.mean(dohat, axis=-1, keepdims=True)
                     - ohat * jnp.mean(dohat * ohat, axis=-1, keepdims=True))
        gs = gs_scr[...]
        s = _dot(q, k, NT) * dm
        dsr = _dot(do, v, NT) * dm
        dq = _dot(dsr, k, NN) + _dot(do, st, NT) * qdv
        dk = _dot(dsr, q, TN) + _dot(v, gs, NT) * kdv
        dv = _dot(s, do, TN) + _dot(k * kdv, gs, NN)
        gs_scr[...] = gs * cd_ref[:, :1] + _dot(q * qdv, do, TN)
        dq_ref[...] = _rot_t(dq, cosv, sinv).astype(BF16)
        dk_ref[...] = (_rot_t(dk, cosv, sinv) * (RET_DK ** -0.5)).astype(BF16)
        dv_ref[...] = dv.astype(BF16)

    rev = lambda t: last - t
    vspec = pl.BlockSpec((1, RET_DV), lambda h, t: (0, h))
    vblk = pl.BlockSpec((CHUNK, RET_DV), lambda h, t: (rev(t), h))
    qblk = pl.BlockSpec((CHUNK, RET_DK), lambda h, t: (rev(t), h))
    return _pcall(body, name="ret_bwd", grid=(RET_HEADS, N_CHUNK),
                  in_specs=_ret_in_specs(rev) + [vspec, vspec, vblk,
                                                 pl.BlockSpec((None, None, RET_DK, RET_DV), lambda h, t: (h, rev(t), 0, 0)),
                                                 vblk],
                  out_specs=(qblk, qblk, vblk, vblk, vspec, vspec),
                  out_shape=(jax.ShapeDtypeStruct((S, RET_HEADS * RET_DK), BF16),
                             jax.ShapeDtypeStruct((S, RET_HEADS * RET_DK), BF16),
                             jax.ShapeDtypeStruct((S, RET_HEADS * RET_DV), BF16),
                             jax.ShapeDtypeStruct((S, RET_HEADS * RET_DV), BF16),
                             jax.ShapeDtypeStruct((1, RET_HEADS * RET_DV), F32),
                             jax.ShapeDtypeStruct((1, RET_HEADS * RET_DV), F32)),
                  scratch_shapes=[pltpu.VMEM((RET_DK, RET_DV), F32)],
                  compiler_params=_params(("parallel", "arbitrary")))(
        proj, proj, proj, proj, cos, sin, inner, qd, kd, cd, gn_g, gn_b, ro, states, dgated)


def _bucket_tables():
    qi = np.arange(ATT_BLK)[:, None]
    kj = np.arange(2 * ATT_BLK)[None, :]
    m = ATT_BLK + qi - kj
    out = []
    for win, dil in ATT_GROUPS:
        w = win // dil
        dist = (np.clip(m, 0, w) * dil).astype(np.int32)
        max_exact = N_BUCKETS // 2
        d_f = np.maximum(dist, 1).astype(np.float32)
        large = max_exact + (np.log(d_f / np.float32(max_exact)) / np.float32(math.log(MAX_DIST / max_exact))
                             * np.float32(N_BUCKETS - max_exact)).astype(np.int32)
        large = np.minimum(large, N_BUCKETS - 1)
        out.append(np.where(dist < max_exact, dist, large).astype(np.int32))
    return np.stack(out)


def _bias_build(rel_bias, buckets):
    def body(tab_ref, bk_ref, o_ref):
        hh = pl.program_id(0)
        bk = bk_ref[...]
        acc = jnp.zeros((ATT_BLK, 2 * ATT_BLK), F32)
        for b in range(N_BUCKETS):
            acc = jnp.where(bk == b, tab_ref[b, hh], acc)
        o_ref[...] = acc

    nh = len(ATT_GROUPS) * ATT_HG
    return _pcall(body, name="bias_build", grid=(nh,),
                  in_specs=[pl.BlockSpec(memory_space=pltpu.SMEM),
                            pl.BlockSpec((None, ATT_BLK, 2 * ATT_BLK), lambda hh: (hh // ATT_HG, 0, 0))],
                  out_specs=pl.BlockSpec((None, ATT_BLK, 2 * ATT_BLK), lambda hh: (hh, 0, 0)),
                  out_shape=jax.ShapeDtypeStruct((nh, ATT_BLK, 2 * ATT_BLK), F32),
                  compiler_params=_params(("parallel",)))(rel_bias, buckets)


def _bias_grad(ds_sum, buckets):
    def body(ds_ref, bk_ref, o_ref):
        bk = bk_ref[...]
        ds = ds_ref[...]
        rows = lax.broadcasted_iota(jnp.int32, (N_BUCKETS, 128), 0)
        acc = jnp.zeros((N_BUCKETS, 128), F32)
        for b in range(N_BUCKETS):
            acc = jnp.where(rows == b, jnp.sum(jnp.where(bk == b, ds, 0.0)), acc)
        o_ref[...] = acc

    nh = len(ATT_GROUPS) * ATT_HG
    return _pcall(body, name="bias_grad", grid=(nh,),
                  in_specs=[pl.BlockSpec((None, ATT_BLK, 2 * ATT_BLK), lambda hh: (hh, 0, 0)),
                            pl.BlockSpec((None, ATT_BLK, 2 * ATT_BLK), lambda hh: (hh // ATT_HG, 0, 0))],
                  out_specs=pl.BlockSpec((None, N_BUCKETS, 128), lambda hh: (hh, 0, 0)),
                  out_shape=jax.ShapeDtypeStruct((nh, N_BUCKETS, 128), F32),
                  compiler_params=_params(("parallel",)))(ds_sum, buckets)


def _att_valid(n):
    qi = lax.broadcasted_iota(jnp.int32, (ATT_BLK, 2 * ATT_BLK), 0)
    kj = lax.broadcasted_iota(jnp.int32, (ATT_BLK, 2 * ATT_BLK), 1)
    m = ATT_BLK + qi - kj
    first_key = jnp.where(n > 0, 0, ATT_BLK)
    return (m >= 0) & (m <= ATT_BLK) & (kj >= first_key)


def _att_fwd(a, bias, gi):
    _, dil = ATT_GROUPS[gi]
    L = S // dil
    nb = L // ATT_BLK
    scale = ATT_DH ** -0.5

    def body(q_ref, kc_ref, kp_ref, vc_ref, vp_ref, bias_ref, o_ref, l_ref):
        n = pl.program_id(2)
        kk = jnp.concatenate([kp_ref[...], kc_ref[...]], axis=0)
        vv = jnp.concatenate([vp_ref[...], vc_ref[...]], axis=0)
        s = _dot(q_ref[...], kk, NT) * scale + bias_ref[...]
        s = jnp.where(_att_valid(n), s, -1e30)
        mx = jnp.max(s, axis=-1, keepdims=True)
        e = jnp.exp(s - mx)
        den = jnp.sum(e, axis=-1, keepdims=True)
        o_ref[...] = _dot(e / den, vv, NN)
        l_ref[...] = jnp.broadcast_to(mx + jnp.log(den), (ATT_BLK, ATT_DH))

    blk = lambda part, prev: pl.BlockSpec(
        (ATT_BLK, ATT_DH), lambda r, h, n: (jnp.maximum(n - 1, 0) if prev else n, r * 12 + 4 * part + h))
    oblk = pl.BlockSpec((ATT_BLK, ATT_DH), lambda r, h, n: (n, r * 4 + h))
    osh = jax.ShapeDtypeStruct((L, dil * ATT_HG * ATT_DH), F32)
    return _pcall(body, name=f"att_fwd{gi}", grid=(dil, ATT_HG, nb),
                  in_specs=[blk(0, False), blk(1, False), blk(1, True), blk(2, False), blk(2, True),
                            pl.BlockSpec((None, ATT_BLK, 2 * ATT_BLK), lambda r, h, n: (gi * ATT_HG + h, 0, 0))],
                  out_specs=(oblk, oblk), out_shape=(osh, osh),
                  compiler_params=_params(("parallel", "parallel", "arbitrary")))(a, a, a, a, a, bias)


def _att_bwd(a, bias, o, lse, do, dlse, gi):
    _, dil = ATT_GROUPS[gi]
    L = S // dil
    nb = L // ATT_BLK
    scale = ATT_DH ** -0.5

    def body(q_ref, kc_ref, kp_ref, vc_ref, vp_ref, bias_ref, o_ref, l_ref, do_ref, dl_ref,
             dq_ref, dk_ref, dv_ref, ds_ref, ck_scr, cv_scr):
        r = pl.program_id(1)
        n = pl.program_id(2)

        @pl.when((r == 0) & (n == 0))
        def _():
            ds_ref[...] = jnp.zeros_like(ds_ref)

        @pl.when(n < nb)
        def _():
            q = q_ref[...]
            kk = jnp.concatenate([kp_ref[...], kc_ref[...]], axis=0)
            vv = jnp.concatenate([vp_ref[...], vc_ref[...]], axis=0)
            dov = do_ref[...]
            s = _dot(q, kk, NT) * scale + bias_ref[...]
            p = jnp.where(_att_valid(n), jnp.exp(s - l_ref[:, :1]), 0.0)
            dp = _dot(dov, vv, NT)
            delta = jnp.sum(dov * o_ref[...], axis=-1, keepdims=True)
            ds = p * (dp - delta + dl_ref[:, :1])
            ds_ref[...] += ds
            dq_ref[...] = (_dot(ds, kk, NN) * scale).astype(BF16)
            dkk = _dot(ds, q, TN) * scale
            dvv = _dot(p, dov, TN)

            @pl.when(n > 0)
            def _():
                dk_ref[...] = (ck_scr[...] + dkk[:ATT_BLK]).astype(BF16)
                dv_ref[...] = (cv_scr[...] + dvv[:ATT_BLK]).astype(BF16)

            ck_scr[...] = dkk[ATT_BLK:]
            cv_scr[...] = dvv[ATT_BLK:]

        @pl.when(n == nb)
        def _():
            dk_ref[...] = ck_scr[...].astype(BF16)
            dv_ref[...] = cv_scr[...].astype(BF16)

    cur = lambda n: jnp.minimum(n, nb - 1)
    prv = lambda n: jnp.maximum(jnp.minimum(n, nb - 1) - 1, 0)
    blk = lambda part, prev: pl.BlockSpec(
        (ATT_BLK, ATT_DH), lambda h, r, n: (prv(n) if prev else cur(n), r * 12 + 4 * part + h))
    oblk = pl.BlockSpec((ATT_BLK, ATT_DH), lambda h, r, n: (cur(n), r * 4 + h))
    kvblk = pl.BlockSpec((ATT_BLK, ATT_DH), lambda h, r, n: (jnp.maximum(n - 1, 0), r * 4 + h))
    wide = pl.BlockSpec((None, ATT_BLK, 2 * ATT_BLK), lambda h, r, n: (gi * ATT_HG + h, 0, 0))
    osh = jax.ShapeDtypeStruct((L, dil * ATT_HG * ATT_DH), BF16)
    return _pcall(body, name=f"att_bwd{gi}", grid=(ATT_HG, dil, nb + 1),
                  in_specs=[blk(0, False), blk(1, False), blk(1, True), blk(2, False), blk(2, True), wide,
                            oblk, oblk, oblk, oblk],
                  out_specs=(oblk, kvblk, kvblk,
                             pl.BlockSpec((None, ATT_BLK, 2 * ATT_BLK), lambda h, r, n: (h, 0, 0))),
                  out_shape=(osh, osh, osh, jax.ShapeDtypeStruct((ATT_HG, ATT_BLK, 2 * ATT_BLK), F32)),
                  scratch_shapes=[pltpu.VMEM((ATT_BLK, ATT_DH), F32), pltpu.VMEM((ATT_BLK, ATT_DH), F32)],
                  compiler_params=_params(("parallel", "arbitrary", "arbitrary")))(
        a, a, a, a, a, bias, o, lse, do, dlse)


AW = ATT_HG * ATT_DH


def _mix_weights(l0, l1, l2):
    mx = jnp.maximum(jnp.maximum(l0, l1), l2)
    e0, e1, e2 = jnp.exp(l0 - mx), jnp.exp(l1 - mx), jnp.exp(l2 - mx)
    den = e0 + e1 + e2
    return e0 / den, e1 / den, e2 / den


def _mix_fwd(os_, ls):
    def body(o0, o1, o2, l0, l1, l2, att_ref):
        w0, w1, w2 = _mix_weights(l0[...], l1[...], l2[...])
        att_ref[...] = (w0 * o0[...] + w1 * o1[...] + w2 * o2[...]).astype(BF16)

    return _pcall(body, name="mix_fwd", grid=(S // TR,), in_specs=[_row_spec(AW)] * 6, out_specs=_row_spec(AW),
                  out_shape=jax.ShapeDtypeStruct((S, AW), BF16), compiler_params=_params(("parallel",)))(*os_, *ls)


def _mix_bwd(os_, ls, datt):
    def body(o0, o1, o2, l0, l1, l2, da_ref, d0, d1, d2, e0, e1, e2):
        ws = _mix_weights(l0[...], l1[...], l2[...])
        da = da_ref[...]
        dws = []
        for o_ref, w, d_ref in zip((o0, o1, o2), ws, (d0, d1, d2)):
            d_ref[...] = w * da
            prod = da * o_ref[...]
            parts = [jnp.broadcast_to(jnp.sum(prod[:, h * ATT_DH:(h + 1) * ATT_DH], axis=-1, keepdims=True),
                                      (TR, ATT_DH)) for h in range(ATT_HG)]
            dws.append(jnp.concatenate(parts, axis=1))
        tot = ws[0] * dws[0] + ws[1] * dws[1] + ws[2] * dws[2]
        for w, dw, e_ref in zip(ws, dws, (e0, e1, e2)):
            e_ref[...] = w * (dw - tot)

    o = jax.ShapeDtypeStruct((S, AW), F32)
    return _pcall(body, name="mix_bwd", grid=(S // TR,), in_specs=[_row_spec(AW)] * 7, out_specs=(_row_spec(AW),) * 6,
                  out_shape=(o,) * 6, compiler_params=_params(("parallel",)))(*os_, *ls, datt)


def _ada_fwd(c_all, w_sh, b_sl):
    def body(c_ref, w_ref, b_ref, o_ref):
        cv = c_ref[...]
        o_ref[...] = _dot(cv * jax.nn.sigmoid(cv), w_ref[...], NN) + b_ref[...]

    return _pcall(body, name="ada_fwd", out_shape=jax.ShapeDtypeStruct((N_DEV, w_sh.shape[1]), F32),
                  compiler_params=_params())(c_all, w_sh, b_sl)


def _ada_bwd(c_all, dm_sl):
    def body(c_ref, d_ref, o_ref):
        cv = c_ref[...]
        o_ref[...] = _dot(cv * jax.nn.sigmoid(cv), d_ref[...], TN)

    return _pcall(body, name="ada_bwd", out_shape=jax.ShapeDtypeStruct((D, dm_sl.shape[1]), F32),
                  compiler_params=_params())(c_all, dm_sl)


def _sum_slots(g, name):
    n = g.shape[0]

    def body(g_ref, o_ref):
        acc = g_ref[0]
        for e in range(1, n):
            acc = acc + g_ref[e]
        o_ref[...] = acc

    return _pcall(body, name=name, out_shape=jax.ShapeDtypeStruct(g.shape[1:], F32), compiler_params=_params())(g)


def _row_tile(m, n):
    t = max(8, min(m, (1 << 19) // n // 8 * 8))
    while m % t:
        t -= 8
    return t


def _pair_sum(full, recv, sel, name):
    _, _, m, n = full.shape
    t = _row_tile(m, n)

    def body(sel_ref, a_ref, b_ref, o_ref):
        o_ref[...] = a_ref[...] + b_ref[...]

    gs = pltpu.PrefetchScalarGridSpec(
        num_scalar_prefetch=1, grid=(4, m // t),
        in_specs=[pl.BlockSpec((None, None, t, n), lambda q, i, s: (q, s[0], i, 0)),
                  pl.BlockSpec((None, t, n), lambda q, i, s: (q, i, 0))],
        out_specs=pl.BlockSpec((None, t, n), lambda q, i, s: (q, i, 0)))
    return _pcall(body, name=name, grid_spec=gs, out_shape=jax.ShapeDtypeStruct((4, m, n), F32),
                  compiler_params=_params(("parallel", "parallel")))(sel, full, recv)


def _chip_sum(part, recv, sel, name):
    _, m, n = part.shape
    t = _row_tile(m, n)

    def body(sel_ref, a_ref, r_ref, o_ref):
        o_ref[...] = ((a_ref[...] + r_ref[0]) + r_ref[1]) + r_ref[2]

    gs = pltpu.PrefetchScalarGridSpec(
        num_scalar_prefetch=1, grid=(m // t,),
        in_specs=[pl.BlockSpec((None, t, n), lambda i, s: (s[0], i, 0)),
                  pl.BlockSpec((3, t, n), lambda i, s: (0, i, 0))],
        out_specs=pl.BlockSpec((t, n), lambda i, s: (i, 0)))
    return _pcall(body, name=name, grid_spec=gs, out_shape=jax.ShapeDtypeStruct((m, n), F32),
                  compiler_params=_params(("parallel",)))(sel, part, recv)


def _adamw(w, g, m, v, name):
    rows, cols = w.shape
    t = _row_tile(rows, cols) if rows >= 8 else rows

    def body(w_ref, g_ref, m_ref, v_ref, d_ref, nm_ref, nv_ref):
        gv = g_ref[...]
        nm = ADAM_B1 * m_ref[...] + (1.0 - ADAM_B1) * gv
        nv = ADAM_B2 * v_ref[...] + (1.0 - ADAM_B2) * (gv * gv)
        m_hat = nm / (1.0 - ADAM_B1 ** ADAM_STEP)
        v_hat = nv / (1.0 - ADAM_B2 ** ADAM_STEP)
        d_ref[...] = -ADAM_LR * (m_hat / (jnp.sqrt(v_hat) + ADAM_EPS) + ADAM_WD * w_ref[...])
        nm_ref[...] = nm
        nv_ref[...] = nv

    spec = pl.BlockSpec((t, cols), lambda i: (i, 0))
    o = jax.ShapeDtypeStruct((rows, cols), F32)
    return _pcall(body, name=name, grid=(rows // t,), in_specs=[spec] * 4, out_specs=(spec,) * 3,
                  out_shape=(o, o, o), compiler_params=_params(("parallel",)))(w, g, m, v)


HBM_SPEC = pl.BlockSpec(memory_space=pl.ANY)


def _mesh_pos():
    return lax.axis_index("x"), lax.axis_index("y"), lax.axis_index("c")


def _all_gather(arrs, name):
    na = len(arrs)

    def body(*refs):
        ins, outs = refs[:na], refs[na:2 * na]
        send_sems, recv_sems, local_sems = refs[2 * na:]
        x, y, c = _mesh_pos()
        me, sibling = (x, y, c), (x, y, 1 - c)
        chips = [(1 - x, y), (x, 1 - y), (1 - x, 1 - y)]

        def slot(p):
            return 4 * p[0] + 2 * p[1] + p[2]

        def copy(a, k, block, to, src=None):
            dst = outs[a].at[slot(block)]
            return pltpu.make_async_remote_copy(
                src_ref=dst if src is None else src, dst_ref=dst, send_sem=send_sems.at[7 * a + k],
                recv_sem=recv_sems.at[7 * a + k], device_id=to, device_id_type=MESH)

        mine, first, passed = [], [], []
        for a in range(na):
            cp = pltpu.make_async_copy(ins[a], outs[a].at[slot(me)], local_sems.at[a])
            cp.start()
            mine.append(cp)
            first.append(copy(a, 0, me, sibling, src=ins[a]))
            first += [copy(a, 1 + j, me, (*chip, c), src=ins[a]) for j, chip in enumerate(chips)]
        for cp in first:
            cp.start()
        for j, chip in enumerate(chips):
            for a in range(na):
                copy(a, 1 + j, (*chip, c), me).wait_recv()
                cp = copy(a, 4 + j, (*chip, c), sibling)
                cp.start()
                passed.append(cp)
        for a in range(na):
            copy(a, 0, sibling, me).wait_recv()
            for j, chip in enumerate(chips):
                copy(a, 4 + j, (*chip, 1 - c), me).wait_recv()
        for cp in first + passed:
            cp.wait_send()
        for cp in mine:
            cp.wait()

    return _pcall(body, name=name, in_specs=[HBM_SPEC] * na, out_specs=tuple([HBM_SPEC] * na),
                  out_shape=tuple(jax.ShapeDtypeStruct((N_DEV,) + a.shape, a.dtype) for a in arrs),
                  scratch_shapes=[pltpu.SemaphoreType.DMA((7 * na,)), pltpu.SemaphoreType.DMA((7 * na,)),
                                  pltpu.SemaphoreType.DMA((na,))])(*arrs)


def _exchange_core(fulls, name):
    na = len(fulls)

    def body(*refs):
        ins, outs = refs[:na], refs[na:2 * na]
        send_sems, recv_sems = refs[2 * na:]
        x, y, c = _mesh_pos()
        copies = []
        for a in range(na):
            for q in range(4):
                cp = pltpu.make_async_remote_copy(
                    src_ref=ins[a].at[q, 1 - c], dst_ref=outs[a].at[q], send_sem=send_sems.at[4 * a + q],
                    recv_sem=recv_sems.at[4 * a + q], device_id=(x, y, 1 - c), device_id_type=MESH)
                cp.start()
                copies.append(cp)
        for cp in copies:
            cp.wait()

    return _pcall(body, name=name, in_specs=[HBM_SPEC] * na, out_specs=tuple([HBM_SPEC] * na),
                  out_shape=tuple(jax.ShapeDtypeStruct((4,) + f.shape[2:], f.dtype) for f in fulls),
                  scratch_shapes=[pltpu.SemaphoreType.DMA((4 * na,)), pltpu.SemaphoreType.DMA((4 * na,))])(*fulls)


def _exchange_chip(parts, name):
    na = len(parts)

    def body(*refs):
        ins, outs = refs[:na], refs[na:2 * na]
        send_sems, recv_sems = refs[2 * na:]
        x, y, c = _mesh_pos()
        chips = [(1 - x, y), (x, 1 - y), (1 - x, 1 - y)]
        copies = []
        for a in range(na):
            for j, (px, py) in enumerate(chips):
                cp = pltpu.make_async_remote_copy(
                    src_ref=ins[a].at[2 * px + py], dst_ref=outs[a].at[j], send_sem=send_sems.at[3 * a + j],
                    recv_sem=recv_sems.at[3 * a + j], device_id=(px, py, c), device_id_type=MESH)
                cp.start()
                copies.append(cp)
        for cp in copies:
            cp.wait()

    return _pcall(body, name=name, in_specs=[HBM_SPEC] * na, out_specs=tuple([HBM_SPEC] * na),
                  out_shape=tuple(jax.ShapeDtypeStruct((3,) + p.shape[1:], p.dtype) for p in parts),
                  scratch_shapes=[pltpu.SemaphoreType.DMA((3 * na,)), pltpu.SemaphoreType.DMA((3 * na,))])(*parts)


def _dilated_view(t, dil):
    return t if dil == 1 else t.reshape(S // dil, dil * t.shape[1])


def _local_step(x, tgt, mods, wts, small):
    sh1, sc1, g1, sh2, sc2, g2 = mods
    w_in, w_ret_out, w_att_out, w_o, w_ff1, w_ff2 = wts
    norm1_g, rel_bias, gn_g, gn_b, norm2_g, norm_f_g = small
    tables = _ret_tables()
    buckets = jnp.asarray(_bucket_tables())

    h1 = _norm_mod_fwd(x, norm1_g, sh1, sc1, "norm1_fwd")
    proj = _mm(h1, w_in, 'nn', tm=512, tn=512, tk=D, name="proj")
    gated, ro, states = _ret_fwd(proj, tables, gn_g, gn_b)
    bias = _bias_build(rel_bias, buckets)
    views, outs, lses = [], [], []
    for gi, (_, dil) in enumerate(ATT_GROUPS):
        a = _dilated_view(proj[:, OFF_ATT + 1536 * gi: OFF_ATT + 1536 * (gi + 1)], dil)
        o, l = _att_fwd(a, bias, gi)
        views.append(a)
        outs.append(o)
        lses.append(l)
    o_flat = [o.reshape(S, AW) for o in outs]
    l_flat = [l.reshape(S, AW) for l in lses]
    att = _mix_fwd(o_flat, l_flat)
    ret_out = _mm(gated, w_ret_out, 'nn', tm=512, tn=512, tk=2048, name="ret_out")
    att_out = _mm(att, w_att_out, 'nn', tm=512, tn=512, tk=AW, name="att_out")
    merged = _merge_fwd(proj, ret_out, att_out)
    mixo, x1 = _mm(merged, w_o, 'nn', tm=512, tn=512, tk=D, name="w_o", res=x, gvec=g1)
    h2 = _norm_mod_fwd(x1, norm2_g, sh2, sc2, "norm2_fwd")
    u = _mm(h2, w_ff1, 'nn', tm=512, tn=512, tk=D, name="ff1")
    act = _relu2_fwd(u)
    f, x2 = _mm(act, w_ff2, 'nn', tm=512, tn=512, tk=2048, name="ff2", res=x1, gvec=g2)
    loss, dx2, g_normf = _final_loss(x2, tgt, norm_f_g)

    df, dg2 = _gate_bwd(dx2, f, g2, "gate2_bwd")
    gw_ff2 = _mm(act, df, 'tn', tm=512, tn=512, tk=S, name="gw_ff2")
    d_act = _mm(df, w_ff2, 'nt', tm=512, tn=512, tk=D, name="d_act")
    du = _relu2_bwd(d_act, u)
    gw_ff1 = _mm(h2, du, 'tn', tm=512, tn=512, tk=S, name="gw_ff1")
    dh2 = _mm(du, w_ff1, 'nt', tm=512, tn=512, tk=2048, name="dh2")
    dx1, dsc2, dsh2, g_norm2 = _norm_mod_bwd(x1, norm2_g, sc2, dh2, dx2, "norm2_bwd")

    dmixo, dg1 = _gate_bwd(dx1, mixo, g1, "gate1_bwd")
    gw_o = _mm(merged, dmixo, 'tn', tm=512, tn=512, tk=S, name="gw_o")
    dmerged = _mm(dmixo, w_o, 'nt', tm=512, tn=512, tk=D, name="dmerged")
    d_ret_out, d_att_out, dga, dgb = _merge_bwd(proj, ret_out, att_out, dmerged)
    gw_ret_out = _mm(gated, d_ret_out, 'tn', tm=512, tn=512, tk=S, name="gw_ret_out")
    gw_att_out = _mm(att, d_att_out, 'tn', tm=AW, tn=512, tk=S, name="gw_att_out")
    dgated = _mm(d_ret_out, w_ret_out, 'nt', tm=512, tn=512, tk=D, name="dgated")
    datt = _mm(d_att_out, w_att_out, 'nt', tm=512, tn=AW, tk=D, name="datt")
    mix_grads = _mix_bwd(o_flat, l_flat, datt)
    datt_parts, ds_sums = [], []
    for gi, (_, dil) in enumerate(ATT_GROUPS):
        do = _dilated_view(mix_grads[gi], dil)
        dl = _dilated_view(mix_grads[3 + gi], dil)
        dq, dk, dv, ds_sum = _att_bwd(views[gi], bias, outs[gi], lses[gi], do, dl, gi)
        datt_parts += [dq.reshape(S, AW), dk.reshape(S, AW), dv.reshape(S, AW)]
        ds_sums.append(ds_sum)
    g_bias = _bias_grad(jnp.concatenate(ds_sums, axis=0), buckets)[:, :, 0].T
    dq_r, dk_r, dv_r, drg, g_gn_g, g_gn_b = _ret_bwd(proj, tables, gn_g, gn_b, ro, states, dgated)
    dproj = jnp.concatenate([dq_r, dk_r, dv_r, drg] + datt_parts + [dga, dgb], axis=1)
    gw_in = _mm(h1, dproj, 'tn', tm=512, tn=512, tk=S, name="gw_in")
    dh1 = _mm(dproj, w_in, 'nt', tm=512, tn=512, tk=2560, name="dh1")
    gx, dsc1, dsh1, g_norm1 = _norm_mod_bwd(x, norm1_g, sc1, dh1, dx1, "norm1_bwd")

    dmod = jnp.concatenate([dsh1, dsc1, dg1, dsh2, dsc2, dg2], axis=1)
    big = (gw_in, gw_ret_out, gw_att_out, gw_o, gw_ff1, gw_ff2)
    small_g = (g_norm1, g_bias, g_gn_g, g_gn_b, g_norm2, g_normf)
    return loss, gx, big, small_g, dmod


def _to_slots(g, axis):
    if axis == 0:
        return g.reshape(4, 2, g.shape[0] // N_DEV, g.shape[1])
    return g.reshape(g.shape[0], N_DEV, g.shape[1] // N_DEV).transpose(1, 0, 2).reshape(4, 2, g.shape[0], -1)


def _from_slots(w8, axis):
    if axis == 0:
        return w8.reshape(-1, w8.shape[2])
    return w8.transpose(1, 0, 2).reshape(w8.shape[1], -1)


BIG_AXES = (1, 0, 1, 0, 1, 0)


def kernel(x, c, w_ada, b_ada, norm1_g, w_in, rel_bias, ret_gn_g, ret_gn_b, w_ret_out, w_att_out, w_o, norm2_g, w_ff1, w_ff2, norm_f_g, loss_target, m_w_ada, m_b_ada, m_norm1_g, m_w_in, m_rel_bias, m_ret_gn_g, m_ret_gn_b, m_w_ret_out, m_w_att_out, m_w_o, m_norm2_g, m_w_ff1, m_w_ff2, m_norm_f_g, v_w_ada, v_b_ada, v_norm1_g, v_w_in, v_rel_bias, v_ret_gn_g, v_ret_gn_b, v_w_ret_out, v_w_att_out, v_w_o, v_norm2_g, v_w_ff1, v_w_ff2, v_norm_f_g):
    mx, my, mc = _mesh_pos()
    dev = 4 * mx + 2 * my + mc
    chip = jnp.reshape(2 * mx + my, (1,)).astype(jnp.int32)
    core = jnp.reshape(mc, (1,)).astype(jnp.int32)
    ada_w = D * 6 // N_DEV

    (c_all,) = _all_gather([c], "gather_c")
    c_all = c_all.reshape(N_DEV, D)
    b_sl = lax.dynamic_slice(b_ada, (0, dev * ada_w), (1, ada_w))
    (mod_all,) = _all_gather([_ada_fwd(c_all, w_ada[0], b_sl)], "gather_mod")
    mod = lax.dynamic_index_in_dim(mod_all, dev, axis=1, keepdims=False).reshape(6, D)
    mods = tuple(mod[i:i + 1] for i in range(6))

    shards = [w[0].astype(BF16) for w in (w_in, w_ret_out, w_att_out, w_o, w_ff1, w_ff2)]
    gathered = _all_gather(shards, "gather_w")
    wts = tuple(_from_slots(g, ax) for g, ax in zip(gathered, BIG_AXES))

    small = (norm1_g, rel_bias, ret_gn_g, ret_gn_b, norm2_g, norm_f_g.reshape(1, D))
    loss, gx, big_g, small_g, dmod = _local_step(x[0], loss_target[0], mods, wts, small)

    g_norm1, g_bias, g_gn_g, g_gn_b, g_norm2, g_normf = small_g
    pack = jnp.concatenate([dmod, g_norm1, g_bias.reshape(1, -1), g_gn_g, g_gn_b, g_norm2, g_normf, loss], axis=1)
    (pack_all,) = _all_gather([pack], "gather_small")
    tot = _sum_slots(pack_all, "sum_small")
    offs = np.cumsum([0, 6 * D, D, N_BUCKETS * 12, 2048, 2048, D, D, 128])
    seg = [tot[:, offs[i]:offs[i + 1]] for i in range(8)]
    g_b_ada, g_norm1, g_bias, g_gn_g, g_gn_b, g_norm2, g_normf = seg[:7]
    loss_out = seg[7][0, 0]
    dmod_all = pack_all[:, 0, :6 * D]
    g_w_ada = _ada_bwd(c_all, lax.dynamic_slice(dmod_all, (0, dev * ada_w), (N_DEV, ada_w)))

    fulls = [_to_slots(g, ax) for g, ax in zip(big_g, BIG_AXES)]
    recv_core = _exchange_core(fulls, "rs_core")
    parts = [_pair_sum(f, r, core, f"rs_pair{i}") for i, (f, r) in enumerate(zip(fulls, recv_core))]
    recv_chip = _exchange_chip(parts, "rs_chip")
    big_red = [_chip_sum(p, r, chip, f"rs_chip_sum{i}") for i, (p, r) in enumerate(zip(parts, recv_chip))]

    names = ['w_ada', 'b_ada', 'norm1_g', 'w_in', 'rel_bias', 'ret_gn_g', 'ret_gn_b', 'w_ret_out', 'w_att_out',
             'w_o', 'norm2_g', 'w_ff1', 'w_ff2', 'norm_f_g']
    ws = dict(zip(names, (w_ada, b_ada, norm1_g, w_in, rel_bias, ret_gn_g, ret_gn_b, w_ret_out, w_att_out, w_o,
                          norm2_g, w_ff1, w_ff2, norm_f_g)))
    ms = dict(zip(names, (m_w_ada, m_b_ada, m_norm1_g, m_w_in, m_rel_bias, m_ret_gn_g, m_ret_gn_b, m_w_ret_out,
                          m_w_att_out, m_w_o, m_norm2_g, m_w_ff1, m_w_ff2, m_norm_f_g)))
    vs = dict(zip(names, (v_w_ada, v_b_ada, v_norm1_g, v_w_in, v_rel_bias, v_ret_gn_g, v_ret_gn_b, v_w_ret_out,
                          v_w_att_out, v_w_o, v_norm2_g, v_w_ff1, v_w_ff2, v_norm_f_g)))
    grads = dict(w_ada=g_w_ada, w_in=big_red[0], w_ret_out=big_red[1], w_att_out=big_red[2], w_o=big_red[3],
                 w_ff1=big_red[4], w_ff2=big_red[5], b_ada=g_b_ada, norm1_g=g_norm1, rel_bias=g_bias,
                 ret_gn_g=g_gn_g, ret_gn_b=g_gn_b, norm2_g=g_norm2, norm_f_g=g_normf)
    delta, new_m, new_v = {}, {}, {}
    for n in ('w_ada', 'w_in', 'w_ret_out', 'w_att_out', 'w_o', 'w_ff1', 'w_ff2'):
        shp = ws[n].shape
        d_, m_, v_ = _adamw(ws[n][0], grads[n], ms[n][0], vs[n][0], "adamw_" + n)
        delta[n], new_m[n], new_v[n] = d_.reshape(shp), m_.reshape(shp), v_.reshape(shp)
        grads[n] = grads[n].reshape(shp)
    small_names = ('b_ada', 'norm1_g', 'rel_bias', 'ret_gn_g', 'ret_gn_b', 'norm2_g', 'norm_f_g')
    flat = lambda d: jnp.concatenate([d[n].reshape(1, -1) for n in small_names], axis=1)
    d_, m_, v_ = _adamw(flat(ws), flat(grads), flat(ms), flat(vs), "adamw_small")
    o = 0
    for n in small_names:
        shp = ws[n].shape
        sz = int(np.prod(shp))
        delta[n], new_m[n], new_v[n] = (t[:, o:o + sz].reshape(shp) for t in (d_, m_, v_))
        grads[n] = grads[n].reshape(shp)
        o += sz
    return (loss_out, gx[None], *[grads[n] for n in names], *[delta[n] for n in names],
            *[new_m[n] for n in names], *[new_v[n] for n in names])
```

```python
import functools
import math

import numpy as np
import jax
import jax.numpy as jnp
from jax import lax
from jax.experimental import pallas as pl
from jax.experimental.pallas import tpu as pltpu

F32 = jnp.float32
BF16 = jnp.bfloat16
MESH = pl.DeviceIdType.MESH

N_DEV = 8
S = 2048
D = 1024
RET_HEADS = 4
RET_DK = 256
RET_DV = 512
CHUNK = 128
N_CHUNK = S // CHUNK
ATT_GROUPS = ((128, 1), (512, 4), (2048, 16))
ATT_HG = 4
ATT_DH = 128
ATT_BLK = 128
N_BUCKETS = 32
MAX_DIST = 2048
D_FF = 4096
IN_COLS = 12800
OFF_RQ, OFF_RK, OFF_RV, OFF_RG, OFF_ATT, OFF_GA, OFF_GB = 0, 1024, 2048, 4096, 6144, 10752, 11776
RMS_EPS = 1e-6
GN_EPS = 1e-5
ADAM_LR, ADAM_B1, ADAM_B2, ADAM_EPS, ADAM_WD, ADAM_STEP = 0.001, 0.9, 0.999, 1e-08, 0.01, 10
VMEM_LIMIT = 48 * 1024 * 1024


def _pcall(body, **kw):
    return pl.pallas_call(body, **kw)


def _params(sem=None):
    return pltpu.CompilerParams(dimension_semantics=sem, vmem_limit_bytes=VMEM_LIMIT)


HBM_SPEC = pl.BlockSpec(memory_space=pl.ANY)


def _carry(body, comm, *, name, grid, in_specs, out_specs, out_shape, scratch_shapes=()):
    single = not isinstance(out_specs, (tuple, list))
    o_specs = (out_specs,) if single else tuple(out_specs)
    o_shape = (out_shape,) if single else tuple(out_shape)
    n_in, n_out, n_scr = len(in_specs), len(o_specs), len(scratch_shapes)
    nci, nco = len(comm.ins), len(comm.out_shape)
    total = int(np.prod(grid))

    def wrapped(*refs):
        bounds = np.cumsum([0, n_in, nci, n_out, nco, n_scr])
        a, ci, o, co, scr = (refs[bounds[i]:bounds[i + 1]] for i in range(5))
        sems = refs[bounds[5]:]
        flat = 0
        for d, g in enumerate(grid):
            flat = flat * g + pl.program_id(d)

        @pl.when(flat == 0)
        def _():
            comm.start(ci, co, sems)

        body(*a, *o, *scr)

        @pl.when(flat == total - 1)
        def _():
            comm.finish(ci, co, sems)

    call = _pcall(wrapped, name=name, grid=grid, in_specs=list(in_specs) + [HBM_SPEC] * nci,
                  out_specs=o_specs + (HBM_SPEC,) * nco, out_shape=o_shape + tuple(comm.out_shape),
                  scratch_shapes=list(scratch_shapes) + list(comm.sems),
                  compiler_params=_params(("arbitrary",) * len(grid)))

    def run(*args):
        res = call(*args, *comm.ins)
        own = res[0] if single else tuple(res[:n_out])
        return own, tuple(res[n_out:])

    return run


def _run_comm(comm, name):
    nci, nco = len(comm.ins), len(comm.out_shape)

    def body(*refs):
        ci, co, sems = refs[:nci], refs[nci:nci + nco], refs[nci + nco:]
        comm.start(ci, co, sems)
        comm.finish(ci, co, sems)

    return _pcall(body, name=name, in_specs=[HBM_SPEC] * nci, out_specs=(HBM_SPEC,) * nco,
                  out_shape=tuple(comm.out_shape), scratch_shapes=list(comm.sems))(*comm.ins)


def _dot(a, b, dn):
    return lax.dot_general(a.astype(BF16), b.astype(BF16), (dn, ((), ())), preferred_element_type=F32)


NN = ((1,), (0,))
NT = ((1,), (1,))
TN = ((0,), (0,))


def _mm(a, b, mode, *, tm, tn, tk, name, out_dtype=F32, res=None, gvec=None, comm=None):
    if mode == 'nn':
        (M, K), (_, N) = a.shape, b.shape
        a_spec = pl.BlockSpec((tm, tk), lambda i, j, k: (i, k))
        b_spec = pl.BlockSpec((tk, tn), lambda i, j, k: (k, j))
        dn = NN
    elif mode == 'nt':
        (M, K), (N, _) = a.shape, b.shape
        a_spec = pl.BlockSpec((tm, tk), lambda i, j, k: (i, k))
        b_spec = pl.BlockSpec((tn, tk), lambda i, j, k: (j, k))
        dn = NT
    else:
        (K, M), (_, N) = a.shape, b.shape
        a_spec = pl.BlockSpec((tk, tm), lambda i, j, k: (k, i))
        b_spec = pl.BlockSpec((tk, tn), lambda i, j, k: (k, j))
        dn = TN
    assert M % tm == 0 and N % tn == 0 and K % tk == 0, (name, M, N, K)
    nk = K // tk
    fused = res is not None
    o_spec = pl.BlockSpec((tm, tn), lambda i, j, k: (i, j))

    def body(a_ref, b_ref, *rest):
        if fused:
            res_ref, g_ref, o_ref, x_ref, acc_ref = rest
        else:
            o_ref, acc_ref = rest

        def finish(acc):
            o_ref[...] = acc.astype(o_ref.dtype)
            if fused:
                x_ref[...] = res_ref[...] + g_ref[...] * acc

        p = _dot(a_ref[...], b_ref[...], dn)
        if nk == 1:
            finish(p)
        else:
            k = pl.program_id(2)

            @pl.when(k == 0)
            def _():
                acc_ref[...] = p

            @pl.when(k > 0)
            def _():
                acc_ref[...] += p

            @pl.when(k == nk - 1)
            def _():
                finish(acc_ref[...])

    in_specs = [a_spec, b_spec]
    args = [a, b]
    out_shape = jax.ShapeDtypeStruct((M, N), out_dtype)
    out_specs = o_spec
    if fused:
        in_specs += [pl.BlockSpec((tm, tn), lambda i, j, k: (i, j)), pl.BlockSpec((1, tn), lambda i, j, k: (0, j))]
        args += [res, gvec]
        out_shape = (out_shape, jax.ShapeDtypeStruct((M, N), F32))
        out_specs = (o_spec, pl.BlockSpec((tm, tn), lambda i, j, k: (i, j)))
    kw = dict(name=name, grid=(M // tm, N // tn, nk), in_specs=in_specs, out_specs=out_specs,
              out_shape=out_shape, scratch_shapes=[pltpu.VMEM((tm, tn), F32)])
    if comm is not None:
        return _carry(body, comm, **kw)(*args)
    return _pcall(body, compiler_params=_params(("parallel", "parallel", "arbitrary")), **kw)(*args)


TR = 256


def _row_spec(w=D):
    return pl.BlockSpec((TR, w), lambda i: (i, 0))


def _vec_spec(w=D):
    return pl.BlockSpec((1, w), lambda i: (0, 0))


def _norm_mod_fwd(x, g, sh, sc, name):
    def body(x_ref, g_ref, sh_ref, sc_ref, o_ref):
        xv = x_ref[...]
        rstd = lax.rsqrt(jnp.mean(xv * xv, axis=-1, keepdims=True) + RMS_EPS)
        n = xv * rstd * g_ref[...]
        o_ref[...] = (n * (1.0 + sc_ref[...]) + sh_ref[...]).astype(BF16)

    return _pcall(body, name=name, grid=(S // TR,), in_specs=[_row_spec(), _vec_spec(), _vec_spec(), _vec_spec()],
                  out_specs=_row_spec(), out_shape=jax.ShapeDtypeStruct((S, D), BF16),
                  compiler_params=_params(("parallel",)))(x, g, sh, sc)


def _norm_mod_bwd(x, g, sc, dh, dres, name):
    def body(x_ref, g_ref, sc_ref, dh_ref, dres_ref, dx_ref, dsc_ref, dsh_ref, dg_ref):
        i = pl.program_id(0)
        xv = x_ref[...]
        dh = dh_ref[...]
        rstd = lax.rsqrt(jnp.mean(xv * xv, axis=-1, keepdims=True) + RMS_EPS)
        xhat = xv * rstd
        gv = g_ref[...]
        dn = dh * (1.0 + sc_ref[...])
        dxhat = dn * gv
        dx_ref[...] = dres_ref[...] + rstd * (dxhat - xhat * jnp.mean(dxhat * xhat, axis=-1, keepdims=True))
        p_sc = jnp.sum(dh * (xhat * gv), axis=0, keepdims=True)
        p_sh = jnp.sum(dh, axis=0, keepdims=True)
        p_g = jnp.sum(dn * xhat, axis=0, keepdims=True)

        @pl.when(i == 0)
        def _():
            dsc_ref[...] = p_sc
            dsh_ref[...] = p_sh
            dg_ref[...] = p_g

        @pl.when(i > 0)
        def _():
            dsc_ref[...] += p_sc
            dsh_ref[...] += p_sh
            dg_ref[...] += p_g

    vec = jax.ShapeDtypeStruct((1, D), F32)
    return _pcall(body, name=name, grid=(S // TR,),
                  in_specs=[_row_spec(), _vec_spec(), _vec_spec(), _row_spec(), _row_spec()],
                  out_specs=(_row_spec(), _vec_spec(), _vec_spec(), _vec_spec()),
                  out_shape=(jax.ShapeDtypeStruct((S, D), F32), vec, vec, vec),
                  compiler_params=_params(("arbitrary",)))(x, g, sc, dh, dres)


def _gate_bwd(dxr, fval, gvec, name):
    def body(dx_ref, f_ref, g_ref, dz_ref, dg_ref):
        i = pl.program_id(0)
        dx = dx_ref[...]
        dz_ref[...] = (dx * g_ref[...]).astype(BF16)
        p = jnp.sum(dx * f_ref[...], axis=0, keepdims=True)

        @pl.when(i == 0)
        def _():
            dg_ref[...] = p

        @pl.when(i > 0)
        def _():
            dg_ref[...] += p

    return _pcall(body, name=name, grid=(S // TR,), in_specs=[_row_spec(), _row_spec(), _vec_spec()],
                  out_specs=(_row_spec(), _vec_spec()),
                  out_shape=(jax.ShapeDtypeStruct((S, D), BF16), jax.ShapeDtypeStruct((1, D), F32)),
                  compiler_params=_params(("arbitrary",)))(dxr, fval, gvec)


def _relu2_fwd(u):
    def body(u_ref, a_ref):
        r = jnp.maximum(u_ref[...], 0.0)
        a_ref[...] = (r * r).astype(BF16)

    return _pcall(body, name="relu2_fwd", grid=(S // TR,), in_specs=[_row_spec(D_FF)], out_specs=_row_spec(D_FF),
                  out_shape=jax.ShapeDtypeStruct((S, D_FF), BF16), compiler_params=_params(("parallel",)))(u)


def _relu2_bwd(da, u):
    def body(da_ref, u_ref, du_ref):
        du_ref[...] = (da_ref[...] * (2.0 * jnp.maximum(u_ref[...], 0.0))).astype(BF16)

    return _pcall(body, name="relu2_bwd", grid=(S // TR,), in_specs=[_row_spec(D_FF), _row_spec(D_FF)],
                  out_specs=_row_spec(D_FF), out_shape=jax.ShapeDtypeStruct((S, D_FF), BF16),
                  compiler_params=_params(("parallel",)))(da, u)


def _final_loss(x2, tgt, g):
    def body(x_ref, t_ref, g_ref, loss_ref, dx_ref, dg_ref):
        i = pl.program_id(0)
        xv = x_ref[...]
        gv = g_ref[...]
        rstd = lax.rsqrt(jnp.mean(xv * xv, axis=-1, keepdims=True) + RMS_EPS)
        xhat = xv * rstd
        err = xhat * gv - t_ref[...]
        dy = err * (1.0 / D)
        dxhat = dy * gv
        dx_ref[...] = rstd * (dxhat - xhat * jnp.mean(dxhat * xhat, axis=-1, keepdims=True))
        p_g = jnp.sum(dy * xhat, axis=0, keepdims=True)
        p_l = jnp.zeros((1, 128), F32) + 0.5 * jnp.sum(jnp.mean(err * err, axis=-1, keepdims=True))

        @pl.when(i == 0)
        def _():
            dg_ref[...] = p_g
            loss_ref[...] = p_l

        @pl.when(i > 0)
        def _():
            dg_ref[...] += p_g
            loss_ref[...] += p_l

    return _pcall(body, name="final_loss", grid=(S // TR,), in_specs=[_row_spec(), _row_spec(), _vec_spec()],
                  out_specs=(_vec_spec(128), _row_spec(), _vec_spec()),
                  out_shape=(jax.ShapeDtypeStruct((1, 128), F32), jax.ShapeDtypeStruct((S, D), F32),
                             jax.ShapeDtypeStruct((1, D), F32)),
                  compiler_params=_params(("arbitrary",)))(x2, tgt, g)


HALF = 512


def _merge_fwd(proj, ret_out, att_out):
    def body(ga_ref, gb_ref, r_ref, a_ref, o_ref):
        o_ref[...] = (jax.nn.sigmoid(ga_ref[...]) * r_ref[...] + jax.nn.sigmoid(gb_ref[...]) * a_ref[...]).astype(BF16)

    blk = lambda off: pl.BlockSpec((TR, HALF), lambda i, j: (i, off // HALF + j))
    return _pcall(body, name="merge_fwd", grid=(S // TR, D // HALF),
                  in_specs=[blk(OFF_GA), blk(OFF_GB), blk(0), blk(0)], out_specs=blk(0),
                  out_shape=jax.ShapeDtypeStruct((S, D), BF16),
                  compiler_params=_params(("parallel", "parallel")))(proj, proj, ret_out, att_out)


def _merge_bwd(proj, ret_out, att_out, dmerged):
    def body(ga_ref, gb_ref, r_ref, a_ref, dm_ref, dr_ref, da_ref, dga_ref, dgb_ref):
        sa = jax.nn.sigmoid(ga_ref[...])
        sb = jax.nn.sigmoid(gb_ref[...])
        dm = dm_ref[...]
        dr_ref[...] = (dm * sa).astype(BF16)
        da_ref[...] = (dm * sb).astype(BF16)
        dga_ref[...] = (dm * r_ref[...] * (sa * (1.0 - sa))).astype(BF16)
        dgb_ref[...] = (dm * a_ref[...] * (sb * (1.0 - sb))).astype(BF16)

    blk = lambda off: pl.BlockSpec((TR, HALF), lambda i, j: (i, off // HALF + j))
    o = jax.ShapeDtypeStruct((S, D), BF16)
    return _pcall(body, name="merge_bwd", grid=(S // TR, D // HALF),
                  in_specs=[blk(OFF_GA), blk(OFF_GB), blk(0), blk(0), blk(0)], out_specs=(blk(0),) * 4,
                  out_shape=(o, o, o, o),
                  compiler_params=_params(("parallel", "parallel")))(proj, proj, ret_out, att_out, dmerged)


def _ret_tables():
    H, C = RET_HEADS, CHUNK
    log_g = jnp.log1p(-(2.0 ** (-5.0 - jnp.arange(H, dtype=F32))))
    idx = jnp.arange(C, dtype=F32)
    rel = idx[:, None] - idx[None, :]
    inner = jnp.where(rel >= 0, jnp.exp(log_g[:, None, None] * jnp.maximum(rel, 0.0)), 0.0)
    qd = jnp.exp(log_g[:, None] * (idx + 1.0))[:, :, None]
    kd = jnp.exp(log_g[:, None] * (C - 1.0 - idx))[:, :, None]
    cd = jnp.broadcast_to(jnp.exp(log_g * C)[:, None, None], (H, 1, 128))
    half = RET_DK // 2
    inv = 10000.0 ** (-jnp.arange(half, dtype=F32) / half)
    ang = jnp.arange(S, dtype=F32)[:, None] * inv[None, :]
    return inner, qd, kd, cd, jnp.cos(ang), jnp.sin(ang)


def _rot(x, cos, sin):
    x1, x2 = x[:, :128], x[:, 128:]
    return jnp.concatenate([x1 * cos - x2 * sin, x1 * sin + x2 * cos], axis=1)


def _rot_t(d, cos, sin):
    d1, d2 = d[:, :128], d[:, 128:]
    return jnp.concatenate([d1 * cos + d2 * sin, d2 * cos - d1 * sin], axis=1)


def _ret_in_specs(chunk_of):
    ci = chunk_of
    return [
        pl.BlockSpec((CHUNK, RET_DK), lambda h, t: (ci(t), OFF_RQ // RET_DK + h)),
        pl.BlockSpec((CHUNK, RET_DK), lambda h, t: (ci(t), OFF_RK // RET_DK + h)),
        pl.BlockSpec((CHUNK, RET_DV), lambda h, t: (ci(t), OFF_RV // RET_DV + h)),
        pl.BlockSpec((CHUNK, RET_DV), lambda h, t: (ci(t), OFF_RG // RET_DV + h)),
        pl.BlockSpec((CHUNK, 128), lambda h, t: (ci(t), 0)),
        pl.BlockSpec((CHUNK, 128), lambda h, t: (ci(t), 0)),
        pl.BlockSpec((None, CHUNK, CHUNK), lambda h, t: (h, 0, 0)),
        pl.BlockSpec((None, CHUNK, 1), lambda h, t: (h, 0, 0)),
        pl.BlockSpec((None, CHUNK, 1), lambda h, t: (h, 0, 0)),
        pl.BlockSpec((None, 1, 128), lambda h, t: (h, 0, 0)),
    ]


def _ret_fwd(proj, tables, gn_g, gn_b):
    inner, qd, kd, cd, cos, sin = tables

    def body(q_ref, k_ref, v_ref, rg_ref, cos_ref, sin_ref, in_ref, qd_ref, kd_ref, cd_ref, g_ref, b_ref,
             gated_ref, ro_ref, st_ref, s_scr):
        i = pl.program_id(1)

        @pl.when(i == 0)
        def _():
            s_scr[...] = jnp.zeros_like(s_scr)

        cosv, sinv = cos_ref[...], sin_ref[...]
        q = _rot(q_ref[...], cosv, sinv)
        k = _rot(k_ref[...], cosv, sinv) * (RET_DK ** -0.5)
        v = v_ref[...]
        st = s_scr[...]
        st_ref[...] = st
        s = _dot(q, k, NT) * in_ref[...]
        o = _dot(s, v, NN) + _dot(q, st, NN) * qd_ref[...]
        s_scr[...] = st * cd_ref[:, :1] + _dot(k * kd_ref[...], v, TN)
        ro_ref[...] = o
        mu = jnp.mean(o, axis=-1, keepdims=True)
        oc = o - mu
        var = jnp.mean(oc * oc, axis=-1, keepdims=True)
        rn = oc * lax.rsqrt(var + GN_EPS) * g_ref[...] + b_ref[...]
        rg = rg_ref[...]
        gated_ref[...] = (rg * jax.nn.sigmoid(rg) * rn).astype(BF16)

    vspec = pl.BlockSpec((1, RET_DV), lambda h, t: (0, h))
    ospec = pl.BlockSpec((CHUNK, RET_DV), lambda h, t: (t, h))
    return _pcall(body, name="ret_fwd", grid=(RET_HEADS, N_CHUNK),
                  in_specs=_ret_in_specs(lambda t: t) + [vspec, vspec],
                  out_specs=(ospec, ospec, pl.BlockSpec((None, None, RET_DK, RET_DV), lambda h, t: (h, t, 0, 0))),
                  out_shape=(jax.ShapeDtypeStruct((S, RET_HEADS * RET_DV), BF16),
                             jax.ShapeDtypeStruct((S, RET_HEADS * RET_DV), F32),
                             jax.ShapeDtypeStruct((RET_HEADS, N_CHUNK, RET_DK, RET_DV), F32)),
                  scratch_shapes=[pltpu.VMEM((RET_DK, RET_DV), F32)],
                  compiler_params=_params(("parallel", "arbitrary")))(
        proj, proj, proj, proj, cos, sin, inner, qd, kd, cd, gn_g, gn_b)


def _ret_bwd(proj, tables, gn_g, gn_b, ro, states, dgated):
    inner, qd, kd, cd, cos, sin = tables
    last = N_CHUNK - 1

    def body(q_ref, k_ref, v_ref, rg_ref, cos_ref, sin_ref, in_ref, qd_ref, kd_ref, cd_ref, g_ref, b_ref,
             ro_ref, st_ref, dg_ref, dq_ref, dk_ref, dv_ref, drg_ref, gg_ref, gb_ref, gs_scr):
        t = pl.program_id(1)
        cosv, sinv = cos_ref[...], sin_ref[...]
        q = _rot(q_ref[...], cosv, sinv)
        k = _rot(k_ref[...], cosv, sinv) * (RET_DK ** -0.5)
        v = v_ref[...]
        qdv, kdv, dm = qd_ref[...], kd_ref[...], in_ref[...]
        st = st_ref[...]
        o = ro_ref[...]
        gv = g_ref[...]
        mu = jnp.mean(o, axis=-1, keepdims=True)
        oc = o - mu
        rstd = lax.rsqrt(jnp.mean(oc * oc, axis=-1, keepdims=True) + GN_EPS)
        ohat = oc * rstd
        rn = ohat * gv + b_ref[...]
        rg = rg_ref[...]
        sg = jax.nn.sigmoid(rg)
        dgt = dg_ref[...]
        drn = dgt * (rg * sg)
        drg_ref[...] = (dgt * rn * (sg * (1.0 + rg * (1.0 - sg)))).astype(BF16)
        p_g = jnp.sum(drn * ohat, axis=0, keepdims=True)
        p_b = jnp.sum(drn, axis=0, keepdims=True)

        @pl.when(t == 0)
        def _():
            gs_scr[...] = jnp.zeros_like(gs_scr)
            gg_ref[...] = p_g
            gb_ref[...] = p_b

        @pl.when(t > 0)
        def _():
            gg_ref[...] += p_g
            gb_ref[...] += p_b

        dohat = drn * gv
        do = rstd * (dohat - jnp.mean(dohat, axis=-1, keepdims=True)
                     - ohat * jnp.mean(dohat * ohat, axis=-1, keepdims=True))
        gs = gs_scr[...]
        s = _dot(q, k, NT) * dm
        dsr = _dot(do, v, NT) * dm
        dq = _dot(dsr, k, NN) + _dot(do, st, NT) * qdv
        dk = _dot(dsr, q, TN) + _dot(v, gs, NT) * kdv
        dv = _dot(s, do, TN) + _dot(k * kdv, gs, NN)
        gs_scr[...] = gs * cd_ref[:, :1] + _dot(q * qdv, do, TN)
        dq_ref[...] = _rot_t(dq, cosv, sinv).astype(BF16)
        dk_ref[...] = (_rot_t(dk, cosv, sinv) * (RET_DK ** -0.5)).astype(BF16)
        dv_ref[...] = dv.astype(BF16)

    rev = lambda t: last - t
    vspec = pl.BlockSpec((1, RET_DV), lambda h, t: (0, h))
    vblk = pl.BlockSpec((CHUNK, RET_DV), lambda h, t: (rev(t), h))
    qblk = pl.BlockSpec((CHUNK, RET_DK), lambda h, t: (rev(t), h))
    return _pcall(body, name="ret_bwd", grid=(RET_HEADS, N_CHUNK),
                  in_specs=_ret_in_specs(rev) + [vspec, vspec, vblk,
                                                 pl.BlockSpec((None, None, RET_DK, RET_DV), lambda h, t: (h, rev(t), 0, 0)),
                                                 vblk],
                  out_specs=(qblk, qblk, vblk, vblk, vspec, vspec),
                  out_shape=(jax.ShapeDtypeStruct((S, RET_HEADS * RET_DK), BF16),
                             jax.ShapeDtypeStruct((S, RET_HEADS * RET_DK), BF16),
                             jax.ShapeDtypeStruct((S, RET_HEADS * RET_DV), BF16),
                             jax.ShapeDtypeStruct((S, RET_HEADS * RET_DV), BF16),
                             jax.ShapeDtypeStruct((1, RET_HEADS * RET_DV), F32),
                             jax.ShapeDtypeStruct((1, RET_HEADS * RET_DV), F32)),
                  scratch_shapes=[pltpu.VMEM((RET_DK, RET_DV), F32)],
                  compiler_params=_params(("parallel", "arbitrary")))(
        proj, proj, proj, proj, cos, sin, inner, qd, kd, cd, gn_g, gn_b, ro, states, dgated)


def _bucket_tables():
    qi = np.arange(ATT_BLK)[:, None]
    kj = np.arange(2 * ATT_BLK)[None, :]
    m = ATT_BLK + qi - kj
    out = []
    for win, dil in ATT_GROUPS:
        w = win // dil
        dist = (np.clip(m, 0, w) * dil).astype(np.int32)
        max_exact = N_BUCKETS // 2
        d_f = np.maximum(dist, 1).astype(np.float32)
        large = max_exact + (np.log(d_f / np.float32(max_exact)) / np.float32(math.log(MAX_DIST / max_exact))
                             * np.float32(N_BUCKETS - max_exact)).astype(np.int32)
        large = np.minimum(large, N_BUCKETS - 1)
        out.append(np.where(dist < max_exact, dist, large).astype(np.int32))
    return np.stack(out)


def _bias_build(rel_bias, buckets):
    def body(tab_ref, bk_ref, o_ref):
        hh = pl.program_id(0)
        bk = bk_ref[...]
        acc = jnp.zeros((ATT_BLK, 2 * ATT_BLK), F32)
        for b in range(N_BUCKETS):
            acc = jnp.where(bk == b, tab_ref[b, hh], acc)
        o_ref[...] = acc

    nh = len(ATT_GROUPS) * ATT_HG
    return _pcall(body, name="bias_build", grid=(nh,),
                  in_specs=[pl.BlockSpec(memory_space=pltpu.SMEM),
                            pl.BlockSpec((None, ATT_BLK, 2 * ATT_BLK), lambda hh: (hh // ATT_HG, 0, 0))],
                  out_specs=pl.BlockSpec((None, ATT_BLK, 2 * ATT_BLK), lambda hh: (hh, 0, 0)),
                  out_shape=jax.ShapeDtypeStruct((nh, ATT_BLK, 2 * ATT_BLK), F32),
                  compiler_params=_params(("parallel",)))(rel_bias, buckets)


def _bias_grad(ds_sum, buckets):
    def body(ds_ref, bk_ref, o_ref):
        bk = bk_ref[...]
        ds = ds_ref[...]
        rows = lax.broadcasted_iota(jnp.int32, (N_BUCKETS, 128), 0)
        acc = jnp.zeros((N_BUCKETS, 128), F32)
        for b in range(N_BUCKETS):
            acc = jnp.where(rows == b, jnp.sum(jnp.where(bk == b, ds, 0.0)), acc)
        o_ref[...] = acc

    nh = len(ATT_GROUPS) * ATT_HG
    return _pcall(body, name="bias_grad", grid=(nh,),
                  in_specs=[pl.BlockSpec((None, ATT_BLK, 2 * ATT_BLK), lambda hh: (hh, 0, 0)),
                            pl.BlockSpec((None, ATT_BLK, 2 * ATT_BLK), lambda hh: (hh // ATT_HG, 0, 0))],
                  out_specs=pl.BlockSpec((None, N_BUCKETS, 128), lambda hh: (hh, 0, 0)),
                  out_shape=jax.ShapeDtypeStruct((nh, N_BUCKETS, 128), F32),
                  compiler_params=_params(("parallel",)))(ds_sum, buckets)


def _att_valid(n):
    qi = lax.broadcasted_iota(jnp.int32, (ATT_BLK, 2 * ATT_BLK), 0)
    kj = lax.broadcasted_iota(jnp.int32, (ATT_BLK, 2 * ATT_BLK), 1)
    m = ATT_BLK + qi - kj
    first_key = jnp.where(n > 0, 0, ATT_BLK)
    return (m >= 0) & (m <= ATT_BLK) & (kj >= first_key)


def _att_fwd(a, bias, gi):
    _, dil = ATT_GROUPS[gi]
    L = S // dil
    nb = L // ATT_BLK
    scale = ATT_DH ** -0.5

    def body(q_ref, kc_ref, kp_ref, vc_ref, vp_ref, bias_ref, o_ref, l_ref):
        n = pl.program_id(2)
        kk = jnp.concatenate([kp_ref[...], kc_ref[...]], axis=0)
        vv = jnp.concatenate([vp_ref[...], vc_ref[...]], axis=0)
        s = _dot(q_ref[...], kk, NT) * scale + bias_ref[...]
        s = jnp.where(_att_valid(n), s, -1e30)
        mx = jnp.max(s, axis=-1, keepdims=True)
        e = jnp.exp(s - mx)
        den = jnp.sum(e, axis=-1, keepdims=True)
        o_ref[...] = _dot(e / den, vv, NN)
        l_ref[...] = jnp.broadcast_to(mx + jnp.log(den), (ATT_BLK, ATT_DH))

    blk = lambda part, prev: pl.BlockSpec(
        (ATT_BLK, ATT_DH), lambda r, h, n: (jnp.maximum(n - 1, 0) if prev else n, r * 12 + 4 * part + h))
    oblk = pl.BlockSpec((ATT_BLK, ATT_DH), lambda r, h, n: (n, r * 4 + h))
    osh = jax.ShapeDtypeStruct((L, dil * ATT_HG * ATT_DH), F32)
    return _pcall(body, name=f"att_fwd{gi}", grid=(dil, ATT_HG, nb),
                  in_specs=[blk(0, False), blk(1, False), blk(1, True), blk(2, False), blk(2, True),
                            pl.BlockSpec((None, ATT_BLK, 2 * ATT_BLK), lambda r, h, n: (gi * ATT_HG + h, 0, 0))],
                  out_specs=(oblk, oblk), out_shape=(osh, osh),
                  compiler_params=_params(("parallel", "parallel", "arbitrary")))(a, a, a, a, a, bias)


def _att_bwd(a, bias, o, lse, do, dlse, gi, comm=None):
    _, dil = ATT_GROUPS[gi]
    L = S // dil
    nb = L // ATT_BLK
    scale = ATT_DH ** -0.5

    def body(q_ref, kc_ref, kp_ref, vc_ref, vp_ref, bias_ref, o_ref, l_ref, do_ref, dl_ref,
             dq_ref, dk_ref, dv_ref, ds_ref, ck_scr, cv_scr):
        r = pl.program_id(1)
        n = pl.program_id(2)

        @pl.when((r == 0) & (n == 0))
        def _():
            ds_ref[...] = jnp.zeros_like(ds_ref)

        @pl.when(n < nb)
        def _():
            q = q_ref[...]
            kk = jnp.concatenate([kp_ref[...], kc_ref[...]], axis=0)
            vv = jnp.concatenate([vp_ref[...], vc_ref[...]], axis=0)
            dov = do_ref[...]
            s = _dot(q, kk, NT) * scale + bias_ref[...]
            p = jnp.where(_att_valid(n), jnp.exp(s - l_ref[:, :1]), 0.0)
            dp = _dot(dov, vv, NT)
            delta = jnp.sum(dov * o_ref[...], axis=-1, keepdims=True)
            ds = p * (dp - delta + dl_ref[:, :1])
            ds_ref[...] += ds
            dq_ref[...] = (_dot(ds, kk, NN) * scale).astype(BF16)
            dkk = _dot(ds, q, TN) * scale
            dvv = _dot(p, dov, TN)

            @pl.when(n > 0)
            def _():
                dk_ref[...] = (ck_scr[...] + dkk[:ATT_BLK]).astype(BF16)
                dv_ref[...] = (cv_scr[...] + dvv[:ATT_BLK]).astype(BF16)

            ck_scr[...] = dkk[ATT_BLK:]
            cv_scr[...] = dvv[ATT_BLK:]

        @pl.when(n == nb)
        def _():
            dk_ref[...] = ck_scr[...].astype(BF16)
            dv_ref[...] = cv_scr[...].astype(BF16)

    cur = lambda n: jnp.minimum(n, nb - 1)
    prv = lambda n: jnp.maximum(jnp.minimum(n, nb - 1) - 1, 0)
    blk = lambda part, prev: pl.BlockSpec(
        (ATT_BLK, ATT_DH), lambda h, r, n: (prv(n) if prev else cur(n), r * 12 + 4 * part + h))
    oblk = pl.BlockSpec((ATT_BLK, ATT_DH), lambda h, r, n: (cur(n), r * 4 + h))
    kvblk = pl.BlockSpec((ATT_BLK, ATT_DH), lambda h, r, n: (jnp.maximum(n - 1, 0), r * 4 + h))
    wide = pl.BlockSpec((None, ATT_BLK, 2 * ATT_BLK), lambda h, r, n: (gi * ATT_HG + h, 0, 0))
    osh = jax.ShapeDtypeStruct((L, dil * ATT_HG * ATT_DH), BF16)
    kw = dict(name=f"att_bwd{gi}", grid=(ATT_HG, dil, nb + 1),
              in_specs=[blk(0, False), blk(1, False), blk(1, True), blk(2, False), blk(2, True), wide,
                        oblk, oblk, oblk, oblk],
              out_specs=(oblk, kvblk, kvblk,
                         pl.BlockSpec((None, ATT_BLK, 2 * ATT_BLK), lambda h, r, n: (h, 0, 0))),
              out_shape=(osh, osh, osh, jax.ShapeDtypeStruct((ATT_HG, ATT_BLK, 2 * ATT_BLK), F32)),
              scratch_shapes=[pltpu.VMEM((ATT_BLK, ATT_DH), F32), pltpu.VMEM((ATT_BLK, ATT_DH), F32)])
    args = (a, a, a, a, a, bias, o, lse, do, dlse)
    if comm is not None:
        return _carry(body, comm, **kw)(*args)
    return _pcall(body, compiler_params=_params(("parallel", "arbitrary", "arbitrary")), **kw)(*args)


AW = ATT_HG * ATT_DH


def _mix_weights(l0, l1, l2):
    mx = jnp.maximum(jnp.maximum(l0, l1), l2)
    e0, e1, e2 = jnp.exp(l0 - mx), jnp.exp(l1 - mx), jnp.exp(l2 - mx)
    den = e0 + e1 + e2
    return e0 / den, e1 / den, e2 / den


def _mix_fwd(os_, ls):
    def body(o0, o1, o2, l0, l1, l2, att_ref):
        w0, w1, w2 = _mix_weights(l0[...], l1[...], l2[...])
        att_ref[...] = (w0 * o0[...] + w1 * o1[...] + w2 * o2[...]).astype(BF16)

    return _pcall(body, name="mix_fwd", grid=(S // TR,), in_specs=[_row_spec(AW)] * 6, out_specs=_row_spec(AW),
                  out_shape=jax.ShapeDtypeStruct((S, AW), BF16), compiler_params=_params(("parallel",)))(*os_, *ls)


def _mix_bwd(os_, ls, datt):
    def body(o0, o1, o2, l0, l1, l2, da_ref, d0, d1, d2, e0, e1, e2):
        ws = _mix_weights(l0[...], l1[...], l2[...])
        da = da_ref[...]
        dws = []
        for o_ref, w, d_ref in zip((o0, o1, o2), ws, (d0, d1, d2)):
            d_ref[...] = w * da
            prod = da * o_ref[...]
            parts = [jnp.broadcast_to(jnp.sum(prod[:, h * ATT_DH:(h + 1) * ATT_DH], axis=-1, keepdims=True),
                                      (TR, ATT_DH)) for h in range(ATT_HG)]
            dws.append(jnp.concatenate(parts, axis=1))
        tot = ws[0] * dws[0] + ws[1] * dws[1] + ws[2] * dws[2]
        for w, dw, e_ref in zip(ws, dws, (e0, e1, e2)):
            e_ref[...] = w * (dw - tot)

    o = jax.ShapeDtypeStruct((S, AW), F32)
    return _pcall(body, name="mix_bwd", grid=(S // TR,), in_specs=[_row_spec(AW)] * 7, out_specs=(_row_spec(AW),) * 6,
                  out_shape=(o,) * 6, compiler_params=_params(("parallel",)))(*os_, *ls, datt)


def _ada_fwd(c_all, w_sh, b_sl):
    def body(c_ref, w_ref, b_ref, o_ref):
        cv = c_ref[...]
        o_ref[...] = _dot(cv * jax.nn.sigmoid(cv), w_ref[...], NN) + b_ref[...]

    return _pcall(body, name="ada_fwd", out_shape=jax.ShapeDtypeStruct((N_DEV, w_sh.shape[1]), F32),
                  compiler_params=_params())(c_all, w_sh, b_sl)


def _ada_bwd(c_all, dm_sl):
    def body(c_ref, d_ref, o_ref):
        cv = c_ref[...]
        o_ref[...] = _dot(cv * jax.nn.sigmoid(cv), d_ref[...], TN)

    return _pcall(body, name="ada_bwd", out_shape=jax.ShapeDtypeStruct((D, dm_sl.shape[1]), F32),
                  compiler_params=_params())(c_all, dm_sl)


def _sum_slots(g, name):
    n = g.shape[0]

    def body(g_ref, o_ref):
        acc = g_ref[0]
        for e in range(1, n):
            acc = acc + g_ref[e]
        o_ref[...] = acc

    return _pcall(body, name=name, out_shape=jax.ShapeDtypeStruct(g.shape[1:], F32), compiler_params=_params())(g)


def _row_tile(m, n):
    t = max(8, min(m, (1 << 19) // n // 8 * 8))
    while m % t:
        t -= 8
    return t


def _pair_sum(full, recv, sel, name):
    _, _, m, n = full.shape
    t = _row_tile(m, n)

    def body(sel_ref, a_ref, b_ref, o_ref):
        o_ref[...] = (a_ref[...].astype(F32) + b_ref[...].astype(F32)).astype(o_ref.dtype)

    gs = pltpu.PrefetchScalarGridSpec(
        num_scalar_prefetch=1, grid=(4, m // t),
        in_specs=[pl.BlockSpec((None, None, t, n), lambda q, i, s: (q, s[0], i, 0)),
                  pl.BlockSpec((None, t, n), lambda q, i, s: (q, i, 0))],
        out_specs=pl.BlockSpec((None, t, n), lambda q, i, s: (q, i, 0)))
    return _pcall(body, name=name, grid_spec=gs, out_shape=jax.ShapeDtypeStruct((4, m, n), full.dtype),
                  compiler_params=_params(("parallel", "parallel")))(sel, full, recv)


def _chip_sum(part, recv, sel, name):
    _, m, n = part.shape
    t = _row_tile(m, n)

    def body(sel_ref, a_ref, r_ref, o_ref):
        o_ref[...] = ((a_ref[...].astype(F32) + r_ref[0].astype(F32)) + r_ref[1].astype(F32)) + r_ref[2].astype(F32)

    gs = pltpu.PrefetchScalarGridSpec(
        num_scalar_prefetch=1, grid=(m // t,),
        in_specs=[pl.BlockSpec((None, t, n), lambda i, s: (s[0], i, 0)),
                  pl.BlockSpec((3, t, n), lambda i, s: (0, i, 0))],
        out_specs=pl.BlockSpec((t, n), lambda i, s: (i, 0)))
    return _pcall(body, name=name, grid_spec=gs, out_shape=jax.ShapeDtypeStruct((m, n), F32),
                  compiler_params=_params(("parallel",)))(sel, part, recv)


def _adamw(w, g, m, v, name):
    rows, cols = w.shape
    t = _row_tile(rows, cols) if rows >= 8 else rows

    def body(w_ref, g_ref, m_ref, v_ref, d_ref, nm_ref, nv_ref):
        gv = g_ref[...]
        nm = ADAM_B1 * m_ref[...] + (1.0 - ADAM_B1) * gv
        nv = ADAM_B2 * v_ref[...] + (1.0 - ADAM_B2) * (gv * gv)
        m_hat = nm / (1.0 - ADAM_B1 ** ADAM_STEP)
        v_hat = nv / (1.0 - ADAM_B2 ** ADAM_STEP)
        d_ref[...] = -ADAM_LR * (m_hat / (jnp.sqrt(v_hat) + ADAM_EPS) + ADAM_WD * w_ref[...])
        nm_ref[...] = nm
        nv_ref[...] = nv

    spec = pl.BlockSpec((t, cols), lambda i: (i, 0))
    o = jax.ShapeDtypeStruct((rows, cols), F32)
    return _pcall(body, name=name, grid=(rows // t,), in_specs=[spec] * 4, out_specs=(spec,) * 3,
                  out_shape=(o, o, o), compiler_params=_params(("parallel",)))(w, g, m, v)


def _mesh_pos():
    return lax.axis_index("x"), lax.axis_index("y"), lax.axis_index("c")


class _Gather:
    def __init__(self, arrs):
        self.ins = list(arrs)
        na = self.na = len(arrs)
        self.out_shape = tuple(jax.ShapeDtypeStruct((N_DEV,) + a.shape, a.dtype) for a in arrs)
        self.sems = [pltpu.SemaphoreType.DMA((7 * na,)), pltpu.SemaphoreType.DMA((7 * na,)),
                     pltpu.SemaphoreType.DMA((na,))]

    def _copies(self, ins, outs, sems):
        send_sems, recv_sems, local_sems = sems
        x, y, c = _mesh_pos()
        me, sibling = (x, y, c), (x, y, 1 - c)
        chips = [(1 - x, y), (x, 1 - y), (1 - x, 1 - y)]

        def slot(p):
            return 4 * p[0] + 2 * p[1] + p[2]

        def copy(a, k, block, to, src=None):
            dst = outs[a].at[slot(block)]
            return pltpu.make_async_remote_copy(
                src_ref=dst if src is None else src, dst_ref=dst, send_sem=send_sems.at[7 * a + k],
                recv_sem=recv_sems.at[7 * a + k], device_id=to, device_id_type=MESH)

        mine = [pltpu.make_async_copy(ins[a], outs[a].at[slot(me)], local_sems.at[a]) for a in range(self.na)]
        first = []
        for a in range(self.na):
            first.append(copy(a, 0, me, sibling, src=ins[a]))
            first += [copy(a, 1 + j, me, (*chip, c), src=ins[a]) for j, chip in enumerate(chips)]
        return me, sibling, chips, c, copy, mine, first

    def start(self, ins, outs, sems):
        *_, mine, first = self._copies(ins, outs, sems)
        for cp in mine + first:
            cp.start()

    def finish(self, ins, outs, sems):
        me, sibling, chips, c, copy, mine, first = self._copies(ins, outs, sems)
        passed = []
        for j, chip in enumerate(chips):
            for a in range(self.na):
                copy(a, 1 + j, (*chip, c), me).wait_recv()
                cp = copy(a, 4 + j, (*chip, c), sibling)
                cp.start()
                passed.append(cp)
        for a in range(self.na):
            copy(a, 0, sibling, me).wait_recv()
            for j, chip in enumerate(chips):
                copy(a, 4 + j, (*chip, 1 - c), me).wait_recv()
        for cp in first + passed:
            cp.wait_send()
        for cp in mine:
            cp.wait()


class _ExchangeCore:
    def __init__(self, fulls):
        self.ins = list(fulls)
        self.out_shape = tuple(jax.ShapeDtypeStruct((4,) + f.shape[2:], f.dtype) for f in fulls)
        self.sems = [pltpu.SemaphoreType.DMA((4 * len(fulls),)), pltpu.SemaphoreType.DMA((4 * len(fulls),))]

    def _copies(self, ins, outs, sems):
        send_sems, recv_sems = sems
        x, y, c = _mesh_pos()
        return [pltpu.make_async_remote_copy(
            src_ref=ins[a].at[q, 1 - c], dst_ref=outs[a].at[q], send_sem=send_sems.at[4 * a + q],
            recv_sem=recv_sems.at[4 * a + q], device_id=(x, y, 1 - c), device_id_type=MESH)
            for a in range(len(self.ins)) for q in range(4)]

    def start(self, ins, outs, sems):
        for cp in self._copies(ins, outs, sems):
            cp.start()

    def finish(self, ins, outs, sems):
        for cp in self._copies(ins, outs, sems):
            cp.wait()


class _ExchangeChip:
    def __init__(self, parts):
        self.ins = list(parts)
        self.out_shape = tuple(jax.ShapeDtypeStruct((3,) + p.shape[1:], p.dtype) for p in parts)
        self.sems = [pltpu.SemaphoreType.DMA((3 * len(parts),)), pltpu.SemaphoreType.DMA((3 * len(parts),))]

    def _copies(self, ins, outs, sems):
        send_sems, recv_sems = sems
        x, y, c = _mesh_pos()
        chips = [(1 - x, y), (x, 1 - y), (1 - x, 1 - y)]
        return [pltpu.make_async_remote_copy(
            src_ref=ins[a].at[2 * px + py], dst_ref=outs[a].at[j], send_sem=send_sems.at[3 * a + j],
            recv_sem=recv_sems.at[3 * a + j], device_id=(px, py, c), device_id_type=MESH)
            for a in range(len(self.ins)) for j, (px, py) in enumerate(chips)]

    def start(self, ins, outs, sems):
        for cp in self._copies(ins, outs, sems):
            cp.start()

    def finish(self, ins, outs, sems):
        for cp in self._copies(ins, outs, sems):
            cp.wait()


def _dilated_view(t, dil):
    return t if dil == 1 else t.reshape(S // dil, dil * t.shape[1])


def _reduce_sums(fulls, recv_core, core, tag):
    return [_pair_sum(f, r, core, f"rs_pair_{tag}{i}") for i, (f, r) in enumerate(zip(fulls, recv_core))]


def _local_step(x, tgt, mods, w_in, shards, small, chip, core):
    sh1, sc1, g1, sh2, sc2, g2 = mods
    norm1_g, rel_bias, gn_g, gn_b, norm2_g, norm_f_g = small
    tables = _ret_tables()
    buckets = jnp.asarray(_bucket_tables())

    h1 = _norm_mod_fwd(x, norm1_g, sh1, sc1, "norm1_fwd")
    proj, gathered = _mm(h1, w_in, 'nn', tm=512, tn=512, tk=D, name="proj", comm=_Gather(shards))
    w_ret_out, w_att_out, w_o, w_ff1, w_ff2 = (_from_slots(g, ax) for g, ax in zip(gathered, BIG_AXES[1:]))
    gated, ro, states = _ret_fwd(proj, tables, gn_g, gn_b)
    bias = _bias_build(rel_bias, buckets)
    views, outs, lses = [], [], []
    for gi, (_, dil) in enumerate(ATT_GROUPS):
        a = _dilated_view(proj[:, OFF_ATT + 1536 * gi: OFF_ATT + 1536 * (gi + 1)], dil)
        o, l = _att_fwd(a, bias, gi)
        views.append(a)
        outs.append(o)
        lses.append(l)
    o_flat = [o.reshape(S, AW) for o in outs]
    l_flat = [l.reshape(S, AW) for l in lses]
    att = _mix_fwd(o_flat, l_flat)
    ret_out = _mm(gated, w_ret_out, 'nn', tm=512, tn=512, tk=2048, name="ret_out")
    att_out = _mm(att, w_att_out, 'nn', tm=512, tn=512, tk=AW, name="att_out")
    merged = _merge_fwd(proj, ret_out, att_out)
    mixo, x1 = _mm(merged, w_o, 'nn', tm=512, tn=512, tk=D, name="w_o", res=x, gvec=g1)
    h2 = _norm_mod_fwd(x1, norm2_g, sh2, sc2, "norm2_fwd")
    u = _mm(h2, w_ff1, 'nn', tm=512, tn=512, tk=D, name="ff1")
    act = _relu2_fwd(u)
    f, x2 = _mm(act, w_ff2, 'nn', tm=512, tn=512, tk=2048, name="ff2", res=x1, gvec=g2)
    loss, dx2, g_normf = _final_loss(x2, tgt, norm_f_g)

    df, dg2 = _gate_bwd(dx2, f, g2, "gate2_bwd")
    gw_ff2 = _mm(act, df, 'tn', tm=512, tn=512, tk=S, name="gw_ff2", out_dtype=BF16)
    d_act = _mm(df, w_ff2, 'nt', tm=512, tn=512, tk=D, name="d_act")
    du = _relu2_bwd(d_act, u)
    gw_ff1 = _mm(h2, du, 'tn', tm=512, tn=512, tk=S, name="gw_ff1", out_dtype=BF16)
    dh2 = _mm(du, w_ff1, 'nt', tm=512, tn=512, tk=2048, name="dh2")
    dx1, dsc2, dsh2, g_norm2 = _norm_mod_bwd(x1, norm2_g, sc2, dh2, dx2, "norm2_bwd")

    dmixo, dg1 = _gate_bwd(dx1, mixo, g1, "gate1_bwd")
    gw_o = _mm(merged, dmixo, 'tn', tm=512, tn=512, tk=S, name="gw_o", out_dtype=BF16)
    dmerged = _mm(dmixo, w_o, 'nt', tm=512, tn=512, tk=D, name="dmerged")
    d_ret_out, d_att_out, dga, dgb = _merge_bwd(proj, ret_out, att_out, dmerged)
    gw_ret_out = _mm(gated, d_ret_out, 'tn', tm=512, tn=512, tk=S, name="gw_ret_out", out_dtype=BF16)
    gw_att_out = _mm(att, d_att_out, 'tn', tm=AW, tn=512, tk=S, name="gw_att_out", out_dtype=BF16)
    fulls = [_to_slots(g, ax) for g, ax in zip((gw_ret_out, gw_att_out, gw_o, gw_ff1, gw_ff2), BIG_AXES[1:])]
    dgated, recv_core = _mm(d_ret_out, w_ret_out, 'nt', tm=512, tn=512, tk=D, name="dgated",
                            comm=_ExchangeCore(fulls))
    parts = _reduce_sums(fulls, recv_core, core, "a")
    datt = _mm(d_att_out, w_att_out, 'nt', tm=512, tn=AW, tk=D, name="datt")
    mix_grads = _mix_bwd(o_flat, l_flat, datt)
    datt_parts, ds_sums = [], []
    for gi, (_, dil) in enumerate(ATT_GROUPS):
        do = _dilated_view(mix_grads[gi], dil)
        dl = _dilated_view(mix_grads[3 + gi], dil)
        res = _att_bwd(views[gi], bias, outs[gi], lses[gi], do, dl, gi, comm=_ExchangeChip(parts) if gi == 2 else None)
        if gi == 2:
            res, recv_chip = res
        dq, dk, dv, ds_sum = res
        datt_parts += [dq.reshape(S, AW), dk.reshape(S, AW), dv.reshape(S, AW)]
        ds_sums.append(ds_sum)
    red = [_chip_sum(p, r, chip, f"rs_sum_a{i}") for i, (p, r) in enumerate(zip(parts, recv_chip))]
    g_bias = _bias_grad(jnp.concatenate(ds_sums, axis=0), buckets)[:, :, 0].T
    dq_r, dk_r, dv_r, drg, g_gn_g, g_gn_b = _ret_bwd(proj, tables, gn_g, gn_b, ro, states, dgated)
    dproj = jnp.concatenate([dq_r, dk_r, dv_r, drg] + datt_parts + [dga, dgb], axis=1)
    gw_in = _mm(h1, dproj, 'tn', tm=512, tn=512, tk=S, name="gw_in", out_dtype=BF16)
    full_in = [_to_slots(gw_in, BIG_AXES[0])]
    dh1, recv_core_in = _mm(dproj, w_in, 'nt', tm=512, tn=512, tk=2560, name="dh1", comm=_ExchangeCore(full_in))
    part_in = _reduce_sums(full_in, recv_core_in, core, "b")
    gx, dsc1, dsh1, g_norm1 = _norm_mod_bwd(x, norm1_g, sc1, dh1, dx1, "norm1_bwd")
    recv_chip_in = _run_comm(_ExchangeChip(part_in), "rs_chip_in")
    red_in = _chip_sum(part_in[0], recv_chip_in[0], chip, "rs_sum_b")

    dmod = jnp.concatenate([dsh1, dsc1, dg1, dsh2, dsc2, dg2], axis=1)
    small_g = (g_norm1, g_bias, g_gn_g, g_gn_b, g_norm2, g_normf)
    return loss, gx, [red_in] + red, small_g, dmod


def _to_slots(g, axis):
    if axis == 0:
        return g.reshape(4, 2, g.shape[0] // N_DEV, g.shape[1])
    return g.reshape(g.shape[0], N_DEV, g.shape[1] // N_DEV).transpose(1, 0, 2).reshape(4, 2, g.shape[0], -1)


def _from_slots(w8, axis):
    if axis == 0:
        return w8.reshape(-1, w8.shape[2])
    return w8.transpose(1, 0, 2).reshape(w8.shape[1], -1)


BIG_AXES = (1, 0, 1, 0, 1, 0)


def kernel(x, c, w_ada, b_ada, norm1_g, w_in, rel_bias, ret_gn_g, ret_gn_b, w_ret_out, w_att_out, w_o, norm2_g, w_ff1, w_ff2, norm_f_g, loss_target, m_w_ada, m_b_ada, m_norm1_g, m_w_in, m_rel_bias, m_ret_gn_g, m_ret_gn_b, m_w_ret_out, m_w_att_out, m_w_o, m_norm2_g, m_w_ff1, m_w_ff2, m_norm_f_g, v_w_ada, v_b_ada, v_norm1_g, v_w_in, v_rel_bias, v_ret_gn_g, v_ret_gn_b, v_w_ret_out, v_w_att_out, v_w_o, v_norm2_g, v_w_ff1, v_w_ff2, v_norm_f_g):
    mx, my, mc = _mesh_pos()
    dev = 4 * mx + 2 * my + mc
    chip = jnp.reshape(2 * mx + my, (1,)).astype(jnp.int32)
    core = jnp.reshape(mc, (1,)).astype(jnp.int32)
    ada_w = D * 6 // N_DEV

    shards = [w[0].astype(BF16) for w in (w_in, w_ret_out, w_att_out, w_o, w_ff1, w_ff2)]
    c_all, w_in8 = _run_comm(_Gather([c, shards[0]]), "gather_c_w_in")
    c_all = c_all.reshape(N_DEV, D)
    b_sl = lax.dynamic_slice(b_ada, (0, dev * ada_w), (1, ada_w))
    (mod_all,) = _run_comm(_Gather([_ada_fwd(c_all, w_ada[0], b_sl)]), "gather_mod")
    mod = lax.dynamic_index_in_dim(mod_all, dev, axis=1, keepdims=False).reshape(6, D)
    mods = tuple(mod[i:i + 1] for i in range(6))

    small = (norm1_g, rel_bias, ret_gn_g, ret_gn_b, norm2_g, norm_f_g.reshape(1, D))
    loss, gx, big_red, small_g, dmod = _local_step(x[0], loss_target[0], mods, _from_slots(w_in8, BIG_AXES[0]),
                                                   shards[1:], small, chip, core)

    g_norm1, g_bias, g_gn_g, g_gn_b, g_norm2, g_normf = small_g
    pack = jnp.concatenate([dmod, g_norm1, g_bias.reshape(1, -1), g_gn_g, g_gn_b, g_norm2, g_normf, loss], axis=1)
    (pack_all,) = _run_comm(_Gather([pack]), "gather_small")
    tot = _sum_slots(pack_all, "sum_small")
    offs = np.cumsum([0, 6 * D, D, N_BUCKETS * 12, 2048, 2048, D, D, 128])
    seg = [tot[:, offs[i]:offs[i + 1]] for i in range(8)]
    g_b_ada, g_norm1, g_bias, g_gn_g, g_gn_b, g_norm2, g_normf = seg[:7]
    loss_out = seg[7][0, 0]
    dmod_all = pack_all[:, 0, :6 * D]
    g_w_ada = _ada_bwd(c_all, lax.dynamic_slice(dmod_all, (0, dev * ada_w), (N_DEV, ada_w)))

    names = ['w_ada', 'b_ada', 'norm1_g', 'w_in', 'rel_bias', 'ret_gn_g', 'ret_gn_b', 'w_ret_out', 'w_att_out',
             'w_o', 'norm2_g', 'w_ff1', 'w_ff2', 'norm_f_g']
    ws = dict(zip(names, (w_ada, b_ada, norm1_g, w_in, rel_bias, ret_gn_g, ret_gn_b, w_ret_out, w_att_out, w_o,
                          norm2_g, w_ff1, w_ff2, norm_f_g)))
    ms = dict(zip(names, (m_w_ada, m_b_ada, m_norm1_g, m_w_in, m_rel_bias, m_ret_gn_g, m_ret_gn_b, m_w_ret_out,
                          m_w_att_out, m_w_o, m_norm2_g, m_w_ff1, m_w_ff2, m_norm_f_g)))
    vs = dict(zip(names, (v_w_ada, v_b_ada, v_norm1_g, v_w_in, v_rel_bias, v_ret_gn_g, v_ret_gn_b, v_w_ret_out,
                          v_w_att_out, v_w_o, v_norm2_g, v_w_ff1, v_w_ff2, v_norm_f_g)))
    grads = dict(w_ada=g_w_ada, w_in=big_red[0], w_ret_out=big_red[1], w_att_out=big_red[2], w_o=big_red[3],
                 w_ff1=big_red[4], w_ff2=big_red[5], b_ada=g_b_ada, norm1_g=g_norm1, rel_bias=g_bias,
                 ret_gn_g=g_gn_g, ret_gn_b=g_gn_b, norm2_g=g_norm2, norm_f_g=g_normf)
    delta, new_m, new_v = {}, {}, {}
    for n in ('w_ada', 'w_in', 'w_ret_out', 'w_att_out', 'w_o', 'w_ff1', 'w_ff2'):
        shp = ws[n].shape
        d_, m_, v_ = _adamw(ws[n][0], grads[n], ms[n][0], vs[n][0], "adamw_" + n)
        delta[n], new_m[n], new_v[n] = d_.reshape(shp), m_.reshape(shp), v_.reshape(shp)
        grads[n] = grads[n].reshape(shp)
    small_names = ('b_ada', 'norm1_g', 'rel_bias', 'ret_gn_g', 'ret_gn_b', 'norm2_g', 'norm_f_g')
    flat = lambda d: jnp.concatenate([d[n].reshape(1, -1) for n in small_names], axis=1)
    d_, m_, v_ = _adamw(flat(ws), flat(grads), flat(ms), flat(vs), "adamw_small")
    o = 0
    for n in small_names:
        shp = ws[n].shape
        sz = int(np.prod(shp))
        delta[n], new_m[n], new_v[n] = (t[:, o:o + sz].reshape(shp) for t in (d_, m_, v_))
        grads[n] = grads[n].reshape(shp)
        o += sz
    return (loss_out, gx[None], *[grads[n] for n in names], *[delta[n] for n in names],
            *[new_m[n] for n in names], *[new_v[n] for n in names])
```

```python
import functools
import math

import numpy as np
import jax
import jax.numpy as jnp
from jax import lax
from jax.experimental import pallas as pl
from jax.experimental.pallas import tpu as pltpu

F32 = jnp.float32
BF16 = jnp.bfloat16
MESH = pl.DeviceIdType.MESH

N_DEV = 8
S = 2048
D = 1024
RET_HEADS = 4
RET_DK = 256
RET_DV = 512
CHUNK = 128
N_CHUNK = S // CHUNK
ATT_GROUPS = ((128, 1), (512, 4), (2048, 16))
ATT_HG = 4
ATT_DH = 128
ATT_BLK = 128
N_BUCKETS = 32
MAX_DIST = 2048
D_FF = 4096
IN_COLS = 12800
OFF_RQ, OFF_RK, OFF_RV, OFF_RG, OFF_ATT, OFF_GA, OFF_GB = 0, 1024, 2048, 4096, 6144, 10752, 11776
RMS_EPS = 1e-6
GN_EPS = 1e-5
ADAM_LR, ADAM_B1, ADAM_B2, ADAM_EPS, ADAM_WD, ADAM_STEP = 0.001, 0.9, 0.999, 1e-08, 0.01, 10
VMEM_LIMIT = 48 * 1024 * 1024


def _pcall(body, **kw):
    return pl.pallas_call(body, **kw)


def _params(sem=None):
    return pltpu.CompilerParams(dimension_semantics=sem, vmem_limit_bytes=VMEM_LIMIT)


HBM_SPEC = pl.BlockSpec(memory_space=pl.ANY)


def _carry(body, comm, *, name, grid, in_specs, out_specs, out_shape, scratch_shapes=()):
    single = not isinstance(out_specs, (tuple, list))
    o_specs = (out_specs,) if single else tuple(out_specs)
    o_shape = (out_shape,) if single else tuple(out_shape)
    n_in, n_out, n_scr = len(in_specs), len(o_specs), len(scratch_shapes)
    nci, nco = len(comm.ins), len(comm.out_shape)
    total = int(np.prod(grid))

    def wrapped(*refs):
        bounds = np.cumsum([0, n_in, nci, n_out, nco, n_scr])
        a, ci, o, co, scr = (refs[bounds[i]:bounds[i + 1]] for i in range(5))
        sems = refs[bounds[5]:]
        flat = 0
        for d, g in enumerate(grid):
            flat = flat * g + pl.program_id(d)

        @pl.when(flat == 0)
        def _():
            comm.start(ci, co, sems)

        body(*a, *o, *scr)

        @pl.when(flat == total - 1)
        def _():
            comm.finish(ci, co, sems)

    call = _pcall(wrapped, name=name, grid=grid, in_specs=list(in_specs) + [HBM_SPEC] * nci,
                  out_specs=o_specs + (HBM_SPEC,) * nco, out_shape=o_shape + tuple(comm.out_shape),
                  scratch_shapes=list(scratch_shapes) + list(comm.sems),
                  compiler_params=_params(("arbitrary",) * len(grid)))

    def run(*args):
        res = call(*args, *comm.ins)
        own = res[0] if single else tuple(res[:n_out])
        return own, tuple(res[n_out:])

    return run


def _run_comm(comm, name):
    nci, nco = len(comm.ins), len(comm.out_shape)

    def body(*refs):
        ci, co, sems = refs[:nci], refs[nci:nci + nco], refs[nci + nco:]
        comm.start(ci, co, sems)
        comm.finish(ci, co, sems)

    return _pcall(body, name=name, in_specs=[HBM_SPEC] * nci, out_specs=(HBM_SPEC,) * nco,
                  out_shape=tuple(comm.out_shape), scratch_shapes=list(comm.sems))(*comm.ins)


def _dot(a, b, dn):
    return lax.dot_general(a.astype(BF16), b.astype(BF16), (dn, ((), ())), preferred_element_type=F32)


NN = ((1,), (0,))
NT = ((1,), (1,))
TN = ((0,), (0,))


def _mm(a, b, mode, *, tm, tn, tk, name, out_dtype=F32, res=None, gvec=None, comm=None):
    if mode == 'nn':
        (M, K), (_, N) = a.shape, b.shape
        a_spec = pl.BlockSpec((tm, tk), lambda i, j, k: (i, k))
        b_spec = pl.BlockSpec((tk, tn), lambda i, j, k: (k, j))
        dn = NN
    elif mode == 'nt':
        (M, K), (N, _) = a.shape, b.shape
        a_spec = pl.BlockSpec((tm, tk), lambda i, j, k: (i, k))
        b_spec = pl.BlockSpec((tn, tk), lambda i, j, k: (j, k))
        dn = NT
    else:
        (K, M), (_, N) = a.shape, b.shape
        a_spec = pl.BlockSpec((tk, tm), lambda i, j, k: (k, i))
        b_spec = pl.BlockSpec((tk, tn), lambda i, j, k: (k, j))
        dn = TN
    assert M % tm == 0 and N % tn == 0 and K % tk == 0, (name, M, N, K)
    nk = K // tk
    fused = res is not None
    o_spec = pl.BlockSpec((tm, tn), lambda i, j, k: (i, j))

    def body(a_ref, b_ref, *rest):
        acc_ref = rest[-1] if nk > 1 else None
        if fused:
            res_ref, g_ref, o_ref, x_ref = rest[:4]
        else:
            o_ref = rest[0]

        def finish(acc):
            o_ref[...] = acc.astype(o_ref.dtype)
            if fused:
                x_ref[...] = res_ref[...] + g_ref[...] * acc

        p = _dot(a_ref[...], b_ref[...], dn)
        if nk == 1:
            finish(p)
        else:
            k = pl.program_id(2)

            @pl.when(k == 0)
            def _():
                acc_ref[...] = p

            @pl.when(k > 0)
            def _():
                acc_ref[...] += p

            @pl.when(k == nk - 1)
            def _():
                finish(acc_ref[...])

    in_specs = [a_spec, b_spec]
    args = [a, b]
    out_shape = jax.ShapeDtypeStruct((M, N), out_dtype)
    out_specs = o_spec
    if fused:
        in_specs += [pl.BlockSpec((tm, tn), lambda i, j, k: (i, j)), pl.BlockSpec((1, tn), lambda i, j, k: (0, j))]
        args += [res, gvec]
        out_shape = (out_shape, jax.ShapeDtypeStruct((M, N), F32))
        out_specs = (o_spec, pl.BlockSpec((tm, tn), lambda i, j, k: (i, j)))
    kw = dict(name=name, grid=(M // tm, N // tn, nk), in_specs=in_specs, out_specs=out_specs,
              out_shape=out_shape, scratch_shapes=[pltpu.VMEM((tm, tn), F32)] if nk > 1 else [])
    if comm is not None:
        return _carry(body, comm, **kw)(*args)
    return _pcall(body, compiler_params=_params(("parallel", "parallel", "arbitrary")), **kw)(*args)


TR = 256


def _row_spec(w=D):
    return pl.BlockSpec((TR, w), lambda i: (i, 0))


def _vec_spec(w=D):
    return pl.BlockSpec((1, w), lambda i: (0, 0))


def _norm_mod_fwd(x, g, sh, sc, name):
    def body(x_ref, g_ref, sh_ref, sc_ref, o_ref):
        xv = x_ref[...]
        rstd = lax.rsqrt(jnp.mean(xv * xv, axis=-1, keepdims=True) + RMS_EPS)
        n = xv * rstd * g_ref[...]
        o_ref[...] = (n * (1.0 + sc_ref[...]) + sh_ref[...]).astype(BF16)

    return _pcall(body, name=name, grid=(S // TR,), in_specs=[_row_spec(), _vec_spec(), _vec_spec(), _vec_spec()],
                  out_specs=_row_spec(), out_shape=jax.ShapeDtypeStruct((S, D), BF16),
                  compiler_params=_params(("parallel",)))(x, g, sh, sc)


def _norm_mod_bwd(x, g, sc, dh, dres, name):
    def body(x_ref, g_ref, sc_ref, dh_ref, dres_ref, dx_ref, dsc_ref, dsh_ref, dg_ref):
        i = pl.program_id(0)
        xv = x_ref[...]
        dh = dh_ref[...]
        rstd = lax.rsqrt(jnp.mean(xv * xv, axis=-1, keepdims=True) + RMS_EPS)
        xhat = xv * rstd
        gv = g_ref[...]
        dn = dh * (1.0 + sc_ref[...])
        dxhat = dn * gv
        dx_ref[...] = dres_ref[...] + rstd * (dxhat - xhat * jnp.mean(dxhat * xhat, axis=-1, keepdims=True))
        p_sc = jnp.sum(dh * (xhat * gv), axis=0, keepdims=True)
        p_sh = jnp.sum(dh, axis=0, keepdims=True)
        p_g = jnp.sum(dn * xhat, axis=0, keepdims=True)

        @pl.when(i == 0)
        def _():
            dsc_ref[...] = p_sc
            dsh_ref[...] = p_sh
            dg_ref[...] = p_g

        @pl.when(i > 0)
        def _():
            dsc_ref[...] += p_sc
            dsh_ref[...] += p_sh
            dg_ref[...] += p_g

    vec = jax.ShapeDtypeStruct((1, D), F32)
    return _pcall(body, name=name, grid=(S // TR,),
                  in_specs=[_row_spec(), _vec_spec(), _vec_spec(), _row_spec(), _row_spec()],
                  out_specs=(_row_spec(), _vec_spec(), _vec_spec(), _vec_spec()),
                  out_shape=(jax.ShapeDtypeStruct((S, D), F32), vec, vec, vec),
                  compiler_params=_params(("arbitrary",)))(x, g, sc, dh, dres)


def _gate_bwd(dxr, fval, gvec, name):
    def body(dx_ref, f_ref, g_ref, dz_ref, dg_ref):
        i = pl.program_id(0)
        dx = dx_ref[...]
        dz_ref[...] = (dx * g_ref[...]).astype(BF16)
        p = jnp.sum(dx * f_ref[...], axis=0, keepdims=True)

        @pl.when(i == 0)
        def _():
            dg_ref[...] = p

        @pl.when(i > 0)
        def _():
            dg_ref[...] += p

    return _pcall(body, name=name, grid=(S // TR,), in_specs=[_row_spec(), _row_spec(), _vec_spec()],
                  out_specs=(_row_spec(), _vec_spec()),
                  out_shape=(jax.ShapeDtypeStruct((S, D), BF16), jax.ShapeDtypeStruct((1, D), F32)),
                  compiler_params=_params(("arbitrary",)))(dxr, fval, gvec)


def _relu2_fwd(u):
    def body(u_ref, a_ref):
        r = jnp.maximum(u_ref[...], 0.0)
        a_ref[...] = (r * r).astype(BF16)

    return _pcall(body, name="relu2_fwd", grid=(S // TR,), in_specs=[_row_spec(D_FF)], out_specs=_row_spec(D_FF),
                  out_shape=jax.ShapeDtypeStruct((S, D_FF), BF16), compiler_params=_params(("parallel",)))(u)


def _relu2_bwd(da, u):
    def body(da_ref, u_ref, du_ref):
        du_ref[...] = (da_ref[...] * (2.0 * jnp.maximum(u_ref[...], 0.0))).astype(BF16)

    return _pcall(body, name="relu2_bwd", grid=(S // TR,), in_specs=[_row_spec(D_FF), _row_spec(D_FF)],
                  out_specs=_row_spec(D_FF), out_shape=jax.ShapeDtypeStruct((S, D_FF), BF16),
                  compiler_params=_params(("parallel",)))(da, u)


def _final_loss(x2, tgt, g):
    def body(x_ref, t_ref, g_ref, loss_ref, dx_ref, dg_ref):
        i = pl.program_id(0)
        xv = x_ref[...]
        gv = g_ref[...]
        rstd = lax.rsqrt(jnp.mean(xv * xv, axis=-1, keepdims=True) + RMS_EPS)
        xhat = xv * rstd
        err = xhat * gv - t_ref[...]
        dy = err * (1.0 / D)
        dxhat = dy * gv
        dx_ref[...] = rstd * (dxhat - xhat * jnp.mean(dxhat * xhat, axis=-1, keepdims=True))
        p_g = jnp.sum(dy * xhat, axis=0, keepdims=True)
        p_l = jnp.zeros((1, 128), F32) + 0.5 * jnp.sum(jnp.mean(err * err, axis=-1, keepdims=True))

        @pl.when(i == 0)
        def _():
            dg_ref[...] = p_g
            loss_ref[...] = p_l

        @pl.when(i > 0)
        def _():
            dg_ref[...] += p_g
            loss_ref[...] += p_l

    return _pcall(body, name="final_loss", grid=(S // TR,), in_specs=[_row_spec(), _row_spec(), _vec_spec()],
                  out_specs=(_vec_spec(128), _row_spec(), _vec_spec()),
                  out_shape=(jax.ShapeDtypeStruct((1, 128), F32), jax.ShapeDtypeStruct((S, D), F32),
                             jax.ShapeDtypeStruct((1, D), F32)),
                  compiler_params=_params(("arbitrary",)))(x2, tgt, g)


HALF = 512


def _merge_fwd(proj, ret_out, att_out):
    def body(ga_ref, gb_ref, r_ref, a_ref, o_ref):
        o_ref[...] = (jax.nn.sigmoid(ga_ref[...]) * r_ref[...] + jax.nn.sigmoid(gb_ref[...]) * a_ref[...]).astype(BF16)

    blk = lambda off: pl.BlockSpec((TR, HALF), lambda i, j: (i, off // HALF + j))
    return _pcall(body, name="merge_fwd", grid=(S // TR, D // HALF),
                  in_specs=[blk(OFF_GA), blk(OFF_GB), blk(0), blk(0)], out_specs=blk(0),
                  out_shape=jax.ShapeDtypeStruct((S, D), BF16),
                  compiler_params=_params(("parallel", "parallel")))(proj, proj, ret_out, att_out)


def _merge_bwd(proj, ret_out, att_out, dmerged):
    def body(ga_ref, gb_ref, r_ref, a_ref, dm_ref, dr_ref, da_ref, dga_ref, dgb_ref):
        sa = jax.nn.sigmoid(ga_ref[...])
        sb = jax.nn.sigmoid(gb_ref[...])
        dm = dm_ref[...]
        dr_ref[...] = (dm * sa).astype(BF16)
        da_ref[...] = (dm * sb).astype(BF16)
        dga_ref[...] = (dm * r_ref[...] * (sa * (1.0 - sa))).astype(BF16)
        dgb_ref[...] = (dm * a_ref[...] * (sb * (1.0 - sb))).astype(BF16)

    blk = lambda off: pl.BlockSpec((TR, HALF), lambda i, j: (i, off // HALF + j))
    o = jax.ShapeDtypeStruct((S, D), BF16)
    return _pcall(body, name="merge_bwd", grid=(S // TR, D // HALF),
                  in_specs=[blk(OFF_GA), blk(OFF_GB), blk(0), blk(0), blk(0)], out_specs=(blk(0),) * 4,
                  out_shape=(o, o, o, o),
                  compiler_params=_params(("parallel", "parallel")))(proj, proj, ret_out, att_out, dmerged)


def _ret_tables():
    H, C = RET_HEADS, CHUNK
    log_g = jnp.log1p(-(2.0 ** (-5.0 - jnp.arange(H, dtype=F32))))
    idx = jnp.arange(C, dtype=F32)
    rel = idx[:, None] - idx[None, :]
    inner = jnp.where(rel >= 0, jnp.exp(log_g[:, None, None] * jnp.maximum(rel, 0.0)), 0.0)
    qd = jnp.exp(log_g[:, None] * (idx + 1.0))[:, :, None]
    kd = jnp.exp(log_g[:, None] * (C - 1.0 - idx))[:, :, None]
    cd = jnp.broadcast_to(jnp.exp(log_g * C)[:, None, None], (H, 1, 128))
    half = RET_DK // 2
    inv = 10000.0 ** (-jnp.arange(half, dtype=F32) / half)
    ang = jnp.arange(S, dtype=F32)[:, None] * inv[None, :]
    return inner, qd, kd, cd, jnp.cos(ang), jnp.sin(ang)


def _rot(x, cos, sin):
    x1, x2 = x[:, :128], x[:, 128:]
    return jnp.concatenate([x1 * cos - x2 * sin, x1 * sin + x2 * cos], axis=1)


def _rot_t(d, cos, sin):
    d1, d2 = d[:, :128], d[:, 128:]
    return jnp.concatenate([d1 * cos + d2 * sin, d2 * cos - d1 * sin], axis=1)


def _ret_in_specs(chunk_of):
    ci = chunk_of
    return [
        pl.BlockSpec((CHUNK, RET_DK), lambda h, t: (ci(t), OFF_RQ // RET_DK + h)),
        pl.BlockSpec((CHUNK, RET_DK), lambda h, t: (ci(t), OFF_RK // RET_DK + h)),
        pl.BlockSpec((CHUNK, RET_DV), lambda h, t: (ci(t), OFF_RV // RET_DV + h)),
        pl.BlockSpec((CHUNK, RET_DV), lambda h, t: (ci(t), OFF_RG // RET_DV + h)),
        pl.BlockSpec((CHUNK, 128), lambda h, t: (ci(t), 0)),
        pl.BlockSpec((CHUNK, 128), lambda h, t: (ci(t), 0)),
        pl.BlockSpec((None, CHUNK, CHUNK), lambda h, t: (h, 0, 0)),
        pl.BlockSpec((None, CHUNK, 1), lambda h, t: (h, 0, 0)),
        pl.BlockSpec((None, CHUNK, 1), lambda h, t: (h, 0, 0)),
        pl.BlockSpec((None, 1, 128), lambda h, t: (h, 0, 0)),
    ]


def _ret_fwd(proj, tables, gn_g, gn_b):
    inner, qd, kd, cd, cos, sin = tables

    def body(q_ref, k_ref, v_ref, rg_ref, cos_ref, sin_ref, in_ref, qd_ref, kd_ref, cd_ref, g_ref, b_ref,
             gated_ref, ro_ref, st_ref, s_scr):
        i = pl.program_id(1)

        @pl.when(i == 0)
        def _():
            s_scr[...] = jnp.zeros_like(s_scr)

        cosv, sinv = cos_ref[...], sin_ref[...]
        q = _rot(q_ref[...], cosv, sinv)
        k = _rot(k_ref[...], cosv, sinv) * (RET_DK ** -0.5)
        v = v_ref[...]
        st = s_scr[...]
        st_ref[...] = st
        s = _dot(q, k, NT) * in_ref[...]
        o = _dot(s, v, NN) + _dot(q, st, NN) * qd_ref[...]
        s_scr[...] = st * cd_ref[:, :1] + _dot(k * kd_ref[...], v, TN)
        ro_ref[...] = o
        mu = jnp.mean(o, axis=-1, keepdims=True)
        oc = o - mu
        var = jnp.mean(oc * oc, axis=-1, keepdims=True)
        rn = oc * lax.rsqrt(var + GN_EPS) * g_ref[...] + b_ref[...]
        rg = rg_ref[...]
        gated_ref[...] = (rg * jax.nn.sigmoid(rg) * rn).astype(BF16)

    vspec = pl.BlockSpec((1, RET_DV), lambda h, t: (0, h))
    ospec = pl.BlockSpec((CHUNK, RET_DV), lambda h, t: (t, h))
    return _pcall(body, name="ret_fwd", grid=(RET_HEADS, N_CHUNK),
                  in_specs=_ret_in_specs(lambda t: t) + [vspec, vspec],
                  out_specs=(ospec, ospec, pl.BlockSpec((None, None, RET_DK, RET_DV), lambda h, t: (h, t, 0, 0))),
                  out_shape=(jax.ShapeDtypeStruct((S, RET_HEADS * RET_DV), BF16),
                             jax.ShapeDtypeStruct((S, RET_HEADS * RET_DV), F32),
                             jax.ShapeDtypeStruct((RET_HEADS, N_CHUNK, RET_DK, RET_DV), F32)),
                  scratch_shapes=[pltpu.VMEM((RET_DK, RET_DV), F32)],
                  compiler_params=_params(("parallel", "arbitrary")))(
        proj, proj, proj, proj, cos, sin, inner, qd, kd, cd, gn_g, gn_b)


def _ret_bwd(proj, tables, gn_g, gn_b, ro, states, dgated):
    inner, qd, kd, cd, cos, sin = tables
    last = N_CHUNK - 1

    def body(q_ref, k_ref, v_ref, rg_ref, cos_ref, sin_ref, in_ref, qd_ref, kd_ref, cd_ref, g_ref, b_ref,
             ro_ref, st_ref, dg_ref, dq_ref, dk_ref, dv_ref, drg_ref, gg_ref, gb_ref, gs_scr):
        t = pl.program_id(1)
        cosv, sinv = cos_ref[...], sin_ref[...]
        q = _rot(q_ref[...], cosv, sinv)
        k = _rot(k_ref[...], cosv, sinv) * (RET_DK ** -0.5)
        v = v_ref[...]
        qdv, kdv, dm = qd_ref[...], kd_ref[...], in_ref[...]
        st = st_ref[...]
        o = ro_ref[...]
        gv = g_ref[...]
        mu = jnp.mean(o, axis=-1, keepdims=True)
        oc = o - mu
        rstd = lax.rsqrt(jnp.mean(oc * oc, axis=-1, keepdims=True) + GN_EPS)
        ohat = oc * rstd
        rn = ohat * gv + b_ref[...]
        rg = rg_ref[...]
        sg = jax.nn.sigmoid(rg)
        dgt = dg_ref[...]
        drn = dgt * (rg * sg)
        drg_ref[...] = (dgt * rn * (sg * (1.0 + rg * (1.0 - sg)))).astype(BF16)
        p_g = jnp.sum(drn * ohat, axis=0, keepdims=True)
        p_b = jnp.sum(drn, axis=0, keepdims=True)

        @pl.when(t == 0)
        def _():
            gs_scr[...] = jnp.zeros_like(gs_scr)
            gg_ref[...] = p_g
            gb_ref[...] = p_b

        @pl.when(t > 0)
        def _():
            gg_ref[...] += p_g
            gb_ref[...] += p_b

        dohat = drn * gv
        do = rstd * (dohat - jnp.mean(dohat, axis=-1, keepdims=True)
                     - ohat * jnp.mean(dohat * ohat, axis=-1, keepdims=True))
        gs = gs_scr[...]
        s = _dot(q, k, NT) * dm
        dsr = _dot(do, v, NT) * dm
        dq = _dot(dsr, k, NN) + _dot(do, st, NT) * qdv
        dk = _dot(dsr, q, TN) + _dot(v, gs, NT) * kdv
        dv = _dot(s, do, TN) + _dot(k * kdv, gs, NN)
        gs_scr[...] = gs * cd_ref[:, :1] + _dot(q * qdv, do, TN)
        dq_ref[...] = _rot_t(dq, cosv, sinv).astype(BF16)
        dk_ref[...] = (_rot_t(dk, cosv, sinv) * (RET_DK ** -0.5)).astype(BF16)
        dv_ref[...] = dv.astype(BF16)

    rev = lambda t: last - t
    vspec = pl.BlockSpec((1, RET_DV), lambda h, t: (0, h))
    vblk = pl.BlockSpec((CHUNK, RET_DV), lambda h, t: (rev(t), h))
    qblk = pl.BlockSpec((CHUNK, RET_DK), lambda h, t: (rev(t), h))
    return _pcall(body, name="ret_bwd", grid=(RET_HEADS, N_CHUNK),
                  in_specs=_ret_in_specs(rev) + [vspec, vspec, vblk,
                                                 pl.BlockSpec((None, None, RET_DK, RET_DV), lambda h, t: (h, rev(t), 0, 0)),
                                                 vblk],
                  out_specs=(qblk, qblk, vblk, vblk, vspec, vspec),
                  out_shape=(jax.ShapeDtypeStruct((S, RET_HEADS * RET_DK), BF16),
                             jax.ShapeDtypeStruct((S, RET_HEADS * RET_DK), BF16),
                             jax.ShapeDtypeStruct((S, RET_HEADS * RET_DV), BF16),
                             jax.ShapeDtypeStruct((S, RET_HEADS * RET_DV), BF16),
                             jax.ShapeDtypeStruct((1, RET_HEADS * RET_DV), F32),
                             jax.ShapeDtypeStruct((1, RET_HEADS * RET_DV), F32)),
                  scratch_shapes=[pltpu.VMEM((RET_DK, RET_DV), F32)],
                  compiler_params=_params(("parallel", "arbitrary")))(
        proj, proj, proj, proj, cos, sin, inner, qd, kd, cd, gn_g, gn_b, ro, states, dgated)


def _bucket_tables():
    qi = np.arange(ATT_BLK)[:, None]
    kj = np.arange(2 * ATT_BLK)[None, :]
    m = ATT_BLK + qi - kj
    out = []
    for win, dil in ATT_GROUPS:
        w = win // dil
        dist = (np.clip(m, 0, w) * dil).astype(np.int32)
        max_exact = N_BUCKETS // 2
        d_f = np.maximum(dist, 1).astype(np.float32)
        large = max_exact + (np.log(d_f / np.float32(max_exact)) / np.float32(math.log(MAX_DIST / max_exact))
                             * np.float32(N_BUCKETS - max_exact)).astype(np.int32)
        large = np.minimum(large, N_BUCKETS - 1)
        out.append(np.where(dist < max_exact, dist, large).astype(np.int32))
    return np.stack(out)


def _bias_build(rel_bias, buckets):
    def body(tab_ref, bk_ref, o_ref):
        hh = pl.program_id(0)
        bk = bk_ref[...]
        acc = jnp.zeros((ATT_BLK, 2 * ATT_BLK), F32)
        for b in range(N_BUCKETS):
            acc = jnp.where(bk == b, tab_ref[b, hh], acc)
        o_ref[...] = acc

    nh = len(ATT_GROUPS) * ATT_HG
    return _pcall(body, name="bias_build", grid=(nh,),
                  in_specs=[pl.BlockSpec(memory_space=pltpu.SMEM),
                            pl.BlockSpec((None, ATT_BLK, 2 * ATT_BLK), lambda hh: (hh // ATT_HG, 0, 0))],
                  out_specs=pl.BlockSpec((None, ATT_BLK, 2 * ATT_BLK), lambda hh: (hh, 0, 0)),
                  out_shape=jax.ShapeDtypeStruct((nh, ATT_BLK, 2 * ATT_BLK), F32),
                  compiler_params=_params(("parallel",)))(rel_bias, buckets)


def _bias_grad(ds_sum, buckets):
    def body(ds_ref, bk_ref, o_ref):
        bk = bk_ref[...]
        ds = ds_ref[...]
        rows = lax.broadcasted_iota(jnp.int32, (N_BUCKETS, 128), 0)
        acc = jnp.zeros((N_BUCKETS, 128), F32)
        for b in range(N_BUCKETS):
            acc = jnp.where(rows == b, jnp.sum(jnp.where(bk == b, ds, 0.0)), acc)
        o_ref[...] = acc

    nh = len(ATT_GROUPS) * ATT_HG
    return _pcall(body, name="bias_grad", grid=(nh,),
                  in_specs=[pl.BlockSpec((None, ATT_BLK, 2 * ATT_BLK), lambda hh: (hh, 0, 0)),
                            pl.BlockSpec((None, ATT_BLK, 2 * ATT_BLK), lambda hh: (hh // ATT_HG, 0, 0))],
                  out_specs=pl.BlockSpec((None, N_BUCKETS, 128), lambda hh: (hh, 0, 0)),
                  out_shape=jax.ShapeDtypeStruct((nh, N_BUCKETS, 128), F32),
                  compiler_params=_params(("parallel",)))(ds_sum, buckets)


def _att_valid(n):
    qi = lax.broadcasted_iota(jnp.int32, (ATT_BLK, 2 * ATT_BLK), 0)
    kj = lax.broadcasted_iota(jnp.int32, (ATT_BLK, 2 * ATT_BLK), 1)
    m = ATT_BLK + qi - kj
    first_key = jnp.where(n > 0, 0, ATT_BLK)
    return (m >= 0) & (m <= ATT_BLK) & (kj >= first_key)


def _att_fwd(a, bias, gi):
    _, dil = ATT_GROUPS[gi]
    L = S // dil
    nb = L // ATT_BLK
    scale = ATT_DH ** -0.5

    def body(q_ref, kc_ref, kp_ref, vc_ref, vp_ref, bias_ref, o_ref, l_ref):
        valid = _att_valid(pl.program_id(1))
        for h in range(ATT_HG):
            sl = slice(h * ATT_DH, (h + 1) * ATT_DH)
            kk = jnp.concatenate([kp_ref[:, sl], kc_ref[:, sl]], axis=0)
            vv = jnp.concatenate([vp_ref[:, sl], vc_ref[:, sl]], axis=0)
            s = _dot(q_ref[:, sl], kk, NT) * scale + bias_ref[h]
            s = jnp.where(valid, s, -1e30)
            mx = jnp.max(s, axis=-1, keepdims=True)
            e = jnp.exp(s - mx)
            den = jnp.sum(e, axis=-1, keepdims=True)
            o_ref[:, sl] = _dot(e / den, vv, NN)
            l_ref[:, sl] = jnp.broadcast_to(mx + jnp.log(den), (ATT_BLK, ATT_DH))

    blk = lambda part, prev: pl.BlockSpec(
        (ATT_BLK, AW), lambda r, n: (jnp.maximum(n - 1, 0) if prev else n, r * 3 + part))
    oblk = pl.BlockSpec((ATT_BLK, AW), lambda r, n: (n, r))
    osh = jax.ShapeDtypeStruct((L, dil * AW), F32)
    return _pcall(body, name=f"att_fwd{gi}", grid=(dil, nb),
                  in_specs=[blk(0, False), blk(1, False), blk(1, True), blk(2, False), blk(2, True),
                            pl.BlockSpec((ATT_HG, ATT_BLK, 2 * ATT_BLK), lambda r, n: (gi, 0, 0))],
                  out_specs=(oblk, oblk), out_shape=(osh, osh),
                  compiler_params=_params(("parallel", "arbitrary")))(a, a, a, a, a, bias)


def _att_bwd(a, bias, o, lse, do, dlse, gi, comm=None):
    _, dil = ATT_GROUPS[gi]
    L = S // dil
    nb = L // ATT_BLK
    scale = ATT_DH ** -0.5

    def body(q_ref, kc_ref, kp_ref, vc_ref, vp_ref, bias_ref, o_ref, l_ref, do_ref, dl_ref,
             dq_ref, dk_ref, dv_ref, ds_ref, ck_scr, cv_scr):
        r = pl.program_id(0)
        n = pl.program_id(1)

        @pl.when((r == 0) & (n == 0))
        def _():
            ds_ref[...] = jnp.zeros_like(ds_ref)

        @pl.when(n < nb)
        def _():
            valid = _att_valid(n)
            for h in range(ATT_HG):
                sl = slice(h * ATT_DH, (h + 1) * ATT_DH)
                q = q_ref[:, sl]
                kk = jnp.concatenate([kp_ref[:, sl], kc_ref[:, sl]], axis=0)
                vv = jnp.concatenate([vp_ref[:, sl], vc_ref[:, sl]], axis=0)
                dov = do_ref[:, sl]
                s = _dot(q, kk, NT) * scale + bias_ref[h]
                p = jnp.where(valid, jnp.exp(s - l_ref[:, h * ATT_DH:h * ATT_DH + 1]), 0.0)
                dp = _dot(dov, vv, NT)
                delta = jnp.sum(dov * o_ref[:, sl], axis=-1, keepdims=True)
                ds = p * (dp - delta + dl_ref[:, h * ATT_DH:h * ATT_DH + 1])
                ds_ref[h] += ds
                dq_ref[:, sl] = (_dot(ds, kk, NN) * scale).astype(BF16)
                dkk = _dot(ds, q, TN) * scale
                dvv = _dot(p, dov, TN)

                @pl.when(n > 0)
                def _():
                    dk_ref[:, sl] = (ck_scr[:, sl] + dkk[:ATT_BLK]).astype(BF16)
                    dv_ref[:, sl] = (cv_scr[:, sl] + dvv[:ATT_BLK]).astype(BF16)

                ck_scr[:, sl] = dkk[ATT_BLK:]
                cv_scr[:, sl] = dvv[ATT_BLK:]

        @pl.when(n == nb)
        def _():
            dk_ref[...] = ck_scr[...].astype(BF16)
            dv_ref[...] = cv_scr[...].astype(BF16)

    cur = lambda n: jnp.minimum(n, nb - 1)
    prv = lambda n: jnp.maximum(jnp.minimum(n, nb - 1) - 1, 0)
    blk = lambda part, prev: pl.BlockSpec((ATT_BLK, AW), lambda r, n: (prv(n) if prev else cur(n), r * 3 + part))
    oblk = pl.BlockSpec((ATT_BLK, AW), lambda r, n: (cur(n), r))
    kvblk = pl.BlockSpec((ATT_BLK, AW), lambda r, n: (jnp.maximum(n - 1, 0), r))
    wide = pl.BlockSpec((ATT_HG, ATT_BLK, 2 * ATT_BLK), lambda r, n: (gi, 0, 0))
    osh = jax.ShapeDtypeStruct((L, dil * AW), BF16)
    kw = dict(name=f"att_bwd{gi}", grid=(dil, nb + 1),
              in_specs=[blk(0, False), blk(1, False), blk(1, True), blk(2, False), blk(2, True), wide,
                        oblk, oblk, oblk, oblk],
              out_specs=(oblk, kvblk, kvblk, pl.BlockSpec((ATT_HG, ATT_BLK, 2 * ATT_BLK), lambda r, n: (0, 0, 0))),
              out_shape=(osh, osh, osh, jax.ShapeDtypeStruct((ATT_HG, ATT_BLK, 2 * ATT_BLK), F32)),
              scratch_shapes=[pltpu.VMEM((ATT_BLK, AW), F32), pltpu.VMEM((ATT_BLK, AW), F32)])
    args = (a, a, a, a, a, bias, o, lse, do, dlse)
    if comm is not None:
        return _carry(body, comm, **kw)(*args)
    return _pcall(body, compiler_params=_params(("arbitrary", "arbitrary")), **kw)(*args)


AW = ATT_HG * ATT_DH


def _mix_weights(l0, l1, l2):
    mx = jnp.maximum(jnp.maximum(l0, l1), l2)
    e0, e1, e2 = jnp.exp(l0 - mx), jnp.exp(l1 - mx), jnp.exp(l2 - mx)
    den = e0 + e1 + e2
    return e0 / den, e1 / den, e2 / den


def _mix_fwd(os_, ls):
    def body(o0, o1, o2, l0, l1, l2, att_ref):
        w0, w1, w2 = _mix_weights(l0[...], l1[...], l2[...])
        att_ref[...] = (w0 * o0[...] + w1 * o1[...] + w2 * o2[...]).astype(BF16)

    return _pcall(body, name="mix_fwd", grid=(S // TR,), in_specs=[_row_spec(AW)] * 6, out_specs=_row_spec(AW),
                  out_shape=jax.ShapeDtypeStruct((S, AW), BF16), compiler_params=_params(("parallel",)))(*os_, *ls)


def _mix_bwd(os_, ls, datt):
    def body(o0, o1, o2, l0, l1, l2, da_ref, d0, d1, d2, e0, e1, e2):
        ws = _mix_weights(l0[...], l1[...], l2[...])
        da = da_ref[...]
        dws = []
        for o_ref, w, d_ref in zip((o0, o1, o2), ws, (d0, d1, d2)):
            d_ref[...] = w * da
            prod = da * o_ref[...]
            parts = [jnp.broadcast_to(jnp.sum(prod[:, h * ATT_DH:(h + 1) * ATT_DH], axis=-1, keepdims=True),
                                      (TR, ATT_DH)) for h in range(ATT_HG)]
            dws.append(jnp.concatenate(parts, axis=1))
        tot = ws[0] * dws[0] + ws[1] * dws[1] + ws[2] * dws[2]
        for w, dw, e_ref in zip(ws, dws, (e0, e1, e2)):
            e_ref[...] = w * (dw - tot)

    o = jax.ShapeDtypeStruct((S, AW), F32)
    return _pcall(body, name="mix_bwd", grid=(S // TR,), in_specs=[_row_spec(AW)] * 7, out_specs=(_row_spec(AW),) * 6,
                  out_shape=(o,) * 6, compiler_params=_params(("parallel",)))(*os_, *ls, datt)


def _ada_fwd(c_all, w_sh, b_sl):
    def body(c_ref, w_ref, b_ref, o_ref):
        cv = c_ref[...]
        o_ref[...] = _dot(cv * jax.nn.sigmoid(cv), w_ref[...], NN) + b_ref[...]

    return _pcall(body, name="ada_fwd", out_shape=jax.ShapeDtypeStruct((N_DEV, w_sh.shape[1]), F32),
                  compiler_params=_params())(c_all, w_sh, b_sl)


def _ada_bwd(c_all, dm_sl):
    def body(c_ref, d_ref, o_ref):
        cv = c_ref[...]
        o_ref[...] = _dot(cv * jax.nn.sigmoid(cv), d_ref[...], TN)

    return _pcall(body, name="ada_bwd", out_shape=jax.ShapeDtypeStruct((D, dm_sl.shape[1]), F32),
                  compiler_params=_params())(c_all, dm_sl)


def _sum_slots(g, name):
    n = g.shape[0]

    def body(g_ref, o_ref):
        acc = g_ref[0]
        for e in range(1, n):
            acc = acc + g_ref[e]
        o_ref[...] = acc

    return _pcall(body, name=name, out_shape=jax.ShapeDtypeStruct(g.shape[1:], F32), compiler_params=_params())(g)


def _row_tile(m, n):
    t = max(8, min(m, (1 << 19) // n // 8 * 8))
    while m % t:
        t -= 8
    return t


def _pair_sum(full, recv, sel, name):
    _, _, m, n = full.shape
    t = _row_tile(m, n)

    def body(sel_ref, a_ref, b_ref, o_ref):
        o_ref[...] = (a_ref[...].astype(F32) + b_ref[...].astype(F32)).astype(o_ref.dtype)

    gs = pltpu.PrefetchScalarGridSpec(
        num_scalar_prefetch=1, grid=(4, m // t),
        in_specs=[pl.BlockSpec((None, None, t, n), lambda q, i, s: (q, s[0], i, 0)),
                  pl.BlockSpec((None, t, n), lambda q, i, s: (q, i, 0))],
        out_specs=pl.BlockSpec((None, t, n), lambda q, i, s: (q, i, 0)))
    return _pcall(body, name=name, grid_spec=gs, out_shape=jax.ShapeDtypeStruct((4, m, n), full.dtype),
                  compiler_params=_params(("parallel", "parallel")))(sel, full, recv)


def _chip_sum(part, recv, sel, name):
    _, m, n = part.shape
    t = _row_tile(m, n)

    def body(sel_ref, a_ref, r_ref, o_ref):
        o_ref[...] = ((a_ref[...].astype(F32) + r_ref[0].astype(F32)) + r_ref[1].astype(F32)) + r_ref[2].astype(F32)

    gs = pltpu.PrefetchScalarGridSpec(
        num_scalar_prefetch=1, grid=(m // t,),
        in_specs=[pl.BlockSpec((None, t, n), lambda i, s: (s[0], i, 0)),
                  pl.BlockSpec((3, t, n), lambda i, s: (0, i, 0))],
        out_specs=pl.BlockSpec((t, n), lambda i, s: (i, 0)))
    return _pcall(body, name=name, grid_spec=gs, out_shape=jax.ShapeDtypeStruct((m, n), F32),
                  compiler_params=_params(("parallel",)))(sel, part, recv)


def _adamw(w, g, m, v, name):
    rows, cols = w.shape
    t = _row_tile(rows, cols) if rows >= 8 else rows

    def body(w_ref, g_ref, m_ref, v_ref, d_ref, nm_ref, nv_ref):
        gv = g_ref[...]
        nm = ADAM_B1 * m_ref[...] + (1.0 - ADAM_B1) * gv
        nv = ADAM_B2 * v_ref[...] + (1.0 - ADAM_B2) * (gv * gv)
        m_hat = nm / (1.0 - ADAM_B1 ** ADAM_STEP)
        v_hat = nv / (1.0 - ADAM_B2 ** ADAM_STEP)
        d_ref[...] = -ADAM_LR * (m_hat / (jnp.sqrt(v_hat) + ADAM_EPS) + ADAM_WD * w_ref[...])
        nm_ref[...] = nm
        nv_ref[...] = nv

    spec = pl.BlockSpec((t, cols), lambda i: (i, 0))
    o = jax.ShapeDtypeStruct((rows, cols), F32)
    return _pcall(body, name=name, grid=(rows // t,), in_specs=[spec] * 4, out_specs=(spec,) * 3,
                  out_shape=(o, o, o), compiler_params=_params(("parallel",)))(w, g, m, v)


def _mesh_pos():
    return lax.axis_index("x"), lax.axis_index("y"), lax.axis_index("c")


class _Gather:
    def __init__(self, arrs):
        self.ins = list(arrs)
        na = self.na = len(arrs)
        self.out_shape = tuple(jax.ShapeDtypeStruct((N_DEV,) + a.shape, a.dtype) for a in arrs)
        self.sems = [pltpu.SemaphoreType.DMA((7 * na,)), pltpu.SemaphoreType.DMA((7 * na,)),
                     pltpu.SemaphoreType.DMA((na,))]

    def _copies(self, ins, outs, sems):
        send_sems, recv_sems, local_sems = sems
        x, y, c = _mesh_pos()
        me, sibling = (x, y, c), (x, y, 1 - c)
        chips = [(1 - x, y), (x, 1 - y), (1 - x, 1 - y)]

        def slot(p):
            return 4 * p[0] + 2 * p[1] + p[2]

        def copy(a, k, block, to, src=None):
            dst = outs[a].at[slot(block)]
            return pltpu.make_async_remote_copy(
                src_ref=dst if src is None else src, dst_ref=dst, send_sem=send_sems.at[7 * a + k],
                recv_sem=recv_sems.at[7 * a + k], device_id=to, device_id_type=MESH)

        mine = [pltpu.make_async_copy(ins[a], outs[a].at[slot(me)], local_sems.at[a]) for a in range(self.na)]
        first = []
        for a in range(self.na):
            first.append(copy(a, 0, me, sibling, src=ins[a]))
            first += [copy(a, 1 + j, me, (*chip, c), src=ins[a]) for j, chip in enumerate(chips)]
        return me, sibling, chips, c, copy, mine, first

    def start(self, ins, outs, sems):
        *_, mine, first = self._copies(ins, outs, sems)
        for cp in mine + first:
            cp.start()

    def finish(self, ins, outs, sems):
        me, sibling, chips, c, copy, mine, first = self._copies(ins, outs, sems)
        passed = []
        for j, chip in enumerate(chips):
            for a in range(self.na):
                copy(a, 1 + j, (*chip, c), me).wait_recv()
                cp = copy(a, 4 + j, (*chip, c), sibling)
                cp.start()
                passed.append(cp)
        for a in range(self.na):
            copy(a, 0, sibling, me).wait_recv()
            for j, chip in enumerate(chips):
                copy(a, 4 + j, (*chip, 1 - c), me).wait_recv()
        for cp in first + passed:
            cp.wait_send()
        for cp in mine:
            cp.wait()


class _ExchangeCore:
    def __init__(self, fulls):
        self.ins = list(fulls)
        self.out_shape = tuple(jax.ShapeDtypeStruct((4,) + f.shape[2:], f.dtype) for f in fulls)
        self.sems = [pltpu.SemaphoreType.DMA((4 * len(fulls),)), pltpu.SemaphoreType.DMA((4 * len(fulls),))]

    def _copies(self, ins, outs, sems):
        send_sems, recv_sems = sems
        x, y, c = _mesh_pos()
        return [pltpu.make_async_remote_copy(
            src_ref=ins[a].at[q, 1 - c], dst_ref=outs[a].at[q], send_sem=send_sems.at[4 * a + q],
            recv_sem=recv_sems.at[4 * a + q], device_id=(x, y, 1 - c), device_id_type=MESH)
            for a in range(len(self.ins)) for q in range(4)]

    def start(self, ins, outs, sems):
        for cp in self._copies(ins, outs, sems):
            cp.start()

    def finish(self, ins, outs, sems):
        for cp in self._copies(ins, outs, sems):
            cp.wait()


class _ExchangeChip:
    def __init__(self, parts):
        self.ins = list(parts)
        self.out_shape = tuple(jax.ShapeDtypeStruct((3,) + p.shape[1:], p.dtype) for p in parts)
        self.sems = [pltpu.SemaphoreType.DMA((3 * len(parts),)), pltpu.SemaphoreType.DMA((3 * len(parts),))]

    def _copies(self, ins, outs, sems):
        send_sems, recv_sems = sems
        x, y, c = _mesh_pos()
        chips = [(1 - x, y), (x, 1 - y), (1 - x, 1 - y)]
        return [pltpu.make_async_remote_copy(
            src_ref=ins[a].at[2 * px + py], dst_ref=outs[a].at[j], send_sem=send_sems.at[3 * a + j],
            recv_sem=recv_sems.at[3 * a + j], device_id=(px, py, c), device_id_type=MESH)
            for a in range(len(self.ins)) for j, (px, py) in enumerate(chips)]

    def start(self, ins, outs, sems):
        for cp in self._copies(ins, outs, sems):
            cp.start()

    def finish(self, ins, outs, sems):
        for cp in self._copies(ins, outs, sems):
            cp.wait()


def _dilated_view(t, dil):
    return t if dil == 1 else t.reshape(S // dil, dil * t.shape[1])


def _reduce_sums(fulls, recv_core, core, tag):
    return [_pair_sum(f, r, core, f"rs_pair_{tag}{i}") for i, (f, r) in enumerate(zip(fulls, recv_core))]


def _local_step(x, tgt, mods, w_in, shards, small, chip, core):
    sh1, sc1, g1, sh2, sc2, g2 = mods
    norm1_g, rel_bias, gn_g, gn_b, norm2_g, norm_f_g = small
    tables = _ret_tables()
    buckets = jnp.asarray(_bucket_tables())

    h1 = _norm_mod_fwd(x, norm1_g, sh1, sc1, "norm1_fwd")
    proj, gathered = _mm(h1, w_in, 'nn', tm=S, tn=512, tk=D, name="proj", comm=_Gather(shards))
    w_ret_out, w_att_out, w_o, w_ff1, w_ff2 = (_from_slots(g, ax) for g, ax in zip(gathered, BIG_AXES[1:]))
    gated, ro, states = _ret_fwd(proj, tables, gn_g, gn_b)
    bias = _bias_build(rel_bias, buckets)
    views, outs, lses = [], [], []
    for gi, (_, dil) in enumerate(ATT_GROUPS):
        a = _dilated_view(proj[:, OFF_ATT + 1536 * gi: OFF_ATT + 1536 * (gi + 1)], dil)
        o, l = _att_fwd(a, bias, gi)
        views.append(a)
        outs.append(o)
        lses.append(l)
    o_flat = [o.reshape(S, AW) for o in outs]
    l_flat = [l.reshape(S, AW) for l in lses]
    att = _mix_fwd(o_flat, l_flat)
    ret_out = _mm(gated, w_ret_out, 'nn', tm=S, tn=256, tk=2048, name="ret_out")
    att_out = _mm(att, w_att_out, 'nn', tm=S, tn=512, tk=AW, name="att_out")
    merged = _merge_fwd(proj, ret_out, att_out)
    mixo, x1 = _mm(merged, w_o, 'nn', tm=S, tn=256, tk=D, name="w_o", res=x, gvec=g1)
    h2 = _norm_mod_fwd(x1, norm2_g, sh2, sc2, "norm2_fwd")
    u = _mm(h2, w_ff1, 'nn', tm=S, tn=512, tk=D, name="ff1")
    act = _relu2_fwd(u)
    f, x2 = _mm(act, w_ff2, 'nn', tm=1024, tn=512, tk=2048, name="ff2", res=x1, gvec=g2)
    loss, dx2, g_normf = _final_loss(x2, tgt, norm_f_g)

    df, dg2 = _gate_bwd(dx2, f, g2, "gate2_bwd")
    gw_ff2 = _mm(act, df, 'tn', tm=512, tn=D, tk=S, name="gw_ff2", out_dtype=BF16)
    d_act = _mm(df, w_ff2, 'nt', tm=S, tn=512, tk=D, name="d_act")
    du = _relu2_bwd(d_act, u)
    gw_ff1 = _mm(h2, du, 'tn', tm=D, tn=512, tk=S, name="gw_ff1", out_dtype=BF16)
    dh2 = _mm(du, w_ff1, 'nt', tm=1024, tn=1024, tk=1024, name="dh2")
    dx1, dsc2, dsh2, g_norm2 = _norm_mod_bwd(x1, norm2_g, sc2, dh2, dx2, "norm2_bwd")

    dmixo, dg1 = _gate_bwd(dx1, mixo, g1, "gate1_bwd")
    gw_o = _mm(merged, dmixo, 'tn', tm=D, tn=512, tk=S, name="gw_o", out_dtype=BF16)
    dmerged = _mm(dmixo, w_o, 'nt', tm=S, tn=512, tk=D, name="dmerged")
    d_ret_out, d_att_out, dga, dgb = _merge_bwd(proj, ret_out, att_out, dmerged)
    gw_ret_out = _mm(gated, d_ret_out, 'tn', tm=512, tn=D, tk=S, name="gw_ret_out", out_dtype=BF16)
    gw_att_out = _mm(att, d_att_out, 'tn', tm=AW, tn=D, tk=S, name="gw_att_out", out_dtype=BF16)
    fulls = [_to_slots(g, ax) for g, ax in zip((gw_ret_out, gw_att_out, gw_o, gw_ff1, gw_ff2), BIG_AXES[1:])]
    dgated, recv_core = _mm(d_ret_out, w_ret_out, 'nt', tm=S, tn=512, tk=D, name="dgated",
                            comm=_ExchangeCore(fulls))
    parts = _reduce_sums(fulls, recv_core, core, "a")
    datt = _mm(d_att_out, w_att_out, 'nt', tm=S, tn=AW, tk=D, name="datt")
    mix_grads = _mix_bwd(o_flat, l_flat, datt)
    datt_parts, ds_sums = [], []
    for gi, (_, dil) in enumerate(ATT_GROUPS):
        do = _dilated_view(mix_grads[gi], dil)
        dl = _dilated_view(mix_grads[3 + gi], dil)
        res = _att_bwd(views[gi], bias, outs[gi], lses[gi], do, dl, gi, comm=_ExchangeChip(parts) if gi == 2 else None)
        if gi == 2:
            res, recv_chip = res
        dq, dk, dv, ds_sum = res
        datt_parts += [dq.reshape(S, AW), dk.reshape(S, AW), dv.reshape(S, AW)]
        ds_sums.append(ds_sum)
    red = [_chip_sum(p, r, chip, f"rs_sum_a{i}") for i, (p, r) in enumerate(zip(parts, recv_chip))]
    g_bias = _bias_grad(jnp.concatenate(ds_sums, axis=0), buckets)[:, :, 0].T
    dq_r, dk_r, dv_r, drg, g_gn_g, g_gn_b = _ret_bwd(proj, tables, gn_g, gn_b, ro, states, dgated)
    dproj = jnp.concatenate([dq_r, dk_r, dv_r, drg] + datt_parts + [dga, dgb], axis=1)
    gw_in = _mm(h1, dproj, 'tn', tm=D, tn=512, tk=S, name="gw_in", out_dtype=BF16)
    full_in = [_to_slots(gw_in, BIG_AXES[0])]
    dh1, recv_core_in = _mm(dproj, w_in, 'nt', tm=1024, tn=1024, tk=512, name="dh1", comm=_ExchangeCore(full_in))
    part_in = _reduce_sums(full_in, recv_core_in, core, "b")
    gx, dsc1, dsh1, g_norm1 = _norm_mod_bwd(x, norm1_g, sc1, dh1, dx1, "norm1_bwd")
    recv_chip_in = _run_comm(_ExchangeChip(part_in), "rs_chip_in")
    red_in = _chip_sum(part_in[0], recv_chip_in[0], chip, "rs_sum_b")

    dmod = jnp.concatenate([dsh1, dsc1, dg1, dsh2, dsc2, dg2], axis=1)
    small_g = (g_norm1, g_bias, g_gn_g, g_gn_b, g_norm2, g_normf)
    return loss, gx, [red_in] + red, small_g, dmod


def _to_slots(g, axis):
    if axis == 0:
        return g.reshape(4, 2, g.shape[0] // N_DEV, g.shape[1])
    return g.reshape(g.shape[0], N_DEV, g.shape[1] // N_DEV).transpose(1, 0, 2).reshape(4, 2, g.shape[0], -1)


def _from_slots(w8, axis):
    if axis == 0:
        return w8.reshape(-1, w8.shape[2])
    return w8.transpose(1, 0, 2).reshape(w8.shape[1], -1)


BIG_AXES = (1, 0, 1, 0, 1, 0)


def kernel(x, c, w_ada, b_ada, norm1_g, w_in, rel_bias, ret_gn_g, ret_gn_b, w_ret_out, w_att_out, w_o, norm2_g, w_ff1, w_ff2, norm_f_g, loss_target, m_w_ada, m_b_ada, m_norm1_g, m_w_in, m_rel_bias, m_ret_gn_g, m_ret_gn_b, m_w_ret_out, m_w_att_out, m_w_o, m_norm2_g, m_w_ff1, m_w_ff2, m_norm_f_g, v_w_ada, v_b_ada, v_norm1_g, v_w_in, v_rel_bias, v_ret_gn_g, v_ret_gn_b, v_w_ret_out, v_w_att_out, v_w_o, v_norm2_g, v_w_ff1, v_w_ff2, v_norm_f_g):
    mx, my, mc = _mesh_pos()
    dev = 4 * mx + 2 * my + mc
    chip = jnp.reshape(2 * mx + my, (1,)).astype(jnp.int32)
    core = jnp.reshape(mc, (1,)).astype(jnp.int32)
    ada_w = D * 6 // N_DEV

    shards = [w[0].astype(BF16) for w in (w_in, w_ret_out, w_att_out, w_o, w_ff1, w_ff2)]
    c_all, w_in8 = _run_comm(_Gather([c, shards[0]]), "gather_c_w_in")
    c_all = c_all.reshape(N_DEV, D)
    b_sl = lax.dynamic_slice(b_ada, (0, dev * ada_w), (1, ada_w))
    (mod_all,) = _run_comm(_Gather([_ada_fwd(c_all, w_ada[0], b_sl)]), "gather_mod")
    mod = lax.dynamic_index_in_dim(mod_all, dev, axis=1, keepdims=False).reshape(6, D)
    mods = tuple(mod[i:i + 1] for i in range(6))

    small = (norm1_g, rel_bias, ret_gn_g, ret_gn_b, norm2_g, norm_f_g.reshape(1, D))
    loss, gx, big_red, small_g, dmod = _local_step(x[0], loss_target[0], mods, _from_slots(w_in8, BIG_AXES[0]),
                                                   shards[1:], small, chip, core)

    g_norm1, g_bias, g_gn_g, g_gn_b, g_norm2, g_normf = small_g
    pack = jnp.concatenate([dmod, g_norm1, g_bias.reshape(1, -1), g_gn_g, g_gn_b, g_norm2, g_normf, loss], axis=1)
    (pack_all,) = _run_comm(_Gather([pack]), "gather_small")
    tot = _sum_slots(pack_all, "sum_small")
    offs = np.cumsum([0, 6 * D, D, N_BUCKETS * 12, 2048, 2048, D, D, 128])
    seg = [tot[:, offs[i]:offs[i + 1]] for i in range(8)]
    g_b_ada, g_norm1, g_bias, g_gn_g, g_gn_b, g_norm2, g_normf = seg[:7]
    loss_out = seg[7][0, 0]
    dmod_all = pack_all[:, 0, :6 * D]
    g_w_ada = _ada_bwd(c_all, lax.dynamic_slice(dmod_all, (0, dev * ada_w), (N_DEV, ada_w)))

    names = ['w_ada', 'b_ada', 'norm1_g', 'w_in', 'rel_bias', 'ret_gn_g', 'ret_gn_b', 'w_ret_out', 'w_att_out',
             'w_o', 'norm2_g', 'w_ff1', 'w_ff2', 'norm_f_g']
    ws = dict(zip(names, (w_ada, b_ada, norm1_g, w_in, rel_bias, ret_gn_g, ret_gn_b, w_ret_out, w_att_out, w_o,
                          norm2_g, w_ff1, w_ff2, norm_f_g)))
    ms = dict(zip(names, (m_w_ada, m_b_ada, m_norm1_g, m_w_in, m_rel_bias, m_ret_gn_g, m_ret_gn_b, m_w_ret_out,
                          m_w_att_out, m_w_o, m_norm2_g, m_w_ff1, m_w_ff2, m_norm_f_g)))
    vs = dict(zip(names, (v_w_ada, v_b_ada, v_norm1_g, v_w_in, v_rel_bias, v_ret_gn_g, v_ret_gn_b, v_w_ret_out,
                          v_w_att_out, v_w_o, v_norm2_g, v_w_ff1, v_w_ff2, v_norm_f_g)))
    grads = dict(w_ada=g_w_ada, w_in=big_red[0], w_ret_out=big_red[1], w_att_out=big_red[2], w_o=big_red[3],
                 w_ff1=big_red[4], w_ff2=big_red[5], b_ada=g_b_ada, norm1_g=g_norm1, rel_bias=g_bias,
                 ret_gn_g=g_gn_g, ret_gn_b=g_gn_b, norm2_g=g_norm2, norm_f_g=g_normf)
    delta, new_m, new_v = {}, {}, {}
    for n in ('w_ada', 'w_in', 'w_ret_out', 'w_att_out', 'w_o', 'w_ff1', 'w_ff2'):
        shp = ws[n].shape
        d_, m_, v_ = _adamw(ws[n][0], grads[n], ms[n][0], vs[n][0], "adamw_" + n)
        delta[n], new_m[n], new_v[n] = d_.reshape(shp), m_.reshape(shp), v_.reshape(shp)
        grads[n] = grads[n].reshape(shp)
    small_names = ('b_ada', 'norm1_g', 'rel_bias', 'ret_gn_g', 'ret_gn_b', 'norm2_g', 'norm_f_g')
    flat = lambda d: jnp.concatenate([d[n].reshape(1, -1) for n in small_names], axis=1)
    d_, m_, v_ = _adamw(flat(ws), flat(grads), flat(ms), flat(vs), "adamw_small")
    o = 0
    for n in small_names:
        shp = ws[n].shape
        sz = int(np.prod(shp))
        delta[n], new_m[n], new_v[n] = (t[:, o:o + sz].reshape(shp) for t in (d_, m_, v_))
        grads[n] = grads[n].reshape(shp)
        o += sz
    return (loss_out, gx[None], *[grads[n] for n in names], *[delta[n] for n in names],
            *[new_m[n] for n in names], *[new_v[n] for n in names])
```

```python
import functools
import math

import numpy as np
import jax
import jax.numpy as jnp
from jax import lax
from jax.experimental import pallas as pl
from jax.experimental.pallas import tpu as pltpu

F32 = jnp.float32
BF16 = jnp.bfloat16
MESH = pl.DeviceIdType.MESH

N_DEV = 8
S = 2048
D = 1024
RET_HEADS = 4
RET_DK = 256
RET_DV = 512
CHUNK = 128
N_CHUNK = S // CHUNK
ATT_GROUPS = ((128, 1), (512, 4), (2048, 16))
ATT_HG = 4
ATT_DH = 128
ATT_BLK = 128
N_BUCKETS = 32
MAX_DIST = 2048
D_FF = 4096
IN_COLS = 12800
OFF_RQ, OFF_RK, OFF_RV, OFF_RG, OFF_ATT, OFF_GA, OFF_GB = 0, 1024, 2048, 4096, 6144, 10752, 11776
RMS_EPS = 1e-6
GN_EPS = 1e-5
ADAM_LR, ADAM_B1, ADAM_B2, ADAM_EPS, ADAM_WD, ADAM_STEP = 0.001, 0.9, 0.999, 1e-08, 0.01, 10
VMEM_LIMIT = 48 * 1024 * 1024


def _pcall(body, **kw):
    return pl.pallas_call(body, **kw)


def _params(sem=None):
    return pltpu.CompilerParams(dimension_semantics=sem, vmem_limit_bytes=VMEM_LIMIT)


HBM_SPEC = pl.BlockSpec(memory_space=pl.ANY)


def _carry(body, comm, *, name, grid, in_specs, out_specs, out_shape, scratch_shapes=()):
    single = not isinstance(out_specs, (tuple, list))
    o_specs = (out_specs,) if single else tuple(out_specs)
    o_shape = (out_shape,) if single else tuple(out_shape)
    n_in, n_out, n_scr = len(in_specs), len(o_specs), len(scratch_shapes)
    nci, nco = len(comm.ins), len(comm.out_shape)
    total = int(np.prod(grid))

    def wrapped(*refs):
        bounds = np.cumsum([0, n_in, nci, n_out, nco, n_scr])
        a, ci, o, co, scr = (refs[bounds[i]:bounds[i + 1]] for i in range(5))
        sems = refs[bounds[5]:]
        flat = 0
        for d, g in enumerate(grid):
            flat = flat * g + pl.program_id(d)

        @pl.when(flat == 0)
        def _():
            comm.start(ci, co, sems)

        body(*a, *o, *scr)

        @pl.when(flat == total - 1)
        def _():
            comm.finish(ci, co, sems)

    call = _pcall(wrapped, name=name, grid=grid, in_specs=list(in_specs) + [HBM_SPEC] * nci,
                  out_specs=o_specs + (HBM_SPEC,) * nco, out_shape=o_shape + tuple(comm.out_shape),
                  scratch_shapes=list(scratch_shapes) + list(comm.sems),
                  compiler_params=_params(("arbitrary",) * len(grid)))

    def run(*args):
        res = call(*args, *comm.ins)
        own = res[0] if single else tuple(res[:n_out])
        return own, tuple(res[n_out:])

    return run


def _run_comm(comm, name):
    nci, nco = len(comm.ins), len(comm.out_shape)

    def body(*refs):
        ci, co, sems = refs[:nci], refs[nci:nci + nco], refs[nci + nco:]
        comm.start(ci, co, sems)
        comm.finish(ci, co, sems)

    return _pcall(body, name=name, in_specs=[HBM_SPEC] * nci, out_specs=(HBM_SPEC,) * nco,
                  out_shape=tuple(comm.out_shape), scratch_shapes=list(comm.sems))(*comm.ins)


def _dot(a, b, dn):
    return lax.dot_general(a.astype(BF16), b.astype(BF16), (dn, ((), ())), preferred_element_type=F32)


NN = ((1,), (0,))
NT = ((1,), (1,))
TN = ((0,), (0,))


def _mm(a, b, mode, *, tm, tn, tk, name, out_dtype=F32, res=None, gvec=None, comm=None):
    if mode == 'nn':
        (M, K), (_, N) = a.shape, b.shape
        a_spec = pl.BlockSpec((tm, tk), lambda i, j, k: (i, k))
        b_spec = pl.BlockSpec((tk, tn), lambda i, j, k: (k, j))
        dn = NN
    elif mode == 'nt':
        (M, K), (N, _) = a.shape, b.shape
        a_spec = pl.BlockSpec((tm, tk), lambda i, j, k: (i, k))
        b_spec = pl.BlockSpec((tn, tk), lambda i, j, k: (j, k))
        dn = NT
    else:
        (K, M), (_, N) = a.shape, b.shape
        a_spec = pl.BlockSpec((tk, tm), lambda i, j, k: (k, i))
        b_spec = pl.BlockSpec((tk, tn), lambda i, j, k: (k, j))
        dn = TN
    assert M % tm == 0 and N % tn == 0 and K % tk == 0, (name, M, N, K)
    nk = K // tk
    fused = res is not None
    o_spec = pl.BlockSpec((tm, tn), lambda i, j, k: (i, j))

    def body(a_ref, b_ref, *rest):
        acc_ref = rest[-1] if nk > 1 else None
        if fused:
            res_ref, g_ref, o_ref, x_ref = rest[:4]
        else:
            o_ref = rest[0]

        def finish(acc):
            o_ref[...] = acc.astype(o_ref.dtype)
            if fused:
                x_ref[...] = res_ref[...] + g_ref[...] * acc

        p = _dot(a_ref[...], b_ref[...], dn)
        if nk == 1:
            finish(p)
        else:
            k = pl.program_id(2)

            @pl.when(k == 0)
            def _():
                acc_ref[...] = p

            @pl.when(k > 0)
            def _():
                acc_ref[...] += p

            @pl.when(k == nk - 1)
            def _():
                finish(acc_ref[...])

    in_specs = [a_spec, b_spec]
    args = [a, b]
    out_shape = jax.ShapeDtypeStruct((M, N), out_dtype)
    out_specs = o_spec
    if fused:
        in_specs += [pl.BlockSpec((tm, tn), lambda i, j, k: (i, j)), pl.BlockSpec((1, tn), lambda i, j, k: (0, j))]
        args += [res, gvec]
        out_shape = (out_shape, jax.ShapeDtypeStruct((M, N), F32))
        out_specs = (o_spec, pl.BlockSpec((tm, tn), lambda i, j, k: (i, j)))
    kw = dict(name=name, grid=(M // tm, N // tn, nk), in_specs=in_specs, out_specs=out_specs,
              out_shape=out_shape, scratch_shapes=[pltpu.VMEM((tm, tn), F32)] if nk > 1 else [])
    if comm is not None:
        return _carry(body, comm, **kw)(*args)
    return _pcall(body, compiler_params=_params(("parallel", "parallel", "arbitrary")), **kw)(*args)


TR = 256


def _row_spec(w=D):
    return pl.BlockSpec((TR, w), lambda i: (i, 0))


def _vec_spec(w=D):
    return pl.BlockSpec((1, w), lambda i: (0, 0))


def _norm_mod_fwd(x, g, sh, sc, name):
    def body(x_ref, g_ref, sh_ref, sc_ref, o_ref):
        xv = x_ref[...]
        rstd = lax.rsqrt(jnp.mean(xv * xv, axis=-1, keepdims=True) + RMS_EPS)
        n = xv * rstd * g_ref[...]
        o_ref[...] = (n * (1.0 + sc_ref[...]) + sh_ref[...]).astype(BF16)

    return _pcall(body, name=name, grid=(S // TR,), in_specs=[_row_spec(), _vec_spec(), _vec_spec(), _vec_spec()],
                  out_specs=_row_spec(), out_shape=jax.ShapeDtypeStruct((S, D), BF16),
                  compiler_params=_params(("parallel",)))(x, g, sh, sc)


def _norm_mod_bwd(x, g, sc, dh, dres, name):
    def body(x_ref, g_ref, sc_ref, dh_ref, dres_ref, dx_ref, dsc_ref, dsh_ref, dg_ref):
        i = pl.program_id(0)
        xv = x_ref[...]
        dh = dh_ref[...]
        rstd = lax.rsqrt(jnp.mean(xv * xv, axis=-1, keepdims=True) + RMS_EPS)
        xhat = xv * rstd
        gv = g_ref[...]
        dn = dh * (1.0 + sc_ref[...])
        dxhat = dn * gv
        dx_ref[...] = dres_ref[...] + rstd * (dxhat - xhat * jnp.mean(dxhat * xhat, axis=-1, keepdims=True))
        p_sc = jnp.sum(dh * (xhat * gv), axis=0, keepdims=True)
        p_sh = jnp.sum(dh, axis=0, keepdims=True)
        p_g = jnp.sum(dn * xhat, axis=0, keepdims=True)

        @pl.when(i == 0)
        def _():
            dsc_ref[...] = p_sc
            dsh_ref[...] = p_sh
            dg_ref[...] = p_g

        @pl.when(i > 0)
        def _():
            dsc_ref[...] += p_sc
            dsh_ref[...] += p_sh
            dg_ref[...] += p_g

    vec = jax.ShapeDtypeStruct((1, D), F32)
    return _pcall(body, name=name, grid=(S // TR,),
                  in_specs=[_row_spec(), _vec_spec(), _vec_spec(), _row_spec(), _row_spec()],
                  out_specs=(_row_spec(), _vec_spec(), _vec_spec(), _vec_spec()),
                  out_shape=(jax.ShapeDtypeStruct((S, D), F32), vec, vec, vec),
                  compiler_params=_params(("arbitrary",)))(x, g, sc, dh, dres)


def _gate_bwd(dxr, fval, gvec, name):
    def body(dx_ref, f_ref, g_ref, dz_ref, dg_ref):
        i = pl.program_id(0)
        dx = dx_ref[...]
        dz_ref[...] = (dx * g_ref[...]).astype(BF16)
        p = jnp.sum(dx * f_ref[...], axis=0, keepdims=True)

        @pl.when(i == 0)
        def _():
            dg_ref[...] = p

        @pl.when(i > 0)
        def _():
            dg_ref[...] += p

    return _pcall(body, name=name, grid=(S // TR,), in_specs=[_row_spec(), _row_spec(), _vec_spec()],
                  out_specs=(_row_spec(), _vec_spec()),
                  out_shape=(jax.ShapeDtypeStruct((S, D), BF16), jax.ShapeDtypeStruct((1, D), F32)),
                  compiler_params=_params(("arbitrary",)))(dxr, fval, gvec)


def _relu2_fwd(u):
    def body(u_ref, a_ref):
        r = jnp.maximum(u_ref[...], 0.0)
        a_ref[...] = (r * r).astype(BF16)

    return _pcall(body, name="relu2_fwd", grid=(S // TR,), in_specs=[_row_spec(D_FF)], out_specs=_row_spec(D_FF),
                  out_shape=jax.ShapeDtypeStruct((S, D_FF), BF16), compiler_params=_params(("parallel",)))(u)


def _relu2_bwd(da, u):
    def body(da_ref, u_ref, du_ref):
        du_ref[...] = (da_ref[...] * (2.0 * jnp.maximum(u_ref[...], 0.0))).astype(BF16)

    return _pcall(body, name="relu2_bwd", grid=(S // TR,), in_specs=[_row_spec(D_FF), _row_spec(D_FF)],
                  out_specs=_row_spec(D_FF), out_shape=jax.ShapeDtypeStruct((S, D_FF), BF16),
                  compiler_params=_params(("parallel",)))(da, u)


def _final_loss(x2, tgt, g):
    def body(x_ref, t_ref, g_ref, loss_ref, dx_ref, dg_ref):
        i = pl.program_id(0)
        xv = x_ref[...]
        gv = g_ref[...]
        rstd = lax.rsqrt(jnp.mean(xv * xv, axis=-1, keepdims=True) + RMS_EPS)
        xhat = xv * rstd
        err = xhat * gv - t_ref[...]
        dy = err * (1.0 / D)
        dxhat = dy * gv
        dx_ref[...] = rstd * (dxhat - xhat * jnp.mean(dxhat * xhat, axis=-1, keepdims=True))
        p_g = jnp.sum(dy * xhat, axis=0, keepdims=True)
        p_l = jnp.zeros((1, 128), F32) + 0.5 * jnp.sum(jnp.mean(err * err, axis=-1, keepdims=True))

        @pl.when(i == 0)
        def _():
            dg_ref[...] = p_g
            loss_ref[...] = p_l

        @pl.when(i > 0)
        def _():
            dg_ref[...] += p_g
            loss_ref[...] += p_l

    return _pcall(body, name="final_loss", grid=(S // TR,), in_specs=[_row_spec(), _row_spec(), _vec_spec()],
                  out_specs=(_vec_spec(128), _row_spec(), _vec_spec()),
                  out_shape=(jax.ShapeDtypeStruct((1, 128), F32), jax.ShapeDtypeStruct((S, D), F32),
                             jax.ShapeDtypeStruct((1, D), F32)),
                  compiler_params=_params(("arbitrary",)))(x2, tgt, g)


HALF = 512


def _merge_fwd(proj, ret_out, att_out):
    def body(ga_ref, gb_ref, r_ref, a_ref, o_ref):
        o_ref[...] = (jax.nn.sigmoid(ga_ref[...]) * r_ref[...] + jax.nn.sigmoid(gb_ref[...]) * a_ref[...]).astype(BF16)

    blk = lambda off: pl.BlockSpec((TR, HALF), lambda i, j: (i, off // HALF + j))
    return _pcall(body, name="merge_fwd", grid=(S // TR, D // HALF),
                  in_specs=[blk(OFF_GA), blk(OFF_GB), blk(0), blk(0)], out_specs=blk(0),
                  out_shape=jax.ShapeDtypeStruct((S, D), BF16),
                  compiler_params=_params(("parallel", "parallel")))(proj, proj, ret_out, att_out)


def _merge_bwd(proj, ret_out, att_out, dmerged):
    def body(ga_ref, gb_ref, r_ref, a_ref, dm_ref, dr_ref, da_ref, dga_ref, dgb_ref):
        sa = jax.nn.sigmoid(ga_ref[...])
        sb = jax.nn.sigmoid(gb_ref[...])
        dm = dm_ref[...]
        dr_ref[...] = (dm * sa).astype(BF16)
        da_ref[...] = (dm * sb).astype(BF16)
        dga_ref[...] = (dm * r_ref[...] * (sa * (1.0 - sa))).astype(BF16)
        dgb_ref[...] = (dm * a_ref[...] * (sb * (1.0 - sb))).astype(BF16)

    blk = lambda off: pl.BlockSpec((TR, HALF), lambda i, j: (i, off // HALF + j))
    o = jax.ShapeDtypeStruct((S, D), BF16)
    return _pcall(body, name="merge_bwd", grid=(S // TR, D // HALF),
                  in_specs=[blk(OFF_GA), blk(OFF_GB), blk(0), blk(0), blk(0)], out_specs=(blk(0),) * 4,
                  out_shape=(o, o, o, o),
                  compiler_params=_params(("parallel", "parallel")))(proj, proj, ret_out, att_out, dmerged)


def _ret_tables():
    H, C = RET_HEADS, CHUNK
    log_g = jnp.log1p(-(2.0 ** (-5.0 - jnp.arange(H, dtype=F32))))
    idx = jnp.arange(C, dtype=F32)
    rel = idx[:, None] - idx[None, :]
    inner = jnp.where(rel >= 0, jnp.exp(log_g[:, None, None] * jnp.maximum(rel, 0.0)), 0.0)
    qd = jnp.exp(log_g[:, None] * (idx + 1.0))[:, :, None]
    kd = jnp.exp(log_g[:, None] * (C - 1.0 - idx))[:, :, None]
    cd = jnp.broadcast_to(jnp.exp(log_g * C)[:, None, None], (H, 1, 128))
    half = RET_DK // 2
    inv = 10000.0 ** (-jnp.arange(half, dtype=F32) / half)
    ang = jnp.arange(S, dtype=F32)[:, None] * inv[None, :]
    return inner, qd, kd, cd, jnp.cos(ang), jnp.sin(ang)


def _rot(x, cos, sin):
    x1, x2 = x[:, :128], x[:, 128:]
    return jnp.concatenate([x1 * cos - x2 * sin, x1 * sin + x2 * cos], axis=1)


def _rot_t(d, cos, sin):
    d1, d2 = d[:, :128], d[:, 128:]
    return jnp.concatenate([d1 * cos + d2 * sin, d2 * cos - d1 * sin], axis=1)


RET_COLS = OFF_ATT
RET_VW = RET_HEADS * RET_DV


def _ret_specs(chunk_of):
    ci = chunk_of
    whole = lambda shape: pl.BlockSpec(shape, lambda t: (0,) * len(shape))
    return [
        pl.BlockSpec((CHUNK, RET_COLS), lambda t: (ci(t), 0)),
        pl.BlockSpec((CHUNK, 128), lambda t: (ci(t), 0)),
        pl.BlockSpec((CHUNK, 128), lambda t: (ci(t), 0)),
        whole((RET_HEADS, CHUNK, CHUNK)), whole((RET_HEADS, CHUNK, 1)), whole((RET_HEADS, CHUNK, 1)),
        whole((RET_HEADS, 1, 128)), whole((1, RET_VW)), whole((1, RET_VW)),
    ]


def _ret_cols(h):
    q = slice(OFF_RQ + h * RET_DK, OFF_RQ + (h + 1) * RET_DK)
    k = slice(OFF_RK + h * RET_DK, OFF_RK + (h + 1) * RET_DK)
    v = slice(OFF_RV + h * RET_DV, OFF_RV + (h + 1) * RET_DV)
    g = slice(OFF_RG + h * RET_DV, OFF_RG + (h + 1) * RET_DV)
    return q, k, v, g, slice(h * RET_DV, (h + 1) * RET_DV)


def _ret_fwd(proj, tables, gn_g, gn_b, comm=None):
    inner, qd, kd, cd, cos, sin = tables

    def body(x_ref, cos_ref, sin_ref, in_ref, qd_ref, kd_ref, cd_ref, g_ref, b_ref,
             gated_ref, ro_ref, st_ref, s_scr):
        i = pl.program_id(0)

        @pl.when(i == 0)
        def _():
            s_scr[...] = jnp.zeros_like(s_scr)

        cosv, sinv = cos_ref[...], sin_ref[...]
        for h in range(RET_HEADS):
            cq, ck, cv, cg, co = _ret_cols(h)
            q = _rot(x_ref[:, cq], cosv, sinv)
            k = _rot(x_ref[:, ck], cosv, sinv) * (RET_DK ** -0.5)
            v = x_ref[:, cv]
            st = s_scr[h]
            st_ref[h] = st
            s = _dot(q, k, NT) * in_ref[h]
            o = _dot(s, v, NN) + _dot(q, st, NN) * qd_ref[h]
            s_scr[h] = st * cd_ref[h, :, :1] + _dot(k * kd_ref[h], v, TN)
            ro_ref[:, co] = o
            mu = jnp.mean(o, axis=-1, keepdims=True)
            oc = o - mu
            var = jnp.mean(oc * oc, axis=-1, keepdims=True)
            rn = oc * lax.rsqrt(var + GN_EPS) * g_ref[:, co] + b_ref[:, co]
            rg = x_ref[:, cg]
            gated_ref[:, co] = (rg * jax.nn.sigmoid(rg) * rn).astype(BF16)

    ospec = pl.BlockSpec((CHUNK, RET_VW), lambda t: (t, 0))
    kw = dict(name="ret_fwd", grid=(N_CHUNK,), in_specs=_ret_specs(lambda t: t),
              out_specs=(ospec, ospec, pl.BlockSpec((RET_HEADS, None, RET_DK, RET_DV), lambda t: (0, t, 0, 0))),
              out_shape=(jax.ShapeDtypeStruct((S, RET_VW), BF16), jax.ShapeDtypeStruct((S, RET_VW), F32),
                         jax.ShapeDtypeStruct((RET_HEADS, N_CHUNK, RET_DK, RET_DV), F32)),
              scratch_shapes=[pltpu.VMEM((RET_HEADS, RET_DK, RET_DV), F32)])
    args = (proj, cos, sin, inner, qd, kd, cd, gn_g, gn_b)
    if comm is not None:
        return _carry(body, comm, **kw)(*args)
    return _pcall(body, compiler_params=_params(("arbitrary",)), **kw)(*args)


def _ret_bwd(proj, tables, gn_g, gn_b, ro, states, dgated, comm=None):
    inner, qd, kd, cd, cos, sin = tables
    last = N_CHUNK - 1

    def body(x_ref, cos_ref, sin_ref, in_ref, qd_ref, kd_ref, cd_ref, g_ref, b_ref, ro_ref, st_ref, dg_ref,
             dx_ref, gg_ref, gb_ref, gs_scr):
        t = pl.program_id(0)

        @pl.when(t == 0)
        def _():
            gs_scr[...] = jnp.zeros_like(gs_scr)
            gg_ref[...] = jnp.zeros_like(gg_ref)
            gb_ref[...] = jnp.zeros_like(gb_ref)

        cosv, sinv = cos_ref[...], sin_ref[...]
        for h in range(RET_HEADS):
            cq, ck, cv, cg, co = _ret_cols(h)
            q = _rot(x_ref[:, cq], cosv, sinv)
            k = _rot(x_ref[:, ck], cosv, sinv) * (RET_DK ** -0.5)
            v = x_ref[:, cv]
            qdv, kdv, dm = qd_ref[h], kd_ref[h], in_ref[h]
            st = st_ref[h]
            o = ro_ref[:, co]
            gv = g_ref[:, co]
            mu = jnp.mean(o, axis=-1, keepdims=True)
            oc = o - mu
            rstd = lax.rsqrt(jnp.mean(oc * oc, axis=-1, keepdims=True) + GN_EPS)
            ohat = oc * rstd
            rn = ohat * gv + b_ref[:, co]
            rg = x_ref[:, cg]
            sg = jax.nn.sigmoid(rg)
            dgt = dg_ref[:, co]
            drn = dgt * (rg * sg)
            dx_ref[:, cg] = (dgt * rn * (sg * (1.0 + rg * (1.0 - sg)))).astype(BF16)
            gg_ref[:, co] += jnp.sum(drn * ohat, axis=0, keepdims=True)
            gb_ref[:, co] += jnp.sum(drn, axis=0, keepdims=True)
            dohat = drn * gv
            do = rstd * (dohat - jnp.mean(dohat, axis=-1, keepdims=True)
                         - ohat * jnp.mean(dohat * ohat, axis=-1, keepdims=True))
            gs = gs_scr[h]
            s = _dot(q, k, NT) * dm
            dsr = _dot(do, v, NT) * dm
            dq = _dot(dsr, k, NN) + _dot(do, st, NT) * qdv
            dk = _dot(dsr, q, TN) + _dot(v, gs, NT) * kdv
            dv = _dot(s, do, TN) + _dot(k * kdv, gs, NN)
            gs_scr[h] = gs * cd_ref[h, :, :1] + _dot(q * qdv, do, TN)
            dx_ref[:, cq] = _rot_t(dq, cosv, sinv).astype(BF16)
            dx_ref[:, ck] = (_rot_t(dk, cosv, sinv) * (RET_DK ** -0.5)).astype(BF16)
            dx_ref[:, cv] = dv.astype(BF16)

    rev = lambda t: last - t
    vblk = pl.BlockSpec((CHUNK, RET_VW), lambda t: (rev(t), 0))
    vspec = pl.BlockSpec((1, RET_VW), lambda t: (0, 0))
    kw = dict(name="ret_bwd", grid=(N_CHUNK,),
              in_specs=_ret_specs(rev) + [vblk, pl.BlockSpec((RET_HEADS, None, RET_DK, RET_DV),
                                                             lambda t: (0, rev(t), 0, 0)), vblk],
              out_specs=(pl.BlockSpec((CHUNK, RET_COLS), lambda t: (rev(t), 0)), vspec, vspec),
              out_shape=(jax.ShapeDtypeStruct((S, RET_COLS), BF16), jax.ShapeDtypeStruct((1, RET_VW), F32),
                         jax.ShapeDtypeStruct((1, RET_VW), F32)),
              scratch_shapes=[pltpu.VMEM((RET_HEADS, RET_DK, RET_DV), F32)])
    args = (proj, cos, sin, inner, qd, kd, cd, gn_g, gn_b, ro, states, dgated)
    if comm is not None:
        return _carry(body, comm, **kw)(*args)
    return _pcall(body, compiler_params=_params(("arbitrary",)), **kw)(*args)


def _bucket_tables():
    qi = np.arange(ATT_BLK)[:, None]
    kj = np.arange(2 * ATT_BLK)[None, :]
    m = ATT_BLK + qi - kj
    out = []
    for win, dil in ATT_GROUPS:
        w = win // dil
        dist = (np.clip(m, 0, w) * dil).astype(np.int32)
        max_exact = N_BUCKETS // 2
        d_f = np.maximum(dist, 1).astype(np.float32)
        large = max_exact + (np.log(d_f / np.float32(max_exact)) / np.float32(math.log(MAX_DIST / max_exact))
                             * np.float32(N_BUCKETS - max_exact)).astype(np.int32)
        large = np.minimum(large, N_BUCKETS - 1)
        out.append(np.where(dist < max_exact, dist, large).astype(np.int32))
    return np.stack(out)


def _bias_build(rel_bias, buckets):
    def body(tab_ref, bk_ref, o_ref):
        hh = pl.program_id(0)
        bk = bk_ref[...]
        acc = jnp.zeros((ATT_BLK, 2 * ATT_BLK), F32)
        for b in range(N_BUCKETS):
            acc = jnp.where(bk == b, tab_ref[b, hh], acc)
        o_ref[...] = acc

    nh = len(ATT_GROUPS) * ATT_HG
    return _pcall(body, name="bias_build", grid=(nh,),
                  in_specs=[pl.BlockSpec(memory_space=pltpu.SMEM),
                            pl.BlockSpec((None, ATT_BLK, 2 * ATT_BLK), lambda hh: (hh // ATT_HG, 0, 0))],
                  out_specs=pl.BlockSpec((None, ATT_BLK, 2 * ATT_BLK), lambda hh: (hh, 0, 0)),
                  out_shape=jax.ShapeDtypeStruct((nh, ATT_BLK, 2 * ATT_BLK), F32),
                  compiler_params=_params(("parallel",)))(rel_bias, buckets)


def _bias_grad(ds_sum, buckets):
    def body(ds_ref, bk_ref, o_ref):
        bk = bk_ref[...]
        ds = ds_ref[...]
        rows = lax.broadcasted_iota(jnp.int32, (N_BUCKETS, 128), 0)
        acc = jnp.zeros((N_BUCKETS, 128), F32)
        for b in range(N_BUCKETS):
            acc = jnp.where(rows == b, jnp.sum(jnp.where(bk == b, ds, 0.0)), acc)
        o_ref[...] = acc

    nh = len(ATT_GROUPS) * ATT_HG
    return _pcall(body, name="bias_grad", grid=(nh,),
                  in_specs=[pl.BlockSpec((None, ATT_BLK, 2 * ATT_BLK), lambda hh: (hh, 0, 0)),
                            pl.BlockSpec((None, ATT_BLK, 2 * ATT_BLK), lambda hh: (hh // ATT_HG, 0, 0))],
                  out_specs=pl.BlockSpec((None, N_BUCKETS, 128), lambda hh: (hh, 0, 0)),
                  out_shape=jax.ShapeDtypeStruct((nh, N_BUCKETS, 128), F32),
                  compiler_params=_params(("parallel",)))(ds_sum, buckets)


def _att_valid(n):
    qi = lax.broadcasted_iota(jnp.int32, (ATT_BLK, 2 * ATT_BLK), 0)
    kj = lax.broadcasted_iota(jnp.int32, (ATT_BLK, 2 * ATT_BLK), 1)
    m = ATT_BLK + qi - kj
    first_key = jnp.where(n > 0, 0, ATT_BLK)
    return (m >= 0) & (m <= ATT_BLK) & (kj >= first_key)


def _att_fwd(a, bias, gi):
    _, dil = ATT_GROUPS[gi]
    L = S // dil
    nb = L // ATT_BLK
    scale = ATT_DH ** -0.5

    def body(q_ref, kc_ref, kp_ref, vc_ref, vp_ref, bias_ref, o_ref, l_ref):
        valid = _att_valid(pl.program_id(1))
        for h in range(ATT_HG):
            sl = slice(h * ATT_DH, (h + 1) * ATT_DH)
            kk = jnp.concatenate([kp_ref[:, sl], kc_ref[:, sl]], axis=0)
            vv = jnp.concatenate([vp_ref[:, sl], vc_ref[:, sl]], axis=0)
            s = _dot(q_ref[:, sl], kk, NT) * scale + bias_ref[h]
            s = jnp.where(valid, s, -1e30)
            mx = jnp.max(s, axis=-1, keepdims=True)
            e = jnp.exp(s - mx)
            den = jnp.sum(e, axis=-1, keepdims=True)
            o_ref[:, sl] = _dot(e / den, vv, NN)
            l_ref[:, sl] = jnp.broadcast_to(mx + jnp.log(den), (ATT_BLK, ATT_DH))

    blk = lambda part, prev: pl.BlockSpec(
        (ATT_BLK, AW), lambda r, n: (jnp.maximum(n - 1, 0) if prev else n, r * 3 + part))
    oblk = pl.BlockSpec((ATT_BLK, AW), lambda r, n: (n, r))
    osh = jax.ShapeDtypeStruct((L, dil * AW), F32)
    return _pcall(body, name=f"att_fwd{gi}", grid=(dil, nb),
                  in_specs=[blk(0, False), blk(1, False), blk(1, True), blk(2, False), blk(2, True),
                            pl.BlockSpec((ATT_HG, ATT_BLK, 2 * ATT_BLK), lambda r, n: (gi, 0, 0))],
                  out_specs=(oblk, oblk), out_shape=(osh, osh),
                  compiler_params=_params(("parallel", "arbitrary")))(a, a, a, a, a, bias)


def _att_bwd(a, bias, o, lse, do, dlse, gi, comm=None):
    _, dil = ATT_GROUPS[gi]
    L = S // dil
    nb = L // ATT_BLK
    scale = ATT_DH ** -0.5

    def body(q_ref, kc_ref, kp_ref, vc_ref, vp_ref, bias_ref, o_ref, l_ref, do_ref, dl_ref,
             dq_ref, dk_ref, dv_ref, ds_ref, ck_scr, cv_scr):
        r = pl.program_id(0)
        n = pl.program_id(1)

        @pl.when((r == 0) & (n == 0))
        def _():
            ds_ref[...] = jnp.zeros_like(ds_ref)

        @pl.when(n < nb)
        def _():
            valid = _att_valid(n)
            for h in range(ATT_HG):
                sl = slice(h * ATT_DH, (h + 1) * ATT_DH)
                q = q_ref[:, sl]
                kk = jnp.concatenate([kp_ref[:, sl], kc_ref[:, sl]], axis=0)
                vv = jnp.concatenate([vp_ref[:, sl], vc_ref[:, sl]], axis=0)
                dov = do_ref[:, sl]
                s = _dot(q, kk, NT) * scale + bias_ref[h]
                p = jnp.where(valid, jnp.exp(s - l_ref[:, h * ATT_DH:h * ATT_DH + 1]), 0.0)
                dp = _dot(dov, vv, NT)
                delta = jnp.sum(dov * o_ref[:, sl], axis=-1, keepdims=True)
                ds = p * (dp - delta + dl_ref[:, h * ATT_DH:h * ATT_DH + 1])
                ds_ref[h] += ds
                dq_ref[:, sl] = (_dot(ds, kk, NN) * scale).astype(BF16)
                dkk = _dot(ds, q, TN) * scale
                dvv = _dot(p, dov, TN)

                @pl.when(n > 0)
                def _():
                    dk_ref[:, sl] = (ck_scr[:, sl] + dkk[:ATT_BLK]).astype(BF16)
                    dv_ref[:, sl] = (cv_scr[:, sl] + dvv[:ATT_BLK]).astype(BF16)

                ck_scr[:, sl] = dkk[ATT_BLK:]
                cv_scr[:, sl] = dvv[ATT_BLK:]

        @pl.when(n == nb)
        def _():
            dk_ref[...] = ck_scr[...].astype(BF16)
            dv_ref[...] = cv_scr[...].astype(BF16)

    cur = lambda n: jnp.minimum(n, nb - 1)
    prv = lambda n: jnp.maximum(jnp.minimum(n, nb - 1) - 1, 0)
    blk = lambda part, prev: pl.BlockSpec((ATT_BLK, AW), lambda r, n: (prv(n) if prev else cur(n), r * 3 + part))
    oblk = pl.BlockSpec((ATT_BLK, AW), lambda r, n: (cur(n), r))
    kvblk = pl.BlockSpec((ATT_BLK, AW), lambda r, n: (jnp.maximum(n - 1, 0), r))
    wide = pl.BlockSpec((ATT_HG, ATT_BLK, 2 * ATT_BLK), lambda r, n: (gi, 0, 0))
    osh = jax.ShapeDtypeStruct((L, dil * AW), BF16)
    kw = dict(name=f"att_bwd{gi}", grid=(dil, nb + 1),
              in_specs=[blk(0, False), blk(1, False), blk(1, True), blk(2, False), blk(2, True), wide,
                        oblk, oblk, oblk, oblk],
              out_specs=(oblk, kvblk, kvblk, pl.BlockSpec((ATT_HG, ATT_BLK, 2 * ATT_BLK), lambda r, n: (0, 0, 0))),
              out_shape=(osh, osh, osh, jax.ShapeDtypeStruct((ATT_HG, ATT_BLK, 2 * ATT_BLK), F32)),
              scratch_shapes=[pltpu.VMEM((ATT_BLK, AW), F32), pltpu.VMEM((ATT_BLK, AW), F32)])
    args = (a, a, a, a, a, bias, o, lse, do, dlse)
    if comm is not None:
        return _carry(body, comm, **kw)(*args)
    return _pcall(body, compiler_params=_params(("arbitrary", "arbitrary")), **kw)(*args)


AW = ATT_HG * ATT_DH


def _mix_weights(l0, l1, l2):
    mx = jnp.maximum(jnp.maximum(l0, l1), l2)
    e0, e1, e2 = jnp.exp(l0 - mx), jnp.exp(l1 - mx), jnp.exp(l2 - mx)
    den = e0 + e1 + e2
    return e0 / den, e1 / den, e2 / den


def _mix_fwd(os_, ls):
    def body(o0, o1, o2, l0, l1, l2, att_ref):
        w0, w1, w2 = _mix_weights(l0[...], l1[...], l2[...])
        att_ref[...] = (w0 * o0[...] + w1 * o1[...] + w2 * o2[...]).astype(BF16)

    return _pcall(body, name="mix_fwd", grid=(S // TR,), in_specs=[_row_spec(AW)] * 6, out_specs=_row_spec(AW),
                  out_shape=jax.ShapeDtypeStruct((S, AW), BF16), compiler_params=_params(("parallel",)))(*os_, *ls)


def _mix_bwd(os_, ls, datt):
    def body(o0, o1, o2, l0, l1, l2, da_ref, d0, d1, d2, e0, e1, e2):
        ws = _mix_weights(l0[...], l1[...], l2[...])
        da = da_ref[...]
        dws = []
        for o_ref, w, d_ref in zip((o0, o1, o2), ws, (d0, d1, d2)):
            d_ref[...] = w * da
            prod = da * o_ref[...]
            parts = [jnp.broadcast_to(jnp.sum(prod[:, h * ATT_DH:(h + 1) * ATT_DH], axis=-1, keepdims=True),
                                      (TR, ATT_DH)) for h in range(ATT_HG)]
            dws.append(jnp.concatenate(parts, axis=1))
        tot = ws[0] * dws[0] + ws[1] * dws[1] + ws[2] * dws[2]
        for w, dw, e_ref in zip(ws, dws, (e0, e1, e2)):
            e_ref[...] = w * (dw - tot)

    o = jax.ShapeDtypeStruct((S, AW), F32)
    return _pcall(body, name="mix_bwd", grid=(S // TR,), in_specs=[_row_spec(AW)] * 7, out_specs=(_row_spec(AW),) * 6,
                  out_shape=(o,) * 6, compiler_params=_params(("parallel",)))(*os_, *ls, datt)


def _ada_fwd(c_all, w_sh, b_sl):
    def body(c_ref, w_ref, b_ref, o_ref):
        cv = c_ref[...]
        o_ref[...] = _dot(cv * jax.nn.sigmoid(cv), w_ref[...], NN) + b_ref[...]

    return _pcall(body, name="ada_fwd", out_shape=jax.ShapeDtypeStruct((N_DEV, w_sh.shape[1]), F32),
                  compiler_params=_params())(c_all, w_sh, b_sl)


def _ada_bwd(c_all, dm_sl):
    def body(c_ref, d_ref, o_ref):
        cv = c_ref[...]
        o_ref[...] = _dot(cv * jax.nn.sigmoid(cv), d_ref[...], TN)

    return _pcall(body, name="ada_bwd", out_shape=jax.ShapeDtypeStruct((D, dm_sl.shape[1]), F32),
                  compiler_params=_params())(c_all, dm_sl)


def _sum_slots(g, name):
    n = g.shape[0]

    def body(g_ref, o_ref):
        acc = g_ref[0]
        for e in range(1, n):
            acc = acc + g_ref[e]
        o_ref[...] = acc

    return _pcall(body, name=name, out_shape=jax.ShapeDtypeStruct(g.shape[1:], F32), compiler_params=_params())(g)


def _row_tile(m, n):
    t = max(8, min(m, (1 << 19) // n // 8 * 8))
    while m % t:
        t -= 8
    return t


def _pair_sum(full, recv, sel, name):
    _, _, m, n = full.shape
    t = _row_tile(m, n)

    def body(sel_ref, a_ref, b_ref, o_ref):
        o_ref[...] = (a_ref[...].astype(F32) + b_ref[...].astype(F32)).astype(o_ref.dtype)

    gs = pltpu.PrefetchScalarGridSpec(
        num_scalar_prefetch=1, grid=(4, m // t),
        in_specs=[pl.BlockSpec((None, None, t, n), lambda q, i, s: (q, s[0], i, 0)),
                  pl.BlockSpec((None, t, n), lambda q, i, s: (q, i, 0))],
        out_specs=pl.BlockSpec((None, t, n), lambda q, i, s: (q, i, 0)))
    return _pcall(body, name=name, grid_spec=gs, out_shape=jax.ShapeDtypeStruct((4, m, n), full.dtype),
                  compiler_params=_params(("parallel", "parallel")))(sel, full, recv)


def _chip_sum(part, recv, sel, name):
    _, m, n = part.shape
    t = _row_tile(m, n)

    def body(sel_ref, a_ref, r_ref, o_ref):
        o_ref[...] = ((a_ref[...].astype(F32) + r_ref[0].astype(F32)) + r_ref[1].astype(F32)) + r_ref[2].astype(F32)

    gs = pltpu.PrefetchScalarGridSpec(
        num_scalar_prefetch=1, grid=(m // t,),
        in_specs=[pl.BlockSpec((None, t, n), lambda i, s: (s[0], i, 0)),
                  pl.BlockSpec((3, t, n), lambda i, s: (0, i, 0))],
        out_specs=pl.BlockSpec((t, n), lambda i, s: (i, 0)))
    return _pcall(body, name=name, grid_spec=gs, out_shape=jax.ShapeDtypeStruct((m, n), F32),
                  compiler_params=_params(("parallel",)))(sel, part, recv)


def _adamw(w, g, m, v, name):
    rows, cols = w.shape
    t = _row_tile(rows, cols) if rows >= 8 else rows

    def body(w_ref, g_ref, m_ref, v_ref, d_ref, nm_ref, nv_ref):
        gv = g_ref[...]
        nm = ADAM_B1 * m_ref[...] + (1.0 - ADAM_B1) * gv
        nv = ADAM_B2 * v_ref[...] + (1.0 - ADAM_B2) * (gv * gv)
        m_hat = nm / (1.0 - ADAM_B1 ** ADAM_STEP)
        v_hat = nv / (1.0 - ADAM_B2 ** ADAM_STEP)
        d_ref[...] = -ADAM_LR * (m_hat / (jnp.sqrt(v_hat) + ADAM_EPS) + ADAM_WD * w_ref[...])
        nm_ref[...] = nm
        nv_ref[...] = nv

    spec = pl.BlockSpec((t, cols), lambda i: (i, 0))
    o = jax.ShapeDtypeStruct((rows, cols), F32)
    return _pcall(body, name=name, grid=(rows // t,), in_specs=[spec] * 4, out_specs=(spec,) * 3,
                  out_shape=(o, o, o), compiler_params=_params(("parallel",)))(w, g, m, v)


def _mesh_pos():
    return lax.axis_index("x"), lax.axis_index("y"), lax.axis_index("c")


class _Gather:
    def __init__(self, arrs):
        self.ins = list(arrs)
        na = self.na = len(arrs)
        self.out_shape = tuple(jax.ShapeDtypeStruct((N_DEV,) + a.shape, a.dtype) for a in arrs)
        self.sems = [pltpu.SemaphoreType.DMA((7 * na,)), pltpu.SemaphoreType.DMA((7 * na,)),
                     pltpu.SemaphoreType.DMA((na,))]

    def _copies(self, ins, outs, sems):
        send_sems, recv_sems, local_sems = sems
        x, y, c = _mesh_pos()
        me, sibling = (x, y, c), (x, y, 1 - c)
        chips = [(1 - x, y), (x, 1 - y), (1 - x, 1 - y)]

        def slot(p):
            return 4 * p[0] + 2 * p[1] + p[2]

        def copy(a, k, block, to, src=None):
            dst = outs[a].at[slot(block)]
            return pltpu.make_async_remote_copy(
                src_ref=dst if src is None else src, dst_ref=dst, send_sem=send_sems.at[7 * a + k],
                recv_sem=recv_sems.at[7 * a + k], device_id=to, device_id_type=MESH)

        mine = [pltpu.make_async_copy(ins[a], outs[a].at[slot(me)], local_sems.at[a]) for a in range(self.na)]
        first = []
        for a in range(self.na):
            first.append(copy(a, 0, me, sibling, src=ins[a]))
            first += [copy(a, 1 + j, me, (*chip, c), src=ins[a]) for j, chip in enumerate(chips)]
        return me, sibling, chips, c, copy, mine, first

    def start(self, ins, outs, sems):
        *_, mine, first = self._copies(ins, outs, sems)
        for cp in mine + first:
            cp.start()

    def finish(self, ins, outs, sems):
        me, sibling, chips, c, copy, mine, first = self._copies(ins, outs, sems)
        passed = []
        for j, chip in enumerate(chips):
            for a in range(self.na):
                copy(a, 1 + j, (*chip, c), me).wait_recv()
                cp = copy(a, 4 + j, (*chip, c), sibling)
                cp.start()
                passed.append(cp)
        for a in range(self.na):
            copy(a, 0, sibling, me).wait_recv()
            for j, chip in enumerate(chips):
                copy(a, 4 + j, (*chip, 1 - c), me).wait_recv()
        for cp in first + passed:
            cp.wait_send()
        for cp in mine:
            cp.wait()


class _ExchangeCore:
    def __init__(self, fulls):
        self.ins = list(fulls)
        self.out_shape = tuple(jax.ShapeDtypeStruct((4,) + f.shape[2:], f.dtype) for f in fulls)
        self.sems = [pltpu.SemaphoreType.DMA((4 * len(fulls),)), pltpu.SemaphoreType.DMA((4 * len(fulls),))]

    def _copies(self, ins, outs, sems):
        send_sems, recv_sems = sems
        x, y, c = _mesh_pos()
        return [pltpu.make_async_remote_copy(
            src_ref=ins[a].at[q, 1 - c], dst_ref=outs[a].at[q], send_sem=send_sems.at[4 * a + q],
            recv_sem=recv_sems.at[4 * a + q], device_id=(x, y, 1 - c), device_id_type=MESH)
            for a in range(len(self.ins)) for q in range(4)]

    def start(self, ins, outs, sems):
        for cp in self._copies(ins, outs, sems):
            cp.start()

    def finish(self, ins, outs, sems):
        for cp in self._copies(ins, outs, sems):
            cp.wait()


class _ExchangeChip:
    def __init__(self, parts):
        self.ins = list(parts)
        self.out_shape = tuple(jax.ShapeDtypeStruct((3,) + p.shape[1:], p.dtype) for p in parts)
        self.sems = [pltpu.SemaphoreType.DMA((3 * len(parts),)), pltpu.SemaphoreType.DMA((3 * len(parts),))]

    def _copies(self, ins, outs, sems):
        send_sems, recv_sems = sems
        x, y, c = _mesh_pos()
        chips = [(1 - x, y), (x, 1 - y), (1 - x, 1 - y)]
        return [pltpu.make_async_remote_copy(
            src_ref=ins[a].at[2 * px + py], dst_ref=outs[a].at[j], send_sem=send_sems.at[3 * a + j],
            recv_sem=recv_sems.at[3 * a + j], device_id=(px, py, c), device_id_type=MESH)
            for a in range(len(self.ins)) for j, (px, py) in enumerate(chips)]

    def start(self, ins, outs, sems):
        for cp in self._copies(ins, outs, sems):
            cp.start()

    def finish(self, ins, outs, sems):
        for cp in self._copies(ins, outs, sems):
            cp.wait()


def _dilated_view(t, dil):
    return t if dil == 1 else t.reshape(S // dil, dil * t.shape[1])


def _reduce_sums(fulls, recv_core, core, tag):
    return [_pair_sum(f, r, core, f"rs_pair_{tag}{i}") for i, (f, r) in enumerate(zip(fulls, recv_core))]


def _local_step(x, tgt, mods, w_in, shards, small, chip, core):
    sh1, sc1, g1, sh2, sc2, g2 = mods
    norm1_g, rel_bias, gn_g, gn_b, norm2_g, norm_f_g = small
    tables = _ret_tables()
    buckets = jnp.asarray(_bucket_tables())

    h1 = _norm_mod_fwd(x, norm1_g, sh1, sc1, "norm1_fwd")
    proj, gathered = _mm(h1, w_in, 'nn', tm=S, tn=512, tk=D, name="proj", comm=_Gather(shards[:3]))
    w_ret_out, w_att_out, w_o = (_from_slots(g, ax) for g, ax in zip(gathered, BIG_AXES[1:4]))
    (gated, ro, states), gathered = _ret_fwd(proj, tables, gn_g, gn_b, comm=_Gather(shards[3:]))
    w_ff1, w_ff2 = (_from_slots(g, ax) for g, ax in zip(gathered, BIG_AXES[4:]))
    bias = _bias_build(rel_bias, buckets)
    views, outs, lses = [], [], []
    for gi, (_, dil) in enumerate(ATT_GROUPS):
        a = _dilated_view(proj[:, OFF_ATT + 1536 * gi: OFF_ATT + 1536 * (gi + 1)], dil)
        o, l = _att_fwd(a, bias, gi)
        views.append(a)
        outs.append(o)
        lses.append(l)
    o_flat = [o.reshape(S, AW) for o in outs]
    l_flat = [l.reshape(S, AW) for l in lses]
    att = _mix_fwd(o_flat, l_flat)
    ret_out = _mm(gated, w_ret_out, 'nn', tm=S, tn=256, tk=2048, name="ret_out")
    att_out = _mm(att, w_att_out, 'nn', tm=S, tn=512, tk=AW, name="att_out")
    merged = _merge_fwd(proj, ret_out, att_out)
    mixo, x1 = _mm(merged, w_o, 'nn', tm=S, tn=256, tk=D, name="w_o", res=x, gvec=g1)
    h2 = _norm_mod_fwd(x1, norm2_g, sh2, sc2, "norm2_fwd")
    u = _mm(h2, w_ff1, 'nn', tm=S, tn=512, tk=D, name="ff1")
    act = _relu2_fwd(u)
    f, x2 = _mm(act, w_ff2, 'nn', tm=1024, tn=512, tk=2048, name="ff2", res=x1, gvec=g2)
    loss, dx2, g_normf = _final_loss(x2, tgt, norm_f_g)

    df, dg2 = _gate_bwd(dx2, f, g2, "gate2_bwd")
    gw_ff2 = _mm(act, df, 'tn', tm=512, tn=D, tk=S, name="gw_ff2", out_dtype=BF16)
    d_act = _mm(df, w_ff2, 'nt', tm=S, tn=512, tk=D, name="d_act")
    du = _relu2_bwd(d_act, u)
    gw_ff1 = _mm(h2, du, 'tn', tm=D, tn=512, tk=S, name="gw_ff1", out_dtype=BF16)
    fulls_a = [_to_slots(g, ax) for g, ax in zip((gw_ff1, gw_ff2), BIG_AXES[4:])]
    dh2, recv_core_a = _mm(du, w_ff1, 'nt', tm=1024, tn=1024, tk=1024, name="dh2", comm=_ExchangeCore(fulls_a))
    parts_a = _reduce_sums(fulls_a, recv_core_a, core, "a")
    dx1, dsc2, dsh2, g_norm2 = _norm_mod_bwd(x1, norm2_g, sc2, dh2, dx2, "norm2_bwd")

    dmixo, dg1 = _gate_bwd(dx1, mixo, g1, "gate1_bwd")
    gw_o = _mm(merged, dmixo, 'tn', tm=D, tn=512, tk=S, name="gw_o", out_dtype=BF16)
    dmerged = _mm(dmixo, w_o, 'nt', tm=S, tn=512, tk=D, name="dmerged")
    d_ret_out, d_att_out, dga, dgb = _merge_bwd(proj, ret_out, att_out, dmerged)
    gw_ret_out = _mm(gated, d_ret_out, 'tn', tm=512, tn=D, tk=S, name="gw_ret_out", out_dtype=BF16)
    gw_att_out = _mm(att, d_att_out, 'tn', tm=AW, tn=D, tk=S, name="gw_att_out", out_dtype=BF16)
    fulls_b = [_to_slots(g, ax) for g, ax in zip((gw_ret_out, gw_att_out, gw_o), BIG_AXES[1:4])]
    dgated, recv_core_b = _mm(d_ret_out, w_ret_out, 'nt', tm=S, tn=512, tk=D, name="dgated",
                              comm=_ExchangeCore(fulls_b))
    parts_b = _reduce_sums(fulls_b, recv_core_b, core, "b")
    datt = _mm(d_att_out, w_att_out, 'nt', tm=S, tn=AW, tk=D, name="datt")
    mix_grads = _mix_bwd(o_flat, l_flat, datt)
    datt_parts, ds_sums = [], []
    for gi, (_, dil) in enumerate(ATT_GROUPS):
        do = _dilated_view(mix_grads[gi], dil)
        dl = _dilated_view(mix_grads[3 + gi], dil)
        res = _att_bwd(views[gi], bias, outs[gi], lses[gi], do, dl, gi,
                       comm=_ExchangeChip(parts_b) if gi == 1 else None)
        if gi == 1:
            res, recv_chip_b = res
        dq, dk, dv, ds_sum = res
        datt_parts += [dq.reshape(S, AW), dk.reshape(S, AW), dv.reshape(S, AW)]
        ds_sums.append(ds_sum)
    red_b = [_chip_sum(p, r, chip, f"rs_sum_b{i}") for i, (p, r) in enumerate(zip(parts_b, recv_chip_b))]
    g_bias = _bias_grad(jnp.concatenate(ds_sums, axis=0), buckets)[:, :, 0].T
    (dret, g_gn_g, g_gn_b), recv_chip_a = _ret_bwd(proj, tables, gn_g, gn_b, ro, states, dgated,
                                                   comm=_ExchangeChip(parts_a))
    red_a = [_chip_sum(p, r, chip, f"rs_sum_a{i}") for i, (p, r) in enumerate(zip(parts_a, recv_chip_a))]
    dproj = jnp.concatenate([dret] + datt_parts + [dga, dgb], axis=1)
    gw_in = _mm(h1, dproj, 'tn', tm=D, tn=512, tk=S, name="gw_in", out_dtype=BF16)
    full_in = [_to_slots(gw_in, BIG_AXES[0])]
    dh1, recv_core_in = _mm(dproj, w_in, 'nt', tm=1024, tn=1024, tk=512, name="dh1", comm=_ExchangeCore(full_in))
    part_in = _reduce_sums(full_in, recv_core_in, core, "c")
    gx, dsc1, dsh1, g_norm1 = _norm_mod_bwd(x, norm1_g, sc1, dh1, dx1, "norm1_bwd")
    recv_chip_in = _run_comm(_ExchangeChip(part_in), "rs_chip_in")
    red_in = _chip_sum(part_in[0], recv_chip_in[0], chip, "rs_sum_c")

    dmod = jnp.concatenate([dsh1, dsc1, dg1, dsh2, dsc2, dg2], axis=1)
    small_g = (g_norm1, g_bias, g_gn_g, g_gn_b, g_norm2, g_normf)
    return loss, gx, [red_in] + red_b + red_a, small_g, dmod


def _to_slots(g, axis):
    if axis == 0:
        return g.reshape(4, 2, g.shape[0] // N_DEV, g.shape[1])
    return g.reshape(g.shape[0], N_DEV, g.shape[1] // N_DEV).transpose(1, 0, 2).reshape(4, 2, g.shape[0], -1)


def _from_slots(w8, axis):
    if axis == 0:
        return w8.reshape(-1, w8.shape[2])
    return w8.transpose(1, 0, 2).reshape(w8.shape[1], -1)


BIG_AXES = (1, 0, 1, 0, 1, 0)


def kernel(x, c, w_ada, b_ada, norm1_g, w_in, rel_bias, ret_gn_g, ret_gn_b, w_ret_out, w_att_out, w_o, norm2_g, w_ff1, w_ff2, norm_f_g, loss_target, m_w_ada, m_b_ada, m_norm1_g, m_w_in, m_rel_bias, m_ret_gn_g, m_ret_gn_b, m_w_ret_out, m_w_att_out, m_w_o, m_norm2_g, m_w_ff1, m_w_ff2, m_norm_f_g, v_w_ada, v_b_ada, v_norm1_g, v_w_in, v_rel_bias, v_ret_gn_g, v_ret_gn_b, v_w_ret_out, v_w_att_out, v_w_o, v_norm2_g, v_w_ff1, v_w_ff2, v_norm_f_g):
    mx, my, mc = _mesh_pos()
    dev = 4 * mx + 2 * my + mc
    chip = jnp.reshape(2 * mx + my, (1,)).astype(jnp.int32)
    core = jnp.reshape(mc, (1,)).astype(jnp.int32)
    ada_w = D * 6 // N_DEV

    shards = [w[0].astype(BF16) for w in (w_in, w_ret_out, w_att_out, w_o, w_ff1, w_ff2)]
    c_all, w_in8 = _run_comm(_Gather([c, shards[0]]), "gather_c_w_in")
    c_all = c_all.reshape(N_DEV, D)
    b_sl = lax.dynamic_slice(b_ada, (0, dev * ada_w), (1, ada_w))
    (mod_all,) = _run_comm(_Gather([_ada_fwd(c_all, w_ada[0], b_sl)]), "gather_mod")
    mod = lax.dynamic_index_in_dim(mod_all, dev, axis=1, keepdims=False).reshape(6, D)
    mods = tuple(mod[i:i + 1] for i in range(6))

    small = (norm1_g, rel_bias, ret_gn_g, ret_gn_b, norm2_g, norm_f_g.reshape(1, D))
    loss, gx, big_red, small_g, dmod = _local_step(x[0], loss_target[0], mods, _from_slots(w_in8, BIG_AXES[0]),
                                                   shards[1:], small, chip, core)

    g_norm1, g_bias, g_gn_g, g_gn_b, g_norm2, g_normf = small_g
    pack = jnp.concatenate([dmod, g_norm1, g_bias.reshape(1, -1), g_gn_g, g_gn_b, g_norm2, g_normf, loss], axis=1)
    (pack_all,) = _run_comm(_Gather([pack]), "gather_small")
    tot = _sum_slots(pack_all, "sum_small")
    offs = np.cumsum([0, 6 * D, D, N_BUCKETS * 12, 2048, 2048, D, D, 128])
    seg = [tot[:, offs[i]:offs[i + 1]] for i in range(8)]
    g_b_ada, g_norm1, g_bias, g_gn_g, g_gn_b, g_norm2, g_normf = seg[:7]
    loss_out = seg[7][0, 0]
    dmod_all = pack_all[:, 0, :6 * D]
    g_w_ada = _ada_bwd(c_all, lax.dynamic_slice(dmod_all, (0, dev * ada_w), (N_DEV, ada_w)))

    names = ['w_ada', 'b_ada', 'norm1_g', 'w_in', 'rel_bias', 'ret_gn_g', 'ret_gn_b', 'w_ret_out', 'w_att_out',
             'w_o', 'norm2_g', 'w_ff1', 'w_ff2', 'norm_f_g']
    ws = dict(zip(names, (w_ada, b_ada, norm1_g, w_in, rel_bias, ret_gn_g, ret_gn_b, w_ret_out, w_att_out, w_o,
                          norm2_g, w_ff1, w_ff2, norm_f_g)))
    ms = dict(zip(names, (m_w_ada, m_b_ada, m_norm1_g, m_w_in, m_rel_bias, m_ret_gn_g, m_ret_gn_b, m_w_ret_out,
                          m_w_att_out, m_w_o, m_norm2_g, m_w_ff1, m_w_ff2, m_norm_f_g)))
    vs = dict(zip(names, (v_w_ada, v_b_ada, v_norm1_g, v_w_in, v_rel_bias, v_ret_gn_g, v_ret_gn_b, v_w_ret_out,
                          v_w_att_out, v_w_o, v_norm2_g, v_w_ff1, v_w_ff2, v_norm_f_g)))
    grads = dict(w_ada=g_w_ada, w_in=big_red[0], w_ret_out=big_red[1], w_att_out=big_red[2], w_o=big_red[3],
                 w_ff1=big_red[4], w_ff2=big_red[5], b_ada=g_b_ada, norm1_g=g_norm1, rel_bias=g_bias,
                 ret_gn_g=g_gn_g, ret_gn_b=g_gn_b, norm2_g=g_norm2, norm_f_g=g_normf)
    delta, new_m, new_v = {}, {}, {}
    for n in ('w_ada', 'w_in', 'w_ret_out', 'w_att_out', 'w_o', 'w_ff1', 'w_ff2'):
        shp = ws[n].shape
        d_, m_, v_ = _adamw(ws[n][0], grads[n], ms[n][0], vs[n][0], "adamw_" + n)
        delta[n], new_m[n], new_v[n] = d_.reshape(shp), m_.reshape(shp), v_.reshape(shp)
        grads[n] = grads[n].reshape(shp)
    small_names = ('b_ada', 'norm1_g', 'rel_bias', 'ret_gn_g', 'ret_gn_b', 'norm2_g', 'norm_f_g')
    flat = lambda d: jnp.concatenate([d[n].reshape(1, -1) for n in small_names], axis=1)
    d_, m_, v_ = _adamw(flat(ws), flat(grads), flat(ms), flat(vs), "adamw_small")
    o = 0
    for n in small_names:
        shp = ws[n].shape
        sz = int(np.prod(shp))
        delta[n], new_m[n], new_v[n] = (t[:, o:o + sz].reshape(shp) for t in (d_, m_, v_))
        grads[n] = grads[n].reshape(shp)
        o += sz
    return (loss_out, gx[None], *[grads[n] for n in names], *[delta[n] for n in names],
            *[new_m[n] for n in names], *[new_v[n] for n in names])
```

```python
import functools
import math

import numpy as np
import jax
import jax.numpy as jnp
from jax import lax
from jax.experimental import pallas as pl
from jax.experimental.pallas import tpu as pltpu

F32 = jnp.float32
BF16 = jnp.bfloat16
MESH = pl.DeviceIdType.MESH

N_DEV = 8
S = 2048
D = 1024
RET_HEADS = 4
RET_DK = 256
RET_DV = 512
CHUNK = 128
N_CHUNK = S // CHUNK
ATT_GROUPS = ((128, 1), (512, 4), (2048, 16))
ATT_HG = 4
ATT_DH = 128
ATT_BLK = 128
N_BUCKETS = 32
MAX_DIST = 2048
D_FF = 4096
IN_COLS = 12800
OFF_RQ, OFF_RK, OFF_RV, OFF_RG, OFF_ATT = 0, 1024, 2048, 4096, 6144
OFF_GA, OFF_GB = 6144, 7168
RMS_EPS = 1e-6
GN_EPS = 1e-5
ADAM_LR, ADAM_B1, ADAM_B2, ADAM_EPS, ADAM_WD, ADAM_STEP = 0.001, 0.9, 0.999, 1e-08, 0.01, 10
VMEM_LIMIT = 48 * 1024 * 1024


def _pcall(body, **kw):
    return pl.pallas_call(body, **kw)


def _params(sem=None):
    return pltpu.CompilerParams(dimension_semantics=sem, vmem_limit_bytes=VMEM_LIMIT)


HBM_SPEC = pl.BlockSpec(memory_space=pl.ANY)


def _carry(body, comm, *, name, grid, in_specs, out_specs, out_shape, scratch_shapes=()):
    single = not isinstance(out_specs, (tuple, list))
    o_specs = (out_specs,) if single else tuple(out_specs)
    o_shape = (out_shape,) if single else tuple(out_shape)
    n_in, n_out, n_scr = len(in_specs), len(o_specs), len(scratch_shapes)
    nci, nco = len(comm.ins), len(comm.out_shape)
    total = int(np.prod(grid))

    def wrapped(*refs):
        bounds = np.cumsum([0, n_in, nci, n_out, nco, n_scr])
        a, ci, o, co, scr = (refs[bounds[i]:bounds[i + 1]] for i in range(5))
        sems = refs[bounds[5]:]
        flat = 0
        for d, g in enumerate(grid):
            flat = flat * g + pl.program_id(d)

        @pl.when(flat == 0)
        def _():
            comm.start(ci, co, sems)

        body(*a, *o, *scr)

        @pl.when(flat == total - 1)
        def _():
            comm.finish(ci, co, sems)

    call = _pcall(wrapped, name=name, grid=grid, in_specs=list(in_specs) + [HBM_SPEC] * nci,
                  out_specs=o_specs + (HBM_SPEC,) * nco, out_shape=o_shape + tuple(comm.out_shape),
                  scratch_shapes=list(scratch_shapes) + list(comm.sems),
                  compiler_params=_params(("arbitrary",) * len(grid)))

    def run(*args):
        res = call(*args, *comm.ins)
        own = res[0] if single else tuple(res[:n_out])
        return own, tuple(res[n_out:])

    return run


def _run_comm(comm, name):
    nci, nco = len(comm.ins), len(comm.out_shape)

    def body(*refs):
        ci, co, sems = refs[:nci], refs[nci:nci + nco], refs[nci + nco:]
        comm.start(ci, co, sems)
        comm.finish(ci, co, sems)

    return _pcall(body, name=name, in_specs=[HBM_SPEC] * nci, out_specs=(HBM_SPEC,) * nco,
                  out_shape=tuple(comm.out_shape), scratch_shapes=list(comm.sems))(*comm.ins)


def _dot(a, b, dn):
    return lax.dot_general(a.astype(BF16), b.astype(BF16), (dn, ((), ())), preferred_element_type=F32)


NN = ((1,), (0,))
NT = ((1,), (1,))
TN = ((0,), (0,))


def _mm(a, b, mode, *, tm, tn, tk, name, out_dtype=F32, res=None, gvec=None, comm=None):
    if mode == 'nn':
        (M, K), (_, N) = a.shape, b.shape
        a_spec = pl.BlockSpec((tm, tk), lambda i, j, k: (i, k))
        b_spec = pl.BlockSpec((tk, tn), lambda i, j, k: (k, j))
        dn = NN
    elif mode == 'nt':
        (M, K), (N, _) = a.shape, b.shape
        a_spec = pl.BlockSpec((tm, tk), lambda i, j, k: (i, k))
        b_spec = pl.BlockSpec((tn, tk), lambda i, j, k: (j, k))
        dn = NT
    else:
        (K, M), (_, N) = a.shape, b.shape
        a_spec = pl.BlockSpec((tk, tm), lambda i, j, k: (k, i))
        b_spec = pl.BlockSpec((tk, tn), lambda i, j, k: (k, j))
        dn = TN
    assert M % tm == 0 and N % tn == 0 and K % tk == 0, (name, M, N, K)
    nk = K // tk
    fused = res is not None
    o_spec = pl.BlockSpec((tm, tn), lambda i, j, k: (i, j))

    def body(a_ref, b_ref, *rest):
        acc_ref = rest[-1] if nk > 1 else None
        if fused:
            res_ref, g_ref, o_ref, x_ref = rest[:4]
        else:
            o_ref = rest[0]

        def finish(acc):
            o_ref[...] = acc.astype(o_ref.dtype)
            if fused:
                x_ref[...] = res_ref[...] + g_ref[...] * acc

        p = _dot(a_ref[...], b_ref[...], dn)
        if nk == 1:
            finish(p)
        else:
            k = pl.program_id(2)

            @pl.when(k == 0)
            def _():
                acc_ref[...] = p

            @pl.when(k > 0)
            def _():
                acc_ref[...] += p

            @pl.when(k == nk - 1)
            def _():
                finish(acc_ref[...])

    in_specs = [a_spec, b_spec]
    args = [a, b]
    out_shape = jax.ShapeDtypeStruct((M, N), out_dtype)
    out_specs = o_spec
    if fused:
        in_specs += [pl.BlockSpec((tm, tn), lambda i, j, k: (i, j)), pl.BlockSpec((1, tn), lambda i, j, k: (0, j))]
        args += [res, gvec]
        out_shape = (out_shape, jax.ShapeDtypeStruct((M, N), F32))
        out_specs = (o_spec, pl.BlockSpec((tm, tn), lambda i, j, k: (i, j)))
    kw = dict(name=name, grid=(M // tm, N // tn, nk), in_specs=in_specs, out_specs=out_specs,
              out_shape=out_shape, scratch_shapes=[pltpu.VMEM((tm, tn), F32)] if nk > 1 else [])
    if comm is not None:
        return _carry(body, comm, **kw)(*args)
    return _pcall(body, compiler_params=_params(("parallel", "parallel", "arbitrary")), **kw)(*args)


PROJ_TN = 512
ATT_T0, ATT_T1 = 6144 // PROJ_TN, 10752 // PROJ_TN
N_SLABS = (ATT_T1 - ATT_T0) * 4
MAIN_COLS = IN_COLS - (ATT_T1 - ATT_T0) * PROJ_TN


def _proj(h1, w_in, comm):
    nj = IN_COLS // PROJ_TN

    def body(a_ref, b_ref, main_ref, slab_ref):
        j = pl.program_id(1)
        p = _dot(a_ref[...], b_ref[...], NN)
        is_att = (j >= ATT_T0) & (j < ATT_T1)

        @pl.when(jnp.logical_not(is_att))
        def _():
            main_ref[...] = p

        @pl.when(is_att)
        def _():
            for h in range(4):
                slab_ref[h] = p[:, h * 128:(h + 1) * 128]

    main_idx = lambda j: jnp.where(j < ATT_T0, j, jnp.where(j < ATT_T1, ATT_T0 - 1, j - (ATT_T1 - ATT_T0)))
    slab_idx = lambda j: jnp.clip(j - ATT_T0, 0, ATT_T1 - ATT_T0 - 1)
    (main, slabs), got = _carry(
        body, comm, name="proj", grid=(1, nj, 1),
        in_specs=[pl.BlockSpec((S, D), lambda i, j, k: (0, 0)), pl.BlockSpec((D, PROJ_TN), lambda i, j, k: (0, j))],
        out_specs=(pl.BlockSpec((S, PROJ_TN), lambda i, j, k: (0, main_idx(j))),
                   pl.BlockSpec((4, S, 128), lambda i, j, k: (slab_idx(j), 0, 0))),
        out_shape=(jax.ShapeDtypeStruct((S, MAIN_COLS), F32), jax.ShapeDtypeStruct((N_SLABS, S, 128), F32)))(h1, w_in)
    return main, slabs, got


TR = 256


def _row_spec(w=D):
    return pl.BlockSpec((TR, w), lambda i: (i, 0))


def _vec_spec(w=D):
    return pl.BlockSpec((1, w), lambda i: (0, 0))


def _norm_mod_fwd(x, g, sh, sc, name):
    def body(x_ref, g_ref, sh_ref, sc_ref, o_ref):
        xv = x_ref[...]
        rstd = lax.rsqrt(jnp.mean(xv * xv, axis=-1, keepdims=True) + RMS_EPS)
        n = xv * rstd * g_ref[...]
        o_ref[...] = (n * (1.0 + sc_ref[...]) + sh_ref[...]).astype(BF16)

    return _pcall(body, name=name, grid=(S // TR,), in_specs=[_row_spec(), _vec_spec(), _vec_spec(), _vec_spec()],
                  out_specs=_row_spec(), out_shape=jax.ShapeDtypeStruct((S, D), BF16),
                  compiler_params=_params(("parallel",)))(x, g, sh, sc)


def _norm_mod_bwd(x, g, sc, dh, dres, name):
    def body(x_ref, g_ref, sc_ref, dh_ref, dres_ref, dx_ref, dsc_ref, dsh_ref, dg_ref):
        i = pl.program_id(0)
        xv = x_ref[...]
        dh = dh_ref[...]
        rstd = lax.rsqrt(jnp.mean(xv * xv, axis=-1, keepdims=True) + RMS_EPS)
        xhat = xv * rstd
        gv = g_ref[...]
        dn = dh * (1.0 + sc_ref[...])
        dxhat = dn * gv
        dx_ref[...] = dres_ref[...] + rstd * (dxhat - xhat * jnp.mean(dxhat * xhat, axis=-1, keepdims=True))
        p_sc = jnp.sum(dh * (xhat * gv), axis=0, keepdims=True)
        p_sh = jnp.sum(dh, axis=0, keepdims=True)
        p_g = jnp.sum(dn * xhat, axis=0, keepdims=True)

        @pl.when(i == 0)
        def _():
            dsc_ref[...] = p_sc
            dsh_ref[...] = p_sh
            dg_ref[...] = p_g

        @pl.when(i > 0)
        def _():
            dsc_ref[...] += p_sc
            dsh_ref[...] += p_sh
            dg_ref[...] += p_g

    vec = jax.ShapeDtypeStruct((1, D), F32)
    return _pcall(body, name=name, grid=(S // TR,),
                  in_specs=[_row_spec(), _vec_spec(), _vec_spec(), _row_spec(), _row_spec()],
                  out_specs=(_row_spec(), _vec_spec(), _vec_spec(), _vec_spec()),
                  out_shape=(jax.ShapeDtypeStruct((S, D), F32), vec, vec, vec),
                  compiler_params=_params(("arbitrary",)))(x, g, sc, dh, dres)


def _gate_bwd(dxr, fval, gvec, name):
    def body(dx_ref, f_ref, g_ref, dz_ref, dg_ref):
        i = pl.program_id(0)
        dx = dx_ref[...]
        dz_ref[...] = (dx * g_ref[...]).astype(BF16)
        p = jnp.sum(dx * f_ref[...], axis=0, keepdims=True)

        @pl.when(i == 0)
        def _():
            dg_ref[...] = p

        @pl.when(i > 0)
        def _():
            dg_ref[...] += p

    return _pcall(body, name=name, grid=(S // TR,), in_specs=[_row_spec(), _row_spec(), _vec_spec()],
                  out_specs=(_row_spec(), _vec_spec()),
                  out_shape=(jax.ShapeDtypeStruct((S, D), BF16), jax.ShapeDtypeStruct((1, D), F32)),
                  compiler_params=_params(("arbitrary",)))(dxr, fval, gvec)


def _relu2_fwd(u):
    def body(u_ref, a_ref):
        r = jnp.maximum(u_ref[...], 0.0)
        a_ref[...] = (r * r).astype(BF16)

    return _pcall(body, name="relu2_fwd", grid=(S // TR,), in_specs=[_row_spec(D_FF)], out_specs=_row_spec(D_FF),
                  out_shape=jax.ShapeDtypeStruct((S, D_FF), BF16), compiler_params=_params(("parallel",)))(u)


def _relu2_bwd(da, u):
    def body(da_ref, u_ref, du_ref):
        du_ref[...] = (da_ref[...] * (2.0 * jnp.maximum(u_ref[...], 0.0))).astype(BF16)

    return _pcall(body, name="relu2_bwd", grid=(S // TR,), in_specs=[_row_spec(D_FF), _row_spec(D_FF)],
                  out_specs=_row_spec(D_FF), out_shape=jax.ShapeDtypeStruct((S, D_FF), BF16),
                  compiler_params=_params(("parallel",)))(da, u)


def _final_loss(x2, tgt, g):
    def body(x_ref, t_ref, g_ref, loss_ref, dx_ref, dg_ref):
        i = pl.program_id(0)
        xv = x_ref[...]
        gv = g_ref[...]
        rstd = lax.rsqrt(jnp.mean(xv * xv, axis=-1, keepdims=True) + RMS_EPS)
        xhat = xv * rstd
        err = xhat * gv - t_ref[...]
        dy = err * (1.0 / D)
        dxhat = dy * gv
        dx_ref[...] = rstd * (dxhat - xhat * jnp.mean(dxhat * xhat, axis=-1, keepdims=True))
        p_g = jnp.sum(dy * xhat, axis=0, keepdims=True)
        p_l = jnp.zeros((1, 128), F32) + 0.5 * jnp.sum(jnp.mean(err * err, axis=-1, keepdims=True))

        @pl.when(i == 0)
        def _():
            dg_ref[...] = p_g
            loss_ref[...] = p_l

        @pl.when(i > 0)
        def _():
            dg_ref[...] += p_g
            loss_ref[...] += p_l

    return _pcall(body, name="final_loss", grid=(S // TR,), in_specs=[_row_spec(), _row_spec(), _vec_spec()],
                  out_specs=(_vec_spec(128), _row_spec(), _vec_spec()),
                  out_shape=(jax.ShapeDtypeStruct((1, 128), F32), jax.ShapeDtypeStruct((S, D), F32),
                             jax.ShapeDtypeStruct((1, D), F32)),
                  compiler_params=_params(("arbitrary",)))(x2, tgt, g)


HALF = 512


def _merge_fwd(proj, ret_out, att_out):
    def body(ga_ref, gb_ref, r_ref, a_ref, o_ref):
        o_ref[...] = (jax.nn.sigmoid(ga_ref[...]) * r_ref[...] + jax.nn.sigmoid(gb_ref[...]) * a_ref[...]).astype(BF16)

    blk = lambda off: pl.BlockSpec((TR, HALF), lambda i, j: (i, off // HALF + j))
    return _pcall(body, name="merge_fwd", grid=(S // TR, D // HALF),
                  in_specs=[blk(OFF_GA), blk(OFF_GB), blk(0), blk(0)], out_specs=blk(0),
                  out_shape=jax.ShapeDtypeStruct((S, D), BF16),
                  compiler_params=_params(("parallel", "parallel")))(proj, proj, ret_out, att_out)


def _merge_bwd(proj, ret_out, att_out, dmerged):
    def body(ga_ref, gb_ref, r_ref, a_ref, dm_ref, dr_ref, da_ref, dga_ref, dgb_ref):
        sa = jax.nn.sigmoid(ga_ref[...])
        sb = jax.nn.sigmoid(gb_ref[...])
        dm = dm_ref[...]
        dr_ref[...] = (dm * sa).astype(BF16)
        da_ref[...] = (dm * sb).astype(BF16)
        dga_ref[...] = (dm * r_ref[...] * (sa * (1.0 - sa))).astype(BF16)
        dgb_ref[...] = (dm * a_ref[...] * (sb * (1.0 - sb))).astype(BF16)

    blk = lambda off: pl.BlockSpec((TR, HALF), lambda i, j: (i, off // HALF + j))
    o = jax.ShapeDtypeStruct((S, D), BF16)
    return _pcall(body, name="merge_bwd", grid=(S // TR, D // HALF),
                  in_specs=[blk(OFF_GA), blk(OFF_GB), blk(0), blk(0), blk(0)], out_specs=(blk(0),) * 4,
                  out_shape=(o, o, o, o),
                  compiler_params=_params(("parallel", "parallel")))(proj, proj, ret_out, att_out, dmerged)


def _ret_tables():
    H, C = RET_HEADS, CHUNK
    log_g = jnp.log1p(-(2.0 ** (-5.0 - jnp.arange(H, dtype=F32))))
    idx = jnp.arange(C, dtype=F32)
    rel = idx[:, None] - idx[None, :]
    inner = jnp.where(rel >= 0, jnp.exp(log_g[:, None, None] * jnp.maximum(rel, 0.0)), 0.0)
    qd = jnp.exp(log_g[:, None] * (idx + 1.0))[:, :, None]
    kd = jnp.exp(log_g[:, None] * (C - 1.0 - idx))[:, :, None]
    cd = jnp.broadcast_to(jnp.exp(log_g * C)[:, None, None], (H, 1, 128))
    half = RET_DK // 2
    inv = 10000.0 ** (-jnp.arange(half, dtype=F32) / half)
    ang = jnp.arange(S, dtype=F32)[:, None] * inv[None, :]
    return inner, qd, kd, cd, jnp.cos(ang), jnp.sin(ang)


def _rot(x, cos, sin):
    x1, x2 = x[:, :128], x[:, 128:]
    return jnp.concatenate([x1 * cos - x2 * sin, x1 * sin + x2 * cos], axis=1)


def _rot_t(d, cos, sin):
    d1, d2 = d[:, :128], d[:, 128:]
    return jnp.concatenate([d1 * cos + d2 * sin, d2 * cos - d1 * sin], axis=1)


RET_COLS = OFF_ATT
RET_VW = RET_HEADS * RET_DV


def _ret_specs(chunk_of):
    ci = chunk_of
    whole = lambda shape: pl.BlockSpec(shape, lambda t: (0,) * len(shape))
    return [
        pl.BlockSpec((CHUNK, RET_COLS), lambda t: (ci(t), 0)),
        pl.BlockSpec((CHUNK, 128), lambda t: (ci(t), 0)),
        pl.BlockSpec((CHUNK, 128), lambda t: (ci(t), 0)),
        whole((RET_HEADS, CHUNK, CHUNK)), whole((RET_HEADS, CHUNK, 1)), whole((RET_HEADS, CHUNK, 1)),
        whole((RET_HEADS, 1, 128)), whole((1, RET_VW)), whole((1, RET_VW)),
    ]


def _ret_cols(h):
    q = slice(OFF_RQ + h * RET_DK, OFF_RQ + (h + 1) * RET_DK)
    k = slice(OFF_RK + h * RET_DK, OFF_RK + (h + 1) * RET_DK)
    v = slice(OFF_RV + h * RET_DV, OFF_RV + (h + 1) * RET_DV)
    g = slice(OFF_RG + h * RET_DV, OFF_RG + (h + 1) * RET_DV)
    return q, k, v, g, slice(h * RET_DV, (h + 1) * RET_DV)


def _ret_fwd(proj, tables, gn_g, gn_b, comm=None):
    inner, qd, kd, cd, cos, sin = tables

    def body(x_ref, cos_ref, sin_ref, in_ref, qd_ref, kd_ref, cd_ref, g_ref, b_ref,
             gated_ref, ro_ref, st_ref, s_scr):
        i = pl.program_id(0)

        @pl.when(i == 0)
        def _():
            s_scr[...] = jnp.zeros_like(s_scr)

        cosv, sinv = cos_ref[...], sin_ref[...]
        for h in range(RET_HEADS):
            cq, ck, cv, cg, co = _ret_cols(h)
            q = _rot(x_ref[:, cq], cosv, sinv)
            k = _rot(x_ref[:, ck], cosv, sinv) * (RET_DK ** -0.5)
            v = x_ref[:, cv]
            st = s_scr[h]
            st_ref[h] = st
            s = _dot(q, k, NT) * in_ref[h]
            o = _dot(s, v, NN) + _dot(q, st, NN) * qd_ref[h]
            s_scr[h] = st * cd_ref[h, :, :1] + _dot(k * kd_ref[h], v, TN)
            ro_ref[:, co] = o
            mu = jnp.mean(o, axis=-1, keepdims=True)
            oc = o - mu
            var = jnp.mean(oc * oc, axis=-1, keepdims=True)
            rn = oc * lax.rsqrt(var + GN_EPS) * g_ref[:, co] + b_ref[:, co]
            rg = x_ref[:, cg]
            gated_ref[:, co] = (rg * jax.nn.sigmoid(rg) * rn).astype(BF16)

    ospec = pl.BlockSpec((CHUNK, RET_VW), lambda t: (t, 0))
    kw = dict(name="ret_fwd", grid=(N_CHUNK,), in_specs=_ret_specs(lambda t: t),
              out_specs=(ospec, ospec, pl.BlockSpec((RET_HEADS, None, RET_DK, RET_DV), lambda t: (0, t, 0, 0))),
              out_shape=(jax.ShapeDtypeStruct((S, RET_VW), BF16), jax.ShapeDtypeStruct((S, RET_VW), F32),
                         jax.ShapeDtypeStruct((RET_HEADS, N_CHUNK, RET_DK, RET_DV), F32)),
              scratch_shapes=[pltpu.VMEM((RET_HEADS, RET_DK, RET_DV), F32)])
    args = (proj, cos, sin, inner, qd, kd, cd, gn_g, gn_b)
    if comm is not None:
        return _carry(body, comm, **kw)(*args)
    return _pcall(body, compiler_params=_params(("arbitrary",)), **kw)(*args)


def _ret_bwd(proj, tables, gn_g, gn_b, ro, states, dgated, comm=None):
    inner, qd, kd, cd, cos, sin = tables
    last = N_CHUNK - 1

    def body(x_ref, cos_ref, sin_ref, in_ref, qd_ref, kd_ref, cd_ref, g_ref, b_ref, ro_ref, st_ref, dg_ref,
             dx_ref, gg_ref, gb_ref, gs_scr):
        t = pl.program_id(0)

        @pl.when(t == 0)
        def _():
            gs_scr[...] = jnp.zeros_like(gs_scr)
            gg_ref[...] = jnp.zeros_like(gg_ref)
            gb_ref[...] = jnp.zeros_like(gb_ref)

        cosv, sinv = cos_ref[...], sin_ref[...]
        for h in range(RET_HEADS):
            cq, ck, cv, cg, co = _ret_cols(h)
            q = _rot(x_ref[:, cq], cosv, sinv)
            k = _rot(x_ref[:, ck], cosv, sinv) * (RET_DK ** -0.5)
            v = x_ref[:, cv]
            qdv, kdv, dm = qd_ref[h], kd_ref[h], in_ref[h]
            st = st_ref[h]
            o = ro_ref[:, co]
            gv = g_ref[:, co]
            mu = jnp.mean(o, axis=-1, keepdims=True)
            oc = o - mu
            rstd = lax.rsqrt(jnp.mean(oc * oc, axis=-1, keepdims=True) + GN_EPS)
            ohat = oc * rstd
            rn = ohat * gv + b_ref[:, co]
            rg = x_ref[:, cg]
            sg = jax.nn.sigmoid(rg)
            dgt = dg_ref[:, co]
            drn = dgt * (rg * sg)
            dx_ref[:, cg] = (dgt * rn * (sg * (1.0 + rg * (1.0 - sg)))).astype(BF16)
            gg_ref[:, co] += jnp.sum(drn * ohat, axis=0, keepdims=True)
            gb_ref[:, co] += jnp.sum(drn, axis=0, keepdims=True)
            dohat = drn * gv
            do = rstd * (dohat - jnp.mean(dohat, axis=-1, keepdims=True)
                         - ohat * jnp.mean(dohat * ohat, axis=-1, keepdims=True))
            gs = gs_scr[h]
            s = _dot(q, k, NT) * dm
            dsr = _dot(do, v, NT) * dm
            dq = _dot(dsr, k, NN) + _dot(do, st, NT) * qdv
            dk = _dot(dsr, q, TN) + _dot(v, gs, NT) * kdv
            dv = _dot(s, do, TN) + _dot(k * kdv, gs, NN)
            gs_scr[h] = gs * cd_ref[h, :, :1] + _dot(q * qdv, do, TN)
            dx_ref[:, cq] = _rot_t(dq, cosv, sinv).astype(BF16)
            dx_ref[:, ck] = (_rot_t(dk, cosv, sinv) * (RET_DK ** -0.5)).astype(BF16)
            dx_ref[:, cv] = dv.astype(BF16)

    rev = lambda t: last - t
    vblk = pl.BlockSpec((CHUNK, RET_VW), lambda t: (rev(t), 0))
    vspec = pl.BlockSpec((1, RET_VW), lambda t: (0, 0))
    kw = dict(name="ret_bwd", grid=(N_CHUNK,),
              in_specs=_ret_specs(rev) + [vblk, pl.BlockSpec((RET_HEADS, None, RET_DK, RET_DV),
                                                             lambda t: (0, rev(t), 0, 0)), vblk],
              out_specs=(pl.BlockSpec((CHUNK, RET_COLS), lambda t: (rev(t), 0)), vspec, vspec),
              out_shape=(jax.ShapeDtypeStruct((S, RET_COLS), BF16), jax.ShapeDtypeStruct((1, RET_VW), F32),
                         jax.ShapeDtypeStruct((1, RET_VW), F32)),
              scratch_shapes=[pltpu.VMEM((RET_HEADS, RET_DK, RET_DV), F32)])
    args = (proj, cos, sin, inner, qd, kd, cd, gn_g, gn_b, ro, states, dgated)
    if comm is not None:
        return _carry(body, comm, **kw)(*args)
    return _pcall(body, compiler_params=_params(("arbitrary",)), **kw)(*args)


def _bucket_tables():
    qi = np.arange(ATT_BLK)[:, None]
    kj = np.arange(2 * ATT_BLK)[None, :]
    m = ATT_BLK + qi - kj
    out = []
    for win, dil in ATT_GROUPS:
        w = win // dil
        dist = (np.clip(m, 0, w) * dil).astype(np.int32)
        max_exact = N_BUCKETS // 2
        d_f = np.maximum(dist, 1).astype(np.float32)
        large = max_exact + (np.log(d_f / np.float32(max_exact)) / np.float32(math.log(MAX_DIST / max_exact))
                             * np.float32(N_BUCKETS - max_exact)).astype(np.int32)
        large = np.minimum(large, N_BUCKETS - 1)
        out.append(np.where(dist < max_exact, dist, large).astype(np.int32))
    return np.stack(out)


def _bias_build(rel_bias, buckets):
    def body(tab_ref, bk_ref, o_ref):
        hh = pl.program_id(0)
        bk = bk_ref[...]
        acc = jnp.zeros((ATT_BLK, 2 * ATT_BLK), F32)
        for b in range(N_BUCKETS):
            acc = jnp.where(bk == b, tab_ref[b, hh], acc)
        o_ref[...] = acc

    nh = len(ATT_GROUPS) * ATT_HG
    return _pcall(body, name="bias_build", grid=(nh,),
                  in_specs=[pl.BlockSpec(memory_space=pltpu.SMEM),
                            pl.BlockSpec((None, ATT_BLK, 2 * ATT_BLK), lambda hh: (hh // ATT_HG, 0, 0))],
                  out_specs=pl.BlockSpec((None, ATT_BLK, 2 * ATT_BLK), lambda hh: (hh, 0, 0)),
                  out_shape=jax.ShapeDtypeStruct((nh, ATT_BLK, 2 * ATT_BLK), F32),
                  compiler_params=_params(("parallel",)))(rel_bias, buckets)


def _bias_grad(ds_sum, buckets):
    def body(ds_ref, bk_ref, o_ref):
        bk = bk_ref[...]
        ds = ds_ref[...]
        rows = lax.broadcasted_iota(jnp.int32, (N_BUCKETS, 128), 0)
        acc = jnp.zeros((N_BUCKETS, 128), F32)
        for b in range(N_BUCKETS):
            acc = jnp.where(rows == b, jnp.sum(jnp.where(bk == b, ds, 0.0)), acc)
        o_ref[...] = acc

    nh = len(ATT_GROUPS) * ATT_HG
    return _pcall(body, name="bias_grad", grid=(nh,),
                  in_specs=[pl.BlockSpec((None, ATT_BLK, 2 * ATT_BLK), lambda hh: (hh, 0, 0)),
                            pl.BlockSpec((None, ATT_BLK, 2 * ATT_BLK), lambda hh: (hh // ATT_HG, 0, 0))],
                  out_specs=pl.BlockSpec((None, N_BUCKETS, 128), lambda hh: (hh, 0, 0)),
                  out_shape=jax.ShapeDtypeStruct((nh, N_BUCKETS, 128), F32),
                  compiler_params=_params(("parallel",)))(ds_sum, buckets)


def _att_valid(n):
    qi = lax.broadcasted_iota(jnp.int32, (ATT_BLK, 2 * ATT_BLK), 0)
    kj = lax.broadcasted_iota(jnp.int32, (ATT_BLK, 2 * ATT_BLK), 1)
    m = ATT_BLK + qi - kj
    first_key = jnp.where(n > 0, 0, ATT_BLK)
    return (m >= 0) & (m <= ATT_BLK) & (kj >= first_key)


ATT_HP = (1, 2, 2)


def _att_geometry(gi):
    _, dil = ATT_GROUPS[gi]
    return dil, S // dil // ATT_BLK, ATT_HP[gi]


def _blk(dil, r, n):
    if dil == 1:
        return pl.ds(n * ATT_BLK, ATT_BLK)
    return pl.ds(r + n * ATT_BLK * dil, ATT_BLK, stride=dil)


def _slab_specs(gi):
    _, _, hp = _att_geometry(gi)
    per = ATT_HG // hp
    return [pl.BlockSpec((hp, S, ATT_DH), lambda g, r, part=part: ((3 * gi + part) * per + g, 0, 0))
            for part in range(3)]


def _head_specs(gi, count):
    _, _, hp = _att_geometry(gi)
    return [pl.BlockSpec((hp, S, ATT_DH), lambda g, r: (g, 0, 0))] * count


def _bias_spec(gi):
    _, _, hp = _att_geometry(gi)
    return pl.BlockSpec((hp, ATT_BLK, 2 * ATT_BLK), lambda g, r: (gi * (ATT_HG // hp) + g, 0, 0))


def _att_fwd(slabs, bias, gi):
    dil, nb, hp = _att_geometry(gi)
    scale = ATT_DH ** -0.5

    def body(q_ref, k_ref, v_ref, bias_ref, o_ref, l_ref):
        r = pl.program_id(1)
        for n in range(nb):
            valid = _att_valid(n)
            prev = _blk(dil, r, max(n - 1, 0))
            cur = _blk(dil, r, n)
            for h in range(hp):
                kk = jnp.concatenate([k_ref[h, prev, :], k_ref[h, cur, :]], axis=0)
                vv = jnp.concatenate([v_ref[h, prev, :], v_ref[h, cur, :]], axis=0)
                s = _dot(q_ref[h, cur, :], kk, NT) * scale + bias_ref[h]
                s = jnp.where(valid, s, -1e30)
                mx = jnp.max(s, axis=-1, keepdims=True)
                e = jnp.exp(s - mx)
                den = jnp.sum(e, axis=-1, keepdims=True)
                o_ref[h, cur, :] = _dot(e / den, vv, NN)
                l_ref[h, cur, :] = jnp.broadcast_to(mx + jnp.log(den), (ATT_BLK, ATT_DH))

    osh = jax.ShapeDtypeStruct((ATT_HG, S, ATT_DH), F32)
    return _pcall(body, name=f"att_fwd{gi}", grid=(ATT_HG // hp, dil), in_specs=_slab_specs(gi) + [_bias_spec(gi)],
                  out_specs=tuple(_head_specs(gi, 2)), out_shape=(osh, osh),
                  compiler_params=_params(("parallel", "arbitrary")))(slabs, slabs, slabs, bias)


def _att_bwd(slabs, bias, o, lse, do, dlse, gi, comm=None):
    dil, nb, hp = _att_geometry(gi)
    per = ATT_HG // hp
    scale = ATT_DH ** -0.5
    wh = hp * ATT_DH
    wide = lambda t: jnp.concatenate([t, t], axis=1)

    def body(q_ref, k_ref, v_ref, bias_ref, o_ref, l_ref, do_ref, dl_ref, dq_ref, dk_ref, dv_ref, ds_ref):
        r = pl.program_id(1)

        @pl.when(r == 0)
        def _():
            ds_ref[...] = jnp.zeros_like(ds_ref)

        for h in range(hp):
            sl = slice(h * ATT_DH, (h + 1) * ATT_DH)
            carry_k = carry_v = None
            for n in range(nb):
                valid = _att_valid(n)
                prev = _blk(dil, r, max(n - 1, 0))
                cur = _blk(dil, r, n)
                q = q_ref[h, cur, :]
                kk = jnp.concatenate([k_ref[h, prev, :], k_ref[h, cur, :]], axis=0)
                vv = jnp.concatenate([v_ref[h, prev, :], v_ref[h, cur, :]], axis=0)
                dov = do_ref[h, cur, :]
                s = _dot(q, kk, NT) * scale + bias_ref[h]
                p = jnp.where(valid, jnp.exp(s - wide(l_ref[h, cur, :])), 0.0)
                dp = _dot(dov, vv, NT)
                delta = jnp.sum(dov * o_ref[h, cur, :], axis=-1, keepdims=True)
                ds = p * (dp - delta + wide(dl_ref[h, cur, :]))
                ds_ref[h] += ds
                out_rows = pl.ds(n * ATT_BLK, ATT_BLK)
                dq_ref[out_rows, sl] = (_dot(ds, kk, NN) * scale).astype(BF16)
                dkk = _dot(ds, q, TN) * scale
                dvv = _dot(p, dov, TN)
                if n > 0:
                    before = pl.ds((n - 1) * ATT_BLK, ATT_BLK)
                    dk_ref[before, sl] = (carry_k + dkk[:ATT_BLK]).astype(BF16)
                    dv_ref[before, sl] = (carry_v + dvv[:ATT_BLK]).astype(BF16)
                carry_k, carry_v = dkk[ATT_BLK:], dvv[ATT_BLK:]
            last = pl.ds((nb - 1) * ATT_BLK, ATT_BLK)
            dk_ref[last, sl] = carry_k.astype(BF16)
            dv_ref[last, sl] = carry_v.astype(BF16)

    out_spec = pl.BlockSpec((S // dil, wh), lambda g, r: (0, r * per + g))
    osh = jax.ShapeDtypeStruct((S // dil, dil * AW), BF16)
    kw = dict(name=f"att_bwd{gi}", grid=(per, dil), in_specs=_slab_specs(gi) + [_bias_spec(gi)] + _head_specs(gi, 4),
              out_specs=(out_spec, out_spec, out_spec,
                         pl.BlockSpec((hp, ATT_BLK, 2 * ATT_BLK), lambda g, r: (g, 0, 0))),
              out_shape=(osh, osh, osh, jax.ShapeDtypeStruct((ATT_HG, ATT_BLK, 2 * ATT_BLK), F32)))
    args = (slabs, slabs, slabs, bias, o, lse, do, dlse)
    if comm is not None:
        return _carry(body, comm, **kw)(*args)
    return _pcall(body, compiler_params=_params(("arbitrary", "arbitrary")), **kw)(*args)


AW = ATT_HG * ATT_DH


def _mix_weights(l0, l1, l2):
    mx = jnp.maximum(jnp.maximum(l0, l1), l2)
    e0, e1, e2 = jnp.exp(l0 - mx), jnp.exp(l1 - mx), jnp.exp(l2 - mx)
    den = e0 + e1 + e2
    return e0 / den, e1 / den, e2 / den


def _heads_spec():
    return pl.BlockSpec((ATT_HG, TR, ATT_DH), lambda i: (0, i, 0))


def _mix_fwd(os_, ls):
    def body(o0, o1, o2, l0, l1, l2, att_ref):
        for h in range(ATT_HG):
            w0, w1, w2 = _mix_weights(l0[h], l1[h], l2[h])
            att_ref[:, h * ATT_DH:(h + 1) * ATT_DH] = (w0 * o0[h] + w1 * o1[h] + w2 * o2[h]).astype(BF16)

    return _pcall(body, name="mix_fwd", grid=(S // TR,), in_specs=[_heads_spec()] * 6, out_specs=_row_spec(AW),
                  out_shape=jax.ShapeDtypeStruct((S, AW), BF16), compiler_params=_params(("parallel",)))(*os_, *ls)


def _mix_bwd(os_, ls, datt):
    def body(o0, o1, o2, l0, l1, l2, da_ref, d0, d1, d2, e0, e1, e2):
        for h in range(ATT_HG):
            ws = _mix_weights(l0[h], l1[h], l2[h])
            da = da_ref[:, h * ATT_DH:(h + 1) * ATT_DH]
            dws = []
            for o_ref, w, d_ref in zip((o0, o1, o2), ws, (d0, d1, d2)):
                d_ref[h] = w * da
                dws.append(jnp.broadcast_to(jnp.sum(da * o_ref[h], axis=-1, keepdims=True), (TR, ATT_DH)))
            tot = ws[0] * dws[0] + ws[1] * dws[1] + ws[2] * dws[2]
            for w, dw, e_ref in zip(ws, dws, (e0, e1, e2)):
                e_ref[h] = w * (dw - tot)

    o = jax.ShapeDtypeStruct((ATT_HG, S, ATT_DH), F32)
    return _pcall(body, name="mix_bwd", grid=(S // TR,), in_specs=[_heads_spec()] * 6 + [_row_spec(AW)],
                  out_specs=(_heads_spec(),) * 6, out_shape=(o,) * 6,
                  compiler_params=_params(("parallel",)))(*os_, *ls, datt)


def _ada_fwd(c_all, w_sh, b_sl):
    def body(c_ref, w_ref, b_ref, o_ref):
        cv = c_ref[...]
        o_ref[...] = _dot(cv * jax.nn.sigmoid(cv), w_ref[...], NN) + b_ref[...]

    return _pcall(body, name="ada_fwd", out_shape=jax.ShapeDtypeStruct((N_DEV, w_sh.shape[1]), F32),
                  compiler_params=_params())(c_all, w_sh, b_sl)


def _ada_bwd(c_all, dm_sl):
    def body(c_ref, d_ref, o_ref):
        cv = c_ref[...]
        o_ref[...] = _dot(cv * jax.nn.sigmoid(cv), d_ref[...], TN)

    return _pcall(body, name="ada_bwd", out_shape=jax.ShapeDtypeStruct((D, dm_sl.shape[1]), F32),
                  compiler_params=_params())(c_all, dm_sl)


def _sum_slots(g, name):
    n = g.shape[0]

    def body(g_ref, o_ref):
        acc = g_ref[0]
        for e in range(1, n):
            acc = acc + g_ref[e]
        o_ref[...] = acc

    return _pcall(body, name=name, out_shape=jax.ShapeDtypeStruct(g.shape[1:], F32), compiler_params=_params())(g)


def _row_tile(m, n):
    t = max(8, min(m, (1 << 19) // n // 8 * 8))
    while m % t:
        t -= 8
    return t


def _pair_sum(full, recv, sel, name):
    _, _, m, n = full.shape
    t = _row_tile(m, n)

    def body(sel_ref, a_ref, b_ref, o_ref):
        o_ref[...] = (a_ref[...].astype(F32) + b_ref[...].astype(F32)).astype(o_ref.dtype)

    gs = pltpu.PrefetchScalarGridSpec(
        num_scalar_prefetch=1, grid=(4, m // t),
        in_specs=[pl.BlockSpec((None, None, t, n), lambda q, i, s: (q, s[0], i, 0)),
                  pl.BlockSpec((None, t, n), lambda q, i, s: (q, i, 0))],
        out_specs=pl.BlockSpec((None, t, n), lambda q, i, s: (q, i, 0)))
    return _pcall(body, name=name, grid_spec=gs, out_shape=jax.ShapeDtypeStruct((4, m, n), full.dtype),
                  compiler_params=_params(("parallel", "parallel")))(sel, full, recv)


def _chip_sum(part, recv, sel, name):
    _, m, n = part.shape
    t = _row_tile(m, n)

    def body(sel_ref, a_ref, r_ref, o_ref):
        o_ref[...] = ((a_ref[...].astype(F32) + r_ref[0].astype(F32)) + r_ref[1].astype(F32)) + r_ref[2].astype(F32)

    gs = pltpu.PrefetchScalarGridSpec(
        num_scalar_prefetch=1, grid=(m // t,),
        in_specs=[pl.BlockSpec((None, t, n), lambda i, s: (s[0], i, 0)),
                  pl.BlockSpec((3, t, n), lambda i, s: (0, i, 0))],
        out_specs=pl.BlockSpec((t, n), lambda i, s: (i, 0)))
    return _pcall(body, name=name, grid_spec=gs, out_shape=jax.ShapeDtypeStruct((m, n), F32),
                  compiler_params=_params(("parallel",)))(sel, part, recv)


def _adamw(w, g, m, v, name):
    rows, cols = w.shape
    t = _row_tile(rows, cols) if rows >= 8 else rows

    def body(w_ref, g_ref, m_ref, v_ref, d_ref, nm_ref, nv_ref):
        gv = g_ref[...]
        nm = ADAM_B1 * m_ref[...] + (1.0 - ADAM_B1) * gv
        nv = ADAM_B2 * v_ref[...] + (1.0 - ADAM_B2) * (gv * gv)
        m_hat = nm / (1.0 - ADAM_B1 ** ADAM_STEP)
        v_hat = nv / (1.0 - ADAM_B2 ** ADAM_STEP)
        d_ref[...] = -ADAM_LR * (m_hat / (jnp.sqrt(v_hat) + ADAM_EPS) + ADAM_WD * w_ref[...])
        nm_ref[...] = nm
        nv_ref[...] = nv

    spec = pl.BlockSpec((t, cols), lambda i: (i, 0))
    o = jax.ShapeDtypeStruct((rows, cols), F32)
    return _pcall(body, name=name, grid=(rows // t,), in_specs=[spec] * 4, out_specs=(spec,) * 3,
                  out_shape=(o, o, o), compiler_params=_params(("parallel",)))(w, g, m, v)


def _mesh_pos():
    return lax.axis_index("x"), lax.axis_index("y"), lax.axis_index("c")


class _Gather:
    def __init__(self, arrs):
        self.ins = list(arrs)
        na = self.na = len(arrs)
        self.out_shape = tuple(jax.ShapeDtypeStruct((N_DEV,) + a.shape, a.dtype) for a in arrs)
        self.sems = [pltpu.SemaphoreType.DMA((7 * na,)), pltpu.SemaphoreType.DMA((7 * na,)),
                     pltpu.SemaphoreType.DMA((na,))]

    def _copies(self, ins, outs, sems):
        send_sems, recv_sems, local_sems = sems
        x, y, c = _mesh_pos()
        me, sibling = (x, y, c), (x, y, 1 - c)
        chips = [(1 - x, y), (x, 1 - y), (1 - x, 1 - y)]

        def slot(p):
            return 4 * p[0] + 2 * p[1] + p[2]

        def copy(a, k, block, to, src=None):
            dst = outs[a].at[slot(block)]
            return pltpu.make_async_remote_copy(
                src_ref=dst if src is None else src, dst_ref=dst, send_sem=send_sems.at[7 * a + k],
                recv_sem=recv_sems.at[7 * a + k], device_id=to, device_id_type=MESH)

        mine = [pltpu.make_async_copy(ins[a], outs[a].at[slot(me)], local_sems.at[a]) for a in range(self.na)]
        first = []
        for a in range(self.na):
            first.append(copy(a, 0, me, sibling, src=ins[a]))
            first += [copy(a, 1 + j, me, (*chip, c), src=ins[a]) for j, chip in enumerate(chips)]
        return me, sibling, chips, c, copy, mine, first

    def start(self, ins, outs, sems):
        *_, mine, first = self._copies(ins, outs, sems)
        for cp in mine + first:
            cp.start()

    def finish(self, ins, outs, sems):
        me, sibling, chips, c, copy, mine, first = self._copies(ins, outs, sems)
        passed = []
        for j, chip in enumerate(chips):
            for a in range(self.na):
                copy(a, 1 + j, (*chip, c), me).wait_recv()
                cp = copy(a, 4 + j, (*chip, c), sibling)
                cp.start()
                passed.append(cp)
        for a in range(self.na):
            copy(a, 0, sibling, me).wait_recv()
            for j, chip in enumerate(chips):
                copy(a, 4 + j, (*chip, 1 - c), me).wait_recv()
        for cp in first + passed:
            cp.wait_send()
        for cp in mine:
            cp.wait()


class _ExchangeCore:
    def __init__(self, fulls):
        self.ins = list(fulls)
        self.out_shape = tuple(jax.ShapeDtypeStruct((4,) + f.shape[2:], f.dtype) for f in fulls)
        self.sems = [pltpu.SemaphoreType.DMA((4 * len(fulls),)), pltpu.SemaphoreType.DMA((4 * len(fulls),))]

    def _copies(self, ins, outs, sems):
        send_sems, recv_sems = sems
        x, y, c = _mesh_pos()
        return [pltpu.make_async_remote_copy(
            src_ref=ins[a].at[q, 1 - c], dst_ref=outs[a].at[q], send_sem=send_sems.at[4 * a + q],
            recv_sem=recv_sems.at[4 * a + q], device_id=(x, y, 1 - c), device_id_type=MESH)
            for a in range(len(self.ins)) for q in range(4)]

    def start(self, ins, outs, sems):
        for cp in self._copies(ins, outs, sems):
            cp.start()

    def finish(self, ins, outs, sems):
        for cp in self._copies(ins, outs, sems):
            cp.wait()


class _ExchangeChip:
    def __init__(self, parts):
        self.ins = list(parts)
        self.out_shape = tuple(jax.ShapeDtypeStruct((3,) + p.shape[1:], p.dtype) for p in parts)
        self.sems = [pltpu.SemaphoreType.DMA((3 * len(parts),)), pltpu.SemaphoreType.DMA((3 * len(parts),))]

    def _copies(self, ins, outs, sems):
        send_sems, recv_sems = sems
        x, y, c = _mesh_pos()
        chips = [(1 - x, y), (x, 1 - y), (1 - x, 1 - y)]
        return [pltpu.make_async_remote_copy(
            src_ref=ins[a].at[2 * px + py], dst_ref=outs[a].at[j], send_sem=send_sems.at[3 * a + j],
            recv_sem=recv_sems.at[3 * a + j], device_id=(px, py, c), device_id_type=MESH)
            for a in range(len(self.ins)) for j, (px, py) in enumerate(chips)]

    def start(self, ins, outs, sems):
        for cp in self._copies(ins, outs, sems):
            cp.start()

    def finish(self, ins, outs, sems):
        for cp in self._copies(ins, outs, sems):
            cp.wait()


def _reduce_sums(fulls, recv_core, core, tag):
    return [_pair_sum(f, r, core, f"rs_pair_{tag}{i}") for i, (f, r) in enumerate(zip(fulls, recv_core))]


def _local_step(x, tgt, mods, w_in, shards, small, chip, core):
    sh1, sc1, g1, sh2, sc2, g2 = mods
    norm1_g, rel_bias, gn_g, gn_b, norm2_g, norm_f_g = small
    tables = _ret_tables()
    buckets = jnp.asarray(_bucket_tables())

    h1 = _norm_mod_fwd(x, norm1_g, sh1, sc1, "norm1_fwd")
    proj, slabs, gathered = _proj(h1, w_in, _Gather(shards[:3]))
    w_ret_out, w_att_out, w_o = (_from_slots(g, ax) for g, ax in zip(gathered, BIG_AXES[1:4]))
    (gated, ro, states), gathered = _ret_fwd(proj, tables, gn_g, gn_b, comm=_Gather(shards[3:]))
    w_ff1, w_ff2 = (_from_slots(g, ax) for g, ax in zip(gathered, BIG_AXES[4:]))
    bias = _bias_build(rel_bias, buckets)
    outs, lses = [], []
    for gi in range(len(ATT_GROUPS)):
        o, l = _att_fwd(slabs, bias, gi)
        outs.append(o)
        lses.append(l)
    att = _mix_fwd(outs, lses)
    ret_out = _mm(gated, w_ret_out, 'nn', tm=S, tn=256, tk=2048, name="ret_out")
    att_out = _mm(att, w_att_out, 'nn', tm=S, tn=512, tk=AW, name="att_out")
    merged = _merge_fwd(proj, ret_out, att_out)
    mixo, x1 = _mm(merged, w_o, 'nn', tm=S, tn=256, tk=D, name="w_o", res=x, gvec=g1)
    h2 = _norm_mod_fwd(x1, norm2_g, sh2, sc2, "norm2_fwd")
    u = _mm(h2, w_ff1, 'nn', tm=S, tn=512, tk=D, name="ff1")
    act = _relu2_fwd(u)
    f, x2 = _mm(act, w_ff2, 'nn', tm=1024, tn=512, tk=2048, name="ff2", res=x1, gvec=g2)
    loss, dx2, g_normf = _final_loss(x2, tgt, norm_f_g)

    df, dg2 = _gate_bwd(dx2, f, g2, "gate2_bwd")
    gw_ff2 = _mm(act, df, 'tn', tm=512, tn=D, tk=S, name="gw_ff2", out_dtype=BF16)
    d_act = _mm(df, w_ff2, 'nt', tm=S, tn=512, tk=D, name="d_act")
    du = _relu2_bwd(d_act, u)
    gw_ff1 = _mm(h2, du, 'tn', tm=D, tn=512, tk=S, name="gw_ff1", out_dtype=BF16)
    fulls_a = [_to_slots(g, ax) for g, ax in zip((gw_ff1, gw_ff2), BIG_AXES[4:])]
    dh2, recv_core_a = _mm(du, w_ff1, 'nt', tm=1024, tn=1024, tk=1024, name="dh2", comm=_ExchangeCore(fulls_a))
    parts_a = _reduce_sums(fulls_a, recv_core_a, core, "a")
    dx1, dsc2, dsh2, g_norm2 = _norm_mod_bwd(x1, norm2_g, sc2, dh2, dx2, "norm2_bwd")

    dmixo, dg1 = _gate_bwd(dx1, mixo, g1, "gate1_bwd")
    gw_o = _mm(merged, dmixo, 'tn', tm=D, tn=512, tk=S, name="gw_o", out_dtype=BF16)
    dmerged = _mm(dmixo, w_o, 'nt', tm=S, tn=512, tk=D, name="dmerged")
    d_ret_out, d_att_out, dga, dgb = _merge_bwd(proj, ret_out, att_out, dmerged)
    gw_ret_out = _mm(gated, d_ret_out, 'tn', tm=512, tn=D, tk=S, name="gw_ret_out", out_dtype=BF16)
    gw_att_out = _mm(att, d_att_out, 'tn', tm=AW, tn=D, tk=S, name="gw_att_out", out_dtype=BF16)
    fulls_b = [_to_slots(g, ax) for g, ax in zip((gw_ret_out, gw_att_out, gw_o), BIG_AXES[1:4])]
    dgated, recv_core_b = _mm(d_ret_out, w_ret_out, 'nt', tm=S, tn=512, tk=D, name="dgated",
                              comm=_ExchangeCore(fulls_b))
    parts_b = _reduce_sums(fulls_b, recv_core_b, core, "b")
    datt = _mm(d_att_out, w_att_out, 'nt', tm=S, tn=AW, tk=D, name="datt")
    mix_grads = _mix_bwd(outs, lses, datt)
    datt_parts, ds_sums = [], []
    for gi in range(len(ATT_GROUPS)):
        res = _att_bwd(slabs, bias, outs[gi], lses[gi], mix_grads[gi], mix_grads[3 + gi], gi,
                       comm=_ExchangeChip(parts_b) if gi == 1 else None)
        if gi == 1:
            res, recv_chip_b = res
        dq, dk, dv, ds_sum = res
        datt_parts += [dq.reshape(S, AW), dk.reshape(S, AW), dv.reshape(S, AW)]
        ds_sums.append(ds_sum)
    red_b = [_chip_sum(p, r, chip, f"rs_sum_b{i}") for i, (p, r) in enumerate(zip(parts_b, recv_chip_b))]
    g_bias = _bias_grad(jnp.concatenate(ds_sums, axis=0), buckets)[:, :, 0].T
    (dret, g_gn_g, g_gn_b), recv_chip_a = _ret_bwd(proj, tables, gn_g, gn_b, ro, states, dgated,
                                                   comm=_ExchangeChip(parts_a))
    red_a = [_chip_sum(p, r, chip, f"rs_sum_a{i}") for i, (p, r) in enumerate(zip(parts_a, recv_chip_a))]
    dproj = jnp.concatenate([dret] + datt_parts + [dga, dgb], axis=1)
    gw_in = _mm(h1, dproj, 'tn', tm=D, tn=512, tk=S, name="gw_in", out_dtype=BF16)
    full_in = [_to_slots(gw_in, BIG_AXES[0])]
    dh1, recv_core_in = _mm(dproj, w_in, 'nt', tm=1024, tn=1024, tk=512, name="dh1", comm=_ExchangeCore(full_in))
    part_in = _reduce_sums(full_in, recv_core_in, core, "c")
    gx, dsc1, dsh1, g_norm1 = _norm_mod_bwd(x, norm1_g, sc1, dh1, dx1, "norm1_bwd")
    recv_chip_in = _run_comm(_ExchangeChip(part_in), "rs_chip_in")
    red_in = _chip_sum(part_in[0], recv_chip_in[0], chip, "rs_sum_c")

    dmod = jnp.concatenate([dsh1, dsc1, dg1, dsh2, dsc2, dg2], axis=1)
    small_g = (g_norm1, g_bias, g_gn_g, g_gn_b, g_norm2, g_normf)
    return loss, gx, [red_in] + red_b + red_a, small_g, dmod


def _to_slots(g, axis):
    if axis == 0:
        return g.reshape(4, 2, g.shape[0] // N_DEV, g.shape[1])
    return g.reshape(g.shape[0], N_DEV, g.shape[1] // N_DEV).transpose(1, 0, 2).reshape(4, 2, g.shape[0], -1)


def _from_slots(w8, axis):
    if axis == 0:
        return w8.reshape(-1, w8.shape[2])
    return w8.transpose(1, 0, 2).reshape(w8.shape[1], -1)


BIG_AXES = (1, 0, 1, 0, 1, 0)


def kernel(x, c, w_ada, b_ada, norm1_g, w_in, rel_bias, ret_gn_g, ret_gn_b, w_ret_out, w_att_out, w_o, norm2_g, w_ff1, w_ff2, norm_f_g, loss_target, m_w_ada, m_b_ada, m_norm1_g, m_w_in, m_rel_bias, m_ret_gn_g, m_ret_gn_b, m_w_ret_out, m_w_att_out, m_w_o, m_norm2_g, m_w_ff1, m_w_ff2, m_norm_f_g, v_w_ada, v_b_ada, v_norm1_g, v_w_in, v_rel_bias, v_ret_gn_g, v_ret_gn_b, v_w_ret_out, v_w_att_out, v_w_o, v_norm2_g, v_w_ff1, v_w_ff2, v_norm_f_g):
    mx, my, mc = _mesh_pos()
    dev = 4 * mx + 2 * my + mc
    chip = jnp.reshape(2 * mx + my, (1,)).astype(jnp.int32)
    core = jnp.reshape(mc, (1,)).astype(jnp.int32)
    ada_w = D * 6 // N_DEV

    shards = [w[0].astype(BF16) for w in (w_in, w_ret_out, w_att_out, w_o, w_ff1, w_ff2)]
    c_all, w_in8 = _run_comm(_Gather([c, shards[0]]), "gather_c_w_in")
    c_all = c_all.reshape(N_DEV, D)
    b_sl = lax.dynamic_slice(b_ada, (0, dev * ada_w), (1, ada_w))
    (mod_all,) = _run_comm(_Gather([_ada_fwd(c_all, w_ada[0], b_sl)]), "gather_mod")
    mod = lax.dynamic_index_in_dim(mod_all, dev, axis=1, keepdims=False).reshape(6, D)
    mods = tuple(mod[i:i + 1] for i in range(6))

    small = (norm1_g, rel_bias, ret_gn_g, ret_gn_b, norm2_g, norm_f_g.reshape(1, D))
    loss, gx, big_red, small_g, dmod = _local_step(x[0], loss_target[0], mods, _from_slots(w_in8, BIG_AXES[0]),
                                                   shards[1:], small, chip, core)

    g_norm1, g_bias, g_gn_g, g_gn_b, g_norm2, g_normf = small_g
    pack = jnp.concatenate([dmod, g_norm1, g_bias.reshape(1, -1), g_gn_g, g_gn_b, g_norm2, g_normf, loss], axis=1)
    (pack_all,) = _run_comm(_Gather([pack]), "gather_small")
    tot = _sum_slots(pack_all, "sum_small")
    offs = np.cumsum([0, 6 * D, D, N_BUCKETS * 12, 2048, 2048, D, D, 128])
    seg = [tot[:, offs[i]:offs[i + 1]] for i in range(8)]
    g_b_ada, g_norm1, g_bias, g_gn_g, g_gn_b, g_norm2, g_normf = seg[:7]
    loss_out = seg[7][0, 0]
    dmod_all = pack_all[:, 0, :6 * D]
    g_w_ada = _ada_bwd(c_all, lax.dynamic_slice(dmod_all, (0, dev * ada_w), (N_DEV, ada_w)))

    names = ['w_ada', 'b_ada', 'norm1_g', 'w_in', 'rel_bias', 'ret_gn_g', 'ret_gn_b', 'w_ret_out', 'w_att_out',
             'w_o', 'norm2_g', 'w_ff1', 'w_ff2', 'norm_f_g']
    ws = dict(zip(names, (w_ada, b_ada, norm1_g, w_in, rel_bias, ret_gn_g, ret_gn_b, w_ret_out, w_att_out, w_o,
                          norm2_g, w_ff1, w_ff2, norm_f_g)))
    ms = dict(zip(names, (m_w_ada, m_b_ada, m_norm1_g, m_w_in, m_rel_bias, m_ret_gn_g, m_ret_gn_b, m_w_ret_out,
                          m_w_att_out, m_w_o, m_norm2_g, m_w_ff1, m_w_ff2, m_norm_f_g)))
    vs = dict(zip(names, (v_w_ada, v_b_ada, v_norm1_g, v_w_in, v_rel_bias, v_ret_gn_g, v_ret_gn_b, v_w_ret_out,
                          v_w_att_out, v_w_o, v_norm2_g, v_w_ff1, v_w_ff2, v_norm_f_g)))
    grads = dict(w_ada=g_w_ada, w_in=big_red[0], w_ret_out=big_red[1], w_att_out=big_red[2], w_o=big_red[3],
                 w_ff1=big_red[4], w_ff2=big_red[5], b_ada=g_b_ada, norm1_g=g_norm1, rel_bias=g_bias,
                 ret_gn_g=g_gn_g, ret_gn_b=g_gn_b, norm2_g=g_norm2, norm_f_g=g_normf)
    delta, new_m, new_v = {}, {}, {}
    for n in ('w_ada', 'w_in', 'w_ret_out', 'w_att_out', 'w_o', 'w_ff1', 'w_ff2'):
        shp = ws[n].shape
        d_, m_, v_ = _adamw(ws[n][0], grads[n], ms[n][0], vs[n][0], "adamw_" + n)
        delta[n], new_m[n], new_v[n] = d_.reshape(shp), m_.reshape(shp), v_.reshape(shp)
        grads[n] = grads[n].reshape(shp)
    small_names = ('b_ada', 'norm1_g', 'rel_bias', 'ret_gn_g', 'ret_gn_b', 'norm2_g', 'norm_f_g')
    flat = lambda d: jnp.concatenate([d[n].reshape(1, -1) for n in small_names], axis=1)
    d_, m_, v_ = _adamw(flat(ws), flat(grads), flat(ms), flat(vs), "adamw_small")
    o = 0
    for n in small_names:
        shp = ws[n].shape
        sz = int(np.prod(shp))
        delta[n], new_m[n], new_v[n] = (t[:, o:o + sz].reshape(shp) for t in (d_, m_, v_))
        grads[n] = grads[n].reshape(shp)
        o += sz
    return (loss_out, gx[None], *[grads[n] for n in names], *[delta[n] for n in names],
            *[new_m[n] for n in names], *[new_v[n] for n in names])
```

```python
import functools
import math

import numpy as np
import jax
import jax.numpy as jnp
from jax import lax
from jax.experimental import pallas as pl
from jax.experimental.pallas import tpu as pltpu

F32 = jnp.float32
BF16 = jnp.bfloat16
MESH = pl.DeviceIdType.MESH

N_DEV = 8
S = 2048
D = 1024
RET_HEADS = 4
RET_DK = 256
RET_DV = 512
CHUNK = 128
N_CHUNK = S // CHUNK
ATT_GROUPS = ((128, 1), (512, 4), (2048, 16))
ATT_HG = 4
ATT_DH = 128
ATT_BLK = 128
N_BUCKETS = 32
MAX_DIST = 2048
D_FF = 4096
IN_COLS = 12800
OFF_RQ, OFF_RK, OFF_RV, OFF_RG, OFF_ATT = 0, 1024, 2048, 4096, 6144
OFF_GA, OFF_GB = 6144, 7168
RMS_EPS = 1e-6
GN_EPS = 1e-5
ADAM_LR, ADAM_B1, ADAM_B2, ADAM_EPS, ADAM_WD, ADAM_STEP = 0.001, 0.9, 0.999, 1e-08, 0.01, 10
VMEM_LIMIT = 48 * 1024 * 1024


def _pcall(body, **kw):
    return pl.pallas_call(body, **kw)


def _params(sem=None):
    return pltpu.CompilerParams(dimension_semantics=sem, vmem_limit_bytes=VMEM_LIMIT)


HBM_SPEC = pl.BlockSpec(memory_space=pl.ANY)


def _carry(body, comm, *, name, grid, in_specs, out_specs, out_shape, scratch_shapes=()):
    single = not isinstance(out_specs, (tuple, list))
    o_specs = (out_specs,) if single else tuple(out_specs)
    o_shape = (out_shape,) if single else tuple(out_shape)
    n_in, n_out, n_scr = len(in_specs), len(o_specs), len(scratch_shapes)
    nci, nco = len(comm.ins), len(comm.out_shape)
    total = int(np.prod(grid))

    def wrapped(*refs):
        bounds = np.cumsum([0, n_in, nci, n_out, nco, n_scr])
        a, ci, o, co, scr = (refs[bounds[i]:bounds[i + 1]] for i in range(5))
        sems = refs[bounds[5]:]
        flat = 0
        for d, g in enumerate(grid):
            flat = flat * g + pl.program_id(d)

        @pl.when(flat == 0)
        def _():
            comm.start(ci, co, sems)

        body(*a, *o, *scr)

        @pl.when(flat == total - 1)
        def _():
            comm.finish(ci, co, sems)

    call = _pcall(wrapped, name=name, grid=grid, in_specs=list(in_specs) + [HBM_SPEC] * nci,
                  out_specs=o_specs + (HBM_SPEC,) * nco, out_shape=o_shape + tuple(comm.out_shape),
                  scratch_shapes=list(scratch_shapes) + list(comm.sems),
                  compiler_params=_params(("arbitrary",) * len(grid)))

    def run(*args):
        res = call(*args, *comm.ins)
        own = res[0] if single else tuple(res[:n_out])
        return own, tuple(res[n_out:])

    return run


def _run_comm(comm, name):
    nci, nco = len(comm.ins), len(comm.out_shape)

    def body(*refs):
        ci, co, sems = refs[:nci], refs[nci:nci + nco], refs[nci + nco:]
        comm.start(ci, co, sems)
        comm.finish(ci, co, sems)

    return _pcall(body, name=name, in_specs=[HBM_SPEC] * nci, out_specs=(HBM_SPEC,) * nco,
                  out_shape=tuple(comm.out_shape), scratch_shapes=list(comm.sems))(*comm.ins)


def _dot(a, b, dn):
    return lax.dot_general(a.astype(BF16), b.astype(BF16), (dn, ((), ())), preferred_element_type=F32)


NN = ((1,), (0,))
NT = ((1,), (1,))
TN = ((0,), (0,))


def _mm(a, b, mode, *, tm, tn, tk, name, out_dtype=F32, res=None, gvec=None, relu2=False, relu2_of=None, comm=None):
    if mode == 'nn':
        (M, K), (_, N) = a.shape, b.shape
        a_spec = pl.BlockSpec((tm, tk), lambda i, j, k: (i, k))
        b_spec = pl.BlockSpec((tk, tn), lambda i, j, k: (k, j))
        dn = NN
    elif mode == 'nt':
        (M, K), (N, _) = a.shape, b.shape
        a_spec = pl.BlockSpec((tm, tk), lambda i, j, k: (i, k))
        b_spec = pl.BlockSpec((tn, tk), lambda i, j, k: (j, k))
        dn = NT
    else:
        (K, M), (_, N) = a.shape, b.shape
        a_spec = pl.BlockSpec((tk, tm), lambda i, j, k: (k, i))
        b_spec = pl.BlockSpec((tk, tn), lambda i, j, k: (k, j))
        dn = TN
    assert M % tm == 0 and N % tn == 0 and K % tk == 0, (name, M, N, K)
    nk = K // tk
    fused = res is not None
    o_spec = pl.BlockSpec((tm, tn), lambda i, j, k: (i, j))

    def body(a_ref, b_ref, *rest):
        acc_ref = rest[-1] if nk > 1 else None
        if fused:
            res_ref, g_ref, o_ref, x_ref = rest[:4]
        elif relu2_of is not None:
            u_ref, o_ref = rest[:2]
        elif relu2:
            o_ref, act_ref = rest[:2]
        else:
            o_ref = rest[0]

        def finish(acc):
            if relu2_of is not None:
                acc = acc * (2.0 * jnp.maximum(u_ref[...], 0.0))
            o_ref[...] = acc.astype(o_ref.dtype)
            if fused:
                x_ref[...] = res_ref[...] + g_ref[...] * acc
            if relu2:
                r = jnp.maximum(acc, 0.0)
                act_ref[...] = (r * r).astype(BF16)

        p = _dot(a_ref[...], b_ref[...], dn)
        if nk == 1:
            finish(p)
        else:
            k = pl.program_id(2)

            @pl.when(k == 0)
            def _():
                acc_ref[...] = p

            @pl.when(k > 0)
            def _():
                acc_ref[...] += p

            @pl.when(k == nk - 1)
            def _():
                finish(acc_ref[...])

    in_specs = [a_spec, b_spec]
    args = [a, b]
    out_shape = jax.ShapeDtypeStruct((M, N), out_dtype)
    out_specs = o_spec
    if fused:
        in_specs += [pl.BlockSpec((tm, tn), lambda i, j, k: (i, j)), pl.BlockSpec((1, tn), lambda i, j, k: (0, j))]
        args += [res, gvec]
        out_shape = (out_shape, jax.ShapeDtypeStruct((M, N), F32))
        out_specs = (o_spec, pl.BlockSpec((tm, tn), lambda i, j, k: (i, j)))
    elif relu2_of is not None:
        in_specs.append(pl.BlockSpec((tm, tn), lambda i, j, k: (i, j)))
        args.append(relu2_of)
    elif relu2:
        out_shape = (out_shape, jax.ShapeDtypeStruct((M, N), BF16))
        out_specs = (o_spec, pl.BlockSpec((tm, tn), lambda i, j, k: (i, j)))
    kw = dict(name=name, grid=(M // tm, N // tn, nk), in_specs=in_specs, out_specs=out_specs,
              out_shape=out_shape, scratch_shapes=[pltpu.VMEM((tm, tn), F32)] if nk > 1 else [])
    if comm is not None:
        return _carry(body, comm, **kw)(*args)
    return _pcall(body, compiler_params=_params(("parallel", "parallel", "arbitrary")), **kw)(*args)


PROJ_TN = 512
ATT_T0, ATT_T1 = 6144 // PROJ_TN, 10752 // PROJ_TN
N_SLABS = (ATT_T1 - ATT_T0) * 4
MAIN_COLS = IN_COLS - (ATT_T1 - ATT_T0) * PROJ_TN


def _proj(h1, w_in, comm):
    nj = IN_COLS // PROJ_TN

    def body(a_ref, b_ref, main_ref, slab_ref):
        j = pl.program_id(1)
        p = _dot(a_ref[...], b_ref[...], NN)
        is_att = (j >= ATT_T0) & (j < ATT_T1)

        @pl.when(jnp.logical_not(is_att))
        def _():
            main_ref[...] = p

        @pl.when(is_att)
        def _():
            for h in range(4):
                slab_ref[h] = p[:, h * 128:(h + 1) * 128]

    main_idx = lambda j: jnp.where(j < ATT_T0, j, jnp.where(j < ATT_T1, ATT_T0 - 1, j - (ATT_T1 - ATT_T0)))
    slab_idx = lambda j: jnp.clip(j - ATT_T0, 0, ATT_T1 - ATT_T0 - 1)
    (main, slabs), got = _carry(
        body, comm, name="proj", grid=(1, nj, 1),
        in_specs=[pl.BlockSpec((S, D), lambda i, j, k: (0, 0)), pl.BlockSpec((D, PROJ_TN), lambda i, j, k: (0, j))],
        out_specs=(pl.BlockSpec((S, PROJ_TN), lambda i, j, k: (0, main_idx(j))),
                   pl.BlockSpec((4, S, 128), lambda i, j, k: (slab_idx(j), 0, 0))),
        out_shape=(jax.ShapeDtypeStruct((S, MAIN_COLS), F32), jax.ShapeDtypeStruct((N_SLABS, S, 128), F32)))(h1, w_in)
    return main, slabs, got


TR = 256


def _row_spec(w=D):
    return pl.BlockSpec((TR, w), lambda i: (i, 0))


def _vec_spec(w=D):
    return pl.BlockSpec((1, w), lambda i: (0, 0))


def _norm_mod_fwd(x, g, sh, sc, name):
    def body(x_ref, g_ref, sh_ref, sc_ref, o_ref):
        xv = x_ref[...]
        rstd = lax.rsqrt(jnp.mean(xv * xv, axis=-1, keepdims=True) + RMS_EPS)
        n = xv * rstd * g_ref[...]
        o_ref[...] = (n * (1.0 + sc_ref[...]) + sh_ref[...]).astype(BF16)

    return _pcall(body, name=name, grid=(S // TR,), in_specs=[_row_spec(), _vec_spec(), _vec_spec(), _vec_spec()],
                  out_specs=_row_spec(), out_shape=jax.ShapeDtypeStruct((S, D), BF16),
                  compiler_params=_params(("parallel",)))(x, g, sh, sc)


def _norm_mod_bwd(x, g, sc, dh, dres, name, gate=None):
    gated = gate is not None

    def body(x_ref, g_ref, sc_ref, dh_ref, dres_ref, *rest):
        if gated:
            f_ref, gv_ref, dx_ref, dsc_ref, dsh_ref, dg_ref, dz_ref, dgv_ref = rest
        else:
            dx_ref, dsc_ref, dsh_ref, dg_ref = rest
        i = pl.program_id(0)
        xv = x_ref[...]
        dh = dh_ref[...]
        rstd = lax.rsqrt(jnp.mean(xv * xv, axis=-1, keepdims=True) + RMS_EPS)
        xhat = xv * rstd
        gv = g_ref[...]
        dn = dh * (1.0 + sc_ref[...])
        dxhat = dn * gv
        dx = dres_ref[...] + rstd * (dxhat - xhat * jnp.mean(dxhat * xhat, axis=-1, keepdims=True))
        dx_ref[...] = dx
        sums = [(dsc_ref, jnp.sum(dh * (xhat * gv), axis=0, keepdims=True)),
                (dsh_ref, jnp.sum(dh, axis=0, keepdims=True)),
                (dg_ref, jnp.sum(dn * xhat, axis=0, keepdims=True))]
        if gated:
            dz_ref[...] = (dx * gv_ref[...]).astype(BF16)
            sums.append((dgv_ref, jnp.sum(dx * f_ref[...], axis=0, keepdims=True)))

        @pl.when(i == 0)
        def _():
            for ref, p in sums:
                ref[...] = p

        @pl.when(i > 0)
        def _():
            for ref, p in sums:
                ref[...] += p

    vec = jax.ShapeDtypeStruct((1, D), F32)
    in_specs = [_row_spec(), _vec_spec(), _vec_spec(), _row_spec(), _row_spec()]
    out_specs = [_row_spec(), _vec_spec(), _vec_spec(), _vec_spec()]
    out_shape = [jax.ShapeDtypeStruct((S, D), F32), vec, vec, vec]
    args = [x, g, sc, dh, dres]
    if gated:
        in_specs += [_row_spec(), _vec_spec()]
        out_specs += [_row_spec(), _vec_spec()]
        out_shape += [jax.ShapeDtypeStruct((S, D), BF16), vec]
        args += list(gate)
    return _pcall(body, name=name, grid=(S // TR,), in_specs=in_specs, out_specs=tuple(out_specs),
                  out_shape=tuple(out_shape), compiler_params=_params(("arbitrary",)))(*args)


def _final_loss(x2, tgt, g, f, g2):
    def body(x_ref, t_ref, g_ref, f_ref, g2_ref, loss_ref, dx_ref, dg_ref, df_ref, dg2_ref):
        i = pl.program_id(0)
        xv = x_ref[...]
        gv = g_ref[...]
        rstd = lax.rsqrt(jnp.mean(xv * xv, axis=-1, keepdims=True) + RMS_EPS)
        xhat = xv * rstd
        err = xhat * gv - t_ref[...]
        dy = err * (1.0 / D)
        dxhat = dy * gv
        dx = rstd * (dxhat - xhat * jnp.mean(dxhat * xhat, axis=-1, keepdims=True))
        dx_ref[...] = dx
        df_ref[...] = (dx * g2_ref[...]).astype(BF16)
        p_g = jnp.sum(dy * xhat, axis=0, keepdims=True)
        p_g2 = jnp.sum(dx * f_ref[...], axis=0, keepdims=True)
        p_l = jnp.zeros((1, 128), F32) + 0.5 * jnp.sum(jnp.mean(err * err, axis=-1, keepdims=True))

        @pl.when(i == 0)
        def _():
            dg_ref[...] = p_g
            dg2_ref[...] = p_g2
            loss_ref[...] = p_l

        @pl.when(i > 0)
        def _():
            dg_ref[...] += p_g
            dg2_ref[...] += p_g2
            loss_ref[...] += p_l

    vec = jax.ShapeDtypeStruct((1, D), F32)
    return _pcall(body, name="final_loss", grid=(S // TR,),
                  in_specs=[_row_spec(), _row_spec(), _vec_spec(), _row_spec(), _vec_spec()],
                  out_specs=(_vec_spec(128), _row_spec(), _vec_spec(), _row_spec(), _vec_spec()),
                  out_shape=(jax.ShapeDtypeStruct((1, 128), F32), jax.ShapeDtypeStruct((S, D), F32), vec,
                             jax.ShapeDtypeStruct((S, D), BF16), vec),
                  compiler_params=_params(("arbitrary",)))(x2, tgt, g, f, g2)


HALF = 512


def _merge_fwd(proj, ret_out, att_out):
    def body(ga_ref, gb_ref, r_ref, a_ref, o_ref):
        o_ref[...] = (jax.nn.sigmoid(ga_ref[...]) * r_ref[...] + jax.nn.sigmoid(gb_ref[...]) * a_ref[...]).astype(BF16)

    blk = lambda off: pl.BlockSpec((TR, HALF), lambda i, j: (i, off // HALF + j))
    return _pcall(body, name="merge_fwd", grid=(S // TR, D // HALF),
                  in_specs=[blk(OFF_GA), blk(OFF_GB), blk(0), blk(0)], out_specs=blk(0),
                  out_shape=jax.ShapeDtypeStruct((S, D), BF16),
                  compiler_params=_params(("parallel", "parallel")))(proj, proj, ret_out, att_out)


def _merge_bwd(proj, ret_out, att_out, dmerged):
    def body(ga_ref, gb_ref, r_ref, a_ref, dm_ref, dr_ref, da_ref, dga_ref, dgb_ref):
        sa = jax.nn.sigmoid(ga_ref[...])
        sb = jax.nn.sigmoid(gb_ref[...])
        dm = dm_ref[...]
        dr_ref[...] = (dm * sa).astype(BF16)
        da_ref[...] = (dm * sb).astype(BF16)
        dga_ref[...] = (dm * r_ref[...] * (sa * (1.0 - sa))).astype(BF16)
        dgb_ref[...] = (dm * a_ref[...] * (sb * (1.0 - sb))).astype(BF16)

    blk = lambda off: pl.BlockSpec((TR, HALF), lambda i, j: (i, off // HALF + j))
    o = jax.ShapeDtypeStruct((S, D), BF16)
    return _pcall(body, name="merge_bwd", grid=(S // TR, D // HALF),
                  in_specs=[blk(OFF_GA), blk(OFF_GB), blk(0), blk(0), blk(0)], out_specs=(blk(0),) * 4,
                  out_shape=(o, o, o, o),
                  compiler_params=_params(("parallel", "parallel")))(proj, proj, ret_out, att_out, dmerged)


def _ret_tables():
    H, C = RET_HEADS, CHUNK
    log_g = jnp.log1p(-(2.0 ** (-5.0 - jnp.arange(H, dtype=F32))))
    idx = jnp.arange(C, dtype=F32)
    rel = idx[:, None] - idx[None, :]
    inner = jnp.where(rel >= 0, jnp.exp(log_g[:, None, None] * jnp.maximum(rel, 0.0)), 0.0)
    qd = jnp.exp(log_g[:, None] * (idx + 1.0))[:, :, None]
    kd = jnp.exp(log_g[:, None] * (C - 1.0 - idx))[:, :, None]
    cd = jnp.broadcast_to(jnp.exp(log_g * C)[:, None, None], (H, 1, 128))
    half = RET_DK // 2
    inv = 10000.0 ** (-jnp.arange(half, dtype=F32) / half)
    ang = jnp.arange(S, dtype=F32)[:, None] * inv[None, :]
    return inner, qd, kd, cd, jnp.cos(ang), jnp.sin(ang)


def _rot(x, cos, sin):
    x1, x2 = x[:, :128], x[:, 128:]
    return jnp.concatenate([x1 * cos - x2 * sin, x1 * sin + x2 * cos], axis=1)


def _rot_t(d, cos, sin):
    d1, d2 = d[:, :128], d[:, 128:]
    return jnp.concatenate([d1 * cos + d2 * sin, d2 * cos - d1 * sin], axis=1)


RET_COLS = OFF_ATT
RET_VW = RET_HEADS * RET_DV


def _ret_specs(chunk_of):
    ci = chunk_of
    whole = lambda shape: pl.BlockSpec(shape, lambda t: (0,) * len(shape))
    return [
        pl.BlockSpec((CHUNK, RET_COLS), lambda t: (ci(t), 0)),
        pl.BlockSpec((CHUNK, 128), lambda t: (ci(t), 0)),
        pl.BlockSpec((CHUNK, 128), lambda t: (ci(t), 0)),
        whole((RET_HEADS, CHUNK, CHUNK)), whole((RET_HEADS, CHUNK, 1)), whole((RET_HEADS, CHUNK, 1)),
        whole((RET_HEADS, 1, 128)), whole((1, RET_VW)), whole((1, RET_VW)),
    ]


def _ret_cols(h):
    q = slice(OFF_RQ + h * RET_DK, OFF_RQ + (h + 1) * RET_DK)
    k = slice(OFF_RK + h * RET_DK, OFF_RK + (h + 1) * RET_DK)
    v = slice(OFF_RV + h * RET_DV, OFF_RV + (h + 1) * RET_DV)
    g = slice(OFF_RG + h * RET_DV, OFF_RG + (h + 1) * RET_DV)
    return q, k, v, g, slice(h * RET_DV, (h + 1) * RET_DV)


def _ret_fwd(proj, tables, gn_g, gn_b, comm=None):
    inner, qd, kd, cd, cos, sin = tables

    def body(x_ref, cos_ref, sin_ref, in_ref, qd_ref, kd_ref, cd_ref, g_ref, b_ref,
             gated_ref, ro_ref, st_ref, s_scr):
        i = pl.program_id(0)

        @pl.when(i == 0)
        def _():
            s_scr[...] = jnp.zeros_like(s_scr)

        cosv, sinv = cos_ref[...], sin_ref[...]
        for h in range(RET_HEADS):
            cq, ck, cv, cg, co = _ret_cols(h)
            q = _rot(x_ref[:, cq], cosv, sinv)
            k = _rot(x_ref[:, ck], cosv, sinv) * (RET_DK ** -0.5)
            v = x_ref[:, cv]
            st = s_scr[h]
            st_ref[h] = st
            s = _dot(q, k, NT) * in_ref[h]
            o = _dot(s, v, NN) + _dot(q, st, NN) * qd_ref[h]
            s_scr[h] = st * cd_ref[h, :, :1] + _dot(k * kd_ref[h], v, TN)
            ro_ref[:, co] = o
            mu = jnp.mean(o, axis=-1, keepdims=True)
            oc = o - mu
            var = jnp.mean(oc * oc, axis=-1, keepdims=True)
            rn = oc * lax.rsqrt(var + GN_EPS) * g_ref[:, co] + b_ref[:, co]
            rg = x_ref[:, cg]
            gated_ref[:, co] = (rg * jax.nn.sigmoid(rg) * rn).astype(BF16)

    ospec = pl.BlockSpec((CHUNK, RET_VW), lambda t: (t, 0))
    kw = dict(name="ret_fwd", grid=(N_CHUNK,), in_specs=_ret_specs(lambda t: t),
              out_specs=(ospec, ospec, pl.BlockSpec((RET_HEADS, None, RET_DK, RET_DV), lambda t: (0, t, 0, 0))),
              out_shape=(jax.ShapeDtypeStruct((S, RET_VW), BF16), jax.ShapeDtypeStruct((S, RET_VW), F32),
                         jax.ShapeDtypeStruct((RET_HEADS, N_CHUNK, RET_DK, RET_DV), F32)),
              scratch_shapes=[pltpu.VMEM((RET_HEADS, RET_DK, RET_DV), F32)])
    args = (proj, cos, sin, inner, qd, kd, cd, gn_g, gn_b)
    if comm is not None:
        return _carry(body, comm, **kw)(*args)
    return _pcall(body, compiler_params=_params(("arbitrary",)), **kw)(*args)


def _ret_bwd(proj, tables, gn_g, gn_b, ro, states, dgated, comm=None):
    inner, qd, kd, cd, cos, sin = tables
    last = N_CHUNK - 1

    def body(x_ref, cos_ref, sin_ref, in_ref, qd_ref, kd_ref, cd_ref, g_ref, b_ref, ro_ref, st_ref, dg_ref,
             dx_ref, gg_ref, gb_ref, gs_scr):
        t = pl.program_id(0)

        @pl.when(t == 0)
        def _():
            gs_scr[...] = jnp.zeros_like(gs_scr)
            gg_ref[...] = jnp.zeros_like(gg_ref)
            gb_ref[...] = jnp.zeros_like(gb_ref)

        cosv, sinv = cos_ref[...], sin_ref[...]
        for h in range(RET_HEADS):
            cq, ck, cv, cg, co = _ret_cols(h)
            q = _rot(x_ref[:, cq], cosv, sinv)
            k = _rot(x_ref[:, ck], cosv, sinv) * (RET_DK ** -0.5)
            v = x_ref[:, cv]
            qdv, kdv, dm = qd_ref[h], kd_ref[h], in_ref[h]
            st = st_ref[h]
            o = ro_ref[:, co]
            gv = g_ref[:, co]
            mu = jnp.mean(o, axis=-1, keepdims=True)
            oc = o - mu
            rstd = lax.rsqrt(jnp.mean(oc * oc, axis=-1, keepdims=True) + GN_EPS)
            ohat = oc * rstd
            rn = ohat * gv + b_ref[:, co]
            rg = x_ref[:, cg]
            sg = jax.nn.sigmoid(rg)
            dgt = dg_ref[:, co]
            drn = dgt * (rg * sg)
            dx_ref[:, cg] = (dgt * rn * (sg * (1.0 + rg * (1.0 - sg)))).astype(BF16)
            gg_ref[:, co] += jnp.sum(drn * ohat, axis=0, keepdims=True)
            gb_ref[:, co] += jnp.sum(drn, axis=0, keepdims=True)
            dohat = drn * gv
            do = rstd * (dohat - jnp.mean(dohat, axis=-1, keepdims=True)
                         - ohat * jnp.mean(dohat * ohat, axis=-1, keepdims=True))
            gs = gs_scr[h]
            s = _dot(q, k, NT) * dm
            dsr = _dot(do, v, NT) * dm
            dq = _dot(dsr, k, NN) + _dot(do, st, NT) * qdv
            dk = _dot(dsr, q, TN) + _dot(v, gs, NT) * kdv
            dv = _dot(s, do, TN) + _dot(k * kdv, gs, NN)
            gs_scr[h] = gs * cd_ref[h, :, :1] + _dot(q * qdv, do, TN)
            dx_ref[:, cq] = _rot_t(dq, cosv, sinv).astype(BF16)
            dx_ref[:, ck] = (_rot_t(dk, cosv, sinv) * (RET_DK ** -0.5)).astype(BF16)
            dx_ref[:, cv] = dv.astype(BF16)

    rev = lambda t: last - t
    vblk = pl.BlockSpec((CHUNK, RET_VW), lambda t: (rev(t), 0))
    vspec = pl.BlockSpec((1, RET_VW), lambda t: (0, 0))
    kw = dict(name="ret_bwd", grid=(N_CHUNK,),
              in_specs=_ret_specs(rev) + [vblk, pl.BlockSpec((RET_HEADS, None, RET_DK, RET_DV),
                                                             lambda t: (0, rev(t), 0, 0)), vblk],
              out_specs=(pl.BlockSpec((CHUNK, RET_COLS), lambda t: (rev(t), 0)), vspec, vspec),
              out_shape=(jax.ShapeDtypeStruct((S, RET_COLS), BF16), jax.ShapeDtypeStruct((1, RET_VW), F32),
                         jax.ShapeDtypeStruct((1, RET_VW), F32)),
              scratch_shapes=[pltpu.VMEM((RET_HEADS, RET_DK, RET_DV), F32)])
    args = (proj, cos, sin, inner, qd, kd, cd, gn_g, gn_b, ro, states, dgated)
    if comm is not None:
        return _carry(body, comm, **kw)(*args)
    return _pcall(body, compiler_params=_params(("arbitrary",)), **kw)(*args)


def _bucket_tables():
    qi = np.arange(ATT_BLK)[:, None]
    kj = np.arange(2 * ATT_BLK)[None, :]
    m = ATT_BLK + qi - kj
    out = []
    for win, dil in ATT_GROUPS:
        w = win // dil
        dist = (np.clip(m, 0, w) * dil).astype(np.int32)
        max_exact = N_BUCKETS // 2
        d_f = np.maximum(dist, 1).astype(np.float32)
        large = max_exact + (np.log(d_f / np.float32(max_exact)) / np.float32(math.log(MAX_DIST / max_exact))
                             * np.float32(N_BUCKETS - max_exact)).astype(np.int32)
        large = np.minimum(large, N_BUCKETS - 1)
        out.append(np.where(dist < max_exact, dist, large).astype(np.int32))
    return np.stack(out)


def _bias_build(rel_bias, buckets):
    def body(tab_ref, bk_ref, o_ref):
        hh = pl.program_id(0)
        bk = bk_ref[...]
        acc = jnp.zeros((ATT_BLK, 2 * ATT_BLK), F32)
        for b in range(N_BUCKETS):
            acc = jnp.where(bk == b, tab_ref[b, hh], acc)
        o_ref[...] = acc

    nh = len(ATT_GROUPS) * ATT_HG
    return _pcall(body, name="bias_build", grid=(nh,),
                  in_specs=[pl.BlockSpec(memory_space=pltpu.SMEM),
                            pl.BlockSpec((None, ATT_BLK, 2 * ATT_BLK), lambda hh: (hh // ATT_HG, 0, 0))],
                  out_specs=pl.BlockSpec((None, ATT_BLK, 2 * ATT_BLK), lambda hh: (hh, 0, 0)),
                  out_shape=jax.ShapeDtypeStruct((nh, ATT_BLK, 2 * ATT_BLK), F32),
                  compiler_params=_params(("parallel",)))(rel_bias, buckets)


def _bias_grad(ds_sum, buckets):
    def body(ds_ref, bk_ref, o_ref):
        bk = bk_ref[...]
        ds = ds_ref[...]
        rows = lax.broadcasted_iota(jnp.int32, (N_BUCKETS, 128), 0)
        acc = jnp.zeros((N_BUCKETS, 128), F32)
        for b in range(N_BUCKETS):
            acc = jnp.where(rows == b, jnp.sum(jnp.where(bk == b, ds, 0.0)), acc)
        o_ref[...] = acc

    nh = len(ATT_GROUPS) * ATT_HG
    return _pcall(body, name="bias_grad", grid=(nh,),
                  in_specs=[pl.BlockSpec((None, ATT_BLK, 2 * ATT_BLK), lambda hh: (hh, 0, 0)),
                            pl.BlockSpec((None, ATT_BLK, 2 * ATT_BLK), lambda hh: (hh // ATT_HG, 0, 0))],
                  out_specs=pl.BlockSpec((None, N_BUCKETS, 128), lambda hh: (hh, 0, 0)),
                  out_shape=jax.ShapeDtypeStruct((nh, N_BUCKETS, 128), F32),
                  compiler_params=_params(("parallel",)))(ds_sum, buckets)


def _att_valid(n):
    qi = lax.broadcasted_iota(jnp.int32, (ATT_BLK, 2 * ATT_BLK), 0)
    kj = lax.broadcasted_iota(jnp.int32, (ATT_BLK, 2 * ATT_BLK), 1)
    m = ATT_BLK + qi - kj
    first_key = jnp.where(n > 0, 0, ATT_BLK)
    return (m >= 0) & (m <= ATT_BLK) & (kj >= first_key)


ATT_HP = (1, 2, 2)


def _att_geometry(gi):
    _, dil = ATT_GROUPS[gi]
    return dil, S // dil // ATT_BLK, ATT_HP[gi]


def _blk(dil, r, n):
    if dil == 1:
        return pl.ds(n * ATT_BLK, ATT_BLK)
    return pl.ds(r + n * ATT_BLK * dil, ATT_BLK, stride=dil)


def _slab_specs(gi):
    _, _, hp = _att_geometry(gi)
    per = ATT_HG // hp
    return [pl.BlockSpec((hp, S, ATT_DH), lambda g, r, part=part: ((3 * gi + part) * per + g, 0, 0))
            for part in range(3)]


def _head_specs(gi, count):
    _, _, hp = _att_geometry(gi)
    return [pl.BlockSpec((hp, S, ATT_DH), lambda g, r: (g, 0, 0))] * count


def _bias_spec(gi):
    _, _, hp = _att_geometry(gi)
    return pl.BlockSpec((hp, ATT_BLK, 2 * ATT_BLK), lambda g, r: (gi * (ATT_HG // hp) + g, 0, 0))


def _att_fwd(slabs, bias, gi):
    dil, nb, hp = _att_geometry(gi)
    scale = ATT_DH ** -0.5

    def body(q_ref, k_ref, v_ref, bias_ref, o_ref, l_ref):
        r = pl.program_id(1)
        for n in range(nb):
            valid = _att_valid(n)
            prev = _blk(dil, r, max(n - 1, 0))
            cur = _blk(dil, r, n)
            for h in range(hp):
                kk = jnp.concatenate([k_ref[h, prev, :], k_ref[h, cur, :]], axis=0)
                vv = jnp.concatenate([v_ref[h, prev, :], v_ref[h, cur, :]], axis=0)
                s = _dot(q_ref[h, cur, :], kk, NT) * scale + bias_ref[h]
                s = jnp.where(valid, s, -1e30)
                mx = jnp.max(s, axis=-1, keepdims=True)
                e = jnp.exp(s - mx)
                den = jnp.sum(e, axis=-1, keepdims=True)
                o_ref[h, cur, :] = _dot(e / den, vv, NN)
                l_ref[h, cur, :] = jnp.broadcast_to(mx + jnp.log(den), (ATT_BLK, ATT_DH))

    osh = jax.ShapeDtypeStruct((ATT_HG, S, ATT_DH), F32)
    return _pcall(body, name=f"att_fwd{gi}", grid=(ATT_HG // hp, dil), in_specs=_slab_specs(gi) + [_bias_spec(gi)],
                  out_specs=tuple(_head_specs(gi, 2)), out_shape=(osh, osh),
                  compiler_params=_params(("parallel", "arbitrary")))(slabs, slabs, slabs, bias)


def _att_bwd(slabs, bias, o, lse, do, dlse, gi, comm=None):
    dil, nb, hp = _att_geometry(gi)
    per = ATT_HG // hp
    scale = ATT_DH ** -0.5
    wh = hp * ATT_DH
    wide = lambda t: jnp.concatenate([t, t], axis=1)

    def body(q_ref, k_ref, v_ref, bias_ref, o_ref, l_ref, do_ref, dl_ref, dq_ref, dk_ref, dv_ref, ds_ref):
        r = pl.program_id(1)

        @pl.when(r == 0)
        def _():
            ds_ref[...] = jnp.zeros_like(ds_ref)

        for h in range(hp):
            sl = slice(h * ATT_DH, (h + 1) * ATT_DH)
            carry_k = carry_v = None
            for n in range(nb):
                valid = _att_valid(n)
                prev = _blk(dil, r, max(n - 1, 0))
                cur = _blk(dil, r, n)
                q = q_ref[h, cur, :]
                kk = jnp.concatenate([k_ref[h, prev, :], k_ref[h, cur, :]], axis=0)
                vv = jnp.concatenate([v_ref[h, prev, :], v_ref[h, cur, :]], axis=0)
                dov = do_ref[h, cur, :]
                s = _dot(q, kk, NT) * scale + bias_ref[h]
                p = jnp.where(valid, jnp.exp(s - wide(l_ref[h, cur, :])), 0.0)
                dp = _dot(dov, vv, NT)
                delta = jnp.sum(dov * o_ref[h, cur, :], axis=-1, keepdims=True)
                ds = p * (dp - delta + wide(dl_ref[h, cur, :]))
                ds_ref[h] += ds
                out_rows = pl.ds(n * ATT_BLK, ATT_BLK)
                dq_ref[out_rows, sl] = (_dot(ds, kk, NN) * scale).astype(BF16)
                dkk = _dot(ds, q, TN) * scale
                dvv = _dot(p, dov, TN)
                if n > 0:
                    before = pl.ds((n - 1) * ATT_BLK, ATT_BLK)
                    dk_ref[before, sl] = (carry_k + dkk[:ATT_BLK]).astype(BF16)
                    dv_ref[before, sl] = (carry_v + dvv[:ATT_BLK]).astype(BF16)
                carry_k, carry_v = dkk[ATT_BLK:], dvv[ATT_BLK:]
            last = pl.ds((nb - 1) * ATT_BLK, ATT_BLK)
            dk_ref[last, sl] = carry_k.astype(BF16)
            dv_ref[last, sl] = carry_v.astype(BF16)

    out_spec = pl.BlockSpec((S // dil, wh), lambda g, r: (0, r * per + g))
    osh = jax.ShapeDtypeStruct((S // dil, dil * AW), BF16)
    kw = dict(name=f"att_bwd{gi}", grid=(per, dil), in_specs=_slab_specs(gi) + [_bias_spec(gi)] + _head_specs(gi, 4),
              out_specs=(out_spec, out_spec, out_spec,
                         pl.BlockSpec((hp, ATT_BLK, 2 * ATT_BLK), lambda g, r: (g, 0, 0))),
              out_shape=(osh, osh, osh, jax.ShapeDtypeStruct((ATT_HG, ATT_BLK, 2 * ATT_BLK), F32)))
    args = (slabs, slabs, slabs, bias, o, lse, do, dlse)
    if comm is not None:
        return _carry(body, comm, **kw)(*args)
    return _pcall(body, compiler_params=_params(("arbitrary", "arbitrary")), **kw)(*args)


AW = ATT_HG * ATT_DH


def _mix_weights(l0, l1, l2):
    mx = jnp.maximum(jnp.maximum(l0, l1), l2)
    e0, e1, e2 = jnp.exp(l0 - mx), jnp.exp(l1 - mx), jnp.exp(l2 - mx)
    den = e0 + e1 + e2
    return e0 / den, e1 / den, e2 / den


def _heads_spec():
    return pl.BlockSpec((ATT_HG, TR, ATT_DH), lambda i: (0, i, 0))


def _mix_fwd(os_, ls):
    def body(o0, o1, o2, l0, l1, l2, att_ref):
        for h in range(ATT_HG):
            w0, w1, w2 = _mix_weights(l0[h], l1[h], l2[h])
            att_ref[:, h * ATT_DH:(h + 1) * ATT_DH] = (w0 * o0[h] + w1 * o1[h] + w2 * o2[h]).astype(BF16)

    return _pcall(body, name="mix_fwd", grid=(S // TR,), in_specs=[_heads_spec()] * 6, out_specs=_row_spec(AW),
                  out_shape=jax.ShapeDtypeStruct((S, AW), BF16), compiler_params=_params(("parallel",)))(*os_, *ls)


def _mix_bwd(os_, ls, datt):
    def body(o0, o1, o2, l0, l1, l2, da_ref, d0, d1, d2, e0, e1, e2):
        for h in range(ATT_HG):
            ws = _mix_weights(l0[h], l1[h], l2[h])
            da = da_ref[:, h * ATT_DH:(h + 1) * ATT_DH]
            dws = []
            for o_ref, w, d_ref in zip((o0, o1, o2), ws, (d0, d1, d2)):
                d_ref[h] = w * da
                dws.append(jnp.broadcast_to(jnp.sum(da * o_ref[h], axis=-1, keepdims=True), (TR, ATT_DH)))
            tot = ws[0] * dws[0] + ws[1] * dws[1] + ws[2] * dws[2]
            for w, dw, e_ref in zip(ws, dws, (e0, e1, e2)):
                e_ref[h] = w * (dw - tot)

    o = jax.ShapeDtypeStruct((ATT_HG, S, ATT_DH), F32)
    return _pcall(body, name="mix_bwd", grid=(S // TR,), in_specs=[_heads_spec()] * 6 + [_row_spec(AW)],
                  out_specs=(_heads_spec(),) * 6, out_shape=(o,) * 6,
                  compiler_params=_params(("parallel",)))(*os_, *ls, datt)


def _ada_fwd(c_all, w_sh, b_sl):
    def body(c_ref, w_ref, b_ref, o_ref):
        cv = c_ref[...]
        o_ref[...] = _dot(cv * jax.nn.sigmoid(cv), w_ref[...], NN) + b_ref[...]

    return _pcall(body, name="ada_fwd", out_shape=jax.ShapeDtypeStruct((N_DEV, w_sh.shape[1]), F32),
                  compiler_params=_params())(c_all, w_sh, b_sl)


def _ada_bwd(c_all, dm_sl):
    def body(c_ref, d_ref, o_ref):
        cv = c_ref[...]
        o_ref[...] = _dot(cv * jax.nn.sigmoid(cv), d_ref[...], TN)

    return _pcall(body, name="ada_bwd", out_shape=jax.ShapeDtypeStruct((D, dm_sl.shape[1]), F32),
                  compiler_params=_params())(c_all, dm_sl)


N_MOD = 6


def _sum_small(gathered):
    n = len(gathered)

    def body(*refs):
        ins, (gb_ref, dm_ref), outs = refs[:n], refs[n:n + 2], refs[n + 2:]

        def total(r):
            acc = r[0]
            for e in range(1, N_DEV):
                acc = acc + r[e]
            return acc

        for i in range(N_MOD):
            cols = slice(i * D, (i + 1) * D)
            gb_ref[:, cols] = total(ins[i])
            for e in range(N_DEV):
                dm_ref[e:e + 1, cols] = ins[i][e]
        for r, o_ref in zip(ins[N_MOD:], outs):
            o_ref[...] = total(r)

    shapes = (jax.ShapeDtypeStruct((1, N_MOD * D), F32), jax.ShapeDtypeStruct((N_DEV, N_MOD * D), F32),
              *[jax.ShapeDtypeStruct(g.shape[1:], F32) for g in gathered[N_MOD:]])
    res = _pcall(body, name="sum_small", out_shape=shapes, compiler_params=_params())(*gathered)
    return res[0], res[1], res[2:]


def _row_tile(m, n):
    t = max(8, min(m, (1 << 19) // n // 8 * 8))
    while m % t:
        t -= 8
    return t


def _pair_sum(full, recv, sel, name):
    _, _, m, n = full.shape
    t = _row_tile(m, n)

    def body(sel_ref, a_ref, b_ref, o_ref):
        o_ref[...] = (a_ref[...].astype(F32) + b_ref[...].astype(F32)).astype(o_ref.dtype)

    gs = pltpu.PrefetchScalarGridSpec(
        num_scalar_prefetch=1, grid=(4, m // t),
        in_specs=[pl.BlockSpec((None, None, t, n), lambda q, i, s: (q, s[0], i, 0)),
                  pl.BlockSpec((None, t, n), lambda q, i, s: (q, i, 0))],
        out_specs=pl.BlockSpec((None, t, n), lambda q, i, s: (q, i, 0)))
    return _pcall(body, name=name, grid_spec=gs, out_shape=jax.ShapeDtypeStruct((4, m, n), full.dtype),
                  compiler_params=_params(("parallel", "parallel")))(sel, full, recv)


def _chip_sum(part, recv, sel, name):
    _, m, n = part.shape
    t = _row_tile(m, n)

    def body(sel_ref, a_ref, r_ref, o_ref):
        o_ref[...] = ((a_ref[...].astype(F32) + r_ref[0].astype(F32)) + r_ref[1].astype(F32)) + r_ref[2].astype(F32)

    gs = pltpu.PrefetchScalarGridSpec(
        num_scalar_prefetch=1, grid=(m // t,),
        in_specs=[pl.BlockSpec((None, t, n), lambda i, s: (s[0], i, 0)),
                  pl.BlockSpec((3, t, n), lambda i, s: (0, i, 0))],
        out_specs=pl.BlockSpec((t, n), lambda i, s: (i, 0)))
    return _pcall(body, name=name, grid_spec=gs, out_shape=jax.ShapeDtypeStruct((m, n), F32),
                  compiler_params=_params(("parallel",)))(sel, part, recv)


def _adamw_math(w, g, m, v):
    nm = ADAM_B1 * m + (1.0 - ADAM_B1) * g
    nv = ADAM_B2 * v + (1.0 - ADAM_B2) * (g * g)
    m_hat = nm / (1.0 - ADAM_B1 ** ADAM_STEP)
    v_hat = nv / (1.0 - ADAM_B2 ** ADAM_STEP)
    return -ADAM_LR * (m_hat / (jnp.sqrt(v_hat) + ADAM_EPS) + ADAM_WD * w), nm, nv


def _adamw(w, g, m, v, name):
    _, rows, cols = w.shape
    t = _row_tile(rows, cols)

    def body(w_ref, g_ref, m_ref, v_ref, d_ref, nm_ref, nv_ref):
        d_ref[...], nm_ref[...], nv_ref[...] = _adamw_math(w_ref[...], g_ref[...], m_ref[...], v_ref[...])

    spec3 = pl.BlockSpec((None, t, cols), lambda i: (0, i, 0))
    spec2 = pl.BlockSpec((t, cols), lambda i: (i, 0))
    o = jax.ShapeDtypeStruct(w.shape, F32)
    return _pcall(body, name=name, grid=(rows // t,), in_specs=[spec3, spec2, spec3, spec3], out_specs=(spec3,) * 3,
                  out_shape=(o, o, o), compiler_params=_params(("parallel",)))(w, g, m, v)


def _adamw_small(ws, gs, ms, vs):
    n = len(ws)

    def body(*refs):
        for i in range(n):
            w_ref, g_ref, m_ref, v_ref = (refs[k * n + i] for k in range(4))
            d, nm, nv = _adamw_math(w_ref[...], g_ref[...], m_ref[...], v_ref[...])
            refs[4 * n + i][...] = d
            refs[5 * n + i][...] = nm
            refs[6 * n + i][...] = nv

    shapes = tuple(jax.ShapeDtypeStruct(w.shape, F32) for w in ws)
    res = _pcall(body, name="adamw_small", out_shape=shapes * 3, compiler_params=_params())(*ws, *gs, *ms, *vs)
    return res[:n], res[n:2 * n], res[2 * n:]


def _mesh_pos():
    return lax.axis_index("x"), lax.axis_index("y"), lax.axis_index("c")


class _Gather:
    def __init__(self, arrs):
        self.ins = list(arrs)
        na = self.na = len(arrs)
        self.out_shape = tuple(jax.ShapeDtypeStruct((N_DEV,) + a.shape, a.dtype) for a in arrs)
        self.sems = [pltpu.SemaphoreType.DMA((7 * na,)), pltpu.SemaphoreType.DMA((7 * na,)),
                     pltpu.SemaphoreType.DMA((na,))]

    def _copies(self, ins, outs, sems):
        send_sems, recv_sems, local_sems = sems
        x, y, c = _mesh_pos()
        me, sibling = (x, y, c), (x, y, 1 - c)
        chips = [(1 - x, y), (x, 1 - y), (1 - x, 1 - y)]

        def slot(p):
            return 4 * p[0] + 2 * p[1] + p[2]

        def copy(a, k, block, to, src=None):
            dst = outs[a].at[slot(block)]
            return pltpu.make_async_remote_copy(
                src_ref=dst if src is None else src, dst_ref=dst, send_sem=send_sems.at[7 * a + k],
                recv_sem=recv_sems.at[7 * a + k], device_id=to, device_id_type=MESH)

        mine = [pltpu.make_async_copy(ins[a], outs[a].at[slot(me)], local_sems.at[a]) for a in range(self.na)]
        first = []
        for a in range(self.na):
            first.append(copy(a, 0, me, sibling, src=ins[a]))
            first += [copy(a, 1 + j, me, (*chip, c), src=ins[a]) for j, chip in enumerate(chips)]
        return me, sibling, chips, c, copy, mine, first

    def start(self, ins, outs, sems):
        *_, mine, first = self._copies(ins, outs, sems)
        for cp in mine + first:
            cp.start()

    def finish(self, ins, outs, sems):
        me, sibling, chips, c, copy, mine, first = self._copies(ins, outs, sems)
        passed = []
        for j, chip in enumerate(chips):
            for a in range(self.na):
                copy(a, 1 + j, (*chip, c), me).wait_recv()
                cp = copy(a, 4 + j, (*chip, c), sibling)
                cp.start()
                passed.append(cp)
        for a in range(self.na):
            copy(a, 0, sibling, me).wait_recv()
            for j, chip in enumerate(chips):
                copy(a, 4 + j, (*chip, 1 - c), me).wait_recv()
        for cp in first + passed:
            cp.wait_send()
        for cp in mine:
            cp.wait()


class _ExchangeCore:
    def __init__(self, fulls):
        self.ins = list(fulls)
        self.out_shape = tuple(jax.ShapeDtypeStruct((4,) + f.shape[2:], f.dtype) for f in fulls)
        self.sems = [pltpu.SemaphoreType.DMA((4 * len(fulls),)), pltpu.SemaphoreType.DMA((4 * len(fulls),))]

    def _copies(self, ins, outs, sems):
        send_sems, recv_sems = sems
        x, y, c = _mesh_pos()
        return [pltpu.make_async_remote_copy(
            src_ref=ins[a].at[q, 1 - c], dst_ref=outs[a].at[q], send_sem=send_sems.at[4 * a + q],
            recv_sem=recv_sems.at[4 * a + q], device_id=(x, y, 1 - c), device_id_type=MESH)
            for a in range(len(self.ins)) for q in range(4)]

    def start(self, ins, outs, sems):
        for cp in self._copies(ins, outs, sems):
            cp.start()

    def finish(self, ins, outs, sems):
        for cp in self._copies(ins, outs, sems):
            cp.wait()


class _ExchangeChip:
    def __init__(self, parts):
        self.ins = list(parts)
        self.out_shape = tuple(jax.ShapeDtypeStruct((3,) + p.shape[1:], p.dtype) for p in parts)
        self.sems = [pltpu.SemaphoreType.DMA((3 * len(parts),)), pltpu.SemaphoreType.DMA((3 * len(parts),))]

    def _copies(self, ins, outs, sems):
        send_sems, recv_sems = sems
        x, y, c = _mesh_pos()
        chips = [(1 - x, y), (x, 1 - y), (1 - x, 1 - y)]
        return [pltpu.make_async_remote_copy(
            src_ref=ins[a].at[2 * px + py], dst_ref=outs[a].at[j], send_sem=send_sems.at[3 * a + j],
            recv_sem=recv_sems.at[3 * a + j], device_id=(px, py, c), device_id_type=MESH)
            for a in range(len(self.ins)) for j, (px, py) in enumerate(chips)]

    def start(self, ins, outs, sems):
        for cp in self._copies(ins, outs, sems):
            cp.start()

    def finish(self, ins, outs, sems):
        for cp in self._copies(ins, outs, sems):
            cp.wait()


def _reduce_sums(fulls, recv_core, core, tag):
    return [_pair_sum(f, r, core, f"rs_pair_{tag}{i}") for i, (f, r) in enumerate(zip(fulls, recv_core))]


def _local_step(x, tgt, mods, w_in, shards, small, chip, core):
    sh1, sc1, g1, sh2, sc2, g2 = mods
    norm1_g, rel_bias, gn_g, gn_b, norm2_g, norm_f_g = small
    tables = _ret_tables()
    buckets = jnp.asarray(_bucket_tables())

    h1 = _norm_mod_fwd(x, norm1_g, sh1, sc1, "norm1_fwd")
    proj, slabs, gathered = _proj(h1, w_in, _Gather(shards[:3]))
    w_ret_out, w_att_out, w_o = (_from_slots(g, ax) for g, ax in zip(gathered, BIG_AXES[1:4]))
    (gated, ro, states), gathered = _ret_fwd(proj, tables, gn_g, gn_b, comm=_Gather(shards[3:]))
    w_ff1, w_ff2 = (_from_slots(g, ax) for g, ax in zip(gathered, BIG_AXES[4:]))
    bias = _bias_build(rel_bias, buckets)
    outs, lses = [], []
    for gi in range(len(ATT_GROUPS)):
        o, l = _att_fwd(slabs, bias, gi)
        outs.append(o)
        lses.append(l)
    att = _mix_fwd(outs, lses)
    ret_out = _mm(gated, w_ret_out, 'nn', tm=S, tn=256, tk=2048, name="ret_out")
    att_out = _mm(att, w_att_out, 'nn', tm=S, tn=512, tk=AW, name="att_out")
    merged = _merge_fwd(proj, ret_out, att_out)
    mixo, x1 = _mm(merged, w_o, 'nn', tm=S, tn=256, tk=D, name="w_o", res=x, gvec=g1)
    h2 = _norm_mod_fwd(x1, norm2_g, sh2, sc2, "norm2_fwd")
    u, act = _mm(h2, w_ff1, 'nn', tm=S, tn=512, tk=D, name="ff1", relu2=True)
    f, x2 = _mm(act, w_ff2, 'nn', tm=1024, tn=512, tk=2048, name="ff2", res=x1, gvec=g2)
    loss, dx2, g_normf, df, dg2 = _final_loss(x2, tgt, norm_f_g, f, g2)

    gw_ff2 = _mm(act, df, 'tn', tm=512, tn=D, tk=S, name="gw_ff2", out_dtype=BF16)
    du = _mm(df, w_ff2, 'nt', tm=S, tn=512, tk=D, name="d_act", out_dtype=BF16, relu2_of=u)
    gw_ff1 = _mm(h2, du, 'tn', tm=D, tn=512, tk=S, name="gw_ff1", out_dtype=BF16)
    fulls_a = [_to_slots(g, ax) for g, ax in zip((gw_ff1, gw_ff2), BIG_AXES[4:])]
    dh2, recv_core_a = _mm(du, w_ff1, 'nt', tm=1024, tn=1024, tk=1024, name="dh2", comm=_ExchangeCore(fulls_a))
    parts_a = _reduce_sums(fulls_a, recv_core_a, core, "a")
    dx1, dsc2, dsh2, g_norm2, dmixo, dg1 = _norm_mod_bwd(x1, norm2_g, sc2, dh2, dx2, "norm2_bwd", gate=(mixo, g1))

    gw_o = _mm(merged, dmixo, 'tn', tm=D, tn=512, tk=S, name="gw_o", out_dtype=BF16)
    dmerged = _mm(dmixo, w_o, 'nt', tm=S, tn=512, tk=D, name="dmerged")
    d_ret_out, d_att_out, dga, dgb = _merge_bwd(proj, ret_out, att_out, dmerged)
    gw_ret_out = _mm(gated, d_ret_out, 'tn', tm=512, tn=D, tk=S, name="gw_ret_out", out_dtype=BF16)
    gw_att_out = _mm(att, d_att_out, 'tn', tm=AW, tn=D, tk=S, name="gw_att_out", out_dtype=BF16)
    fulls_b = [_to_slots(g, ax) for g, ax in zip((gw_ret_out, gw_att_out, gw_o), BIG_AXES[1:4])]
    dgated, recv_core_b = _mm(d_ret_out, w_ret_out, 'nt', tm=S, tn=512, tk=D, name="dgated",
                              comm=_ExchangeCore(fulls_b))
    parts_b = _reduce_sums(fulls_b, recv_core_b, core, "b")
    datt = _mm(d_att_out, w_att_out, 'nt', tm=S, tn=AW, tk=D, name="datt")
    mix_grads = _mix_bwd(outs, lses, datt)
    datt_parts, ds_sums = [], []
    for gi in range(len(ATT_GROUPS)):
        res = _att_bwd(slabs, bias, outs[gi], lses[gi], mix_grads[gi], mix_grads[3 + gi], gi,
                       comm=_ExchangeChip(parts_b) if gi == 1 else None)
        if gi == 1:
            res, recv_chip_b = res
        dq, dk, dv, ds_sum = res
        datt_parts += [dq.reshape(S, AW), dk.reshape(S, AW), dv.reshape(S, AW)]
        ds_sums.append(ds_sum)
    red_b = [_chip_sum(p, r, chip, f"rs_sum_b{i}") for i, (p, r) in enumerate(zip(parts_b, recv_chip_b))]
    g_bias = _bias_grad(jnp.concatenate(ds_sums, axis=0), buckets)[:, :, 0].T.reshape(1, -1)
    (dret, g_gn_g, g_gn_b), recv_chip_a = _ret_bwd(proj, tables, gn_g, gn_b, ro, states, dgated,
                                                   comm=_ExchangeChip(parts_a))
    red_a = [_chip_sum(p, r, chip, f"rs_sum_a{i}") for i, (p, r) in enumerate(zip(parts_a, recv_chip_a))]
    dproj = jnp.concatenate([dret] + datt_parts + [dga, dgb], axis=1)
    gw_in = _mm(h1, dproj, 'tn', tm=D, tn=512, tk=S, name="gw_in", out_dtype=BF16)
    full_in = [_to_slots(gw_in, BIG_AXES[0])]
    dh1, recv_core_in = _mm(dproj, w_in, 'nt', tm=1024, tn=1024, tk=512, name="dh1", comm=_ExchangeCore(full_in))
    part_in = _reduce_sums(full_in, recv_core_in, core, "c")
    gx, dsc1, dsh1, g_norm1 = _norm_mod_bwd(x, norm1_g, sc1, dh1, dx1, "norm1_bwd")
    recv_chip_in = _run_comm(_ExchangeChip(part_in), "rs_chip_in")
    red_in = _chip_sum(part_in[0], recv_chip_in[0], chip, "rs_sum_c")

    dmod = [dsh1, dsc1, dg1, dsh2, dsc2, dg2]
    small_g = [g_norm1, g_bias, g_gn_g, g_gn_b, g_norm2, g_normf]
    return loss, gx, [red_in] + red_b + red_a, small_g, dmod


def _to_slots(g, axis):
    if axis == 0:
        return g.reshape(4, 2, g.shape[0] // N_DEV, g.shape[1])
    return g.reshape(g.shape[0], N_DEV, g.shape[1] // N_DEV).transpose(1, 0, 2).reshape(4, 2, g.shape[0], -1)


def _from_slots(w8, axis):
    if axis == 0:
        return w8.reshape(-1, w8.shape[2])
    return w8.transpose(1, 0, 2).reshape(w8.shape[1], -1)


BIG_AXES = (1, 0, 1, 0, 1, 0)


def kernel(x, c, w_ada, b_ada, norm1_g, w_in, rel_bias, ret_gn_g, ret_gn_b, w_ret_out, w_att_out, w_o, norm2_g, w_ff1, w_ff2, norm_f_g, loss_target, m_w_ada, m_b_ada, m_norm1_g, m_w_in, m_rel_bias, m_ret_gn_g, m_ret_gn_b, m_w_ret_out, m_w_att_out, m_w_o, m_norm2_g, m_w_ff1, m_w_ff2, m_norm_f_g, v_w_ada, v_b_ada, v_norm1_g, v_w_in, v_rel_bias, v_ret_gn_g, v_ret_gn_b, v_w_ret_out, v_w_att_out, v_w_o, v_norm2_g, v_w_ff1, v_w_ff2, v_norm_f_g):
    mx, my, mc = _mesh_pos()
    dev = 4 * mx + 2 * my + mc
    chip = jnp.reshape(2 * mx + my, (1,)).astype(jnp.int32)
    core = jnp.reshape(mc, (1,)).astype(jnp.int32)
    ada_w = D * 6 // N_DEV

    shards = [w[0].astype(BF16) for w in (w_in, w_ret_out, w_att_out, w_o, w_ff1, w_ff2)]
    c_all, w_in8 = _run_comm(_Gather([c, shards[0]]), "gather_c_w_in")
    c_all = c_all.reshape(N_DEV, D)
    b_sl = lax.dynamic_slice(b_ada, (0, dev * ada_w), (1, ada_w))
    (mod_all,) = _run_comm(_Gather([_ada_fwd(c_all, w_ada[0], b_sl)]), "gather_mod")
    mod = lax.dynamic_index_in_dim(mod_all, dev, axis=1, keepdims=False).reshape(6, D)
    mods = tuple(mod[i:i + 1] for i in range(6))

    small = (norm1_g, rel_bias, ret_gn_g, ret_gn_b, norm2_g, norm_f_g.reshape(1, D))
    loss, gx, big_red, small_g, dmod = _local_step(x[0], loss_target[0], mods, _from_slots(w_in8, BIG_AXES[0]),
                                                   shards[1:], small, chip, core)

    gathered = _run_comm(_Gather(dmod + small_g + [loss]), "gather_small")
    g_b_ada, dmod_all, (g_norm1, g_bias, g_gn_g, g_gn_b, g_norm2, g_normf, loss_sum) = _sum_small(gathered)
    loss_out = loss_sum[0, 0]
    g_w_ada = _ada_bwd(c_all, lax.dynamic_slice(dmod_all, (0, dev * ada_w), (N_DEV, ada_w)))

    names = ['w_ada', 'b_ada', 'norm1_g', 'w_in', 'rel_bias', 'ret_gn_g', 'ret_gn_b', 'w_ret_out', 'w_att_out',
             'w_o', 'norm2_g', 'w_ff1', 'w_ff2', 'norm_f_g']
    ws = dict(zip(names, (w_ada, b_ada, norm1_g, w_in, rel_bias, ret_gn_g, ret_gn_b, w_ret_out, w_att_out, w_o,
                          norm2_g, w_ff1, w_ff2, norm_f_g)))
    ms = dict(zip(names, (m_w_ada, m_b_ada, m_norm1_g, m_w_in, m_rel_bias, m_ret_gn_g, m_ret_gn_b, m_w_ret_out,
                          m_w_att_out, m_w_o, m_norm2_g, m_w_ff1, m_w_ff2, m_norm_f_g)))
    vs = dict(zip(names, (v_w_ada, v_b_ada, v_norm1_g, v_w_in, v_rel_bias, v_ret_gn_g, v_ret_gn_b, v_w_ret_out,
                          v_w_att_out, v_w_o, v_norm2_g, v_w_ff1, v_w_ff2, v_norm_f_g)))
    grads = dict(w_ada=g_w_ada, w_in=big_red[0], w_ret_out=big_red[1], w_att_out=big_red[2], w_o=big_red[3],
                 w_ff1=big_red[4], w_ff2=big_red[5], b_ada=g_b_ada, norm1_g=g_norm1, rel_bias=g_bias,
                 ret_gn_g=g_gn_g, ret_gn_b=g_gn_b, norm2_g=g_norm2, norm_f_g=g_normf)
    delta, new_m, new_v = {}, {}, {}
    for n in ('w_ada', 'w_in', 'w_ret_out', 'w_att_out', 'w_o', 'w_ff1', 'w_ff2'):
        delta[n], new_m[n], new_v[n] = _adamw(ws[n], grads[n], ms[n], vs[n], "adamw_" + n)
        grads[n] = grads[n].reshape(ws[n].shape)
    small_names = ('b_ada', 'norm1_g', 'rel_bias', 'ret_gn_g', 'ret_gn_b', 'norm2_g', 'norm_f_g')
    two_d = {n: (1, ws[n].size) if ws[n].ndim == 1 else ws[n].shape for n in small_names}
    d_, m_, v_ = _adamw_small(*[[src[n].reshape(two_d[n]) for n in small_names] for src in (ws, grads, ms, vs)])
    for i, n in enumerate(small_names):
        shp = ws[n].shape
        delta[n], new_m[n], new_v[n] = d_[i].reshape(shp), m_[i].reshape(shp), v_[i].reshape(shp)
        grads[n] = grads[n].reshape(shp)
    return (loss_out, gx[None], *[grads[n] for n in names], *[delta[n] for n in names],
            *[new_m[n] for n in names], *[new_v[n] for n in names])
```

```python
import functools
import math

import numpy as np
import jax
import jax.numpy as jnp
from jax import lax
from jax.experimental import pallas as pl
from jax.experimental.pallas import tpu as pltpu

F32 = jnp.float32
BF16 = jnp.bfloat16
MESH = pl.DeviceIdType.MESH

N_DEV = 8
S = 2048
D = 1024
RET_HEADS = 4
RET_DK = 256
RET_DV = 512
CHUNK = 128
N_CHUNK = S // CHUNK
ATT_GROUPS = ((128, 1), (512, 4), (2048, 16))
ATT_HG = 4
ATT_DH = 128
ATT_BLK = 128
N_BUCKETS = 32
MAX_DIST = 2048
D_FF = 4096
IN_COLS = 12800
OFF_RQ, OFF_RK, OFF_RV, OFF_RG, OFF_ATT = 0, 1024, 2048, 4096, 6144
OFF_GA, OFF_GB = 6144, 7168
RMS_EPS = 1e-6
GN_EPS = 1e-5
ADAM_LR, ADAM_B1, ADAM_B2, ADAM_EPS, ADAM_WD, ADAM_STEP = 0.001, 0.9, 0.999, 1e-08, 0.01, 10
VMEM_LIMIT = 48 * 1024 * 1024


def _pcall(body, **kw):
    return pl.pallas_call(body, **kw)


def _params(sem=None):
    return pltpu.CompilerParams(dimension_semantics=sem, vmem_limit_bytes=VMEM_LIMIT)


HBM_SPEC = pl.BlockSpec(memory_space=pl.ANY)


def _carry(body, comm, *, name, grid, in_specs, out_specs, out_shape, scratch_shapes=()):
    single = not isinstance(out_specs, (tuple, list))
    o_specs = (out_specs,) if single else tuple(out_specs)
    o_shape = (out_shape,) if single else tuple(out_shape)
    n_in, n_out, n_scr = len(in_specs), len(o_specs), len(scratch_shapes)
    nci, nco = len(comm.ins), len(comm.out_shape)
    total = int(np.prod(grid))

    def wrapped(*refs):
        bounds = np.cumsum([0, n_in, nci, n_out, nco, n_scr])
        a, ci, o, co, scr = (refs[bounds[i]:bounds[i + 1]] for i in range(5))
        sems = refs[bounds[5]:]
        flat = 0
        for d, g in enumerate(grid):
            flat = flat * g + pl.program_id(d)

        @pl.when(flat == 0)
        def _():
            comm.start(ci, co, sems)

        body(*a, *o, *scr)

        @pl.when(flat == total - 1)
        def _():
            comm.finish(ci, co, sems)

    call = _pcall(wrapped, name=name, grid=grid, in_specs=list(in_specs) + [HBM_SPEC] * nci,
                  out_specs=o_specs + (HBM_SPEC,) * nco, out_shape=o_shape + tuple(comm.out_shape),
                  scratch_shapes=list(scratch_shapes) + list(comm.sems),
                  compiler_params=_params(("arbitrary",) * len(grid)))

    def run(*args):
        res = call(*args, *comm.ins)
        own = res[0] if single else tuple(res[:n_out])
        return own, tuple(res[n_out:])

    return run


def _run_comm(comm, name):
    nci, nco = len(comm.ins), len(comm.out_shape)

    def body(*refs):
        ci, co, sems = refs[:nci], refs[nci:nci + nco], refs[nci + nco:]
        comm.start(ci, co, sems)
        comm.finish(ci, co, sems)

    return _pcall(body, name=name, in_specs=[HBM_SPEC] * nci, out_specs=(HBM_SPEC,) * nco,
                  out_shape=tuple(comm.out_shape), scratch_shapes=list(comm.sems))(*comm.ins)


def _dot(a, b, dn):
    return lax.dot_general(a.astype(BF16), b.astype(BF16), (dn, ((), ())), preferred_element_type=F32)


NN = ((1,), (0,))
NT = ((1,), (1,))
TN = ((0,), (0,))


def _mm(a, b, mode, *, tm, tn, tk, name, out_dtype=F32, res=None, gvec=None, relu2=False, relu2_of=None, comm=None):
    if mode == 'nn':
        (M, K), (_, N) = a.shape, b.shape
        a_spec = pl.BlockSpec((tm, tk), lambda i, j, k: (i, k))
        b_spec = pl.BlockSpec((tk, tn), lambda i, j, k: (k, j))
        dn = NN
    elif mode == 'nt':
        (M, K), (N, _) = a.shape, b.shape
        a_spec = pl.BlockSpec((tm, tk), lambda i, j, k: (i, k))
        b_spec = pl.BlockSpec((tn, tk), lambda i, j, k: (j, k))
        dn = NT
    else:
        (K, M), (_, N) = a.shape, b.shape
        a_spec = pl.BlockSpec((tk, tm), lambda i, j, k: (k, i))
        b_spec = pl.BlockSpec((tk, tn), lambda i, j, k: (k, j))
        dn = TN
    assert M % tm == 0 and N % tn == 0 and K % tk == 0, (name, M, N, K)
    nk = K // tk
    fused = res is not None
    o_spec = pl.BlockSpec((tm, tn), lambda i, j, k: (i, j))

    def body(a_ref, b_ref, *rest):
        acc_ref = rest[-1] if nk > 1 else None
        if fused:
            res_ref, g_ref, o_ref, x_ref = rest[:4]
        elif relu2_of is not None:
            u_ref, o_ref = rest[:2]
        elif relu2:
            o_ref, act_ref = rest[:2]
        else:
            o_ref = rest[0]

        def finish(acc):
            if relu2_of is not None:
                acc = acc * (2.0 * jnp.maximum(u_ref[...], 0.0))
            o_ref[...] = acc.astype(o_ref.dtype)
            if fused:
                x_ref[...] = res_ref[...] + g_ref[...] * acc
            if relu2:
                r = jnp.maximum(acc, 0.0)
                act_ref[...] = (r * r).astype(BF16)

        p = _dot(a_ref[...], b_ref[...], dn)
        if nk == 1:
            finish(p)
        else:
            k = pl.program_id(2)

            @pl.when(k == 0)
            def _():
                acc_ref[...] = p

            @pl.when(k > 0)
            def _():
                acc_ref[...] += p

            @pl.when(k == nk - 1)
            def _():
                finish(acc_ref[...])

    in_specs = [a_spec, b_spec]
    args = [a, b]
    out_shape = jax.ShapeDtypeStruct((M, N), out_dtype)
    out_specs = o_spec
    if fused:
        in_specs += [pl.BlockSpec((tm, tn), lambda i, j, k: (i, j)), pl.BlockSpec((1, tn), lambda i, j, k: (0, j))]
        args += [res, gvec]
        out_shape = (out_shape, jax.ShapeDtypeStruct((M, N), F32))
        out_specs = (o_spec, pl.BlockSpec((tm, tn), lambda i, j, k: (i, j)))
    elif relu2_of is not None:
        in_specs.append(pl.BlockSpec((tm, tn), lambda i, j, k: (i, j)))
        args.append(relu2_of)
    elif relu2:
        out_shape = (out_shape, jax.ShapeDtypeStruct((M, N), BF16))
        out_specs = (o_spec, pl.BlockSpec((tm, tn), lambda i, j, k: (i, j)))
    kw = dict(name=name, grid=(M // tm, N // tn, nk), in_specs=in_specs, out_specs=out_specs,
              out_shape=out_shape, scratch_shapes=[pltpu.VMEM((tm, tn), F32)] if nk > 1 else [])
    if comm is not None:
        return _carry(body, comm, **kw)(*args)
    return _pcall(body, compiler_params=_params(("parallel", "parallel", "arbitrary")), **kw)(*args)


PROJ_TN = 512
ATT_T0, ATT_T1 = 6144 // PROJ_TN, 10752 // PROJ_TN
N_SLABS = (ATT_T1 - ATT_T0) * 4
MAIN_COLS = IN_COLS - (ATT_T1 - ATT_T0) * PROJ_TN


def _proj(h1, w_in_t, comm):
    nj = IN_COLS // PROJ_TN

    def body(a_ref, b_ref, main_ref, slab_ref):
        j = pl.program_id(1)
        p = _dot(a_ref[...], b_ref[...], NT)
        is_att = (j >= ATT_T0) & (j < ATT_T1)

        @pl.when(jnp.logical_not(is_att))
        def _():
            main_ref[...] = p

        @pl.when(is_att)
        def _():
            for h in range(4):
                slab_ref[h] = p[:, h * 128:(h + 1) * 128]

    main_idx = lambda j: jnp.where(j < ATT_T0, j, jnp.where(j < ATT_T1, ATT_T0 - 1, j - (ATT_T1 - ATT_T0)))
    slab_idx = lambda j: jnp.clip(j - ATT_T0, 0, ATT_T1 - ATT_T0 - 1)
    (main, slabs), got = _carry(
        body, comm, name="proj", grid=(1, nj, 1),
        in_specs=[pl.BlockSpec((S, D), lambda i, j, k: (0, 0)), pl.BlockSpec((PROJ_TN, D), lambda i, j, k: (j, 0))],
        out_specs=(pl.BlockSpec((S, PROJ_TN), lambda i, j, k: (0, main_idx(j))),
                   pl.BlockSpec((4, S, 128), lambda i, j, k: (slab_idx(j), 0, 0))),
        out_shape=(jax.ShapeDtypeStruct((S, MAIN_COLS), F32), jax.ShapeDtypeStruct((N_SLABS, S, 128), F32)))(h1, w_in_t)
    return main, slabs, got


TR = 256


def _row_spec(w=D):
    return pl.BlockSpec((TR, w), lambda i: (i, 0))


def _vec_spec(w=D):
    return pl.BlockSpec((1, w), lambda i: (0, 0))


def _norm_mod_fwd(x, g, sh, sc, name):
    def body(x_ref, g_ref, sh_ref, sc_ref, o_ref):
        xv = x_ref[...]
        rstd = lax.rsqrt(jnp.mean(xv * xv, axis=-1, keepdims=True) + RMS_EPS)
        n = xv * rstd * g_ref[...]
        o_ref[...] = (n * (1.0 + sc_ref[...]) + sh_ref[...]).astype(BF16)

    return _pcall(body, name=name, grid=(S // TR,), in_specs=[_row_spec(), _vec_spec(), _vec_spec(), _vec_spec()],
                  out_specs=_row_spec(), out_shape=jax.ShapeDtypeStruct((S, D), BF16),
                  compiler_params=_params(("parallel",)))(x, g, sh, sc)


def _norm_mod_bwd(x, g, sc, dh, dres, name, gate=None):
    gated = gate is not None

    def body(x_ref, g_ref, sc_ref, dh_ref, dres_ref, *rest):
        if gated:
            f_ref, gv_ref, dx_ref, dsc_ref, dsh_ref, dg_ref, dz_ref, dgv_ref = rest
        else:
            dx_ref, dsc_ref, dsh_ref, dg_ref = rest
        i = pl.program_id(0)
        xv = x_ref[...]
        dh = dh_ref[...]
        rstd = lax.rsqrt(jnp.mean(xv * xv, axis=-1, keepdims=True) + RMS_EPS)
        xhat = xv * rstd
        gv = g_ref[...]
        dn = dh * (1.0 + sc_ref[...])
        dxhat = dn * gv
        dx = dres_ref[...] + rstd * (dxhat - xhat * jnp.mean(dxhat * xhat, axis=-1, keepdims=True))
        dx_ref[...] = dx
        sums = [(dsc_ref, jnp.sum(dh * (xhat * gv), axis=0, keepdims=True)),
                (dsh_ref, jnp.sum(dh, axis=0, keepdims=True)),
                (dg_ref, jnp.sum(dn * xhat, axis=0, keepdims=True))]
        if gated:
            dz_ref[...] = (dx * gv_ref[...]).astype(BF16)
            sums.append((dgv_ref, jnp.sum(dx * f_ref[...], axis=0, keepdims=True)))

        @pl.when(i == 0)
        def _():
            for ref, p in sums:
                ref[...] = p

        @pl.when(i > 0)
        def _():
            for ref, p in sums:
                ref[...] += p

    vec = jax.ShapeDtypeStruct((1, D), F32)
    in_specs = [_row_spec(), _vec_spec(), _vec_spec(), _row_spec(), _row_spec()]
    out_specs = [_row_spec(), _vec_spec(), _vec_spec(), _vec_spec()]
    out_shape = [jax.ShapeDtypeStruct((S, D), F32), vec, vec, vec]
    args = [x, g, sc, dh, dres]
    if gated:
        in_specs += [_row_spec(), _vec_spec()]
        out_specs += [_row_spec(), _vec_spec()]
        out_shape += [jax.ShapeDtypeStruct((S, D), BF16), vec]
        args += list(gate)
    return _pcall(body, name=name, grid=(S // TR,), in_specs=in_specs, out_specs=tuple(out_specs),
                  out_shape=tuple(out_shape), compiler_params=_params(("arbitrary",)))(*args)


def _final_loss(x2, tgt, g, f, g2):
    def body(x_ref, t_ref, g_ref, f_ref, g2_ref, loss_ref, dx_ref, dg_ref, df_ref, dg2_ref):
        i = pl.program_id(0)
        xv = x_ref[...]
        gv = g_ref[...]
        rstd = lax.rsqrt(jnp.mean(xv * xv, axis=-1, keepdims=True) + RMS_EPS)
        xhat = xv * rstd
        err = xhat * gv - t_ref[...]
        dy = err * (1.0 / D)
        dxhat = dy * gv
        dx = rstd * (dxhat - xhat * jnp.mean(dxhat * xhat, axis=-1, keepdims=True))
        dx_ref[...] = dx
        df_ref[...] = (dx * g2_ref[...]).astype(BF16)
        p_g = jnp.sum(dy * xhat, axis=0, keepdims=True)
        p_g2 = jnp.sum(dx * f_ref[...], axis=0, keepdims=True)
        p_l = jnp.zeros((1, 128), F32) + 0.5 * jnp.sum(jnp.mean(err * err, axis=-1, keepdims=True))

        @pl.when(i == 0)
        def _():
            dg_ref[...] = p_g
            dg2_ref[...] = p_g2
            loss_ref[...] = p_l

        @pl.when(i > 0)
        def _():
            dg_ref[...] += p_g
            dg2_ref[...] += p_g2
            loss_ref[...] += p_l

    vec = jax.ShapeDtypeStruct((1, D), F32)
    return _pcall(body, name="final_loss", grid=(S // TR,),
                  in_specs=[_row_spec(), _row_spec(), _vec_spec(), _row_spec(), _vec_spec()],
                  out_specs=(_vec_spec(128), _row_spec(), _vec_spec(), _row_spec(), _vec_spec()),
                  out_shape=(jax.ShapeDtypeStruct((1, 128), F32), jax.ShapeDtypeStruct((S, D), F32), vec,
                             jax.ShapeDtypeStruct((S, D), BF16), vec),
                  compiler_params=_params(("arbitrary",)))(x2, tgt, g, f, g2)


HALF = 512


def _merge_fwd(proj, ret_out, att_out):
    def body(ga_ref, gb_ref, r_ref, a_ref, o_ref):
        o_ref[...] = (jax.nn.sigmoid(ga_ref[...]) * r_ref[...] + jax.nn.sigmoid(gb_ref[...]) * a_ref[...]).astype(BF16)

    blk = lambda off: pl.BlockSpec((TR, HALF), lambda i, j: (i, off // HALF + j))
    return _pcall(body, name="merge_fwd", grid=(S // TR, D // HALF),
                  in_specs=[blk(OFF_GA), blk(OFF_GB), blk(0), blk(0)], out_specs=blk(0),
                  out_shape=jax.ShapeDtypeStruct((S, D), BF16),
                  compiler_params=_params(("parallel", "parallel")))(proj, proj, ret_out, att_out)


def _merge_bwd(proj, ret_out, att_out, dmerged):
    def body(ga_ref, gb_ref, r_ref, a_ref, dm_ref, dr_ref, da_ref, dga_ref, dgb_ref):
        sa = jax.nn.sigmoid(ga_ref[...])
        sb = jax.nn.sigmoid(gb_ref[...])
        dm = dm_ref[...]
        dr_ref[...] = (dm * sa).astype(BF16)
        da_ref[...] = (dm * sb).astype(BF16)
        dga_ref[...] = (dm * r_ref[...] * (sa * (1.0 - sa))).astype(BF16)
        dgb_ref[...] = (dm * a_ref[...] * (sb * (1.0 - sb))).astype(BF16)

    blk = lambda off: pl.BlockSpec((TR, HALF), lambda i, j: (i, off // HALF + j))
    o = jax.ShapeDtypeStruct((S, D), BF16)
    return _pcall(body, name="merge_bwd", grid=(S // TR, D // HALF),
                  in_specs=[blk(OFF_GA), blk(OFF_GB), blk(0), blk(0), blk(0)], out_specs=(blk(0),) * 4,
                  out_shape=(o, o, o, o),
                  compiler_params=_params(("parallel", "parallel")))(proj, proj, ret_out, att_out, dmerged)


def _ret_tables():
    H, C = RET_HEADS, CHUNK
    log_g = jnp.log1p(-(2.0 ** (-5.0 - jnp.arange(H, dtype=F32))))
    idx = jnp.arange(C, dtype=F32)
    rel = idx[:, None] - idx[None, :]
    inner = jnp.where(rel >= 0, jnp.exp(log_g[:, None, None] * jnp.maximum(rel, 0.0)), 0.0)
    qd = jnp.exp(log_g[:, None] * (idx + 1.0))[:, :, None]
    kd = jnp.exp(log_g[:, None] * (C - 1.0 - idx))[:, :, None]
    cd = jnp.broadcast_to(jnp.exp(log_g * C)[:, None, None], (H, 1, 128))
    half = RET_DK // 2
    inv = 10000.0 ** (-jnp.arange(half, dtype=F32) / half)
    ang = jnp.arange(S, dtype=F32)[:, None] * inv[None, :]
    return inner, qd, kd, cd, jnp.cos(ang), jnp.sin(ang)


def _rot(x, cos, sin):
    x1, x2 = x[:, :128], x[:, 128:]
    return jnp.concatenate([x1 * cos - x2 * sin, x1 * sin + x2 * cos], axis=1)


def _rot_t(d, cos, sin):
    d1, d2 = d[:, :128], d[:, 128:]
    return jnp.concatenate([d1 * cos + d2 * sin, d2 * cos - d1 * sin], axis=1)


RET_COLS = OFF_ATT
RET_VW = RET_HEADS * RET_DV


def _ret_specs(chunk_of):
    ci = chunk_of
    whole = lambda shape: pl.BlockSpec(shape, lambda t: (0,) * len(shape))
    return [
        pl.BlockSpec((CHUNK, RET_COLS), lambda t: (ci(t), 0)),
        pl.BlockSpec((CHUNK, 128), lambda t: (ci(t), 0)),
        pl.BlockSpec((CHUNK, 128), lambda t: (ci(t), 0)),
        whole((RET_HEADS, CHUNK, CHUNK)), whole((RET_HEADS, CHUNK, 1)), whole((RET_HEADS, CHUNK, 1)),
        whole((RET_HEADS, 1, 128)), whole((1, RET_VW)), whole((1, RET_VW)),
    ]


def _ret_cols(h):
    q = slice(OFF_RQ + h * RET_DK, OFF_RQ + (h + 1) * RET_DK)
    k = slice(OFF_RK + h * RET_DK, OFF_RK + (h + 1) * RET_DK)
    v = slice(OFF_RV + h * RET_DV, OFF_RV + (h + 1) * RET_DV)
    g = slice(OFF_RG + h * RET_DV, OFF_RG + (h + 1) * RET_DV)
    return q, k, v, g, slice(h * RET_DV, (h + 1) * RET_DV)


def _ret_fwd(proj, tables, gn_g, gn_b, comm=None):
    inner, qd, kd, cd, cos, sin = tables

    def body(x_ref, cos_ref, sin_ref, in_ref, qd_ref, kd_ref, cd_ref, g_ref, b_ref,
             gated_ref, ro_ref, st_ref, s_scr):
        i = pl.program_id(0)

        @pl.when(i == 0)
        def _():
            s_scr[...] = jnp.zeros_like(s_scr)

        cosv, sinv = cos_ref[...], sin_ref[...]
        for h in range(RET_HEADS):
            cq, ck, cv, cg, co = _ret_cols(h)
            q = _rot(x_ref[:, cq], cosv, sinv)
            k = _rot(x_ref[:, ck], cosv, sinv) * (RET_DK ** -0.5)
            v = x_ref[:, cv]
            st = s_scr[h]
            st_ref[h] = st
            s = _dot(q, k, NT) * in_ref[h]
            o = _dot(s, v, NN) + _dot(q, st, NN) * qd_ref[h]
            s_scr[h] = st * cd_ref[h, :, :1] + _dot(k * kd_ref[h], v, TN)
            ro_ref[:, co] = o
            mu = jnp.mean(o, axis=-1, keepdims=True)
            oc = o - mu
            var = jnp.mean(oc * oc, axis=-1, keepdims=True)
            rn = oc * lax.rsqrt(var + GN_EPS) * g_ref[:, co] + b_ref[:, co]
            rg = x_ref[:, cg]
            gated_ref[:, co] = (rg * jax.nn.sigmoid(rg) * rn).astype(BF16)

    ospec = pl.BlockSpec((CHUNK, RET_VW), lambda t: (t, 0))
    kw = dict(name="ret_fwd", grid=(N_CHUNK,), in_specs=_ret_specs(lambda t: t),
              out_specs=(ospec, ospec, pl.BlockSpec((RET_HEADS, None, RET_DK, RET_DV), lambda t: (0, t, 0, 0))),
              out_shape=(jax.ShapeDtypeStruct((S, RET_VW), BF16), jax.ShapeDtypeStruct((S, RET_VW), F32),
                         jax.ShapeDtypeStruct((RET_HEADS, N_CHUNK, RET_DK, RET_DV), F32)),
              scratch_shapes=[pltpu.VMEM((RET_HEADS, RET_DK, RET_DV), F32)])
    args = (proj, cos, sin, inner, qd, kd, cd, gn_g, gn_b)
    if comm is not None:
        return _carry(body, comm, **kw)(*args)
    return _pcall(body, compiler_params=_params(("arbitrary",)), **kw)(*args)


def _ret_bwd(proj, tables, gn_g, gn_b, ro, states, dgated, comm=None):
    inner, qd, kd, cd, cos, sin = tables
    last = N_CHUNK - 1

    def body(x_ref, cos_ref, sin_ref, in_ref, qd_ref, kd_ref, cd_ref, g_ref, b_ref, ro_ref, st_ref, dg_ref,
             dx_ref, gg_ref, gb_ref, gs_scr):
        t = pl.program_id(0)

        @pl.when(t == 0)
        def _():
            gs_scr[...] = jnp.zeros_like(gs_scr)
            gg_ref[...] = jnp.zeros_like(gg_ref)
            gb_ref[...] = jnp.zeros_like(gb_ref)

        cosv, sinv = cos_ref[...], sin_ref[...]
        for h in range(RET_HEADS):
            cq, ck, cv, cg, co = _ret_cols(h)
            q = _rot(x_ref[:, cq], cosv, sinv)
            k = _rot(x_ref[:, ck], cosv, sinv) * (RET_DK ** -0.5)
            v = x_ref[:, cv]
            qdv, kdv, dm = qd_ref[h], kd_ref[h], in_ref[h]
            st = st_ref[h]
            o = ro_ref[:, co]
            gv = g_ref[:, co]
            mu = jnp.mean(o, axis=-1, keepdims=True)
            oc = o - mu
            rstd = lax.rsqrt(jnp.mean(oc * oc, axis=-1, keepdims=True) + GN_EPS)
            ohat = oc * rstd
            rn = ohat * gv + b_ref[:, co]
            rg = x_ref[:, cg]
            sg = jax.nn.sigmoid(rg)
            dgt = dg_ref[:, co]
            drn = dgt * (rg * sg)
            dx_ref[:, cg] = (dgt * rn * (sg * (1.0 + rg * (1.0 - sg)))).astype(BF16)
            gg_ref[:, co] += jnp.sum(drn * ohat, axis=0, keepdims=True)
            gb_ref[:, co] += jnp.sum(drn, axis=0, keepdims=True)
            dohat = drn * gv
            do = rstd * (dohat - jnp.mean(dohat, axis=-1, keepdims=True)
                         - ohat * jnp.mean(dohat * ohat, axis=-1, keepdims=True))
            gs = gs_scr[h]
            s = _dot(q, k, NT) * dm
            dsr = _dot(do, v, NT) * dm
            dq = _dot(dsr, k, NN) + _dot(do, st, NT) * qdv
            dk = _dot(dsr, q, TN) + _dot(v, gs, NT) * kdv
            dv = _dot(s, do, TN) + _dot(k * kdv, gs, NN)
            gs_scr[h] = gs * cd_ref[h, :, :1] + _dot(q * qdv, do, TN)
            dx_ref[:, cq] = _rot_t(dq, cosv, sinv).astype(BF16)
            dx_ref[:, ck] = (_rot_t(dk, cosv, sinv) * (RET_DK ** -0.5)).astype(BF16)
            dx_ref[:, cv] = dv.astype(BF16)

    rev = lambda t: last - t
    vblk = pl.BlockSpec((CHUNK, RET_VW), lambda t: (rev(t), 0))
    vspec = pl.BlockSpec((1, RET_VW), lambda t: (0, 0))
    kw = dict(name="ret_bwd", grid=(N_CHUNK,),
              in_specs=_ret_specs(rev) + [vblk, pl.BlockSpec((RET_HEADS, None, RET_DK, RET_DV),
                                                             lambda t: (0, rev(t), 0, 0)), vblk],
              out_specs=(pl.BlockSpec((CHUNK, RET_COLS), lambda t: (rev(t), 0)), vspec, vspec),
              out_shape=(jax.ShapeDtypeStruct((S, RET_COLS), BF16), jax.ShapeDtypeStruct((1, RET_VW), F32),
                         jax.ShapeDtypeStruct((1, RET_VW), F32)),
              scratch_shapes=[pltpu.VMEM((RET_HEADS, RET_DK, RET_DV), F32)])
    args = (proj, cos, sin, inner, qd, kd, cd, gn_g, gn_b, ro, states, dgated)
    if comm is not None:
        return _carry(body, comm, **kw)(*args)
    return _pcall(body, compiler_params=_params(("arbitrary",)), **kw)(*args)


def _bucket_tables():
    qi = np.arange(ATT_BLK)[:, None]
    kj = np.arange(2 * ATT_BLK)[None, :]
    m = ATT_BLK + qi - kj
    out = []
    for win, dil in ATT_GROUPS:
        w = win // dil
        dist = (np.clip(m, 0, w) * dil).astype(np.int32)
        max_exact = N_BUCKETS // 2
        d_f = np.maximum(dist, 1).astype(np.float32)
        large = max_exact + (np.log(d_f / np.float32(max_exact)) / np.float32(math.log(MAX_DIST / max_exact))
                             * np.float32(N_BUCKETS - max_exact)).astype(np.int32)
        large = np.minimum(large, N_BUCKETS - 1)
        out.append(np.where(dist < max_exact, dist, large).astype(np.int32))
    return np.stack(out)


def _bias_build(rel_bias, buckets):
    def body(tab_ref, bk_ref, o_ref):
        hh = pl.program_id(0)
        bk = bk_ref[...]
        acc = jnp.zeros((ATT_BLK, 2 * ATT_BLK), F32)
        for b in range(N_BUCKETS):
            acc = jnp.where(bk == b, tab_ref[b, hh], acc)
        o_ref[...] = acc

    nh = len(ATT_GROUPS) * ATT_HG
    return _pcall(body, name="bias_build", grid=(nh,),
                  in_specs=[pl.BlockSpec(memory_space=pltpu.SMEM),
                            pl.BlockSpec((None, ATT_BLK, 2 * ATT_BLK), lambda hh: (hh // ATT_HG, 0, 0))],
                  out_specs=pl.BlockSpec((None, ATT_BLK, 2 * ATT_BLK), lambda hh: (hh, 0, 0)),
                  out_shape=jax.ShapeDtypeStruct((nh, ATT_BLK, 2 * ATT_BLK), F32),
                  compiler_params=_params(("parallel",)))(rel_bias, buckets)


def _bias_grad(ds_sum, buckets):
    def body(ds_ref, bk_ref, o_ref):
        bk = bk_ref[...]
        ds = ds_ref[...]
        rows = lax.broadcasted_iota(jnp.int32, (N_BUCKETS, 128), 0)
        acc = jnp.zeros((N_BUCKETS, 128), F32)
        for b in range(N_BUCKETS):
            acc = jnp.where(rows == b, jnp.sum(jnp.where(bk == b, ds, 0.0)), acc)
        o_ref[...] = acc

    nh = len(ATT_GROUPS) * ATT_HG
    return _pcall(body, name="bias_grad", grid=(nh,),
                  in_specs=[pl.BlockSpec((None, ATT_BLK, 2 * ATT_BLK), lambda hh: (hh, 0, 0)),
                            pl.BlockSpec((None, ATT_BLK, 2 * ATT_BLK), lambda hh: (hh // ATT_HG, 0, 0))],
                  out_specs=pl.BlockSpec((None, N_BUCKETS, 128), lambda hh: (hh, 0, 0)),
                  out_shape=jax.ShapeDtypeStruct((nh, N_BUCKETS, 128), F32),
                  compiler_params=_params(("parallel",)))(ds_sum, buckets)


def _att_valid(n):
    qi = lax.broadcasted_iota(jnp.int32, (ATT_BLK, 2 * ATT_BLK), 0)
    kj = lax.broadcasted_iota(jnp.int32, (ATT_BLK, 2 * ATT_BLK), 1)
    m = ATT_BLK + qi - kj
    first_key = jnp.where(n > 0, 0, ATT_BLK)
    return (m >= 0) & (m <= ATT_BLK) & (kj >= first_key)


ATT_HP = (1, 2, 2)


def _att_geometry(gi):
    _, dil = ATT_GROUPS[gi]
    return dil, S // dil // ATT_BLK, ATT_HP[gi]


def _blk(dil, r, n):
    if dil == 1:
        return pl.ds(n * ATT_BLK, ATT_BLK)
    return pl.ds(r + n * ATT_BLK * dil, ATT_BLK, stride=dil)


def _slab_specs(gi):
    _, _, hp = _att_geometry(gi)
    per = ATT_HG // hp
    return [pl.BlockSpec((hp, S, ATT_DH), lambda g, r, part=part: ((3 * gi + part) * per + g, 0, 0))
            for part in range(3)]


def _head_specs(gi, count):
    _, _, hp = _att_geometry(gi)
    return [pl.BlockSpec((hp, S, ATT_DH), lambda g, r: (g, 0, 0))] * count


def _bias_spec(gi):
    _, _, hp = _att_geometry(gi)
    return pl.BlockSpec((hp, ATT_BLK, 2 * ATT_BLK), lambda g, r: (gi * (ATT_HG // hp) + g, 0, 0))


def _att_fwd(slabs, bias, gi):
    dil, nb, hp = _att_geometry(gi)
    scale = ATT_DH ** -0.5

    def body(q_ref, k_ref, v_ref, bias_ref, o_ref, l_ref):
        r = pl.program_id(1)
        for n in range(nb):
            valid = _att_valid(n)
            prev = _blk(dil, r, max(n - 1, 0))
            cur = _blk(dil, r, n)
            for h in range(hp):
                kk = jnp.concatenate([k_ref[h, prev, :], k_ref[h, cur, :]], axis=0)
                vv = jnp.concatenate([v_ref[h, prev, :], v_ref[h, cur, :]], axis=0)
                s = _dot(q_ref[h, cur, :], kk, NT) * scale + bias_ref[h]
                s = jnp.where(valid, s, -1e30)
                mx = jnp.max(s, axis=-1, keepdims=True)
                e = jnp.exp(s - mx)
                den = jnp.sum(e, axis=-1, keepdims=True)
                o_ref[h, cur, :] = _dot(e / den, vv, NN)
                l_ref[h, cur, :] = jnp.broadcast_to(mx + jnp.log(den), (ATT_BLK, ATT_DH))

    osh = jax.ShapeDtypeStruct((ATT_HG, S, ATT_DH), F32)
    return _pcall(body, name=f"att_fwd{gi}", grid=(ATT_HG // hp, dil), in_specs=_slab_specs(gi) + [_bias_spec(gi)],
                  out_specs=tuple(_head_specs(gi, 2)), out_shape=(osh, osh),
                  compiler_params=_params(("parallel", "arbitrary")))(slabs, slabs, slabs, bias)


def _att_bwd(slabs, bias, o, lse, do, dlse, gi, comm=None):
    dil, nb, hp = _att_geometry(gi)
    per = ATT_HG // hp
    scale = ATT_DH ** -0.5
    wh = hp * ATT_DH
    wide = lambda t: jnp.concatenate([t, t], axis=1)

    def body(q_ref, k_ref, v_ref, bias_ref, o_ref, l_ref, do_ref, dl_ref, dq_ref, dk_ref, dv_ref, ds_ref):
        r = pl.program_id(1)

        @pl.when(r == 0)
        def _():
            ds_ref[...] = jnp.zeros_like(ds_ref)

        for h in range(hp):
            sl = slice(h * ATT_DH, (h + 1) * ATT_DH)
            carry_k = carry_v = None
            for n in range(nb):
                valid = _att_valid(n)
                prev = _blk(dil, r, max(n - 1, 0))
                cur = _blk(dil, r, n)
                q = q_ref[h, cur, :]
                kk = jnp.concatenate([k_ref[h, prev, :], k_ref[h, cur, :]], axis=0)
                vv = jnp.concatenate([v_ref[h, prev, :], v_ref[h, cur, :]], axis=0)
                dov = do_ref[h, cur, :]
                s = _dot(q, kk, NT) * scale + bias_ref[h]
                p = jnp.where(valid, jnp.exp(s - wide(l_ref[h, cur, :])), 0.0)
                dp = _dot(dov, vv, NT)
                delta = jnp.sum(dov * o_ref[h, cur, :], axis=-1, keepdims=True)
                ds = p * (dp - delta + wide(dl_ref[h, cur, :]))
                ds_ref[h] += ds
                out_rows = pl.ds(n * ATT_BLK, ATT_BLK)
                dq_ref[out_rows, sl] = (_dot(ds, kk, NN) * scale).astype(BF16)
                dkk = _dot(ds, q, TN) * scale
                dvv = _dot(p, dov, TN)
                if n > 0:
                    before = pl.ds((n - 1) * ATT_BLK, ATT_BLK)
                    dk_ref[before, sl] = (carry_k + dkk[:ATT_BLK]).astype(BF16)
                    dv_ref[before, sl] = (carry_v + dvv[:ATT_BLK]).astype(BF16)
                carry_k, carry_v = dkk[ATT_BLK:], dvv[ATT_BLK:]
            last = pl.ds((nb - 1) * ATT_BLK, ATT_BLK)
            dk_ref[last, sl] = carry_k.astype(BF16)
            dv_ref[last, sl] = carry_v.astype(BF16)

    out_spec = pl.BlockSpec((S // dil, wh), lambda g, r: (0, r * per + g))
    osh = jax.ShapeDtypeStruct((S // dil, dil * AW), BF16)
    kw = dict(name=f"att_bwd{gi}", grid=(per, dil), in_specs=_slab_specs(gi) + [_bias_spec(gi)] + _head_specs(gi, 4),
              out_specs=(out_spec, out_spec, out_spec,
                         pl.BlockSpec((hp, ATT_BLK, 2 * ATT_BLK), lambda g, r: (g, 0, 0))),
              out_shape=(osh, osh, osh, jax.ShapeDtypeStruct((ATT_HG, ATT_BLK, 2 * ATT_BLK), F32)))
    args = (slabs, slabs, slabs, bias, o, lse, do, dlse)
    if comm is not None:
        return _carry(body, comm, **kw)(*args)
    return _pcall(body, compiler_params=_params(("arbitrary", "arbitrary")), **kw)(*args)


AW = ATT_HG * ATT_DH


def _mix_weights(l0, l1, l2):
    mx = jnp.maximum(jnp.maximum(l0, l1), l2)
    e0, e1, e2 = jnp.exp(l0 - mx), jnp.exp(l1 - mx), jnp.exp(l2 - mx)
    den = e0 + e1 + e2
    return e0 / den, e1 / den, e2 / den


def _heads_spec():
    return pl.BlockSpec((ATT_HG, TR, ATT_DH), lambda i: (0, i, 0))


def _mix_fwd(os_, ls):
    def body(o0, o1, o2, l0, l1, l2, att_ref):
        for h in range(ATT_HG):
            w0, w1, w2 = _mix_weights(l0[h], l1[h], l2[h])
            att_ref[:, h * ATT_DH:(h + 1) * ATT_DH] = (w0 * o0[h] + w1 * o1[h] + w2 * o2[h]).astype(BF16)

    return _pcall(body, name="mix_fwd", grid=(S // TR,), in_specs=[_heads_spec()] * 6, out_specs=_row_spec(AW),
                  out_shape=jax.ShapeDtypeStruct((S, AW), BF16), compiler_params=_params(("parallel",)))(*os_, *ls)


def _mix_bwd(os_, ls, datt):
    def body(o0, o1, o2, l0, l1, l2, da_ref, d0, d1, d2, e0, e1, e2):
        for h in range(ATT_HG):
            ws = _mix_weights(l0[h], l1[h], l2[h])
            da = da_ref[:, h * ATT_DH:(h + 1) * ATT_DH]
            dws = []
            for o_ref, w, d_ref in zip((o0, o1, o2), ws, (d0, d1, d2)):
                d_ref[h] = w * da
                dws.append(jnp.broadcast_to(jnp.sum(da * o_ref[h], axis=-1, keepdims=True), (TR, ATT_DH)))
            tot = ws[0] * dws[0] + ws[1] * dws[1] + ws[2] * dws[2]
            for w, dw, e_ref in zip(ws, dws, (e0, e1, e2)):
                e_ref[h] = w * (dw - tot)

    o = jax.ShapeDtypeStruct((ATT_HG, S, ATT_DH), F32)
    return _pcall(body, name="mix_bwd", grid=(S // TR,), in_specs=[_heads_spec()] * 6 + [_row_spec(AW)],
                  out_specs=(_heads_spec(),) * 6, out_shape=(o,) * 6,
                  compiler_params=_params(("parallel",)))(*os_, *ls, datt)


def _ada_fwd(c_all, w_sh, b_sl):
    def body(c_ref, w_ref, b_ref, o_ref):
        cv = c_ref[...]
        o_ref[...] = _dot(cv * jax.nn.sigmoid(cv), w_ref[...], NN) + b_ref[...]

    return _pcall(body, name="ada_fwd", out_shape=jax.ShapeDtypeStruct((N_DEV, w_sh.shape[1]), F32),
                  compiler_params=_params())(c_all, w_sh, b_sl)


def _ada_bwd(c_all, dm_sl):
    def body(c_ref, d_ref, o_ref):
        cv = c_ref[...]
        o_ref[...] = _dot(cv * jax.nn.sigmoid(cv), d_ref[...], TN)

    return _pcall(body, name="ada_bwd", out_shape=jax.ShapeDtypeStruct((D, dm_sl.shape[1]), F32),
                  compiler_params=_params())(c_all, dm_sl)


N_MOD = 6


def _sum_small(gathered):
    n = len(gathered)

    def body(*refs):
        ins, (gb_ref, dm_ref), outs = refs[:n], refs[n:n + 2], refs[n + 2:]

        def total(r):
            acc = r[0]
            for e in range(1, N_DEV):
                acc = acc + r[e]
            return acc

        for i in range(N_MOD):
            cols = slice(i * D, (i + 1) * D)
            gb_ref[:, cols] = total(ins[i])
            for e in range(N_DEV):
                dm_ref[e:e + 1, cols] = ins[i][e]
        for r, o_ref in zip(ins[N_MOD:], outs):
            o_ref[...] = total(r)

    shapes = (jax.ShapeDtypeStruct((1, N_MOD * D), F32), jax.ShapeDtypeStruct((N_DEV, N_MOD * D), F32),
              *[jax.ShapeDtypeStruct(g.shape[1:], F32) for g in gathered[N_MOD:]])
    res = _pcall(body, name="sum_small", out_shape=shapes, compiler_params=_params())(*gathered)
    return res[0], res[1], res[2:]


def _row_tile(m, n):
    t = max(8, min(m, (1 << 19) // n // 8 * 8))
    while m % t:
        t -= 8
    return t


def _pair_sum(full, recv, sel, name):
    _, _, m, n = full.shape
    t = _row_tile(m, n)

    def body(sel_ref, a_ref, b_ref, o_ref):
        o_ref[...] = (a_ref[...].astype(F32) + b_ref[...].astype(F32)).astype(o_ref.dtype)

    gs = pltpu.PrefetchScalarGridSpec(
        num_scalar_prefetch=1, grid=(4, m // t),
        in_specs=[pl.BlockSpec((None, None, t, n), lambda q, i, s: (q, s[0], i, 0)),
                  pl.BlockSpec((None, t, n), lambda q, i, s: (q, i, 0))],
        out_specs=pl.BlockSpec((None, t, n), lambda q, i, s: (q, i, 0)))
    return _pcall(body, name=name, grid_spec=gs, out_shape=jax.ShapeDtypeStruct((4, m, n), full.dtype),
                  compiler_params=_params(("parallel", "parallel")))(sel, full, recv)


def _chip_sum(part, recv, sel, name):
    _, m, n = part.shape
    t = _row_tile(m, n)

    def body(sel_ref, a_ref, r_ref, o_ref):
        o_ref[...] = ((a_ref[...].astype(F32) + r_ref[0].astype(F32)) + r_ref[1].astype(F32)) + r_ref[2].astype(F32)

    gs = pltpu.PrefetchScalarGridSpec(
        num_scalar_prefetch=1, grid=(m // t,),
        in_specs=[pl.BlockSpec((None, t, n), lambda i, s: (s[0], i, 0)),
                  pl.BlockSpec((3, t, n), lambda i, s: (0, i, 0))],
        out_specs=pl.BlockSpec((t, n), lambda i, s: (i, 0)))
    return _pcall(body, name=name, grid_spec=gs, out_shape=jax.ShapeDtypeStruct((m, n), F32),
                  compiler_params=_params(("parallel",)))(sel, part, recv)


def _adamw_math(w, g, m, v):
    nm = ADAM_B1 * m + (1.0 - ADAM_B1) * g
    nv = ADAM_B2 * v + (1.0 - ADAM_B2) * (g * g)
    m_hat = nm / (1.0 - ADAM_B1 ** ADAM_STEP)
    v_hat = nv / (1.0 - ADAM_B2 ** ADAM_STEP)
    return -ADAM_LR * (m_hat / (jnp.sqrt(v_hat) + ADAM_EPS) + ADAM_WD * w), nm, nv


def _adamw(w, g, m, v, name):
    _, rows, cols = w.shape
    t = _row_tile(rows, cols)

    def body(w_ref, g_ref, m_ref, v_ref, d_ref, nm_ref, nv_ref):
        d_ref[...], nm_ref[...], nv_ref[...] = _adamw_math(w_ref[...], g_ref[...], m_ref[...], v_ref[...])

    spec3 = pl.BlockSpec((None, t, cols), lambda i: (0, i, 0))
    spec2 = pl.BlockSpec((t, cols), lambda i: (i, 0))
    o = jax.ShapeDtypeStruct(w.shape, F32)
    return _pcall(body, name=name, grid=(rows // t,), in_specs=[spec3, spec2, spec3, spec3], out_specs=(spec3,) * 3,
                  out_shape=(o, o, o), compiler_params=_params(("parallel",)))(w, g, m, v)


def _adamw_small(ws, gs, ms, vs):
    n = len(ws)

    def body(*refs):
        for i in range(n):
            w_ref, g_ref, m_ref, v_ref = (refs[k * n + i] for k in range(4))
            d, nm, nv = _adamw_math(w_ref[...], g_ref[...], m_ref[...], v_ref[...])
            refs[4 * n + i][...] = d
            refs[5 * n + i][...] = nm
            refs[6 * n + i][...] = nv

    shapes = tuple(jax.ShapeDtypeStruct(w.shape, F32) for w in ws)
    res = _pcall(body, name="adamw_small", out_shape=shapes * 3, compiler_params=_params())(*ws, *gs, *ms, *vs)
    return res[:n], res[n:2 * n], res[2 * n:]


def _mesh_pos():
    return lax.axis_index("x"), lax.axis_index("y"), lax.axis_index("c")


class _Gather:
    def __init__(self, arrs):
        self.ins = list(arrs)
        na = self.na = len(arrs)
        self.out_shape = tuple(jax.ShapeDtypeStruct((N_DEV,) + a.shape, a.dtype) for a in arrs)
        self.sems = [pltpu.SemaphoreType.DMA((7 * na,)), pltpu.SemaphoreType.DMA((7 * na,)),
                     pltpu.SemaphoreType.DMA((na,))]

    def _copies(self, ins, outs, sems):
        send_sems, recv_sems, local_sems = sems
        x, y, c = _mesh_pos()
        me, sibling = (x, y, c), (x, y, 1 - c)
        chips = [(1 - x, y), (x, 1 - y), (1 - x, 1 - y)]

        def slot(p):
            return 4 * p[0] + 2 * p[1] + p[2]

        def copy(a, k, block, to, src=None):
            dst = outs[a].at[slot(block)]
            return pltpu.make_async_remote_copy(
                src_ref=dst if src is None else src, dst_ref=dst, send_sem=send_sems.at[7 * a + k],
                recv_sem=recv_sems.at[7 * a + k], device_id=to, device_id_type=MESH)

        mine = [pltpu.make_async_copy(ins[a], outs[a].at[slot(me)], local_sems.at[a]) for a in range(self.na)]
        first = []
        for a in range(self.na):
            first.append(copy(a, 0, me, sibling, src=ins[a]))
            first += [copy(a, 1 + j, me, (*chip, c), src=ins[a]) for j, chip in enumerate(chips)]
        return me, sibling, chips, c, copy, mine, first

    def start(self, ins, outs, sems):
        *_, mine, first = self._copies(ins, outs, sems)
        for cp in mine + first:
            cp.start()

    def finish(self, ins, outs, sems):
        me, sibling, chips, c, copy, mine, first = self._copies(ins, outs, sems)
        passed = []
        for j, chip in enumerate(chips):
            for a in range(self.na):
                copy(a, 1 + j, (*chip, c), me).wait_recv()
                cp = copy(a, 4 + j, (*chip, c), sibling)
                cp.start()
                passed.append(cp)
        for a in range(self.na):
            copy(a, 0, sibling, me).wait_recv()
            for j, chip in enumerate(chips):
                copy(a, 4 + j, (*chip, 1 - c), me).wait_recv()
        for cp in first + passed:
            cp.wait_send()
        for cp in mine:
            cp.wait()


class _ExchangeCore:
    def __init__(self, fulls):
        self.ins = list(fulls)
        self.out_shape = tuple(jax.ShapeDtypeStruct((4,) + f.shape[2:], f.dtype) for f in fulls)
        self.sems = [pltpu.SemaphoreType.DMA((4 * len(fulls),)), pltpu.SemaphoreType.DMA((4 * len(fulls),))]

    def _copies(self, ins, outs, sems):
        send_sems, recv_sems = sems
        x, y, c = _mesh_pos()
        return [pltpu.make_async_remote_copy(
            src_ref=ins[a].at[q, 1 - c], dst_ref=outs[a].at[q], send_sem=send_sems.at[4 * a + q],
            recv_sem=recv_sems.at[4 * a + q], device_id=(x, y, 1 - c), device_id_type=MESH)
            for a in range(len(self.ins)) for q in range(4)]

    def start(self, ins, outs, sems):
        for cp in self._copies(ins, outs, sems):
            cp.start()

    def finish(self, ins, outs, sems):
        for cp in self._copies(ins, outs, sems):
            cp.wait()


class _ExchangeChip:
    def __init__(self, parts):
        self.ins = list(parts)
        self.out_shape = tuple(jax.ShapeDtypeStruct((3,) + p.shape[1:], p.dtype) for p in parts)
        self.sems = [pltpu.SemaphoreType.DMA((3 * len(parts),)), pltpu.SemaphoreType.DMA((3 * len(parts),))]

    def _copies(self, ins, outs, sems):
        send_sems, recv_sems = sems
        x, y, c = _mesh_pos()
        chips = [(1 - x, y), (x, 1 - y), (1 - x, 1 - y)]
        return [pltpu.make_async_remote_copy(
            src_ref=ins[a].at[2 * px + py], dst_ref=outs[a].at[j], send_sem=send_sems.at[3 * a + j],
            recv_sem=recv_sems.at[3 * a + j], device_id=(px, py, c), device_id_type=MESH)
            for a in range(len(self.ins)) for j, (px, py) in enumerate(chips)]

    def start(self, ins, outs, sems):
        for cp in self._copies(ins, outs, sems):
            cp.start()

    def finish(self, ins, outs, sems):
        for cp in self._copies(ins, outs, sems):
            cp.wait()


def _reduce_sums(fulls, recv_core, core, tag):
    return [_pair_sum(f, r, core, f"rs_pair_{tag}{i}") for i, (f, r) in enumerate(zip(fulls, recv_core))]


def _local_step(x, tgt, mods, w_in_t, shards, small, chip, core):
    sh1, sc1, g1, sh2, sc2, g2 = mods
    norm1_g, rel_bias, gn_g, gn_b, norm2_g, norm_f_g = small
    tables = _ret_tables()
    buckets = jnp.asarray(_bucket_tables())

    h1 = _norm_mod_fwd(x, norm1_g, sh1, sc1, "norm1_fwd")
    proj, slabs, gathered = _proj(h1, w_in_t, _Gather(shards[:3]))
    w_ret_out, w_att_out, w_o = (_from_slots(g, ax) for g, ax in zip(gathered, BIG_AXES[1:4]))
    (gated, ro, states), gathered = _ret_fwd(proj, tables, gn_g, gn_b, comm=_Gather(shards[3:]))
    w_ff1, w_ff2 = (_from_slots(g, ax) for g, ax in zip(gathered, BIG_AXES[4:]))
    bias = _bias_build(rel_bias, buckets)
    outs, lses = [], []
    for gi in range(len(ATT_GROUPS)):
        o, l = _att_fwd(slabs, bias, gi)
        outs.append(o)
        lses.append(l)
    att = _mix_fwd(outs, lses)
    ret_out = _mm(gated, w_ret_out, 'nn', tm=S, tn=256, tk=2048, name="ret_out")
    att_out = _mm(att, w_att_out, 'nn', tm=S, tn=512, tk=AW, name="att_out")
    merged = _merge_fwd(proj, ret_out, att_out)
    mixo, x1 = _mm(merged, w_o, 'nn', tm=S, tn=256, tk=D, name="w_o", res=x, gvec=g1)
    h2 = _norm_mod_fwd(x1, norm2_g, sh2, sc2, "norm2_fwd")
    u, act = _mm(h2, w_ff1, 'nn', tm=S, tn=512, tk=D, name="ff1", relu2=True)
    f, x2 = _mm(act, w_ff2, 'nn', tm=1024, tn=512, tk=2048, name="ff2", res=x1, gvec=g2)
    loss, dx2, g_normf, df, dg2 = _final_loss(x2, tgt, norm_f_g, f, g2)

    gw_ff2 = _mm(act, df, 'tn', tm=512, tn=D, tk=S, name="gw_ff2", out_dtype=BF16)
    du = _mm(df, w_ff2, 'nt', tm=S, tn=512, tk=D, name="d_act", out_dtype=BF16, relu2_of=u)
    gw_ff1 = _mm(h2, du, 'tn', tm=D, tn=512, tk=S, name="gw_ff1", out_dtype=BF16)
    fulls_a = [_to_slots(g, ax) for g, ax in zip((gw_ff1, gw_ff2), BIG_AXES[4:])]
    dh2, recv_core_a = _mm(du, w_ff1, 'nt', tm=1024, tn=1024, tk=1024, name="dh2", comm=_ExchangeCore(fulls_a))
    parts_a = _reduce_sums(fulls_a, recv_core_a, core, "a")
    dx1, dsc2, dsh2, g_norm2, dmixo, dg1 = _norm_mod_bwd(x1, norm2_g, sc2, dh2, dx2, "norm2_bwd", gate=(mixo, g1))

    gw_o = _mm(merged, dmixo, 'tn', tm=D, tn=512, tk=S, name="gw_o", out_dtype=BF16)
    dmerged = _mm(dmixo, w_o, 'nt', tm=S, tn=512, tk=D, name="dmerged")
    d_ret_out, d_att_out, dga, dgb = _merge_bwd(proj, ret_out, att_out, dmerged)
    gw_ret_out = _mm(gated, d_ret_out, 'tn', tm=512, tn=D, tk=S, name="gw_ret_out", out_dtype=BF16)
    gw_att_out = _mm(att, d_att_out, 'tn', tm=AW, tn=D, tk=S, name="gw_att_out", out_dtype=BF16)
    fulls_b = [_to_slots(g, ax) for g, ax in zip((gw_ret_out, gw_att_out, gw_o), BIG_AXES[1:4])]
    dgated, recv_core_b = _mm(d_ret_out, w_ret_out, 'nt', tm=S, tn=512, tk=D, name="dgated",
                              comm=_ExchangeCore(fulls_b))
    parts_b = _reduce_sums(fulls_b, recv_core_b, core, "b")
    datt = _mm(d_att_out, w_att_out, 'nt', tm=S, tn=AW, tk=D, name="datt")
    mix_grads = _mix_bwd(outs, lses, datt)
    datt_parts, ds_sums = [], []
    for gi in range(len(ATT_GROUPS)):
        res = _att_bwd(slabs, bias, outs[gi], lses[gi], mix_grads[gi], mix_grads[3 + gi], gi,
                       comm=_ExchangeChip(parts_b) if gi == 1 else None)
        if gi == 1:
            res, recv_chip_b = res
        dq, dk, dv, ds_sum = res
        datt_parts += [dq.reshape(S, AW), dk.reshape(S, AW), dv.reshape(S, AW)]
        ds_sums.append(ds_sum)
    red_b = [_chip_sum(p, r, chip, f"rs_sum_b{i}") for i, (p, r) in enumerate(zip(parts_b, recv_chip_b))]
    g_bias = _bias_grad(jnp.concatenate(ds_sums, axis=0), buckets)[:, :, 0].T.reshape(1, -1)
    (dret, g_gn_g, g_gn_b), recv_chip_a = _ret_bwd(proj, tables, gn_g, gn_b, ro, states, dgated,
                                                   comm=_ExchangeChip(parts_a))
    red_a = [_chip_sum(p, r, chip, f"rs_sum_a{i}") for i, (p, r) in enumerate(zip(parts_a, recv_chip_a))]
    dproj = jnp.concatenate([dret] + datt_parts + [dga, dgb], axis=1)
    gw_in_t = _mm(dproj, h1, 'tn', tm=512, tn=D, tk=S, name="gw_in", out_dtype=BF16)
    full_in = [_to_slots(gw_in_t, 0)]
    dh1, recv_core_in = _mm(dproj, w_in_t, 'nn', tm=1024, tn=1024, tk=512, name="dh1", comm=_ExchangeCore(full_in))
    part_in = _reduce_sums(full_in, recv_core_in, core, "c")
    gx, dsc1, dsh1, g_norm1 = _norm_mod_bwd(x, norm1_g, sc1, dh1, dx1, "norm1_bwd")
    recv_chip_in = _run_comm(_ExchangeChip(part_in), "rs_chip_in")
    red_in = _chip_sum(part_in[0], recv_chip_in[0], chip, "rs_sum_c")

    dmod = [dsh1, dsc1, dg1, dsh2, dsc2, dg2]
    small_g = [g_norm1, g_bias, g_gn_g, g_gn_b, g_norm2, g_normf]
    return loss, gx, [red_in] + red_b + red_a, small_g, dmod


def _to_slots(g, axis):
    if axis == 0:
        return g.reshape(4, 2, g.shape[0] // N_DEV, g.shape[1])
    return g.reshape(g.shape[0], N_DEV, g.shape[1] // N_DEV).transpose(1, 0, 2).reshape(4, 2, g.shape[0], -1)


def _from_slots(w8, axis):
    if axis == 0:
        return w8.reshape(-1, w8.shape[2])
    return w8.transpose(1, 0, 2).reshape(w8.shape[1], -1)


BIG_AXES = (1, 0, 1, 0, 1, 0)


def kernel(x, c, w_ada, b_ada, norm1_g, w_in, rel_bias, ret_gn_g, ret_gn_b, w_ret_out, w_att_out, w_o, norm2_g, w_ff1, w_ff2, norm_f_g, loss_target, m_w_ada, m_b_ada, m_norm1_g, m_w_in, m_rel_bias, m_ret_gn_g, m_ret_gn_b, m_w_ret_out, m_w_att_out, m_w_o, m_norm2_g, m_w_ff1, m_w_ff2, m_norm_f_g, v_w_ada, v_b_ada, v_norm1_g, v_w_in, v_rel_bias, v_ret_gn_g, v_ret_gn_b, v_w_ret_out, v_w_att_out, v_w_o, v_norm2_g, v_w_ff1, v_w_ff2, v_norm_f_g):
    mx, my, mc = _mesh_pos()
    dev = 4 * mx + 2 * my + mc
    chip = jnp.reshape(2 * mx + my, (1,)).astype(jnp.int32)
    core = jnp.reshape(mc, (1,)).astype(jnp.int32)
    ada_w = D * 6 // N_DEV

    w_in, m_w_in, v_w_in = (jnp.transpose(t, (0, 2, 1)) for t in (w_in, m_w_in, v_w_in))

    shards = [w[0].astype(BF16) for w in (w_in, w_ret_out, w_att_out, w_o, w_ff1, w_ff2)]
    c_all, w_in8 = _run_comm(_Gather([c, shards[0]]), "gather_c_w_in")
    c_all = c_all.reshape(N_DEV, D)
    b_sl = lax.dynamic_slice(b_ada, (0, dev * ada_w), (1, ada_w))
    (mod_all,) = _run_comm(_Gather([_ada_fwd(c_all, w_ada[0], b_sl)]), "gather_mod")
    mod = lax.dynamic_index_in_dim(mod_all, dev, axis=1, keepdims=False).reshape(6, D)
    mods = tuple(mod[i:i + 1] for i in range(6))

    small = (norm1_g, rel_bias, ret_gn_g, ret_gn_b, norm2_g, norm_f_g.reshape(1, D))
    loss, gx, big_red, small_g, dmod = _local_step(x[0], loss_target[0], mods, w_in8.reshape(IN_COLS, D),
                                                   shards[1:], small, chip, core)

    gathered = _run_comm(_Gather(dmod + small_g + [loss]), "gather_small")
    g_b_ada, dmod_all, (g_norm1, g_bias, g_gn_g, g_gn_b, g_norm2, g_normf, loss_sum) = _sum_small(gathered)
    loss_out = loss_sum[0, 0]
    g_w_ada = _ada_bwd(c_all, lax.dynamic_slice(dmod_all, (0, dev * ada_w), (N_DEV, ada_w)))

    names = ['w_ada', 'b_ada', 'norm1_g', 'w_in', 'rel_bias', 'ret_gn_g', 'ret_gn_b', 'w_ret_out', 'w_att_out',
             'w_o', 'norm2_g', 'w_ff1', 'w_ff2', 'norm_f_g']
    ws = dict(zip(names, (w_ada, b_ada, norm1_g, w_in, rel_bias, ret_gn_g, ret_gn_b, w_ret_out, w_att_out, w_o,
                          norm2_g, w_ff1, w_ff2, norm_f_g)))
    ms = dict(zip(names, (m_w_ada, m_b_ada, m_norm1_g, m_w_in, m_rel_bias, m_ret_gn_g, m_ret_gn_b, m_w_ret_out,
                          m_w_att_out, m_w_o, m_norm2_g, m_w_ff1, m_w_ff2, m_norm_f_g)))
    vs = dict(zip(names, (v_w_ada, v_b_ada, v_norm1_g, v_w_in, v_rel_bias, v_ret_gn_g, v_ret_gn_b, v_w_ret_out,
                          v_w_att_out, v_w_o, v_norm2_g, v_w_ff1, v_w_ff2, v_norm_f_g)))
    grads = dict(w_ada=g_w_ada, w_in=big_red[0], w_ret_out=big_red[1], w_att_out=big_red[2], w_o=big_red[3],
                 w_ff1=big_red[4], w_ff2=big_red[5], b_ada=g_b_ada, norm1_g=g_norm1, rel_bias=g_bias,
                 ret_gn_g=g_gn_g, ret_gn_b=g_gn_b, norm2_g=g_norm2, norm_f_g=g_normf)
    delta, new_m, new_v = {}, {}, {}
    for n in ('w_ada', 'w_in', 'w_ret_out', 'w_att_out', 'w_o', 'w_ff1', 'w_ff2'):
        delta[n], new_m[n], new_v[n] = _adamw(ws[n], grads[n], ms[n], vs[n], "adamw_" + n)
        grads[n] = grads[n].reshape(ws[n].shape)
    for d in (grads, delta, new_m, new_v):
        d['w_in'] = jnp.transpose(d['w_in'], (0, 2, 1))
    small_names = ('b_ada', 'norm1_g', 'rel_bias', 'ret_gn_g', 'ret_gn_b', 'norm2_g', 'norm_f_g')
    two_d = {n: (1, ws[n].size) if ws[n].ndim == 1 else ws[n].shape for n in small_names}
    d_, m_, v_ = _adamw_small(*[[src[n].reshape(two_d[n]) for n in small_names] for src in (ws, grads, ms, vs)])
    for i, n in enumerate(small_names):
        shp = ws[n].shape
        delta[n], new_m[n], new_v[n] = d_[i].reshape(shp), m_[i].reshape(shp), v_[i].reshape(shp)
        grads[n] = grads[n].reshape(shp)
    return (loss_out, gx[None], *[grads[n] for n in names], *[delta[n] for n in names],
            *[new_m[n] for n in names], *[new_v[n] for n in names])
```

```python
import functools
import math

import numpy as np
import jax
import jax.numpy as jnp
from jax import lax
from jax.experimental import pallas as pl
from jax.experimental.pallas import tpu as pltpu

F32 = jnp.float32
BF16 = jnp.bfloat16
MESH = pl.DeviceIdType.MESH

N_DEV = 8
S = 2048
D = 1024
RET_HEADS = 4
RET_DK = 256
RET_DV = 512
CHUNK = 128
N_CHUNK = S // CHUNK
ATT_GROUPS = ((128, 1), (512, 4), (2048, 16))
ATT_HG = 4
ATT_DH = 128
ATT_BLK = 128
N_BUCKETS = 32
MAX_DIST = 2048
D_FF = 4096
IN_COLS = 12800
OFF_RQ, OFF_RK, OFF_RV, OFF_RG, OFF_ATT = 0, 1024, 2048, 4096, 6144
OFF_GA, OFF_GB = 6144, 7168
RMS_EPS = 1e-6
GN_EPS = 1e-5
ADAM_LR, ADAM_B1, ADAM_B2, ADAM_EPS, ADAM_WD, ADAM_STEP = 0.001, 0.9, 0.999, 1e-08, 0.01, 10
VMEM_LIMIT = 48 * 1024 * 1024


def _pcall(body, **kw):
    return pl.pallas_call(body, **kw)


def _params(sem=None):
    return pltpu.CompilerParams(dimension_semantics=sem, vmem_limit_bytes=VMEM_LIMIT)


HBM_SPEC = pl.BlockSpec(memory_space=pl.ANY)


def _carry(body, comm, *, name, grid, in_specs, out_specs, out_shape, scratch_shapes=()):
    single = not isinstance(out_specs, (tuple, list))
    o_specs = (out_specs,) if single else tuple(out_specs)
    o_shape = (out_shape,) if single else tuple(out_shape)
    n_in, n_out, n_scr = len(in_specs), len(o_specs), len(scratch_shapes)
    nci, nco = len(comm.ins), len(comm.out_shape)
    total = int(np.prod(grid))

    def wrapped(*refs):
        bounds = np.cumsum([0, n_in, nci, n_out, nco, n_scr])
        a, ci, o, co, scr = (refs[bounds[i]:bounds[i + 1]] for i in range(5))
        sems = refs[bounds[5]:]
        flat = 0
        for d, g in enumerate(grid):
            flat = flat * g + pl.program_id(d)

        @pl.when(flat == 0)
        def _():
            comm.start(ci, co, sems)

        body(*a, *o, *scr)

        @pl.when(flat == total - 1)
        def _():
            comm.finish(ci, co, sems)

    call = _pcall(wrapped, name=name, grid=grid, in_specs=list(in_specs) + [HBM_SPEC] * nci,
                  out_specs=o_specs + (HBM_SPEC,) * nco, out_shape=o_shape + tuple(comm.out_shape),
                  scratch_shapes=list(scratch_shapes) + list(comm.sems),
                  compiler_params=_params(("arbitrary",) * len(grid)))

    def run(*args):
        res = call(*args, *comm.ins)
        own = res[0] if single else tuple(res[:n_out])
        return own, tuple(res[n_out:])

    return run


def _run_comm(comm, name):
    nci, nco = len(comm.ins), len(comm.out_shape)

    def body(*refs):
        ci, co, sems = refs[:nci], refs[nci:nci + nco], refs[nci + nco:]
        comm.start(ci, co, sems)
        comm.finish(ci, co, sems)

    return _pcall(body, name=name, in_specs=[HBM_SPEC] * nci, out_specs=(HBM_SPEC,) * nco,
                  out_shape=tuple(comm.out_shape), scratch_shapes=list(comm.sems))(*comm.ins)


def _dot(a, b, dn):
    return lax.dot_general(a.astype(BF16), b.astype(BF16), (dn, ((), ())), preferred_element_type=F32)


NN = ((1,), (0,))
NT = ((1,), (1,))
TN = ((0,), (0,))


def _mm(a, b, mode, *, tm, tn, tk, name, out_dtype=F32, res=None, gvec=None, relu2=False, relu2_of=None, comm=None):
    if mode == 'nn':
        (M, K), (_, N) = a.shape, b.shape
        a_spec = pl.BlockSpec((tm, tk), lambda i, j, k: (i, k))
        b_spec = pl.BlockSpec((tk, tn), lambda i, j, k: (k, j))
        dn = NN
    elif mode == 'nt':
        (M, K), (N, _) = a.shape, b.shape
        a_spec = pl.BlockSpec((tm, tk), lambda i, j, k: (i, k))
        b_spec = pl.BlockSpec((tn, tk), lambda i, j, k: (j, k))
        dn = NT
    else:
        (K, M), (_, N) = a.shape, b.shape
        a_spec = pl.BlockSpec((tk, tm), lambda i, j, k: (k, i))
        b_spec = pl.BlockSpec((tk, tn), lambda i, j, k: (k, j))
        dn = TN
    assert M % tm == 0 and N % tn == 0 and K % tk == 0, (name, M, N, K)
    nk = K // tk
    fused = res is not None
    o_spec = pl.BlockSpec((tm, tn), lambda i, j, k: (i, j))

    def body(a_ref, b_ref, *rest):
        acc_ref = rest[-1] if nk > 1 else None
        if fused:
            res_ref, g_ref, o_ref, x_ref = rest[:4]
        elif relu2_of is not None:
            u_ref, o_ref = rest[:2]
        elif relu2:
            o_ref, act_ref = rest[:2]
        else:
            o_ref = rest[0]

        def finish(acc):
            if relu2_of is not None:
                acc = acc * (2.0 * jnp.maximum(u_ref[...], 0.0))
            o_ref[...] = acc.astype(o_ref.dtype)
            if fused:
                x_ref[...] = res_ref[...] + g_ref[...] * acc
            if relu2:
                r = jnp.maximum(acc, 0.0)
                act_ref[...] = (r * r).astype(BF16)

        p = _dot(a_ref[...], b_ref[...], dn)
        if nk == 1:
            finish(p)
        else:
            k = pl.program_id(2)

            @pl.when(k == 0)
            def _():
                acc_ref[...] = p

            @pl.when(k > 0)
            def _():
                acc_ref[...] += p

            @pl.when(k == nk - 1)
            def _():
                finish(acc_ref[...])

    in_specs = [a_spec, b_spec]
    args = [a, b]
    out_shape = jax.ShapeDtypeStruct((M, N), out_dtype)
    out_specs = o_spec
    if fused:
        in_specs += [pl.BlockSpec((tm, tn), lambda i, j, k: (i, j)), pl.BlockSpec((1, tn), lambda i, j, k: (0, j))]
        args += [res, gvec]
        out_shape = (out_shape, jax.ShapeDtypeStruct((M, N), F32))
        out_specs = (o_spec, pl.BlockSpec((tm, tn), lambda i, j, k: (i, j)))
    elif relu2_of is not None:
        in_specs.append(pl.BlockSpec((tm, tn), lambda i, j, k: (i, j)))
        args.append(relu2_of)
    elif relu2:
        out_shape = (out_shape, jax.ShapeDtypeStruct((M, N), BF16))
        out_specs = (o_spec, pl.BlockSpec((tm, tn), lambda i, j, k: (i, j)))
    kw = dict(name=name, grid=(M // tm, N // tn, nk), in_specs=in_specs, out_specs=out_specs,
              out_shape=out_shape, scratch_shapes=[pltpu.VMEM((tm, tn), F32)] if nk > 1 else [])
    if comm is not None:
        return _carry(body, comm, **kw)(*args)
    return _pcall(body, compiler_params=_params(("parallel", "parallel", "arbitrary")), **kw)(*args)


PROJ_TN = 512
ATT_T0, ATT_T1 = 6144 // PROJ_TN, 10752 // PROJ_TN
N_SLABS = (ATT_T1 - ATT_T0) * 4
MAIN_COLS = IN_COLS - (ATT_T1 - ATT_T0) * PROJ_TN


def _proj(h1, w_in_t, comm):
    nj = IN_COLS // PROJ_TN

    def body(a_ref, b_ref, main_ref, slab_ref):
        j = pl.program_id(1)
        is_att = (j >= ATT_T0) & (j < ATT_T1)
        chunks = [pl.ds(c * 512, 512) for c in range(S // 512)]

        @pl.when(jnp.logical_not(is_att))
        def _():
            for rows in chunks:
                main_ref[rows, :] = _dot(a_ref[rows, :], b_ref[...], NT)

        @pl.when(is_att)
        def _():
            for rows in chunks:
                p = _dot(a_ref[rows, :], b_ref[...], NT)
                for h in range(4):
                    slab_ref[h, rows, :] = p[:, h * 128:(h + 1) * 128]

    main_idx = lambda j: jnp.where(j < ATT_T0, j, jnp.where(j < ATT_T1, ATT_T0 - 1, j - (ATT_T1 - ATT_T0)))
    slab_idx = lambda j: jnp.clip(j - ATT_T0, 0, ATT_T1 - ATT_T0 - 1)
    (main, slabs), got = _carry(
        body, comm, name="proj", grid=(1, nj, 1),
        in_specs=[pl.BlockSpec((S, D), lambda i, j, k: (0, 0)), pl.BlockSpec((PROJ_TN, D), lambda i, j, k: (j, 0))],
        out_specs=(pl.BlockSpec((S, PROJ_TN), lambda i, j, k: (0, main_idx(j))),
                   pl.BlockSpec((4, S, 128), lambda i, j, k: (slab_idx(j), 0, 0))),
        out_shape=(jax.ShapeDtypeStruct((S, MAIN_COLS), F32), jax.ShapeDtypeStruct((N_SLABS, S, 128), F32)))(h1, w_in_t)
    return main, slabs, got


TR = 256


def _row_spec(w=D):
    return pl.BlockSpec((TR, w), lambda i: (i, 0))


def _vec_spec(w=D):
    return pl.BlockSpec((1, w), lambda i: (0, 0))


def _norm_mod_fwd(x, g, sh, sc, name):
    def body(x_ref, g_ref, sh_ref, sc_ref, o_ref):
        xv = x_ref[...]
        rstd = lax.rsqrt(jnp.mean(xv * xv, axis=-1, keepdims=True) + RMS_EPS)
        n = xv * rstd * g_ref[...]
        o_ref[...] = (n * (1.0 + sc_ref[...]) + sh_ref[...]).astype(BF16)

    return _pcall(body, name=name, grid=(S // TR,), in_specs=[_row_spec(), _vec_spec(), _vec_spec(), _vec_spec()],
                  out_specs=_row_spec(), out_shape=jax.ShapeDtypeStruct((S, D), BF16),
                  compiler_params=_params(("parallel",)))(x, g, sh, sc)


def _norm_mod_bwd(x, g, sc, dh, dres, name, gate=None):
    gated = gate is not None

    def body(x_ref, g_ref, sc_ref, dh_ref, dres_ref, *rest):
        if gated:
            f_ref, gv_ref, dx_ref, dsc_ref, dsh_ref, dg_ref, dz_ref, dgv_ref = rest
        else:
            dx_ref, dsc_ref, dsh_ref, dg_ref = rest
        i = pl.program_id(0)
        xv = x_ref[...]
        dh = dh_ref[...]
        rstd = lax.rsqrt(jnp.mean(xv * xv, axis=-1, keepdims=True) + RMS_EPS)
        xhat = xv * rstd
        gv = g_ref[...]
        dn = dh * (1.0 + sc_ref[...])
        dxhat = dn * gv
        dx = dres_ref[...] + rstd * (dxhat - xhat * jnp.mean(dxhat * xhat, axis=-1, keepdims=True))
        dx_ref[...] = dx
        sums = [(dsc_ref, jnp.sum(dh * (xhat * gv), axis=0, keepdims=True)),
                (dsh_ref, jnp.sum(dh, axis=0, keepdims=True)),
                (dg_ref, jnp.sum(dn * xhat, axis=0, keepdims=True))]
        if gated:
            dz_ref[...] = (dx * gv_ref[...]).astype(BF16)
            sums.append((dgv_ref, jnp.sum(dx * f_ref[...], axis=0, keepdims=True)))

        @pl.when(i == 0)
        def _():
            for ref, p in sums:
                ref[...] = p

        @pl.when(i > 0)
        def _():
            for ref, p in sums:
                ref[...] += p

    vec = jax.ShapeDtypeStruct((1, D), F32)
    in_specs = [_row_spec(), _vec_spec(), _vec_spec(), _row_spec(), _row_spec()]
    out_specs = [_row_spec(), _vec_spec(), _vec_spec(), _vec_spec()]
    out_shape = [jax.ShapeDtypeStruct((S, D), F32), vec, vec, vec]
    args = [x, g, sc, dh, dres]
    if gated:
        in_specs += [_row_spec(), _vec_spec()]
        out_specs += [_row_spec(), _vec_spec()]
        out_shape += [jax.ShapeDtypeStruct((S, D), BF16), vec]
        args += list(gate)
    return _pcall(body, name=name, grid=(S // TR,), in_specs=in_specs, out_specs=tuple(out_specs),
                  out_shape=tuple(out_shape), compiler_params=_params(("arbitrary",)))(*args)


def _final_loss(x2, tgt, g, f, g2):
    def body(x_ref, t_ref, g_ref, f_ref, g2_ref, loss_ref, dx_ref, dg_ref, df_ref, dg2_ref):
        i = pl.program_id(0)
        xv = x_ref[...]
        gv = g_ref[...]
        rstd = lax.rsqrt(jnp.mean(xv * xv, axis=-1, keepdims=True) + RMS_EPS)
        xhat = xv * rstd
        err = xhat * gv - t_ref[...]
        dy = err * (1.0 / D)
        dxhat = dy * gv
        dx = rstd * (dxhat - xhat * jnp.mean(dxhat * xhat, axis=-1, keepdims=True))
        dx_ref[...] = dx
        df_ref[...] = (dx * g2_ref[...]).astype(BF16)
        p_g = jnp.sum(dy * xhat, axis=0, keepdims=True)
        p_g2 = jnp.sum(dx * f_ref[...], axis=0, keepdims=True)
        p_l = jnp.zeros((1, 128), F32) + 0.5 * jnp.sum(jnp.mean(err * err, axis=-1, keepdims=True))

        @pl.when(i == 0)
        def _():
            dg_ref[...] = p_g
            dg2_ref[...] = p_g2
            loss_ref[...] = p_l

        @pl.when(i > 0)
        def _():
            dg_ref[...] += p_g
            dg2_ref[...] += p_g2
            loss_ref[...] += p_l

    vec = jax.ShapeDtypeStruct((1, D), F32)
    return _pcall(body, name="final_loss", grid=(S // TR,),
                  in_specs=[_row_spec(), _row_spec(), _vec_spec(), _row_spec(), _vec_spec()],
                  out_specs=(_vec_spec(128), _row_spec(), _vec_spec(), _row_spec(), _vec_spec()),
                  out_shape=(jax.ShapeDtypeStruct((1, 128), F32), jax.ShapeDtypeStruct((S, D), F32), vec,
                             jax.ShapeDtypeStruct((S, D), BF16), vec),
                  compiler_params=_params(("arbitrary",)))(x2, tgt, g, f, g2)


HALF = 512


def _merge_fwd(proj, ret_out, att_out):
    def body(ga_ref, gb_ref, r_ref, a_ref, o_ref):
        o_ref[...] = (jax.nn.sigmoid(ga_ref[...]) * r_ref[...] + jax.nn.sigmoid(gb_ref[...]) * a_ref[...]).astype(BF16)

    blk = lambda off: pl.BlockSpec((TR, HALF), lambda i, j: (i, off // HALF + j))
    return _pcall(body, name="merge_fwd", grid=(S // TR, D // HALF),
                  in_specs=[blk(OFF_GA), blk(OFF_GB), blk(0), blk(0)], out_specs=blk(0),
                  out_shape=jax.ShapeDtypeStruct((S, D), BF16),
                  compiler_params=_params(("parallel", "parallel")))(proj, proj, ret_out, att_out)


def _merge_bwd(proj, ret_out, att_out, dmerged):
    def body(ga_ref, gb_ref, r_ref, a_ref, dm_ref, dr_ref, da_ref, dga_ref, dgb_ref):
        sa = jax.nn.sigmoid(ga_ref[...])
        sb = jax.nn.sigmoid(gb_ref[...])
        dm = dm_ref[...]
        dr_ref[...] = (dm * sa).astype(BF16)
        da_ref[...] = (dm * sb).astype(BF16)
        dga_ref[...] = (dm * r_ref[...] * (sa * (1.0 - sa))).astype(BF16)
        dgb_ref[...] = (dm * a_ref[...] * (sb * (1.0 - sb))).astype(BF16)

    blk = lambda off: pl.BlockSpec((TR, HALF), lambda i, j: (i, off // HALF + j))
    o = jax.ShapeDtypeStruct((S, D), BF16)
    return _pcall(body, name="merge_bwd", grid=(S // TR, D // HALF),
                  in_specs=[blk(OFF_GA), blk(OFF_GB), blk(0), blk(0), blk(0)], out_specs=(blk(0),) * 4,
                  out_shape=(o, o, o, o),
                  compiler_params=_params(("parallel", "parallel")))(proj, proj, ret_out, att_out, dmerged)


def _ret_tables():
    H, C = RET_HEADS, CHUNK
    log_g = jnp.log1p(-(2.0 ** (-5.0 - jnp.arange(H, dtype=F32))))
    idx = jnp.arange(C, dtype=F32)
    rel = idx[:, None] - idx[None, :]
    inner = jnp.where(rel >= 0, jnp.exp(log_g[:, None, None] * jnp.maximum(rel, 0.0)), 0.0)
    qd = jnp.exp(log_g[:, None] * (idx + 1.0))[:, :, None]
    kd = jnp.exp(log_g[:, None] * (C - 1.0 - idx))[:, :, None]
    cd = jnp.broadcast_to(jnp.exp(log_g * C)[:, None, None], (H, 1, 128))
    half = RET_DK // 2
    inv = 10000.0 ** (-jnp.arange(half, dtype=F32) / half)
    ang = jnp.arange(S, dtype=F32)[:, None] * inv[None, :]
    return inner, qd, kd, cd, jnp.cos(ang), jnp.sin(ang)


def _rot(x, cos, sin):
    x1, x2 = x[:, :128], x[:, 128:]
    return jnp.concatenate([x1 * cos - x2 * sin, x1 * sin + x2 * cos], axis=1)


def _rot_t(d, cos, sin):
    d1, d2 = d[:, :128], d[:, 128:]
    return jnp.concatenate([d1 * cos + d2 * sin, d2 * cos - d1 * sin], axis=1)


RET_COLS = OFF_ATT
RET_VW = RET_HEADS * RET_DV


def _ret_specs(chunk_of):
    ci = chunk_of
    whole = lambda shape: pl.BlockSpec(shape, lambda t: (0,) * len(shape))
    return [
        pl.BlockSpec((CHUNK, RET_COLS), lambda t: (ci(t), 0)),
        pl.BlockSpec((CHUNK, 128), lambda t: (ci(t), 0)),
        pl.BlockSpec((CHUNK, 128), lambda t: (ci(t), 0)),
        whole((RET_HEADS, CHUNK, CHUNK)), whole((RET_HEADS, CHUNK, 1)), whole((RET_HEADS, CHUNK, 1)),
        whole((RET_HEADS, 1, 128)), whole((1, RET_VW)), whole((1, RET_VW)),
    ]


def _ret_cols(h):
    q = slice(OFF_RQ + h * RET_DK, OFF_RQ + (h + 1) * RET_DK)
    k = slice(OFF_RK + h * RET_DK, OFF_RK + (h + 1) * RET_DK)
    v = slice(OFF_RV + h * RET_DV, OFF_RV + (h + 1) * RET_DV)
    g = slice(OFF_RG + h * RET_DV, OFF_RG + (h + 1) * RET_DV)
    return q, k, v, g, slice(h * RET_DV, (h + 1) * RET_DV)


def _ret_fwd(proj, tables, gn_g, gn_b, comm=None):
    inner, qd, kd, cd, cos, sin = tables

    def body(x_ref, cos_ref, sin_ref, in_ref, qd_ref, kd_ref, cd_ref, g_ref, b_ref,
             gated_ref, ro_ref, st_ref, s_scr):
        i = pl.program_id(0)

        @pl.when(i == 0)
        def _():
            s_scr[...] = jnp.zeros_like(s_scr)

        cosv, sinv = cos_ref[...], sin_ref[...]
        for h in range(RET_HEADS):
            cq, ck, cv, cg, co = _ret_cols(h)
            q = _rot(x_ref[:, cq], cosv, sinv)
            k = _rot(x_ref[:, ck], cosv, sinv) * (RET_DK ** -0.5)
            v = x_ref[:, cv]
            st = s_scr[h]
            st_ref[h] = st
            s = _dot(q, k, NT) * in_ref[h]
            o = _dot(s, v, NN) + _dot(q, st, NN) * qd_ref[h]
            s_scr[h] = st * cd_ref[h, :, :1] + _dot(k * kd_ref[h], v, TN)
            ro_ref[:, co] = o
            mu = jnp.mean(o, axis=-1, keepdims=True)
            oc = o - mu
            var = jnp.mean(oc * oc, axis=-1, keepdims=True)
            rn = oc * lax.rsqrt(var + GN_EPS) * g_ref[:, co] + b_ref[:, co]
            rg = x_ref[:, cg]
            gated_ref[:, co] = (rg * jax.nn.sigmoid(rg) * rn).astype(BF16)

    ospec = pl.BlockSpec((CHUNK, RET_VW), lambda t: (t, 0))
    kw = dict(name="ret_fwd", grid=(N_CHUNK,), in_specs=_ret_specs(lambda t: t),
              out_specs=(ospec, ospec, pl.BlockSpec((RET_HEADS, None, RET_DK, RET_DV), lambda t: (0, t, 0, 0))),
              out_shape=(jax.ShapeDtypeStruct((S, RET_VW), BF16), jax.ShapeDtypeStruct((S, RET_VW), F32),
                         jax.ShapeDtypeStruct((RET_HEADS, N_CHUNK, RET_DK, RET_DV), F32)),
              scratch_shapes=[pltpu.VMEM((RET_HEADS, RET_DK, RET_DV), F32)])
    args = (proj, cos, sin, inner, qd, kd, cd, gn_g, gn_b)
    if comm is not None:
        return _carry(body, comm, **kw)(*args)
    return _pcall(body, compiler_params=_params(("arbitrary",)), **kw)(*args)


def _ret_bwd(proj, tables, gn_g, gn_b, ro, states, dgated, comm=None):
    inner, qd, kd, cd, cos, sin = tables
    last = N_CHUNK - 1

    def body(x_ref, cos_ref, sin_ref, in_ref, qd_ref, kd_ref, cd_ref, g_ref, b_ref, ro_ref, st_ref, dg_ref,
             dx_ref, gg_ref, gb_ref, gs_scr):
        t = pl.program_id(0)

        @pl.when(t == 0)
        def _():
            gs_scr[...] = jnp.zeros_like(gs_scr)
            gg_ref[...] = jnp.zeros_like(gg_ref)
            gb_ref[...] = jnp.zeros_like(gb_ref)

        cosv, sinv = cos_ref[...], sin_ref[...]
        for h in range(RET_HEADS):
            cq, ck, cv, cg, co = _ret_cols(h)
            q = _rot(x_ref[:, cq], cosv, sinv)
            k = _rot(x_ref[:, ck], cosv, sinv) * (RET_DK ** -0.5)
            v = x_ref[:, cv]
            qdv, kdv, dm = qd_ref[h], kd_ref[h], in_ref[h]
            st = st_ref[h]
            o = ro_ref[:, co]
            gv = g_ref[:, co]
            mu = jnp.mean(o, axis=-1, keepdims=True)
            oc = o - mu
            rstd = lax.rsqrt(jnp.mean(oc * oc, axis=-1, keepdims=True) + GN_EPS)
            ohat = oc * rstd
            rn = ohat * gv + b_ref[:, co]
            rg = x_ref[:, cg]
            sg = jax.nn.sigmoid(rg)
            dgt = dg_ref[:, co]
            drn = dgt * (rg * sg)
            dx_ref[:, cg] = (dgt * rn * (sg * (1.0 + rg * (1.0 - sg)))).astype(BF16)
            gg_ref[:, co] += jnp.sum(drn * ohat, axis=0, keepdims=True)
            gb_ref[:, co] += jnp.sum(drn, axis=0, keepdims=True)
            dohat = drn * gv
            do = rstd * (dohat - jnp.mean(dohat, axis=-1, keepdims=True)
                         - ohat * jnp.mean(dohat * ohat, axis=-1, keepdims=True))
            gs = gs_scr[h]
            s = _dot(q, k, NT) * dm
            dsr = _dot(do, v, NT) * dm
            dq = _dot(dsr, k, NN) + _dot(do, st, NT) * qdv
            dk = _dot(dsr, q, TN) + _dot(v, gs, NT) * kdv
            dv = _dot(s, do, TN) + _dot(k * kdv, gs, NN)
            gs_scr[h] = gs * cd_ref[h, :, :1] + _dot(q * qdv, do, TN)
            dx_ref[:, cq] = _rot_t(dq, cosv, sinv).astype(BF16)
            dx_ref[:, ck] = (_rot_t(dk, cosv, sinv) * (RET_DK ** -0.5)).astype(BF16)
            dx_ref[:, cv] = dv.astype(BF16)

    rev = lambda t: last - t
    vblk = pl.BlockSpec((CHUNK, RET_VW), lambda t: (rev(t), 0))
    vspec = pl.BlockSpec((1, RET_VW), lambda t: (0, 0))
    kw = dict(name="ret_bwd", grid=(N_CHUNK,),
              in_specs=_ret_specs(rev) + [vblk, pl.BlockSpec((RET_HEADS, None, RET_DK, RET_DV),
                                                             lambda t: (0, rev(t), 0, 0)), vblk],
              out_specs=(pl.BlockSpec((CHUNK, RET_COLS), lambda t: (rev(t), 0)), vspec, vspec),
              out_shape=(jax.ShapeDtypeStruct((S, RET_COLS), BF16), jax.ShapeDtypeStruct((1, RET_VW), F32),
                         jax.ShapeDtypeStruct((1, RET_VW), F32)),
              scratch_shapes=[pltpu.VMEM((RET_HEADS, RET_DK, RET_DV), F32)])
    args = (proj, cos, sin, inner, qd, kd, cd, gn_g, gn_b, ro, states, dgated)
    if comm is not None:
        return _carry(body, comm, **kw)(*args)
    return _pcall(body, compiler_params=_params(("arbitrary",)), **kw)(*args)


def _bucket_tables():
    qi = np.arange(ATT_BLK)[:, None]
    kj = np.arange(2 * ATT_BLK)[None, :]
    m = ATT_BLK + qi - kj
    out = []
    for win, dil in ATT_GROUPS:
        w = win // dil
        dist = (np.clip(m, 0, w) * dil).astype(np.int32)
        max_exact = N_BUCKETS // 2
        d_f = np.maximum(dist, 1).astype(np.float32)
        large = max_exact + (np.log(d_f / np.float32(max_exact)) / np.float32(math.log(MAX_DIST / max_exact))
                             * np.float32(N_BUCKETS - max_exact)).astype(np.int32)
        large = np.minimum(large, N_BUCKETS - 1)
        out.append(np.where(dist < max_exact, dist, large).astype(np.int32))
    return np.stack(out)


def _bias_build(rel_bias, buckets):
    def body(tab_ref, bk_ref, o_ref):
        hh = pl.program_id(0)
        bk = bk_ref[...]
        acc = jnp.zeros((ATT_BLK, 2 * ATT_BLK), F32)
        for b in range(N_BUCKETS):
            acc = jnp.where(bk == b, tab_ref[b, hh], acc)
        o_ref[...] = acc

    nh = len(ATT_GROUPS) * ATT_HG
    return _pcall(body, name="bias_build", grid=(nh,),
                  in_specs=[pl.BlockSpec(memory_space=pltpu.SMEM),
                            pl.BlockSpec((None, ATT_BLK, 2 * ATT_BLK), lambda hh: (hh // ATT_HG, 0, 0))],
                  out_specs=pl.BlockSpec((None, ATT_BLK, 2 * ATT_BLK), lambda hh: (hh, 0, 0)),
                  out_shape=jax.ShapeDtypeStruct((nh, ATT_BLK, 2 * ATT_BLK), F32),
                  compiler_params=_params(("parallel",)))(rel_bias, buckets)


def _bias_grad(ds_sum, buckets):
    def body(ds_ref, bk_ref, o_ref):
        bk = bk_ref[...]
        ds = ds_ref[...]
        rows = lax.broadcasted_iota(jnp.int32, (N_BUCKETS, 128), 0)
        acc = jnp.zeros((N_BUCKETS, 128), F32)
        for b in range(N_BUCKETS):
            acc = jnp.where(rows == b, jnp.sum(jnp.where(bk == b, ds, 0.0)), acc)
        o_ref[...] = acc

    nh = len(ATT_GROUPS) * ATT_HG
    return _pcall(body, name="bias_grad", grid=(nh,),
                  in_specs=[pl.BlockSpec((None, ATT_BLK, 2 * ATT_BLK), lambda hh: (hh, 0, 0)),
                            pl.BlockSpec((None, ATT_BLK, 2 * ATT_BLK), lambda hh: (hh // ATT_HG, 0, 0))],
                  out_specs=pl.BlockSpec((None, N_BUCKETS, 128), lambda hh: (hh, 0, 0)),
                  out_shape=jax.ShapeDtypeStruct((nh, N_BUCKETS, 128), F32),
                  compiler_params=_params(("parallel",)))(ds_sum, buckets)


def _att_valid(n):
    qi = lax.broadcasted_iota(jnp.int32, (ATT_BLK, 2 * ATT_BLK), 0)
    kj = lax.broadcasted_iota(jnp.int32, (ATT_BLK, 2 * ATT_BLK), 1)
    m = ATT_BLK + qi - kj
    first_key = jnp.where(n > 0, 0, ATT_BLK)
    return (m >= 0) & (m <= ATT_BLK) & (kj >= first_key)


ATT_HP = (1, 2, 2)


def _att_geometry(gi):
    _, dil = ATT_GROUPS[gi]
    return dil, S // dil // ATT_BLK, ATT_HP[gi]


def _blk(dil, r, n):
    if dil == 1:
        return pl.ds(n * ATT_BLK, ATT_BLK)
    return pl.ds(r + n * ATT_BLK * dil, ATT_BLK, stride=dil)


def _slab_specs(gi):
    _, _, hp = _att_geometry(gi)
    per = ATT_HG // hp
    return [pl.BlockSpec((hp, S, ATT_DH), lambda g, r, part=part: ((3 * gi + part) * per + g, 0, 0))
            for part in range(3)]


def _head_specs(gi, count):
    _, _, hp = _att_geometry(gi)
    return [pl.BlockSpec((hp, S, ATT_DH), lambda g, r: (g, 0, 0))] * count


def _bias_spec(gi):
    _, _, hp = _att_geometry(gi)
    return pl.BlockSpec((hp, ATT_BLK, 2 * ATT_BLK), lambda g, r: (gi * (ATT_HG // hp) + g, 0, 0))


def _att_fwd(slabs, bias, gi, comm=None):
    dil, nb, hp = _att_geometry(gi)
    scale = ATT_DH ** -0.5

    def body(q_ref, k_ref, v_ref, bias_ref, o_ref, l_ref):
        r = pl.program_id(1)
        for n in range(nb):
            valid = _att_valid(n)
            prev = _blk(dil, r, max(n - 1, 0))
            cur = _blk(dil, r, n)
            for h in range(hp):
                kk = jnp.concatenate([k_ref[h, prev, :], k_ref[h, cur, :]], axis=0)
                vv = jnp.concatenate([v_ref[h, prev, :], v_ref[h, cur, :]], axis=0)
                s = _dot(q_ref[h, cur, :], kk, NT) * scale + bias_ref[h]
                s = jnp.where(valid, s, -1e30)
                mx = jnp.max(s, axis=-1, keepdims=True)
                e = jnp.exp(s - mx)
                den = jnp.sum(e, axis=-1, keepdims=True)
                o_ref[h, cur, :] = _dot(e / den, vv, NN)
                l_ref[h, cur, :] = jnp.broadcast_to(mx + jnp.log(den), (ATT_BLK, ATT_DH))

    osh = jax.ShapeDtypeStruct((ATT_HG, S, ATT_DH), F32)
    kw = dict(name=f"att_fwd{gi}", grid=(ATT_HG // hp, dil), in_specs=_slab_specs(gi) + [_bias_spec(gi)],
              out_specs=tuple(_head_specs(gi, 2)), out_shape=(osh, osh))
    if comm is not None:
        return _carry(body, comm, **kw)(slabs, slabs, slabs, bias)
    return _pcall(body, compiler_params=_params(("parallel", "arbitrary")), **kw)(slabs, slabs, slabs, bias)


def _att_bwd(slabs, bias, o, lse, do, dlse, gi, comm=None):
    dil, nb, hp = _att_geometry(gi)
    per = ATT_HG // hp
    scale = ATT_DH ** -0.5
    wh = hp * ATT_DH
    wide = lambda t: jnp.concatenate([t, t], axis=1)

    def body(q_ref, k_ref, v_ref, bias_ref, o_ref, l_ref, do_ref, dl_ref, dq_ref, dk_ref, dv_ref, ds_ref):
        r = pl.program_id(1)

        @pl.when(r == 0)
        def _():
            ds_ref[...] = jnp.zeros_like(ds_ref)

        for h in range(hp):
            sl = slice(h * ATT_DH, (h + 1) * ATT_DH)
            carry_k = carry_v = None
            for n in range(nb):
                valid = _att_valid(n)
                prev = _blk(dil, r, max(n - 1, 0))
                cur = _blk(dil, r, n)
                q = q_ref[h, cur, :]
                kk = jnp.concatenate([k_ref[h, prev, :], k_ref[h, cur, :]], axis=0)
                vv = jnp.concatenate([v_ref[h, prev, :], v_ref[h, cur, :]], axis=0)
                dov = do_ref[h, cur, :]
                s = _dot(q, kk, NT) * scale + bias_ref[h]
                p = jnp.where(valid, jnp.exp(s - wide(l_ref[h, cur, :])), 0.0)
                dp = _dot(dov, vv, NT)
                delta = jnp.sum(dov * o_ref[h, cur, :], axis=-1, keepdims=True)
                ds = p * (dp - delta + wide(dl_ref[h, cur, :]))
                ds_ref[h] += ds
                out_rows = pl.ds(n * ATT_BLK, ATT_BLK)
                dq_ref[out_rows, sl] = (_dot(ds, kk, NN) * scale).astype(BF16)
                dkk = _dot(ds, q, TN) * scale
                dvv = _dot(p, dov, TN)
                if n > 0:
                    before = pl.ds((n - 1) * ATT_BLK, ATT_BLK)
                    dk_ref[before, sl] = (carry_k + dkk[:ATT_BLK]).astype(BF16)
                    dv_ref[before, sl] = (carry_v + dvv[:ATT_BLK]).astype(BF16)
                carry_k, carry_v = dkk[ATT_BLK:], dvv[ATT_BLK:]
            last = pl.ds((nb - 1) * ATT_BLK, ATT_BLK)
            dk_ref[last, sl] = carry_k.astype(BF16)
            dv_ref[last, sl] = carry_v.astype(BF16)

    out_spec = pl.BlockSpec((S // dil, wh), lambda g, r: (0, r * per + g))
    osh = jax.ShapeDtypeStruct((S // dil, dil * AW), BF16)
    kw = dict(name=f"att_bwd{gi}", grid=(per, dil), in_specs=_slab_specs(gi) + [_bias_spec(gi)] + _head_specs(gi, 4),
              out_specs=(out_spec, out_spec, out_spec,
                         pl.BlockSpec((hp, ATT_BLK, 2 * ATT_BLK), lambda g, r: (g, 0, 0))),
              out_shape=(osh, osh, osh, jax.ShapeDtypeStruct((ATT_HG, ATT_BLK, 2 * ATT_BLK), F32)))
    args = (slabs, slabs, slabs, bias, o, lse, do, dlse)
    if comm is not None:
        return _carry(body, comm, **kw)(*args)
    return _pcall(body, compiler_params=_params(("arbitrary", "arbitrary")), **kw)(*args)


AW = ATT_HG * ATT_DH


def _mix_weights(l0, l1, l2):
    mx = jnp.maximum(jnp.maximum(l0, l1), l2)
    e0, e1, e2 = jnp.exp(l0 - mx), jnp.exp(l1 - mx), jnp.exp(l2 - mx)
    den = e0 + e1 + e2
    return e0 / den, e1 / den, e2 / den


def _heads_spec():
    return pl.BlockSpec((ATT_HG, TR, ATT_DH), lambda i: (0, i, 0))


def _mix_fwd(os_, ls):
    def body(o0, o1, o2, l0, l1, l2, att_ref):
        for h in range(ATT_HG):
            w0, w1, w2 = _mix_weights(l0[h], l1[h], l2[h])
            att_ref[:, h * ATT_DH:(h + 1) * ATT_DH] = (w0 * o0[h] + w1 * o1[h] + w2 * o2[h]).astype(BF16)

    return _pcall(body, name="mix_fwd", grid=(S // TR,), in_specs=[_heads_spec()] * 6, out_specs=_row_spec(AW),
                  out_shape=jax.ShapeDtypeStruct((S, AW), BF16), compiler_params=_params(("parallel",)))(*os_, *ls)


def _mix_bwd(os_, ls, datt):
    def body(o0, o1, o2, l0, l1, l2, da_ref, d0, d1, d2, e0, e1, e2):
        for h in range(ATT_HG):
            ws = _mix_weights(l0[h], l1[h], l2[h])
            da = da_ref[:, h * ATT_DH:(h + 1) * ATT_DH]
            dws = []
            for o_ref, w, d_ref in zip((o0, o1, o2), ws, (d0, d1, d2)):
                d_ref[h] = w * da
                dws.append(jnp.broadcast_to(jnp.sum(da * o_ref[h], axis=-1, keepdims=True), (TR, ATT_DH)))
            tot = ws[0] * dws[0] + ws[1] * dws[1] + ws[2] * dws[2]
            for w, dw, e_ref in zip(ws, dws, (e0, e1, e2)):
                e_ref[h] = w * (dw - tot)

    o = jax.ShapeDtypeStruct((ATT_HG, S, ATT_DH), F32)
    return _pcall(body, name="mix_bwd", grid=(S // TR,), in_specs=[_heads_spec()] * 6 + [_row_spec(AW)],
                  out_specs=(_heads_spec(),) * 6, out_shape=(o,) * 6,
                  compiler_params=_params(("parallel",)))(*os_, *ls, datt)


def _ada_fwd(c_all, w_sh, b_sl):
    def body(c_ref, w_ref, b_ref, o_ref):
        cv = c_ref[...]
        o_ref[...] = _dot(cv * jax.nn.sigmoid(cv), w_ref[...], NN) + b_ref[...]

    return _pcall(body, name="ada_fwd", out_shape=jax.ShapeDtypeStruct((N_DEV, w_sh.shape[1]), F32),
                  compiler_params=_params())(c_all, w_sh, b_sl)


def _ada_bwd(c_all, dm_sl):
    def body(c_ref, d_ref, o_ref):
        cv = c_ref[...]
        o_ref[...] = _dot(cv * jax.nn.sigmoid(cv), d_ref[...], TN)

    return _pcall(body, name="ada_bwd", out_shape=jax.ShapeDtypeStruct((D, dm_sl.shape[1]), F32),
                  compiler_params=_params())(c_all, dm_sl)


N_MOD = 6


def _sum_small(gathered):
    n = len(gathered)

    def body(*refs):
        ins, (gb_ref, dm_ref), outs = refs[:n], refs[n:n + 2], refs[n + 2:]

        def total(r):
            acc = r[0]
            for e in range(1, N_DEV):
                acc = acc + r[e]
            return acc

        for i in range(N_MOD):
            cols = slice(i * D, (i + 1) * D)
            gb_ref[:, cols] = total(ins[i])
            for e in range(N_DEV):
                dm_ref[e:e + 1, cols] = ins[i][e]
        for r, o_ref in zip(ins[N_MOD:], outs):
            o_ref[...] = total(r)

    shapes = (jax.ShapeDtypeStruct((1, N_MOD * D), F32), jax.ShapeDtypeStruct((N_DEV, N_MOD * D), F32),
              *[jax.ShapeDtypeStruct(g.shape[1:], F32) for g in gathered[N_MOD:]])
    res = _pcall(body, name="sum_small", out_shape=shapes, compiler_params=_params())(*gathered)
    return res[0], res[1], res[2:]


def _row_tile(m, n):
    t = max(8, min(m, (1 << 19) // n // 8 * 8))
    while m % t:
        t -= 8
    return t


def _pair_sum(full, recv, sel, name):
    _, _, m, n = full.shape
    t = _row_tile(m, n)

    def body(sel_ref, a_ref, b_ref, o_ref):
        o_ref[...] = (a_ref[...].astype(F32) + b_ref[...].astype(F32)).astype(o_ref.dtype)

    gs = pltpu.PrefetchScalarGridSpec(
        num_scalar_prefetch=1, grid=(4, m // t),
        in_specs=[pl.BlockSpec((None, None, t, n), lambda q, i, s: (q, s[0], i, 0)),
                  pl.BlockSpec((None, t, n), lambda q, i, s: (q, i, 0))],
        out_specs=pl.BlockSpec((None, t, n), lambda q, i, s: (q, i, 0)))
    return _pcall(body, name=name, grid_spec=gs, out_shape=jax.ShapeDtypeStruct((4, m, n), full.dtype),
                  compiler_params=_params(("parallel", "parallel")))(sel, full, recv)


def _chip_sum(part, recv, sel, name):
    _, m, n = part.shape
    t = _row_tile(m, n)

    def body(sel_ref, a_ref, r_ref, o_ref):
        o_ref[...] = ((a_ref[...].astype(F32) + r_ref[0].astype(F32)) + r_ref[1].astype(F32)) + r_ref[2].astype(F32)

    gs = pltpu.PrefetchScalarGridSpec(
        num_scalar_prefetch=1, grid=(m // t,),
        in_specs=[pl.BlockSpec((None, t, n), lambda i, s: (s[0], i, 0)),
                  pl.BlockSpec((3, t, n), lambda i, s: (0, i, 0))],
        out_specs=pl.BlockSpec((t, n), lambda i, s: (i, 0)))
    return _pcall(body, name=name, grid_spec=gs, out_shape=jax.ShapeDtypeStruct((m, n), F32),
                  compiler_params=_params(("parallel",)))(sel, part, recv)


def _adamw_math(w, g, m, v):
    nm = ADAM_B1 * m + (1.0 - ADAM_B1) * g
    nv = ADAM_B2 * v + (1.0 - ADAM_B2) * (g * g)
    m_hat = nm / (1.0 - ADAM_B1 ** ADAM_STEP)
    v_hat = nv / (1.0 - ADAM_B2 ** ADAM_STEP)
    return -ADAM_LR * (m_hat / (jnp.sqrt(v_hat) + ADAM_EPS) + ADAM_WD * w), nm, nv


def _adamw(w, g, m, v, name):
    _, rows, cols = w.shape
    t = _row_tile(rows, cols)

    def body(w_ref, g_ref, m_ref, v_ref, d_ref, nm_ref, nv_ref):
        d_ref[...], nm_ref[...], nv_ref[...] = _adamw_math(w_ref[...], g_ref[...], m_ref[...], v_ref[...])

    spec3 = pl.BlockSpec((None, t, cols), lambda i: (0, i, 0))
    spec2 = pl.BlockSpec((t, cols), lambda i: (i, 0))
    o = jax.ShapeDtypeStruct(w.shape, F32)
    return _pcall(body, name=name, grid=(rows // t,), in_specs=[spec3, spec2, spec3, spec3], out_specs=(spec3,) * 3,
                  out_shape=(o, o, o), compiler_params=_params(("parallel",)))(w, g, m, v)


def _adamw_small(ws, gs, ms, vs):
    n = len(ws)

    def body(*refs):
        for i in range(n):
            w_ref, g_ref, m_ref, v_ref = (refs[k * n + i] for k in range(4))
            d, nm, nv = _adamw_math(w_ref[...], g_ref[...], m_ref[...], v_ref[...])
            refs[4 * n + i][...] = d
            refs[5 * n + i][...] = nm
            refs[6 * n + i][...] = nv

    shapes = tuple(jax.ShapeDtypeStruct(w.shape, F32) for w in ws)
    res = _pcall(body, name="adamw_small", out_shape=shapes * 3, compiler_params=_params())(*ws, *gs, *ms, *vs)
    return res[:n], res[n:2 * n], res[2 * n:]


def _mesh_pos():
    return lax.axis_index("x"), lax.axis_index("y"), lax.axis_index("c")


class _Gather:
    def __init__(self, arrs):
        self.ins = list(arrs)
        na = self.na = len(arrs)
        self.out_shape = tuple(jax.ShapeDtypeStruct((N_DEV,) + a.shape, a.dtype) for a in arrs)
        self.sems = [pltpu.SemaphoreType.DMA((7 * na,)), pltpu.SemaphoreType.DMA((7 * na,)),
                     pltpu.SemaphoreType.DMA((na,))]

    def _copies(self, ins, outs, sems):
        send_sems, recv_sems, local_sems = sems
        x, y, c = _mesh_pos()
        me, sibling = (x, y, c), (x, y, 1 - c)
        chips = [(1 - x, y), (x, 1 - y), (1 - x, 1 - y)]

        def slot(p):
            return 4 * p[0] + 2 * p[1] + p[2]

        def copy(a, k, block, to, src=None):
            dst = outs[a].at[slot(block)]
            return pltpu.make_async_remote_copy(
                src_ref=dst if src is None else src, dst_ref=dst, send_sem=send_sems.at[7 * a + k],
                recv_sem=recv_sems.at[7 * a + k], device_id=to, device_id_type=MESH)

        mine = [pltpu.make_async_copy(ins[a], outs[a].at[slot(me)], local_sems.at[a]) for a in range(self.na)]
        first = []
        for a in range(self.na):
            first.append(copy(a, 0, me, sibling, src=ins[a]))
            first += [copy(a, 1 + j, me, (*chip, c), src=ins[a]) for j, chip in enumerate(chips)]
        return me, sibling, chips, c, copy, mine, first

    def start(self, ins, outs, sems):
        *_, mine, first = self._copies(ins, outs, sems)
        for cp in mine + first:
            cp.start()

    def finish(self, ins, outs, sems):
        me, sibling, chips, c, copy, mine, first = self._copies(ins, outs, sems)
        passed = []
        for j, chip in enumerate(chips):
            for a in range(self.na):
                copy(a, 1 + j, (*chip, c), me).wait_recv()
                cp = copy(a, 4 + j, (*chip, c), sibling)
                cp.start()
                passed.append(cp)
        for a in range(self.na):
            copy(a, 0, sibling, me).wait_recv()
            for j, chip in enumerate(chips):
                copy(a, 4 + j, (*chip, 1 - c), me).wait_recv()
        for cp in first + passed:
            cp.wait_send()
        for cp in mine:
            cp.wait()


class _ExchangeCore:
    def __init__(self, fulls):
        self.ins = list(fulls)
        self.out_shape = tuple(jax.ShapeDtypeStruct((4,) + f.shape[2:], f.dtype) for f in fulls)
        self.sems = [pltpu.SemaphoreType.DMA((4 * len(fulls),)), pltpu.SemaphoreType.DMA((4 * len(fulls),))]

    def _copies(self, ins, outs, sems):
        send_sems, recv_sems = sems
        x, y, c = _mesh_pos()
        return [pltpu.make_async_remote_copy(
            src_ref=ins[a].at[q, 1 - c], dst_ref=outs[a].at[q], send_sem=send_sems.at[4 * a + q],
            recv_sem=recv_sems.at[4 * a + q], device_id=(x, y, 1 - c), device_id_type=MESH)
            for a in range(len(self.ins)) for q in range(4)]

    def start(self, ins, outs, sems):
        for cp in self._copies(ins, outs, sems):
            cp.start()

    def finish(self, ins, outs, sems):
        for cp in self._copies(ins, outs, sems):
            cp.wait()


class _ExchangeChip:
    def __init__(self, parts):
        self.ins = list(parts)
        self.out_shape = tuple(jax.ShapeDtypeStruct((3,) + p.shape[1:], p.dtype) for p in parts)
        self.sems = [pltpu.SemaphoreType.DMA((3 * len(parts),)), pltpu.SemaphoreType.DMA((3 * len(parts),))]

    def _copies(self, ins, outs, sems):
        send_sems, recv_sems = sems
        x, y, c = _mesh_pos()
        chips = [(1 - x, y), (x, 1 - y), (1 - x, 1 - y)]
        return [pltpu.make_async_remote_copy(
            src_ref=ins[a].at[2 * px + py], dst_ref=outs[a].at[j], send_sem=send_sems.at[3 * a + j],
            recv_sem=recv_sems.at[3 * a + j], device_id=(px, py, c), device_id_type=MESH)
            for a in range(len(self.ins)) for j, (px, py) in enumerate(chips)]

    def start(self, ins, outs, sems):
        for cp in self._copies(ins, outs, sems):
            cp.start()

    def finish(self, ins, outs, sems):
        for cp in self._copies(ins, outs, sems):
            cp.wait()


def _reduce_sums(fulls, recv_core, core, tag):
    return [_pair_sum(f, r, core, f"rs_pair_{tag}{i}") for i, (f, r) in enumerate(zip(fulls, recv_core))]


def _local_step(x, tgt, mods, w_in_t, shards, small, chip, core):
    sh1, sc1, g1, sh2, sc2, g2 = mods
    norm1_g, rel_bias, gn_g, gn_b, norm2_g, norm_f_g = small
    tables = _ret_tables()
    buckets = jnp.asarray(_bucket_tables())

    h1 = _norm_mod_fwd(x, norm1_g, sh1, sc1, "norm1_fwd")
    proj, slabs, gathered = _proj(h1, w_in_t, _Gather(shards[:3]))
    w_ret_out, w_att_out, w_o = (_from_slots(g, ax) for g, ax in zip(gathered, BIG_AXES[1:4]))
    (gated, ro, states), (w_ff1_8,) = _ret_fwd(proj, tables, gn_g, gn_b, comm=_Gather(shards[3:4]))
    w_ff1 = _from_slots(w_ff1_8, BIG_AXES[4])
    bias = _bias_build(rel_bias, buckets)
    outs, lses = [], []
    for gi in range(len(ATT_GROUPS)):
        res = _att_fwd(slabs, bias, gi, comm=_Gather(shards[4:]) if gi == 2 else None)
        if gi == 2:
            res, (w_ff2_8,) = res
        outs.append(res[0])
        lses.append(res[1])
    w_ff2 = _from_slots(w_ff2_8, BIG_AXES[5])
    att = _mix_fwd(outs, lses)
    ret_out = _mm(gated, w_ret_out, 'nn', tm=S, tn=256, tk=2048, name="ret_out")
    att_out = _mm(att, w_att_out, 'nn', tm=S, tn=512, tk=AW, name="att_out")
    merged = _merge_fwd(proj, ret_out, att_out)
    mixo, x1 = _mm(merged, w_o, 'nn', tm=S, tn=256, tk=D, name="w_o", res=x, gvec=g1)
    h2 = _norm_mod_fwd(x1, norm2_g, sh2, sc2, "norm2_fwd")
    u, act = _mm(h2, w_ff1, 'nn', tm=S, tn=512, tk=D, name="ff1", relu2=True)
    f, x2 = _mm(act, w_ff2, 'nn', tm=1024, tn=512, tk=2048, name="ff2", res=x1, gvec=g2)
    loss, dx2, g_normf, df, dg2 = _final_loss(x2, tgt, norm_f_g, f, g2)

    gw_ff2 = _mm(act, df, 'tn', tm=512, tn=D, tk=S, name="gw_ff2", out_dtype=BF16)
    du = _mm(df, w_ff2, 'nt', tm=S, tn=512, tk=D, name="d_act", out_dtype=BF16, relu2_of=u)
    gw_ff1 = _mm(h2, du, 'tn', tm=D, tn=512, tk=S, name="gw_ff1", out_dtype=BF16)
    fulls_a = [_to_slots(g, ax) for g, ax in zip((gw_ff1, gw_ff2), BIG_AXES[4:])]
    dh2, recv_core_a = _mm(du, w_ff1, 'nt', tm=1024, tn=1024, tk=2048, name="dh2", comm=_ExchangeCore(fulls_a))
    parts_a = _reduce_sums(fulls_a, recv_core_a, core, "a")
    dx1, dsc2, dsh2, g_norm2, dmixo, dg1 = _norm_mod_bwd(x1, norm2_g, sc2, dh2, dx2, "norm2_bwd", gate=(mixo, g1))

    gw_o = _mm(merged, dmixo, 'tn', tm=D, tn=512, tk=S, name="gw_o", out_dtype=BF16)
    dmerged = _mm(dmixo, w_o, 'nt', tm=S, tn=512, tk=D, name="dmerged")
    d_ret_out, d_att_out, dga, dgb = _merge_bwd(proj, ret_out, att_out, dmerged)
    gw_ret_out = _mm(gated, d_ret_out, 'tn', tm=512, tn=D, tk=S, name="gw_ret_out", out_dtype=BF16)
    gw_att_out = _mm(att, d_att_out, 'tn', tm=AW, tn=D, tk=S, name="gw_att_out", out_dtype=BF16)
    fulls_b = [_to_slots(g, ax) for g, ax in zip((gw_ret_out, gw_att_out, gw_o), BIG_AXES[1:4])]
    dgated, recv_core_b = _mm(d_ret_out, w_ret_out, 'nt', tm=S, tn=512, tk=D, name="dgated",
                              comm=_ExchangeCore(fulls_b))
    parts_b = _reduce_sums(fulls_b, recv_core_b, core, "b")
    datt = _mm(d_att_out, w_att_out, 'nt', tm=S, tn=AW, tk=D, name="datt")
    mix_grads = _mix_bwd(outs, lses, datt)
    datt_parts, ds_sums = [], []
    for gi in range(len(ATT_GROUPS)):
        comm = {1: _ExchangeChip(parts_b), 2: _ExchangeChip(parts_a[:1])}.get(gi)
        res = _att_bwd(slabs, bias, outs[gi], lses[gi], mix_grads[gi], mix_grads[3 + gi], gi, comm=comm)
        if gi == 1:
            res, recv_chip_b = res
        if gi == 2:
            res, recv_chip_a1 = res
        dq, dk, dv, ds_sum = res
        datt_parts += [dq.reshape(S, AW), dk.reshape(S, AW), dv.reshape(S, AW)]
        ds_sums.append(ds_sum)
    red_b = [_chip_sum(p, r, chip, f"rs_sum_b{i}") for i, (p, r) in enumerate(zip(parts_b, recv_chip_b))]
    g_bias = _bias_grad(jnp.concatenate(ds_sums, axis=0), buckets)[:, :, 0].T.reshape(1, -1)
    (dret, g_gn_g, g_gn_b), recv_chip_a2 = _ret_bwd(proj, tables, gn_g, gn_b, ro, states, dgated,
                                                    comm=_ExchangeChip(parts_a[1:]))
    recv_chip_a = list(recv_chip_a1) + list(recv_chip_a2)
    red_a = [_chip_sum(p, r, chip, f"rs_sum_a{i}") for i, (p, r) in enumerate(zip(parts_a, recv_chip_a))]
    dproj = jnp.concatenate([dret] + datt_parts + [dga, dgb], axis=1)
    gw_in_t = _mm(dproj, h1, 'tn', tm=512, tn=D, tk=S, name="gw_in", out_dtype=BF16)
    full_in = [_to_slots(gw_in_t, 0)]
    dh1, recv_core_in = _mm(dproj, w_in_t, 'nn', tm=1024, tn=1024, tk=2560, name="dh1", comm=_ExchangeCore(full_in))
    part_in = _reduce_sums(full_in, recv_core_in, core, "c")
    gx, dsc1, dsh1, g_norm1 = _norm_mod_bwd(x, norm1_g, sc1, dh1, dx1, "norm1_bwd")
    recv_chip_in = _run_comm(_ExchangeChip(part_in), "rs_chip_in")
    red_in = _chip_sum(part_in[0], recv_chip_in[0], chip, "rs_sum_c")

    dmod = [dsh1, dsc1, dg1, dsh2, dsc2, dg2]
    small_g = [g_norm1, g_bias, g_gn_g, g_gn_b, g_norm2, g_normf]
    return loss, gx, [red_in] + red_b + red_a, small_g, dmod


def _to_slots(g, axis):
    if axis == 0:
        return g.reshape(4, 2, g.shape[0] // N_DEV, g.shape[1])
    return g.reshape(g.shape[0], N_DEV, g.shape[1] // N_DEV).transpose(1, 0, 2).reshape(4, 2, g.shape[0], -1)


def _from_slots(w8, axis):
    if axis == 0:
        return w8.reshape(-1, w8.shape[2])
    return w8.transpose(1, 0, 2).reshape(w8.shape[1], -1)


BIG_AXES = (1, 0, 1, 0, 1, 0)


def kernel(x, c, w_ada, b_ada, norm1_g, w_in, rel_bias, ret_gn_g, ret_gn_b, w_ret_out, w_att_out, w_o, norm2_g, w_ff1, w_ff2, norm_f_g, loss_target, m_w_ada, m_b_ada, m_norm1_g, m_w_in, m_rel_bias, m_ret_gn_g, m_ret_gn_b, m_w_ret_out, m_w_att_out, m_w_o, m_norm2_g, m_w_ff1, m_w_ff2, m_norm_f_g, v_w_ada, v_b_ada, v_norm1_g, v_w_in, v_rel_bias, v_ret_gn_g, v_ret_gn_b, v_w_ret_out, v_w_att_out, v_w_o, v_norm2_g, v_w_ff1, v_w_ff2, v_norm_f_g):
    mx, my, mc = _mesh_pos()
    dev = 4 * mx + 2 * my + mc
    chip = jnp.reshape(2 * mx + my, (1,)).astype(jnp.int32)
    core = jnp.reshape(mc, (1,)).astype(jnp.int32)
    ada_w = D * 6 // N_DEV

    w_in, m_w_in, v_w_in = (jnp.transpose(t, (0, 2, 1)) for t in (w_in, m_w_in, v_w_in))

    shards = [w[0].astype(BF16) for w in (w_in, w_ret_out, w_att_out, w_o, w_ff1, w_ff2)]
    c_all, w_in8 = _run_comm(_Gather([c, shards[0]]), "gather_c_w_in")
    c_all = c_all.reshape(N_DEV, D)
    b_sl = lax.dynamic_slice(b_ada, (0, dev * ada_w), (1, ada_w))
    (mod_all,) = _run_comm(_Gather([_ada_fwd(c_all, w_ada[0], b_sl)]), "gather_mod")
    mod = lax.dynamic_index_in_dim(mod_all, dev, axis=1, keepdims=False).reshape(6, D)
    mods = tuple(mod[i:i + 1] for i in range(6))

    small = (norm1_g, rel_bias, ret_gn_g, ret_gn_b, norm2_g, norm_f_g.reshape(1, D))
    loss, gx, big_red, small_g, dmod = _local_step(x[0], loss_target[0], mods, w_in8.reshape(IN_COLS, D),
                                                   shards[1:], small, chip, core)

    gathered = _run_comm(_Gather(dmod + small_g + [loss]), "gather_small")
    g_b_ada, dmod_all, (g_norm1, g_bias, g_gn_g, g_gn_b, g_norm2, g_normf, loss_sum) = _sum_small(gathered)
    loss_out = loss_sum[0, 0]
    g_w_ada = _ada_bwd(c_all, lax.dynamic_slice(dmod_all, (0, dev * ada_w), (N_DEV, ada_w)))

    names = ['w_ada', 'b_ada', 'norm1_g', 'w_in', 'rel_bias', 'ret_gn_g', 'ret_gn_b', 'w_ret_out', 'w_att_out',
             'w_o', 'norm2_g', 'w_ff1', 'w_ff2', 'norm_f_g']
    ws = dict(zip(names, (w_ada, b_ada, norm1_g, w_in, rel_bias, ret_gn_g, ret_gn_b, w_ret_out, w_att_out, w_o,
                          norm2_g, w_ff1, w_ff2, norm_f_g)))
    ms = dict(zip(names, (m_w_ada, m_b_ada, m_norm1_g, m_w_in, m_rel_bias, m_ret_gn_g, m_ret_gn_b, m_w_ret_out,
                          m_w_att_out, m_w_o, m_norm2_g, m_w_ff1, m_w_ff2, m_norm_f_g)))
    vs = dict(zip(names, (v_w_ada, v_b_ada, v_norm1_g, v_w_in, v_rel_bias, v_ret_gn_g, v_ret_gn_b, v_w_ret_out,
                          v_w_att_out, v_w_o, v_norm2_g, v_w_ff1, v_w_ff2, v_norm_f_g)))
    grads = dict(w_ada=g_w_ada, w_in=big_red[0], w_ret_out=big_red[1], w_att_out=big_red[2], w_o=big_red[3],
                 w_ff1=big_red[4], w_ff2=big_red[5], b_ada=g_b_ada, norm1_g=g_norm1, rel_bias=g_bias,
                 ret_gn_g=g_gn_g, ret_gn_b=g_gn_b, norm2_g=g_norm2, norm_f_g=g_normf)
    delta, new_m, new_v = {}, {}, {}
    for n in ('w_ada', 'w_in', 'w_ret_out', 'w_att_out', 'w_o', 'w_ff1', 'w_ff2'):
        delta[n], new_m[n], new_v[n] = _adamw(ws[n], grads[n], ms[n], vs[n], "adamw_" + n)
        grads[n] = grads[n].reshape(ws[n].shape)
    for d in (grads, delta, new_m, new_v):
        d['w_in'] = jnp.transpose(d['w_in'], (0, 2, 1))
    small_names = ('b_ada', 'norm1_g', 'rel_bias', 'ret_gn_g', 'ret_gn_b', 'norm2_g', 'norm_f_g')
    two_d = {n: (1, ws[n].size) if ws[n].ndim == 1 else ws[n].shape for n in small_names}
    d_, m_, v_ = _adamw_small(*[[src[n].reshape(two_d[n]) for n in small_names] for src in (ws, grads, ms, vs)])
    for i, n in enumerate(small_names):
        shp = ws[n].shape
        delta[n], new_m[n], new_v[n] = d_[i].reshape(shp), m_[i].reshape(shp), v_[i].reshape(shp)
        grads[n] = grads[n].reshape(shp)
    return (loss_out, gx[None], *[grads[n] for n in names], *[delta[n] for n in names],
            *[new_m[n] for n in names], *[new_v[n] for n in names])
```

```python
import functools
import math

import numpy as np
import jax
import jax.numpy as jnp
from jax import lax
from jax.experimental import pallas as pl
from jax.experimental.pallas import tpu as pltpu

F32 = jnp.float32
BF16 = jnp.bfloat16
MESH = pl.DeviceIdType.MESH

N_DEV = 8
S = 2048
D = 1024
RET_HEADS = 4
RET_DK = 256
RET_DV = 512
CHUNK = 128
N_CHUNK = S // CHUNK
ATT_GROUPS = ((128, 1), (512, 4), (2048, 16))
ATT_HG = 4
ATT_DH = 128
ATT_BLK = 128
N_BUCKETS = 32
MAX_DIST = 2048
D_FF = 4096
IN_COLS = 12800
OFF_RQ, OFF_RK, OFF_RV, OFF_RG, OFF_ATT = 0, 1024, 2048, 4096, 6144
OFF_GA, OFF_GB = 6144, 7168
RMS_EPS = 1e-6
GN_EPS = 1e-5
ADAM_LR, ADAM_B1, ADAM_B2, ADAM_EPS, ADAM_WD, ADAM_STEP = 0.001, 0.9, 0.999, 1e-08, 0.01, 10
VMEM_LIMIT = 48 * 1024 * 1024


def _pcall(body, **kw):
    return pl.pallas_call(body, **kw)


def _params(sem=None):
    return pltpu.CompilerParams(dimension_semantics=sem, vmem_limit_bytes=VMEM_LIMIT)


HBM_SPEC = pl.BlockSpec(memory_space=pl.ANY)


def _carry(body, comm, *, name, grid, in_specs, out_specs, out_shape, scratch_shapes=()):
    single = not isinstance(out_specs, (tuple, list))
    o_specs = (out_specs,) if single else tuple(out_specs)
    o_shape = (out_shape,) if single else tuple(out_shape)
    n_in, n_out, n_scr = len(in_specs), len(o_specs), len(scratch_shapes)
    nci, nco = len(comm.ins), len(comm.out_shape)
    total = int(np.prod(grid))

    def wrapped(*refs):
        bounds = np.cumsum([0, n_in, nci, n_out, nco, n_scr])
        a, ci, o, co, scr = (refs[bounds[i]:bounds[i + 1]] for i in range(5))
        sems = refs[bounds[5]:]
        flat = 0
        for d, g in enumerate(grid):
            flat = flat * g + pl.program_id(d)

        @pl.when(flat == 0)
        def _():
            comm.start(ci, co, sems)

        body(*a, *o, *scr)

        @pl.when(flat == total - 1)
        def _():
            comm.finish(ci, co, sems)

    call = _pcall(wrapped, name=name, grid=grid, in_specs=list(in_specs) + [HBM_SPEC] * nci,
                  out_specs=o_specs + (HBM_SPEC,) * nco, out_shape=o_shape + tuple(comm.out_shape),
                  scratch_shapes=list(scratch_shapes) + list(comm.sems),
                  compiler_params=_params(("arbitrary",) * len(grid)))

    def run(*args):
        res = call(*args, *comm.ins)
        own = res[0] if single else tuple(res[:n_out])
        return own, tuple(res[n_out:])

    return run


def _run_comm(comm, name):
    nci, nco = len(comm.ins), len(comm.out_shape)

    def body(*refs):
        ci, co, sems = refs[:nci], refs[nci:nci + nco], refs[nci + nco:]
        comm.start(ci, co, sems)
        comm.finish(ci, co, sems)

    return _pcall(body, name=name, in_specs=[HBM_SPEC] * nci, out_specs=(HBM_SPEC,) * nco,
                  out_shape=tuple(comm.out_shape), scratch_shapes=list(comm.sems))(*comm.ins)


def _dot(a, b, dn):
    return lax.dot_general(a.astype(BF16), b.astype(BF16), (dn, ((), ())), preferred_element_type=F32)


NN = ((1,), (0,))
NT = ((1,), (1,))
TN = ((0,), (0,))


def _mm(a, b, mode, *, tm, tn, tk, name, out_dtype=F32, res=None, gvec=None, relu2=False, relu2_of=None, comm=None):
    if mode == 'nn':
        (M, K), (_, N) = a.shape, b.shape
        a_spec = pl.BlockSpec((tm, tk), lambda i, j, k: (i, k))
        b_spec = pl.BlockSpec((tk, tn), lambda i, j, k: (k, j))
        dn = NN
    elif mode == 'nt':
        (M, K), (N, _) = a.shape, b.shape
        a_spec = pl.BlockSpec((tm, tk), lambda i, j, k: (i, k))
        b_spec = pl.BlockSpec((tn, tk), lambda i, j, k: (j, k))
        dn = NT
    else:
        (K, M), (_, N) = a.shape, b.shape
        a_spec = pl.BlockSpec((tk, tm), lambda i, j, k: (k, i))
        b_spec = pl.BlockSpec((tk, tn), lambda i, j, k: (k, j))
        dn = TN
    assert M % tm == 0 and N % tn == 0 and K % tk == 0, (name, M, N, K)
    nk = K // tk
    fused = res is not None
    o_spec = pl.BlockSpec((tm, tn), lambda i, j, k: (i, j))

    def body(a_ref, b_ref, *rest):
        acc_ref = rest[-1] if nk > 1 else None
        if fused:
            res_ref, g_ref, o_ref, x_ref = rest[:4]
        elif relu2_of is not None:
            u_ref, o_ref = rest[:2]
        elif relu2:
            o_ref, act_ref = rest[:2]
        else:
            o_ref = rest[0]

        def finish(acc):
            if relu2_of is not None:
                acc = acc * (2.0 * jnp.maximum(u_ref[...], 0.0))
            o_ref[...] = acc.astype(o_ref.dtype)
            if fused:
                x_ref[...] = res_ref[...] + g_ref[...] * acc
            if relu2:
                r = jnp.maximum(acc, 0.0)
                act_ref[...] = (r * r).astype(BF16)

        p = _dot(a_ref[...], b_ref[...], dn)
        if nk == 1:
            finish(p)
        else:
            k = pl.program_id(2)

            @pl.when(k == 0)
            def _():
                acc_ref[...] = p

            @pl.when(k > 0)
            def _():
                acc_ref[...] += p

            @pl.when(k == nk - 1)
            def _():
                finish(acc_ref[...])

    in_specs = [a_spec, b_spec]
    args = [a, b]
    out_shape = jax.ShapeDtypeStruct((M, N), out_dtype)
    out_specs = o_spec
    if fused:
        in_specs += [pl.BlockSpec((tm, tn), lambda i, j, k: (i, j)), pl.BlockSpec((1, tn), lambda i, j, k: (0, j))]
        args += [res, gvec]
        out_shape = (out_shape, jax.ShapeDtypeStruct((M, N), F32))
        out_specs = (o_spec, pl.BlockSpec((tm, tn), lambda i, j, k: (i, j)))
    elif relu2_of is not None:
        in_specs.append(pl.BlockSpec((tm, tn), lambda i, j, k: (i, j)))
        args.append(relu2_of)
    elif relu2:
        out_shape = (out_shape, jax.ShapeDtypeStruct((M, N), BF16))
        out_specs = (o_spec, pl.BlockSpec((tm, tn), lambda i, j, k: (i, j)))
    kw = dict(name=name, grid=(M // tm, N // tn, nk), in_specs=in_specs, out_specs=out_specs,
              out_shape=out_shape, scratch_shapes=[pltpu.VMEM((tm, tn), F32)] if nk > 1 else [])
    if comm is not None:
        return _carry(body, comm, **kw)(*args)
    return _pcall(body, compiler_params=_params(("parallel", "parallel", "arbitrary")), **kw)(*args)


PROJ_TN = 512
ATT_T0, ATT_T1 = 6144 // PROJ_TN, 10752 // PROJ_TN
N_SLABS = (ATT_T1 - ATT_T0) * 4
MAIN_COLS = IN_COLS - (ATT_T1 - ATT_T0) * PROJ_TN


def _proj(h1, w_in_t, comm):
    nj = IN_COLS // PROJ_TN

    def body(a_ref, b_ref, main_ref, slab_ref):
        j = pl.program_id(1)
        is_att = (j >= ATT_T0) & (j < ATT_T1)
        chunks = [pl.ds(c * 512, 512) for c in range(S // 512)]

        @pl.when(jnp.logical_not(is_att))
        def _():
            for rows in chunks:
                main_ref[rows, :] = _dot(a_ref[rows, :], b_ref[...], NT)

        @pl.when(is_att)
        def _():
            for rows in chunks:
                p = _dot(a_ref[rows, :], b_ref[...], NT)
                for h in range(4):
                    slab_ref[h, rows, :] = p[:, h * 128:(h + 1) * 128]

    main_idx = lambda j: jnp.where(j < ATT_T0, j, jnp.where(j < ATT_T1, ATT_T0 - 1, j - (ATT_T1 - ATT_T0)))
    slab_idx = lambda j: jnp.clip(j - ATT_T0, 0, ATT_T1 - ATT_T0 - 1)
    (main, slabs), got = _carry(
        body, comm, name="proj", grid=(1, nj, 1),
        in_specs=[pl.BlockSpec((S, D), lambda i, j, k: (0, 0)), pl.BlockSpec((PROJ_TN, D), lambda i, j, k: (j, 0))],
        out_specs=(pl.BlockSpec((S, PROJ_TN), lambda i, j, k: (0, main_idx(j))),
                   pl.BlockSpec((4, S, 128), lambda i, j, k: (slab_idx(j), 0, 0))),
        out_shape=(jax.ShapeDtypeStruct((S, MAIN_COLS), F32), jax.ShapeDtypeStruct((N_SLABS, S, 128), F32)))(h1, w_in_t)
    return main, slabs, got


TR = 256


def _row_spec(w=D):
    return pl.BlockSpec((TR, w), lambda i: (i, 0))


def _vec_spec(w=D):
    return pl.BlockSpec((1, w), lambda i: (0, 0))


def _norm_mod_fwd(x, g, sh, sc, name):
    def body(x_ref, g_ref, sh_ref, sc_ref, o_ref):
        xv = x_ref[...]
        rstd = lax.rsqrt(jnp.mean(xv * xv, axis=-1, keepdims=True) + RMS_EPS)
        n = xv * rstd * g_ref[...]
        o_ref[...] = (n * (1.0 + sc_ref[...]) + sh_ref[...]).astype(BF16)

    return _pcall(body, name=name, grid=(S // TR,), in_specs=[_row_spec(), _vec_spec(), _vec_spec(), _vec_spec()],
                  out_specs=_row_spec(), out_shape=jax.ShapeDtypeStruct((S, D), BF16),
                  compiler_params=_params(("parallel",)))(x, g, sh, sc)


def _norm_mod_bwd(x, g, sc, dh, dres, name, gate=None):
    gated = gate is not None

    def body(x_ref, g_ref, sc_ref, dh_ref, dres_ref, *rest):
        if gated:
            f_ref, gv_ref, dx_ref, dsc_ref, dsh_ref, dg_ref, dz_ref, dgv_ref = rest
        else:
            dx_ref, dsc_ref, dsh_ref, dg_ref = rest
        i = pl.program_id(0)
        xv = x_ref[...]
        dh = dh_ref[...]
        rstd = lax.rsqrt(jnp.mean(xv * xv, axis=-1, keepdims=True) + RMS_EPS)
        xhat = xv * rstd
        gv = g_ref[...]
        dn = dh * (1.0 + sc_ref[...])
        dxhat = dn * gv
        dx = dres_ref[...] + rstd * (dxhat - xhat * jnp.mean(dxhat * xhat, axis=-1, keepdims=True))
        dx_ref[...] = dx
        sums = [(dsc_ref, jnp.sum(dh * (xhat * gv), axis=0, keepdims=True)),
                (dsh_ref, jnp.sum(dh, axis=0, keepdims=True)),
                (dg_ref, jnp.sum(dn * xhat, axis=0, keepdims=True))]
        if gated:
            dz_ref[...] = (dx * gv_ref[...]).astype(BF16)
            sums.append((dgv_ref, jnp.sum(dx * f_ref[...], axis=0, keepdims=True)))

        @pl.when(i == 0)
        def _():
            for ref, p in sums:
                ref[...] = p

        @pl.when(i > 0)
        def _():
            for ref, p in sums:
                ref[...] += p

    vec = jax.ShapeDtypeStruct((1, D), F32)
    in_specs = [_row_spec(), _vec_spec(), _vec_spec(), _row_spec(), _row_spec()]
    out_specs = [_row_spec(), _vec_spec(), _vec_spec(), _vec_spec()]
    out_shape = [jax.ShapeDtypeStruct((S, D), F32), vec, vec, vec]
    args = [x, g, sc, dh, dres]
    if gated:
        in_specs += [_row_spec(), _vec_spec()]
        out_specs += [_row_spec(), _vec_spec()]
        out_shape += [jax.ShapeDtypeStruct((S, D), BF16), vec]
        args += list(gate)
    return _pcall(body, name=name, grid=(S // TR,), in_specs=in_specs, out_specs=tuple(out_specs),
                  out_shape=tuple(out_shape), compiler_params=_params(("arbitrary",)))(*args)


def _final_loss(x2, tgt, g, f, g2):
    def body(x_ref, t_ref, g_ref, f_ref, g2_ref, loss_ref, dx_ref, dg_ref, df_ref, dg2_ref):
        i = pl.program_id(0)
        xv = x_ref[...]
        gv = g_ref[...]
        rstd = lax.rsqrt(jnp.mean(xv * xv, axis=-1, keepdims=True) + RMS_EPS)
        xhat = xv * rstd
        err = xhat * gv - t_ref[...]
        dy = err * (1.0 / D)
        dxhat = dy * gv
        dx = rstd * (dxhat - xhat * jnp.mean(dxhat * xhat, axis=-1, keepdims=True))
        dx_ref[...] = dx
        df_ref[...] = (dx * g2_ref[...]).astype(BF16)
        p_g = jnp.sum(dy * xhat, axis=0, keepdims=True)
        p_g2 = jnp.sum(dx * f_ref[...], axis=0, keepdims=True)
        p_l = jnp.zeros((1, 128), F32) + 0.5 * jnp.sum(jnp.mean(err * err, axis=-1, keepdims=True))

        @pl.when(i == 0)
        def _():
            dg_ref[...] = p_g
            dg2_ref[...] = p_g2
            loss_ref[...] = p_l

        @pl.when(i > 0)
        def _():
            dg_ref[...] += p_g
            dg2_ref[...] += p_g2
            loss_ref[...] += p_l

    vec = jax.ShapeDtypeStruct((1, D), F32)
    return _pcall(body, name="final_loss", grid=(S // TR,),
                  in_specs=[_row_spec(), _row_spec(), _vec_spec(), _row_spec(), _vec_spec()],
                  out_specs=(_vec_spec(128), _row_spec(), _vec_spec(), _row_spec(), _vec_spec()),
                  out_shape=(jax.ShapeDtypeStruct((1, 128), F32), jax.ShapeDtypeStruct((S, D), F32), vec,
                             jax.ShapeDtypeStruct((S, D), BF16), vec),
                  compiler_params=_params(("arbitrary",)))(x2, tgt, g, f, g2)


HALF = 512


def _merge_fwd(proj, ret_out, att_out):
    def body(ga_ref, gb_ref, r_ref, a_ref, o_ref):
        o_ref[...] = (jax.nn.sigmoid(ga_ref[...]) * r_ref[...] + jax.nn.sigmoid(gb_ref[...]) * a_ref[...]).astype(BF16)

    blk = lambda off: pl.BlockSpec((TR, HALF), lambda i, j: (i, off // HALF + j))
    return _pcall(body, name="merge_fwd", grid=(S // TR, D // HALF),
                  in_specs=[blk(OFF_GA), blk(OFF_GB), blk(0), blk(0)], out_specs=blk(0),
                  out_shape=jax.ShapeDtypeStruct((S, D), BF16),
                  compiler_params=_params(("parallel", "parallel")))(proj, proj, ret_out, att_out)


def _merge_bwd(proj, ret_out, att_out, dmerged):
    def body(ga_ref, gb_ref, r_ref, a_ref, dm_ref, dr_ref, da_ref, dga_ref, dgb_ref):
        sa = jax.nn.sigmoid(ga_ref[...])
        sb = jax.nn.sigmoid(gb_ref[...])
        dm = dm_ref[...]
        dr_ref[...] = (dm * sa).astype(BF16)
        da_ref[...] = (dm * sb).astype(BF16)
        dga_ref[...] = (dm * r_ref[...] * (sa * (1.0 - sa))).astype(BF16)
        dgb_ref[...] = (dm * a_ref[...] * (sb * (1.0 - sb))).astype(BF16)

    blk = lambda off: pl.BlockSpec((TR, HALF), lambda i, j: (i, off // HALF + j))
    o = jax.ShapeDtypeStruct((S, D), BF16)
    return _pcall(body, name="merge_bwd", grid=(S // TR, D // HALF),
                  in_specs=[blk(OFF_GA), blk(OFF_GB), blk(0), blk(0), blk(0)], out_specs=(blk(0),) * 4,
                  out_shape=(o, o, o, o),
                  compiler_params=_params(("parallel", "parallel")))(proj, proj, ret_out, att_out, dmerged)


def _ret_tables():
    H, C = RET_HEADS, CHUNK
    log_g = jnp.log1p(-(2.0 ** (-5.0 - jnp.arange(H, dtype=F32))))
    idx = jnp.arange(C, dtype=F32)
    rel = idx[:, None] - idx[None, :]
    inner = jnp.where(rel >= 0, jnp.exp(log_g[:, None, None] * jnp.maximum(rel, 0.0)), 0.0)
    qd = jnp.exp(log_g[:, None] * (idx + 1.0))[:, :, None]
    kd = jnp.exp(log_g[:, None] * (C - 1.0 - idx))[:, :, None]
    cd = jnp.broadcast_to(jnp.exp(log_g * C)[:, None, None], (H, 1, 128))
    half = RET_DK // 2
    inv = 10000.0 ** (-jnp.arange(half, dtype=F32) / half)
    ang = jnp.arange(S, dtype=F32)[:, None] * inv[None, :]
    return inner, qd, kd, cd, jnp.cos(ang), jnp.sin(ang)


def _rot(x, cos, sin):
    x1, x2 = x[:, :128], x[:, 128:]
    return jnp.concatenate([x1 * cos - x2 * sin, x1 * sin + x2 * cos], axis=1)


def _rot_t(d, cos, sin):
    d1, d2 = d[:, :128], d[:, 128:]
    return jnp.concatenate([d1 * cos + d2 * sin, d2 * cos - d1 * sin], axis=1)


RET_COLS = OFF_ATT
RET_VW = RET_HEADS * RET_DV


def _ret_specs(chunk_of):
    ci = chunk_of
    whole = lambda shape: pl.BlockSpec(shape, lambda t: (0,) * len(shape))
    return [
        pl.BlockSpec((CHUNK, RET_COLS), lambda t: (ci(t), 0)),
        pl.BlockSpec((CHUNK, 128), lambda t: (ci(t), 0)),
        pl.BlockSpec((CHUNK, 128), lambda t: (ci(t), 0)),
        whole((RET_HEADS, CHUNK, CHUNK)), whole((RET_HEADS, CHUNK, 1)), whole((RET_HEADS, CHUNK, 1)),
        whole((RET_HEADS, 1, 128)), whole((1, RET_VW)), whole((1, RET_VW)),
    ]


def _ret_cols(h):
    q = slice(OFF_RQ + h * RET_DK, OFF_RQ + (h + 1) * RET_DK)
    k = slice(OFF_RK + h * RET_DK, OFF_RK + (h + 1) * RET_DK)
    v = slice(OFF_RV + h * RET_DV, OFF_RV + (h + 1) * RET_DV)
    g = slice(OFF_RG + h * RET_DV, OFF_RG + (h + 1) * RET_DV)
    return q, k, v, g, slice(h * RET_DV, (h + 1) * RET_DV)


def _ret_fwd(proj, tables, gn_g, gn_b, comm=None):
    inner, qd, kd, cd, cos, sin = tables

    def body(x_ref, cos_ref, sin_ref, in_ref, qd_ref, kd_ref, cd_ref, g_ref, b_ref,
             gated_ref, ro_ref, st_ref, s_scr):
        i = pl.program_id(0)

        @pl.when(i == 0)
        def _():
            s_scr[...] = jnp.zeros_like(s_scr)

        cosv, sinv = cos_ref[...], sin_ref[...]
        for h in range(RET_HEADS):
            cq, ck, cv, cg, co = _ret_cols(h)
            q = _rot(x_ref[:, cq], cosv, sinv)
            k = _rot(x_ref[:, ck], cosv, sinv) * (RET_DK ** -0.5)
            v = x_ref[:, cv]
            st = s_scr[h]
            st_ref[h] = st
            s = _dot(q, k, NT) * in_ref[h]
            o = _dot(s, v, NN) + _dot(q, st, NN) * qd_ref[h]
            s_scr[h] = st * cd_ref[h, :, :1] + _dot(k * kd_ref[h], v, TN)
            ro_ref[:, co] = o
            mu = jnp.mean(o, axis=-1, keepdims=True)
            oc = o - mu
            var = jnp.mean(oc * oc, axis=-1, keepdims=True)
            rn = oc * lax.rsqrt(var + GN_EPS) * g_ref[:, co] + b_ref[:, co]
            rg = x_ref[:, cg]
            gated_ref[:, co] = (rg * jax.nn.sigmoid(rg) * rn).astype(BF16)

    ospec = pl.BlockSpec((CHUNK, RET_VW), lambda t: (t, 0))
    kw = dict(name="ret_fwd", grid=(N_CHUNK,), in_specs=_ret_specs(lambda t: t),
              out_specs=(ospec, ospec, pl.BlockSpec((RET_HEADS, None, RET_DK, RET_DV), lambda t: (0, t, 0, 0))),
              out_shape=(jax.ShapeDtypeStruct((S, RET_VW), BF16), jax.ShapeDtypeStruct((S, RET_VW), F32),
                         jax.ShapeDtypeStruct((RET_HEADS, N_CHUNK, RET_DK, RET_DV), F32)),
              scratch_shapes=[pltpu.VMEM((RET_HEADS, RET_DK, RET_DV), F32)])
    args = (proj, cos, sin, inner, qd, kd, cd, gn_g, gn_b)
    if comm is not None:
        return _carry(body, comm, **kw)(*args)
    return _pcall(body, compiler_params=_params(("arbitrary",)), **kw)(*args)


def _ret_bwd(proj, tables, gn_g, gn_b, ro, states, dgated, comm=None):
    inner, qd, kd, cd, cos, sin = tables
    last = N_CHUNK - 1

    def body(x_ref, cos_ref, sin_ref, in_ref, qd_ref, kd_ref, cd_ref, g_ref, b_ref, ro_ref, st_ref, dg_ref,
             dx_ref, gg_ref, gb_ref, gs_scr):
        t = pl.program_id(0)

        @pl.when(t == 0)
        def _():
            gs_scr[...] = jnp.zeros_like(gs_scr)
            gg_ref[...] = jnp.zeros_like(gg_ref)
            gb_ref[...] = jnp.zeros_like(gb_ref)

        cosv, sinv = cos_ref[...], sin_ref[...]
        for h in range(RET_HEADS):
            cq, ck, cv, cg, co = _ret_cols(h)
            q = _rot(x_ref[:, cq], cosv, sinv)
            k = _rot(x_ref[:, ck], cosv, sinv) * (RET_DK ** -0.5)
            v = x_ref[:, cv]
            qdv, kdv, dm = qd_ref[h], kd_ref[h], in_ref[h]
            st = st_ref[h]
            o = ro_ref[:, co]
            gv = g_ref[:, co]
            mu = jnp.mean(o, axis=-1, keepdims=True)
            oc = o - mu
            rstd = lax.rsqrt(jnp.mean(oc * oc, axis=-1, keepdims=True) + GN_EPS)
            ohat = oc * rstd
            rn = ohat * gv + b_ref[:, co]
            rg = x_ref[:, cg]
            sg = jax.nn.sigmoid(rg)
            dgt = dg_ref[:, co]
            drn = dgt * (rg * sg)
            dx_ref[:, cg] = (dgt * rn * (sg * (1.0 + rg * (1.0 - sg)))).astype(BF16)
            gg_ref[:, co] += jnp.sum(drn * ohat, axis=0, keepdims=True)
            gb_ref[:, co] += jnp.sum(drn, axis=0, keepdims=True)
            dohat = drn * gv
            do = rstd * (dohat - jnp.mean(dohat, axis=-1, keepdims=True)
                         - ohat * jnp.mean(dohat * ohat, axis=-1, keepdims=True))
            gs = gs_scr[h]
            s = _dot(q, k, NT) * dm
            dsr = _dot(do, v, NT) * dm
            dq = _dot(dsr, k, NN) + _dot(do, st, NT) * qdv
            dk = _dot(dsr, q, TN) + _dot(v, gs, NT) * kdv
            dv = _dot(s, do, TN) + _dot(k * kdv, gs, NN)
            gs_scr[h] = gs * cd_ref[h, :, :1] + _dot(q * qdv, do, TN)
            dx_ref[:, cq] = _rot_t(dq, cosv, sinv).astype(BF16)
            dx_ref[:, ck] = (_rot_t(dk, cosv, sinv) * (RET_DK ** -0.5)).astype(BF16)
            dx_ref[:, cv] = dv.astype(BF16)

    rev = lambda t: last - t
    vblk = pl.BlockSpec((CHUNK, RET_VW), lambda t: (rev(t), 0))
    vspec = pl.BlockSpec((1, RET_VW), lambda t: (0, 0))
    kw = dict(name="ret_bwd", grid=(N_CHUNK,),
              in_specs=_ret_specs(rev) + [vblk, pl.BlockSpec((RET_HEADS, None, RET_DK, RET_DV),
                                                             lambda t: (0, rev(t), 0, 0)), vblk],
              out_specs=(pl.BlockSpec((CHUNK, RET_COLS), lambda t: (rev(t), 0)), vspec, vspec),
              out_shape=(jax.ShapeDtypeStruct((S, RET_COLS), BF16), jax.ShapeDtypeStruct((1, RET_VW), F32),
                         jax.ShapeDtypeStruct((1, RET_VW), F32)),
              scratch_shapes=[pltpu.VMEM((RET_HEADS, RET_DK, RET_DV), F32)])
    args = (proj, cos, sin, inner, qd, kd, cd, gn_g, gn_b, ro, states, dgated)
    if comm is not None:
        return _carry(body, comm, **kw)(*args)
    return _pcall(body, compiler_params=_params(("arbitrary",)), **kw)(*args)


def _bucket_tables():
    qi = np.arange(ATT_BLK)[:, None]
    kj = np.arange(2 * ATT_BLK)[None, :]
    m = ATT_BLK + qi - kj
    out = []
    for win, dil in ATT_GROUPS:
        w = win // dil
        dist = (np.clip(m, 0, w) * dil).astype(np.int32)
        max_exact = N_BUCKETS // 2
        d_f = np.maximum(dist, 1).astype(np.float32)
        large = max_exact + (np.log(d_f / np.float32(max_exact)) / np.float32(math.log(MAX_DIST / max_exact))
                             * np.float32(N_BUCKETS - max_exact)).astype(np.int32)
        large = np.minimum(large, N_BUCKETS - 1)
        out.append(np.where(dist < max_exact, dist, large).astype(np.int32))
    return np.stack(out)


def _bias_build(rel_bias, buckets):
    def body(tab_ref, bk_ref, o_ref):
        hh = pl.program_id(0)
        bk = bk_ref[...]
        acc = jnp.zeros((ATT_BLK, 2 * ATT_BLK), F32)
        for b in range(N_BUCKETS):
            acc = jnp.where(bk == b, tab_ref[b, hh], acc)
        o_ref[...] = acc

    nh = len(ATT_GROUPS) * ATT_HG
    return _pcall(body, name="bias_build", grid=(nh,),
                  in_specs=[pl.BlockSpec(memory_space=pltpu.SMEM),
                            pl.BlockSpec((None, ATT_BLK, 2 * ATT_BLK), lambda hh: (hh // ATT_HG, 0, 0))],
                  out_specs=pl.BlockSpec((None, ATT_BLK, 2 * ATT_BLK), lambda hh: (hh, 0, 0)),
                  out_shape=jax.ShapeDtypeStruct((nh, ATT_BLK, 2 * ATT_BLK), F32),
                  compiler_params=_params(("parallel",)))(rel_bias, buckets)


def _bias_grad(ds_sum, buckets):
    def body(ds_ref, bk_ref, o_ref):
        bk = bk_ref[...]
        ds = ds_ref[...]
        rows = lax.broadcasted_iota(jnp.int32, (N_BUCKETS, 128), 0)
        acc = jnp.zeros((N_BUCKETS, 128), F32)
        for b in range(N_BUCKETS):
            acc = jnp.where(rows == b, jnp.sum(jnp.where(bk == b, ds, 0.0)), acc)
        o_ref[...] = acc

    nh = len(ATT_GROUPS) * ATT_HG
    return _pcall(body, name="bias_grad", grid=(nh,),
                  in_specs=[pl.BlockSpec((None, ATT_BLK, 2 * ATT_BLK), lambda hh: (hh, 0, 0)),
                            pl.BlockSpec((None, ATT_BLK, 2 * ATT_BLK), lambda hh: (hh // ATT_HG, 0, 0))],
                  out_specs=pl.BlockSpec((None, N_BUCKETS, 128), lambda hh: (hh, 0, 0)),
                  out_shape=jax.ShapeDtypeStruct((nh, N_BUCKETS, 128), F32),
                  compiler_params=_params(("parallel",)))(ds_sum, buckets)


def _att_valid(n):
    qi = lax.broadcasted_iota(jnp.int32, (ATT_BLK, 2 * ATT_BLK), 0)
    kj = lax.broadcasted_iota(jnp.int32, (ATT_BLK, 2 * ATT_BLK), 1)
    m = ATT_BLK + qi - kj
    first_key = jnp.where(n > 0, 0, ATT_BLK)
    return (m >= 0) & (m <= ATT_BLK) & (kj >= first_key)


ATT_HP = (1, 2, 2)


def _att_geometry(gi):
    _, dil = ATT_GROUPS[gi]
    return dil, S // dil // ATT_BLK, ATT_HP[gi]


def _blk(dil, r, n):
    if dil == 1:
        return pl.ds(n * ATT_BLK, ATT_BLK)
    return pl.ds(r + n * ATT_BLK * dil, ATT_BLK, stride=dil)


def _slab_specs(gi):
    _, _, hp = _att_geometry(gi)
    per = ATT_HG // hp
    return [pl.BlockSpec((hp, S, ATT_DH), lambda g, r, part=part: ((3 * gi + part) * per + g, 0, 0))
            for part in range(3)]


def _head_specs(gi, count):
    _, _, hp = _att_geometry(gi)
    return [pl.BlockSpec((hp, S, ATT_DH), lambda g, r: (g, 0, 0))] * count


def _bias_spec(gi):
    _, _, hp = _att_geometry(gi)
    return pl.BlockSpec((hp, ATT_BLK, 2 * ATT_BLK), lambda g, r: (gi * (ATT_HG // hp) + g, 0, 0))


def _att_fwd(slabs, bias, gi, comm=None):
    dil, nb, hp = _att_geometry(gi)
    scale = ATT_DH ** -0.5

    def body(q_ref, k_ref, v_ref, bias_ref, o_ref, l_ref):
        r = pl.program_id(1)
        for n in range(nb):
            valid = _att_valid(n)
            prev = _blk(dil, r, max(n - 1, 0))
            cur = _blk(dil, r, n)
            for h in range(hp):
                kk = jnp.concatenate([k_ref[h, prev, :], k_ref[h, cur, :]], axis=0)
                vv = jnp.concatenate([v_ref[h, prev, :], v_ref[h, cur, :]], axis=0)
                s = _dot(q_ref[h, cur, :], kk, NT) * scale + bias_ref[h]
                s = jnp.where(valid, s, -1e30)
                mx = jnp.max(s, axis=-1, keepdims=True)
                e = jnp.exp(s - mx)
                den = jnp.sum(e, axis=-1, keepdims=True)
                o_ref[h, cur, :] = _dot(e / den, vv, NN)
                l_ref[h, cur, :] = jnp.broadcast_to(mx + jnp.log(den), (ATT_BLK, ATT_DH))

    osh = jax.ShapeDtypeStruct((ATT_HG, S, ATT_DH), F32)
    kw = dict(name=f"att_fwd{gi}", grid=(ATT_HG // hp, dil), in_specs=_slab_specs(gi) + [_bias_spec(gi)],
              out_specs=tuple(_head_specs(gi, 2)), out_shape=(osh, osh))
    if comm is not None:
        return _carry(body, comm, **kw)(slabs, slabs, slabs, bias)
    return _pcall(body, compiler_params=_params(("parallel", "arbitrary")), **kw)(slabs, slabs, slabs, bias)


def _att_bwd(slabs, bias, o, lse, do, dlse, gi, comm=None):
    dil, nb, hp = _att_geometry(gi)
    per = ATT_HG // hp
    scale = ATT_DH ** -0.5
    wh = hp * ATT_DH
    wide = lambda t: jnp.concatenate([t, t], axis=1)

    def body(q_ref, k_ref, v_ref, bias_ref, o_ref, l_ref, do_ref, dl_ref, dq_ref, dk_ref, dv_ref, ds_ref):
        r = pl.program_id(1)

        @pl.when(r == 0)
        def _():
            ds_ref[...] = jnp.zeros_like(ds_ref)

        for h in range(hp):
            sl = slice(h * ATT_DH, (h + 1) * ATT_DH)
            carry_k = carry_v = None
            for n in range(nb):
                valid = _att_valid(n)
                prev = _blk(dil, r, max(n - 1, 0))
                cur = _blk(dil, r, n)
                q = q_ref[h, cur, :]
                kk = jnp.concatenate([k_ref[h, prev, :], k_ref[h, cur, :]], axis=0)
                vv = jnp.concatenate([v_ref[h, prev, :], v_ref[h, cur, :]], axis=0)
                dov = do_ref[h, cur, :]
                s = _dot(q, kk, NT) * scale + bias_ref[h]
                p = jnp.where(valid, jnp.exp(s - wide(l_ref[h, cur, :])), 0.0)
                dp = _dot(dov, vv, NT)
                delta = jnp.sum(dov * o_ref[h, cur, :], axis=-1, keepdims=True)
                ds = p * (dp - delta + wide(dl_ref[h, cur, :]))
                ds_ref[h] += ds
                out_rows = pl.ds(n * ATT_BLK, ATT_BLK)
                dq_ref[out_rows, sl] = (_dot(ds, kk, NN) * scale).astype(BF16)
                dkk = _dot(ds, q, TN) * scale
                dvv = _dot(p, dov, TN)
                if n > 0:
                    before = pl.ds((n - 1) * ATT_BLK, ATT_BLK)
                    dk_ref[before, sl] = (carry_k + dkk[:ATT_BLK]).astype(BF16)
                    dv_ref[before, sl] = (carry_v + dvv[:ATT_BLK]).astype(BF16)
                carry_k, carry_v = dkk[ATT_BLK:], dvv[ATT_BLK:]
            last = pl.ds((nb - 1) * ATT_BLK, ATT_BLK)
            dk_ref[last, sl] = carry_k.astype(BF16)
            dv_ref[last, sl] = carry_v.astype(BF16)

    out_spec = pl.BlockSpec((S // dil, wh), lambda g, r: (0, r * per + g))
    osh = jax.ShapeDtypeStruct((S // dil, dil * AW), BF16)
    kw = dict(name=f"att_bwd{gi}", grid=(per, dil), in_specs=_slab_specs(gi) + [_bias_spec(gi)] + _head_specs(gi, 4),
              out_specs=(out_spec, out_spec, out_spec,
                         pl.BlockSpec((hp, ATT_BLK, 2 * ATT_BLK), lambda g, r: (g, 0, 0))),
              out_shape=(osh, osh, osh, jax.ShapeDtypeStruct((ATT_HG, ATT_BLK, 2 * ATT_BLK), F32)))
    args = (slabs, slabs, slabs, bias, o, lse, do, dlse)
    if comm is not None:
        return _carry(body, comm, **kw)(*args)
    return _pcall(body, compiler_params=_params(("arbitrary", "arbitrary")), **kw)(*args)


AW = ATT_HG * ATT_DH


def _mix_weights(l0, l1, l2):
    mx = jnp.maximum(jnp.maximum(l0, l1), l2)
    e0, e1, e2 = jnp.exp(l0 - mx), jnp.exp(l1 - mx), jnp.exp(l2 - mx)
    den = e0 + e1 + e2
    return e0 / den, e1 / den, e2 / den


def _heads_spec():
    return pl.BlockSpec((ATT_HG, TR, ATT_DH), lambda i: (0, i, 0))


def _mix_fwd(os_, ls):
    def body(o0, o1, o2, l0, l1, l2, att_ref):
        for h in range(ATT_HG):
            w0, w1, w2 = _mix_weights(l0[h], l1[h], l2[h])
            att_ref[:, h * ATT_DH:(h + 1) * ATT_DH] = (w0 * o0[h] + w1 * o1[h] + w2 * o2[h]).astype(BF16)

    return _pcall(body, name="mix_fwd", grid=(S // TR,), in_specs=[_heads_spec()] * 6, out_specs=_row_spec(AW),
                  out_shape=jax.ShapeDtypeStruct((S, AW), BF16), compiler_params=_params(("parallel",)))(*os_, *ls)


def _mix_bwd(os_, ls, datt):
    def body(o0, o1, o2, l0, l1, l2, da_ref, d0, d1, d2, e0, e1, e2):
        for h in range(ATT_HG):
            ws = _mix_weights(l0[h], l1[h], l2[h])
            da = da_ref[:, h * ATT_DH:(h + 1) * ATT_DH]
            dws = []
            for o_ref, w, d_ref in zip((o0, o1, o2), ws, (d0, d1, d2)):
                d_ref[h] = w * da
                dws.append(jnp.broadcast_to(jnp.sum(da * o_ref[h], axis=-1, keepdims=True), (TR, ATT_DH)))
            tot = ws[0] * dws[0] + ws[1] * dws[1] + ws[2] * dws[2]
            for w, dw, e_ref in zip(ws, dws, (e0, e1, e2)):
                e_ref[h] = w * (dw - tot)

    o = jax.ShapeDtypeStruct((ATT_HG, S, ATT_DH), F32)
    return _pcall(body, name="mix_bwd", grid=(S // TR,), in_specs=[_heads_spec()] * 6 + [_row_spec(AW)],
                  out_specs=(_heads_spec(),) * 6, out_shape=(o,) * 6,
                  compiler_params=_params(("parallel",)))(*os_, *ls, datt)


def _ada_fwd(c_all, w_sh, b_sl):
    def body(c_ref, w_ref, b_ref, o_ref):
        cv = c_ref[...]
        o_ref[...] = _dot(cv * jax.nn.sigmoid(cv), w_ref[...], NN) + b_ref[...]

    return _pcall(body, name="ada_fwd", out_shape=jax.ShapeDtypeStruct((N_DEV, w_sh.shape[1]), F32),
                  compiler_params=_params())(c_all, w_sh, b_sl)


def _ada_bwd(c_all, dm_sl):
    def body(c_ref, d_ref, o_ref):
        cv = c_ref[...]
        o_ref[...] = _dot(cv * jax.nn.sigmoid(cv), d_ref[...], TN)

    return _pcall(body, name="ada_bwd", out_shape=jax.ShapeDtypeStruct((D, dm_sl.shape[1]), F32),
                  compiler_params=_params())(c_all, dm_sl)


N_MOD = 6


def _sum_small(gathered):
    n = len(gathered)

    def body(*refs):
        ins, (gb_ref, dm_ref), outs = refs[:n], refs[n:n + 2], refs[n + 2:]

        def total(r):
            acc = r[0]
            for e in range(1, N_DEV):
                acc = acc + r[e]
            return acc

        for i in range(N_MOD):
            cols = slice(i * D, (i + 1) * D)
            gb_ref[:, cols] = total(ins[i])
            for e in range(N_DEV):
                dm_ref[e:e + 1, cols] = ins[i][e]
        for r, o_ref in zip(ins[N_MOD:], outs):
            o_ref[...] = total(r)

    shapes = (jax.ShapeDtypeStruct((1, N_MOD * D), F32), jax.ShapeDtypeStruct((N_DEV, N_MOD * D), F32),
              *[jax.ShapeDtypeStruct(g.shape[1:], F32) for g in gathered[N_MOD:]])
    res = _pcall(body, name="sum_small", out_shape=shapes, compiler_params=_params())(*gathered)
    return res[0], res[1], res[2:]


def _row_tile(m, n):
    t = max(8, min(m, (1 << 19) // n // 8 * 8))
    while m % t:
        t -= 8
    return t


def _pair_sum(full, recv, sel, name):
    _, _, m, n = full.shape
    t = _row_tile(m, n)

    def body(sel_ref, a_ref, b_ref, o_ref):
        o_ref[...] = (a_ref[...].astype(F32) + b_ref[...].astype(F32)).astype(o_ref.dtype)

    gs = pltpu.PrefetchScalarGridSpec(
        num_scalar_prefetch=1, grid=(4, m // t),
        in_specs=[pl.BlockSpec((None, None, t, n), lambda q, i, s: (q, s[0], i, 0)),
                  pl.BlockSpec((None, t, n), lambda q, i, s: (q, i, 0))],
        out_specs=pl.BlockSpec((None, t, n), lambda q, i, s: (q, i, 0)))
    return _pcall(body, name=name, grid_spec=gs, out_shape=jax.ShapeDtypeStruct((4, m, n), full.dtype),
                  compiler_params=_params(("parallel", "parallel")))(sel, full, recv)


def _chip_sum(part, recv, sel, name):
    _, m, n = part.shape
    t = _row_tile(m, n)

    def body(sel_ref, a_ref, r_ref, o_ref):
        o_ref[...] = ((a_ref[...].astype(F32) + r_ref[0].astype(F32)) + r_ref[1].astype(F32)) + r_ref[2].astype(F32)

    gs = pltpu.PrefetchScalarGridSpec(
        num_scalar_prefetch=1, grid=(m // t,),
        in_specs=[pl.BlockSpec((None, t, n), lambda i, s: (s[0], i, 0)),
                  pl.BlockSpec((3, t, n), lambda i, s: (0, i, 0))],
        out_specs=pl.BlockSpec((t, n), lambda i, s: (i, 0)))
    return _pcall(body, name=name, grid_spec=gs, out_shape=jax.ShapeDtypeStruct((m, n), F32),
                  compiler_params=_params(("parallel",)))(sel, part, recv)


def _adamw_math(w, g, m, v):
    nm = ADAM_B1 * m + (1.0 - ADAM_B1) * g
    nv = ADAM_B2 * v + (1.0 - ADAM_B2) * (g * g)
    m_hat = nm / (1.0 - ADAM_B1 ** ADAM_STEP)
    v_hat = nv / (1.0 - ADAM_B2 ** ADAM_STEP)
    return -ADAM_LR * (m_hat / (jnp.sqrt(v_hat) + ADAM_EPS) + ADAM_WD * w), nm, nv


def _adamw(w, g, m, v, name):
    _, rows, cols = w.shape
    t = _row_tile(rows, cols)

    def body(w_ref, g_ref, m_ref, v_ref, d_ref, nm_ref, nv_ref):
        d_ref[...], nm_ref[...], nv_ref[...] = _adamw_math(w_ref[...], g_ref[...], m_ref[...], v_ref[...])

    spec3 = pl.BlockSpec((None, t, cols), lambda i: (0, i, 0))
    spec2 = pl.BlockSpec((t, cols), lambda i: (i, 0))
    o = jax.ShapeDtypeStruct(w.shape, F32)
    return _pcall(body, name=name, grid=(rows // t,), in_specs=[spec3, spec2, spec3, spec3], out_specs=(spec3,) * 3,
                  out_shape=(o, o, o), compiler_params=_params(("parallel",)))(w, g, m, v)


def _adamw_small(ws, gs, ms, vs):
    n = len(ws)

    def body(*refs):
        for i in range(n):
            w_ref, g_ref, m_ref, v_ref = (refs[k * n + i] for k in range(4))
            d, nm, nv = _adamw_math(w_ref[...], g_ref[...], m_ref[...], v_ref[...])
            refs[4 * n + i][...] = d
            refs[5 * n + i][...] = nm
            refs[6 * n + i][...] = nv

    shapes = tuple(jax.ShapeDtypeStruct(w.shape, F32) for w in ws)
    res = _pcall(body, name="adamw_small", out_shape=shapes * 3, compiler_params=_params())(*ws, *gs, *ms, *vs)
    return res[:n], res[n:2 * n], res[2 * n:]


def _mesh_pos():
    return lax.axis_index("x"), lax.axis_index("y"), lax.axis_index("c")


class _Gather:
    def __init__(self, arrs, relay=False):
        self.relay = relay
        self.ins = list(arrs)
        na = self.na = len(arrs)
        self.out_shape = tuple(jax.ShapeDtypeStruct((N_DEV,) + a.shape, a.dtype) for a in arrs)
        self.sems = [pltpu.SemaphoreType.DMA((7 * na,)), pltpu.SemaphoreType.DMA((7 * na,)),
                     pltpu.SemaphoreType.DMA((na,))]

    def _copies(self, ins, outs, sems):
        send_sems, recv_sems, local_sems = sems
        x, y, c = _mesh_pos()
        me, sibling = (x, y, c), (x, y, 1 - c)
        chips = [(1 - x, y), (x, 1 - y), (1 - x, 1 - y)]

        def slot(p):
            return 4 * p[0] + 2 * p[1] + p[2]

        def copy(a, k, block, to, src=None):
            dst = outs[a].at[slot(block)]
            return pltpu.make_async_remote_copy(
                src_ref=dst if src is None else src, dst_ref=dst, send_sem=send_sems.at[7 * a + k],
                recv_sem=recv_sems.at[7 * a + k], device_id=to, device_id_type=MESH)

        mine = [pltpu.make_async_copy(ins[a], outs[a].at[slot(me)], local_sems.at[a]) for a in range(self.na)]
        direct = chips[:2] if self.relay else chips
        first = []
        for a in range(self.na):
            first.append(copy(a, 0, me, sibling, src=ins[a]))
            first += [copy(a, 1 + j, me, (*chip, c), src=ins[a]) for j, chip in enumerate(direct)]
        return me, sibling, chips, c, copy, mine, first

    def start(self, ins, outs, sems):
        *_, mine, first = self._copies(ins, outs, sems)
        for cp in mine + first:
            cp.start()

    def finish(self, ins, outs, sems):
        me, sibling, chips, c, copy, mine, first = self._copies(ins, outs, sems)
        x, y = me[0], me[1]
        passed = []
        for j, chip in enumerate(chips):
            for a in range(self.na):
                if self.relay and j == 2:
                    owner = ((x + 1 - c) % 2, (y + c) % 2, c)
                    cp = copy(a, 3, owner, ((x + c) % 2, (y + 1 - c) % 2, c))
                    cp.start()
                    passed.append(cp)
                copy(a, 1 + j, (*chip, c), me).wait_recv()
                cp = copy(a, 4 + j, (*chip, c), sibling)
                cp.start()
                passed.append(cp)
        for a in range(self.na):
            copy(a, 0, sibling, me).wait_recv()
            for j, chip in enumerate(chips):
                copy(a, 4 + j, (*chip, 1 - c), me).wait_recv()
        for cp in first + passed:
            cp.wait_send()
        for cp in mine:
            cp.wait()


class _ExchangeCore:
    def __init__(self, fulls):
        self.ins = list(fulls)
        self.out_shape = tuple(jax.ShapeDtypeStruct((4,) + f.shape[2:], f.dtype) for f in fulls)
        self.sems = [pltpu.SemaphoreType.DMA((4 * len(fulls),)), pltpu.SemaphoreType.DMA((4 * len(fulls),))]

    def _copies(self, ins, outs, sems):
        send_sems, recv_sems = sems
        x, y, c = _mesh_pos()
        return [pltpu.make_async_remote_copy(
            src_ref=ins[a].at[q, 1 - c], dst_ref=outs[a].at[q], send_sem=send_sems.at[4 * a + q],
            recv_sem=recv_sems.at[4 * a + q], device_id=(x, y, 1 - c), device_id_type=MESH)
            for a in range(len(self.ins)) for q in range(4)]

    def start(self, ins, outs, sems):
        for cp in self._copies(ins, outs, sems):
            cp.start()

    def finish(self, ins, outs, sems):
        for cp in self._copies(ins, outs, sems):
            cp.wait()


class _ExchangeChip:
    def __init__(self, parts):
        self.ins = list(parts)
        self.out_shape = tuple(jax.ShapeDtypeStruct((3,) + p.shape[1:], p.dtype) for p in parts)
        self.sems = [pltpu.SemaphoreType.DMA((3 * len(parts),)), pltpu.SemaphoreType.DMA((3 * len(parts),))]

    def _copies(self, ins, outs, sems):
        send_sems, recv_sems = sems
        x, y, c = _mesh_pos()
        chips = [(1 - x, y), (x, 1 - y), (1 - x, 1 - y)]
        return [pltpu.make_async_remote_copy(
            src_ref=ins[a].at[2 * px + py], dst_ref=outs[a].at[j], send_sem=send_sems.at[3 * a + j],
            recv_sem=recv_sems.at[3 * a + j], device_id=(px, py, c), device_id_type=MESH)
            for a in range(len(self.ins)) for j, (px, py) in enumerate(chips)]

    def start(self, ins, outs, sems):
        for cp in self._copies(ins, outs, sems):
            cp.start()

    def finish(self, ins, outs, sems):
        for cp in self._copies(ins, outs, sems):
            cp.wait()


def _reduce_sums(fulls, recv_core, core, tag):
    return [_pair_sum(f, r, core, f"rs_pair_{tag}{i}") for i, (f, r) in enumerate(zip(fulls, recv_core))]


def _local_step(x, tgt, mods, w_in_t, shards, small, chip, core):
    sh1, sc1, g1, sh2, sc2, g2 = mods
    norm1_g, rel_bias, gn_g, gn_b, norm2_g, norm_f_g = small
    tables = _ret_tables()
    buckets = jnp.asarray(_bucket_tables())

    h1 = _norm_mod_fwd(x, norm1_g, sh1, sc1, "norm1_fwd")
    proj, slabs, gathered = _proj(h1, w_in_t, _Gather(shards[:3]))
    w_ret_out, w_att_out, w_o = (_from_slots(g, ax) for g, ax in zip(gathered, BIG_AXES[1:4]))
    (gated, ro, states), (w_ff1_8,) = _ret_fwd(proj, tables, gn_g, gn_b, comm=_Gather(shards[3:4]))
    w_ff1 = _from_slots(w_ff1_8, BIG_AXES[4])
    bias = _bias_build(rel_bias, buckets)
    outs, lses = [], []
    for gi in range(len(ATT_GROUPS)):
        res = _att_fwd(slabs, bias, gi, comm=_Gather(shards[4:]) if gi == 2 else None)
        if gi == 2:
            res, (w_ff2_8,) = res
        outs.append(res[0])
        lses.append(res[1])
    w_ff2 = _from_slots(w_ff2_8, BIG_AXES[5])
    att = _mix_fwd(outs, lses)
    ret_out = _mm(gated, w_ret_out, 'nn', tm=S, tn=256, tk=2048, name="ret_out")
    att_out = _mm(att, w_att_out, 'nn', tm=S, tn=512, tk=AW, name="att_out")
    merged = _merge_fwd(proj, ret_out, att_out)
    mixo, x1 = _mm(merged, w_o, 'nn', tm=S, tn=256, tk=D, name="w_o", res=x, gvec=g1)
    h2 = _norm_mod_fwd(x1, norm2_g, sh2, sc2, "norm2_fwd")
    u, act = _mm(h2, w_ff1, 'nn', tm=S, tn=512, tk=D, name="ff1", relu2=True)
    f, x2 = _mm(act, w_ff2, 'nn', tm=1024, tn=512, tk=2048, name="ff2", res=x1, gvec=g2)
    loss, dx2, g_normf, df, dg2 = _final_loss(x2, tgt, norm_f_g, f, g2)

    gw_ff2 = _mm(act, df, 'tn', tm=512, tn=D, tk=S, name="gw_ff2", out_dtype=BF16)
    du = _mm(df, w_ff2, 'nt', tm=S, tn=512, tk=D, name="d_act", out_dtype=BF16, relu2_of=u)
    gw_ff1 = _mm(h2, du, 'tn', tm=D, tn=512, tk=S, name="gw_ff1", out_dtype=BF16)
    fulls_a = [_to_slots(g, ax) for g, ax in zip((gw_ff1, gw_ff2), BIG_AXES[4:])]
    dh2, recv_core_a = _mm(du, w_ff1, 'nt', tm=1024, tn=1024, tk=2048, name="dh2", comm=_ExchangeCore(fulls_a))
    parts_a = _reduce_sums(fulls_a, recv_core_a, core, "a")
    dx1, dsc2, dsh2, g_norm2, dmixo, dg1 = _norm_mod_bwd(x1, norm2_g, sc2, dh2, dx2, "norm2_bwd", gate=(mixo, g1))

    gw_o = _mm(merged, dmixo, 'tn', tm=D, tn=512, tk=S, name="gw_o", out_dtype=BF16)
    dmerged = _mm(dmixo, w_o, 'nt', tm=S, tn=512, tk=D, name="dmerged")
    d_ret_out, d_att_out, dga, dgb = _merge_bwd(proj, ret_out, att_out, dmerged)
    gw_ret_out = _mm(gated, d_ret_out, 'tn', tm=512, tn=D, tk=S, name="gw_ret_out", out_dtype=BF16)
    gw_att_out = _mm(att, d_att_out, 'tn', tm=AW, tn=D, tk=S, name="gw_att_out", out_dtype=BF16)
    fulls_b = [_to_slots(g, ax) for g, ax in zip((gw_ret_out, gw_att_out, gw_o), BIG_AXES[1:4])]
    dgated, recv_core_b = _mm(d_ret_out, w_ret_out, 'nt', tm=S, tn=512, tk=D, name="dgated",
                              comm=_ExchangeCore(fulls_b))
    parts_b = _reduce_sums(fulls_b, recv_core_b, core, "b")
    datt = _mm(d_att_out, w_att_out, 'nt', tm=S, tn=AW, tk=D, name="datt")
    mix_grads = _mix_bwd(outs, lses, datt)
    datt_parts, ds_sums = [], []
    for gi in range(len(ATT_GROUPS)):
        comm = {1: _ExchangeChip(parts_b), 2: _ExchangeChip(parts_a[:1])}.get(gi)
        res = _att_bwd(slabs, bias, outs[gi], lses[gi], mix_grads[gi], mix_grads[3 + gi], gi, comm=comm)
        if gi == 1:
            res, recv_chip_b = res
        if gi == 2:
            res, recv_chip_a1 = res
        dq, dk, dv, ds_sum = res
        datt_parts += [dq.reshape(S, AW), dk.reshape(S, AW), dv.reshape(S, AW)]
        ds_sums.append(ds_sum)
    red_b = [_chip_sum(p, r, chip, f"rs_sum_b{i}") for i, (p, r) in enumerate(zip(parts_b, recv_chip_b))]
    g_bias = _bias_grad(jnp.concatenate(ds_sums, axis=0), buckets)[:, :, 0].T.reshape(1, -1)
    (dret, g_gn_g, g_gn_b), recv_chip_a2 = _ret_bwd(proj, tables, gn_g, gn_b, ro, states, dgated,
                                                    comm=_ExchangeChip(parts_a[1:]))
    recv_chip_a = list(recv_chip_a1) + list(recv_chip_a2)
    red_a = [_chip_sum(p, r, chip, f"rs_sum_a{i}") for i, (p, r) in enumerate(zip(parts_a, recv_chip_a))]
    dproj = jnp.concatenate([dret] + datt_parts + [dga, dgb], axis=1)
    gw_in_t = _mm(dproj, h1, 'tn', tm=512, tn=D, tk=S, name="gw_in", out_dtype=BF16)
    full_in = [_to_slots(gw_in_t, 0)]
    recv_core_in = _run_comm(_ExchangeCore(full_in), "rs_core_in")
    part_in = _reduce_sums(full_in, recv_core_in, core, "c")
    dh1, recv_chip_in = _mm(dproj, w_in_t, 'nn', tm=1024, tn=1024, tk=2560, name="dh1", comm=_ExchangeChip(part_in))
    gx, dsc1, dsh1, g_norm1 = _norm_mod_bwd(x, norm1_g, sc1, dh1, dx1, "norm1_bwd")
    red_in = _chip_sum(part_in[0], recv_chip_in[0], chip, "rs_sum_c")

    dmod = [dsh1, dsc1, dg1, dsh2, dsc2, dg2]
    small_g = [g_norm1, g_bias, g_gn_g, g_gn_b, g_norm2, g_normf]
    return loss, gx, [red_in] + red_b + red_a, small_g, dmod


def _to_slots(g, axis):
    if axis == 0:
        return g.reshape(4, 2, g.shape[0] // N_DEV, g.shape[1])
    return g.reshape(g.shape[0], N_DEV, g.shape[1] // N_DEV).transpose(1, 0, 2).reshape(4, 2, g.shape[0], -1)


def _from_slots(w8, axis):
    if axis == 0:
        return w8.reshape(-1, w8.shape[2])
    return w8.transpose(1, 0, 2).reshape(w8.shape[1], -1)


BIG_AXES = (1, 0, 1, 0, 1, 0)


def kernel(x, c, w_ada, b_ada, norm1_g, w_in, rel_bias, ret_gn_g, ret_gn_b, w_ret_out, w_att_out, w_o, norm2_g, w_ff1, w_ff2, norm_f_g, loss_target, m_w_ada, m_b_ada, m_norm1_g, m_w_in, m_rel_bias, m_ret_gn_g, m_ret_gn_b, m_w_ret_out, m_w_att_out, m_w_o, m_norm2_g, m_w_ff1, m_w_ff2, m_norm_f_g, v_w_ada, v_b_ada, v_norm1_g, v_w_in, v_rel_bias, v_ret_gn_g, v_ret_gn_b, v_w_ret_out, v_w_att_out, v_w_o, v_norm2_g, v_w_ff1, v_w_ff2, v_norm_f_g):
    mx, my, mc = _mesh_pos()
    dev = 4 * mx + 2 * my + mc
    chip = jnp.reshape(2 * mx + my, (1,)).astype(jnp.int32)
    core = jnp.reshape(mc, (1,)).astype(jnp.int32)
    ada_w = D * 6 // N_DEV

    w_in, m_w_in, v_w_in = (jnp.transpose(t, (0, 2, 1)) for t in (w_in, m_w_in, v_w_in))

    shards = [w[0].astype(BF16) for w in (w_in, w_ret_out, w_att_out, w_o, w_ff1, w_ff2)]
    c_all, w_in8 = _run_comm(_Gather([c, shards[0]], relay=True), "gather_c_w_in")
    c_all = c_all.reshape(N_DEV, D)
    b_sl = lax.dynamic_slice(b_ada, (0, dev * ada_w), (1, ada_w))
    (mod_all,) = _run_comm(_Gather([_ada_fwd(c_all, w_ada[0], b_sl)]), "gather_mod")
    mod = lax.dynamic_index_in_dim(mod_all, dev, axis=1, keepdims=False).reshape(6, D)
    mods = tuple(mod[i:i + 1] for i in range(6))

    small = (norm1_g, rel_bias, ret_gn_g, ret_gn_b, norm2_g, norm_f_g.reshape(1, D))
    loss, gx, big_red, small_g, dmod = _local_step(x[0], loss_target[0], mods, w_in8.reshape(IN_COLS, D),
                                                   shards[1:], small, chip, core)

    gathered = _run_comm(_Gather(dmod + small_g + [loss]), "gather_small")
    g_b_ada, dmod_all, (g_norm1, g_bias, g_gn_g, g_gn_b, g_norm2, g_normf, loss_sum) = _sum_small(gathered)
    loss_out = loss_sum[0, 0]
    g_w_ada = _ada_bwd(c_all, lax.dynamic_slice(dmod_all, (0, dev * ada_w), (N_DEV, ada_w)))

    names = ['w_ada', 'b_ada', 'norm1_g', 'w_in', 'rel_bias', 'ret_gn_g', 'ret_gn_b', 'w_ret_out', 'w_att_out',
             'w_o', 'norm2_g', 'w_ff1', 'w_ff2', 'norm_f_g']
    ws = dict(zip(names, (w_ada, b_ada, norm1_g, w_in, rel_bias, ret_gn_g, ret_gn_b, w_ret_out, w_att_out, w_o,
                          norm2_g, w_ff1, w_ff2, norm_f_g)))
    ms = dict(zip(names, (m_w_ada, m_b_ada, m_norm1_g, m_w_in, m_rel_bias, m_ret_gn_g, m_ret_gn_b, m_w_ret_out,
                          m_w_att_out, m_w_o, m_norm2_g, m_w_ff1, m_w_ff2, m_norm_f_g)))
    vs = dict(zip(names, (v_w_ada, v_b_ada, v_norm1_g, v_w_in, v_rel_bias, v_ret_gn_g, v_ret_gn_b, v_w_ret_out,
                          v_w_att_out, v_w_o, v_norm2_g, v_w_ff1, v_w_ff2, v_norm_f_g)))
    grads = dict(w_ada=g_w_ada, w_in=big_red[0], w_ret_out=big_red[1], w_att_out=big_red[2], w_o=big_red[3],
                 w_ff1=big_red[4], w_ff2=big_red[5], b_ada=g_b_ada, norm1_g=g_norm1, rel_bias=g_bias,
                 ret_gn_g=g_gn_g, ret_gn_b=g_gn_b, norm2_g=g_norm2, norm_f_g=g_normf)
    delta, new_m, new_v = {}, {}, {}
    for n in ('w_ada', 'w_in', 'w_ret_out', 'w_att_out', 'w_o', 'w_ff1', 'w_ff2'):
        delta[n], new_m[n], new_v[n] = _adamw(ws[n], grads[n], ms[n], vs[n], "adamw_" + n)
        grads[n] = grads[n].reshape(ws[n].shape)
    for d in (grads, delta, new_m, new_v):
        d['w_in'] = jnp.transpose(d['w_in'], (0, 2, 1))
    small_names = ('b_ada', 'norm1_g', 'rel_bias', 'ret_gn_g', 'ret_gn_b', 'norm2_g', 'norm_f_g')
    two_d = {n: (1, ws[n].size) if ws[n].ndim == 1 else ws[n].shape for n in small_names}
    d_, m_, v_ = _adamw_small(*[[src[n].reshape(two_d[n]) for n in small_names] for src in (ws, grads, ms, vs)])
    for i, n in enumerate(small_names):
        shp = ws[n].shape
        delta[n], new_m[n], new_v[n] = d_[i].reshape(shp), m_[i].reshape(shp), v_[i].reshape(shp)
        grads[n] = grads[n].reshape(shp)
    return (loss_out, gx[None], *[grads[n] for n in names], *[delta[n] for n in names],
            *[new_m[n] for n in names], *[new_v[n] for n in names])
```

```python
import functools
import math

import numpy as np
import jax
import jax.numpy as jnp
from jax import lax
from jax.experimental import pallas as pl
from jax.experimental.pallas import tpu as pltpu

F32 = jnp.float32
BF16 = jnp.bfloat16
MESH = pl.DeviceIdType.MESH

N_DEV = 8
S = 2048
D = 1024
RET_HEADS = 4
RET_DK = 256
RET_DV = 512
CHUNK = 128
N_CHUNK = S // CHUNK
ATT_GROUPS = ((128, 1), (512, 4), (2048, 16))
ATT_HG = 4
ATT_DH = 128
ATT_BLK = 128
N_BUCKETS = 32
MAX_DIST = 2048
D_FF = 4096
IN_COLS = 12800
OFF_RQ, OFF_RK, OFF_RV, OFF_RG, OFF_ATT = 0, 1024, 2048, 4096, 6144
OFF_GA, OFF_GB = 6144, 7168
RMS_EPS = 1e-6
GN_EPS = 1e-5
ADAM_LR, ADAM_B1, ADAM_B2, ADAM_EPS, ADAM_WD, ADAM_STEP = 0.001, 0.9, 0.999, 1e-08, 0.01, 10
VMEM_LIMIT = 48 * 1024 * 1024


def _pcall(body, **kw):
    return pl.pallas_call(body, **kw)


def _params(sem=None):
    return pltpu.CompilerParams(dimension_semantics=sem, vmem_limit_bytes=VMEM_LIMIT)


HBM_SPEC = pl.BlockSpec(memory_space=pl.ANY)


def _carry(body, comm, *, name, grid, in_specs, out_specs, out_shape, scratch_shapes=()):
    single = not isinstance(out_specs, (tuple, list))
    o_specs = (out_specs,) if single else tuple(out_specs)
    o_shape = (out_shape,) if single else tuple(out_shape)
    n_in, n_out, n_scr = len(in_specs), len(o_specs), len(scratch_shapes)
    nci, nco = len(comm.ins), len(comm.out_shape)
    total = int(np.prod(grid))

    def wrapped(*refs):
        bounds = np.cumsum([0, n_in, nci, n_out, nco, n_scr])
        a, ci, o, co, scr = (refs[bounds[i]:bounds[i + 1]] for i in range(5))
        sems = refs[bounds[5]:]
        flat = 0
        for d, g in enumerate(grid):
            flat = flat * g + pl.program_id(d)

        @pl.when(flat == 0)
        def _():
            comm.start(ci, co, sems)

        body(*a, *o, *scr)

        @pl.when(flat == total - 1)
        def _():
            comm.finish(ci, co, sems)

    call = _pcall(wrapped, name=name, grid=grid, in_specs=list(in_specs) + [HBM_SPEC] * nci,
                  out_specs=o_specs + (HBM_SPEC,) * nco, out_shape=o_shape + tuple(comm.out_shape),
                  scratch_shapes=list(scratch_shapes) + list(comm.sems),
                  compiler_params=_params(("arbitrary",) * len(grid)))

    def run(*args):
        res = call(*args, *comm.ins)
        own = res[0] if single else tuple(res[:n_out])
        return own, tuple(res[n_out:])

    return run


def _run_comm(comm, name):
    nci, nco = len(comm.ins), len(comm.out_shape)

    def body(*refs):
        ci, co, sems = refs[:nci], refs[nci:nci + nco], refs[nci + nco:]
        comm.start(ci, co, sems)
        comm.finish(ci, co, sems)

    return _pcall(body, name=name, in_specs=[HBM_SPEC] * nci, out_specs=(HBM_SPEC,) * nco,
                  out_shape=tuple(comm.out_shape), scratch_shapes=list(comm.sems))(*comm.ins)


def _dot(a, b, dn):
    return lax.dot_general(a.astype(BF16), b.astype(BF16), (dn, ((), ())), preferred_element_type=F32)


NN = ((1,), (0,))
NT = ((1,), (1,))
TN = ((0,), (0,))


def _mm(a, b, mode, *, tm, tn, tk, name, out_dtype=F32, res=None, gvec=None, relu2=False, relu2_of=None, comm=None):
    if mode == 'nn':
        (M, K), (_, N) = a.shape, b.shape
        a_spec = pl.BlockSpec((tm, tk), lambda i, j, k: (i, k))
        b_spec = pl.BlockSpec((tk, tn), lambda i, j, k: (k, j))
        dn = NN
    elif mode == 'nt':
        (M, K), (N, _) = a.shape, b.shape
        a_spec = pl.BlockSpec((tm, tk), lambda i, j, k: (i, k))
        b_spec = pl.BlockSpec((tn, tk), lambda i, j, k: (j, k))
        dn = NT
    else:
        (K, M), (_, N) = a.shape, b.shape
        a_spec = pl.BlockSpec((tk, tm), lambda i, j, k: (k, i))
        b_spec = pl.BlockSpec((tk, tn), lambda i, j, k: (k, j))
        dn = TN
    assert M % tm == 0 and N % tn == 0 and K % tk == 0, (name, M, N, K)
    nk = K // tk
    fused = res is not None
    o_spec = pl.BlockSpec((tm, tn), lambda i, j, k: (i, j))

    def body(a_ref, b_ref, *rest):
        acc_ref = rest[-1] if nk > 1 else None
        if fused:
            res_ref, g_ref, o_ref, x_ref = rest[:4]
        elif relu2_of is not None:
            u_ref, o_ref = rest[:2]
        elif relu2:
            o_ref, act_ref = rest[:2]
        else:
            o_ref = rest[0]

        def finish(acc):
            if relu2_of is not None:
                acc = acc * (2.0 * jnp.maximum(u_ref[...], 0.0))
            o_ref[...] = acc.astype(o_ref.dtype)
            if fused:
                x_ref[...] = res_ref[...] + g_ref[...] * acc
            if relu2:
                r = jnp.maximum(acc, 0.0)
                act_ref[...] = (r * r).astype(BF16)

        p = _dot(a_ref[...], b_ref[...], dn)
        if nk == 1:
            finish(p)
        else:
            k = pl.program_id(2)

            @pl.when(k == 0)
            def _():
                acc_ref[...] = p

            @pl.when(k > 0)
            def _():
                acc_ref[...] += p

            @pl.when(k == nk - 1)
            def _():
                finish(acc_ref[...])

    in_specs = [a_spec, b_spec]
    args = [a, b]
    out_shape = jax.ShapeDtypeStruct((M, N), out_dtype)
    out_specs = o_spec
    if fused:
        in_specs += [pl.BlockSpec((tm, tn), lambda i, j, k: (i, j)), pl.BlockSpec((1, tn), lambda i, j, k: (0, j))]
        args += [res, gvec]
        out_shape = (out_shape, jax.ShapeDtypeStruct((M, N), F32))
        out_specs = (o_spec, pl.BlockSpec((tm, tn), lambda i, j, k: (i, j)))
    elif relu2_of is not None:
        in_specs.append(pl.BlockSpec((tm, tn), lambda i, j, k: (i, j)))
        args.append(relu2_of)
    elif relu2:
        out_shape = (out_shape, jax.ShapeDtypeStruct((M, N), BF16))
        out_specs = (o_spec, pl.BlockSpec((tm, tn), lambda i, j, k: (i, j)))
    kw = dict(name=name, grid=(M // tm, N // tn, nk), in_specs=in_specs, out_specs=out_specs,
              out_shape=out_shape, scratch_shapes=[pltpu.VMEM((tm, tn), F32)] if nk > 1 else [])
    if comm is not None:
        return _carry(body, comm, **kw)(*args)
    return _pcall(body, compiler_params=_params(("parallel", "parallel", "arbitrary")), **kw)(*args)


PROJ_TN = 512
ATT_T0, ATT_T1 = 6144 // PROJ_TN, 10752 // PROJ_TN
N_SLABS = (ATT_T1 - ATT_T0) * 4
MAIN_COLS = IN_COLS - (ATT_T1 - ATT_T0) * PROJ_TN


def _proj(h1, w_in_t, comm):
    nj = IN_COLS // PROJ_TN

    def body(a_ref, b_ref, main_ref, slab_ref):
        j = pl.program_id(1)
        is_att = (j >= ATT_T0) & (j < ATT_T1)
        chunks = [pl.ds(c * 512, 512) for c in range(S // 512)]

        @pl.when(jnp.logical_not(is_att))
        def _():
            for rows in chunks:
                main_ref[rows, :] = _dot(a_ref[rows, :], b_ref[...], NT)

        @pl.when(is_att)
        def _():
            for rows in chunks:
                p = _dot(a_ref[rows, :], b_ref[...], NT)
                for h in range(4):
                    slab_ref[h, rows, :] = p[:, h * 128:(h + 1) * 128]

    main_idx = lambda j: jnp.where(j < ATT_T0, j, jnp.where(j < ATT_T1, ATT_T0 - 1, j - (ATT_T1 - ATT_T0)))
    slab_idx = lambda j: jnp.clip(j - ATT_T0, 0, ATT_T1 - ATT_T0 - 1)
    (main, slabs), got = _carry(
        body, comm, name="proj", grid=(1, nj, 1),
        in_specs=[pl.BlockSpec((S, D), lambda i, j, k: (0, 0)), pl.BlockSpec((PROJ_TN, D), lambda i, j, k: (j, 0))],
        out_specs=(pl.BlockSpec((S, PROJ_TN), lambda i, j, k: (0, main_idx(j))),
                   pl.BlockSpec((4, S, 128), lambda i, j, k: (slab_idx(j), 0, 0))),
        out_shape=(jax.ShapeDtypeStruct((S, MAIN_COLS), F32), jax.ShapeDtypeStruct((N_SLABS, S, 128), F32)))(h1, w_in_t)
    return main, slabs, got


TR = 256


def _row_spec(w=D):
    return pl.BlockSpec((TR, w), lambda i: (i, 0))


def _vec_spec(w=D):
    return pl.BlockSpec((1, w), lambda i: (0, 0))


def _norm_mod_fwd(x, g, sh, sc, name):
    def body(x_ref, g_ref, sh_ref, sc_ref, o_ref):
        xv = x_ref[...]
        rstd = lax.rsqrt(jnp.mean(xv * xv, axis=-1, keepdims=True) + RMS_EPS)
        n = xv * rstd * g_ref[...]
        o_ref[...] = (n * (1.0 + sc_ref[...]) + sh_ref[...]).astype(BF16)

    return _pcall(body, name=name, grid=(S // TR,), in_specs=[_row_spec(), _vec_spec(), _vec_spec(), _vec_spec()],
                  out_specs=_row_spec(), out_shape=jax.ShapeDtypeStruct((S, D), BF16),
                  compiler_params=_params(("parallel",)))(x, g, sh, sc)


def _norm_mod_bwd(x, g, sc, dh, dres, name, gate=None):
    gated = gate is not None

    def body(x_ref, g_ref, sc_ref, dh_ref, dres_ref, *rest):
        if gated:
            f_ref, gv_ref, dx_ref, dsc_ref, dsh_ref, dg_ref, dz_ref, dgv_ref = rest
        else:
            dx_ref, dsc_ref, dsh_ref, dg_ref = rest
        i = pl.program_id(0)
        xv = x_ref[...]
        dh = dh_ref[...]
        rstd = lax.rsqrt(jnp.mean(xv * xv, axis=-1, keepdims=True) + RMS_EPS)
        xhat = xv * rstd
        gv = g_ref[...]
        dn = dh * (1.0 + sc_ref[...])
        dxhat = dn * gv
        dx = dres_ref[...] + rstd * (dxhat - xhat * jnp.mean(dxhat * xhat, axis=-1, keepdims=True))
        dx_ref[...] = dx
        sums = [(dsc_ref, jnp.sum(dh * (xhat * gv), axis=0, keepdims=True)),
                (dsh_ref, jnp.sum(dh, axis=0, keepdims=True)),
                (dg_ref, jnp.sum(dn * xhat, axis=0, keepdims=True))]
        if gated:
            dz_ref[...] = (dx * gv_ref[...]).astype(BF16)
            sums.append((dgv_ref, jnp.sum(dx * f_ref[...], axis=0, keepdims=True)))

        @pl.when(i == 0)
        def _():
            for ref, p in sums:
                ref[...] = p

        @pl.when(i > 0)
        def _():
            for ref, p in sums:
                ref[...] += p

    vec = jax.ShapeDtypeStruct((1, D), F32)
    in_specs = [_row_spec(), _vec_spec(), _vec_spec(), _row_spec(), _row_spec()]
    out_specs = [_row_spec(), _vec_spec(), _vec_spec(), _vec_spec()]
    out_shape = [jax.ShapeDtypeStruct((S, D), F32), vec, vec, vec]
    args = [x, g, sc, dh, dres]
    if gated:
        in_specs += [_row_spec(), _vec_spec()]
        out_specs += [_row_spec(), _vec_spec()]
        out_shape += [jax.ShapeDtypeStruct((S, D), BF16), vec]
        args += list(gate)
    return _pcall(body, name=name, grid=(S // TR,), in_specs=in_specs, out_specs=tuple(out_specs),
                  out_shape=tuple(out_shape), compiler_params=_params(("arbitrary",)))(*args)


def _final_loss(x2, tgt, g, f, g2):
    def body(x_ref, t_ref, g_ref, f_ref, g2_ref, loss_ref, dx_ref, dg_ref, df_ref, dg2_ref):
        i = pl.program_id(0)
        xv = x_ref[...]
        gv = g_ref[...]
        rstd = lax.rsqrt(jnp.mean(xv * xv, axis=-1, keepdims=True) + RMS_EPS)
        xhat = xv * rstd
        err = xhat * gv - t_ref[...]
        dy = err * (1.0 / D)
        dxhat = dy * gv
        dx = rstd * (dxhat - xhat * jnp.mean(dxhat * xhat, axis=-1, keepdims=True))
        dx_ref[...] = dx
        df_ref[...] = (dx * g2_ref[...]).astype(BF16)
        p_g = jnp.sum(dy * xhat, axis=0, keepdims=True)
        p_g2 = jnp.sum(dx * f_ref[...], axis=0, keepdims=True)
        p_l = jnp.zeros((1, 128), F32) + 0.5 * jnp.sum(jnp.mean(err * err, axis=-1, keepdims=True))

        @pl.when(i == 0)
        def _():
            dg_ref[...] = p_g
            dg2_ref[...] = p_g2
            loss_ref[...] = p_l

        @pl.when(i > 0)
        def _():
            dg_ref[...] += p_g
            dg2_ref[...] += p_g2
            loss_ref[...] += p_l

    vec = jax.ShapeDtypeStruct((1, D), F32)
    return _pcall(body, name="final_loss", grid=(S // TR,),
                  in_specs=[_row_spec(), _row_spec(), _vec_spec(), _row_spec(), _vec_spec()],
                  out_specs=(_vec_spec(128), _row_spec(), _vec_spec(), _row_spec(), _vec_spec()),
                  out_shape=(jax.ShapeDtypeStruct((1, 128), F32), jax.ShapeDtypeStruct((S, D), F32), vec,
                             jax.ShapeDtypeStruct((S, D), BF16), vec),
                  compiler_params=_params(("arbitrary",)))(x2, tgt, g, f, g2)


HALF = 512


def _merge_fwd(proj, ret_out, att_out):
    def body(ga_ref, gb_ref, r_ref, a_ref, o_ref):
        o_ref[...] = (jax.nn.sigmoid(ga_ref[...]) * r_ref[...] + jax.nn.sigmoid(gb_ref[...]) * a_ref[...]).astype(BF16)

    blk = lambda off: pl.BlockSpec((TR, HALF), lambda i, j: (i, off // HALF + j))
    return _pcall(body, name="merge_fwd", grid=(S // TR, D // HALF),
                  in_specs=[blk(OFF_GA), blk(OFF_GB), blk(0), blk(0)], out_specs=blk(0),
                  out_shape=jax.ShapeDtypeStruct((S, D), BF16),
                  compiler_params=_params(("parallel", "parallel")))(proj, proj, ret_out, att_out)


def _merge_bwd(proj, ret_out, att_out, dmerged):
    def body(ga_ref, gb_ref, r_ref, a_ref, dm_ref, dr_ref, da_ref, dga_ref, dgb_ref):
        sa = jax.nn.sigmoid(ga_ref[...])
        sb = jax.nn.sigmoid(gb_ref[...])
        dm = dm_ref[...]
        dr_ref[...] = (dm * sa).astype(BF16)
        da_ref[...] = (dm * sb).astype(BF16)
        dga_ref[...] = (dm * r_ref[...] * (sa * (1.0 - sa))).astype(BF16)
        dgb_ref[...] = (dm * a_ref[...] * (sb * (1.0 - sb))).astype(BF16)

    blk = lambda off: pl.BlockSpec((TR, HALF), lambda i, j: (i, off // HALF + j))
    o = jax.ShapeDtypeStruct((S, D), BF16)
    return _pcall(body, name="merge_bwd", grid=(S // TR, D // HALF),
                  in_specs=[blk(OFF_GA), blk(OFF_GB), blk(0), blk(0), blk(0)], out_specs=(blk(0),) * 4,
                  out_shape=(o, o, o, o),
                  compiler_params=_params(("parallel", "parallel")))(proj, proj, ret_out, att_out, dmerged)


def _ret_tables():
    H, C = RET_HEADS, CHUNK
    log_g = jnp.log1p(-(2.0 ** (-5.0 - jnp.arange(H, dtype=F32))))
    idx = jnp.arange(C, dtype=F32)
    rel = idx[:, None] - idx[None, :]
    inner = jnp.where(rel >= 0, jnp.exp(log_g[:, None, None] * jnp.maximum(rel, 0.0)), 0.0)
    qd = jnp.exp(log_g[:, None] * (idx + 1.0))[:, :, None]
    kd = jnp.exp(log_g[:, None] * (C - 1.0 - idx))[:, :, None]
    cd = jnp.broadcast_to(jnp.exp(log_g * C)[:, None, None], (H, 1, 128))
    half = RET_DK // 2
    inv = 10000.0 ** (-jnp.arange(half, dtype=F32) / half)
    ang = jnp.arange(S, dtype=F32)[:, None] * inv[None, :]
    return inner, qd, kd, cd, jnp.cos(ang), jnp.sin(ang)


def _rot(x, cos, sin):
    x1, x2 = x[:, :128], x[:, 128:]
    return jnp.concatenate([x1 * cos - x2 * sin, x1 * sin + x2 * cos], axis=1)


def _rot_t(d, cos, sin):
    d1, d2 = d[:, :128], d[:, 128:]
    return jnp.concatenate([d1 * cos + d2 * sin, d2 * cos - d1 * sin], axis=1)


RET_COLS = OFF_ATT
RET_VW = RET_HEADS * RET_DV


def _ret_specs(chunk_of):
    ci = chunk_of
    whole = lambda shape: pl.BlockSpec(shape, lambda t: (0,) * len(shape))
    return [
        pl.BlockSpec((CHUNK, RET_COLS), lambda t: (ci(t), 0)),
        pl.BlockSpec((CHUNK, 128), lambda t: (ci(t), 0)),
        pl.BlockSpec((CHUNK, 128), lambda t: (ci(t), 0)),
        whole((RET_HEADS, CHUNK, CHUNK)), whole((RET_HEADS, CHUNK, 1)), whole((RET_HEADS, CHUNK, 1)),
        whole((RET_HEADS, 1, 128)), whole((1, RET_VW)), whole((1, RET_VW)),
    ]


def _ret_cols(h):
    q = slice(OFF_RQ + h * RET_DK, OFF_RQ + (h + 1) * RET_DK)
    k = slice(OFF_RK + h * RET_DK, OFF_RK + (h + 1) * RET_DK)
    v = slice(OFF_RV + h * RET_DV, OFF_RV + (h + 1) * RET_DV)
    g = slice(OFF_RG + h * RET_DV, OFF_RG + (h + 1) * RET_DV)
    return q, k, v, g, slice(h * RET_DV, (h + 1) * RET_DV)


def _ret_fwd(proj, tables, gn_g, gn_b, comm=None):
    inner, qd, kd, cd, cos, sin = tables

    def body(x_ref, cos_ref, sin_ref, in_ref, qd_ref, kd_ref, cd_ref, g_ref, b_ref,
             gated_ref, ro_ref, st_ref, s_scr):
        i = pl.program_id(0)

        @pl.when(i == 0)
        def _():
            s_scr[...] = jnp.zeros_like(s_scr)

        cosv, sinv = cos_ref[...], sin_ref[...]
        for h in range(RET_HEADS):
            cq, ck, cv, cg, co = _ret_cols(h)
            q = _rot(x_ref[:, cq], cosv, sinv)
            k = _rot(x_ref[:, ck], cosv, sinv) * (RET_DK ** -0.5)
            v = x_ref[:, cv]
            st = s_scr[h]
            st_ref[h] = st
            s = _dot(q, k, NT) * in_ref[h]
            o = _dot(s, v, NN) + _dot(q, st, NN) * qd_ref[h]
            s_scr[h] = st * cd_ref[h, :, :1] + _dot(k * kd_ref[h], v, TN)
            ro_ref[:, co] = o
            mu = jnp.mean(o, axis=-1, keepdims=True)
            oc = o - mu
            var = jnp.mean(oc * oc, axis=-1, keepdims=True)
            rn = oc * lax.rsqrt(var + GN_EPS) * g_ref[:, co] + b_ref[:, co]
            rg = x_ref[:, cg]
            gated_ref[:, co] = (rg * jax.nn.sigmoid(rg) * rn).astype(BF16)

    ospec = pl.BlockSpec((CHUNK, RET_VW), lambda t: (t, 0))
    kw = dict(name="ret_fwd", grid=(N_CHUNK,), in_specs=_ret_specs(lambda t: t),
              out_specs=(ospec, ospec, pl.BlockSpec((RET_HEADS, None, RET_DK, RET_DV), lambda t: (0, t, 0, 0))),
              out_shape=(jax.ShapeDtypeStruct((S, RET_VW), BF16), jax.ShapeDtypeStruct((S, RET_VW), F32),
                         jax.ShapeDtypeStruct((RET_HEADS, N_CHUNK, RET_DK, RET_DV), F32)),
              scratch_shapes=[pltpu.VMEM((RET_HEADS, RET_DK, RET_DV), F32)])
    args = (proj, cos, sin, inner, qd, kd, cd, gn_g, gn_b)
    if comm is not None:
        return _carry(body, comm, **kw)(*args)
    return _pcall(body, compiler_params=_params(("arbitrary",)), **kw)(*args)


def _ret_bwd(proj, tables, gn_g, gn_b, ro, states, dgated, comm=None):
    inner, qd, kd, cd, cos, sin = tables
    last = N_CHUNK - 1

    def body(x_ref, cos_ref, sin_ref, in_ref, qd_ref, kd_ref, cd_ref, g_ref, b_ref, ro_ref, st_ref, dg_ref,
             dx_ref, gg_ref, gb_ref, gs_scr):
        t = pl.program_id(0)

        @pl.when(t == 0)
        def _():
            gs_scr[...] = jnp.zeros_like(gs_scr)
            gg_ref[...] = jnp.zeros_like(gg_ref)
            gb_ref[...] = jnp.zeros_like(gb_ref)

        cosv, sinv = cos_ref[...], sin_ref[...]
        for h in range(RET_HEADS):
            cq, ck, cv, cg, co = _ret_cols(h)
            q = _rot(x_ref[:, cq], cosv, sinv)
            k = _rot(x_ref[:, ck], cosv, sinv) * (RET_DK ** -0.5)
            v = x_ref[:, cv]
            qdv, kdv, dm = qd_ref[h], kd_ref[h], in_ref[h]
            st = st_ref[h]
            o = ro_ref[:, co]
            gv = g_ref[:, co]
            mu = jnp.mean(o, axis=-1, keepdims=True)
            oc = o - mu
            rstd = lax.rsqrt(jnp.mean(oc * oc, axis=-1, keepdims=True) + GN_EPS)
            ohat = oc * rstd
            rn = ohat * gv + b_ref[:, co]
            rg = x_ref[:, cg]
            sg = jax.nn.sigmoid(rg)
            dgt = dg_ref[:, co]
            drn = dgt * (rg * sg)
            dx_ref[:, cg] = (dgt * rn * (sg * (1.0 + rg * (1.0 - sg)))).astype(BF16)
            gg_ref[:, co] += jnp.sum(drn * ohat, axis=0, keepdims=True)
            gb_ref[:, co] += jnp.sum(drn, axis=0, keepdims=True)
            dohat = drn * gv
            do = rstd * (dohat - jnp.mean(dohat, axis=-1, keepdims=True)
                         - ohat * jnp.mean(dohat * ohat, axis=-1, keepdims=True))
            gs = gs_scr[h]
            s = _dot(q, k, NT) * dm
            dsr = _dot(do, v, NT) * dm
            dq = _dot(dsr, k, NN) + _dot(do, st, NT) * qdv
            dk = _dot(dsr, q, TN) + _dot(v, gs, NT) * kdv
            dv = _dot(s, do, TN) + _dot(k * kdv, gs, NN)
            gs_scr[h] = gs * cd_ref[h, :, :1] + _dot(q * qdv, do, TN)
            dx_ref[:, cq] = _rot_t(dq, cosv, sinv).astype(BF16)
            dx_ref[:, ck] = (_rot_t(dk, cosv, sinv) * (RET_DK ** -0.5)).astype(BF16)
            dx_ref[:, cv] = dv.astype(BF16)

    rev = lambda t: last - t
    vblk = pl.BlockSpec((CHUNK, RET_VW), lambda t: (rev(t), 0))
    vspec = pl.BlockSpec((1, RET_VW), lambda t: (0, 0))
    kw = dict(name="ret_bwd", grid=(N_CHUNK,),
              in_specs=_ret_specs(rev) + [vblk, pl.BlockSpec((RET_HEADS, None, RET_DK, RET_DV),
                                                             lambda t: (0, rev(t), 0, 0)), vblk],
              out_specs=(pl.BlockSpec((CHUNK, RET_COLS), lambda t: (rev(t), 0)), vspec, vspec),
              out_shape=(jax.ShapeDtypeStruct((S, RET_COLS), BF16), jax.ShapeDtypeStruct((1, RET_VW), F32),
                         jax.ShapeDtypeStruct((1, RET_VW), F32)),
              scratch_shapes=[pltpu.VMEM((RET_HEADS, RET_DK, RET_DV), F32)])
    args = (proj, cos, sin, inner, qd, kd, cd, gn_g, gn_b, ro, states, dgated)
    if comm is not None:
        return _carry(body, comm, **kw)(*args)
    return _pcall(body, compiler_params=_params(("arbitrary",)), **kw)(*args)


def _bucket_tables():
    qi = np.arange(ATT_BLK)[:, None]
    kj = np.arange(2 * ATT_BLK)[None, :]
    m = ATT_BLK + qi - kj
    out = []
    for win, dil in ATT_GROUPS:
        w = win // dil
        dist = (np.clip(m, 0, w) * dil).astype(np.int32)
        max_exact = N_BUCKETS // 2
        d_f = np.maximum(dist, 1).astype(np.float32)
        large = max_exact + (np.log(d_f / np.float32(max_exact)) / np.float32(math.log(MAX_DIST / max_exact))
                             * np.float32(N_BUCKETS - max_exact)).astype(np.int32)
        large = np.minimum(large, N_BUCKETS - 1)
        out.append(np.where(dist < max_exact, dist, large).astype(np.int32))
    return np.stack(out)


def _bias_build(rel_bias, buckets):
    def body(tab_ref, bk_ref, o_ref):
        hh = pl.program_id(0)
        bk = bk_ref[...]
        acc = jnp.zeros((ATT_BLK, 2 * ATT_BLK), F32)
        for b in range(N_BUCKETS):
            acc = jnp.where(bk == b, tab_ref[b, hh], acc)
        o_ref[...] = acc

    nh = len(ATT_GROUPS) * ATT_HG
    return _pcall(body, name="bias_build", grid=(nh,),
                  in_specs=[pl.BlockSpec(memory_space=pltpu.SMEM),
                            pl.BlockSpec((None, ATT_BLK, 2 * ATT_BLK), lambda hh: (hh // ATT_HG, 0, 0))],
                  out_specs=pl.BlockSpec((None, ATT_BLK, 2 * ATT_BLK), lambda hh: (hh, 0, 0)),
                  out_shape=jax.ShapeDtypeStruct((nh, ATT_BLK, 2 * ATT_BLK), F32),
                  compiler_params=_params(("parallel",)))(rel_bias, buckets)


def _bias_grad(ds_sum, buckets):
    def body(ds_ref, bk_ref, o_ref):
        bk = bk_ref[...]
        ds = ds_ref[...]
        rows = lax.broadcasted_iota(jnp.int32, (N_BUCKETS, 128), 0)
        acc = jnp.zeros((N_BUCKETS, 128), F32)
        for b in range(N_BUCKETS):
            acc = jnp.where(rows == b, jnp.sum(jnp.where(bk == b, ds, 0.0)), acc)
        o_ref[...] = acc

    nh = len(ATT_GROUPS) * ATT_HG
    return _pcall(body, name="bias_grad", grid=(nh,),
                  in_specs=[pl.BlockSpec((None, ATT_BLK, 2 * ATT_BLK), lambda hh: (hh, 0, 0)),
                            pl.BlockSpec((None, ATT_BLK, 2 * ATT_BLK), lambda hh: (hh // ATT_HG, 0, 0))],
                  out_specs=pl.BlockSpec((None, N_BUCKETS, 128), lambda hh: (hh, 0, 0)),
                  out_shape=jax.ShapeDtypeStruct((nh, N_BUCKETS, 128), F32),
                  compiler_params=_params(("parallel",)))(ds_sum, buckets)


def _att_valid(n):
    qi = lax.broadcasted_iota(jnp.int32, (ATT_BLK, 2 * ATT_BLK), 0)
    kj = lax.broadcasted_iota(jnp.int32, (ATT_BLK, 2 * ATT_BLK), 1)
    m = ATT_BLK + qi - kj
    first_key = jnp.where(n > 0, 0, ATT_BLK)
    return (m >= 0) & (m <= ATT_BLK) & (kj >= first_key)


ATT_HP = (1, 2, 2)


def _att_geometry(gi):
    _, dil = ATT_GROUPS[gi]
    return dil, S // dil // ATT_BLK, ATT_HP[gi]


def _blk(dil, r, n):
    if dil == 1:
        return pl.ds(n * ATT_BLK, ATT_BLK)
    return pl.ds(r + n * ATT_BLK * dil, ATT_BLK, stride=dil)


def _slab_specs(gi):
    _, _, hp = _att_geometry(gi)
    per = ATT_HG // hp
    return [pl.BlockSpec((hp, S, ATT_DH), lambda g, r, part=part: ((3 * gi + part) * per + g, 0, 0))
            for part in range(3)]


def _head_specs(gi, count):
    _, _, hp = _att_geometry(gi)
    return [pl.BlockSpec((hp, S, ATT_DH), lambda g, r: (g, 0, 0))] * count


def _bias_spec(gi):
    _, _, hp = _att_geometry(gi)
    return pl.BlockSpec((hp, ATT_BLK, 2 * ATT_BLK), lambda g, r: (gi * (ATT_HG // hp) + g, 0, 0))


def _att_fwd(slabs, bias, gi, comm=None):
    dil, nb, hp = _att_geometry(gi)
    scale = ATT_DH ** -0.5

    def body(q_ref, k_ref, v_ref, bias_ref, o_ref, l_ref):
        r = pl.program_id(1)
        for n in range(nb):
            valid = _att_valid(n)
            prev = _blk(dil, r, max(n - 1, 0))
            cur = _blk(dil, r, n)
            for h in range(hp):
                kk = jnp.concatenate([k_ref[h, prev, :], k_ref[h, cur, :]], axis=0)
                vv = jnp.concatenate([v_ref[h, prev, :], v_ref[h, cur, :]], axis=0)
                s = _dot(q_ref[h, cur, :], kk, NT) * scale + bias_ref[h]
                s = jnp.where(valid, s, -1e30)
                mx = jnp.max(s, axis=-1, keepdims=True)
                e = jnp.exp(s - mx)
                den = jnp.sum(e, axis=-1, keepdims=True)
                o_ref[h, cur, :] = _dot(e / den, vv, NN)
                l_ref[h, cur, :] = jnp.broadcast_to(mx + jnp.log(den), (ATT_BLK, ATT_DH))

    osh = jax.ShapeDtypeStruct((ATT_HG, S, ATT_DH), F32)
    kw = dict(name=f"att_fwd{gi}", grid=(ATT_HG // hp, dil), in_specs=_slab_specs(gi) + [_bias_spec(gi)],
              out_specs=tuple(_head_specs(gi, 2)), out_shape=(osh, osh))
    if comm is not None:
        return _carry(body, comm, **kw)(slabs, slabs, slabs, bias)
    return _pcall(body, compiler_params=_params(("parallel", "arbitrary")), **kw)(slabs, slabs, slabs, bias)


def _att_bwd(slabs, bias, o, lse, do, dlse, gi, comm=None):
    dil, nb, hp = _att_geometry(gi)
    per = ATT_HG // hp
    scale = ATT_DH ** -0.5
    wh = hp * ATT_DH
    wide = lambda t: jnp.concatenate([t, t], axis=1)

    def body(q_ref, k_ref, v_ref, bias_ref, o_ref, l_ref, do_ref, dl_ref, dq_ref, dk_ref, dv_ref, ds_ref):
        r = pl.program_id(1)

        @pl.when(r == 0)
        def _():
            ds_ref[...] = jnp.zeros_like(ds_ref)

        for h in range(hp):
            sl = slice(h * ATT_DH, (h + 1) * ATT_DH)
            carry_k = carry_v = None
            for n in range(nb):
                valid = _att_valid(n)
                prev = _blk(dil, r, max(n - 1, 0))
                cur = _blk(dil, r, n)
                q = q_ref[h, cur, :]
                kk = jnp.concatenate([k_ref[h, prev, :], k_ref[h, cur, :]], axis=0)
                vv = jnp.concatenate([v_ref[h, prev, :], v_ref[h, cur, :]], axis=0)
                dov = do_ref[h, cur, :]
                s = _dot(q, kk, NT) * scale + bias_ref[h]
                p = jnp.where(valid, jnp.exp(s - wide(l_ref[h, cur, :])), 0.0)
                dp = _dot(dov, vv, NT)
                delta = jnp.sum(dov * o_ref[h, cur, :], axis=-1, keepdims=True)
                ds = p * (dp - delta + wide(dl_ref[h, cur, :]))
                ds_ref[h] += ds
                out_rows = pl.ds(n * ATT_BLK, ATT_BLK)
                dq_ref[out_rows, sl] = (_dot(ds, kk, NN) * scale).astype(BF16)
                dkk = _dot(ds, q, TN) * scale
                dvv = _dot(p, dov, TN)
                if n > 0:
                    before = pl.ds((n - 1) * ATT_BLK, ATT_BLK)
                    dk_ref[before, sl] = (carry_k + dkk[:ATT_BLK]).astype(BF16)
                    dv_ref[before, sl] = (carry_v + dvv[:ATT_BLK]).astype(BF16)
                carry_k, carry_v = dkk[ATT_BLK:], dvv[ATT_BLK:]
            last = pl.ds((nb - 1) * ATT_BLK, ATT_BLK)
            dk_ref[last, sl] = carry_k.astype(BF16)
            dv_ref[last, sl] = carry_v.astype(BF16)

    out_spec = pl.BlockSpec((S // dil, wh), lambda g, r: (0, r * per + g))
    osh = jax.ShapeDtypeStruct((S // dil, dil * AW), BF16)
    kw = dict(name=f"att_bwd{gi}", grid=(per, dil), in_specs=_slab_specs(gi) + [_bias_spec(gi)] + _head_specs(gi, 4),
              out_specs=(out_spec, out_spec, out_spec,
                         pl.BlockSpec((hp, ATT_BLK, 2 * ATT_BLK), lambda g, r: (g, 0, 0))),
              out_shape=(osh, osh, osh, jax.ShapeDtypeStruct((ATT_HG, ATT_BLK, 2 * ATT_BLK), F32)))
    args = (slabs, slabs, slabs, bias, o, lse, do, dlse)
    if comm is not None:
        return _carry(body, comm, **kw)(*args)
    return _pcall(body, compiler_params=_params(("arbitrary", "arbitrary")), **kw)(*args)


AW = ATT_HG * ATT_DH


def _mix_weights(l0, l1, l2):
    mx = jnp.maximum(jnp.maximum(l0, l1), l2)
    e0, e1, e2 = jnp.exp(l0 - mx), jnp.exp(l1 - mx), jnp.exp(l2 - mx)
    den = e0 + e1 + e2
    return e0 / den, e1 / den, e2 / den


def _heads_spec():
    return pl.BlockSpec((ATT_HG, TR, ATT_DH), lambda i: (0, i, 0))


def _mix_fwd(os_, ls):
    def body(o0, o1, o2, l0, l1, l2, att_ref):
        for h in range(ATT_HG):
            w0, w1, w2 = _mix_weights(l0[h], l1[h], l2[h])
            att_ref[:, h * ATT_DH:(h + 1) * ATT_DH] = (w0 * o0[h] + w1 * o1[h] + w2 * o2[h]).astype(BF16)

    return _pcall(body, name="mix_fwd", grid=(S // TR,), in_specs=[_heads_spec()] * 6, out_specs=_row_spec(AW),
                  out_shape=jax.ShapeDtypeStruct((S, AW), BF16), compiler_params=_params(("parallel",)))(*os_, *ls)


def _mix_bwd(os_, ls, datt):
    def body(o0, o1, o2, l0, l1, l2, da_ref, d0, d1, d2, e0, e1, e2):
        for h in range(ATT_HG):
            ws = _mix_weights(l0[h], l1[h], l2[h])
            da = da_ref[:, h * ATT_DH:(h + 1) * ATT_DH]
            dws = []
            for o_ref, w, d_ref in zip((o0, o1, o2), ws, (d0, d1, d2)):
                d_ref[h] = w * da
                dws.append(jnp.broadcast_to(jnp.sum(da * o_ref[h], axis=-1, keepdims=True), (TR, ATT_DH)))
            tot = ws[0] * dws[0] + ws[1] * dws[1] + ws[2] * dws[2]
            for w, dw, e_ref in zip(ws, dws, (e0, e1, e2)):
                e_ref[h] = w * (dw - tot)

    o = jax.ShapeDtypeStruct((ATT_HG, S, ATT_DH), F32)
    return _pcall(body, name="mix_bwd", grid=(S // TR,), in_specs=[_heads_spec()] * 6 + [_row_spec(AW)],
                  out_specs=(_heads_spec(),) * 6, out_shape=(o,) * 6,
                  compiler_params=_params(("parallel",)))(*os_, *ls, datt)


def _ada_fwd(c_all, w_sh, b_sl):
    def body(c_ref, w_ref, b_ref, o_ref):
        cv = c_ref[...]
        o_ref[...] = _dot(cv * jax.nn.sigmoid(cv), w_ref[...], NN) + b_ref[...]

    return _pcall(body, name="ada_fwd", out_shape=jax.ShapeDtypeStruct((N_DEV, w_sh.shape[1]), F32),
                  compiler_params=_params())(c_all, w_sh, b_sl)


def _ada_bwd(c_all, dm_sl):
    def body(c_ref, d_ref, o_ref):
        cv = c_ref[...]
        o_ref[...] = _dot(cv * jax.nn.sigmoid(cv), d_ref[...], TN)

    return _pcall(body, name="ada_bwd", out_shape=jax.ShapeDtypeStruct((D, dm_sl.shape[1]), F32),
                  compiler_params=_params())(c_all, dm_sl)


N_MOD = 6


def _sum_small(gathered):
    n = len(gathered)

    def body(*refs):
        ins, (gb_ref, dm_ref), outs = refs[:n], refs[n:n + 2], refs[n + 2:]

        def total(r):
            acc = r[0]
            for e in range(1, N_DEV):
                acc = acc + r[e]
            return acc

        for i in range(N_MOD):
            cols = slice(i * D, (i + 1) * D)
            gb_ref[:, cols] = total(ins[i])
            for e in range(N_DEV):
                dm_ref[e:e + 1, cols] = ins[i][e]
        for r, o_ref in zip(ins[N_MOD:], outs):
            o_ref[...] = total(r)

    shapes = (jax.ShapeDtypeStruct((1, N_MOD * D), F32), jax.ShapeDtypeStruct((N_DEV, N_MOD * D), F32),
              *[jax.ShapeDtypeStruct(g.shape[1:], F32) for g in gathered[N_MOD:]])
    res = _pcall(body, name="sum_small", out_shape=shapes, compiler_params=_params())(*gathered)
    return res[0], res[1], res[2:]


def _row_tile(m, n):
    t = max(8, min(m, (1 << 19) // n // 8 * 8))
    while m % t:
        t -= 8
    return t


def _pair_sum(full, recv, sel, name):
    _, _, m, n = full.shape
    t = _row_tile(m, n)

    def body(sel_ref, a_ref, b_ref, o_ref):
        o_ref[...] = (a_ref[...].astype(F32) + b_ref[...].astype(F32)).astype(o_ref.dtype)

    gs = pltpu.PrefetchScalarGridSpec(
        num_scalar_prefetch=1, grid=(4, m // t),
        in_specs=[pl.BlockSpec((None, None, t, n), lambda q, i, s: (q, s[0], i, 0)),
                  pl.BlockSpec((None, t, n), lambda q, i, s: (q, i, 0))],
        out_specs=pl.BlockSpec((None, t, n), lambda q, i, s: (q, i, 0)))
    return _pcall(body, name=name, grid_spec=gs, out_shape=jax.ShapeDtypeStruct((4, m, n), full.dtype),
                  compiler_params=_params(("parallel", "parallel")))(sel, full, recv)


def _chip_sum(part, recv, sel, name):
    _, m, n = part.shape
    t = _row_tile(m, n)

    def body(sel_ref, a_ref, r_ref, o_ref):
        o_ref[...] = ((a_ref[...].astype(F32) + r_ref[0].astype(F32)) + r_ref[1].astype(F32)) + r_ref[2].astype(F32)

    gs = pltpu.PrefetchScalarGridSpec(
        num_scalar_prefetch=1, grid=(m // t,),
        in_specs=[pl.BlockSpec((None, t, n), lambda i, s: (s[0], i, 0)),
                  pl.BlockSpec((3, t, n), lambda i, s: (0, i, 0))],
        out_specs=pl.BlockSpec((t, n), lambda i, s: (i, 0)))
    return _pcall(body, name=name, grid_spec=gs, out_shape=jax.ShapeDtypeStruct((m, n), F32),
                  compiler_params=_params(("parallel",)))(sel, part, recv)


def _adamw_math(w, g, m, v):
    nm = ADAM_B1 * m + (1.0 - ADAM_B1) * g
    nv = ADAM_B2 * v + (1.0 - ADAM_B2) * (g * g)
    m_hat = nm / (1.0 - ADAM_B1 ** ADAM_STEP)
    v_hat = nv / (1.0 - ADAM_B2 ** ADAM_STEP)
    return -ADAM_LR * (m_hat / (jnp.sqrt(v_hat) + ADAM_EPS) + ADAM_WD * w), nm, nv


def _adamw(w, g, m, v, name):
    _, rows, cols = w.shape
    t = _row_tile(rows, cols)

    def body(w_ref, g_ref, m_ref, v_ref, d_ref, nm_ref, nv_ref):
        d_ref[...], nm_ref[...], nv_ref[...] = _adamw_math(w_ref[...], g_ref[...], m_ref[...], v_ref[...])

    spec3 = pl.BlockSpec((None, t, cols), lambda i: (0, i, 0))
    spec2 = pl.BlockSpec((t, cols), lambda i: (i, 0))
    o = jax.ShapeDtypeStruct(w.shape, F32)
    return _pcall(body, name=name, grid=(rows // t,), in_specs=[spec3, spec2, spec3, spec3], out_specs=(spec3,) * 3,
                  out_shape=(o, o, o), compiler_params=_params(("parallel",)))(w, g, m, v)


def _adamw_small(ws, gs, ms, vs):
    n = len(ws)

    def body(*refs):
        for i in range(n):
            w_ref, g_ref, m_ref, v_ref = (refs[k * n + i] for k in range(4))
            d, nm, nv = _adamw_math(w_ref[...], g_ref[...], m_ref[...], v_ref[...])
            refs[4 * n + i][...] = d
            refs[5 * n + i][...] = nm
            refs[6 * n + i][...] = nv

    shapes = tuple(jax.ShapeDtypeStruct(w.shape, F32) for w in ws)
    res = _pcall(body, name="adamw_small", out_shape=shapes * 3, compiler_params=_params())(*ws, *gs, *ms, *vs)
    return res[:n], res[n:2 * n], res[2 * n:]


def _mesh_pos():
    return lax.axis_index("x"), lax.axis_index("y"), lax.axis_index("c")


class _Gather:
    def __init__(self, arrs, relay=False):
        self.relay = relay
        self.ins = list(arrs)
        na = self.na = len(arrs)
        self.out_shape = tuple(jax.ShapeDtypeStruct((N_DEV,) + a.shape, a.dtype) for a in arrs)
        self.sems = [pltpu.SemaphoreType.DMA((7 * na,)), pltpu.SemaphoreType.DMA((7 * na,)),
                     pltpu.SemaphoreType.DMA((na,))]

    def _copies(self, ins, outs, sems):
        send_sems, recv_sems, local_sems = sems
        x, y, c = _mesh_pos()
        me, sibling = (x, y, c), (x, y, 1 - c)
        chips = [(1 - x, y), (x, 1 - y), (1 - x, 1 - y)]

        def slot(p):
            return 4 * p[0] + 2 * p[1] + p[2]

        def copy(a, k, block, to, src=None):
            dst = outs[a].at[slot(block)]
            return pltpu.make_async_remote_copy(
                src_ref=dst if src is None else src, dst_ref=dst, send_sem=send_sems.at[7 * a + k],
                recv_sem=recv_sems.at[7 * a + k], device_id=to, device_id_type=MESH)

        mine = [pltpu.make_async_copy(ins[a], outs[a].at[slot(me)], local_sems.at[a]) for a in range(self.na)]
        direct = chips[:2] if self.relay else chips
        first = []
        for a in range(self.na):
            first.append(copy(a, 0, me, sibling, src=ins[a]))
            first += [copy(a, 1 + j, me, (*chip, c), src=ins[a]) for j, chip in enumerate(direct)]
        return me, sibling, chips, c, copy, mine, first

    def start(self, ins, outs, sems):
        *_, mine, first = self._copies(ins, outs, sems)
        for cp in mine + first:
            cp.start()

    def finish(self, ins, outs, sems):
        me, sibling, chips, c, copy, mine, first = self._copies(ins, outs, sems)
        x, y = me[0], me[1]
        passed = []
        for j, chip in enumerate(chips):
            for a in range(self.na):
                if self.relay and j == 2:
                    owner = ((x + 1 - c) % 2, (y + c) % 2, c)
                    cp = copy(a, 3, owner, ((x + c) % 2, (y + 1 - c) % 2, c))
                    cp.start()
                    passed.append(cp)
                copy(a, 1 + j, (*chip, c), me).wait_recv()
                cp = copy(a, 4 + j, (*chip, c), sibling)
                cp.start()
                passed.append(cp)
        for a in range(self.na):
            copy(a, 0, sibling, me).wait_recv()
            for j, chip in enumerate(chips):
                copy(a, 4 + j, (*chip, 1 - c), me).wait_recv()
        for cp in first + passed:
            cp.wait_send()
        for cp in mine:
            cp.wait()


class _ExchangeCore:
    def __init__(self, fulls):
        self.ins = list(fulls)
        self.out_shape = tuple(jax.ShapeDtypeStruct((4,) + f.shape[2:], f.dtype) for f in fulls)
        self.sems = [pltpu.SemaphoreType.DMA((4 * len(fulls),)), pltpu.SemaphoreType.DMA((4 * len(fulls),))]

    def _copies(self, ins, outs, sems):
        send_sems, recv_sems = sems
        x, y, c = _mesh_pos()
        return [pltpu.make_async_remote_copy(
            src_ref=ins[a].at[q, 1 - c], dst_ref=outs[a].at[q], send_sem=send_sems.at[4 * a + q],
            recv_sem=recv_sems.at[4 * a + q], device_id=(x, y, 1 - c), device_id_type=MESH)
            for a in range(len(self.ins)) for q in range(4)]

    def start(self, ins, outs, sems):
        for cp in self._copies(ins, outs, sems):
            cp.start()

    def finish(self, ins, outs, sems):
        for cp in self._copies(ins, outs, sems):
            cp.wait()


class _ExchangeChip:
    def __init__(self, parts):
        self.ins = list(parts)
        self.out_shape = tuple(jax.ShapeDtypeStruct((3,) + p.shape[1:], p.dtype) for p in parts)
        self.sems = [pltpu.SemaphoreType.DMA((3 * len(parts),)), pltpu.SemaphoreType.DMA((3 * len(parts),))]

    def _copies(self, ins, outs, sems):
        send_sems, recv_sems = sems
        x, y, c = _mesh_pos()
        chips = [(1 - x, y), (x, 1 - y), (1 - x, 1 - y)]
        return [pltpu.make_async_remote_copy(
            src_ref=ins[a].at[2 * px + py], dst_ref=outs[a].at[j], send_sem=send_sems.at[3 * a + j],
            recv_sem=recv_sems.at[3 * a + j], device_id=(px, py, c), device_id_type=MESH)
            for a in range(len(self.ins)) for j, (px, py) in enumerate(chips)]

    def start(self, ins, outs, sems):
        for cp in self._copies(ins, outs, sems):
            cp.start()

    def finish(self, ins, outs, sems):
        for cp in self._copies(ins, outs, sems):
            cp.wait()


HBM_ONLY = pl.BlockSpec(memory_space=pltpu.HBM)
SEM_SPEC = pl.BlockSpec(memory_space=pltpu.SEMAPHORE)
SIDE_EFFECT = pltpu.SideEffectType.DATAFLOW_SIDE_EFFECTING


def _chip_copies(p_ref, land_ref, send_sems, recv_sems):
    x, y, c = _mesh_pos()
    return [pltpu.make_async_remote_copy(
        src_ref=p_ref.at[2 * px + py], dst_ref=land_ref.at[j], send_sem=send_sems.at[j], recv_sem=recv_sems.at[j],
        device_id=(px, py, c), device_id_type=MESH) for j, (px, py) in enumerate([(1 - x, y), (x, 1 - y), (1 - x, 1 - y)])]


def _chip_exchange_start(part):
    land_shape = (3,) + part.shape[1:]

    def body(p_ref, land_ref, send_sems, recv_sems, p_thru, land_thru, token):
        for cp in _chip_copies(p_ref, land_ref, send_sems, recv_sems):
            cp.start()
        token[...] = jnp.zeros_like(token)

    return pl.pallas_call(
        body, name="rs_in_start",
        out_shape=(pltpu.SemaphoreType.DMA((3,)), pltpu.SemaphoreType.DMA((3,)), pltpu.HBM(part.shape, part.dtype),
                   pltpu.HBM(land_shape, part.dtype), jax.ShapeDtypeStruct((8, 128), F32)),
        in_specs=(HBM_ONLY, HBM_ONLY),
        out_specs=(SEM_SPEC, SEM_SPEC, HBM_ONLY, HBM_ONLY, pl.BlockSpec(memory_space=pltpu.VMEM)),
        input_output_aliases={0: 2, 1: 3}, compiler_params=pltpu.CompilerParams(has_side_effects=SIDE_EFFECT))(
        pltpu.with_memory_space_constraint(part, pltpu.HBM),
        pltpu.with_memory_space_constraint(lax.empty(land_shape, part.dtype), pltpu.HBM))


def _chip_exchange_wait(send_sems, recv_sems, part, land, after):
    def body(p_ref, land_ref, send_sems, recv_sems, after_ref, p_dead, got_ref):
        for cp in _chip_copies(p_ref, land_ref, send_sems, recv_sems):
            cp.wait_send()
            cp.wait_recv()

    return pl.pallas_call(
        body, name="rs_in_wait", out_shape=(pltpu.HBM(part.shape, part.dtype), pltpu.HBM(land.shape, land.dtype)),
        in_specs=(HBM_ONLY, HBM_ONLY, SEM_SPEC, SEM_SPEC, pl.BlockSpec(memory_space=pl.ANY)),
        out_specs=(HBM_ONLY, HBM_ONLY), input_output_aliases={0: 0, 1: 1},
        compiler_params=pltpu.CompilerParams(has_side_effects=SIDE_EFFECT))(part, land, send_sems, recv_sems, after)


def _reduce_sums(fulls, recv_core, core, tag):
    return [_pair_sum(f, r, core, f"rs_pair_{tag}{i}") for i, (f, r) in enumerate(zip(fulls, recv_core))]


def _local_step(x, tgt, mods, w_in_t, shards, small, chip, core):
    sh1, sc1, g1, sh2, sc2, g2 = mods
    norm1_g, rel_bias, gn_g, gn_b, norm2_g, norm_f_g = small
    tables = _ret_tables()
    buckets = jnp.asarray(_bucket_tables())

    h1 = _norm_mod_fwd(x, norm1_g, sh1, sc1, "norm1_fwd")
    proj, slabs, gathered = _proj(h1, w_in_t, _Gather(shards[:3]))
    w_ret_out, w_att_out, w_o = (_from_slots(g, ax) for g, ax in zip(gathered, BIG_AXES[1:4]))
    (gated, ro, states), (w_ff1_8,) = _ret_fwd(proj, tables, gn_g, gn_b, comm=_Gather(shards[3:4]))
    w_ff1 = _from_slots(w_ff1_8, BIG_AXES[4])
    bias = _bias_build(rel_bias, buckets)
    outs, lses = [], []
    for gi in range(len(ATT_GROUPS)):
        res = _att_fwd(slabs, bias, gi, comm=_Gather(shards[4:]) if gi == 2 else None)
        if gi == 2:
            res, (w_ff2_8,) = res
        outs.append(res[0])
        lses.append(res[1])
    w_ff2 = _from_slots(w_ff2_8, BIG_AXES[5])
    att = _mix_fwd(outs, lses)
    ret_out = _mm(gated, w_ret_out, 'nn', tm=S, tn=256, tk=2048, name="ret_out")
    att_out = _mm(att, w_att_out, 'nn', tm=S, tn=512, tk=AW, name="att_out")
    merged = _merge_fwd(proj, ret_out, att_out)
    mixo, x1 = _mm(merged, w_o, 'nn', tm=S, tn=256, tk=D, name="w_o", res=x, gvec=g1)
    h2 = _norm_mod_fwd(x1, norm2_g, sh2, sc2, "norm2_fwd")
    u, act = _mm(h2, w_ff1, 'nn', tm=S, tn=512, tk=D, name="ff1", relu2=True)
    f, x2 = _mm(act, w_ff2, 'nn', tm=1024, tn=512, tk=2048, name="ff2", res=x1, gvec=g2)
    loss, dx2, g_normf, df, dg2 = _final_loss(x2, tgt, norm_f_g, f, g2)

    gw_ff2 = _mm(act, df, 'tn', tm=512, tn=D, tk=S, name="gw_ff2", out_dtype=BF16)
    du = _mm(df, w_ff2, 'nt', tm=S, tn=512, tk=D, name="d_act", out_dtype=BF16, relu2_of=u)
    gw_ff1 = _mm(h2, du, 'tn', tm=D, tn=512, tk=S, name="gw_ff1", out_dtype=BF16)
    fulls_a = [_to_slots(g, ax) for g, ax in zip((gw_ff1, gw_ff2), BIG_AXES[4:])]
    dh2, recv_core_a = _mm(du, w_ff1, 'nt', tm=1024, tn=1024, tk=2048, name="dh2", comm=_ExchangeCore(fulls_a))
    parts_a = _reduce_sums(fulls_a, recv_core_a, core, "a")
    dx1, dsc2, dsh2, g_norm2, dmixo, dg1 = _norm_mod_bwd(x1, norm2_g, sc2, dh2, dx2, "norm2_bwd", gate=(mixo, g1))

    gw_o = _mm(merged, dmixo, 'tn', tm=D, tn=512, tk=S, name="gw_o", out_dtype=BF16)
    dmerged = _mm(dmixo, w_o, 'nt', tm=S, tn=512, tk=D, name="dmerged")
    d_ret_out, d_att_out, dga, dgb = _merge_bwd(proj, ret_out, att_out, dmerged)
    gw_ret_out = _mm(gated, d_ret_out, 'tn', tm=512, tn=D, tk=S, name="gw_ret_out", out_dtype=BF16)
    gw_att_out = _mm(att, d_att_out, 'tn', tm=AW, tn=D, tk=S, name="gw_att_out", out_dtype=BF16)
    fulls_b = [_to_slots(g, ax) for g, ax in zip((gw_ret_out, gw_att_out, gw_o), BIG_AXES[1:4])]
    dgated, recv_core_b = _mm(d_ret_out, w_ret_out, 'nt', tm=S, tn=512, tk=D, name="dgated",
                              comm=_ExchangeCore(fulls_b))
    parts_b = _reduce_sums(fulls_b, recv_core_b, core, "b")
    datt = _mm(d_att_out, w_att_out, 'nt', tm=S, tn=AW, tk=D, name="datt")
    mix_grads = _mix_bwd(outs, lses, datt)
    datt_parts, ds_sums = [], []
    for gi in range(len(ATT_GROUPS)):
        comm = {1: _ExchangeChip(parts_b), 2: _ExchangeChip(parts_a[:1])}.get(gi)
        res = _att_bwd(slabs, bias, outs[gi], lses[gi], mix_grads[gi], mix_grads[3 + gi], gi, comm=comm)
        if gi == 1:
            res, recv_chip_b = res
        if gi == 2:
            res, recv_chip_a1 = res
        dq, dk, dv, ds_sum = res
        datt_parts += [dq.reshape(S, AW), dk.reshape(S, AW), dv.reshape(S, AW)]
        ds_sums.append(ds_sum)
    red_b = [_chip_sum(p, r, chip, f"rs_sum_b{i}") for i, (p, r) in enumerate(zip(parts_b, recv_chip_b))]
    g_bias = _bias_grad(jnp.concatenate(ds_sums, axis=0), buckets)[:, :, 0].T.reshape(1, -1)
    (dret, g_gn_g, g_gn_b), recv_chip_a2 = _ret_bwd(proj, tables, gn_g, gn_b, ro, states, dgated,
                                                    comm=_ExchangeChip(parts_a[1:]))
    recv_chip_a = list(recv_chip_a1) + list(recv_chip_a2)
    red_a = [_chip_sum(p, r, chip, f"rs_sum_a{i}") for i, (p, r) in enumerate(zip(parts_a, recv_chip_a))]
    dproj = jnp.concatenate([dret] + datt_parts + [dga, dgb], axis=1)
    gw_in_t = _mm(dproj, h1, 'tn', tm=512, tn=D, tk=S, name="gw_in", out_dtype=BF16)
    full_in = [_to_slots(gw_in_t, 0)]
    recv_core_in = _run_comm(_ExchangeCore(full_in), "rs_core_in")
    (part_in,) = _reduce_sums(full_in, recv_core_in, core, "c")
    *in_flight, token = _chip_exchange_start(part_in)
    _, dproj = lax.optimization_barrier((token, dproj))
    dh1 = _mm(dproj, w_in_t, 'nn', tm=1024, tn=1024, tk=2560, name="dh1")
    gx, dsc1, dsh1, g_norm1 = _norm_mod_bwd(x, norm1_g, sc1, dh1, dx1, "norm1_bwd")

    dmod = [dsh1, dsc1, dg1, dsh2, dsc2, dg2]
    small_g = [g_norm1, g_bias, g_gn_g, g_gn_b, g_norm2, g_normf]
    return loss, gx, in_flight, red_b + red_a, small_g, dmod


def _to_slots(g, axis):
    if axis == 0:
        return g.reshape(4, 2, g.shape[0] // N_DEV, g.shape[1])
    return g.reshape(g.shape[0], N_DEV, g.shape[1] // N_DEV).transpose(1, 0, 2).reshape(4, 2, g.shape[0], -1)


def _from_slots(w8, axis):
    if axis == 0:
        return w8.reshape(-1, w8.shape[2])
    return w8.transpose(1, 0, 2).reshape(w8.shape[1], -1)


BIG_AXES = (1, 0, 1, 0, 1, 0)


def kernel(x, c, w_ada, b_ada, norm1_g, w_in, rel_bias, ret_gn_g, ret_gn_b, w_ret_out, w_att_out, w_o, norm2_g, w_ff1, w_ff2, norm_f_g, loss_target, m_w_ada, m_b_ada, m_norm1_g, m_w_in, m_rel_bias, m_ret_gn_g, m_ret_gn_b, m_w_ret_out, m_w_att_out, m_w_o, m_norm2_g, m_w_ff1, m_w_ff2, m_norm_f_g, v_w_ada, v_b_ada, v_norm1_g, v_w_in, v_rel_bias, v_ret_gn_g, v_ret_gn_b, v_w_ret_out, v_w_att_out, v_w_o, v_norm2_g, v_w_ff1, v_w_ff2, v_norm_f_g):
    mx, my, mc = _mesh_pos()
    dev = 4 * mx + 2 * my + mc
    chip = jnp.reshape(2 * mx + my, (1,)).astype(jnp.int32)
    core = jnp.reshape(mc, (1,)).astype(jnp.int32)
    ada_w = D * 6 // N_DEV

    w_in, m_w_in, v_w_in = (jnp.transpose(t, (0, 2, 1)) for t in (w_in, m_w_in, v_w_in))

    shards = [w[0].astype(BF16) for w in (w_in, w_ret_out, w_att_out, w_o, w_ff1, w_ff2)]
    c_all, w_in8 = _run_comm(_Gather([c, shards[0]], relay=True), "gather_c_w_in")
    c_all = c_all.reshape(N_DEV, D)
    b_sl = lax.dynamic_slice(b_ada, (0, dev * ada_w), (1, ada_w))
    (mod_all,) = _run_comm(_Gather([_ada_fwd(c_all, w_ada[0], b_sl)]), "gather_mod")
    mod = lax.dynamic_index_in_dim(mod_all, dev, axis=1, keepdims=False).reshape(6, D)
    mods = tuple(mod[i:i + 1] for i in range(6))

    small = (norm1_g, rel_bias, ret_gn_g, ret_gn_b, norm2_g, norm_f_g.reshape(1, D))
    loss, gx, in_flight, big_red, small_g, dmod = _local_step(x[0], loss_target[0], mods, w_in8.reshape(IN_COLS, D),
                                                              shards[1:], small, chip, core)

    gathered = _run_comm(_Gather(dmod + small_g + [loss]), "gather_small")
    g_b_ada, dmod_all, (g_norm1, g_bias, g_gn_g, g_gn_b, g_norm2, g_normf, loss_sum) = _sum_small(gathered)
    loss_out = loss_sum[0, 0]
    g_w_ada = _ada_bwd(c_all, lax.dynamic_slice(dmod_all, (0, dev * ada_w), (N_DEV, ada_w)))

    names = ['w_ada', 'b_ada', 'norm1_g', 'w_in', 'rel_bias', 'ret_gn_g', 'ret_gn_b', 'w_ret_out', 'w_att_out',
             'w_o', 'norm2_g', 'w_ff1', 'w_ff2', 'norm_f_g']
    ws = dict(zip(names, (w_ada, b_ada, norm1_g, w_in, rel_bias, ret_gn_g, ret_gn_b, w_ret_out, w_att_out, w_o,
                          norm2_g, w_ff1, w_ff2, norm_f_g)))
    ms = dict(zip(names, (m_w_ada, m_b_ada, m_norm1_g, m_w_in, m_rel_bias, m_ret_gn_g, m_ret_gn_b, m_w_ret_out,
                          m_w_att_out, m_w_o, m_norm2_g, m_w_ff1, m_w_ff2, m_norm_f_g)))
    vs = dict(zip(names, (v_w_ada, v_b_ada, v_norm1_g, v_w_in, v_rel_bias, v_ret_gn_g, v_ret_gn_b, v_w_ret_out,
                          v_w_att_out, v_w_o, v_norm2_g, v_w_ff1, v_w_ff2, v_norm_f_g)))
    grads = dict(w_ada=g_w_ada, w_ret_out=big_red[0], w_att_out=big_red[1], w_o=big_red[2],
                 w_ff1=big_red[3], w_ff2=big_red[4], b_ada=g_b_ada, norm1_g=g_norm1, rel_bias=g_bias,
                 ret_gn_g=g_gn_g, ret_gn_b=g_gn_b, norm2_g=g_norm2, norm_f_g=g_normf)
    delta, new_m, new_v = {}, {}, {}
    for n in ('w_ada', 'w_ret_out', 'w_att_out', 'w_o', 'w_ff1', 'w_ff2'):
        delta[n], new_m[n], new_v[n] = _adamw(ws[n], grads[n], ms[n], vs[n], "adamw_" + n)
        grads[n] = grads[n].reshape(ws[n].shape)
    small_names = ('b_ada', 'norm1_g', 'rel_bias', 'ret_gn_g', 'ret_gn_b', 'norm2_g', 'norm_f_g')
    two_d = {n: (1, ws[n].size) if ws[n].ndim == 1 else ws[n].shape for n in small_names}
    d_, m_, v_ = _adamw_small(*[[src[n].reshape(two_d[n]) for n in small_names] for src in (ws, grads, ms, vs)])
    for i, n in enumerate(small_names):
        shp = ws[n].shape
        delta[n], new_m[n], new_v[n] = d_[i].reshape(shp), m_[i].reshape(shp), v_[i].reshape(shp)
        grads[n] = grads[n].reshape(shp)

    done = lax.optimization_barrier((gx, tuple(d_), tuple(delta[n] for n in ('w_ada', 'w_ff1', 'w_ff2'))))
    send_sems, recv_sems, part_in, land = in_flight
    part_in, recv_chip_in = _chip_exchange_wait(send_sems, recv_sems, part_in, land, done[0])
    grads['w_in'] = _chip_sum(part_in, recv_chip_in, chip, "rs_sum_c")
    delta['w_in'], new_m['w_in'], new_v['w_in'] = _adamw(w_in, grads['w_in'], m_w_in, v_w_in, "adamw_w_in")
    grads['w_in'] = grads['w_in'].reshape(w_in.shape)
    for d in (grads, delta, new_m, new_v):
        d['w_in'] = jnp.transpose(d['w_in'], (0, 2, 1))
    return (loss_out, gx[None], *[grads[n] for n in names], *[delta[n] for n in names],
            *[new_m[n] for n in names], *[new_v[n] for n in names])
```

```python
import functools
import math

import numpy as np
import jax
import jax.numpy as jnp
from jax import lax
from jax.experimental import pallas as pl
from jax.experimental.pallas import tpu as pltpu

F32 = jnp.float32
BF16 = jnp.bfloat16
MESH = pl.DeviceIdType.MESH

N_DEV = 8
S = 2048
D = 1024
RET_HEADS = 4
RET_DK = 256
RET_DV = 512
CHUNK = 128
N_CHUNK = S // CHUNK
ATT_GROUPS = ((128, 1), (512, 4), (2048, 16))
ATT_HG = 4
ATT_DH = 128
ATT_BLK = 128
N_BUCKETS = 32
MAX_DIST = 2048
D_FF = 4096
IN_COLS = 12800
OFF_RQ, OFF_RK, OFF_RV, OFF_RG, OFF_ATT = 0, 1024, 2048, 4096, 6144
OFF_GA, OFF_GB = 6144, 7168
RMS_EPS = 1e-6
GN_EPS = 1e-5
ADAM_LR, ADAM_B1, ADAM_B2, ADAM_EPS, ADAM_WD, ADAM_STEP = 0.001, 0.9, 0.999, 1e-08, 0.01, 10
VMEM_LIMIT = 48 * 1024 * 1024


def _pcall(body, **kw):
    return pl.pallas_call(body, **kw)


def _params(sem=None):
    return pltpu.CompilerParams(dimension_semantics=sem, vmem_limit_bytes=VMEM_LIMIT)


HBM_SPEC = pl.BlockSpec(memory_space=pl.ANY)


def _carry(body, comm, *, name, grid, in_specs, out_specs, out_shape, scratch_shapes=()):
    single = not isinstance(out_specs, (tuple, list))
    o_specs = (out_specs,) if single else tuple(out_specs)
    o_shape = (out_shape,) if single else tuple(out_shape)
    n_in, n_out, n_scr = len(in_specs), len(o_specs), len(scratch_shapes)
    nci, nco = len(comm.ins), len(comm.out_shape)
    total = int(np.prod(grid))

    def wrapped(*refs):
        bounds = np.cumsum([0, n_in, nci, n_out, nco, n_scr])
        a, ci, o, co, scr = (refs[bounds[i]:bounds[i + 1]] for i in range(5))
        sems = refs[bounds[5]:]
        flat = 0
        for d, g in enumerate(grid):
            flat = flat * g + pl.program_id(d)

        @pl.when(flat == 0)
        def _():
            comm.start(ci, co, sems)

        body(*a, *o, *scr)

        @pl.when(flat == total - 1)
        def _():
            comm.finish(ci, co, sems)

    call = _pcall(wrapped, name=name, grid=grid, in_specs=list(in_specs) + [HBM_SPEC] * nci,
                  out_specs=o_specs + (HBM_SPEC,) * nco, out_shape=o_shape + tuple(comm.out_shape),
                  scratch_shapes=list(scratch_shapes) + list(comm.sems),
                  compiler_params=_params(("arbitrary",) * len(grid)))

    def run(*args):
        res = call(*args, *comm.ins)
        own = res[0] if single else tuple(res[:n_out])
        return own, tuple(res[n_out:])

    return run


def _run_comm(comm, name):
    nci, nco = len(comm.ins), len(comm.out_shape)

    def body(*refs):
        ci, co, sems = refs[:nci], refs[nci:nci + nco], refs[nci + nco:]
        comm.start(ci, co, sems)
        comm.finish(ci, co, sems)

    return _pcall(body, name=name, in_specs=[HBM_SPEC] * nci, out_specs=(HBM_SPEC,) * nco,
                  out_shape=tuple(comm.out_shape), scratch_shapes=list(comm.sems))(*comm.ins)


def _dot(a, b, dn):
    return lax.dot_general(a.astype(BF16), b.astype(BF16), (dn, ((), ())), preferred_element_type=F32)


NN = ((1,), (0,))
NT = ((1,), (1,))
TN = ((0,), (0,))


def _mm(a, b, mode, *, tm, tn, tk, name, out_dtype=F32, res=None, gvec=None, relu2=False, relu2_of=None, comm=None,
        after=None):
    if mode == 'nn':
        (M, K), (_, N) = a.shape, b.shape
        a_spec = pl.BlockSpec((tm, tk), lambda i, j, k: (i, k))
        b_spec = pl.BlockSpec((tk, tn), lambda i, j, k: (k, j))
        dn = NN
    elif mode == 'nt':
        (M, K), (N, _) = a.shape, b.shape
        a_spec = pl.BlockSpec((tm, tk), lambda i, j, k: (i, k))
        b_spec = pl.BlockSpec((tn, tk), lambda i, j, k: (j, k))
        dn = NT
    else:
        (K, M), (_, N) = a.shape, b.shape
        a_spec = pl.BlockSpec((tk, tm), lambda i, j, k: (k, i))
        b_spec = pl.BlockSpec((tk, tn), lambda i, j, k: (k, j))
        dn = TN
    assert M % tm == 0 and N % tn == 0 and K % tk == 0, (name, M, N, K)
    nk = K // tk
    fused = res is not None
    o_spec = pl.BlockSpec((tm, tn), lambda i, j, k: (i, j))

    def body(a_ref, b_ref, *rest):
        acc_ref = rest[-1] if nk > 1 else None
        if after is not None:
            rest = rest[1:]
        if fused:
            res_ref, g_ref, o_ref, x_ref = rest[:4]
        elif relu2_of is not None:
            u_ref, o_ref = rest[:2]
        elif relu2:
            o_ref, act_ref = rest[:2]
        else:
            o_ref = rest[0]

        def finish(acc):
            if relu2_of is not None:
                acc = acc * (2.0 * jnp.maximum(u_ref[...], 0.0))
            o_ref[...] = acc.astype(o_ref.dtype)
            if fused:
                x_ref[...] = res_ref[...] + g_ref[...] * acc
            if relu2:
                r = jnp.maximum(acc, 0.0)
                act_ref[...] = (r * r).astype(BF16)

        p = _dot(a_ref[...], b_ref[...], dn)
        if nk == 1:
            finish(p)
        else:
            k = pl.program_id(2)

            @pl.when(k == 0)
            def _():
                acc_ref[...] = p

            @pl.when(k > 0)
            def _():
                acc_ref[...] += p

            @pl.when(k == nk - 1)
            def _():
                finish(acc_ref[...])

    in_specs = [a_spec, b_spec]
    args = [a, b]
    if after is not None:
        in_specs.append(pl.BlockSpec(memory_space=pl.ANY))
        args.append(after)
    out_shape = jax.ShapeDtypeStruct((M, N), out_dtype)
    out_specs = o_spec
    if fused:
        in_specs += [pl.BlockSpec((tm, tn), lambda i, j, k: (i, j)), pl.BlockSpec((1, tn), lambda i, j, k: (0, j))]
        args += [res, gvec]
        out_shape = (out_shape, jax.ShapeDtypeStruct((M, N), F32))
        out_specs = (o_spec, pl.BlockSpec((tm, tn), lambda i, j, k: (i, j)))
    elif relu2_of is not None:
        in_specs.append(pl.BlockSpec((tm, tn), lambda i, j, k: (i, j)))
        args.append(relu2_of)
    elif relu2:
        out_shape = (out_shape, jax.ShapeDtypeStruct((M, N), BF16))
        out_specs = (o_spec, pl.BlockSpec((tm, tn), lambda i, j, k: (i, j)))
    kw = dict(name=name, grid=(M // tm, N // tn, nk), in_specs=in_specs, out_specs=out_specs,
              out_shape=out_shape, scratch_shapes=[pltpu.VMEM((tm, tn), F32)] if nk > 1 else [])
    if comm is not None:
        return _carry(body, comm, **kw)(*args)
    return _pcall(body, compiler_params=_params(("parallel", "parallel", "arbitrary")), **kw)(*args)


PROJ_TN = 512
ATT_T0, ATT_T1 = 6144 // PROJ_TN, 10752 // PROJ_TN
N_SLABS = (ATT_T1 - ATT_T0) * 4
MAIN_COLS = IN_COLS - (ATT_T1 - ATT_T0) * PROJ_TN


def _proj(h1, w_in_t, comm):
    nj = IN_COLS // PROJ_TN

    def body(a_ref, b_ref, main_ref, slab_ref):
        j = pl.program_id(1)
        is_att = (j >= ATT_T0) & (j < ATT_T1)
        chunks = [pl.ds(c * 512, 512) for c in range(S // 512)]

        @pl.when(jnp.logical_not(is_att))
        def _():
            for rows in chunks:
                main_ref[rows, :] = _dot(a_ref[rows, :], b_ref[...], NT)

        @pl.when(is_att)
        def _():
            for rows in chunks:
                p = _dot(a_ref[rows, :], b_ref[...], NT)
                for h in range(4):
                    slab_ref[h, rows, :] = p[:, h * 128:(h + 1) * 128]

    main_idx = lambda j: jnp.where(j < ATT_T0, j, jnp.where(j < ATT_T1, ATT_T0 - 1, j - (ATT_T1 - ATT_T0)))
    slab_idx = lambda j: jnp.clip(j - ATT_T0, 0, ATT_T1 - ATT_T0 - 1)
    (main, slabs), got = _carry(
        body, comm, name="proj", grid=(1, nj, 1),
        in_specs=[pl.BlockSpec((S, D), lambda i, j, k: (0, 0)), pl.BlockSpec((PROJ_TN, D), lambda i, j, k: (j, 0))],
        out_specs=(pl.BlockSpec((S, PROJ_TN), lambda i, j, k: (0, main_idx(j))),
                   pl.BlockSpec((4, S, 128), lambda i, j, k: (slab_idx(j), 0, 0))),
        out_shape=(jax.ShapeDtypeStruct((S, MAIN_COLS), F32), jax.ShapeDtypeStruct((N_SLABS, S, 128), F32)))(h1, w_in_t)
    return main, slabs, got


TR = 256


def _row_spec(w=D):
    return pl.BlockSpec((TR, w), lambda i: (i, 0))


def _vec_spec(w=D):
    return pl.BlockSpec((1, w), lambda i: (0, 0))


def _norm_mod_fwd(x, g, sh, sc, name):
    def body(x_ref, g_ref, sh_ref, sc_ref, o_ref):
        xv = x_ref[...]
        rstd = lax.rsqrt(jnp.mean(xv * xv, axis=-1, keepdims=True) + RMS_EPS)
        n = xv * rstd * g_ref[...]
        o_ref[...] = (n * (1.0 + sc_ref[...]) + sh_ref[...]).astype(BF16)

    return _pcall(body, name=name, grid=(S // TR,), in_specs=[_row_spec(), _vec_spec(), _vec_spec(), _vec_spec()],
                  out_specs=_row_spec(), out_shape=jax.ShapeDtypeStruct((S, D), BF16),
                  compiler_params=_params(("parallel",)))(x, g, sh, sc)


def _norm_mod_bwd(x, g, sc, dh, dres, name, gate=None):
    gated = gate is not None

    def body(x_ref, g_ref, sc_ref, dh_ref, dres_ref, *rest):
        if gated:
            f_ref, gv_ref, dx_ref, dsc_ref, dsh_ref, dg_ref, dz_ref, dgv_ref = rest
        else:
            dx_ref, dsc_ref, dsh_ref, dg_ref = rest
        i = pl.program_id(0)
        xv = x_ref[...]
        dh = dh_ref[...]
        rstd = lax.rsqrt(jnp.mean(xv * xv, axis=-1, keepdims=True) + RMS_EPS)
        xhat = xv * rstd
        gv = g_ref[...]
        dn = dh * (1.0 + sc_ref[...])
        dxhat = dn * gv
        dx = dres_ref[...] + rstd * (dxhat - xhat * jnp.mean(dxhat * xhat, axis=-1, keepdims=True))
        dx_ref[...] = dx
        sums = [(dsc_ref, jnp.sum(dh * (xhat * gv), axis=0, keepdims=True)),
                (dsh_ref, jnp.sum(dh, axis=0, keepdims=True)),
                (dg_ref, jnp.sum(dn * xhat, axis=0, keepdims=True))]
        if gated:
            dz_ref[...] = (dx * gv_ref[...]).astype(BF16)
            sums.append((dgv_ref, jnp.sum(dx * f_ref[...], axis=0, keepdims=True)))

        @pl.when(i == 0)
        def _():
            for ref, p in sums:
                ref[...] = p

        @pl.when(i > 0)
        def _():
            for ref, p in sums:
                ref[...] += p

    vec = jax.ShapeDtypeStruct((1, D), F32)
    in_specs = [_row_spec(), _vec_spec(), _vec_spec(), _row_spec(), _row_spec()]
    out_specs = [_row_spec(), _vec_spec(), _vec_spec(), _vec_spec()]
    out_shape = [jax.ShapeDtypeStruct((S, D), F32), vec, vec, vec]
    args = [x, g, sc, dh, dres]
    if gated:
        in_specs += [_row_spec(), _vec_spec()]
        out_specs += [_row_spec(), _vec_spec()]
        out_shape += [jax.ShapeDtypeStruct((S, D), BF16), vec]
        args += list(gate)
    return _pcall(body, name=name, grid=(S // TR,), in_specs=in_specs, out_specs=tuple(out_specs),
                  out_shape=tuple(out_shape), compiler_params=_params(("arbitrary",)))(*args)


def _final_loss(x2, tgt, g, f, g2):
    def body(x_ref, t_ref, g_ref, f_ref, g2_ref, loss_ref, dx_ref, dg_ref, df_ref, dg2_ref):
        i = pl.program_id(0)
        xv = x_ref[...]
        gv = g_ref[...]
        rstd = lax.rsqrt(jnp.mean(xv * xv, axis=-1, keepdims=True) + RMS_EPS)
        xhat = xv * rstd
        err = xhat * gv - t_ref[...]
        dy = err * (1.0 / D)
        dxhat = dy * gv
        dx = rstd * (dxhat - xhat * jnp.mean(dxhat * xhat, axis=-1, keepdims=True))
        dx_ref[...] = dx
        df_ref[...] = (dx * g2_ref[...]).astype(BF16)
        p_g = jnp.sum(dy * xhat, axis=0, keepdims=True)
        p_g2 = jnp.sum(dx * f_ref[...], axis=0, keepdims=True)
        p_l = jnp.zeros((1, 128), F32) + 0.5 * jnp.sum(jnp.mean(err * err, axis=-1, keepdims=True))

        @pl.when(i == 0)
        def _():
            dg_ref[...] = p_g
            dg2_ref[...] = p_g2
            loss_ref[...] = p_l

        @pl.when(i > 0)
        def _():
            dg_ref[...] += p_g
            dg2_ref[...] += p_g2
            loss_ref[...] += p_l

    vec = jax.ShapeDtypeStruct((1, D), F32)
    return _pcall(body, name="final_loss", grid=(S // TR,),
                  in_specs=[_row_spec(), _row_spec(), _vec_spec(), _row_spec(), _vec_spec()],
                  out_specs=(_vec_spec(128), _row_spec(), _vec_spec(), _row_spec(), _vec_spec()),
                  out_shape=(jax.ShapeDtypeStruct((1, 128), F32), jax.ShapeDtypeStruct((S, D), F32), vec,
                             jax.ShapeDtypeStruct((S, D), BF16), vec),
                  compiler_params=_params(("arbitrary",)))(x2, tgt, g, f, g2)


HALF = 512


def _merge_fwd(proj, ret_out, att_out):
    def body(ga_ref, gb_ref, r_ref, a_ref, o_ref):
        o_ref[...] = (jax.nn.sigmoid(ga_ref[...]) * r_ref[...] + jax.nn.sigmoid(gb_ref[...]) * a_ref[...]).astype(BF16)

    blk = lambda off: pl.BlockSpec((TR, HALF), lambda i, j: (i, off // HALF + j))
    return _pcall(body, name="merge_fwd", grid=(S // TR, D // HALF),
                  in_specs=[blk(OFF_GA), blk(OFF_GB), blk(0), blk(0)], out_specs=blk(0),
                  out_shape=jax.ShapeDtypeStruct((S, D), BF16),
                  compiler_params=_params(("parallel", "parallel")))(proj, proj, ret_out, att_out)


def _merge_bwd(proj, ret_out, att_out, dmerged):
    def body(ga_ref, gb_ref, r_ref, a_ref, dm_ref, dr_ref, da_ref, dga_ref, dgb_ref):
        sa = jax.nn.sigmoid(ga_ref[...])
        sb = jax.nn.sigmoid(gb_ref[...])
        dm = dm_ref[...]
        dr_ref[...] = (dm * sa).astype(BF16)
        da_ref[...] = (dm * sb).astype(BF16)
        dga_ref[...] = (dm * r_ref[...] * (sa * (1.0 - sa))).astype(BF16)
        dgb_ref[...] = (dm * a_ref[...] * (sb * (1.0 - sb))).astype(BF16)

    blk = lambda off: pl.BlockSpec((TR, HALF), lambda i, j: (i, off // HALF + j))
    o = jax.ShapeDtypeStruct((S, D), BF16)
    return _pcall(body, name="merge_bwd", grid=(S // TR, D // HALF),
                  in_specs=[blk(OFF_GA), blk(OFF_GB), blk(0), blk(0), blk(0)], out_specs=(blk(0),) * 4,
                  out_shape=(o, o, o, o),
                  compiler_params=_params(("parallel", "parallel")))(proj, proj, ret_out, att_out, dmerged)


def _ret_tables():
    H, C = RET_HEADS, CHUNK
    log_g = jnp.log1p(-(2.0 ** (-5.0 - jnp.arange(H, dtype=F32))))
    idx = jnp.arange(C, dtype=F32)
    rel = idx[:, None] - idx[None, :]
    inner = jnp.where(rel >= 0, jnp.exp(log_g[:, None, None] * jnp.maximum(rel, 0.0)), 0.0)
    qd = jnp.exp(log_g[:, None] * (idx + 1.0))[:, :, None]
    kd = jnp.exp(log_g[:, None] * (C - 1.0 - idx))[:, :, None]
    cd = jnp.broadcast_to(jnp.exp(log_g * C)[:, None, None], (H, 1, 128))
    half = RET_DK // 2
    inv = 10000.0 ** (-jnp.arange(half, dtype=F32) / half)
    ang = jnp.arange(S, dtype=F32)[:, None] * inv[None, :]
    return inner, qd, kd, cd, jnp.cos(ang), jnp.sin(ang)


def _rot(x, cos, sin):
    x1, x2 = x[:, :128], x[:, 128:]
    return jnp.concatenate([x1 * cos - x2 * sin, x1 * sin + x2 * cos], axis=1)


def _rot_t(d, cos, sin):
    d1, d2 = d[:, :128], d[:, 128:]
    return jnp.concatenate([d1 * cos + d2 * sin, d2 * cos - d1 * sin], axis=1)


RET_COLS = OFF_ATT
RET_VW = RET_HEADS * RET_DV


def _ret_specs(chunk_of):
    ci = chunk_of
    whole = lambda shape: pl.BlockSpec(shape, lambda t: (0,) * len(shape))
    return [
        pl.BlockSpec((CHUNK, RET_COLS), lambda t: (ci(t), 0)),
        pl.BlockSpec((CHUNK, 128), lambda t: (ci(t), 0)),
        pl.BlockSpec((CHUNK, 128), lambda t: (ci(t), 0)),
        whole((RET_HEADS, CHUNK, CHUNK)), whole((RET_HEADS, CHUNK, 1)), whole((RET_HEADS, CHUNK, 1)),
        whole((RET_HEADS, 1, 128)), whole((1, RET_VW)), whole((1, RET_VW)),
    ]


def _ret_cols(h):
    q = slice(OFF_RQ + h * RET_DK, OFF_RQ + (h + 1) * RET_DK)
    k = slice(OFF_RK + h * RET_DK, OFF_RK + (h + 1) * RET_DK)
    v = slice(OFF_RV + h * RET_DV, OFF_RV + (h + 1) * RET_DV)
    g = slice(OFF_RG + h * RET_DV, OFF_RG + (h + 1) * RET_DV)
    return q, k, v, g, slice(h * RET_DV, (h + 1) * RET_DV)


def _ret_fwd(proj, tables, gn_g, gn_b, comm=None):
    inner, qd, kd, cd, cos, sin = tables

    def body(x_ref, cos_ref, sin_ref, in_ref, qd_ref, kd_ref, cd_ref, g_ref, b_ref,
             gated_ref, ro_ref, st_ref, s_scr):
        i = pl.program_id(0)

        @pl.when(i == 0)
        def _():
            s_scr[...] = jnp.zeros_like(s_scr)

        cosv, sinv = cos_ref[...], sin_ref[...]
        for h in range(RET_HEADS):
            cq, ck, cv, cg, co = _ret_cols(h)
            q = _rot(x_ref[:, cq], cosv, sinv)
            k = _rot(x_ref[:, ck], cosv, sinv) * (RET_DK ** -0.5)
            v = x_ref[:, cv]
            st = s_scr[h]
            st_ref[h] = st
            s = _dot(q, k, NT) * in_ref[h]
            o = _dot(s, v, NN) + _dot(q, st, NN) * qd_ref[h]
            s_scr[h] = st * cd_ref[h, :, :1] + _dot(k * kd_ref[h], v, TN)
            ro_ref[:, co] = o
            mu = jnp.mean(o, axis=-1, keepdims=True)
            oc = o - mu
            var = jnp.mean(oc * oc, axis=-1, keepdims=True)
            rn = oc * lax.rsqrt(var + GN_EPS) * g_ref[:, co] + b_ref[:, co]
            rg = x_ref[:, cg]
            gated_ref[:, co] = (rg * jax.nn.sigmoid(rg) * rn).astype(BF16)

    ospec = pl.BlockSpec((CHUNK, RET_VW), lambda t: (t, 0))
    kw = dict(name="ret_fwd", grid=(N_CHUNK,), in_specs=_ret_specs(lambda t: t),
              out_specs=(ospec, ospec, pl.BlockSpec((RET_HEADS, None, RET_DK, RET_DV), lambda t: (0, t, 0, 0))),
              out_shape=(jax.ShapeDtypeStruct((S, RET_VW), BF16), jax.ShapeDtypeStruct((S, RET_VW), F32),
                         jax.ShapeDtypeStruct((RET_HEADS, N_CHUNK, RET_DK, RET_DV), F32)),
              scratch_shapes=[pltpu.VMEM((RET_HEADS, RET_DK, RET_DV), F32)])
    args = (proj, cos, sin, inner, qd, kd, cd, gn_g, gn_b)
    if comm is not None:
        return _carry(body, comm, **kw)(*args)
    return _pcall(body, compiler_params=_params(("arbitrary",)), **kw)(*args)


def _ret_bwd(proj, tables, gn_g, gn_b, ro, states, dgated, comm=None):
    inner, qd, kd, cd, cos, sin = tables
    last = N_CHUNK - 1

    def body(x_ref, cos_ref, sin_ref, in_ref, qd_ref, kd_ref, cd_ref, g_ref, b_ref, ro_ref, st_ref, dg_ref,
             dx_ref, gg_ref, gb_ref, gs_scr):
        t = pl.program_id(0)

        @pl.when(t == 0)
        def _():
            gs_scr[...] = jnp.zeros_like(gs_scr)
            gg_ref[...] = jnp.zeros_like(gg_ref)
            gb_ref[...] = jnp.zeros_like(gb_ref)

        cosv, sinv = cos_ref[...], sin_ref[...]
        for h in range(RET_HEADS):
            cq, ck, cv, cg, co = _ret_cols(h)
            q = _rot(x_ref[:, cq], cosv, sinv)
            k = _rot(x_ref[:, ck], cosv, sinv) * (RET_DK ** -0.5)
            v = x_ref[:, cv]
            qdv, kdv, dm = qd_ref[h], kd_ref[h], in_ref[h]
            st = st_ref[h]
            o = ro_ref[:, co]
            gv = g_ref[:, co]
            mu = jnp.mean(o, axis=-1, keepdims=True)
            oc = o - mu
            rstd = lax.rsqrt(jnp.mean(oc * oc, axis=-1, keepdims=True) + GN_EPS)
            ohat = oc * rstd
            rn = ohat * gv + b_ref[:, co]
            rg = x_ref[:, cg]
            sg = jax.nn.sigmoid(rg)
            dgt = dg_ref[:, co]
            drn = dgt * (rg * sg)
            dx_ref[:, cg] = (dgt * rn * (sg * (1.0 + rg * (1.0 - sg)))).astype(BF16)
            gg_ref[:, co] += jnp.sum(drn * ohat, axis=0, keepdims=True)
            gb_ref[:, co] += jnp.sum(drn, axis=0, keepdims=True)
            dohat = drn * gv
            do = rstd * (dohat - jnp.mean(dohat, axis=-1, keepdims=True)
                         - ohat * jnp.mean(dohat * ohat, axis=-1, keepdims=True))
            gs = gs_scr[h]
            s = _dot(q, k, NT) * dm
            dsr = _dot(do, v, NT) * dm
            dq = _dot(dsr, k, NN) + _dot(do, st, NT) * qdv
            dk = _dot(dsr, q, TN) + _dot(v, gs, NT) * kdv
            dv = _dot(s, do, TN) + _dot(k * kdv, gs, NN)
            gs_scr[h] = gs * cd_ref[h, :, :1] + _dot(q * qdv, do, TN)
            dx_ref[:, cq] = _rot_t(dq, cosv, sinv).astype(BF16)
            dx_ref[:, ck] = (_rot_t(dk, cosv, sinv) * (RET_DK ** -0.5)).astype(BF16)
            dx_ref[:, cv] = dv.astype(BF16)

    rev = lambda t: last - t
    vblk = pl.BlockSpec((CHUNK, RET_VW), lambda t: (rev(t), 0))
    vspec = pl.BlockSpec((1, RET_VW), lambda t: (0, 0))
    kw = dict(name="ret_bwd", grid=(N_CHUNK,),
              in_specs=_ret_specs(rev) + [vblk, pl.BlockSpec((RET_HEADS, None, RET_DK, RET_DV),
                                                             lambda t: (0, rev(t), 0, 0)), vblk],
              out_specs=(pl.BlockSpec((CHUNK, RET_COLS), lambda t: (rev(t), 0)), vspec, vspec),
              out_shape=(jax.ShapeDtypeStruct((S, RET_COLS), BF16), jax.ShapeDtypeStruct((1, RET_VW), F32),
                         jax.ShapeDtypeStruct((1, RET_VW), F32)),
              scratch_shapes=[pltpu.VMEM((RET_HEADS, RET_DK, RET_DV), F32)])
    args = (proj, cos, sin, inner, qd, kd, cd, gn_g, gn_b, ro, states, dgated)
    if comm is not None:
        return _carry(body, comm, **kw)(*args)
    return _pcall(body, compiler_params=_params(("arbitrary",)), **kw)(*args)


def _bucket_tables():
    qi = np.arange(ATT_BLK)[:, None]
    kj = np.arange(2 * ATT_BLK)[None, :]
    m = ATT_BLK + qi - kj
    out = []
    for win, dil in ATT_GROUPS:
        w = win // dil
        dist = (np.clip(m, 0, w) * dil).astype(np.int32)
        max_exact = N_BUCKETS // 2
        d_f = np.maximum(dist, 1).astype(np.float32)
        large = max_exact + (np.log(d_f / np.float32(max_exact)) / np.float32(math.log(MAX_DIST / max_exact))
                             * np.float32(N_BUCKETS - max_exact)).astype(np.int32)
        large = np.minimum(large, N_BUCKETS - 1)
        out.append(np.where(dist < max_exact, dist, large).astype(np.int32))
    return np.stack(out)


def _bias_build(rel_bias, buckets):
    def body(tab_ref, bk_ref, o_ref):
        hh = pl.program_id(0)
        bk = bk_ref[...]
        acc = jnp.zeros((ATT_BLK, 2 * ATT_BLK), F32)
        for b in range(N_BUCKETS):
            acc = jnp.where(bk == b, tab_ref[b, hh], acc)
        o_ref[...] = acc

    nh = len(ATT_GROUPS) * ATT_HG
    return _pcall(body, name="bias_build", grid=(nh,),
                  in_specs=[pl.BlockSpec(memory_space=pltpu.SMEM),
                            pl.BlockSpec((None, ATT_BLK, 2 * ATT_BLK), lambda hh: (hh // ATT_HG, 0, 0))],
                  out_specs=pl.BlockSpec((None, ATT_BLK, 2 * ATT_BLK), lambda hh: (hh, 0, 0)),
                  out_shape=jax.ShapeDtypeStruct((nh, ATT_BLK, 2 * ATT_BLK), F32),
                  compiler_params=_params(("parallel",)))(rel_bias, buckets)


def _bias_grad(ds_sum, buckets):
    def body(ds_ref, bk_ref, o_ref):
        bk = bk_ref[...]
        ds = ds_ref[...]
        rows = lax.broadcasted_iota(jnp.int32, (N_BUCKETS, 128), 0)
        acc = jnp.zeros((N_BUCKETS, 128), F32)
        for b in range(N_BUCKETS):
            acc = jnp.where(rows == b, jnp.sum(jnp.where(bk == b, ds, 0.0)), acc)
        o_ref[...] = acc

    nh = len(ATT_GROUPS) * ATT_HG
    return _pcall(body, name="bias_grad", grid=(nh,),
                  in_specs=[pl.BlockSpec((None, ATT_BLK, 2 * ATT_BLK), lambda hh: (hh, 0, 0)),
                            pl.BlockSpec((None, ATT_BLK, 2 * ATT_BLK), lambda hh: (hh // ATT_HG, 0, 0))],
                  out_specs=pl.BlockSpec((None, N_BUCKETS, 128), lambda hh: (hh, 0, 0)),
                  out_shape=jax.ShapeDtypeStruct((nh, N_BUCKETS, 128), F32),
                  compiler_params=_params(("parallel",)))(ds_sum, buckets)


def _att_valid(n):
    qi = lax.broadcasted_iota(jnp.int32, (ATT_BLK, 2 * ATT_BLK), 0)
    kj = lax.broadcasted_iota(jnp.int32, (ATT_BLK, 2 * ATT_BLK), 1)
    m = ATT_BLK + qi - kj
    first_key = jnp.where(n > 0, 0, ATT_BLK)
    return (m >= 0) & (m <= ATT_BLK) & (kj >= first_key)


ATT_HP = (1, 2, 2)


def _att_geometry(gi):
    _, dil = ATT_GROUPS[gi]
    return dil, S // dil // ATT_BLK, ATT_HP[gi]


def _blk(dil, r, n):
    if dil == 1:
        return pl.ds(n * ATT_BLK, ATT_BLK)
    return pl.ds(r + n * ATT_BLK * dil, ATT_BLK, stride=dil)


def _slab_specs(gi):
    _, _, hp = _att_geometry(gi)
    per = ATT_HG // hp
    return [pl.BlockSpec((hp, S, ATT_DH), lambda g, r, part=part: ((3 * gi + part) * per + g, 0, 0))
            for part in range(3)]


def _head_specs(gi, count):
    _, _, hp = _att_geometry(gi)
    return [pl.BlockSpec((hp, S, ATT_DH), lambda g, r: (g, 0, 0))] * count


def _bias_spec(gi):
    _, _, hp = _att_geometry(gi)
    return pl.BlockSpec((hp, ATT_BLK, 2 * ATT_BLK), lambda g, r: (gi * (ATT_HG // hp) + g, 0, 0))


def _att_fwd(slabs, bias, gi, comm=None):
    dil, nb, hp = _att_geometry(gi)
    scale = ATT_DH ** -0.5

    def body(q_ref, k_ref, v_ref, bias_ref, o_ref, l_ref):
        r = pl.program_id(1)
        for n in range(nb):
            valid = _att_valid(n)
            prev = _blk(dil, r, max(n - 1, 0))
            cur = _blk(dil, r, n)
            for h in range(hp):
                kk = jnp.concatenate([k_ref[h, prev, :], k_ref[h, cur, :]], axis=0)
                vv = jnp.concatenate([v_ref[h, prev, :], v_ref[h, cur, :]], axis=0)
                s = _dot(q_ref[h, cur, :], kk, NT) * scale + bias_ref[h]
                s = jnp.where(valid, s, -1e30)
                mx = jnp.max(s, axis=-1, keepdims=True)
                e = jnp.exp(s - mx)
                den = jnp.sum(e, axis=-1, keepdims=True)
                o_ref[h, cur, :] = _dot(e / den, vv, NN)
                l_ref[h, cur, :] = jnp.broadcast_to(mx + jnp.log(den), (ATT_BLK, ATT_DH))

    osh = jax.ShapeDtypeStruct((ATT_HG, S, ATT_DH), F32)
    kw = dict(name=f"att_fwd{gi}", grid=(ATT_HG // hp, dil), in_specs=_slab_specs(gi) + [_bias_spec(gi)],
              out_specs=tuple(_head_specs(gi, 2)), out_shape=(osh, osh))
    if comm is not None:
        return _carry(body, comm, **kw)(slabs, slabs, slabs, bias)
    return _pcall(body, compiler_params=_params(("parallel", "arbitrary")), **kw)(slabs, slabs, slabs, bias)


def _att_bwd(slabs, bias, o, lse, do, dlse, gi, comm=None):
    dil, nb, hp = _att_geometry(gi)
    per = ATT_HG // hp
    scale = ATT_DH ** -0.5
    wh = hp * ATT_DH
    wide = lambda t: jnp.concatenate([t, t], axis=1)

    def body(q_ref, k_ref, v_ref, bias_ref, o_ref, l_ref, do_ref, dl_ref, dq_ref, dk_ref, dv_ref, ds_ref):
        r = pl.program_id(1)

        @pl.when(r == 0)
        def _():
            ds_ref[...] = jnp.zeros_like(ds_ref)

        for h in range(hp):
            sl = slice(h * ATT_DH, (h + 1) * ATT_DH)
            carry_k = carry_v = None
            for n in range(nb):
                valid = _att_valid(n)
                prev = _blk(dil, r, max(n - 1, 0))
                cur = _blk(dil, r, n)
                q = q_ref[h, cur, :]
                kk = jnp.concatenate([k_ref[h, prev, :], k_ref[h, cur, :]], axis=0)
                vv = jnp.concatenate([v_ref[h, prev, :], v_ref[h, cur, :]], axis=0)
                dov = do_ref[h, cur, :]
                s = _dot(q, kk, NT) * scale + bias_ref[h]
                p = jnp.where(valid, jnp.exp(s - wide(l_ref[h, cur, :])), 0.0)
                dp = _dot(dov, vv, NT)
                delta = jnp.sum(dov * o_ref[h, cur, :], axis=-1, keepdims=True)
                ds = p * (dp - delta + wide(dl_ref[h, cur, :]))
                ds_ref[h] += ds
                out_rows = pl.ds(n * ATT_BLK, ATT_BLK)
                dq_ref[out_rows, sl] = (_dot(ds, kk, NN) * scale).astype(BF16)
                dkk = _dot(ds, q, TN) * scale
                dvv = _dot(p, dov, TN)
                if n > 0:
                    before = pl.ds((n - 1) * ATT_BLK, ATT_BLK)
                    dk_ref[before, sl] = (carry_k + dkk[:ATT_BLK]).astype(BF16)
                    dv_ref[before, sl] = (carry_v + dvv[:ATT_BLK]).astype(BF16)
                carry_k, carry_v = dkk[ATT_BLK:], dvv[ATT_BLK:]
            last = pl.ds((nb - 1) * ATT_BLK, ATT_BLK)
            dk_ref[last, sl] = carry_k.astype(BF16)
            dv_ref[last, sl] = carry_v.astype(BF16)

    out_spec = pl.BlockSpec((S // dil, wh), lambda g, r: (0, r * per + g))
    osh = jax.ShapeDtypeStruct((S // dil, dil * AW), BF16)
    kw = dict(name=f"att_bwd{gi}", grid=(per, dil), in_specs=_slab_specs(gi) + [_bias_spec(gi)] + _head_specs(gi, 4),
              out_specs=(out_spec, out_spec, out_spec,
                         pl.BlockSpec((hp, ATT_BLK, 2 * ATT_BLK), lambda g, r: (g, 0, 0))),
              out_shape=(osh, osh, osh, jax.ShapeDtypeStruct((ATT_HG, ATT_BLK, 2 * ATT_BLK), F32)))
    args = (slabs, slabs, slabs, bias, o, lse, do, dlse)
    if comm is not None:
        return _carry(body, comm, **kw)(*args)
    return _pcall(body, compiler_params=_params(("arbitrary", "arbitrary")), **kw)(*args)


AW = ATT_HG * ATT_DH


def _mix_weights(l0, l1, l2):
    mx = jnp.maximum(jnp.maximum(l0, l1), l2)
    e0, e1, e2 = jnp.exp(l0 - mx), jnp.exp(l1 - mx), jnp.exp(l2 - mx)
    den = e0 + e1 + e2
    return e0 / den, e1 / den, e2 / den


def _heads_spec():
    return pl.BlockSpec((ATT_HG, TR, ATT_DH), lambda i: (0, i, 0))


def _mix_fwd(os_, ls):
    def body(o0, o1, o2, l0, l1, l2, att_ref):
        for h in range(ATT_HG):
            w0, w1, w2 = _mix_weights(l0[h], l1[h], l2[h])
            att_ref[:, h * ATT_DH:(h + 1) * ATT_DH] = (w0 * o0[h] + w1 * o1[h] + w2 * o2[h]).astype(BF16)

    return _pcall(body, name="mix_fwd", grid=(S // TR,), in_specs=[_heads_spec()] * 6, out_specs=_row_spec(AW),
                  out_shape=jax.ShapeDtypeStruct((S, AW), BF16), compiler_params=_params(("parallel",)))(*os_, *ls)


def _mix_bwd(os_, ls, datt):
    def body(o0, o1, o2, l0, l1, l2, da_ref, d0, d1, d2, e0, e1, e2):
        for h in range(ATT_HG):
            ws = _mix_weights(l0[h], l1[h], l2[h])
            da = da_ref[:, h * ATT_DH:(h + 1) * ATT_DH]
            dws = []
            for o_ref, w, d_ref in zip((o0, o1, o2), ws, (d0, d1, d2)):
                d_ref[h] = w * da
                dws.append(jnp.broadcast_to(jnp.sum(da * o_ref[h], axis=-1, keepdims=True), (TR, ATT_DH)))
            tot = ws[0] * dws[0] + ws[1] * dws[1] + ws[2] * dws[2]
            for w, dw, e_ref in zip(ws, dws, (e0, e1, e2)):
                e_ref[h] = w * (dw - tot)

    o = jax.ShapeDtypeStruct((ATT_HG, S, ATT_DH), F32)
    return _pcall(body, name="mix_bwd", grid=(S // TR,), in_specs=[_heads_spec()] * 6 + [_row_spec(AW)],
                  out_specs=(_heads_spec(),) * 6, out_shape=(o,) * 6,
                  compiler_params=_params(("parallel",)))(*os_, *ls, datt)


def _ada_fwd(c_all, w_sh, b_sl):
    def body(c_ref, w_ref, b_ref, o_ref):
        cv = c_ref[...]
        o_ref[...] = _dot(cv * jax.nn.sigmoid(cv), w_ref[...], NN) + b_ref[...]

    return _pcall(body, name="ada_fwd", out_shape=jax.ShapeDtypeStruct((N_DEV, w_sh.shape[1]), F32),
                  compiler_params=_params())(c_all, w_sh, b_sl)


def _ada_bwd(c_all, dm_sl):
    def body(c_ref, d_ref, o_ref):
        cv = c_ref[...]
        o_ref[...] = _dot(cv * jax.nn.sigmoid(cv), d_ref[...], TN)

    return _pcall(body, name="ada_bwd", out_shape=jax.ShapeDtypeStruct((D, dm_sl.shape[1]), F32),
                  compiler_params=_params())(c_all, dm_sl)


N_MOD = 6


def _sum_small(gathered):
    n = len(gathered)

    def body(*refs):
        ins, (gb_ref, dm_ref), outs = refs[:n], refs[n:n + 2], refs[n + 2:]

        def total(r):
            acc = r[0]
            for e in range(1, N_DEV):
                acc = acc + r[e]
            return acc

        for i in range(N_MOD):
            cols = slice(i * D, (i + 1) * D)
            gb_ref[:, cols] = total(ins[i])
            for e in range(N_DEV):
                dm_ref[e:e + 1, cols] = ins[i][e]
        for r, o_ref in zip(ins[N_MOD:], outs):
            o_ref[...] = total(r)

    shapes = (jax.ShapeDtypeStruct((1, N_MOD * D), F32), jax.ShapeDtypeStruct((N_DEV, N_MOD * D), F32),
              *[jax.ShapeDtypeStruct(g.shape[1:], F32) for g in gathered[N_MOD:]])
    res = _pcall(body, name="sum_small", out_shape=shapes, compiler_params=_params())(*gathered)
    return res[0], res[1], res[2:]


def _row_tile(m, n):
    t = max(8, min(m, (1 << 19) // n // 8 * 8))
    while m % t:
        t -= 8
    return t


def _pair_sum(full, recv, sel, name):
    _, _, m, n = full.shape
    t = _row_tile(m, n)

    def body(sel_ref, a_ref, b_ref, o_ref):
        o_ref[...] = (a_ref[...].astype(F32) + b_ref[...].astype(F32)).astype(o_ref.dtype)

    gs = pltpu.PrefetchScalarGridSpec(
        num_scalar_prefetch=1, grid=(4, m // t),
        in_specs=[pl.BlockSpec((None, None, t, n), lambda q, i, s: (q, s[0], i, 0)),
                  pl.BlockSpec((None, t, n), lambda q, i, s: (q, i, 0))],
        out_specs=pl.BlockSpec((None, t, n), lambda q, i, s: (q, i, 0)))
    return _pcall(body, name=name, grid_spec=gs, out_shape=jax.ShapeDtypeStruct((4, m, n), full.dtype),
                  compiler_params=_params(("parallel", "parallel")))(sel, full, recv)


def _chip_sum(part, recv, sel, name):
    _, m, n = part.shape
    t = _row_tile(m, n)

    def body(sel_ref, a_ref, r_ref, o_ref):
        o_ref[...] = ((a_ref[...].astype(F32) + r_ref[0].astype(F32)) + r_ref[1].astype(F32)) + r_ref[2].astype(F32)

    gs = pltpu.PrefetchScalarGridSpec(
        num_scalar_prefetch=1, grid=(m // t,),
        in_specs=[pl.BlockSpec((None, t, n), lambda i, s: (s[0], i, 0)),
                  pl.BlockSpec((3, t, n), lambda i, s: (0, i, 0))],
        out_specs=pl.BlockSpec((t, n), lambda i, s: (i, 0)))
    return _pcall(body, name=name, grid_spec=gs, out_shape=jax.ShapeDtypeStruct((m, n), F32),
                  compiler_params=_params(("parallel",)))(sel, part, recv)


def _adamw_math(w, g, m, v):
    nm = ADAM_B1 * m + (1.0 - ADAM_B1) * g
    nv = ADAM_B2 * v + (1.0 - ADAM_B2) * (g * g)
    m_hat = nm / (1.0 - ADAM_B1 ** ADAM_STEP)
    v_hat = nv / (1.0 - ADAM_B2 ** ADAM_STEP)
    return -ADAM_LR * (m_hat / (jnp.sqrt(v_hat) + ADAM_EPS) + ADAM_WD * w), nm, nv


def _adamw(w, g, m, v, name):
    _, rows, cols = w.shape
    t = _row_tile(rows, cols)

    def body(w_ref, g_ref, m_ref, v_ref, d_ref, nm_ref, nv_ref):
        d_ref[...], nm_ref[...], nv_ref[...] = _adamw_math(w_ref[...], g_ref[...], m_ref[...], v_ref[...])

    spec3 = pl.BlockSpec((None, t, cols), lambda i: (0, i, 0))
    spec2 = pl.BlockSpec((t, cols), lambda i: (i, 0))
    o = jax.ShapeDtypeStruct(w.shape, F32)
    return _pcall(body, name=name, grid=(rows // t,), in_specs=[spec3, spec2, spec3, spec3], out_specs=(spec3,) * 3,
                  out_shape=(o, o, o), compiler_params=_params(("parallel",)))(w, g, m, v)


def _adamw_small(ws, gs, ms, vs):
    n = len(ws)

    def body(*refs):
        for i in range(n):
            w_ref, g_ref, m_ref, v_ref = (refs[k * n + i] for k in range(4))
            d, nm, nv = _adamw_math(w_ref[...], g_ref[...], m_ref[...], v_ref[...])
            refs[4 * n + i][...] = d
            refs[5 * n + i][...] = nm
            refs[6 * n + i][...] = nv

    shapes = tuple(jax.ShapeDtypeStruct(w.shape, F32) for w in ws)
    res = _pcall(body, name="adamw_small", out_shape=shapes * 3, compiler_params=_params())(*ws, *gs, *ms, *vs)
    return res[:n], res[n:2 * n], res[2 * n:]


def _mesh_pos():
    return lax.axis_index("x"), lax.axis_index("y"), lax.axis_index("c")


class _Gather:
    def __init__(self, arrs, relay=False):
        self.relay = relay
        self.ins = list(arrs)
        na = self.na = len(arrs)
        self.out_shape = tuple(jax.ShapeDtypeStruct((N_DEV,) + a.shape, a.dtype) for a in arrs)
        self.sems = [pltpu.SemaphoreType.DMA((7 * na,)), pltpu.SemaphoreType.DMA((7 * na,)),
                     pltpu.SemaphoreType.DMA((na,))]

    def _copies(self, ins, outs, sems):
        send_sems, recv_sems, local_sems = sems
        x, y, c = _mesh_pos()
        me, sibling = (x, y, c), (x, y, 1 - c)
        chips = [(1 - x, y), (x, 1 - y), (1 - x, 1 - y)]

        def slot(p):
            return 4 * p[0] + 2 * p[1] + p[2]

        def copy(a, k, block, to, src=None):
            dst = outs[a].at[slot(block)]
            return pltpu.make_async_remote_copy(
                src_ref=dst if src is None else src, dst_ref=dst, send_sem=send_sems.at[7 * a + k],
                recv_sem=recv_sems.at[7 * a + k], device_id=to, device_id_type=MESH)

        mine = [pltpu.make_async_copy(ins[a], outs[a].at[slot(me)], local_sems.at[a]) for a in range(self.na)]
        direct = chips[:2] if self.relay else chips
        first = []
        for a in range(self.na):
            first.append(copy(a, 0, me, sibling, src=ins[a]))
            first += [copy(a, 1 + j, me, (*chip, c), src=ins[a]) for j, chip in enumerate(direct)]
        return me, sibling, chips, c, copy, mine, first

    def start(self, ins, outs, sems):
        *_, mine, first = self._copies(ins, outs, sems)
        for cp in mine + first:
            cp.start()

    def finish(self, ins, outs, sems):
        me, sibling, chips, c, copy, mine, first = self._copies(ins, outs, sems)
        x, y = me[0], me[1]
        passed = []
        for j, chip in enumerate(chips):
            for a in range(self.na):
                if self.relay and j == 2:
                    owner = ((x + 1 - c) % 2, (y + c) % 2, c)
                    cp = copy(a, 3, owner, ((x + c) % 2, (y + 1 - c) % 2, c))
                    cp.start()
                    passed.append(cp)
                copy(a, 1 + j, (*chip, c), me).wait_recv()
                cp = copy(a, 4 + j, (*chip, c), sibling)
                cp.start()
                passed.append(cp)
        for a in range(self.na):
            copy(a, 0, sibling, me).wait_recv()
            for j, chip in enumerate(chips):
                copy(a, 4 + j, (*chip, 1 - c), me).wait_recv()
        for cp in first + passed:
            cp.wait_send()
        for cp in mine:
            cp.wait()


class _ExchangeCore:
    def __init__(self, fulls):
        self.ins = list(fulls)
        self.out_shape = tuple(jax.ShapeDtypeStruct((4,) + f.shape[2:], f.dtype) for f in fulls)
        self.sems = [pltpu.SemaphoreType.DMA((4 * len(fulls),)), pltpu.SemaphoreType.DMA((4 * len(fulls),))]

    def _copies(self, ins, outs, sems):
        send_sems, recv_sems = sems
        x, y, c = _mesh_pos()
        return [pltpu.make_async_remote_copy(
            src_ref=ins[a].at[q, 1 - c], dst_ref=outs[a].at[q], send_sem=send_sems.at[4 * a + q],
            recv_sem=recv_sems.at[4 * a + q], device_id=(x, y, 1 - c), device_id_type=MESH)
            for a in range(len(self.ins)) for q in range(4)]

    def start(self, ins, outs, sems):
        for cp in self._copies(ins, outs, sems):
            cp.start()

    def finish(self, ins, outs, sems):
        for cp in self._copies(ins, outs, sems):
            cp.wait()


class _ExchangeChip:
    def __init__(self, parts):
        self.ins = list(parts)
        self.out_shape = tuple(jax.ShapeDtypeStruct((3,) + p.shape[1:], p.dtype) for p in parts)
        self.sems = [pltpu.SemaphoreType.DMA((3 * len(parts),)), pltpu.SemaphoreType.DMA((3 * len(parts),))]

    def _copies(self, ins, outs, sems):
        send_sems, recv_sems = sems
        x, y, c = _mesh_pos()
        chips = [(1 - x, y), (x, 1 - y), (1 - x, 1 - y)]
        return [pltpu.make_async_remote_copy(
            src_ref=ins[a].at[2 * px + py], dst_ref=outs[a].at[j], send_sem=send_sems.at[3 * a + j],
            recv_sem=recv_sems.at[3 * a + j], device_id=(px, py, c), device_id_type=MESH)
            for a in range(len(self.ins)) for j, (px, py) in enumerate(chips)]

    def start(self, ins, outs, sems):
        for cp in self._copies(ins, outs, sems):
            cp.start()

    def finish(self, ins, outs, sems):
        for cp in self._copies(ins, outs, sems):
            cp.wait()


HBM_ONLY = pl.BlockSpec(memory_space=pltpu.HBM)
SEM_SPEC = pl.BlockSpec(memory_space=pltpu.SEMAPHORE)
SIDE_EFFECT = pltpu.SideEffectType.DATAFLOW_SIDE_EFFECTING


def _chip_copies(p_ref, land_ref, send_sems, recv_sems):
    x, y, c = _mesh_pos()
    return [pltpu.make_async_remote_copy(
        src_ref=p_ref.at[2 * px + py], dst_ref=land_ref.at[j], send_sem=send_sems.at[j], recv_sem=recv_sems.at[j],
        device_id=(px, py, c), device_id_type=MESH) for j, (px, py) in enumerate([(1 - x, y), (x, 1 - y), (1 - x, 1 - y)])]


def _chip_exchange_start(part):
    land_shape = (3,) + part.shape[1:]

    def body(p_ref, land_ref, send_sems, recv_sems, p_thru, land_thru, token):
        for cp in _chip_copies(p_ref, land_ref, send_sems, recv_sems):
            cp.start()
        token[...] = jnp.zeros_like(token)

    return pl.pallas_call(
        body, name="rs_in_start",
        out_shape=(pltpu.SemaphoreType.DMA((3,)), pltpu.SemaphoreType.DMA((3,)), pltpu.HBM(part.shape, part.dtype),
                   pltpu.HBM(land_shape, part.dtype), jax.ShapeDtypeStruct((8, 128), F32)),
        in_specs=(HBM_ONLY, HBM_ONLY),
        out_specs=(SEM_SPEC, SEM_SPEC, HBM_ONLY, HBM_ONLY, pl.BlockSpec(memory_space=pltpu.VMEM)),
        input_output_aliases={0: 2, 1: 3}, compiler_params=pltpu.CompilerParams(has_side_effects=SIDE_EFFECT))(
        pltpu.with_memory_space_constraint(part, pltpu.HBM),
        pltpu.with_memory_space_constraint(lax.empty(land_shape, part.dtype), pltpu.HBM))


def _chip_exchange_wait(send_sems, recv_sems, part, land, after):
    def body(p_ref, land_ref, send_sems, recv_sems, after_ref, p_dead, got_ref):
        for cp in _chip_copies(p_ref, land_ref, send_sems, recv_sems):
            cp.wait_send()
            cp.wait_recv()

    return pl.pallas_call(
        body, name="rs_in_wait", out_shape=(pltpu.HBM(part.shape, part.dtype), pltpu.HBM(land.shape, land.dtype)),
        in_specs=(HBM_ONLY, HBM_ONLY, SEM_SPEC, SEM_SPEC, pl.BlockSpec(memory_space=pl.ANY)),
        out_specs=(HBM_ONLY, HBM_ONLY), input_output_aliases={0: 0, 1: 1},
        compiler_params=pltpu.CompilerParams(has_side_effects=SIDE_EFFECT))(part, land, send_sems, recv_sems, after)


def _reduce_sums(fulls, recv_core, core, tag):
    return [_pair_sum(f, r, core, f"rs_pair_{tag}{i}") for i, (f, r) in enumerate(zip(fulls, recv_core))]


def _local_step(x, tgt, mods, w_in_t, shards, small, chip, core):
    sh1, sc1, g1, sh2, sc2, g2 = mods
    norm1_g, rel_bias, gn_g, gn_b, norm2_g, norm_f_g = small
    tables = _ret_tables()
    buckets = jnp.asarray(_bucket_tables())

    h1 = _norm_mod_fwd(x, norm1_g, sh1, sc1, "norm1_fwd")
    proj, slabs, gathered = _proj(h1, w_in_t, _Gather(shards[:3]))
    w_ret_out, w_att_out, w_o = (_from_slots(g, ax) for g, ax in zip(gathered, BIG_AXES[1:4]))
    (gated, ro, states), (w_ff1_8,) = _ret_fwd(proj, tables, gn_g, gn_b, comm=_Gather(shards[3:4]))
    w_ff1 = _from_slots(w_ff1_8, BIG_AXES[4])
    bias = _bias_build(rel_bias, buckets)
    outs, lses = [], []
    for gi in range(len(ATT_GROUPS)):
        res = _att_fwd(slabs, bias, gi, comm=_Gather(shards[4:]) if gi == 2 else None)
        if gi == 2:
            res, (w_ff2_8,) = res
        outs.append(res[0])
        lses.append(res[1])
    w_ff2 = _from_slots(w_ff2_8, BIG_AXES[5])
    att = _mix_fwd(outs, lses)
    ret_out = _mm(gated, w_ret_out, 'nn', tm=S, tn=256, tk=2048, name="ret_out")
    att_out = _mm(att, w_att_out, 'nn', tm=S, tn=512, tk=AW, name="att_out")
    merged = _merge_fwd(proj, ret_out, att_out)
    mixo, x1 = _mm(merged, w_o, 'nn', tm=S, tn=256, tk=D, name="w_o", res=x, gvec=g1)
    h2 = _norm_mod_fwd(x1, norm2_g, sh2, sc2, "norm2_fwd")
    u, act = _mm(h2, w_ff1, 'nn', tm=S, tn=512, tk=D, name="ff1", relu2=True)
    f, x2 = _mm(act, w_ff2, 'nn', tm=1024, tn=512, tk=2048, name="ff2", res=x1, gvec=g2)
    loss, dx2, g_normf, df, dg2 = _final_loss(x2, tgt, norm_f_g, f, g2)

    gw_ff2 = _mm(act, df, 'tn', tm=512, tn=D, tk=S, name="gw_ff2", out_dtype=BF16)
    du = _mm(df, w_ff2, 'nt', tm=S, tn=512, tk=D, name="d_act", out_dtype=BF16, relu2_of=u)
    gw_ff1 = _mm(h2, du, 'tn', tm=D, tn=512, tk=S, name="gw_ff1", out_dtype=BF16)
    fulls_a = [_to_slots(g, ax) for g, ax in zip((gw_ff1, gw_ff2), BIG_AXES[4:])]
    dh2, recv_core_a = _mm(du, w_ff1, 'nt', tm=1024, tn=1024, tk=2048, name="dh2", comm=_ExchangeCore(fulls_a))
    parts_a = _reduce_sums(fulls_a, recv_core_a, core, "a")
    dx1, dsc2, dsh2, g_norm2, dmixo, dg1 = _norm_mod_bwd(x1, norm2_g, sc2, dh2, dx2, "norm2_bwd", gate=(mixo, g1))

    gw_o = _mm(merged, dmixo, 'tn', tm=D, tn=512, tk=S, name="gw_o", out_dtype=BF16)
    dmerged = _mm(dmixo, w_o, 'nt', tm=S, tn=512, tk=D, name="dmerged")
    d_ret_out, d_att_out, dga, dgb = _merge_bwd(proj, ret_out, att_out, dmerged)
    gw_ret_out = _mm(gated, d_ret_out, 'tn', tm=512, tn=D, tk=S, name="gw_ret_out", out_dtype=BF16)
    gw_att_out = _mm(att, d_att_out, 'tn', tm=AW, tn=D, tk=S, name="gw_att_out", out_dtype=BF16)
    fulls_b = [_to_slots(g, ax) for g, ax in zip((gw_ret_out, gw_att_out, gw_o), BIG_AXES[1:4])]
    dgated, recv_core_b = _mm(d_ret_out, w_ret_out, 'nt', tm=S, tn=512, tk=D, name="dgated",
                              comm=_ExchangeCore(fulls_b))
    parts_b = _reduce_sums(fulls_b, recv_core_b, core, "b")
    datt = _mm(d_att_out, w_att_out, 'nt', tm=S, tn=AW, tk=D, name="datt")
    mix_grads = _mix_bwd(outs, lses, datt)
    datt_parts, ds_sums = [], []
    for gi in range(len(ATT_GROUPS)):
        comm = {1: _ExchangeChip(parts_b), 2: _ExchangeChip(parts_a[:1])}.get(gi)
        res = _att_bwd(slabs, bias, outs[gi], lses[gi], mix_grads[gi], mix_grads[3 + gi], gi, comm=comm)
        if gi == 1:
            res, recv_chip_b = res
        if gi == 2:
            res, recv_chip_a1 = res
        dq, dk, dv, ds_sum = res
        datt_parts += [dq.reshape(S, AW), dk.reshape(S, AW), dv.reshape(S, AW)]
        ds_sums.append(ds_sum)
    red_b = [_chip_sum(p, r, chip, f"rs_sum_b{i}") for i, (p, r) in enumerate(zip(parts_b, recv_chip_b))]
    g_bias = _bias_grad(jnp.concatenate(ds_sums, axis=0), buckets)[:, :, 0].T.reshape(1, -1)
    (dret, g_gn_g, g_gn_b), recv_chip_a2 = _ret_bwd(proj, tables, gn_g, gn_b, ro, states, dgated,
                                                    comm=_ExchangeChip(parts_a[1:]))
    recv_chip_a = list(recv_chip_a1) + list(recv_chip_a2)
    red_a = [_chip_sum(p, r, chip, f"rs_sum_a{i}") for i, (p, r) in enumerate(zip(parts_a, recv_chip_a))]
    dproj = jnp.concatenate([dret] + datt_parts + [dga, dgb], axis=1)
    gw_in_t = _mm(dproj, h1, 'tn', tm=512, tn=D, tk=S, name="gw_in", out_dtype=BF16)
    full_in = [_to_slots(gw_in_t, 0)]
    recv_core_in = _run_comm(_ExchangeCore(full_in), "rs_core_in")
    (part_in,) = _reduce_sums(full_in, recv_core_in, core, "c")
    *in_flight, token = _chip_exchange_start(part_in)
    dh1 = _mm(dproj, w_in_t, 'nn', tm=1024, tn=1024, tk=2560, name="dh1", after=token)
    gx, dsc1, dsh1, g_norm1 = _norm_mod_bwd(x, norm1_g, sc1, dh1, dx1, "norm1_bwd")

    dmod = [dsh1, dsc1, dg1, dsh2, dsc2, dg2]
    small_g = [g_norm1, g_bias, g_gn_g, g_gn_b, g_norm2, g_normf]
    return loss, gx, in_flight, red_b + red_a, small_g, dmod


def _to_slots(g, axis):
    if axis == 0:
        return g.reshape(4, 2, g.shape[0] // N_DEV, g.shape[1])
    return g.reshape(g.shape[0], N_DEV, g.shape[1] // N_DEV).transpose(1, 0, 2).reshape(4, 2, g.shape[0], -1)


def _from_slots(w8, axis):
    if axis == 0:
        return w8.reshape(-1, w8.shape[2])
    return w8.transpose(1, 0, 2).reshape(w8.shape[1], -1)


BIG_AXES = (1, 0, 1, 0, 1, 0)


def kernel(x, c, w_ada, b_ada, norm1_g, w_in, rel_bias, ret_gn_g, ret_gn_b, w_ret_out, w_att_out, w_o, norm2_g, w_ff1, w_ff2, norm_f_g, loss_target, m_w_ada, m_b_ada, m_norm1_g, m_w_in, m_rel_bias, m_ret_gn_g, m_ret_gn_b, m_w_ret_out, m_w_att_out, m_w_o, m_norm2_g, m_w_ff1, m_w_ff2, m_norm_f_g, v_w_ada, v_b_ada, v_norm1_g, v_w_in, v_rel_bias, v_ret_gn_g, v_ret_gn_b, v_w_ret_out, v_w_att_out, v_w_o, v_norm2_g, v_w_ff1, v_w_ff2, v_norm_f_g):
    mx, my, mc = _mesh_pos()
    dev = 4 * mx + 2 * my + mc
    chip = jnp.reshape(2 * mx + my, (1,)).astype(jnp.int32)
    core = jnp.reshape(mc, (1,)).astype(jnp.int32)
    ada_w = D * 6 // N_DEV

    w_in, m_w_in, v_w_in = (jnp.transpose(t, (0, 2, 1)) for t in (w_in, m_w_in, v_w_in))

    shards = [w[0].astype(BF16) for w in (w_in, w_ret_out, w_att_out, w_o, w_ff1, w_ff2)]
    c_all, w_in8 = _run_comm(_Gather([c, shards[0]], relay=True), "gather_c_w_in")
    c_all = c_all.reshape(N_DEV, D)
    b_sl = lax.dynamic_slice(b_ada, (0, dev * ada_w), (1, ada_w))
    (mod_all,) = _run_comm(_Gather([_ada_fwd(c_all, w_ada[0], b_sl)]), "gather_mod")
    mod = lax.dynamic_index_in_dim(mod_all, dev, axis=1, keepdims=False).reshape(6, D)
    mods = tuple(mod[i:i + 1] for i in range(6))

    small = (norm1_g, rel_bias, ret_gn_g, ret_gn_b, norm2_g, norm_f_g.reshape(1, D))
    loss, gx, in_flight, big_red, small_g, dmod = _local_step(x[0], loss_target[0], mods, w_in8.reshape(IN_COLS, D),
                                                              shards[1:], small, chip, core)

    gathered = _run_comm(_Gather(dmod + small_g + [loss]), "gather_small")
    g_b_ada, dmod_all, (g_norm1, g_bias, g_gn_g, g_gn_b, g_norm2, g_normf, loss_sum) = _sum_small(gathered)
    loss_out = loss_sum[0, 0]
    g_w_ada = _ada_bwd(c_all, lax.dynamic_slice(dmod_all, (0, dev * ada_w), (N_DEV, ada_w)))

    names = ['w_ada', 'b_ada', 'norm1_g', 'w_in', 'rel_bias', 'ret_gn_g', 'ret_gn_b', 'w_ret_out', 'w_att_out',
             'w_o', 'norm2_g', 'w_ff1', 'w_ff2', 'norm_f_g']
    ws = dict(zip(names, (w_ada, b_ada, norm1_g, w_in, rel_bias, ret_gn_g, ret_gn_b, w_ret_out, w_att_out, w_o,
                          norm2_g, w_ff1, w_ff2, norm_f_g)))
    ms = dict(zip(names, (m_w_ada, m_b_ada, m_norm1_g, m_w_in, m_rel_bias, m_ret_gn_g, m_ret_gn_b, m_w_ret_out,
                          m_w_att_out, m_w_o, m_norm2_g, m_w_ff1, m_w_ff2, m_norm_f_g)))
    vs = dict(zip(names, (v_w_ada, v_b_ada, v_norm1_g, v_w_in, v_rel_bias, v_ret_gn_g, v_ret_gn_b, v_w_ret_out,
                          v_w_att_out, v_w_o, v_norm2_g, v_w_ff1, v_w_ff2, v_norm_f_g)))
    grads = dict(w_ada=g_w_ada, w_ret_out=big_red[0], w_att_out=big_red[1], w_o=big_red[2],
                 w_ff1=big_red[3], w_ff2=big_red[4], b_ada=g_b_ada, norm1_g=g_norm1, rel_bias=g_bias,
                 ret_gn_g=g_gn_g, ret_gn_b=g_gn_b, norm2_g=g_norm2, norm_f_g=g_normf)
    delta, new_m, new_v = {}, {}, {}
    for n in ('w_ada', 'w_ret_out', 'w_att_out', 'w_o', 'w_ff1', 'w_ff2'):
        delta[n], new_m[n], new_v[n] = _adamw(ws[n], grads[n], ms[n], vs[n], "adamw_" + n)
        grads[n] = grads[n].reshape(ws[n].shape)
    small_names = ('b_ada', 'norm1_g', 'rel_bias', 'ret_gn_g', 'ret_gn_b', 'norm2_g', 'norm_f_g')
    two_d = {n: (1, ws[n].size) if ws[n].ndim == 1 else ws[n].shape for n in small_names}
    d_, m_, v_ = _adamw_small(*[[src[n].reshape(two_d[n]) for n in small_names] for src in (ws, grads, ms, vs)])
    for i, n in enumerate(small_names):
        shp = ws[n].shape
        delta[n], new_m[n], new_v[n] = d_[i].reshape(shp), m_[i].reshape(shp), v_[i].reshape(shp)
        grads[n] = grads[n].reshape(shp)

    done = lax.optimization_barrier((gx, tuple(d_), tuple(delta[n] for n in ('w_ada', 'w_ret_out', 'w_att_out', 'w_o',
                                                                               'w_ff1', 'w_ff2'))))
    send_sems, recv_sems, part_in, land = in_flight
    part_in, recv_chip_in = _chip_exchange_wait(send_sems, recv_sems, part_in, land, done[0])
    grads['w_in'] = _chip_sum(part_in, recv_chip_in, chip, "rs_sum_c")
    delta['w_in'], new_m['w_in'], new_v['w_in'] = _adamw(w_in, grads['w_in'], m_w_in, v_w_in, "adamw_w_in")
    grads['w_in'] = grads['w_in'].reshape(w_in.shape)
    for d in (grads, delta, new_m, new_v):
        d['w_in'] = jnp.transpose(d['w_in'], (0, 2, 1))
    return (loss_out, gx[None], *[grads[n] for n in names], *[delta[n] for n in names],
            *[new_m[n] for n in names], *[new_v[n] for n in names])
```

```python
import functools
import math

import numpy as np
import jax
import jax.numpy as jnp
from jax import lax
from jax.experimental import pallas as pl
from jax.experimental.pallas import tpu as pltpu

F32 = jnp.float32
BF16 = jnp.bfloat16
MESH = pl.DeviceIdType.MESH

N_DEV = 8
S = 2048
D = 1024
RET_HEADS = 4
RET_DK = 256
RET_DV = 512
CHUNK = 128
N_CHUNK = S // CHUNK
ATT_GROUPS = ((128, 1), (512, 4), (2048, 16))
ATT_HG = 4
ATT_DH = 128
ATT_BLK = 128
N_BUCKETS = 32
MAX_DIST = 2048
D_FF = 4096
IN_COLS = 12800
OFF_RQ, OFF_RK, OFF_RV, OFF_RG, OFF_ATT = 0, 1024, 2048, 4096, 6144
OFF_GA, OFF_GB = 6144, 7168
RMS_EPS = 1e-6
GN_EPS = 1e-5
ADAM_LR, ADAM_B1, ADAM_B2, ADAM_EPS, ADAM_WD, ADAM_STEP = 0.001, 0.9, 0.999, 1e-08, 0.01, 10
VMEM_LIMIT = 48 * 1024 * 1024


def _pcall(body, **kw):
    return pl.pallas_call(body, **kw)


def _params(sem=None):
    return pltpu.CompilerParams(dimension_semantics=sem, vmem_limit_bytes=VMEM_LIMIT)


HBM_SPEC = pl.BlockSpec(memory_space=pl.ANY)


def _carry(body, comm, *, name, grid, in_specs, out_specs, out_shape, scratch_shapes=()):
    single = not isinstance(out_specs, (tuple, list))
    o_specs = (out_specs,) if single else tuple(out_specs)
    o_shape = (out_shape,) if single else tuple(out_shape)
    n_in, n_out, n_scr = len(in_specs), len(o_specs), len(scratch_shapes)
    nci, nco = len(comm.ins), len(comm.out_shape)
    total = int(np.prod(grid))

    def wrapped(*refs):
        bounds = np.cumsum([0, n_in, nci, n_out, nco, n_scr])
        a, ci, o, co, scr = (refs[bounds[i]:bounds[i + 1]] for i in range(5))
        sems = refs[bounds[5]:]
        flat = 0
        for d, g in enumerate(grid):
            flat = flat * g + pl.program_id(d)

        @pl.when(flat == 0)
        def _():
            comm.start(ci, co, sems)

        body(*a, *o, *scr)

        @pl.when(flat == total - 1)
        def _():
            comm.finish(ci, co, sems)

    call = _pcall(wrapped, name=name, grid=grid, in_specs=list(in_specs) + [HBM_SPEC] * nci,
                  out_specs=o_specs + (HBM_SPEC,) * nco, out_shape=o_shape + tuple(comm.out_shape),
                  scratch_shapes=list(scratch_shapes) + list(comm.sems),
                  compiler_params=_params(("arbitrary",) * len(grid)))

    def run(*args):
        res = call(*args, *comm.ins)
        own = res[0] if single else tuple(res[:n_out])
        return own, tuple(res[n_out:])

    return run


def _run_comm(comm, name):
    nci, nco = len(comm.ins), len(comm.out_shape)

    def body(*refs):
        ci, co, sems = refs[:nci], refs[nci:nci + nco], refs[nci + nco:]
        comm.start(ci, co, sems)
        comm.finish(ci, co, sems)

    return _pcall(body, name=name, in_specs=[HBM_SPEC] * nci, out_specs=(HBM_SPEC,) * nco,
                  out_shape=tuple(comm.out_shape), scratch_shapes=list(comm.sems))(*comm.ins)


def _dot(a, b, dn):
    return lax.dot_general(a.astype(BF16), b.astype(BF16), (dn, ((), ())), preferred_element_type=F32)


NN = ((1,), (0,))
NT = ((1,), (1,))
TN = ((0,), (0,))


def _mm(a, b, mode, *, tm, tn, tk, name, out_dtype=F32, res=None, gvec=None, relu2=False, relu2_of=None, comm=None,
        after=None):
    if mode == 'nn':
        (M, K), (_, N) = a.shape, b.shape
        a_spec = pl.BlockSpec((tm, tk), lambda i, j, k: (i, k))
        b_spec = pl.BlockSpec((tk, tn), lambda i, j, k: (k, j))
        dn = NN
    elif mode == 'nt':
        (M, K), (N, _) = a.shape, b.shape
        a_spec = pl.BlockSpec((tm, tk), lambda i, j, k: (i, k))
        b_spec = pl.BlockSpec((tn, tk), lambda i, j, k: (j, k))
        dn = NT
    else:
        (K, M), (_, N) = a.shape, b.shape
        a_spec = pl.BlockSpec((tk, tm), lambda i, j, k: (k, i))
        b_spec = pl.BlockSpec((tk, tn), lambda i, j, k: (k, j))
        dn = TN
    assert M % tm == 0 and N % tn == 0 and K % tk == 0, (name, M, N, K)
    nk = K // tk
    fused = res is not None
    o_spec = pl.BlockSpec((tm, tn), lambda i, j, k: (i, j))

    def body(a_ref, b_ref, *rest):
        acc_ref = rest[-1] if nk > 1 else None
        if after is not None:
            rest = rest[1:]
        if fused:
            res_ref, g_ref, o_ref, x_ref = rest[:4]
        elif relu2_of is not None:
            u_ref, o_ref = rest[:2]
        elif relu2:
            o_ref, act_ref = rest[:2]
        else:
            o_ref = rest[0]

        def finish(acc):
            if relu2_of is not None:
                acc = acc * (2.0 * jnp.maximum(u_ref[...], 0.0))
            o_ref[...] = acc.astype(o_ref.dtype)
            if fused:
                x_ref[...] = res_ref[...] + g_ref[...] * acc
            if relu2:
                r = jnp.maximum(acc, 0.0)
                act_ref[...] = (r * r).astype(BF16)

        p = _dot(a_ref[...], b_ref[...], dn)
        if nk == 1:
            finish(p)
        else:
            k = pl.program_id(2)

            @pl.when(k == 0)
            def _():
                acc_ref[...] = p

            @pl.when(k > 0)
            def _():
                acc_ref[...] += p

            @pl.when(k == nk - 1)
            def _():
                finish(acc_ref[...])

    in_specs = [a_spec, b_spec]
    args = [a, b]
    if after is not None:
        in_specs.append(pl.BlockSpec(memory_space=pl.ANY))
        args.append(after)
    out_shape = jax.ShapeDtypeStruct((M, N), out_dtype)
    out_specs = o_spec
    if fused:
        in_specs += [pl.BlockSpec((tm, tn), lambda i, j, k: (i, j)), pl.BlockSpec((1, tn), lambda i, j, k: (0, j))]
        args += [res, gvec]
        out_shape = (out_shape, jax.ShapeDtypeStruct((M, N), F32))
        out_specs = (o_spec, pl.BlockSpec((tm, tn), lambda i, j, k: (i, j)))
    elif relu2_of is not None:
        in_specs.append(pl.BlockSpec((tm, tn), lambda i, j, k: (i, j)))
        args.append(relu2_of)
    elif relu2:
        out_shape = (out_shape, jax.ShapeDtypeStruct((M, N), BF16))
        out_specs = (o_spec, pl.BlockSpec((tm, tn), lambda i, j, k: (i, j)))
    kw = dict(name=name, grid=(M // tm, N // tn, nk), in_specs=in_specs, out_specs=out_specs,
              out_shape=out_shape, scratch_shapes=[pltpu.VMEM((tm, tn), F32)] if nk > 1 else [])
    if comm is not None:
        return _carry(body, comm, **kw)(*args)
    return _pcall(body, compiler_params=_params(("parallel", "parallel", "arbitrary")), **kw)(*args)


PROJ_TN = 512
ATT_T0, ATT_T1 = 6144 // PROJ_TN, 10752 // PROJ_TN
N_SLABS = (ATT_T1 - ATT_T0) * 4
MAIN_COLS = IN_COLS - (ATT_T1 - ATT_T0) * PROJ_TN


def _proj(h1, w_in_t, comm):
    nj = IN_COLS // PROJ_TN

    def body(a_ref, b_ref, main_ref, slab_ref):
        j = pl.program_id(1)
        is_att = (j >= ATT_T0) & (j < ATT_T1)
        chunks = [pl.ds(c * 512, 512) for c in range(S // 512)]

        @pl.when(jnp.logical_not(is_att))
        def _():
            for rows in chunks:
                main_ref[rows, :] = _dot(a_ref[rows, :], b_ref[...], NT)

        @pl.when(is_att)
        def _():
            for rows in chunks:
                p = _dot(a_ref[rows, :], b_ref[...], NT)
                for h in range(4):
                    slab_ref[h, rows, :] = p[:, h * 128:(h + 1) * 128]

    main_idx = lambda j: jnp.where(j < ATT_T0, j, jnp.where(j < ATT_T1, ATT_T0 - 1, j - (ATT_T1 - ATT_T0)))
    slab_idx = lambda j: jnp.clip(j - ATT_T0, 0, ATT_T1 - ATT_T0 - 1)
    (main, slabs), got = _carry(
        body, comm, name="proj", grid=(1, nj, 1),
        in_specs=[pl.BlockSpec((S, D), lambda i, j, k: (0, 0)), pl.BlockSpec((PROJ_TN, D), lambda i, j, k: (j, 0))],
        out_specs=(pl.BlockSpec((S, PROJ_TN), lambda i, j, k: (0, main_idx(j))),
                   pl.BlockSpec((4, S, 128), lambda i, j, k: (slab_idx(j), 0, 0))),
        out_shape=(jax.ShapeDtypeStruct((S, MAIN_COLS), F32), jax.ShapeDtypeStruct((N_SLABS, S, 128), F32)))(h1, w_in_t)
    return main, slabs, got


TR = 256


def _row_spec(w=D):
    return pl.BlockSpec((TR, w), lambda i: (i, 0))


def _vec_spec(w=D):
    return pl.BlockSpec((1, w), lambda i: (0, 0))


def _norm_mod_fwd(x, g, sh, sc, name):
    def body(x_ref, g_ref, sh_ref, sc_ref, o_ref):
        xv = x_ref[...]
        rstd = lax.rsqrt(jnp.mean(xv * xv, axis=-1, keepdims=True) + RMS_EPS)
        n = xv * rstd * g_ref[...]
        o_ref[...] = (n * (1.0 + sc_ref[...]) + sh_ref[...]).astype(BF16)

    return _pcall(body, name=name, grid=(S // TR,), in_specs=[_row_spec(), _vec_spec(), _vec_spec(), _vec_spec()],
                  out_specs=_row_spec(), out_shape=jax.ShapeDtypeStruct((S, D), BF16),
                  compiler_params=_params(("parallel",)))(x, g, sh, sc)


def _norm_mod_bwd(x, g, sc, dh, dres, name, gate=None):
    gated = gate is not None

    def body(x_ref, g_ref, sc_ref, dh_ref, dres_ref, *rest):
        if gated:
            f_ref, gv_ref, dx_ref, dsc_ref, dsh_ref, dg_ref, dz_ref, dgv_ref = rest
        else:
            dx_ref, dsc_ref, dsh_ref, dg_ref = rest
        i = pl.program_id(0)
        xv = x_ref[...]
        dh = dh_ref[...]
        rstd = lax.rsqrt(jnp.mean(xv * xv, axis=-1, keepdims=True) + RMS_EPS)
        xhat = xv * rstd
        gv = g_ref[...]
        dn = dh * (1.0 + sc_ref[...])
        dxhat = dn * gv
        dx = dres_ref[...] + rstd * (dxhat - xhat * jnp.mean(dxhat * xhat, axis=-1, keepdims=True))
        dx_ref[...] = dx
        sums = [(dsc_ref, jnp.sum(dh * (xhat * gv), axis=0, keepdims=True)),
                (dsh_ref, jnp.sum(dh, axis=0, keepdims=True)),
                (dg_ref, jnp.sum(dn * xhat, axis=0, keepdims=True))]
        if gated:
            dz_ref[...] = (dx * gv_ref[...]).astype(BF16)
            sums.append((dgv_ref, jnp.sum(dx * f_ref[...], axis=0, keepdims=True)))

        @pl.when(i == 0)
        def _():
            for ref, p in sums:
                ref[...] = p

        @pl.when(i > 0)
        def _():
            for ref, p in sums:
                ref[...] += p

    vec = jax.ShapeDtypeStruct((1, D), F32)
    in_specs = [_row_spec(), _vec_spec(), _vec_spec(), _row_spec(), _row_spec()]
    out_specs = [_row_spec(), _vec_spec(), _vec_spec(), _vec_spec()]
    out_shape = [jax.ShapeDtypeStruct((S, D), F32), vec, vec, vec]
    args = [x, g, sc, dh, dres]
    if gated:
        in_specs += [_row_spec(), _vec_spec()]
        out_specs += [_row_spec(), _vec_spec()]
        out_shape += [jax.ShapeDtypeStruct((S, D), BF16), vec]
        args += list(gate)
    return _pcall(body, name=name, grid=(S // TR,), in_specs=in_specs, out_specs=tuple(out_specs),
                  out_shape=tuple(out_shape), compiler_params=_params(("arbitrary",)))(*args)


def _final_loss(x2, tgt, g, f, g2):
    def body(x_ref, t_ref, g_ref, f_ref, g2_ref, loss_ref, dx_ref, dg_ref, df_ref, dg2_ref):
        i = pl.program_id(0)
        xv = x_ref[...]
        gv = g_ref[...]
        rstd = lax.rsqrt(jnp.mean(xv * xv, axis=-1, keepdims=True) + RMS_EPS)
        xhat = xv * rstd
        err = xhat * gv - t_ref[...]
        dy = err * (1.0 / D)
        dxhat = dy * gv
        dx = rstd * (dxhat - xhat * jnp.mean(dxhat * xhat, axis=-1, keepdims=True))
        dx_ref[...] = dx
        df_ref[...] = (dx * g2_ref[...]).astype(BF16)
        p_g = jnp.sum(dy * xhat, axis=0, keepdims=True)
        p_g2 = jnp.sum(dx * f_ref[...], axis=0, keepdims=True)
        p_l = jnp.zeros((1, 128), F32) + 0.5 * jnp.sum(jnp.mean(err * err, axis=-1, keepdims=True))

        @pl.when(i == 0)
        def _():
            dg_ref[...] = p_g
            dg2_ref[...] = p_g2
            loss_ref[...] = p_l

        @pl.when(i > 0)
        def _():
            dg_ref[...] += p_g
            dg2_ref[...] += p_g2
            loss_ref[...] += p_l

    vec = jax.ShapeDtypeStruct((1, D), F32)
    return _pcall(body, name="final_loss", grid=(S // TR,),
                  in_specs=[_row_spec(), _row_spec(), _vec_spec(), _row_spec(), _vec_spec()],
                  out_specs=(_vec_spec(128), _row_spec(), _vec_spec(), _row_spec(), _vec_spec()),
                  out_shape=(jax.ShapeDtypeStruct((1, 128), F32), jax.ShapeDtypeStruct((S, D), F32), vec,
                             jax.ShapeDtypeStruct((S, D), BF16), vec),
                  compiler_params=_params(("arbitrary",)))(x2, tgt, g, f, g2)


HALF = 512


def _merge_fwd(proj, ret_out, att_out):
    def body(ga_ref, gb_ref, r_ref, a_ref, o_ref):
        o_ref[...] = (jax.nn.sigmoid(ga_ref[...]) * r_ref[...] + jax.nn.sigmoid(gb_ref[...]) * a_ref[...]).astype(BF16)

    blk = lambda off: pl.BlockSpec((TR, HALF), lambda i, j: (i, off // HALF + j))
    return _pcall(body, name="merge_fwd", grid=(S // TR, D // HALF),
                  in_specs=[blk(OFF_GA), blk(OFF_GB), blk(0), blk(0)], out_specs=blk(0),
                  out_shape=jax.ShapeDtypeStruct((S, D), BF16),
                  compiler_params=_params(("parallel", "parallel")))(proj, proj, ret_out, att_out)


def _merge_bwd(proj, ret_out, att_out, dmerged):
    def body(ga_ref, gb_ref, r_ref, a_ref, dm_ref, dr_ref, da_ref, dga_ref, dgb_ref):
        sa = jax.nn.sigmoid(ga_ref[...])
        sb = jax.nn.sigmoid(gb_ref[...])
        dm = dm_ref[...]
        dr_ref[...] = (dm * sa).astype(BF16)
        da_ref[...] = (dm * sb).astype(BF16)
        dga_ref[...] = (dm * r_ref[...] * (sa * (1.0 - sa))).astype(BF16)
        dgb_ref[...] = (dm * a_ref[...] * (sb * (1.0 - sb))).astype(BF16)

    blk = lambda off: pl.BlockSpec((TR, HALF), lambda i, j: (i, off // HALF + j))
    o = jax.ShapeDtypeStruct((S, D), BF16)
    return _pcall(body, name="merge_bwd", grid=(S // TR, D // HALF),
                  in_specs=[blk(OFF_GA), blk(OFF_GB), blk(0), blk(0), blk(0)], out_specs=(blk(0),) * 4,
                  out_shape=(o, o, o, o),
                  compiler_params=_params(("parallel", "parallel")))(proj, proj, ret_out, att_out, dmerged)


def _ret_tables():
    H, C = RET_HEADS, CHUNK
    log_g = jnp.log1p(-(2.0 ** (-5.0 - jnp.arange(H, dtype=F32))))
    idx = jnp.arange(C, dtype=F32)
    rel = idx[:, None] - idx[None, :]
    inner = jnp.where(rel >= 0, jnp.exp(log_g[:, None, None] * jnp.maximum(rel, 0.0)), 0.0)
    qd = jnp.exp(log_g[:, None] * (idx + 1.0))[:, :, None]
    kd = jnp.exp(log_g[:, None] * (C - 1.0 - idx))[:, :, None]
    cd = jnp.broadcast_to(jnp.exp(log_g * C)[:, None, None], (H, 1, 128))
    half = RET_DK // 2
    inv = 10000.0 ** (-jnp.arange(half, dtype=F32) / half)
    ang = jnp.arange(S, dtype=F32)[:, None] * inv[None, :]
    return inner, qd, kd, cd, jnp.cos(ang), jnp.sin(ang)


def _rot(x, cos, sin):
    x1, x2 = x[:, :128], x[:, 128:]
    return jnp.concatenate([x1 * cos - x2 * sin, x1 * sin + x2 * cos], axis=1)


def _rot_t(d, cos, sin):
    d1, d2 = d[:, :128], d[:, 128:]
    return jnp.concatenate([d1 * cos + d2 * sin, d2 * cos - d1 * sin], axis=1)


RET_COLS = OFF_ATT
RET_VW = RET_HEADS * RET_DV


def _ret_specs(chunk_of):
    ci = chunk_of
    whole = lambda shape: pl.BlockSpec(shape, lambda t: (0,) * len(shape))
    return [
        pl.BlockSpec((CHUNK, RET_COLS), lambda t: (ci(t), 0)),
        pl.BlockSpec((CHUNK, 128), lambda t: (ci(t), 0)),
        pl.BlockSpec((CHUNK, 128), lambda t: (ci(t), 0)),
        whole((RET_HEADS, CHUNK, CHUNK)), whole((RET_HEADS, CHUNK, 1)), whole((RET_HEADS, CHUNK, 1)),
        whole((RET_HEADS, 1, 128)), whole((1, RET_VW)), whole((1, RET_VW)),
    ]


def _ret_cols(h):
    q = slice(OFF_RQ + h * RET_DK, OFF_RQ + (h + 1) * RET_DK)
    k = slice(OFF_RK + h * RET_DK, OFF_RK + (h + 1) * RET_DK)
    v = slice(OFF_RV + h * RET_DV, OFF_RV + (h + 1) * RET_DV)
    g = slice(OFF_RG + h * RET_DV, OFF_RG + (h + 1) * RET_DV)
    return q, k, v, g, slice(h * RET_DV, (h + 1) * RET_DV)


def _ret_fwd(proj, tables, gn_g, gn_b, comm=None):
    inner, qd, kd, cd, cos, sin = tables

    def body(x_ref, cos_ref, sin_ref, in_ref, qd_ref, kd_ref, cd_ref, g_ref, b_ref,
             gated_ref, ro_ref, st_ref, s_scr):
        i = pl.program_id(0)

        @pl.when(i == 0)
        def _():
            s_scr[...] = jnp.zeros_like(s_scr)

        cosv, sinv = cos_ref[...], sin_ref[...]
        for h in range(RET_HEADS):
            cq, ck, cv, cg, co = _ret_cols(h)
            q = _rot(x_ref[:, cq], cosv, sinv)
            k = _rot(x_ref[:, ck], cosv, sinv) * (RET_DK ** -0.5)
            v = x_ref[:, cv]
            st = s_scr[h]
            st_ref[h] = st
            s = _dot(q, k, NT) * in_ref[h]
            o = _dot(s, v, NN) + _dot(q, st, NN) * qd_ref[h]
            s_scr[h] = st * cd_ref[h, :, :1] + _dot(k * kd_ref[h], v, TN)
            ro_ref[:, co] = o
            mu = jnp.mean(o, axis=-1, keepdims=True)
            oc = o - mu
            var = jnp.mean(oc * oc, axis=-1, keepdims=True)
            rn = oc * lax.rsqrt(var + GN_EPS) * g_ref[:, co] + b_ref[:, co]
            rg = x_ref[:, cg]
            gated_ref[:, co] = (rg * jax.nn.sigmoid(rg) * rn).astype(BF16)

    ospec = pl.BlockSpec((CHUNK, RET_VW), lambda t: (t, 0))
    kw = dict(name="ret_fwd", grid=(N_CHUNK,), in_specs=_ret_specs(lambda t: t),
              out_specs=(ospec, ospec, pl.BlockSpec((RET_HEADS, None, RET_DK, RET_DV), lambda t: (0, t, 0, 0))),
              out_shape=(jax.ShapeDtypeStruct((S, RET_VW), BF16), jax.ShapeDtypeStruct((S, RET_VW), F32),
                         jax.ShapeDtypeStruct((RET_HEADS, N_CHUNK, RET_DK, RET_DV), F32)),
              scratch_shapes=[pltpu.VMEM((RET_HEADS, RET_DK, RET_DV), F32)])
    args = (proj, cos, sin, inner, qd, kd, cd, gn_g, gn_b)
    if comm is not None:
        return _carry(body, comm, **kw)(*args)
    return _pcall(body, compiler_params=_params(("arbitrary",)), **kw)(*args)


def _ret_bwd(proj, tables, gn_g, gn_b, ro, states, dgated, comm=None):
    inner, qd, kd, cd, cos, sin = tables
    last = N_CHUNK - 1

    def body(x_ref, cos_ref, sin_ref, in_ref, qd_ref, kd_ref, cd_ref, g_ref, b_ref, ro_ref, st_ref, dg_ref,
             dx_ref, gg_ref, gb_ref, gs_scr):
        t = pl.program_id(0)

        @pl.when(t == 0)
        def _():
            gs_scr[...] = jnp.zeros_like(gs_scr)
            gg_ref[...] = jnp.zeros_like(gg_ref)
            gb_ref[...] = jnp.zeros_like(gb_ref)

        cosv, sinv = cos_ref[...], sin_ref[...]
        for h in range(RET_HEADS):
            cq, ck, cv, cg, co = _ret_cols(h)
            q = _rot(x_ref[:, cq], cosv, sinv)
            k = _rot(x_ref[:, ck], cosv, sinv) * (RET_DK ** -0.5)
            v = x_ref[:, cv]
            qdv, kdv, dm = qd_ref[h], kd_ref[h], in_ref[h]
            st = st_ref[h]
            o = ro_ref[:, co]
            gv = g_ref[:, co]
            mu = jnp.mean(o, axis=-1, keepdims=True)
            oc = o - mu
            rstd = lax.rsqrt(jnp.mean(oc * oc, axis=-1, keepdims=True) + GN_EPS)
            ohat = oc * rstd
            rn = ohat * gv + b_ref[:, co]
            rg = x_ref[:, cg]
            sg = jax.nn.sigmoid(rg)
            dgt = dg_ref[:, co]
            drn = dgt * (rg * sg)
            dx_ref[:, cg] = (dgt * rn * (sg * (1.0 + rg * (1.0 - sg)))).astype(BF16)
            gg_ref[:, co] += jnp.sum(drn * ohat, axis=0, keepdims=True)
            gb_ref[:, co] += jnp.sum(drn, axis=0, keepdims=True)
            dohat = drn * gv
            do = rstd * (dohat - jnp.mean(dohat, axis=-1, keepdims=True)
                         - ohat * jnp.mean(dohat * ohat, axis=-1, keepdims=True))
            gs = gs_scr[h]
            s = _dot(q, k, NT) * dm
            dsr = _dot(do, v, NT) * dm
            dq = _dot(dsr, k, NN) + _dot(do, st, NT) * qdv
            dk = _dot(dsr, q, TN) + _dot(v, gs, NT) * kdv
            dv = _dot(s, do, TN) + _dot(k * kdv, gs, NN)
            gs_scr[h] = gs * cd_ref[h, :, :1] + _dot(q * qdv, do, TN)
            dx_ref[:, cq] = _rot_t(dq, cosv, sinv).astype(BF16)
            dx_ref[:, ck] = (_rot_t(dk, cosv, sinv) * (RET_DK ** -0.5)).astype(BF16)
            dx_ref[:, cv] = dv.astype(BF16)

    rev = lambda t: last - t
    vblk = pl.BlockSpec((CHUNK, RET_VW), lambda t: (rev(t), 0))
    vspec = pl.BlockSpec((1, RET_VW), lambda t: (0, 0))
    kw = dict(name="ret_bwd", grid=(N_CHUNK,),
              in_specs=_ret_specs(rev) + [vblk, pl.BlockSpec((RET_HEADS, None, RET_DK, RET_DV),
                                                             lambda t: (0, rev(t), 0, 0)), vblk],
              out_specs=(pl.BlockSpec((CHUNK, RET_COLS), lambda t: (rev(t), 0)), vspec, vspec),
              out_shape=(jax.ShapeDtypeStruct((S, RET_COLS), BF16), jax.ShapeDtypeStruct((1, RET_VW), F32),
                         jax.ShapeDtypeStruct((1, RET_VW), F32)),
              scratch_shapes=[pltpu.VMEM((RET_HEADS, RET_DK, RET_DV), F32)])
    args = (proj, cos, sin, inner, qd, kd, cd, gn_g, gn_b, ro, states, dgated)
    if comm is not None:
        return _carry(body, comm, **kw)(*args)
    return _pcall(body, compiler_params=_params(("arbitrary",)), **kw)(*args)


def _bucket_tables():
    qi = np.arange(ATT_BLK)[:, None]
    kj = np.arange(2 * ATT_BLK)[None, :]
    m = ATT_BLK + qi - kj
    out = []
    for win, dil in ATT_GROUPS:
        w = win // dil
        dist = (np.clip(m, 0, w) * dil).astype(np.int32)
        max_exact = N_BUCKETS // 2
        d_f = np.maximum(dist, 1).astype(np.float32)
        large = max_exact + (np.log(d_f / np.float32(max_exact)) / np.float32(math.log(MAX_DIST / max_exact))
                             * np.float32(N_BUCKETS - max_exact)).astype(np.int32)
        large = np.minimum(large, N_BUCKETS - 1)
        out.append(np.where(dist < max_exact, dist, large).astype(np.int32))
    return np.stack(out)


def _bias_build(rel_bias, buckets):
    def body(tab_ref, bk_ref, o_ref):
        hh = pl.program_id(0)
        bk = bk_ref[...]
        acc = jnp.zeros((ATT_BLK, 2 * ATT_BLK), F32)
        for b in range(N_BUCKETS):
            acc = jnp.where(bk == b, tab_ref[b, hh], acc)
        o_ref[...] = acc

    nh = len(ATT_GROUPS) * ATT_HG
    return _pcall(body, name="bias_build", grid=(nh,),
                  in_specs=[pl.BlockSpec(memory_space=pltpu.SMEM),
                            pl.BlockSpec((None, ATT_BLK, 2 * ATT_BLK), lambda hh: (hh // ATT_HG, 0, 0))],
                  out_specs=pl.BlockSpec((None, ATT_BLK, 2 * ATT_BLK), lambda hh: (hh, 0, 0)),
                  out_shape=jax.ShapeDtypeStruct((nh, ATT_BLK, 2 * ATT_BLK), F32),
                  compiler_params=_params(("parallel",)))(rel_bias, buckets)


def _bias_grad(ds_sum, buckets):
    def body(ds_ref, bk_ref, o_ref):
        bk = bk_ref[...]
        ds = ds_ref[...]
        rows = lax.broadcasted_iota(jnp.int32, (N_BUCKETS, 128), 0)
        acc = jnp.zeros((N_BUCKETS, 128), F32)
        for b in range(N_BUCKETS):
            acc = jnp.where(rows == b, jnp.sum(jnp.where(bk == b, ds, 0.0)), acc)
        o_ref[...] = acc

    nh = len(ATT_GROUPS) * ATT_HG
    return _pcall(body, name="bias_grad", grid=(nh,),
                  in_specs=[pl.BlockSpec((None, ATT_BLK, 2 * ATT_BLK), lambda hh: (hh, 0, 0)),
                            pl.BlockSpec((None, ATT_BLK, 2 * ATT_BLK), lambda hh: (hh // ATT_HG, 0, 0))],
                  out_specs=pl.BlockSpec((None, N_BUCKETS, 128), lambda hh: (hh, 0, 0)),
                  out_shape=jax.ShapeDtypeStruct((nh, N_BUCKETS, 128), F32),
                  compiler_params=_params(("parallel",)))(ds_sum, buckets)


def _att_valid(n):
    qi = lax.broadcasted_iota(jnp.int32, (ATT_BLK, 2 * ATT_BLK), 0)
    kj = lax.broadcasted_iota(jnp.int32, (ATT_BLK, 2 * ATT_BLK), 1)
    m = ATT_BLK + qi - kj
    first_key = jnp.where(n > 0, 0, ATT_BLK)
    return (m >= 0) & (m <= ATT_BLK) & (kj >= first_key)


ATT_HP = (1, 2, 2)


def _att_geometry(gi):
    _, dil = ATT_GROUPS[gi]
    return dil, S // dil // ATT_BLK, ATT_HP[gi]


def _blk(dil, r, n):
    if dil == 1:
        return pl.ds(n * ATT_BLK, ATT_BLK)
    return pl.ds(r + n * ATT_BLK * dil, ATT_BLK, stride=dil)


def _slab_specs(gi):
    _, _, hp = _att_geometry(gi)
    per = ATT_HG // hp
    return [pl.BlockSpec((hp, S, ATT_DH), lambda g, r, part=part: ((3 * gi + part) * per + g, 0, 0))
            for part in range(3)]


def _head_specs(gi, count):
    _, _, hp = _att_geometry(gi)
    return [pl.BlockSpec((hp, S, ATT_DH), lambda g, r: (g, 0, 0))] * count


def _bias_spec(gi):
    _, _, hp = _att_geometry(gi)
    return pl.BlockSpec((hp, ATT_BLK, 2 * ATT_BLK), lambda g, r: (gi * (ATT_HG // hp) + g, 0, 0))


def _att_fwd(slabs, bias, gi, comm=None):
    dil, nb, hp = _att_geometry(gi)
    scale = ATT_DH ** -0.5

    def body(q_ref, k_ref, v_ref, bias_ref, o_ref, l_ref):
        r = pl.program_id(1)
        for n in range(nb):
            valid = _att_valid(n)
            prev = _blk(dil, r, max(n - 1, 0))
            cur = _blk(dil, r, n)
            for h in range(hp):
                kk = jnp.concatenate([k_ref[h, prev, :], k_ref[h, cur, :]], axis=0)
                vv = jnp.concatenate([v_ref[h, prev, :], v_ref[h, cur, :]], axis=0)
                s = _dot(q_ref[h, cur, :], kk, NT) * scale + bias_ref[h]
                s = jnp.where(valid, s, -1e30)
                mx = jnp.max(s, axis=-1, keepdims=True)
                e = jnp.exp(s - mx)
                den = jnp.sum(e, axis=-1, keepdims=True)
                o_ref[h, cur, :] = _dot(e / den, vv, NN)
                l_ref[h, cur, :] = jnp.broadcast_to(mx + jnp.log(den), (ATT_BLK, ATT_DH))

    osh = jax.ShapeDtypeStruct((ATT_HG, S, ATT_DH), F32)
    kw = dict(name=f"att_fwd{gi}", grid=(ATT_HG // hp, dil), in_specs=_slab_specs(gi) + [_bias_spec(gi)],
              out_specs=tuple(_head_specs(gi, 2)), out_shape=(osh, osh))
    if comm is not None:
        return _carry(body, comm, **kw)(slabs, slabs, slabs, bias)
    return _pcall(body, compiler_params=_params(("parallel", "arbitrary")), **kw)(slabs, slabs, slabs, bias)


def _att_bwd(slabs, bias, o, lse, do, dlse, gi, comm=None):
    dil, nb, hp = _att_geometry(gi)
    per = ATT_HG // hp
    scale = ATT_DH ** -0.5
    wh = hp * ATT_DH
    wide = lambda t: jnp.concatenate([t, t], axis=1)

    def body(q_ref, k_ref, v_ref, bias_ref, o_ref, l_ref, do_ref, dl_ref, dq_ref, dk_ref, dv_ref, ds_ref):
        r = pl.program_id(1)

        @pl.when(r == 0)
        def _():
            ds_ref[...] = jnp.zeros_like(ds_ref)

        for h in range(hp):
            sl = slice(h * ATT_DH, (h + 1) * ATT_DH)
            carry_k = carry_v = None
            for n in range(nb):
                valid = _att_valid(n)
                prev = _blk(dil, r, max(n - 1, 0))
                cur = _blk(dil, r, n)
                q = q_ref[h, cur, :]
                kk = jnp.concatenate([k_ref[h, prev, :], k_ref[h, cur, :]], axis=0)
                vv = jnp.concatenate([v_ref[h, prev, :], v_ref[h, cur, :]], axis=0)
                dov = do_ref[h, cur, :]
                s = _dot(q, kk, NT) * scale + bias_ref[h]
                p = jnp.where(valid, jnp.exp(s - wide(l_ref[h, cur, :])), 0.0)
                dp = _dot(dov, vv, NT)
                delta = jnp.sum(dov * o_ref[h, cur, :], axis=-1, keepdims=True)
                ds = p * (dp - delta + wide(dl_ref[h, cur, :]))
                ds_ref[h] += ds
                out_rows = pl.ds(n * ATT_BLK, ATT_BLK)
                dq_ref[out_rows, sl] = (_dot(ds, kk, NN) * scale).astype(BF16)
                dkk = _dot(ds, q, TN) * scale
                dvv = _dot(p, dov, TN)
                if n > 0:
                    before = pl.ds((n - 1) * ATT_BLK, ATT_BLK)
                    dk_ref[before, sl] = (carry_k + dkk[:ATT_BLK]).astype(BF16)
                    dv_ref[before, sl] = (carry_v + dvv[:ATT_BLK]).astype(BF16)
                carry_k, carry_v = dkk[ATT_BLK:], dvv[ATT_BLK:]
            last = pl.ds((nb - 1) * ATT_BLK, ATT_BLK)
            dk_ref[last, sl] = carry_k.astype(BF16)
            dv_ref[last, sl] = carry_v.astype(BF16)

    out_spec = pl.BlockSpec((S // dil, wh), lambda g, r: (0, r * per + g))
    osh = jax.ShapeDtypeStruct((S // dil, dil * AW), BF16)
    kw = dict(name=f"att_bwd{gi}", grid=(per, dil), in_specs=_slab_specs(gi) + [_bias_spec(gi)] + _head_specs(gi, 4),
              out_specs=(out_spec, out_spec, out_spec,
                         pl.BlockSpec((hp, ATT_BLK, 2 * ATT_BLK), lambda g, r: (g, 0, 0))),
              out_shape=(osh, osh, osh, jax.ShapeDtypeStruct((ATT_HG, ATT_BLK, 2 * ATT_BLK), F32)))
    args = (slabs, slabs, slabs, bias, o, lse, do, dlse)
    if comm is not None:
        return _carry(body, comm, **kw)(*args)
    return _pcall(body, compiler_params=_params(("arbitrary", "arbitrary")), **kw)(*args)


AW = ATT_HG * ATT_DH


def _mix_weights(l0, l1, l2):
    mx = jnp.maximum(jnp.maximum(l0, l1), l2)
    e0, e1, e2 = jnp.exp(l0 - mx), jnp.exp(l1 - mx), jnp.exp(l2 - mx)
    den = e0 + e1 + e2
    return e0 / den, e1 / den, e2 / den


def _heads_spec():
    return pl.BlockSpec((ATT_HG, TR, ATT_DH), lambda i: (0, i, 0))


def _mix_fwd(os_, ls):
    def body(o0, o1, o2, l0, l1, l2, att_ref):
        for h in range(ATT_HG):
            w0, w1, w2 = _mix_weights(l0[h], l1[h], l2[h])
            att_ref[:, h * ATT_DH:(h + 1) * ATT_DH] = (w0 * o0[h] + w1 * o1[h] + w2 * o2[h]).astype(BF16)

    return _pcall(body, name="mix_fwd", grid=(S // TR,), in_specs=[_heads_spec()] * 6, out_specs=_row_spec(AW),
                  out_shape=jax.ShapeDtypeStruct((S, AW), BF16), compiler_params=_params(("parallel",)))(*os_, *ls)


def _mix_bwd(os_, ls, datt):
    def body(o0, o1, o2, l0, l1, l2, da_ref, d0, d1, d2, e0, e1, e2):
        for h in range(ATT_HG):
            ws = _mix_weights(l0[h], l1[h], l2[h])
            da = da_ref[:, h * ATT_DH:(h + 1) * ATT_DH]
            dws = []
            for o_ref, w, d_ref in zip((o0, o1, o2), ws, (d0, d1, d2)):
                d_ref[h] = w * da
                dws.append(jnp.broadcast_to(jnp.sum(da * o_ref[h], axis=-1, keepdims=True), (TR, ATT_DH)))
            tot = ws[0] * dws[0] + ws[1] * dws[1] + ws[2] * dws[2]
            for w, dw, e_ref in zip(ws, dws, (e0, e1, e2)):
                e_ref[h] = w * (dw - tot)

    o = jax.ShapeDtypeStruct((ATT_HG, S, ATT_DH), F32)
    return _pcall(body, name="mix_bwd", grid=(S // TR,), in_specs=[_heads_spec()] * 6 + [_row_spec(AW)],
                  out_specs=(_heads_spec(),) * 6, out_shape=(o,) * 6,
                  compiler_params=_params(("parallel",)))(*os_, *ls, datt)


def _ada_fwd(c_all, w_sh, b_sl):
    def body(c_ref, w_ref, b_ref, o_ref):
        cv = c_ref[...]
        o_ref[...] = _dot(cv * jax.nn.sigmoid(cv), w_ref[...], NN) + b_ref[...]

    return _pcall(body, name="ada_fwd", out_shape=jax.ShapeDtypeStruct((N_DEV, w_sh.shape[1]), F32),
                  compiler_params=_params())(c_all, w_sh, b_sl)


def _ada_bwd(c_all, dm_sl):
    def body(c_ref, d_ref, o_ref):
        cv = c_ref[...]
        o_ref[...] = _dot(cv * jax.nn.sigmoid(cv), d_ref[...], TN)

    return _pcall(body, name="ada_bwd", out_shape=jax.ShapeDtypeStruct((D, dm_sl.shape[1]), F32),
                  compiler_params=_params())(c_all, dm_sl)


N_MOD = 6


def _sum_small(gathered):
    n = len(gathered)

    def body(*refs):
        ins, (gb_ref, dm_ref), outs = refs[:n], refs[n:n + 2], refs[n + 2:]

        def total(r):
            acc = r[0]
            for e in range(1, N_DEV):
                acc = acc + r[e]
            return acc

        for i in range(N_MOD):
            cols = slice(i * D, (i + 1) * D)
            gb_ref[:, cols] = total(ins[i])
            for e in range(N_DEV):
                dm_ref[e:e + 1, cols] = ins[i][e]
        for r, o_ref in zip(ins[N_MOD:], outs):
            o_ref[...] = total(r)

    shapes = (jax.ShapeDtypeStruct((1, N_MOD * D), F32), jax.ShapeDtypeStruct((N_DEV, N_MOD * D), F32),
              *[jax.ShapeDtypeStruct(g.shape[1:], F32) for g in gathered[N_MOD:]])
    res = _pcall(body, name="sum_small", out_shape=shapes, compiler_params=_params())(*gathered)
    return res[0], res[1], res[2:]


def _row_tile(m, n):
    t = max(8, min(m, (1 << 19) // n // 8 * 8))
    while m % t:
        t -= 8
    return t


def _pair_sum(full, recv, sel, name):
    _, _, m, n = full.shape
    t = _row_tile(m, n)

    def body(sel_ref, a_ref, b_ref, o_ref):
        o_ref[...] = (a_ref[...].astype(F32) + b_ref[...].astype(F32)).astype(o_ref.dtype)

    gs = pltpu.PrefetchScalarGridSpec(
        num_scalar_prefetch=1, grid=(4, m // t),
        in_specs=[pl.BlockSpec((None, None, t, n), lambda q, i, s: (q, s[0], i, 0)),
                  pl.BlockSpec((None, t, n), lambda q, i, s: (q, i, 0))],
        out_specs=pl.BlockSpec((None, t, n), lambda q, i, s: (q, i, 0)))
    return _pcall(body, name=name, grid_spec=gs, out_shape=jax.ShapeDtypeStruct((4, m, n), full.dtype),
                  compiler_params=_params(("parallel", "parallel")))(sel, full, recv)


def _chip_sum(part, recv, sel, name):
    _, m, n = part.shape
    t = _row_tile(m, n)

    def body(sel_ref, a_ref, r_ref, o_ref):
        o_ref[...] = ((a_ref[...].astype(F32) + r_ref[0].astype(F32)) + r_ref[1].astype(F32)) + r_ref[2].astype(F32)

    gs = pltpu.PrefetchScalarGridSpec(
        num_scalar_prefetch=1, grid=(m // t,),
        in_specs=[pl.BlockSpec((None, t, n), lambda i, s: (s[0], i, 0)),
                  pl.BlockSpec((3, t, n), lambda i, s: (0, i, 0))],
        out_specs=pl.BlockSpec((t, n), lambda i, s: (i, 0)))
    return _pcall(body, name=name, grid_spec=gs, out_shape=jax.ShapeDtypeStruct((m, n), F32),
                  compiler_params=_params(("parallel",)))(sel, part, recv)


def _adamw_math(w, g, m, v):
    nm = ADAM_B1 * m + (1.0 - ADAM_B1) * g
    nv = ADAM_B2 * v + (1.0 - ADAM_B2) * (g * g)
    m_hat = nm / (1.0 - ADAM_B1 ** ADAM_STEP)
    v_hat = nv / (1.0 - ADAM_B2 ** ADAM_STEP)
    return -ADAM_LR * (m_hat / (jnp.sqrt(v_hat) + ADAM_EPS) + ADAM_WD * w), nm, nv


def _adamw(w, g, m, v, name):
    _, rows, cols = w.shape
    t = _row_tile(rows, cols)

    def body(w_ref, g_ref, m_ref, v_ref, d_ref, nm_ref, nv_ref):
        d_ref[...], nm_ref[...], nv_ref[...] = _adamw_math(w_ref[...], g_ref[...], m_ref[...], v_ref[...])

    spec3 = pl.BlockSpec((None, t, cols), lambda i: (0, i, 0))
    spec2 = pl.BlockSpec((t, cols), lambda i: (i, 0))
    o = jax.ShapeDtypeStruct(w.shape, F32)
    return _pcall(body, name=name, grid=(rows // t,), in_specs=[spec3, spec2, spec3, spec3], out_specs=(spec3,) * 3,
                  out_shape=(o, o, o), compiler_params=_params(("parallel",)))(w, g, m, v)


def _adamw_small(ws, gs, ms, vs):
    n = len(ws)

    def body(*refs):
        for i in range(n):
            w_ref, g_ref, m_ref, v_ref = (refs[k * n + i] for k in range(4))
            d, nm, nv = _adamw_math(w_ref[...], g_ref[...], m_ref[...], v_ref[...])
            refs[4 * n + i][...] = d
            refs[5 * n + i][...] = nm
            refs[6 * n + i][...] = nv

    shapes = tuple(jax.ShapeDtypeStruct(w.shape, F32) for w in ws)
    res = _pcall(body, name="adamw_small", out_shape=shapes * 3, compiler_params=_params())(*ws, *gs, *ms, *vs)
    return res[:n], res[n:2 * n], res[2 * n:]


def _mesh_pos():
    return lax.axis_index("x"), lax.axis_index("y"), lax.axis_index("c")


class _Gather:
    def __init__(self, arrs, relay=False):
        self.relay = relay
        self.ins = list(arrs)
        na = self.na = len(arrs)
        self.out_shape = tuple(jax.ShapeDtypeStruct((N_DEV,) + a.shape, a.dtype) for a in arrs)
        self.sems = [pltpu.SemaphoreType.DMA((7 * na,)), pltpu.SemaphoreType.DMA((7 * na,)),
                     pltpu.SemaphoreType.DMA((na,))]

    def _copies(self, ins, outs, sems):
        send_sems, recv_sems, local_sems = sems
        x, y, c = _mesh_pos()
        me, sibling = (x, y, c), (x, y, 1 - c)
        chips = [(1 - x, y), (x, 1 - y), (1 - x, 1 - y)]

        def slot(p):
            return 4 * p[0] + 2 * p[1] + p[2]

        def copy(a, k, block, to, src=None):
            dst = outs[a].at[slot(block)]
            return pltpu.make_async_remote_copy(
                src_ref=dst if src is None else src, dst_ref=dst, send_sem=send_sems.at[7 * a + k],
                recv_sem=recv_sems.at[7 * a + k], device_id=to, device_id_type=MESH)

        mine = [pltpu.make_async_copy(ins[a], outs[a].at[slot(me)], local_sems.at[a]) for a in range(self.na)]
        direct = chips[:2] if self.relay else chips
        first = []
        for a in range(self.na):
            first.append(copy(a, 0, me, sibling, src=ins[a]))
            first += [copy(a, 1 + j, me, (*chip, c), src=ins[a]) for j, chip in enumerate(direct)]
        return me, sibling, chips, c, copy, mine, first

    def start(self, ins, outs, sems):
        *_, mine, first = self._copies(ins, outs, sems)
        for cp in mine + first:
            cp.start()

    def finish(self, ins, outs, sems):
        me, sibling, chips, c, copy, mine, first = self._copies(ins, outs, sems)
        x, y = me[0], me[1]
        passed = []
        for j, chip in enumerate(chips):
            for a in range(self.na):
                if self.relay and j == 2:
                    owner = ((x + 1 - c) % 2, (y + c) % 2, c)
                    cp = copy(a, 3, owner, ((x + c) % 2, (y + 1 - c) % 2, c))
                    cp.start()
                    passed.append(cp)
                copy(a, 1 + j, (*chip, c), me).wait_recv()
                cp = copy(a, 4 + j, (*chip, c), sibling)
                cp.start()
                passed.append(cp)
        for a in range(self.na):
            copy(a, 0, sibling, me).wait_recv()
            for j, chip in enumerate(chips):
                copy(a, 4 + j, (*chip, 1 - c), me).wait_recv()
        for cp in first + passed:
            cp.wait_send()
        for cp in mine:
            cp.wait()


class _ExchangeCore:
    def __init__(self, fulls):
        self.ins = list(fulls)
        self.out_shape = tuple(jax.ShapeDtypeStruct((4,) + f.shape[2:], f.dtype) for f in fulls)
        self.sems = [pltpu.SemaphoreType.DMA((4 * len(fulls),)), pltpu.SemaphoreType.DMA((4 * len(fulls),))]

    def _copies(self, ins, outs, sems):
        send_sems, recv_sems = sems
        x, y, c = _mesh_pos()
        return [pltpu.make_async_remote_copy(
            src_ref=ins[a].at[q, 1 - c], dst_ref=outs[a].at[q], send_sem=send_sems.at[4 * a + q],
            recv_sem=recv_sems.at[4 * a + q], device_id=(x, y, 1 - c), device_id_type=MESH)
            for a in range(len(self.ins)) for q in range(4)]

    def start(self, ins, outs, sems):
        for cp in self._copies(ins, outs, sems):
            cp.start()

    def finish(self, ins, outs, sems):
        for cp in self._copies(ins, outs, sems):
            cp.wait()


class _ExchangeChip:
    def __init__(self, parts):
        self.ins = list(parts)
        self.out_shape = tuple(jax.ShapeDtypeStruct((3,) + p.shape[1:], p.dtype) for p in parts)
        self.sems = [pltpu.SemaphoreType.DMA((3 * len(parts),)), pltpu.SemaphoreType.DMA((3 * len(parts),))]

    def _copies(self, ins, outs, sems):
        send_sems, recv_sems = sems
        x, y, c = _mesh_pos()
        chips = [(1 - x, y), (x, 1 - y), (1 - x, 1 - y)]
        return [pltpu.make_async_remote_copy(
            src_ref=ins[a].at[2 * px + py], dst_ref=outs[a].at[j], send_sem=send_sems.at[3 * a + j],
            recv_sem=recv_sems.at[3 * a + j], device_id=(px, py, c), device_id_type=MESH)
            for a in range(len(self.ins)) for j, (px, py) in enumerate(chips)]

    def start(self, ins, outs, sems):
        for cp in self._copies(ins, outs, sems):
            cp.start()

    def finish(self, ins, outs, sems):
        for cp in self._copies(ins, outs, sems):
            cp.wait()


HBM_ONLY = pl.BlockSpec(memory_space=pltpu.HBM)
SEM_SPEC = pl.BlockSpec(memory_space=pltpu.SEMAPHORE)
SIDE_EFFECT = pltpu.SideEffectType.DATAFLOW_SIDE_EFFECTING


def _chip_copies(p_refs, land_refs, send_sems, recv_sems):
    x, y, c = _mesh_pos()
    return [pltpu.make_async_remote_copy(
        src_ref=p_refs[a].at[2 * px + py], dst_ref=land_refs[a].at[j], send_sem=send_sems.at[3 * a + j],
        recv_sem=recv_sems.at[3 * a + j], device_id=(px, py, c), device_id_type=MESH)
        for a in range(len(p_refs)) for j, (px, py) in enumerate([(1 - x, y), (x, 1 - y), (1 - x, 1 - y)])]


def _chip_exchange_start(parts, name):
    n = len(parts)
    lands = [lax.empty((3,) + p.shape[1:], p.dtype) for p in parts]

    def body(*refs):
        p_refs, land_refs, (send_sems, recv_sems) = refs[:n], refs[n:2 * n], refs[2 * n:2 * n + 2]
        for cp in _chip_copies(p_refs, land_refs, send_sems, recv_sems):
            cp.start()
        token = refs[-1]
        token[...] = jnp.zeros_like(token)

    hbm = lambda t: pltpu.HBM(t.shape, t.dtype)
    res = pl.pallas_call(
        body, name=name,
        out_shape=(pltpu.SemaphoreType.DMA((3 * n,)), pltpu.SemaphoreType.DMA((3 * n,)), *[hbm(t) for t in parts + lands],
                   jax.ShapeDtypeStruct((8, 128), F32)),
        in_specs=(HBM_ONLY,) * (2 * n),
        out_specs=(SEM_SPEC, SEM_SPEC, *[HBM_ONLY] * (2 * n), pl.BlockSpec(memory_space=pltpu.VMEM)),
        input_output_aliases={i: 2 + i for i in range(2 * n)},
        compiler_params=pltpu.CompilerParams(has_side_effects=SIDE_EFFECT))(
        *[pltpu.with_memory_space_constraint(t, pltpu.HBM) for t in parts + lands])
    return (res[0], res[1], list(res[2:2 + n]), list(res[2 + n:2 + 2 * n])), res[-1]


def _chip_exchange_wait(in_flight, after, name):
    send_sems, recv_sems, parts, lands = in_flight
    n = len(parts)

    def body(*refs):
        p_refs, land_refs, (send_sems, recv_sems) = refs[:n], refs[n:2 * n], refs[2 * n:2 * n + 2]
        for cp in _chip_copies(p_refs, land_refs, send_sems, recv_sems):
            cp.wait_send()
            cp.wait_recv()

    res = pl.pallas_call(
        body, name=name, out_shape=tuple(pltpu.HBM(t.shape, t.dtype) for t in parts + lands),
        in_specs=(*[HBM_ONLY] * (2 * n), SEM_SPEC, SEM_SPEC, pl.BlockSpec(memory_space=pl.ANY)),
        out_specs=(HBM_ONLY,) * (2 * n), input_output_aliases={i: i for i in range(2 * n)},
        compiler_params=pltpu.CompilerParams(has_side_effects=SIDE_EFFECT))(*parts, *lands, send_sems, recv_sems, after)
    return list(res[:n]), list(res[n:])


def _reduce_sums(fulls, recv_core, core, tag):
    return [_pair_sum(f, r, core, f"rs_pair_{tag}{i}") for i, (f, r) in enumerate(zip(fulls, recv_core))]


def _local_step(x, tgt, mods, w_in_t, shards, small, chip, core):
    sh1, sc1, g1, sh2, sc2, g2 = mods
    norm1_g, rel_bias, gn_g, gn_b, norm2_g, norm_f_g = small
    tables = _ret_tables()
    buckets = jnp.asarray(_bucket_tables())

    h1 = _norm_mod_fwd(x, norm1_g, sh1, sc1, "norm1_fwd")
    proj, slabs, gathered = _proj(h1, w_in_t, _Gather(shards[:4]))
    w_ret_out, w_att_out, w_o, w_ff1 = (_from_slots(g, ax) for g, ax in zip(gathered, BIG_AXES[1:5]))
    (gated, ro, states), (w_ff2_8,) = _ret_fwd(proj, tables, gn_g, gn_b, comm=_Gather(shards[4:]))
    w_ff2 = _from_slots(w_ff2_8, BIG_AXES[5])
    bias = _bias_build(rel_bias, buckets)
    outs, lses = [], []
    for gi in range(len(ATT_GROUPS)):
        o, l = _att_fwd(slabs, bias, gi)
        outs.append(o)
        lses.append(l)
    att = _mix_fwd(outs, lses)
    ret_out = _mm(gated, w_ret_out, 'nn', tm=S, tn=256, tk=2048, name="ret_out")
    att_out = _mm(att, w_att_out, 'nn', tm=S, tn=512, tk=AW, name="att_out")
    merged = _merge_fwd(proj, ret_out, att_out)
    mixo, x1 = _mm(merged, w_o, 'nn', tm=S, tn=256, tk=D, name="w_o", res=x, gvec=g1)
    h2 = _norm_mod_fwd(x1, norm2_g, sh2, sc2, "norm2_fwd")
    u, act = _mm(h2, w_ff1, 'nn', tm=S, tn=512, tk=D, name="ff1", relu2=True)
    f, x2 = _mm(act, w_ff2, 'nn', tm=1024, tn=512, tk=2048, name="ff2", res=x1, gvec=g2)
    loss, dx2, g_normf, df, dg2 = _final_loss(x2, tgt, norm_f_g, f, g2)

    gw_ff2 = _mm(act, df, 'tn', tm=512, tn=D, tk=S, name="gw_ff2", out_dtype=BF16)
    du = _mm(df, w_ff2, 'nt', tm=S, tn=512, tk=D, name="d_act", out_dtype=BF16, relu2_of=u)
    gw_ff1 = _mm(h2, du, 'tn', tm=D, tn=512, tk=S, name="gw_ff1", out_dtype=BF16)
    fulls_a = [_to_slots(g, ax) for g, ax in zip((gw_ff1, gw_ff2), BIG_AXES[4:])]
    dh2, recv_core_a = _mm(du, w_ff1, 'nt', tm=1024, tn=1024, tk=2048, name="dh2", comm=_ExchangeCore(fulls_a))
    parts_a = _reduce_sums(fulls_a, recv_core_a, core, "a")
    flight_a, token_a = _chip_exchange_start(parts_a, "rs_a_start")
    dx1, dsc2, dsh2, g_norm2, dmixo, dg1 = _norm_mod_bwd(x1, norm2_g, sc2, dh2, dx2, "norm2_bwd", gate=(mixo, g1))

    gw_o = _mm(merged, dmixo, 'tn', tm=D, tn=512, tk=S, name="gw_o", out_dtype=BF16, after=token_a)
    dmerged = _mm(dmixo, w_o, 'nt', tm=S, tn=512, tk=D, name="dmerged")
    d_ret_out, d_att_out, dga, dgb = _merge_bwd(proj, ret_out, att_out, dmerged)
    gw_ret_out = _mm(gated, d_ret_out, 'tn', tm=512, tn=D, tk=S, name="gw_ret_out", out_dtype=BF16)
    gw_att_out = _mm(att, d_att_out, 'tn', tm=AW, tn=D, tk=S, name="gw_att_out", out_dtype=BF16)
    fulls_b = [_to_slots(g, ax) for g, ax in zip((gw_ret_out, gw_att_out, gw_o), BIG_AXES[1:4])]
    dgated, recv_core_b = _mm(d_ret_out, w_ret_out, 'nt', tm=S, tn=512, tk=D, name="dgated",
                              comm=_ExchangeCore(fulls_b))
    parts_b = _reduce_sums(fulls_b, recv_core_b, core, "b")
    flight_b, token_b = _chip_exchange_start(parts_b, "rs_b_start")
    datt = _mm(d_att_out, w_att_out, 'nt', tm=S, tn=AW, tk=D, name="datt", after=token_b)
    mix_grads = _mix_bwd(outs, lses, datt)
    datt_parts, ds_sums = [], []
    for gi in range(len(ATT_GROUPS)):
        dq, dk, dv, ds_sum = _att_bwd(slabs, bias, outs[gi], lses[gi], mix_grads[gi], mix_grads[3 + gi], gi)
        datt_parts += [dq.reshape(S, AW), dk.reshape(S, AW), dv.reshape(S, AW)]
        ds_sums.append(ds_sum)
    g_bias = _bias_grad(jnp.concatenate(ds_sums, axis=0), buckets)[:, :, 0].T.reshape(1, -1)
    dret, g_gn_g, g_gn_b = _ret_bwd(proj, tables, gn_g, gn_b, ro, states, dgated)
    parts_a, recv_chip_a = _chip_exchange_wait(flight_a, dret, "rs_a_wait")
    parts_b, recv_chip_b = _chip_exchange_wait(flight_b, dret, "rs_b_wait")
    red_a = [_chip_sum(p, r, chip, f"rs_sum_a{i}") for i, (p, r) in enumerate(zip(parts_a, recv_chip_a))]
    red_b = [_chip_sum(p, r, chip, f"rs_sum_b{i}") for i, (p, r) in enumerate(zip(parts_b, recv_chip_b))]
    dproj = jnp.concatenate([dret] + datt_parts + [dga, dgb], axis=1)
    gw_in_t = _mm(dproj, h1, 'tn', tm=512, tn=D, tk=S, name="gw_in", out_dtype=BF16)
    full_in = [_to_slots(gw_in_t, 0)]
    recv_core_in = _run_comm(_ExchangeCore(full_in), "rs_core_in")
    part_in = _reduce_sums(full_in, recv_core_in, core, "c")
    in_flight, token = _chip_exchange_start(part_in, "rs_in_start")
    dh1 = _mm(dproj, w_in_t, 'nn', tm=1024, tn=1024, tk=2560, name="dh1", after=token)
    gx, dsc1, dsh1, g_norm1 = _norm_mod_bwd(x, norm1_g, sc1, dh1, dx1, "norm1_bwd")

    dmod = [dsh1, dsc1, dg1, dsh2, dsc2, dg2]
    small_g = [g_norm1, g_bias, g_gn_g, g_gn_b, g_norm2, g_normf]
    return loss, gx, in_flight, red_b + red_a, small_g, dmod


def _to_slots(g, axis):
    if axis == 0:
        return g.reshape(4, 2, g.shape[0] // N_DEV, g.shape[1])
    return g.reshape(g.shape[0], N_DEV, g.shape[1] // N_DEV).transpose(1, 0, 2).reshape(4, 2, g.shape[0], -1)


def _from_slots(w8, axis):
    if axis == 0:
        return w8.reshape(-1, w8.shape[2])
    return w8.transpose(1, 0, 2).reshape(w8.shape[1], -1)


BIG_AXES = (1, 0, 1, 0, 1, 0)


def kernel(x, c, w_ada, b_ada, norm1_g, w_in, rel_bias, ret_gn_g, ret_gn_b, w_ret_out, w_att_out, w_o, norm2_g, w_ff1, w_ff2, norm_f_g, loss_target, m_w_ada, m_b_ada, m_norm1_g, m_w_in, m_rel_bias, m_ret_gn_g, m_ret_gn_b, m_w_ret_out, m_w_att_out, m_w_o, m_norm2_g, m_w_ff1, m_w_ff2, m_norm_f_g, v_w_ada, v_b_ada, v_norm1_g, v_w_in, v_rel_bias, v_ret_gn_g, v_ret_gn_b, v_w_ret_out, v_w_att_out, v_w_o, v_norm2_g, v_w_ff1, v_w_ff2, v_norm_f_g):
    mx, my, mc = _mesh_pos()
    dev = 4 * mx + 2 * my + mc
    chip = jnp.reshape(2 * mx + my, (1,)).astype(jnp.int32)
    core = jnp.reshape(mc, (1,)).astype(jnp.int32)
    ada_w = D * 6 // N_DEV

    w_in, m_w_in, v_w_in = (jnp.transpose(t, (0, 2, 1)) for t in (w_in, m_w_in, v_w_in))

    shards = [w[0].astype(BF16) for w in (w_in, w_ret_out, w_att_out, w_o, w_ff1, w_ff2)]
    c_all, w_in8 = _run_comm(_Gather([c, shards[0]], relay=True), "gather_c_w_in")
    c_all = c_all.reshape(N_DEV, D)
    b_sl = lax.dynamic_slice(b_ada, (0, dev * ada_w), (1, ada_w))
    (mod_all,) = _run_comm(_Gather([_ada_fwd(c_all, w_ada[0], b_sl)]), "gather_mod")
    mod = lax.dynamic_index_in_dim(mod_all, dev, axis=1, keepdims=False).reshape(6, D)
    mods = tuple(mod[i:i + 1] for i in range(6))

    small = (norm1_g, rel_bias, ret_gn_g, ret_gn_b, norm2_g, norm_f_g.reshape(1, D))
    loss, gx, in_flight, big_red, small_g, dmod = _local_step(x[0], loss_target[0], mods, w_in8.reshape(IN_COLS, D),
                                                              shards[1:], small, chip, core)

    gathered = _run_comm(_Gather(dmod + small_g + [loss]), "gather_small")
    g_b_ada, dmod_all, (g_norm1, g_bias, g_gn_g, g_gn_b, g_norm2, g_normf, loss_sum) = _sum_small(gathered)
    loss_out = loss_sum[0, 0]
    g_w_ada = _ada_bwd(c_all, lax.dynamic_slice(dmod_all, (0, dev * ada_w), (N_DEV, ada_w)))

    names = ['w_ada', 'b_ada', 'norm1_g', 'w_in', 'rel_bias', 'ret_gn_g', 'ret_gn_b', 'w_ret_out', 'w_att_out',
             'w_o', 'norm2_g', 'w_ff1', 'w_ff2', 'norm_f_g']
    ws = dict(zip(names, (w_ada, b_ada, norm1_g, w_in, rel_bias, ret_gn_g, ret_gn_b, w_ret_out, w_att_out, w_o,
                          norm2_g, w_ff1, w_ff2, norm_f_g)))
    ms = dict(zip(names, (m_w_ada, m_b_ada, m_norm1_g, m_w_in, m_rel_bias, m_ret_gn_g, m_ret_gn_b, m_w_ret_out,
                          m_w_att_out, m_w_o, m_norm2_g, m_w_ff1, m_w_ff2, m_norm_f_g)))
    vs = dict(zip(names, (v_w_ada, v_b_ada, v_norm1_g, v_w_in, v_rel_bias, v_ret_gn_g, v_ret_gn_b, v_w_ret_out,
                          v_w_att_out, v_w_o, v_norm2_g, v_w_ff1, v_w_ff2, v_norm_f_g)))
    grads = dict(w_ada=g_w_ada, w_ret_out=big_red[0], w_att_out=big_red[1], w_o=big_red[2],
                 w_ff1=big_red[3], w_ff2=big_red[4], b_ada=g_b_ada, norm1_g=g_norm1, rel_bias=g_bias,
                 ret_gn_g=g_gn_g, ret_gn_b=g_gn_b, norm2_g=g_norm2, norm_f_g=g_normf)
    delta, new_m, new_v = {}, {}, {}
    for n in ('w_ada', 'w_ret_out', 'w_att_out', 'w_o', 'w_ff1', 'w_ff2'):
        delta[n], new_m[n], new_v[n] = _adamw(ws[n], grads[n], ms[n], vs[n], "adamw_" + n)
        grads[n] = grads[n].reshape(ws[n].shape)
    small_names = ('b_ada', 'norm1_g', 'rel_bias', 'ret_gn_g', 'ret_gn_b', 'norm2_g', 'norm_f_g')
    two_d = {n: (1, ws[n].size) if ws[n].ndim == 1 else ws[n].shape for n in small_names}
    d_, m_, v_ = _adamw_small(*[[src[n].reshape(two_d[n]) for n in small_names] for src in (ws, grads, ms, vs)])
    for i, n in enumerate(small_names):
        shp = ws[n].shape
        delta[n], new_m[n], new_v[n] = d_[i].reshape(shp), m_[i].reshape(shp), v_[i].reshape(shp)
        grads[n] = grads[n].reshape(shp)

    done = lax.optimization_barrier((gx, tuple(d_), tuple(delta[n] for n in ('w_ada', 'w_ret_out', 'w_att_out', 'w_o',
                                                                               'w_ff1', 'w_ff2'))))
    (part_in,), (recv_chip_in,) = _chip_exchange_wait(in_flight, done[0], "rs_in_wait")
    grads['w_in'] = _chip_sum(part_in, recv_chip_in, chip, "rs_sum_c")
    delta['w_in'], new_m['w_in'], new_v['w_in'] = _adamw(w_in, grads['w_in'], m_w_in, v_w_in, "adamw_w_in")
    grads['w_in'] = grads['w_in'].reshape(w_in.shape)
    for d in (grads, delta, new_m, new_v):
        d['w_in'] = jnp.transpose(d['w_in'], (0, 2, 1))
    return (loss_out, gx[None], *[grads[n] for n in names], *[delta[n] for n in names],
            *[new_m[n] for n in names], *[new_v[n] for n in names])
```

```python
import functools
import math

import numpy as np
import jax
import jax.numpy as jnp
from jax import lax
from jax.experimental import pallas as pl
from jax.experimental.pallas import tpu as pltpu

F32 = jnp.float32
BF16 = jnp.bfloat16
MESH = pl.DeviceIdType.MESH

N_DEV = 8
S = 2048
D = 1024
RET_HEADS = 4
RET_DK = 256
RET_DV = 512
CHUNK = 128
N_CHUNK = S // CHUNK
ATT_GROUPS = ((128, 1), (512, 4), (2048, 16))
ATT_HG = 4
ATT_DH = 128
ATT_BLK = 128
N_BUCKETS = 32
MAX_DIST = 2048
D_FF = 4096
IN_COLS = 12800
OFF_RQ, OFF_RK, OFF_RV, OFF_RG, OFF_ATT = 0, 1024, 2048, 4096, 6144
OFF_GA, OFF_GB = 6144, 7168
RMS_EPS = 1e-6
GN_EPS = 1e-5
ADAM_LR, ADAM_B1, ADAM_B2, ADAM_EPS, ADAM_WD, ADAM_STEP = 0.001, 0.9, 0.999, 1e-08, 0.01, 10
VMEM_LIMIT = 48 * 1024 * 1024


def _pcall(body, **kw):
    return pl.pallas_call(body, **kw)


def _params(sem=None):
    return pltpu.CompilerParams(dimension_semantics=sem, vmem_limit_bytes=VMEM_LIMIT)


HBM_SPEC = pl.BlockSpec(memory_space=pl.ANY)


def _carry(body, comm, *, name, grid, in_specs, out_specs, out_shape, scratch_shapes=()):
    single = not isinstance(out_specs, (tuple, list))
    o_specs = (out_specs,) if single else tuple(out_specs)
    o_shape = (out_shape,) if single else tuple(out_shape)
    n_in, n_out, n_scr = len(in_specs), len(o_specs), len(scratch_shapes)
    nci, nco = len(comm.ins), len(comm.out_shape)
    total = int(np.prod(grid))

    def wrapped(*refs):
        bounds = np.cumsum([0, n_in, nci, n_out, nco, n_scr])
        a, ci, o, co, scr = (refs[bounds[i]:bounds[i + 1]] for i in range(5))
        sems = refs[bounds[5]:]
        flat = 0
        for d, g in enumerate(grid):
            flat = flat * g + pl.program_id(d)

        @pl.when(flat == 0)
        def _():
            comm.start(ci, co, sems)

        body(*a, *o, *scr)

        @pl.when(flat == total - 1)
        def _():
            comm.finish(ci, co, sems)

    call = _pcall(wrapped, name=name, grid=grid, in_specs=list(in_specs) + [HBM_SPEC] * nci,
                  out_specs=o_specs + (HBM_SPEC,) * nco, out_shape=o_shape + tuple(comm.out_shape),
                  scratch_shapes=list(scratch_shapes) + list(comm.sems),
                  compiler_params=_params(("arbitrary",) * len(grid)))

    def run(*args):
        res = call(*args, *comm.ins)
        own = res[0] if single else tuple(res[:n_out])
        return own, tuple(res[n_out:])

    return run


def _run_comm(comm, name):
    nci, nco = len(comm.ins), len(comm.out_shape)

    def body(*refs):
        ci, co, sems = refs[:nci], refs[nci:nci + nco], refs[nci + nco:]
        comm.start(ci, co, sems)
        comm.finish(ci, co, sems)

    return _pcall(body, name=name, in_specs=[HBM_SPEC] * nci, out_specs=(HBM_SPEC,) * nco,
                  out_shape=tuple(comm.out_shape), scratch_shapes=list(comm.sems))(*comm.ins)


def _dot(a, b, dn):
    return lax.dot_general(a.astype(BF16), b.astype(BF16), (dn, ((), ())), preferred_element_type=F32)


NN = ((1,), (0,))
NT = ((1,), (1,))
TN = ((0,), (0,))


def _mm(a, b, mode, *, tm, tn, tk, name, out_dtype=F32, res=None, gvec=None, relu2=False, relu2_of=None, comm=None,
        after=None):
    if mode == 'nn':
        (M, K), (_, N) = a.shape, b.shape
        a_spec = pl.BlockSpec((tm, tk), lambda i, j, k: (i, k))
        b_spec = pl.BlockSpec((tk, tn), lambda i, j, k: (k, j))
        dn = NN
    elif mode == 'nt':
        (M, K), (N, _) = a.shape, b.shape
        a_spec = pl.BlockSpec((tm, tk), lambda i, j, k: (i, k))
        b_spec = pl.BlockSpec((tn, tk), lambda i, j, k: (j, k))
        dn = NT
    else:
        (K, M), (_, N) = a.shape, b.shape
        a_spec = pl.BlockSpec((tk, tm), lambda i, j, k: (k, i))
        b_spec = pl.BlockSpec((tk, tn), lambda i, j, k: (k, j))
        dn = TN
    assert M % tm == 0 and N % tn == 0 and K % tk == 0, (name, M, N, K)
    nk = K // tk
    fused = res is not None
    o_spec = pl.BlockSpec((tm, tn), lambda i, j, k: (i, j))

    def body(a_ref, b_ref, *rest):
        acc_ref = rest[-1] if nk > 1 else None
        if after is not None:
            rest = rest[1:]
        if fused:
            res_ref, g_ref, o_ref, x_ref = rest[:4]
        elif relu2_of is not None:
            u_ref, o_ref = rest[:2]
        elif relu2:
            o_ref, act_ref = rest[:2]
        else:
            o_ref = rest[0]

        def finish(acc):
            if relu2_of is not None:
                acc = acc * (2.0 * jnp.maximum(u_ref[...], 0.0))
            o_ref[...] = acc.astype(o_ref.dtype)
            if fused:
                x_ref[...] = res_ref[...] + g_ref[...] * acc
            if relu2:
                r = jnp.maximum(acc, 0.0)
                act_ref[...] = (r * r).astype(BF16)

        p = _dot(a_ref[...], b_ref[...], dn)
        if nk == 1:
            finish(p)
        else:
            k = pl.program_id(2)

            @pl.when(k == 0)
            def _():
                acc_ref[...] = p

            @pl.when(k > 0)
            def _():
                acc_ref[...] += p

            @pl.when(k == nk - 1)
            def _():
                finish(acc_ref[...])

    in_specs = [a_spec, b_spec]
    args = [a, b]
    if after is not None:
        in_specs.append(pl.BlockSpec(memory_space=pl.ANY))
        args.append(after)
    out_shape = jax.ShapeDtypeStruct((M, N), out_dtype)
    out_specs = o_spec
    if fused:
        in_specs += [pl.BlockSpec((tm, tn), lambda i, j, k: (i, j)), pl.BlockSpec((1, tn), lambda i, j, k: (0, j))]
        args += [res, gvec]
        out_shape = (out_shape, jax.ShapeDtypeStruct((M, N), F32))
        out_specs = (o_spec, pl.BlockSpec((tm, tn), lambda i, j, k: (i, j)))
    elif relu2_of is not None:
        in_specs.append(pl.BlockSpec((tm, tn), lambda i, j, k: (i, j)))
        args.append(relu2_of)
    elif relu2:
        out_shape = (out_shape, jax.ShapeDtypeStruct((M, N), BF16))
        out_specs = (o_spec, pl.BlockSpec((tm, tn), lambda i, j, k: (i, j)))
    kw = dict(name=name, grid=(M // tm, N // tn, nk), in_specs=in_specs, out_specs=out_specs,
              out_shape=out_shape, scratch_shapes=[pltpu.VMEM((tm, tn), F32)] if nk > 1 else [])
    if comm is not None:
        return _carry(body, comm, **kw)(*args)
    return _pcall(body, compiler_params=_params(("parallel", "parallel", "arbitrary")), **kw)(*args)


PROJ_TN = 512
ATT_T0, ATT_T1 = 6144 // PROJ_TN, 10752 // PROJ_TN
N_SLABS = (ATT_T1 - ATT_T0) * 4
MAIN_COLS = IN_COLS - (ATT_T1 - ATT_T0) * PROJ_TN


def _proj(h1, w_in_t, comm):
    nj = IN_COLS // PROJ_TN

    def body(a_ref, b_ref, main_ref, slab_ref):
        j = pl.program_id(1)
        is_att = (j >= ATT_T0) & (j < ATT_T1)
        chunks = [pl.ds(c * 512, 512) for c in range(S // 512)]

        @pl.when(jnp.logical_not(is_att))
        def _():
            for rows in chunks:
                main_ref[rows, :] = _dot(a_ref[rows, :], b_ref[...], NT)

        @pl.when(is_att)
        def _():
            for rows in chunks:
                p = _dot(a_ref[rows, :], b_ref[...], NT)
                for h in range(4):
                    slab_ref[h, rows, :] = p[:, h * 128:(h + 1) * 128]

    main_idx = lambda j: jnp.where(j < ATT_T0, j, jnp.where(j < ATT_T1, ATT_T0 - 1, j - (ATT_T1 - ATT_T0)))
    slab_idx = lambda j: jnp.clip(j - ATT_T0, 0, ATT_T1 - ATT_T0 - 1)
    (main, slabs), got = _carry(
        body, comm, name="proj", grid=(1, nj, 1),
        in_specs=[pl.BlockSpec((S, D), lambda i, j, k: (0, 0)), pl.BlockSpec((PROJ_TN, D), lambda i, j, k: (j, 0))],
        out_specs=(pl.BlockSpec((S, PROJ_TN), lambda i, j, k: (0, main_idx(j))),
                   pl.BlockSpec((4, S, 128), lambda i, j, k: (slab_idx(j), 0, 0))),
        out_shape=(jax.ShapeDtypeStruct((S, MAIN_COLS), F32), jax.ShapeDtypeStruct((N_SLABS, S, 128), F32)))(h1, w_in_t)
    return main, slabs, got


TR = 256


def _row_spec(w=D):
    return pl.BlockSpec((TR, w), lambda i: (i, 0))


def _vec_spec(w=D):
    return pl.BlockSpec((1, w), lambda i: (0, 0))


def _norm_mod_fwd(x, g, sh, sc, name):
    def body(x_ref, g_ref, sh_ref, sc_ref, o_ref):
        xv = x_ref[...]
        rstd = lax.rsqrt(jnp.mean(xv * xv, axis=-1, keepdims=True) + RMS_EPS)
        n = xv * rstd * g_ref[...]
        o_ref[...] = (n * (1.0 + sc_ref[...]) + sh_ref[...]).astype(BF16)

    return _pcall(body, name=name, grid=(S // TR,), in_specs=[_row_spec(), _vec_spec(), _vec_spec(), _vec_spec()],
                  out_specs=_row_spec(), out_shape=jax.ShapeDtypeStruct((S, D), BF16),
                  compiler_params=_params(("parallel",)))(x, g, sh, sc)


def _norm_mod_bwd(x, g, sc, dh, dres, name, gate=None):
    gated = gate is not None

    def body(x_ref, g_ref, sc_ref, dh_ref, dres_ref, *rest):
        if gated:
            f_ref, gv_ref, dx_ref, dsc_ref, dsh_ref, dg_ref, dz_ref, dgv_ref = rest
        else:
            dx_ref, dsc_ref, dsh_ref, dg_ref = rest
        i = pl.program_id(0)
        xv = x_ref[...]
        dh = dh_ref[...]
        rstd = lax.rsqrt(jnp.mean(xv * xv, axis=-1, keepdims=True) + RMS_EPS)
        xhat = xv * rstd
        gv = g_ref[...]
        dn = dh * (1.0 + sc_ref[...])
        dxhat = dn * gv
        dx = dres_ref[...] + rstd * (dxhat - xhat * jnp.mean(dxhat * xhat, axis=-1, keepdims=True))
        dx_ref[...] = dx
        sums = [(dsc_ref, jnp.sum(dh * (xhat * gv), axis=0, keepdims=True)),
                (dsh_ref, jnp.sum(dh, axis=0, keepdims=True)),
                (dg_ref, jnp.sum(dn * xhat, axis=0, keepdims=True))]
        if gated:
            dz_ref[...] = (dx * gv_ref[...]).astype(BF16)
            sums.append((dgv_ref, jnp.sum(dx * f_ref[...], axis=0, keepdims=True)))

        @pl.when(i == 0)
        def _():
            for ref, p in sums:
                ref[...] = p

        @pl.when(i > 0)
        def _():
            for ref, p in sums:
                ref[...] += p

    vec = jax.ShapeDtypeStruct((1, D), F32)
    in_specs = [_row_spec(), _vec_spec(), _vec_spec(), _row_spec(), _row_spec()]
    out_specs = [_row_spec(), _vec_spec(), _vec_spec(), _vec_spec()]
    out_shape = [jax.ShapeDtypeStruct((S, D), F32), vec, vec, vec]
    args = [x, g, sc, dh, dres]
    if gated:
        in_specs += [_row_spec(), _vec_spec()]
        out_specs += [_row_spec(), _vec_spec()]
        out_shape += [jax.ShapeDtypeStruct((S, D), BF16), vec]
        args += list(gate)
    return _pcall(body, name=name, grid=(S // TR,), in_specs=in_specs, out_specs=tuple(out_specs),
                  out_shape=tuple(out_shape), compiler_params=_params(("arbitrary",)))(*args)


def _final_loss(x2, tgt, g, f, g2):
    def body(x_ref, t_ref, g_ref, f_ref, g2_ref, loss_ref, dx_ref, dg_ref, df_ref, dg2_ref):
        i = pl.program_id(0)
        xv = x_ref[...]
        gv = g_ref[...]
        rstd = lax.rsqrt(jnp.mean(xv * xv, axis=-1, keepdims=True) + RMS_EPS)
        xhat = xv * rstd
        err = xhat * gv - t_ref[...]
        dy = err * (1.0 / D)
        dxhat = dy * gv
        dx = rstd * (dxhat - xhat * jnp.mean(dxhat * xhat, axis=-1, keepdims=True))
        dx_ref[...] = dx
        df_ref[...] = (dx * g2_ref[...]).astype(BF16)
        p_g = jnp.sum(dy * xhat, axis=0, keepdims=True)
        p_g2 = jnp.sum(dx * f_ref[...], axis=0, keepdims=True)
        p_l = jnp.zeros((1, 128), F32) + 0.5 * jnp.sum(jnp.mean(err * err, axis=-1, keepdims=True))

        @pl.when(i == 0)
        def _():
            dg_ref[...] = p_g
            dg2_ref[...] = p_g2
            loss_ref[...] = p_l

        @pl.when(i > 0)
        def _():
            dg_ref[...] += p_g
            dg2_ref[...] += p_g2
            loss_ref[...] += p_l

    vec = jax.ShapeDtypeStruct((1, D), F32)
    return _pcall(body, name="final_loss", grid=(S // TR,),
                  in_specs=[_row_spec(), _row_spec(), _vec_spec(), _row_spec(), _vec_spec()],
                  out_specs=(_vec_spec(128), _row_spec(), _vec_spec(), _row_spec(), _vec_spec()),
                  out_shape=(jax.ShapeDtypeStruct((1, 128), F32), jax.ShapeDtypeStruct((S, D), F32), vec,
                             jax.ShapeDtypeStruct((S, D), BF16), vec),
                  compiler_params=_params(("arbitrary",)))(x2, tgt, g, f, g2)


HALF = 512


def _merge_fwd(proj, ret_out, att_out):
    def body(ga_ref, gb_ref, r_ref, a_ref, o_ref):
        o_ref[...] = (jax.nn.sigmoid(ga_ref[...]) * r_ref[...] + jax.nn.sigmoid(gb_ref[...]) * a_ref[...]).astype(BF16)

    blk = lambda off: pl.BlockSpec((TR, HALF), lambda i, j: (i, off // HALF + j))
    return _pcall(body, name="merge_fwd", grid=(S // TR, D // HALF),
                  in_specs=[blk(OFF_GA), blk(OFF_GB), blk(0), blk(0)], out_specs=blk(0),
                  out_shape=jax.ShapeDtypeStruct((S, D), BF16),
                  compiler_params=_params(("parallel", "parallel")))(proj, proj, ret_out, att_out)


def _merge_bwd(proj, ret_out, att_out, dmerged):
    def body(ga_ref, gb_ref, r_ref, a_ref, dm_ref, dr_ref, da_ref, dga_ref, dgb_ref):
        sa = jax.nn.sigmoid(ga_ref[...])
        sb = jax.nn.sigmoid(gb_ref[...])
        dm = dm_ref[...]
        dr_ref[...] = (dm * sa).astype(BF16)
        da_ref[...] = (dm * sb).astype(BF16)
        dga_ref[...] = (dm * r_ref[...] * (sa * (1.0 - sa))).astype(BF16)
        dgb_ref[...] = (dm * a_ref[...] * (sb * (1.0 - sb))).astype(BF16)

    blk = lambda off: pl.BlockSpec((TR, HALF), lambda i, j: (i, off // HALF + j))
    o = jax.ShapeDtypeStruct((S, D), BF16)
    return _pcall(body, name="merge_bwd", grid=(S // TR, D // HALF),
                  in_specs=[blk(OFF_GA), blk(OFF_GB), blk(0), blk(0), blk(0)], out_specs=(blk(0),) * 4,
                  out_shape=(o, o, o, o),
                  compiler_params=_params(("parallel", "parallel")))(proj, proj, ret_out, att_out, dmerged)


def _ret_tables():
    H, C = RET_HEADS, CHUNK
    log_g = jnp.log1p(-(2.0 ** (-5.0 - jnp.arange(H, dtype=F32))))
    idx = jnp.arange(C, dtype=F32)
    rel = idx[:, None] - idx[None, :]
    inner = jnp.where(rel >= 0, jnp.exp(log_g[:, None, None] * jnp.maximum(rel, 0.0)), 0.0)
    qd = jnp.exp(log_g[:, None] * (idx + 1.0))[:, :, None]
    kd = jnp.exp(log_g[:, None] * (C - 1.0 - idx))[:, :, None]
    cd = jnp.broadcast_to(jnp.exp(log_g * C)[:, None, None], (H, 1, 128))
    half = RET_DK // 2
    inv = 10000.0 ** (-jnp.arange(half, dtype=F32) / half)
    ang = jnp.arange(S, dtype=F32)[:, None] * inv[None, :]
    return inner, qd, kd, cd, jnp.cos(ang), jnp.sin(ang)


def _rot(x, cos, sin):
    x1, x2 = x[:, :128], x[:, 128:]
    return jnp.concatenate([x1 * cos - x2 * sin, x1 * sin + x2 * cos], axis=1)


def _rot_t(d, cos, sin):
    d1, d2 = d[:, :128], d[:, 128:]
    return jnp.concatenate([d1 * cos + d2 * sin, d2 * cos - d1 * sin], axis=1)


RET_COLS = OFF_ATT
RET_VW = RET_HEADS * RET_DV


def _ret_specs(chunk_of):
    ci = chunk_of
    whole = lambda shape: pl.BlockSpec(shape, lambda t: (0,) * len(shape))
    return [
        pl.BlockSpec((CHUNK, RET_COLS), lambda t: (ci(t), 0)),
        pl.BlockSpec((CHUNK, 128), lambda t: (ci(t), 0)),
        pl.BlockSpec((CHUNK, 128), lambda t: (ci(t), 0)),
        whole((RET_HEADS, CHUNK, CHUNK)), whole((RET_HEADS, CHUNK, 1)), whole((RET_HEADS, CHUNK, 1)),
        whole((RET_HEADS, 1, 128)), whole((1, RET_VW)), whole((1, RET_VW)),
    ]


def _ret_cols(h):
    q = slice(OFF_RQ + h * RET_DK, OFF_RQ + (h + 1) * RET_DK)
    k = slice(OFF_RK + h * RET_DK, OFF_RK + (h + 1) * RET_DK)
    v = slice(OFF_RV + h * RET_DV, OFF_RV + (h + 1) * RET_DV)
    g = slice(OFF_RG + h * RET_DV, OFF_RG + (h + 1) * RET_DV)
    return q, k, v, g, slice(h * RET_DV, (h + 1) * RET_DV)


def _ret_fwd(proj, tables, gn_g, gn_b, comm=None):
    inner, qd, kd, cd, cos, sin = tables

    def body(x_ref, cos_ref, sin_ref, in_ref, qd_ref, kd_ref, cd_ref, g_ref, b_ref,
             gated_ref, ro_ref, st_ref, s_scr):
        i = pl.program_id(0)

        @pl.when(i == 0)
        def _():
            s_scr[...] = jnp.zeros_like(s_scr)

        cosv, sinv = cos_ref[...], sin_ref[...]
        for h in range(RET_HEADS):
            cq, ck, cv, cg, co = _ret_cols(h)
            q = _rot(x_ref[:, cq], cosv, sinv)
            k = _rot(x_ref[:, ck], cosv, sinv) * (RET_DK ** -0.5)
            v = x_ref[:, cv]
            st = s_scr[h]
            st_ref[h] = st
            s = _dot(q, k, NT) * in_ref[h]
            o = _dot(s, v, NN) + _dot(q, st, NN) * qd_ref[h]
            s_scr[h] = st * cd_ref[h, :, :1] + _dot(k * kd_ref[h], v, TN)
            ro_ref[:, co] = o
            mu = jnp.mean(o, axis=-1, keepdims=True)
            oc = o - mu
            var = jnp.mean(oc * oc, axis=-1, keepdims=True)
            rn = oc * lax.rsqrt(var + GN_EPS) * g_ref[:, co] + b_ref[:, co]
            rg = x_ref[:, cg]
            gated_ref[:, co] = (rg * jax.nn.sigmoid(rg) * rn).astype(BF16)

    ospec = pl.BlockSpec((CHUNK, RET_VW), lambda t: (t, 0))
    kw = dict(name="ret_fwd", grid=(N_CHUNK,), in_specs=_ret_specs(lambda t: t),
              out_specs=(ospec, ospec, pl.BlockSpec((RET_HEADS, None, RET_DK, RET_DV), lambda t: (0, t, 0, 0))),
              out_shape=(jax.ShapeDtypeStruct((S, RET_VW), BF16), jax.ShapeDtypeStruct((S, RET_VW), F32),
                         jax.ShapeDtypeStruct((RET_HEADS, N_CHUNK, RET_DK, RET_DV), F32)),
              scratch_shapes=[pltpu.VMEM((RET_HEADS, RET_DK, RET_DV), F32)])
    args = (proj, cos, sin, inner, qd, kd, cd, gn_g, gn_b)
    if comm is not None:
        return _carry(body, comm, **kw)(*args)
    return _pcall(body, compiler_params=_params(("arbitrary",)), **kw)(*args)


def _ret_bwd(proj, tables, gn_g, gn_b, ro, states, dgated, comm=None):
    inner, qd, kd, cd, cos, sin = tables
    last = N_CHUNK - 1

    def body(x_ref, cos_ref, sin_ref, in_ref, qd_ref, kd_ref, cd_ref, g_ref, b_ref, ro_ref, st_ref, dg_ref,
             dx_ref, gg_ref, gb_ref, gs_scr):
        t = pl.program_id(0)

        @pl.when(t == 0)
        def _():
            gs_scr[...] = jnp.zeros_like(gs_scr)
            gg_ref[...] = jnp.zeros_like(gg_ref)
            gb_ref[...] = jnp.zeros_like(gb_ref)

        cosv, sinv = cos_ref[...], sin_ref[...]
        for h in range(RET_HEADS):
            cq, ck, cv, cg, co = _ret_cols(h)
            q = _rot(x_ref[:, cq], cosv, sinv)
            k = _rot(x_ref[:, ck], cosv, sinv) * (RET_DK ** -0.5)
            v = x_ref[:, cv]
            qdv, kdv, dm = qd_ref[h], kd_ref[h], in_ref[h]
            st = st_ref[h]
            o = ro_ref[:, co]
            gv = g_ref[:, co]
            mu = jnp.mean(o, axis=-1, keepdims=True)
            oc = o - mu
            rstd = lax.rsqrt(jnp.mean(oc * oc, axis=-1, keepdims=True) + GN_EPS)
            ohat = oc * rstd
            rn = ohat * gv + b_ref[:, co]
            rg = x_ref[:, cg]
            sg = jax.nn.sigmoid(rg)
            dgt = dg_ref[:, co]
            drn = dgt * (rg * sg)
            dx_ref[:, cg] = (dgt * rn * (sg * (1.0 + rg * (1.0 - sg)))).astype(BF16)
            gg_ref[:, co] += jnp.sum(drn * ohat, axis=0, keepdims=True)
            gb_ref[:, co] += jnp.sum(drn, axis=0, keepdims=True)
            dohat = drn * gv
            do = rstd * (dohat - jnp.mean(dohat, axis=-1, keepdims=True)
                         - ohat * jnp.mean(dohat * ohat, axis=-1, keepdims=True))
            gs = gs_scr[h]
            s = _dot(q, k, NT) * dm
            dsr = _dot(do, v, NT) * dm
            dq = _dot(dsr, k, NN) + _dot(do, st, NT) * qdv
            dk = _dot(dsr, q, TN) + _dot(v, gs, NT) * kdv
            dv = _dot(s, do, TN) + _dot(k * kdv, gs, NN)
            gs_scr[h] = gs * cd_ref[h, :, :1] + _dot(q * qdv, do, TN)
            dx_ref[:, cq] = _rot_t(dq, cosv, sinv).astype(BF16)
            dx_ref[:, ck] = (_rot_t(dk, cosv, sinv) * (RET_DK ** -0.5)).astype(BF16)
            dx_ref[:, cv] = dv.astype(BF16)

    rev = lambda t: last - t
    vblk = pl.BlockSpec((CHUNK, RET_VW), lambda t: (rev(t), 0))
    vspec = pl.BlockSpec((1, RET_VW), lambda t: (0, 0))
    kw = dict(name="ret_bwd", grid=(N_CHUNK,),
              in_specs=_ret_specs(rev) + [vblk, pl.BlockSpec((RET_HEADS, None, RET_DK, RET_DV),
                                                             lambda t: (0, rev(t), 0, 0)), vblk],
              out_specs=(pl.BlockSpec((CHUNK, RET_COLS), lambda t: (rev(t), 0)), vspec, vspec),
              out_shape=(jax.ShapeDtypeStruct((S, RET_COLS), BF16), jax.ShapeDtypeStruct((1, RET_VW), F32),
                         jax.ShapeDtypeStruct((1, RET_VW), F32)),
              scratch_shapes=[pltpu.VMEM((RET_HEADS, RET_DK, RET_DV), F32)])
    args = (proj, cos, sin, inner, qd, kd, cd, gn_g, gn_b, ro, states, dgated)
    if comm is not None:
        return _carry(body, comm, **kw)(*args)
    return _pcall(body, compiler_params=_params(("arbitrary",)), **kw)(*args)


def _bucket_tables():
    qi = np.arange(ATT_BLK)[:, None]
    kj = np.arange(2 * ATT_BLK)[None, :]
    m = ATT_BLK + qi - kj
    out = []
    for win, dil in ATT_GROUPS:
        w = win // dil
        dist = (np.clip(m, 0, w) * dil).astype(np.int32)
        max_exact = N_BUCKETS // 2
        d_f = np.maximum(dist, 1).astype(np.float32)
        large = max_exact + (np.log(d_f / np.float32(max_exact)) / np.float32(math.log(MAX_DIST / max_exact))
                             * np.float32(N_BUCKETS - max_exact)).astype(np.int32)
        large = np.minimum(large, N_BUCKETS - 1)
        out.append(np.where(dist < max_exact, dist, large).astype(np.int32))
    return np.stack(out)


def _bias_build(rel_bias, buckets):
    def body(tab_ref, bk_ref, o_ref):
        hh = pl.program_id(0)
        bk = bk_ref[...]
        acc = jnp.zeros((ATT_BLK, 2 * ATT_BLK), F32)
        for b in range(N_BUCKETS):
            acc = jnp.where(bk == b, tab_ref[b, hh], acc)
        o_ref[...] = acc

    nh = len(ATT_GROUPS) * ATT_HG
    return _pcall(body, name="bias_build", grid=(nh,),
                  in_specs=[pl.BlockSpec(memory_space=pltpu.SMEM),
                            pl.BlockSpec((None, ATT_BLK, 2 * ATT_BLK), lambda hh: (hh // ATT_HG, 0, 0))],
                  out_specs=pl.BlockSpec((None, ATT_BLK, 2 * ATT_BLK), lambda hh: (hh, 0, 0)),
                  out_shape=jax.ShapeDtypeStruct((nh, ATT_BLK, 2 * ATT_BLK), F32),
                  compiler_params=_params(("parallel",)))(rel_bias, buckets)


def _bias_grad(ds_sum, buckets):
    def body(ds_ref, bk_ref, o_ref):
        bk = bk_ref[...]
        ds = ds_ref[...]
        rows = lax.broadcasted_iota(jnp.int32, (N_BUCKETS, 128), 0)
        acc = jnp.zeros((N_BUCKETS, 128), F32)
        for b in range(N_BUCKETS):
            acc = jnp.where(rows == b, jnp.sum(jnp.where(bk == b, ds, 0.0)), acc)
        o_ref[...] = acc

    nh = len(ATT_GROUPS) * ATT_HG
    return _pcall(body, name="bias_grad", grid=(nh,),
                  in_specs=[pl.BlockSpec((None, ATT_BLK, 2 * ATT_BLK), lambda hh: (hh, 0, 0)),
                            pl.BlockSpec((None, ATT_BLK, 2 * ATT_BLK), lambda hh: (hh // ATT_HG, 0, 0))],
                  out_specs=pl.BlockSpec((None, N_BUCKETS, 128), lambda hh: (hh, 0, 0)),
                  out_shape=jax.ShapeDtypeStruct((nh, N_BUCKETS, 128), F32),
                  compiler_params=_params(("parallel",)))(ds_sum, buckets)


def _att_valid(n):
    qi = lax.broadcasted_iota(jnp.int32, (ATT_BLK, 2 * ATT_BLK), 0)
    kj = lax.broadcasted_iota(jnp.int32, (ATT_BLK, 2 * ATT_BLK), 1)
    m = ATT_BLK + qi - kj
    first_key = jnp.where(n > 0, 0, ATT_BLK)
    return (m >= 0) & (m <= ATT_BLK) & (kj >= first_key)


ATT_HP = (1, 2, 2)


def _att_geometry(gi):
    _, dil = ATT_GROUPS[gi]
    return dil, S // dil // ATT_BLK, ATT_HP[gi]


def _blk(dil, r, n):
    if dil == 1:
        return pl.ds(n * ATT_BLK, ATT_BLK)
    return pl.ds(r + n * ATT_BLK * dil, ATT_BLK, stride=dil)


def _slab_specs(gi):
    _, _, hp = _att_geometry(gi)
    per = ATT_HG // hp
    return [pl.BlockSpec((hp, S, ATT_DH), lambda g, r, part=part: ((3 * gi + part) * per + g, 0, 0))
            for part in range(3)]


def _head_specs(gi, count):
    _, _, hp = _att_geometry(gi)
    return [pl.BlockSpec((hp, S, ATT_DH), lambda g, r: (g, 0, 0))] * count


def _bias_spec(gi):
    _, _, hp = _att_geometry(gi)
    return pl.BlockSpec((hp, ATT_BLK, 2 * ATT_BLK), lambda g, r: (gi * (ATT_HG // hp) + g, 0, 0))


def _att_fwd(slabs, bias, gi, comm=None):
    dil, nb, hp = _att_geometry(gi)
    scale = ATT_DH ** -0.5

    def body(q_ref, k_ref, v_ref, bias_ref, o_ref, l_ref):
        r = pl.program_id(1)
        for n in range(nb):
            valid = _att_valid(n)
            prev = _blk(dil, r, max(n - 1, 0))
            cur = _blk(dil, r, n)
            for h in range(hp):
                kk = jnp.concatenate([k_ref[h, prev, :], k_ref[h, cur, :]], axis=0)
                vv = jnp.concatenate([v_ref[h, prev, :], v_ref[h, cur, :]], axis=0)
                s = _dot(q_ref[h, cur, :], kk, NT) * scale + bias_ref[h]
                s = jnp.where(valid, s, -1e30)
                mx = jnp.max(s, axis=-1, keepdims=True)
                e = jnp.exp(s - mx)
                den = jnp.sum(e, axis=-1, keepdims=True)
                o_ref[h, cur, :] = _dot(e / den, vv, NN)
                l_ref[h, cur, :] = jnp.broadcast_to(mx + jnp.log(den), (ATT_BLK, ATT_DH))

    osh = jax.ShapeDtypeStruct((ATT_HG, S, ATT_DH), F32)
    kw = dict(name=f"att_fwd{gi}", grid=(ATT_HG // hp, dil), in_specs=_slab_specs(gi) + [_bias_spec(gi)],
              out_specs=tuple(_head_specs(gi, 2)), out_shape=(osh, osh))
    if comm is not None:
        return _carry(body, comm, **kw)(slabs, slabs, slabs, bias)
    return _pcall(body, compiler_params=_params(("parallel", "arbitrary")), **kw)(slabs, slabs, slabs, bias)


def _att_bwd(slabs, bias, o, lse, do, dlse, gi, comm=None):
    dil, nb, hp = _att_geometry(gi)
    per = ATT_HG // hp
    scale = ATT_DH ** -0.5
    wh = hp * ATT_DH
    wide = lambda t: jnp.concatenate([t, t], axis=1)

    def body(q_ref, k_ref, v_ref, bias_ref, o_ref, l_ref, do_ref, dl_ref, dq_ref, dk_ref, dv_ref, ds_ref):
        r = pl.program_id(1)

        @pl.when(r == 0)
        def _():
            ds_ref[...] = jnp.zeros_like(ds_ref)

        for h in range(hp):
            sl = slice(h * ATT_DH, (h + 1) * ATT_DH)
            carry_k = carry_v = None
            for n in range(nb):
                valid = _att_valid(n)
                prev = _blk(dil, r, max(n - 1, 0))
                cur = _blk(dil, r, n)
                q = q_ref[h, cur, :]
                kk = jnp.concatenate([k_ref[h, prev, :], k_ref[h, cur, :]], axis=0)
                vv = jnp.concatenate([v_ref[h, prev, :], v_ref[h, cur, :]], axis=0)
                dov = do_ref[h, cur, :]
                s = _dot(q, kk, NT) * scale + bias_ref[h]
                p = jnp.where(valid, jnp.exp(s - wide(l_ref[h, cur, :])), 0.0)
                dp = _dot(dov, vv, NT)
                delta = jnp.sum(dov * o_ref[h, cur, :], axis=-1, keepdims=True)
                ds = p * (dp - delta + wide(dl_ref[h, cur, :]))
                ds_ref[h] += ds
                out_rows = pl.ds(n * ATT_BLK, ATT_BLK)
                dq_ref[out_rows, sl] = (_dot(ds, kk, NN) * scale).astype(BF16)
                dkk = _dot(ds, q, TN) * scale
                dvv = _dot(p, dov, TN)
                if n > 0:
                    before = pl.ds((n - 1) * ATT_BLK, ATT_BLK)
                    dk_ref[before, sl] = (carry_k + dkk[:ATT_BLK]).astype(BF16)
                    dv_ref[before, sl] = (carry_v + dvv[:ATT_BLK]).astype(BF16)
                carry_k, carry_v = dkk[ATT_BLK:], dvv[ATT_BLK:]
            last = pl.ds((nb - 1) * ATT_BLK, ATT_BLK)
            dk_ref[last, sl] = carry_k.astype(BF16)
            dv_ref[last, sl] = carry_v.astype(BF16)

    out_spec = pl.BlockSpec((S // dil, wh), lambda g, r: (0, r * per + g))
    osh = jax.ShapeDtypeStruct((S // dil, dil * AW), BF16)
    kw = dict(name=f"att_bwd{gi}", grid=(per, dil), in_specs=_slab_specs(gi) + [_bias_spec(gi)] + _head_specs(gi, 4),
              out_specs=(out_spec, out_spec, out_spec,
                         pl.BlockSpec((hp, ATT_BLK, 2 * ATT_BLK), lambda g, r: (g, 0, 0))),
              out_shape=(osh, osh, osh, jax.ShapeDtypeStruct((ATT_HG, ATT_BLK, 2 * ATT_BLK), F32)))
    args = (slabs, slabs, slabs, bias, o, lse, do, dlse)
    if comm is not None:
        return _carry(body, comm, **kw)(*args)
    return _pcall(body, compiler_params=_params(("arbitrary", "arbitrary")), **kw)(*args)


AW = ATT_HG * ATT_DH


def _mix_weights(l0, l1, l2):
    mx = jnp.maximum(jnp.maximum(l0, l1), l2)
    e0, e1, e2 = jnp.exp(l0 - mx), jnp.exp(l1 - mx), jnp.exp(l2 - mx)
    den = e0 + e1 + e2
    return e0 / den, e1 / den, e2 / den


def _heads_spec():
    return pl.BlockSpec((ATT_HG, TR, ATT_DH), lambda i: (0, i, 0))


def _mix_fwd(os_, ls):
    def body(o0, o1, o2, l0, l1, l2, att_ref):
        for h in range(ATT_HG):
            w0, w1, w2 = _mix_weights(l0[h], l1[h], l2[h])
            att_ref[:, h * ATT_DH:(h + 1) * ATT_DH] = (w0 * o0[h] + w1 * o1[h] + w2 * o2[h]).astype(BF16)

    return _pcall(body, name="mix_fwd", grid=(S // TR,), in_specs=[_heads_spec()] * 6, out_specs=_row_spec(AW),
                  out_shape=jax.ShapeDtypeStruct((S, AW), BF16), compiler_params=_params(("parallel",)))(*os_, *ls)


def _mix_bwd(os_, ls, datt):
    def body(o0, o1, o2, l0, l1, l2, da_ref, d0, d1, d2, e0, e1, e2):
        for h in range(ATT_HG):
            ws = _mix_weights(l0[h], l1[h], l2[h])
            da = da_ref[:, h * ATT_DH:(h + 1) * ATT_DH]
            dws = []
            for o_ref, w, d_ref in zip((o0, o1, o2), ws, (d0, d1, d2)):
                d_ref[h] = w * da
                dws.append(jnp.broadcast_to(jnp.sum(da * o_ref[h], axis=-1, keepdims=True), (TR, ATT_DH)))
            tot = ws[0] * dws[0] + ws[1] * dws[1] + ws[2] * dws[2]
            for w, dw, e_ref in zip(ws, dws, (e0, e1, e2)):
                e_ref[h] = w * (dw - tot)

    o = jax.ShapeDtypeStruct((ATT_HG, S, ATT_DH), F32)
    return _pcall(body, name="mix_bwd", grid=(S // TR,), in_specs=[_heads_spec()] * 6 + [_row_spec(AW)],
                  out_specs=(_heads_spec(),) * 6, out_shape=(o,) * 6,
                  compiler_params=_params(("parallel",)))(*os_, *ls, datt)


def _ada_fwd(c_all, w_sh, b_sl):
    def body(c_ref, w_ref, b_ref, o_ref):
        cv = c_ref[...]
        o_ref[...] = _dot(cv * jax.nn.sigmoid(cv), w_ref[...], NN) + b_ref[...]

    return _pcall(body, name="ada_fwd", out_shape=jax.ShapeDtypeStruct((N_DEV, w_sh.shape[1]), F32),
                  compiler_params=_params())(c_all, w_sh, b_sl)


def _ada_bwd(c_all, dm_sl):
    def body(c_ref, d_ref, o_ref):
        cv = c_ref[...]
        o_ref[...] = _dot(cv * jax.nn.sigmoid(cv), d_ref[...], TN)

    return _pcall(body, name="ada_bwd", out_shape=jax.ShapeDtypeStruct((D, dm_sl.shape[1]), F32),
                  compiler_params=_params())(c_all, dm_sl)


N_MOD = 6


def _sum_small(gathered):
    n = len(gathered)

    def body(*refs):
        ins, (gb_ref, dm_ref), outs = refs[:n], refs[n:n + 2], refs[n + 2:]

        def total(r):
            acc = r[0]
            for e in range(1, N_DEV):
                acc = acc + r[e]
            return acc

        for i in range(N_MOD):
            cols = slice(i * D, (i + 1) * D)
            gb_ref[:, cols] = total(ins[i])
            for e in range(N_DEV):
                dm_ref[e:e + 1, cols] = ins[i][e]
        for r, o_ref in zip(ins[N_MOD:], outs):
            o_ref[...] = total(r)

    shapes = (jax.ShapeDtypeStruct((1, N_MOD * D), F32), jax.ShapeDtypeStruct((N_DEV, N_MOD * D), F32),
              *[jax.ShapeDtypeStruct(g.shape[1:], F32) for g in gathered[N_MOD:]])
    res = _pcall(body, name="sum_small", out_shape=shapes, compiler_params=_params())(*gathered)
    return res[0], res[1], res[2:]


def _row_tile(m, n):
    t = max(8, min(m, (1 << 19) // n // 8 * 8))
    while m % t:
        t -= 8
    return t


def _pair_sum(full, recv, sel, name):
    _, _, m, n = full.shape
    t = _row_tile(m, n)

    def body(sel_ref, a_ref, b_ref, o_ref):
        o_ref[...] = (a_ref[...].astype(F32) + b_ref[...].astype(F32)).astype(o_ref.dtype)

    gs = pltpu.PrefetchScalarGridSpec(
        num_scalar_prefetch=1, grid=(4, m // t),
        in_specs=[pl.BlockSpec((None, None, t, n), lambda q, i, s: (q, s[0], i, 0)),
                  pl.BlockSpec((None, t, n), lambda q, i, s: (q, i, 0))],
        out_specs=pl.BlockSpec((None, t, n), lambda q, i, s: (q, i, 0)))
    return _pcall(body, name=name, grid_spec=gs, out_shape=jax.ShapeDtypeStruct((4, m, n), full.dtype),
                  compiler_params=_params(("parallel", "parallel")))(sel, full, recv)


def _chip_sum(part, recv, sel, name):
    _, m, n = part.shape
    t = _row_tile(m, n)

    def body(sel_ref, a_ref, r_ref, o_ref):
        o_ref[...] = ((a_ref[...].astype(F32) + r_ref[0].astype(F32)) + r_ref[1].astype(F32)) + r_ref[2].astype(F32)

    gs = pltpu.PrefetchScalarGridSpec(
        num_scalar_prefetch=1, grid=(m // t,),
        in_specs=[pl.BlockSpec((None, t, n), lambda i, s: (s[0], i, 0)),
                  pl.BlockSpec((3, t, n), lambda i, s: (0, i, 0))],
        out_specs=pl.BlockSpec((t, n), lambda i, s: (i, 0)))
    return _pcall(body, name=name, grid_spec=gs, out_shape=jax.ShapeDtypeStruct((m, n), F32),
                  compiler_params=_params(("parallel",)))(sel, part, recv)


def _adamw_math(w, g, m, v):
    nm = ADAM_B1 * m + (1.0 - ADAM_B1) * g
    nv = ADAM_B2 * v + (1.0 - ADAM_B2) * (g * g)
    m_hat = nm / (1.0 - ADAM_B1 ** ADAM_STEP)
    v_hat = nv / (1.0 - ADAM_B2 ** ADAM_STEP)
    return -ADAM_LR * (m_hat / (jnp.sqrt(v_hat) + ADAM_EPS) + ADAM_WD * w), nm, nv


def _adamw(w, g, m, v, name):
    _, rows, cols = w.shape
    t = _row_tile(rows, cols)

    def body(w_ref, g_ref, m_ref, v_ref, d_ref, nm_ref, nv_ref):
        d_ref[...], nm_ref[...], nv_ref[...] = _adamw_math(w_ref[...], g_ref[...], m_ref[...], v_ref[...])

    spec3 = pl.BlockSpec((None, t, cols), lambda i: (0, i, 0))
    spec2 = pl.BlockSpec((t, cols), lambda i: (i, 0))
    o = jax.ShapeDtypeStruct(w.shape, F32)
    return _pcall(body, name=name, grid=(rows // t,), in_specs=[spec3, spec2, spec3, spec3], out_specs=(spec3,) * 3,
                  out_shape=(o, o, o), compiler_params=_params(("parallel",)))(w, g, m, v)


def _adamw_small(ws, gs, ms, vs):
    n = len(ws)

    def body(*refs):
        for i in range(n):
            w_ref, g_ref, m_ref, v_ref = (refs[k * n + i] for k in range(4))
            d, nm, nv = _adamw_math(w_ref[...], g_ref[...], m_ref[...], v_ref[...])
            refs[4 * n + i][...] = d
            refs[5 * n + i][...] = nm
            refs[6 * n + i][...] = nv

    shapes = tuple(jax.ShapeDtypeStruct(w.shape, F32) for w in ws)
    res = _pcall(body, name="adamw_small", out_shape=shapes * 3, compiler_params=_params())(*ws, *gs, *ms, *vs)
    return res[:n], res[n:2 * n], res[2 * n:]


def _mesh_pos():
    return lax.axis_index("x"), lax.axis_index("y"), lax.axis_index("c")


class _Gather:
    def __init__(self, arrs, relay=False):
        self.relay = relay
        self.ins = list(arrs)
        na = self.na = len(arrs)
        self.out_shape = tuple(jax.ShapeDtypeStruct((N_DEV,) + a.shape, a.dtype) for a in arrs)
        self.sems = [pltpu.SemaphoreType.DMA((7 * na,)), pltpu.SemaphoreType.DMA((7 * na,)),
                     pltpu.SemaphoreType.DMA((na,))]

    def _copies(self, ins, outs, sems):
        send_sems, recv_sems, local_sems = sems
        x, y, c = _mesh_pos()
        me, sibling = (x, y, c), (x, y, 1 - c)
        chips = [(1 - x, y), (x, 1 - y), (1 - x, 1 - y)]

        def slot(p):
            return 4 * p[0] + 2 * p[1] + p[2]

        def copy(a, k, block, to, src=None):
            dst = outs[a].at[slot(block)]
            return pltpu.make_async_remote_copy(
                src_ref=dst if src is None else src, dst_ref=dst, send_sem=send_sems.at[7 * a + k],
                recv_sem=recv_sems.at[7 * a + k], device_id=to, device_id_type=MESH)

        mine = [pltpu.make_async_copy(ins[a], outs[a].at[slot(me)], local_sems.at[a]) for a in range(self.na)]
        direct = chips[:2] if self.relay else chips
        first = []
        for a in range(self.na):
            first.append(copy(a, 0, me, sibling, src=ins[a]))
            first += [copy(a, 1 + j, me, (*chip, c), src=ins[a]) for j, chip in enumerate(direct)]
        return me, sibling, chips, c, copy, mine, first

    def start(self, ins, outs, sems):
        *_, mine, first = self._copies(ins, outs, sems)
        for cp in mine + first:
            cp.start()

    def finish(self, ins, outs, sems):
        me, sibling, chips, c, copy, mine, first = self._copies(ins, outs, sems)
        x, y = me[0], me[1]
        passed = []
        for j, chip in enumerate(chips):
            for a in range(self.na):
                if self.relay and j == 2:
                    owner = ((x + 1 - c) % 2, (y + c) % 2, c)
                    cp = copy(a, 3, owner, ((x + c) % 2, (y + 1 - c) % 2, c))
                    cp.start()
                    passed.append(cp)
                copy(a, 1 + j, (*chip, c), me).wait_recv()
                cp = copy(a, 4 + j, (*chip, c), sibling)
                cp.start()
                passed.append(cp)
        for a in range(self.na):
            copy(a, 0, sibling, me).wait_recv()
            for j, chip in enumerate(chips):
                copy(a, 4 + j, (*chip, 1 - c), me).wait_recv()
        for cp in first + passed:
            cp.wait_send()
        for cp in mine:
            cp.wait()


class _ExchangeCore:
    def __init__(self, fulls):
        self.ins = list(fulls)
        self.out_shape = tuple(jax.ShapeDtypeStruct((4,) + f.shape[2:], f.dtype) for f in fulls)
        self.sems = [pltpu.SemaphoreType.DMA((4 * len(fulls),)), pltpu.SemaphoreType.DMA((4 * len(fulls),))]

    def _copies(self, ins, outs, sems):
        send_sems, recv_sems = sems
        x, y, c = _mesh_pos()
        return [pltpu.make_async_remote_copy(
            src_ref=ins[a].at[q, 1 - c], dst_ref=outs[a].at[q], send_sem=send_sems.at[4 * a + q],
            recv_sem=recv_sems.at[4 * a + q], device_id=(x, y, 1 - c), device_id_type=MESH)
            for a in range(len(self.ins)) for q in range(4)]

    def start(self, ins, outs, sems):
        for cp in self._copies(ins, outs, sems):
            cp.start()

    def finish(self, ins, outs, sems):
        for cp in self._copies(ins, outs, sems):
            cp.wait()


class _ExchangeChip:
    def __init__(self, parts):
        self.ins = list(parts)
        self.out_shape = tuple(jax.ShapeDtypeStruct((3,) + p.shape[1:], p.dtype) for p in parts)
        self.sems = [pltpu.SemaphoreType.DMA((3 * len(parts),)), pltpu.SemaphoreType.DMA((3 * len(parts),))]

    def _copies(self, ins, outs, sems):
        send_sems, recv_sems = sems
        x, y, c = _mesh_pos()
        chips = [(1 - x, y), (x, 1 - y), (1 - x, 1 - y)]
        return [pltpu.make_async_remote_copy(
            src_ref=ins[a].at[2 * px + py], dst_ref=outs[a].at[j], send_sem=send_sems.at[3 * a + j],
            recv_sem=recv_sems.at[3 * a + j], device_id=(px, py, c), device_id_type=MESH)
            for a in range(len(self.ins)) for j, (px, py) in enumerate(chips)]

    def start(self, ins, outs, sems):
        for cp in self._copies(ins, outs, sems):
            cp.start()

    def finish(self, ins, outs, sems):
        for cp in self._copies(ins, outs, sems):
            cp.wait()


HBM_ONLY = pl.BlockSpec(memory_space=pltpu.HBM)
SEM_SPEC = pl.BlockSpec(memory_space=pltpu.SEMAPHORE)
SIDE_EFFECT = pltpu.SideEffectType.DATAFLOW_SIDE_EFFECTING


def _chip_copies(p_refs, land_refs, send_sems, recv_sems):
    x, y, c = _mesh_pos()
    return [pltpu.make_async_remote_copy(
        src_ref=p_refs[a].at[2 * px + py], dst_ref=land_refs[a].at[j], send_sem=send_sems.at[3 * a + j],
        recv_sem=recv_sems.at[3 * a + j], device_id=(px, py, c), device_id_type=MESH)
        for a in range(len(p_refs)) for j, (px, py) in enumerate([(1 - x, y), (x, 1 - y), (1 - x, 1 - y)])]


def _chip_exchange_start(parts, name):
    n = len(parts)
    lands = [lax.empty((3,) + p.shape[1:], p.dtype) for p in parts]

    def body(*refs):
        p_refs, land_refs, (send_sems, recv_sems) = refs[:n], refs[n:2 * n], refs[2 * n:2 * n + 2]
        for cp in _chip_copies(p_refs, land_refs, send_sems, recv_sems):
            cp.start()
        token = refs[-1]
        token[...] = jnp.zeros_like(token)

    hbm = lambda t: pltpu.HBM(t.shape, t.dtype)
    res = pl.pallas_call(
        body, name=name,
        out_shape=(pltpu.SemaphoreType.DMA((3 * n,)), pltpu.SemaphoreType.DMA((3 * n,)), *[hbm(t) for t in parts + lands],
                   jax.ShapeDtypeStruct((8, 128), F32)),
        in_specs=(HBM_ONLY,) * (2 * n),
        out_specs=(SEM_SPEC, SEM_SPEC, *[HBM_ONLY] * (2 * n), pl.BlockSpec(memory_space=pltpu.VMEM)),
        input_output_aliases={i: 2 + i for i in range(2 * n)},
        compiler_params=pltpu.CompilerParams(has_side_effects=SIDE_EFFECT))(
        *[pltpu.with_memory_space_constraint(t, pltpu.HBM) for t in parts + lands])
    return (res[0], res[1], list(res[2:2 + n]), list(res[2 + n:2 + 2 * n])), res[-1]


def _chip_exchange_wait(in_flight, after, name):
    send_sems, recv_sems, parts, lands = in_flight
    n = len(parts)

    def body(*refs):
        p_refs, land_refs, (send_sems, recv_sems) = refs[:n], refs[n:2 * n], refs[2 * n:2 * n + 2]
        for cp in _chip_copies(p_refs, land_refs, send_sems, recv_sems):
            cp.wait_send()
            cp.wait_recv()

    res = pl.pallas_call(
        body, name=name, out_shape=tuple(pltpu.HBM(t.shape, t.dtype) for t in parts + lands),
        in_specs=(*[HBM_ONLY] * (2 * n), SEM_SPEC, SEM_SPEC, pl.BlockSpec(memory_space=pl.ANY)),
        out_specs=(HBM_ONLY,) * (2 * n), input_output_aliases={i: i for i in range(2 * n)},
        compiler_params=pltpu.CompilerParams(has_side_effects=SIDE_EFFECT))(*parts, *lands, send_sems, recv_sems, after)
    return list(res[:n]), list(res[n:])


def _reduce_sums(fulls, recv_core, core, tag):
    return [_pair_sum(f, r, core, f"rs_pair_{tag}{i}") for i, (f, r) in enumerate(zip(fulls, recv_core))]


def _local_step(x, tgt, mods, w_in_t, shards, small, chip, core):
    sh1, sc1, g1, sh2, sc2, g2 = mods
    norm1_g, rel_bias, gn_g, gn_b, norm2_g, norm_f_g = small
    tables = _ret_tables()
    buckets = jnp.asarray(_bucket_tables())

    h1 = _norm_mod_fwd(x, norm1_g, sh1, sc1, "norm1_fwd")
    proj, slabs, gathered = _proj(h1, w_in_t, _Gather(shards[:4]))
    w_ret_out, w_att_out, w_o, w_ff1 = (_from_slots(g, ax) for g, ax in zip(gathered, BIG_AXES[1:5]))
    (gated, ro, states), (w_ff2_8,) = _ret_fwd(proj, tables, gn_g, gn_b, comm=_Gather(shards[4:]))
    w_ff2 = _from_slots(w_ff2_8, BIG_AXES[5])
    bias = _bias_build(rel_bias, buckets)
    outs, lses = [], []
    for gi in range(len(ATT_GROUPS)):
        o, l = _att_fwd(slabs, bias, gi)
        outs.append(o)
        lses.append(l)
    att = _mix_fwd(outs, lses)
    ret_out = _mm(gated, w_ret_out, 'nn', tm=S, tn=256, tk=2048, name="ret_out")
    att_out = _mm(att, w_att_out, 'nn', tm=S, tn=512, tk=AW, name="att_out")
    merged = _merge_fwd(proj, ret_out, att_out)
    mixo, x1 = _mm(merged, w_o, 'nn', tm=S, tn=256, tk=D, name="w_o", res=x, gvec=g1)
    h2 = _norm_mod_fwd(x1, norm2_g, sh2, sc2, "norm2_fwd")
    u, act = _mm(h2, w_ff1, 'nn', tm=S, tn=512, tk=D, name="ff1", relu2=True)
    f, x2 = _mm(act, w_ff2, 'nn', tm=1024, tn=512, tk=D_FF, name="ff2", res=x1, gvec=g2)
    loss, dx2, g_normf, df, dg2 = _final_loss(x2, tgt, norm_f_g, f, g2)

    gw_ff2 = _mm(act, df, 'tn', tm=512, tn=D, tk=S, name="gw_ff2", out_dtype=BF16)
    du = _mm(df, w_ff2, 'nt', tm=S, tn=512, tk=D, name="d_act", out_dtype=BF16, relu2_of=u)
    gw_ff1 = _mm(h2, du, 'tn', tm=D, tn=512, tk=S, name="gw_ff1", out_dtype=BF16)
    fulls_a = [_to_slots(g, ax) for g, ax in zip((gw_ff1, gw_ff2), BIG_AXES[4:])]
    dh2, recv_core_a = _mm(du, w_ff1, 'nt', tm=1024, tn=1024, tk=2048, name="dh2", comm=_ExchangeCore(fulls_a))
    parts_a = _reduce_sums(fulls_a, recv_core_a, core, "a")
    flight_a, token_a = _chip_exchange_start(parts_a, "rs_a_start")
    dx1, dsc2, dsh2, g_norm2, dmixo, dg1 = _norm_mod_bwd(x1, norm2_g, sc2, dh2, dx2, "norm2_bwd", gate=(mixo, g1))

    gw_o = _mm(merged, dmixo, 'tn', tm=D, tn=512, tk=S, name="gw_o", out_dtype=BF16, after=token_a)
    dmerged = _mm(dmixo, w_o, 'nt', tm=S, tn=512, tk=D, name="dmerged")
    d_ret_out, d_att_out, dga, dgb = _merge_bwd(proj, ret_out, att_out, dmerged)
    gw_ret_out = _mm(gated, d_ret_out, 'tn', tm=512, tn=D, tk=S, name="gw_ret_out", out_dtype=BF16)
    gw_att_out = _mm(att, d_att_out, 'tn', tm=AW, tn=D, tk=S, name="gw_att_out", out_dtype=BF16)
    fulls_b = [_to_slots(g, ax) for g, ax in zip((gw_ret_out, gw_att_out, gw_o), BIG_AXES[1:4])]
    dgated, recv_core_b = _mm(d_ret_out, w_ret_out, 'nt', tm=S, tn=512, tk=D, name="dgated",
                              comm=_ExchangeCore(fulls_b))
    parts_b = _reduce_sums(fulls_b, recv_core_b, core, "b")
    flight_b, token_b = _chip_exchange_start(parts_b, "rs_b_start")
    datt = _mm(d_att_out, w_att_out, 'nt', tm=S, tn=AW, tk=D, name="datt", after=token_b)
    mix_grads = _mix_bwd(outs, lses, datt)
    datt_parts, ds_sums = [], []
    for gi in range(len(ATT_GROUPS)):
        dq, dk, dv, ds_sum = _att_bwd(slabs, bias, outs[gi], lses[gi], mix_grads[gi], mix_grads[3 + gi], gi)
        datt_parts += [dq.reshape(S, AW), dk.reshape(S, AW), dv.reshape(S, AW)]
        ds_sums.append(ds_sum)
    g_bias = _bias_grad(jnp.concatenate(ds_sums, axis=0), buckets)[:, :, 0].T.reshape(1, -1)
    dret, g_gn_g, g_gn_b = _ret_bwd(proj, tables, gn_g, gn_b, ro, states, dgated)
    parts_a, recv_chip_a = _chip_exchange_wait(flight_a, dret, "rs_a_wait")
    parts_b, recv_chip_b = _chip_exchange_wait(flight_b, dret, "rs_b_wait")
    red_a = [_chip_sum(p, r, chip, f"rs_sum_a{i}") for i, (p, r) in enumerate(zip(parts_a, recv_chip_a))]
    red_b = [_chip_sum(p, r, chip, f"rs_sum_b{i}") for i, (p, r) in enumerate(zip(parts_b, recv_chip_b))]
    dproj = jnp.concatenate([dret] + datt_parts + [dga, dgb], axis=1)
    in_flight, token = [], None
    for half in range(2):
        h1_half = h1[:, half * (D // 2):(half + 1) * (D // 2)]
        gw_half = _mm(dproj, h1_half, 'tn', tm=512, tn=D // 2, tk=S, name=f"gw_in{half}", out_dtype=BF16, after=token)
        full_in = [_to_slots(gw_half, 0)]
        recv_core_in = _run_comm(_ExchangeCore(full_in), f"rs_core_in{half}")
        part_in = _reduce_sums(full_in, recv_core_in, core, f"c{half}")
        flight, token = _chip_exchange_start(part_in, f"rs_in{half}_start")
        in_flight.append(flight)
    dh1 = _mm(dproj, w_in_t, 'nn', tm=1024, tn=1024, tk=2560, name="dh1", after=token)
    gx, dsc1, dsh1, g_norm1 = _norm_mod_bwd(x, norm1_g, sc1, dh1, dx1, "norm1_bwd")

    dmod = [dsh1, dsc1, dg1, dsh2, dsc2, dg2]
    small_g = [g_norm1, g_bias, g_gn_g, g_gn_b, g_norm2, g_normf]
    return loss, gx, in_flight, red_b + red_a, small_g, dmod


def _to_slots(g, axis):
    if axis == 0:
        return g.reshape(4, 2, g.shape[0] // N_DEV, g.shape[1])
    return g.reshape(g.shape[0], N_DEV, g.shape[1] // N_DEV).transpose(1, 0, 2).reshape(4, 2, g.shape[0], -1)


def _from_slots(w8, axis):
    if axis == 0:
        return w8.reshape(-1, w8.shape[2])
    return w8.transpose(1, 0, 2).reshape(w8.shape[1], -1)


BIG_AXES = (1, 0, 1, 0, 1, 0)


def kernel(x, c, w_ada, b_ada, norm1_g, w_in, rel_bias, ret_gn_g, ret_gn_b, w_ret_out, w_att_out, w_o, norm2_g, w_ff1, w_ff2, norm_f_g, loss_target, m_w_ada, m_b_ada, m_norm1_g, m_w_in, m_rel_bias, m_ret_gn_g, m_ret_gn_b, m_w_ret_out, m_w_att_out, m_w_o, m_norm2_g, m_w_ff1, m_w_ff2, m_norm_f_g, v_w_ada, v_b_ada, v_norm1_g, v_w_in, v_rel_bias, v_ret_gn_g, v_ret_gn_b, v_w_ret_out, v_w_att_out, v_w_o, v_norm2_g, v_w_ff1, v_w_ff2, v_norm_f_g):
    mx, my, mc = _mesh_pos()
    dev = 4 * mx + 2 * my + mc
    chip = jnp.reshape(2 * mx + my, (1,)).astype(jnp.int32)
    core = jnp.reshape(mc, (1,)).astype(jnp.int32)
    ada_w = D * 6 // N_DEV

    w_in, m_w_in, v_w_in = (jnp.transpose(t, (0, 2, 1)) for t in (w_in, m_w_in, v_w_in))

    shards = [w[0].astype(BF16) for w in (w_in, w_ret_out, w_att_out, w_o, w_ff1, w_ff2)]
    c_all, w_in8 = _run_comm(_Gather([c, shards[0]], relay=True), "gather_c_w_in")
    c_all = c_all.reshape(N_DEV, D)
    b_sl = lax.dynamic_slice(b_ada, (0, dev * ada_w), (1, ada_w))
    (mod_all,) = _run_comm(_Gather([_ada_fwd(c_all, w_ada[0], b_sl)]), "gather_mod")
    mod = lax.dynamic_index_in_dim(mod_all, dev, axis=1, keepdims=False).reshape(6, D)
    mods = tuple(mod[i:i + 1] for i in range(6))

    small = (norm1_g, rel_bias, ret_gn_g, ret_gn_b, norm2_g, norm_f_g.reshape(1, D))
    loss, gx, in_flight, big_red, small_g, dmod = _local_step(x[0], loss_target[0], mods, w_in8.reshape(IN_COLS, D),
                                                              shards[1:], small, chip, core)

    gathered = _run_comm(_Gather(dmod + small_g + [loss]), "gather_small")
    g_b_ada, dmod_all, (g_norm1, g_bias, g_gn_g, g_gn_b, g_norm2, g_normf, loss_sum) = _sum_small(gathered)
    loss_out = loss_sum[0, 0]
    g_w_ada = _ada_bwd(c_all, lax.dynamic_slice(dmod_all, (0, dev * ada_w), (N_DEV, ada_w)))

    names = ['w_ada', 'b_ada', 'norm1_g', 'w_in', 'rel_bias', 'ret_gn_g', 'ret_gn_b', 'w_ret_out', 'w_att_out',
             'w_o', 'norm2_g', 'w_ff1', 'w_ff2', 'norm_f_g']
    ws = dict(zip(names, (w_ada, b_ada, norm1_g, w_in, rel_bias, ret_gn_g, ret_gn_b, w_ret_out, w_att_out, w_o,
                          norm2_g, w_ff1, w_ff2, norm_f_g)))
    ms = dict(zip(names, (m_w_ada, m_b_ada, m_norm1_g, m_w_in, m_rel_bias, m_ret_gn_g, m_ret_gn_b, m_w_ret_out,
                          m_w_att_out, m_w_o, m_norm2_g, m_w_ff1, m_w_ff2, m_norm_f_g)))
    vs = dict(zip(names, (v_w_ada, v_b_ada, v_norm1_g, v_w_in, v_rel_bias, v_ret_gn_g, v_ret_gn_b, v_w_ret_out,
                          v_w_att_out, v_w_o, v_norm2_g, v_w_ff1, v_w_ff2, v_norm_f_g)))
    grads = dict(w_ada=g_w_ada, w_ret_out=big_red[0], w_att_out=big_red[1], w_o=big_red[2],
                 w_ff1=big_red[3], w_ff2=big_red[4], b_ada=g_b_ada, norm1_g=g_norm1, rel_bias=g_bias,
                 ret_gn_g=g_gn_g, ret_gn_b=g_gn_b, norm2_g=g_norm2, norm_f_g=g_normf)
    delta, new_m, new_v = {}, {}, {}
    for n in ('w_ada', 'w_ret_out', 'w_att_out', 'w_o', 'w_ff1', 'w_ff2'):
        delta[n], new_m[n], new_v[n] = _adamw(ws[n], grads[n], ms[n], vs[n], "adamw_" + n)
        grads[n] = grads[n].reshape(ws[n].shape)
    small_names = ('b_ada', 'norm1_g', 'rel_bias', 'ret_gn_g', 'ret_gn_b', 'norm2_g', 'norm_f_g')
    two_d = {n: (1, ws[n].size) if ws[n].ndim == 1 else ws[n].shape for n in small_names}
    d_, m_, v_ = _adamw_small(*[[src[n].reshape(two_d[n]) for n in small_names] for src in (ws, grads, ms, vs)])
    for i, n in enumerate(small_names):
        shp = ws[n].shape
        delta[n], new_m[n], new_v[n] = d_[i].reshape(shp), m_[i].reshape(shp), v_[i].reshape(shp)
        grads[n] = grads[n].reshape(shp)

    done = lax.optimization_barrier((gx, tuple(d_), tuple(delta[n] for n in ('w_ada', 'w_ret_out', 'w_att_out', 'w_o',
                                                                               'w_ff1', 'w_ff2'))))
    halves = []
    for half, flight in enumerate(in_flight):
        (part_in,), (recv_chip_in,) = _chip_exchange_wait(flight, done[0], f"rs_in{half}_wait")
        halves.append(_chip_sum(part_in, recv_chip_in, chip, f"rs_sum_c{half}"))
    grads['w_in'] = jnp.concatenate(halves, axis=1)
    delta['w_in'], new_m['w_in'], new_v['w_in'] = _adamw(w_in, grads['w_in'], m_w_in, v_w_in, "adamw_w_in")
    grads['w_in'] = grads['w_in'].reshape(w_in.shape)
    for d in (grads, delta, new_m, new_v):
        d['w_in'] = jnp.transpose(d['w_in'], (0, 2, 1))
    return (loss_out, gx[None], *[grads[n] for n in names], *[delta[n] for n in names],
            *[new_m[n] for n in names], *[new_v[n] for n in names])
```

```python
import functools
import math

import numpy as np
import jax
import jax.numpy as jnp
from jax import lax
from jax.experimental import pallas as pl
from jax.experimental.pallas import tpu as pltpu

F32 = jnp.float32
BF16 = jnp.bfloat16
MESH = pl.DeviceIdType.MESH

N_DEV = 8
S = 2048
D = 1024
RET_HEADS = 4
RET_DK = 256
RET_DV = 512
CHUNK = 128
N_CHUNK = S // CHUNK
ATT_GROUPS = ((128, 1), (512, 4), (2048, 16))
ATT_HG = 4
ATT_DH = 128
ATT_BLK = 128
N_BUCKETS = 32
MAX_DIST = 2048
D_FF = 4096
IN_COLS = 12800
OFF_RQ, OFF_RK, OFF_RV, OFF_RG, OFF_ATT = 0, 1024, 2048, 4096, 6144
OFF_GA, OFF_GB = 6144, 7168
RMS_EPS = 1e-6
GN_EPS = 1e-5
ADAM_LR, ADAM_B1, ADAM_B2, ADAM_EPS, ADAM_WD, ADAM_STEP = 0.001, 0.9, 0.999, 1e-08, 0.01, 10
VMEM_LIMIT = 48 * 1024 * 1024


def _pcall(body, **kw):
    return pl.pallas_call(body, **kw)


def _params(sem=None):
    return pltpu.CompilerParams(dimension_semantics=sem, vmem_limit_bytes=VMEM_LIMIT)


HBM_SPEC = pl.BlockSpec(memory_space=pl.ANY)


def _carry(body, comm, *, name, grid, in_specs, out_specs, out_shape, scratch_shapes=()):
    single = not isinstance(out_specs, (tuple, list))
    o_specs = (out_specs,) if single else tuple(out_specs)
    o_shape = (out_shape,) if single else tuple(out_shape)
    n_in, n_out, n_scr = len(in_specs), len(o_specs), len(scratch_shapes)
    nci, nco = len(comm.ins), len(comm.out_shape)
    total = int(np.prod(grid))

    def wrapped(*refs):
        bounds = np.cumsum([0, n_in, nci, n_out, nco, n_scr])
        a, ci, o, co, scr = (refs[bounds[i]:bounds[i + 1]] for i in range(5))
        sems = refs[bounds[5]:]
        flat = 0
        for d, g in enumerate(grid):
            flat = flat * g + pl.program_id(d)

        @pl.when(flat == 0)
        def _():
            comm.start(ci, co, sems)

        body(*a, *o, *scr)

        @pl.when(flat == total - 1)
        def _():
            comm.finish(ci, co, sems)

    aliases = {n_in + i: n_out + o for i, o in getattr(comm, "aliases", {}).items()}
    call = _pcall(wrapped, name=name, grid=grid, in_specs=list(in_specs) + [HBM_SPEC] * nci,
                  out_specs=o_specs + (HBM_SPEC,) * nco, out_shape=o_shape + tuple(comm.out_shape),
                  scratch_shapes=list(scratch_shapes) + list(comm.sems), input_output_aliases=aliases,
                  compiler_params=_params(("arbitrary",) * len(grid)))

    def run(*args):
        res = call(*args, *comm.ins)
        own = res[0] if single else tuple(res[:n_out])
        return own, tuple(res[n_out:])

    return run


def _run_comm(comm, name):
    nci, nco = len(comm.ins), len(comm.out_shape)

    def body(*refs):
        ci, co, sems = refs[:nci], refs[nci:nci + nco], refs[nci + nco:]
        comm.start(ci, co, sems)
        comm.finish(ci, co, sems)

    return _pcall(body, name=name, in_specs=[HBM_SPEC] * nci, out_specs=(HBM_SPEC,) * nco,
                  out_shape=tuple(comm.out_shape), scratch_shapes=list(comm.sems))(*comm.ins)


def _dot(a, b, dn):
    return lax.dot_general(a.astype(BF16), b.astype(BF16), (dn, ((), ())), preferred_element_type=F32)


NN = ((1,), (0,))
NT = ((1,), (1,))
TN = ((0,), (0,))


def _mm(a, b, mode, *, tm, tn, tk, name, out_dtype=F32, res=None, gvec=None, relu2=False, relu2_of=None, comm=None,
        after=None):
    if mode == 'nn':
        (M, K), (_, N) = a.shape, b.shape
        a_spec = pl.BlockSpec((tm, tk), lambda i, j, k: (i, k))
        b_spec = pl.BlockSpec((tk, tn), lambda i, j, k: (k, j))
        dn = NN
    elif mode == 'nt':
        (M, K), (N, _) = a.shape, b.shape
        a_spec = pl.BlockSpec((tm, tk), lambda i, j, k: (i, k))
        b_spec = pl.BlockSpec((tn, tk), lambda i, j, k: (j, k))
        dn = NT
    else:
        (K, M), (_, N) = a.shape, b.shape
        a_spec = pl.BlockSpec((tk, tm), lambda i, j, k: (k, i))
        b_spec = pl.BlockSpec((tk, tn), lambda i, j, k: (k, j))
        dn = TN
    assert M % tm == 0 and N % tn == 0 and K % tk == 0, (name, M, N, K)
    nk = K // tk
    fused = res is not None
    o_spec = pl.BlockSpec((tm, tn), lambda i, j, k: (i, j))

    def body(a_ref, b_ref, *rest):
        acc_ref = rest[-1] if nk > 1 else None
        if after is not None:
            rest = rest[1:]
        if fused:
            res_ref, g_ref, o_ref, x_ref = rest[:4]
        elif relu2_of is not None:
            u_ref, o_ref = rest[:2]
        elif relu2:
            o_ref, act_ref = rest[:2]
        else:
            o_ref = rest[0]

        def finish(acc):
            if relu2_of is not None:
                acc = acc * (2.0 * jnp.maximum(u_ref[...], 0.0))
            o_ref[...] = acc.astype(o_ref.dtype)
            if fused:
                x_ref[...] = res_ref[...] + g_ref[...] * acc
            if relu2:
                r = jnp.maximum(acc, 0.0)
                act_ref[...] = (r * r).astype(BF16)

        p = _dot(a_ref[...], b_ref[...], dn)
        if nk == 1:
            finish(p)
        else:
            k = pl.program_id(2)

            @pl.when(k == 0)
            def _():
                acc_ref[...] = p

            @pl.when(k > 0)
            def _():
                acc_ref[...] += p

            @pl.when(k == nk - 1)
            def _():
                finish(acc_ref[...])

    in_specs = [a_spec, b_spec]
    args = [a, b]
    if after is not None:
        in_specs.append(pl.BlockSpec(memory_space=pl.ANY))
        args.append(after)
    out_shape = jax.ShapeDtypeStruct((M, N), out_dtype)
    out_specs = o_spec
    if fused:
        in_specs += [pl.BlockSpec((tm, tn), lambda i, j, k: (i, j)), pl.BlockSpec((1, tn), lambda i, j, k: (0, j))]
        args += [res, gvec]
        out_shape = (out_shape, jax.ShapeDtypeStruct((M, N), F32))
        out_specs = (o_spec, pl.BlockSpec((tm, tn), lambda i, j, k: (i, j)))
    elif relu2_of is not None:
        in_specs.append(pl.BlockSpec((tm, tn), lambda i, j, k: (i, j)))
        args.append(relu2_of)
    elif relu2:
        out_shape = (out_shape, jax.ShapeDtypeStruct((M, N), BF16))
        out_specs = (o_spec, pl.BlockSpec((tm, tn), lambda i, j, k: (i, j)))
    kw = dict(name=name, grid=(M // tm, N // tn, nk), in_specs=in_specs, out_specs=out_specs,
              out_shape=out_shape, scratch_shapes=[pltpu.VMEM((tm, tn), F32)] if nk > 1 else [])
    if comm is not None:
        return _carry(body, comm, **kw)(*args)
    return _pcall(body, compiler_params=_params(("parallel", "parallel", "arbitrary")), **kw)(*args)


PROJ_TN = 512
ATT_T0, ATT_T1 = 6144 // PROJ_TN, 10752 // PROJ_TN
N_SLABS = (ATT_T1 - ATT_T0) * 4
MAIN_COLS = IN_COLS - (ATT_T1 - ATT_T0) * PROJ_TN


def _proj(h1, w_in_t, after):
    nj = IN_COLS // PROJ_TN

    def body(a_ref, b_ref, after_ref, main_ref, slab_ref):
        j = pl.program_id(1)
        is_att = (j >= ATT_T0) & (j < ATT_T1)
        chunks = [pl.ds(c * 512, 512) for c in range(S // 512)]

        @pl.when(jnp.logical_not(is_att))
        def _():
            for rows in chunks:
                main_ref[rows, :] = _dot(a_ref[rows, :], b_ref[...], NT)

        @pl.when(is_att)
        def _():
            for rows in chunks:
                p = _dot(a_ref[rows, :], b_ref[...], NT)
                for h in range(4):
                    slab_ref[h, rows, :] = p[:, h * 128:(h + 1) * 128]

    main_idx = lambda j: jnp.where(j < ATT_T0, j, jnp.where(j < ATT_T1, ATT_T0 - 1, j - (ATT_T1 - ATT_T0)))
    slab_idx = lambda j: jnp.clip(j - ATT_T0, 0, ATT_T1 - ATT_T0 - 1)
    return _pcall(
        body, name="proj", grid=(1, nj, 1),
        in_specs=[pl.BlockSpec((S, D), lambda i, j, k: (0, 0)), pl.BlockSpec((PROJ_TN, D), lambda i, j, k: (j, 0)),
                  HBM_SPEC],
        out_specs=(pl.BlockSpec((S, PROJ_TN), lambda i, j, k: (0, main_idx(j))),
                   pl.BlockSpec((4, S, 128), lambda i, j, k: (slab_idx(j), 0, 0))),
        out_shape=(jax.ShapeDtypeStruct((S, MAIN_COLS), F32), jax.ShapeDtypeStruct((N_SLABS, S, 128), F32)),
        compiler_params=_params(("arbitrary",) * 3))(h1, w_in_t, after)


TR = 256


def _row_spec(w=D):
    return pl.BlockSpec((TR, w), lambda i: (i, 0))


def _vec_spec(w=D):
    return pl.BlockSpec((1, w), lambda i: (0, 0))


def _norm_mod_fwd(x, g, sh, sc, name):
    def body(x_ref, g_ref, sh_ref, sc_ref, o_ref):
        xv = x_ref[...]
        rstd = lax.rsqrt(jnp.mean(xv * xv, axis=-1, keepdims=True) + RMS_EPS)
        n = xv * rstd * g_ref[...]
        o_ref[...] = (n * (1.0 + sc_ref[...]) + sh_ref[...]).astype(BF16)

    return _pcall(body, name=name, grid=(S // TR,), in_specs=[_row_spec(), _vec_spec(), _vec_spec(), _vec_spec()],
                  out_specs=_row_spec(), out_shape=jax.ShapeDtypeStruct((S, D), BF16),
                  compiler_params=_params(("parallel",)))(x, g, sh, sc)


def _norm_mod_bwd(x, g, sc, dh, dres, name, gate=None):
    gated = gate is not None

    def body(x_ref, g_ref, sc_ref, dh_ref, dres_ref, *rest):
        if gated:
            f_ref, gv_ref, dx_ref, dsc_ref, dsh_ref, dg_ref, dz_ref, dgv_ref = rest
        else:
            dx_ref, dsc_ref, dsh_ref, dg_ref = rest
        i = pl.program_id(0)
        xv = x_ref[...]
        dh = dh_ref[...]
        rstd = lax.rsqrt(jnp.mean(xv * xv, axis=-1, keepdims=True) + RMS_EPS)
        xhat = xv * rstd
        gv = g_ref[...]
        dn = dh * (1.0 + sc_ref[...])
        dxhat = dn * gv
        dx = dres_ref[...] + rstd * (dxhat - xhat * jnp.mean(dxhat * xhat, axis=-1, keepdims=True))
        dx_ref[...] = dx
        sums = [(dsc_ref, jnp.sum(dh * (xhat * gv), axis=0, keepdims=True)),
                (dsh_ref, jnp.sum(dh, axis=0, keepdims=True)),
                (dg_ref, jnp.sum(dn * xhat, axis=0, keepdims=True))]
        if gated:
            dz_ref[...] = (dx * gv_ref[...]).astype(BF16)
            sums.append((dgv_ref, jnp.sum(dx * f_ref[...], axis=0, keepdims=True)))

        @pl.when(i == 0)
        def _():
            for ref, p in sums:
                ref[...] = p

        @pl.when(i > 0)
        def _():
            for ref, p in sums:
                ref[...] += p

    vec = jax.ShapeDtypeStruct((1, D), F32)
    in_specs = [_row_spec(), _vec_spec(), _vec_spec(), _row_spec(), _row_spec()]
    out_specs = [_row_spec(), _vec_spec(), _vec_spec(), _vec_spec()]
    out_shape = [jax.ShapeDtypeStruct((S, D), F32), vec, vec, vec]
    args = [x, g, sc, dh, dres]
    if gated:
        in_specs += [_row_spec(), _vec_spec()]
        out_specs += [_row_spec(), _vec_spec()]
        out_shape += [jax.ShapeDtypeStruct((S, D), BF16), vec]
        args += list(gate)
    return _pcall(body, name=name, grid=(S // TR,), in_specs=in_specs, out_specs=tuple(out_specs),
                  out_shape=tuple(out_shape), compiler_params=_params(("arbitrary",)))(*args)


def _final_loss(x2, tgt, g, f, g2):
    def body(x_ref, t_ref, g_ref, f_ref, g2_ref, loss_ref, dx_ref, dg_ref, df_ref, dg2_ref):
        i = pl.program_id(0)
        xv = x_ref[...]
        gv = g_ref[...]
        rstd = lax.rsqrt(jnp.mean(xv * xv, axis=-1, keepdims=True) + RMS_EPS)
        xhat = xv * rstd
        err = xhat * gv - t_ref[...]
        dy = err * (1.0 / D)
        dxhat = dy * gv
        dx = rstd * (dxhat - xhat * jnp.mean(dxhat * xhat, axis=-1, keepdims=True))
        dx_ref[...] = dx
        df_ref[...] = (dx * g2_ref[...]).astype(BF16)
        p_g = jnp.sum(dy * xhat, axis=0, keepdims=True)
        p_g2 = jnp.sum(dx * f_ref[...], axis=0, keepdims=True)
        p_l = jnp.zeros((1, 128), F32) + 0.5 * jnp.sum(jnp.mean(err * err, axis=-1, keepdims=True))

        @pl.when(i == 0)
        def _():
            dg_ref[...] = p_g
            dg2_ref[...] = p_g2
            loss_ref[...] = p_l

        @pl.when(i > 0)
        def _():
            dg_ref[...] += p_g
            dg2_ref[...] += p_g2
            loss_ref[...] += p_l

    vec = jax.ShapeDtypeStruct((1, D), F32)
    return _pcall(body, name="final_loss", grid=(S // TR,),
                  in_specs=[_row_spec(), _row_spec(), _vec_spec(), _row_spec(), _vec_spec()],
                  out_specs=(_vec_spec(128), _row_spec(), _vec_spec(), _row_spec(), _vec_spec()),
                  out_shape=(jax.ShapeDtypeStruct((1, 128), F32), jax.ShapeDtypeStruct((S, D), F32), vec,
                             jax.ShapeDtypeStruct((S, D), BF16), vec),
                  compiler_params=_params(("arbitrary",)))(x2, tgt, g, f, g2)


HALF = 512


def _merge_fwd(proj, ret_out, att_out):
    def body(ga_ref, gb_ref, r_ref, a_ref, o_ref):
        o_ref[...] = (jax.nn.sigmoid(ga_ref[...]) * r_ref[...] + jax.nn.sigmoid(gb_ref[...]) * a_ref[...]).astype(BF16)

    blk = lambda off: pl.BlockSpec((TR, HALF), lambda i, j: (i, off // HALF + j))
    return _pcall(body, name="merge_fwd", grid=(S // TR, D // HALF),
                  in_specs=[blk(OFF_GA), blk(OFF_GB), blk(0), blk(0)], out_specs=blk(0),
                  out_shape=jax.ShapeDtypeStruct((S, D), BF16),
                  compiler_params=_params(("parallel", "parallel")))(proj, proj, ret_out, att_out)


def _merge_bwd(proj, ret_out, att_out, dmerged):
    def body(ga_ref, gb_ref, r_ref, a_ref, dm_ref, dr_ref, da_ref, dga_ref, dgb_ref):
        sa = jax.nn.sigmoid(ga_ref[...])
        sb = jax.nn.sigmoid(gb_ref[...])
        dm = dm_ref[...]
        dr_ref[...] = (dm * sa).astype(BF16)
        da_ref[...] = (dm * sb).astype(BF16)
        dga_ref[...] = (dm * r_ref[...] * (sa * (1.0 - sa))).astype(BF16)
        dgb_ref[...] = (dm * a_ref[...] * (sb * (1.0 - sb))).astype(BF16)

    blk = lambda off: pl.BlockSpec((TR, HALF), lambda i, j: (i, off // HALF + j))
    o = jax.ShapeDtypeStruct((S, D), BF16)
    return _pcall(body, name="merge_bwd", grid=(S // TR, D // HALF),
                  in_specs=[blk(OFF_GA), blk(OFF_GB), blk(0), blk(0), blk(0)], out_specs=(blk(0),) * 4,
                  out_shape=(o, o, o, o),
                  compiler_params=_params(("parallel", "parallel")))(proj, proj, ret_out, att_out, dmerged)


def _ret_tables():
    H, C = RET_HEADS, CHUNK
    log_g = jnp.log1p(-(2.0 ** (-5.0 - jnp.arange(H, dtype=F32))))
    idx = jnp.arange(C, dtype=F32)
    rel = idx[:, None] - idx[None, :]
    inner = jnp.where(rel >= 0, jnp.exp(log_g[:, None, None] * jnp.maximum(rel, 0.0)), 0.0)
    qd = jnp.exp(log_g[:, None] * (idx + 1.0))[:, :, None]
    kd = jnp.exp(log_g[:, None] * (C - 1.0 - idx))[:, :, None]
    cd = jnp.broadcast_to(jnp.exp(log_g * C)[:, None, None], (H, 1, 128))
    half = RET_DK // 2
    inv = 10000.0 ** (-jnp.arange(half, dtype=F32) / half)
    ang = jnp.arange(S, dtype=F32)[:, None] * inv[None, :]
    return inner, qd, kd, cd, jnp.cos(ang), jnp.sin(ang)


def _rot(x, cos, sin):
    x1, x2 = x[:, :128], x[:, 128:]
    return jnp.concatenate([x1 * cos - x2 * sin, x1 * sin + x2 * cos], axis=1)


def _rot_t(d, cos, sin):
    d1, d2 = d[:, :128], d[:, 128:]
    return jnp.concatenate([d1 * cos + d2 * sin, d2 * cos - d1 * sin], axis=1)


RET_COLS = OFF_ATT
RET_VW = RET_HEADS * RET_DV


def _ret_specs(chunk_of):
    ci = chunk_of
    whole = lambda shape: pl.BlockSpec(shape, lambda t: (0,) * len(shape))
    return [
        pl.BlockSpec((CHUNK, RET_COLS), lambda t: (ci(t), 0)),
        pl.BlockSpec((CHUNK, 128), lambda t: (ci(t), 0)),
        pl.BlockSpec((CHUNK, 128), lambda t: (ci(t), 0)),
        whole((RET_HEADS, CHUNK, CHUNK)), whole((RET_HEADS, CHUNK, 1)), whole((RET_HEADS, CHUNK, 1)),
        whole((RET_HEADS, 1, 128)), whole((1, RET_VW)), whole((1, RET_VW)),
    ]


def _ret_cols(h):
    q = slice(OFF_RQ + h * RET_DK, OFF_RQ + (h + 1) * RET_DK)
    k = slice(OFF_RK + h * RET_DK, OFF_RK + (h + 1) * RET_DK)
    v = slice(OFF_RV + h * RET_DV, OFF_RV + (h + 1) * RET_DV)
    g = slice(OFF_RG + h * RET_DV, OFF_RG + (h + 1) * RET_DV)
    return q, k, v, g, slice(h * RET_DV, (h + 1) * RET_DV)


def _ret_fwd(proj, tables, gn_g, gn_b, comm=None):
    inner, qd, kd, cd, cos, sin = tables

    def body(x_ref, cos_ref, sin_ref, in_ref, qd_ref, kd_ref, cd_ref, g_ref, b_ref,
             gated_ref, ro_ref, st_ref, s_scr):
        i = pl.program_id(0)

        @pl.when(i == 0)
        def _():
            s_scr[...] = jnp.zeros_like(s_scr)

        cosv, sinv = cos_ref[...], sin_ref[...]
        for h in range(RET_HEADS):
            cq, ck, cv, cg, co = _ret_cols(h)
            q = _rot(x_ref[:, cq], cosv, sinv)
            k = _rot(x_ref[:, ck], cosv, sinv) * (RET_DK ** -0.5)
            v = x_ref[:, cv]
            st = s_scr[h]
            st_ref[h] = st
            s = _dot(q, k, NT) * in_ref[h]
            o = _dot(s, v, NN) + _dot(q, st, NN) * qd_ref[h]
            s_scr[h] = st * cd_ref[h, :, :1] + _dot(k * kd_ref[h], v, TN)
            ro_ref[:, co] = o
            mu = jnp.mean(o, axis=-1, keepdims=True)
            oc = o - mu
            var = jnp.mean(oc * oc, axis=-1, keepdims=True)
            rn = oc * lax.rsqrt(var + GN_EPS) * g_ref[:, co] + b_ref[:, co]
            rg = x_ref[:, cg]
            gated_ref[:, co] = (rg * jax.nn.sigmoid(rg) * rn).astype(BF16)

    ospec = pl.BlockSpec((CHUNK, RET_VW), lambda t: (t, 0))
    kw = dict(name="ret_fwd", grid=(N_CHUNK,), in_specs=_ret_specs(lambda t: t),
              out_specs=(ospec, ospec, pl.BlockSpec((RET_HEADS, None, RET_DK, RET_DV), lambda t: (0, t, 0, 0))),
              out_shape=(jax.ShapeDtypeStruct((S, RET_VW), BF16), jax.ShapeDtypeStruct((S, RET_VW), F32),
                         jax.ShapeDtypeStruct((RET_HEADS, N_CHUNK, RET_DK, RET_DV), F32)),
              scratch_shapes=[pltpu.VMEM((RET_HEADS, RET_DK, RET_DV), F32)])
    args = (proj, cos, sin, inner, qd, kd, cd, gn_g, gn_b)
    if comm is not None:
        return _carry(body, comm, **kw)(*args)
    return _pcall(body, compiler_params=_params(("arbitrary",)), **kw)(*args)


def _ret_bwd(proj, tables, gn_g, gn_b, ro, states, dgated, comm=None):
    inner, qd, kd, cd, cos, sin = tables
    last = N_CHUNK - 1

    def body(x_ref, cos_ref, sin_ref, in_ref, qd_ref, kd_ref, cd_ref, g_ref, b_ref, ro_ref, st_ref, dg_ref,
             dx_ref, gg_ref, gb_ref, gs_scr):
        t = pl.program_id(0)

        @pl.when(t == 0)
        def _():
            gs_scr[...] = jnp.zeros_like(gs_scr)
            gg_ref[...] = jnp.zeros_like(gg_ref)
            gb_ref[...] = jnp.zeros_like(gb_ref)

        cosv, sinv = cos_ref[...], sin_ref[...]
        for h in range(RET_HEADS):
            cq, ck, cv, cg, co = _ret_cols(h)
            q = _rot(x_ref[:, cq], cosv, sinv)
            k = _rot(x_ref[:, ck], cosv, sinv) * (RET_DK ** -0.5)
            v = x_ref[:, cv]
            qdv, kdv, dm = qd_ref[h], kd_ref[h], in_ref[h]
            st = st_ref[h]
            o = ro_ref[:, co]
            gv = g_ref[:, co]
            mu = jnp.mean(o, axis=-1, keepdims=True)
            oc = o - mu
            rstd = lax.rsqrt(jnp.mean(oc * oc, axis=-1, keepdims=True) + GN_EPS)
            ohat = oc * rstd
            rn = ohat * gv + b_ref[:, co]
            rg = x_ref[:, cg]
            sg = jax.nn.sigmoid(rg)
            dgt = dg_ref[:, co]
            drn = dgt * (rg * sg)
            dx_ref[:, cg] = (dgt * rn * (sg * (1.0 + rg * (1.0 - sg)))).astype(BF16)
            gg_ref[:, co] += jnp.sum(drn * ohat, axis=0, keepdims=True)
            gb_ref[:, co] += jnp.sum(drn, axis=0, keepdims=True)
            dohat = drn * gv
            do = rstd * (dohat - jnp.mean(dohat, axis=-1, keepdims=True)
                         - ohat * jnp.mean(dohat * ohat, axis=-1, keepdims=True))
            gs = gs_scr[h]
            s = _dot(q, k, NT) * dm
            dsr = _dot(do, v, NT) * dm
            dq = _dot(dsr, k, NN) + _dot(do, st, NT) * qdv
            dk = _dot(dsr, q, TN) + _dot(v, gs, NT) * kdv
            dv = _dot(s, do, TN) + _dot(k * kdv, gs, NN)
            gs_scr[h] = gs * cd_ref[h, :, :1] + _dot(q * qdv, do, TN)
            dx_ref[:, cq] = _rot_t(dq, cosv, sinv).astype(BF16)
            dx_ref[:, ck] = (_rot_t(dk, cosv, sinv) * (RET_DK ** -0.5)).astype(BF16)
            dx_ref[:, cv] = dv.astype(BF16)

    rev = lambda t: last - t
    vblk = pl.BlockSpec((CHUNK, RET_VW), lambda t: (rev(t), 0))
    vspec = pl.BlockSpec((1, RET_VW), lambda t: (0, 0))
    kw = dict(name="ret_bwd", grid=(N_CHUNK,),
              in_specs=_ret_specs(rev) + [vblk, pl.BlockSpec((RET_HEADS, None, RET_DK, RET_DV),
                                                             lambda t: (0, rev(t), 0, 0)), vblk],
              out_specs=(pl.BlockSpec((CHUNK, RET_COLS), lambda t: (rev(t), 0)), vspec, vspec),
              out_shape=(jax.ShapeDtypeStruct((S, RET_COLS), BF16), jax.ShapeDtypeStruct((1, RET_VW), F32),
                         jax.ShapeDtypeStruct((1, RET_VW), F32)),
              scratch_shapes=[pltpu.VMEM((RET_HEADS, RET_DK, RET_DV), F32)])
    args = (proj, cos, sin, inner, qd, kd, cd, gn_g, gn_b, ro, states, dgated)
    if comm is not None:
        return _carry(body, comm, **kw)(*args)
    return _pcall(body, compiler_params=_params(("arbitrary",)), **kw)(*args)


def _bucket_tables():
    qi = np.arange(ATT_BLK)[:, None]
    kj = np.arange(2 * ATT_BLK)[None, :]
    m = ATT_BLK + qi - kj
    out = []
    for win, dil in ATT_GROUPS:
        w = win // dil
        dist = (np.clip(m, 0, w) * dil).astype(np.int32)
        max_exact = N_BUCKETS // 2
        d_f = np.maximum(dist, 1).astype(np.float32)
        large = max_exact + (np.log(d_f / np.float32(max_exact)) / np.float32(math.log(MAX_DIST / max_exact))
                             * np.float32(N_BUCKETS - max_exact)).astype(np.int32)
        large = np.minimum(large, N_BUCKETS - 1)
        out.append(np.where(dist < max_exact, dist, large).astype(np.int32))
    return np.stack(out)


def _bias_build(rel_bias, buckets):
    def body(tab_ref, bk_ref, o_ref):
        hh = pl.program_id(0)
        bk = bk_ref[...]
        acc = jnp.zeros((ATT_BLK, 2 * ATT_BLK), F32)
        for b in range(N_BUCKETS):
            acc = jnp.where(bk == b, tab_ref[b, hh], acc)
        o_ref[...] = acc

    nh = len(ATT_GROUPS) * ATT_HG
    return _pcall(body, name="bias_build", grid=(nh,),
                  in_specs=[pl.BlockSpec(memory_space=pltpu.SMEM),
                            pl.BlockSpec((None, ATT_BLK, 2 * ATT_BLK), lambda hh: (hh // ATT_HG, 0, 0))],
                  out_specs=pl.BlockSpec((None, ATT_BLK, 2 * ATT_BLK), lambda hh: (hh, 0, 0)),
                  out_shape=jax.ShapeDtypeStruct((nh, ATT_BLK, 2 * ATT_BLK), F32),
                  compiler_params=_params(("parallel",)))(rel_bias, buckets)


def _bias_grad(ds_sum, buckets):
    def body(ds_ref, bk_ref, o_ref):
        bk = bk_ref[...]
        ds = ds_ref[...]
        rows = lax.broadcasted_iota(jnp.int32, (N_BUCKETS, 128), 0)
        acc = jnp.zeros((N_BUCKETS, 128), F32)
        for b in range(N_BUCKETS):
            acc = jnp.where(rows == b, jnp.sum(jnp.where(bk == b, ds, 0.0)), acc)
        o_ref[...] = acc

    nh = len(ATT_GROUPS) * ATT_HG
    return _pcall(body, name="bias_grad", grid=(nh,),
                  in_specs=[pl.BlockSpec((None, ATT_BLK, 2 * ATT_BLK), lambda hh: (hh, 0, 0)),
                            pl.BlockSpec((None, ATT_BLK, 2 * ATT_BLK), lambda hh: (hh // ATT_HG, 0, 0))],
                  out_specs=pl.BlockSpec((None, N_BUCKETS, 128), lambda hh: (hh, 0, 0)),
                  out_shape=jax.ShapeDtypeStruct((nh, N_BUCKETS, 128), F32),
                  compiler_params=_params(("parallel",)))(ds_sum, buckets)


def _att_valid(n):
    qi = lax.broadcasted_iota(jnp.int32, (ATT_BLK, 2 * ATT_BLK), 0)
    kj = lax.broadcasted_iota(jnp.int32, (ATT_BLK, 2 * ATT_BLK), 1)
    m = ATT_BLK + qi - kj
    first_key = jnp.where(n > 0, 0, ATT_BLK)
    return (m >= 0) & (m <= ATT_BLK) & (kj >= first_key)


ATT_HP = (1, 2, 2)


def _att_geometry(gi):
    _, dil = ATT_GROUPS[gi]
    return dil, S // dil // ATT_BLK, ATT_HP[gi]


def _blk(dil, r, n):
    if dil == 1:
        return pl.ds(n * ATT_BLK, ATT_BLK)
    return pl.ds(r + n * ATT_BLK * dil, ATT_BLK, stride=dil)


def _slab_specs(gi):
    _, _, hp = _att_geometry(gi)
    per = ATT_HG // hp
    return [pl.BlockSpec((hp, S, ATT_DH), lambda g, r, part=part: ((3 * gi + part) * per + g, 0, 0))
            for part in range(3)]


def _head_specs(gi, count):
    _, _, hp = _att_geometry(gi)
    return [pl.BlockSpec((hp, S, ATT_DH), lambda g, r: (g, 0, 0))] * count


def _bias_spec(gi):
    _, _, hp = _att_geometry(gi)
    return pl.BlockSpec((hp, ATT_BLK, 2 * ATT_BLK), lambda g, r: (gi * (ATT_HG // hp) + g, 0, 0))


def _att_fwd(slabs, bias, gi, comm=None):
    dil, nb, hp = _att_geometry(gi)
    scale = ATT_DH ** -0.5

    def body(q_ref, k_ref, v_ref, bias_ref, o_ref, l_ref):
        r = pl.program_id(1)
        for n in range(nb):
            valid = _att_valid(n)
            prev = _blk(dil, r, max(n - 1, 0))
            cur = _blk(dil, r, n)
            for h in range(hp):
                kk = jnp.concatenate([k_ref[h, prev, :], k_ref[h, cur, :]], axis=0)
                vv = jnp.concatenate([v_ref[h, prev, :], v_ref[h, cur, :]], axis=0)
                s = _dot(q_ref[h, cur, :], kk, NT) * scale + bias_ref[h]
                s = jnp.where(valid, s, -1e30)
                mx = jnp.max(s, axis=-1, keepdims=True)
                e = jnp.exp(s - mx)
                den = jnp.sum(e, axis=-1, keepdims=True)
                o_ref[h, cur, :] = _dot(e / den, vv, NN)
                l_ref[h, cur, :] = jnp.broadcast_to(mx + jnp.log(den), (ATT_BLK, ATT_DH))

    osh = jax.ShapeDtypeStruct((ATT_HG, S, ATT_DH), F32)
    kw = dict(name=f"att_fwd{gi}", grid=(ATT_HG // hp, dil), in_specs=_slab_specs(gi) + [_bias_spec(gi)],
              out_specs=tuple(_head_specs(gi, 2)), out_shape=(osh, osh))
    if comm is not None:
        return _carry(body, comm, **kw)(slabs, slabs, slabs, bias)
    return _pcall(body, compiler_params=_params(("parallel", "arbitrary")), **kw)(slabs, slabs, slabs, bias)


def _att_bwd(slabs, bias, o, lse, do, dlse, gi, comm=None):
    dil, nb, hp = _att_geometry(gi)
    per = ATT_HG // hp
    scale = ATT_DH ** -0.5
    wh = hp * ATT_DH
    wide = lambda t: jnp.concatenate([t, t], axis=1)

    def body(q_ref, k_ref, v_ref, bias_ref, o_ref, l_ref, do_ref, dl_ref, dq_ref, dk_ref, dv_ref, ds_ref):
        r = pl.program_id(1)

        @pl.when(r == 0)
        def _():
            ds_ref[...] = jnp.zeros_like(ds_ref)

        for h in range(hp):
            sl = slice(h * ATT_DH, (h + 1) * ATT_DH)
            carry_k = carry_v = None
            for n in range(nb):
                valid = _att_valid(n)
                prev = _blk(dil, r, max(n - 1, 0))
                cur = _blk(dil, r, n)
                q = q_ref[h, cur, :]
                kk = jnp.concatenate([k_ref[h, prev, :], k_ref[h, cur, :]], axis=0)
                vv = jnp.concatenate([v_ref[h, prev, :], v_ref[h, cur, :]], axis=0)
                dov = do_ref[h, cur, :]
                s = _dot(q, kk, NT) * scale + bias_ref[h]
                p = jnp.where(valid, jnp.exp(s - wide(l_ref[h, cur, :])), 0.0)
                dp = _dot(dov, vv, NT)
                delta = jnp.sum(dov * o_ref[h, cur, :], axis=-1, keepdims=True)
                ds = p * (dp - delta + wide(dl_ref[h, cur, :]))
                ds_ref[h] += ds
                out_rows = pl.ds(n * ATT_BLK, ATT_BLK)
                dq_ref[out_rows, sl] = (_dot(ds, kk, NN) * scale).astype(BF16)
                dkk = _dot(ds, q, TN) * scale
                dvv = _dot(p, dov, TN)
                if n > 0:
                    before = pl.ds((n - 1) * ATT_BLK, ATT_BLK)
                    dk_ref[before, sl] = (carry_k + dkk[:ATT_BLK]).astype(BF16)
                    dv_ref[before, sl] = (carry_v + dvv[:ATT_BLK]).astype(BF16)
                carry_k, carry_v = dkk[ATT_BLK:], dvv[ATT_BLK:]
            last = pl.ds((nb - 1) * ATT_BLK, ATT_BLK)
            dk_ref[last, sl] = carry_k.astype(BF16)
            dv_ref[last, sl] = carry_v.astype(BF16)

    out_spec = pl.BlockSpec((S // dil, wh), lambda g, r: (0, r * per + g))
    osh = jax.ShapeDtypeStruct((S // dil, dil * AW), BF16)
    kw = dict(name=f"att_bwd{gi}", grid=(per, dil), in_specs=_slab_specs(gi) + [_bias_spec(gi)] + _head_specs(gi, 4),
              out_specs=(out_spec, out_spec, out_spec,
                         pl.BlockSpec((hp, ATT_BLK, 2 * ATT_BLK), lambda g, r: (g, 0, 0))),
              out_shape=(osh, osh, osh, jax.ShapeDtypeStruct((ATT_HG, ATT_BLK, 2 * ATT_BLK), F32)))
    args = (slabs, slabs, slabs, bias, o, lse, do, dlse)
    if comm is not None:
        return _carry(body, comm, **kw)(*args)
    return _pcall(body, compiler_params=_params(("arbitrary", "arbitrary")), **kw)(*args)


AW = ATT_HG * ATT_DH


def _mix_weights(l0, l1, l2):
    mx = jnp.maximum(jnp.maximum(l0, l1), l2)
    e0, e1, e2 = jnp.exp(l0 - mx), jnp.exp(l1 - mx), jnp.exp(l2 - mx)
    den = e0 + e1 + e2
    return e0 / den, e1 / den, e2 / den


def _heads_spec():
    return pl.BlockSpec((ATT_HG, TR, ATT_DH), lambda i: (0, i, 0))


def _mix_fwd(os_, ls):
    def body(o0, o1, o2, l0, l1, l2, att_ref):
        for h in range(ATT_HG):
            w0, w1, w2 = _mix_weights(l0[h], l1[h], l2[h])
            att_ref[:, h * ATT_DH:(h + 1) * ATT_DH] = (w0 * o0[h] + w1 * o1[h] + w2 * o2[h]).astype(BF16)

    return _pcall(body, name="mix_fwd", grid=(S // TR,), in_specs=[_heads_spec()] * 6, out_specs=_row_spec(AW),
                  out_shape=jax.ShapeDtypeStruct((S, AW), BF16), compiler_params=_params(("parallel",)))(*os_, *ls)


def _mix_bwd(os_, ls, datt):
    def body(o0, o1, o2, l0, l1, l2, da_ref, d0, d1, d2, e0, e1, e2):
        for h in range(ATT_HG):
            ws = _mix_weights(l0[h], l1[h], l2[h])
            da = da_ref[:, h * ATT_DH:(h + 1) * ATT_DH]
            dws = []
            for o_ref, w, d_ref in zip((o0, o1, o2), ws, (d0, d1, d2)):
                d_ref[h] = w * da
                dws.append(jnp.broadcast_to(jnp.sum(da * o_ref[h], axis=-1, keepdims=True), (TR, ATT_DH)))
            tot = ws[0] * dws[0] + ws[1] * dws[1] + ws[2] * dws[2]
            for w, dw, e_ref in zip(ws, dws, (e0, e1, e2)):
                e_ref[h] = w * (dw - tot)

    o = jax.ShapeDtypeStruct((ATT_HG, S, ATT_DH), F32)
    return _pcall(body, name="mix_bwd", grid=(S // TR,), in_specs=[_heads_spec()] * 6 + [_row_spec(AW)],
                  out_specs=(_heads_spec(),) * 6, out_shape=(o,) * 6,
                  compiler_params=_params(("parallel",)))(*os_, *ls, datt)


def _ada_fwd(c_all, w_sh, b_sl):
    def body(c_ref, w_ref, b_ref, o_ref):
        cv = c_ref[...]
        o_ref[...] = _dot(cv * jax.nn.sigmoid(cv), w_ref[...], NN) + b_ref[...]

    return _pcall(body, name="ada_fwd", out_shape=jax.ShapeDtypeStruct((N_DEV, w_sh.shape[1]), F32),
                  compiler_params=_params())(c_all, w_sh, b_sl)


def _ada_bwd(c_all, dm_sl):
    def body(c_ref, d_ref, o_ref):
        cv = c_ref[...]
        o_ref[...] = _dot(cv * jax.nn.sigmoid(cv), d_ref[...], TN)

    return _pcall(body, name="ada_bwd", out_shape=jax.ShapeDtypeStruct((D, dm_sl.shape[1]), F32),
                  compiler_params=_params())(c_all, dm_sl)


N_MOD = 6


def _sum_small(gathered):
    n = len(gathered)

    def body(*refs):
        ins, (gb_ref, dm_ref), outs = refs[:n], refs[n:n + 2], refs[n + 2:]

        def total(r):
            acc = r[0]
            for e in range(1, N_DEV):
                acc = acc + r[e]
            return acc

        for i in range(N_MOD):
            cols = slice(i * D, (i + 1) * D)
            gb_ref[:, cols] = total(ins[i])
            for e in range(N_DEV):
                dm_ref[e:e + 1, cols] = ins[i][e]
        for r, o_ref in zip(ins[N_MOD:], outs):
            o_ref[...] = total(r)

    shapes = (jax.ShapeDtypeStruct((1, N_MOD * D), F32), jax.ShapeDtypeStruct((N_DEV, N_MOD * D), F32),
              *[jax.ShapeDtypeStruct(g.shape[1:], F32) for g in gathered[N_MOD:]])
    res = _pcall(body, name="sum_small", out_shape=shapes, compiler_params=_params())(*gathered)
    return res[0], res[1], res[2:]


def _row_tile(m, n):
    t = max(8, min(m, (1 << 19) // n // 8 * 8))
    while m % t:
        t -= 8
    return t


def _pair_sum(full, recv, sel, name):
    _, _, m, n = full.shape
    t = _row_tile(m, n)

    def body(sel_ref, a_ref, b_ref, o_ref):
        o_ref[...] = (a_ref[...].astype(F32) + b_ref[...].astype(F32)).astype(o_ref.dtype)

    gs = pltpu.PrefetchScalarGridSpec(
        num_scalar_prefetch=1, grid=(4, m // t),
        in_specs=[pl.BlockSpec((None, None, t, n), lambda q, i, s: (q, s[0], i, 0)),
                  pl.BlockSpec((None, t, n), lambda q, i, s: (q, i, 0))],
        out_specs=pl.BlockSpec((None, t, n), lambda q, i, s: (q, i, 0)))
    return _pcall(body, name=name, grid_spec=gs, out_shape=jax.ShapeDtypeStruct((4, m, n), full.dtype),
                  compiler_params=_params(("parallel", "parallel")))(sel, full, recv)


def _chip_sum(part, recv, sel, name):
    _, m, n = part.shape
    t = _row_tile(m, n)

    def body(sel_ref, a_ref, r_ref, o_ref):
        o_ref[...] = ((a_ref[...].astype(F32) + r_ref[0].astype(F32)) + r_ref[1].astype(F32)) + r_ref[2].astype(F32)

    gs = pltpu.PrefetchScalarGridSpec(
        num_scalar_prefetch=1, grid=(m // t,),
        in_specs=[pl.BlockSpec((None, t, n), lambda i, s: (s[0], i, 0)),
                  pl.BlockSpec((3, t, n), lambda i, s: (0, i, 0))],
        out_specs=pl.BlockSpec((t, n), lambda i, s: (i, 0)))
    return _pcall(body, name=name, grid_spec=gs, out_shape=jax.ShapeDtypeStruct((m, n), F32),
                  compiler_params=_params(("parallel",)))(sel, part, recv)


def _adamw_math(w, g, m, v):
    nm = ADAM_B1 * m + (1.0 - ADAM_B1) * g
    nv = ADAM_B2 * v + (1.0 - ADAM_B2) * (g * g)
    m_hat = nm / (1.0 - ADAM_B1 ** ADAM_STEP)
    v_hat = nv / (1.0 - ADAM_B2 ** ADAM_STEP)
    return -ADAM_LR * (m_hat / (jnp.sqrt(v_hat) + ADAM_EPS) + ADAM_WD * w), nm, nv


def _adamw(w, g, m, v, name):
    _, rows, cols = w.shape
    t = _row_tile(rows, cols)

    def body(w_ref, g_ref, m_ref, v_ref, d_ref, nm_ref, nv_ref):
        d_ref[...], nm_ref[...], nv_ref[...] = _adamw_math(w_ref[...], g_ref[...], m_ref[...], v_ref[...])

    spec3 = pl.BlockSpec((None, t, cols), lambda i: (0, i, 0))
    spec2 = pl.BlockSpec((t, cols), lambda i: (i, 0))
    o = jax.ShapeDtypeStruct(w.shape, F32)
    return _pcall(body, name=name, grid=(rows // t,), in_specs=[spec3, spec2, spec3, spec3], out_specs=(spec3,) * 3,
                  out_shape=(o, o, o), compiler_params=_params(("parallel",)))(w, g, m, v)


def _adamw_small(ws, gs, ms, vs):
    n = len(ws)

    def body(*refs):
        for i in range(n):
            w_ref, g_ref, m_ref, v_ref = (refs[k * n + i] for k in range(4))
            d, nm, nv = _adamw_math(w_ref[...], g_ref[...], m_ref[...], v_ref[...])
            refs[4 * n + i][...] = d
            refs[5 * n + i][...] = nm
            refs[6 * n + i][...] = nv

    shapes = tuple(jax.ShapeDtypeStruct(w.shape, F32) for w in ws)
    res = _pcall(body, name="adamw_small", out_shape=shapes * 3, compiler_params=_params())(*ws, *gs, *ms, *vs)
    return res[:n], res[n:2 * n], res[2 * n:]


def _mesh_pos():
    return lax.axis_index("x"), lax.axis_index("y"), lax.axis_index("c")


class _Gather:
    def __init__(self, arrs, relay=False):
        self.relay = relay
        self.ins = list(arrs)
        na = self.na = len(arrs)
        self.out_shape = tuple(jax.ShapeDtypeStruct((N_DEV,) + a.shape, a.dtype) for a in arrs)
        self.sems = [pltpu.SemaphoreType.DMA((7 * na,)), pltpu.SemaphoreType.DMA((7 * na,)),
                     pltpu.SemaphoreType.DMA((na,))]

    def _copies(self, ins, outs, sems):
        send_sems, recv_sems, local_sems = sems
        x, y, c = _mesh_pos()
        me, sibling = (x, y, c), (x, y, 1 - c)
        chips = [(1 - x, y), (x, 1 - y), (1 - x, 1 - y)]

        def slot(p):
            return 4 * p[0] + 2 * p[1] + p[2]

        def copy(a, k, block, to, src=None):
            dst = outs[a].at[slot(block)]
            return pltpu.make_async_remote_copy(
                src_ref=dst if src is None else src, dst_ref=dst, send_sem=send_sems.at[7 * a + k],
                recv_sem=recv_sems.at[7 * a + k], device_id=to, device_id_type=MESH)

        mine = [pltpu.make_async_copy(ins[a], outs[a].at[slot(me)], local_sems.at[a]) for a in range(self.na)]
        direct = chips[:2] if self.relay else chips
        first = []
        for a in range(self.na):
            first.append(copy(a, 0, me, sibling, src=ins[a]))
            first += [copy(a, 1 + j, me, (*chip, c), src=ins[a]) for j, chip in enumerate(direct)]
        return me, sibling, chips, c, copy, mine, first

    def start(self, ins, outs, sems):
        *_, mine, first = self._copies(ins, outs, sems)
        for cp in mine + first:
            cp.start()

    def finish(self, ins, outs, sems):
        me, sibling, chips, c, copy, mine, first = self._copies(ins, outs, sems)
        x, y = me[0], me[1]
        passed = []
        for j, chip in enumerate(chips):
            for a in range(self.na):
                if self.relay and j == 2:
                    owner = ((x + 1 - c) % 2, (y + c) % 2, c)
                    cp = copy(a, 3, owner, ((x + c) % 2, (y + 1 - c) % 2, c))
                    cp.start()
                    passed.append(cp)
                copy(a, 1 + j, (*chip, c), me).wait_recv()
                cp = copy(a, 4 + j, (*chip, c), sibling)
                cp.start()
                passed.append(cp)
        for a in range(self.na):
            copy(a, 0, sibling, me).wait_recv()
            for j, chip in enumerate(chips):
                copy(a, 4 + j, (*chip, 1 - c), me).wait_recv()
        for cp in first + passed:
            cp.wait_send()
        for cp in mine:
            cp.wait()


class _ExchangeCore:
    def __init__(self, fulls):
        self.ins = list(fulls)
        self.out_shape = tuple(jax.ShapeDtypeStruct((4,) + f.shape[2:], f.dtype) for f in fulls)
        self.sems = [pltpu.SemaphoreType.DMA((4 * len(fulls),)), pltpu.SemaphoreType.DMA((4 * len(fulls),))]

    def _copies(self, ins, outs, sems):
        send_sems, recv_sems = sems
        x, y, c = _mesh_pos()
        return [pltpu.make_async_remote_copy(
            src_ref=ins[a].at[q, 1 - c], dst_ref=outs[a].at[q], send_sem=send_sems.at[4 * a + q],
            recv_sem=recv_sems.at[4 * a + q], device_id=(x, y, 1 - c), device_id_type=MESH)
            for a in range(len(self.ins)) for q in range(4)]

    def start(self, ins, outs, sems):
        for cp in self._copies(ins, outs, sems):
            cp.start()

    def finish(self, ins, outs, sems):
        for cp in self._copies(ins, outs, sems):
            cp.wait()


class _ExchangeChip:
    def __init__(self, parts):
        self.ins = list(parts)
        self.out_shape = tuple(jax.ShapeDtypeStruct((3,) + p.shape[1:], p.dtype) for p in parts)
        self.sems = [pltpu.SemaphoreType.DMA((3 * len(parts),)), pltpu.SemaphoreType.DMA((3 * len(parts),))]

    def _copies(self, ins, outs, sems):
        send_sems, recv_sems = sems
        x, y, c = _mesh_pos()
        chips = [(1 - x, y), (x, 1 - y), (1 - x, 1 - y)]
        return [pltpu.make_async_remote_copy(
            src_ref=ins[a].at[2 * px + py], dst_ref=outs[a].at[j], send_sem=send_sems.at[3 * a + j],
            recv_sem=recv_sems.at[3 * a + j], device_id=(px, py, c), device_id_type=MESH)
            for a in range(len(self.ins)) for j, (px, py) in enumerate(chips)]

    def start(self, ins, outs, sems):
        for cp in self._copies(ins, outs, sems):
            cp.start()

    def finish(self, ins, outs, sems):
        for cp in self._copies(ins, outs, sems):
            cp.wait()


HBM_ONLY = pl.BlockSpec(memory_space=pltpu.HBM)
SEM_SPEC = pl.BlockSpec(memory_space=pltpu.SEMAPHORE)
SIDE_EFFECT = pltpu.SideEffectType.DATAFLOW_SIDE_EFFECTING


def _chip_copies(p_refs, land_refs, send_sems, recv_sems):
    x, y, c = _mesh_pos()
    return [pltpu.make_async_remote_copy(
        src_ref=p_refs[a].at[2 * px + py], dst_ref=land_refs[a].at[j], send_sem=send_sems.at[3 * a + j],
        recv_sem=recv_sems.at[3 * a + j], device_id=(px, py, c), device_id_type=MESH)
        for a in range(len(p_refs)) for j, (px, py) in enumerate([(1 - x, y), (x, 1 - y), (1 - x, 1 - y)])]


def _chip_exchange_start(parts, name):
    n = len(parts)
    lands = [lax.empty((3,) + p.shape[1:], p.dtype) for p in parts]

    def body(*refs):
        p_refs, land_refs, (send_sems, recv_sems) = refs[:n], refs[n:2 * n], refs[2 * n:2 * n + 2]
        for cp in _chip_copies(p_refs, land_refs, send_sems, recv_sems):
            cp.start()
        token = refs[-1]
        token[...] = jnp.zeros_like(token)

    hbm = lambda t: pltpu.HBM(t.shape, t.dtype)
    res = pl.pallas_call(
        body, name=name,
        out_shape=(pltpu.SemaphoreType.DMA((3 * n,)), pltpu.SemaphoreType.DMA((3 * n,)), *[hbm(t) for t in parts + lands],
                   jax.ShapeDtypeStruct((8, 128), F32)),
        in_specs=(HBM_ONLY,) * (2 * n),
        out_specs=(SEM_SPEC, SEM_SPEC, *[HBM_ONLY] * (2 * n), pl.BlockSpec(memory_space=pltpu.VMEM)),
        input_output_aliases={i: 2 + i for i in range(2 * n)},
        compiler_params=pltpu.CompilerParams(has_side_effects=SIDE_EFFECT))(
        *[pltpu.with_memory_space_constraint(t, pltpu.HBM) for t in parts + lands])
    return (res[0], res[1], list(res[2:2 + n]), list(res[2 + n:2 + 2 * n])), res[-1]


def _chip_exchange_wait(in_flight, after, name):
    send_sems, recv_sems, parts, lands = in_flight
    n = len(parts)

    def body(*refs):
        p_refs, land_refs, (send_sems, recv_sems) = refs[:n], refs[n:2 * n], refs[2 * n:2 * n + 2]
        for cp in _chip_copies(p_refs, land_refs, send_sems, recv_sems):
            cp.wait_send()
            cp.wait_recv()

    res = pl.pallas_call(
        body, name=name, out_shape=tuple(pltpu.HBM(t.shape, t.dtype) for t in parts + lands),
        in_specs=(*[HBM_ONLY] * (2 * n), SEM_SPEC, SEM_SPEC, pl.BlockSpec(memory_space=pl.ANY)),
        out_specs=(HBM_ONLY,) * (2 * n), input_output_aliases={i: i for i in range(2 * n)},
        compiler_params=pltpu.CompilerParams(has_side_effects=SIDE_EFFECT))(*parts, *lands, send_sems, recv_sems, after)
    return list(res[:n]), list(res[n:])


def _slot(p):
    return 4 * p[0] + 2 * p[1] + p[2]


def _gather_copies(src_refs, out_refs, send_sems, recv_sems):
    x, y, c = _mesh_pos()
    targets = [(x, y, 1 - c), (1 - x, y, c), (x, 1 - y, c), (1 - x, 1 - y, c)]
    return [pltpu.make_async_remote_copy(
        src_ref=src_refs[a], dst_ref=out_refs[a].at[_slot((x, y, c))], send_sem=send_sems.at[4 * a + k],
        recv_sem=recv_sems.at[4 * a + k], device_id=to, device_id_type=MESH)
        for a in range(len(src_refs)) for k, to in enumerate(targets)]


def _gather_start(shards, after, name):
    n = len(shards)
    outs = [lax.empty((N_DEV,) + s.shape, s.dtype) for s in shards]

    def body(*refs):
        for cp in _gather_copies(refs[:n], refs[n:2 * n], refs[2 * n + 1], refs[2 * n + 2]):
            cp.start()
        token = refs[-1]
        token[...] = jnp.zeros_like(token)

    res = pl.pallas_call(
        body, name=name,
        out_shape=(pltpu.SemaphoreType.DMA((4 * n,)), pltpu.SemaphoreType.DMA((4 * n,)),
                   *[pltpu.HBM(t.shape, t.dtype) for t in shards + outs], jax.ShapeDtypeStruct((8, 128), F32)),
        in_specs=(*[HBM_ONLY] * (2 * n), pl.BlockSpec(memory_space=pl.ANY)),
        out_specs=(SEM_SPEC, SEM_SPEC, *[HBM_ONLY] * (2 * n), pl.BlockSpec(memory_space=pltpu.VMEM)),
        input_output_aliases={i: 2 + i for i in range(2 * n)},
        compiler_params=pltpu.CompilerParams(has_side_effects=SIDE_EFFECT))(
        *[pltpu.with_memory_space_constraint(t, pltpu.HBM) for t in shards + outs], after)
    return (res[0], res[1], list(res[2:2 + n]), list(res[2 + n:2 + 2 * n])), res[-1]


def _gather_wait(in_flight, after, name):
    send_sems, recv_sems, shards, outs = in_flight
    n = len(shards)

    def body(*refs):
        for cp in _gather_copies(refs[:n], refs[n:2 * n], refs[2 * n], refs[2 * n + 1]):
            cp.wait_send()
            cp.wait_recv()

    res = pl.pallas_call(
        body, name=name, out_shape=tuple(pltpu.HBM(t.shape, t.dtype) for t in shards + outs),
        in_specs=(*[HBM_ONLY] * (2 * n), SEM_SPEC, SEM_SPEC, pl.BlockSpec(memory_space=pl.ANY)),
        out_specs=(HBM_ONLY,) * (2 * n), input_output_aliases={i: i for i in range(2 * n)},
        compiler_params=pltpu.CompilerParams(has_side_effects=SIDE_EFFECT))(*shards, *outs, send_sems, recv_sems, after)
    return list(res[:n]), list(res[n:])


class _PassToSibling:
    def __init__(self, shards, gathered):
        n = self.n = len(shards)
        self.ins = list(shards) + list(gathered)
        self.out_shape = tuple(jax.ShapeDtypeStruct(g.shape, g.dtype) for g in gathered)
        self.aliases = {n + a: a for a in range(n)}
        self.sems = [pltpu.SemaphoreType.DMA((3 * n,)), pltpu.SemaphoreType.DMA((3 * n,)),
                     pltpu.SemaphoreType.DMA((n,))]

    def _copies(self, ins, outs, sems):
        send_sems, recv_sems, local_sems = sems
        x, y, c = _mesh_pos()
        chips = [(1 - x, y), (x, 1 - y), (1 - x, 1 - y)]
        mine = [pltpu.make_async_copy(ins[a], outs[a].at[_slot((x, y, c))], local_sems.at[a]) for a in range(self.n)]
        passed, awaited = [], []
        for a in range(self.n):
            for j, chip in enumerate(chips):
                sems_j = dict(send_sem=send_sems.at[3 * a + j], recv_sem=recv_sems.at[3 * a + j],
                              device_id=(x, y, 1 - c), device_id_type=MESH)
                blk = outs[a].at[_slot((*chip, c))]
                passed.append(pltpu.make_async_remote_copy(src_ref=blk, dst_ref=blk, **sems_j))
                got = outs[a].at[_slot((*chip, 1 - c))]
                awaited.append(pltpu.make_async_remote_copy(src_ref=got, dst_ref=got, **sems_j))
        return mine, passed, awaited

    def start(self, ins, outs, sems):
        mine, passed, _ = self._copies(ins, outs, sems)
        for cp in mine + passed:
            cp.start()

    def finish(self, ins, outs, sems):
        mine, passed, awaited = self._copies(ins, outs, sems)
        for cp in passed:
            cp.wait_send()
        for cp in awaited:
            cp.wait_recv()
        for cp in mine:
            cp.wait()


def _reduce_sums(fulls, recv_core, core, tag):
    return [_pair_sum(f, r, core, f"rs_pair_{tag}{i}") for i, (f, r) in enumerate(zip(fulls, recv_core))]


def _local_step(x, tgt, mods, w_in_t, shards, small, chip, core):
    sh1, sc1, g1, sh2, sc2, g2 = mods
    norm1_g, rel_bias, gn_g, gn_b, norm2_g, norm_f_g = small
    tables = _ret_tables()
    buckets = jnp.asarray(_bucket_tables())

    h1 = _norm_mod_fwd(x, norm1_g, sh1, sc1, "norm1_fwd")
    flight_w, token_w = _gather_start(list(shards), h1, "gather_w_start")
    proj, slabs = _proj(h1, w_in_t, token_w)
    gated, ro, states = _ret_fwd(proj, tables, gn_g, gn_b)
    bias = _bias_build(rel_bias, buckets)
    outs, lses = [], []
    for gi in range(len(ATT_GROUPS)):
        comm = None
        if gi == 2:
            comm = _PassToSibling(*_gather_wait(flight_w, lses[1], "gather_w_wait"))
        res = _att_fwd(slabs, bias, gi, comm=comm)
        if gi == 2:
            res, gathered = res
        outs.append(res[0])
        lses.append(res[1])
    w_ret_out, w_att_out, w_o, w_ff1, w_ff2 = (_from_slots(g, ax) for g, ax in zip(gathered, BIG_AXES[1:]))
    att = _mix_fwd(outs, lses)
    ret_out = _mm(gated, w_ret_out, 'nn', tm=S, tn=256, tk=2048, name="ret_out")
    att_out = _mm(att, w_att_out, 'nn', tm=S, tn=512, tk=AW, name="att_out")
    merged = _merge_fwd(proj, ret_out, att_out)
    mixo, x1 = _mm(merged, w_o, 'nn', tm=S, tn=256, tk=D, name="w_o", res=x, gvec=g1)
    h2 = _norm_mod_fwd(x1, norm2_g, sh2, sc2, "norm2_fwd")
    u, act = _mm(h2, w_ff1, 'nn', tm=S, tn=512, tk=D, name="ff1", relu2=True)
    f, x2 = _mm(act, w_ff2, 'nn', tm=1024, tn=512, tk=D_FF, name="ff2", res=x1, gvec=g2)
    loss, dx2, g_normf, df, dg2 = _final_loss(x2, tgt, norm_f_g, f, g2)

    gw_ff2 = _mm(act, df, 'tn', tm=512, tn=D, tk=S, name="gw_ff2", out_dtype=BF16)
    du = _mm(df, w_ff2, 'nt', tm=S, tn=512, tk=D, name="d_act", out_dtype=BF16, relu2_of=u)
    gw_ff1 = _mm(h2, du, 'tn', tm=D, tn=512, tk=S, name="gw_ff1", out_dtype=BF16)
    fulls_a = [_to_slots(g, ax) for g, ax in zip((gw_ff1, gw_ff2), BIG_AXES[4:])]
    dh2, recv_core_a = _mm(du, w_ff1, 'nt', tm=1024, tn=1024, tk=2048, name="dh2", comm=_ExchangeCore(fulls_a))
    parts_a = _reduce_sums(fulls_a, recv_core_a, core, "a")
    flight_a, token_a = _chip_exchange_start(parts_a, "rs_a_start")
    dx1, dsc2, dsh2, g_norm2, dmixo, dg1 = _norm_mod_bwd(x1, norm2_g, sc2, dh2, dx2, "norm2_bwd", gate=(mixo, g1))

    gw_o = _mm(merged, dmixo, 'tn', tm=D, tn=512, tk=S, name="gw_o", out_dtype=BF16, after=token_a)
    dmerged = _mm(dmixo, w_o, 'nt', tm=S, tn=512, tk=D, name="dmerged")
    d_ret_out, d_att_out, dga, dgb = _merge_bwd(proj, ret_out, att_out, dmerged)
    gw_ret_out = _mm(gated, d_ret_out, 'tn', tm=512, tn=D, tk=S, name="gw_ret_out", out_dtype=BF16)
    gw_att_out = _mm(att, d_att_out, 'tn', tm=AW, tn=D, tk=S, name="gw_att_out", out_dtype=BF16)
    fulls_b = [_to_slots(g, ax) for g, ax in zip((gw_ret_out, gw_att_out, gw_o), BIG_AXES[1:4])]
    dgated, recv_core_b = _mm(d_ret_out, w_ret_out, 'nt', tm=S, tn=512, tk=D, name="dgated",
                              comm=_ExchangeCore(fulls_b))
    parts_b = _reduce_sums(fulls_b, recv_core_b, core, "b")
    flight_b, token_b = _chip_exchange_start(parts_b, "rs_b_start")
    datt = _mm(d_att_out, w_att_out, 'nt', tm=S, tn=AW, tk=D, name="datt", after=token_b)
    mix_grads = _mix_bwd(outs, lses, datt)
    datt_parts, ds_sums = [], []
    for gi in range(len(ATT_GROUPS)):
        dq, dk, dv, ds_sum = _att_bwd(slabs, bias, outs[gi], lses[gi], mix_grads[gi], mix_grads[3 + gi], gi)
        datt_parts += [dq.reshape(S, AW), dk.reshape(S, AW), dv.reshape(S, AW)]
        ds_sums.append(ds_sum)
    g_bias = _bias_grad(jnp.concatenate(ds_sums, axis=0), buckets)[:, :, 0].T.reshape(1, -1)
    dret, g_gn_g, g_gn_b = _ret_bwd(proj, tables, gn_g, gn_b, ro, states, dgated)
    parts_a, recv_chip_a = _chip_exchange_wait(flight_a, dret, "rs_a_wait")
    parts_b, recv_chip_b = _chip_exchange_wait(flight_b, dret, "rs_b_wait")
    red_a = [_chip_sum(p, r, chip, f"rs_sum_a{i}") for i, (p, r) in enumerate(zip(parts_a, recv_chip_a))]
    red_b = [_chip_sum(p, r, chip, f"rs_sum_b{i}") for i, (p, r) in enumerate(zip(parts_b, recv_chip_b))]
    dproj = jnp.concatenate([dret] + datt_parts + [dga, dgb], axis=1)
    in_flight, token = [], None
    for half in range(2):
        h1_half = h1[:, half * (D // 2):(half + 1) * (D // 2)]
        gw_half = _mm(dproj, h1_half, 'tn', tm=512, tn=D // 2, tk=S, name=f"gw_in{half}", out_dtype=BF16, after=token)
        full_in = [_to_slots(gw_half, 0)]
        recv_core_in = _run_comm(_ExchangeCore(full_in), f"rs_core_in{half}")
        part_in = _reduce_sums(full_in, recv_core_in, core, f"c{half}")
        flight, token = _chip_exchange_start(part_in, f"rs_in{half}_start")
        in_flight.append(flight)
    dh1 = _mm(dproj, w_in_t, 'nn', tm=1024, tn=1024, tk=2560, name="dh1", after=token)
    gx, dsc1, dsh1, g_norm1 = _norm_mod_bwd(x, norm1_g, sc1, dh1, dx1, "norm1_bwd")

    dmod = [dsh1, dsc1, dg1, dsh2, dsc2, dg2]
    small_g = [g_norm1, g_bias, g_gn_g, g_gn_b, g_norm2, g_normf]
    return loss, gx, in_flight, red_b + red_a, small_g, dmod


def _to_slots(g, axis):
    if axis == 0:
        return g.reshape(4, 2, g.shape[0] // N_DEV, g.shape[1])
    return g.reshape(g.shape[0], N_DEV, g.shape[1] // N_DEV).transpose(1, 0, 2).reshape(4, 2, g.shape[0], -1)


def _from_slots(w8, axis):
    if axis == 0:
        return w8.reshape(-1, w8.shape[2])
    return w8.transpose(1, 0, 2).reshape(w8.shape[1], -1)


BIG_AXES = (1, 0, 1, 0, 1, 0)


def kernel(x, c, w_ada, b_ada, norm1_g, w_in, rel_bias, ret_gn_g, ret_gn_b, w_ret_out, w_att_out, w_o, norm2_g, w_ff1, w_ff2, norm_f_g, loss_target, m_w_ada, m_b_ada, m_norm1_g, m_w_in, m_rel_bias, m_ret_gn_g, m_ret_gn_b, m_w_ret_out, m_w_att_out, m_w_o, m_norm2_g, m_w_ff1, m_w_ff2, m_norm_f_g, v_w_ada, v_b_ada, v_norm1_g, v_w_in, v_rel_bias, v_ret_gn_g, v_ret_gn_b, v_w_ret_out, v_w_att_out, v_w_o, v_norm2_g, v_w_ff1, v_w_ff2, v_norm_f_g):
    mx, my, mc = _mesh_pos()
    dev = 4 * mx + 2 * my + mc
    chip = jnp.reshape(2 * mx + my, (1,)).astype(jnp.int32)
    core = jnp.reshape(mc, (1,)).astype(jnp.int32)
    ada_w = D * 6 // N_DEV

    w_in, m_w_in, v_w_in = (jnp.transpose(t, (0, 2, 1)) for t in (w_in, m_w_in, v_w_in))

    shards = [w[0].astype(BF16) for w in (w_in, w_ret_out, w_att_out, w_o, w_ff1, w_ff2)]
    c_all, w_in8 = _run_comm(_Gather([c, shards[0]], relay=True), "gather_c_w_in")
    c_all = c_all.reshape(N_DEV, D)
    b_sl = lax.dynamic_slice(b_ada, (0, dev * ada_w), (1, ada_w))
    (mod_all,) = _run_comm(_Gather([_ada_fwd(c_all, w_ada[0], b_sl)]), "gather_mod")
    mod = lax.dynamic_index_in_dim(mod_all, dev, axis=1, keepdims=False).reshape(6, D)
    mods = tuple(mod[i:i + 1] for i in range(6))

    small = (norm1_g, rel_bias, ret_gn_g, ret_gn_b, norm2_g, norm_f_g.reshape(1, D))
    loss, gx, in_flight, big_red, small_g, dmod = _local_step(x[0], loss_target[0], mods, w_in8.reshape(IN_COLS, D),
                                                              shards[1:], small, chip, core)

    gathered = _run_comm(_Gather(dmod + small_g + [loss]), "gather_small")
    g_b_ada, dmod_all, (g_norm1, g_bias, g_gn_g, g_gn_b, g_norm2, g_normf, loss_sum) = _sum_small(gathered)
    loss_out = loss_sum[0, 0]
    g_w_ada = _ada_bwd(c_all, lax.dynamic_slice(dmod_all, (0, dev * ada_w), (N_DEV, ada_w)))

    names = ['w_ada', 'b_ada', 'norm1_g', 'w_in', 'rel_bias', 'ret_gn_g', 'ret_gn_b', 'w_ret_out', 'w_att_out',
             'w_o', 'norm2_g', 'w_ff1', 'w_ff2', 'norm_f_g']
    ws = dict(zip(names, (w_ada, b_ada, norm1_g, w_in, rel_bias, ret_gn_g, ret_gn_b, w_ret_out, w_att_out, w_o,
                          norm2_g, w_ff1, w_ff2, norm_f_g)))
    ms = dict(zip(names, (m_w_ada, m_b_ada, m_norm1_g, m_w_in, m_rel_bias, m_ret_gn_g, m_ret_gn_b, m_w_ret_out,
                          m_w_att_out, m_w_o, m_norm2_g, m_w_ff1, m_w_ff2, m_norm_f_g)))
    vs = dict(zip(names, (v_w_ada, v_b_ada, v_norm1_g, v_w_in, v_rel_bias, v_ret_gn_g, v_ret_gn_b, v_w_ret_out,
                          v_w_att_out, v_w_o, v_norm2_g, v_w_ff1, v_w_ff2, v_norm_f_g)))
    grads = dict(w_ada=g_w_ada, w_ret_out=big_red[0], w_att_out=big_red[1], w_o=big_red[2],
                 w_ff1=big_red[3], w_ff2=big_red[4], b_ada=g_b_ada, norm1_g=g_norm1, rel_bias=g_bias,
                 ret_gn_g=g_gn_g, ret_gn_b=g_gn_b, norm2_g=g_norm2, norm_f_g=g_normf)
    delta, new_m, new_v = {}, {}, {}
    for n in ('w_ada', 'w_ret_out', 'w_att_out', 'w_o', 'w_ff1', 'w_ff2'):
        delta[n], new_m[n], new_v[n] = _adamw(ws[n], grads[n], ms[n], vs[n], "adamw_" + n)
        grads[n] = grads[n].reshape(ws[n].shape)
    small_names = ('b_ada', 'norm1_g', 'rel_bias', 'ret_gn_g', 'ret_gn_b', 'norm2_g', 'norm_f_g')
    two_d = {n: (1, ws[n].size) if ws[n].ndim == 1 else ws[n].shape for n in small_names}
    d_, m_, v_ = _adamw_small(*[[src[n].reshape(two_d[n]) for n in small_names] for src in (ws, grads, ms, vs)])
    for i, n in enumerate(small_names):
        shp = ws[n].shape
        delta[n], new_m[n], new_v[n] = d_[i].reshape(shp), m_[i].reshape(shp), v_[i].reshape(shp)
        grads[n] = grads[n].reshape(shp)

    done = lax.optimization_barrier((gx, tuple(d_), tuple(delta[n] for n in ('w_ada', 'w_ret_out', 'w_att_out', 'w_o',
                                                                               'w_ff1', 'w_ff2'))))
    halves = []
    for half, flight in enumerate(in_flight):
        (part_in,), (recv_chip_in,) = _chip_exchange_wait(flight, done[0], f"rs_in{half}_wait")
        halves.append(_chip_sum(part_in, recv_chip_in, chip, f"rs_sum_c{half}"))
    grads['w_in'] = jnp.concatenate(halves, axis=1)
    delta['w_in'], new_m['w_in'], new_v['w_in'] = _adamw(w_in, grads['w_in'], m_w_in, v_w_in, "adamw_w_in")
    grads['w_in'] = grads['w_in'].reshape(w_in.shape)
    for d in (grads, delta, new_m, new_v):
        d['w_in'] = jnp.transpose(d['w_in'], (0, 2, 1))
    return (loss_out, gx[None], *[grads[n] for n in names], *[delta[n] for n in names],
            *[new_m[n] for n in names], *[new_v[n] for n in names])
```

```python
import functools
import math

import numpy as np
import jax
import jax.numpy as jnp
from jax import lax
from jax.experimental import pallas as pl
from jax.experimental.pallas import tpu as pltpu

F32 = jnp.float32
BF16 = jnp.bfloat16
MESH = pl.DeviceIdType.MESH

N_DEV = 8
S = 2048
D = 1024
RET_HEADS = 4
RET_DK = 256
RET_DV = 512
CHUNK = 128
N_CHUNK = S // CHUNK
ATT_GROUPS = ((128, 1), (512, 4), (2048, 16))
ATT_HG = 4
ATT_DH = 128
ATT_BLK = 128
N_BUCKETS = 32
MAX_DIST = 2048
D_FF = 4096
IN_COLS = 12800
OFF_RQ, OFF_RK, OFF_RV, OFF_RG, OFF_ATT = 0, 1024, 2048, 4096, 6144
OFF_GA, OFF_GB = 6144, 7168
RMS_EPS = 1e-6
GN_EPS = 1e-5
ADAM_LR, ADAM_B1, ADAM_B2, ADAM_EPS, ADAM_WD, ADAM_STEP = 0.001, 0.9, 0.999, 1e-08, 0.01, 10
VMEM_LIMIT = 48 * 1024 * 1024


def _pcall(body, **kw):
    return pl.pallas_call(body, **kw)


def _params(sem=None):
    return pltpu.CompilerParams(dimension_semantics=sem, vmem_limit_bytes=VMEM_LIMIT)


HBM_SPEC = pl.BlockSpec(memory_space=pl.ANY)


def _carry(body, comm, *, name, grid, in_specs, out_specs, out_shape, scratch_shapes=()):
    single = not isinstance(out_specs, (tuple, list))
    o_specs = (out_specs,) if single else tuple(out_specs)
    o_shape = (out_shape,) if single else tuple(out_shape)
    n_in, n_out, n_scr = len(in_specs), len(o_specs), len(scratch_shapes)
    nci, nco = len(comm.ins), len(comm.out_shape)
    total = int(np.prod(grid))

    def wrapped(*refs):
        bounds = np.cumsum([0, n_in, nci, n_out, nco, n_scr])
        a, ci, o, co, scr = (refs[bounds[i]:bounds[i + 1]] for i in range(5))
        sems = refs[bounds[5]:]
        flat = 0
        for d, g in enumerate(grid):
            flat = flat * g + pl.program_id(d)

        @pl.when(flat == 0)
        def _():
            comm.start(ci, co, sems)

        body(*a, *o, *scr)

        @pl.when(flat == total - 1)
        def _():
            comm.finish(ci, co, sems)

    aliases = {n_in + i: n_out + o for i, o in getattr(comm, "aliases", {}).items()}
    call = _pcall(wrapped, name=name, grid=grid, in_specs=list(in_specs) + [HBM_SPEC] * nci,
                  out_specs=o_specs + (HBM_SPEC,) * nco, out_shape=o_shape + tuple(comm.out_shape),
                  scratch_shapes=list(scratch_shapes) + list(comm.sems), input_output_aliases=aliases,
                  compiler_params=_params(("arbitrary",) * len(grid)))

    def run(*args):
        res = call(*args, *comm.ins)
        own = res[0] if single else tuple(res[:n_out])
        return own, tuple(res[n_out:])

    return run


def _run_comm(comm, name):
    nci, nco = len(comm.ins), len(comm.out_shape)

    def body(*refs):
        ci, co, sems = refs[:nci], refs[nci:nci + nco], refs[nci + nco:]
        comm.start(ci, co, sems)
        comm.finish(ci, co, sems)

    return _pcall(body, name=name, in_specs=[HBM_SPEC] * nci, out_specs=(HBM_SPEC,) * nco,
                  out_shape=tuple(comm.out_shape), scratch_shapes=list(comm.sems))(*comm.ins)


def _dot(a, b, dn):
    return lax.dot_general(a.astype(BF16), b.astype(BF16), (dn, ((), ())), preferred_element_type=F32)


NN = ((1,), (0,))
NT = ((1,), (1,))
TN = ((0,), (0,))


def _mm(a, b, mode, *, tm, tn, tk, name, out_dtype=F32, res=None, gvec=None, relu2=False, relu2_of=None, comm=None,
        after=None):
    if mode == 'nn':
        (M, K), (_, N) = a.shape, b.shape
        a_spec = pl.BlockSpec((tm, tk), lambda i, j, k: (i, k))
        b_spec = pl.BlockSpec((tk, tn), lambda i, j, k: (k, j))
        dn = NN
    elif mode == 'nt':
        (M, K), (N, _) = a.shape, b.shape
        a_spec = pl.BlockSpec((tm, tk), lambda i, j, k: (i, k))
        b_spec = pl.BlockSpec((tn, tk), lambda i, j, k: (j, k))
        dn = NT
    else:
        (K, M), (_, N) = a.shape, b.shape
        a_spec = pl.BlockSpec((tk, tm), lambda i, j, k: (k, i))
        b_spec = pl.BlockSpec((tk, tn), lambda i, j, k: (k, j))
        dn = TN
    assert M % tm == 0 and N % tn == 0 and K % tk == 0, (name, M, N, K)
    nk = K // tk
    fused = res is not None
    o_spec = pl.BlockSpec((tm, tn), lambda i, j, k: (i, j))

    def body(a_ref, b_ref, *rest):
        acc_ref = rest[-1] if nk > 1 else None
        if after is not None:
            rest = rest[1:]
        if fused:
            res_ref, g_ref, o_ref, x_ref = rest[:4]
        elif relu2_of is not None:
            u_ref, o_ref = rest[:2]
        elif relu2:
            o_ref, act_ref = rest[:2]
        else:
            o_ref = rest[0]

        def finish(acc):
            if relu2_of is not None:
                acc = acc * (2.0 * jnp.maximum(u_ref[...], 0.0))
            o_ref[...] = acc.astype(o_ref.dtype)
            if fused:
                x_ref[...] = res_ref[...] + g_ref[...] * acc
            if relu2:
                r = jnp.maximum(acc, 0.0)
                act_ref[...] = (r * r).astype(BF16)

        p = _dot(a_ref[...], b_ref[...], dn)
        if nk == 1:
            finish(p)
        else:
            k = pl.program_id(2)

            @pl.when(k == 0)
            def _():
                acc_ref[...] = p

            @pl.when(k > 0)
            def _():
                acc_ref[...] += p

            @pl.when(k == nk - 1)
            def _():
                finish(acc_ref[...])

    in_specs = [a_spec, b_spec]
    args = [a, b]
    if after is not None:
        in_specs.append(pl.BlockSpec(memory_space=pl.ANY))
        args.append(after)
    out_shape = jax.ShapeDtypeStruct((M, N), out_dtype)
    out_specs = o_spec
    if fused:
        in_specs += [pl.BlockSpec((tm, tn), lambda i, j, k: (i, j)), pl.BlockSpec((1, tn), lambda i, j, k: (0, j))]
        args += [res, gvec]
        out_shape = (out_shape, jax.ShapeDtypeStruct((M, N), F32))
        out_specs = (o_spec, pl.BlockSpec((tm, tn), lambda i, j, k: (i, j)))
    elif relu2_of is not None:
        in_specs.append(pl.BlockSpec((tm, tn), lambda i, j, k: (i, j)))
        args.append(relu2_of)
    elif relu2:
        out_shape = (out_shape, jax.ShapeDtypeStruct((M, N), BF16))
        out_specs = (o_spec, pl.BlockSpec((tm, tn), lambda i, j, k: (i, j)))
    kw = dict(name=name, grid=(M // tm, N // tn, nk), in_specs=in_specs, out_specs=out_specs,
              out_shape=out_shape, scratch_shapes=[pltpu.VMEM((tm, tn), F32)] if nk > 1 else [])
    if comm is not None:
        return _carry(body, comm, **kw)(*args)
    return _pcall(body, compiler_params=_params(("parallel", "parallel", "arbitrary")), **kw)(*args)


PROJ_TN = 512
ATT_T0, ATT_T1 = 6144 // PROJ_TN, 10752 // PROJ_TN
N_SLABS = (ATT_T1 - ATT_T0) * 4
MAIN_COLS = IN_COLS - (ATT_T1 - ATT_T0) * PROJ_TN


def _proj(h1, w_in_t, after):
    nj = IN_COLS // PROJ_TN

    def body(a_ref, b_ref, after_ref, main_ref, slab_ref):
        j = pl.program_id(1)
        is_att = (j >= ATT_T0) & (j < ATT_T1)
        chunks = [pl.ds(c * 512, 512) for c in range(S // 512)]

        @pl.when(jnp.logical_not(is_att))
        def _():
            for rows in chunks:
                main_ref[rows, :] = _dot(a_ref[rows, :], b_ref[...], NT)

        @pl.when(is_att)
        def _():
            for rows in chunks:
                p = _dot(a_ref[rows, :], b_ref[...], NT)
                for h in range(4):
                    slab_ref[h, rows, :] = p[:, h * 128:(h + 1) * 128]

    main_idx = lambda j: jnp.where(j < ATT_T0, j, jnp.where(j < ATT_T1, ATT_T0 - 1, j - (ATT_T1 - ATT_T0)))
    slab_idx = lambda j: jnp.clip(j - ATT_T0, 0, ATT_T1 - ATT_T0 - 1)
    return _pcall(
        body, name="proj", grid=(1, nj, 1),
        in_specs=[pl.BlockSpec((S, D), lambda i, j, k: (0, 0)), pl.BlockSpec((PROJ_TN, D), lambda i, j, k: (j, 0)),
                  HBM_SPEC],
        out_specs=(pl.BlockSpec((S, PROJ_TN), lambda i, j, k: (0, main_idx(j))),
                   pl.BlockSpec((4, S, 128), lambda i, j, k: (slab_idx(j), 0, 0))),
        out_shape=(jax.ShapeDtypeStruct((S, MAIN_COLS), F32), jax.ShapeDtypeStruct((N_SLABS, S, 128), F32)),
        compiler_params=_params(("arbitrary",) * 3))(h1, w_in_t, after)


TR = 256


def _row_spec(w=D):
    return pl.BlockSpec((TR, w), lambda i: (i, 0))


def _vec_spec(w=D):
    return pl.BlockSpec((1, w), lambda i: (0, 0))


def _norm_mod_fwd(x, g, sh, sc, name):
    def body(x_ref, g_ref, sh_ref, sc_ref, o_ref):
        xv = x_ref[...]
        rstd = lax.rsqrt(jnp.mean(xv * xv, axis=-1, keepdims=True) + RMS_EPS)
        n = xv * rstd * g_ref[...]
        o_ref[...] = (n * (1.0 + sc_ref[...]) + sh_ref[...]).astype(BF16)

    return _pcall(body, name=name, grid=(S // TR,), in_specs=[_row_spec(), _vec_spec(), _vec_spec(), _vec_spec()],
                  out_specs=_row_spec(), out_shape=jax.ShapeDtypeStruct((S, D), BF16),
                  compiler_params=_params(("parallel",)))(x, g, sh, sc)


def _norm_mod_bwd(x, g, sc, dh, dres, name, gate=None):
    gated = gate is not None

    def body(x_ref, g_ref, sc_ref, dh_ref, dres_ref, *rest):
        if gated:
            f_ref, gv_ref, dx_ref, dsc_ref, dsh_ref, dg_ref, dz_ref, dgv_ref = rest
        else:
            dx_ref, dsc_ref, dsh_ref, dg_ref = rest
        i = pl.program_id(0)
        xv = x_ref[...]
        dh = dh_ref[...]
        rstd = lax.rsqrt(jnp.mean(xv * xv, axis=-1, keepdims=True) + RMS_EPS)
        xhat = xv * rstd
        gv = g_ref[...]
        dn = dh * (1.0 + sc_ref[...])
        dxhat = dn * gv
        dx = dres_ref[...] + rstd * (dxhat - xhat * jnp.mean(dxhat * xhat, axis=-1, keepdims=True))
        dx_ref[...] = dx
        sums = [(dsc_ref, jnp.sum(dh * (xhat * gv), axis=0, keepdims=True)),
                (dsh_ref, jnp.sum(dh, axis=0, keepdims=True)),
                (dg_ref, jnp.sum(dn * xhat, axis=0, keepdims=True))]
        if gated:
            dz_ref[...] = (dx * gv_ref[...]).astype(BF16)
            sums.append((dgv_ref, jnp.sum(dx * f_ref[...], axis=0, keepdims=True)))

        @pl.when(i == 0)
        def _():
            for ref, p in sums:
                ref[...] = p

        @pl.when(i > 0)
        def _():
            for ref, p in sums:
                ref[...] += p

    vec = jax.ShapeDtypeStruct((1, D), F32)
    in_specs = [_row_spec(), _vec_spec(), _vec_spec(), _row_spec(), _row_spec()]
    out_specs = [_row_spec(), _vec_spec(), _vec_spec(), _vec_spec()]
    out_shape = [jax.ShapeDtypeStruct((S, D), F32), vec, vec, vec]
    args = [x, g, sc, dh, dres]
    if gated:
        in_specs += [_row_spec(), _vec_spec()]
        out_specs += [_row_spec(), _vec_spec()]
        out_shape += [jax.ShapeDtypeStruct((S, D), BF16), vec]
        args += list(gate)
    return _pcall(body, name=name, grid=(S // TR,), in_specs=in_specs, out_specs=tuple(out_specs),
                  out_shape=tuple(out_shape), compiler_params=_params(("arbitrary",)))(*args)


def _final_loss(x2, tgt, g, f, g2):
    def body(x_ref, t_ref, g_ref, f_ref, g2_ref, loss_ref, dx_ref, dg_ref, df_ref, dg2_ref):
        i = pl.program_id(0)
        xv = x_ref[...]
        gv = g_ref[...]
        rstd = lax.rsqrt(jnp.mean(xv * xv, axis=-1, keepdims=True) + RMS_EPS)
        xhat = xv * rstd
        err = xhat * gv - t_ref[...]
        dy = err * (1.0 / D)
        dxhat = dy * gv
        dx = rstd * (dxhat - xhat * jnp.mean(dxhat * xhat, axis=-1, keepdims=True))
        dx_ref[...] = dx
        df_ref[...] = (dx * g2_ref[...]).astype(BF16)
        p_g = jnp.sum(dy * xhat, axis=0, keepdims=True)
        p_g2 = jnp.sum(dx * f_ref[...], axis=0, keepdims=True)
        p_l = jnp.zeros((1, 128), F32) + 0.5 * jnp.sum(jnp.mean(err * err, axis=-1, keepdims=True))

        @pl.when(i == 0)
        def _():
            dg_ref[...] = p_g
            dg2_ref[...] = p_g2
            loss_ref[...] = p_l

        @pl.when(i > 0)
        def _():
            dg_ref[...] += p_g
            dg2_ref[...] += p_g2
            loss_ref[...] += p_l

    vec = jax.ShapeDtypeStruct((1, D), F32)
    return _pcall(body, name="final_loss", grid=(S // TR,),
                  in_specs=[_row_spec(), _row_spec(), _vec_spec(), _row_spec(), _vec_spec()],
                  out_specs=(_vec_spec(128), _row_spec(), _vec_spec(), _row_spec(), _vec_spec()),
                  out_shape=(jax.ShapeDtypeStruct((1, 128), F32), jax.ShapeDtypeStruct((S, D), F32), vec,
                             jax.ShapeDtypeStruct((S, D), BF16), vec),
                  compiler_params=_params(("arbitrary",)))(x2, tgt, g, f, g2)


HALF = 512


def _merge_fwd(proj, ret_out, att_out):
    def body(ga_ref, gb_ref, r_ref, a_ref, o_ref):
        o_ref[...] = (jax.nn.sigmoid(ga_ref[...]) * r_ref[...] + jax.nn.sigmoid(gb_ref[...]) * a_ref[...]).astype(BF16)

    blk = lambda off: pl.BlockSpec((TR, HALF), lambda i, j: (i, off // HALF + j))
    return _pcall(body, name="merge_fwd", grid=(S // TR, D // HALF),
                  in_specs=[blk(OFF_GA), blk(OFF_GB), blk(0), blk(0)], out_specs=blk(0),
                  out_shape=jax.ShapeDtypeStruct((S, D), BF16),
                  compiler_params=_params(("parallel", "parallel")))(proj, proj, ret_out, att_out)


def _merge_bwd(proj, ret_out, att_out, dmerged):
    def body(ga_ref, gb_ref, r_ref, a_ref, dm_ref, dr_ref, da_ref, dga_ref, dgb_ref):
        sa = jax.nn.sigmoid(ga_ref[...])
        sb = jax.nn.sigmoid(gb_ref[...])
        dm = dm_ref[...]
        dr_ref[...] = (dm * sa).astype(BF16)
        da_ref[...] = (dm * sb).astype(BF16)
        dga_ref[...] = (dm * r_ref[...] * (sa * (1.0 - sa))).astype(BF16)
        dgb_ref[...] = (dm * a_ref[...] * (sb * (1.0 - sb))).astype(BF16)

    blk = lambda off: pl.BlockSpec((TR, HALF), lambda i, j: (i, off // HALF + j))
    o = jax.ShapeDtypeStruct((S, D), BF16)
    return _pcall(body, name="merge_bwd", grid=(S // TR, D // HALF),
                  in_specs=[blk(OFF_GA), blk(OFF_GB), blk(0), blk(0), blk(0)], out_specs=(blk(0),) * 4,
                  out_shape=(o, o, o, o),
                  compiler_params=_params(("parallel", "parallel")))(proj, proj, ret_out, att_out, dmerged)


def _ret_tables():
    H, C = RET_HEADS, CHUNK
    log_g = jnp.log1p(-(2.0 ** (-5.0 - jnp.arange(H, dtype=F32))))
    idx = jnp.arange(C, dtype=F32)
    rel = idx[:, None] - idx[None, :]
    inner = jnp.where(rel >= 0, jnp.exp(log_g[:, None, None] * jnp.maximum(rel, 0.0)), 0.0)
    qd = jnp.exp(log_g[:, None] * (idx + 1.0))[:, :, None]
    kd = jnp.exp(log_g[:, None] * (C - 1.0 - idx))[:, :, None]
    cd = jnp.broadcast_to(jnp.exp(log_g * C)[:, None, None], (H, 1, 128))
    half = RET_DK // 2
    inv = 10000.0 ** (-jnp.arange(half, dtype=F32) / half)
    ang = jnp.arange(S, dtype=F32)[:, None] * inv[None, :]
    return inner, qd, kd, cd, jnp.cos(ang), jnp.sin(ang)


def _rot(x, cos, sin):
    x1, x2 = x[:, :128], x[:, 128:]
    return jnp.concatenate([x1 * cos - x2 * sin, x1 * sin + x2 * cos], axis=1)


def _rot_t(d, cos, sin):
    d1, d2 = d[:, :128], d[:, 128:]
    return jnp.concatenate([d1 * cos + d2 * sin, d2 * cos - d1 * sin], axis=1)


RET_COLS = OFF_ATT
RET_VW = RET_HEADS * RET_DV


def _ret_specs(chunk_of):
    ci = chunk_of
    whole = lambda shape: pl.BlockSpec(shape, lambda t: (0,) * len(shape))
    return [
        pl.BlockSpec((CHUNK, RET_COLS), lambda t: (ci(t), 0)),
        pl.BlockSpec((CHUNK, 128), lambda t: (ci(t), 0)),
        pl.BlockSpec((CHUNK, 128), lambda t: (ci(t), 0)),
        whole((RET_HEADS, CHUNK, CHUNK)), whole((RET_HEADS, CHUNK, 1)), whole((RET_HEADS, CHUNK, 1)),
        whole((RET_HEADS, 1, 128)), whole((1, RET_VW)), whole((1, RET_VW)),
    ]


def _ret_cols(h):
    q = slice(OFF_RQ + h * RET_DK, OFF_RQ + (h + 1) * RET_DK)
    k = slice(OFF_RK + h * RET_DK, OFF_RK + (h + 1) * RET_DK)
    v = slice(OFF_RV + h * RET_DV, OFF_RV + (h + 1) * RET_DV)
    g = slice(OFF_RG + h * RET_DV, OFF_RG + (h + 1) * RET_DV)
    return q, k, v, g, slice(h * RET_DV, (h + 1) * RET_DV)


def _ret_fwd(proj, tables, gn_g, gn_b, comm=None):
    inner, qd, kd, cd, cos, sin = tables

    def body(x_ref, cos_ref, sin_ref, in_ref, qd_ref, kd_ref, cd_ref, g_ref, b_ref,
             gated_ref, ro_ref, st_ref, s_scr):
        i = pl.program_id(0)

        @pl.when(i == 0)
        def _():
            s_scr[...] = jnp.zeros_like(s_scr)

        cosv, sinv = cos_ref[...], sin_ref[...]
        for h in range(RET_HEADS):
            cq, ck, cv, cg, co = _ret_cols(h)
            q = _rot(x_ref[:, cq], cosv, sinv)
            k = _rot(x_ref[:, ck], cosv, sinv) * (RET_DK ** -0.5)
            v = x_ref[:, cv]
            st = s_scr[h]
            st_ref[h] = st
            s = _dot(q, k, NT) * in_ref[h]
            o = _dot(s, v, NN) + _dot(q, st, NN) * qd_ref[h]
            s_scr[h] = st * cd_ref[h, :, :1] + _dot(k * kd_ref[h], v, TN)
            ro_ref[:, co] = o
            mu = jnp.mean(o, axis=-1, keepdims=True)
            oc = o - mu
            var = jnp.mean(oc * oc, axis=-1, keepdims=True)
            rn = oc * lax.rsqrt(var + GN_EPS) * g_ref[:, co] + b_ref[:, co]
            rg = x_ref[:, cg]
            gated_ref[:, co] = (rg * jax.nn.sigmoid(rg) * rn).astype(BF16)

    ospec = pl.BlockSpec((CHUNK, RET_VW), lambda t: (t, 0))
    kw = dict(name="ret_fwd", grid=(N_CHUNK,), in_specs=_ret_specs(lambda t: t),
              out_specs=(ospec, ospec, pl.BlockSpec((RET_HEADS, None, RET_DK, RET_DV), lambda t: (0, t, 0, 0))),
              out_shape=(jax.ShapeDtypeStruct((S, RET_VW), BF16), jax.ShapeDtypeStruct((S, RET_VW), F32),
                         jax.ShapeDtypeStruct((RET_HEADS, N_CHUNK, RET_DK, RET_DV), F32)),
              scratch_shapes=[pltpu.VMEM((RET_HEADS, RET_DK, RET_DV), F32)])
    args = (proj, cos, sin, inner, qd, kd, cd, gn_g, gn_b)
    if comm is not None:
        return _carry(body, comm, **kw)(*args)
    return _pcall(body, compiler_params=_params(("arbitrary",)), **kw)(*args)


def _ret_bwd(proj, tables, gn_g, gn_b, ro, states, dgated, comm=None):
    inner, qd, kd, cd, cos, sin = tables
    last = N_CHUNK - 1

    def body(x_ref, cos_ref, sin_ref, in_ref, qd_ref, kd_ref, cd_ref, g_ref, b_ref, ro_ref, st_ref, dg_ref,
             dx_ref, gg_ref, gb_ref, gs_scr):
        t = pl.program_id(0)

        @pl.when(t == 0)
        def _():
            gs_scr[...] = jnp.zeros_like(gs_scr)
            gg_ref[...] = jnp.zeros_like(gg_ref)
            gb_ref[...] = jnp.zeros_like(gb_ref)

        cosv, sinv = cos_ref[...], sin_ref[...]
        for h in range(RET_HEADS):
            cq, ck, cv, cg, co = _ret_cols(h)
            q = _rot(x_ref[:, cq], cosv, sinv)
            k = _rot(x_ref[:, ck], cosv, sinv) * (RET_DK ** -0.5)
            v = x_ref[:, cv]
            qdv, kdv, dm = qd_ref[h], kd_ref[h], in_ref[h]
            st = st_ref[h]
            o = ro_ref[:, co]
            gv = g_ref[:, co]
            mu = jnp.mean(o, axis=-1, keepdims=True)
            oc = o - mu
            rstd = lax.rsqrt(jnp.mean(oc * oc, axis=-1, keepdims=True) + GN_EPS)
            ohat = oc * rstd
            rn = ohat * gv + b_ref[:, co]
            rg = x_ref[:, cg]
            sg = jax.nn.sigmoid(rg)
            dgt = dg_ref[:, co]
            drn = dgt * (rg * sg)
            dx_ref[:, cg] = (dgt * rn * (sg * (1.0 + rg * (1.0 - sg)))).astype(BF16)
            gg_ref[:, co] += jnp.sum(drn * ohat, axis=0, keepdims=True)
            gb_ref[:, co] += jnp.sum(drn, axis=0, keepdims=True)
            dohat = drn * gv
            do = rstd * (dohat - jnp.mean(dohat, axis=-1, keepdims=True)
                         - ohat * jnp.mean(dohat * ohat, axis=-1, keepdims=True))
            gs = gs_scr[h]
            s = _dot(q, k, NT) * dm
            dsr = _dot(do, v, NT) * dm
            dq = _dot(dsr, k, NN) + _dot(do, st, NT) * qdv
            dk = _dot(dsr, q, TN) + _dot(v, gs, NT) * kdv
            dv = _dot(s, do, TN) + _dot(k * kdv, gs, NN)
            gs_scr[h] = gs * cd_ref[h, :, :1] + _dot(q * qdv, do, TN)
            dx_ref[:, cq] = _rot_t(dq, cosv, sinv).astype(BF16)
            dx_ref[:, ck] = (_rot_t(dk, cosv, sinv) * (RET_DK ** -0.5)).astype(BF16)
            dx_ref[:, cv] = dv.astype(BF16)

    rev = lambda t: last - t
    vblk = pl.BlockSpec((CHUNK, RET_VW), lambda t: (rev(t), 0))
    vspec = pl.BlockSpec((1, RET_VW), lambda t: (0, 0))
    kw = dict(name="ret_bwd", grid=(N_CHUNK,),
              in_specs=_ret_specs(rev) + [vblk, pl.BlockSpec((RET_HEADS, None, RET_DK, RET_DV),
                                                             lambda t: (0, rev(t), 0, 0)), vblk],
              out_specs=(pl.BlockSpec((CHUNK, RET_COLS), lambda t: (rev(t), 0)), vspec, vspec),
              out_shape=(jax.ShapeDtypeStruct((S, RET_COLS), BF16), jax.ShapeDtypeStruct((1, RET_VW), F32),
                         jax.ShapeDtypeStruct((1, RET_VW), F32)),
              scratch_shapes=[pltpu.VMEM((RET_HEADS, RET_DK, RET_DV), F32)])
    args = (proj, cos, sin, inner, qd, kd, cd, gn_g, gn_b, ro, states, dgated)
    if comm is not None:
        return _carry(body, comm, **kw)(*args)
    return _pcall(body, compiler_params=_params(("arbitrary",)), **kw)(*args)


def _bucket_tables():
    qi = np.arange(ATT_BLK)[:, None]
    kj = np.arange(2 * ATT_BLK)[None, :]
    m = ATT_BLK + qi - kj
    out = []
    for win, dil in ATT_GROUPS:
        w = win // dil
        dist = (np.clip(m, 0, w) * dil).astype(np.int32)
        max_exact = N_BUCKETS // 2
        d_f = np.maximum(dist, 1).astype(np.float32)
        large = max_exact + (np.log(d_f / np.float32(max_exact)) / np.float32(math.log(MAX_DIST / max_exact))
                             * np.float32(N_BUCKETS - max_exact)).astype(np.int32)
        large = np.minimum(large, N_BUCKETS - 1)
        out.append(np.where(dist < max_exact, dist, large).astype(np.int32))
    return np.stack(out)


def _bias_build(rel_bias, buckets):
    def body(tab_ref, bk_ref, o_ref):
        hh = pl.program_id(0)
        bk = bk_ref[...]
        acc = jnp.zeros((ATT_BLK, 2 * ATT_BLK), F32)
        for b in range(N_BUCKETS):
            acc = jnp.where(bk == b, tab_ref[b, hh], acc)
        o_ref[...] = acc

    nh = len(ATT_GROUPS) * ATT_HG
    return _pcall(body, name="bias_build", grid=(nh,),
                  in_specs=[pl.BlockSpec(memory_space=pltpu.SMEM),
                            pl.BlockSpec((None, ATT_BLK, 2 * ATT_BLK), lambda hh: (hh // ATT_HG, 0, 0))],
                  out_specs=pl.BlockSpec((None, ATT_BLK, 2 * ATT_BLK), lambda hh: (hh, 0, 0)),
                  out_shape=jax.ShapeDtypeStruct((nh, ATT_BLK, 2 * ATT_BLK), F32),
                  compiler_params=_params(("parallel",)))(rel_bias, buckets)


def _bias_grad(ds_sum, buckets):
    def body(ds_ref, bk_ref, o_ref):
        bk = bk_ref[...]
        ds = ds_ref[...]
        rows = lax.broadcasted_iota(jnp.int32, (N_BUCKETS, 128), 0)
        acc = jnp.zeros((N_BUCKETS, 128), F32)
        for b in range(N_BUCKETS):
            acc = jnp.where(rows == b, jnp.sum(jnp.where(bk == b, ds, 0.0)), acc)
        o_ref[...] = acc

    nh = len(ATT_GROUPS) * ATT_HG
    return _pcall(body, name="bias_grad", grid=(nh,),
                  in_specs=[pl.BlockSpec((None, ATT_BLK, 2 * ATT_BLK), lambda hh: (hh, 0, 0)),
                            pl.BlockSpec((None, ATT_BLK, 2 * ATT_BLK), lambda hh: (hh // ATT_HG, 0, 0))],
                  out_specs=pl.BlockSpec((None, N_BUCKETS, 128), lambda hh: (hh, 0, 0)),
                  out_shape=jax.ShapeDtypeStruct((nh, N_BUCKETS, 128), F32),
                  compiler_params=_params(("parallel",)))(ds_sum, buckets)


def _att_valid(n):
    qi = lax.broadcasted_iota(jnp.int32, (ATT_BLK, 2 * ATT_BLK), 0)
    kj = lax.broadcasted_iota(jnp.int32, (ATT_BLK, 2 * ATT_BLK), 1)
    m = ATT_BLK + qi - kj
    first_key = jnp.where(n > 0, 0, ATT_BLK)
    return (m >= 0) & (m <= ATT_BLK) & (kj >= first_key)


ATT_HP = (1, 2, 2)


def _att_geometry(gi):
    _, dil = ATT_GROUPS[gi]
    return dil, S // dil // ATT_BLK, ATT_HP[gi]


def _blk(dil, r, n):
    if dil == 1:
        return pl.ds(n * ATT_BLK, ATT_BLK)
    return pl.ds(r + n * ATT_BLK * dil, ATT_BLK, stride=dil)


def _slab_specs(gi):
    _, _, hp = _att_geometry(gi)
    per = ATT_HG // hp
    return [pl.BlockSpec((hp, S, ATT_DH), lambda g, r, part=part: ((3 * gi + part) * per + g, 0, 0))
            for part in range(3)]


def _head_specs(gi, count):
    _, _, hp = _att_geometry(gi)
    return [pl.BlockSpec((hp, S, ATT_DH), lambda g, r: (g, 0, 0))] * count


def _bias_spec(gi):
    _, _, hp = _att_geometry(gi)
    return pl.BlockSpec((hp, ATT_BLK, 2 * ATT_BLK), lambda g, r: (gi * (ATT_HG // hp) + g, 0, 0))


def _att_fwd(slabs, bias, gi, comm=None):
    dil, nb, hp = _att_geometry(gi)
    scale = ATT_DH ** -0.5

    def body(q_ref, k_ref, v_ref, bias_ref, o_ref, l_ref):
        r = pl.program_id(1)
        for n in range(nb):
            valid = _att_valid(n)
            prev = _blk(dil, r, max(n - 1, 0))
            cur = _blk(dil, r, n)
            for h in range(hp):
                kk = jnp.concatenate([k_ref[h, prev, :], k_ref[h, cur, :]], axis=0)
                vv = jnp.concatenate([v_ref[h, prev, :], v_ref[h, cur, :]], axis=0)
                s = _dot(q_ref[h, cur, :], kk, NT) * scale + bias_ref[h]
                s = jnp.where(valid, s, -1e30)
                mx = jnp.max(s, axis=-1, keepdims=True)
                e = jnp.exp(s - mx)
                den = jnp.sum(e, axis=-1, keepdims=True)
                o_ref[h, cur, :] = _dot(e / den, vv, NN)
                l_ref[h, cur, :] = jnp.broadcast_to(mx + jnp.log(den), (ATT_BLK, ATT_DH))

    osh = jax.ShapeDtypeStruct((ATT_HG, S, ATT_DH), F32)
    kw = dict(name=f"att_fwd{gi}", grid=(ATT_HG // hp, dil), in_specs=_slab_specs(gi) + [_bias_spec(gi)],
              out_specs=tuple(_head_specs(gi, 2)), out_shape=(osh, osh))
    if comm is not None:
        return _carry(body, comm, **kw)(slabs, slabs, slabs, bias)
    return _pcall(body, compiler_params=_params(("parallel", "arbitrary")), **kw)(slabs, slabs, slabs, bias)


def _att_bwd(slabs, bias, o, lse, do, dlse, gi, comm=None):
    dil, nb, hp = _att_geometry(gi)
    per = ATT_HG // hp
    scale = ATT_DH ** -0.5
    wh = hp * ATT_DH
    wide = lambda t: jnp.concatenate([t, t], axis=1)

    def body(q_ref, k_ref, v_ref, bias_ref, o_ref, l_ref, do_ref, dl_ref, dq_ref, dk_ref, dv_ref, ds_ref):
        r = pl.program_id(1)

        @pl.when(r == 0)
        def _():
            ds_ref[...] = jnp.zeros_like(ds_ref)

        for h in range(hp):
            sl = slice(h * ATT_DH, (h + 1) * ATT_DH)
            carry_k = carry_v = None
            for n in range(nb):
                valid = _att_valid(n)
                prev = _blk(dil, r, max(n - 1, 0))
                cur = _blk(dil, r, n)
                q = q_ref[h, cur, :]
                kk = jnp.concatenate([k_ref[h, prev, :], k_ref[h, cur, :]], axis=0)
                vv = jnp.concatenate([v_ref[h, prev, :], v_ref[h, cur, :]], axis=0)
                dov = do_ref[h, cur, :]
                s = _dot(q, kk, NT) * scale + bias_ref[h]
                p = jnp.where(valid, jnp.exp(s - wide(l_ref[h, cur, :])), 0.0)
                dp = _dot(dov, vv, NT)
                delta = jnp.sum(dov * o_ref[h, cur, :], axis=-1, keepdims=True)
                ds = p * (dp - delta + wide(dl_ref[h, cur, :]))
                ds_ref[h] += ds
                out_rows = pl.ds(n * ATT_BLK, ATT_BLK)
                dq_ref[out_rows, sl] = (_dot(ds, kk, NN) * scale).astype(BF16)
                dkk = _dot(ds, q, TN) * scale
                dvv = _dot(p, dov, TN)
                if n > 0:
                    before = pl.ds((n - 1) * ATT_BLK, ATT_BLK)
                    dk_ref[before, sl] = (carry_k + dkk[:ATT_BLK]).astype(BF16)
                    dv_ref[before, sl] = (carry_v + dvv[:ATT_BLK]).astype(BF16)
                carry_k, carry_v = dkk[ATT_BLK:], dvv[ATT_BLK:]
            last = pl.ds((nb - 1) * ATT_BLK, ATT_BLK)
            dk_ref[last, sl] = carry_k.astype(BF16)
            dv_ref[last, sl] = carry_v.astype(BF16)

    out_spec = pl.BlockSpec((S // dil, wh), lambda g, r: (0, r * per + g))
    osh = jax.ShapeDtypeStruct((S // dil, dil * AW), BF16)
    kw = dict(name=f"att_bwd{gi}", grid=(per, dil), in_specs=_slab_specs(gi) + [_bias_spec(gi)] + _head_specs(gi, 4),
              out_specs=(out_spec, out_spec, out_spec,
                         pl.BlockSpec((hp, ATT_BLK, 2 * ATT_BLK), lambda g, r: (g, 0, 0))),
              out_shape=(osh, osh, osh, jax.ShapeDtypeStruct((ATT_HG, ATT_BLK, 2 * ATT_BLK), F32)))
    args = (slabs, slabs, slabs, bias, o, lse, do, dlse)
    if comm is not None:
        return _carry(body, comm, **kw)(*args)
    return _pcall(body, compiler_params=_params(("arbitrary", "arbitrary")), **kw)(*args)


AW = ATT_HG * ATT_DH


def _mix_weights(l0, l1, l2):
    mx = jnp.maximum(jnp.maximum(l0, l1), l2)
    e0, e1, e2 = jnp.exp(l0 - mx), jnp.exp(l1 - mx), jnp.exp(l2 - mx)
    den = e0 + e1 + e2
    return e0 / den, e1 / den, e2 / den


def _heads_spec():
    return pl.BlockSpec((ATT_HG, TR, ATT_DH), lambda i: (0, i, 0))


def _mix_fwd(os_, ls, comm=None):
    def body(o0, o1, o2, l0, l1, l2, att_ref):
        for h in range(ATT_HG):
            w0, w1, w2 = _mix_weights(l0[h], l1[h], l2[h])
            att_ref[:, h * ATT_DH:(h + 1) * ATT_DH] = (w0 * o0[h] + w1 * o1[h] + w2 * o2[h]).astype(BF16)

    kw = dict(name="mix_fwd", grid=(S // TR,), in_specs=[_heads_spec()] * 6, out_specs=_row_spec(AW),
              out_shape=jax.ShapeDtypeStruct((S, AW), BF16))
    if comm is not None:
        return _carry(body, comm, **kw)(*os_, *ls)
    return _pcall(body, compiler_params=_params(("parallel",)), **kw)(*os_, *ls)


def _mix_bwd(os_, ls, datt):
    def body(o0, o1, o2, l0, l1, l2, da_ref, d0, d1, d2, e0, e1, e2):
        for h in range(ATT_HG):
            ws = _mix_weights(l0[h], l1[h], l2[h])
            da = da_ref[:, h * ATT_DH:(h + 1) * ATT_DH]
            dws = []
            for o_ref, w, d_ref in zip((o0, o1, o2), ws, (d0, d1, d2)):
                d_ref[h] = w * da
                dws.append(jnp.broadcast_to(jnp.sum(da * o_ref[h], axis=-1, keepdims=True), (TR, ATT_DH)))
            tot = ws[0] * dws[0] + ws[1] * dws[1] + ws[2] * dws[2]
            for w, dw, e_ref in zip(ws, dws, (e0, e1, e2)):
                e_ref[h] = w * (dw - tot)

    o = jax.ShapeDtypeStruct((ATT_HG, S, ATT_DH), F32)
    return _pcall(body, name="mix_bwd", grid=(S // TR,), in_specs=[_heads_spec()] * 6 + [_row_spec(AW)],
                  out_specs=(_heads_spec(),) * 6, out_shape=(o,) * 6,
                  compiler_params=_params(("parallel",)))(*os_, *ls, datt)


def _ada_fwd(c_all, w_sh, b_sl):
    def body(c_ref, w_ref, b_ref, o_ref):
        cv = c_ref[...]
        o_ref[...] = _dot(cv * jax.nn.sigmoid(cv), w_ref[...], NN) + b_ref[...]

    return _pcall(body, name="ada_fwd", out_shape=jax.ShapeDtypeStruct((N_DEV, w_sh.shape[1]), F32),
                  compiler_params=_params())(c_all, w_sh, b_sl)


def _ada_bwd(c_all, dm_sl):
    def body(c_ref, d_ref, o_ref):
        cv = c_ref[...]
        o_ref[...] = _dot(cv * jax.nn.sigmoid(cv), d_ref[...], TN)

    return _pcall(body, name="ada_bwd", out_shape=jax.ShapeDtypeStruct((D, dm_sl.shape[1]), F32),
                  compiler_params=_params())(c_all, dm_sl)


N_MOD = 6


def _sum_small(gathered):
    n = len(gathered)

    def body(*refs):
        ins, (gb_ref, dm_ref), outs = refs[:n], refs[n:n + 2], refs[n + 2:]

        def total(r):
            acc = r[0]
            for e in range(1, N_DEV):
                acc = acc + r[e]
            return acc

        for i in range(N_MOD):
            cols = slice(i * D, (i + 1) * D)
            gb_ref[:, cols] = total(ins[i])
            for e in range(N_DEV):
                dm_ref[e:e + 1, cols] = ins[i][e]
        for r, o_ref in zip(ins[N_MOD:], outs):
            o_ref[...] = total(r)

    shapes = (jax.ShapeDtypeStruct((1, N_MOD * D), F32), jax.ShapeDtypeStruct((N_DEV, N_MOD * D), F32),
              *[jax.ShapeDtypeStruct(g.shape[1:], F32) for g in gathered[N_MOD:]])
    res = _pcall(body, name="sum_small", out_shape=shapes, compiler_params=_params())(*gathered)
    return res[0], res[1], res[2:]


def _row_tile(m, n):
    t = max(8, min(m, (1 << 19) // n // 8 * 8))
    while m % t:
        t -= 8
    return t


def _pair_sum(full, recv, sel, name):
    _, _, m, n = full.shape
    t = _row_tile(m, n)

    def body(sel_ref, a_ref, b_ref, o_ref):
        o_ref[...] = (a_ref[...].astype(F32) + b_ref[...].astype(F32)).astype(o_ref.dtype)

    gs = pltpu.PrefetchScalarGridSpec(
        num_scalar_prefetch=1, grid=(4, m // t),
        in_specs=[pl.BlockSpec((None, None, t, n), lambda q, i, s: (q, s[0], i, 0)),
                  pl.BlockSpec((None, t, n), lambda q, i, s: (q, i, 0))],
        out_specs=pl.BlockSpec((None, t, n), lambda q, i, s: (q, i, 0)))
    return _pcall(body, name=name, grid_spec=gs, out_shape=jax.ShapeDtypeStruct((4, m, n), full.dtype),
                  compiler_params=_params(("parallel", "parallel")))(sel, full, recv)


def _chip_sum(part, recv, sel, name):
    _, m, n = part.shape
    t = _row_tile(m, n)

    def body(sel_ref, a_ref, r_ref, o_ref):
        o_ref[...] = ((a_ref[...].astype(F32) + r_ref[0].astype(F32)) + r_ref[1].astype(F32)) + r_ref[2].astype(F32)

    gs = pltpu.PrefetchScalarGridSpec(
        num_scalar_prefetch=1, grid=(m // t,),
        in_specs=[pl.BlockSpec((None, t, n), lambda i, s: (s[0], i, 0)),
                  pl.BlockSpec((3, t, n), lambda i, s: (0, i, 0))],
        out_specs=pl.BlockSpec((t, n), lambda i, s: (i, 0)))
    return _pcall(body, name=name, grid_spec=gs, out_shape=jax.ShapeDtypeStruct((m, n), F32),
                  compiler_params=_params(("parallel",)))(sel, part, recv)


def _adamw_math(w, g, m, v):
    nm = ADAM_B1 * m + (1.0 - ADAM_B1) * g
    nv = ADAM_B2 * v + (1.0 - ADAM_B2) * (g * g)
    m_hat = nm / (1.0 - ADAM_B1 ** ADAM_STEP)
    v_hat = nv / (1.0 - ADAM_B2 ** ADAM_STEP)
    return -ADAM_LR * (m_hat / (jnp.sqrt(v_hat) + ADAM_EPS) + ADAM_WD * w), nm, nv


def _adamw(w, g, m, v, name):
    _, rows, cols = w.shape
    t = _row_tile(rows, cols)

    def body(w_ref, g_ref, m_ref, v_ref, d_ref, nm_ref, nv_ref):
        d_ref[...], nm_ref[...], nv_ref[...] = _adamw_math(w_ref[...], g_ref[...], m_ref[...], v_ref[...])

    spec3 = pl.BlockSpec((None, t, cols), lambda i: (0, i, 0))
    spec2 = pl.BlockSpec((t, cols), lambda i: (i, 0))
    o = jax.ShapeDtypeStruct(w.shape, F32)
    return _pcall(body, name=name, grid=(rows // t,), in_specs=[spec3, spec2, spec3, spec3], out_specs=(spec3,) * 3,
                  out_shape=(o, o, o), compiler_params=_params(("parallel",)))(w, g, m, v)


def _adamw_small(ws, gs, ms, vs):
    n = len(ws)

    def body(*refs):
        for i in range(n):
            w_ref, g_ref, m_ref, v_ref = (refs[k * n + i] for k in range(4))
            d, nm, nv = _adamw_math(w_ref[...], g_ref[...], m_ref[...], v_ref[...])
            refs[4 * n + i][...] = d
            refs[5 * n + i][...] = nm
            refs[6 * n + i][...] = nv

    shapes = tuple(jax.ShapeDtypeStruct(w.shape, F32) for w in ws)
    res = _pcall(body, name="adamw_small", out_shape=shapes * 3, compiler_params=_params())(*ws, *gs, *ms, *vs)
    return res[:n], res[n:2 * n], res[2 * n:]


def _mesh_pos():
    return lax.axis_index("x"), lax.axis_index("y"), lax.axis_index("c")


class _Gather:
    def __init__(self, arrs, relay=False):
        self.relay = relay
        self.ins = list(arrs)
        na = self.na = len(arrs)
        self.out_shape = tuple(jax.ShapeDtypeStruct((N_DEV,) + a.shape, a.dtype) for a in arrs)
        self.sems = [pltpu.SemaphoreType.DMA((7 * na,)), pltpu.SemaphoreType.DMA((7 * na,)),
                     pltpu.SemaphoreType.DMA((na,))]

    def _copies(self, ins, outs, sems):
        send_sems, recv_sems, local_sems = sems
        x, y, c = _mesh_pos()
        me, sibling = (x, y, c), (x, y, 1 - c)
        chips = [(1 - x, y), (x, 1 - y), (1 - x, 1 - y)]

        def slot(p):
            return 4 * p[0] + 2 * p[1] + p[2]

        def copy(a, k, block, to, src=None):
            dst = outs[a].at[slot(block)]
            return pltpu.make_async_remote_copy(
                src_ref=dst if src is None else src, dst_ref=dst, send_sem=send_sems.at[7 * a + k],
                recv_sem=recv_sems.at[7 * a + k], device_id=to, device_id_type=MESH)

        mine = [pltpu.make_async_copy(ins[a], outs[a].at[slot(me)], local_sems.at[a]) for a in range(self.na)]
        direct = chips[:2] if self.relay else chips
        first = []
        for a in range(self.na):
            first.append(copy(a, 0, me, sibling, src=ins[a]))
            first += [copy(a, 1 + j, me, (*chip, c), src=ins[a]) for j, chip in enumerate(direct)]
        return me, sibling, chips, c, copy, mine, first

    def start(self, ins, outs, sems):
        *_, mine, first = self._copies(ins, outs, sems)
        for cp in mine + first:
            cp.start()

    def finish(self, ins, outs, sems):
        me, sibling, chips, c, copy, mine, first = self._copies(ins, outs, sems)
        x, y = me[0], me[1]
        passed = []
        for j, chip in enumerate(chips):
            for a in range(self.na):
                if self.relay and j == 2:
                    owner = ((x + 1 - c) % 2, (y + c) % 2, c)
                    cp = copy(a, 3, owner, ((x + c) % 2, (y + 1 - c) % 2, c))
                    cp.start()
                    passed.append(cp)
                copy(a, 1 + j, (*chip, c), me).wait_recv()
                cp = copy(a, 4 + j, (*chip, c), sibling)
                cp.start()
                passed.append(cp)
        for a in range(self.na):
            copy(a, 0, sibling, me).wait_recv()
            for j, chip in enumerate(chips):
                copy(a, 4 + j, (*chip, 1 - c), me).wait_recv()
        for cp in first + passed:
            cp.wait_send()
        for cp in mine:
            cp.wait()


class _ExchangeCore:
    def __init__(self, fulls):
        self.ins = list(fulls)
        self.out_shape = tuple(jax.ShapeDtypeStruct((4,) + f.shape[2:], f.dtype) for f in fulls)
        self.sems = [pltpu.SemaphoreType.DMA((4 * len(fulls),)), pltpu.SemaphoreType.DMA((4 * len(fulls),))]

    def _copies(self, ins, outs, sems):
        send_sems, recv_sems = sems
        x, y, c = _mesh_pos()
        return [pltpu.make_async_remote_copy(
            src_ref=ins[a].at[q, 1 - c], dst_ref=outs[a].at[q], send_sem=send_sems.at[4 * a + q],
            recv_sem=recv_sems.at[4 * a + q], device_id=(x, y, 1 - c), device_id_type=MESH)
            for a in range(len(self.ins)) for q in range(4)]

    def start(self, ins, outs, sems):
        for cp in self._copies(ins, outs, sems):
            cp.start()

    def finish(self, ins, outs, sems):
        for cp in self._copies(ins, outs, sems):
            cp.wait()


class _ExchangeChip:
    def __init__(self, parts):
        self.ins = list(parts)
        self.out_shape = tuple(jax.ShapeDtypeStruct((3,) + p.shape[1:], p.dtype) for p in parts)
        self.sems = [pltpu.SemaphoreType.DMA((3 * len(parts),)), pltpu.SemaphoreType.DMA((3 * len(parts),))]

    def _copies(self, ins, outs, sems):
        send_sems, recv_sems = sems
        x, y, c = _mesh_pos()
        chips = [(1 - x, y), (x, 1 - y), (1 - x, 1 - y)]
        return [pltpu.make_async_remote_copy(
            src_ref=ins[a].at[2 * px + py], dst_ref=outs[a].at[j], send_sem=send_sems.at[3 * a + j],
            recv_sem=recv_sems.at[3 * a + j], device_id=(px, py, c), device_id_type=MESH)
            for a in range(len(self.ins)) for j, (px, py) in enumerate(chips)]

    def start(self, ins, outs, sems):
        for cp in self._copies(ins, outs, sems):
            cp.start()

    def finish(self, ins, outs, sems):
        for cp in self._copies(ins, outs, sems):
            cp.wait()


HBM_ONLY = pl.BlockSpec(memory_space=pltpu.HBM)
SEM_SPEC = pl.BlockSpec(memory_space=pltpu.SEMAPHORE)
SIDE_EFFECT = pltpu.SideEffectType.DATAFLOW_SIDE_EFFECTING


def _chip_copies(p_refs, land_refs, send_sems, recv_sems):
    x, y, c = _mesh_pos()
    return [pltpu.make_async_remote_copy(
        src_ref=p_refs[a].at[2 * px + py], dst_ref=land_refs[a].at[j], send_sem=send_sems.at[3 * a + j],
        recv_sem=recv_sems.at[3 * a + j], device_id=(px, py, c), device_id_type=MESH)
        for a in range(len(p_refs)) for j, (px, py) in enumerate([(1 - x, y), (x, 1 - y), (1 - x, 1 - y)])]


def _chip_exchange_start(parts, name):
    n = len(parts)
    lands = [lax.empty((3,) + p.shape[1:], p.dtype) for p in parts]

    def body(*refs):
        p_refs, land_refs, (send_sems, recv_sems) = refs[:n], refs[n:2 * n], refs[2 * n:2 * n + 2]
        for cp in _chip_copies(p_refs, land_refs, send_sems, recv_sems):
            cp.start()
        token = refs[-1]
        token[...] = jnp.zeros_like(token)

    hbm = lambda t: pltpu.HBM(t.shape, t.dtype)
    res = pl.pallas_call(
        body, name=name,
        out_shape=(pltpu.SemaphoreType.DMA((3 * n,)), pltpu.SemaphoreType.DMA((3 * n,)), *[hbm(t) for t in parts + lands],
                   jax.ShapeDtypeStruct((8, 128), F32)),
        in_specs=(HBM_ONLY,) * (2 * n),
        out_specs=(SEM_SPEC, SEM_SPEC, *[HBM_ONLY] * (2 * n), pl.BlockSpec(memory_space=pltpu.VMEM)),
        input_output_aliases={i: 2 + i for i in range(2 * n)},
        compiler_params=pltpu.CompilerParams(has_side_effects=SIDE_EFFECT))(
        *[pltpu.with_memory_space_constraint(t, pltpu.HBM) for t in parts + lands])
    return (res[0], res[1], list(res[2:2 + n]), list(res[2 + n:2 + 2 * n])), res[-1]


def _chip_exchange_wait(in_flight, after, name):
    send_sems, recv_sems, parts, lands = in_flight
    n = len(parts)

    def body(*refs):
        p_refs, land_refs, (send_sems, recv_sems) = refs[:n], refs[n:2 * n], refs[2 * n:2 * n + 2]
        for cp in _chip_copies(p_refs, land_refs, send_sems, recv_sems):
            cp.wait_send()
            cp.wait_recv()

    res = pl.pallas_call(
        body, name=name, out_shape=tuple(pltpu.HBM(t.shape, t.dtype) for t in parts + lands),
        in_specs=(*[HBM_ONLY] * (2 * n), SEM_SPEC, SEM_SPEC, pl.BlockSpec(memory_space=pl.ANY)),
        out_specs=(HBM_ONLY,) * (2 * n), input_output_aliases={i: i for i in range(2 * n)},
        compiler_params=pltpu.CompilerParams(has_side_effects=SIDE_EFFECT))(*parts, *lands, send_sems, recv_sems, after)
    return list(res[:n]), list(res[n:])


def _slot(p):
    return 4 * p[0] + 2 * p[1] + p[2]


def _gather_copies(src_refs, out_refs, send_sems, recv_sems):
    x, y, c = _mesh_pos()
    targets = [(x, y, 1 - c), (1 - x, y, c), (x, 1 - y, c), (1 - x, 1 - y, c)]
    return [pltpu.make_async_remote_copy(
        src_ref=src_refs[a], dst_ref=out_refs[a].at[_slot((x, y, c))], send_sem=send_sems.at[4 * a + k],
        recv_sem=recv_sems.at[4 * a + k], device_id=to, device_id_type=MESH)
        for a in range(len(src_refs)) for k, to in enumerate(targets)]


def _gather_start(shards, after, name):
    n = len(shards)
    outs = [lax.empty((N_DEV,) + s.shape, s.dtype) for s in shards]

    def body(*refs):
        for cp in _gather_copies(refs[:n], refs[n:2 * n], refs[2 * n + 1], refs[2 * n + 2]):
            cp.start()
        token = refs[-1]
        token[...] = jnp.zeros_like(token)

    res = pl.pallas_call(
        body, name=name,
        out_shape=(pltpu.SemaphoreType.DMA((4 * n,)), pltpu.SemaphoreType.DMA((4 * n,)),
                   *[pltpu.HBM(t.shape, t.dtype) for t in shards + outs], jax.ShapeDtypeStruct((8, 128), F32)),
        in_specs=(*[HBM_ONLY] * (2 * n), pl.BlockSpec(memory_space=pl.ANY)),
        out_specs=(SEM_SPEC, SEM_SPEC, *[HBM_ONLY] * (2 * n), pl.BlockSpec(memory_space=pltpu.VMEM)),
        input_output_aliases={i: 2 + i for i in range(2 * n)},
        compiler_params=pltpu.CompilerParams(has_side_effects=SIDE_EFFECT))(
        *[pltpu.with_memory_space_constraint(t, pltpu.HBM) for t in shards + outs], after)
    return (res[0], res[1], list(res[2:2 + n]), list(res[2 + n:2 + 2 * n])), res[-1]


def _gather_wait(in_flight, after, name):
    send_sems, recv_sems, shards, outs = in_flight
    n = len(shards)

    def body(*refs):
        for cp in _gather_copies(refs[:n], refs[n:2 * n], refs[2 * n], refs[2 * n + 1]):
            cp.wait_send()
            cp.wait_recv()

    res = pl.pallas_call(
        body, name=name, out_shape=tuple(pltpu.HBM(t.shape, t.dtype) for t in shards + outs),
        in_specs=(*[HBM_ONLY] * (2 * n), SEM_SPEC, SEM_SPEC, pl.BlockSpec(memory_space=pl.ANY)),
        out_specs=(HBM_ONLY,) * (2 * n), input_output_aliases={i: i for i in range(2 * n)},
        compiler_params=pltpu.CompilerParams(has_side_effects=SIDE_EFFECT))(*shards, *outs, send_sems, recv_sems, after)
    return list(res[:n]), list(res[n:])


class _PassToSibling:
    def __init__(self, shards, gathered):
        n = self.n = len(shards)
        self.ins = list(shards) + list(gathered)
        self.out_shape = tuple(jax.ShapeDtypeStruct(g.shape, g.dtype) for g in gathered)
        self.aliases = {n + a: a for a in range(n)}
        self.sems = [pltpu.SemaphoreType.DMA((3 * n,)), pltpu.SemaphoreType.DMA((3 * n,)),
                     pltpu.SemaphoreType.DMA((n,))]

    def _copies(self, ins, outs, sems):
        send_sems, recv_sems, local_sems = sems
        x, y, c = _mesh_pos()
        chips = [(1 - x, y), (x, 1 - y), (1 - x, 1 - y)]
        mine = [pltpu.make_async_copy(ins[a], outs[a].at[_slot((x, y, c))], local_sems.at[a]) for a in range(self.n)]
        passed, awaited = [], []
        for a in range(self.n):
            for j, chip in enumerate(chips):
                sems_j = dict(send_sem=send_sems.at[3 * a + j], recv_sem=recv_sems.at[3 * a + j],
                              device_id=(x, y, 1 - c), device_id_type=MESH)
                blk = outs[a].at[_slot((*chip, c))]
                passed.append(pltpu.make_async_remote_copy(src_ref=blk, dst_ref=blk, **sems_j))
                got = outs[a].at[_slot((*chip, 1 - c))]
                awaited.append(pltpu.make_async_remote_copy(src_ref=got, dst_ref=got, **sems_j))
        return mine, passed, awaited

    def start(self, ins, outs, sems):
        mine, passed, _ = self._copies(ins, outs, sems)
        for cp in mine + passed:
            cp.start()

    def finish(self, ins, outs, sems):
        mine, passed, awaited = self._copies(ins, outs, sems)
        for cp in passed:
            cp.wait_send()
        for cp in awaited:
            cp.wait_recv()
        for cp in mine:
            cp.wait()


def _reduce_sums(fulls, recv_core, core, tag):
    return [_pair_sum(f, r, core, f"rs_pair_{tag}{i}") for i, (f, r) in enumerate(zip(fulls, recv_core))]


def _local_step(x, tgt, mods, w_in_t, shards, small, chip, core):
    sh1, sc1, g1, sh2, sc2, g2 = mods
    norm1_g, rel_bias, gn_g, gn_b, norm2_g, norm_f_g = small
    tables = _ret_tables()
    buckets = jnp.asarray(_bucket_tables())

    h1 = _norm_mod_fwd(x, norm1_g, sh1, sc1, "norm1_fwd")
    flight_w, token_w = _gather_start(list(shards), h1, "gather_w_start")
    proj, slabs = _proj(h1, w_in_t, token_w)
    gated, ro, states = _ret_fwd(proj, tables, gn_g, gn_b)
    bias = _bias_build(rel_bias, buckets)
    outs, lses = [], []
    for gi in range(len(ATT_GROUPS)):
        o, l = _att_fwd(slabs, bias, gi)
        outs.append(o)
        lses.append(l)
    att, gathered = _mix_fwd(outs, lses, comm=_PassToSibling(*_gather_wait(flight_w, lses[2], "gather_w_wait")))
    w_ret_out, w_att_out, w_o, w_ff1, w_ff2 = (_from_slots(g, ax) for g, ax in zip(gathered, BIG_AXES[1:]))
    ret_out = _mm(gated, w_ret_out, 'nn', tm=S, tn=256, tk=2048, name="ret_out")
    att_out = _mm(att, w_att_out, 'nn', tm=S, tn=512, tk=AW, name="att_out")
    merged = _merge_fwd(proj, ret_out, att_out)
    mixo, x1 = _mm(merged, w_o, 'nn', tm=S, tn=256, tk=D, name="w_o", res=x, gvec=g1)
    h2 = _norm_mod_fwd(x1, norm2_g, sh2, sc2, "norm2_fwd")
    u, act = _mm(h2, w_ff1, 'nn', tm=S, tn=512, tk=D, name="ff1", relu2=True)
    f, x2 = _mm(act, w_ff2, 'nn', tm=1024, tn=512, tk=D_FF, name="ff2", res=x1, gvec=g2)
    loss, dx2, g_normf, df, dg2 = _final_loss(x2, tgt, norm_f_g, f, g2)

    gw_ff2 = _mm(act, df, 'tn', tm=512, tn=D, tk=S, name="gw_ff2", out_dtype=BF16)
    du = _mm(df, w_ff2, 'nt', tm=S, tn=512, tk=D, name="d_act", out_dtype=BF16, relu2_of=u)
    gw_ff1 = _mm(h2, du, 'tn', tm=D, tn=512, tk=S, name="gw_ff1", out_dtype=BF16)
    fulls_a = [_to_slots(g, ax) for g, ax in zip((gw_ff1, gw_ff2), BIG_AXES[4:])]
    dh2, recv_core_a = _mm(du, w_ff1, 'nt', tm=1024, tn=1024, tk=2048, name="dh2", comm=_ExchangeCore(fulls_a))
    parts_a = _reduce_sums(fulls_a, recv_core_a, core, "a")
    flight_a, token_a = _chip_exchange_start(parts_a, "rs_a_start")
    dx1, dsc2, dsh2, g_norm2, dmixo, dg1 = _norm_mod_bwd(x1, norm2_g, sc2, dh2, dx2, "norm2_bwd", gate=(mixo, g1))

    gw_o = _mm(merged, dmixo, 'tn', tm=D, tn=512, tk=S, name="gw_o", out_dtype=BF16, after=token_a)
    dmerged = _mm(dmixo, w_o, 'nt', tm=S, tn=512, tk=D, name="dmerged")
    d_ret_out, d_att_out, dga, dgb = _merge_bwd(proj, ret_out, att_out, dmerged)
    gw_ret_out = _mm(gated, d_ret_out, 'tn', tm=512, tn=D, tk=S, name="gw_ret_out", out_dtype=BF16)
    gw_att_out = _mm(att, d_att_out, 'tn', tm=AW, tn=D, tk=S, name="gw_att_out", out_dtype=BF16)
    fulls_b = [_to_slots(g, ax) for g, ax in zip((gw_ret_out, gw_att_out, gw_o), BIG_AXES[1:4])]
    dgated, recv_core_b = _mm(d_ret_out, w_ret_out, 'nt', tm=S, tn=512, tk=D, name="dgated",
                              comm=_ExchangeCore(fulls_b))
    parts_b = _reduce_sums(fulls_b, recv_core_b, core, "b")
    flight_b, token_b = _chip_exchange_start(parts_b, "rs_b_start")
    datt = _mm(d_att_out, w_att_out, 'nt', tm=S, tn=AW, tk=D, name="datt", after=token_b)
    mix_grads = _mix_bwd(outs, lses, datt)
    datt_parts, ds_sums = [], []
    for gi in range(len(ATT_GROUPS)):
        dq, dk, dv, ds_sum = _att_bwd(slabs, bias, outs[gi], lses[gi], mix_grads[gi], mix_grads[3 + gi], gi)
        datt_parts += [dq.reshape(S, AW), dk.reshape(S, AW), dv.reshape(S, AW)]
        ds_sums.append(ds_sum)
    g_bias = _bias_grad(jnp.concatenate(ds_sums, axis=0), buckets)[:, :, 0].T.reshape(1, -1)
    dret, g_gn_g, g_gn_b = _ret_bwd(proj, tables, gn_g, gn_b, ro, states, dgated)
    parts_a, recv_chip_a = _chip_exchange_wait(flight_a, dret, "rs_a_wait")
    parts_b, recv_chip_b = _chip_exchange_wait(flight_b, dret, "rs_b_wait")
    red_a = [_chip_sum(p, r, chip, f"rs_sum_a{i}") for i, (p, r) in enumerate(zip(parts_a, recv_chip_a))]
    red_b = [_chip_sum(p, r, chip, f"rs_sum_b{i}") for i, (p, r) in enumerate(zip(parts_b, recv_chip_b))]
    dproj = jnp.concatenate([dret] + datt_parts + [dga, dgb], axis=1)
    in_flight, token = [], None
    for half in range(2):
        h1_half = h1[:, half * (D // 2):(half + 1) * (D // 2)]
        gw_half = _mm(dproj, h1_half, 'tn', tm=512, tn=D // 2, tk=S, name=f"gw_in{half}", out_dtype=BF16, after=token)
        full_in = [_to_slots(gw_half, 0)]
        recv_core_in = _run_comm(_ExchangeCore(full_in), f"rs_core_in{half}")
        part_in = _reduce_sums(full_in, recv_core_in, core, f"c{half}")
        flight, token = _chip_exchange_start(part_in, f"rs_in{half}_start")
        in_flight.append(flight)
    dh1 = _mm(dproj, w_in_t, 'nn', tm=1024, tn=1024, tk=2560, name="dh1", after=token)
    gx, dsc1, dsh1, g_norm1 = _norm_mod_bwd(x, norm1_g, sc1, dh1, dx1, "norm1_bwd")

    dmod = [dsh1, dsc1, dg1, dsh2, dsc2, dg2]
    small_g = [g_norm1, g_bias, g_gn_g, g_gn_b, g_norm2, g_normf]
    return loss, gx, in_flight, red_b + red_a, small_g, dmod


def _to_slots(g, axis):
    if axis == 0:
        return g.reshape(4, 2, g.shape[0] // N_DEV, g.shape[1])
    return g.reshape(g.shape[0], N_DEV, g.shape[1] // N_DEV).transpose(1, 0, 2).reshape(4, 2, g.shape[0], -1)


def _from_slots(w8, axis):
    if axis == 0:
        return w8.reshape(-1, w8.shape[2])
    return w8.transpose(1, 0, 2).reshape(w8.shape[1], -1)


BIG_AXES = (1, 0, 1, 0, 1, 0)


def kernel(x, c, w_ada, b_ada, norm1_g, w_in, rel_bias, ret_gn_g, ret_gn_b, w_ret_out, w_att_out, w_o, norm2_g, w_ff1, w_ff2, norm_f_g, loss_target, m_w_ada, m_b_ada, m_norm1_g, m_w_in, m_rel_bias, m_ret_gn_g, m_ret_gn_b, m_w_ret_out, m_w_att_out, m_w_o, m_norm2_g, m_w_ff1, m_w_ff2, m_norm_f_g, v_w_ada, v_b_ada, v_norm1_g, v_w_in, v_rel_bias, v_ret_gn_g, v_ret_gn_b, v_w_ret_out, v_w_att_out, v_w_o, v_norm2_g, v_w_ff1, v_w_ff2, v_norm_f_g):
    mx, my, mc = _mesh_pos()
    dev = 4 * mx + 2 * my + mc
    chip = jnp.reshape(2 * mx + my, (1,)).astype(jnp.int32)
    core = jnp.reshape(mc, (1,)).astype(jnp.int32)
    ada_w = D * 6 // N_DEV

    w_in, m_w_in, v_w_in = (jnp.transpose(t, (0, 2, 1)) for t in (w_in, m_w_in, v_w_in))

    shards = [w[0].astype(BF16) for w in (w_in, w_ret_out, w_att_out, w_o, w_ff1, w_ff2)]
    c_all, w_in8 = _run_comm(_Gather([c, shards[0]], relay=True), "gather_c_w_in")
    c_all = c_all.reshape(N_DEV, D)
    b_sl = lax.dynamic_slice(b_ada, (0, dev * ada_w), (1, ada_w))
    (mod_all,) = _run_comm(_Gather([_ada_fwd(c_all, w_ada[0], b_sl)]), "gather_mod")
    mod = lax.dynamic_index_in_dim(mod_all, dev, axis=1, keepdims=False).reshape(6, D)
    mods = tuple(mod[i:i + 1] for i in range(6))

    small = (norm1_g, rel_bias, ret_gn_g, ret_gn_b, norm2_g, norm_f_g.reshape(1, D))
    loss, gx, in_flight, big_red, small_g, dmod = _local_step(x[0], loss_target[0], mods, w_in8.reshape(IN_COLS, D),
                                                              shards[1:], small, chip, core)

    gathered = _run_comm(_Gather(dmod + small_g + [loss]), "gather_small")
    g_b_ada, dmod_all, (g_norm1, g_bias, g_gn_g, g_gn_b, g_norm2, g_normf, loss_sum) = _sum_small(gathered)
    loss_out = loss_sum[0, 0]
    g_w_ada = _ada_bwd(c_all, lax.dynamic_slice(dmod_all, (0, dev * ada_w), (N_DEV, ada_w)))

    names = ['w_ada', 'b_ada', 'norm1_g', 'w_in', 'rel_bias', 'ret_gn_g', 'ret_gn_b', 'w_ret_out', 'w_att_out',
             'w_o', 'norm2_g', 'w_ff1', 'w_ff2', 'norm_f_g']
    ws = dict(zip(names, (w_ada, b_ada, norm1_g, w_in, rel_bias, ret_gn_g, ret_gn_b, w_ret_out, w_att_out, w_o,
                          norm2_g, w_ff1, w_ff2, norm_f_g)))
    ms = dict(zip(names, (m_w_ada, m_b_ada, m_norm1_g, m_w_in, m_rel_bias, m_ret_gn_g, m_ret_gn_b, m_w_ret_out,
                          m_w_att_out, m_w_o, m_norm2_g, m_w_ff1, m_w_ff2, m_norm_f_g)))
    vs = dict(zip(names, (v_w_ada, v_b_ada, v_norm1_g, v_w_in, v_rel_bias, v_ret_gn_g, v_ret_gn_b, v_w_ret_out,
                          v_w_att_out, v_w_o, v_norm2_g, v_w_ff1, v_w_ff2, v_norm_f_g)))
    grads = dict(w_ada=g_w_ada, w_ret_out=big_red[0], w_att_out=big_red[1], w_o=big_red[2],
                 w_ff1=big_red[3], w_ff2=big_red[4], b_ada=g_b_ada, norm1_g=g_norm1, rel_bias=g_bias,
                 ret_gn_g=g_gn_g, ret_gn_b=g_gn_b, norm2_g=g_norm2, norm_f_g=g_normf)
    delta, new_m, new_v = {}, {}, {}
    for n in ('w_ada', 'w_ret_out', 'w_att_out', 'w_o', 'w_ff1', 'w_ff2'):
        delta[n], new_m[n], new_v[n] = _adamw(ws[n], grads[n], ms[n], vs[n], "adamw_" + n)
        grads[n] = grads[n].reshape(ws[n].shape)
    small_names = ('b_ada', 'norm1_g', 'rel_bias', 'ret_gn_g', 'ret_gn_b', 'norm2_g', 'norm_f_g')
    two_d = {n: (1, ws[n].size) if ws[n].ndim == 1 else ws[n].shape for n in small_names}
    d_, m_, v_ = _adamw_small(*[[src[n].reshape(two_d[n]) for n in small_names] for src in (ws, grads, ms, vs)])
    for i, n in enumerate(small_names):
        shp = ws[n].shape
        delta[n], new_m[n], new_v[n] = d_[i].reshape(shp), m_[i].reshape(shp), v_[i].reshape(shp)
        grads[n] = grads[n].reshape(shp)

    done = lax.optimization_barrier((gx, tuple(d_), tuple(delta[n] for n in ('w_ada', 'w_ret_out', 'w_att_out', 'w_o',
                                                                               'w_ff1', 'w_ff2'))))
    halves = []
    for half, flight in enumerate(in_flight):
        (part_in,), (recv_chip_in,) = _chip_exchange_wait(flight, done[0], f"rs_in{half}_wait")
        halves.append(_chip_sum(part_in, recv_chip_in, chip, f"rs_sum_c{half}"))
    grads['w_in'] = jnp.concatenate(halves, axis=1)
    delta['w_in'], new_m['w_in'], new_v['w_in'] = _adamw(w_in, grads['w_in'], m_w_in, v_w_in, "adamw_w_in")
    grads['w_in'] = grads['w_in'].reshape(w_in.shape)
    for d in (grads, delta, new_m, new_v):
        d['w_in'] = jnp.transpose(d['w_in'], (0, 2, 1))
    return (loss_out, gx[None], *[grads[n] for n in names], *[delta[n] for n in names],
            *[new_m[n] for n in names], *[new_v[n] for n in names])
```

```python
import functools
import math

import numpy as np
import jax
import jax.numpy as jnp
from jax import lax
from jax.experimental import pallas as pl
from jax.experimental.pallas import tpu as pltpu

F32 = jnp.float32
BF16 = jnp.bfloat16
MESH = pl.DeviceIdType.MESH

N_DEV = 8
S = 2048
D = 1024
RET_HEADS = 4
RET_DK = 256
RET_DV = 512
CHUNK = 128
N_CHUNK = S // CHUNK
ATT_GROUPS = ((128, 1), (512, 4), (2048, 16))
ATT_HG = 4
ATT_DH = 128
ATT_BLK = 128
N_BUCKETS = 32
MAX_DIST = 2048
D_FF = 4096
IN_COLS = 12800
OFF_RQ, OFF_RK, OFF_RV, OFF_RG, OFF_ATT = 0, 1024, 2048, 4096, 6144
OFF_GA, OFF_GB = 6144, 7168
RMS_EPS = 1e-6
GN_EPS = 1e-5
ADAM_LR, ADAM_B1, ADAM_B2, ADAM_EPS, ADAM_WD, ADAM_STEP = 0.001, 0.9, 0.999, 1e-08, 0.01, 10
VMEM_LIMIT = 48 * 1024 * 1024


def _pcall(body, **kw):
    return pl.pallas_call(body, **kw)


def _params(sem=None):
    return pltpu.CompilerParams(dimension_semantics=sem, vmem_limit_bytes=VMEM_LIMIT)


HBM_SPEC = pl.BlockSpec(memory_space=pl.ANY)


def _carry(body, comm, *, name, grid, in_specs, out_specs, out_shape, scratch_shapes=()):
    single = not isinstance(out_specs, (tuple, list))
    o_specs = (out_specs,) if single else tuple(out_specs)
    o_shape = (out_shape,) if single else tuple(out_shape)
    n_in, n_out, n_scr = len(in_specs), len(o_specs), len(scratch_shapes)
    nci, nco = len(comm.ins), len(comm.out_shape)
    total = int(np.prod(grid))

    def wrapped(*refs):
        bounds = np.cumsum([0, n_in, nci, n_out, nco, n_scr])
        a, ci, o, co, scr = (refs[bounds[i]:bounds[i + 1]] for i in range(5))
        sems = refs[bounds[5]:]
        flat = 0
        for d, g in enumerate(grid):
            flat = flat * g + pl.program_id(d)

        @pl.when(flat == 0)
        def _():
            comm.start(ci, co, sems)

        body(*a, *o, *scr)

        @pl.when(flat == total - 1)
        def _():
            comm.finish(ci, co, sems)

    aliases = {n_in + i: n_out + o for i, o in getattr(comm, "aliases", {}).items()}
    call = _pcall(wrapped, name=name, grid=grid, in_specs=list(in_specs) + [HBM_SPEC] * nci,
                  out_specs=o_specs + (HBM_SPEC,) * nco, out_shape=o_shape + tuple(comm.out_shape),
                  scratch_shapes=list(scratch_shapes) + list(comm.sems), input_output_aliases=aliases,
                  compiler_params=_params(("arbitrary",) * len(grid)))

    def run(*args):
        res = call(*args, *comm.ins)
        own = res[0] if single else tuple(res[:n_out])
        return own, tuple(res[n_out:])

    return run


def _run_comm(comm, name):
    nci, nco = len(comm.ins), len(comm.out_shape)

    def body(*refs):
        ci, co, sems = refs[:nci], refs[nci:nci + nco], refs[nci + nco:]
        comm.start(ci, co, sems)
        comm.finish(ci, co, sems)

    return _pcall(body, name=name, in_specs=[HBM_SPEC] * nci, out_specs=(HBM_SPEC,) * nco,
                  out_shape=tuple(comm.out_shape), scratch_shapes=list(comm.sems))(*comm.ins)


def _dot(a, b, dn):
    return lax.dot_general(a.astype(BF16), b.astype(BF16), (dn, ((), ())), preferred_element_type=F32)


NN = ((1,), (0,))
NT = ((1,), (1,))
TN = ((0,), (0,))


def _mm(a, b, mode, *, tm, tn, tk, name, out_dtype=F32, res=None, gvec=None, relu2=False, relu2_of=None, comm=None,
        after=None):
    if mode == 'nn':
        (M, K), (_, N) = a.shape, b.shape
        a_spec = pl.BlockSpec((tm, tk), lambda i, j, k: (i, k))
        b_spec = pl.BlockSpec((tk, tn), lambda i, j, k: (k, j))
        dn = NN
    elif mode == 'nt':
        (M, K), (N, _) = a.shape, b.shape
        a_spec = pl.BlockSpec((tm, tk), lambda i, j, k: (i, k))
        b_spec = pl.BlockSpec((tn, tk), lambda i, j, k: (j, k))
        dn = NT
    else:
        (K, M), (_, N) = a.shape, b.shape
        a_spec = pl.BlockSpec((tk, tm), lambda i, j, k: (k, i))
        b_spec = pl.BlockSpec((tk, tn), lambda i, j, k: (k, j))
        dn = TN
    assert M % tm == 0 and N % tn == 0 and K % tk == 0, (name, M, N, K)
    nk = K // tk
    fused = res is not None
    o_spec = pl.BlockSpec((tm, tn), lambda i, j, k: (i, j))

    def body(a_ref, b_ref, *rest):
        acc_ref = rest[-1] if nk > 1 else None
        if after is not None:
            rest = rest[1:]
        if fused:
            res_ref, g_ref, o_ref, x_ref = rest[:4]
        elif relu2_of is not None:
            u_ref, o_ref = rest[:2]
        elif relu2:
            o_ref, act_ref = rest[:2]
        else:
            o_ref = rest[0]

        def finish(acc):
            if relu2_of is not None:
                acc = acc * (2.0 * jnp.maximum(u_ref[...], 0.0))
            o_ref[...] = acc.astype(o_ref.dtype)
            if fused:
                x_ref[...] = res_ref[...] + g_ref[...] * acc
            if relu2:
                r = jnp.maximum(acc, 0.0)
                act_ref[...] = (r * r).astype(BF16)

        p = _dot(a_ref[...], b_ref[...], dn)
        if nk == 1:
            finish(p)
        else:
            k = pl.program_id(2)

            @pl.when(k == 0)
            def _():
                acc_ref[...] = p

            @pl.when(k > 0)
            def _():
                acc_ref[...] += p

            @pl.when(k == nk - 1)
            def _():
                finish(acc_ref[...])

    in_specs = [a_spec, b_spec]
    args = [a, b]
    if after is not None:
        in_specs.append(pl.BlockSpec(memory_space=pl.ANY))
        args.append(after)
    out_shape = jax.ShapeDtypeStruct((M, N), out_dtype)
    out_specs = o_spec
    if fused:
        in_specs += [pl.BlockSpec((tm, tn), lambda i, j, k: (i, j)), pl.BlockSpec((1, tn), lambda i, j, k: (0, j))]
        args += [res, gvec]
        out_shape = (out_shape, jax.ShapeDtypeStruct((M, N), F32))
        out_specs = (o_spec, pl.BlockSpec((tm, tn), lambda i, j, k: (i, j)))
    elif relu2_of is not None:
        in_specs.append(pl.BlockSpec((tm, tn), lambda i, j, k: (i, j)))
        args.append(relu2_of)
    elif relu2:
        out_shape = (out_shape, jax.ShapeDtypeStruct((M, N), BF16))
        out_specs = (o_spec, pl.BlockSpec((tm, tn), lambda i, j, k: (i, j)))
    kw = dict(name=name, grid=(M // tm, N // tn, nk), in_specs=in_specs, out_specs=out_specs,
              out_shape=out_shape, scratch_shapes=[pltpu.VMEM((tm, tn), F32)] if nk > 1 else [])
    if comm is not None:
        return _carry(body, comm, **kw)(*args)
    return _pcall(body, compiler_params=_params(("parallel", "parallel", "arbitrary")), **kw)(*args)


PROJ_TN = 512
ATT_T0, ATT_T1 = 6144 // PROJ_TN, 10752 // PROJ_TN
N_SLABS = (ATT_T1 - ATT_T0) * 4
MAIN_COLS = IN_COLS - (ATT_T1 - ATT_T0) * PROJ_TN


def _proj(h1, w_in_t, after):
    nj = IN_COLS // PROJ_TN

    def body(a_ref, b_ref, after_ref, main_ref, slab_ref):
        j = pl.program_id(1)
        is_att = (j >= ATT_T0) & (j < ATT_T1)
        chunks = [pl.ds(c * 512, 512) for c in range(S // 512)]

        @pl.when(jnp.logical_not(is_att))
        def _():
            for rows in chunks:
                main_ref[rows, :] = _dot(a_ref[rows, :], b_ref[...], NT)

        @pl.when(is_att)
        def _():
            for rows in chunks:
                p = _dot(a_ref[rows, :], b_ref[...], NT)
                for h in range(4):
                    slab_ref[h, rows, :] = p[:, h * 128:(h + 1) * 128]

    main_idx = lambda j: jnp.where(j < ATT_T0, j, jnp.where(j < ATT_T1, ATT_T0 - 1, j - (ATT_T1 - ATT_T0)))
    slab_idx = lambda j: jnp.clip(j - ATT_T0, 0, ATT_T1 - ATT_T0 - 1)
    return _pcall(
        body, name="proj", grid=(1, nj, 1),
        in_specs=[pl.BlockSpec((S, D), lambda i, j, k: (0, 0)), pl.BlockSpec((PROJ_TN, D), lambda i, j, k: (j, 0)),
                  HBM_SPEC],
        out_specs=(pl.BlockSpec((S, PROJ_TN), lambda i, j, k: (0, main_idx(j))),
                   pl.BlockSpec((4, S, 128), lambda i, j, k: (slab_idx(j), 0, 0))),
        out_shape=(jax.ShapeDtypeStruct((S, MAIN_COLS), F32), jax.ShapeDtypeStruct((N_SLABS, S, 128), F32)),
        compiler_params=_params(("arbitrary",) * 3))(h1, w_in_t, after)


TR = 256


def _row_spec(w=D):
    return pl.BlockSpec((TR, w), lambda i: (i, 0))


def _vec_spec(w=D):
    return pl.BlockSpec((1, w), lambda i: (0, 0))


def _norm_mod_fwd(x, g, sh, sc, name):
    def body(x_ref, g_ref, sh_ref, sc_ref, o_ref):
        xv = x_ref[...]
        rstd = lax.rsqrt(jnp.mean(xv * xv, axis=-1, keepdims=True) + RMS_EPS)
        n = xv * rstd * g_ref[...]
        o_ref[...] = (n * (1.0 + sc_ref[...]) + sh_ref[...]).astype(BF16)

    return _pcall(body, name=name, grid=(S // TR,), in_specs=[_row_spec(), _vec_spec(), _vec_spec(), _vec_spec()],
                  out_specs=_row_spec(), out_shape=jax.ShapeDtypeStruct((S, D), BF16),
                  compiler_params=_params(("parallel",)))(x, g, sh, sc)


def _norm_mod_bwd(x, g, sc, dh, dres, name, gate=None):
    gated = gate is not None

    def body(x_ref, g_ref, sc_ref, dh_ref, dres_ref, *rest):
        if gated:
            f_ref, gv_ref, dx_ref, dsc_ref, dsh_ref, dg_ref, dz_ref, dgv_ref = rest
        else:
            dx_ref, dsc_ref, dsh_ref, dg_ref = rest
        i = pl.program_id(0)
        xv = x_ref[...]
        dh = dh_ref[...]
        rstd = lax.rsqrt(jnp.mean(xv * xv, axis=-1, keepdims=True) + RMS_EPS)
        xhat = xv * rstd
        gv = g_ref[...]
        dn = dh * (1.0 + sc_ref[...])
        dxhat = dn * gv
        dx = dres_ref[...] + rstd * (dxhat - xhat * jnp.mean(dxhat * xhat, axis=-1, keepdims=True))
        dx_ref[...] = dx
        sums = [(dsc_ref, jnp.sum(dh * (xhat * gv), axis=0, keepdims=True)),
                (dsh_ref, jnp.sum(dh, axis=0, keepdims=True)),
                (dg_ref, jnp.sum(dn * xhat, axis=0, keepdims=True))]
        if gated:
            dz_ref[...] = (dx * gv_ref[...]).astype(BF16)
            sums.append((dgv_ref, jnp.sum(dx * f_ref[...], axis=0, keepdims=True)))

        @pl.when(i == 0)
        def _():
            for ref, p in sums:
                ref[...] = p

        @pl.when(i > 0)
        def _():
            for ref, p in sums:
                ref[...] += p

    vec = jax.ShapeDtypeStruct((1, D), F32)
    in_specs = [_row_spec(), _vec_spec(), _vec_spec(), _row_spec(), _row_spec()]
    out_specs = [_row_spec(), _vec_spec(), _vec_spec(), _vec_spec()]
    out_shape = [jax.ShapeDtypeStruct((S, D), F32), vec, vec, vec]
    args = [x, g, sc, dh, dres]
    if gated:
        in_specs += [_row_spec(), _vec_spec()]
        out_specs += [_row_spec(), _vec_spec()]
        out_shape += [jax.ShapeDtypeStruct((S, D), BF16), vec]
        args += list(gate)
    return _pcall(body, name=name, grid=(S // TR,), in_specs=in_specs, out_specs=tuple(out_specs),
                  out_shape=tuple(out_shape), compiler_params=_params(("arbitrary",)))(*args)


def _final_loss(x2, tgt, g, f, g2):
    def body(x_ref, t_ref, g_ref, f_ref, g2_ref, loss_ref, dx_ref, dg_ref, df_ref, dg2_ref):
        i = pl.program_id(0)
        xv = x_ref[...]
        gv = g_ref[...]
        rstd = lax.rsqrt(jnp.mean(xv * xv, axis=-1, keepdims=True) + RMS_EPS)
        xhat = xv * rstd
        err = xhat * gv - t_ref[...]
        dy = err * (1.0 / D)
        dxhat = dy * gv
        dx = rstd * (dxhat - xhat * jnp.mean(dxhat * xhat, axis=-1, keepdims=True))
        dx_ref[...] = dx
        df_ref[...] = (dx * g2_ref[...]).astype(BF16)
        p_g = jnp.sum(dy * xhat, axis=0, keepdims=True)
        p_g2 = jnp.sum(dx * f_ref[...], axis=0, keepdims=True)
        p_l = jnp.zeros((1, 128), F32) + 0.5 * jnp.sum(jnp.mean(err * err, axis=-1, keepdims=True))

        @pl.when(i == 0)
        def _():
            dg_ref[...] = p_g
            dg2_ref[...] = p_g2
            loss_ref[...] = p_l

        @pl.when(i > 0)
        def _():
            dg_ref[...] += p_g
            dg2_ref[...] += p_g2
            loss_ref[...] += p_l

    vec = jax.ShapeDtypeStruct((1, D), F32)
    return _pcall(body, name="final_loss", grid=(S // TR,),
                  in_specs=[_row_spec(), _row_spec(), _vec_spec(), _row_spec(), _vec_spec()],
                  out_specs=(_vec_spec(128), _row_spec(), _vec_spec(), _row_spec(), _vec_spec()),
                  out_shape=(jax.ShapeDtypeStruct((1, 128), F32), jax.ShapeDtypeStruct((S, D), F32), vec,
                             jax.ShapeDtypeStruct((S, D), BF16), vec),
                  compiler_params=_params(("arbitrary",)))(x2, tgt, g, f, g2)


HALF = 512


def _merge_fwd(proj, ret_out, att_out):
    def body(ga_ref, gb_ref, r_ref, a_ref, o_ref):
        o_ref[...] = (jax.nn.sigmoid(ga_ref[...]) * r_ref[...] + jax.nn.sigmoid(gb_ref[...]) * a_ref[...]).astype(BF16)

    blk = lambda off: pl.BlockSpec((TR, HALF), lambda i, j: (i, off // HALF + j))
    return _pcall(body, name="merge_fwd", grid=(S // TR, D // HALF),
                  in_specs=[blk(OFF_GA), blk(OFF_GB), blk(0), blk(0)], out_specs=blk(0),
                  out_shape=jax.ShapeDtypeStruct((S, D), BF16),
                  compiler_params=_params(("parallel", "parallel")))(proj, proj, ret_out, att_out)


def _merge_bwd(proj, ret_out, att_out, dmerged):
    def body(ga_ref, gb_ref, r_ref, a_ref, dm_ref, dr_ref, da_ref, dga_ref, dgb_ref):
        sa = jax.nn.sigmoid(ga_ref[...])
        sb = jax.nn.sigmoid(gb_ref[...])
        dm = dm_ref[...]
        dr_ref[...] = (dm * sa).astype(BF16)
        da_ref[...] = (dm * sb).astype(BF16)
        dga_ref[...] = (dm * r_ref[...] * (sa * (1.0 - sa))).astype(BF16)
        dgb_ref[...] = (dm * a_ref[...] * (sb * (1.0 - sb))).astype(BF16)

    blk = lambda off: pl.BlockSpec((TR, HALF), lambda i, j: (i, off // HALF + j))
    o = jax.ShapeDtypeStruct((S, D), BF16)
    return _pcall(body, name="merge_bwd", grid=(S // TR, D // HALF),
                  in_specs=[blk(OFF_GA), blk(OFF_GB), blk(0), blk(0), blk(0)], out_specs=(blk(0),) * 4,
                  out_shape=(o, o, o, o),
                  compiler_params=_params(("parallel", "parallel")))(proj, proj, ret_out, att_out, dmerged)


def _ret_tables():
    H, C = RET_HEADS, CHUNK
    log_g = jnp.log1p(-(2.0 ** (-5.0 - jnp.arange(H, dtype=F32))))
    idx = jnp.arange(C, dtype=F32)
    rel = idx[:, None] - idx[None, :]
    inner = jnp.where(rel >= 0, jnp.exp(log_g[:, None, None] * jnp.maximum(rel, 0.0)), 0.0)
    qd = jnp.exp(log_g[:, None] * (idx + 1.0))[:, :, None]
    kd = jnp.exp(log_g[:, None] * (C - 1.0 - idx))[:, :, None]
    cd = jnp.broadcast_to(jnp.exp(log_g * C)[:, None, None], (H, 1, 128))
    half = RET_DK // 2
    inv = 10000.0 ** (-jnp.arange(half, dtype=F32) / half)
    ang = jnp.arange(S, dtype=F32)[:, None] * inv[None, :]
    return inner, qd, kd, cd, jnp.cos(ang), jnp.sin(ang)


def _rot(x, cos, sin):
    x1, x2 = x[:, :128], x[:, 128:]
    return jnp.concatenate([x1 * cos - x2 * sin, x1 * sin + x2 * cos], axis=1)


def _rot_t(d, cos, sin):
    d1, d2 = d[:, :128], d[:, 128:]
    return jnp.concatenate([d1 * cos + d2 * sin, d2 * cos - d1 * sin], axis=1)


RET_COLS = OFF_ATT
RET_VW = RET_HEADS * RET_DV


def _ret_specs(chunk_of):
    ci = chunk_of
    whole = lambda shape: pl.BlockSpec(shape, lambda t: (0,) * len(shape))
    return [
        pl.BlockSpec((CHUNK, RET_COLS), lambda t: (ci(t), 0)),
        pl.BlockSpec((CHUNK, 128), lambda t: (ci(t), 0)),
        pl.BlockSpec((CHUNK, 128), lambda t: (ci(t), 0)),
        whole((RET_HEADS, CHUNK, CHUNK)), whole((RET_HEADS, CHUNK, 1)), whole((RET_HEADS, CHUNK, 1)),
        whole((RET_HEADS, 1, 128)), whole((1, RET_VW)), whole((1, RET_VW)),
    ]


def _ret_cols(h):
    q = slice(OFF_RQ + h * RET_DK, OFF_RQ + (h + 1) * RET_DK)
    k = slice(OFF_RK + h * RET_DK, OFF_RK + (h + 1) * RET_DK)
    v = slice(OFF_RV + h * RET_DV, OFF_RV + (h + 1) * RET_DV)
    g = slice(OFF_RG + h * RET_DV, OFF_RG + (h + 1) * RET_DV)
    return q, k, v, g, slice(h * RET_DV, (h + 1) * RET_DV)


def _ret_fwd(proj, tables, gn_g, gn_b, comm=None):
    inner, qd, kd, cd, cos, sin = tables

    def body(x_ref, cos_ref, sin_ref, in_ref, qd_ref, kd_ref, cd_ref, g_ref, b_ref,
             gated_ref, ro_ref, st_ref, s_scr):
        i = pl.program_id(0)

        @pl.when(i == 0)
        def _():
            s_scr[...] = jnp.zeros_like(s_scr)

        cosv, sinv = cos_ref[...], sin_ref[...]
        for h in range(RET_HEADS):
            cq, ck, cv, cg, co = _ret_cols(h)
            q = _rot(x_ref[:, cq], cosv, sinv)
            k = _rot(x_ref[:, ck], cosv, sinv) * (RET_DK ** -0.5)
            v = x_ref[:, cv]
            st = s_scr[h]
            st_ref[h] = st.astype(BF16)
            s = _dot(q, k, NT) * in_ref[h]
            o = _dot(s, v, NN) + _dot(q, st, NN) * qd_ref[h]
            s_scr[h] = st * cd_ref[h, :, :1] + _dot(k * kd_ref[h], v, TN)
            ro_ref[:, co] = o
            mu = jnp.mean(o, axis=-1, keepdims=True)
            oc = o - mu
            var = jnp.mean(oc * oc, axis=-1, keepdims=True)
            rn = oc * lax.rsqrt(var + GN_EPS) * g_ref[:, co] + b_ref[:, co]
            rg = x_ref[:, cg]
            gated_ref[:, co] = (rg * jax.nn.sigmoid(rg) * rn).astype(BF16)

    ospec = pl.BlockSpec((CHUNK, RET_VW), lambda t: (t, 0))
    kw = dict(name="ret_fwd", grid=(N_CHUNK,), in_specs=_ret_specs(lambda t: t),
              out_specs=(ospec, ospec, pl.BlockSpec((RET_HEADS, None, RET_DK, RET_DV), lambda t: (0, t, 0, 0))),
              out_shape=(jax.ShapeDtypeStruct((S, RET_VW), BF16), jax.ShapeDtypeStruct((S, RET_VW), F32),
                         jax.ShapeDtypeStruct((RET_HEADS, N_CHUNK, RET_DK, RET_DV), BF16)),
              scratch_shapes=[pltpu.VMEM((RET_HEADS, RET_DK, RET_DV), F32)])
    args = (proj, cos, sin, inner, qd, kd, cd, gn_g, gn_b)
    if comm is not None:
        return _carry(body, comm, **kw)(*args)
    return _pcall(body, compiler_params=_params(("arbitrary",)), **kw)(*args)


def _ret_bwd(proj, tables, gn_g, gn_b, ro, states, dgated, comm=None):
    inner, qd, kd, cd, cos, sin = tables
    last = N_CHUNK - 1

    def body(x_ref, cos_ref, sin_ref, in_ref, qd_ref, kd_ref, cd_ref, g_ref, b_ref, ro_ref, st_ref, dg_ref,
             dx_ref, gg_ref, gb_ref, gs_scr):
        t = pl.program_id(0)

        @pl.when(t == 0)
        def _():
            gs_scr[...] = jnp.zeros_like(gs_scr)
            gg_ref[...] = jnp.zeros_like(gg_ref)
            gb_ref[...] = jnp.zeros_like(gb_ref)

        cosv, sinv = cos_ref[...], sin_ref[...]
        for h in range(RET_HEADS):
            cq, ck, cv, cg, co = _ret_cols(h)
            q = _rot(x_ref[:, cq], cosv, sinv)
            k = _rot(x_ref[:, ck], cosv, sinv) * (RET_DK ** -0.5)
            v = x_ref[:, cv]
            qdv, kdv, dm = qd_ref[h], kd_ref[h], in_ref[h]
            st = st_ref[h]
            o = ro_ref[:, co]
            gv = g_ref[:, co]
            mu = jnp.mean(o, axis=-1, keepdims=True)
            oc = o - mu
            rstd = lax.rsqrt(jnp.mean(oc * oc, axis=-1, keepdims=True) + GN_EPS)
            ohat = oc * rstd
            rn = ohat * gv + b_ref[:, co]
            rg = x_ref[:, cg]
            sg = jax.nn.sigmoid(rg)
            dgt = dg_ref[:, co]
            drn = dgt * (rg * sg)
            dx_ref[:, cg] = (dgt * rn * (sg * (1.0 + rg * (1.0 - sg)))).astype(BF16)
            gg_ref[:, co] += jnp.sum(drn * ohat, axis=0, keepdims=True)
            gb_ref[:, co] += jnp.sum(drn, axis=0, keepdims=True)
            dohat = drn * gv
            do = rstd * (dohat - jnp.mean(dohat, axis=-1, keepdims=True)
                         - ohat * jnp.mean(dohat * ohat, axis=-1, keepdims=True))
            gs = gs_scr[h]
            s = _dot(q, k, NT) * dm
            dsr = _dot(do, v, NT) * dm
            dq = _dot(dsr, k, NN) + _dot(do, st, NT) * qdv
            dk = _dot(dsr, q, TN) + _dot(v, gs, NT) * kdv
            dv = _dot(s, do, TN) + _dot(k * kdv, gs, NN)
            gs_scr[h] = gs * cd_ref[h, :, :1] + _dot(q * qdv, do, TN)
            dx_ref[:, cq] = _rot_t(dq, cosv, sinv).astype(BF16)
            dx_ref[:, ck] = (_rot_t(dk, cosv, sinv) * (RET_DK ** -0.5)).astype(BF16)
            dx_ref[:, cv] = dv.astype(BF16)

    rev = lambda t: last - t
    vblk = pl.BlockSpec((CHUNK, RET_VW), lambda t: (rev(t), 0))
    vspec = pl.BlockSpec((1, RET_VW), lambda t: (0, 0))
    kw = dict(name="ret_bwd", grid=(N_CHUNK,),
              in_specs=_ret_specs(rev) + [vblk, pl.BlockSpec((RET_HEADS, None, RET_DK, RET_DV),
                                                             lambda t: (0, rev(t), 0, 0)), vblk],
              out_specs=(pl.BlockSpec((CHUNK, RET_COLS), lambda t: (rev(t), 0)), vspec, vspec),
              out_shape=(jax.ShapeDtypeStruct((S, RET_COLS), BF16), jax.ShapeDtypeStruct((1, RET_VW), F32),
                         jax.ShapeDtypeStruct((1, RET_VW), F32)),
              scratch_shapes=[pltpu.VMEM((RET_HEADS, RET_DK, RET_DV), F32)])
    args = (proj, cos, sin, inner, qd, kd, cd, gn_g, gn_b, ro, states, dgated)
    if comm is not None:
        return _carry(body, comm, **kw)(*args)
    return _pcall(body, compiler_params=_params(("arbitrary",)), **kw)(*args)


def _bucket_tables():
    qi = np.arange(ATT_BLK)[:, None]
    kj = np.arange(2 * ATT_BLK)[None, :]
    m = ATT_BLK + qi - kj
    out = []
    for win, dil in ATT_GROUPS:
        w = win // dil
        dist = (np.clip(m, 0, w) * dil).astype(np.int32)
        max_exact = N_BUCKETS // 2
        d_f = np.maximum(dist, 1).astype(np.float32)
        large = max_exact + (np.log(d_f / np.float32(max_exact)) / np.float32(math.log(MAX_DIST / max_exact))
                             * np.float32(N_BUCKETS - max_exact)).astype(np.int32)
        large = np.minimum(large, N_BUCKETS - 1)
        out.append(np.where(dist < max_exact, dist, large).astype(np.int32))
    return np.stack(out)


def _bias_build(rel_bias, buckets):
    def body(tab_ref, bk_ref, o_ref):
        hh = pl.program_id(0)
        bk = bk_ref[...]
        acc = jnp.zeros((ATT_BLK, 2 * ATT_BLK), F32)
        for b in range(N_BUCKETS):
            acc = jnp.where(bk == b, tab_ref[b, hh], acc)
        o_ref[...] = acc

    nh = len(ATT_GROUPS) * ATT_HG
    return _pcall(body, name="bias_build", grid=(nh,),
                  in_specs=[pl.BlockSpec(memory_space=pltpu.SMEM),
                            pl.BlockSpec((None, ATT_BLK, 2 * ATT_BLK), lambda hh: (hh // ATT_HG, 0, 0))],
                  out_specs=pl.BlockSpec((None, ATT_BLK, 2 * ATT_BLK), lambda hh: (hh, 0, 0)),
                  out_shape=jax.ShapeDtypeStruct((nh, ATT_BLK, 2 * ATT_BLK), F32),
                  compiler_params=_params(("parallel",)))(rel_bias, buckets)


def _bias_grad(ds_sum, buckets):
    def body(ds_ref, bk_ref, o_ref):
        bk = bk_ref[...]
        ds = ds_ref[...]
        rows = lax.broadcasted_iota(jnp.int32, (N_BUCKETS, 128), 0)
        acc = jnp.zeros((N_BUCKETS, 128), F32)
        for b in range(N_BUCKETS):
            acc = jnp.where(rows == b, jnp.sum(jnp.where(bk == b, ds, 0.0)), acc)
        o_ref[...] = acc

    nh = len(ATT_GROUPS) * ATT_HG
    return _pcall(body, name="bias_grad", grid=(nh,),
                  in_specs=[pl.BlockSpec((None, ATT_BLK, 2 * ATT_BLK), lambda hh: (hh, 0, 0)),
                            pl.BlockSpec((None, ATT_BLK, 2 * ATT_BLK), lambda hh: (hh // ATT_HG, 0, 0))],
                  out_specs=pl.BlockSpec((None, N_BUCKETS, 128), lambda hh: (hh, 0, 0)),
                  out_shape=jax.ShapeDtypeStruct((nh, N_BUCKETS, 128), F32),
                  compiler_params=_params(("parallel",)))(ds_sum, buckets)


def _att_valid(n):
    qi = lax.broadcasted_iota(jnp.int32, (ATT_BLK, 2 * ATT_BLK), 0)
    kj = lax.broadcasted_iota(jnp.int32, (ATT_BLK, 2 * ATT_BLK), 1)
    m = ATT_BLK + qi - kj
    first_key = jnp.where(n > 0, 0, ATT_BLK)
    return (m >= 0) & (m <= ATT_BLK) & (kj >= first_key)


ATT_HP = (1, 2, 2)


def _att_geometry(gi):
    _, dil = ATT_GROUPS[gi]
    return dil, S // dil // ATT_BLK, ATT_HP[gi]


def _blk(dil, r, n):
    if dil == 1:
        return pl.ds(n * ATT_BLK, ATT_BLK)
    return pl.ds(r + n * ATT_BLK * dil, ATT_BLK, stride=dil)


def _slab_specs(gi):
    _, _, hp = _att_geometry(gi)
    per = ATT_HG // hp
    return [pl.BlockSpec((hp, S, ATT_DH), lambda g, r, part=part: ((3 * gi + part) * per + g, 0, 0))
            for part in range(3)]


def _head_specs(gi, count):
    _, _, hp = _att_geometry(gi)
    return [pl.BlockSpec((hp, S, ATT_DH), lambda g, r: (g, 0, 0))] * count


def _bias_spec(gi):
    _, _, hp = _att_geometry(gi)
    return pl.BlockSpec((hp, ATT_BLK, 2 * ATT_BLK), lambda g, r: (gi * (ATT_HG // hp) + g, 0, 0))


def _att_fwd(slabs, bias, gi, comm=None):
    dil, nb, hp = _att_geometry(gi)
    scale = ATT_DH ** -0.5

    def body(q_ref, k_ref, v_ref, bias_ref, o_ref, l_ref):
        r = pl.program_id(1)
        for n in range(nb):
            valid = _att_valid(n)
            prev = _blk(dil, r, max(n - 1, 0))
            cur = _blk(dil, r, n)
            for h in range(hp):
                kk = jnp.concatenate([k_ref[h, prev, :], k_ref[h, cur, :]], axis=0)
                vv = jnp.concatenate([v_ref[h, prev, :], v_ref[h, cur, :]], axis=0)
                s = _dot(q_ref[h, cur, :], kk, NT) * scale + bias_ref[h]
                s = jnp.where(valid, s, -1e30)
                mx = jnp.max(s, axis=-1, keepdims=True)
                e = jnp.exp(s - mx)
                den = jnp.sum(e, axis=-1, keepdims=True)
                o_ref[h, cur, :] = _dot(e / den, vv, NN)
                l_ref[h, cur, :] = jnp.broadcast_to(mx + jnp.log(den), (ATT_BLK, ATT_DH))

    osh = jax.ShapeDtypeStruct((ATT_HG, S, ATT_DH), F32)
    kw = dict(name=f"att_fwd{gi}", grid=(ATT_HG // hp, dil), in_specs=_slab_specs(gi) + [_bias_spec(gi)],
              out_specs=tuple(_head_specs(gi, 2)), out_shape=(osh, osh))
    if comm is not None:
        return _carry(body, comm, **kw)(slabs, slabs, slabs, bias)
    return _pcall(body, compiler_params=_params(("parallel", "arbitrary")), **kw)(slabs, slabs, slabs, bias)


def _att_bwd(slabs, bias, o, lse, do, dlse, gi, comm=None):
    dil, nb, hp = _att_geometry(gi)
    per = ATT_HG // hp
    scale = ATT_DH ** -0.5
    wh = hp * ATT_DH
    wide = lambda t: jnp.concatenate([t, t], axis=1)

    def body(q_ref, k_ref, v_ref, bias_ref, o_ref, l_ref, do_ref, dl_ref, dq_ref, dk_ref, dv_ref, ds_ref):
        r = pl.program_id(1)

        @pl.when(r == 0)
        def _():
            ds_ref[...] = jnp.zeros_like(ds_ref)

        for h in range(hp):
            sl = slice(h * ATT_DH, (h + 1) * ATT_DH)
            carry_k = carry_v = None
            for n in range(nb):
                valid = _att_valid(n)
                prev = _blk(dil, r, max(n - 1, 0))
                cur = _blk(dil, r, n)
                q = q_ref[h, cur, :]
                kk = jnp.concatenate([k_ref[h, prev, :], k_ref[h, cur, :]], axis=0)
                vv = jnp.concatenate([v_ref[h, prev, :], v_ref[h, cur, :]], axis=0)
                dov = do_ref[h, cur, :]
                s = _dot(q, kk, NT) * scale + bias_ref[h]
                p = jnp.where(valid, jnp.exp(s - wide(l_ref[h, cur, :])), 0.0)
                dp = _dot(dov, vv, NT)
                delta = jnp.sum(dov * o_ref[h, cur, :], axis=-1, keepdims=True)
                ds = p * (dp - delta + wide(dl_ref[h, cur, :]))
                ds_ref[h] += ds
                out_rows = pl.ds(n * ATT_BLK, ATT_BLK)
                dq_ref[out_rows, sl] = (_dot(ds, kk, NN) * scale).astype(BF16)
                dkk = _dot(ds, q, TN) * scale
                dvv = _dot(p, dov, TN)
                if n > 0:
                    before = pl.ds((n - 1) * ATT_BLK, ATT_BLK)
                    dk_ref[before, sl] = (carry_k + dkk[:ATT_BLK]).astype(BF16)
                    dv_ref[before, sl] = (carry_v + dvv[:ATT_BLK]).astype(BF16)
                carry_k, carry_v = dkk[ATT_BLK:], dvv[ATT_BLK:]
            last = pl.ds((nb - 1) * ATT_BLK, ATT_BLK)
            dk_ref[last, sl] = carry_k.astype(BF16)
            dv_ref[last, sl] = carry_v.astype(BF16)

    out_spec = pl.BlockSpec((S // dil, wh), lambda g, r: (0, r * per + g))
    osh = jax.ShapeDtypeStruct((S // dil, dil * AW), BF16)
    kw = dict(name=f"att_bwd{gi}", grid=(per, dil), in_specs=_slab_specs(gi) + [_bias_spec(gi)] + _head_specs(gi, 4),
              out_specs=(out_spec, out_spec, out_spec,
                         pl.BlockSpec((hp, ATT_BLK, 2 * ATT_BLK), lambda g, r: (g, 0, 0))),
              out_shape=(osh, osh, osh, jax.ShapeDtypeStruct((ATT_HG, ATT_BLK, 2 * ATT_BLK), F32)))
    args = (slabs, slabs, slabs, bias, o, lse, do, dlse)
    if comm is not None:
        return _carry(body, comm, **kw)(*args)
    return _pcall(body, compiler_params=_params(("arbitrary", "arbitrary")), **kw)(*args)


AW = ATT_HG * ATT_DH


def _mix_weights(l0, l1, l2):
    mx = jnp.maximum(jnp.maximum(l0, l1), l2)
    e0, e1, e2 = jnp.exp(l0 - mx), jnp.exp(l1 - mx), jnp.exp(l2 - mx)
    den = e0 + e1 + e2
    return e0 / den, e1 / den, e2 / den


def _heads_spec():
    return pl.BlockSpec((ATT_HG, TR, ATT_DH), lambda i: (0, i, 0))


def _mix_fwd(os_, ls, comm=None):
    def body(o0, o1, o2, l0, l1, l2, att_ref):
        for h in range(ATT_HG):
            w0, w1, w2 = _mix_weights(l0[h], l1[h], l2[h])
            att_ref[:, h * ATT_DH:(h + 1) * ATT_DH] = (w0 * o0[h] + w1 * o1[h] + w2 * o2[h]).astype(BF16)

    kw = dict(name="mix_fwd", grid=(S // TR,), in_specs=[_heads_spec()] * 6, out_specs=_row_spec(AW),
              out_shape=jax.ShapeDtypeStruct((S, AW), BF16))
    if comm is not None:
        return _carry(body, comm, **kw)(*os_, *ls)
    return _pcall(body, compiler_params=_params(("parallel",)), **kw)(*os_, *ls)


def _mix_bwd(os_, ls, datt):
    def body(o0, o1, o2, l0, l1, l2, da_ref, d0, d1, d2, e0, e1, e2):
        for h in range(ATT_HG):
            ws = _mix_weights(l0[h], l1[h], l2[h])
            da = da_ref[:, h * ATT_DH:(h + 1) * ATT_DH]
            dws = []
            for o_ref, w, d_ref in zip((o0, o1, o2), ws, (d0, d1, d2)):
                d_ref[h] = w * da
                dws.append(jnp.broadcast_to(jnp.sum(da * o_ref[h], axis=-1, keepdims=True), (TR, ATT_DH)))
            tot = ws[0] * dws[0] + ws[1] * dws[1] + ws[2] * dws[2]
            for w, dw, e_ref in zip(ws, dws, (e0, e1, e2)):
                e_ref[h] = w * (dw - tot)

    o = jax.ShapeDtypeStruct((ATT_HG, S, ATT_DH), F32)
    return _pcall(body, name="mix_bwd", grid=(S // TR,), in_specs=[_heads_spec()] * 6 + [_row_spec(AW)],
                  out_specs=(_heads_spec(),) * 6, out_shape=(o,) * 6,
                  compiler_params=_params(("parallel",)))(*os_, *ls, datt)


def _ada_fwd(c_all, w_sh, b_sl):
    def body(c_ref, w_ref, b_ref, o_ref):
        cv = c_ref[...]
        o_ref[...] = _dot(cv * jax.nn.sigmoid(cv), w_ref[...], NN) + b_ref[...]

    return _pcall(body, name="ada_fwd", out_shape=jax.ShapeDtypeStruct((N_DEV, w_sh.shape[1]), F32),
                  compiler_params=_params())(c_all, w_sh, b_sl)


def _ada_bwd(c_all, dm_sl):
    def body(c_ref, d_ref, o_ref):
        cv = c_ref[...]
        o_ref[...] = _dot(cv * jax.nn.sigmoid(cv), d_ref[...], TN)

    return _pcall(body, name="ada_bwd", out_shape=jax.ShapeDtypeStruct((D, dm_sl.shape[1]), F32),
                  compiler_params=_params())(c_all, dm_sl)


N_MOD = 6


def _sum_small(gathered):
    n = len(gathered)

    def body(*refs):
        ins, (gb_ref, dm_ref), outs = refs[:n], refs[n:n + 2], refs[n + 2:]

        def total(r):
            acc = r[0]
            for e in range(1, N_DEV):
                acc = acc + r[e]
            return acc

        for i in range(N_MOD):
            cols = slice(i * D, (i + 1) * D)
            gb_ref[:, cols] = total(ins[i])
            for e in range(N_DEV):
                dm_ref[e:e + 1, cols] = ins[i][e]
        for r, o_ref in zip(ins[N_MOD:], outs):
            o_ref[...] = total(r)

    shapes = (jax.ShapeDtypeStruct((1, N_MOD * D), F32), jax.ShapeDtypeStruct((N_DEV, N_MOD * D), F32),
              *[jax.ShapeDtypeStruct(g.shape[1:], F32) for g in gathered[N_MOD:]])
    res = _pcall(body, name="sum_small", out_shape=shapes, compiler_params=_params())(*gathered)
    return res[0], res[1], res[2:]


def _row_tile(m, n):
    t = max(8, min(m, (1 << 19) // n // 8 * 8))
    while m % t:
        t -= 8
    return t


def _pair_sum(full, recv, sel, name):
    _, _, m, n = full.shape
    t = _row_tile(m, n)

    def body(sel_ref, a_ref, b_ref, o_ref):
        o_ref[...] = (a_ref[...].astype(F32) + b_ref[...].astype(F32)).astype(o_ref.dtype)

    gs = pltpu.PrefetchScalarGridSpec(
        num_scalar_prefetch=1, grid=(4, m // t),
        in_specs=[pl.BlockSpec((None, None, t, n), lambda q, i, s: (q, s[0], i, 0)),
                  pl.BlockSpec((None, t, n), lambda q, i, s: (q, i, 0))],
        out_specs=pl.BlockSpec((None, t, n), lambda q, i, s: (q, i, 0)))
    return _pcall(body, name=name, grid_spec=gs, out_shape=jax.ShapeDtypeStruct((4, m, n), full.dtype),
                  compiler_params=_params(("parallel", "parallel")))(sel, full, recv)


def _chip_sum(part, recv, sel, name):
    _, m, n = part.shape
    t = _row_tile(m, n)

    def body(sel_ref, a_ref, r_ref, o_ref):
        o_ref[...] = ((a_ref[...].astype(F32) + r_ref[0].astype(F32)) + r_ref[1].astype(F32)) + r_ref[2].astype(F32)

    gs = pltpu.PrefetchScalarGridSpec(
        num_scalar_prefetch=1, grid=(m // t,),
        in_specs=[pl.BlockSpec((None, t, n), lambda i, s: (s[0], i, 0)),
                  pl.BlockSpec((3, t, n), lambda i, s: (0, i, 0))],
        out_specs=pl.BlockSpec((t, n), lambda i, s: (i, 0)))
    return _pcall(body, name=name, grid_spec=gs, out_shape=jax.ShapeDtypeStruct((m, n), F32),
                  compiler_params=_params(("parallel",)))(sel, part, recv)


def _adamw_math(w, g, m, v):
    nm = ADAM_B1 * m + (1.0 - ADAM_B1) * g
    nv = ADAM_B2 * v + (1.0 - ADAM_B2) * (g * g)
    m_hat = nm / (1.0 - ADAM_B1 ** ADAM_STEP)
    v_hat = nv / (1.0 - ADAM_B2 ** ADAM_STEP)
    return -ADAM_LR * (m_hat / (jnp.sqrt(v_hat) + ADAM_EPS) + ADAM_WD * w), nm, nv


def _adamw(w, g, m, v, name):
    _, rows, cols = w.shape
    t = _row_tile(rows, cols)

    def body(w_ref, g_ref, m_ref, v_ref, d_ref, nm_ref, nv_ref):
        d_ref[...], nm_ref[...], nv_ref[...] = _adamw_math(w_ref[...], g_ref[...], m_ref[...], v_ref[...])

    spec3 = pl.BlockSpec((None, t, cols), lambda i: (0, i, 0))
    spec2 = pl.BlockSpec((t, cols), lambda i: (i, 0))
    o = jax.ShapeDtypeStruct(w.shape, F32)
    return _pcall(body, name=name, grid=(rows // t,), in_specs=[spec3, spec2, spec3, spec3], out_specs=(spec3,) * 3,
                  out_shape=(o, o, o), compiler_params=_params(("parallel",)))(w, g, m, v)


def _adamw_reduced(w, m, v, parts, recvs, sel):
    _, rows, cols = w.shape
    half = cols // 2
    t = _row_tile(rows, half)

    def body(sel_ref, w_ref, m_ref, v_ref, pa_ref, pb_ref, ra_ref, rb_ref, g_ref, d_ref, nm_ref, nv_ref):
        total = lambda p_ref, r_ref: ((p_ref[...].astype(F32) + r_ref[0].astype(F32)) + r_ref[1].astype(F32)) \
            + r_ref[2].astype(F32)
        g = jnp.where(pl.program_id(1) == 0, total(pa_ref, ra_ref), total(pb_ref, rb_ref))
        g_ref[...] = g
        d_ref[...], nm_ref[...], nv_ref[...] = _adamw_math(w_ref[...], g, m_ref[...], v_ref[...])

    wspec = pl.BlockSpec((None, t, half), lambda i, j, s: (0, i, j))
    pspec = pl.BlockSpec((None, t, half), lambda i, j, s: (s[0], i, 0))
    rspec = pl.BlockSpec((3, t, half), lambda i, j, s: (0, i, 0))
    gs = pltpu.PrefetchScalarGridSpec(num_scalar_prefetch=1, grid=(rows // t, 2),
                                      in_specs=[wspec, wspec, wspec, pspec, pspec, rspec, rspec],
                                      out_specs=(wspec,) * 4)
    o = jax.ShapeDtypeStruct(w.shape, F32)
    return _pcall(body, name="adamw_w_in", grid_spec=gs, out_shape=(o, o, o, o),
                  compiler_params=_params(("parallel", "arbitrary")))(sel, w, m, v, *parts, *recvs)


def _adamw_small(ws, gs, ms, vs):
    n = len(ws)

    def body(*refs):
        for i in range(n):
            w_ref, g_ref, m_ref, v_ref = (refs[k * n + i] for k in range(4))
            d, nm, nv = _adamw_math(w_ref[...], g_ref[...], m_ref[...], v_ref[...])
            refs[4 * n + i][...] = d
            refs[5 * n + i][...] = nm
            refs[6 * n + i][...] = nv

    shapes = tuple(jax.ShapeDtypeStruct(w.shape, F32) for w in ws)
    res = _pcall(body, name="adamw_small", out_shape=shapes * 3, compiler_params=_params())(*ws, *gs, *ms, *vs)
    return res[:n], res[n:2 * n], res[2 * n:]


def _mesh_pos():
    return lax.axis_index("x"), lax.axis_index("y"), lax.axis_index("c")


class _Gather:
    def __init__(self, arrs, relay=False):
        self.relay = relay
        self.ins = list(arrs)
        na = self.na = len(arrs)
        self.out_shape = tuple(jax.ShapeDtypeStruct((N_DEV,) + a.shape, a.dtype) for a in arrs)
        self.sems = [pltpu.SemaphoreType.DMA((7 * na,)), pltpu.SemaphoreType.DMA((7 * na,)),
                     pltpu.SemaphoreType.DMA((na,))]

    def _copies(self, ins, outs, sems):
        send_sems, recv_sems, local_sems = sems
        x, y, c = _mesh_pos()
        me, sibling = (x, y, c), (x, y, 1 - c)
        chips = [(1 - x, y), (x, 1 - y), (1 - x, 1 - y)]

        def slot(p):
            return 4 * p[0] + 2 * p[1] + p[2]

        def copy(a, k, block, to, src=None):
            dst = outs[a].at[slot(block)]
            return pltpu.make_async_remote_copy(
                src_ref=dst if src is None else src, dst_ref=dst, send_sem=send_sems.at[7 * a + k],
                recv_sem=recv_sems.at[7 * a + k], device_id=to, device_id_type=MESH)

        mine = [pltpu.make_async_copy(ins[a], outs[a].at[slot(me)], local_sems.at[a]) for a in range(self.na)]
        direct = chips[:2] if self.relay else chips
        first = []
        for a in range(self.na):
            first.append(copy(a, 0, me, sibling, src=ins[a]))
            first += [copy(a, 1 + j, me, (*chip, c), src=ins[a]) for j, chip in enumerate(direct)]
        return me, sibling, chips, c, copy, mine, first

    def start(self, ins, outs, sems):
        *_, mine, first = self._copies(ins, outs, sems)
        for cp in mine + first:
            cp.start()

    def finish(self, ins, outs, sems):
        me, sibling, chips, c, copy, mine, first = self._copies(ins, outs, sems)
        x, y = me[0], me[1]
        passed = []
        for j, chip in enumerate(chips):
            for a in range(self.na):
                if self.relay and j == 2:
                    owner = ((x + 1 - c) % 2, (y + c) % 2, c)
                    cp = copy(a, 3, owner, ((x + c) % 2, (y + 1 - c) % 2, c))
                    cp.start()
                    passed.append(cp)
                copy(a, 1 + j, (*chip, c), me).wait_recv()
                cp = copy(a, 4 + j, (*chip, c), sibling)
                cp.start()
                passed.append(cp)
        for a in range(self.na):
            copy(a, 0, sibling, me).wait_recv()
            for j, chip in enumerate(chips):
                copy(a, 4 + j, (*chip, 1 - c), me).wait_recv()
        for cp in first + passed:
            cp.wait_send()
        for cp in mine:
            cp.wait()


class _ExchangeCore:
    def __init__(self, fulls):
        self.ins = list(fulls)
        self.out_shape = tuple(jax.ShapeDtypeStruct((4,) + f.shape[2:], f.dtype) for f in fulls)
        self.sems = [pltpu.SemaphoreType.DMA((4 * len(fulls),)), pltpu.SemaphoreType.DMA((4 * len(fulls),))]

    def _copies(self, ins, outs, sems):
        send_sems, recv_sems = sems
        x, y, c = _mesh_pos()
        return [pltpu.make_async_remote_copy(
            src_ref=ins[a].at[q, 1 - c], dst_ref=outs[a].at[q], send_sem=send_sems.at[4 * a + q],
            recv_sem=recv_sems.at[4 * a + q], device_id=(x, y, 1 - c), device_id_type=MESH)
            for a in range(len(self.ins)) for q in range(4)]

    def start(self, ins, outs, sems):
        for cp in self._copies(ins, outs, sems):
            cp.start()

    def finish(self, ins, outs, sems):
        for cp in self._copies(ins, outs, sems):
            cp.wait()


class _ExchangeChip:
    def __init__(self, parts):
        self.ins = list(parts)
        self.out_shape = tuple(jax.ShapeDtypeStruct((3,) + p.shape[1:], p.dtype) for p in parts)
        self.sems = [pltpu.SemaphoreType.DMA((3 * len(parts),)), pltpu.SemaphoreType.DMA((3 * len(parts),))]

    def _copies(self, ins, outs, sems):
        send_sems, recv_sems = sems
        x, y, c = _mesh_pos()
        chips = [(1 - x, y), (x, 1 - y), (1 - x, 1 - y)]
        return [pltpu.make_async_remote_copy(
            src_ref=ins[a].at[2 * px + py], dst_ref=outs[a].at[j], send_sem=send_sems.at[3 * a + j],
            recv_sem=recv_sems.at[3 * a + j], device_id=(px, py, c), device_id_type=MESH)
            for a in range(len(self.ins)) for j, (px, py) in enumerate(chips)]

    def start(self, ins, outs, sems):
        for cp in self._copies(ins, outs, sems):
            cp.start()

    def finish(self, ins, outs, sems):
        for cp in self._copies(ins, outs, sems):
            cp.wait()


HBM_ONLY = pl.BlockSpec(memory_space=pltpu.HBM)
SEM_SPEC = pl.BlockSpec(memory_space=pltpu.SEMAPHORE)
SIDE_EFFECT = pltpu.SideEffectType.DATAFLOW_SIDE_EFFECTING


def _chip_copies(p_refs, land_refs, send_sems, recv_sems):
    x, y, c = _mesh_pos()
    return [pltpu.make_async_remote_copy(
        src_ref=p_refs[a].at[2 * px + py], dst_ref=land_refs[a].at[j], send_sem=send_sems.at[3 * a + j],
        recv_sem=recv_sems.at[3 * a + j], device_id=(px, py, c), device_id_type=MESH)
        for a in range(len(p_refs)) for j, (px, py) in enumerate([(1 - x, y), (x, 1 - y), (1 - x, 1 - y)])]


def _chip_exchange_start(parts, name):
    n = len(parts)
    lands = [lax.empty((3,) + p.shape[1:], p.dtype) for p in parts]

    def body(*refs):
        p_refs, land_refs, (send_sems, recv_sems) = refs[:n], refs[n:2 * n], refs[2 * n:2 * n + 2]
        for cp in _chip_copies(p_refs, land_refs, send_sems, recv_sems):
            cp.start()
        token = refs[-1]
        token[...] = jnp.zeros_like(token)

    hbm = lambda t: pltpu.HBM(t.shape, t.dtype)
    res = pl.pallas_call(
        body, name=name,
        out_shape=(pltpu.SemaphoreType.DMA((3 * n,)), pltpu.SemaphoreType.DMA((3 * n,)), *[hbm(t) for t in parts + lands],
                   jax.ShapeDtypeStruct((8, 128), F32)),
        in_specs=(HBM_ONLY,) * (2 * n),
        out_specs=(SEM_SPEC, SEM_SPEC, *[HBM_ONLY] * (2 * n), pl.BlockSpec(memory_space=pltpu.VMEM)),
        input_output_aliases={i: 2 + i for i in range(2 * n)},
        compiler_params=pltpu.CompilerParams(has_side_effects=SIDE_EFFECT))(
        *[pltpu.with_memory_space_constraint(t, pltpu.HBM) for t in parts + lands])
    return (res[0], res[1], list(res[2:2 + n]), list(res[2 + n:2 + 2 * n])), res[-1]


def _chip_exchange_wait(in_flight, after, name):
    send_sems, recv_sems, parts, lands = in_flight
    n = len(parts)

    def body(*refs):
        p_refs, land_refs, (send_sems, recv_sems) = refs[:n], refs[n:2 * n], refs[2 * n:2 * n + 2]
        for cp in _chip_copies(p_refs, land_refs, send_sems, recv_sems):
            cp.wait_send()
            cp.wait_recv()

    res = pl.pallas_call(
        body, name=name, out_shape=tuple(pltpu.HBM(t.shape, t.dtype) for t in parts + lands),
        in_specs=(*[HBM_ONLY] * (2 * n), SEM_SPEC, SEM_SPEC, pl.BlockSpec(memory_space=pl.ANY)),
        out_specs=(HBM_ONLY,) * (2 * n), input_output_aliases={i: i for i in range(2 * n)},
        compiler_params=pltpu.CompilerParams(has_side_effects=SIDE_EFFECT))(*parts, *lands, send_sems, recv_sems, after)
    return list(res[:n]), list(res[n:])


def _slot(p):
    return 4 * p[0] + 2 * p[1] + p[2]


def _gather_copies(src_refs, out_refs, send_sems, recv_sems):
    x, y, c = _mesh_pos()
    targets = [(x, y, 1 - c), (1 - x, y, c), (x, 1 - y, c), (1 - x, 1 - y, c)]
    return [pltpu.make_async_remote_copy(
        src_ref=src_refs[a], dst_ref=out_refs[a].at[_slot((x, y, c))], send_sem=send_sems.at[4 * a + k],
        recv_sem=recv_sems.at[4 * a + k], device_id=to, device_id_type=MESH)
        for a in range(len(src_refs)) for k, to in enumerate(targets)]


def _gather_start(shards, after, name):
    n = len(shards)
    outs = [lax.empty((N_DEV,) + s.shape, s.dtype) for s in shards]

    def body(*refs):
        for cp in _gather_copies(refs[:n], refs[n:2 * n], refs[2 * n + 1], refs[2 * n + 2]):
            cp.start()
        token = refs[-1]
        token[...] = jnp.zeros_like(token)

    res = pl.pallas_call(
        body, name=name,
        out_shape=(pltpu.SemaphoreType.DMA((4 * n,)), pltpu.SemaphoreType.DMA((4 * n,)),
                   *[pltpu.HBM(t.shape, t.dtype) for t in shards + outs], jax.ShapeDtypeStruct((8, 128), F32)),
        in_specs=(*[HBM_ONLY] * (2 * n), pl.BlockSpec(memory_space=pl.ANY)),
        out_specs=(SEM_SPEC, SEM_SPEC, *[HBM_ONLY] * (2 * n), pl.BlockSpec(memory_space=pltpu.VMEM)),
        input_output_aliases={i: 2 + i for i in range(2 * n)},
        compiler_params=pltpu.CompilerParams(has_side_effects=SIDE_EFFECT))(
        *[pltpu.with_memory_space_constraint(t, pltpu.HBM) for t in shards + outs], after)
    return (res[0], res[1], list(res[2:2 + n]), list(res[2 + n:2 + 2 * n])), res[-1]


def _gather_wait(in_flight, after, name):
    send_sems, recv_sems, shards, outs = in_flight
    n = len(shards)

    def body(*refs):
        for cp in _gather_copies(refs[:n], refs[n:2 * n], refs[2 * n], refs[2 * n + 1]):
            cp.wait_send()
            cp.wait_recv()

    res = pl.pallas_call(
        body, name=name, out_shape=tuple(pltpu.HBM(t.shape, t.dtype) for t in shards + outs),
        in_specs=(*[HBM_ONLY] * (2 * n), SEM_SPEC, SEM_SPEC, pl.BlockSpec(memory_space=pl.ANY)),
        out_specs=(HBM_ONLY,) * (2 * n), input_output_aliases={i: i for i in range(2 * n)},
        compiler_params=pltpu.CompilerParams(has_side_effects=SIDE_EFFECT))(*shards, *outs, send_sems, recv_sems, after)
    return list(res[:n]), list(res[n:])


class _PassToSibling:
    def __init__(self, shards, gathered):
        n = self.n = len(shards)
        self.ins = list(shards) + list(gathered)
        self.out_shape = tuple(jax.ShapeDtypeStruct(g.shape, g.dtype) for g in gathered)
        self.aliases = {n + a: a for a in range(n)}
        self.sems = [pltpu.SemaphoreType.DMA((3 * n,)), pltpu.SemaphoreType.DMA((3 * n,)),
                     pltpu.SemaphoreType.DMA((n,))]

    def _copies(self, ins, outs, sems):
        send_sems, recv_sems, local_sems = sems
        x, y, c = _mesh_pos()
        chips = [(1 - x, y), (x, 1 - y), (1 - x, 1 - y)]
        mine = [pltpu.make_async_copy(ins[a], outs[a].at[_slot((x, y, c))], local_sems.at[a]) for a in range(self.n)]
        passed, awaited = [], []
        for a in range(self.n):
            for j, chip in enumerate(chips):
                sems_j = dict(send_sem=send_sems.at[3 * a + j], recv_sem=recv_sems.at[3 * a + j],
                              device_id=(x, y, 1 - c), device_id_type=MESH)
                blk = outs[a].at[_slot((*chip, c))]
                passed.append(pltpu.make_async_remote_copy(src_ref=blk, dst_ref=blk, **sems_j))
                got = outs[a].at[_slot((*chip, 1 - c))]
                awaited.append(pltpu.make_async_remote_copy(src_ref=got, dst_ref=got, **sems_j))
        return mine, passed, awaited

    def start(self, ins, outs, sems):
        mine, passed, _ = self._copies(ins, outs, sems)
        for cp in mine + passed:
            cp.start()

    def finish(self, ins, outs, sems):
        mine, passed, awaited = self._copies(ins, outs, sems)
        for cp in passed:
            cp.wait_send()
        for cp in awaited:
            cp.wait_recv()
        for cp in mine:
            cp.wait()


def _reduce_sums(fulls, recv_core, core, tag):
    return [_pair_sum(f, r, core, f"rs_pair_{tag}{i}") for i, (f, r) in enumerate(zip(fulls, recv_core))]


def _local_step(x, tgt, mods, w_in_t, shards, small, chip, core):
    sh1, sc1, g1, sh2, sc2, g2 = mods
    norm1_g, rel_bias, gn_g, gn_b, norm2_g, norm_f_g = small
    tables = _ret_tables()
    buckets = jnp.asarray(_bucket_tables())

    h1 = _norm_mod_fwd(x, norm1_g, sh1, sc1, "norm1_fwd")
    flight_w, token_w = _gather_start(list(shards), h1, "gather_w_start")
    proj, slabs = _proj(h1, w_in_t, token_w)
    gated, ro, states = _ret_fwd(proj, tables, gn_g, gn_b)
    bias = _bias_build(rel_bias, buckets)
    outs, lses = [], []
    for gi in range(len(ATT_GROUPS)):
        o, l = _att_fwd(slabs, bias, gi)
        outs.append(o)
        lses.append(l)
    att, gathered = _mix_fwd(outs, lses, comm=_PassToSibling(*_gather_wait(flight_w, lses[2], "gather_w_wait")))
    w_ret_out, w_att_out, w_o, w_ff1, w_ff2 = (_from_slots(g, ax) for g, ax in zip(gathered, BIG_AXES[1:]))
    ret_out = _mm(gated, w_ret_out, 'nn', tm=S, tn=256, tk=2048, name="ret_out")
    att_out = _mm(att, w_att_out, 'nn', tm=S, tn=512, tk=AW, name="att_out")
    merged = _merge_fwd(proj, ret_out, att_out)
    mixo, x1 = _mm(merged, w_o, 'nn', tm=S, tn=256, tk=D, name="w_o", res=x, gvec=g1)
    h2 = _norm_mod_fwd(x1, norm2_g, sh2, sc2, "norm2_fwd")
    u, act = _mm(h2, w_ff1, 'nn', tm=S, tn=512, tk=D, name="ff1", relu2=True)
    f, x2 = _mm(act, w_ff2, 'nn', tm=1024, tn=512, tk=D_FF, name="ff2", res=x1, gvec=g2)
    loss, dx2, g_normf, df, dg2 = _final_loss(x2, tgt, norm_f_g, f, g2)

    gw_ff2 = _mm(act, df, 'tn', tm=512, tn=D, tk=S, name="gw_ff2", out_dtype=BF16)
    du = _mm(df, w_ff2, 'nt', tm=S, tn=512, tk=D, name="d_act", out_dtype=BF16, relu2_of=u)
    gw_ff1 = _mm(h2, du, 'tn', tm=D, tn=512, tk=S, name="gw_ff1", out_dtype=BF16)
    fulls_a = [_to_slots(g, ax) for g, ax in zip((gw_ff1, gw_ff2), BIG_AXES[4:])]
    dh2, recv_core_a = _mm(du, w_ff1, 'nt', tm=1024, tn=1024, tk=2048, name="dh2", comm=_ExchangeCore(fulls_a))
    parts_a = _reduce_sums(fulls_a, recv_core_a, core, "a")
    flight_a, token_a = _chip_exchange_start(parts_a, "rs_a_start")
    dx1, dsc2, dsh2, g_norm2, dmixo, dg1 = _norm_mod_bwd(x1, norm2_g, sc2, dh2, dx2, "norm2_bwd", gate=(mixo, g1))

    gw_o = _mm(merged, dmixo, 'tn', tm=D, tn=512, tk=S, name="gw_o", out_dtype=BF16, after=token_a)
    dmerged = _mm(dmixo, w_o, 'nt', tm=S, tn=512, tk=D, name="dmerged")
    d_ret_out, d_att_out, dga, dgb = _merge_bwd(proj, ret_out, att_out, dmerged)
    gw_ret_out = _mm(gated, d_ret_out, 'tn', tm=512, tn=D, tk=S, name="gw_ret_out", out_dtype=BF16)
    gw_att_out = _mm(att, d_att_out, 'tn', tm=AW, tn=D, tk=S, name="gw_att_out", out_dtype=BF16)
    fulls_b = [_to_slots(g, ax) for g, ax in zip((gw_ret_out, gw_att_out, gw_o), BIG_AXES[1:4])]
    dgated, recv_core_b = _mm(d_ret_out, w_ret_out, 'nt', tm=S, tn=512, tk=D, name="dgated",
                              comm=_ExchangeCore(fulls_b))
    parts_b = _reduce_sums(fulls_b, recv_core_b, core, "b")
    flight_b, token_b = _chip_exchange_start(parts_b, "rs_b_start")
    datt = _mm(d_att_out, w_att_out, 'nt', tm=S, tn=AW, tk=D, name="datt", after=token_b)
    mix_grads = _mix_bwd(outs, lses, datt)
    datt_parts, ds_sums = [], []
    for gi in range(len(ATT_GROUPS)):
        dq, dk, dv, ds_sum = _att_bwd(slabs, bias, outs[gi], lses[gi], mix_grads[gi], mix_grads[3 + gi], gi)
        datt_parts += [dq.reshape(S, AW), dk.reshape(S, AW), dv.reshape(S, AW)]
        ds_sums.append(ds_sum)
    g_bias = _bias_grad(jnp.concatenate(ds_sums, axis=0), buckets)[:, :, 0].T.reshape(1, -1)
    dret, g_gn_g, g_gn_b = _ret_bwd(proj, tables, gn_g, gn_b, ro, states, dgated)
    parts_a, recv_chip_a = _chip_exchange_wait(flight_a, dret, "rs_a_wait")
    parts_b, recv_chip_b = _chip_exchange_wait(flight_b, dret, "rs_b_wait")
    red_a = [_chip_sum(p, r, chip, f"rs_sum_a{i}") for i, (p, r) in enumerate(zip(parts_a, recv_chip_a))]
    red_b = [_chip_sum(p, r, chip, f"rs_sum_b{i}") for i, (p, r) in enumerate(zip(parts_b, recv_chip_b))]
    dproj = jnp.concatenate([dret] + datt_parts + [dga, dgb], axis=1)
    in_flight, token = [], None
    for half in range(2):
        h1_half = h1[:, half * (D // 2):(half + 1) * (D // 2)]
        gw_half = _mm(dproj, h1_half, 'tn', tm=512, tn=D // 2, tk=S, name=f"gw_in{half}", out_dtype=BF16, after=token)
        full_in = [_to_slots(gw_half, 0)]
        recv_core_in = _run_comm(_ExchangeCore(full_in), f"rs_core_in{half}")
        part_in = _reduce_sums(full_in, recv_core_in, core, f"c{half}")
        flight, token = _chip_exchange_start(part_in, f"rs_in{half}_start")
        in_flight.append(flight)
    dh1 = _mm(dproj, w_in_t, 'nn', tm=1024, tn=1024, tk=2560, name="dh1", after=token)
    gx, dsc1, dsh1, g_norm1 = _norm_mod_bwd(x, norm1_g, sc1, dh1, dx1, "norm1_bwd")

    dmod = [dsh1, dsc1, dg1, dsh2, dsc2, dg2]
    small_g = [g_norm1, g_bias, g_gn_g, g_gn_b, g_norm2, g_normf]
    return loss, gx, in_flight, red_b + red_a, small_g, dmod


def _to_slots(g, axis):
    if axis == 0:
        return g.reshape(4, 2, g.shape[0] // N_DEV, g.shape[1])
    return g.reshape(g.shape[0], N_DEV, g.shape[1] // N_DEV).transpose(1, 0, 2).reshape(4, 2, g.shape[0], -1)


def _from_slots(w8, axis):
    if axis == 0:
        return w8.reshape(-1, w8.shape[2])
    return w8.transpose(1, 0, 2).reshape(w8.shape[1], -1)


BIG_AXES = (1, 0, 1, 0, 1, 0)


def kernel(x, c, w_ada, b_ada, norm1_g, w_in, rel_bias, ret_gn_g, ret_gn_b, w_ret_out, w_att_out, w_o, norm2_g, w_ff1, w_ff2, norm_f_g, loss_target, m_w_ada, m_b_ada, m_norm1_g, m_w_in, m_rel_bias, m_ret_gn_g, m_ret_gn_b, m_w_ret_out, m_w_att_out, m_w_o, m_norm2_g, m_w_ff1, m_w_ff2, m_norm_f_g, v_w_ada, v_b_ada, v_norm1_g, v_w_in, v_rel_bias, v_ret_gn_g, v_ret_gn_b, v_w_ret_out, v_w_att_out, v_w_o, v_norm2_g, v_w_ff1, v_w_ff2, v_norm_f_g):
    mx, my, mc = _mesh_pos()
    dev = 4 * mx + 2 * my + mc
    chip = jnp.reshape(2 * mx + my, (1,)).astype(jnp.int32)
    core = jnp.reshape(mc, (1,)).astype(jnp.int32)
    ada_w = D * 6 // N_DEV

    w_in, m_w_in, v_w_in = (jnp.transpose(t, (0, 2, 1)) for t in (w_in, m_w_in, v_w_in))

    shards = [w[0].astype(BF16) for w in (w_in, w_ret_out, w_att_out, w_o, w_ff1, w_ff2)]
    c_all, w_in8 = _run_comm(_Gather([c, shards[0]], relay=True), "gather_c_w_in")
    c_all = c_all.reshape(N_DEV, D)
    b_sl = lax.dynamic_slice(b_ada, (0, dev * ada_w), (1, ada_w))
    (mod_all,) = _run_comm(_Gather([_ada_fwd(c_all, w_ada[0], b_sl)]), "gather_mod")
    mod = lax.dynamic_index_in_dim(mod_all, dev, axis=1, keepdims=False).reshape(6, D)
    mods = tuple(mod[i:i + 1] for i in range(6))

    small = (norm1_g, rel_bias, ret_gn_g, ret_gn_b, norm2_g, norm_f_g.reshape(1, D))
    loss, gx, in_flight, big_red, small_g, dmod = _local_step(x[0], loss_target[0], mods, w_in8.reshape(IN_COLS, D),
                                                              shards[1:], small, chip, core)

    gathered = _run_comm(_Gather(dmod + small_g + [loss]), "gather_small")
    g_b_ada, dmod_all, (g_norm1, g_bias, g_gn_g, g_gn_b, g_norm2, g_normf, loss_sum) = _sum_small(gathered)
    loss_out = loss_sum[0, 0]
    g_w_ada = _ada_bwd(c_all, lax.dynamic_slice(dmod_all, (0, dev * ada_w), (N_DEV, ada_w)))

    names = ['w_ada', 'b_ada', 'norm1_g', 'w_in', 'rel_bias', 'ret_gn_g', 'ret_gn_b', 'w_ret_out', 'w_att_out',
             'w_o', 'norm2_g', 'w_ff1', 'w_ff2', 'norm_f_g']
    ws = dict(zip(names, (w_ada, b_ada, norm1_g, w_in, rel_bias, ret_gn_g, ret_gn_b, w_ret_out, w_att_out, w_o,
                          norm2_g, w_ff1, w_ff2, norm_f_g)))
    ms = dict(zip(names, (m_w_ada, m_b_ada, m_norm1_g, m_w_in, m_rel_bias, m_ret_gn_g, m_ret_gn_b, m_w_ret_out,
                          m_w_att_out, m_w_o, m_norm2_g, m_w_ff1, m_w_ff2, m_norm_f_g)))
    vs = dict(zip(names, (v_w_ada, v_b_ada, v_norm1_g, v_w_in, v_rel_bias, v_ret_gn_g, v_ret_gn_b, v_w_ret_out,
                          v_w_att_out, v_w_o, v_norm2_g, v_w_ff1, v_w_ff2, v_norm_f_g)))
    grads = dict(w_ada=g_w_ada, w_ret_out=big_red[0], w_att_out=big_red[1], w_o=big_red[2],
                 w_ff1=big_red[3], w_ff2=big_red[4], b_ada=g_b_ada, norm1_g=g_norm1, rel_bias=g_bias,
                 ret_gn_g=g_gn_g, ret_gn_b=g_gn_b, norm2_g=g_norm2, norm_f_g=g_normf)
    delta, new_m, new_v = {}, {}, {}
    for n in ('w_ada', 'w_ret_out', 'w_att_out', 'w_o', 'w_ff1', 'w_ff2'):
        delta[n], new_m[n], new_v[n] = _adamw(ws[n], grads[n], ms[n], vs[n], "adamw_" + n)
        grads[n] = grads[n].reshape(ws[n].shape)
    small_names = ('b_ada', 'norm1_g', 'rel_bias', 'ret_gn_g', 'ret_gn_b', 'norm2_g', 'norm_f_g')
    two_d = {n: (1, ws[n].size) if ws[n].ndim == 1 else ws[n].shape for n in small_names}
    d_, m_, v_ = _adamw_small(*[[src[n].reshape(two_d[n]) for n in small_names] for src in (ws, grads, ms, vs)])
    for i, n in enumerate(small_names):
        shp = ws[n].shape
        delta[n], new_m[n], new_v[n] = d_[i].reshape(shp), m_[i].reshape(shp), v_[i].reshape(shp)
        grads[n] = grads[n].reshape(shp)

    done = lax.optimization_barrier((gx, tuple(d_), tuple(delta[n] for n in ('w_ada', 'w_ret_out', 'w_att_out', 'w_o',
                                                                               'w_ff1', 'w_ff2'))))
    parts_in, recvs_in = [], []
    for half, flight in enumerate(in_flight):
        (part_in,), (recv_chip_in,) = _chip_exchange_wait(flight, done[0], f"rs_in{half}_wait")
        parts_in.append(part_in)
        recvs_in.append(recv_chip_in)
    grads['w_in'], delta['w_in'], new_m['w_in'], new_v['w_in'] = _adamw_reduced(w_in, m_w_in, v_w_in, parts_in,
                                                                               recvs_in, chip)
    for d in (grads, delta, new_m, new_v):
        d['w_in'] = jnp.transpose(d['w_in'], (0, 2, 1))
    return (loss_out, gx[None], *[grads[n] for n in names], *[delta[n] for n in names],
            *[new_m[n] for n in names], *[new_v[n] for n in names])
```

```python
import functools
import math

import numpy as np
import jax
import jax.numpy as jnp
from jax import lax
from jax.experimental import pallas as pl
from jax.experimental.pallas import tpu as pltpu

F32 = jnp.float32
BF16 = jnp.bfloat16
MESH = pl.DeviceIdType.MESH

N_DEV = 8
S = 2048
D = 1024
RET_HEADS = 4
RET_DK = 256
RET_DV = 512
CHUNK = 128
N_CHUNK = S // CHUNK
ATT_GROUPS = ((128, 1), (512, 4), (2048, 16))
ATT_HG = 4
ATT_DH = 128
ATT_BLK = 128
N_BUCKETS = 32
MAX_DIST = 2048
D_FF = 4096
IN_COLS = 12800
OFF_RQ, OFF_RK, OFF_RV, OFF_RG, OFF_ATT = 0, 1024, 2048, 4096, 6144
OFF_GA, OFF_GB = 6144, 7168
RMS_EPS = 1e-6
GN_EPS = 1e-5
ADAM_LR, ADAM_B1, ADAM_B2, ADAM_EPS, ADAM_WD, ADAM_STEP = 0.001, 0.9, 0.999, 1e-08, 0.01, 10
VMEM_LIMIT = 48 * 1024 * 1024


def _pcall(body, **kw):
    return pl.pallas_call(body, **kw)


def _params(sem=None):
    return pltpu.CompilerParams(dimension_semantics=sem, vmem_limit_bytes=VMEM_LIMIT)


HBM_SPEC = pl.BlockSpec(memory_space=pl.ANY)


def _carry(body, comm, *, name, grid, in_specs, out_specs, out_shape, scratch_shapes=()):
    single = not isinstance(out_specs, (tuple, list))
    o_specs = (out_specs,) if single else tuple(out_specs)
    o_shape = (out_shape,) if single else tuple(out_shape)
    n_in, n_out, n_scr = len(in_specs), len(o_specs), len(scratch_shapes)
    nci, nco = len(comm.ins), len(comm.out_shape)
    total = int(np.prod(grid))

    def wrapped(*refs):
        bounds = np.cumsum([0, n_in, nci, n_out, nco, n_scr])
        a, ci, o, co, scr = (refs[bounds[i]:bounds[i + 1]] for i in range(5))
        sems = refs[bounds[5]:]
        flat = 0
        for d, g in enumerate(grid):
            flat = flat * g + pl.program_id(d)

        @pl.when(flat == 0)
        def _():
            comm.start(ci, co, sems)

        body(*a, *o, *scr)

        @pl.when(flat == total - 1)
        def _():
            comm.finish(ci, co, sems)

    aliases = {n_in + i: n_out + o for i, o in getattr(comm, "aliases", {}).items()}
    call = _pcall(wrapped, name=name, grid=grid, in_specs=list(in_specs) + [HBM_SPEC] * nci,
                  out_specs=o_specs + (HBM_SPEC,) * nco, out_shape=o_shape + tuple(comm.out_shape),
                  scratch_shapes=list(scratch_shapes) + list(comm.sems), input_output_aliases=aliases,
                  compiler_params=_params(("arbitrary",) * len(grid)))

    def run(*args):
        res = call(*args, *comm.ins)
        own = res[0] if single else tuple(res[:n_out])
        return own, tuple(res[n_out:])

    return run


def _run_comm(comm, name):
    nci, nco = len(comm.ins), len(comm.out_shape)

    def body(*refs):
        ci, co, sems = refs[:nci], refs[nci:nci + nco], refs[nci + nco:]
        comm.start(ci, co, sems)
        comm.finish(ci, co, sems)

    return _pcall(body, name=name, in_specs=[HBM_SPEC] * nci, out_specs=(HBM_SPEC,) * nco,
                  out_shape=tuple(comm.out_shape), scratch_shapes=list(comm.sems))(*comm.ins)


def _dot(a, b, dn):
    return lax.dot_general(a.astype(BF16), b.astype(BF16), (dn, ((), ())), preferred_element_type=F32)


NN = ((1,), (0,))
NT = ((1,), (1,))
TN = ((0,), (0,))


def _mm(a, b, mode, *, tm, tn, tk, name, out_dtype=F32, res=None, gvec=None, relu2=False, relu2_of=None, comm=None,
        after=None):
    if mode == 'nn':
        (M, K), (_, N) = a.shape, b.shape
        a_spec = pl.BlockSpec((tm, tk), lambda i, j, k: (i, k))
        b_spec = pl.BlockSpec((tk, tn), lambda i, j, k: (k, j))
        dn = NN
    elif mode == 'nt':
        (M, K), (N, _) = a.shape, b.shape
        a_spec = pl.BlockSpec((tm, tk), lambda i, j, k: (i, k))
        b_spec = pl.BlockSpec((tn, tk), lambda i, j, k: (j, k))
        dn = NT
    else:
        (K, M), (_, N) = a.shape, b.shape
        a_spec = pl.BlockSpec((tk, tm), lambda i, j, k: (k, i))
        b_spec = pl.BlockSpec((tk, tn), lambda i, j, k: (k, j))
        dn = TN
    assert M % tm == 0 and N % tn == 0 and K % tk == 0, (name, M, N, K)
    nk = K // tk
    fused = res is not None
    o_spec = pl.BlockSpec((tm, tn), lambda i, j, k: (i, j))

    def body(a_ref, b_ref, *rest):
        acc_ref = rest[-1] if nk > 1 else None
        if after is not None:
            rest = rest[1:]
        if fused:
            res_ref, g_ref, o_ref, x_ref = rest[:4]
        elif relu2_of is not None:
            u_ref, o_ref = rest[:2]
        elif relu2:
            o_ref, act_ref = rest[:2]
        else:
            o_ref = rest[0]

        def finish(acc):
            if relu2_of is not None:
                acc = acc * (2.0 * jnp.maximum(u_ref[...], 0.0))
            o_ref[...] = acc.astype(o_ref.dtype)
            if fused:
                x_ref[...] = res_ref[...] + g_ref[...] * acc
            if relu2:
                r = jnp.maximum(acc, 0.0)
                act_ref[...] = (r * r).astype(BF16)

        p = _dot(a_ref[...], b_ref[...], dn)
        if nk == 1:
            finish(p)
        else:
            k = pl.program_id(2)

            @pl.when(k == 0)
            def _():
                acc_ref[...] = p

            @pl.when(k > 0)
            def _():
                acc_ref[...] += p

            @pl.when(k == nk - 1)
            def _():
                finish(acc_ref[...])

    in_specs = [a_spec, b_spec]
    args = [a, b]
    if after is not None:
        in_specs.append(pl.BlockSpec(memory_space=pl.ANY))
        args.append(after)
    out_shape = jax.ShapeDtypeStruct((M, N), out_dtype)
    out_specs = o_spec
    if fused:
        in_specs += [pl.BlockSpec((tm, tn), lambda i, j, k: (i, j)), pl.BlockSpec((1, tn), lambda i, j, k: (0, j))]
        args += [res, gvec]
        out_shape = (out_shape, jax.ShapeDtypeStruct((M, N), F32))
        out_specs = (o_spec, pl.BlockSpec((tm, tn), lambda i, j, k: (i, j)))
    elif relu2_of is not None:
        in_specs.append(pl.BlockSpec((tm, tn), lambda i, j, k: (i, j)))
        args.append(relu2_of)
    elif relu2:
        out_shape = (out_shape, jax.ShapeDtypeStruct((M, N), BF16))
        out_specs = (o_spec, pl.BlockSpec((tm, tn), lambda i, j, k: (i, j)))
    kw = dict(name=name, grid=(M // tm, N // tn, nk), in_specs=in_specs, out_specs=out_specs,
              out_shape=out_shape, scratch_shapes=[pltpu.VMEM((tm, tn), F32)] if nk > 1 else [])
    if comm is not None:
        return _carry(body, comm, **kw)(*args)
    return _pcall(body, compiler_params=_params(("parallel", "parallel", "arbitrary")), **kw)(*args)


PROJ_TN = 512
ATT_T0, ATT_T1 = 6144 // PROJ_TN, 10752 // PROJ_TN
N_SLABS = (ATT_T1 - ATT_T0) * 4
MAIN_COLS = IN_COLS - (ATT_T1 - ATT_T0) * PROJ_TN


def _proj(h1, w_in_t, after):
    nj = IN_COLS // PROJ_TN

    def body(a_ref, b_ref, after_ref, main_ref, slab_ref):
        j = pl.program_id(1)
        is_att = (j >= ATT_T0) & (j < ATT_T1)
        chunks = [pl.ds(c * 512, 512) for c in range(S // 512)]

        @pl.when(jnp.logical_not(is_att))
        def _():
            for rows in chunks:
                main_ref[rows, :] = _dot(a_ref[rows, :], b_ref[...], NT)

        @pl.when(is_att)
        def _():
            for rows in chunks:
                p = _dot(a_ref[rows, :], b_ref[...], NT)
                for h in range(4):
                    slab_ref[h, rows, :] = p[:, h * 128:(h + 1) * 128]

    main_idx = lambda j: jnp.where(j < ATT_T0, j, jnp.where(j < ATT_T1, ATT_T0 - 1, j - (ATT_T1 - ATT_T0)))
    slab_idx = lambda j: jnp.clip(j - ATT_T0, 0, ATT_T1 - ATT_T0 - 1)
    return _pcall(
        body, name="proj", grid=(1, nj, 1),
        in_specs=[pl.BlockSpec((S, D), lambda i, j, k: (0, 0)), pl.BlockSpec((PROJ_TN, D), lambda i, j, k: (j, 0)),
                  HBM_SPEC],
        out_specs=(pl.BlockSpec((S, PROJ_TN), lambda i, j, k: (0, main_idx(j))),
                   pl.BlockSpec((4, S, 128), lambda i, j, k: (slab_idx(j), 0, 0))),
        out_shape=(jax.ShapeDtypeStruct((S, MAIN_COLS), F32), jax.ShapeDtypeStruct((N_SLABS, S, 128), F32)),
        compiler_params=_params(("arbitrary",) * 3))(h1, w_in_t, after)


TR = 256


def _row_spec(w=D):
    return pl.BlockSpec((TR, w), lambda i: (i, 0))


def _vec_spec(w=D):
    return pl.BlockSpec((1, w), lambda i: (0, 0))


def _norm_mod_fwd(x, g, sh, sc, name):
    def body(x_ref, g_ref, sh_ref, sc_ref, o_ref):
        xv = x_ref[...]
        rstd = lax.rsqrt(jnp.mean(xv * xv, axis=-1, keepdims=True) + RMS_EPS)
        n = xv * rstd * g_ref[...]
        o_ref[...] = (n * (1.0 + sc_ref[...]) + sh_ref[...]).astype(BF16)

    return _pcall(body, name=name, grid=(S // TR,), in_specs=[_row_spec(), _vec_spec(), _vec_spec(), _vec_spec()],
                  out_specs=_row_spec(), out_shape=jax.ShapeDtypeStruct((S, D), BF16),
                  compiler_params=_params(("parallel",)))(x, g, sh, sc)


def _norm_mod_bwd(x, g, sc, dh, dres, name, gate=None):
    gated = gate is not None

    def body(x_ref, g_ref, sc_ref, dh_ref, dres_ref, *rest):
        if gated:
            f_ref, gv_ref, dx_ref, dsc_ref, dsh_ref, dg_ref, dz_ref, dgv_ref = rest
        else:
            dx_ref, dsc_ref, dsh_ref, dg_ref = rest
        i = pl.program_id(0)
        xv = x_ref[...]
        dh = dh_ref[...]
        rstd = lax.rsqrt(jnp.mean(xv * xv, axis=-1, keepdims=True) + RMS_EPS)
        xhat = xv * rstd
        gv = g_ref[...]
        dn = dh * (1.0 + sc_ref[...])
        dxhat = dn * gv
        dx = dres_ref[...] + rstd * (dxhat - xhat * jnp.mean(dxhat * xhat, axis=-1, keepdims=True))
        dx_ref[...] = dx
        sums = [(dsc_ref, jnp.sum(dh * (xhat * gv), axis=0, keepdims=True)),
                (dsh_ref, jnp.sum(dh, axis=0, keepdims=True)),
                (dg_ref, jnp.sum(dn * xhat, axis=0, keepdims=True))]
        if gated:
            dz_ref[...] = (dx * gv_ref[...]).astype(BF16)
            sums.append((dgv_ref, jnp.sum(dx * f_ref[...], axis=0, keepdims=True)))

        @pl.when(i == 0)
        def _():
            for ref, p in sums:
                ref[...] = p

        @pl.when(i > 0)
        def _():
            for ref, p in sums:
                ref[...] += p

    vec = jax.ShapeDtypeStruct((1, D), F32)
    in_specs = [_row_spec(), _vec_spec(), _vec_spec(), _row_spec(), _row_spec()]
    out_specs = [_row_spec(), _vec_spec(), _vec_spec(), _vec_spec()]
    out_shape = [jax.ShapeDtypeStruct((S, D), F32), vec, vec, vec]
    args = [x, g, sc, dh, dres]
    if gated:
        in_specs += [_row_spec(), _vec_spec()]
        out_specs += [_row_spec(), _vec_spec()]
        out_shape += [jax.ShapeDtypeStruct((S, D), BF16), vec]
        args += list(gate)
    return _pcall(body, name=name, grid=(S // TR,), in_specs=in_specs, out_specs=tuple(out_specs),
                  out_shape=tuple(out_shape), compiler_params=_params(("arbitrary",)))(*args)


def _final_loss(x2, tgt, g, f, g2):
    def body(x_ref, t_ref, g_ref, f_ref, g2_ref, loss_ref, dx_ref, dg_ref, df_ref, dg2_ref):
        i = pl.program_id(0)
        xv = x_ref[...]
        gv = g_ref[...]
        rstd = lax.rsqrt(jnp.mean(xv * xv, axis=-1, keepdims=True) + RMS_EPS)
        xhat = xv * rstd
        err = xhat * gv - t_ref[...]
        dy = err * (1.0 / D)
        dxhat = dy * gv
        dx = rstd * (dxhat - xhat * jnp.mean(dxhat * xhat, axis=-1, keepdims=True))
        dx_ref[...] = dx
        df_ref[...] = (dx * g2_ref[...]).astype(BF16)
        p_g = jnp.sum(dy * xhat, axis=0, keepdims=True)
        p_g2 = jnp.sum(dx * f_ref[...], axis=0, keepdims=True)
        p_l = jnp.zeros((1, 128), F32) + 0.5 * jnp.sum(jnp.mean(err * err, axis=-1, keepdims=True))

        @pl.when(i == 0)
        def _():
            dg_ref[...] = p_g
            dg2_ref[...] = p_g2
            loss_ref[...] = p_l

        @pl.when(i > 0)
        def _():
            dg_ref[...] += p_g
            dg2_ref[...] += p_g2
            loss_ref[...] += p_l

    vec = jax.ShapeDtypeStruct((1, D), F32)
    return _pcall(body, name="final_loss", grid=(S // TR,),
                  in_specs=[_row_spec(), _row_spec(), _vec_spec(), _row_spec(), _vec_spec()],
                  out_specs=(_vec_spec(128), _row_spec(), _vec_spec(), _row_spec(), _vec_spec()),
                  out_shape=(jax.ShapeDtypeStruct((1, 128), F32), jax.ShapeDtypeStruct((S, D), F32), vec,
                             jax.ShapeDtypeStruct((S, D), BF16), vec),
                  compiler_params=_params(("arbitrary",)))(x2, tgt, g, f, g2)


HALF = 512


def _merge_fwd(proj, ret_out, att_out):
    def body(ga_ref, gb_ref, r_ref, a_ref, o_ref):
        o_ref[...] = (jax.nn.sigmoid(ga_ref[...]) * r_ref[...] + jax.nn.sigmoid(gb_ref[...]) * a_ref[...]).astype(BF16)

    blk = lambda off: pl.BlockSpec((TR, HALF), lambda i, j: (i, off // HALF + j))
    return _pcall(body, name="merge_fwd", grid=(S // TR, D // HALF),
                  in_specs=[blk(OFF_GA), blk(OFF_GB), blk(0), blk(0)], out_specs=blk(0),
                  out_shape=jax.ShapeDtypeStruct((S, D), BF16),
                  compiler_params=_params(("parallel", "parallel")))(proj, proj, ret_out, att_out)


def _merge_bwd(proj, ret_out, att_out, dmerged):
    def body(ga_ref, gb_ref, r_ref, a_ref, dm_ref, dr_ref, da_ref, dga_ref, dgb_ref):
        sa = jax.nn.sigmoid(ga_ref[...])
        sb = jax.nn.sigmoid(gb_ref[...])
        dm = dm_ref[...]
        dr_ref[...] = (dm * sa).astype(BF16)
        da_ref[...] = (dm * sb).astype(BF16)
        dga_ref[...] = (dm * r_ref[...] * (sa * (1.0 - sa))).astype(BF16)
        dgb_ref[...] = (dm * a_ref[...] * (sb * (1.0 - sb))).astype(BF16)

    blk = lambda off: pl.BlockSpec((TR, HALF), lambda i, j: (i, off // HALF + j))
    o = jax.ShapeDtypeStruct((S, D), BF16)
    return _pcall(body, name="merge_bwd", grid=(S // TR, D // HALF),
                  in_specs=[blk(OFF_GA), blk(OFF_GB), blk(0), blk(0), blk(0)], out_specs=(blk(0),) * 4,
                  out_shape=(o, o, o, o),
                  compiler_params=_params(("parallel", "parallel")))(proj, proj, ret_out, att_out, dmerged)


def _ret_tables():
    H, C = RET_HEADS, CHUNK
    log_g = jnp.log1p(-(2.0 ** (-5.0 - jnp.arange(H, dtype=F32))))
    idx = jnp.arange(C, dtype=F32)
    rel = idx[:, None] - idx[None, :]
    inner = jnp.where(rel >= 0, jnp.exp(log_g[:, None, None] * jnp.maximum(rel, 0.0)), 0.0)
    qd = jnp.exp(log_g[:, None] * (idx + 1.0))[:, :, None]
    kd = jnp.exp(log_g[:, None] * (C - 1.0 - idx))[:, :, None]
    cd = jnp.broadcast_to(jnp.exp(log_g * C)[:, None, None], (H, 1, 128))
    half = RET_DK // 2
    inv = 10000.0 ** (-jnp.arange(half, dtype=F32) / half)
    ang = jnp.arange(S, dtype=F32)[:, None] * inv[None, :]
    return inner, qd, kd, cd, jnp.cos(ang), jnp.sin(ang)


def _rot(x, cos, sin):
    x1, x2 = x[:, :128], x[:, 128:]
    return jnp.concatenate([x1 * cos - x2 * sin, x1 * sin + x2 * cos], axis=1)


def _rot_t(d, cos, sin):
    d1, d2 = d[:, :128], d[:, 128:]
    return jnp.concatenate([d1 * cos + d2 * sin, d2 * cos - d1 * sin], axis=1)


RET_COLS = OFF_ATT
RET_VW = RET_HEADS * RET_DV


def _ret_specs(chunk_of):
    ci = chunk_of
    whole = lambda shape: pl.BlockSpec(shape, lambda t: (0,) * len(shape))
    return [
        pl.BlockSpec((CHUNK, RET_COLS), lambda t: (ci(t), 0)),
        pl.BlockSpec((CHUNK, 128), lambda t: (ci(t), 0)),
        pl.BlockSpec((CHUNK, 128), lambda t: (ci(t), 0)),
        whole((RET_HEADS, CHUNK, CHUNK)), whole((RET_HEADS, CHUNK, 1)), whole((RET_HEADS, CHUNK, 1)),
        whole((RET_HEADS, 1, 128)), whole((1, RET_VW)), whole((1, RET_VW)),
    ]


def _ret_cols(h):
    q = slice(OFF_RQ + h * RET_DK, OFF_RQ + (h + 1) * RET_DK)
    k = slice(OFF_RK + h * RET_DK, OFF_RK + (h + 1) * RET_DK)
    v = slice(OFF_RV + h * RET_DV, OFF_RV + (h + 1) * RET_DV)
    g = slice(OFF_RG + h * RET_DV, OFF_RG + (h + 1) * RET_DV)
    return q, k, v, g, slice(h * RET_DV, (h + 1) * RET_DV)


def _ret_fwd(proj, tables, gn_g, gn_b, comm=None):
    inner, qd, kd, cd, cos, sin = tables

    def body(x_ref, cos_ref, sin_ref, in_ref, qd_ref, kd_ref, cd_ref, g_ref, b_ref,
             gated_ref, ro_ref, st_ref, s_scr):
        i = pl.program_id(0)

        @pl.when(i == 0)
        def _():
            s_scr[...] = jnp.zeros_like(s_scr)

        cosv, sinv = cos_ref[...], sin_ref[...]
        for h in range(RET_HEADS):
            cq, ck, cv, cg, co = _ret_cols(h)
            q = _rot(x_ref[:, cq], cosv, sinv)
            k = _rot(x_ref[:, ck], cosv, sinv) * (RET_DK ** -0.5)
            v = x_ref[:, cv]
            st = s_scr[h]
            st_ref[h] = st.astype(BF16)
            s = _dot(q, k, NT) * in_ref[h]
            o = _dot(s, v, NN) + _dot(q, st, NN) * qd_ref[h]
            s_scr[h] = st * cd_ref[h, :, :1] + _dot(k * kd_ref[h], v, TN)
            ro_ref[:, co] = o
            mu = jnp.mean(o, axis=-1, keepdims=True)
            oc = o - mu
            var = jnp.mean(oc * oc, axis=-1, keepdims=True)
            rn = oc * lax.rsqrt(var + GN_EPS) * g_ref[:, co] + b_ref[:, co]
            rg = x_ref[:, cg]
            gated_ref[:, co] = (rg * jax.nn.sigmoid(rg) * rn).astype(BF16)

    ospec = pl.BlockSpec((CHUNK, RET_VW), lambda t: (t, 0))
    kw = dict(name="ret_fwd", grid=(N_CHUNK,), in_specs=_ret_specs(lambda t: t),
              out_specs=(ospec, ospec, pl.BlockSpec((RET_HEADS, None, RET_DK, RET_DV), lambda t: (0, t, 0, 0))),
              out_shape=(jax.ShapeDtypeStruct((S, RET_VW), BF16), jax.ShapeDtypeStruct((S, RET_VW), F32),
                         jax.ShapeDtypeStruct((RET_HEADS, N_CHUNK, RET_DK, RET_DV), BF16)),
              scratch_shapes=[pltpu.VMEM((RET_HEADS, RET_DK, RET_DV), F32)])
    args = (proj, cos, sin, inner, qd, kd, cd, gn_g, gn_b)
    if comm is not None:
        return _carry(body, comm, **kw)(*args)
    return _pcall(body, compiler_params=_params(("arbitrary",)), **kw)(*args)


def _ret_bwd(proj, tables, gn_g, gn_b, ro, states, dgated, comm=None):
    inner, qd, kd, cd, cos, sin = tables
    last = N_CHUNK - 1

    def body(x_ref, cos_ref, sin_ref, in_ref, qd_ref, kd_ref, cd_ref, g_ref, b_ref, ro_ref, st_ref, dg_ref,
             dx_ref, gg_ref, gb_ref, gs_scr):
        t = pl.program_id(0)

        @pl.when(t == 0)
        def _():
            gs_scr[...] = jnp.zeros_like(gs_scr)
            gg_ref[...] = jnp.zeros_like(gg_ref)
            gb_ref[...] = jnp.zeros_like(gb_ref)

        cosv, sinv = cos_ref[...], sin_ref[...]
        for h in range(RET_HEADS):
            cq, ck, cv, cg, co = _ret_cols(h)
            q = _rot(x_ref[:, cq], cosv, sinv)
            k = _rot(x_ref[:, ck], cosv, sinv) * (RET_DK ** -0.5)
            v = x_ref[:, cv]
            qdv, kdv, dm = qd_ref[h], kd_ref[h], in_ref[h]
            st = st_ref[h]
            o = ro_ref[:, co]
            gv = g_ref[:, co]
            mu = jnp.mean(o, axis=-1, keepdims=True)
            oc = o - mu
            rstd = lax.rsqrt(jnp.mean(oc * oc, axis=-1, keepdims=True) + GN_EPS)
            ohat = oc * rstd
            rn = ohat * gv + b_ref[:, co]
            rg = x_ref[:, cg]
            sg = jax.nn.sigmoid(rg)
            dgt = dg_ref[:, co]
            drn = dgt * (rg * sg)
            dx_ref[:, cg] = (dgt * rn * (sg * (1.0 + rg * (1.0 - sg)))).astype(BF16)
            gg_ref[:, co] += jnp.sum(drn * ohat, axis=0, keepdims=True)
            gb_ref[:, co] += jnp.sum(drn, axis=0, keepdims=True)
            dohat = drn * gv
            do = rstd * (dohat - jnp.mean(dohat, axis=-1, keepdims=True)
                         - ohat * jnp.mean(dohat * ohat, axis=-1, keepdims=True))
            gs = gs_scr[h]
            s = _dot(q, k, NT) * dm
            dsr = _dot(do, v, NT) * dm
            dq = _dot(dsr, k, NN) + _dot(do, st, NT) * qdv
            dk = _dot(dsr, q, TN) + _dot(v, gs, NT) * kdv
            dv = _dot(s, do, TN) + _dot(k * kdv, gs, NN)
            gs_scr[h] = gs * cd_ref[h, :, :1] + _dot(q * qdv, do, TN)
            dx_ref[:, cq] = _rot_t(dq, cosv, sinv).astype(BF16)
            dx_ref[:, ck] = (_rot_t(dk, cosv, sinv) * (RET_DK ** -0.5)).astype(BF16)
            dx_ref[:, cv] = dv.astype(BF16)

    rev = lambda t: last - t
    vblk = pl.BlockSpec((CHUNK, RET_VW), lambda t: (rev(t), 0))
    vspec = pl.BlockSpec((1, RET_VW), lambda t: (0, 0))
    kw = dict(name="ret_bwd", grid=(N_CHUNK,),
              in_specs=_ret_specs(rev) + [vblk, pl.BlockSpec((RET_HEADS, None, RET_DK, RET_DV),
                                                             lambda t: (0, rev(t), 0, 0)), vblk],
              out_specs=(pl.BlockSpec((CHUNK, RET_COLS), lambda t: (rev(t), 0)), vspec, vspec),
              out_shape=(jax.ShapeDtypeStruct((S, RET_COLS), BF16), jax.ShapeDtypeStruct((1, RET_VW), F32),
                         jax.ShapeDtypeStruct((1, RET_VW), F32)),
              scratch_shapes=[pltpu.VMEM((RET_HEADS, RET_DK, RET_DV), F32)])
    args = (proj, cos, sin, inner, qd, kd, cd, gn_g, gn_b, ro, states, dgated)
    if comm is not None:
        return _carry(body, comm, **kw)(*args)
    return _pcall(body, compiler_params=_params(("arbitrary",)), **kw)(*args)


def _bucket_tables():
    qi = np.arange(ATT_BLK)[:, None]
    kj = np.arange(2 * ATT_BLK)[None, :]
    m = ATT_BLK + qi - kj
    out = []
    for win, dil in ATT_GROUPS:
        w = win // dil
        dist = (np.clip(m, 0, w) * dil).astype(np.int32)
        max_exact = N_BUCKETS // 2
        d_f = np.maximum(dist, 1).astype(np.float32)
        large = max_exact + (np.log(d_f / np.float32(max_exact)) / np.float32(math.log(MAX_DIST / max_exact))
                             * np.float32(N_BUCKETS - max_exact)).astype(np.int32)
        large = np.minimum(large, N_BUCKETS - 1)
        out.append(np.where(dist < max_exact, dist, large).astype(np.int32))
    return np.stack(out)


def _bias_build(rel_bias, buckets):
    def body(tab_ref, bk_ref, o_ref):
        hh = pl.program_id(0)
        bk = bk_ref[...]
        acc = jnp.zeros((ATT_BLK, 2 * ATT_BLK), F32)
        for b in range(N_BUCKETS):
            acc = jnp.where(bk == b, tab_ref[b, hh], acc)
        o_ref[...] = acc

    nh = len(ATT_GROUPS) * ATT_HG
    return _pcall(body, name="bias_build", grid=(nh,),
                  in_specs=[pl.BlockSpec(memory_space=pltpu.SMEM),
                            pl.BlockSpec((None, ATT_BLK, 2 * ATT_BLK), lambda hh: (hh // ATT_HG, 0, 0))],
                  out_specs=pl.BlockSpec((None, ATT_BLK, 2 * ATT_BLK), lambda hh: (hh, 0, 0)),
                  out_shape=jax.ShapeDtypeStruct((nh, ATT_BLK, 2 * ATT_BLK), F32),
                  compiler_params=_params(("parallel",)))(rel_bias, buckets)


def _bias_grad(ds_sum, buckets):
    def body(ds_ref, bk_ref, o_ref):
        bk = bk_ref[...]
        ds = ds_ref[...]
        rows = lax.broadcasted_iota(jnp.int32, (N_BUCKETS, 128), 0)
        acc = jnp.zeros((N_BUCKETS, 128), F32)
        for b in range(N_BUCKETS):
            acc = jnp.where(rows == b, jnp.sum(jnp.where(bk == b, ds, 0.0)), acc)
        o_ref[...] = acc

    nh = len(ATT_GROUPS) * ATT_HG
    return _pcall(body, name="bias_grad", grid=(nh,),
                  in_specs=[pl.BlockSpec((None, ATT_BLK, 2 * ATT_BLK), lambda hh: (hh, 0, 0)),
                            pl.BlockSpec((None, ATT_BLK, 2 * ATT_BLK), lambda hh: (hh // ATT_HG, 0, 0))],
                  out_specs=pl.BlockSpec((None, N_BUCKETS, 128), lambda hh: (hh, 0, 0)),
                  out_shape=jax.ShapeDtypeStruct((nh, N_BUCKETS, 128), F32),
                  compiler_params=_params(("parallel",)))(ds_sum, buckets)


def _att_valid(n):
    qi = lax.broadcasted_iota(jnp.int32, (ATT_BLK, 2 * ATT_BLK), 0)
    kj = lax.broadcasted_iota(jnp.int32, (ATT_BLK, 2 * ATT_BLK), 1)
    m = ATT_BLK + qi - kj
    first_key = jnp.where(n > 0, 0, ATT_BLK)
    return (m >= 0) & (m <= ATT_BLK) & (kj >= first_key)


ATT_HP = (1, 2, 2)


def _att_geometry(gi):
    _, dil = ATT_GROUPS[gi]
    return dil, S // dil // ATT_BLK, ATT_HP[gi]


def _blk(dil, r, n):
    if dil == 1:
        return pl.ds(n * ATT_BLK, ATT_BLK)
    return pl.ds(r + n * ATT_BLK * dil, ATT_BLK, stride=dil)


def _slab_specs(gi):
    _, _, hp = _att_geometry(gi)
    per = ATT_HG // hp
    return [pl.BlockSpec((hp, S, ATT_DH), lambda g, r, part=part: ((3 * gi + part) * per + g, 0, 0))
            for part in range(3)]


def _head_specs(gi, count):
    _, _, hp = _att_geometry(gi)
    return [pl.BlockSpec((hp, S, ATT_DH), lambda g, r: (g, 0, 0))] * count


def _bias_spec(gi):
    _, _, hp = _att_geometry(gi)
    return pl.BlockSpec((hp, ATT_BLK, 2 * ATT_BLK), lambda g, r: (gi * (ATT_HG // hp) + g, 0, 0))


def _att_valid_first():
    qi = lax.broadcasted_iota(jnp.int32, (ATT_BLK, ATT_BLK), 0)
    kj = lax.broadcasted_iota(jnp.int32, (ATT_BLK, ATT_BLK), 1)
    return kj <= qi


def _att_fwd(slabs, bias, gi, comm=None):
    dil, nb, hp = _att_geometry(gi)
    scale = ATT_DH ** -0.5

    def body(q_ref, k_ref, v_ref, bias_ref, o_ref, l_ref):
        r = pl.program_id(1)
        for n in range(nb):
            cur = _blk(dil, r, n)
            valid = _att_valid(n) if n > 0 else _att_valid_first()
            for h in range(hp):
                if n > 0:
                    prev = _blk(dil, r, n - 1)
                    kk = jnp.concatenate([k_ref[h, prev, :], k_ref[h, cur, :]], axis=0)
                    vv = jnp.concatenate([v_ref[h, prev, :], v_ref[h, cur, :]], axis=0)
                    bias = bias_ref[h]
                else:
                    kk, vv, bias = k_ref[h, cur, :], v_ref[h, cur, :], bias_ref[h, :, pl.ds(ATT_BLK, ATT_BLK)]
                s = _dot(q_ref[h, cur, :], kk, NT) * scale + bias
                s = jnp.where(valid, s, -1e30)
                mx = jnp.max(s, axis=-1, keepdims=True)
                e = jnp.exp(s - mx)
                den = jnp.sum(e, axis=-1, keepdims=True)
                o_ref[h, cur, :] = _dot(e / den, vv, NN)
                l_ref[h, cur, :] = jnp.broadcast_to(mx + jnp.log(den), (ATT_BLK, ATT_DH))

    osh = jax.ShapeDtypeStruct((ATT_HG, S, ATT_DH), F32)
    kw = dict(name=f"att_fwd{gi}", grid=(ATT_HG // hp, dil), in_specs=_slab_specs(gi) + [_bias_spec(gi)],
              out_specs=tuple(_head_specs(gi, 2)), out_shape=(osh, osh))
    if comm is not None:
        return _carry(body, comm, **kw)(slabs, slabs, slabs, bias)
    return _pcall(body, compiler_params=_params(("parallel", "arbitrary")), **kw)(slabs, slabs, slabs, bias)


def _att_bwd(slabs, bias, o, lse, do, dlse, gi, comm=None):
    dil, nb, hp = _att_geometry(gi)
    per = ATT_HG // hp
    scale = ATT_DH ** -0.5
    wh = hp * ATT_DH
    wide = lambda t: jnp.concatenate([t, t], axis=1)

    def body(q_ref, k_ref, v_ref, bias_ref, o_ref, l_ref, do_ref, dl_ref, dq_ref, dk_ref, dv_ref, ds_ref):
        r = pl.program_id(1)

        @pl.when(r == 0)
        def _():
            ds_ref[...] = jnp.zeros_like(ds_ref)

        for h in range(hp):
            sl = slice(h * ATT_DH, (h + 1) * ATT_DH)
            carry_k = carry_v = None
            for n in range(nb):
                cur = _blk(dil, r, n)
                q = q_ref[h, cur, :]
                dov = do_ref[h, cur, :]
                delta = jnp.sum(dov * o_ref[h, cur, :], axis=-1, keepdims=True)
                out_rows = pl.ds(n * ATT_BLK, ATT_BLK)
                if n == 0:
                    own = pl.ds(ATT_BLK, ATT_BLK)
                    kk, vv = k_ref[h, cur, :], v_ref[h, cur, :]
                    s = _dot(q, kk, NT) * scale + bias_ref[h, :, own]
                    p = jnp.where(_att_valid_first(), jnp.exp(s - l_ref[h, cur, :]), 0.0)
                    ds = p * (_dot(dov, vv, NT) - delta + dl_ref[h, cur, :])
                    ds_ref[h, :, own] += ds
                    dq_ref[out_rows, sl] = (_dot(ds, kk, NN) * scale).astype(BF16)
                    carry_k, carry_v = _dot(ds, q, TN) * scale, _dot(p, dov, TN)
                    continue
                prev = _blk(dil, r, n - 1)
                kk = jnp.concatenate([k_ref[h, prev, :], k_ref[h, cur, :]], axis=0)
                vv = jnp.concatenate([v_ref[h, prev, :], v_ref[h, cur, :]], axis=0)
                s = _dot(q, kk, NT) * scale + bias_ref[h]
                p = jnp.where(_att_valid(n), jnp.exp(s - wide(l_ref[h, cur, :])), 0.0)
                dp = _dot(dov, vv, NT)
                ds = p * (dp - delta + wide(dl_ref[h, cur, :]))
                ds_ref[h] += ds
                dq_ref[out_rows, sl] = (_dot(ds, kk, NN) * scale).astype(BF16)
                dkk = _dot(ds, q, TN) * scale
                dvv = _dot(p, dov, TN)
                before = pl.ds((n - 1) * ATT_BLK, ATT_BLK)
                dk_ref[before, sl] = (carry_k + dkk[:ATT_BLK]).astype(BF16)
                dv_ref[before, sl] = (carry_v + dvv[:ATT_BLK]).astype(BF16)
                carry_k, carry_v = dkk[ATT_BLK:], dvv[ATT_BLK:]
            last = pl.ds((nb - 1) * ATT_BLK, ATT_BLK)
            dk_ref[last, sl] = carry_k.astype(BF16)
            dv_ref[last, sl] = carry_v.astype(BF16)

    out_spec = pl.BlockSpec((S // dil, wh), lambda g, r: (0, r * per + g))
    osh = jax.ShapeDtypeStruct((S // dil, dil * AW), BF16)
    kw = dict(name=f"att_bwd{gi}", grid=(per, dil), in_specs=_slab_specs(gi) + [_bias_spec(gi)] + _head_specs(gi, 4),
              out_specs=(out_spec, out_spec, out_spec,
                         pl.BlockSpec((hp, ATT_BLK, 2 * ATT_BLK), lambda g, r: (g, 0, 0))),
              out_shape=(osh, osh, osh, jax.ShapeDtypeStruct((ATT_HG, ATT_BLK, 2 * ATT_BLK), F32)))
    args = (slabs, slabs, slabs, bias, o, lse, do, dlse)
    if comm is not None:
        return _carry(body, comm, **kw)(*args)
    return _pcall(body, compiler_params=_params(("arbitrary", "arbitrary")), **kw)(*args)


AW = ATT_HG * ATT_DH


def _mix_weights(l0, l1, l2):
    mx = jnp.maximum(jnp.maximum(l0, l1), l2)
    e0, e1, e2 = jnp.exp(l0 - mx), jnp.exp(l1 - mx), jnp.exp(l2 - mx)
    den = e0 + e1 + e2
    return e0 / den, e1 / den, e2 / den


def _heads_spec():
    return pl.BlockSpec((ATT_HG, TR, ATT_DH), lambda i: (0, i, 0))


def _mix_fwd(os_, ls, comm=None):
    def body(o0, o1, o2, l0, l1, l2, att_ref):
        for h in range(ATT_HG):
            w0, w1, w2 = _mix_weights(l0[h], l1[h], l2[h])
            att_ref[:, h * ATT_DH:(h + 1) * ATT_DH] = (w0 * o0[h] + w1 * o1[h] + w2 * o2[h]).astype(BF16)

    kw = dict(name="mix_fwd", grid=(S // TR,), in_specs=[_heads_spec()] * 6, out_specs=_row_spec(AW),
              out_shape=jax.ShapeDtypeStruct((S, AW), BF16))
    if comm is not None:
        return _carry(body, comm, **kw)(*os_, *ls)
    return _pcall(body, compiler_params=_params(("parallel",)), **kw)(*os_, *ls)


def _mix_bwd(os_, ls, datt):
    def body(o0, o1, o2, l0, l1, l2, da_ref, d0, d1, d2, e0, e1, e2):
        for h in range(ATT_HG):
            ws = _mix_weights(l0[h], l1[h], l2[h])
            da = da_ref[:, h * ATT_DH:(h + 1) * ATT_DH]
            dws = []
            for o_ref, w, d_ref in zip((o0, o1, o2), ws, (d0, d1, d2)):
                d_ref[h] = w * da
                dws.append(jnp.broadcast_to(jnp.sum(da * o_ref[h], axis=-1, keepdims=True), (TR, ATT_DH)))
            tot = ws[0] * dws[0] + ws[1] * dws[1] + ws[2] * dws[2]
            for w, dw, e_ref in zip(ws, dws, (e0, e1, e2)):
                e_ref[h] = w * (dw - tot)

    o = jax.ShapeDtypeStruct((ATT_HG, S, ATT_DH), F32)
    return _pcall(body, name="mix_bwd", grid=(S // TR,), in_specs=[_heads_spec()] * 6 + [_row_spec(AW)],
                  out_specs=(_heads_spec(),) * 6, out_shape=(o,) * 6,
                  compiler_params=_params(("parallel",)))(*os_, *ls, datt)


def _ada_fwd(c_all, w_sh, b_sl):
    def body(c_ref, w_ref, b_ref, o_ref):
        cv = c_ref[...]
        o_ref[...] = _dot(cv * jax.nn.sigmoid(cv), w_ref[...], NN) + b_ref[...]

    return _pcall(body, name="ada_fwd", out_shape=jax.ShapeDtypeStruct((N_DEV, w_sh.shape[1]), F32),
                  compiler_params=_params())(c_all, w_sh, b_sl)


def _ada_bwd(c_all, dm_sl):
    def body(c_ref, d_ref, o_ref):
        cv = c_ref[...]
        o_ref[...] = _dot(cv * jax.nn.sigmoid(cv), d_ref[...], TN)

    return _pcall(body, name="ada_bwd", out_shape=jax.ShapeDtypeStruct((D, dm_sl.shape[1]), F32),
                  compiler_params=_params())(c_all, dm_sl)


N_MOD = 6


def _sum_small(gathered):
    n = len(gathered)

    def body(*refs):
        ins, (gb_ref, dm_ref), outs = refs[:n], refs[n:n + 2], refs[n + 2:]

        def total(r):
            acc = r[0]
            for e in range(1, N_DEV):
                acc = acc + r[e]
            return acc

        for i in range(N_MOD):
            cols = slice(i * D, (i + 1) * D)
            gb_ref[:, cols] = total(ins[i])
            for e in range(N_DEV):
                dm_ref[e:e + 1, cols] = ins[i][e]
        for r, o_ref in zip(ins[N_MOD:], outs):
            o_ref[...] = total(r)

    shapes = (jax.ShapeDtypeStruct((1, N_MOD * D), F32), jax.ShapeDtypeStruct((N_DEV, N_MOD * D), F32),
              *[jax.ShapeDtypeStruct(g.shape[1:], F32) for g in gathered[N_MOD:]])
    res = _pcall(body, name="sum_small", out_shape=shapes, compiler_params=_params())(*gathered)
    return res[0], res[1], res[2:]


def _row_tile(m, n):
    t = max(8, min(m, (1 << 19) // n // 8 * 8))
    while m % t:
        t -= 8
    return t


def _pair_sum(full, recv, sel, name):
    _, _, m, n = full.shape
    t = _row_tile(m, n)

    def body(sel_ref, a_ref, b_ref, o_ref):
        o_ref[...] = (a_ref[...].astype(F32) + b_ref[...].astype(F32)).astype(o_ref.dtype)

    gs = pltpu.PrefetchScalarGridSpec(
        num_scalar_prefetch=1, grid=(4, m // t),
        in_specs=[pl.BlockSpec((None, None, t, n), lambda q, i, s: (q, s[0], i, 0)),
                  pl.BlockSpec((None, t, n), lambda q, i, s: (q, i, 0))],
        out_specs=pl.BlockSpec((None, t, n), lambda q, i, s: (q, i, 0)))
    return _pcall(body, name=name, grid_spec=gs, out_shape=jax.ShapeDtypeStruct((4, m, n), full.dtype),
                  compiler_params=_params(("parallel", "parallel")))(sel, full, recv)


def _chip_sum(part, recv, sel, name):
    _, m, n = part.shape
    t = _row_tile(m, n)

    def body(sel_ref, a_ref, r_ref, o_ref):
        o_ref[...] = ((a_ref[...].astype(F32) + r_ref[0].astype(F32)) + r_ref[1].astype(F32)) + r_ref[2].astype(F32)

    gs = pltpu.PrefetchScalarGridSpec(
        num_scalar_prefetch=1, grid=(m // t,),
        in_specs=[pl.BlockSpec((None, t, n), lambda i, s: (s[0], i, 0)),
                  pl.BlockSpec((3, t, n), lambda i, s: (0, i, 0))],
        out_specs=pl.BlockSpec((t, n), lambda i, s: (i, 0)))
    return _pcall(body, name=name, grid_spec=gs, out_shape=jax.ShapeDtypeStruct((m, n), F32),
                  compiler_params=_params(("parallel",)))(sel, part, recv)


def _adamw_math(w, g, m, v):
    nm = ADAM_B1 * m + (1.0 - ADAM_B1) * g
    nv = ADAM_B2 * v + (1.0 - ADAM_B2) * (g * g)
    m_hat = nm / (1.0 - ADAM_B1 ** ADAM_STEP)
    v_hat = nv / (1.0 - ADAM_B2 ** ADAM_STEP)
    return -ADAM_LR * (m_hat / (jnp.sqrt(v_hat) + ADAM_EPS) + ADAM_WD * w), nm, nv


def _adamw(w, g, m, v, name):
    _, rows, cols = w.shape
    t = _row_tile(rows, cols)

    def body(w_ref, g_ref, m_ref, v_ref, d_ref, nm_ref, nv_ref):
        d_ref[...], nm_ref[...], nv_ref[...] = _adamw_math(w_ref[...], g_ref[...], m_ref[...], v_ref[...])

    spec3 = pl.BlockSpec((None, t, cols), lambda i: (0, i, 0))
    spec2 = pl.BlockSpec((t, cols), lambda i: (i, 0))
    o = jax.ShapeDtypeStruct(w.shape, F32)
    return _pcall(body, name=name, grid=(rows // t,), in_specs=[spec3, spec2, spec3, spec3], out_specs=(spec3,) * 3,
                  out_shape=(o, o, o), compiler_params=_params(("parallel",)))(w, g, m, v)


def _adamw_reduced(w, m, v, parts, recvs, sel):
    _, rows, cols = w.shape
    half = cols // 2
    t = _row_tile(rows, half)

    def body(sel_ref, w_ref, m_ref, v_ref, pa_ref, pb_ref, ra_ref, rb_ref, g_ref, d_ref, nm_ref, nv_ref):
        total = lambda p_ref, r_ref: ((p_ref[...].astype(F32) + r_ref[0].astype(F32)) + r_ref[1].astype(F32)) \
            + r_ref[2].astype(F32)
        g = jnp.where(pl.program_id(1) == 0, total(pa_ref, ra_ref), total(pb_ref, rb_ref))
        g_ref[...] = g
        d_ref[...], nm_ref[...], nv_ref[...] = _adamw_math(w_ref[...], g, m_ref[...], v_ref[...])

    wspec = pl.BlockSpec((None, t, half), lambda i, j, s: (0, i, j))
    pspec = pl.BlockSpec((None, t, half), lambda i, j, s: (s[0], i, 0))
    rspec = pl.BlockSpec((3, t, half), lambda i, j, s: (0, i, 0))
    gs = pltpu.PrefetchScalarGridSpec(num_scalar_prefetch=1, grid=(rows // t, 2),
                                      in_specs=[wspec, wspec, wspec, pspec, pspec, rspec, rspec],
                                      out_specs=(wspec,) * 4)
    o = jax.ShapeDtypeStruct(w.shape, F32)
    return _pcall(body, name="adamw_w_in", grid_spec=gs, out_shape=(o, o, o, o),
                  compiler_params=_params(("parallel", "arbitrary")))(sel, w, m, v, *parts, *recvs)


def _adamw_small(ws, gs, ms, vs):
    n = len(ws)

    def body(*refs):
        for i in range(n):
            w_ref, g_ref, m_ref, v_ref = (refs[k * n + i] for k in range(4))
            d, nm, nv = _adamw_math(w_ref[...], g_ref[...], m_ref[...], v_ref[...])
            refs[4 * n + i][...] = d
            refs[5 * n + i][...] = nm
            refs[6 * n + i][...] = nv

    shapes = tuple(jax.ShapeDtypeStruct(w.shape, F32) for w in ws)
    res = _pcall(body, name="adamw_small", out_shape=shapes * 3, compiler_params=_params())(*ws, *gs, *ms, *vs)
    return res[:n], res[n:2 * n], res[2 * n:]


def _mesh_pos():
    return lax.axis_index("x"), lax.axis_index("y"), lax.axis_index("c")


class _Gather:
    def __init__(self, arrs, relay=False):
        self.relay = relay
        self.ins = list(arrs)
        na = self.na = len(arrs)
        self.out_shape = tuple(jax.ShapeDtypeStruct((N_DEV,) + a.shape, a.dtype) for a in arrs)
        self.sems = [pltpu.SemaphoreType.DMA((7 * na,)), pltpu.SemaphoreType.DMA((7 * na,)),
                     pltpu.SemaphoreType.DMA((na,))]

    def _copies(self, ins, outs, sems):
        send_sems, recv_sems, local_sems = sems
        x, y, c = _mesh_pos()
        me, sibling = (x, y, c), (x, y, 1 - c)
        chips = [(1 - x, y), (x, 1 - y), (1 - x, 1 - y)]

        def slot(p):
            return 4 * p[0] + 2 * p[1] + p[2]

        def copy(a, k, block, to, src=None):
            dst = outs[a].at[slot(block)]
            return pltpu.make_async_remote_copy(
                src_ref=dst if src is None else src, dst_ref=dst, send_sem=send_sems.at[7 * a + k],
                recv_sem=recv_sems.at[7 * a + k], device_id=to, device_id_type=MESH)

        mine = [pltpu.make_async_copy(ins[a], outs[a].at[slot(me)], local_sems.at[a]) for a in range(self.na)]
        direct = chips[:2] if self.relay else chips
        first = []
        for a in range(self.na):
            first.append(copy(a, 0, me, sibling, src=ins[a]))
            first += [copy(a, 1 + j, me, (*chip, c), src=ins[a]) for j, chip in enumerate(direct)]
        return me, sibling, chips, c, copy, mine, first

    def start(self, ins, outs, sems):
        *_, mine, first = self._copies(ins, outs, sems)
        for cp in mine + first:
            cp.start()

    def finish(self, ins, outs, sems):
        me, sibling, chips, c, copy, mine, first = self._copies(ins, outs, sems)
        x, y = me[0], me[1]
        passed = []
        for j, chip in enumerate(chips):
            for a in range(self.na):
                if self.relay and j == 2:
                    owner = ((x + 1 - c) % 2, (y + c) % 2, c)
                    cp = copy(a, 3, owner, ((x + c) % 2, (y + 1 - c) % 2, c))
                    cp.start()
                    passed.append(cp)
                copy(a, 1 + j, (*chip, c), me).wait_recv()
                cp = copy(a, 4 + j, (*chip, c), sibling)
                cp.start()
                passed.append(cp)
        for a in range(self.na):
            copy(a, 0, sibling, me).wait_recv()
            for j, chip in enumerate(chips):
                copy(a, 4 + j, (*chip, 1 - c), me).wait_recv()
        for cp in first + passed:
            cp.wait_send()
        for cp in mine:
            cp.wait()


class _ExchangeCore:
    def __init__(self, fulls):
        self.ins = list(fulls)
        self.out_shape = tuple(jax.ShapeDtypeStruct((4,) + f.shape[2:], f.dtype) for f in fulls)
        self.sems = [pltpu.SemaphoreType.DMA((4 * len(fulls),)), pltpu.SemaphoreType.DMA((4 * len(fulls),))]

    def _copies(self, ins, outs, sems):
        send_sems, recv_sems = sems
        x, y, c = _mesh_pos()
        return [pltpu.make_async_remote_copy(
            src_ref=ins[a].at[q, 1 - c], dst_ref=outs[a].at[q], send_sem=send_sems.at[4 * a + q],
            recv_sem=recv_sems.at[4 * a + q], device_id=(x, y, 1 - c), device_id_type=MESH)
            for a in range(len(self.ins)) for q in range(4)]

    def start(self, ins, outs, sems):
        for cp in self._copies(ins, outs, sems):
            cp.start()

    def finish(self, ins, outs, sems):
        for cp in self._copies(ins, outs, sems):
            cp.wait()


class _ExchangeChip:
    def __init__(self, parts):
        self.ins = list(parts)
        self.out_shape = tuple(jax.ShapeDtypeStruct((3,) + p.shape[1:], p.dtype) for p in parts)
        self.sems = [pltpu.SemaphoreType.DMA((3 * len(parts),)), pltpu.SemaphoreType.DMA((3 * len(parts),))]

    def _copies(self, ins, outs, sems):
        send_sems, recv_sems = sems
        x, y, c = _mesh_pos()
        chips = [(1 - x, y), (x, 1 - y), (1 - x, 1 - y)]
        return [pltpu.make_async_remote_copy(
            src_ref=ins[a].at[2 * px + py], dst_ref=outs[a].at[j], send_sem=send_sems.at[3 * a + j],
            recv_sem=recv_sems.at[3 * a + j], device_id=(px, py, c), device_id_type=MESH)
            for a in range(len(self.ins)) for j, (px, py) in enumerate(chips)]

    def start(self, ins, outs, sems):
        for cp in self._copies(ins, outs, sems):
            cp.start()

    def finish(self, ins, outs, sems):
        for cp in self._copies(ins, outs, sems):
            cp.wait()


HBM_ONLY = pl.BlockSpec(memory_space=pltpu.HBM)
SEM_SPEC = pl.BlockSpec(memory_space=pltpu.SEMAPHORE)
SIDE_EFFECT = pltpu.SideEffectType.DATAFLOW_SIDE_EFFECTING


def _chip_copies(p_refs, land_refs, send_sems, recv_sems):
    x, y, c = _mesh_pos()
    return [pltpu.make_async_remote_copy(
        src_ref=p_refs[a].at[2 * px + py], dst_ref=land_refs[a].at[j], send_sem=send_sems.at[3 * a + j],
        recv_sem=recv_sems.at[3 * a + j], device_id=(px, py, c), device_id_type=MESH)
        for a in range(len(p_refs)) for j, (px, py) in enumerate([(1 - x, y), (x, 1 - y), (1 - x, 1 - y)])]


def _chip_exchange_start(parts, name):
    n = len(parts)
    lands = [lax.empty((3,) + p.shape[1:], p.dtype) for p in parts]

    def body(*refs):
        p_refs, land_refs, (send_sems, recv_sems) = refs[:n], refs[n:2 * n], refs[2 * n:2 * n + 2]
        for cp in _chip_copies(p_refs, land_refs, send_sems, recv_sems):
            cp.start()
        token = refs[-1]
        token[...] = jnp.zeros_like(token)

    hbm = lambda t: pltpu.HBM(t.shape, t.dtype)
    res = pl.pallas_call(
        body, name=name,
        out_shape=(pltpu.SemaphoreType.DMA((3 * n,)), pltpu.SemaphoreType.DMA((3 * n,)), *[hbm(t) for t in parts + lands],
                   jax.ShapeDtypeStruct((8, 128), F32)),
        in_specs=(HBM_ONLY,) * (2 * n),
        out_specs=(SEM_SPEC, SEM_SPEC, *[HBM_ONLY] * (2 * n), pl.BlockSpec(memory_space=pltpu.VMEM)),
        input_output_aliases={i: 2 + i for i in range(2 * n)},
        compiler_params=pltpu.CompilerParams(has_side_effects=SIDE_EFFECT))(
        *[pltpu.with_memory_space_constraint(t, pltpu.HBM) for t in parts + lands])
    return (res[0], res[1], list(res[2:2 + n]), list(res[2 + n:2 + 2 * n])), res[-1]


def _chip_exchange_wait(in_flight, after, name):
    send_sems, recv_sems, parts, lands = in_flight
    n = len(parts)

    def body(*refs):
        p_refs, land_refs, (send_sems, recv_sems) = refs[:n], refs[n:2 * n], refs[2 * n:2 * n + 2]
        for cp in _chip_copies(p_refs, land_refs, send_sems, recv_sems):
            cp.wait_send()
            cp.wait_recv()

    res = pl.pallas_call(
        body, name=name, out_shape=tuple(pltpu.HBM(t.shape, t.dtype) for t in parts + lands),
        in_specs=(*[HBM_ONLY] * (2 * n), SEM_SPEC, SEM_SPEC, pl.BlockSpec(memory_space=pl.ANY)),
        out_specs=(HBM_ONLY,) * (2 * n), input_output_aliases={i: i for i in range(2 * n)},
        compiler_params=pltpu.CompilerParams(has_side_effects=SIDE_EFFECT))(*parts, *lands, send_sems, recv_sems, after)
    return list(res[:n]), list(res[n:])


def _slot(p):
    return 4 * p[0] + 2 * p[1] + p[2]


def _gather_copies(src_refs, out_refs, send_sems, recv_sems):
    x, y, c = _mesh_pos()
    targets = [(x, y, 1 - c), (1 - x, y, c), (x, 1 - y, c), (1 - x, 1 - y, c)]
    return [pltpu.make_async_remote_copy(
        src_ref=src_refs[a], dst_ref=out_refs[a].at[_slot((x, y, c))], send_sem=send_sems.at[4 * a + k],
        recv_sem=recv_sems.at[4 * a + k], device_id=to, device_id_type=MESH)
        for a in range(len(src_refs)) for k, to in enumerate(targets)]


def _gather_start(shards, after, name):
    n = len(shards)
    outs = [lax.empty((N_DEV,) + s.shape, s.dtype) for s in shards]

    def body(*refs):
        for cp in _gather_copies(refs[:n], refs[n:2 * n], refs[2 * n + 1], refs[2 * n + 2]):
            cp.start()
        token = refs[-1]
        token[...] = jnp.zeros_like(token)

    res = pl.pallas_call(
        body, name=name,
        out_shape=(pltpu.SemaphoreType.DMA((4 * n,)), pltpu.SemaphoreType.DMA((4 * n,)),
                   *[pltpu.HBM(t.shape, t.dtype) for t in shards + outs], jax.ShapeDtypeStruct((8, 128), F32)),
        in_specs=(*[HBM_ONLY] * (2 * n), pl.BlockSpec(memory_space=pl.ANY)),
        out_specs=(SEM_SPEC, SEM_SPEC, *[HBM_ONLY] * (2 * n), pl.BlockSpec(memory_space=pltpu.VMEM)),
        input_output_aliases={i: 2 + i for i in range(2 * n)},
        compiler_params=pltpu.CompilerParams(has_side_effects=SIDE_EFFECT))(
        *[pltpu.with_memory_space_constraint(t, pltpu.HBM) for t in shards + outs], after)
    return (res[0], res[1], list(res[2:2 + n]), list(res[2 + n:2 + 2 * n])), res[-1]


def _gather_wait(in_flight, after, name):
    send_sems, recv_sems, shards, outs = in_flight
    n = len(shards)

    def body(*refs):
        for cp in _gather_copies(refs[:n], refs[n:2 * n], refs[2 * n], refs[2 * n + 1]):
            cp.wait_send()
            cp.wait_recv()

    res = pl.pallas_call(
        body, name=name, out_shape=tuple(pltpu.HBM(t.shape, t.dtype) for t in shards + outs),
        in_specs=(*[HBM_ONLY] * (2 * n), SEM_SPEC, SEM_SPEC, pl.BlockSpec(memory_space=pl.ANY)),
        out_specs=(HBM_ONLY,) * (2 * n), input_output_aliases={i: i for i in range(2 * n)},
        compiler_params=pltpu.CompilerParams(has_side_effects=SIDE_EFFECT))(*shards, *outs, send_sems, recv_sems, after)
    return list(res[:n]), list(res[n:])


class _PassToSibling:
    def __init__(self, shards, gathered):
        n = self.n = len(shards)
        self.ins = list(shards) + list(gathered)
        self.out_shape = tuple(jax.ShapeDtypeStruct(g.shape, g.dtype) for g in gathered)
        self.aliases = {n + a: a for a in range(n)}
        self.sems = [pltpu.SemaphoreType.DMA((3 * n,)), pltpu.SemaphoreType.DMA((3 * n,)),
                     pltpu.SemaphoreType.DMA((n,))]

    def _copies(self, ins, outs, sems):
        send_sems, recv_sems, local_sems = sems
        x, y, c = _mesh_pos()
        chips = [(1 - x, y), (x, 1 - y), (1 - x, 1 - y)]
        mine = [pltpu.make_async_copy(ins[a], outs[a].at[_slot((x, y, c))], local_sems.at[a]) for a in range(self.n)]
        passed, awaited = [], []
        for a in range(self.n):
            for j, chip in enumerate(chips):
                sems_j = dict(send_sem=send_sems.at[3 * a + j], recv_sem=recv_sems.at[3 * a + j],
                              device_id=(x, y, 1 - c), device_id_type=MESH)
                blk = outs[a].at[_slot((*chip, c))]
                passed.append(pltpu.make_async_remote_copy(src_ref=blk, dst_ref=blk, **sems_j))
                got = outs[a].at[_slot((*chip, 1 - c))]
                awaited.append(pltpu.make_async_remote_copy(src_ref=got, dst_ref=got, **sems_j))
        return mine, passed, awaited

    def start(self, ins, outs, sems):
        mine, passed, _ = self._copies(ins, outs, sems)
        for cp in mine + passed:
            cp.start()

    def finish(self, ins, outs, sems):
        mine, passed, awaited = self._copies(ins, outs, sems)
        for cp in passed:
            cp.wait_send()
        for cp in awaited:
            cp.wait_recv()
        for cp in mine:
            cp.wait()


def _reduce_sums(fulls, recv_core, core, tag):
    return [_pair_sum(f, r, core, f"rs_pair_{tag}{i}") for i, (f, r) in enumerate(zip(fulls, recv_core))]


def _local_step(x, tgt, mods, w_in_t, shards, small, chip, core):
    sh1, sc1, g1, sh2, sc2, g2 = mods
    norm1_g, rel_bias, gn_g, gn_b, norm2_g, norm_f_g = small
    tables = _ret_tables()
    buckets = jnp.asarray(_bucket_tables())

    h1 = _norm_mod_fwd(x, norm1_g, sh1, sc1, "norm1_fwd")
    flight_w, token_w = _gather_start(list(shards), h1, "gather_w_start")
    proj, slabs = _proj(h1, w_in_t, token_w)
    gated, ro, states = _ret_fwd(proj, tables, gn_g, gn_b)
    bias = _bias_build(rel_bias, buckets)
    outs, lses = [], []
    for gi in range(len(ATT_GROUPS)):
        o, l = _att_fwd(slabs, bias, gi)
        outs.append(o)
        lses.append(l)
    att, gathered = _mix_fwd(outs, lses, comm=_PassToSibling(*_gather_wait(flight_w, lses[2], "gather_w_wait")))
    w_ret_out, w_att_out, w_o, w_ff1, w_ff2 = (_from_slots(g, ax) for g, ax in zip(gathered, BIG_AXES[1:]))
    ret_out = _mm(gated, w_ret_out, 'nn', tm=S, tn=256, tk=2048, name="ret_out")
    att_out = _mm(att, w_att_out, 'nn', tm=S, tn=512, tk=AW, name="att_out")
    merged = _merge_fwd(proj, ret_out, att_out)
    mixo, x1 = _mm(merged, w_o, 'nn', tm=S, tn=256, tk=D, name="w_o", res=x, gvec=g1)
    h2 = _norm_mod_fwd(x1, norm2_g, sh2, sc2, "norm2_fwd")
    u, act = _mm(h2, w_ff1, 'nn', tm=S, tn=512, tk=D, name="ff1", relu2=True)
    f, x2 = _mm(act, w_ff2, 'nn', tm=1024, tn=512, tk=D_FF, name="ff2", res=x1, gvec=g2)
    loss, dx2, g_normf, df, dg2 = _final_loss(x2, tgt, norm_f_g, f, g2)

    gw_ff2 = _mm(act, df, 'tn', tm=512, tn=D, tk=S, name="gw_ff2", out_dtype=BF16)
    du = _mm(df, w_ff2, 'nt', tm=S, tn=512, tk=D, name="d_act", out_dtype=BF16, relu2_of=u)
    gw_ff1 = _mm(h2, du, 'tn', tm=D, tn=512, tk=S, name="gw_ff1", out_dtype=BF16)
    fulls_a = [_to_slots(g, ax) for g, ax in zip((gw_ff1, gw_ff2), BIG_AXES[4:])]
    dh2, recv_core_a = _mm(du, w_ff1, 'nt', tm=1024, tn=1024, tk=2048, name="dh2", comm=_ExchangeCore(fulls_a))
    parts_a = _reduce_sums(fulls_a, recv_core_a, core, "a")
    flight_a, token_a = _chip_exchange_start(parts_a, "rs_a_start")
    dx1, dsc2, dsh2, g_norm2, dmixo, dg1 = _norm_mod_bwd(x1, norm2_g, sc2, dh2, dx2, "norm2_bwd", gate=(mixo, g1))

    gw_o = _mm(merged, dmixo, 'tn', tm=D, tn=512, tk=S, name="gw_o", out_dtype=BF16, after=token_a)
    dmerged = _mm(dmixo, w_o, 'nt', tm=S, tn=512, tk=D, name="dmerged")
    d_ret_out, d_att_out, dga, dgb = _merge_bwd(proj, ret_out, att_out, dmerged)
    gw_ret_out = _mm(gated, d_ret_out, 'tn', tm=512, tn=D, tk=S, name="gw_ret_out", out_dtype=BF16)
    gw_att_out = _mm(att, d_att_out, 'tn', tm=AW, tn=D, tk=S, name="gw_att_out", out_dtype=BF16)
    fulls_b = [_to_slots(g, ax) for g, ax in zip((gw_ret_out, gw_att_out, gw_o), BIG_AXES[1:4])]
    dgated, recv_core_b = _mm(d_ret_out, w_ret_out, 'nt', tm=S, tn=512, tk=D, name="dgated",
                              comm=_ExchangeCore(fulls_b))
    parts_b = _reduce_sums(fulls_b, recv_core_b, core, "b")
    flight_b, token_b = _chip_exchange_start(parts_b, "rs_b_start")
    datt = _mm(d_att_out, w_att_out, 'nt', tm=S, tn=AW, tk=D, name="datt", after=token_b)
    mix_grads = _mix_bwd(outs, lses, datt)
    datt_parts, ds_sums = [], []
    for gi in range(len(ATT_GROUPS)):
        dq, dk, dv, ds_sum = _att_bwd(slabs, bias, outs[gi], lses[gi], mix_grads[gi], mix_grads[3 + gi], gi)
        datt_parts += [dq.reshape(S, AW), dk.reshape(S, AW), dv.reshape(S, AW)]
        ds_sums.append(ds_sum)
    g_bias = _bias_grad(jnp.concatenate(ds_sums, axis=0), buckets)[:, :, 0].T.reshape(1, -1)
    dret, g_gn_g, g_gn_b = _ret_bwd(proj, tables, gn_g, gn_b, ro, states, dgated)
    parts_a, recv_chip_a = _chip_exchange_wait(flight_a, dret, "rs_a_wait")
    parts_b, recv_chip_b = _chip_exchange_wait(flight_b, dret, "rs_b_wait")
    red_a = [_chip_sum(p, r, chip, f"rs_sum_a{i}") for i, (p, r) in enumerate(zip(parts_a, recv_chip_a))]
    red_b = [_chip_sum(p, r, chip, f"rs_sum_b{i}") for i, (p, r) in enumerate(zip(parts_b, recv_chip_b))]
    dproj = jnp.concatenate([dret] + datt_parts + [dga, dgb], axis=1)
    in_flight, token = [], None
    for half in range(2):
        h1_half = h1[:, half * (D // 2):(half + 1) * (D // 2)]
        gw_half = _mm(dproj, h1_half, 'tn', tm=512, tn=D // 2, tk=S, name=f"gw_in{half}", out_dtype=BF16, after=token)
        full_in = [_to_slots(gw_half, 0)]
        recv_core_in = _run_comm(_ExchangeCore(full_in), f"rs_core_in{half}")
        part_in = _reduce_sums(full_in, recv_core_in, core, f"c{half}")
        flight, token = _chip_exchange_start(part_in, f"rs_in{half}_start")
        in_flight.append(flight)
    dh1 = _mm(dproj, w_in_t, 'nn', tm=1024, tn=1024, tk=2560, name="dh1", after=token)
    gx, dsc1, dsh1, g_norm1 = _norm_mod_bwd(x, norm1_g, sc1, dh1, dx1, "norm1_bwd")

    dmod = [dsh1, dsc1, dg1, dsh2, dsc2, dg2]
    small_g = [g_norm1, g_bias, g_gn_g, g_gn_b, g_norm2, g_normf]
    return loss, gx, in_flight, red_b + red_a, small_g, dmod


def _to_slots(g, axis):
    if axis == 0:
        return g.reshape(4, 2, g.shape[0] // N_DEV, g.shape[1])
    return g.reshape(g.shape[0], N_DEV, g.shape[1] // N_DEV).transpose(1, 0, 2).reshape(4, 2, g.shape[0], -1)


def _from_slots(w8, axis):
    if axis == 0:
        return w8.reshape(-1, w8.shape[2])
    return w8.transpose(1, 0, 2).reshape(w8.shape[1], -1)


BIG_AXES = (1, 0, 1, 0, 1, 0)


def kernel(x, c, w_ada, b_ada, norm1_g, w_in, rel_bias, ret_gn_g, ret_gn_b, w_ret_out, w_att_out, w_o, norm2_g, w_ff1, w_ff2, norm_f_g, loss_target, m_w_ada, m_b_ada, m_norm1_g, m_w_in, m_rel_bias, m_ret_gn_g, m_ret_gn_b, m_w_ret_out, m_w_att_out, m_w_o, m_norm2_g, m_w_ff1, m_w_ff2, m_norm_f_g, v_w_ada, v_b_ada, v_norm1_g, v_w_in, v_rel_bias, v_ret_gn_g, v_ret_gn_b, v_w_ret_out, v_w_att_out, v_w_o, v_norm2_g, v_w_ff1, v_w_ff2, v_norm_f_g):
    mx, my, mc = _mesh_pos()
    dev = 4 * mx + 2 * my + mc
    chip = jnp.reshape(2 * mx + my, (1,)).astype(jnp.int32)
    core = jnp.reshape(mc, (1,)).astype(jnp.int32)
    ada_w = D * 6 // N_DEV

    w_in, m_w_in, v_w_in = (jnp.transpose(t, (0, 2, 1)) for t in (w_in, m_w_in, v_w_in))

    shards = [w[0].astype(BF16) for w in (w_in, w_ret_out, w_att_out, w_o, w_ff1, w_ff2)]
    c_all, w_in8 = _run_comm(_Gather([c, shards[0]], relay=True), "gather_c_w_in")
    c_all = c_all.reshape(N_DEV, D)
    b_sl = lax.dynamic_slice(b_ada, (0, dev * ada_w), (1, ada_w))
    (mod_all,) = _run_comm(_Gather([_ada_fwd(c_all, w_ada[0], b_sl)]), "gather_mod")
    mod = lax.dynamic_index_in_dim(mod_all, dev, axis=1, keepdims=False).reshape(6, D)
    mods = tuple(mod[i:i + 1] for i in range(6))

    small = (norm1_g, rel_bias, ret_gn_g, ret_gn_b, norm2_g, norm_f_g.reshape(1, D))
    loss, gx, in_flight, big_red, small_g, dmod = _local_step(x[0], loss_target[0], mods, w_in8.reshape(IN_COLS, D),
                                                              shards[1:], small, chip, core)

    gathered = _run_comm(_Gather(dmod + small_g + [loss]), "gather_small")
    g_b_ada, dmod_all, (g_norm1, g_bias, g_gn_g, g_gn_b, g_norm2, g_normf, loss_sum) = _sum_small(gathered)
    loss_out = loss_sum[0, 0]
    g_w_ada = _ada_bwd(c_all, lax.dynamic_slice(dmod_all, (0, dev * ada_w), (N_DEV, ada_w)))

    names = ['w_ada', 'b_ada', 'norm1_g', 'w_in', 'rel_bias', 'ret_gn_g', 'ret_gn_b', 'w_ret_out', 'w_att_out',
             'w_o', 'norm2_g', 'w_ff1', 'w_ff2', 'norm_f_g']
    ws = dict(zip(names, (w_ada, b_ada, norm1_g, w_in, rel_bias, ret_gn_g, ret_gn_b, w_ret_out, w_att_out, w_o,
                          norm2_g, w_ff1, w_ff2, norm_f_g)))
    ms = dict(zip(names, (m_w_ada, m_b_ada, m_norm1_g, m_w_in, m_rel_bias, m_ret_gn_g, m_ret_gn_b, m_w_ret_out,
                          m_w_att_out, m_w_o, m_norm2_g, m_w_ff1, m_w_ff2, m_norm_f_g)))
    vs = dict(zip(names, (v_w_ada, v_b_ada, v_norm1_g, v_w_in, v_rel_bias, v_ret_gn_g, v_ret_gn_b, v_w_ret_out,
                          v_w_att_out, v_w_o, v_norm2_g, v_w_ff1, v_w_ff2, v_norm_f_g)))
    grads = dict(w_ada=g_w_ada, w_ret_out=big_red[0], w_att_out=big_red[1], w_o=big_red[2],
                 w_ff1=big_red[3], w_ff2=big_red[4], b_ada=g_b_ada, norm1_g=g_norm1, rel_bias=g_bias,
                 ret_gn_g=g_gn_g, ret_gn_b=g_gn_b, norm2_g=g_norm2, norm_f_g=g_normf)
    delta, new_m, new_v = {}, {}, {}
    for n in ('w_ada', 'w_ret_out', 'w_att_out', 'w_o', 'w_ff1', 'w_ff2'):
        delta[n], new_m[n], new_v[n] = _adamw(ws[n], grads[n], ms[n], vs[n], "adamw_" + n)
        grads[n] = grads[n].reshape(ws[n].shape)
    small_names = ('b_ada', 'norm1_g', 'rel_bias', 'ret_gn_g', 'ret_gn_b', 'norm2_g', 'norm_f_g')
    two_d = {n: (1, ws[n].size) if ws[n].ndim == 1 else ws[n].shape for n in small_names}
    d_, m_, v_ = _adamw_small(*[[src[n].reshape(two_d[n]) for n in small_names] for src in (ws, grads, ms, vs)])
    for i, n in enumerate(small_names):
        shp = ws[n].shape
        delta[n], new_m[n], new_v[n] = d_[i].reshape(shp), m_[i].reshape(shp), v_[i].reshape(shp)
        grads[n] = grads[n].reshape(shp)

    done = lax.optimization_barrier((gx, tuple(d_), tuple(delta[n] for n in ('w_ada', 'w_ret_out', 'w_att_out', 'w_o',
                                                                               'w_ff1', 'w_ff2'))))
    parts_in, recvs_in = [], []
    for half, flight in enumerate(in_flight):
        (part_in,), (recv_chip_in,) = _chip_exchange_wait(flight, done[0], f"rs_in{half}_wait")
        parts_in.append(part_in)
        recvs_in.append(recv_chip_in)
    grads['w_in'], delta['w_in'], new_m['w_in'], new_v['w_in'] = _adamw_reduced(w_in, m_w_in, v_w_in, parts_in,
                                                                               recvs_in, chip)
    for d in (grads, delta, new_m, new_v):
        d['w_in'] = jnp.transpose(d['w_in'], (0, 2, 1))
    return (loss_out, gx[None], *[grads[n] for n in names], *[delta[n] for n in names],
            *[new_m[n] for n in names], *[new_v[n] for n in names])
```

```python
import functools
import math

import numpy as np
import jax
import jax.numpy as jnp
from jax import lax
from jax.experimental import pallas as pl
from jax.experimental.pallas import tpu as pltpu

F32 = jnp.float32
BF16 = jnp.bfloat16
MESH = pl.DeviceIdType.MESH

N_DEV = 8
S = 2048
D = 1024
RET_HEADS = 4
RET_DK = 256
RET_DV = 512
CHUNK = 128
N_CHUNK = S // CHUNK
ATT_GROUPS = ((128, 1), (512, 4), (2048, 16))
ATT_HG = 4
ATT_DH = 128
ATT_BLK = 128
N_BUCKETS = 32
MAX_DIST = 2048
D_FF = 4096
IN_COLS = 12800
OFF_RQ, OFF_RK, OFF_RV, OFF_RG, OFF_ATT = 0, 1024, 2048, 4096, 6144
OFF_GA, OFF_GB = 6144, 7168
RMS_EPS = 1e-6
GN_EPS = 1e-5
ADAM_LR, ADAM_B1, ADAM_B2, ADAM_EPS, ADAM_WD, ADAM_STEP = 0.001, 0.9, 0.999, 1e-08, 0.01, 10
VMEM_LIMIT = 48 * 1024 * 1024


def _pcall(body, **kw):
    return pl.pallas_call(body, **kw)


def _params(sem=None):
    return pltpu.CompilerParams(dimension_semantics=sem, vmem_limit_bytes=VMEM_LIMIT)


HBM_SPEC = pl.BlockSpec(memory_space=pl.ANY)


def _carry(body, comm, *, name, grid, in_specs, out_specs, out_shape, scratch_shapes=()):
    single = not isinstance(out_specs, (tuple, list))
    o_specs = (out_specs,) if single else tuple(out_specs)
    o_shape = (out_shape,) if single else tuple(out_shape)
    n_in, n_out, n_scr = len(in_specs), len(o_specs), len(scratch_shapes)
    nci, nco = len(comm.ins), len(comm.out_shape)
    total = int(np.prod(grid))

    def wrapped(*refs):
        bounds = np.cumsum([0, n_in, nci, n_out, nco, n_scr])
        a, ci, o, co, scr = (refs[bounds[i]:bounds[i + 1]] for i in range(5))
        sems = refs[bounds[5]:]
        flat = 0
        for d, g in enumerate(grid):
            flat = flat * g + pl.program_id(d)

        @pl.when(flat == 0)
        def _():
            comm.start(ci, co, sems)

        body(*a, *o, *scr)

        @pl.when(flat == total - 1)
        def _():
            comm.finish(ci, co, sems)

    aliases = {n_in + i: n_out + o for i, o in getattr(comm, "aliases", {}).items()}
    call = _pcall(wrapped, name=name, grid=grid, in_specs=list(in_specs) + [HBM_SPEC] * nci,
                  out_specs=o_specs + (HBM_SPEC,) * nco, out_shape=o_shape + tuple(comm.out_shape),
                  scratch_shapes=list(scratch_shapes) + list(comm.sems), input_output_aliases=aliases,
                  compiler_params=_params(("arbitrary",) * len(grid)))

    def run(*args):
        res = call(*args, *comm.ins)
        own = res[0] if single else tuple(res[:n_out])
        return own, tuple(res[n_out:])

    return run


def _run_comm(comm, name):
    nci, nco = len(comm.ins), len(comm.out_shape)

    def body(*refs):
        ci, co, sems = refs[:nci], refs[nci:nci + nco], refs[nci + nco:]
        comm.start(ci, co, sems)
        comm.finish(ci, co, sems)

    return _pcall(body, name=name, in_specs=[HBM_SPEC] * nci, out_specs=(HBM_SPEC,) * nco,
                  out_shape=tuple(comm.out_shape), scratch_shapes=list(comm.sems))(*comm.ins)


def _dot(a, b, dn):
    return lax.dot_general(a.astype(BF16), b.astype(BF16), (dn, ((), ())), preferred_element_type=F32)


NN = ((1,), (0,))
NT = ((1,), (1,))
TN = ((0,), (0,))


def _mm(a, b, mode, *, tm, tn, tk, name, out_dtype=F32, res=None, gvec=None, relu2=False, relu2_of=None, comm=None,
        after=None):
    if mode == 'nn':
        (M, K), (_, N) = a.shape, b.shape
        a_spec = pl.BlockSpec((tm, tk), lambda i, j, k: (i, k))
        b_spec = pl.BlockSpec((tk, tn), lambda i, j, k: (k, j))
        dn = NN
    elif mode == 'nt':
        (M, K), (N, _) = a.shape, b.shape
        a_spec = pl.BlockSpec((tm, tk), lambda i, j, k: (i, k))
        b_spec = pl.BlockSpec((tn, tk), lambda i, j, k: (j, k))
        dn = NT
    else:
        (K, M), (_, N) = a.shape, b.shape
        a_spec = pl.BlockSpec((tk, tm), lambda i, j, k: (k, i))
        b_spec = pl.BlockSpec((tk, tn), lambda i, j, k: (k, j))
        dn = TN
    assert M % tm == 0 and N % tn == 0 and K % tk == 0, (name, M, N, K)
    nk = K // tk
    fused = res is not None
    o_spec = pl.BlockSpec((tm, tn), lambda i, j, k: (i, j))

    def body(a_ref, b_ref, *rest):
        acc_ref = rest[-1] if nk > 1 else None
        if after is not None:
            rest = rest[1:]
        if fused:
            res_ref, g_ref, o_ref, x_ref = rest[:4]
        elif relu2_of is not None:
            u_ref, o_ref = rest[:2]
        elif relu2:
            o_ref, act_ref = rest[:2]
        else:
            o_ref = rest[0]

        def finish(acc):
            if relu2_of is not None:
                acc = acc * (2.0 * jnp.maximum(u_ref[...], 0.0))
            o_ref[...] = acc.astype(o_ref.dtype)
            if fused:
                x_ref[...] = res_ref[...] + g_ref[...] * acc
            if relu2:
                r = jnp.maximum(acc, 0.0)
                act_ref[...] = (r * r).astype(BF16)

        p = _dot(a_ref[...], b_ref[...], dn)
        if nk == 1:
            finish(p)
        else:
            k = pl.program_id(2)

            @pl.when(k == 0)
            def _():
                acc_ref[...] = p

            @pl.when(k > 0)
            def _():
                acc_ref[...] += p

            @pl.when(k == nk - 1)
            def _():
                finish(acc_ref[...])

    in_specs = [a_spec, b_spec]
    args = [a, b]
    if after is not None:
        in_specs.append(pl.BlockSpec(memory_space=pl.ANY))
        args.append(after)
    out_shape = jax.ShapeDtypeStruct((M, N), out_dtype)
    out_specs = o_spec
    if fused:
        in_specs += [pl.BlockSpec((tm, tn), lambda i, j, k: (i, j)), pl.BlockSpec((1, tn), lambda i, j, k: (0, j))]
        args += [res, gvec]
        out_shape = (out_shape, jax.ShapeDtypeStruct((M, N), F32))
        out_specs = (o_spec, pl.BlockSpec((tm, tn), lambda i, j, k: (i, j)))
    elif relu2_of is not None:
        in_specs.append(pl.BlockSpec((tm, tn), lambda i, j, k: (i, j)))
        args.append(relu2_of)
    elif relu2:
        out_shape = (out_shape, jax.ShapeDtypeStruct((M, N), BF16))
        out_specs = (o_spec, pl.BlockSpec((tm, tn), lambda i, j, k: (i, j)))
    kw = dict(name=name, grid=(M // tm, N // tn, nk), in_specs=in_specs, out_specs=out_specs,
              out_shape=out_shape, scratch_shapes=[pltpu.VMEM((tm, tn), F32)] if nk > 1 else [])
    if comm is not None:
        return _carry(body, comm, **kw)(*args)
    return _pcall(body, compiler_params=_params(("parallel", "parallel", "arbitrary")), **kw)(*args)


PROJ_TN = 512
ATT_T0, ATT_T1 = 6144 // PROJ_TN, 10752 // PROJ_TN
N_SLABS = (ATT_T1 - ATT_T0) * 4
MAIN_COLS = IN_COLS - (ATT_T1 - ATT_T0) * PROJ_TN


def _proj(h1, w_in_t, after):
    nj = IN_COLS // PROJ_TN

    def body(a_ref, b_ref, after_ref, main_ref, slab_ref):
        j = pl.program_id(1)
        is_att = (j >= ATT_T0) & (j < ATT_T1)
        chunks = [pl.ds(c * 512, 512) for c in range(S // 512)]

        @pl.when(jnp.logical_not(is_att))
        def _():
            for rows in chunks:
                main_ref[rows, :] = _dot(a_ref[rows, :], b_ref[...], NT)

        @pl.when(is_att)
        def _():
            for rows in chunks:
                p = _dot(a_ref[rows, :], b_ref[...], NT)
                for h in range(4):
                    slab_ref[h, rows, :] = p[:, h * 128:(h + 1) * 128]

    main_idx = lambda j: jnp.where(j < ATT_T0, j, jnp.where(j < ATT_T1, ATT_T0 - 1, j - (ATT_T1 - ATT_T0)))
    slab_idx = lambda j: jnp.clip(j - ATT_T0, 0, ATT_T1 - ATT_T0 - 1)
    return _pcall(
        body, name="proj", grid=(1, nj, 1),
        in_specs=[pl.BlockSpec((S, D), lambda i, j, k: (0, 0)), pl.BlockSpec((PROJ_TN, D), lambda i, j, k: (j, 0)),
                  HBM_SPEC],
        out_specs=(pl.BlockSpec((S, PROJ_TN), lambda i, j, k: (0, main_idx(j))),
                   pl.BlockSpec((4, S, 128), lambda i, j, k: (slab_idx(j), 0, 0))),
        out_shape=(jax.ShapeDtypeStruct((S, MAIN_COLS), F32), jax.ShapeDtypeStruct((N_SLABS, S, 128), F32)),
        compiler_params=_params(("arbitrary",) * 3))(h1, w_in_t, after)


TR = 256


def _row_spec(w=D):
    return pl.BlockSpec((TR, w), lambda i: (i, 0))


def _vec_spec(w=D):
    return pl.BlockSpec((1, w), lambda i: (0, 0))


def _norm_mod_fwd(x, g, sh, sc, name):
    def body(x_ref, g_ref, sh_ref, sc_ref, o_ref):
        xv = x_ref[...]
        rstd = lax.rsqrt(jnp.mean(xv * xv, axis=-1, keepdims=True) + RMS_EPS)
        n = xv * rstd * g_ref[...]
        o_ref[...] = (n * (1.0 + sc_ref[...]) + sh_ref[...]).astype(BF16)

    return _pcall(body, name=name, grid=(S // TR,), in_specs=[_row_spec(), _vec_spec(), _vec_spec(), _vec_spec()],
                  out_specs=_row_spec(), out_shape=jax.ShapeDtypeStruct((S, D), BF16),
                  compiler_params=_params(("parallel",)))(x, g, sh, sc)


def _norm_mod_bwd(x, g, sc, dh, dres, name, gate=None):
    gated = gate is not None

    def body(x_ref, g_ref, sc_ref, dh_ref, dres_ref, *rest):
        if gated:
            f_ref, gv_ref, dx_ref, dsc_ref, dsh_ref, dg_ref, dz_ref, dgv_ref = rest
        else:
            dx_ref, dsc_ref, dsh_ref, dg_ref = rest
        i = pl.program_id(0)
        xv = x_ref[...]
        dh = dh_ref[...]
        rstd = lax.rsqrt(jnp.mean(xv * xv, axis=-1, keepdims=True) + RMS_EPS)
        xhat = xv * rstd
        gv = g_ref[...]
        dn = dh * (1.0 + sc_ref[...])
        dxhat = dn * gv
        dx = dres_ref[...] + rstd * (dxhat - xhat * jnp.mean(dxhat * xhat, axis=-1, keepdims=True))
        dx_ref[...] = dx
        sums = [(dsc_ref, jnp.sum(dh * (xhat * gv), axis=0, keepdims=True)),
                (dsh_ref, jnp.sum(dh, axis=0, keepdims=True)),
                (dg_ref, jnp.sum(dn * xhat, axis=0, keepdims=True))]
        if gated:
            dz_ref[...] = (dx * gv_ref[...]).astype(BF16)
            sums.append((dgv_ref, jnp.sum(dx * f_ref[...], axis=0, keepdims=True)))

        @pl.when(i == 0)
        def _():
            for ref, p in sums:
                ref[...] = p

        @pl.when(i > 0)
        def _():
            for ref, p in sums:
                ref[...] += p

    vec = jax.ShapeDtypeStruct((1, D), F32)
    in_specs = [_row_spec(), _vec_spec(), _vec_spec(), _row_spec(), _row_spec()]
    out_specs = [_row_spec(), _vec_spec(), _vec_spec(), _vec_spec()]
    out_shape = [jax.ShapeDtypeStruct((S, D), F32), vec, vec, vec]
    args = [x, g, sc, dh, dres]
    if gated:
        in_specs += [_row_spec(), _vec_spec()]
        out_specs += [_row_spec(), _vec_spec()]
        out_shape += [jax.ShapeDtypeStruct((S, D), BF16), vec]
        args += list(gate)
    return _pcall(body, name=name, grid=(S // TR,), in_specs=in_specs, out_specs=tuple(out_specs),
                  out_shape=tuple(out_shape), compiler_params=_params(("arbitrary",)))(*args)


def _final_loss(x2, tgt, g, f, g2):
    def body(x_ref, t_ref, g_ref, f_ref, g2_ref, loss_ref, dx_ref, dg_ref, df_ref, dg2_ref):
        i = pl.program_id(0)
        xv = x_ref[...]
        gv = g_ref[...]
        rstd = lax.rsqrt(jnp.mean(xv * xv, axis=-1, keepdims=True) + RMS_EPS)
        xhat = xv * rstd
        err = xhat * gv - t_ref[...]
        dy = err * (1.0 / D)
        dxhat = dy * gv
        dx = rstd * (dxhat - xhat * jnp.mean(dxhat * xhat, axis=-1, keepdims=True))
        dx_ref[...] = dx
        df_ref[...] = (dx * g2_ref[...]).astype(BF16)
        p_g = jnp.sum(dy * xhat, axis=0, keepdims=True)
        p_g2 = jnp.sum(dx * f_ref[...], axis=0, keepdims=True)
        p_l = jnp.zeros((1, 128), F32) + 0.5 * jnp.sum(jnp.mean(err * err, axis=-1, keepdims=True))

        @pl.when(i == 0)
        def _():
            dg_ref[...] = p_g
            dg2_ref[...] = p_g2
            loss_ref[...] = p_l

        @pl.when(i > 0)
        def _():
            dg_ref[...] += p_g
            dg2_ref[...] += p_g2
            loss_ref[...] += p_l

    vec = jax.ShapeDtypeStruct((1, D), F32)
    return _pcall(body, name="final_loss", grid=(S // TR,),
                  in_specs=[_row_spec(), _row_spec(), _vec_spec(), _row_spec(), _vec_spec()],
                  out_specs=(_vec_spec(128), _row_spec(), _vec_spec(), _row_spec(), _vec_spec()),
                  out_shape=(jax.ShapeDtypeStruct((1, 128), F32), jax.ShapeDtypeStruct((S, D), F32), vec,
                             jax.ShapeDtypeStruct((S, D), BF16), vec),
                  compiler_params=_params(("arbitrary",)))(x2, tgt, g, f, g2)


HALF = 512


def _merge_fwd(proj, ret_out, att_out):
    def body(ga_ref, gb_ref, r_ref, a_ref, o_ref):
        o_ref[...] = (jax.nn.sigmoid(ga_ref[...]) * r_ref[...] + jax.nn.sigmoid(gb_ref[...]) * a_ref[...]).astype(BF16)

    blk = lambda off: pl.BlockSpec((TR, HALF), lambda i, j: (i, off // HALF + j))
    return _pcall(body, name="merge_fwd", grid=(S // TR, D // HALF),
                  in_specs=[blk(OFF_GA), blk(OFF_GB), blk(0), blk(0)], out_specs=blk(0),
                  out_shape=jax.ShapeDtypeStruct((S, D), BF16),
                  compiler_params=_params(("parallel", "parallel")))(proj, proj, ret_out, att_out)


def _merge_bwd(proj, ret_out, att_out, dmerged):
    def body(ga_ref, gb_ref, r_ref, a_ref, dm_ref, dr_ref, da_ref, dga_ref, dgb_ref):
        sa = jax.nn.sigmoid(ga_ref[...])
        sb = jax.nn.sigmoid(gb_ref[...])
        dm = dm_ref[...]
        dr_ref[...] = (dm * sa).astype(BF16)
        da_ref[...] = (dm * sb).astype(BF16)
        dga_ref[...] = (dm * r_ref[...] * (sa * (1.0 - sa))).astype(BF16)
        dgb_ref[...] = (dm * a_ref[...] * (sb * (1.0 - sb))).astype(BF16)

    blk = lambda off: pl.BlockSpec((TR, HALF), lambda i, j: (i, off // HALF + j))
    o = jax.ShapeDtypeStruct((S, D), BF16)
    return _pcall(body, name="merge_bwd", grid=(S // TR, D // HALF),
                  in_specs=[blk(OFF_GA), blk(OFF_GB), blk(0), blk(0), blk(0)], out_specs=(blk(0),) * 4,
                  out_shape=(o, o, o, o),
                  compiler_params=_params(("parallel", "parallel")))(proj, proj, ret_out, att_out, dmerged)


def _ret_tables():
    H, C = RET_HEADS, CHUNK
    log_g = jnp.log1p(-(2.0 ** (-5.0 - jnp.arange(H, dtype=F32))))
    idx = jnp.arange(C, dtype=F32)
    rel = idx[:, None] - idx[None, :]
    inner = jnp.where(rel >= 0, jnp.exp(log_g[:, None, None] * jnp.maximum(rel, 0.0)), 0.0)
    qd = jnp.exp(log_g[:, None] * (idx + 1.0))[:, :, None]
    kd = jnp.exp(log_g[:, None] * (C - 1.0 - idx))[:, :, None]
    cd = jnp.broadcast_to(jnp.exp(log_g * C)[:, None, None], (H, 1, 128))
    half = RET_DK // 2
    inv = 10000.0 ** (-jnp.arange(half, dtype=F32) / half)
    ang = jnp.arange(S, dtype=F32)[:, None] * inv[None, :]
    return inner, qd, kd, cd, jnp.cos(ang), jnp.sin(ang)


def _rot(x, cos, sin):
    x1, x2 = x[:, :128], x[:, 128:]
    return jnp.concatenate([x1 * cos - x2 * sin, x1 * sin + x2 * cos], axis=1)


def _rot_t(d, cos, sin):
    d1, d2 = d[:, :128], d[:, 128:]
    return jnp.concatenate([d1 * cos + d2 * sin, d2 * cos - d1 * sin], axis=1)


RET_COLS = OFF_ATT
RET_VW = RET_HEADS * RET_DV


def _ret_specs(chunk_of):
    ci = chunk_of
    whole = lambda shape: pl.BlockSpec(shape, lambda t: (0,) * len(shape))
    return [
        pl.BlockSpec((CHUNK, RET_COLS), lambda t: (ci(t), 0)),
        pl.BlockSpec((CHUNK, 128), lambda t: (ci(t), 0)),
        pl.BlockSpec((CHUNK, 128), lambda t: (ci(t), 0)),
        whole((RET_HEADS, CHUNK, CHUNK)), whole((RET_HEADS, CHUNK, 1)), whole((RET_HEADS, CHUNK, 1)),
        whole((RET_HEADS, 1, 128)), whole((1, RET_VW)), whole((1, RET_VW)),
    ]


def _ret_cols(h):
    q = slice(OFF_RQ + h * RET_DK, OFF_RQ + (h + 1) * RET_DK)
    k = slice(OFF_RK + h * RET_DK, OFF_RK + (h + 1) * RET_DK)
    v = slice(OFF_RV + h * RET_DV, OFF_RV + (h + 1) * RET_DV)
    g = slice(OFF_RG + h * RET_DV, OFF_RG + (h + 1) * RET_DV)
    return q, k, v, g, slice(h * RET_DV, (h + 1) * RET_DV)


def _ret_fwd(proj, tables, gn_g, gn_b, comm=None):
    inner, qd, kd, cd, cos, sin = tables

    def body(x_ref, cos_ref, sin_ref, in_ref, qd_ref, kd_ref, cd_ref, g_ref, b_ref,
             gated_ref, ro_ref, st_ref, s_scr):
        i = pl.program_id(0)

        @pl.when(i == 0)
        def _():
            s_scr[...] = jnp.zeros_like(s_scr)

        cosv, sinv = cos_ref[...], sin_ref[...]
        for h in range(RET_HEADS):
            cq, ck, cv, cg, co = _ret_cols(h)
            q = _rot(x_ref[:, cq], cosv, sinv)
            k = _rot(x_ref[:, ck], cosv, sinv) * (RET_DK ** -0.5)
            v = x_ref[:, cv]
            st = s_scr[h]
            st_ref[h] = st.astype(BF16)
            s = _dot(q, k, NT) * in_ref[h]
            o = _dot(s, v, NN) + _dot(q, st, NN) * qd_ref[h]
            s_scr[h] = st * cd_ref[h, :, :1] + _dot(k * kd_ref[h], v, TN)
            ro_ref[:, co] = o
            mu = jnp.mean(o, axis=-1, keepdims=True)
            oc = o - mu
            var = jnp.mean(oc * oc, axis=-1, keepdims=True)
            rn = oc * lax.rsqrt(var + GN_EPS) * g_ref[:, co] + b_ref[:, co]
            rg = x_ref[:, cg]
            gated_ref[:, co] = (rg * jax.nn.sigmoid(rg) * rn).astype(BF16)

    ospec = pl.BlockSpec((CHUNK, RET_VW), lambda t: (t, 0))
    kw = dict(name="ret_fwd", grid=(N_CHUNK,), in_specs=_ret_specs(lambda t: t),
              out_specs=(ospec, ospec, pl.BlockSpec((RET_HEADS, None, RET_DK, RET_DV), lambda t: (0, t, 0, 0))),
              out_shape=(jax.ShapeDtypeStruct((S, RET_VW), BF16), jax.ShapeDtypeStruct((S, RET_VW), F32),
                         jax.ShapeDtypeStruct((RET_HEADS, N_CHUNK, RET_DK, RET_DV), BF16)),
              scratch_shapes=[pltpu.VMEM((RET_HEADS, RET_DK, RET_DV), F32)])
    args = (proj, cos, sin, inner, qd, kd, cd, gn_g, gn_b)
    if comm is not None:
        return _carry(body, comm, **kw)(*args)
    return _pcall(body, compiler_params=_params(("arbitrary",)), **kw)(*args)


def _ret_bwd(proj, tables, gn_g, gn_b, ro, states, dgated, comm=None):
    inner, qd, kd, cd, cos, sin = tables
    last = N_CHUNK - 1

    def body(x_ref, cos_ref, sin_ref, in_ref, qd_ref, kd_ref, cd_ref, g_ref, b_ref, ro_ref, st_ref, dg_ref,
             dx_ref, gg_ref, gb_ref, gs_scr):
        t = pl.program_id(0)

        @pl.when(t == 0)
        def _():
            gs_scr[...] = jnp.zeros_like(gs_scr)
            gg_ref[...] = jnp.zeros_like(gg_ref)
            gb_ref[...] = jnp.zeros_like(gb_ref)

        cosv, sinv = cos_ref[...], sin_ref[...]
        for h in range(RET_HEADS):
            cq, ck, cv, cg, co = _ret_cols(h)
            q = _rot(x_ref[:, cq], cosv, sinv)
            k = _rot(x_ref[:, ck], cosv, sinv) * (RET_DK ** -0.5)
            v = x_ref[:, cv]
            qdv, kdv, dm = qd_ref[h], kd_ref[h], in_ref[h]
            st = st_ref[h]
            o = ro_ref[:, co]
            gv = g_ref[:, co]
            mu = jnp.mean(o, axis=-1, keepdims=True)
            oc = o - mu
            rstd = lax.rsqrt(jnp.mean(oc * oc, axis=-1, keepdims=True) + GN_EPS)
            ohat = oc * rstd
            rn = ohat * gv + b_ref[:, co]
            rg = x_ref[:, cg]
            sg = jax.nn.sigmoid(rg)
            dgt = dg_ref[:, co]
            drn = dgt * (rg * sg)
            dx_ref[:, cg] = (dgt * rn * (sg * (1.0 + rg * (1.0 - sg)))).astype(BF16)
            gg_ref[:, co] += jnp.sum(drn * ohat, axis=0, keepdims=True)
            gb_ref[:, co] += jnp.sum(drn, axis=0, keepdims=True)
            dohat = drn * gv
            do = rstd * (dohat - jnp.mean(dohat, axis=-1, keepdims=True)
                         - ohat * jnp.mean(dohat * ohat, axis=-1, keepdims=True))
            gs = gs_scr[h]
            s = _dot(q, k, NT) * dm
            dsr = _dot(do, v, NT) * dm
            dq = _dot(dsr, k, NN) + _dot(do, st, NT) * qdv
            dk = _dot(dsr, q, TN) + _dot(v, gs, NT) * kdv
            dv = _dot(s, do, TN) + _dot(k * kdv, gs, NN)
            gs_scr[h] = gs * cd_ref[h, :, :1] + _dot(q * qdv, do, TN)
            dx_ref[:, cq] = _rot_t(dq, cosv, sinv).astype(BF16)
            dx_ref[:, ck] = (_rot_t(dk, cosv, sinv) * (RET_DK ** -0.5)).astype(BF16)
            dx_ref[:, cv] = dv.astype(BF16)

    rev = lambda t: last - t
    vblk = pl.BlockSpec((CHUNK, RET_VW), lambda t: (rev(t), 0))
    vspec = pl.BlockSpec((1, RET_VW), lambda t: (0, 0))
    kw = dict(name="ret_bwd", grid=(N_CHUNK,),
              in_specs=_ret_specs(rev) + [vblk, pl.BlockSpec((RET_HEADS, None, RET_DK, RET_DV),
                                                             lambda t: (0, rev(t), 0, 0)), vblk],
              out_specs=(pl.BlockSpec((CHUNK, RET_COLS), lambda t: (rev(t), 0)), vspec, vspec),
              out_shape=(jax.ShapeDtypeStruct((S, RET_COLS), BF16), jax.ShapeDtypeStruct((1, RET_VW), F32),
                         jax.ShapeDtypeStruct((1, RET_VW), F32)),
              scratch_shapes=[pltpu.VMEM((RET_HEADS, RET_DK, RET_DV), F32)])
    args = (proj, cos, sin, inner, qd, kd, cd, gn_g, gn_b, ro, states, dgated)
    if comm is not None:
        return _carry(body, comm, **kw)(*args)
    return _pcall(body, compiler_params=_params(("arbitrary",)), **kw)(*args)


def _bucket_tables():
    qi = np.arange(ATT_BLK)[:, None]
    kj = np.arange(2 * ATT_BLK)[None, :]
    m = ATT_BLK + qi - kj
    out = []
    for win, dil in ATT_GROUPS:
        w = win // dil
        dist = (np.clip(m, 0, w) * dil).astype(np.int32)
        max_exact = N_BUCKETS // 2
        d_f = np.maximum(dist, 1).astype(np.float32)
        large = max_exact + (np.log(d_f / np.float32(max_exact)) / np.float32(math.log(MAX_DIST / max_exact))
                             * np.float32(N_BUCKETS - max_exact)).astype(np.int32)
        large = np.minimum(large, N_BUCKETS - 1)
        out.append(np.where(dist < max_exact, dist, large).astype(np.int32))
    return np.stack(out)


def _bias_build(rel_bias, buckets):
    def body(tab_ref, bk_ref, o_ref):
        hh = pl.program_id(0)
        bk = bk_ref[...]
        acc = jnp.zeros((ATT_BLK, 2 * ATT_BLK), F32)
        for b in range(N_BUCKETS):
            acc = jnp.where(bk == b, tab_ref[b, hh], acc)
        o_ref[...] = acc

    nh = len(ATT_GROUPS) * ATT_HG
    return _pcall(body, name="bias_build", grid=(nh,),
                  in_specs=[pl.BlockSpec(memory_space=pltpu.SMEM),
                            pl.BlockSpec((None, ATT_BLK, 2 * ATT_BLK), lambda hh: (hh // ATT_HG, 0, 0))],
                  out_specs=pl.BlockSpec((None, ATT_BLK, 2 * ATT_BLK), lambda hh: (hh, 0, 0)),
                  out_shape=jax.ShapeDtypeStruct((nh, ATT_BLK, 2 * ATT_BLK), F32),
                  compiler_params=_params(("parallel",)))(rel_bias, buckets)


def _bias_grad(ds_sum, buckets):
    def body(ds_ref, bk_ref, o_ref):
        bk = bk_ref[...]
        ds = ds_ref[...]
        rows = lax.broadcasted_iota(jnp.int32, (N_BUCKETS, 128), 0)
        acc = jnp.zeros((N_BUCKETS, 128), F32)
        for b in range(N_BUCKETS):
            acc = jnp.where(rows == b, jnp.sum(jnp.where(bk == b, ds, 0.0)), acc)
        o_ref[...] = acc

    nh = len(ATT_GROUPS) * ATT_HG
    return _pcall(body, name="bias_grad", grid=(nh,),
                  in_specs=[pl.BlockSpec((None, ATT_BLK, 2 * ATT_BLK), lambda hh: (hh, 0, 0)),
                            pl.BlockSpec((None, ATT_BLK, 2 * ATT_BLK), lambda hh: (hh // ATT_HG, 0, 0))],
                  out_specs=pl.BlockSpec((None, N_BUCKETS, 128), lambda hh: (hh, 0, 0)),
                  out_shape=jax.ShapeDtypeStruct((nh, N_BUCKETS, 128), F32),
                  compiler_params=_params(("parallel",)))(ds_sum, buckets)


def _att_valid(n):
    qi = lax.broadcasted_iota(jnp.int32, (ATT_BLK, 2 * ATT_BLK), 0)
    kj = lax.broadcasted_iota(jnp.int32, (ATT_BLK, 2 * ATT_BLK), 1)
    m = ATT_BLK + qi - kj
    first_key = jnp.where(n > 0, 0, ATT_BLK)
    return (m >= 0) & (m <= ATT_BLK) & (kj >= first_key)


ATT_HP = (1, 2, 2)


def _att_geometry(gi):
    _, dil = ATT_GROUPS[gi]
    return dil, S // dil // ATT_BLK, ATT_HP[gi]


def _blk(dil, r, n):
    if dil == 1:
        return pl.ds(n * ATT_BLK, ATT_BLK)
    return pl.ds(r + n * ATT_BLK * dil, ATT_BLK, stride=dil)


def _slab_specs(gi):
    _, _, hp = _att_geometry(gi)
    per = ATT_HG // hp
    return [pl.BlockSpec((hp, S, ATT_DH), lambda g, r, part=part: ((3 * gi + part) * per + g, 0, 0))
            for part in range(3)]


def _head_specs(gi, count):
    _, _, hp = _att_geometry(gi)
    return [pl.BlockSpec((hp, S, ATT_DH), lambda g, r: (g, 0, 0))] * count


def _bias_spec(gi):
    _, _, hp = _att_geometry(gi)
    return pl.BlockSpec((hp, ATT_BLK, 2 * ATT_BLK), lambda g, r: (gi * (ATT_HG // hp) + g, 0, 0))


def _att_valid_first():
    qi = lax.broadcasted_iota(jnp.int32, (ATT_BLK, ATT_BLK), 0)
    kj = lax.broadcasted_iota(jnp.int32, (ATT_BLK, ATT_BLK), 1)
    return kj <= qi


def _att_fwd(slabs, bias, gi, comm=None):
    dil, nb, hp = _att_geometry(gi)
    scale = ATT_DH ** -0.5

    def body(q_ref, k_ref, v_ref, bias_ref, o_ref, l_ref):
        r = pl.program_id(1)
        for n in range(nb):
            cur = _blk(dil, r, n)
            valid = _att_valid(n) if n > 0 else _att_valid_first()
            for h in range(hp):
                if n > 0:
                    prev = _blk(dil, r, n - 1)
                    kk = jnp.concatenate([k_ref[h, prev, :], k_ref[h, cur, :]], axis=0)
                    vv = jnp.concatenate([v_ref[h, prev, :], v_ref[h, cur, :]], axis=0)
                    bias = bias_ref[h]
                else:
                    kk, vv, bias = k_ref[h, cur, :], v_ref[h, cur, :], bias_ref[h, :, pl.ds(ATT_BLK, ATT_BLK)]
                s = _dot(q_ref[h, cur, :], kk, NT) * scale + bias
                s = jnp.where(valid, s, -1e30)
                mx = jnp.max(s, axis=-1, keepdims=True)
                e = jnp.exp(s - mx)
                den = jnp.sum(e, axis=-1, keepdims=True)
                o_ref[h, cur, :] = _dot(e / den, vv, NN)
                l_ref[h, cur, :] = jnp.broadcast_to(mx + jnp.log(den), (ATT_BLK, ATT_DH))

    osh = jax.ShapeDtypeStruct((ATT_HG, S, ATT_DH), F32)
    kw = dict(name=f"att_fwd{gi}", grid=(ATT_HG // hp, dil), in_specs=_slab_specs(gi) + [_bias_spec(gi)],
              out_specs=tuple(_head_specs(gi, 2)), out_shape=(osh, osh))
    if comm is not None:
        return _carry(body, comm, **kw)(slabs, slabs, slabs, bias)
    return _pcall(body, compiler_params=_params(("parallel", "arbitrary")), **kw)(slabs, slabs, slabs, bias)


def _att_bwd(slabs, bias, o, lse, do, dlse, gi, comm=None):
    dil, nb, hp = _att_geometry(gi)
    per = ATT_HG // hp
    scale = ATT_DH ** -0.5
    wh = hp * ATT_DH
    wide = lambda t: jnp.concatenate([t, t], axis=1)

    def body(q_ref, k_ref, v_ref, bias_ref, o_ref, l_ref, do_ref, dl_ref, dq_ref, dk_ref, dv_ref, ds_ref):
        r = pl.program_id(1)

        @pl.when(r == 0)
        def _():
            ds_ref[...] = jnp.zeros_like(ds_ref)

        for h in range(hp):
            sl = slice(h * ATT_DH, (h + 1) * ATT_DH)
            carry_k = carry_v = None
            for n in range(nb):
                cur = _blk(dil, r, n)
                q = q_ref[h, cur, :]
                dov = do_ref[h, cur, :]
                delta = jnp.sum(dov * o_ref[h, cur, :], axis=-1, keepdims=True)
                out_rows = pl.ds(n * ATT_BLK, ATT_BLK)
                if n == 0:
                    own = pl.ds(ATT_BLK, ATT_BLK)
                    kk, vv = k_ref[h, cur, :], v_ref[h, cur, :]
                    s = _dot(q, kk, NT) * scale + bias_ref[h, :, own]
                    p = jnp.where(_att_valid_first(), jnp.exp(s - l_ref[h, cur, :]), 0.0)
                    ds = p * (_dot(dov, vv, NT) - delta + dl_ref[h, cur, :])
                    ds_ref[h, :, own] += ds
                    dq_ref[out_rows, sl] = (_dot(ds, kk, NN) * scale).astype(BF16)
                    carry_k, carry_v = _dot(ds, q, TN) * scale, _dot(p, dov, TN)
                    continue
                prev = _blk(dil, r, n - 1)
                kk = jnp.concatenate([k_ref[h, prev, :], k_ref[h, cur, :]], axis=0)
                vv = jnp.concatenate([v_ref[h, prev, :], v_ref[h, cur, :]], axis=0)
                s = _dot(q, kk, NT) * scale + bias_ref[h]
                p = jnp.where(_att_valid(n), jnp.exp(s - wide(l_ref[h, cur, :])), 0.0)
                dp = _dot(dov, vv, NT)
                ds = p * (dp - delta + wide(dl_ref[h, cur, :]))
                ds_ref[h] += ds
                dq_ref[out_rows, sl] = (_dot(ds, kk, NN) * scale).astype(BF16)
                dkk = _dot(ds, q, TN) * scale
                dvv = _dot(p, dov, TN)
                before = pl.ds((n - 1) * ATT_BLK, ATT_BLK)
                dk_ref[before, sl] = (carry_k + dkk[:ATT_BLK]).astype(BF16)
                dv_ref[before, sl] = (carry_v + dvv[:ATT_BLK]).astype(BF16)
                carry_k, carry_v = dkk[ATT_BLK:], dvv[ATT_BLK:]
            last = pl.ds((nb - 1) * ATT_BLK, ATT_BLK)
            dk_ref[last, sl] = carry_k.astype(BF16)
            dv_ref[last, sl] = carry_v.astype(BF16)

    out_spec = pl.BlockSpec((S // dil, wh), lambda g, r: (0, r * per + g))
    osh = jax.ShapeDtypeStruct((S // dil, dil * AW), BF16)
    kw = dict(name=f"att_bwd{gi}", grid=(per, dil), in_specs=_slab_specs(gi) + [_bias_spec(gi)] + _head_specs(gi, 4),
              out_specs=(out_spec, out_spec, out_spec,
                         pl.BlockSpec((hp, ATT_BLK, 2 * ATT_BLK), lambda g, r: (g, 0, 0))),
              out_shape=(osh, osh, osh, jax.ShapeDtypeStruct((ATT_HG, ATT_BLK, 2 * ATT_BLK), F32)))
    args = (slabs, slabs, slabs, bias, o, lse, do, dlse)
    if comm is not None:
        return _carry(body, comm, **kw)(*args)
    return _pcall(body, compiler_params=_params(("arbitrary", "arbitrary")), **kw)(*args)


AW = ATT_HG * ATT_DH


def _mix_weights(l0, l1, l2):
    mx = jnp.maximum(jnp.maximum(l0, l1), l2)
    e0, e1, e2 = jnp.exp(l0 - mx), jnp.exp(l1 - mx), jnp.exp(l2 - mx)
    den = e0 + e1 + e2
    return e0 / den, e1 / den, e2 / den


def _heads_spec():
    return pl.BlockSpec((ATT_HG, TR, ATT_DH), lambda i: (0, i, 0))


def _mix_fwd(os_, ls, comm=None):
    def body(o0, o1, o2, l0, l1, l2, att_ref):
        for h in range(ATT_HG):
            w0, w1, w2 = _mix_weights(l0[h], l1[h], l2[h])
            att_ref[:, h * ATT_DH:(h + 1) * ATT_DH] = (w0 * o0[h] + w1 * o1[h] + w2 * o2[h]).astype(BF16)

    kw = dict(name="mix_fwd", grid=(S // TR,), in_specs=[_heads_spec()] * 6, out_specs=_row_spec(AW),
              out_shape=jax.ShapeDtypeStruct((S, AW), BF16))
    if comm is not None:
        return _carry(body, comm, **kw)(*os_, *ls)
    return _pcall(body, compiler_params=_params(("parallel",)), **kw)(*os_, *ls)


def _mix_bwd(os_, ls, datt):
    def body(o0, o1, o2, l0, l1, l2, da_ref, d0, d1, d2, e0, e1, e2):
        for h in range(ATT_HG):
            ws = _mix_weights(l0[h], l1[h], l2[h])
            da = da_ref[:, h * ATT_DH:(h + 1) * ATT_DH]
            dws = []
            for o_ref, w, d_ref in zip((o0, o1, o2), ws, (d0, d1, d2)):
                d_ref[h] = w * da
                dws.append(jnp.broadcast_to(jnp.sum(da * o_ref[h], axis=-1, keepdims=True), (TR, ATT_DH)))
            tot = ws[0] * dws[0] + ws[1] * dws[1] + ws[2] * dws[2]
            for w, dw, e_ref in zip(ws, dws, (e0, e1, e2)):
                e_ref[h] = w * (dw - tot)

    o = jax.ShapeDtypeStruct((ATT_HG, S, ATT_DH), F32)
    return _pcall(body, name="mix_bwd", grid=(S // TR,), in_specs=[_heads_spec()] * 6 + [_row_spec(AW)],
                  out_specs=(_heads_spec(),) * 6, out_shape=(o,) * 6,
                  compiler_params=_params(("parallel",)))(*os_, *ls, datt)


def _ada_fwd(c_all, w_sh, b_sl):
    def body(c_ref, w_ref, b_ref, o_ref):
        cv = c_ref[...]
        o_ref[...] = _dot(cv * jax.nn.sigmoid(cv), w_ref[...], NN) + b_ref[...]

    return _pcall(body, name="ada_fwd", out_shape=jax.ShapeDtypeStruct((N_DEV, w_sh.shape[1]), F32),
                  compiler_params=_params())(c_all, w_sh, b_sl)


def _ada_bwd(c_all, dm_sl):
    def body(c_ref, d_ref, o_ref):
        cv = c_ref[...]
        o_ref[...] = _dot(cv * jax.nn.sigmoid(cv), d_ref[...], TN)

    return _pcall(body, name="ada_bwd", out_shape=jax.ShapeDtypeStruct((D, dm_sl.shape[1]), F32),
                  compiler_params=_params())(c_all, dm_sl)


N_MOD = 6


def _sum_small(gathered):
    n = len(gathered)

    def body(*refs):
        ins, (gb_ref, dm_ref), outs = refs[:n], refs[n:n + 2], refs[n + 2:]

        def total(r):
            acc = r[0]
            for e in range(1, N_DEV):
                acc = acc + r[e]
            return acc

        for i in range(N_MOD):
            cols = slice(i * D, (i + 1) * D)
            gb_ref[:, cols] = total(ins[i])
            for e in range(N_DEV):
                dm_ref[e:e + 1, cols] = ins[i][e]
        for r, o_ref in zip(ins[N_MOD:], outs):
            o_ref[...] = total(r)

    shapes = (jax.ShapeDtypeStruct((1, N_MOD * D), F32), jax.ShapeDtypeStruct((N_DEV, N_MOD * D), F32),
              *[jax.ShapeDtypeStruct(g.shape[1:], F32) for g in gathered[N_MOD:]])
    res = _pcall(body, name="sum_small", out_shape=shapes, compiler_params=_params())(*gathered)
    return res[0], res[1], res[2:]


def _row_tile(m, n):
    t = max(8, min(m, (1 << 19) // n // 8 * 8))
    while m % t:
        t -= 8
    return t


def _pair_sum(full, recv, sel, name):
    _, _, m, n = full.shape
    t = _row_tile(m, n)

    def body(sel_ref, a_ref, b_ref, o_ref):
        o_ref[...] = (a_ref[...].astype(F32) + b_ref[...].astype(F32)).astype(o_ref.dtype)

    gs = pltpu.PrefetchScalarGridSpec(
        num_scalar_prefetch=1, grid=(4, m // t),
        in_specs=[pl.BlockSpec((None, None, t, n), lambda q, i, s: (q, s[0], i, 0)),
                  pl.BlockSpec((None, t, n), lambda q, i, s: (q, i, 0))],
        out_specs=pl.BlockSpec((None, t, n), lambda q, i, s: (q, i, 0)))
    return _pcall(body, name=name, grid_spec=gs, out_shape=jax.ShapeDtypeStruct((4, m, n), full.dtype),
                  compiler_params=_params(("parallel", "parallel")))(sel, full, recv)


def _chip_sum(part, recv, sel, name):
    _, m, n = part.shape
    t = _row_tile(m, n)

    def body(sel_ref, a_ref, r_ref, o_ref):
        o_ref[...] = ((a_ref[...].astype(F32) + r_ref[0].astype(F32)) + r_ref[1].astype(F32)) + r_ref[2].astype(F32)

    gs = pltpu.PrefetchScalarGridSpec(
        num_scalar_prefetch=1, grid=(m // t,),
        in_specs=[pl.BlockSpec((None, t, n), lambda i, s: (s[0], i, 0)),
                  pl.BlockSpec((3, t, n), lambda i, s: (0, i, 0))],
        out_specs=pl.BlockSpec((t, n), lambda i, s: (i, 0)))
    return _pcall(body, name=name, grid_spec=gs, out_shape=jax.ShapeDtypeStruct((m, n), F32),
                  compiler_params=_params(("parallel",)))(sel, part, recv)


def _adamw_math(w, g, m, v):
    nm = ADAM_B1 * m + (1.0 - ADAM_B1) * g
    nv = ADAM_B2 * v + (1.0 - ADAM_B2) * (g * g)
    m_hat = nm / (1.0 - ADAM_B1 ** ADAM_STEP)
    v_hat = nv / (1.0 - ADAM_B2 ** ADAM_STEP)
    return -ADAM_LR * (m_hat / (jnp.sqrt(v_hat) + ADAM_EPS) + ADAM_WD * w), nm, nv


def _adamw(w, g, m, v, name):
    _, rows, cols = w.shape
    t = _row_tile(rows, cols)

    def body(w_ref, g_ref, m_ref, v_ref, d_ref, nm_ref, nv_ref):
        d_ref[...], nm_ref[...], nv_ref[...] = _adamw_math(w_ref[...], g_ref[...], m_ref[...], v_ref[...])

    spec3 = pl.BlockSpec((None, t, cols), lambda i: (0, i, 0))
    spec2 = pl.BlockSpec((t, cols), lambda i: (i, 0))
    o = jax.ShapeDtypeStruct(w.shape, F32)
    return _pcall(body, name=name, grid=(rows // t,), in_specs=[spec3, spec2, spec3, spec3], out_specs=(spec3,) * 3,
                  out_shape=(o, o, o), compiler_params=_params(("parallel",)))(w, g, m, v)


def _adamw_reduced(w, m, v, parts, recvs, sel):
    _, rows, cols = w.shape
    half = cols // 2
    t = _row_tile(rows, half)

    def body(sel_ref, w_ref, m_ref, v_ref, pa_ref, pb_ref, ra_ref, rb_ref, g_ref, d_ref, nm_ref, nv_ref):
        total = lambda p_ref, r_ref: ((p_ref[...].astype(F32) + r_ref[0].astype(F32)) + r_ref[1].astype(F32)) \
            + r_ref[2].astype(F32)
        g = jnp.where(pl.program_id(1) == 0, total(pa_ref, ra_ref), total(pb_ref, rb_ref))
        g_ref[...] = g
        d_ref[...], nm_ref[...], nv_ref[...] = _adamw_math(w_ref[...], g, m_ref[...], v_ref[...])

    wspec = pl.BlockSpec((None, t, half), lambda i, j, s: (0, i, j))
    pspec = pl.BlockSpec((None, t, half), lambda i, j, s: (s[0], i, 0))
    rspec = pl.BlockSpec((3, t, half), lambda i, j, s: (0, i, 0))
    gs = pltpu.PrefetchScalarGridSpec(num_scalar_prefetch=1, grid=(rows // t, 2),
                                      in_specs=[wspec, wspec, wspec, pspec, pspec, rspec, rspec],
                                      out_specs=(wspec,) * 4)
    o = jax.ShapeDtypeStruct(w.shape, F32)
    return _pcall(body, name="adamw_w_in", grid_spec=gs, out_shape=(o, o, o, o),
                  compiler_params=_params(("parallel", "arbitrary")))(sel, w, m, v, *parts, *recvs)


def _adamw_small(ws, gs, ms, vs):
    n = len(ws)

    def body(*refs):
        for i in range(n):
            w_ref, g_ref, m_ref, v_ref = (refs[k * n + i] for k in range(4))
            d, nm, nv = _adamw_math(w_ref[...], g_ref[...], m_ref[...], v_ref[...])
            refs[4 * n + i][...] = d
            refs[5 * n + i][...] = nm
            refs[6 * n + i][...] = nv

    shapes = tuple(jax.ShapeDtypeStruct(w.shape, F32) for w in ws)
    res = _pcall(body, name="adamw_small", out_shape=shapes * 3, compiler_params=_params())(*ws, *gs, *ms, *vs)
    return res[:n], res[n:2 * n], res[2 * n:]


def _mesh_pos():
    return lax.axis_index("x"), lax.axis_index("y"), lax.axis_index("c")


class _Gather:
    def __init__(self, arrs, relay=False, chunks=None):
        self.relay = relay
        self.ins = list(arrs)
        self.out_shape = tuple(jax.ShapeDtypeStruct((N_DEV,) + a.shape, a.dtype) for a in arrs)
        self.pieces = []
        for a, arr in enumerate(arrs):
            n = (chunks or {}).get(a, 1)
            rows = arr.shape[0] // n
            self.pieces += [(a, None if n == 1 else pl.ds(i * rows, rows)) for i in range(n)]
        npc = len(self.pieces)
        self.sems = [pltpu.SemaphoreType.DMA((7 * npc,)), pltpu.SemaphoreType.DMA((7 * npc,)),
                     pltpu.SemaphoreType.DMA((npc,))]

    def _copies(self, ins, outs, sems):
        send_sems, recv_sems, local_sems = sems
        x, y, c = _mesh_pos()
        me, sibling = (x, y, c), (x, y, 1 - c)
        chips = [(1 - x, y), (x, 1 - y), (1 - x, 1 - y)]

        def src_of(p):
            a, rows = self.pieces[p]
            return ins[a] if rows is None else ins[a].at[rows]

        def dst_of(p, block):
            a, rows = self.pieces[p]
            ref = outs[a].at[_slot(block)]
            return ref if rows is None else ref.at[rows]

        def copy(p, k, block, to, from_input=False):
            dst = dst_of(p, block)
            return pltpu.make_async_remote_copy(
                src_ref=src_of(p) if from_input else dst, dst_ref=dst, send_sem=send_sems.at[7 * p + k],
                recv_sem=recv_sems.at[7 * p + k], device_id=to, device_id_type=MESH)

        npc = len(self.pieces)
        mine = [pltpu.make_async_copy(src_of(p), dst_of(p, me), local_sems.at[p]) for p in range(npc)]
        direct = chips[:2] if self.relay else chips
        first = []
        for p in range(npc):
            first.append(copy(p, 0, me, sibling, from_input=True))
            first += [copy(p, 1 + j, me, (*chip, c), from_input=True) for j, chip in enumerate(direct)]
        return me, sibling, chips, c, copy, mine, first

    def start(self, ins, outs, sems):
        *_, mine, first = self._copies(ins, outs, sems)
        for cp in mine + first:
            cp.start()

    def finish(self, ins, outs, sems):
        me, sibling, chips, c, copy, mine, first = self._copies(ins, outs, sems)
        x, y = me[0], me[1]
        npc = len(self.pieces)
        passed = []

        def pass_on(cp):
            cp.start()
            passed.append(cp)

        for p in range(npc):
            for j in range(2):
                copy(p, 1 + j, (*chips[j], c), me).wait_recv()
                pass_on(copy(p, 4 + j, (*chips[j], c), sibling))
            if self.relay:
                owner = ((x + 1 - c) % 2, (y + c) % 2, c)
                pass_on(copy(p, 3, owner, ((x + c) % 2, (y + 1 - c) % 2, c)))
        for p in range(npc):
            copy(p, 3, (*chips[2], c), me).wait_recv()
            pass_on(copy(p, 6, (*chips[2], c), sibling))
        for p in range(npc):
            copy(p, 0, sibling, me).wait_recv()
            for j, chip in enumerate(chips):
                copy(p, 4 + j, (*chip, 1 - c), me).wait_recv()
        for cp in first + passed:
            cp.wait_send()
        for cp in mine:
            cp.wait()


class _ExchangeCore:
    def __init__(self, fulls):
        self.ins = list(fulls)
        self.out_shape = tuple(jax.ShapeDtypeStruct((4,) + f.shape[2:], f.dtype) for f in fulls)
        self.sems = [pltpu.SemaphoreType.DMA((4 * len(fulls),)), pltpu.SemaphoreType.DMA((4 * len(fulls),))]

    def _copies(self, ins, outs, sems):
        send_sems, recv_sems = sems
        x, y, c = _mesh_pos()
        return [pltpu.make_async_remote_copy(
            src_ref=ins[a].at[q, 1 - c], dst_ref=outs[a].at[q], send_sem=send_sems.at[4 * a + q],
            recv_sem=recv_sems.at[4 * a + q], device_id=(x, y, 1 - c), device_id_type=MESH)
            for a in range(len(self.ins)) for q in range(4)]

    def start(self, ins, outs, sems):
        for cp in self._copies(ins, outs, sems):
            cp.start()

    def finish(self, ins, outs, sems):
        for cp in self._copies(ins, outs, sems):
            cp.wait()


class _ExchangeChip:
    def __init__(self, parts):
        self.ins = list(parts)
        self.out_shape = tuple(jax.ShapeDtypeStruct((3,) + p.shape[1:], p.dtype) for p in parts)
        self.sems = [pltpu.SemaphoreType.DMA((3 * len(parts),)), pltpu.SemaphoreType.DMA((3 * len(parts),))]

    def _copies(self, ins, outs, sems):
        send_sems, recv_sems = sems
        x, y, c = _mesh_pos()
        chips = [(1 - x, y), (x, 1 - y), (1 - x, 1 - y)]
        return [pltpu.make_async_remote_copy(
            src_ref=ins[a].at[2 * px + py], dst_ref=outs[a].at[j], send_sem=send_sems.at[3 * a + j],
            recv_sem=recv_sems.at[3 * a + j], device_id=(px, py, c), device_id_type=MESH)
            for a in range(len(self.ins)) for j, (px, py) in enumerate(chips)]

    def start(self, ins, outs, sems):
        for cp in self._copies(ins, outs, sems):
            cp.start()

    def finish(self, ins, outs, sems):
        for cp in self._copies(ins, outs, sems):
            cp.wait()


HBM_ONLY = pl.BlockSpec(memory_space=pltpu.HBM)
SEM_SPEC = pl.BlockSpec(memory_space=pltpu.SEMAPHORE)
SIDE_EFFECT = pltpu.SideEffectType.DATAFLOW_SIDE_EFFECTING


def _chip_copies(p_refs, land_refs, send_sems, recv_sems):
    x, y, c = _mesh_pos()
    return [pltpu.make_async_remote_copy(
        src_ref=p_refs[a].at[2 * px + py], dst_ref=land_refs[a].at[j], send_sem=send_sems.at[3 * a + j],
        recv_sem=recv_sems.at[3 * a + j], device_id=(px, py, c), device_id_type=MESH)
        for a in range(len(p_refs)) for j, (px, py) in enumerate([(1 - x, y), (x, 1 - y), (1 - x, 1 - y)])]


def _chip_exchange_start(parts, name):
    n = len(parts)
    lands = [lax.empty((3,) + p.shape[1:], p.dtype) for p in parts]

    def body(*refs):
        p_refs, land_refs, (send_sems, recv_sems) = refs[:n], refs[n:2 * n], refs[2 * n:2 * n + 2]
        for cp in _chip_copies(p_refs, land_refs, send_sems, recv_sems):
            cp.start()
        token = refs[-1]
        token[...] = jnp.zeros_like(token)

    hbm = lambda t: pltpu.HBM(t.shape, t.dtype)
    res = pl.pallas_call(
        body, name=name,
        out_shape=(pltpu.SemaphoreType.DMA((3 * n,)), pltpu.SemaphoreType.DMA((3 * n,)), *[hbm(t) for t in parts + lands],
                   jax.ShapeDtypeStruct((8, 128), F32)),
        in_specs=(HBM_ONLY,) * (2 * n),
        out_specs=(SEM_SPEC, SEM_SPEC, *[HBM_ONLY] * (2 * n), pl.BlockSpec(memory_space=pltpu.VMEM)),
        input_output_aliases={i: 2 + i for i in range(2 * n)},
        compiler_params=pltpu.CompilerParams(has_side_effects=SIDE_EFFECT))(
        *[pltpu.with_memory_space_constraint(t, pltpu.HBM) for t in parts + lands])
    return (res[0], res[1], list(res[2:2 + n]), list(res[2 + n:2 + 2 * n])), res[-1]


def _chip_exchange_wait(in_flight, after, name):
    send_sems, recv_sems, parts, lands = in_flight
    n = len(parts)

    def body(*refs):
        p_refs, land_refs, (send_sems, recv_sems) = refs[:n], refs[n:2 * n], refs[2 * n:2 * n + 2]
        for cp in _chip_copies(p_refs, land_refs, send_sems, recv_sems):
            cp.wait_send()
            cp.wait_recv()

    res = pl.pallas_call(
        body, name=name, out_shape=tuple(pltpu.HBM(t.shape, t.dtype) for t in parts + lands),
        in_specs=(*[HBM_ONLY] * (2 * n), SEM_SPEC, SEM_SPEC, pl.BlockSpec(memory_space=pl.ANY)),
        out_specs=(HBM_ONLY,) * (2 * n), input_output_aliases={i: i for i in range(2 * n)},
        compiler_params=pltpu.CompilerParams(has_side_effects=SIDE_EFFECT))(*parts, *lands, send_sems, recv_sems, after)
    return list(res[:n]), list(res[n:])


def _slot(p):
    return 4 * p[0] + 2 * p[1] + p[2]


def _gather_copies(src_refs, out_refs, send_sems, recv_sems):
    x, y, c = _mesh_pos()
    targets = [(x, y, 1 - c), (1 - x, y, c), (x, 1 - y, c), (1 - x, 1 - y, c)]
    return [pltpu.make_async_remote_copy(
        src_ref=src_refs[a], dst_ref=out_refs[a].at[_slot((x, y, c))], send_sem=send_sems.at[4 * a + k],
        recv_sem=recv_sems.at[4 * a + k], device_id=to, device_id_type=MESH)
        for a in range(len(src_refs)) for k, to in enumerate(targets)]


def _gather_start(shards, after, name):
    n = len(shards)
    outs = [lax.empty((N_DEV,) + s.shape, s.dtype) for s in shards]

    def body(*refs):
        for cp in _gather_copies(refs[:n], refs[n:2 * n], refs[2 * n + 1], refs[2 * n + 2]):
            cp.start()
        token = refs[-1]
        token[...] = jnp.zeros_like(token)

    res = pl.pallas_call(
        body, name=name,
        out_shape=(pltpu.SemaphoreType.DMA((4 * n,)), pltpu.SemaphoreType.DMA((4 * n,)),
                   *[pltpu.HBM(t.shape, t.dtype) for t in shards + outs], jax.ShapeDtypeStruct((8, 128), F32)),
        in_specs=(*[HBM_ONLY] * (2 * n), pl.BlockSpec(memory_space=pl.ANY)),
        out_specs=(SEM_SPEC, SEM_SPEC, *[HBM_ONLY] * (2 * n), pl.BlockSpec(memory_space=pltpu.VMEM)),
        input_output_aliases={i: 2 + i for i in range(2 * n)},
        compiler_params=pltpu.CompilerParams(has_side_effects=SIDE_EFFECT))(
        *[pltpu.with_memory_space_constraint(t, pltpu.HBM) for t in shards + outs], after)
    return (res[0], res[1], list(res[2:2 + n]), list(res[2 + n:2 + 2 * n])), res[-1]


def _gather_wait(in_flight, after, name):
    send_sems, recv_sems, shards, outs = in_flight
    n = len(shards)

    def body(*refs):
        for cp in _gather_copies(refs[:n], refs[n:2 * n], refs[2 * n], refs[2 * n + 1]):
            cp.wait_send()
            cp.wait_recv()

    res = pl.pallas_call(
        body, name=name, out_shape=tuple(pltpu.HBM(t.shape, t.dtype) for t in shards + outs),
        in_specs=(*[HBM_ONLY] * (2 * n), SEM_SPEC, SEM_SPEC, pl.BlockSpec(memory_space=pl.ANY)),
        out_specs=(HBM_ONLY,) * (2 * n), input_output_aliases={i: i for i in range(2 * n)},
        compiler_params=pltpu.CompilerParams(has_side_effects=SIDE_EFFECT))(*shards, *outs, send_sems, recv_sems, after)
    return list(res[:n]), list(res[n:])


class _PassToSibling:
    def __init__(self, shards, gathered):
        n = self.n = len(shards)
        self.ins = list(shards) + list(gathered)
        self.out_shape = tuple(jax.ShapeDtypeStruct(g.shape, g.dtype) for g in gathered)
        self.aliases = {n + a: a for a in range(n)}
        self.sems = [pltpu.SemaphoreType.DMA((3 * n,)), pltpu.SemaphoreType.DMA((3 * n,)),
                     pltpu.SemaphoreType.DMA((n,))]

    def _copies(self, ins, outs, sems):
        send_sems, recv_sems, local_sems = sems
        x, y, c = _mesh_pos()
        chips = [(1 - x, y), (x, 1 - y), (1 - x, 1 - y)]
        mine = [pltpu.make_async_copy(ins[a], outs[a].at[_slot((x, y, c))], local_sems.at[a]) for a in range(self.n)]
        passed, awaited = [], []
        for a in range(self.n):
            for j, chip in enumerate(chips):
                sems_j = dict(send_sem=send_sems.at[3 * a + j], recv_sem=recv_sems.at[3 * a + j],
                              device_id=(x, y, 1 - c), device_id_type=MESH)
                blk = outs[a].at[_slot((*chip, c))]
                passed.append(pltpu.make_async_remote_copy(src_ref=blk, dst_ref=blk, **sems_j))
                got = outs[a].at[_slot((*chip, 1 - c))]
                awaited.append(pltpu.make_async_remote_copy(src_ref=got, dst_ref=got, **sems_j))
        return mine, passed, awaited

    def start(self, ins, outs, sems):
        mine, passed, _ = self._copies(ins, outs, sems)
        for cp in mine + passed:
            cp.start()

    def finish(self, ins, outs, sems):
        mine, passed, awaited = self._copies(ins, outs, sems)
        for cp in passed:
            cp.wait_send()
        for cp in awaited:
            cp.wait_recv()
        for cp in mine:
            cp.wait()


def _reduce_sums(fulls, recv_core, core, tag):
    return [_pair_sum(f, r, core, f"rs_pair_{tag}{i}") for i, (f, r) in enumerate(zip(fulls, recv_core))]


def _local_step(x, tgt, mods, w_in_t, shards, small, chip, core):
    sh1, sc1, g1, sh2, sc2, g2 = mods
    norm1_g, rel_bias, gn_g, gn_b, norm2_g, norm_f_g = small
    tables = _ret_tables()
    buckets = jnp.asarray(_bucket_tables())

    h1 = _norm_mod_fwd(x, norm1_g, sh1, sc1, "norm1_fwd")
    flight_w, token_w = _gather_start(list(shards), h1, "gather_w_start")
    proj, slabs = _proj(h1, w_in_t, token_w)
    gated, ro, states = _ret_fwd(proj, tables, gn_g, gn_b)
    bias = _bias_build(rel_bias, buckets)
    outs, lses = [], []
    for gi in range(len(ATT_GROUPS)):
        o, l = _att_fwd(slabs, bias, gi)
        outs.append(o)
        lses.append(l)
    att, gathered = _mix_fwd(outs, lses, comm=_PassToSibling(*_gather_wait(flight_w, lses[2], "gather_w_wait")))
    w_ret_out, w_att_out, w_o, w_ff1, w_ff2 = (_from_slots(g, ax) for g, ax in zip(gathered, BIG_AXES[1:]))
    ret_out = _mm(gated, w_ret_out, 'nn', tm=S, tn=256, tk=2048, name="ret_out")
    att_out = _mm(att, w_att_out, 'nn', tm=S, tn=512, tk=AW, name="att_out")
    merged = _merge_fwd(proj, ret_out, att_out)
    mixo, x1 = _mm(merged, w_o, 'nn', tm=S, tn=256, tk=D, name="w_o", res=x, gvec=g1)
    h2 = _norm_mod_fwd(x1, norm2_g, sh2, sc2, "norm2_fwd")
    u, act = _mm(h2, w_ff1, 'nn', tm=S, tn=512, tk=D, name="ff1", relu2=True)
    f, x2 = _mm(act, w_ff2, 'nn', tm=1024, tn=512, tk=D_FF, name="ff2", res=x1, gvec=g2)
    loss, dx2, g_normf, df, dg2 = _final_loss(x2, tgt, norm_f_g, f, g2)

    gw_ff2 = _mm(act, df, 'tn', tm=512, tn=D, tk=S, name="gw_ff2", out_dtype=BF16)
    du = _mm(df, w_ff2, 'nt', tm=S, tn=512, tk=D, name="d_act", out_dtype=BF16, relu2_of=u)
    gw_ff1 = _mm(h2, du, 'tn', tm=D, tn=512, tk=S, name="gw_ff1", out_dtype=BF16)
    fulls_a = [_to_slots(g, ax) for g, ax in zip((gw_ff1, gw_ff2), BIG_AXES[4:])]
    dh2, recv_core_a = _mm(du, w_ff1, 'nt', tm=1024, tn=1024, tk=2048, name="dh2", comm=_ExchangeCore(fulls_a))
    parts_a = _reduce_sums(fulls_a, recv_core_a, core, "a")
    flight_a, token_a = _chip_exchange_start(parts_a, "rs_a_start")
    dx1, dsc2, dsh2, g_norm2, dmixo, dg1 = _norm_mod_bwd(x1, norm2_g, sc2, dh2, dx2, "norm2_bwd", gate=(mixo, g1))

    gw_o = _mm(merged, dmixo, 'tn', tm=D, tn=512, tk=S, name="gw_o", out_dtype=BF16, after=token_a)
    dmerged = _mm(dmixo, w_o, 'nt', tm=S, tn=512, tk=D, name="dmerged")
    d_ret_out, d_att_out, dga, dgb = _merge_bwd(proj, ret_out, att_out, dmerged)
    gw_ret_out = _mm(gated, d_ret_out, 'tn', tm=512, tn=D, tk=S, name="gw_ret_out", out_dtype=BF16)
    gw_att_out = _mm(att, d_att_out, 'tn', tm=AW, tn=D, tk=S, name="gw_att_out", out_dtype=BF16)
    fulls_b = [_to_slots(g, ax) for g, ax in zip((gw_ret_out, gw_att_out, gw_o), BIG_AXES[1:4])]
    dgated, recv_core_b = _mm(d_ret_out, w_ret_out, 'nt', tm=S, tn=512, tk=D, name="dgated",
                              comm=_ExchangeCore(fulls_b))
    parts_b = _reduce_sums(fulls_b, recv_core_b, core, "b")
    flight_b, token_b = _chip_exchange_start(parts_b, "rs_b_start")
    datt = _mm(d_att_out, w_att_out, 'nt', tm=S, tn=AW, tk=D, name="datt", after=token_b)
    mix_grads = _mix_bwd(outs, lses, datt)
    datt_parts, ds_sums = [], []
    for gi in range(len(ATT_GROUPS)):
        dq, dk, dv, ds_sum = _att_bwd(slabs, bias, outs[gi], lses[gi], mix_grads[gi], mix_grads[3 + gi], gi)
        datt_parts += [dq.reshape(S, AW), dk.reshape(S, AW), dv.reshape(S, AW)]
        ds_sums.append(ds_sum)
    g_bias = _bias_grad(jnp.concatenate(ds_sums, axis=0), buckets)[:, :, 0].T.reshape(1, -1)
    dret, g_gn_g, g_gn_b = _ret_bwd(proj, tables, gn_g, gn_b, ro, states, dgated)
    parts_a, recv_chip_a = _chip_exchange_wait(flight_a, dret, "rs_a_wait")
    parts_b, recv_chip_b = _chip_exchange_wait(flight_b, dret, "rs_b_wait")
    red_a = [_chip_sum(p, r, chip, f"rs_sum_a{i}") for i, (p, r) in enumerate(zip(parts_a, recv_chip_a))]
    red_b = [_chip_sum(p, r, chip, f"rs_sum_b{i}") for i, (p, r) in enumerate(zip(parts_b, recv_chip_b))]
    dproj = jnp.concatenate([dret] + datt_parts + [dga, dgb], axis=1)
    in_flight, token = [], None
    for half in range(2):
        h1_half = h1[:, half * (D // 2):(half + 1) * (D // 2)]
        gw_half = _mm(dproj, h1_half, 'tn', tm=512, tn=D // 2, tk=S, name=f"gw_in{half}", out_dtype=BF16, after=token)
        full_in = [_to_slots(gw_half, 0)]
        recv_core_in = _run_comm(_ExchangeCore(full_in), f"rs_core_in{half}")
        part_in = _reduce_sums(full_in, recv_core_in, core, f"c{half}")
        flight, token = _chip_exchange_start(part_in, f"rs_in{half}_start")
        in_flight.append(flight)
    dh1 = _mm(dproj, w_in_t, 'nn', tm=1024, tn=1024, tk=2560, name="dh1", after=token)
    gx, dsc1, dsh1, g_norm1 = _norm_mod_bwd(x, norm1_g, sc1, dh1, dx1, "norm1_bwd")

    dmod = [dsh1, dsc1, dg1, dsh2, dsc2, dg2]
    small_g = [g_norm1, g_bias, g_gn_g, g_gn_b, g_norm2, g_normf]
    return loss, gx, in_flight, red_b + red_a, small_g, dmod


def _to_slots(g, axis):
    if axis == 0:
        return g.reshape(4, 2, g.shape[0] // N_DEV, g.shape[1])
    return g.reshape(g.shape[0], N_DEV, g.shape[1] // N_DEV).transpose(1, 0, 2).reshape(4, 2, g.shape[0], -1)


def _from_slots(w8, axis):
    if axis == 0:
        return w8.reshape(-1, w8.shape[2])
    return w8.transpose(1, 0, 2).reshape(w8.shape[1], -1)


BIG_AXES = (1, 0, 1, 0, 1, 0)


def kernel(x, c, w_ada, b_ada, norm1_g, w_in, rel_bias, ret_gn_g, ret_gn_b, w_ret_out, w_att_out, w_o, norm2_g, w_ff1, w_ff2, norm_f_g, loss_target, m_w_ada, m_b_ada, m_norm1_g, m_w_in, m_rel_bias, m_ret_gn_g, m_ret_gn_b, m_w_ret_out, m_w_att_out, m_w_o, m_norm2_g, m_w_ff1, m_w_ff2, m_norm_f_g, v_w_ada, v_b_ada, v_norm1_g, v_w_in, v_rel_bias, v_ret_gn_g, v_ret_gn_b, v_w_ret_out, v_w_att_out, v_w_o, v_norm2_g, v_w_ff1, v_w_ff2, v_norm_f_g):
    mx, my, mc = _mesh_pos()
    dev = 4 * mx + 2 * my + mc
    chip = jnp.reshape(2 * mx + my, (1,)).astype(jnp.int32)
    core = jnp.reshape(mc, (1,)).astype(jnp.int32)
    ada_w = D * 6 // N_DEV

    w_in, m_w_in, v_w_in = (jnp.transpose(t, (0, 2, 1)) for t in (w_in, m_w_in, v_w_in))

    shards = [w[0].astype(BF16) for w in (w_in, w_ret_out, w_att_out, w_o, w_ff1, w_ff2)]
    c_all, w_in8 = _run_comm(_Gather([c, shards[0]], relay=True, chunks={1: 4}), "gather_c_w_in")
    c_all = c_all.reshape(N_DEV, D)
    b_sl = lax.dynamic_slice(b_ada, (0, dev * ada_w), (1, ada_w))
    (mod_all,) = _run_comm(_Gather([_ada_fwd(c_all, w_ada[0], b_sl)]), "gather_mod")
    mod = lax.dynamic_index_in_dim(mod_all, dev, axis=1, keepdims=False).reshape(6, D)
    mods = tuple(mod[i:i + 1] for i in range(6))

    small = (norm1_g, rel_bias, ret_gn_g, ret_gn_b, norm2_g, norm_f_g.reshape(1, D))
    loss, gx, in_flight, big_red, small_g, dmod = _local_step(x[0], loss_target[0], mods, w_in8.reshape(IN_COLS, D),
                                                              shards[1:], small, chip, core)

    gathered = _run_comm(_Gather(dmod + small_g + [loss]), "gather_small")
    g_b_ada, dmod_all, (g_norm1, g_bias, g_gn_g, g_gn_b, g_norm2, g_normf, loss_sum) = _sum_small(gathered)
    loss_out = loss_sum[0, 0]
    g_w_ada = _ada_bwd(c_all, lax.dynamic_slice(dmod_all, (0, dev * ada_w), (N_DEV, ada_w)))

    names = ['w_ada', 'b_ada', 'norm1_g', 'w_in', 'rel_bias', 'ret_gn_g', 'ret_gn_b', 'w_ret_out', 'w_att_out',
             'w_o', 'norm2_g', 'w_ff1', 'w_ff2', 'norm_f_g']
    ws = dict(zip(names, (w_ada, b_ada, norm1_g, w_in, rel_bias, ret_gn_g, ret_gn_b, w_ret_out, w_att_out, w_o,
                          norm2_g, w_ff1, w_ff2, norm_f_g)))
    ms = dict(zip(names, (m_w_ada, m_b_ada, m_norm1_g, m_w_in, m_rel_bias, m_ret_gn_g, m_ret_gn_b, m_w_ret_out,
                          m_w_att_out, m_w_o, m_norm2_g, m_w_ff1, m_w_ff2, m_norm_f_g)))
    vs = dict(zip(names, (v_w_ada, v_b_ada, v_norm1_g, v_w_in, v_rel_bias, v_ret_gn_g, v_ret_gn_b, v_w_ret_out,
                          v_w_att_out, v_w_o, v_norm2_g, v_w_ff1, v_w_ff2, v_norm_f_g)))
    grads = dict(w_ada=g_w_ada, w_ret_out=big_red[0], w_att_out=big_red[1], w_o=big_red[2],
                 w_ff1=big_red[3], w_ff2=big_red[4], b_ada=g_b_ada, norm1_g=g_norm1, rel_bias=g_bias,
                 ret_gn_g=g_gn_g, ret_gn_b=g_gn_b, norm2_g=g_norm2, norm_f_g=g_normf)
    delta, new_m, new_v = {}, {}, {}
    for n in ('w_ada', 'w_ret_out', 'w_att_out', 'w_o', 'w_ff1', 'w_ff2'):
        delta[n], new_m[n], new_v[n] = _adamw(ws[n], grads[n], ms[n], vs[n], "adamw_" + n)
        grads[n] = grads[n].reshape(ws[n].shape)
    small_names = ('b_ada', 'norm1_g', 'rel_bias', 'ret_gn_g', 'ret_gn_b', 'norm2_g', 'norm_f_g')
    two_d = {n: (1, ws[n].size) if ws[n].ndim == 1 else ws[n].shape for n in small_names}
    d_, m_, v_ = _adamw_small(*[[src[n].reshape(two_d[n]) for n in small_names] for src in (ws, grads, ms, vs)])
    for i, n in enumerate(small_names):
        shp = ws[n].shape
        delta[n], new_m[n], new_v[n] = d_[i].reshape(shp), m_[i].reshape(shp), v_[i].reshape(shp)
        grads[n] = grads[n].reshape(shp)

    done = lax.optimization_barrier((gx, tuple(d_), tuple(delta[n] for n in ('w_ada', 'w_ret_out', 'w_att_out', 'w_o',
                                                                               'w_ff1', 'w_ff2'))))
    parts_in, recvs_in = [], []
    for half, flight in enumerate(in_flight):
        (part_in,), (recv_chip_in,) = _chip_exchange_wait(flight, done[0], f"rs_in{half}_wait")
        parts_in.append(part_in)
        recvs_in.append(recv_chip_in)
    grads['w_in'], delta['w_in'], new_m['w_in'], new_v['w_in'] = _adamw_reduced(w_in, m_w_in, v_w_in, parts_in,
                                                                               recvs_in, chip)
    for d in (grads, delta, new_m, new_v):
        d['w_in'] = jnp.transpose(d['w_in'], (0, 2, 1))
    return (loss_out, gx[None], *[grads[n] for n in names], *[delta[n] for n in names],
            *[new_m[n] for n in names], *[new_v[n] for n in names])
```

```python
import functools
import math

import numpy as np
import jax
import jax.numpy as jnp
from jax import lax
from jax.experimental import pallas as pl
from jax.experimental.pallas import tpu as pltpu

F32 = jnp.float32
BF16 = jnp.bfloat16
MESH = pl.DeviceIdType.MESH

N_DEV = 8
S = 2048
D = 1024
RET_HEADS = 4
RET_DK = 256
RET_DV = 512
CHUNK = 128
N_CHUNK = S // CHUNK
ATT_GROUPS = ((128, 1), (512, 4), (2048, 16))
ATT_HG = 4
ATT_DH = 128
ATT_BLK = 128
N_BUCKETS = 32
MAX_DIST = 2048
D_FF = 4096
IN_COLS = 12800
OFF_RQ, OFF_RK, OFF_RV, OFF_RG, OFF_ATT = 0, 1024, 2048, 4096, 6144
OFF_GA, OFF_GB = 6144, 7168
RMS_EPS = 1e-6
GN_EPS = 1e-5
ADAM_LR, ADAM_B1, ADAM_B2, ADAM_EPS, ADAM_WD, ADAM_STEP = 0.001, 0.9, 0.999, 1e-08, 0.01, 10
VMEM_LIMIT = 48 * 1024 * 1024


def _pcall(body, **kw):
    return pl.pallas_call(body, **kw)


def _params(sem=None):
    return pltpu.CompilerParams(dimension_semantics=sem, vmem_limit_bytes=VMEM_LIMIT)


HBM_SPEC = pl.BlockSpec(memory_space=pl.ANY)


def _carry(body, comm, *, name, grid, in_specs, out_specs, out_shape, scratch_shapes=()):
    single = not isinstance(out_specs, (tuple, list))
    o_specs = (out_specs,) if single else tuple(out_specs)
    o_shape = (out_shape,) if single else tuple(out_shape)
    n_in, n_out, n_scr = len(in_specs), len(o_specs), len(scratch_shapes)
    nci, nco = len(comm.ins), len(comm.out_shape)
    total = int(np.prod(grid))

    def wrapped(*refs):
        bounds = np.cumsum([0, n_in, nci, n_out, nco, n_scr])
        a, ci, o, co, scr = (refs[bounds[i]:bounds[i + 1]] for i in range(5))
        sems = refs[bounds[5]:]
        flat = 0
        for d, g in enumerate(grid):
            flat = flat * g + pl.program_id(d)

        @pl.when(flat == 0)
        def _():
            comm.start(ci, co, sems)

        body(*a, *o, *scr)

        @pl.when(flat == total - 1)
        def _():
            comm.finish(ci, co, sems)

    aliases = {n_in + i: n_out + o for i, o in getattr(comm, "aliases", {}).items()}
    call = _pcall(wrapped, name=name, grid=grid, in_specs=list(in_specs) + [HBM_SPEC] * nci,
                  out_specs=o_specs + (HBM_SPEC,) * nco, out_shape=o_shape + tuple(comm.out_shape),
                  scratch_shapes=list(scratch_shapes) + list(comm.sems), input_output_aliases=aliases,
                  compiler_params=_params(("arbitrary",) * len(grid)))

    def run(*args):
        res = call(*args, *comm.ins)
        own = res[0] if single else tuple(res[:n_out])
        return own, tuple(res[n_out:])

    return run


def _run_comm(comm, name, after=None):
    nci, nco = len(comm.ins), len(comm.out_shape)
    extra = [] if after is None else [after]

    def body(*refs):
        ci, co, sems = refs[:nci], refs[nci + len(extra):nci + len(extra) + nco], refs[nci + len(extra) + nco:]
        comm.start(ci, co, sems)
        comm.finish(ci, co, sems)

    return _pcall(body, name=name, in_specs=[HBM_SPEC] * (nci + len(extra)), out_specs=(HBM_SPEC,) * nco,
                  out_shape=tuple(comm.out_shape), scratch_shapes=list(comm.sems))(*comm.ins, *extra)


def _dot(a, b, dn):
    return lax.dot_general(a.astype(BF16), b.astype(BF16), (dn, ((), ())), preferred_element_type=F32)


NN = ((1,), (0,))
NT = ((1,), (1,))
TN = ((0,), (0,))


def _mm(a, b, mode, *, tm, tn, tk, name, out_dtype=F32, res=None, gvec=None, relu2=False, relu2_of=None, comm=None,
        after=None):
    if mode == 'nn':
        (M, K), (_, N) = a.shape, b.shape
        a_spec = pl.BlockSpec((tm, tk), lambda i, j, k: (i, k))
        b_spec = pl.BlockSpec((tk, tn), lambda i, j, k: (k, j))
        dn = NN
    elif mode == 'nt':
        (M, K), (N, _) = a.shape, b.shape
        a_spec = pl.BlockSpec((tm, tk), lambda i, j, k: (i, k))
        b_spec = pl.BlockSpec((tn, tk), lambda i, j, k: (j, k))
        dn = NT
    else:
        (K, M), (_, N) = a.shape, b.shape
        a_spec = pl.BlockSpec((tk, tm), lambda i, j, k: (k, i))
        b_spec = pl.BlockSpec((tk, tn), lambda i, j, k: (k, j))
        dn = TN
    assert M % tm == 0 and N % tn == 0 and K % tk == 0, (name, M, N, K)
    nk = K // tk
    fused = res is not None
    o_spec = pl.BlockSpec((tm, tn), lambda i, j, k: (i, j))

    def body(a_ref, b_ref, *rest):
        acc_ref = rest[-1] if nk > 1 else None
        if after is not None:
            rest = rest[1:]
        if fused:
            res_ref, g_ref, o_ref, x_ref = rest[:4]
        elif relu2_of is not None:
            u_ref, o_ref = rest[:2]
        elif relu2:
            o_ref, act_ref = rest[:2]
        else:
            o_ref = rest[0]

        def finish(acc):
            if relu2_of is not None:
                acc = acc * (2.0 * jnp.maximum(u_ref[...], 0.0))
            o_ref[...] = acc.astype(o_ref.dtype)
            if fused:
                x_ref[...] = res_ref[...] + g_ref[...] * acc
            if relu2:
                r = jnp.maximum(acc, 0.0)
                act_ref[...] = (r * r).astype(BF16)

        p = _dot(a_ref[...], b_ref[...], dn)
        if nk == 1:
            finish(p)
        else:
            k = pl.program_id(2)

            @pl.when(k == 0)
            def _():
                acc_ref[...] = p

            @pl.when(k > 0)
            def _():
                acc_ref[...] += p

            @pl.when(k == nk - 1)
            def _():
                finish(acc_ref[...])

    in_specs = [a_spec, b_spec]
    args = [a, b]
    if after is not None:
        in_specs.append(pl.BlockSpec(memory_space=pl.ANY))
        args.append(after)
    out_shape = jax.ShapeDtypeStruct((M, N), out_dtype)
    out_specs = o_spec
    if fused:
        in_specs += [pl.BlockSpec((tm, tn), lambda i, j, k: (i, j)), pl.BlockSpec((1, tn), lambda i, j, k: (0, j))]
        args += [res, gvec]
        out_shape = (out_shape, jax.ShapeDtypeStruct((M, N), F32))
        out_specs = (o_spec, pl.BlockSpec((tm, tn), lambda i, j, k: (i, j)))
    elif relu2_of is not None:
        in_specs.append(pl.BlockSpec((tm, tn), lambda i, j, k: (i, j)))
        args.append(relu2_of)
    elif relu2:
        out_shape = (out_shape, jax.ShapeDtypeStruct((M, N), BF16))
        out_specs = (o_spec, pl.BlockSpec((tm, tn), lambda i, j, k: (i, j)))
    kw = dict(name=name, grid=(M // tm, N // tn, nk), in_specs=in_specs, out_specs=out_specs,
              out_shape=out_shape, scratch_shapes=[pltpu.VMEM((tm, tn), F32)] if nk > 1 else [])
    if comm is not None:
        return _carry(body, comm, **kw)(*args)
    return _pcall(body, compiler_params=_params(("parallel", "parallel", "arbitrary")), **kw)(*args)


PROJ_TN = 512
ATT_T0, ATT_T1 = 6144 // PROJ_TN, 10752 // PROJ_TN
N_SLABS = (ATT_T1 - ATT_T0) * 4
MAIN_COLS = IN_COLS - (ATT_T1 - ATT_T0) * PROJ_TN


def _proj(h1, w_in_t, after):
    nj = IN_COLS // PROJ_TN

    def body(a_ref, b_ref, after_ref, main_ref, slab_ref):
        j = pl.program_id(1)
        is_att = (j >= ATT_T0) & (j < ATT_T1)
        chunks = [pl.ds(c * 512, 512) for c in range(S // 512)]

        @pl.when(jnp.logical_not(is_att))
        def _():
            for rows in chunks:
                main_ref[rows, :] = _dot(a_ref[rows, :], b_ref[...], NT)

        @pl.when(is_att)
        def _():
            for rows in chunks:
                p = _dot(a_ref[rows, :], b_ref[...], NT)
                for h in range(4):
                    slab_ref[h, rows, :] = p[:, h * 128:(h + 1) * 128]

    main_idx = lambda j: jnp.where(j < ATT_T0, j, jnp.where(j < ATT_T1, ATT_T0 - 1, j - (ATT_T1 - ATT_T0)))
    slab_idx = lambda j: jnp.clip(j - ATT_T0, 0, ATT_T1 - ATT_T0 - 1)
    return _pcall(
        body, name="proj", grid=(1, nj, 1),
        in_specs=[pl.BlockSpec((S, D), lambda i, j, k: (0, 0)), pl.BlockSpec((PROJ_TN, D), lambda i, j, k: (j, 0)),
                  HBM_SPEC],
        out_specs=(pl.BlockSpec((S, PROJ_TN), lambda i, j, k: (0, main_idx(j))),
                   pl.BlockSpec((4, S, 128), lambda i, j, k: (slab_idx(j), 0, 0))),
        out_shape=(jax.ShapeDtypeStruct((S, MAIN_COLS), F32), jax.ShapeDtypeStruct((N_SLABS, S, 128), F32)),
        compiler_params=_params(("arbitrary",) * 3))(h1, w_in_t, after)


TR = 256


def _row_spec(w=D):
    return pl.BlockSpec((TR, w), lambda i: (i, 0))


def _vec_spec(w=D):
    return pl.BlockSpec((1, w), lambda i: (0, 0))


def _norm_mod_fwd(x, g, sh, sc, name):
    def body(x_ref, g_ref, sh_ref, sc_ref, o_ref):
        xv = x_ref[...]
        rstd = lax.rsqrt(jnp.mean(xv * xv, axis=-1, keepdims=True) + RMS_EPS)
        n = xv * rstd * g_ref[...]
        o_ref[...] = (n * (1.0 + sc_ref[...]) + sh_ref[...]).astype(BF16)

    return _pcall(body, name=name, grid=(S // TR,), in_specs=[_row_spec(), _vec_spec(), _vec_spec(), _vec_spec()],
                  out_specs=_row_spec(), out_shape=jax.ShapeDtypeStruct((S, D), BF16),
                  compiler_params=_params(("parallel",)))(x, g, sh, sc)


def _norm_mod_bwd(x, g, sc, dh, dres, name, gate=None):
    gated = gate is not None

    def body(x_ref, g_ref, sc_ref, dh_ref, dres_ref, *rest):
        if gated:
            f_ref, gv_ref, dx_ref, dsc_ref, dsh_ref, dg_ref, dz_ref, dgv_ref = rest
        else:
            dx_ref, dsc_ref, dsh_ref, dg_ref = rest
        i = pl.program_id(0)
        xv = x_ref[...]
        dh = dh_ref[...]
        rstd = lax.rsqrt(jnp.mean(xv * xv, axis=-1, keepdims=True) + RMS_EPS)
        xhat = xv * rstd
        gv = g_ref[...]
        dn = dh * (1.0 + sc_ref[...])
        dxhat = dn * gv
        dx = dres_ref[...] + rstd * (dxhat - xhat * jnp.mean(dxhat * xhat, axis=-1, keepdims=True))
        dx_ref[...] = dx
        sums = [(dsc_ref, jnp.sum(dh * (xhat * gv), axis=0, keepdims=True)),
                (dsh_ref, jnp.sum(dh, axis=0, keepdims=True)),
                (dg_ref, jnp.sum(dn * xhat, axis=0, keepdims=True))]
        if gated:
            dz_ref[...] = (dx * gv_ref[...]).astype(BF16)
            sums.append((dgv_ref, jnp.sum(dx * f_ref[...], axis=0, keepdims=True)))

        @pl.when(i == 0)
        def _():
            for ref, p in sums:
                ref[...] = p

        @pl.when(i > 0)
        def _():
            for ref, p in sums:
                ref[...] += p

    vec = jax.ShapeDtypeStruct((1, D), F32)
    in_specs = [_row_spec(), _vec_spec(), _vec_spec(), _row_spec(), _row_spec()]
    out_specs = [_row_spec(), _vec_spec(), _vec_spec(), _vec_spec()]
    out_shape = [jax.ShapeDtypeStruct((S, D), F32), vec, vec, vec]
    args = [x, g, sc, dh, dres]
    if gated:
        in_specs += [_row_spec(), _vec_spec()]
        out_specs += [_row_spec(), _vec_spec()]
        out_shape += [jax.ShapeDtypeStruct((S, D), BF16), vec]
        args += list(gate)
    return _pcall(body, name=name, grid=(S // TR,), in_specs=in_specs, out_specs=tuple(out_specs),
                  out_shape=tuple(out_shape), compiler_params=_params(("arbitrary",)))(*args)


def _final_loss(x2, tgt, g, f, g2):
    def body(x_ref, t_ref, g_ref, f_ref, g2_ref, loss_ref, dx_ref, dg_ref, df_ref, dg2_ref):
        i = pl.program_id(0)
        xv = x_ref[...]
        gv = g_ref[...]
        rstd = lax.rsqrt(jnp.mean(xv * xv, axis=-1, keepdims=True) + RMS_EPS)
        xhat = xv * rstd
        err = xhat * gv - t_ref[...]
        dy = err * (1.0 / D)
        dxhat = dy * gv
        dx = rstd * (dxhat - xhat * jnp.mean(dxhat * xhat, axis=-1, keepdims=True))
        dx_ref[...] = dx
        df_ref[...] = (dx * g2_ref[...]).astype(BF16)
        p_g = jnp.sum(dy * xhat, axis=0, keepdims=True)
        p_g2 = jnp.sum(dx * f_ref[...], axis=0, keepdims=True)
        p_l = jnp.zeros((1, 128), F32) + 0.5 * jnp.sum(jnp.mean(err * err, axis=-1, keepdims=True))

        @pl.when(i == 0)
        def _():
            dg_ref[...] = p_g
            dg2_ref[...] = p_g2
            loss_ref[...] = p_l

        @pl.when(i > 0)
        def _():
            dg_ref[...] += p_g
            dg2_ref[...] += p_g2
            loss_ref[...] += p_l

    vec = jax.ShapeDtypeStruct((1, D), F32)
    return _pcall(body, name="final_loss", grid=(S // TR,),
                  in_specs=[_row_spec(), _row_spec(), _vec_spec(), _row_spec(), _vec_spec()],
                  out_specs=(_vec_spec(128), _row_spec(), _vec_spec(), _row_spec(), _vec_spec()),
                  out_shape=(jax.ShapeDtypeStruct((1, 128), F32), jax.ShapeDtypeStruct((S, D), F32), vec,
                             jax.ShapeDtypeStruct((S, D), BF16), vec),
                  compiler_params=_params(("arbitrary",)))(x2, tgt, g, f, g2)


HALF = 512


def _merge_fwd(proj, ret_out, att_out):
    def body(ga_ref, gb_ref, r_ref, a_ref, o_ref):
        o_ref[...] = (jax.nn.sigmoid(ga_ref[...]) * r_ref[...] + jax.nn.sigmoid(gb_ref[...]) * a_ref[...]).astype(BF16)

    blk = lambda off: pl.BlockSpec((TR, HALF), lambda i, j: (i, off // HALF + j))
    return _pcall(body, name="merge_fwd", grid=(S // TR, D // HALF),
                  in_specs=[blk(OFF_GA), blk(OFF_GB), blk(0), blk(0)], out_specs=blk(0),
                  out_shape=jax.ShapeDtypeStruct((S, D), BF16),
                  compiler_params=_params(("parallel", "parallel")))(proj, proj, ret_out, att_out)


def _merge_bwd(proj, ret_out, att_out, dmerged):
    def body(ga_ref, gb_ref, r_ref, a_ref, dm_ref, dr_ref, da_ref, dga_ref, dgb_ref):
        sa = jax.nn.sigmoid(ga_ref[...])
        sb = jax.nn.sigmoid(gb_ref[...])
        dm = dm_ref[...]
        dr_ref[...] = (dm * sa).astype(BF16)
        da_ref[...] = (dm * sb).astype(BF16)
        dga_ref[...] = (dm * r_ref[...] * (sa * (1.0 - sa))).astype(BF16)
        dgb_ref[...] = (dm * a_ref[...] * (sb * (1.0 - sb))).astype(BF16)

    blk = lambda off: pl.BlockSpec((TR, HALF), lambda i, j: (i, off // HALF + j))
    o = jax.ShapeDtypeStruct((S, D), BF16)
    return _pcall(body, name="merge_bwd", grid=(S // TR, D // HALF),
                  in_specs=[blk(OFF_GA), blk(OFF_GB), blk(0), blk(0), blk(0)], out_specs=(blk(0),) * 4,
                  out_shape=(o, o, o, o),
                  compiler_params=_params(("parallel", "parallel")))(proj, proj, ret_out, att_out, dmerged)


def _ret_tables():
    H, C = RET_HEADS, CHUNK
    log_g = jnp.log1p(-(2.0 ** (-5.0 - jnp.arange(H, dtype=F32))))
    idx = jnp.arange(C, dtype=F32)
    rel = idx[:, None] - idx[None, :]
    inner = jnp.where(rel >= 0, jnp.exp(log_g[:, None, None] * jnp.maximum(rel, 0.0)), 0.0)
    qd = jnp.exp(log_g[:, None] * (idx + 1.0))[:, :, None]
    kd = jnp.exp(log_g[:, None] * (C - 1.0 - idx))[:, :, None]
    cd = jnp.broadcast_to(jnp.exp(log_g * C)[:, None, None], (H, 1, 128))
    half = RET_DK // 2
    inv = 10000.0 ** (-jnp.arange(half, dtype=F32) / half)
    ang = jnp.arange(S, dtype=F32)[:, None] * inv[None, :]
    return inner, qd, kd, cd, jnp.cos(ang), jnp.sin(ang)


def _rot(x, cos, sin):
    x1, x2 = x[:, :128], x[:, 128:]
    return jnp.concatenate([x1 * cos - x2 * sin, x1 * sin + x2 * cos], axis=1)


def _rot_t(d, cos, sin):
    d1, d2 = d[:, :128], d[:, 128:]
    return jnp.concatenate([d1 * cos + d2 * sin, d2 * cos - d1 * sin], axis=1)


RET_COLS = OFF_ATT
RET_VW = RET_HEADS * RET_DV


def _ret_specs(chunk_of):
    ci = chunk_of
    whole = lambda shape: pl.BlockSpec(shape, lambda t: (0,) * len(shape))
    return [
        pl.BlockSpec((CHUNK, RET_COLS), lambda t: (ci(t), 0)),
        pl.BlockSpec((CHUNK, 128), lambda t: (ci(t), 0)),
        pl.BlockSpec((CHUNK, 128), lambda t: (ci(t), 0)),
        whole((RET_HEADS, CHUNK, CHUNK)), whole((RET_HEADS, CHUNK, 1)), whole((RET_HEADS, CHUNK, 1)),
        whole((RET_HEADS, 1, 128)), whole((1, RET_VW)), whole((1, RET_VW)),
    ]


def _ret_cols(h):
    q = slice(OFF_RQ + h * RET_DK, OFF_RQ + (h + 1) * RET_DK)
    k = slice(OFF_RK + h * RET_DK, OFF_RK + (h + 1) * RET_DK)
    v = slice(OFF_RV + h * RET_DV, OFF_RV + (h + 1) * RET_DV)
    g = slice(OFF_RG + h * RET_DV, OFF_RG + (h + 1) * RET_DV)
    return q, k, v, g, slice(h * RET_DV, (h + 1) * RET_DV)


def _ret_fwd(proj, tables, gn_g, gn_b, comm=None):
    inner, qd, kd, cd, cos, sin = tables

    def body(x_ref, cos_ref, sin_ref, in_ref, qd_ref, kd_ref, cd_ref, g_ref, b_ref,
             gated_ref, ro_ref, st_ref, s_scr):
        i = pl.program_id(0)

        @pl.when(i == 0)
        def _():
            s_scr[...] = jnp.zeros_like(s_scr)

        cosv, sinv = cos_ref[...], sin_ref[...]
        for h in range(RET_HEADS):
            cq, ck, cv, cg, co = _ret_cols(h)
            q = _rot(x_ref[:, cq], cosv, sinv)
            k = _rot(x_ref[:, ck], cosv, sinv) * (RET_DK ** -0.5)
            v = x_ref[:, cv]
            st = s_scr[h]
            st_ref[h] = st.astype(BF16)
            s = _dot(q, k, NT) * in_ref[h]
            o = _dot(s, v, NN) + _dot(q, st, NN) * qd_ref[h]
            s_scr[h] = st * cd_ref[h, :, :1] + _dot(k * kd_ref[h], v, TN)
            ro_ref[:, co] = o
            mu = jnp.mean(o, axis=-1, keepdims=True)
            oc = o - mu
            var = jnp.mean(oc * oc, axis=-1, keepdims=True)
            rn = oc * lax.rsqrt(var + GN_EPS) * g_ref[:, co] + b_ref[:, co]
            rg = x_ref[:, cg]
            gated_ref[:, co] = (rg * jax.nn.sigmoid(rg) * rn).astype(BF16)

    ospec = pl.BlockSpec((CHUNK, RET_VW), lambda t: (t, 0))
    kw = dict(name="ret_fwd", grid=(N_CHUNK,), in_specs=_ret_specs(lambda t: t),
              out_specs=(ospec, ospec, pl.BlockSpec((RET_HEADS, None, RET_DK, RET_DV), lambda t: (0, t, 0, 0))),
              out_shape=(jax.ShapeDtypeStruct((S, RET_VW), BF16), jax.ShapeDtypeStruct((S, RET_VW), F32),
                         jax.ShapeDtypeStruct((RET_HEADS, N_CHUNK, RET_DK, RET_DV), BF16)),
              scratch_shapes=[pltpu.VMEM((RET_HEADS, RET_DK, RET_DV), F32)])
    args = (proj, cos, sin, inner, qd, kd, cd, gn_g, gn_b)
    if comm is not None:
        return _carry(body, comm, **kw)(*args)
    return _pcall(body, compiler_params=_params(("arbitrary",)), **kw)(*args)


def _ret_bwd(proj, tables, gn_g, gn_b, ro, states, dgated, comm=None):
    inner, qd, kd, cd, cos, sin = tables
    last = N_CHUNK - 1

    def body(x_ref, cos_ref, sin_ref, in_ref, qd_ref, kd_ref, cd_ref, g_ref, b_ref, ro_ref, st_ref, dg_ref,
             dx_ref, gg_ref, gb_ref, gs_scr):
        t = pl.program_id(0)

        @pl.when(t == 0)
        def _():
            gs_scr[...] = jnp.zeros_like(gs_scr)
            gg_ref[...] = jnp.zeros_like(gg_ref)
            gb_ref[...] = jnp.zeros_like(gb_ref)

        cosv, sinv = cos_ref[...], sin_ref[...]
        for h in range(RET_HEADS):
            cq, ck, cv, cg, co = _ret_cols(h)
            q = _rot(x_ref[:, cq], cosv, sinv)
            k = _rot(x_ref[:, ck], cosv, sinv) * (RET_DK ** -0.5)
            v = x_ref[:, cv]
            qdv, kdv, dm = qd_ref[h], kd_ref[h], in_ref[h]
            st = st_ref[h]
            o = ro_ref[:, co]
            gv = g_ref[:, co]
            mu = jnp.mean(o, axis=-1, keepdims=True)
            oc = o - mu
            rstd = lax.rsqrt(jnp.mean(oc * oc, axis=-1, keepdims=True) + GN_EPS)
            ohat = oc * rstd
            rn = ohat * gv + b_ref[:, co]
            rg = x_ref[:, cg]
            sg = jax.nn.sigmoid(rg)
            dgt = dg_ref[:, co]
            drn = dgt * (rg * sg)
            dx_ref[:, cg] = (dgt * rn * (sg * (1.0 + rg * (1.0 - sg)))).astype(BF16)
            gg_ref[:, co] += jnp.sum(drn * ohat, axis=0, keepdims=True)
            gb_ref[:, co] += jnp.sum(drn, axis=0, keepdims=True)
            dohat = drn * gv
            do = rstd * (dohat - jnp.mean(dohat, axis=-1, keepdims=True)
                         - ohat * jnp.mean(dohat * ohat, axis=-1, keepdims=True))
            gs = gs_scr[h]
            s = _dot(q, k, NT) * dm
            dsr = _dot(do, v, NT) * dm
            dq = _dot(dsr, k, NN) + _dot(do, st, NT) * qdv
            dk = _dot(dsr, q, TN) + _dot(v, gs, NT) * kdv
            dv = _dot(s, do, TN) + _dot(k * kdv, gs, NN)
            gs_scr[h] = gs * cd_ref[h, :, :1] + _dot(q * qdv, do, TN)
            dx_ref[:, cq] = _rot_t(dq, cosv, sinv).astype(BF16)
            dx_ref[:, ck] = (_rot_t(dk, cosv, sinv) * (RET_DK ** -0.5)).astype(BF16)
            dx_ref[:, cv] = dv.astype(BF16)

    rev = lambda t: last - t
    vblk = pl.BlockSpec((CHUNK, RET_VW), lambda t: (rev(t), 0))
    vspec = pl.BlockSpec((1, RET_VW), lambda t: (0, 0))
    kw = dict(name="ret_bwd", grid=(N_CHUNK,),
              in_specs=_ret_specs(rev) + [vblk, pl.BlockSpec((RET_HEADS, None, RET_DK, RET_DV),
                                                             lambda t: (0, rev(t), 0, 0)), vblk],
              out_specs=(pl.BlockSpec((CHUNK, RET_COLS), lambda t: (rev(t), 0)), vspec, vspec),
              out_shape=(jax.ShapeDtypeStruct((S, RET_COLS), BF16), jax.ShapeDtypeStruct((1, RET_VW), F32),
                         jax.ShapeDtypeStruct((1, RET_VW), F32)),
              scratch_shapes=[pltpu.VMEM((RET_HEADS, RET_DK, RET_DV), F32)])
    args = (proj, cos, sin, inner, qd, kd, cd, gn_g, gn_b, ro, states, dgated)
    if comm is not None:
        return _carry(body, comm, **kw)(*args)
    return _pcall(body, compiler_params=_params(("arbitrary",)), **kw)(*args)


def _bucket_tables():
    qi = np.arange(ATT_BLK)[:, None]
    kj = np.arange(2 * ATT_BLK)[None, :]
    m = ATT_BLK + qi - kj
    out = []
    for win, dil in ATT_GROUPS:
        w = win // dil
        dist = (np.clip(m, 0, w) * dil).astype(np.int32)
        max_exact = N_BUCKETS // 2
        d_f = np.maximum(dist, 1).astype(np.float32)
        large = max_exact + (np.log(d_f / np.float32(max_exact)) / np.float32(math.log(MAX_DIST / max_exact))
                             * np.float32(N_BUCKETS - max_exact)).astype(np.int32)
        large = np.minimum(large, N_BUCKETS - 1)
        out.append(np.where(dist < max_exact, dist, large).astype(np.int32))
    return np.stack(out)


def _bias_build(rel_bias, buckets):
    def body(tab_ref, bk_ref, o_ref):
        hh = pl.program_id(0)
        bk = bk_ref[...]
        acc = jnp.zeros((ATT_BLK, 2 * ATT_BLK), F32)
        for b in range(N_BUCKETS):
            acc = jnp.where(bk == b, tab_ref[b, hh], acc)
        o_ref[...] = acc

    nh = len(ATT_GROUPS) * ATT_HG
    return _pcall(body, name="bias_build", grid=(nh,),
                  in_specs=[pl.BlockSpec(memory_space=pltpu.SMEM),
                            pl.BlockSpec((None, ATT_BLK, 2 * ATT_BLK), lambda hh: (hh // ATT_HG, 0, 0))],
                  out_specs=pl.BlockSpec((None, ATT_BLK, 2 * ATT_BLK), lambda hh: (hh, 0, 0)),
                  out_shape=jax.ShapeDtypeStruct((nh, ATT_BLK, 2 * ATT_BLK), F32),
                  compiler_params=_params(("parallel",)))(rel_bias, buckets)


def _bias_grad(ds_sum, buckets):
    def body(ds_ref, bk_ref, o_ref):
        bk = bk_ref[...]
        ds = ds_ref[...]
        rows = lax.broadcasted_iota(jnp.int32, (N_BUCKETS, 128), 0)
        acc = jnp.zeros((N_BUCKETS, 128), F32)
        for b in range(N_BUCKETS):
            acc = jnp.where(rows == b, jnp.sum(jnp.where(bk == b, ds, 0.0)), acc)
        o_ref[...] = acc

    nh = len(ATT_GROUPS) * ATT_HG
    return _pcall(body, name="bias_grad", grid=(nh,),
                  in_specs=[pl.BlockSpec((None, ATT_BLK, 2 * ATT_BLK), lambda hh: (hh, 0, 0)),
                            pl.BlockSpec((None, ATT_BLK, 2 * ATT_BLK), lambda hh: (hh // ATT_HG, 0, 0))],
                  out_specs=pl.BlockSpec((None, N_BUCKETS, 128), lambda hh: (hh, 0, 0)),
                  out_shape=jax.ShapeDtypeStruct((nh, N_BUCKETS, 128), F32),
                  compiler_params=_params(("parallel",)))(ds_sum, buckets)


def _att_valid(n):
    qi = lax.broadcasted_iota(jnp.int32, (ATT_BLK, 2 * ATT_BLK), 0)
    kj = lax.broadcasted_iota(jnp.int32, (ATT_BLK, 2 * ATT_BLK), 1)
    m = ATT_BLK + qi - kj
    first_key = jnp.where(n > 0, 0, ATT_BLK)
    return (m >= 0) & (m <= ATT_BLK) & (kj >= first_key)


ATT_HP = (1, 2, 2)


def _att_geometry(gi):
    _, dil = ATT_GROUPS[gi]
    return dil, S // dil // ATT_BLK, ATT_HP[gi]


def _blk(dil, r, n):
    if dil == 1:
        return pl.ds(n * ATT_BLK, ATT_BLK)
    return pl.ds(r + n * ATT_BLK * dil, ATT_BLK, stride=dil)


def _slab_specs(gi):
    _, _, hp = _att_geometry(gi)
    per = ATT_HG // hp
    return [pl.BlockSpec((hp, S, ATT_DH), lambda g, r, part=part: ((3 * gi + part) * per + g, 0, 0))
            for part in range(3)]


def _head_specs(gi, count):
    _, _, hp = _att_geometry(gi)
    return [pl.BlockSpec((hp, S, ATT_DH), lambda g, r: (g, 0, 0))] * count


def _bias_spec(gi):
    _, _, hp = _att_geometry(gi)
    return pl.BlockSpec((hp, ATT_BLK, 2 * ATT_BLK), lambda g, r: (gi * (ATT_HG // hp) + g, 0, 0))


def _att_valid_first():
    qi = lax.broadcasted_iota(jnp.int32, (ATT_BLK, ATT_BLK), 0)
    kj = lax.broadcasted_iota(jnp.int32, (ATT_BLK, ATT_BLK), 1)
    return kj <= qi


def _att_fwd(slabs, bias, gi, comm=None):
    dil, nb, hp = _att_geometry(gi)
    scale = ATT_DH ** -0.5

    def body(q_ref, k_ref, v_ref, bias_ref, o_ref, l_ref):
        r = pl.program_id(1)
        for n in range(nb):
            cur = _blk(dil, r, n)
            valid = _att_valid(n) if n > 0 else _att_valid_first()
            for h in range(hp):
                if n > 0:
                    prev = _blk(dil, r, n - 1)
                    kk = jnp.concatenate([k_ref[h, prev, :], k_ref[h, cur, :]], axis=0)
                    vv = jnp.concatenate([v_ref[h, prev, :], v_ref[h, cur, :]], axis=0)
                    bias = bias_ref[h]
                else:
                    kk, vv, bias = k_ref[h, cur, :], v_ref[h, cur, :], bias_ref[h, :, pl.ds(ATT_BLK, ATT_BLK)]
                s = _dot(q_ref[h, cur, :], kk, NT) * scale + bias
                s = jnp.where(valid, s, -1e30)
                mx = jnp.max(s, axis=-1, keepdims=True)
                e = jnp.exp(s - mx)
                den = jnp.sum(e, axis=-1, keepdims=True)
                o_ref[h, cur, :] = _dot(e / den, vv, NN)
                l_ref[h, cur, :] = jnp.broadcast_to(mx + jnp.log(den), (ATT_BLK, ATT_DH))

    osh = jax.ShapeDtypeStruct((ATT_HG, S, ATT_DH), F32)
    kw = dict(name=f"att_fwd{gi}", grid=(ATT_HG // hp, dil), in_specs=_slab_specs(gi) + [_bias_spec(gi)],
              out_specs=tuple(_head_specs(gi, 2)), out_shape=(osh, osh))
    if comm is not None:
        return _carry(body, comm, **kw)(slabs, slabs, slabs, bias)
    return _pcall(body, compiler_params=_params(("parallel", "arbitrary")), **kw)(slabs, slabs, slabs, bias)


def _att_bwd(slabs, bias, o, lse, do, dlse, gi, comm=None):
    dil, nb, hp = _att_geometry(gi)
    per = ATT_HG // hp
    scale = ATT_DH ** -0.5
    wh = hp * ATT_DH
    wide = lambda t: jnp.concatenate([t, t], axis=1)

    def body(q_ref, k_ref, v_ref, bias_ref, o_ref, l_ref, do_ref, dl_ref, dq_ref, dk_ref, dv_ref, ds_ref):
        r = pl.program_id(1)

        @pl.when(r == 0)
        def _():
            ds_ref[...] = jnp.zeros_like(ds_ref)

        for h in range(hp):
            sl = slice(h * ATT_DH, (h + 1) * ATT_DH)
            carry_k = carry_v = None
            for n in range(nb):
                cur = _blk(dil, r, n)
                q = q_ref[h, cur, :]
                dov = do_ref[h, cur, :]
                delta = jnp.sum(dov * o_ref[h, cur, :], axis=-1, keepdims=True)
                out_rows = pl.ds(n * ATT_BLK, ATT_BLK)
                if n == 0:
                    own = pl.ds(ATT_BLK, ATT_BLK)
                    kk, vv = k_ref[h, cur, :], v_ref[h, cur, :]
                    s = _dot(q, kk, NT) * scale + bias_ref[h, :, own]
                    p = jnp.where(_att_valid_first(), jnp.exp(s - l_ref[h, cur, :]), 0.0)
                    ds = p * (_dot(dov, vv, NT) - delta + dl_ref[h, cur, :])
                    ds_ref[h, :, own] += ds
                    dq_ref[out_rows, sl] = (_dot(ds, kk, NN) * scale).astype(BF16)
                    carry_k, carry_v = _dot(ds, q, TN) * scale, _dot(p, dov, TN)
                    continue
                prev = _blk(dil, r, n - 1)
                kk = jnp.concatenate([k_ref[h, prev, :], k_ref[h, cur, :]], axis=0)
                vv = jnp.concatenate([v_ref[h, prev, :], v_ref[h, cur, :]], axis=0)
                s = _dot(q, kk, NT) * scale + bias_ref[h]
                p = jnp.where(_att_valid(n), jnp.exp(s - wide(l_ref[h, cur, :])), 0.0)
                dp = _dot(dov, vv, NT)
                ds = p * (dp - delta + wide(dl_ref[h, cur, :]))
                ds_ref[h] += ds
                dq_ref[out_rows, sl] = (_dot(ds, kk, NN) * scale).astype(BF16)
                dkk = _dot(ds, q, TN) * scale
                dvv = _dot(p, dov, TN)
                before = pl.ds((n - 1) * ATT_BLK, ATT_BLK)
                dk_ref[before, sl] = (carry_k + dkk[:ATT_BLK]).astype(BF16)
                dv_ref[before, sl] = (carry_v + dvv[:ATT_BLK]).astype(BF16)
                carry_k, carry_v = dkk[ATT_BLK:], dvv[ATT_BLK:]
            last = pl.ds((nb - 1) * ATT_BLK, ATT_BLK)
            dk_ref[last, sl] = carry_k.astype(BF16)
            dv_ref[last, sl] = carry_v.astype(BF16)

    out_spec = pl.BlockSpec((S // dil, wh), lambda g, r: (0, r * per + g))
    osh = jax.ShapeDtypeStruct((S // dil, dil * AW), BF16)
    kw = dict(name=f"att_bwd{gi}", grid=(per, dil), in_specs=_slab_specs(gi) + [_bias_spec(gi)] + _head_specs(gi, 4),
              out_specs=(out_spec, out_spec, out_spec,
                         pl.BlockSpec((hp, ATT_BLK, 2 * ATT_BLK), lambda g, r: (g, 0, 0))),
              out_shape=(osh, osh, osh, jax.ShapeDtypeStruct((ATT_HG, ATT_BLK, 2 * ATT_BLK), F32)))
    args = (slabs, slabs, slabs, bias, o, lse, do, dlse)
    if comm is not None:
        return _carry(body, comm, **kw)(*args)
    return _pcall(body, compiler_params=_params(("arbitrary", "arbitrary")), **kw)(*args)


AW = ATT_HG * ATT_DH


def _mix_weights(l0, l1, l2):
    mx = jnp.maximum(jnp.maximum(l0, l1), l2)
    e0, e1, e2 = jnp.exp(l0 - mx), jnp.exp(l1 - mx), jnp.exp(l2 - mx)
    den = e0 + e1 + e2
    return e0 / den, e1 / den, e2 / den


def _heads_spec():
    return pl.BlockSpec((ATT_HG, TR, ATT_DH), lambda i: (0, i, 0))


def _mix_fwd(os_, ls, comm=None):
    def body(o0, o1, o2, l0, l1, l2, att_ref):
        for h in range(ATT_HG):
            w0, w1, w2 = _mix_weights(l0[h], l1[h], l2[h])
            att_ref[:, h * ATT_DH:(h + 1) * ATT_DH] = (w0 * o0[h] + w1 * o1[h] + w2 * o2[h]).astype(BF16)

    kw = dict(name="mix_fwd", grid=(S // TR,), in_specs=[_heads_spec()] * 6, out_specs=_row_spec(AW),
              out_shape=jax.ShapeDtypeStruct((S, AW), BF16))
    if comm is not None:
        return _carry(body, comm, **kw)(*os_, *ls)
    return _pcall(body, compiler_params=_params(("parallel",)), **kw)(*os_, *ls)


def _mix_bwd(os_, ls, datt):
    def body(o0, o1, o2, l0, l1, l2, da_ref, d0, d1, d2, e0, e1, e2):
        for h in range(ATT_HG):
            ws = _mix_weights(l0[h], l1[h], l2[h])
            da = da_ref[:, h * ATT_DH:(h + 1) * ATT_DH]
            dws = []
            for o_ref, w, d_ref in zip((o0, o1, o2), ws, (d0, d1, d2)):
                d_ref[h] = w * da
                dws.append(jnp.broadcast_to(jnp.sum(da * o_ref[h], axis=-1, keepdims=True), (TR, ATT_DH)))
            tot = ws[0] * dws[0] + ws[1] * dws[1] + ws[2] * dws[2]
            for w, dw, e_ref in zip(ws, dws, (e0, e1, e2)):
                e_ref[h] = w * (dw - tot)

    o = jax.ShapeDtypeStruct((ATT_HG, S, ATT_DH), F32)
    return _pcall(body, name="mix_bwd", grid=(S // TR,), in_specs=[_heads_spec()] * 6 + [_row_spec(AW)],
                  out_specs=(_heads_spec(),) * 6, out_shape=(o,) * 6,
                  compiler_params=_params(("parallel",)))(*os_, *ls, datt)


def _ada_fwd(c_all, w_sh, b_sl):
    def body(c_ref, w_ref, b_ref, o_ref):
        cv = c_ref[...]
        o_ref[...] = _dot(cv * jax.nn.sigmoid(cv), w_ref[...], NN) + b_ref[...]

    return _pcall(body, name="ada_fwd", out_shape=jax.ShapeDtypeStruct((N_DEV, w_sh.shape[1]), F32),
                  compiler_params=_params())(c_all, w_sh, b_sl)


def _ada_bwd(c_all, dm_sl):
    def body(c_ref, d_ref, o_ref):
        cv = c_ref[...]
        o_ref[...] = _dot(cv * jax.nn.sigmoid(cv), d_ref[...], TN)

    return _pcall(body, name="ada_bwd", out_shape=jax.ShapeDtypeStruct((D, dm_sl.shape[1]), F32),
                  compiler_params=_params())(c_all, dm_sl)


N_MOD = 6


def _sum_small(gathered):
    n = len(gathered)

    def body(*refs):
        ins, (gb_ref, dm_ref), outs = refs[:n], refs[n:n + 2], refs[n + 2:]

        def total(r):
            acc = r[0]
            for e in range(1, N_DEV):
                acc = acc + r[e]
            return acc

        for i in range(N_MOD):
            cols = slice(i * D, (i + 1) * D)
            gb_ref[:, cols] = total(ins[i])
            for e in range(N_DEV):
                dm_ref[e:e + 1, cols] = ins[i][e]
        for r, o_ref in zip(ins[N_MOD:], outs):
            o_ref[...] = total(r)

    shapes = (jax.ShapeDtypeStruct((1, N_MOD * D), F32), jax.ShapeDtypeStruct((N_DEV, N_MOD * D), F32),
              *[jax.ShapeDtypeStruct(g.shape[1:], F32) for g in gathered[N_MOD:]])
    res = _pcall(body, name="sum_small", out_shape=shapes, compiler_params=_params())(*gathered)
    return res[0], res[1], res[2:]


def _row_tile(m, n):
    t = max(8, min(m, (1 << 19) // n // 8 * 8))
    while m % t:
        t -= 8
    return t


def _pair_sum(full, recv, sel, name, col_block=0):
    _, m, n = recv.shape
    t = _row_tile(m, n)

    def body(sel_ref, a_ref, b_ref, o_ref):
        o_ref[...] = (a_ref[...].astype(F32) + b_ref[...].astype(F32)).astype(o_ref.dtype)

    gs = pltpu.PrefetchScalarGridSpec(
        num_scalar_prefetch=1, grid=(4, m // t),
        in_specs=[pl.BlockSpec((None, None, t, n), lambda q, i, s: (q, s[0], i, col_block)),
                  pl.BlockSpec((None, t, n), lambda q, i, s: (q, i, 0))],
        out_specs=pl.BlockSpec((None, t, n), lambda q, i, s: (q, i, 0)))
    return _pcall(body, name=name, grid_spec=gs, out_shape=jax.ShapeDtypeStruct((4, m, n), full.dtype),
                  compiler_params=_params(("parallel", "parallel")))(sel, full, recv)


def _chip_sum(part, recv, sel, name):
    _, m, n = part.shape
    t = _row_tile(m, n)

    def body(sel_ref, a_ref, r_ref, o_ref):
        o_ref[...] = ((a_ref[...].astype(F32) + r_ref[0].astype(F32)) + r_ref[1].astype(F32)) + r_ref[2].astype(F32)

    gs = pltpu.PrefetchScalarGridSpec(
        num_scalar_prefetch=1, grid=(m // t,),
        in_specs=[pl.BlockSpec((None, t, n), lambda i, s: (s[0], i, 0)),
                  pl.BlockSpec((3, t, n), lambda i, s: (0, i, 0))],
        out_specs=pl.BlockSpec((t, n), lambda i, s: (i, 0)))
    return _pcall(body, name=name, grid_spec=gs, out_shape=jax.ShapeDtypeStruct((m, n), F32),
                  compiler_params=_params(("parallel",)))(sel, part, recv)


def _adamw_math(w, g, m, v):
    nm = ADAM_B1 * m + (1.0 - ADAM_B1) * g
    nv = ADAM_B2 * v + (1.0 - ADAM_B2) * (g * g)
    m_hat = nm / (1.0 - ADAM_B1 ** ADAM_STEP)
    v_hat = nv / (1.0 - ADAM_B2 ** ADAM_STEP)
    return -ADAM_LR * (m_hat / (jnp.sqrt(v_hat) + ADAM_EPS) + ADAM_WD * w), nm, nv


def _adamw(w, g, m, v, name):
    _, rows, cols = w.shape
    t = _row_tile(rows, cols)

    def body(w_ref, g_ref, m_ref, v_ref, d_ref, nm_ref, nv_ref):
        d_ref[...], nm_ref[...], nv_ref[...] = _adamw_math(w_ref[...], g_ref[...], m_ref[...], v_ref[...])

    spec3 = pl.BlockSpec((None, t, cols), lambda i: (0, i, 0))
    spec2 = pl.BlockSpec((t, cols), lambda i: (i, 0))
    o = jax.ShapeDtypeStruct(w.shape, F32)
    return _pcall(body, name=name, grid=(rows // t,), in_specs=[spec3, spec2, spec3, spec3], out_specs=(spec3,) * 3,
                  out_shape=(o, o, o), compiler_params=_params(("parallel",)))(w, g, m, v)


def _adamw_reduced(w, m, v, parts, recvs, sel):
    _, rows, cols = w.shape
    half = cols // 2
    t = _row_tile(rows, half)

    def body(sel_ref, w_ref, m_ref, v_ref, pa_ref, pb_ref, ra_ref, rb_ref, g_ref, d_ref, nm_ref, nv_ref):
        total = lambda p_ref, r_ref: ((p_ref[...].astype(F32) + r_ref[0].astype(F32)) + r_ref[1].astype(F32)) \
            + r_ref[2].astype(F32)
        g = jnp.where(pl.program_id(1) == 0, total(pa_ref, ra_ref), total(pb_ref, rb_ref))
        g_ref[...] = g
        d_ref[...], nm_ref[...], nv_ref[...] = _adamw_math(w_ref[...], g, m_ref[...], v_ref[...])

    wspec = pl.BlockSpec((None, t, half), lambda i, j, s: (0, i, j))
    pspec = pl.BlockSpec((None, t, half), lambda i, j, s: (s[0], i, 0))
    rspec = pl.BlockSpec((3, t, half), lambda i, j, s: (0, i, 0))
    gs = pltpu.PrefetchScalarGridSpec(num_scalar_prefetch=1, grid=(rows // t, 2),
                                      in_specs=[wspec, wspec, wspec, pspec, pspec, rspec, rspec],
                                      out_specs=(wspec,) * 4)
    o = jax.ShapeDtypeStruct(w.shape, F32)
    return _pcall(body, name="adamw_w_in", grid_spec=gs, out_shape=(o, o, o, o),
                  compiler_params=_params(("parallel", "arbitrary")))(sel, w, m, v, *parts, *recvs)


def _adamw_small(ws, gs, ms, vs):
    n = len(ws)

    def body(*refs):
        for i in range(n):
            w_ref, g_ref, m_ref, v_ref = (refs[k * n + i] for k in range(4))
            d, nm, nv = _adamw_math(w_ref[...], g_ref[...], m_ref[...], v_ref[...])
            refs[4 * n + i][...] = d
            refs[5 * n + i][...] = nm
            refs[6 * n + i][...] = nv

    shapes = tuple(jax.ShapeDtypeStruct(w.shape, F32) for w in ws)
    res = _pcall(body, name="adamw_small", out_shape=shapes * 3, compiler_params=_params())(*ws, *gs, *ms, *vs)
    return res[:n], res[n:2 * n], res[2 * n:]


def _mesh_pos():
    return lax.axis_index("x"), lax.axis_index("y"), lax.axis_index("c")


class _Gather:
    def __init__(self, arrs, relay=False, chunks=None):
        self.relay = relay
        self.ins = list(arrs)
        self.out_shape = tuple(jax.ShapeDtypeStruct((N_DEV,) + a.shape, a.dtype) for a in arrs)
        self.pieces = []
        for a, arr in enumerate(arrs):
            n = (chunks or {}).get(a, 1)
            rows = arr.shape[0] // n
            self.pieces += [(a, None if n == 1 else pl.ds(i * rows, rows)) for i in range(n)]
        npc = len(self.pieces)
        self.sems = [pltpu.SemaphoreType.DMA((7 * npc,)), pltpu.SemaphoreType.DMA((7 * npc,)),
                     pltpu.SemaphoreType.DMA((npc,))]

    def _copies(self, ins, outs, sems):
        send_sems, recv_sems, local_sems = sems
        x, y, c = _mesh_pos()
        me, sibling = (x, y, c), (x, y, 1 - c)
        chips = [(1 - x, y), (x, 1 - y), (1 - x, 1 - y)]

        def src_of(p):
            a, rows = self.pieces[p]
            return ins[a] if rows is None else ins[a].at[rows]

        def dst_of(p, block):
            a, rows = self.pieces[p]
            ref = outs[a].at[_slot(block)]
            return ref if rows is None else ref.at[rows]

        def copy(p, k, block, to, from_input=False):
            dst = dst_of(p, block)
            return pltpu.make_async_remote_copy(
                src_ref=src_of(p) if from_input else dst, dst_ref=dst, send_sem=send_sems.at[7 * p + k],
                recv_sem=recv_sems.at[7 * p + k], device_id=to, device_id_type=MESH)

        npc = len(self.pieces)
        mine = [pltpu.make_async_copy(src_of(p), dst_of(p, me), local_sems.at[p]) for p in range(npc)]
        direct = chips[:2] if self.relay else chips
        first = []
        for p in range(npc):
            first.append(copy(p, 0, me, sibling, from_input=True))
            first += [copy(p, 1 + j, me, (*chip, c), from_input=True) for j, chip in enumerate(direct)]
        return me, sibling, chips, c, copy, mine, first

    def start(self, ins, outs, sems):
        *_, mine, first = self._copies(ins, outs, sems)
        for cp in mine + first:
            cp.start()

    def finish(self, ins, outs, sems):
        me, sibling, chips, c, copy, mine, first = self._copies(ins, outs, sems)
        x, y = me[0], me[1]
        npc = len(self.pieces)
        passed = []

        def pass_on(cp):
            cp.start()
            passed.append(cp)

        for p in range(npc):
            for j in range(2):
                copy(p, 1 + j, (*chips[j], c), me).wait_recv()
                pass_on(copy(p, 4 + j, (*chips[j], c), sibling))
            if self.relay:
                owner = ((x + 1 - c) % 2, (y + c) % 2, c)
                pass_on(copy(p, 3, owner, ((x + c) % 2, (y + 1 - c) % 2, c)))
        for p in range(npc):
            copy(p, 3, (*chips[2], c), me).wait_recv()
            pass_on(copy(p, 6, (*chips[2], c), sibling))
        for p in range(npc):
            copy(p, 0, sibling, me).wait_recv()
            for j, chip in enumerate(chips):
                copy(p, 4 + j, (*chip, 1 - c), me).wait_recv()
        for cp in first + passed:
            cp.wait_send()
        for cp in mine:
            cp.wait()


class _ExchangeCore:
    def __init__(self, fulls, cols=None):
        self.ins = list(fulls)
        self.cols = cols
        width = lambda f: f.shape[3] if cols is None else cols[1]
        self.out_shape = tuple(jax.ShapeDtypeStruct((4, f.shape[2], width(f)), f.dtype) for f in fulls)
        self.sems = [pltpu.SemaphoreType.DMA((4 * len(fulls),)), pltpu.SemaphoreType.DMA((4 * len(fulls),))]

    def _copies(self, ins, outs, sems):
        send_sems, recv_sems = sems
        x, y, c = _mesh_pos()

        def src(a, q):
            ref = ins[a].at[q, 1 - c]
            return ref if self.cols is None else ref.at[:, pl.ds(*self.cols)]

        return [pltpu.make_async_remote_copy(
            src_ref=src(a, q), dst_ref=outs[a].at[q], send_sem=send_sems.at[4 * a + q],
            recv_sem=recv_sems.at[4 * a + q], device_id=(x, y, 1 - c), device_id_type=MESH)
            for a in range(len(self.ins)) for q in range(4)]

    def start(self, ins, outs, sems):
        for cp in self._copies(ins, outs, sems):
            cp.start()

    def finish(self, ins, outs, sems):
        for cp in self._copies(ins, outs, sems):
            cp.wait()


class _ExchangeChip:
    def __init__(self, parts):
        self.ins = list(parts)
        self.out_shape = tuple(jax.ShapeDtypeStruct((3,) + p.shape[1:], p.dtype) for p in parts)
        self.sems = [pltpu.SemaphoreType.DMA((3 * len(parts),)), pltpu.SemaphoreType.DMA((3 * len(parts),))]

    def _copies(self, ins, outs, sems):
        send_sems, recv_sems = sems
        x, y, c = _mesh_pos()
        chips = [(1 - x, y), (x, 1 - y), (1 - x, 1 - y)]
        return [pltpu.make_async_remote_copy(
            src_ref=ins[a].at[2 * px + py], dst_ref=outs[a].at[j], send_sem=send_sems.at[3 * a + j],
            recv_sem=recv_sems.at[3 * a + j], device_id=(px, py, c), device_id_type=MESH)
            for a in range(len(self.ins)) for j, (px, py) in enumerate(chips)]

    def start(self, ins, outs, sems):
        for cp in self._copies(ins, outs, sems):
            cp.start()

    def finish(self, ins, outs, sems):
        for cp in self._copies(ins, outs, sems):
            cp.wait()


HBM_ONLY = pl.BlockSpec(memory_space=pltpu.HBM)
SEM_SPEC = pl.BlockSpec(memory_space=pltpu.SEMAPHORE)
SIDE_EFFECT = pltpu.SideEffectType.DATAFLOW_SIDE_EFFECTING


def _chip_copies(p_refs, land_refs, send_sems, recv_sems):
    x, y, c = _mesh_pos()
    return [pltpu.make_async_remote_copy(
        src_ref=p_refs[a].at[2 * px + py], dst_ref=land_refs[a].at[j], send_sem=send_sems.at[3 * a + j],
        recv_sem=recv_sems.at[3 * a + j], device_id=(px, py, c), device_id_type=MESH)
        for a in range(len(p_refs)) for j, (px, py) in enumerate([(1 - x, y), (x, 1 - y), (1 - x, 1 - y)])]


def _chip_exchange_start(parts, name):
    n = len(parts)
    lands = [lax.empty((3,) + p.shape[1:], p.dtype) for p in parts]

    def body(*refs):
        p_refs, land_refs, (send_sems, recv_sems) = refs[:n], refs[n:2 * n], refs[2 * n:2 * n + 2]
        for cp in _chip_copies(p_refs, land_refs, send_sems, recv_sems):
            cp.start()
        token = refs[-1]
        token[...] = jnp.zeros_like(token)

    hbm = lambda t: pltpu.HBM(t.shape, t.dtype)
    res = pl.pallas_call(
        body, name=name,
        out_shape=(pltpu.SemaphoreType.DMA((3 * n,)), pltpu.SemaphoreType.DMA((3 * n,)), *[hbm(t) for t in parts + lands],
                   jax.ShapeDtypeStruct((8, 128), F32)),
        in_specs=(HBM_ONLY,) * (2 * n),
        out_specs=(SEM_SPEC, SEM_SPEC, *[HBM_ONLY] * (2 * n), pl.BlockSpec(memory_space=pltpu.VMEM)),
        input_output_aliases={i: 2 + i for i in range(2 * n)},
        compiler_params=pltpu.CompilerParams(has_side_effects=SIDE_EFFECT))(
        *[pltpu.with_memory_space_constraint(t, pltpu.HBM) for t in parts + lands])
    return (res[0], res[1], list(res[2:2 + n]), list(res[2 + n:2 + 2 * n])), res[-1]


def _chip_exchange_wait(in_flight, after, name):
    send_sems, recv_sems, parts, lands = in_flight
    n = len(parts)

    def body(*refs):
        p_refs, land_refs, (send_sems, recv_sems) = refs[:n], refs[n:2 * n], refs[2 * n:2 * n + 2]
        for cp in _chip_copies(p_refs, land_refs, send_sems, recv_sems):
            cp.wait_send()
            cp.wait_recv()

    res = pl.pallas_call(
        body, name=name, out_shape=tuple(pltpu.HBM(t.shape, t.dtype) for t in parts + lands),
        in_specs=(*[HBM_ONLY] * (2 * n), SEM_SPEC, SEM_SPEC, pl.BlockSpec(memory_space=pl.ANY)),
        out_specs=(HBM_ONLY,) * (2 * n), input_output_aliases={i: i for i in range(2 * n)},
        compiler_params=pltpu.CompilerParams(has_side_effects=SIDE_EFFECT))(*parts, *lands, send_sems, recv_sems, after)
    return list(res[:n]), list(res[n:])


def _slot(p):
    return 4 * p[0] + 2 * p[1] + p[2]


def _gather_copies(src_refs, out_refs, send_sems, recv_sems):
    x, y, c = _mesh_pos()
    targets = [(x, y, 1 - c), (1 - x, y, c), (x, 1 - y, c), (1 - x, 1 - y, c)]
    return [pltpu.make_async_remote_copy(
        src_ref=src_refs[a], dst_ref=out_refs[a].at[_slot((x, y, c))], send_sem=send_sems.at[4 * a + k],
        recv_sem=recv_sems.at[4 * a + k], device_id=to, device_id_type=MESH)
        for a in range(len(src_refs)) for k, to in enumerate(targets)]


def _gather_start(shards, after, name):
    n = len(shards)
    outs = [lax.empty((N_DEV,) + s.shape, s.dtype) for s in shards]

    def body(*refs):
        for cp in _gather_copies(refs[:n], refs[n:2 * n], refs[2 * n + 1], refs[2 * n + 2]):
            cp.start()
        token = refs[-1]
        token[...] = jnp.zeros_like(token)

    res = pl.pallas_call(
        body, name=name,
        out_shape=(pltpu.SemaphoreType.DMA((4 * n,)), pltpu.SemaphoreType.DMA((4 * n,)),
                   *[pltpu.HBM(t.shape, t.dtype) for t in shards + outs], jax.ShapeDtypeStruct((8, 128), F32)),
        in_specs=(*[HBM_ONLY] * (2 * n), pl.BlockSpec(memory_space=pl.ANY)),
        out_specs=(SEM_SPEC, SEM_SPEC, *[HBM_ONLY] * (2 * n), pl.BlockSpec(memory_space=pltpu.VMEM)),
        input_output_aliases={i: 2 + i for i in range(2 * n)},
        compiler_params=pltpu.CompilerParams(has_side_effects=SIDE_EFFECT))(
        *[pltpu.with_memory_space_constraint(t, pltpu.HBM) for t in shards + outs], after)
    return (res[0], res[1], list(res[2:2 + n]), list(res[2 + n:2 + 2 * n])), res[-1]


def _gather_wait(in_flight, after, name):
    send_sems, recv_sems, shards, outs = in_flight
    n = len(shards)

    def body(*refs):
        for cp in _gather_copies(refs[:n], refs[n:2 * n], refs[2 * n], refs[2 * n + 1]):
            cp.wait_send()
            cp.wait_recv()

    res = pl.pallas_call(
        body, name=name, out_shape=tuple(pltpu.HBM(t.shape, t.dtype) for t in shards + outs),
        in_specs=(*[HBM_ONLY] * (2 * n), SEM_SPEC, SEM_SPEC, pl.BlockSpec(memory_space=pl.ANY)),
        out_specs=(HBM_ONLY,) * (2 * n), input_output_aliases={i: i for i in range(2 * n)},
        compiler_params=pltpu.CompilerParams(has_side_effects=SIDE_EFFECT))(*shards, *outs, send_sems, recv_sems, after)
    return list(res[:n]), list(res[n:])


class _PassToSibling:
    def __init__(self, shards, gathered):
        n = self.n = len(shards)
        self.ins = list(shards) + list(gathered)
        self.out_shape = tuple(jax.ShapeDtypeStruct(g.shape, g.dtype) for g in gathered)
        self.aliases = {n + a: a for a in range(n)}
        self.sems = [pltpu.SemaphoreType.DMA((3 * n,)), pltpu.SemaphoreType.DMA((3 * n,)),
                     pltpu.SemaphoreType.DMA((n,))]

    def _copies(self, ins, outs, sems):
        send_sems, recv_sems, local_sems = sems
        x, y, c = _mesh_pos()
        chips = [(1 - x, y), (x, 1 - y), (1 - x, 1 - y)]
        mine = [pltpu.make_async_copy(ins[a], outs[a].at[_slot((x, y, c))], local_sems.at[a]) for a in range(self.n)]
        passed, awaited = [], []
        for a in range(self.n):
            for j, chip in enumerate(chips):
                sems_j = dict(send_sem=send_sems.at[3 * a + j], recv_sem=recv_sems.at[3 * a + j],
                              device_id=(x, y, 1 - c), device_id_type=MESH)
                blk = outs[a].at[_slot((*chip, c))]
                passed.append(pltpu.make_async_remote_copy(src_ref=blk, dst_ref=blk, **sems_j))
                got = outs[a].at[_slot((*chip, 1 - c))]
                awaited.append(pltpu.make_async_remote_copy(src_ref=got, dst_ref=got, **sems_j))
        return mine, passed, awaited

    def start(self, ins, outs, sems):
        mine, passed, _ = self._copies(ins, outs, sems)
        for cp in mine + passed:
            cp.start()

    def finish(self, ins, outs, sems):
        mine, passed, awaited = self._copies(ins, outs, sems)
        for cp in passed:
            cp.wait_send()
        for cp in awaited:
            cp.wait_recv()
        for cp in mine:
            cp.wait()


def _reduce_sums(fulls, recv_core, core, tag):
    return [_pair_sum(f, r, core, f"rs_pair_{tag}{i}") for i, (f, r) in enumerate(zip(fulls, recv_core))]


def _local_step(x, tgt, mods, w_in_t, shards, small, chip, core):
    sh1, sc1, g1, sh2, sc2, g2 = mods
    norm1_g, rel_bias, gn_g, gn_b, norm2_g, norm_f_g = small
    tables = _ret_tables()
    buckets = jnp.asarray(_bucket_tables())

    h1 = _norm_mod_fwd(x, norm1_g, sh1, sc1, "norm1_fwd")
    flight_w, token_w = _gather_start(list(shards), h1, "gather_w_start")
    proj, slabs = _proj(h1, w_in_t, token_w)
    gated, ro, states = _ret_fwd(proj, tables, gn_g, gn_b)
    bias = _bias_build(rel_bias, buckets)
    outs, lses = [], []
    for gi in range(len(ATT_GROUPS)):
        o, l = _att_fwd(slabs, bias, gi)
        outs.append(o)
        lses.append(l)
    att, gathered = _mix_fwd(outs, lses, comm=_PassToSibling(*_gather_wait(flight_w, lses[2], "gather_w_wait")))
    w_ret_out, w_att_out, w_o, w_ff1, w_ff2 = (_from_slots(g, ax) for g, ax in zip(gathered, BIG_AXES[1:]))
    ret_out = _mm(gated, w_ret_out, 'nn', tm=S, tn=256, tk=2048, name="ret_out")
    att_out = _mm(att, w_att_out, 'nn', tm=S, tn=512, tk=AW, name="att_out")
    merged = _merge_fwd(proj, ret_out, att_out)
    mixo, x1 = _mm(merged, w_o, 'nn', tm=S, tn=256, tk=D, name="w_o", res=x, gvec=g1)
    h2 = _norm_mod_fwd(x1, norm2_g, sh2, sc2, "norm2_fwd")
    u, act = _mm(h2, w_ff1, 'nn', tm=S, tn=512, tk=D, name="ff1", relu2=True)
    f, x2 = _mm(act, w_ff2, 'nn', tm=1024, tn=512, tk=D_FF, name="ff2", res=x1, gvec=g2)
    loss, dx2, g_normf, df, dg2 = _final_loss(x2, tgt, norm_f_g, f, g2)

    gw_ff2 = _mm(act, df, 'tn', tm=512, tn=D, tk=S, name="gw_ff2", out_dtype=BF16)
    du = _mm(df, w_ff2, 'nt', tm=S, tn=512, tk=D, name="d_act", out_dtype=BF16, relu2_of=u)
    gw_ff1 = _mm(h2, du, 'tn', tm=D, tn=512, tk=S, name="gw_ff1", out_dtype=BF16)
    fulls_a = [_to_slots(g, ax) for g, ax in zip((gw_ff1, gw_ff2), BIG_AXES[4:])]
    dh2, recv_core_a = _mm(du, w_ff1, 'nt', tm=1024, tn=1024, tk=2048, name="dh2", comm=_ExchangeCore(fulls_a))
    parts_a = _reduce_sums(fulls_a, recv_core_a, core, "a")
    flight_a, token_a = _chip_exchange_start(parts_a, "rs_a_start")
    dx1, dsc2, dsh2, g_norm2, dmixo, dg1 = _norm_mod_bwd(x1, norm2_g, sc2, dh2, dx2, "norm2_bwd", gate=(mixo, g1))

    gw_o = _mm(merged, dmixo, 'tn', tm=D, tn=512, tk=S, name="gw_o", out_dtype=BF16, after=token_a)
    dmerged = _mm(dmixo, w_o, 'nt', tm=S, tn=512, tk=D, name="dmerged")
    d_ret_out, d_att_out, dga, dgb = _merge_bwd(proj, ret_out, att_out, dmerged)
    gw_ret_out = _mm(gated, d_ret_out, 'tn', tm=512, tn=D, tk=S, name="gw_ret_out", out_dtype=BF16)
    gw_att_out = _mm(att, d_att_out, 'tn', tm=AW, tn=D, tk=S, name="gw_att_out", out_dtype=BF16)
    fulls_b = [_to_slots(g, ax) for g, ax in zip((gw_ret_out, gw_att_out, gw_o), BIG_AXES[1:4])]
    dgated, recv_core_b = _mm(d_ret_out, w_ret_out, 'nt', tm=S, tn=512, tk=D, name="dgated",
                              comm=_ExchangeCore(fulls_b))
    parts_b = _reduce_sums(fulls_b, recv_core_b, core, "b")
    flight_b, token_b = _chip_exchange_start(parts_b, "rs_b_start")
    datt = _mm(d_att_out, w_att_out, 'nt', tm=S, tn=AW, tk=D, name="datt", after=token_b)
    mix_grads = _mix_bwd(outs, lses, datt)
    datt_parts, ds_sums = [], []
    for gi in range(len(ATT_GROUPS)):
        dq, dk, dv, ds_sum = _att_bwd(slabs, bias, outs[gi], lses[gi], mix_grads[gi], mix_grads[3 + gi], gi)
        datt_parts += [dq.reshape(S, AW), dk.reshape(S, AW), dv.reshape(S, AW)]
        ds_sums.append(ds_sum)
    g_bias = _bias_grad(jnp.concatenate(ds_sums, axis=0), buckets)[:, :, 0].T.reshape(1, -1)
    dret, g_gn_g, g_gn_b = _ret_bwd(proj, tables, gn_g, gn_b, ro, states, dgated)
    parts_a, recv_chip_a = _chip_exchange_wait(flight_a, dret, "rs_a_wait")
    parts_b, recv_chip_b = _chip_exchange_wait(flight_b, dret, "rs_b_wait")
    red_a = [_chip_sum(p, r, chip, f"rs_sum_a{i}") for i, (p, r) in enumerate(zip(parts_a, recv_chip_a))]
    red_b = [_chip_sum(p, r, chip, f"rs_sum_b{i}") for i, (p, r) in enumerate(zip(parts_b, recv_chip_b))]
    dproj = jnp.concatenate([dret] + datt_parts + [dga, dgb], axis=1)
    full_in = _to_slots(_mm(dproj, h1, 'tn', tm=512, tn=D, tk=S, name="gw_in", out_dtype=BF16), 0)
    in_flight, token = [], None
    for half in range(2):
        (recv_core_in,) = _run_comm(_ExchangeCore([full_in], cols=(half * (D // 2), D // 2)), f"rs_core_in{half}",
                                    after=token)
        part_in = [_pair_sum(full_in, recv_core_in, core, f"rs_pair_c{half}", col_block=half)]
        flight, token = _chip_exchange_start(part_in, f"rs_in{half}_start")
        in_flight.append(flight)
    dh1 = _mm(dproj, w_in_t, 'nn', tm=1024, tn=1024, tk=2560, name="dh1", after=token)
    gx, dsc1, dsh1, g_norm1 = _norm_mod_bwd(x, norm1_g, sc1, dh1, dx1, "norm1_bwd")

    dmod = [dsh1, dsc1, dg1, dsh2, dsc2, dg2]
    small_g = [g_norm1, g_bias, g_gn_g, g_gn_b, g_norm2, g_normf]
    return loss, gx, in_flight, red_b + red_a, small_g, dmod


def _to_slots(g, axis):
    if axis == 0:
        return g.reshape(4, 2, g.shape[0] // N_DEV, g.shape[1])
    return g.reshape(g.shape[0], N_DEV, g.shape[1] // N_DEV).transpose(1, 0, 2).reshape(4, 2, g.shape[0], -1)


def _from_slots(w8, axis):
    if axis == 0:
        return w8.reshape(-1, w8.shape[2])
    return w8.transpose(1, 0, 2).reshape(w8.shape[1], -1)


BIG_AXES = (1, 0, 1, 0, 1, 0)


def kernel(x, c, w_ada, b_ada, norm1_g, w_in, rel_bias, ret_gn_g, ret_gn_b, w_ret_out, w_att_out, w_o, norm2_g, w_ff1, w_ff2, norm_f_g, loss_target, m_w_ada, m_b_ada, m_norm1_g, m_w_in, m_rel_bias, m_ret_gn_g, m_ret_gn_b, m_w_ret_out, m_w_att_out, m_w_o, m_norm2_g, m_w_ff1, m_w_ff2, m_norm_f_g, v_w_ada, v_b_ada, v_norm1_g, v_w_in, v_rel_bias, v_ret_gn_g, v_ret_gn_b, v_w_ret_out, v_w_att_out, v_w_o, v_norm2_g, v_w_ff1, v_w_ff2, v_norm_f_g):
    mx, my, mc = _mesh_pos()
    dev = 4 * mx + 2 * my + mc
    chip = jnp.reshape(2 * mx + my, (1,)).astype(jnp.int32)
    core = jnp.reshape(mc, (1,)).astype(jnp.int32)
    ada_w = D * 6 // N_DEV

    w_in, m_w_in, v_w_in = (jnp.transpose(t, (0, 2, 1)) for t in (w_in, m_w_in, v_w_in))

    shards = [w[0].astype(BF16) for w in (w_in, w_ret_out, w_att_out, w_o, w_ff1, w_ff2)]
    c_all, w_in8 = _run_comm(_Gather([c, shards[0]], relay=True, chunks={1: 4}), "gather_c_w_in")
    c_all = c_all.reshape(N_DEV, D)
    b_sl = lax.dynamic_slice(b_ada, (0, dev * ada_w), (1, ada_w))
    (mod_all,) = _run_comm(_Gather([_ada_fwd(c_all, w_ada[0], b_sl)]), "gather_mod")
    mod = lax.dynamic_index_in_dim(mod_all, dev, axis=1, keepdims=False).reshape(6, D)
    mods = tuple(mod[i:i + 1] for i in range(6))

    small = (norm1_g, rel_bias, ret_gn_g, ret_gn_b, norm2_g, norm_f_g.reshape(1, D))
    loss, gx, in_flight, big_red, small_g, dmod = _local_step(x[0], loss_target[0], mods, w_in8.reshape(IN_COLS, D),
                                                              shards[1:], small, chip, core)

    gathered = _run_comm(_Gather(dmod + small_g + [loss]), "gather_small")
    g_b_ada, dmod_all, (g_norm1, g_bias, g_gn_g, g_gn_b, g_norm2, g_normf, loss_sum) = _sum_small(gathered)
    loss_out = loss_sum[0, 0]
    g_w_ada = _ada_bwd(c_all, lax.dynamic_slice(dmod_all, (0, dev * ada_w), (N_DEV, ada_w)))

    names = ['w_ada', 'b_ada', 'norm1_g', 'w_in', 'rel_bias', 'ret_gn_g', 'ret_gn_b', 'w_ret_out', 'w_att_out',
             'w_o', 'norm2_g', 'w_ff1', 'w_ff2', 'norm_f_g']
    ws = dict(zip(names, (w_ada, b_ada, norm1_g, w_in, rel_bias, ret_gn_g, ret_gn_b, w_ret_out, w_att_out, w_o,
                          norm2_g, w_ff1, w_ff2, norm_f_g)))
    ms = dict(zip(names, (m_w_ada, m_b_ada, m_norm1_g, m_w_in, m_rel_bias, m_ret_gn_g, m_ret_gn_b, m_w_ret_out,
                          m_w_att_out, m_w_o, m_norm2_g, m_w_ff1, m_w_ff2, m_norm_f_g)))
    vs = dict(zip(names, (v_w_ada, v_b_ada, v_norm1_g, v_w_in, v_rel_bias, v_ret_gn_g, v_ret_gn_b, v_w_ret_out,
                          v_w_att_out, v_w_o, v_norm2_g, v_w_ff1, v_w_ff2, v_norm_f_g)))
    grads = dict(w_ada=g_w_ada, w_ret_out=big_red[0], w_att_out=big_red[1], w_o=big_red[2],
                 w_ff1=big_red[3], w_ff2=big_red[4], b_ada=g_b_ada, norm1_g=g_norm1, rel_bias=g_bias,
                 ret_gn_g=g_gn_g, ret_gn_b=g_gn_b, norm2_g=g_norm2, norm_f_g=g_normf)
    delta, new_m, new_v = {}, {}, {}
    for n in ('w_ada', 'w_ret_out', 'w_att_out', 'w_o', 'w_ff1', 'w_ff2'):
        delta[n], new_m[n], new_v[n] = _adamw(ws[n], grads[n], ms[n], vs[n], "adamw_" + n)
        grads[n] = grads[n].reshape(ws[n].shape)
    small_names = ('b_ada', 'norm1_g', 'rel_bias', 'ret_gn_g', 'ret_gn_b', 'norm2_g', 'norm_f_g')
    two_d = {n: (1, ws[n].size) if ws[n].ndim == 1 else ws[n].shape for n in small_names}
    d_, m_, v_ = _adamw_small(*[[src[n].reshape(two_d[n]) for n in small_names] for src in (ws, grads, ms, vs)])
    for i, n in enumerate(small_names):
        shp = ws[n].shape
        delta[n], new_m[n], new_v[n] = d_[i].reshape(shp), m_[i].reshape(shp), v_[i].reshape(shp)
        grads[n] = grads[n].reshape(shp)

    done = lax.optimization_barrier((gx, tuple(d_), tuple(delta[n] for n in ('w_ada', 'w_ret_out', 'w_att_out', 'w_o',
                                                                               'w_ff1', 'w_ff2'))))
    parts_in, recvs_in = [], []
    for half, flight in enumerate(in_flight):
        (part_in,), (recv_chip_in,) = _chip_exchange_wait(flight, done[0], f"rs_in{half}_wait")
        parts_in.append(part_in)
        recvs_in.append(recv_chip_in)
    grads['w_in'], delta['w_in'], new_m['w_in'], new_v['w_in'] = _adamw_reduced(w_in, m_w_in, v_w_in, parts_in,
                                                                               recvs_in, chip)
    for d in (grads, delta, new_m, new_v):
        d['w_in'] = jnp.transpose(d['w_in'], (0, 2, 1))
    return (loss_out, gx[None], *[grads[n] for n in names], *[delta[n] for n in names],
            *[new_m[n] for n in names], *[new_v[n] for n in names])
```

```python
import functools
import math

import numpy as np
import jax
import jax.numpy as jnp
from jax import lax
from jax.experimental import pallas as pl
from jax.experimental.pallas import tpu as pltpu

F32 = jnp.float32
BF16 = jnp.bfloat16
MESH = pl.DeviceIdType.MESH

N_DEV = 8
S = 2048
D = 1024
RET_HEADS = 4
RET_DK = 256
RET_DV = 512
CHUNK = 128
N_CHUNK = S // CHUNK
ATT_GROUPS = ((128, 1), (512, 4), (2048, 16))
ATT_HG = 4
ATT_DH = 128
ATT_BLK = 128
N_BUCKETS = 32
MAX_DIST = 2048
D_FF = 4096
IN_COLS = 12800
OFF_RQ, OFF_RK, OFF_RV, OFF_RG, OFF_ATT = 0, 1024, 2048, 4096, 6144
OFF_GA, OFF_GB = 6144, 7168
RMS_EPS = 1e-6
GN_EPS = 1e-5
ADAM_LR, ADAM_B1, ADAM_B2, ADAM_EPS, ADAM_WD, ADAM_STEP = 0.001, 0.9, 0.999, 1e-08, 0.01, 10
VMEM_LIMIT = 48 * 1024 * 1024


def _pcall(body, **kw):
    return pl.pallas_call(body, **kw)


def _params(sem=None):
    return pltpu.CompilerParams(dimension_semantics=sem, vmem_limit_bytes=VMEM_LIMIT)


HBM_SPEC = pl.BlockSpec(memory_space=pl.ANY)


def _carry(body, comm, *, name, grid, in_specs, out_specs, out_shape, scratch_shapes=()):
    single = not isinstance(out_specs, (tuple, list))
    o_specs = (out_specs,) if single else tuple(out_specs)
    o_shape = (out_shape,) if single else tuple(out_shape)
    n_in, n_out, n_scr = len(in_specs), len(o_specs), len(scratch_shapes)
    nci, nco = len(comm.ins), len(comm.out_shape)
    total = int(np.prod(grid))

    def wrapped(*refs):
        bounds = np.cumsum([0, n_in, nci, n_out, nco, n_scr])
        a, ci, o, co, scr = (refs[bounds[i]:bounds[i + 1]] for i in range(5))
        sems = refs[bounds[5]:]
        flat = 0
        for d, g in enumerate(grid):
            flat = flat * g + pl.program_id(d)

        @pl.when(flat == 0)
        def _():
            comm.start(ci, co, sems)

        body(*a, *o, *scr)

        @pl.when(flat == total - 1)
        def _():
            comm.finish(ci, co, sems)

    aliases = {n_in + i: n_out + o for i, o in getattr(comm, "aliases", {}).items()}
    call = _pcall(wrapped, name=name, grid=grid, in_specs=list(in_specs) + [HBM_SPEC] * nci,
                  out_specs=o_specs + (HBM_SPEC,) * nco, out_shape=o_shape + tuple(comm.out_shape),
                  scratch_shapes=list(scratch_shapes) + list(comm.sems), input_output_aliases=aliases,
                  compiler_params=_params(("arbitrary",) * len(grid)))

    def run(*args):
        res = call(*args, *comm.ins)
        own = res[0] if single else tuple(res[:n_out])
        return own, tuple(res[n_out:])

    return run


def _run_comm(comm, name, after=None):
    nci, nco = len(comm.ins), len(comm.out_shape)
    extra = [] if after is None else [after]

    def body(*refs):
        ci, co, sems = refs[:nci], refs[nci + len(extra):nci + len(extra) + nco], refs[nci + len(extra) + nco:]
        comm.start(ci, co, sems)
        comm.finish(ci, co, sems)

    return _pcall(body, name=name, in_specs=[HBM_SPEC] * (nci + len(extra)), out_specs=(HBM_SPEC,) * nco,
                  out_shape=tuple(comm.out_shape), scratch_shapes=list(comm.sems))(*comm.ins, *extra)


def _dot(a, b, dn):
    return lax.dot_general(a.astype(BF16), b.astype(BF16), (dn, ((), ())), preferred_element_type=F32)


NN = ((1,), (0,))
NT = ((1,), (1,))
TN = ((0,), (0,))


def _mm(a, b, mode, *, tm, tn, tk, name, out_dtype=F32, res=None, gvec=None, relu2=False, relu2_of=None, comm=None,
        after=None):
    if mode == 'nn':
        (M, K), (_, N) = a.shape, b.shape
        a_spec = pl.BlockSpec((tm, tk), lambda i, j, k: (i, k))
        b_spec = pl.BlockSpec((tk, tn), lambda i, j, k: (k, j))
        dn = NN
    elif mode == 'nt':
        (M, K), (N, _) = a.shape, b.shape
        a_spec = pl.BlockSpec((tm, tk), lambda i, j, k: (i, k))
        b_spec = pl.BlockSpec((tn, tk), lambda i, j, k: (j, k))
        dn = NT
    else:
        (K, M), (_, N) = a.shape, b.shape
        a_spec = pl.BlockSpec((tk, tm), lambda i, j, k: (k, i))
        b_spec = pl.BlockSpec((tk, tn), lambda i, j, k: (k, j))
        dn = TN
    assert M % tm == 0 and N % tn == 0 and K % tk == 0, (name, M, N, K)
    nk = K // tk
    fused = res is not None
    o_spec = pl.BlockSpec((tm, tn), lambda i, j, k: (i, j))

    def body(a_ref, b_ref, *rest):
        acc_ref = rest[-1] if nk > 1 else None
        if after is not None:
            rest = rest[1:]
        if fused:
            res_ref, g_ref, o_ref, x_ref = rest[:4]
        elif relu2_of is not None:
            u_ref, o_ref = rest[:2]
        elif relu2:
            o_ref, act_ref = rest[:2]
        else:
            o_ref = rest[0]

        def finish(acc):
            if relu2_of is not None:
                acc = acc * (2.0 * jnp.maximum(u_ref[...], 0.0))
            o_ref[...] = acc.astype(o_ref.dtype)
            if fused:
                x_ref[...] = res_ref[...] + g_ref[...] * acc
            if relu2:
                r = jnp.maximum(acc, 0.0)
                act_ref[...] = (r * r).astype(BF16)

        p = _dot(a_ref[...], b_ref[...], dn)
        if nk == 1:
            finish(p)
        else:
            k = pl.program_id(2)

            @pl.when(k == 0)
            def _():
                acc_ref[...] = p

            @pl.when(k > 0)
            def _():
                acc_ref[...] += p

            @pl.when(k == nk - 1)
            def _():
                finish(acc_ref[...])

    in_specs = [a_spec, b_spec]
    args = [a, b]
    if after is not None:
        in_specs.append(pl.BlockSpec(memory_space=pl.ANY))
        args.append(after)
    out_shape = jax.ShapeDtypeStruct((M, N), out_dtype)
    out_specs = o_spec
    if fused:
        in_specs += [pl.BlockSpec((tm, tn), lambda i, j, k: (i, j)), pl.BlockSpec((1, tn), lambda i, j, k: (0, j))]
        args += [res, gvec]
        out_shape = (out_shape, jax.ShapeDtypeStruct((M, N), F32))
        out_specs = (o_spec, pl.BlockSpec((tm, tn), lambda i, j, k: (i, j)))
    elif relu2_of is not None:
        in_specs.append(pl.BlockSpec((tm, tn), lambda i, j, k: (i, j)))
        args.append(relu2_of)
    elif relu2:
        out_shape = (out_shape, jax.ShapeDtypeStruct((M, N), BF16))
        out_specs = (o_spec, pl.BlockSpec((tm, tn), lambda i, j, k: (i, j)))
    kw = dict(name=name, grid=(M // tm, N // tn, nk), in_specs=in_specs, out_specs=out_specs,
              out_shape=out_shape, scratch_shapes=[pltpu.VMEM((tm, tn), F32)] if nk > 1 else [])
    if comm is not None:
        return _carry(body, comm, **kw)(*args)
    return _pcall(body, compiler_params=_params(("parallel", "parallel", "arbitrary")), **kw)(*args)


PROJ_TN = 512
ATT_T0, ATT_T1 = 6144 // PROJ_TN, 10752 // PROJ_TN
N_SLABS = (ATT_T1 - ATT_T0) * 4
MAIN_COLS = IN_COLS - (ATT_T1 - ATT_T0) * PROJ_TN


def _proj(h1, w_in_t, after):
    nj = IN_COLS // PROJ_TN

    def body(a_ref, b_ref, after_ref, main_ref, slab_ref):
        j = pl.program_id(1)
        is_att = (j >= ATT_T0) & (j < ATT_T1)
        chunks = [pl.ds(c * 512, 512) for c in range(S // 512)]

        @pl.when(jnp.logical_not(is_att))
        def _():
            for rows in chunks:
                main_ref[rows, :] = _dot(a_ref[rows, :], b_ref[...], NT)

        @pl.when(is_att)
        def _():
            for rows in chunks:
                p = _dot(a_ref[rows, :], b_ref[...], NT)
                for h in range(4):
                    slab_ref[h, rows, :] = p[:, h * 128:(h + 1) * 128]

    main_idx = lambda j: jnp.where(j < ATT_T0, j, jnp.where(j < ATT_T1, ATT_T0 - 1, j - (ATT_T1 - ATT_T0)))
    slab_idx = lambda j: jnp.clip(j - ATT_T0, 0, ATT_T1 - ATT_T0 - 1)
    return _pcall(
        body, name="proj", grid=(1, nj, 1),
        in_specs=[pl.BlockSpec((S, D), lambda i, j, k: (0, 0)), pl.BlockSpec((PROJ_TN, D), lambda i, j, k: (j, 0)),
                  HBM_SPEC],
        out_specs=(pl.BlockSpec((S, PROJ_TN), lambda i, j, k: (0, main_idx(j))),
                   pl.BlockSpec((4, S, 128), lambda i, j, k: (slab_idx(j), 0, 0))),
        out_shape=(jax.ShapeDtypeStruct((S, MAIN_COLS), F32), jax.ShapeDtypeStruct((N_SLABS, S, 128), F32)),
        compiler_params=_params(("arbitrary",) * 3))(h1, w_in_t, after)


TR = 256


def _row_spec(w=D):
    return pl.BlockSpec((TR, w), lambda i: (i, 0))


def _vec_spec(w=D):
    return pl.BlockSpec((1, w), lambda i: (0, 0))


def _norm_mod_fwd(x, g, sh, sc, name):
    def body(x_ref, g_ref, sh_ref, sc_ref, o_ref):
        xv = x_ref[...]
        rstd = lax.rsqrt(jnp.mean(xv * xv, axis=-1, keepdims=True) + RMS_EPS)
        n = xv * rstd * g_ref[...]
        o_ref[...] = (n * (1.0 + sc_ref[...]) + sh_ref[...]).astype(BF16)

    return _pcall(body, name=name, grid=(S // TR,), in_specs=[_row_spec(), _vec_spec(), _vec_spec(), _vec_spec()],
                  out_specs=_row_spec(), out_shape=jax.ShapeDtypeStruct((S, D), BF16),
                  compiler_params=_params(("parallel",)))(x, g, sh, sc)


def _norm_mod_bwd(x, g, sc, dh, dres, name, gate=None):
    gated = gate is not None

    def body(x_ref, g_ref, sc_ref, dh_ref, dres_ref, *rest):
        if gated:
            f_ref, gv_ref, dx_ref, dsc_ref, dsh_ref, dg_ref, dz_ref, dgv_ref = rest
        else:
            dx_ref, dsc_ref, dsh_ref, dg_ref = rest
        i = pl.program_id(0)
        xv = x_ref[...]
        dh = dh_ref[...]
        rstd = lax.rsqrt(jnp.mean(xv * xv, axis=-1, keepdims=True) + RMS_EPS)
        xhat = xv * rstd
        gv = g_ref[...]
        dn = dh * (1.0 + sc_ref[...])
        dxhat = dn * gv
        dx = dres_ref[...] + rstd * (dxhat - xhat * jnp.mean(dxhat * xhat, axis=-1, keepdims=True))
        dx_ref[...] = dx
        sums = [(dsc_ref, jnp.sum(dh * (xhat * gv), axis=0, keepdims=True)),
                (dsh_ref, jnp.sum(dh, axis=0, keepdims=True)),
                (dg_ref, jnp.sum(dn * xhat, axis=0, keepdims=True))]
        if gated:
            dz_ref[...] = (dx * gv_ref[...]).astype(BF16)
            sums.append((dgv_ref, jnp.sum(dx * f_ref[...], axis=0, keepdims=True)))

        @pl.when(i == 0)
        def _():
            for ref, p in sums:
                ref[...] = p

        @pl.when(i > 0)
        def _():
            for ref, p in sums:
                ref[...] += p

    vec = jax.ShapeDtypeStruct((1, D), F32)
    in_specs = [_row_spec(), _vec_spec(), _vec_spec(), _row_spec(), _row_spec()]
    out_specs = [_row_spec(), _vec_spec(), _vec_spec(), _vec_spec()]
    out_shape = [jax.ShapeDtypeStruct((S, D), F32), vec, vec, vec]
    args = [x, g, sc, dh, dres]
    if gated:
        in_specs += [_row_spec(), _vec_spec()]
        out_specs += [_row_spec(), _vec_spec()]
        out_shape += [jax.ShapeDtypeStruct((S, D), BF16), vec]
        args += list(gate)
    return _pcall(body, name=name, grid=(S // TR,), in_specs=in_specs, out_specs=tuple(out_specs),
                  out_shape=tuple(out_shape), compiler_params=_params(("arbitrary",)))(*args)


def _final_loss(x2, tgt, g, f, g2):
    def body(x_ref, t_ref, g_ref, f_ref, g2_ref, loss_ref, dx_ref, dg_ref, df_ref, dg2_ref):
        i = pl.program_id(0)
        xv = x_ref[...]
        gv = g_ref[...]
        rstd = lax.rsqrt(jnp.mean(xv * xv, axis=-1, keepdims=True) + RMS_EPS)
        xhat = xv * rstd
        err = xhat * gv - t_ref[...]
        dy = err * (1.0 / D)
        dxhat = dy * gv
        dx = rstd * (dxhat - xhat * jnp.mean(dxhat * xhat, axis=-1, keepdims=True))
        dx_ref[...] = dx
        df_ref[...] = (dx * g2_ref[...]).astype(BF16)
        p_g = jnp.sum(dy * xhat, axis=0, keepdims=True)
        p_g2 = jnp.sum(dx * f_ref[...], axis=0, keepdims=True)
        p_l = jnp.zeros((1, 128), F32) + 0.5 * jnp.sum(jnp.mean(err * err, axis=-1, keepdims=True))

        @pl.when(i == 0)
        def _():
            dg_ref[...] = p_g
            dg2_ref[...] = p_g2
            loss_ref[...] = p_l

        @pl.when(i > 0)
        def _():
            dg_ref[...] += p_g
            dg2_ref[...] += p_g2
            loss_ref[...] += p_l

    vec = jax.ShapeDtypeStruct((1, D), F32)
    return _pcall(body, name="final_loss", grid=(S // TR,),
                  in_specs=[_row_spec(), _row_spec(), _vec_spec(), _row_spec(), _vec_spec()],
                  out_specs=(_vec_spec(128), _row_spec(), _vec_spec(), _row_spec(), _vec_spec()),
                  out_shape=(jax.ShapeDtypeStruct((1, 128), F32), jax.ShapeDtypeStruct((S, D), F32), vec,
                             jax.ShapeDtypeStruct((S, D), BF16), vec),
                  compiler_params=_params(("arbitrary",)))(x2, tgt, g, f, g2)


HALF = 512


MERGE_TM = 1024


def _merge_specs():
    blk = lambda off: pl.BlockSpec((MERGE_TM, HALF), lambda i, j: (i, off // HALF + j))
    return blk(OFF_GA), blk(OFF_GB), blk(0)


def _att_out_merge(att, w_att_out, proj, ret_out):
    def body(a_ref, b_ref, ga_ref, gb_ref, r_ref, o_ref, m_ref):
        acc = _dot(a_ref[...], b_ref[...], NN)
        o_ref[...] = acc
        m_ref[...] = (jax.nn.sigmoid(ga_ref[...]) * r_ref[...] + jax.nn.sigmoid(gb_ref[...]) * acc).astype(BF16)

    ga, gb, tile = _merge_specs()
    return _pcall(body, name="att_out", grid=(S // MERGE_TM, D // HALF),
                  in_specs=[pl.BlockSpec((MERGE_TM, AW), lambda i, j: (i, 0)), pl.BlockSpec((AW, HALF), lambda i, j: (0, j)),
                            ga, gb, tile],
                  out_specs=(tile, tile),
                  out_shape=(jax.ShapeDtypeStruct((S, D), F32), jax.ShapeDtypeStruct((S, D), BF16)),
                  compiler_params=_params(("parallel", "parallel")))(att, w_att_out, proj, proj, ret_out)


def _dmerged_split(dmixo, w_o, proj, ret_out, att_out):
    def body(a_ref, b_ref, ga_ref, gb_ref, r_ref, at_ref, dr_ref, da_ref, dga_ref, dgb_ref):
        dm = _dot(a_ref[...], b_ref[...], NT)
        sa = jax.nn.sigmoid(ga_ref[...])
        sb = jax.nn.sigmoid(gb_ref[...])
        dr_ref[...] = (dm * sa).astype(BF16)
        da_ref[...] = (dm * sb).astype(BF16)
        dga_ref[...] = (dm * r_ref[...] * (sa * (1.0 - sa))).astype(BF16)
        dgb_ref[...] = (dm * at_ref[...] * (sb * (1.0 - sb))).astype(BF16)

    ga, gb, tile = _merge_specs()
    o = jax.ShapeDtypeStruct((S, D), BF16)
    return _pcall(body, name="dmerged", grid=(S // MERGE_TM, D // HALF),
                  in_specs=[pl.BlockSpec((MERGE_TM, D), lambda i, j: (i, 0)), pl.BlockSpec((HALF, D), lambda i, j: (j, 0)),
                            ga, gb, tile, tile],
                  out_specs=(tile,) * 4, out_shape=(o, o, o, o),
                  compiler_params=_params(("parallel", "parallel")))(dmixo, w_o, proj, proj, ret_out, att_out)


def _ret_tables():
    H, C = RET_HEADS, CHUNK
    log_g = jnp.log1p(-(2.0 ** (-5.0 - jnp.arange(H, dtype=F32))))
    idx = jnp.arange(C, dtype=F32)
    rel = idx[:, None] - idx[None, :]
    inner = jnp.where(rel >= 0, jnp.exp(log_g[:, None, None] * jnp.maximum(rel, 0.0)), 0.0)
    qd = jnp.exp(log_g[:, None] * (idx + 1.0))[:, :, None]
    kd = jnp.exp(log_g[:, None] * (C - 1.0 - idx))[:, :, None]
    cd = jnp.broadcast_to(jnp.exp(log_g * C)[:, None, None], (H, 1, 128))
    half = RET_DK // 2
    inv = 10000.0 ** (-jnp.arange(half, dtype=F32) / half)
    ang = jnp.arange(S, dtype=F32)[:, None] * inv[None, :]
    return inner, qd, kd, cd, jnp.cos(ang), jnp.sin(ang)


def _rot(x, cos, sin):
    x1, x2 = x[:, :128], x[:, 128:]
    return jnp.concatenate([x1 * cos - x2 * sin, x1 * sin + x2 * cos], axis=1)


def _rot_t(d, cos, sin):
    d1, d2 = d[:, :128], d[:, 128:]
    return jnp.concatenate([d1 * cos + d2 * sin, d2 * cos - d1 * sin], axis=1)


RET_COLS = OFF_ATT
RET_VW = RET_HEADS * RET_DV


def _ret_specs(chunk_of):
    ci = chunk_of
    whole = lambda shape: pl.BlockSpec(shape, lambda t: (0,) * len(shape))
    return [
        pl.BlockSpec((CHUNK, RET_COLS), lambda t: (ci(t), 0)),
        pl.BlockSpec((CHUNK, 128), lambda t: (ci(t), 0)),
        pl.BlockSpec((CHUNK, 128), lambda t: (ci(t), 0)),
        whole((RET_HEADS, CHUNK, CHUNK)), whole((RET_HEADS, CHUNK, 1)), whole((RET_HEADS, CHUNK, 1)),
        whole((RET_HEADS, 1, 128)), whole((1, RET_VW)), whole((1, RET_VW)),
    ]


def _ret_cols(h):
    q = slice(OFF_RQ + h * RET_DK, OFF_RQ + (h + 1) * RET_DK)
    k = slice(OFF_RK + h * RET_DK, OFF_RK + (h + 1) * RET_DK)
    v = slice(OFF_RV + h * RET_DV, OFF_RV + (h + 1) * RET_DV)
    g = slice(OFF_RG + h * RET_DV, OFF_RG + (h + 1) * RET_DV)
    return q, k, v, g, slice(h * RET_DV, (h + 1) * RET_DV)


def _ret_fwd(proj, tables, gn_g, gn_b, comm=None):
    inner, qd, kd, cd, cos, sin = tables

    def body(x_ref, cos_ref, sin_ref, in_ref, qd_ref, kd_ref, cd_ref, g_ref, b_ref,
             gated_ref, ro_ref, st_ref, s_scr):
        i = pl.program_id(0)

        @pl.when(i == 0)
        def _():
            s_scr[...] = jnp.zeros_like(s_scr)

        cosv, sinv = cos_ref[...], sin_ref[...]
        for h in range(RET_HEADS):
            cq, ck, cv, cg, co = _ret_cols(h)
            q = _rot(x_ref[:, cq], cosv, sinv)
            k = _rot(x_ref[:, ck], cosv, sinv) * (RET_DK ** -0.5)
            v = x_ref[:, cv]
            st = s_scr[h]
            st_ref[h] = st.astype(BF16)
            s = _dot(q, k, NT) * in_ref[h]
            o = _dot(s, v, NN) + _dot(q, st, NN) * qd_ref[h]
            s_scr[h] = st * cd_ref[h, :, :1] + _dot(k * kd_ref[h], v, TN)
            ro_ref[:, co] = o
            mu = jnp.mean(o, axis=-1, keepdims=True)
            oc = o - mu
            var = jnp.mean(oc * oc, axis=-1, keepdims=True)
            rn = oc * lax.rsqrt(var + GN_EPS) * g_ref[:, co] + b_ref[:, co]
            rg = x_ref[:, cg]
            gated_ref[:, co] = (rg * jax.nn.sigmoid(rg) * rn).astype(BF16)

    ospec = pl.BlockSpec((CHUNK, RET_VW), lambda t: (t, 0))
    kw = dict(name="ret_fwd", grid=(N_CHUNK,), in_specs=_ret_specs(lambda t: t),
              out_specs=(ospec, ospec, pl.BlockSpec((RET_HEADS, None, RET_DK, RET_DV), lambda t: (0, t, 0, 0))),
              out_shape=(jax.ShapeDtypeStruct((S, RET_VW), BF16), jax.ShapeDtypeStruct((S, RET_VW), F32),
                         jax.ShapeDtypeStruct((RET_HEADS, N_CHUNK, RET_DK, RET_DV), BF16)),
              scratch_shapes=[pltpu.VMEM((RET_HEADS, RET_DK, RET_DV), F32)])
    args = (proj, cos, sin, inner, qd, kd, cd, gn_g, gn_b)
    if comm is not None:
        return _carry(body, comm, **kw)(*args)
    return _pcall(body, compiler_params=_params(("arbitrary",)), **kw)(*args)


def _ret_bwd(proj, tables, gn_g, gn_b, ro, states, dgated, comm=None):
    inner, qd, kd, cd, cos, sin = tables
    last = N_CHUNK - 1

    def body(x_ref, cos_ref, sin_ref, in_ref, qd_ref, kd_ref, cd_ref, g_ref, b_ref, ro_ref, st_ref, dg_ref,
             dx_ref, gg_ref, gb_ref, gs_scr):
        t = pl.program_id(0)

        @pl.when(t == 0)
        def _():
            gs_scr[...] = jnp.zeros_like(gs_scr)
            gg_ref[...] = jnp.zeros_like(gg_ref)
            gb_ref[...] = jnp.zeros_like(gb_ref)

        cosv, sinv = cos_ref[...], sin_ref[...]
        for h in range(RET_HEADS):
            cq, ck, cv, cg, co = _ret_cols(h)
            q = _rot(x_ref[:, cq], cosv, sinv)
            k = _rot(x_ref[:, ck], cosv, sinv) * (RET_DK ** -0.5)
            v = x_ref[:, cv]
            qdv, kdv, dm = qd_ref[h], kd_ref[h], in_ref[h]
            st = st_ref[h]
            o = ro_ref[:, co]
            gv = g_ref[:, co]
            mu = jnp.mean(o, axis=-1, keepdims=True)
            oc = o - mu
            rstd = lax.rsqrt(jnp.mean(oc * oc, axis=-1, keepdims=True) + GN_EPS)
            ohat = oc * rstd
            rn = ohat * gv + b_ref[:, co]
            rg = x_ref[:, cg]
            sg = jax.nn.sigmoid(rg)
            dgt = dg_ref[:, co]
            drn = dgt * (rg * sg)
            dx_ref[:, cg] = (dgt * rn * (sg * (1.0 + rg * (1.0 - sg)))).astype(BF16)
            gg_ref[:, co] += jnp.sum(drn * ohat, axis=0, keepdims=True)
            gb_ref[:, co] += jnp.sum(drn, axis=0, keepdims=True)
            dohat = drn * gv
            do = rstd * (dohat - jnp.mean(dohat, axis=-1, keepdims=True)
                         - ohat * jnp.mean(dohat * ohat, axis=-1, keepdims=True))
            gs = gs_scr[h]
            s = _dot(q, k, NT) * dm
            dsr = _dot(do, v, NT) * dm
            dq = _dot(dsr, k, NN) + _dot(do, st, NT) * qdv
            dk = _dot(dsr, q, TN) + _dot(v, gs, NT) * kdv
            dv = _dot(s, do, TN) + _dot(k * kdv, gs, NN)
            gs_scr[h] = gs * cd_ref[h, :, :1] + _dot(q * qdv, do, TN)
            dx_ref[:, cq] = _rot_t(dq, cosv, sinv).astype(BF16)
            dx_ref[:, ck] = (_rot_t(dk, cosv, sinv) * (RET_DK ** -0.5)).astype(BF16)
            dx_ref[:, cv] = dv.astype(BF16)

    rev = lambda t: last - t
    vblk = pl.BlockSpec((CHUNK, RET_VW), lambda t: (rev(t), 0))
    vspec = pl.BlockSpec((1, RET_VW), lambda t: (0, 0))
    kw = dict(name="ret_bwd", grid=(N_CHUNK,),
              in_specs=_ret_specs(rev) + [vblk, pl.BlockSpec((RET_HEADS, None, RET_DK, RET_DV),
                                                             lambda t: (0, rev(t), 0, 0)), vblk],
              out_specs=(pl.BlockSpec((CHUNK, RET_COLS), lambda t: (rev(t), 0)), vspec, vspec),
              out_shape=(jax.ShapeDtypeStruct((S, RET_COLS), BF16), jax.ShapeDtypeStruct((1, RET_VW), F32),
                         jax.ShapeDtypeStruct((1, RET_VW), F32)),
              scratch_shapes=[pltpu.VMEM((RET_HEADS, RET_DK, RET_DV), F32)])
    args = (proj, cos, sin, inner, qd, kd, cd, gn_g, gn_b, ro, states, dgated)
    if comm is not None:
        return _carry(body, comm, **kw)(*args)
    return _pcall(body, compiler_params=_params(("arbitrary",)), **kw)(*args)


def _bucket_tables():
    qi = np.arange(ATT_BLK)[:, None]
    kj = np.arange(2 * ATT_BLK)[None, :]
    m = ATT_BLK + qi - kj
    out = []
    for win, dil in ATT_GROUPS:
        w = win // dil
        dist = (np.clip(m, 0, w) * dil).astype(np.int32)
        max_exact = N_BUCKETS // 2
        d_f = np.maximum(dist, 1).astype(np.float32)
        large = max_exact + (np.log(d_f / np.float32(max_exact)) / np.float32(math.log(MAX_DIST / max_exact))
                             * np.float32(N_BUCKETS - max_exact)).astype(np.int32)
        large = np.minimum(large, N_BUCKETS - 1)
        out.append(np.where(dist < max_exact, dist, large).astype(np.int32))
    return np.stack(out)


def _bias_build(rel_bias, buckets):
    def body(tab_ref, bk_ref, o_ref):
        hh = pl.program_id(0)
        bk = bk_ref[...]
        acc = jnp.zeros((ATT_BLK, 2 * ATT_BLK), F32)
        for b in range(N_BUCKETS):
            acc = jnp.where(bk == b, tab_ref[b, hh], acc)
        o_ref[...] = acc

    nh = len(ATT_GROUPS) * ATT_HG
    return _pcall(body, name="bias_build", grid=(nh,),
                  in_specs=[pl.BlockSpec(memory_space=pltpu.SMEM),
                            pl.BlockSpec((None, ATT_BLK, 2 * ATT_BLK), lambda hh: (hh // ATT_HG, 0, 0))],
                  out_specs=pl.BlockSpec((None, ATT_BLK, 2 * ATT_BLK), lambda hh: (hh, 0, 0)),
                  out_shape=jax.ShapeDtypeStruct((nh, ATT_BLK, 2 * ATT_BLK), F32),
                  compiler_params=_params(("parallel",)))(rel_bias, buckets)


def _bias_grad(ds_sum, buckets):
    def body(ds_ref, bk_ref, o_ref):
        bk = bk_ref[...]
        ds = ds_ref[...]
        rows = lax.broadcasted_iota(jnp.int32, (N_BUCKETS, 128), 0)
        acc = jnp.zeros((N_BUCKETS, 128), F32)
        for b in range(N_BUCKETS):
            acc = jnp.where(rows == b, jnp.sum(jnp.where(bk == b, ds, 0.0)), acc)
        o_ref[...] = acc

    nh = len(ATT_GROUPS) * ATT_HG
    return _pcall(body, name="bias_grad", grid=(nh,),
                  in_specs=[pl.BlockSpec((None, ATT_BLK, 2 * ATT_BLK), lambda hh: (hh, 0, 0)),
                            pl.BlockSpec((None, ATT_BLK, 2 * ATT_BLK), lambda hh: (hh // ATT_HG, 0, 0))],
                  out_specs=pl.BlockSpec((None, N_BUCKETS, 128), lambda hh: (hh, 0, 0)),
                  out_shape=jax.ShapeDtypeStruct((nh, N_BUCKETS, 128), F32),
                  compiler_params=_params(("parallel",)))(ds_sum, buckets)


def _att_valid(n):
    qi = lax.broadcasted_iota(jnp.int32, (ATT_BLK, 2 * ATT_BLK), 0)
    kj = lax.broadcasted_iota(jnp.int32, (ATT_BLK, 2 * ATT_BLK), 1)
    m = ATT_BLK + qi - kj
    first_key = jnp.where(n > 0, 0, ATT_BLK)
    return (m >= 0) & (m <= ATT_BLK) & (kj >= first_key)


ATT_HP = (1, 2, 2)


def _att_geometry(gi):
    _, dil = ATT_GROUPS[gi]
    return dil, S // dil // ATT_BLK, ATT_HP[gi]


def _blk(dil, r, n):
    if dil == 1:
        return pl.ds(n * ATT_BLK, ATT_BLK)
    return pl.ds(r + n * ATT_BLK * dil, ATT_BLK, stride=dil)


def _slab_specs(gi):
    _, _, hp = _att_geometry(gi)
    per = ATT_HG // hp
    return [pl.BlockSpec((hp, S, ATT_DH), lambda g, r, part=part: ((3 * gi + part) * per + g, 0, 0))
            for part in range(3)]


def _head_specs(gi, count):
    _, _, hp = _att_geometry(gi)
    return [pl.BlockSpec((hp, S, ATT_DH), lambda g, r: (g, 0, 0))] * count


def _bias_spec(gi):
    _, _, hp = _att_geometry(gi)
    return pl.BlockSpec((hp, ATT_BLK, 2 * ATT_BLK), lambda g, r: (gi * (ATT_HG // hp) + g, 0, 0))


def _att_valid_first():
    qi = lax.broadcasted_iota(jnp.int32, (ATT_BLK, ATT_BLK), 0)
    kj = lax.broadcasted_iota(jnp.int32, (ATT_BLK, ATT_BLK), 1)
    return kj <= qi


def _att_fwd(slabs, bias, gi, comm=None):
    dil, nb, hp = _att_geometry(gi)
    scale = ATT_DH ** -0.5

    def body(q_ref, k_ref, v_ref, bias_ref, o_ref, l_ref):
        r = pl.program_id(1)
        for n in range(nb):
            cur = _blk(dil, r, n)
            valid = _att_valid(n) if n > 0 else _att_valid_first()
            for h in range(hp):
                if n > 0:
                    prev = _blk(dil, r, n - 1)
                    kk = jnp.concatenate([k_ref[h, prev, :], k_ref[h, cur, :]], axis=0)
                    vv = jnp.concatenate([v_ref[h, prev, :], v_ref[h, cur, :]], axis=0)
                    bias = bias_ref[h]
                else:
                    kk, vv, bias = k_ref[h, cur, :], v_ref[h, cur, :], bias_ref[h, :, pl.ds(ATT_BLK, ATT_BLK)]
                s = _dot(q_ref[h, cur, :], kk, NT) * scale + bias
                s = jnp.where(valid, s, -1e30)
                mx = jnp.max(s, axis=-1, keepdims=True)
                e = jnp.exp(s - mx)
                den = jnp.sum(e, axis=-1, keepdims=True)
                o_ref[h, cur, :] = _dot(e / den, vv, NN)
                l_ref[h, cur, :] = jnp.broadcast_to(mx + jnp.log(den), (ATT_BLK, ATT_DH))

    osh = jax.ShapeDtypeStruct((ATT_HG, S, ATT_DH), F32)
    kw = dict(name=f"att_fwd{gi}", grid=(ATT_HG // hp, dil), in_specs=_slab_specs(gi) + [_bias_spec(gi)],
              out_specs=tuple(_head_specs(gi, 2)), out_shape=(osh, osh))
    if comm is not None:
        return _carry(body, comm, **kw)(slabs, slabs, slabs, bias)
    return _pcall(body, compiler_params=_params(("parallel", "arbitrary")), **kw)(slabs, slabs, slabs, bias)


def _att_bwd(slabs, bias, o, lse, do, dlse, gi, comm=None):
    dil, nb, hp = _att_geometry(gi)
    per = ATT_HG // hp
    scale = ATT_DH ** -0.5
    wh = hp * ATT_DH
    wide = lambda t: jnp.concatenate([t, t], axis=1)

    def body(q_ref, k_ref, v_ref, bias_ref, o_ref, l_ref, do_ref, dl_ref, dq_ref, dk_ref, dv_ref, ds_ref):
        r = pl.program_id(1)

        @pl.when(r == 0)
        def _():
            ds_ref[...] = jnp.zeros_like(ds_ref)

        for h in range(hp):
            sl = slice(h * ATT_DH, (h + 1) * ATT_DH)
            carry_k = carry_v = None
            for n in range(nb):
                cur = _blk(dil, r, n)
                q = q_ref[h, cur, :]
                dov = do_ref[h, cur, :]
                delta = jnp.sum(dov * o_ref[h, cur, :], axis=-1, keepdims=True)
                out_rows = pl.ds(n * ATT_BLK, ATT_BLK)
                if n == 0:
                    own = pl.ds(ATT_BLK, ATT_BLK)
                    kk, vv = k_ref[h, cur, :], v_ref[h, cur, :]
                    s = _dot(q, kk, NT) * scale + bias_ref[h, :, own]
                    p = jnp.where(_att_valid_first(), jnp.exp(s - l_ref[h, cur, :]), 0.0)
                    ds = p * (_dot(dov, vv, NT) - delta + dl_ref[h, cur, :])
                    ds_ref[h, :, own] += ds
                    dq_ref[out_rows, sl] = (_dot(ds, kk, NN) * scale).astype(BF16)
                    carry_k, carry_v = _dot(ds, q, TN) * scale, _dot(p, dov, TN)
                    continue
                prev = _blk(dil, r, n - 1)
                kk = jnp.concatenate([k_ref[h, prev, :], k_ref[h, cur, :]], axis=0)
                vv = jnp.concatenate([v_ref[h, prev, :], v_ref[h, cur, :]], axis=0)
                s = _dot(q, kk, NT) * scale + bias_ref[h]
                p = jnp.where(_att_valid(n), jnp.exp(s - wide(l_ref[h, cur, :])), 0.0)
                dp = _dot(dov, vv, NT)
                ds = p * (dp - delta + wide(dl_ref[h, cur, :]))
                ds_ref[h] += ds
                dq_ref[out_rows, sl] = (_dot(ds, kk, NN) * scale).astype(BF16)
                dkk = _dot(ds, q, TN) * scale
                dvv = _dot(p, dov, TN)
                before = pl.ds((n - 1) * ATT_BLK, ATT_BLK)
                dk_ref[before, sl] = (carry_k + dkk[:ATT_BLK]).astype(BF16)
                dv_ref[before, sl] = (carry_v + dvv[:ATT_BLK]).astype(BF16)
                carry_k, carry_v = dkk[ATT_BLK:], dvv[ATT_BLK:]
            last = pl.ds((nb - 1) * ATT_BLK, ATT_BLK)
            dk_ref[last, sl] = carry_k.astype(BF16)
            dv_ref[last, sl] = carry_v.astype(BF16)

    out_spec = pl.BlockSpec((S // dil, wh), lambda g, r: (0, r * per + g))
    osh = jax.ShapeDtypeStruct((S // dil, dil * AW), BF16)
    kw = dict(name=f"att_bwd{gi}", grid=(per, dil), in_specs=_slab_specs(gi) + [_bias_spec(gi)] + _head_specs(gi, 4),
              out_specs=(out_spec, out_spec, out_spec,
                         pl.BlockSpec((hp, ATT_BLK, 2 * ATT_BLK), lambda g, r: (g, 0, 0))),
              out_shape=(osh, osh, osh, jax.ShapeDtypeStruct((ATT_HG, ATT_BLK, 2 * ATT_BLK), F32)))
    args = (slabs, slabs, slabs, bias, o, lse, do, dlse)
    if comm is not None:
        return _carry(body, comm, **kw)(*args)
    return _pcall(body, compiler_params=_params(("arbitrary", "arbitrary")), **kw)(*args)


AW = ATT_HG * ATT_DH


def _mix_weights(l0, l1, l2):
    mx = jnp.maximum(jnp.maximum(l0, l1), l2)
    e0, e1, e2 = jnp.exp(l0 - mx), jnp.exp(l1 - mx), jnp.exp(l2 - mx)
    den = e0 + e1 + e2
    return e0 / den, e1 / den, e2 / den


def _heads_spec():
    return pl.BlockSpec((ATT_HG, TR, ATT_DH), lambda i: (0, i, 0))


def _mix_fwd(os_, ls, comm=None):
    def body(o0, o1, o2, l0, l1, l2, att_ref):
        for h in range(ATT_HG):
            w0, w1, w2 = _mix_weights(l0[h], l1[h], l2[h])
            att_ref[:, h * ATT_DH:(h + 1) * ATT_DH] = (w0 * o0[h] + w1 * o1[h] + w2 * o2[h]).astype(BF16)

    kw = dict(name="mix_fwd", grid=(S // TR,), in_specs=[_heads_spec()] * 6, out_specs=_row_spec(AW),
              out_shape=jax.ShapeDtypeStruct((S, AW), BF16))
    if comm is not None:
        return _carry(body, comm, **kw)(*os_, *ls)
    return _pcall(body, compiler_params=_params(("parallel",)), **kw)(*os_, *ls)


def _mix_bwd(os_, ls, datt):
    def body(o0, o1, o2, l0, l1, l2, da_ref, d0, d1, d2, e0, e1, e2):
        for h in range(ATT_HG):
            ws = _mix_weights(l0[h], l1[h], l2[h])
            da = da_ref[:, h * ATT_DH:(h + 1) * ATT_DH]
            dws = []
            for o_ref, w, d_ref in zip((o0, o1, o2), ws, (d0, d1, d2)):
                d_ref[h] = w * da
                dws.append(jnp.broadcast_to(jnp.sum(da * o_ref[h], axis=-1, keepdims=True), (TR, ATT_DH)))
            tot = ws[0] * dws[0] + ws[1] * dws[1] + ws[2] * dws[2]
            for w, dw, e_ref in zip(ws, dws, (e0, e1, e2)):
                e_ref[h] = w * (dw - tot)

    o = jax.ShapeDtypeStruct((ATT_HG, S, ATT_DH), F32)
    return _pcall(body, name="mix_bwd", grid=(S // TR,), in_specs=[_heads_spec()] * 6 + [_row_spec(AW)],
                  out_specs=(_heads_spec(),) * 6, out_shape=(o,) * 6,
                  compiler_params=_params(("parallel",)))(*os_, *ls, datt)


def _ada_fwd(c_all, w_sh, b_sl):
    def body(c_ref, w_ref, b_ref, o_ref):
        cv = c_ref[...]
        o_ref[...] = _dot(cv * jax.nn.sigmoid(cv), w_ref[...], NN) + b_ref[...]

    return _pcall(body, name="ada_fwd", out_shape=jax.ShapeDtypeStruct((N_DEV, w_sh.shape[1]), F32),
                  compiler_params=_params())(c_all, w_sh, b_sl)


def _ada_bwd(c_all, dm_sl):
    def body(c_ref, d_ref, o_ref):
        cv = c_ref[...]
        o_ref[...] = _dot(cv * jax.nn.sigmoid(cv), d_ref[...], TN)

    return _pcall(body, name="ada_bwd", out_shape=jax.ShapeDtypeStruct((D, dm_sl.shape[1]), F32),
                  compiler_params=_params())(c_all, dm_sl)


N_MOD = 6


def _sum_small(gathered):
    n = len(gathered)

    def body(*refs):
        ins, (gb_ref, dm_ref), outs = refs[:n], refs[n:n + 2], refs[n + 2:]

        def total(r):
            acc = r[0]
            for e in range(1, N_DEV):
                acc = acc + r[e]
            return acc

        for i in range(N_MOD):
            cols = slice(i * D, (i + 1) * D)
            gb_ref[:, cols] = total(ins[i])
            for e in range(N_DEV):
                dm_ref[e:e + 1, cols] = ins[i][e]
        for r, o_ref in zip(ins[N_MOD:], outs):
            o_ref[...] = total(r)

    shapes = (jax.ShapeDtypeStruct((1, N_MOD * D), F32), jax.ShapeDtypeStruct((N_DEV, N_MOD * D), F32),
              *[jax.ShapeDtypeStruct(g.shape[1:], F32) for g in gathered[N_MOD:]])
    res = _pcall(body, name="sum_small", out_shape=shapes, compiler_params=_params())(*gathered)
    return res[0], res[1], res[2:]


def _row_tile(m, n):
    t = max(8, min(m, (1 << 19) // n // 8 * 8))
    while m % t:
        t -= 8
    return t


def _pair_sum(full, recv, sel, name, col_block=0):
    _, m, n = recv.shape
    t = _row_tile(m, n)

    def body(sel_ref, a_ref, b_ref, o_ref):
        o_ref[...] = (a_ref[...].astype(F32) + b_ref[...].astype(F32)).astype(o_ref.dtype)

    gs = pltpu.PrefetchScalarGridSpec(
        num_scalar_prefetch=1, grid=(4, m // t),
        in_specs=[pl.BlockSpec((None, None, t, n), lambda q, i, s: (q, s[0], i, col_block)),
                  pl.BlockSpec((None, t, n), lambda q, i, s: (q, i, 0))],
        out_specs=pl.BlockSpec((None, t, n), lambda q, i, s: (q, i, 0)))
    return _pcall(body, name=name, grid_spec=gs, out_shape=jax.ShapeDtypeStruct((4, m, n), full.dtype),
                  compiler_params=_params(("parallel", "parallel")))(sel, full, recv)


def _chip_sum(part, recv, sel, name):
    _, m, n = part.shape
    t = _row_tile(m, n)

    def body(sel_ref, a_ref, r_ref, o_ref):
        o_ref[...] = ((a_ref[...].astype(F32) + r_ref[0].astype(F32)) + r_ref[1].astype(F32)) + r_ref[2].astype(F32)

    gs = pltpu.PrefetchScalarGridSpec(
        num_scalar_prefetch=1, grid=(m // t,),
        in_specs=[pl.BlockSpec((None, t, n), lambda i, s: (s[0], i, 0)),
                  pl.BlockSpec((3, t, n), lambda i, s: (0, i, 0))],
        out_specs=pl.BlockSpec((t, n), lambda i, s: (i, 0)))
    return _pcall(body, name=name, grid_spec=gs, out_shape=jax.ShapeDtypeStruct((m, n), F32),
                  compiler_params=_params(("parallel",)))(sel, part, recv)


def _adamw_math(w, g, m, v):
    nm = ADAM_B1 * m + (1.0 - ADAM_B1) * g
    nv = ADAM_B2 * v + (1.0 - ADAM_B2) * (g * g)
    m_hat = nm / (1.0 - ADAM_B1 ** ADAM_STEP)
    v_hat = nv / (1.0 - ADAM_B2 ** ADAM_STEP)
    return -ADAM_LR * (m_hat / (jnp.sqrt(v_hat) + ADAM_EPS) + ADAM_WD * w), nm, nv


def _adamw(w, g, m, v, name):
    _, rows, cols = w.shape
    t = _row_tile(rows, cols)

    def body(w_ref, g_ref, m_ref, v_ref, d_ref, nm_ref, nv_ref):
        d_ref[...], nm_ref[...], nv_ref[...] = _adamw_math(w_ref[...], g_ref[...], m_ref[...], v_ref[...])

    spec3 = pl.BlockSpec((None, t, cols), lambda i: (0, i, 0))
    spec2 = pl.BlockSpec((t, cols), lambda i: (i, 0))
    o = jax.ShapeDtypeStruct(w.shape, F32)
    return _pcall(body, name=name, grid=(rows // t,), in_specs=[spec3, spec2, spec3, spec3], out_specs=(spec3,) * 3,
                  out_shape=(o, o, o), compiler_params=_params(("parallel",)))(w, g, m, v)


def _adamw_reduced1(w, m, v, part, recv, sel, name):
    _, rows, cols = w.shape
    t = _row_tile(rows, cols)

    def body(sel_ref, w_ref, m_ref, v_ref, p_ref, r_ref, g_ref, d_ref, nm_ref, nv_ref):
        g = ((p_ref[...].astype(F32) + r_ref[0].astype(F32)) + r_ref[1].astype(F32)) + r_ref[2].astype(F32)
        g_ref[...] = g
        d_ref[...], nm_ref[...], nv_ref[...] = _adamw_math(w_ref[...], g, m_ref[...], v_ref[...])

    wspec = pl.BlockSpec((None, t, cols), lambda i, s: (0, i, 0))
    gs = pltpu.PrefetchScalarGridSpec(
        num_scalar_prefetch=1, grid=(rows // t,),
        in_specs=[wspec, wspec, wspec, pl.BlockSpec((None, t, cols), lambda i, s: (s[0], i, 0)),
                  pl.BlockSpec((3, t, cols), lambda i, s: (0, i, 0))],
        out_specs=(wspec,) * 4)
    o = jax.ShapeDtypeStruct(w.shape, F32)
    return _pcall(body, name=name, grid_spec=gs, out_shape=(o, o, o, o),
                  compiler_params=_params(("parallel",)))(sel, w, m, v, part, recv)


def _adamw_reduced(w, m, v, parts, recvs, sel):
    _, rows, cols = w.shape
    half = cols // 2
    t = _row_tile(rows, half)

    def body(sel_ref, w_ref, m_ref, v_ref, pa_ref, pb_ref, ra_ref, rb_ref, g_ref, d_ref, nm_ref, nv_ref):
        total = lambda p_ref, r_ref: ((p_ref[...].astype(F32) + r_ref[0].astype(F32)) + r_ref[1].astype(F32)) \
            + r_ref[2].astype(F32)
        g = jnp.where(pl.program_id(1) == 0, total(pa_ref, ra_ref), total(pb_ref, rb_ref))
        g_ref[...] = g
        d_ref[...], nm_ref[...], nv_ref[...] = _adamw_math(w_ref[...], g, m_ref[...], v_ref[...])

    wspec = pl.BlockSpec((None, t, half), lambda i, j, s: (0, i, j))
    pspec = pl.BlockSpec((None, t, half), lambda i, j, s: (s[0], i, 0))
    rspec = pl.BlockSpec((3, t, half), lambda i, j, s: (0, i, 0))
    gs = pltpu.PrefetchScalarGridSpec(num_scalar_prefetch=1, grid=(rows // t, 2),
                                      in_specs=[wspec, wspec, wspec, pspec, pspec, rspec, rspec],
                                      out_specs=(wspec,) * 4)
    o = jax.ShapeDtypeStruct(w.shape, F32)
    return _pcall(body, name="adamw_w_in", grid_spec=gs, out_shape=(o, o, o, o),
                  compiler_params=_params(("parallel", "arbitrary")))(sel, w, m, v, *parts, *recvs)


def _adamw_small(ws, gs, ms, vs):
    n = len(ws)

    def body(*refs):
        for i in range(n):
            w_ref, g_ref, m_ref, v_ref = (refs[k * n + i] for k in range(4))
            d, nm, nv = _adamw_math(w_ref[...], g_ref[...], m_ref[...], v_ref[...])
            refs[4 * n + i][...] = d
            refs[5 * n + i][...] = nm
            refs[6 * n + i][...] = nv

    shapes = tuple(jax.ShapeDtypeStruct(w.shape, F32) for w in ws)
    res = _pcall(body, name="adamw_small", out_shape=shapes * 3, compiler_params=_params())(*ws, *gs, *ms, *vs)
    return res[:n], res[n:2 * n], res[2 * n:]


def _mesh_pos():
    return lax.axis_index("x"), lax.axis_index("y"), lax.axis_index("c")


class _Gather:
    def __init__(self, arrs, relay=False, chunks=None):
        self.relay = relay
        self.ins = list(arrs)
        self.out_shape = tuple(jax.ShapeDtypeStruct((N_DEV,) + a.shape, a.dtype) for a in arrs)
        self.pieces = []
        for a, arr in enumerate(arrs):
            n = (chunks or {}).get(a, 1)
            rows = arr.shape[0] // n
            self.pieces += [(a, None if n == 1 else pl.ds(i * rows, rows)) for i in range(n)]
        npc = len(self.pieces)
        self.sems = [pltpu.SemaphoreType.DMA((7 * npc,)), pltpu.SemaphoreType.DMA((7 * npc,)),
                     pltpu.SemaphoreType.DMA((npc,))]

    def _copies(self, ins, outs, sems):
        send_sems, recv_sems, local_sems = sems
        x, y, c = _mesh_pos()
        me, sibling = (x, y, c), (x, y, 1 - c)
        chips = [(1 - x, y), (x, 1 - y), (1 - x, 1 - y)]

        def src_of(p):
            a, rows = self.pieces[p]
            return ins[a] if rows is None else ins[a].at[rows]

        def dst_of(p, block):
            a, rows = self.pieces[p]
            ref = outs[a].at[_slot(block)]
            return ref if rows is None else ref.at[rows]

        def copy(p, k, block, to, from_input=False):
            dst = dst_of(p, block)
            return pltpu.make_async_remote_copy(
                src_ref=src_of(p) if from_input else dst, dst_ref=dst, send_sem=send_sems.at[7 * p + k],
                recv_sem=recv_sems.at[7 * p + k], device_id=to, device_id_type=MESH)

        npc = len(self.pieces)
        mine = [pltpu.make_async_copy(src_of(p), dst_of(p, me), local_sems.at[p]) for p in range(npc)]
        direct = chips[:2] if self.relay else chips
        first = []
        for p in range(npc):
            first.append(copy(p, 0, me, sibling, from_input=True))
            first += [copy(p, 1 + j, me, (*chip, c), from_input=True) for j, chip in enumerate(direct)]
        return me, sibling, chips, c, copy, mine, first

    def start(self, ins, outs, sems):
        *_, mine, first = self._copies(ins, outs, sems)
        for cp in mine + first:
            cp.start()

    def finish(self, ins, outs, sems):
        me, sibling, chips, c, copy, mine, first = self._copies(ins, outs, sems)
        x, y = me[0], me[1]
        npc = len(self.pieces)
        passed = []

        def pass_on(cp):
            cp.start()
            passed.append(cp)

        for p in range(npc):
            for j in range(2):
                copy(p, 1 + j, (*chips[j], c), me).wait_recv()
                pass_on(copy(p, 4 + j, (*chips[j], c), sibling))
            if self.relay:
                owner = ((x + 1 - c) % 2, (y + c) % 2, c)
                pass_on(copy(p, 3, owner, ((x + c) % 2, (y + 1 - c) % 2, c)))
        for p in range(npc):
            copy(p, 3, (*chips[2], c), me).wait_recv()
            pass_on(copy(p, 6, (*chips[2], c), sibling))
        for p in range(npc):
            copy(p, 0, sibling, me).wait_recv()
            for j, chip in enumerate(chips):
                copy(p, 4 + j, (*chip, 1 - c), me).wait_recv()
        for cp in first + passed:
            cp.wait_send()
        for cp in mine:
            cp.wait()


class _ExchangeCore:
    def __init__(self, fulls, cols=None):
        self.ins = list(fulls)
        self.cols = cols
        width = lambda f: f.shape[3] if cols is None else cols[1]
        self.out_shape = tuple(jax.ShapeDtypeStruct((4, f.shape[2], width(f)), f.dtype) for f in fulls)
        self.sems = [pltpu.SemaphoreType.DMA((4 * len(fulls),)), pltpu.SemaphoreType.DMA((4 * len(fulls),))]

    def _copies(self, ins, outs, sems):
        send_sems, recv_sems = sems
        x, y, c = _mesh_pos()

        def src(a, q):
            ref = ins[a].at[q, 1 - c]
            return ref if self.cols is None else ref.at[:, pl.ds(*self.cols)]

        return [pltpu.make_async_remote_copy(
            src_ref=src(a, q), dst_ref=outs[a].at[q], send_sem=send_sems.at[4 * a + q],
            recv_sem=recv_sems.at[4 * a + q], device_id=(x, y, 1 - c), device_id_type=MESH)
            for a in range(len(self.ins)) for q in range(4)]

    def start(self, ins, outs, sems):
        for cp in self._copies(ins, outs, sems):
            cp.start()

    def finish(self, ins, outs, sems):
        for cp in self._copies(ins, outs, sems):
            cp.wait()


class _ExchangeChip:
    def __init__(self, parts):
        self.ins = list(parts)
        self.out_shape = tuple(jax.ShapeDtypeStruct((3,) + p.shape[1:], p.dtype) for p in parts)
        self.sems = [pltpu.SemaphoreType.DMA((3 * len(parts),)), pltpu.SemaphoreType.DMA((3 * len(parts),))]

    def _copies(self, ins, outs, sems):
        send_sems, recv_sems = sems
        x, y, c = _mesh_pos()
        chips = [(1 - x, y), (x, 1 - y), (1 - x, 1 - y)]
        return [pltpu.make_async_remote_copy(
            src_ref=ins[a].at[2 * px + py], dst_ref=outs[a].at[j], send_sem=send_sems.at[3 * a + j],
            recv_sem=recv_sems.at[3 * a + j], device_id=(px, py, c), device_id_type=MESH)
            for a in range(len(self.ins)) for j, (px, py) in enumerate(chips)]

    def start(self, ins, outs, sems):
        for cp in self._copies(ins, outs, sems):
            cp.start()

    def finish(self, ins, outs, sems):
        for cp in self._copies(ins, outs, sems):
            cp.wait()


HBM_ONLY = pl.BlockSpec(memory_space=pltpu.HBM)
SEM_SPEC = pl.BlockSpec(memory_space=pltpu.SEMAPHORE)
SIDE_EFFECT = pltpu.SideEffectType.DATAFLOW_SIDE_EFFECTING


def _chip_copies(p_refs, land_refs, send_sems, recv_sems):
    x, y, c = _mesh_pos()
    return [pltpu.make_async_remote_copy(
        src_ref=p_refs[a].at[2 * px + py], dst_ref=land_refs[a].at[j], send_sem=send_sems.at[3 * a + j],
        recv_sem=recv_sems.at[3 * a + j], device_id=(px, py, c), device_id_type=MESH)
        for a in range(len(p_refs)) for j, (px, py) in enumerate([(1 - x, y), (x, 1 - y), (1 - x, 1 - y)])]


def _chip_exchange_start(parts, name):
    n = len(parts)
    lands = [lax.empty((3,) + p.shape[1:], p.dtype) for p in parts]

    def body(*refs):
        p_refs, land_refs, (send_sems, recv_sems) = refs[:n], refs[n:2 * n], refs[2 * n:2 * n + 2]
        for cp in _chip_copies(p_refs, land_refs, send_sems, recv_sems):
            cp.start()
        token = refs[-1]
        token[...] = jnp.zeros_like(token)

    hbm = lambda t: pltpu.HBM(t.shape, t.dtype)
    res = pl.pallas_call(
        body, name=name,
        out_shape=(pltpu.SemaphoreType.DMA((3 * n,)), pltpu.SemaphoreType.DMA((3 * n,)), *[hbm(t) for t in parts + lands],
                   jax.ShapeDtypeStruct((8, 128), F32)),
        in_specs=(HBM_ONLY,) * (2 * n),
        out_specs=(SEM_SPEC, SEM_SPEC, *[HBM_ONLY] * (2 * n), pl.BlockSpec(memory_space=pltpu.VMEM)),
        input_output_aliases={i: 2 + i for i in range(2 * n)},
        compiler_params=pltpu.CompilerParams(has_side_effects=SIDE_EFFECT))(
        *[pltpu.with_memory_space_constraint(t, pltpu.HBM) for t in parts + lands])
    return (res[0], res[1], list(res[2:2 + n]), list(res[2 + n:2 + 2 * n])), res[-1]


def _chip_exchange_wait(in_flight, after, name):
    send_sems, recv_sems, parts, lands = in_flight
    n = len(parts)

    def body(*refs):
        p_refs, land_refs, (send_sems, recv_sems) = refs[:n], refs[n:2 * n], refs[2 * n:2 * n + 2]
        for cp in _chip_copies(p_refs, land_refs, send_sems, recv_sems):
            cp.wait_send()
            cp.wait_recv()

    res = pl.pallas_call(
        body, name=name, out_shape=tuple(pltpu.HBM(t.shape, t.dtype) for t in parts + lands),
        in_specs=(*[HBM_ONLY] * (2 * n), SEM_SPEC, SEM_SPEC, pl.BlockSpec(memory_space=pl.ANY)),
        out_specs=(HBM_ONLY,) * (2 * n), input_output_aliases={i: i for i in range(2 * n)},
        compiler_params=pltpu.CompilerParams(has_side_effects=SIDE_EFFECT))(*parts, *lands, send_sems, recv_sems, after)
    return list(res[:n]), list(res[n:])


def _slot(p):
    return 4 * p[0] + 2 * p[1] + p[2]


def _gather_copies(src_refs, out_refs, send_sems, recv_sems):
    x, y, c = _mesh_pos()
    targets = [(x, y, 1 - c), (1 - x, y, c), (x, 1 - y, c), (1 - x, 1 - y, c)]
    return [pltpu.make_async_remote_copy(
        src_ref=src_refs[a], dst_ref=out_refs[a].at[_slot((x, y, c))], send_sem=send_sems.at[4 * a + k],
        recv_sem=recv_sems.at[4 * a + k], device_id=to, device_id_type=MESH)
        for a in range(len(src_refs)) for k, to in enumerate(targets)]


def _gather_start(shards, after, name):
    n = len(shards)
    outs = [lax.empty((N_DEV,) + s.shape, s.dtype) for s in shards]

    def body(*refs):
        for cp in _gather_copies(refs[:n], refs[n:2 * n], refs[2 * n + 1], refs[2 * n + 2]):
            cp.start()
        token = refs[-1]
        token[...] = jnp.zeros_like(token)

    res = pl.pallas_call(
        body, name=name,
        out_shape=(pltpu.SemaphoreType.DMA((4 * n,)), pltpu.SemaphoreType.DMA((4 * n,)),
                   *[pltpu.HBM(t.shape, t.dtype) for t in shards + outs], jax.ShapeDtypeStruct((8, 128), F32)),
        in_specs=(*[HBM_ONLY] * (2 * n), pl.BlockSpec(memory_space=pl.ANY)),
        out_specs=(SEM_SPEC, SEM_SPEC, *[HBM_ONLY] * (2 * n), pl.BlockSpec(memory_space=pltpu.VMEM)),
        input_output_aliases={i: 2 + i for i in range(2 * n)},
        compiler_params=pltpu.CompilerParams(has_side_effects=SIDE_EFFECT))(
        *[pltpu.with_memory_space_constraint(t, pltpu.HBM) for t in shards + outs], after)
    return (res[0], res[1], list(res[2:2 + n]), list(res[2 + n:2 + 2 * n])), res[-1]


def _gather_wait(in_flight, after, name):
    send_sems, recv_sems, shards, outs = in_flight
    n = len(shards)

    def body(*refs):
        for cp in _gather_copies(refs[:n], refs[n:2 * n], refs[2 * n], refs[2 * n + 1]):
            cp.wait_send()
            cp.wait_recv()

    res = pl.pallas_call(
        body, name=name, out_shape=tuple(pltpu.HBM(t.shape, t.dtype) for t in shards + outs),
        in_specs=(*[HBM_ONLY] * (2 * n), SEM_SPEC, SEM_SPEC, pl.BlockSpec(memory_space=pl.ANY)),
        out_specs=(HBM_ONLY,) * (2 * n), input_output_aliases={i: i for i in range(2 * n)},
        compiler_params=pltpu.CompilerParams(has_side_effects=SIDE_EFFECT))(*shards, *outs, send_sems, recv_sems, after)
    return list(res[:n]), list(res[n:])


class _PassToSibling:
    def __init__(self, shards, gathered):
        n = self.n = len(shards)
        self.ins = list(shards) + list(gathered)
        self.out_shape = tuple(jax.ShapeDtypeStruct(g.shape, g.dtype) for g in gathered)
        self.aliases = {n + a: a for a in range(n)}
        self.sems = [pltpu.SemaphoreType.DMA((3 * n,)), pltpu.SemaphoreType.DMA((3 * n,)),
                     pltpu.SemaphoreType.DMA((n,))]

    def _copies(self, ins, outs, sems):
        send_sems, recv_sems, local_sems = sems
        x, y, c = _mesh_pos()
        chips = [(1 - x, y), (x, 1 - y), (1 - x, 1 - y)]
        mine = [pltpu.make_async_copy(ins[a], outs[a].at[_slot((x, y, c))], local_sems.at[a]) for a in range(self.n)]
        passed, awaited = [], []
        for a in range(self.n):
            for j, chip in enumerate(chips):
                sems_j = dict(send_sem=send_sems.at[3 * a + j], recv_sem=recv_sems.at[3 * a + j],
                              device_id=(x, y, 1 - c), device_id_type=MESH)
                blk = outs[a].at[_slot((*chip, c))]
                passed.append(pltpu.make_async_remote_copy(src_ref=blk, dst_ref=blk, **sems_j))
                got = outs[a].at[_slot((*chip, 1 - c))]
                awaited.append(pltpu.make_async_remote_copy(src_ref=got, dst_ref=got, **sems_j))
        return mine, passed, awaited

    def start(self, ins, outs, sems):
        mine, passed, _ = self._copies(ins, outs, sems)
        for cp in mine + passed:
            cp.start()

    def finish(self, ins, outs, sems):
        mine, passed, awaited = self._copies(ins, outs, sems)
        for cp in passed:
            cp.wait_send()
        for cp in awaited:
            cp.wait_recv()
        for cp in mine:
            cp.wait()


def _reduce_sums(fulls, recv_core, core, tag):
    return [_pair_sum(f, r, core, f"rs_pair_{tag}{i}") for i, (f, r) in enumerate(zip(fulls, recv_core))]


def _local_step(x, tgt, mods, w_in_t, shards, small, chip, core):
    sh1, sc1, g1, sh2, sc2, g2 = mods
    norm1_g, rel_bias, gn_g, gn_b, norm2_g, norm_f_g = small
    tables = _ret_tables()
    buckets = jnp.asarray(_bucket_tables())

    h1 = _norm_mod_fwd(x, norm1_g, sh1, sc1, "norm1_fwd")
    flight_w, token_w = _gather_start(list(shards), h1, "gather_w_start")
    proj, slabs = _proj(h1, w_in_t, token_w)
    gated, ro, states = _ret_fwd(proj, tables, gn_g, gn_b)
    bias = _bias_build(rel_bias, buckets)
    outs, lses = [], []
    for gi in range(len(ATT_GROUPS)):
        o, l = _att_fwd(slabs, bias, gi)
        outs.append(o)
        lses.append(l)
    att, gathered = _mix_fwd(outs, lses, comm=_PassToSibling(*_gather_wait(flight_w, lses[2], "gather_w_wait")))
    w_ret_out, w_att_out, w_o, w_ff1, w_ff2 = (_from_slots(g, ax) for g, ax in zip(gathered, BIG_AXES[1:]))
    ret_out = _mm(gated, w_ret_out, 'nn', tm=S, tn=256, tk=2048, name="ret_out")
    att_out, merged = _att_out_merge(att, w_att_out, proj, ret_out)
    mixo, x1 = _mm(merged, w_o, 'nn', tm=S, tn=256, tk=D, name="w_o", res=x, gvec=g1)
    h2 = _norm_mod_fwd(x1, norm2_g, sh2, sc2, "norm2_fwd")
    u, act = _mm(h2, w_ff1, 'nn', tm=S, tn=512, tk=D, name="ff1", relu2=True)
    f, x2 = _mm(act, w_ff2, 'nn', tm=1024, tn=512, tk=D_FF, name="ff2", res=x1, gvec=g2)
    loss, dx2, g_normf, df, dg2 = _final_loss(x2, tgt, norm_f_g, f, g2)

    gw_ff2 = _mm(act, df, 'tn', tm=512, tn=D, tk=S, name="gw_ff2", out_dtype=BF16)
    du = _mm(df, w_ff2, 'nt', tm=S, tn=512, tk=D, name="d_act", out_dtype=BF16, relu2_of=u)
    gw_ff1 = _mm(h2, du, 'tn', tm=D, tn=512, tk=S, name="gw_ff1", out_dtype=BF16)
    fulls_a = [_to_slots(g, ax) for g, ax in zip((gw_ff1, gw_ff2), BIG_AXES[4:])]
    dh2, recv_core_a = _mm(du, w_ff1, 'nt', tm=1024, tn=1024, tk=2048, name="dh2", comm=_ExchangeCore(fulls_a))
    parts_a = _reduce_sums(fulls_a, recv_core_a, core, "a")
    flight_a, token_a = _chip_exchange_start(parts_a, "rs_a_start")
    dx1, dsc2, dsh2, g_norm2, dmixo, dg1 = _norm_mod_bwd(x1, norm2_g, sc2, dh2, dx2, "norm2_bwd", gate=(mixo, g1))

    gw_o = _mm(merged, dmixo, 'tn', tm=D, tn=512, tk=S, name="gw_o", out_dtype=BF16, after=token_a)
    d_ret_out, d_att_out, dga, dgb = _dmerged_split(dmixo, w_o, proj, ret_out, att_out)
    gw_ret_out = _mm(gated, d_ret_out, 'tn', tm=512, tn=D, tk=S, name="gw_ret_out", out_dtype=BF16)
    gw_att_out = _mm(att, d_att_out, 'tn', tm=AW, tn=D, tk=S, name="gw_att_out", out_dtype=BF16)
    fulls_b = [_to_slots(g, ax) for g, ax in zip((gw_ret_out, gw_att_out, gw_o), BIG_AXES[1:4])]
    dgated, recv_core_b = _mm(d_ret_out, w_ret_out, 'nt', tm=S, tn=512, tk=D, name="dgated",
                              comm=_ExchangeCore(fulls_b))
    parts_b = _reduce_sums(fulls_b, recv_core_b, core, "b")
    flight_b, token_b = _chip_exchange_start(parts_b, "rs_b_start")
    datt = _mm(d_att_out, w_att_out, 'nt', tm=S, tn=AW, tk=D, name="datt", after=token_b)
    mix_grads = _mix_bwd(outs, lses, datt)
    datt_parts, ds_sums = [], []
    for gi in range(len(ATT_GROUPS)):
        dq, dk, dv, ds_sum = _att_bwd(slabs, bias, outs[gi], lses[gi], mix_grads[gi], mix_grads[3 + gi], gi)
        datt_parts += [dq.reshape(S, AW), dk.reshape(S, AW), dv.reshape(S, AW)]
        ds_sums.append(ds_sum)
    g_bias = _bias_grad(jnp.concatenate(ds_sums, axis=0), buckets)[:, :, 0].T.reshape(1, -1)
    dret, g_gn_g, g_gn_b = _ret_bwd(proj, tables, gn_g, gn_b, ro, states, dgated)
    parts_a, recv_chip_a = _chip_exchange_wait(flight_a, dret, "rs_a_wait")
    parts_b, recv_chip_b = _chip_exchange_wait(flight_b, dret, "rs_b_wait")
    reduced = list(zip(parts_b + parts_a, recv_chip_b + recv_chip_a))
    dproj = jnp.concatenate([dret] + datt_parts + [dga, dgb], axis=1)
    full_in = _to_slots(_mm(dproj, h1, 'tn', tm=512, tn=D, tk=S, name="gw_in", out_dtype=BF16), 0)
    in_flight, token = [], None
    for half in range(2):
        (recv_core_in,) = _run_comm(_ExchangeCore([full_in], cols=(half * (D // 2), D // 2)), f"rs_core_in{half}",
                                    after=token)
        part_in = [_pair_sum(full_in, recv_core_in, core, f"rs_pair_c{half}", col_block=half)]
        flight, token = _chip_exchange_start(part_in, f"rs_in{half}_start")
        in_flight.append(flight)
    dh1 = _mm(dproj, w_in_t, 'nn', tm=1024, tn=1024, tk=2560, name="dh1", after=token)
    gx, dsc1, dsh1, g_norm1 = _norm_mod_bwd(x, norm1_g, sc1, dh1, dx1, "norm1_bwd")

    dmod = [dsh1, dsc1, dg1, dsh2, dsc2, dg2]
    small_g = [g_norm1, g_bias, g_gn_g, g_gn_b, g_norm2, g_normf]
    return loss, gx, in_flight, reduced, small_g, dmod


def _to_slots(g, axis):
    if axis == 0:
        return g.reshape(4, 2, g.shape[0] // N_DEV, g.shape[1])
    return g.reshape(g.shape[0], N_DEV, g.shape[1] // N_DEV).transpose(1, 0, 2).reshape(4, 2, g.shape[0], -1)


def _from_slots(w8, axis):
    if axis == 0:
        return w8.reshape(-1, w8.shape[2])
    return w8.transpose(1, 0, 2).reshape(w8.shape[1], -1)


BIG_AXES = (1, 0, 1, 0, 1, 0)


def kernel(x, c, w_ada, b_ada, norm1_g, w_in, rel_bias, ret_gn_g, ret_gn_b, w_ret_out, w_att_out, w_o, norm2_g, w_ff1, w_ff2, norm_f_g, loss_target, m_w_ada, m_b_ada, m_norm1_g, m_w_in, m_rel_bias, m_ret_gn_g, m_ret_gn_b, m_w_ret_out, m_w_att_out, m_w_o, m_norm2_g, m_w_ff1, m_w_ff2, m_norm_f_g, v_w_ada, v_b_ada, v_norm1_g, v_w_in, v_rel_bias, v_ret_gn_g, v_ret_gn_b, v_w_ret_out, v_w_att_out, v_w_o, v_norm2_g, v_w_ff1, v_w_ff2, v_norm_f_g):
    mx, my, mc = _mesh_pos()
    dev = 4 * mx + 2 * my + mc
    chip = jnp.reshape(2 * mx + my, (1,)).astype(jnp.int32)
    core = jnp.reshape(mc, (1,)).astype(jnp.int32)
    ada_w = D * 6 // N_DEV

    w_in, m_w_in, v_w_in = (jnp.transpose(t, (0, 2, 1)) for t in (w_in, m_w_in, v_w_in))

    shards = [w[0].astype(BF16) for w in (w_in, w_ret_out, w_att_out, w_o, w_ff1, w_ff2)]
    c_all, w_in8 = _run_comm(_Gather([c, shards[0]], relay=True, chunks={1: 4}), "gather_c_w_in")
    c_all = c_all.reshape(N_DEV, D)
    b_sl = lax.dynamic_slice(b_ada, (0, dev * ada_w), (1, ada_w))
    (mod_all,) = _run_comm(_Gather([_ada_fwd(c_all, w_ada[0], b_sl)]), "gather_mod")
    mod = lax.dynamic_index_in_dim(mod_all, dev, axis=1, keepdims=False).reshape(6, D)
    mods = tuple(mod[i:i + 1] for i in range(6))

    small = (norm1_g, rel_bias, ret_gn_g, ret_gn_b, norm2_g, norm_f_g.reshape(1, D))
    loss, gx, in_flight, big_red, small_g, dmod = _local_step(x[0], loss_target[0], mods, w_in8.reshape(IN_COLS, D),
                                                              shards[1:], small, chip, core)

    gathered = _run_comm(_Gather(dmod + small_g + [loss]), "gather_small")
    g_b_ada, dmod_all, (g_norm1, g_bias, g_gn_g, g_gn_b, g_norm2, g_normf, loss_sum) = _sum_small(gathered)
    loss_out = loss_sum[0, 0]
    g_w_ada = _ada_bwd(c_all, lax.dynamic_slice(dmod_all, (0, dev * ada_w), (N_DEV, ada_w)))

    names = ['w_ada', 'b_ada', 'norm1_g', 'w_in', 'rel_bias', 'ret_gn_g', 'ret_gn_b', 'w_ret_out', 'w_att_out',
             'w_o', 'norm2_g', 'w_ff1', 'w_ff2', 'norm_f_g']
    ws = dict(zip(names, (w_ada, b_ada, norm1_g, w_in, rel_bias, ret_gn_g, ret_gn_b, w_ret_out, w_att_out, w_o,
                          norm2_g, w_ff1, w_ff2, norm_f_g)))
    ms = dict(zip(names, (m_w_ada, m_b_ada, m_norm1_g, m_w_in, m_rel_bias, m_ret_gn_g, m_ret_gn_b, m_w_ret_out,
                          m_w_att_out, m_w_o, m_norm2_g, m_w_ff1, m_w_ff2, m_norm_f_g)))
    vs = dict(zip(names, (v_w_ada, v_b_ada, v_norm1_g, v_w_in, v_rel_bias, v_ret_gn_g, v_ret_gn_b, v_w_ret_out,
                          v_w_att_out, v_w_o, v_norm2_g, v_w_ff1, v_w_ff2, v_norm_f_g)))
    grads = dict(w_ada=g_w_ada, b_ada=g_b_ada, norm1_g=g_norm1, rel_bias=g_bias,
                 ret_gn_g=g_gn_g, ret_gn_b=g_gn_b, norm2_g=g_norm2, norm_f_g=g_normf)
    delta, new_m, new_v = {}, {}, {}
    delta['w_ada'], new_m['w_ada'], new_v['w_ada'] = _adamw(w_ada, g_w_ada, m_w_ada, v_w_ada, "adamw_w_ada")
    grads['w_ada'] = g_w_ada.reshape(w_ada.shape)
    for n, (part, recv) in zip(('w_ret_out', 'w_att_out', 'w_o', 'w_ff1', 'w_ff2'), big_red):
        grads[n], delta[n], new_m[n], new_v[n] = _adamw_reduced1(ws[n], ms[n], vs[n], part, recv, chip, "adamw_" + n)
    small_names = ('b_ada', 'norm1_g', 'rel_bias', 'ret_gn_g', 'ret_gn_b', 'norm2_g', 'norm_f_g')
    two_d = {n: (1, ws[n].size) if ws[n].ndim == 1 else ws[n].shape for n in small_names}
    d_, m_, v_ = _adamw_small(*[[src[n].reshape(two_d[n]) for n in small_names] for src in (ws, grads, ms, vs)])
    for i, n in enumerate(small_names):
        shp = ws[n].shape
        delta[n], new_m[n], new_v[n] = d_[i].reshape(shp), m_[i].reshape(shp), v_[i].reshape(shp)
        grads[n] = grads[n].reshape(shp)

    done = lax.optimization_barrier((gx, tuple(d_), tuple(delta[n] for n in ('w_ada', 'w_ret_out', 'w_att_out', 'w_o',
                                                                               'w_ff1', 'w_ff2'))))
    parts_in, recvs_in = [], []
    for half, flight in enumerate(in_flight):
        (part_in,), (recv_chip_in,) = _chip_exchange_wait(flight, done[0], f"rs_in{half}_wait")
        parts_in.append(part_in)
        recvs_in.append(recv_chip_in)
    grads['w_in'], delta['w_in'], new_m['w_in'], new_v['w_in'] = _adamw_reduced(w_in, m_w_in, v_w_in, parts_in,
                                                                               recvs_in, chip)
    for d in (grads, delta, new_m, new_v):
        d['w_in'] = jnp.transpose(d['w_in'], (0, 2, 1))
    return (loss_out, gx[None], *[grads[n] for n in names], *[delta[n] for n in names],
            *[new_m[n] for n in names], *[new_v[n] for n in names])
```

```python
import functools
import math

import numpy as np
import jax
import jax.numpy as jnp
from jax import lax
from jax.experimental import pallas as pl
from jax.experimental.pallas import tpu as pltpu

F32 = jnp.float32
BF16 = jnp.bfloat16
MESH = pl.DeviceIdType.MESH

N_DEV = 8
S = 2048
D = 1024
RET_HEADS = 4
RET_DK = 256
RET_DV = 512
CHUNK = 128
N_CHUNK = S // CHUNK
ATT_GROUPS = ((128, 1), (512, 4), (2048, 16))
ATT_HG = 4
ATT_DH = 128
ATT_BLK = 128
N_BUCKETS = 32
MAX_DIST = 2048
D_FF = 4096
IN_COLS = 12800
OFF_RQ, OFF_RK, OFF_RV, OFF_RG, OFF_ATT = 0, 1024, 2048, 4096, 6144
OFF_GA, OFF_GB = 6144, 7168
RMS_EPS = 1e-6
GN_EPS = 1e-5
ADAM_LR, ADAM_B1, ADAM_B2, ADAM_EPS, ADAM_WD, ADAM_STEP = 0.001, 0.9, 0.999, 1e-08, 0.01, 10
VMEM_LIMIT = 48 * 1024 * 1024


def _pcall(body, **kw):
    return pl.pallas_call(body, **kw)


def _params(sem=None):
    return pltpu.CompilerParams(dimension_semantics=sem, vmem_limit_bytes=VMEM_LIMIT)


HBM_SPEC = pl.BlockSpec(memory_space=pl.ANY)


def _carry(body, comm, *, name, grid, in_specs, out_specs, out_shape, scratch_shapes=()):
    single = not isinstance(out_specs, (tuple, list))
    o_specs = (out_specs,) if single else tuple(out_specs)
    o_shape = (out_shape,) if single else tuple(out_shape)
    n_in, n_out, n_scr = len(in_specs), len(o_specs), len(scratch_shapes)
    nci, nco = len(comm.ins), len(comm.out_shape)
    total = int(np.prod(grid))

    def wrapped(*refs):
        bounds = np.cumsum([0, n_in, nci, n_out, nco, n_scr])
        a, ci, o, co, scr = (refs[bounds[i]:bounds[i + 1]] for i in range(5))
        sems = refs[bounds[5]:]
        flat = 0
        for d, g in enumerate(grid):
            flat = flat * g + pl.program_id(d)

        @pl.when(flat == 0)
        def _():
            comm.start(ci, co, sems)

        body(*a, *o, *scr)

        @pl.when(flat == total - 1)
        def _():
            comm.finish(ci, co, sems)

    aliases = {n_in + i: n_out + o for i, o in getattr(comm, "aliases", {}).items()}
    call = _pcall(wrapped, name=name, grid=grid, in_specs=list(in_specs) + [HBM_SPEC] * nci,
                  out_specs=o_specs + (HBM_SPEC,) * nco, out_shape=o_shape + tuple(comm.out_shape),
                  scratch_shapes=list(scratch_shapes) + list(comm.sems), input_output_aliases=aliases,
                  compiler_params=_params(("arbitrary",) * len(grid)))

    def run(*args):
        res = call(*args, *comm.ins)
        own = res[0] if single else tuple(res[:n_out])
        return own, tuple(res[n_out:])

    return run


def _run_comm(comm, name, after=None):
    nci, nco = len(comm.ins), len(comm.out_shape)
    extra = [] if after is None else [after]

    def body(*refs):
        ci, co, sems = refs[:nci], refs[nci + len(extra):nci + len(extra) + nco], refs[nci + len(extra) + nco:]
        comm.start(ci, co, sems)
        comm.finish(ci, co, sems)

    return _pcall(body, name=name, in_specs=[HBM_SPEC] * (nci + len(extra)), out_specs=(HBM_SPEC,) * nco,
                  out_shape=tuple(comm.out_shape), scratch_shapes=list(comm.sems))(*comm.ins, *extra)


def _dot(a, b, dn):
    return lax.dot_general(a.astype(BF16), b.astype(BF16), (dn, ((), ())), preferred_element_type=F32)


NN = ((1,), (0,))
NT = ((1,), (1,))
TN = ((0,), (0,))


def _mm(a, b, mode, *, tm, tn, tk, name, out_dtype=F32, res=None, gvec=None, relu2=False, relu2_of=None, comm=None,
        after=None):
    if mode == 'nn':
        (M, K), (_, N) = a.shape, b.shape
        a_spec = pl.BlockSpec((tm, tk), lambda i, j, k: (i, k))
        b_spec = pl.BlockSpec((tk, tn), lambda i, j, k: (k, j))
        dn = NN
    elif mode == 'nt':
        (M, K), (N, _) = a.shape, b.shape
        a_spec = pl.BlockSpec((tm, tk), lambda i, j, k: (i, k))
        b_spec = pl.BlockSpec((tn, tk), lambda i, j, k: (j, k))
        dn = NT
    else:
        (K, M), (_, N) = a.shape, b.shape
        a_spec = pl.BlockSpec((tk, tm), lambda i, j, k: (k, i))
        b_spec = pl.BlockSpec((tk, tn), lambda i, j, k: (k, j))
        dn = TN
    assert M % tm == 0 and N % tn == 0 and K % tk == 0, (name, M, N, K)
    nk = K // tk
    fused = res is not None
    o_spec = pl.BlockSpec((tm, tn), lambda i, j, k: (i, j))

    def body(a_ref, b_ref, *rest):
        acc_ref = rest[-1] if nk > 1 else None
        if after is not None:
            rest = rest[1:]
        if fused:
            res_ref, g_ref, o_ref, x_ref = rest[:4]
        elif relu2_of is not None:
            u_ref, o_ref = rest[:2]
        elif relu2:
            o_ref, act_ref = rest[:2]
        else:
            o_ref = rest[0]

        def finish(acc):
            if relu2_of is not None:
                acc = acc * (2.0 * jnp.maximum(u_ref[...], 0.0))
            o_ref[...] = acc.astype(o_ref.dtype)
            if fused:
                x_ref[...] = res_ref[...] + g_ref[...] * acc
            if relu2:
                r = jnp.maximum(acc, 0.0)
                act_ref[...] = (r * r).astype(BF16)

        p = _dot(a_ref[...], b_ref[...], dn)
        if nk == 1:
            finish(p)
        else:
            k = pl.program_id(2)

            @pl.when(k == 0)
            def _():
                acc_ref[...] = p

            @pl.when(k > 0)
            def _():
                acc_ref[...] += p

            @pl.when(k == nk - 1)
            def _():
                finish(acc_ref[...])

    in_specs = [a_spec, b_spec]
    args = [a, b]
    if after is not None:
        in_specs.append(pl.BlockSpec(memory_space=pl.ANY))
        args.append(after)
    out_shape = jax.ShapeDtypeStruct((M, N), out_dtype)
    out_specs = o_spec
    if fused:
        in_specs += [pl.BlockSpec((tm, tn), lambda i, j, k: (i, j)), pl.BlockSpec((1, tn), lambda i, j, k: (0, j))]
        args += [res, gvec]
        out_shape = (out_shape, jax.ShapeDtypeStruct((M, N), F32))
        out_specs = (o_spec, pl.BlockSpec((tm, tn), lambda i, j, k: (i, j)))
    elif relu2_of is not None:
        in_specs.append(pl.BlockSpec((tm, tn), lambda i, j, k: (i, j)))
        args.append(relu2_of)
    elif relu2:
        out_shape = (out_shape, jax.ShapeDtypeStruct((M, N), BF16))
        out_specs = (o_spec, pl.BlockSpec((tm, tn), lambda i, j, k: (i, j)))
    kw = dict(name=name, grid=(M // tm, N // tn, nk), in_specs=in_specs, out_specs=out_specs,
              out_shape=out_shape, scratch_shapes=[pltpu.VMEM((tm, tn), F32)] if nk > 1 else [])
    if comm is not None:
        return _carry(body, comm, **kw)(*args)
    return _pcall(body, compiler_params=_params(("parallel", "parallel", "arbitrary")), **kw)(*args)


PROJ_TN = 512
ATT_T0, ATT_T1 = 6144 // PROJ_TN, 10752 // PROJ_TN
N_SLABS = (ATT_T1 - ATT_T0) * 4
MAIN_COLS = IN_COLS - (ATT_T1 - ATT_T0) * PROJ_TN


def _proj(h1, w_in_t, after):
    nj = IN_COLS // PROJ_TN

    def body(a_ref, b_ref, after_ref, main_ref, slab_ref):
        j = pl.program_id(1)
        is_att = (j >= ATT_T0) & (j < ATT_T1)
        chunks = [pl.ds(c * 512, 512) for c in range(S // 512)]

        @pl.when(jnp.logical_not(is_att))
        def _():
            for rows in chunks:
                main_ref[rows, :] = _dot(a_ref[rows, :], b_ref[...], NT)

        @pl.when(is_att)
        def _():
            for rows in chunks:
                p = _dot(a_ref[rows, :], b_ref[...], NT)
                for h in range(4):
                    slab_ref[h, rows, :] = p[:, h * 128:(h + 1) * 128]

    main_idx = lambda j: jnp.where(j < ATT_T0, j, jnp.where(j < ATT_T1, ATT_T0 - 1, j - (ATT_T1 - ATT_T0)))
    slab_idx = lambda j: jnp.clip(j - ATT_T0, 0, ATT_T1 - ATT_T0 - 1)
    return _pcall(
        body, name="proj", grid=(1, nj, 1),
        in_specs=[pl.BlockSpec((S, D), lambda i, j, k: (0, 0)), pl.BlockSpec((PROJ_TN, D), lambda i, j, k: (j, 0)),
                  HBM_SPEC],
        out_specs=(pl.BlockSpec((S, PROJ_TN), lambda i, j, k: (0, main_idx(j))),
                   pl.BlockSpec((4, S, 128), lambda i, j, k: (slab_idx(j), 0, 0))),
        out_shape=(jax.ShapeDtypeStruct((S, MAIN_COLS), F32), jax.ShapeDtypeStruct((N_SLABS, S, 128), F32)),
        compiler_params=_params(("arbitrary",) * 3))(h1, w_in_t, after)


TR = 256


def _row_spec(w=D):
    return pl.BlockSpec((TR, w), lambda i: (i, 0))


def _vec_spec(w=D):
    return pl.BlockSpec((1, w), lambda i: (0, 0))


def _norm_mod_fwd(x, g, sh, sc, name):
    def body(x_ref, g_ref, sh_ref, sc_ref, o_ref):
        xv = x_ref[...]
        rstd = lax.rsqrt(jnp.mean(xv * xv, axis=-1, keepdims=True) + RMS_EPS)
        n = xv * rstd * g_ref[...]
        o_ref[...] = (n * (1.0 + sc_ref[...]) + sh_ref[...]).astype(BF16)

    return _pcall(body, name=name, grid=(S // TR,), in_specs=[_row_spec(), _vec_spec(), _vec_spec(), _vec_spec()],
                  out_specs=_row_spec(), out_shape=jax.ShapeDtypeStruct((S, D), BF16),
                  compiler_params=_params(("parallel",)))(x, g, sh, sc)


def _norm_mod_bwd(x, g, sc, dh, dres, name, gate=None):
    gated = gate is not None

    def body(x_ref, g_ref, sc_ref, dh_ref, dres_ref, *rest):
        if gated:
            f_ref, gv_ref, dx_ref, dsc_ref, dsh_ref, dg_ref, dz_ref, dgv_ref = rest
        else:
            dx_ref, dsc_ref, dsh_ref, dg_ref = rest
        i = pl.program_id(0)
        xv = x_ref[...]
        dh = dh_ref[...]
        rstd = lax.rsqrt(jnp.mean(xv * xv, axis=-1, keepdims=True) + RMS_EPS)
        xhat = xv * rstd
        gv = g_ref[...]
        dn = dh * (1.0 + sc_ref[...])
        dxhat = dn * gv
        dx = dres_ref[...] + rstd * (dxhat - xhat * jnp.mean(dxhat * xhat, axis=-1, keepdims=True))
        dx_ref[...] = dx
        sums = [(dsc_ref, jnp.sum(dh * (xhat * gv), axis=0, keepdims=True)),
                (dsh_ref, jnp.sum(dh, axis=0, keepdims=True)),
                (dg_ref, jnp.sum(dn * xhat, axis=0, keepdims=True))]
        if gated:
            dz_ref[...] = (dx * gv_ref[...]).astype(BF16)
            sums.append((dgv_ref, jnp.sum(dx * f_ref[...], axis=0, keepdims=True)))

        @pl.when(i == 0)
        def _():
            for ref, p in sums:
                ref[...] = p

        @pl.when(i > 0)
        def _():
            for ref, p in sums:
                ref[...] += p

    vec = jax.ShapeDtypeStruct((1, D), F32)
    in_specs = [_row_spec(), _vec_spec(), _vec_spec(), _row_spec(), _row_spec()]
    out_specs = [_row_spec(), _vec_spec(), _vec_spec(), _vec_spec()]
    out_shape = [jax.ShapeDtypeStruct((S, D), F32), vec, vec, vec]
    args = [x, g, sc, dh, dres]
    if gated:
        in_specs += [_row_spec(), _vec_spec()]
        out_specs += [_row_spec(), _vec_spec()]
        out_shape += [jax.ShapeDtypeStruct((S, D), BF16), vec]
        args += list(gate)
    return _pcall(body, name=name, grid=(S // TR,), in_specs=in_specs, out_specs=tuple(out_specs),
                  out_shape=tuple(out_shape), compiler_params=_params(("arbitrary",)))(*args)


DH1_TM, DH1_TK = 1024, 1280


def _dh1_norm(dproj, w_in_t, x, g, sc, dres, after):
    nk = IN_COLS // DH1_TK
    chunk = 256

    def body(a_ref, b_ref, x_ref, g_ref, sc_ref, dres_ref, after_ref, dx_ref, dsc_ref, dsh_ref, dg_ref, acc_ref):
        i, k = pl.program_id(0), pl.program_id(1)
        p = _dot(a_ref[...], b_ref[...], NN)

        @pl.when(k == 0)
        def _():
            acc_ref[...] = p

        @pl.when(k > 0)
        def _():
            acc_ref[...] += p

        @pl.when(k == nk - 1)
        def _():
            gv, scale = g_ref[...], 1.0 + sc_ref[...]
            sums = [jnp.zeros((1, D), F32)] * 3
            for c in range(DH1_TM // chunk):
                rows = pl.ds(c * chunk, chunk)
                dh, xv = acc_ref[rows, :], x_ref[rows, :]
                rstd = lax.rsqrt(jnp.mean(xv * xv, axis=-1, keepdims=True) + RMS_EPS)
                xhat = xv * rstd
                dn = dh * scale
                dxhat = dn * gv
                dx_ref[rows, :] = dres_ref[rows, :] + rstd * (dxhat - xhat * jnp.mean(dxhat * xhat, axis=-1, keepdims=True))
                parts = (dh * (xhat * gv), dh, dn * xhat)
                sums = [s + jnp.sum(t, axis=0, keepdims=True) for s, t in zip(sums, parts)]

            @pl.when(i == 0)
            def _():
                dsc_ref[...], dsh_ref[...], dg_ref[...] = sums

            @pl.when(i > 0)
            def _():
                dsc_ref[...] += sums[0]
                dsh_ref[...] += sums[1]
                dg_ref[...] += sums[2]

    tile = pl.BlockSpec((DH1_TM, D), lambda i, k: (i, 0))
    vecs = pl.BlockSpec((1, D), lambda i, k: (0, 0))
    vec = jax.ShapeDtypeStruct((1, D), F32)
    return _pcall(body, name="dh1_norm1_bwd", grid=(S // DH1_TM, nk),
                  in_specs=[pl.BlockSpec((DH1_TM, DH1_TK), lambda i, k: (i, k)),
                            pl.BlockSpec((DH1_TK, D), lambda i, k: (k, 0)), tile, vecs, vecs, tile, HBM_SPEC],
                  out_specs=(tile, vecs, vecs, vecs), out_shape=(jax.ShapeDtypeStruct((S, D), F32), vec, vec, vec),
                  scratch_shapes=[pltpu.VMEM((DH1_TM, D), F32)],
                  compiler_params=_params(("arbitrary", "arbitrary")))(dproj, w_in_t, x, g, sc, dres, after)


FF2_TM = 512


def _ff2_final(act, w_ff2, x1, g2, tgt, g):
    def body(a_ref, b_ref, x1_ref, g2_ref, t_ref, g_ref, loss_ref, dx_ref, dg_ref, df_ref, dg2_ref):
        i = pl.program_id(0)
        f = _dot(a_ref[...], b_ref[...], NN)
        g2v = g2_ref[...]
        xv = x1_ref[...] + g2v * f
        gv = g_ref[...]
        rstd = lax.rsqrt(jnp.mean(xv * xv, axis=-1, keepdims=True) + RMS_EPS)
        xhat = xv * rstd
        err = xhat * gv - t_ref[...]
        dy = err * (1.0 / D)
        dxhat = dy * gv
        dx = rstd * (dxhat - xhat * jnp.mean(dxhat * xhat, axis=-1, keepdims=True))
        dx_ref[...] = dx
        df_ref[...] = (dx * g2v).astype(BF16)
        p_g = jnp.sum(dy * xhat, axis=0, keepdims=True)
        p_g2 = jnp.sum(dx * f, axis=0, keepdims=True)
        p_l = jnp.zeros((1, 128), F32) + 0.5 * jnp.sum(jnp.mean(err * err, axis=-1, keepdims=True))

        @pl.when(i == 0)
        def _():
            dg_ref[...] = p_g
            dg2_ref[...] = p_g2
            loss_ref[...] = p_l

        @pl.when(i > 0)
        def _():
            dg_ref[...] += p_g
            dg2_ref[...] += p_g2
            loss_ref[...] += p_l

    vec = jax.ShapeDtypeStruct((1, D), F32)
    rows = lambda w: pl.BlockSpec((FF2_TM, w), lambda i: (i, 0))
    return _pcall(body, name="ff2_final", grid=(S // FF2_TM,),
                  in_specs=[rows(D_FF), pl.BlockSpec((D_FF, D), lambda i: (0, 0)), rows(D), _vec_spec(), rows(D),
                            _vec_spec()],
                  out_specs=(_vec_spec(128), rows(D), _vec_spec(), rows(D), _vec_spec()),
                  out_shape=(jax.ShapeDtypeStruct((1, 128), F32), jax.ShapeDtypeStruct((S, D), F32), vec,
                             jax.ShapeDtypeStruct((S, D), BF16), vec),
                  compiler_params=_params(("arbitrary",)))(act, w_ff2, x1, g2, tgt, g)


HALF = 512


MERGE_TM = 1024


def _merge_specs():
    blk = lambda off: pl.BlockSpec((MERGE_TM, HALF), lambda i, j: (i, off // HALF + j))
    return blk(OFF_GA), blk(OFF_GB), blk(0)


def _att_out_merge(att, w_att_out, proj, ret_out):
    def body(a_ref, b_ref, ga_ref, gb_ref, r_ref, o_ref, m_ref):
        acc = _dot(a_ref[...], b_ref[...], NN)
        o_ref[...] = acc
        m_ref[...] = (jax.nn.sigmoid(ga_ref[...]) * r_ref[...] + jax.nn.sigmoid(gb_ref[...]) * acc).astype(BF16)

    ga, gb, tile = _merge_specs()
    return _pcall(body, name="att_out", grid=(S // MERGE_TM, D // HALF),
                  in_specs=[pl.BlockSpec((MERGE_TM, AW), lambda i, j: (i, 0)), pl.BlockSpec((AW, HALF), lambda i, j: (0, j)),
                            ga, gb, tile],
                  out_specs=(tile, tile),
                  out_shape=(jax.ShapeDtypeStruct((S, D), F32), jax.ShapeDtypeStruct((S, D), BF16)),
                  compiler_params=_params(("parallel", "parallel")))(att, w_att_out, proj, proj, ret_out)


def _dmerged_split(dmixo, w_o, proj, ret_out, att_out):
    def body(a_ref, b_ref, ga_ref, gb_ref, r_ref, at_ref, dr_ref, da_ref, dga_ref, dgb_ref):
        dm = _dot(a_ref[...], b_ref[...], NT)
        sa = jax.nn.sigmoid(ga_ref[...])
        sb = jax.nn.sigmoid(gb_ref[...])
        dr_ref[...] = (dm * sa).astype(BF16)
        da_ref[...] = (dm * sb).astype(BF16)
        dga_ref[...] = (dm * r_ref[...] * (sa * (1.0 - sa))).astype(BF16)
        dgb_ref[...] = (dm * at_ref[...] * (sb * (1.0 - sb))).astype(BF16)

    ga, gb, tile = _merge_specs()
    o = jax.ShapeDtypeStruct((S, D), BF16)
    return _pcall(body, name="dmerged", grid=(S // MERGE_TM, D // HALF),
                  in_specs=[pl.BlockSpec((MERGE_TM, D), lambda i, j: (i, 0)), pl.BlockSpec((HALF, D), lambda i, j: (j, 0)),
                            ga, gb, tile, tile],
                  out_specs=(tile,) * 4, out_shape=(o, o, o, o),
                  compiler_params=_params(("parallel", "parallel")))(dmixo, w_o, proj, proj, ret_out, att_out)


def _ret_tables():
    H, C = RET_HEADS, CHUNK
    log_g = jnp.log1p(-(2.0 ** (-5.0 - jnp.arange(H, dtype=F32))))
    idx = jnp.arange(C, dtype=F32)
    rel = idx[:, None] - idx[None, :]
    inner = jnp.where(rel >= 0, jnp.exp(log_g[:, None, None] * jnp.maximum(rel, 0.0)), 0.0)
    qd = jnp.exp(log_g[:, None] * (idx + 1.0))[:, :, None]
    kd = jnp.exp(log_g[:, None] * (C - 1.0 - idx))[:, :, None]
    cd = jnp.broadcast_to(jnp.exp(log_g * C)[:, None, None], (H, 1, 128))
    half = RET_DK // 2
    inv = 10000.0 ** (-jnp.arange(half, dtype=F32) / half)
    ang = jnp.arange(S, dtype=F32)[:, None] * inv[None, :]
    return inner, qd, kd, cd, jnp.cos(ang), jnp.sin(ang)


def _rot(x, cos, sin):
    x1, x2 = x[:, :128], x[:, 128:]
    return jnp.concatenate([x1 * cos - x2 * sin, x1 * sin + x2 * cos], axis=1)


def _rot_t(d, cos, sin):
    d1, d2 = d[:, :128], d[:, 128:]
    return jnp.concatenate([d1 * cos + d2 * sin, d2 * cos - d1 * sin], axis=1)


RET_COLS = OFF_ATT
RET_VW = RET_HEADS * RET_DV


def _ret_specs(chunk_of):
    ci = chunk_of
    whole = lambda shape: pl.BlockSpec(shape, lambda t: (0,) * len(shape))
    return [
        pl.BlockSpec((CHUNK, RET_COLS), lambda t: (ci(t), 0)),
        pl.BlockSpec((CHUNK, 128), lambda t: (ci(t), 0)),
        pl.BlockSpec((CHUNK, 128), lambda t: (ci(t), 0)),
        whole((RET_HEADS, CHUNK, CHUNK)), whole((RET_HEADS, CHUNK, 1)), whole((RET_HEADS, CHUNK, 1)),
        whole((RET_HEADS, 1, 128)), whole((1, RET_VW)), whole((1, RET_VW)),
    ]


def _ret_cols(h):
    q = slice(OFF_RQ + h * RET_DK, OFF_RQ + (h + 1) * RET_DK)
    k = slice(OFF_RK + h * RET_DK, OFF_RK + (h + 1) * RET_DK)
    v = slice(OFF_RV + h * RET_DV, OFF_RV + (h + 1) * RET_DV)
    g = slice(OFF_RG + h * RET_DV, OFF_RG + (h + 1) * RET_DV)
    return q, k, v, g, slice(h * RET_DV, (h + 1) * RET_DV)


def _ret_fwd(proj, tables, gn_g, gn_b, comm=None):
    inner, qd, kd, cd, cos, sin = tables

    def body(x_ref, cos_ref, sin_ref, in_ref, qd_ref, kd_ref, cd_ref, g_ref, b_ref,
             gated_ref, ro_ref, st_ref, s_scr):
        i = pl.program_id(0)

        @pl.when(i == 0)
        def _():
            s_scr[...] = jnp.zeros_like(s_scr)

        cosv, sinv = cos_ref[...], sin_ref[...]
        for h in range(RET_HEADS):
            cq, ck, cv, cg, co = _ret_cols(h)
            q = _rot(x_ref[:, cq], cosv, sinv)
            k = _rot(x_ref[:, ck], cosv, sinv) * (RET_DK ** -0.5)
            v = x_ref[:, cv]
            st = s_scr[h]
            st_ref[h] = st.astype(BF16)
            s = _dot(q, k, NT) * in_ref[h]
            o = _dot(s, v, NN) + _dot(q, st, NN) * qd_ref[h]
            s_scr[h] = st * cd_ref[h, :, :1] + _dot(k * kd_ref[h], v, TN)
            ro_ref[:, co] = o
            mu = jnp.mean(o, axis=-1, keepdims=True)
            oc = o - mu
            var = jnp.mean(oc * oc, axis=-1, keepdims=True)
            rn = oc * lax.rsqrt(var + GN_EPS) * g_ref[:, co] + b_ref[:, co]
            rg = x_ref[:, cg]
            gated_ref[:, co] = (rg * jax.nn.sigmoid(rg) * rn).astype(BF16)

    ospec = pl.BlockSpec((CHUNK, RET_VW), lambda t: (t, 0))
    kw = dict(name="ret_fwd", grid=(N_CHUNK,), in_specs=_ret_specs(lambda t: t),
              out_specs=(ospec, ospec, pl.BlockSpec((RET_HEADS, None, RET_DK, RET_DV), lambda t: (0, t, 0, 0))),
              out_shape=(jax.ShapeDtypeStruct((S, RET_VW), BF16), jax.ShapeDtypeStruct((S, RET_VW), F32),
                         jax.ShapeDtypeStruct((RET_HEADS, N_CHUNK, RET_DK, RET_DV), BF16)),
              scratch_shapes=[pltpu.VMEM((RET_HEADS, RET_DK, RET_DV), F32)])
    args = (proj, cos, sin, inner, qd, kd, cd, gn_g, gn_b)
    if comm is not None:
        return _carry(body, comm, **kw)(*args)
    return _pcall(body, compiler_params=_params(("arbitrary",)), **kw)(*args)


def _ret_bwd(proj, tables, gn_g, gn_b, ro, states, dgated, comm=None):
    inner, qd, kd, cd, cos, sin = tables
    last = N_CHUNK - 1

    def body(x_ref, cos_ref, sin_ref, in_ref, qd_ref, kd_ref, cd_ref, g_ref, b_ref, ro_ref, st_ref, dg_ref,
             dx_ref, gg_ref, gb_ref, gs_scr):
        t = pl.program_id(0)

        @pl.when(t == 0)
        def _():
            gs_scr[...] = jnp.zeros_like(gs_scr)
            gg_ref[...] = jnp.zeros_like(gg_ref)
            gb_ref[...] = jnp.zeros_like(gb_ref)

        cosv, sinv = cos_ref[...], sin_ref[...]
        for h in range(RET_HEADS):
            cq, ck, cv, cg, co = _ret_cols(h)
            q = _rot(x_ref[:, cq], cosv, sinv)
            k = _rot(x_ref[:, ck], cosv, sinv) * (RET_DK ** -0.5)
            v = x_ref[:, cv]
            qdv, kdv, dm = qd_ref[h], kd_ref[h], in_ref[h]
            st = st_ref[h]
            o = ro_ref[:, co]
            gv = g_ref[:, co]
            mu = jnp.mean(o, axis=-1, keepdims=True)
            oc = o - mu
            rstd = lax.rsqrt(jnp.mean(oc * oc, axis=-1, keepdims=True) + GN_EPS)
            ohat = oc * rstd
            rn = ohat * gv + b_ref[:, co]
            rg = x_ref[:, cg]
            sg = jax.nn.sigmoid(rg)
            dgt = dg_ref[:, co]
            drn = dgt * (rg * sg)
            dx_ref[:, cg] = (dgt * rn * (sg * (1.0 + rg * (1.0 - sg)))).astype(BF16)
            gg_ref[:, co] += jnp.sum(drn * ohat, axis=0, keepdims=True)
            gb_ref[:, co] += jnp.sum(drn, axis=0, keepdims=True)
            dohat = drn * gv
            do = rstd * (dohat - jnp.mean(dohat, axis=-1, keepdims=True)
                         - ohat * jnp.mean(dohat * ohat, axis=-1, keepdims=True))
            gs = gs_scr[h]
            s = _dot(q, k, NT) * dm
            dsr = _dot(do, v, NT) * dm
            dq = _dot(dsr, k, NN) + _dot(do, st, NT) * qdv
            dk = _dot(dsr, q, TN) + _dot(v, gs, NT) * kdv
            dv = _dot(s, do, TN) + _dot(k * kdv, gs, NN)
            gs_scr[h] = gs * cd_ref[h, :, :1] + _dot(q * qdv, do, TN)
            dx_ref[:, cq] = _rot_t(dq, cosv, sinv).astype(BF16)
            dx_ref[:, ck] = (_rot_t(dk, cosv, sinv) * (RET_DK ** -0.5)).astype(BF16)
            dx_ref[:, cv] = dv.astype(BF16)

    rev = lambda t: last - t
    vblk = pl.BlockSpec((CHUNK, RET_VW), lambda t: (rev(t), 0))
    vspec = pl.BlockSpec((1, RET_VW), lambda t: (0, 0))
    kw = dict(name="ret_bwd", grid=(N_CHUNK,),
              in_specs=_ret_specs(rev) + [vblk, pl.BlockSpec((RET_HEADS, None, RET_DK, RET_DV),
                                                             lambda t: (0, rev(t), 0, 0)), vblk],
              out_specs=(pl.BlockSpec((CHUNK, RET_COLS), lambda t: (rev(t), 0)), vspec, vspec),
              out_shape=(jax.ShapeDtypeStruct((S, RET_COLS), BF16), jax.ShapeDtypeStruct((1, RET_VW), F32),
                         jax.ShapeDtypeStruct((1, RET_VW), F32)),
              scratch_shapes=[pltpu.VMEM((RET_HEADS, RET_DK, RET_DV), F32)])
    args = (proj, cos, sin, inner, qd, kd, cd, gn_g, gn_b, ro, states, dgated)
    if comm is not None:
        return _carry(body, comm, **kw)(*args)
    return _pcall(body, compiler_params=_params(("arbitrary",)), **kw)(*args)


def _bucket_tables():
    qi = np.arange(ATT_BLK)[:, None]
    kj = np.arange(2 * ATT_BLK)[None, :]
    m = ATT_BLK + qi - kj
    out = []
    for win, dil in ATT_GROUPS:
        w = win // dil
        dist = (np.clip(m, 0, w) * dil).astype(np.int32)
        max_exact = N_BUCKETS // 2
        d_f = np.maximum(dist, 1).astype(np.float32)
        large = max_exact + (np.log(d_f / np.float32(max_exact)) / np.float32(math.log(MAX_DIST / max_exact))
                             * np.float32(N_BUCKETS - max_exact)).astype(np.int32)
        large = np.minimum(large, N_BUCKETS - 1)
        out.append(np.where(dist < max_exact, dist, large).astype(np.int32))
    return np.stack(out)


def _bias_build(rel_bias, buckets):
    def body(tab_ref, bk_ref, o_ref):
        hh = pl.program_id(0)
        bk = bk_ref[...]
        acc = jnp.zeros((ATT_BLK, 2 * ATT_BLK), F32)
        for b in range(N_BUCKETS):
            acc = jnp.where(bk == b, tab_ref[b, hh], acc)
        o_ref[...] = acc

    nh = len(ATT_GROUPS) * ATT_HG
    return _pcall(body, name="bias_build", grid=(nh,),
                  in_specs=[pl.BlockSpec(memory_space=pltpu.SMEM),
                            pl.BlockSpec((None, ATT_BLK, 2 * ATT_BLK), lambda hh: (hh // ATT_HG, 0, 0))],
                  out_specs=pl.BlockSpec((None, ATT_BLK, 2 * ATT_BLK), lambda hh: (hh, 0, 0)),
                  out_shape=jax.ShapeDtypeStruct((nh, ATT_BLK, 2 * ATT_BLK), F32),
                  compiler_params=_params(("parallel",)))(rel_bias, buckets)


def _bias_grad(ds_sum, buckets):
    def body(ds_ref, bk_ref, o_ref):
        bk = bk_ref[...]
        ds = ds_ref[...]
        rows = lax.broadcasted_iota(jnp.int32, (N_BUCKETS, 128), 0)
        acc = jnp.zeros((N_BUCKETS, 128), F32)
        for b in range(N_BUCKETS):
            acc = jnp.where(rows == b, jnp.sum(jnp.where(bk == b, ds, 0.0)), acc)
        o_ref[...] = acc

    nh = len(ATT_GROUPS) * ATT_HG
    return _pcall(body, name="bias_grad", grid=(nh,),
                  in_specs=[pl.BlockSpec((None, ATT_BLK, 2 * ATT_BLK), lambda hh: (hh, 0, 0)),
                            pl.BlockSpec((None, ATT_BLK, 2 * ATT_BLK), lambda hh: (hh // ATT_HG, 0, 0))],
                  out_specs=pl.BlockSpec((None, N_BUCKETS, 128), lambda hh: (hh, 0, 0)),
                  out_shape=jax.ShapeDtypeStruct((nh, N_BUCKETS, 128), F32),
                  compiler_params=_params(("parallel",)))(ds_sum, buckets)


def _att_valid(n):
    qi = lax.broadcasted_iota(jnp.int32, (ATT_BLK, 2 * ATT_BLK), 0)
    kj = lax.broadcasted_iota(jnp.int32, (ATT_BLK, 2 * ATT_BLK), 1)
    m = ATT_BLK + qi - kj
    first_key = jnp.where(n > 0, 0, ATT_BLK)
    return (m >= 0) & (m <= ATT_BLK) & (kj >= first_key)


ATT_HP = (1, 2, 2)


def _att_geometry(gi):
    _, dil = ATT_GROUPS[gi]
    return dil, S // dil // ATT_BLK, ATT_HP[gi]


def _blk(dil, r, n):
    if dil == 1:
        return pl.ds(n * ATT_BLK, ATT_BLK)
    return pl.ds(r + n * ATT_BLK * dil, ATT_BLK, stride=dil)


def _slab_specs(gi):
    _, _, hp = _att_geometry(gi)
    per = ATT_HG // hp
    return [pl.BlockSpec((hp, S, ATT_DH), lambda g, r, part=part: ((3 * gi + part) * per + g, 0, 0))
            for part in range(3)]


def _head_specs(gi, count):
    _, _, hp = _att_geometry(gi)
    return [pl.BlockSpec((hp, S, ATT_DH), lambda g, r: (g, 0, 0))] * count


def _bias_spec(gi):
    _, _, hp = _att_geometry(gi)
    return pl.BlockSpec((hp, ATT_BLK, 2 * ATT_BLK), lambda g, r: (gi * (ATT_HG // hp) + g, 0, 0))


def _att_valid_first():
    qi = lax.broadcasted_iota(jnp.int32, (ATT_BLK, ATT_BLK), 0)
    kj = lax.broadcasted_iota(jnp.int32, (ATT_BLK, ATT_BLK), 1)
    return kj <= qi


def _att_fwd(slabs, bias, gi, comm=None):
    dil, nb, hp = _att_geometry(gi)
    scale = ATT_DH ** -0.5

    def body(q_ref, k_ref, v_ref, bias_ref, o_ref, l_ref):
        r = pl.program_id(1)
        for n in range(nb):
            cur = _blk(dil, r, n)
            valid = _att_valid(n) if n > 0 else _att_valid_first()
            for h in range(hp):
                if n > 0:
                    prev = _blk(dil, r, n - 1)
                    kk = jnp.concatenate([k_ref[h, prev, :], k_ref[h, cur, :]], axis=0)
                    vv = jnp.concatenate([v_ref[h, prev, :], v_ref[h, cur, :]], axis=0)
                    bias = bias_ref[h]
                else:
                    kk, vv, bias = k_ref[h, cur, :], v_ref[h, cur, :], bias_ref[h, :, pl.ds(ATT_BLK, ATT_BLK)]
                s = _dot(q_ref[h, cur, :], kk, NT) * scale + bias
                s = jnp.where(valid, s, -1e30)
                mx = jnp.max(s, axis=-1, keepdims=True)
                e = jnp.exp(s - mx)
                den = jnp.sum(e, axis=-1, keepdims=True)
                o_ref[h, cur, :] = _dot(e / den, vv, NN)
                l_ref[h, cur, :] = jnp.broadcast_to(mx + jnp.log(den), (ATT_BLK, ATT_DH))

    osh = jax.ShapeDtypeStruct((ATT_HG, S, ATT_DH), F32)
    kw = dict(name=f"att_fwd{gi}", grid=(ATT_HG // hp, dil), in_specs=_slab_specs(gi) + [_bias_spec(gi)],
              out_specs=tuple(_head_specs(gi, 2)), out_shape=(osh, osh))
    if comm is not None:
        return _carry(body, comm, **kw)(slabs, slabs, slabs, bias)
    return _pcall(body, compiler_params=_params(("parallel", "arbitrary")), **kw)(slabs, slabs, slabs, bias)


def _att_bwd(slabs, bias, o, lse, do, dlse, gi, comm=None):
    dil, nb, hp = _att_geometry(gi)
    per = ATT_HG // hp
    scale = ATT_DH ** -0.5
    wh = hp * ATT_DH
    wide = lambda t: jnp.concatenate([t, t], axis=1)

    def body(q_ref, k_ref, v_ref, bias_ref, o_ref, l_ref, do_ref, dl_ref, dq_ref, dk_ref, dv_ref, ds_ref):
        r = pl.program_id(1)

        @pl.when(r == 0)
        def _():
            ds_ref[...] = jnp.zeros_like(ds_ref)

        for h in range(hp):
            sl = slice(h * ATT_DH, (h + 1) * ATT_DH)
            carry_k = carry_v = None
            for n in range(nb):
                cur = _blk(dil, r, n)
                q = q_ref[h, cur, :]
                dov = do_ref[h, cur, :]
                delta = jnp.sum(dov * o_ref[h, cur, :], axis=-1, keepdims=True)
                out_rows = pl.ds(n * ATT_BLK, ATT_BLK)
                if n == 0:
                    own = pl.ds(ATT_BLK, ATT_BLK)
                    kk, vv = k_ref[h, cur, :], v_ref[h, cur, :]
                    s = _dot(q, kk, NT) * scale + bias_ref[h, :, own]
                    p = jnp.where(_att_valid_first(), jnp.exp(s - l_ref[h, cur, :]), 0.0)
                    ds = p * (_dot(dov, vv, NT) - delta + dl_ref[h, cur, :])
                    ds_ref[h, :, own] += ds
                    dq_ref[out_rows, sl] = (_dot(ds, kk, NN) * scale).astype(BF16)
                    carry_k, carry_v = _dot(ds, q, TN) * scale, _dot(p, dov, TN)
                    continue
                prev = _blk(dil, r, n - 1)
                kk = jnp.concatenate([k_ref[h, prev, :], k_ref[h, cur, :]], axis=0)
                vv = jnp.concatenate([v_ref[h, prev, :], v_ref[h, cur, :]], axis=0)
                s = _dot(q, kk, NT) * scale + bias_ref[h]
                p = jnp.where(_att_valid(n), jnp.exp(s - wide(l_ref[h, cur, :])), 0.0)
                dp = _dot(dov, vv, NT)
                ds = p * (dp - delta + wide(dl_ref[h, cur, :]))
                ds_ref[h] += ds
                dq_ref[out_rows, sl] = (_dot(ds, kk, NN) * scale).astype(BF16)
                dkk = _dot(ds, q, TN) * scale
                dvv = _dot(p, dov, TN)
                before = pl.ds((n - 1) * ATT_BLK, ATT_BLK)
                dk_ref[before, sl] = (carry_k + dkk[:ATT_BLK]).astype(BF16)
                dv_ref[before, sl] = (carry_v + dvv[:ATT_BLK]).astype(BF16)
                carry_k, carry_v = dkk[ATT_BLK:], dvv[ATT_BLK:]
            last = pl.ds((nb - 1) * ATT_BLK, ATT_BLK)
            dk_ref[last, sl] = carry_k.astype(BF16)
            dv_ref[last, sl] = carry_v.astype(BF16)

    out_spec = pl.BlockSpec((S // dil, wh), lambda g, r: (0, r * per + g))
    osh = jax.ShapeDtypeStruct((S // dil, dil * AW), BF16)
    kw = dict(name=f"att_bwd{gi}", grid=(per, dil), in_specs=_slab_specs(gi) + [_bias_spec(gi)] + _head_specs(gi, 4),
              out_specs=(out_spec, out_spec, out_spec,
                         pl.BlockSpec((hp, ATT_BLK, 2 * ATT_BLK), lambda g, r: (g, 0, 0))),
              out_shape=(osh, osh, osh, jax.ShapeDtypeStruct((ATT_HG, ATT_BLK, 2 * ATT_BLK), F32)))
    args = (slabs, slabs, slabs, bias, o, lse, do, dlse)
    if comm is not None:
        return _carry(body, comm, **kw)(*args)
    return _pcall(body, compiler_params=_params(("arbitrary", "arbitrary")), **kw)(*args)


AW = ATT_HG * ATT_DH


def _mix_weights(l0, l1, l2):
    mx = jnp.maximum(jnp.maximum(l0, l1), l2)
    e0, e1, e2 = jnp.exp(l0 - mx), jnp.exp(l1 - mx), jnp.exp(l2 - mx)
    den = e0 + e1 + e2
    return e0 / den, e1 / den, e2 / den


def _heads_spec():
    return pl.BlockSpec((ATT_HG, TR, ATT_DH), lambda i: (0, i, 0))


def _mix_fwd(os_, ls, comm=None):
    def body(o0, o1, o2, l0, l1, l2, att_ref):
        for h in range(ATT_HG):
            w0, w1, w2 = _mix_weights(l0[h], l1[h], l2[h])
            att_ref[:, h * ATT_DH:(h + 1) * ATT_DH] = (w0 * o0[h] + w1 * o1[h] + w2 * o2[h]).astype(BF16)

    kw = dict(name="mix_fwd", grid=(S // TR,), in_specs=[_heads_spec()] * 6, out_specs=_row_spec(AW),
              out_shape=jax.ShapeDtypeStruct((S, AW), BF16))
    if comm is not None:
        return _carry(body, comm, **kw)(*os_, *ls)
    return _pcall(body, compiler_params=_params(("parallel",)), **kw)(*os_, *ls)


def _mix_bwd(os_, ls, datt):
    def body(o0, o1, o2, l0, l1, l2, da_ref, d0, d1, d2, e0, e1, e2):
        for h in range(ATT_HG):
            ws = _mix_weights(l0[h], l1[h], l2[h])
            da = da_ref[:, h * ATT_DH:(h + 1) * ATT_DH]
            dws = []
            for o_ref, w, d_ref in zip((o0, o1, o2), ws, (d0, d1, d2)):
                d_ref[h] = w * da
                dws.append(jnp.broadcast_to(jnp.sum(da * o_ref[h], axis=-1, keepdims=True), (TR, ATT_DH)))
            tot = ws[0] * dws[0] + ws[1] * dws[1] + ws[2] * dws[2]
            for w, dw, e_ref in zip(ws, dws, (e0, e1, e2)):
                e_ref[h] = w * (dw - tot)

    o = jax.ShapeDtypeStruct((ATT_HG, S, ATT_DH), F32)
    return _pcall(body, name="mix_bwd", grid=(S // TR,), in_specs=[_heads_spec()] * 6 + [_row_spec(AW)],
                  out_specs=(_heads_spec(),) * 6, out_shape=(o,) * 6,
                  compiler_params=_params(("parallel",)))(*os_, *ls, datt)


def _ada_fwd(c_all, w_sh, b_sl):
    def body(c_ref, w_ref, b_ref, o_ref):
        cv = c_ref[...]
        o_ref[...] = _dot(cv * jax.nn.sigmoid(cv), w_ref[...], NN) + b_ref[...]

    return _pcall(body, name="ada_fwd", out_shape=jax.ShapeDtypeStruct((N_DEV, w_sh.shape[1]), F32),
                  compiler_params=_params())(c_all, w_sh, b_sl)


def _ada_bwd(c_all, dm_sl):
    def body(c_ref, d_ref, o_ref):
        cv = c_ref[...]
        o_ref[...] = _dot(cv * jax.nn.sigmoid(cv), d_ref[...], TN)

    return _pcall(body, name="ada_bwd", out_shape=jax.ShapeDtypeStruct((D, dm_sl.shape[1]), F32),
                  compiler_params=_params())(c_all, dm_sl)


N_MOD = 6


def _sum_small(gathered):
    n = len(gathered)

    def body(*refs):
        ins, (gb_ref, dm_ref), outs = refs[:n], refs[n:n + 2], refs[n + 2:]

        def total(r):
            acc = r[0]
            for e in range(1, N_DEV):
                acc = acc + r[e]
            return acc

        for i in range(N_MOD):
            cols = slice(i * D, (i + 1) * D)
            gb_ref[:, cols] = total(ins[i])
            for e in range(N_DEV):
                dm_ref[e:e + 1, cols] = ins[i][e]
        for r, o_ref in zip(ins[N_MOD:], outs):
            o_ref[...] = total(r)

    shapes = (jax.ShapeDtypeStruct((1, N_MOD * D), F32), jax.ShapeDtypeStruct((N_DEV, N_MOD * D), F32),
              *[jax.ShapeDtypeStruct(g.shape[1:], F32) for g in gathered[N_MOD:]])
    res = _pcall(body, name="sum_small", out_shape=shapes, compiler_params=_params())(*gathered)
    return res[0], res[1], res[2:]


def _row_tile(m, n):
    t = max(8, min(m, (1 << 19) // n // 8 * 8))
    while m % t:
        t -= 8
    return t


def _pair_sum(full, recv, sel, name, col_block=0):
    _, m, n = recv.shape
    t = _row_tile(m, n)

    def body(sel_ref, a_ref, b_ref, o_ref):
        o_ref[...] = (a_ref[...].astype(F32) + b_ref[...].astype(F32)).astype(o_ref.dtype)

    gs = pltpu.PrefetchScalarGridSpec(
        num_scalar_prefetch=1, grid=(4, m // t),
        in_specs=[pl.BlockSpec((None, None, t, n), lambda q, i, s: (q, s[0], i, col_block)),
                  pl.BlockSpec((None, t, n), lambda q, i, s: (q, i, 0))],
        out_specs=pl.BlockSpec((None, t, n), lambda q, i, s: (q, i, 0)))
    return _pcall(body, name=name, grid_spec=gs, out_shape=jax.ShapeDtypeStruct((4, m, n), full.dtype),
                  compiler_params=_params(("parallel", "parallel")))(sel, full, recv)


def _chip_sum(part, recv, sel, name):
    _, m, n = part.shape
    t = _row_tile(m, n)

    def body(sel_ref, a_ref, r_ref, o_ref):
        o_ref[...] = ((a_ref[...].astype(F32) + r_ref[0].astype(F32)) + r_ref[1].astype(F32)) + r_ref[2].astype(F32)

    gs = pltpu.PrefetchScalarGridSpec(
        num_scalar_prefetch=1, grid=(m // t,),
        in_specs=[pl.BlockSpec((None, t, n), lambda i, s: (s[0], i, 0)),
                  pl.BlockSpec((3, t, n), lambda i, s: (0, i, 0))],
        out_specs=pl.BlockSpec((t, n), lambda i, s: (i, 0)))
    return _pcall(body, name=name, grid_spec=gs, out_shape=jax.ShapeDtypeStruct((m, n), F32),
                  compiler_params=_params(("parallel",)))(sel, part, recv)


def _adamw_math(w, g, m, v):
    nm = ADAM_B1 * m + (1.0 - ADAM_B1) * g
    nv = ADAM_B2 * v + (1.0 - ADAM_B2) * (g * g)
    m_hat = nm / (1.0 - ADAM_B1 ** ADAM_STEP)
    v_hat = nv / (1.0 - ADAM_B2 ** ADAM_STEP)
    return -ADAM_LR * (m_hat / (jnp.sqrt(v_hat) + ADAM_EPS) + ADAM_WD * w), nm, nv


def _adamw(w, g, m, v, name):
    _, rows, cols = w.shape
    t = _row_tile(rows, cols)

    def body(w_ref, g_ref, m_ref, v_ref, d_ref, nm_ref, nv_ref):
        d_ref[...], nm_ref[...], nv_ref[...] = _adamw_math(w_ref[...], g_ref[...], m_ref[...], v_ref[...])

    spec3 = pl.BlockSpec((None, t, cols), lambda i: (0, i, 0))
    spec2 = pl.BlockSpec((t, cols), lambda i: (i, 0))
    o = jax.ShapeDtypeStruct(w.shape, F32)
    return _pcall(body, name=name, grid=(rows // t,), in_specs=[spec3, spec2, spec3, spec3], out_specs=(spec3,) * 3,
                  out_shape=(o, o, o), compiler_params=_params(("parallel",)))(w, g, m, v)


def _adamw_reduced1(w, m, v, part, recv, sel, name):
    _, rows, cols = w.shape
    t = _row_tile(rows, cols)

    def body(sel_ref, w_ref, m_ref, v_ref, p_ref, r_ref, g_ref, d_ref, nm_ref, nv_ref):
        g = ((p_ref[...].astype(F32) + r_ref[0].astype(F32)) + r_ref[1].astype(F32)) + r_ref[2].astype(F32)
        g_ref[...] = g
        d_ref[...], nm_ref[...], nv_ref[...] = _adamw_math(w_ref[...], g, m_ref[...], v_ref[...])

    wspec = pl.BlockSpec((None, t, cols), lambda i, s: (0, i, 0))
    gs = pltpu.PrefetchScalarGridSpec(
        num_scalar_prefetch=1, grid=(rows // t,),
        in_specs=[wspec, wspec, wspec, pl.BlockSpec((None, t, cols), lambda i, s: (s[0], i, 0)),
                  pl.BlockSpec((3, t, cols), lambda i, s: (0, i, 0))],
        out_specs=(wspec,) * 4)
    o = jax.ShapeDtypeStruct(w.shape, F32)
    return _pcall(body, name=name, grid_spec=gs, out_shape=(o, o, o, o),
                  compiler_params=_params(("parallel",)))(sel, w, m, v, part, recv)


def _adamw_reduced(w, m, v, parts, recvs, sel):
    _, rows, cols = w.shape
    half = cols // 2
    t = _row_tile(rows, half)

    def body(sel_ref, w_ref, m_ref, v_ref, pa_ref, pb_ref, ra_ref, rb_ref, g_ref, d_ref, nm_ref, nv_ref):
        total = lambda p_ref, r_ref: ((p_ref[...].astype(F32) + r_ref[0].astype(F32)) + r_ref[1].astype(F32)) \
            + r_ref[2].astype(F32)
        g = jnp.where(pl.program_id(1) == 0, total(pa_ref, ra_ref), total(pb_ref, rb_ref))
        g_ref[...] = g
        d_ref[...], nm_ref[...], nv_ref[...] = _adamw_math(w_ref[...], g, m_ref[...], v_ref[...])

    wspec = pl.BlockSpec((None, t, half), lambda i, j, s: (0, i, j))
    pspec = pl.BlockSpec((None, t, half), lambda i, j, s: (s[0], i, 0))
    rspec = pl.BlockSpec((3, t, half), lambda i, j, s: (0, i, 0))
    gs = pltpu.PrefetchScalarGridSpec(num_scalar_prefetch=1, grid=(rows // t, 2),
                                      in_specs=[wspec, wspec, wspec, pspec, pspec, rspec, rspec],
                                      out_specs=(wspec,) * 4)
    o = jax.ShapeDtypeStruct(w.shape, F32)
    return _pcall(body, name="adamw_w_in", grid_spec=gs, out_shape=(o, o, o, o),
                  compiler_params=_params(("parallel", "arbitrary")))(sel, w, m, v, *parts, *recvs)


def _adamw_small(ws, gs, ms, vs):
    n = len(ws)

    def body(*refs):
        for i in range(n):
            w_ref, g_ref, m_ref, v_ref = (refs[k * n + i] for k in range(4))
            d, nm, nv = _adamw_math(w_ref[...], g_ref[...], m_ref[...], v_ref[...])
            refs[4 * n + i][...] = d
            refs[5 * n + i][...] = nm
            refs[6 * n + i][...] = nv

    shapes = tuple(jax.ShapeDtypeStruct(w.shape, F32) for w in ws)
    res = _pcall(body, name="adamw_small", out_shape=shapes * 3, compiler_params=_params())(*ws, *gs, *ms, *vs)
    return res[:n], res[n:2 * n], res[2 * n:]


def _mesh_pos():
    return lax.axis_index("x"), lax.axis_index("y"), lax.axis_index("c")


class _Gather:
    def __init__(self, arrs, relay=False, chunks=None):
        self.relay = relay
        self.ins = list(arrs)
        self.out_shape = tuple(jax.ShapeDtypeStruct((N_DEV,) + a.shape, a.dtype) for a in arrs)
        self.pieces = []
        for a, arr in enumerate(arrs):
            n = (chunks or {}).get(a, 1)
            rows = arr.shape[0] // n
            self.pieces += [(a, None if n == 1 else pl.ds(i * rows, rows)) for i in range(n)]
        npc = len(self.pieces)
        self.sems = [pltpu.SemaphoreType.DMA((7 * npc,)), pltpu.SemaphoreType.DMA((7 * npc,)),
                     pltpu.SemaphoreType.DMA((npc,))]

    def _copies(self, ins, outs, sems):
        send_sems, recv_sems, local_sems = sems
        x, y, c = _mesh_pos()
        me, sibling = (x, y, c), (x, y, 1 - c)
        chips = [(1 - x, y), (x, 1 - y), (1 - x, 1 - y)]

        def src_of(p):
            a, rows = self.pieces[p]
            return ins[a] if rows is None else ins[a].at[rows]

        def dst_of(p, block):
            a, rows = self.pieces[p]
            ref = outs[a].at[_slot(block)]
            return ref if rows is None else ref.at[rows]

        def copy(p, k, block, to, from_input=False):
            dst = dst_of(p, block)
            return pltpu.make_async_remote_copy(
                src_ref=src_of(p) if from_input else dst, dst_ref=dst, send_sem=send_sems.at[7 * p + k],
                recv_sem=recv_sems.at[7 * p + k], device_id=to, device_id_type=MESH)

        npc = len(self.pieces)
        mine = [pltpu.make_async_copy(src_of(p), dst_of(p, me), local_sems.at[p]) for p in range(npc)]
        direct = chips[:2] if self.relay else chips
        first = []
        for p in range(npc):
            first.append(copy(p, 0, me, sibling, from_input=True))
            first += [copy(p, 1 + j, me, (*chip, c), from_input=True) for j, chip in enumerate(direct)]
        return me, sibling, chips, c, copy, mine, first

    def start(self, ins, outs, sems):
        *_, mine, first = self._copies(ins, outs, sems)
        for cp in mine + first:
            cp.start()

    def finish(self, ins, outs, sems):
        me, sibling, chips, c, copy, mine, first = self._copies(ins, outs, sems)
        x, y = me[0], me[1]
        npc = len(self.pieces)
        passed = []

        def pass_on(cp):
            cp.start()
            passed.append(cp)

        for p in range(npc):
            for j in range(2):
                copy(p, 1 + j, (*chips[j], c), me).wait_recv()
                pass_on(copy(p, 4 + j, (*chips[j], c), sibling))
            if self.relay:
                owner = ((x + 1 - c) % 2, (y + c) % 2, c)
                pass_on(copy(p, 3, owner, ((x + c) % 2, (y + 1 - c) % 2, c)))
        for p in range(npc):
            copy(p, 3, (*chips[2], c), me).wait_recv()
            pass_on(copy(p, 6, (*chips[2], c), sibling))
        for p in range(npc):
            copy(p, 0, sibling, me).wait_recv()
            for j, chip in enumerate(chips):
                copy(p, 4 + j, (*chip, 1 - c), me).wait_recv()
        for cp in first + passed:
            cp.wait_send()
        for cp in mine:
            cp.wait()


class _ExchangeCore:
    def __init__(self, fulls, cols=None):
        self.ins = list(fulls)
        self.cols = cols
        width = lambda f: f.shape[3] if cols is None else cols[1]
        self.out_shape = tuple(jax.ShapeDtypeStruct((4, f.shape[2], width(f)), f.dtype) for f in fulls)
        self.sems = [pltpu.SemaphoreType.DMA((4 * len(fulls),)), pltpu.SemaphoreType.DMA((4 * len(fulls),))]

    def _copies(self, ins, outs, sems):
        send_sems, recv_sems = sems
        x, y, c = _mesh_pos()

        def src(a, q):
            ref = ins[a].at[q, 1 - c]
            return ref if self.cols is None else ref.at[:, pl.ds(*self.cols)]

        return [pltpu.make_async_remote_copy(
            src_ref=src(a, q), dst_ref=outs[a].at[q], send_sem=send_sems.at[4 * a + q],
            recv_sem=recv_sems.at[4 * a + q], device_id=(x, y, 1 - c), device_id_type=MESH)
            for a in range(len(self.ins)) for q in range(4)]

    def start(self, ins, outs, sems):
        for cp in self._copies(ins, outs, sems):
            cp.start()

    def finish(self, ins, outs, sems):
        for cp in self._copies(ins, outs, sems):
            cp.wait()


class _ExchangeChip:
    def __init__(self, parts):
        self.ins = list(parts)
        self.out_shape = tuple(jax.ShapeDtypeStruct((3,) + p.shape[1:], p.dtype) for p in parts)
        self.sems = [pltpu.SemaphoreType.DMA((3 * len(parts),)), pltpu.SemaphoreType.DMA((3 * len(parts),))]

    def _copies(self, ins, outs, sems):
        send_sems, recv_sems = sems
        x, y, c = _mesh_pos()
        chips = [(1 - x, y), (x, 1 - y), (1 - x, 1 - y)]
        return [pltpu.make_async_remote_copy(
            src_ref=ins[a].at[2 * px + py], dst_ref=outs[a].at[j], send_sem=send_sems.at[3 * a + j],
            recv_sem=recv_sems.at[3 * a + j], device_id=(px, py, c), device_id_type=MESH)
            for a in range(len(self.ins)) for j, (px, py) in enumerate(chips)]

    def start(self, ins, outs, sems):
        for cp in self._copies(ins, outs, sems):
            cp.start()

    def finish(self, ins, outs, sems):
        for cp in self._copies(ins, outs, sems):
            cp.wait()


HBM_ONLY = pl.BlockSpec(memory_space=pltpu.HBM)
SEM_SPEC = pl.BlockSpec(memory_space=pltpu.SEMAPHORE)
SIDE_EFFECT = pltpu.SideEffectType.DATAFLOW_SIDE_EFFECTING


def _chip_copies(p_refs, land_refs, send_sems, recv_sems):
    x, y, c = _mesh_pos()
    return [pltpu.make_async_remote_copy(
        src_ref=p_refs[a].at[2 * px + py], dst_ref=land_refs[a].at[j], send_sem=send_sems.at[3 * a + j],
        recv_sem=recv_sems.at[3 * a + j], device_id=(px, py, c), device_id_type=MESH)
        for a in range(len(p_refs)) for j, (px, py) in enumerate([(1 - x, y), (x, 1 - y), (1 - x, 1 - y)])]


def _chip_exchange_start(parts, name):
    n = len(parts)
    lands = [lax.empty((3,) + p.shape[1:], p.dtype) for p in parts]

    def body(*refs):
        p_refs, land_refs, (send_sems, recv_sems) = refs[:n], refs[n:2 * n], refs[2 * n:2 * n + 2]
        for cp in _chip_copies(p_refs, land_refs, send_sems, recv_sems):
            cp.start()
        token = refs[-1]
        token[...] = jnp.zeros_like(token)

    hbm = lambda t: pltpu.HBM(t.shape, t.dtype)
    res = pl.pallas_call(
        body, name=name,
        out_shape=(pltpu.SemaphoreType.DMA((3 * n,)), pltpu.SemaphoreType.DMA((3 * n,)), *[hbm(t) for t in parts + lands],
                   jax.ShapeDtypeStruct((8, 128), F32)),
        in_specs=(HBM_ONLY,) * (2 * n),
        out_specs=(SEM_SPEC, SEM_SPEC, *[HBM_ONLY] * (2 * n), pl.BlockSpec(memory_space=pltpu.VMEM)),
        input_output_aliases={i: 2 + i for i in range(2 * n)},
        compiler_params=pltpu.CompilerParams(has_side_effects=SIDE_EFFECT))(
        *[pltpu.with_memory_space_constraint(t, pltpu.HBM) for t in parts + lands])
    return (res[0], res[1], list(res[2:2 + n]), list(res[2 + n:2 + 2 * n])), res[-1]


def _chip_exchange_wait(in_flight, after, name):
    send_sems, recv_sems, parts, lands = in_flight
    n = len(parts)

    def body(*refs):
        p_refs, land_refs, (send_sems, recv_sems) = refs[:n], refs[n:2 * n], refs[2 * n:2 * n + 2]
        for cp in _chip_copies(p_refs, land_refs, send_sems, recv_sems):
            cp.wait_send()
            cp.wait_recv()

    res = pl.pallas_call(
        body, name=name, out_shape=tuple(pltpu.HBM(t.shape, t.dtype) for t in parts + lands),
        in_specs=(*[HBM_ONLY] * (2 * n), SEM_SPEC, SEM_SPEC, pl.BlockSpec(memory_space=pl.ANY)),
        out_specs=(HBM_ONLY,) * (2 * n), input_output_aliases={i: i for i in range(2 * n)},
        compiler_params=pltpu.CompilerParams(has_side_effects=SIDE_EFFECT))(*parts, *lands, send_sems, recv_sems, after)
    return list(res[:n]), list(res[n:])


def _slot(p):
    return 4 * p[0] + 2 * p[1] + p[2]


def _gather_copies(src_refs, out_refs, send_sems, recv_sems):
    x, y, c = _mesh_pos()
    targets = [(x, y, 1 - c), (1 - x, y, c), (x, 1 - y, c), (1 - x, 1 - y, c)]
    return [pltpu.make_async_remote_copy(
        src_ref=src_refs[a], dst_ref=out_refs[a].at[_slot((x, y, c))], send_sem=send_sems.at[4 * a + k],
        recv_sem=recv_sems.at[4 * a + k], device_id=to, device_id_type=MESH)
        for a in range(len(src_refs)) for k, to in enumerate(targets)]


def _gather_start(shards, after, name):
    n = len(shards)
    outs = [lax.empty((N_DEV,) + s.shape, s.dtype) for s in shards]

    def body(*refs):
        for cp in _gather_copies(refs[:n], refs[n:2 * n], refs[2 * n + 1], refs[2 * n + 2]):
            cp.start()
        token = refs[-1]
        token[...] = jnp.zeros_like(token)

    res = pl.pallas_call(
        body, name=name,
        out_shape=(pltpu.SemaphoreType.DMA((4 * n,)), pltpu.SemaphoreType.DMA((4 * n,)),
                   *[pltpu.HBM(t.shape, t.dtype) for t in shards + outs], jax.ShapeDtypeStruct((8, 128), F32)),
        in_specs=(*[HBM_ONLY] * (2 * n), pl.BlockSpec(memory_space=pl.ANY)),
        out_specs=(SEM_SPEC, SEM_SPEC, *[HBM_ONLY] * (2 * n), pl.BlockSpec(memory_space=pltpu.VMEM)),
        input_output_aliases={i: 2 + i for i in range(2 * n)},
        compiler_params=pltpu.CompilerParams(has_side_effects=SIDE_EFFECT))(
        *[pltpu.with_memory_space_constraint(t, pltpu.HBM) for t in shards + outs], after)
    return (res[0], res[1], list(res[2:2 + n]), list(res[2 + n:2 + 2 * n])), res[-1]


def _gather_wait(in_flight, after, name):
    send_sems, recv_sems, shards, outs = in_flight
    n = len(shards)

    def body(*refs):
        for cp in _gather_copies(refs[:n], refs[n:2 * n], refs[2 * n], refs[2 * n + 1]):
            cp.wait_send()
            cp.wait_recv()

    res = pl.pallas_call(
        body, name=name, out_shape=tuple(pltpu.HBM(t.shape, t.dtype) for t in shards + outs),
        in_specs=(*[HBM_ONLY] * (2 * n), SEM_SPEC, SEM_SPEC, pl.BlockSpec(memory_space=pl.ANY)),
        out_specs=(HBM_ONLY,) * (2 * n), input_output_aliases={i: i for i in range(2 * n)},
        compiler_params=pltpu.CompilerParams(has_side_effects=SIDE_EFFECT))(*shards, *outs, send_sems, recv_sems, after)
    return list(res[:n]), list(res[n:])


class _PassToSibling:
    def __init__(self, shards, gathered):
        n = self.n = len(shards)
        self.ins = list(shards) + list(gathered)
        self.out_shape = tuple(jax.ShapeDtypeStruct(g.shape, g.dtype) for g in gathered)
        self.aliases = {n + a: a for a in range(n)}
        self.sems = [pltpu.SemaphoreType.DMA((3 * n,)), pltpu.SemaphoreType.DMA((3 * n,)),
                     pltpu.SemaphoreType.DMA((n,))]

    def _copies(self, ins, outs, sems):
        send_sems, recv_sems, local_sems = sems
        x, y, c = _mesh_pos()
        chips = [(1 - x, y), (x, 1 - y), (1 - x, 1 - y)]
        mine = [pltpu.make_async_copy(ins[a], outs[a].at[_slot((x, y, c))], local_sems.at[a]) for a in range(self.n)]
        passed, awaited = [], []
        for a in range(self.n):
            for j, chip in enumerate(chips):
                sems_j = dict(send_sem=send_sems.at[3 * a + j], recv_sem=recv_sems.at[3 * a + j],
                              device_id=(x, y, 1 - c), device_id_type=MESH)
                blk = outs[a].at[_slot((*chip, c))]
                passed.append(pltpu.make_async_remote_copy(src_ref=blk, dst_ref=blk, **sems_j))
                got = outs[a].at[_slot((*chip, 1 - c))]
                awaited.append(pltpu.make_async_remote_copy(src_ref=got, dst_ref=got, **sems_j))
        return mine, passed, awaited

    def start(self, ins, outs, sems):
        mine, passed, _ = self._copies(ins, outs, sems)
        for cp in mine + passed:
            cp.start()

    def finish(self, ins, outs, sems):
        mine, passed, awaited = self._copies(ins, outs, sems)
        for cp in passed:
            cp.wait_send()
        for cp in awaited:
            cp.wait_recv()
        for cp in mine:
            cp.wait()


def _reduce_sums(fulls, recv_core, core, tag):
    return [_pair_sum(f, r, core, f"rs_pair_{tag}{i}") for i, (f, r) in enumerate(zip(fulls, recv_core))]


def _local_step(x, tgt, mods, w_in_t, shards, small, chip, core):
    sh1, sc1, g1, sh2, sc2, g2 = mods
    norm1_g, rel_bias, gn_g, gn_b, norm2_g, norm_f_g = small
    tables = _ret_tables()
    buckets = jnp.asarray(_bucket_tables())

    h1 = _norm_mod_fwd(x, norm1_g, sh1, sc1, "norm1_fwd")
    flight_w, token_w = _gather_start(list(shards), h1, "gather_w_start")
    proj, slabs = _proj(h1, w_in_t, token_w)
    gated, ro, states = _ret_fwd(proj, tables, gn_g, gn_b)
    bias = _bias_build(rel_bias, buckets)
    outs, lses = [], []
    for gi in range(len(ATT_GROUPS)):
        o, l = _att_fwd(slabs, bias, gi)
        outs.append(o)
        lses.append(l)
    att, gathered = _mix_fwd(outs, lses, comm=_PassToSibling(*_gather_wait(flight_w, lses[2], "gather_w_wait")))
    w_ret_out, w_att_out, w_o, w_ff1, w_ff2 = (_from_slots(g, ax) for g, ax in zip(gathered, BIG_AXES[1:]))
    ret_out = _mm(gated, w_ret_out, 'nn', tm=S, tn=256, tk=2048, name="ret_out")
    att_out, merged = _att_out_merge(att, w_att_out, proj, ret_out)
    mixo, x1 = _mm(merged, w_o, 'nn', tm=S, tn=256, tk=D, name="w_o", res=x, gvec=g1)
    h2 = _norm_mod_fwd(x1, norm2_g, sh2, sc2, "norm2_fwd")
    u, act = _mm(h2, w_ff1, 'nn', tm=S, tn=512, tk=D, name="ff1", relu2=True)
    loss, dx2, g_normf, df, dg2 = _ff2_final(act, w_ff2, x1, g2, tgt, norm_f_g)

    gw_ff2 = _mm(act, df, 'tn', tm=512, tn=D, tk=S, name="gw_ff2", out_dtype=BF16)
    du = _mm(df, w_ff2, 'nt', tm=S, tn=512, tk=D, name="d_act", out_dtype=BF16, relu2_of=u)
    gw_ff1 = _mm(h2, du, 'tn', tm=D, tn=512, tk=S, name="gw_ff1", out_dtype=BF16)
    fulls_a = [_to_slots(g, ax) for g, ax in zip((gw_ff1, gw_ff2), BIG_AXES[4:])]
    dh2, recv_core_a = _mm(du, w_ff1, 'nt', tm=1024, tn=1024, tk=2048, name="dh2", comm=_ExchangeCore(fulls_a))
    parts_a = _reduce_sums(fulls_a, recv_core_a, core, "a")
    flight_a, token_a = _chip_exchange_start(parts_a, "rs_a_start")
    dx1, dsc2, dsh2, g_norm2, dmixo, dg1 = _norm_mod_bwd(x1, norm2_g, sc2, dh2, dx2, "norm2_bwd", gate=(mixo, g1))

    gw_o = _mm(merged, dmixo, 'tn', tm=D, tn=512, tk=S, name="gw_o", out_dtype=BF16, after=token_a)
    d_ret_out, d_att_out, dga, dgb = _dmerged_split(dmixo, w_o, proj, ret_out, att_out)
    gw_ret_out = _mm(gated, d_ret_out, 'tn', tm=512, tn=D, tk=S, name="gw_ret_out", out_dtype=BF16)
    gw_att_out = _mm(att, d_att_out, 'tn', tm=AW, tn=D, tk=S, name="gw_att_out", out_dtype=BF16)
    fulls_b = [_to_slots(g, ax) for g, ax in zip((gw_ret_out, gw_att_out, gw_o), BIG_AXES[1:4])]
    dgated, recv_core_b = _mm(d_ret_out, w_ret_out, 'nt', tm=S, tn=512, tk=D, name="dgated",
                              comm=_ExchangeCore(fulls_b))
    parts_b = _reduce_sums(fulls_b, recv_core_b, core, "b")
    flight_b, token_b = _chip_exchange_start(parts_b, "rs_b_start")
    datt = _mm(d_att_out, w_att_out, 'nt', tm=S, tn=AW, tk=D, name="datt", after=token_b)
    mix_grads = _mix_bwd(outs, lses, datt)
    datt_parts, ds_sums = [], []
    for gi in range(len(ATT_GROUPS)):
        dq, dk, dv, ds_sum = _att_bwd(slabs, bias, outs[gi], lses[gi], mix_grads[gi], mix_grads[3 + gi], gi)
        datt_parts += [dq.reshape(S, AW), dk.reshape(S, AW), dv.reshape(S, AW)]
        ds_sums.append(ds_sum)
    g_bias = _bias_grad(jnp.concatenate(ds_sums, axis=0), buckets)[:, :, 0].T.reshape(1, -1)
    dret, g_gn_g, g_gn_b = _ret_bwd(proj, tables, gn_g, gn_b, ro, states, dgated)
    parts_a, recv_chip_a = _chip_exchange_wait(flight_a, dret, "rs_a_wait")
    parts_b, recv_chip_b = _chip_exchange_wait(flight_b, dret, "rs_b_wait")
    reduced = list(zip(parts_b + parts_a, recv_chip_b + recv_chip_a))
    dproj = jnp.concatenate([dret] + datt_parts + [dga, dgb], axis=1)
    full_in = _to_slots(_mm(dproj, h1, 'tn', tm=512, tn=D, tk=S, name="gw_in", out_dtype=BF16), 0)
    in_flight, token = [], None
    for half in range(2):
        (recv_core_in,) = _run_comm(_ExchangeCore([full_in], cols=(half * (D // 2), D // 2)), f"rs_core_in{half}",
                                    after=token)
        part_in = [_pair_sum(full_in, recv_core_in, core, f"rs_pair_c{half}", col_block=half)]
        flight, token = _chip_exchange_start(part_in, f"rs_in{half}_start")
        in_flight.append(flight)
    gx, dsc1, dsh1, g_norm1 = _dh1_norm(dproj, w_in_t, x, norm1_g, sc1, dx1, token)

    dmod = [dsh1, dsc1, dg1, dsh2, dsc2, dg2]
    small_g = [g_norm1, g_bias, g_gn_g, g_gn_b, g_norm2, g_normf]
    return loss, gx, in_flight, reduced, small_g, dmod


def _to_slots(g, axis):
    if axis == 0:
        return g.reshape(4, 2, g.shape[0] // N_DEV, g.shape[1])
    return g.reshape(g.shape[0], N_DEV, g.shape[1] // N_DEV).transpose(1, 0, 2).reshape(4, 2, g.shape[0], -1)


def _from_slots(w8, axis):
    if axis == 0:
        return w8.reshape(-1, w8.shape[2])
    return w8.transpose(1, 0, 2).reshape(w8.shape[1], -1)


BIG_AXES = (1, 0, 1, 0, 1, 0)


def kernel(x, c, w_ada, b_ada, norm1_g, w_in, rel_bias, ret_gn_g, ret_gn_b, w_ret_out, w_att_out, w_o, norm2_g, w_ff1, w_ff2, norm_f_g, loss_target, m_w_ada, m_b_ada, m_norm1_g, m_w_in, m_rel_bias, m_ret_gn_g, m_ret_gn_b, m_w_ret_out, m_w_att_out, m_w_o, m_norm2_g, m_w_ff1, m_w_ff2, m_norm_f_g, v_w_ada, v_b_ada, v_norm1_g, v_w_in, v_rel_bias, v_ret_gn_g, v_ret_gn_b, v_w_ret_out, v_w_att_out, v_w_o, v_norm2_g, v_w_ff1, v_w_ff2, v_norm_f_g):
    mx, my, mc = _mesh_pos()
    dev = 4 * mx + 2 * my + mc
    chip = jnp.reshape(2 * mx + my, (1,)).astype(jnp.int32)
    core = jnp.reshape(mc, (1,)).astype(jnp.int32)
    ada_w = D * 6 // N_DEV

    w_in, m_w_in, v_w_in = (jnp.transpose(t, (0, 2, 1)) for t in (w_in, m_w_in, v_w_in))

    shards = [w[0].astype(BF16) for w in (w_in, w_ret_out, w_att_out, w_o, w_ff1, w_ff2)]
    c_all, w_in8 = _run_comm(_Gather([c, shards[0]], relay=True, chunks={1: 4}), "gather_c_w_in")
    c_all = c_all.reshape(N_DEV, D)
    b_sl = lax.dynamic_slice(b_ada, (0, dev * ada_w), (1, ada_w))
    (mod_all,) = _run_comm(_Gather([_ada_fwd(c_all, w_ada[0], b_sl)]), "gather_mod")
    mod = lax.dynamic_index_in_dim(mod_all, dev, axis=1, keepdims=False).reshape(6, D)
    mods = tuple(mod[i:i + 1] for i in range(6))

    small = (norm1_g, rel_bias, ret_gn_g, ret_gn_b, norm2_g, norm_f_g.reshape(1, D))
    loss, gx, in_flight, big_red, small_g, dmod = _local_step(x[0], loss_target[0], mods, w_in8.reshape(IN_COLS, D),
                                                              shards[1:], small, chip, core)

    gathered = _run_comm(_Gather(dmod + small_g + [loss]), "gather_small")
    g_b_ada, dmod_all, (g_norm1, g_bias, g_gn_g, g_gn_b, g_norm2, g_normf, loss_sum) = _sum_small(gathered)
    loss_out = loss_sum[0, 0]
    g_w_ada = _ada_bwd(c_all, lax.dynamic_slice(dmod_all, (0, dev * ada_w), (N_DEV, ada_w)))

    names = ['w_ada', 'b_ada', 'norm1_g', 'w_in', 'rel_bias', 'ret_gn_g', 'ret_gn_b', 'w_ret_out', 'w_att_out',
             'w_o', 'norm2_g', 'w_ff1', 'w_ff2', 'norm_f_g']
    ws = dict(zip(names, (w_ada, b_ada, norm1_g, w_in, rel_bias, ret_gn_g, ret_gn_b, w_ret_out, w_att_out, w_o,
                          norm2_g, w_ff1, w_ff2, norm_f_g)))
    ms = dict(zip(names, (m_w_ada, m_b_ada, m_norm1_g, m_w_in, m_rel_bias, m_ret_gn_g, m_ret_gn_b, m_w_ret_out,
                          m_w_att_out, m_w_o, m_norm2_g, m_w_ff1, m_w_ff2, m_norm_f_g)))
    vs = dict(zip(names, (v_w_ada, v_b_ada, v_norm1_g, v_w_in, v_rel_bias, v_ret_gn_g, v_ret_gn_b, v_w_ret_out,
                          v_w_att_out, v_w_o, v_norm2_g, v_w_ff1, v_w_ff2, v_norm_f_g)))
    grads = dict(w_ada=g_w_ada, b_ada=g_b_ada, norm1_g=g_norm1, rel_bias=g_bias,
                 ret_gn_g=g_gn_g, ret_gn_b=g_gn_b, norm2_g=g_norm2, norm_f_g=g_normf)
    delta, new_m, new_v = {}, {}, {}
    delta['w_ada'], new_m['w_ada'], new_v['w_ada'] = _adamw(w_ada, g_w_ada, m_w_ada, v_w_ada, "adamw_w_ada")
    grads['w_ada'] = g_w_ada.reshape(w_ada.shape)
    for n, (part, recv) in zip(('w_ret_out', 'w_att_out', 'w_o', 'w_ff1', 'w_ff2'), big_red):
        grads[n], delta[n], new_m[n], new_v[n] = _adamw_reduced1(ws[n], ms[n], vs[n], part, recv, chip, "adamw_" + n)
    small_names = ('b_ada', 'norm1_g', 'rel_bias', 'ret_gn_g', 'ret_gn_b', 'norm2_g', 'norm_f_g')
    two_d = {n: (1, ws[n].size) if ws[n].ndim == 1 else ws[n].shape for n in small_names}
    d_, m_, v_ = _adamw_small(*[[src[n].reshape(two_d[n]) for n in small_names] for src in (ws, grads, ms, vs)])
    for i, n in enumerate(small_names):
        shp = ws[n].shape
        delta[n], new_m[n], new_v[n] = d_[i].reshape(shp), m_[i].reshape(shp), v_[i].reshape(shp)
        grads[n] = grads[n].reshape(shp)

    done = lax.optimization_barrier((gx, tuple(d_), tuple(delta[n] for n in ('w_ada', 'w_ret_out', 'w_att_out', 'w_o',
                                                                               'w_ff1', 'w_ff2'))))
    parts_in, recvs_in = [], []
    for half, flight in enumerate(in_flight):
        (part_in,), (recv_chip_in,) = _chip_exchange_wait(flight, done[0], f"rs_in{half}_wait")
        parts_in.append(part_in)
        recvs_in.append(recv_chip_in)
    grads['w_in'], delta['w_in'], new_m['w_in'], new_v['w_in'] = _adamw_reduced(w_in, m_w_in, v_w_in, parts_in,
                                                                               recvs_in, chip)
    for d in (grads, delta, new_m, new_v):
        d['w_in'] = jnp.transpose(d['w_in'], (0, 2, 1))
    return (loss_out, gx[None], *[grads[n] for n in names], *[delta[n] for n in names],
            *[new_m[n] for n in names], *[new_v[n] for n in names])
```

```python
import functools
import math

import numpy as np
import jax
import jax.numpy as jnp
from jax import lax
from jax.experimental import pallas as pl
from jax.experimental.pallas import tpu as pltpu

F32 = jnp.float32
BF16 = jnp.bfloat16
MESH = pl.DeviceIdType.MESH

N_DEV = 8
S = 2048
D = 1024
RET_HEADS = 4
RET_DK = 256
RET_DV = 512
CHUNK = 128
N_CHUNK = S // CHUNK
ATT_GROUPS = ((128, 1), (512, 4), (2048, 16))
ATT_HG = 4
ATT_DH = 128
ATT_BLK = 128
N_BUCKETS = 32
MAX_DIST = 2048
D_FF = 4096
IN_COLS = 12800
OFF_RQ, OFF_RK, OFF_RV, OFF_RG, OFF_ATT = 0, 1024, 2048, 4096, 6144
OFF_GA, OFF_GB = 6144, 7168
RMS_EPS = 1e-6
GN_EPS = 1e-5
ADAM_LR, ADAM_B1, ADAM_B2, ADAM_EPS, ADAM_WD, ADAM_STEP = 0.001, 0.9, 0.999, 1e-08, 0.01, 10
VMEM_LIMIT = 48 * 1024 * 1024


def _pcall(body, **kw):
    return pl.pallas_call(body, **kw)


def _params(sem=None):
    return pltpu.CompilerParams(dimension_semantics=sem, vmem_limit_bytes=VMEM_LIMIT)


HBM_SPEC = pl.BlockSpec(memory_space=pl.ANY)


def _carry(body, comm, *, name, grid, in_specs, out_specs, out_shape, scratch_shapes=()):
    single = not isinstance(out_specs, (tuple, list))
    o_specs = (out_specs,) if single else tuple(out_specs)
    o_shape = (out_shape,) if single else tuple(out_shape)
    n_in, n_out, n_scr = len(in_specs), len(o_specs), len(scratch_shapes)
    nci, nco = len(comm.ins), len(comm.out_shape)
    total = int(np.prod(grid))

    def wrapped(*refs):
        bounds = np.cumsum([0, n_in, nci, n_out, nco, n_scr])
        a, ci, o, co, scr = (refs[bounds[i]:bounds[i + 1]] for i in range(5))
        sems = refs[bounds[5]:]
        flat = 0
        for d, g in enumerate(grid):
            flat = flat * g + pl.program_id(d)

        @pl.when(flat == 0)
        def _():
            comm.start(ci, co, sems)

        body(*a, *o, *scr)

        @pl.when(flat == total - 1)
        def _():
            comm.finish(ci, co, sems)

    aliases = {n_in + i: n_out + o for i, o in getattr(comm, "aliases", {}).items()}
    call = _pcall(wrapped, name=name, grid=grid, in_specs=list(in_specs) + [HBM_SPEC] * nci,
                  out_specs=o_specs + (HBM_SPEC,) * nco, out_shape=o_shape + tuple(comm.out_shape),
                  scratch_shapes=list(scratch_shapes) + list(comm.sems), input_output_aliases=aliases,
                  compiler_params=_params(("arbitrary",) * len(grid)))

    def run(*args):
        res = call(*args, *comm.ins)
        own = res[0] if single else tuple(res[:n_out])
        return own, tuple(res[n_out:])

    return run


def _run_comm(comm, name, after=None):
    nci, nco = len(comm.ins), len(comm.out_shape)
    extra = [] if after is None else [after]

    def body(*refs):
        ci, co, sems = refs[:nci], refs[nci + len(extra):nci + len(extra) + nco], refs[nci + len(extra) + nco:]
        comm.start(ci, co, sems)
        comm.finish(ci, co, sems)

    return _pcall(body, name=name, in_specs=[HBM_SPEC] * (nci + len(extra)), out_specs=(HBM_SPEC,) * nco,
                  out_shape=tuple(comm.out_shape), scratch_shapes=list(comm.sems))(*comm.ins, *extra)


def _dot(a, b, dn):
    return lax.dot_general(a.astype(BF16), b.astype(BF16), (dn, ((), ())), preferred_element_type=F32)


NN = ((1,), (0,))
NT = ((1,), (1,))
TN = ((0,), (0,))


def _mm(a, b, mode, *, tm, tn, tk, name, out_dtype=F32, res=None, gvec=None, relu2=False, relu2_of=None, comm=None,
        after=None):
    if mode == 'nn':
        (M, K), (_, N) = a.shape, b.shape
        a_spec = pl.BlockSpec((tm, tk), lambda i, j, k: (i, k))
        b_spec = pl.BlockSpec((tk, tn), lambda i, j, k: (k, j))
        dn = NN
    elif mode == 'nt':
        (M, K), (N, _) = a.shape, b.shape
        a_spec = pl.BlockSpec((tm, tk), lambda i, j, k: (i, k))
        b_spec = pl.BlockSpec((tn, tk), lambda i, j, k: (j, k))
        dn = NT
    else:
        (K, M), (_, N) = a.shape, b.shape
        a_spec = pl.BlockSpec((tk, tm), lambda i, j, k: (k, i))
        b_spec = pl.BlockSpec((tk, tn), lambda i, j, k: (k, j))
        dn = TN
    assert M % tm == 0 and N % tn == 0 and K % tk == 0, (name, M, N, K)
    nk = K // tk
    fused = res is not None
    o_spec = pl.BlockSpec((tm, tn), lambda i, j, k: (i, j))

    def body(a_ref, b_ref, *rest):
        acc_ref = rest[-1] if nk > 1 else None
        if after is not None:
            rest = rest[1:]
        if fused:
            res_ref, g_ref, o_ref, x_ref = rest[:4]
        elif relu2_of is not None:
            u_ref, o_ref = rest[:2]
        elif relu2:
            o_ref, act_ref = rest[:2]
        else:
            o_ref = rest[0]

        def finish(acc):
            if relu2_of is not None:
                acc = acc * (2.0 * jnp.maximum(u_ref[...], 0.0))
            o_ref[...] = acc.astype(o_ref.dtype)
            if fused:
                x_ref[...] = res_ref[...] + g_ref[...] * acc
            if relu2:
                r = jnp.maximum(acc, 0.0)
                act_ref[...] = (r * r).astype(BF16)

        p = _dot(a_ref[...], b_ref[...], dn)
        if nk == 1:
            finish(p)
        else:
            k = pl.program_id(2)

            @pl.when(k == 0)
            def _():
                acc_ref[...] = p

            @pl.when(k > 0)
            def _():
                acc_ref[...] += p

            @pl.when(k == nk - 1)
            def _():
                finish(acc_ref[...])

    in_specs = [a_spec, b_spec]
    args = [a, b]
    if after is not None:
        in_specs.append(pl.BlockSpec(memory_space=pl.ANY))
        args.append(after)
    out_shape = jax.ShapeDtypeStruct((M, N), out_dtype)
    out_specs = o_spec
    if fused:
        in_specs += [pl.BlockSpec((tm, tn), lambda i, j, k: (i, j)), pl.BlockSpec((1, tn), lambda i, j, k: (0, j))]
        args += [res, gvec]
        out_shape = (out_shape, jax.ShapeDtypeStruct((M, N), F32))
        out_specs = (o_spec, pl.BlockSpec((tm, tn), lambda i, j, k: (i, j)))
    elif relu2_of is not None:
        in_specs.append(pl.BlockSpec((tm, tn), lambda i, j, k: (i, j)))
        args.append(relu2_of)
    elif relu2:
        out_shape = (out_shape, jax.ShapeDtypeStruct((M, N), BF16))
        out_specs = (o_spec, pl.BlockSpec((tm, tn), lambda i, j, k: (i, j)))
    kw = dict(name=name, grid=(M // tm, N // tn, nk), in_specs=in_specs, out_specs=out_specs,
              out_shape=out_shape, scratch_shapes=[pltpu.VMEM((tm, tn), F32)] if nk > 1 else [])
    if comm is not None:
        return _carry(body, comm, **kw)(*args)
    return _pcall(body, compiler_params=_params(("parallel", "parallel", "arbitrary")), **kw)(*args)


PROJ_TN = 512
ATT_T0, ATT_T1 = 6144 // PROJ_TN, 10752 // PROJ_TN
N_SLABS = (ATT_T1 - ATT_T0) * 4
MAIN_COLS = IN_COLS - (ATT_T1 - ATT_T0) * PROJ_TN


def _proj(h1, w_in_t, after):
    nj = IN_COLS // PROJ_TN

    def body(a_ref, b_ref, after_ref, main_ref, slab_ref):
        j = pl.program_id(1)
        is_att = (j >= ATT_T0) & (j < ATT_T1)
        chunks = [pl.ds(c * 512, 512) for c in range(S // 512)]

        @pl.when(jnp.logical_not(is_att))
        def _():
            for rows in chunks:
                main_ref[rows, :] = _dot(a_ref[rows, :], b_ref[...], NT)

        @pl.when(is_att)
        def _():
            for rows in chunks:
                p = _dot(a_ref[rows, :], b_ref[...], NT)
                for h in range(4):
                    slab_ref[h, rows, :] = p[:, h * 128:(h + 1) * 128]

    main_idx = lambda j: jnp.where(j < ATT_T0, j, jnp.where(j < ATT_T1, ATT_T0 - 1, j - (ATT_T1 - ATT_T0)))
    slab_idx = lambda j: jnp.clip(j - ATT_T0, 0, ATT_T1 - ATT_T0 - 1)
    return _pcall(
        body, name="proj", grid=(1, nj, 1),
        in_specs=[pl.BlockSpec((S, D), lambda i, j, k: (0, 0)), pl.BlockSpec((PROJ_TN, D), lambda i, j, k: (j, 0)),
                  HBM_SPEC],
        out_specs=(pl.BlockSpec((S, PROJ_TN), lambda i, j, k: (0, main_idx(j))),
                   pl.BlockSpec((4, S, 128), lambda i, j, k: (slab_idx(j), 0, 0))),
        out_shape=(jax.ShapeDtypeStruct((S, MAIN_COLS), F32), jax.ShapeDtypeStruct((N_SLABS, S, 128), F32)),
        compiler_params=_params(("arbitrary",) * 3))(h1, w_in_t, after)


TR = 256


def _row_spec(w=D):
    return pl.BlockSpec((TR, w), lambda i: (i, 0))


def _vec_spec(w=D):
    return pl.BlockSpec((1, w), lambda i: (0, 0))


def _norm_mod_fwd(x, g, sh, sc, name):
    def body(x_ref, g_ref, sh_ref, sc_ref, o_ref):
        xv = x_ref[...]
        rstd = lax.rsqrt(jnp.mean(xv * xv, axis=-1, keepdims=True) + RMS_EPS)
        n = xv * rstd * g_ref[...]
        o_ref[...] = (n * (1.0 + sc_ref[...]) + sh_ref[...]).astype(BF16)

    return _pcall(body, name=name, grid=(S // TR,), in_specs=[_row_spec(), _vec_spec(), _vec_spec(), _vec_spec()],
                  out_specs=_row_spec(), out_shape=jax.ShapeDtypeStruct((S, D), BF16),
                  compiler_params=_params(("parallel",)))(x, g, sh, sc)


def _norm_mod_bwd(x, g, sc, dh, dres, name, gate=None):
    gated = gate is not None

    def body(x_ref, g_ref, sc_ref, dh_ref, dres_ref, *rest):
        if gated:
            f_ref, gv_ref, dx_ref, dsc_ref, dsh_ref, dg_ref, dz_ref, dgv_ref = rest
        else:
            dx_ref, dsc_ref, dsh_ref, dg_ref = rest
        i = pl.program_id(0)
        xv = x_ref[...]
        dh = dh_ref[...]
        rstd = lax.rsqrt(jnp.mean(xv * xv, axis=-1, keepdims=True) + RMS_EPS)
        xhat = xv * rstd
        gv = g_ref[...]
        dn = dh * (1.0 + sc_ref[...])
        dxhat = dn * gv
        dx = dres_ref[...] + rstd * (dxhat - xhat * jnp.mean(dxhat * xhat, axis=-1, keepdims=True))
        dx_ref[...] = dx
        sums = [(dsc_ref, jnp.sum(dh * (xhat * gv), axis=0, keepdims=True)),
                (dsh_ref, jnp.sum(dh, axis=0, keepdims=True)),
                (dg_ref, jnp.sum(dn * xhat, axis=0, keepdims=True))]
        if gated:
            dz_ref[...] = (dx * gv_ref[...]).astype(BF16)
            sums.append((dgv_ref, jnp.sum(dx * f_ref[...], axis=0, keepdims=True)))

        @pl.when(i == 0)
        def _():
            for ref, p in sums:
                ref[...] = p

        @pl.when(i > 0)
        def _():
            for ref, p in sums:
                ref[...] += p

    vec = jax.ShapeDtypeStruct((1, D), F32)
    in_specs = [_row_spec(), _vec_spec(), _vec_spec(), _row_spec(), _row_spec()]
    out_specs = [_row_spec(), _vec_spec(), _vec_spec(), _vec_spec()]
    out_shape = [jax.ShapeDtypeStruct((S, D), F32), vec, vec, vec]
    args = [x, g, sc, dh, dres]
    if gated:
        in_specs += [_row_spec(), _vec_spec()]
        out_specs += [_row_spec(), _vec_spec()]
        out_shape += [jax.ShapeDtypeStruct((S, D), BF16), vec]
        args += list(gate)
    return _pcall(body, name=name, grid=(S // TR,), in_specs=in_specs, out_specs=tuple(out_specs),
                  out_shape=tuple(out_shape), compiler_params=_params(("arbitrary",)))(*args)


def _w_o_norm2(merged, w_o, x, g1, g, sh, sc):
    def body(a_ref, b_ref, x_ref, g1_ref, g_ref, sh_ref, sc_ref, o_ref, x1_ref, h_ref):
        acc = _dot(a_ref[...], b_ref[...], NN)
        o_ref[...] = acc
        xv = x_ref[...] + g1_ref[...] * acc
        x1_ref[...] = xv
        rstd = lax.rsqrt(jnp.mean(xv * xv, axis=-1, keepdims=True) + RMS_EPS)
        h_ref[...] = (xv * rstd * g_ref[...] * (1.0 + sc_ref[...]) + sh_ref[...]).astype(BF16)

    rows = pl.BlockSpec((FF2_TM, D), lambda i: (i, 0))
    f32 = jax.ShapeDtypeStruct((S, D), F32)
    return _pcall(body, name="w_o_norm2", grid=(S // FF2_TM,),
                  in_specs=[rows, pl.BlockSpec((D, D), lambda i: (0, 0)), rows] + [_vec_spec()] * 4,
                  out_specs=(rows, rows, rows), out_shape=(f32, f32, jax.ShapeDtypeStruct((S, D), BF16)),
                  compiler_params=_params(("parallel",)))(merged, w_o, x, g1, g, sh, sc)


FF2_TM = 512


def _ff2_final(act, w_ff2, x1, g2, tgt, g):
    def body(a_ref, b_ref, x1_ref, g2_ref, t_ref, g_ref, loss_ref, dx_ref, dg_ref, df_ref, dg2_ref):
        i = pl.program_id(0)
        f = _dot(a_ref[...], b_ref[...], NN)
        g2v = g2_ref[...]
        xv = x1_ref[...] + g2v * f
        gv = g_ref[...]
        rstd = lax.rsqrt(jnp.mean(xv * xv, axis=-1, keepdims=True) + RMS_EPS)
        xhat = xv * rstd
        err = xhat * gv - t_ref[...]
        dy = err * (1.0 / D)
        dxhat = dy * gv
        dx = rstd * (dxhat - xhat * jnp.mean(dxhat * xhat, axis=-1, keepdims=True))
        dx_ref[...] = dx
        df_ref[...] = (dx * g2v).astype(BF16)
        p_g = jnp.sum(dy * xhat, axis=0, keepdims=True)
        p_g2 = jnp.sum(dx * f, axis=0, keepdims=True)
        p_l = jnp.zeros((1, 128), F32) + 0.5 * jnp.sum(jnp.mean(err * err, axis=-1, keepdims=True))

        @pl.when(i == 0)
        def _():
            dg_ref[...] = p_g
            dg2_ref[...] = p_g2
            loss_ref[...] = p_l

        @pl.when(i > 0)
        def _():
            dg_ref[...] += p_g
            dg2_ref[...] += p_g2
            loss_ref[...] += p_l

    vec = jax.ShapeDtypeStruct((1, D), F32)
    rows = lambda w: pl.BlockSpec((FF2_TM, w), lambda i: (i, 0))
    return _pcall(body, name="ff2_final", grid=(S // FF2_TM,),
                  in_specs=[rows(D_FF), pl.BlockSpec((D_FF, D), lambda i: (0, 0)), rows(D), _vec_spec(), rows(D),
                            _vec_spec()],
                  out_specs=(_vec_spec(128), rows(D), _vec_spec(), rows(D), _vec_spec()),
                  out_shape=(jax.ShapeDtypeStruct((1, 128), F32), jax.ShapeDtypeStruct((S, D), F32), vec,
                             jax.ShapeDtypeStruct((S, D), BF16), vec),
                  compiler_params=_params(("arbitrary",)))(act, w_ff2, x1, g2, tgt, g)


HALF = 512


MERGE_TM = 1024


def _merge_specs():
    blk = lambda off: pl.BlockSpec((MERGE_TM, HALF), lambda i, j: (i, off // HALF + j))
    return blk(OFF_GA), blk(OFF_GB), blk(0)


def _att_out_merge(att, w_att_out, proj, ret_out):
    def body(a_ref, b_ref, ga_ref, gb_ref, r_ref, o_ref, m_ref):
        acc = _dot(a_ref[...], b_ref[...], NN)
        o_ref[...] = acc
        m_ref[...] = (jax.nn.sigmoid(ga_ref[...]) * r_ref[...] + jax.nn.sigmoid(gb_ref[...]) * acc).astype(BF16)

    ga, gb, tile = _merge_specs()
    return _pcall(body, name="att_out", grid=(S // MERGE_TM, D // HALF),
                  in_specs=[pl.BlockSpec((MERGE_TM, AW), lambda i, j: (i, 0)), pl.BlockSpec((AW, HALF), lambda i, j: (0, j)),
                            ga, gb, tile],
                  out_specs=(tile, tile),
                  out_shape=(jax.ShapeDtypeStruct((S, D), F32), jax.ShapeDtypeStruct((S, D), BF16)),
                  compiler_params=_params(("parallel", "parallel")))(att, w_att_out, proj, proj, ret_out)


def _dmerged_split(dmixo, w_o, proj, ret_out, att_out):
    def body(a_ref, b_ref, ga_ref, gb_ref, r_ref, at_ref, dr_ref, da_ref, dga_ref, dgb_ref):
        dm = _dot(a_ref[...], b_ref[...], NT)
        sa = jax.nn.sigmoid(ga_ref[...])
        sb = jax.nn.sigmoid(gb_ref[...])
        dr_ref[...] = (dm * sa).astype(BF16)
        da_ref[...] = (dm * sb).astype(BF16)
        dga_ref[...] = (dm * r_ref[...] * (sa * (1.0 - sa))).astype(BF16)
        dgb_ref[...] = (dm * at_ref[...] * (sb * (1.0 - sb))).astype(BF16)

    ga, gb, tile = _merge_specs()
    o = jax.ShapeDtypeStruct((S, D), BF16)
    return _pcall(body, name="dmerged", grid=(S // MERGE_TM, D // HALF),
                  in_specs=[pl.BlockSpec((MERGE_TM, D), lambda i, j: (i, 0)), pl.BlockSpec((HALF, D), lambda i, j: (j, 0)),
                            ga, gb, tile, tile],
                  out_specs=(tile,) * 4, out_shape=(o, o, o, o),
                  compiler_params=_params(("parallel", "parallel")))(dmixo, w_o, proj, proj, ret_out, att_out)


def _ret_tables():
    H, C = RET_HEADS, CHUNK
    log_g = jnp.log1p(-(2.0 ** (-5.0 - jnp.arange(H, dtype=F32))))
    idx = jnp.arange(C, dtype=F32)
    rel = idx[:, None] - idx[None, :]
    inner = jnp.where(rel >= 0, jnp.exp(log_g[:, None, None] * jnp.maximum(rel, 0.0)), 0.0)
    qd = jnp.exp(log_g[:, None] * (idx + 1.0))[:, :, None]
    kd = jnp.exp(log_g[:, None] * (C - 1.0 - idx))[:, :, None]
    cd = jnp.broadcast_to(jnp.exp(log_g * C)[:, None, None], (H, 1, 128))
    half = RET_DK // 2
    inv = 10000.0 ** (-jnp.arange(half, dtype=F32) / half)
    ang = jnp.arange(S, dtype=F32)[:, None] * inv[None, :]
    return inner, qd, kd, cd, jnp.cos(ang), jnp.sin(ang)


def _rot(x, cos, sin):
    x1, x2 = x[:, :128], x[:, 128:]
    return jnp.concatenate([x1 * cos - x2 * sin, x1 * sin + x2 * cos], axis=1)


def _rot_t(d, cos, sin):
    d1, d2 = d[:, :128], d[:, 128:]
    return jnp.concatenate([d1 * cos + d2 * sin, d2 * cos - d1 * sin], axis=1)


RET_COLS = OFF_ATT
RET_VW = RET_HEADS * RET_DV


def _ret_specs(chunk_of):
    ci = chunk_of
    whole = lambda shape: pl.BlockSpec(shape, lambda t: (0,) * len(shape))
    return [
        pl.BlockSpec((CHUNK, RET_COLS), lambda t: (ci(t), 0)),
        pl.BlockSpec((CHUNK, 128), lambda t: (ci(t), 0)),
        pl.BlockSpec((CHUNK, 128), lambda t: (ci(t), 0)),
        whole((RET_HEADS, CHUNK, CHUNK)), whole((RET_HEADS, CHUNK, 1)), whole((RET_HEADS, CHUNK, 1)),
        whole((RET_HEADS, 1, 128)), whole((1, RET_VW)), whole((1, RET_VW)),
    ]


def _ret_cols(h):
    q = slice(OFF_RQ + h * RET_DK, OFF_RQ + (h + 1) * RET_DK)
    k = slice(OFF_RK + h * RET_DK, OFF_RK + (h + 1) * RET_DK)
    v = slice(OFF_RV + h * RET_DV, OFF_RV + (h + 1) * RET_DV)
    g = slice(OFF_RG + h * RET_DV, OFF_RG + (h + 1) * RET_DV)
    return q, k, v, g, slice(h * RET_DV, (h + 1) * RET_DV)


def _ret_fwd(proj, tables, gn_g, gn_b, comm=None):
    inner, qd, kd, cd, cos, sin = tables

    def body(x_ref, cos_ref, sin_ref, in_ref, qd_ref, kd_ref, cd_ref, g_ref, b_ref,
             gated_ref, ro_ref, st_ref, s_scr):
        i = pl.program_id(0)

        @pl.when(i == 0)
        def _():
            s_scr[...] = jnp.zeros_like(s_scr)

        cosv, sinv = cos_ref[...], sin_ref[...]
        for h in range(RET_HEADS):
            cq, ck, cv, cg, co = _ret_cols(h)
            q = _rot(x_ref[:, cq], cosv, sinv)
            k = _rot(x_ref[:, ck], cosv, sinv) * (RET_DK ** -0.5)
            v = x_ref[:, cv]
            st = s_scr[h]
            st_ref[h] = st.astype(BF16)
            s = _dot(q, k, NT) * in_ref[h]
            o = _dot(s, v, NN) + _dot(q, st, NN) * qd_ref[h]
            s_scr[h] = st * cd_ref[h, :, :1] + _dot(k * kd_ref[h], v, TN)
            ro_ref[:, co] = o
            mu = jnp.mean(o, axis=-1, keepdims=True)
            oc = o - mu
            var = jnp.mean(oc * oc, axis=-1, keepdims=True)
            rn = oc * lax.rsqrt(var + GN_EPS) * g_ref[:, co] + b_ref[:, co]
            rg = x_ref[:, cg]
            gated_ref[:, co] = (rg * jax.nn.sigmoid(rg) * rn).astype(BF16)

    ospec = pl.BlockSpec((CHUNK, RET_VW), lambda t: (t, 0))
    kw = dict(name="ret_fwd", grid=(N_CHUNK,), in_specs=_ret_specs(lambda t: t),
              out_specs=(ospec, ospec, pl.BlockSpec((RET_HEADS, None, RET_DK, RET_DV), lambda t: (0, t, 0, 0))),
              out_shape=(jax.ShapeDtypeStruct((S, RET_VW), BF16), jax.ShapeDtypeStruct((S, RET_VW), F32),
                         jax.ShapeDtypeStruct((RET_HEADS, N_CHUNK, RET_DK, RET_DV), BF16)),
              scratch_shapes=[pltpu.VMEM((RET_HEADS, RET_DK, RET_DV), F32)])
    args = (proj, cos, sin, inner, qd, kd, cd, gn_g, gn_b)
    if comm is not None:
        return _carry(body, comm, **kw)(*args)
    return _pcall(body, compiler_params=_params(("arbitrary",)), **kw)(*args)


def _ret_bwd(proj, tables, gn_g, gn_b, ro, states, dgated, comm=None):
    inner, qd, kd, cd, cos, sin = tables
    last = N_CHUNK - 1

    def body(x_ref, cos_ref, sin_ref, in_ref, qd_ref, kd_ref, cd_ref, g_ref, b_ref, ro_ref, st_ref, dg_ref,
             dx_ref, gg_ref, gb_ref, gs_scr):
        t = pl.program_id(0)

        @pl.when(t == 0)
        def _():
            gs_scr[...] = jnp.zeros_like(gs_scr)
            gg_ref[...] = jnp.zeros_like(gg_ref)
            gb_ref[...] = jnp.zeros_like(gb_ref)

        cosv, sinv = cos_ref[...], sin_ref[...]
        for h in range(RET_HEADS):
            cq, ck, cv, cg, co = _ret_cols(h)
            q = _rot(x_ref[:, cq], cosv, sinv)
            k = _rot(x_ref[:, ck], cosv, sinv) * (RET_DK ** -0.5)
            v = x_ref[:, cv]
            qdv, kdv, dm = qd_ref[h], kd_ref[h], in_ref[h]
            st = st_ref[h]
            o = ro_ref[:, co]
            gv = g_ref[:, co]
            mu = jnp.mean(o, axis=-1, keepdims=True)
            oc = o - mu
            rstd = lax.rsqrt(jnp.mean(oc * oc, axis=-1, keepdims=True) + GN_EPS)
            ohat = oc * rstd
            rn = ohat * gv + b_ref[:, co]
            rg = x_ref[:, cg]
            sg = jax.nn.sigmoid(rg)
            dgt = dg_ref[:, co]
            drn = dgt * (rg * sg)
            dx_ref[:, cg] = (dgt * rn * (sg * (1.0 + rg * (1.0 - sg)))).astype(BF16)
            gg_ref[:, co] += jnp.sum(drn * ohat, axis=0, keepdims=True)
            gb_ref[:, co] += jnp.sum(drn, axis=0, keepdims=True)
            dohat = drn * gv
            do = rstd * (dohat - jnp.mean(dohat, axis=-1, keepdims=True)
                         - ohat * jnp.mean(dohat * ohat, axis=-1, keepdims=True))
            gs = gs_scr[h]
            s = _dot(q, k, NT) * dm
            dsr = _dot(do, v, NT) * dm
            dq = _dot(dsr, k, NN) + _dot(do, st, NT) * qdv
            dk = _dot(dsr, q, TN) + _dot(v, gs, NT) * kdv
            dv = _dot(s, do, TN) + _dot(k * kdv, gs, NN)
            gs_scr[h] = gs * cd_ref[h, :, :1] + _dot(q * qdv, do, TN)
            dx_ref[:, cq] = _rot_t(dq, cosv, sinv).astype(BF16)
            dx_ref[:, ck] = (_rot_t(dk, cosv, sinv) * (RET_DK ** -0.5)).astype(BF16)
            dx_ref[:, cv] = dv.astype(BF16)

    rev = lambda t: last - t
    vblk = pl.BlockSpec((CHUNK, RET_VW), lambda t: (rev(t), 0))
    vspec = pl.BlockSpec((1, RET_VW), lambda t: (0, 0))
    kw = dict(name="ret_bwd", grid=(N_CHUNK,),
              in_specs=_ret_specs(rev) + [vblk, pl.BlockSpec((RET_HEADS, None, RET_DK, RET_DV),
                                                             lambda t: (0, rev(t), 0, 0)), vblk],
              out_specs=(pl.BlockSpec((CHUNK, RET_COLS), lambda t: (rev(t), 0)), vspec, vspec),
              out_shape=(jax.ShapeDtypeStruct((S, RET_COLS), BF16), jax.ShapeDtypeStruct((1, RET_VW), F32),
                         jax.ShapeDtypeStruct((1, RET_VW), F32)),
              scratch_shapes=[pltpu.VMEM((RET_HEADS, RET_DK, RET_DV), F32)])
    args = (proj, cos, sin, inner, qd, kd, cd, gn_g, gn_b, ro, states, dgated)
    if comm is not None:
        return _carry(body, comm, **kw)(*args)
    return _pcall(body, compiler_params=_params(("arbitrary",)), **kw)(*args)


def _bucket_tables():
    qi = np.arange(ATT_BLK)[:, None]
    kj = np.arange(2 * ATT_BLK)[None, :]
    m = ATT_BLK + qi - kj
    out = []
    for win, dil in ATT_GROUPS:
        w = win // dil
        dist = (np.clip(m, 0, w) * dil).astype(np.int32)
        max_exact = N_BUCKETS // 2
        d_f = np.maximum(dist, 1).astype(np.float32)
        large = max_exact + (np.log(d_f / np.float32(max_exact)) / np.float32(math.log(MAX_DIST / max_exact))
                             * np.float32(N_BUCKETS - max_exact)).astype(np.int32)
        large = np.minimum(large, N_BUCKETS - 1)
        out.append(np.where(dist < max_exact, dist, large).astype(np.int32))
    return np.stack(out)


def _bias_build(rel_bias, buckets):
    def body(tab_ref, bk_ref, o_ref):
        hh = pl.program_id(0)
        bk = bk_ref[...]
        acc = jnp.zeros((ATT_BLK, 2 * ATT_BLK), F32)
        for b in range(N_BUCKETS):
            acc = jnp.where(bk == b, tab_ref[b, hh], acc)
        o_ref[...] = acc

    nh = len(ATT_GROUPS) * ATT_HG
    return _pcall(body, name="bias_build", grid=(nh,),
                  in_specs=[pl.BlockSpec(memory_space=pltpu.SMEM),
                            pl.BlockSpec((None, ATT_BLK, 2 * ATT_BLK), lambda hh: (hh // ATT_HG, 0, 0))],
                  out_specs=pl.BlockSpec((None, ATT_BLK, 2 * ATT_BLK), lambda hh: (hh, 0, 0)),
                  out_shape=jax.ShapeDtypeStruct((nh, ATT_BLK, 2 * ATT_BLK), F32),
                  compiler_params=_params(("parallel",)))(rel_bias, buckets)


def _bias_grad(ds_sum, buckets):
    def body(ds_ref, bk_ref, o_ref):
        bk = bk_ref[...]
        ds = ds_ref[...]
        rows = lax.broadcasted_iota(jnp.int32, (N_BUCKETS, 128), 0)
        acc = jnp.zeros((N_BUCKETS, 128), F32)
        for b in range(N_BUCKETS):
            acc = jnp.where(rows == b, jnp.sum(jnp.where(bk == b, ds, 0.0)), acc)
        o_ref[...] = acc

    nh = len(ATT_GROUPS) * ATT_HG
    return _pcall(body, name="bias_grad", grid=(nh,),
                  in_specs=[pl.BlockSpec((None, ATT_BLK, 2 * ATT_BLK), lambda hh: (hh, 0, 0)),
                            pl.BlockSpec((None, ATT_BLK, 2 * ATT_BLK), lambda hh: (hh // ATT_HG, 0, 0))],
                  out_specs=pl.BlockSpec((None, N_BUCKETS, 128), lambda hh: (hh, 0, 0)),
                  out_shape=jax.ShapeDtypeStruct((nh, N_BUCKETS, 128), F32),
                  compiler_params=_params(("parallel",)))(ds_sum, buckets)


def _att_valid(n):
    qi = lax.broadcasted_iota(jnp.int32, (ATT_BLK, 2 * ATT_BLK), 0)
    kj = lax.broadcasted_iota(jnp.int32, (ATT_BLK, 2 * ATT_BLK), 1)
    m = ATT_BLK + qi - kj
    first_key = jnp.where(n > 0, 0, ATT_BLK)
    return (m >= 0) & (m <= ATT_BLK) & (kj >= first_key)


ATT_HP = (1, 2, 2)


def _att_geometry(gi):
    _, dil = ATT_GROUPS[gi]
    return dil, S // dil // ATT_BLK, ATT_HP[gi]


def _blk(dil, r, n):
    if dil == 1:
        return pl.ds(n * ATT_BLK, ATT_BLK)
    return pl.ds(r + n * ATT_BLK * dil, ATT_BLK, stride=dil)


def _slab_specs(gi):
    _, _, hp = _att_geometry(gi)
    per = ATT_HG // hp
    return [pl.BlockSpec((hp, S, ATT_DH), lambda g, r, part=part: ((3 * gi + part) * per + g, 0, 0))
            for part in range(3)]


def _head_specs(gi, count):
    _, _, hp = _att_geometry(gi)
    return [pl.BlockSpec((hp, S, ATT_DH), lambda g, r: (g, 0, 0))] * count


def _bias_spec(gi):
    _, _, hp = _att_geometry(gi)
    return pl.BlockSpec((hp, ATT_BLK, 2 * ATT_BLK), lambda g, r: (gi * (ATT_HG // hp) + g, 0, 0))


def _att_valid_first():
    qi = lax.broadcasted_iota(jnp.int32, (ATT_BLK, ATT_BLK), 0)
    kj = lax.broadcasted_iota(jnp.int32, (ATT_BLK, ATT_BLK), 1)
    return kj <= qi


def _att_fwd(slabs, bias, gi, comm=None):
    dil, nb, hp = _att_geometry(gi)
    scale = ATT_DH ** -0.5

    def body(q_ref, k_ref, v_ref, bias_ref, o_ref, l_ref):
        r = pl.program_id(1)
        for n in range(nb):
            cur = _blk(dil, r, n)
            valid = _att_valid(n) if n > 0 else _att_valid_first()
            for h in range(hp):
                if n > 0:
                    prev = _blk(dil, r, n - 1)
                    kk = jnp.concatenate([k_ref[h, prev, :], k_ref[h, cur, :]], axis=0)
                    vv = jnp.concatenate([v_ref[h, prev, :], v_ref[h, cur, :]], axis=0)
                    bias = bias_ref[h]
                else:
                    kk, vv, bias = k_ref[h, cur, :], v_ref[h, cur, :], bias_ref[h, :, pl.ds(ATT_BLK, ATT_BLK)]
                s = _dot(q_ref[h, cur, :], kk, NT) * scale + bias
                s = jnp.where(valid, s, -1e30)
                mx = jnp.max(s, axis=-1, keepdims=True)
                e = jnp.exp(s - mx)
                den = jnp.sum(e, axis=-1, keepdims=True)
                o_ref[h, cur, :] = _dot(e / den, vv, NN)
                l_ref[h, cur, :] = jnp.broadcast_to(mx + jnp.log(den), (ATT_BLK, ATT_DH))

    osh = jax.ShapeDtypeStruct((ATT_HG, S, ATT_DH), F32)
    kw = dict(name=f"att_fwd{gi}", grid=(ATT_HG // hp, dil), in_specs=_slab_specs(gi) + [_bias_spec(gi)],
              out_specs=tuple(_head_specs(gi, 2)), out_shape=(osh, osh))
    if comm is not None:
        return _carry(body, comm, **kw)(slabs, slabs, slabs, bias)
    return _pcall(body, compiler_params=_params(("parallel", "arbitrary")), **kw)(slabs, slabs, slabs, bias)


def _att_bwd(slabs, bias, o, lse, do, dlse, gi, comm=None):
    dil, nb, hp = _att_geometry(gi)
    per = ATT_HG // hp
    scale = ATT_DH ** -0.5
    wh = hp * ATT_DH
    wide = lambda t: jnp.concatenate([t, t], axis=1)

    def body(q_ref, k_ref, v_ref, bias_ref, o_ref, l_ref, do_ref, dl_ref, dq_ref, dk_ref, dv_ref, ds_ref):
        r = pl.program_id(1)

        @pl.when(r == 0)
        def _():
            ds_ref[...] = jnp.zeros_like(ds_ref)

        for h in range(hp):
            sl = slice(h * ATT_DH, (h + 1) * ATT_DH)
            carry_k = carry_v = None
            for n in range(nb):
                cur = _blk(dil, r, n)
                q = q_ref[h, cur, :]
                dov = do_ref[h, cur, :]
                delta = jnp.sum(dov * o_ref[h, cur, :], axis=-1, keepdims=True)
                out_rows = pl.ds(n * ATT_BLK, ATT_BLK)
                if n == 0:
                    own = pl.ds(ATT_BLK, ATT_BLK)
                    kk, vv = k_ref[h, cur, :], v_ref[h, cur, :]
                    s = _dot(q, kk, NT) * scale + bias_ref[h, :, own]
                    p = jnp.where(_att_valid_first(), jnp.exp(s - l_ref[h, cur, :]), 0.0)
                    ds = p * (_dot(dov, vv, NT) - delta + dl_ref[h, cur, :])
                    ds_ref[h, :, own] += ds
                    dq_ref[out_rows, sl] = (_dot(ds, kk, NN) * scale).astype(BF16)
                    carry_k, carry_v = _dot(ds, q, TN) * scale, _dot(p, dov, TN)
                    continue
                prev = _blk(dil, r, n - 1)
                kk = jnp.concatenate([k_ref[h, prev, :], k_ref[h, cur, :]], axis=0)
                vv = jnp.concatenate([v_ref[h, prev, :], v_ref[h, cur, :]], axis=0)
                s = _dot(q, kk, NT) * scale + bias_ref[h]
                p = jnp.where(_att_valid(n), jnp.exp(s - wide(l_ref[h, cur, :])), 0.0)
                dp = _dot(dov, vv, NT)
                ds = p * (dp - delta + wide(dl_ref[h, cur, :]))
                ds_ref[h] += ds
                dq_ref[out_rows, sl] = (_dot(ds, kk, NN) * scale).astype(BF16)
                dkk = _dot(ds, q, TN) * scale
                dvv = _dot(p, dov, TN)
                before = pl.ds((n - 1) * ATT_BLK, ATT_BLK)
                dk_ref[before, sl] = (carry_k + dkk[:ATT_BLK]).astype(BF16)
                dv_ref[before, sl] = (carry_v + dvv[:ATT_BLK]).astype(BF16)
                carry_k, carry_v = dkk[ATT_BLK:], dvv[ATT_BLK:]
            last = pl.ds((nb - 1) * ATT_BLK, ATT_BLK)
            dk_ref[last, sl] = carry_k.astype(BF16)
            dv_ref[last, sl] = carry_v.astype(BF16)

    out_spec = pl.BlockSpec((S // dil, wh), lambda g, r: (0, r * per + g))
    osh = jax.ShapeDtypeStruct((S // dil, dil * AW), BF16)
    kw = dict(name=f"att_bwd{gi}", grid=(per, dil), in_specs=_slab_specs(gi) + [_bias_spec(gi)] + _head_specs(gi, 4),
              out_specs=(out_spec, out_spec, out_spec,
                         pl.BlockSpec((hp, ATT_BLK, 2 * ATT_BLK), lambda g, r: (g, 0, 0))),
              out_shape=(osh, osh, osh, jax.ShapeDtypeStruct((ATT_HG, ATT_BLK, 2 * ATT_BLK), F32)))
    args = (slabs, slabs, slabs, bias, o, lse, do, dlse)
    if comm is not None:
        return _carry(body, comm, **kw)(*args)
    return _pcall(body, compiler_params=_params(("arbitrary", "arbitrary")), **kw)(*args)


AW = ATT_HG * ATT_DH


def _mix_weights(l0, l1, l2):
    mx = jnp.maximum(jnp.maximum(l0, l1), l2)
    e0, e1, e2 = jnp.exp(l0 - mx), jnp.exp(l1 - mx), jnp.exp(l2 - mx)
    den = e0 + e1 + e2
    return e0 / den, e1 / den, e2 / den


def _heads_spec():
    return pl.BlockSpec((ATT_HG, TR, ATT_DH), lambda i: (0, i, 0))


def _mix_fwd(os_, ls, comm=None):
    def body(o0, o1, o2, l0, l1, l2, att_ref):
        for h in range(ATT_HG):
            w0, w1, w2 = _mix_weights(l0[h], l1[h], l2[h])
            att_ref[:, h * ATT_DH:(h + 1) * ATT_DH] = (w0 * o0[h] + w1 * o1[h] + w2 * o2[h]).astype(BF16)

    kw = dict(name="mix_fwd", grid=(S // TR,), in_specs=[_heads_spec()] * 6, out_specs=_row_spec(AW),
              out_shape=jax.ShapeDtypeStruct((S, AW), BF16))
    if comm is not None:
        return _carry(body, comm, **kw)(*os_, *ls)
    return _pcall(body, compiler_params=_params(("parallel",)), **kw)(*os_, *ls)


def _mix_bwd(os_, ls, datt):
    def body(o0, o1, o2, l0, l1, l2, da_ref, d0, d1, d2, e0, e1, e2):
        for h in range(ATT_HG):
            ws = _mix_weights(l0[h], l1[h], l2[h])
            da = da_ref[:, h * ATT_DH:(h + 1) * ATT_DH]
            dws = []
            for o_ref, w, d_ref in zip((o0, o1, o2), ws, (d0, d1, d2)):
                d_ref[h] = w * da
                dws.append(jnp.broadcast_to(jnp.sum(da * o_ref[h], axis=-1, keepdims=True), (TR, ATT_DH)))
            tot = ws[0] * dws[0] + ws[1] * dws[1] + ws[2] * dws[2]
            for w, dw, e_ref in zip(ws, dws, (e0, e1, e2)):
                e_ref[h] = w * (dw - tot)

    o = jax.ShapeDtypeStruct((ATT_HG, S, ATT_DH), F32)
    return _pcall(body, name="mix_bwd", grid=(S // TR,), in_specs=[_heads_spec()] * 6 + [_row_spec(AW)],
                  out_specs=(_heads_spec(),) * 6, out_shape=(o,) * 6,
                  compiler_params=_params(("parallel",)))(*os_, *ls, datt)


def _ada_fwd(c_all, w_sh, b_sl):
    def body(c_ref, w_ref, b_ref, o_ref):
        cv = c_ref[...]
        o_ref[...] = _dot(cv * jax.nn.sigmoid(cv), w_ref[...], NN) + b_ref[...]

    return _pcall(body, name="ada_fwd", out_shape=jax.ShapeDtypeStruct((N_DEV, w_sh.shape[1]), F32),
                  compiler_params=_params())(c_all, w_sh, b_sl)


def _ada_bwd(c_all, dm_sl):
    def body(c_ref, d_ref, o_ref):
        cv = c_ref[...]
        o_ref[...] = _dot(cv * jax.nn.sigmoid(cv), d_ref[...], TN)

    return _pcall(body, name="ada_bwd", out_shape=jax.ShapeDtypeStruct((D, dm_sl.shape[1]), F32),
                  compiler_params=_params())(c_all, dm_sl)


N_MOD = 6


def _sum_small(gathered):
    n = len(gathered)

    def body(*refs):
        ins, (gb_ref, dm_ref), outs = refs[:n], refs[n:n + 2], refs[n + 2:]

        def total(r):
            acc = r[0]
            for e in range(1, N_DEV):
                acc = acc + r[e]
            return acc

        for i in range(N_MOD):
            cols = slice(i * D, (i + 1) * D)
            gb_ref[:, cols] = total(ins[i])
            for e in range(N_DEV):
                dm_ref[e:e + 1, cols] = ins[i][e]
        for r, o_ref in zip(ins[N_MOD:], outs):
            o_ref[...] = total(r)

    shapes = (jax.ShapeDtypeStruct((1, N_MOD * D), F32), jax.ShapeDtypeStruct((N_DEV, N_MOD * D), F32),
              *[jax.ShapeDtypeStruct(g.shape[1:], F32) for g in gathered[N_MOD:]])
    res = _pcall(body, name="sum_small", out_shape=shapes, compiler_params=_params())(*gathered)
    return res[0], res[1], res[2:]


def _row_tile(m, n):
    t = max(8, min(m, (1 << 19) // n // 8 * 8))
    while m % t:
        t -= 8
    return t


def _pair_sum(full, recv, sel, name, col_block=0):
    _, m, n = recv.shape
    t = _row_tile(m, n)

    def body(sel_ref, a_ref, b_ref, o_ref):
        o_ref[...] = (a_ref[...].astype(F32) + b_ref[...].astype(F32)).astype(o_ref.dtype)

    gs = pltpu.PrefetchScalarGridSpec(
        num_scalar_prefetch=1, grid=(4, m // t),
        in_specs=[pl.BlockSpec((None, None, t, n), lambda q, i, s: (q, s[0], i, col_block)),
                  pl.BlockSpec((None, t, n), lambda q, i, s: (q, i, 0))],
        out_specs=pl.BlockSpec((None, t, n), lambda q, i, s: (q, i, 0)))
    return _pcall(body, name=name, grid_spec=gs, out_shape=jax.ShapeDtypeStruct((4, m, n), full.dtype),
                  compiler_params=_params(("parallel", "parallel")))(sel, full, recv)


def _chip_sum(part, recv, sel, name):
    _, m, n = part.shape
    t = _row_tile(m, n)

    def body(sel_ref, a_ref, r_ref, o_ref):
        o_ref[...] = ((a_ref[...].astype(F32) + r_ref[0].astype(F32)) + r_ref[1].astype(F32)) + r_ref[2].astype(F32)

    gs = pltpu.PrefetchScalarGridSpec(
        num_scalar_prefetch=1, grid=(m // t,),
        in_specs=[pl.BlockSpec((None, t, n), lambda i, s: (s[0], i, 0)),
                  pl.BlockSpec((3, t, n), lambda i, s: (0, i, 0))],
        out_specs=pl.BlockSpec((t, n), lambda i, s: (i, 0)))
    return _pcall(body, name=name, grid_spec=gs, out_shape=jax.ShapeDtypeStruct((m, n), F32),
                  compiler_params=_params(("parallel",)))(sel, part, recv)


def _adamw_math(w, g, m, v):
    nm = ADAM_B1 * m + (1.0 - ADAM_B1) * g
    nv = ADAM_B2 * v + (1.0 - ADAM_B2) * (g * g)
    m_hat = nm / (1.0 - ADAM_B1 ** ADAM_STEP)
    v_hat = nv / (1.0 - ADAM_B2 ** ADAM_STEP)
    return -ADAM_LR * (m_hat / (jnp.sqrt(v_hat) + ADAM_EPS) + ADAM_WD * w), nm, nv


def _adamw(w, g, m, v, name):
    _, rows, cols = w.shape
    t = _row_tile(rows, cols)

    def body(w_ref, g_ref, m_ref, v_ref, d_ref, nm_ref, nv_ref):
        d_ref[...], nm_ref[...], nv_ref[...] = _adamw_math(w_ref[...], g_ref[...], m_ref[...], v_ref[...])

    spec3 = pl.BlockSpec((None, t, cols), lambda i: (0, i, 0))
    spec2 = pl.BlockSpec((t, cols), lambda i: (i, 0))
    o = jax.ShapeDtypeStruct(w.shape, F32)
    return _pcall(body, name=name, grid=(rows // t,), in_specs=[spec3, spec2, spec3, spec3], out_specs=(spec3,) * 3,
                  out_shape=(o, o, o), compiler_params=_params(("parallel",)))(w, g, m, v)


def _adamw_reduced1(w, m, v, part, recv, sel, name):
    _, rows, cols = w.shape
    t = _row_tile(rows, cols)

    def body(sel_ref, w_ref, m_ref, v_ref, p_ref, r_ref, g_ref, d_ref, nm_ref, nv_ref):
        g = ((p_ref[...].astype(F32) + r_ref[0].astype(F32)) + r_ref[1].astype(F32)) + r_ref[2].astype(F32)
        g_ref[...] = g
        d_ref[...], nm_ref[...], nv_ref[...] = _adamw_math(w_ref[...], g, m_ref[...], v_ref[...])

    wspec = pl.BlockSpec((None, t, cols), lambda i, s: (0, i, 0))
    gs = pltpu.PrefetchScalarGridSpec(
        num_scalar_prefetch=1, grid=(rows // t,),
        in_specs=[wspec, wspec, wspec, pl.BlockSpec((None, t, cols), lambda i, s: (s[0], i, 0)),
                  pl.BlockSpec((3, t, cols), lambda i, s: (0, i, 0))],
        out_specs=(wspec,) * 4)
    o = jax.ShapeDtypeStruct(w.shape, F32)
    return _pcall(body, name=name, grid_spec=gs, out_shape=(o, o, o, o),
                  compiler_params=_params(("parallel",)))(sel, w, m, v, part, recv)


def _adamw_reduced(w, m, v, parts, recvs, sel):
    _, rows, cols = w.shape
    half = cols // 2
    t = _row_tile(rows, half)

    def body(sel_ref, w_ref, m_ref, v_ref, pa_ref, pb_ref, ra_ref, rb_ref, g_ref, d_ref, nm_ref, nv_ref):
        total = lambda p_ref, r_ref: ((p_ref[...].astype(F32) + r_ref[0].astype(F32)) + r_ref[1].astype(F32)) \
            + r_ref[2].astype(F32)
        g = jnp.where(pl.program_id(1) == 0, total(pa_ref, ra_ref), total(pb_ref, rb_ref))
        g_ref[...] = g
        d_ref[...], nm_ref[...], nv_ref[...] = _adamw_math(w_ref[...], g, m_ref[...], v_ref[...])

    wspec = pl.BlockSpec((None, t, half), lambda i, j, s: (0, i, j))
    pspec = pl.BlockSpec((None, t, half), lambda i, j, s: (s[0], i, 0))
    rspec = pl.BlockSpec((3, t, half), lambda i, j, s: (0, i, 0))
    gs = pltpu.PrefetchScalarGridSpec(num_scalar_prefetch=1, grid=(rows // t, 2),
                                      in_specs=[wspec, wspec, wspec, pspec, pspec, rspec, rspec],
                                      out_specs=(wspec,) * 4)
    o = jax.ShapeDtypeStruct(w.shape, F32)
    return _pcall(body, name="adamw_w_in", grid_spec=gs, out_shape=(o, o, o, o),
                  compiler_params=_params(("parallel", "arbitrary")))(sel, w, m, v, *parts, *recvs)


def _adamw_small(ws, gs, ms, vs):
    n = len(ws)

    def body(*refs):
        for i in range(n):
            w_ref, g_ref, m_ref, v_ref = (refs[k * n + i] for k in range(4))
            d, nm, nv = _adamw_math(w_ref[...], g_ref[...], m_ref[...], v_ref[...])
            refs[4 * n + i][...] = d
            refs[5 * n + i][...] = nm
            refs[6 * n + i][...] = nv

    shapes = tuple(jax.ShapeDtypeStruct(w.shape, F32) for w in ws)
    res = _pcall(body, name="adamw_small", out_shape=shapes * 3, compiler_params=_params())(*ws, *gs, *ms, *vs)
    return res[:n], res[n:2 * n], res[2 * n:]


def _mesh_pos():
    return lax.axis_index("x"), lax.axis_index("y"), lax.axis_index("c")


class _Gather:
    def __init__(self, arrs, relay=False, chunks=None):
        self.relay = relay
        self.ins = list(arrs)
        self.out_shape = tuple(jax.ShapeDtypeStruct((N_DEV,) + a.shape, a.dtype) for a in arrs)
        self.pieces = []
        for a, arr in enumerate(arrs):
            n = (chunks or {}).get(a, 1)
            rows = arr.shape[0] // n
            self.pieces += [(a, None if n == 1 else pl.ds(i * rows, rows)) for i in range(n)]
        npc = len(self.pieces)
        self.sems = [pltpu.SemaphoreType.DMA((7 * npc,)), pltpu.SemaphoreType.DMA((7 * npc,)),
                     pltpu.SemaphoreType.DMA((npc,))]

    def _copies(self, ins, outs, sems):
        send_sems, recv_sems, local_sems = sems
        x, y, c = _mesh_pos()
        me, sibling = (x, y, c), (x, y, 1 - c)
        chips = [(1 - x, y), (x, 1 - y), (1 - x, 1 - y)]

        def src_of(p):
            a, rows = self.pieces[p]
            return ins[a] if rows is None else ins[a].at[rows]

        def dst_of(p, block):
            a, rows = self.pieces[p]
            ref = outs[a].at[_slot(block)]
            return ref if rows is None else ref.at[rows]

        def copy(p, k, block, to, from_input=False):
            dst = dst_of(p, block)
            return pltpu.make_async_remote_copy(
                src_ref=src_of(p) if from_input else dst, dst_ref=dst, send_sem=send_sems.at[7 * p + k],
                recv_sem=recv_sems.at[7 * p + k], device_id=to, device_id_type=MESH)

        npc = len(self.pieces)
        mine = [pltpu.make_async_copy(src_of(p), dst_of(p, me), local_sems.at[p]) for p in range(npc)]
        direct = chips[:2] if self.relay else chips
        first = []
        for p in range(npc):
            first.append(copy(p, 0, me, sibling, from_input=True))
            first += [copy(p, 1 + j, me, (*chip, c), from_input=True) for j, chip in enumerate(direct)]
        return me, sibling, chips, c, copy, mine, first

    def start(self, ins, outs, sems):
        *_, mine, first = self._copies(ins, outs, sems)
        for cp in mine + first:
            cp.start()

    def finish(self, ins, outs, sems):
        me, sibling, chips, c, copy, mine, first = self._copies(ins, outs, sems)
        x, y = me[0], me[1]
        npc = len(self.pieces)
        passed = []

        def pass_on(cp):
            cp.start()
            passed.append(cp)

        for p in range(npc):
            for j in range(2):
                copy(p, 1 + j, (*chips[j], c), me).wait_recv()
                pass_on(copy(p, 4 + j, (*chips[j], c), sibling))
            if self.relay:
                owner = ((x + 1 - c) % 2, (y + c) % 2, c)
                pass_on(copy(p, 3, owner, ((x + c) % 2, (y + 1 - c) % 2, c)))
        for p in range(npc):
            copy(p, 3, (*chips[2], c), me).wait_recv()
            pass_on(copy(p, 6, (*chips[2], c), sibling))
        for p in range(npc):
            copy(p, 0, sibling, me).wait_recv()
            for j, chip in enumerate(chips):
                copy(p, 4 + j, (*chip, 1 - c), me).wait_recv()
        for cp in first + passed:
            cp.wait_send()
        for cp in mine:
            cp.wait()


class _ExchangeCore:
    def __init__(self, fulls, cols=None):
        self.ins = list(fulls)
        self.cols = cols
        width = lambda f: f.shape[3] if cols is None else cols[1]
        self.out_shape = tuple(jax.ShapeDtypeStruct((4, f.shape[2], width(f)), f.dtype) for f in fulls)
        self.sems = [pltpu.SemaphoreType.DMA((4 * len(fulls),)), pltpu.SemaphoreType.DMA((4 * len(fulls),))]

    def _copies(self, ins, outs, sems):
        send_sems, recv_sems = sems
        x, y, c = _mesh_pos()

        def src(a, q):
            ref = ins[a].at[q, 1 - c]
            return ref if self.cols is None else ref.at[:, pl.ds(*self.cols)]

        return [pltpu.make_async_remote_copy(
            src_ref=src(a, q), dst_ref=outs[a].at[q], send_sem=send_sems.at[4 * a + q],
            recv_sem=recv_sems.at[4 * a + q], device_id=(x, y, 1 - c), device_id_type=MESH)
            for a in range(len(self.ins)) for q in range(4)]

    def start(self, ins, outs, sems):
        for cp in self._copies(ins, outs, sems):
            cp.start()

    def finish(self, ins, outs, sems):
        for cp in self._copies(ins, outs, sems):
            cp.wait()


class _ExchangeChip:
    def __init__(self, parts):
        self.ins = list(parts)
        self.out_shape = tuple(jax.ShapeDtypeStruct((3,) + p.shape[1:], p.dtype) for p in parts)
        self.sems = [pltpu.SemaphoreType.DMA((3 * len(parts),)), pltpu.SemaphoreType.DMA((3 * len(parts),))]

    def _copies(self, ins, outs, sems):
        send_sems, recv_sems = sems
        x, y, c = _mesh_pos()
        chips = [(1 - x, y), (x, 1 - y), (1 - x, 1 - y)]
        return [pltpu.make_async_remote_copy(
            src_ref=ins[a].at[2 * px + py], dst_ref=outs[a].at[j], send_sem=send_sems.at[3 * a + j],
            recv_sem=recv_sems.at[3 * a + j], device_id=(px, py, c), device_id_type=MESH)
            for a in range(len(self.ins)) for j, (px, py) in enumerate(chips)]

    def start(self, ins, outs, sems):
        for cp in self._copies(ins, outs, sems):
            cp.start()

    def finish(self, ins, outs, sems):
        for cp in self._copies(ins, outs, sems):
            cp.wait()


HBM_ONLY = pl.BlockSpec(memory_space=pltpu.HBM)
SEM_SPEC = pl.BlockSpec(memory_space=pltpu.SEMAPHORE)
SIDE_EFFECT = pltpu.SideEffectType.DATAFLOW_SIDE_EFFECTING


def _chip_copies(p_refs, land_refs, send_sems, recv_sems):
    x, y, c = _mesh_pos()
    return [pltpu.make_async_remote_copy(
        src_ref=p_refs[a].at[2 * px + py], dst_ref=land_refs[a].at[j], send_sem=send_sems.at[3 * a + j],
        recv_sem=recv_sems.at[3 * a + j], device_id=(px, py, c), device_id_type=MESH)
        for a in range(len(p_refs)) for j, (px, py) in enumerate([(1 - x, y), (x, 1 - y), (1 - x, 1 - y)])]


def _chip_exchange_start(parts, name):
    n = len(parts)
    lands = [lax.empty((3,) + p.shape[1:], p.dtype) for p in parts]

    def body(*refs):
        p_refs, land_refs, (send_sems, recv_sems) = refs[:n], refs[n:2 * n], refs[2 * n:2 * n + 2]
        for cp in _chip_copies(p_refs, land_refs, send_sems, recv_sems):
            cp.start()
        token = refs[-1]
        token[...] = jnp.zeros_like(token)

    hbm = lambda t: pltpu.HBM(t.shape, t.dtype)
    res = pl.pallas_call(
        body, name=name,
        out_shape=(pltpu.SemaphoreType.DMA((3 * n,)), pltpu.SemaphoreType.DMA((3 * n,)), *[hbm(t) for t in parts + lands],
                   jax.ShapeDtypeStruct((8, 128), F32)),
        in_specs=(HBM_ONLY,) * (2 * n),
        out_specs=(SEM_SPEC, SEM_SPEC, *[HBM_ONLY] * (2 * n), pl.BlockSpec(memory_space=pltpu.VMEM)),
        input_output_aliases={i: 2 + i for i in range(2 * n)},
        compiler_params=pltpu.CompilerParams(has_side_effects=SIDE_EFFECT))(
        *[pltpu.with_memory_space_constraint(t, pltpu.HBM) for t in parts + lands])
    return (res[0], res[1], list(res[2:2 + n]), list(res[2 + n:2 + 2 * n])), res[-1]


def _chip_exchange_wait(in_flight, after, name):
    send_sems, recv_sems, parts, lands = in_flight
    n = len(parts)

    def body(*refs):
        p_refs, land_refs, (send_sems, recv_sems) = refs[:n], refs[n:2 * n], refs[2 * n:2 * n + 2]
        for cp in _chip_copies(p_refs, land_refs, send_sems, recv_sems):
            cp.wait_send()
            cp.wait_recv()

    res = pl.pallas_call(
        body, name=name, out_shape=tuple(pltpu.HBM(t.shape, t.dtype) for t in parts + lands),
        in_specs=(*[HBM_ONLY] * (2 * n), SEM_SPEC, SEM_SPEC, pl.BlockSpec(memory_space=pl.ANY)),
        out_specs=(HBM_ONLY,) * (2 * n), input_output_aliases={i: i for i in range(2 * n)},
        compiler_params=pltpu.CompilerParams(has_side_effects=SIDE_EFFECT))(*parts, *lands, send_sems, recv_sems, after)
    return list(res[:n]), list(res[n:])


def _slot(p):
    return 4 * p[0] + 2 * p[1] + p[2]


def _gather_copies(src_refs, out_refs, send_sems, recv_sems):
    x, y, c = _mesh_pos()
    targets = [(x, y, 1 - c), (1 - x, y, c), (x, 1 - y, c), (1 - x, 1 - y, c)]
    return [pltpu.make_async_remote_copy(
        src_ref=src_refs[a], dst_ref=out_refs[a].at[_slot((x, y, c))], send_sem=send_sems.at[4 * a + k],
        recv_sem=recv_sems.at[4 * a + k], device_id=to, device_id_type=MESH)
        for a in range(len(src_refs)) for k, to in enumerate(targets)]


def _gather_start(shards, after, name):
    n = len(shards)
    outs = [lax.empty((N_DEV,) + s.shape, s.dtype) for s in shards]

    def body(*refs):
        for cp in _gather_copies(refs[:n], refs[n:2 * n], refs[2 * n + 1], refs[2 * n + 2]):
            cp.start()
        token = refs[-1]
        token[...] = jnp.zeros_like(token)

    res = pl.pallas_call(
        body, name=name,
        out_shape=(pltpu.SemaphoreType.DMA((4 * n,)), pltpu.SemaphoreType.DMA((4 * n,)),
                   *[pltpu.HBM(t.shape, t.dtype) for t in shards + outs], jax.ShapeDtypeStruct((8, 128), F32)),
        in_specs=(*[HBM_ONLY] * (2 * n), pl.BlockSpec(memory_space=pl.ANY)),
        out_specs=(SEM_SPEC, SEM_SPEC, *[HBM_ONLY] * (2 * n), pl.BlockSpec(memory_space=pltpu.VMEM)),
        input_output_aliases={i: 2 + i for i in range(2 * n)},
        compiler_params=pltpu.CompilerParams(has_side_effects=SIDE_EFFECT))(
        *[pltpu.with_memory_space_constraint(t, pltpu.HBM) for t in shards + outs], after)
    return (res[0], res[1], list(res[2:2 + n]), list(res[2 + n:2 + 2 * n])), res[-1]


def _gather_wait(in_flight, after, name):
    send_sems, recv_sems, shards, outs = in_flight
    n = len(shards)

    def body(*refs):
        for cp in _gather_copies(refs[:n], refs[n:2 * n], refs[2 * n], refs[2 * n + 1]):
            cp.wait_send()
            cp.wait_recv()

    res = pl.pallas_call(
        body, name=name, out_shape=tuple(pltpu.HBM(t.shape, t.dtype) for t in shards + outs),
        in_specs=(*[HBM_ONLY] * (2 * n), SEM_SPEC, SEM_SPEC, pl.BlockSpec(memory_space=pl.ANY)),
        out_specs=(HBM_ONLY,) * (2 * n), input_output_aliases={i: i for i in range(2 * n)},
        compiler_params=pltpu.CompilerParams(has_side_effects=SIDE_EFFECT))(*shards, *outs, send_sems, recv_sems, after)
    return list(res[:n]), list(res[n:])


class _PassToSibling:
    def __init__(self, shards, gathered):
        n = self.n = len(shards)
        self.ins = list(shards) + list(gathered)
        self.out_shape = tuple(jax.ShapeDtypeStruct(g.shape, g.dtype) for g in gathered)
        self.aliases = {n + a: a for a in range(n)}
        self.sems = [pltpu.SemaphoreType.DMA((3 * n,)), pltpu.SemaphoreType.DMA((3 * n,)),
                     pltpu.SemaphoreType.DMA((n,))]

    def _copies(self, ins, outs, sems):
        send_sems, recv_sems, local_sems = sems
        x, y, c = _mesh_pos()
        chips = [(1 - x, y), (x, 1 - y), (1 - x, 1 - y)]
        mine = [pltpu.make_async_copy(ins[a], outs[a].at[_slot((x, y, c))], local_sems.at[a]) for a in range(self.n)]
        passed, awaited = [], []
        for a in range(self.n):
            for j, chip in enumerate(chips):
                sems_j = dict(send_sem=send_sems.at[3 * a + j], recv_sem=recv_sems.at[3 * a + j],
                              device_id=(x, y, 1 - c), device_id_type=MESH)
                blk = outs[a].at[_slot((*chip, c))]
                passed.append(pltpu.make_async_remote_copy(src_ref=blk, dst_ref=blk, **sems_j))
                got = outs[a].at[_slot((*chip, 1 - c))]
                awaited.append(pltpu.make_async_remote_copy(src_ref=got, dst_ref=got, **sems_j))
        return mine, passed, awaited

    def start(self, ins, outs, sems):
        mine, passed, _ = self._copies(ins, outs, sems)
        for cp in mine + passed:
            cp.start()

    def finish(self, ins, outs, sems):
        mine, passed, awaited = self._copies(ins, outs, sems)
        for cp in passed:
            cp.wait_send()
        for cp in awaited:
            cp.wait_recv()
        for cp in mine:
            cp.wait()


def _reduce_sums(fulls, recv_core, core, tag):
    return [_pair_sum(f, r, core, f"rs_pair_{tag}{i}") for i, (f, r) in enumerate(zip(fulls, recv_core))]


def _local_step(x, tgt, mods, w_in_t, shards, small, chip, core):
    sh1, sc1, g1, sh2, sc2, g2 = mods
    norm1_g, rel_bias, gn_g, gn_b, norm2_g, norm_f_g = small
    tables = _ret_tables()
    buckets = jnp.asarray(_bucket_tables())

    h1 = _norm_mod_fwd(x, norm1_g, sh1, sc1, "norm1_fwd")
    flight_w, token_w = _gather_start(list(shards), h1, "gather_w_start")
    proj, slabs = _proj(h1, w_in_t, token_w)
    gated, ro, states = _ret_fwd(proj, tables, gn_g, gn_b)
    bias = _bias_build(rel_bias, buckets)
    outs, lses = [], []
    for gi in range(len(ATT_GROUPS)):
        o, l = _att_fwd(slabs, bias, gi)
        outs.append(o)
        lses.append(l)
    att, gathered = _mix_fwd(outs, lses, comm=_PassToSibling(*_gather_wait(flight_w, lses[2], "gather_w_wait")))
    w_ret_out, w_att_out, w_o, w_ff1, w_ff2 = (_from_slots(g, ax) for g, ax in zip(gathered, BIG_AXES[1:]))
    ret_out = _mm(gated, w_ret_out, 'nn', tm=S, tn=256, tk=2048, name="ret_out")
    att_out, merged = _att_out_merge(att, w_att_out, proj, ret_out)
    mixo, x1, h2 = _w_o_norm2(merged, w_o, x, g1, norm2_g, sh2, sc2)
    u, act = _mm(h2, w_ff1, 'nn', tm=S, tn=512, tk=D, name="ff1", relu2=True)
    loss, dx2, g_normf, df, dg2 = _ff2_final(act, w_ff2, x1, g2, tgt, norm_f_g)

    gw_ff2 = _mm(act, df, 'tn', tm=512, tn=D, tk=S, name="gw_ff2", out_dtype=BF16)
    du = _mm(df, w_ff2, 'nt', tm=S, tn=512, tk=D, name="d_act", out_dtype=BF16, relu2_of=u)
    gw_ff1 = _mm(h2, du, 'tn', tm=D, tn=512, tk=S, name="gw_ff1", out_dtype=BF16)
    fulls_a = [_to_slots(g, ax) for g, ax in zip((gw_ff1, gw_ff2), BIG_AXES[4:])]
    dh2, recv_core_a = _mm(du, w_ff1, 'nt', tm=1024, tn=1024, tk=2048, name="dh2", comm=_ExchangeCore(fulls_a))
    parts_a = _reduce_sums(fulls_a, recv_core_a, core, "a")
    flight_a, token_a = _chip_exchange_start(parts_a, "rs_a_start")
    dx1, dsc2, dsh2, g_norm2, dmixo, dg1 = _norm_mod_bwd(x1, norm2_g, sc2, dh2, dx2, "norm2_bwd", gate=(mixo, g1))

    gw_o = _mm(merged, dmixo, 'tn', tm=D, tn=512, tk=S, name="gw_o", out_dtype=BF16, after=token_a)
    d_ret_out, d_att_out, dga, dgb = _dmerged_split(dmixo, w_o, proj, ret_out, att_out)
    gw_ret_out = _mm(gated, d_ret_out, 'tn', tm=512, tn=D, tk=S, name="gw_ret_out", out_dtype=BF16)
    gw_att_out = _mm(att, d_att_out, 'tn', tm=AW, tn=D, tk=S, name="gw_att_out", out_dtype=BF16)
    fulls_b = [_to_slots(g, ax) for g, ax in zip((gw_ret_out, gw_att_out, gw_o), BIG_AXES[1:4])]
    dgated, recv_core_b = _mm(d_ret_out, w_ret_out, 'nt', tm=S, tn=512, tk=D, name="dgated",
                              comm=_ExchangeCore(fulls_b))
    parts_b = _reduce_sums(fulls_b, recv_core_b, core, "b")
    flight_b, token_b = _chip_exchange_start(parts_b, "rs_b_start")
    datt = _mm(d_att_out, w_att_out, 'nt', tm=S, tn=AW, tk=D, name="datt", after=token_b)
    mix_grads = _mix_bwd(outs, lses, datt)
    datt_parts, ds_sums = [], []
    for gi in range(len(ATT_GROUPS)):
        dq, dk, dv, ds_sum = _att_bwd(slabs, bias, outs[gi], lses[gi], mix_grads[gi], mix_grads[3 + gi], gi)
        datt_parts += [dq.reshape(S, AW), dk.reshape(S, AW), dv.reshape(S, AW)]
        ds_sums.append(ds_sum)
    g_bias = _bias_grad(jnp.concatenate(ds_sums, axis=0), buckets)[:, :, 0].T.reshape(1, -1)
    dret, g_gn_g, g_gn_b = _ret_bwd(proj, tables, gn_g, gn_b, ro, states, dgated)
    parts_a, recv_chip_a = _chip_exchange_wait(flight_a, dret, "rs_a_wait")
    parts_b, recv_chip_b = _chip_exchange_wait(flight_b, dret, "rs_b_wait")
    reduced = list(zip(parts_b + parts_a, recv_chip_b + recv_chip_a))
    dproj = jnp.concatenate([dret] + datt_parts + [dga, dgb], axis=1)
    full_in = _to_slots(_mm(dproj, h1, 'tn', tm=512, tn=D, tk=S, name="gw_in", out_dtype=BF16), 0)
    in_flight, token = [], None
    for half in range(2):
        (recv_core_in,) = _run_comm(_ExchangeCore([full_in], cols=(half * (D // 2), D // 2)), f"rs_core_in{half}",
                                    after=token)
        part_in = [_pair_sum(full_in, recv_core_in, core, f"rs_pair_c{half}", col_block=half)]
        flight, token = _chip_exchange_start(part_in, f"rs_in{half}_start")
        in_flight.append(flight)
    dh1 = _mm(dproj, w_in_t, 'nn', tm=1024, tn=1024, tk=2560, name="dh1", after=token)
    gx, dsc1, dsh1, g_norm1 = _norm_mod_bwd(x, norm1_g, sc1, dh1, dx1, "norm1_bwd")

    dmod = [dsh1, dsc1, dg1, dsh2, dsc2, dg2]
    small_g = [g_norm1, g_bias, g_gn_g, g_gn_b, g_norm2, g_normf]
    return loss, gx, in_flight, reduced, small_g, dmod


def _to_slots(g, axis):
    if axis == 0:
        return g.reshape(4, 2, g.shape[0] // N_DEV, g.shape[1])
    return g.reshape(g.shape[0], N_DEV, g.shape[1] // N_DEV).transpose(1, 0, 2).reshape(4, 2, g.shape[0], -1)


def _from_slots(w8, axis):
    if axis == 0:
        return w8.reshape(-1, w8.shape[2])
    return w8.transpose(1, 0, 2).reshape(w8.shape[1], -1)


BIG_AXES = (1, 0, 1, 0, 1, 0)


def kernel(x, c, w_ada, b_ada, norm1_g, w_in, rel_bias, ret_gn_g, ret_gn_b, w_ret_out, w_att_out, w_o, norm2_g, w_ff1, w_ff2, norm_f_g, loss_target, m_w_ada, m_b_ada, m_norm1_g, m_w_in, m_rel_bias, m_ret_gn_g, m_ret_gn_b, m_w_ret_out, m_w_att_out, m_w_o, m_norm2_g, m_w_ff1, m_w_ff2, m_norm_f_g, v_w_ada, v_b_ada, v_norm1_g, v_w_in, v_rel_bias, v_ret_gn_g, v_ret_gn_b, v_w_ret_out, v_w_att_out, v_w_o, v_norm2_g, v_w_ff1, v_w_ff2, v_norm_f_g):
    mx, my, mc = _mesh_pos()
    dev = 4 * mx + 2 * my + mc
    chip = jnp.reshape(2 * mx + my, (1,)).astype(jnp.int32)
    core = jnp.reshape(mc, (1,)).astype(jnp.int32)
    ada_w = D * 6 // N_DEV

    w_in, m_w_in, v_w_in = (jnp.transpose(t, (0, 2, 1)) for t in (w_in, m_w_in, v_w_in))

    shards = [w[0].astype(BF16) for w in (w_in, w_ret_out, w_att_out, w_o, w_ff1, w_ff2)]
    c_all, w_in8 = _run_comm(_Gather([c, shards[0]], relay=True, chunks={1: 4}), "gather_c_w_in")
    c_all = c_all.reshape(N_DEV, D)
    b_sl = lax.dynamic_slice(b_ada, (0, dev * ada_w), (1, ada_w))
    (mod_all,) = _run_comm(_Gather([_ada_fwd(c_all, w_ada[0], b_sl)]), "gather_mod")
    mod = lax.dynamic_index_in_dim(mod_all, dev, axis=1, keepdims=False).reshape(6, D)
    mods = tuple(mod[i:i + 1] for i in range(6))

    small = (norm1_g, rel_bias, ret_gn_g, ret_gn_b, norm2_g, norm_f_g.reshape(1, D))
    loss, gx, in_flight, big_red, small_g, dmod = _local_step(x[0], loss_target[0], mods, w_in8.reshape(IN_COLS, D),
                                                              shards[1:], small, chip, core)

    gathered = _run_comm(_Gather(dmod + small_g + [loss]), "gather_small")
    g_b_ada, dmod_all, (g_norm1, g_bias, g_gn_g, g_gn_b, g_norm2, g_normf, loss_sum) = _sum_small(gathered)
    loss_out = loss_sum[0, 0]
    g_w_ada = _ada_bwd(c_all, lax.dynamic_slice(dmod_all, (0, dev * ada_w), (N_DEV, ada_w)))

    names = ['w_ada', 'b_ada', 'norm1_g', 'w_in', 'rel_bias', 'ret_gn_g', 'ret_gn_b', 'w_ret_out', 'w_att_out',
             'w_o', 'norm2_g', 'w_ff1', 'w_ff2', 'norm_f_g']
    ws = dict(zip(names, (w_ada, b_ada, norm1_g, w_in, rel_bias, ret_gn_g, ret_gn_b, w_ret_out, w_att_out, w_o,
                          norm2_g, w_ff1, w_ff2, norm_f_g)))
    ms = dict(zip(names, (m_w_ada, m_b_ada, m_norm1_g, m_w_in, m_rel_bias, m_ret_gn_g, m_ret_gn_b, m_w_ret_out,
                          m_w_att_out, m_w_o, m_norm2_g, m_w_ff1, m_w_ff2, m_norm_f_g)))
    vs = dict(zip(names, (v_w_ada, v_b_ada, v_norm1_g, v_w_in, v_rel_bias, v_ret_gn_g, v_ret_gn_b, v_w_ret_out,
                          v_w_att_out, v_w_o, v_norm2_g, v_w_ff1, v_w_ff2, v_norm_f_g)))
    grads = dict(w_ada=g_w_ada, b_ada=g_b_ada, norm1_g=g_norm1, rel_bias=g_bias,
                 ret_gn_g=g_gn_g, ret_gn_b=g_gn_b, norm2_g=g_norm2, norm_f_g=g_normf)
    delta, new_m, new_v = {}, {}, {}
    delta['w_ada'], new_m['w_ada'], new_v['w_ada'] = _adamw(w_ada, g_w_ada, m_w_ada, v_w_ada, "adamw_w_ada")
    grads['w_ada'] = g_w_ada.reshape(w_ada.shape)
    for n, (part, recv) in zip(('w_ret_out', 'w_att_out', 'w_o', 'w_ff1', 'w_ff2'), big_red):
        grads[n], delta[n], new_m[n], new_v[n] = _adamw_reduced1(ws[n], ms[n], vs[n], part, recv, chip, "adamw_" + n)
    small_names = ('b_ada', 'norm1_g', 'rel_bias', 'ret_gn_g', 'ret_gn_b', 'norm2_g', 'norm_f_g')
    two_d = {n: (1, ws[n].size) if ws[n].ndim == 1 else ws[n].shape for n in small_names}
    d_, m_, v_ = _adamw_small(*[[src[n].reshape(two_d[n]) for n in small_names] for src in (ws, grads, ms, vs)])
    for i, n in enumerate(small_names):
        shp = ws[n].shape
        delta[n], new_m[n], new_v[n] = d_[i].reshape(shp), m_[i].reshape(shp), v_[i].reshape(shp)
        grads[n] = grads[n].reshape(shp)

    done = lax.optimization_barrier((gx, tuple(d_), tuple(delta[n] for n in ('w_ada', 'w_ret_out', 'w_att_out', 'w_o',
                                                                               'w_ff1', 'w_ff2'))))
    parts_in, recvs_in = [], []
    for half, flight in enumerate(in_flight):
        (part_in,), (recv_chip_in,) = _chip_exchange_wait(flight, done[0], f"rs_in{half}_wait")
        parts_in.append(part_in)
        recvs_in.append(recv_chip_in)
    grads['w_in'], delta['w_in'], new_m['w_in'], new_v['w_in'] = _adamw_reduced(w_in, m_w_in, v_w_in, parts_in,
                                                                               recvs_in, chip)
    for d in (grads, delta, new_m, new_v):
        d['w_in'] = jnp.transpose(d['w_in'], (0, 2, 1))
    return (loss_out, gx[None], *[grads[n] for n in names], *[delta[n] for n in names],
            *[new_m[n] for n in names], *[new_v[n] for n in names])
```

```python
import functools
import math

import numpy as np
import jax
import jax.numpy as jnp
from jax import lax
from jax.experimental import pallas as pl
from jax.experimental.pallas import tpu as pltpu

F32 = jnp.float32
BF16 = jnp.bfloat16
MESH = pl.DeviceIdType.MESH

N_DEV = 8
S = 2048
D = 1024
RET_HEADS = 4
RET_DK = 256
RET_DV = 512
CHUNK = 128
N_CHUNK = S // CHUNK
ATT_GROUPS = ((128, 1), (512, 4), (2048, 16))
ATT_HG = 4
ATT_DH = 128
ATT_BLK = 128
N_BUCKETS = 32
MAX_DIST = 2048
D_FF = 4096
IN_COLS = 12800
OFF_RQ, OFF_RK, OFF_RV, OFF_RG, OFF_ATT = 0, 1024, 2048, 4096, 6144
OFF_GA, OFF_GB = 6144, 7168
RMS_EPS = 1e-6
GN_EPS = 1e-5
ADAM_LR, ADAM_B1, ADAM_B2, ADAM_EPS, ADAM_WD, ADAM_STEP = 0.001, 0.9, 0.999, 1e-08, 0.01, 10
VMEM_LIMIT = 48 * 1024 * 1024


def _pcall(body, **kw):
    return pl.pallas_call(body, **kw)


def _params(sem=None):
    return pltpu.CompilerParams(dimension_semantics=sem, vmem_limit_bytes=VMEM_LIMIT)


HBM_SPEC = pl.BlockSpec(memory_space=pl.ANY)


def _carry(body, comm, *, name, grid, in_specs, out_specs, out_shape, scratch_shapes=()):
    single = not isinstance(out_specs, (tuple, list))
    o_specs = (out_specs,) if single else tuple(out_specs)
    o_shape = (out_shape,) if single else tuple(out_shape)
    n_in, n_out, n_scr = len(in_specs), len(o_specs), len(scratch_shapes)
    nci, nco = len(comm.ins), len(comm.out_shape)
    total = int(np.prod(grid))

    def wrapped(*refs):
        bounds = np.cumsum([0, n_in, nci, n_out, nco, n_scr])
        a, ci, o, co, scr = (refs[bounds[i]:bounds[i + 1]] for i in range(5))
        sems = refs[bounds[5]:]
        flat = 0
        for d, g in enumerate(grid):
            flat = flat * g + pl.program_id(d)

        @pl.when(flat == 0)
        def _():
            comm.start(ci, co, sems)

        body(*a, *o, *scr)

        @pl.when(flat == total - 1)
        def _():
            comm.finish(ci, co, sems)

    aliases = {n_in + i: n_out + o for i, o in getattr(comm, "aliases", {}).items()}
    call = _pcall(wrapped, name=name, grid=grid, in_specs=list(in_specs) + [HBM_SPEC] * nci,
                  out_specs=o_specs + (HBM_SPEC,) * nco, out_shape=o_shape + tuple(comm.out_shape),
                  scratch_shapes=list(scratch_shapes) + list(comm.sems), input_output_aliases=aliases,
                  compiler_params=_params(("arbitrary",) * len(grid)))

    def run(*args):
        res = call(*args, *comm.ins)
        own = res[0] if single else tuple(res[:n_out])
        return own, tuple(res[n_out:])

    return run


def _run_comm(comm, name, after=None):
    nci, nco = len(comm.ins), len(comm.out_shape)
    extra = [] if after is None else [after]

    def body(*refs):
        ci, co, sems = refs[:nci], refs[nci + len(extra):nci + len(extra) + nco], refs[nci + len(extra) + nco:]
        comm.start(ci, co, sems)
        comm.finish(ci, co, sems)

    return _pcall(body, name=name, in_specs=[HBM_SPEC] * (nci + len(extra)), out_specs=(HBM_SPEC,) * nco,
                  out_shape=tuple(comm.out_shape), scratch_shapes=list(comm.sems))(*comm.ins, *extra)


def _dot(a, b, dn):
    return lax.dot_general(a.astype(BF16), b.astype(BF16), (dn, ((), ())), preferred_element_type=F32)


NN = ((1,), (0,))
NT = ((1,), (1,))
TN = ((0,), (0,))


def _mm(a, b, mode, *, tm, tn, tk, name, out_dtype=F32, res=None, gvec=None, relu2=False, relu2_of=None, comm=None,
        after=None):
    if mode == 'nn':
        (M, K), (_, N) = a.shape, b.shape
        a_spec = pl.BlockSpec((tm, tk), lambda i, j, k: (i, k))
        b_spec = pl.BlockSpec((tk, tn), lambda i, j, k: (k, j))
        dn = NN
    elif mode == 'nt':
        (M, K), (N, _) = a.shape, b.shape
        a_spec = pl.BlockSpec((tm, tk), lambda i, j, k: (i, k))
        b_spec = pl.BlockSpec((tn, tk), lambda i, j, k: (j, k))
        dn = NT
    else:
        (K, M), (_, N) = a.shape, b.shape
        a_spec = pl.BlockSpec((tk, tm), lambda i, j, k: (k, i))
        b_spec = pl.BlockSpec((tk, tn), lambda i, j, k: (k, j))
        dn = TN
    assert M % tm == 0 and N % tn == 0 and K % tk == 0, (name, M, N, K)
    nk = K // tk
    fused = res is not None
    o_spec = pl.BlockSpec((tm, tn), lambda i, j, k: (i, j))

    def body(a_ref, b_ref, *rest):
        acc_ref = rest[-1] if nk > 1 else None
        if after is not None:
            rest = rest[1:]
        if fused:
            res_ref, g_ref, o_ref, x_ref = rest[:4]
        elif relu2_of is not None:
            u_ref, o_ref = rest[:2]
        elif relu2:
            o_ref, act_ref = rest[:2]
        else:
            o_ref = rest[0]

        def finish(acc):
            if relu2_of is not None:
                acc = acc * (2.0 * jnp.maximum(u_ref[...], 0.0))
            o_ref[...] = acc.astype(o_ref.dtype)
            if fused:
                x_ref[...] = res_ref[...] + g_ref[...] * acc
            if relu2:
                r = jnp.maximum(acc, 0.0)
                act_ref[...] = (r * r).astype(BF16)

        p = _dot(a_ref[...], b_ref[...], dn)
        if nk == 1:
            finish(p)
        else:
            k = pl.program_id(2)

            @pl.when(k == 0)
            def _():
                acc_ref[...] = p

            @pl.when(k > 0)
            def _():
                acc_ref[...] += p

            @pl.when(k == nk - 1)
            def _():
                finish(acc_ref[...])

    in_specs = [a_spec, b_spec]
    args = [a, b]
    if after is not None:
        in_specs.append(pl.BlockSpec(memory_space=pl.ANY))
        args.append(after)
    out_shape = jax.ShapeDtypeStruct((M, N), out_dtype)
    out_specs = o_spec
    if fused:
        in_specs += [pl.BlockSpec((tm, tn), lambda i, j, k: (i, j)), pl.BlockSpec((1, tn), lambda i, j, k: (0, j))]
        args += [res, gvec]
        out_shape = (out_shape, jax.ShapeDtypeStruct((M, N), F32))
        out_specs = (o_spec, pl.BlockSpec((tm, tn), lambda i, j, k: (i, j)))
    elif relu2_of is not None:
        in_specs.append(pl.BlockSpec((tm, tn), lambda i, j, k: (i, j)))
        args.append(relu2_of)
    elif relu2:
        out_shape = (out_shape, jax.ShapeDtypeStruct((M, N), BF16))
        out_specs = (o_spec, pl.BlockSpec((tm, tn), lambda i, j, k: (i, j)))
    kw = dict(name=name, grid=(M // tm, N // tn, nk), in_specs=in_specs, out_specs=out_specs,
              out_shape=out_shape, scratch_shapes=[pltpu.VMEM((tm, tn), F32)] if nk > 1 else [])
    if comm is not None:
        return _carry(body, comm, **kw)(*args)
    return _pcall(body, compiler_params=_params(("parallel", "parallel", "arbitrary")), **kw)(*args)


PROJ_TN = 512
ATT_T0, ATT_T1 = 6144 // PROJ_TN, 10752 // PROJ_TN
N_SLABS = (ATT_T1 - ATT_T0) * 4
MAIN_COLS = IN_COLS - (ATT_T1 - ATT_T0) * PROJ_TN


def _proj(h1, w_in_t, after):
    nj = IN_COLS // PROJ_TN

    def body(a_ref, b_ref, after_ref, main_ref, slab_ref):
        j = pl.program_id(1)
        is_att = (j >= ATT_T0) & (j < ATT_T1)
        chunks = [pl.ds(c * 512, 512) for c in range(S // 512)]

        @pl.when(jnp.logical_not(is_att))
        def _():
            for rows in chunks:
                main_ref[rows, :] = _dot(a_ref[rows, :], b_ref[...], NT)

        @pl.when(is_att)
        def _():
            for rows in chunks:
                p = _dot(a_ref[rows, :], b_ref[...], NT)
                for h in range(4):
                    slab_ref[h, rows, :] = p[:, h * 128:(h + 1) * 128]

    main_idx = lambda j: jnp.where(j < ATT_T0, j, jnp.where(j < ATT_T1, ATT_T0 - 1, j - (ATT_T1 - ATT_T0)))
    slab_idx = lambda j: jnp.clip(j - ATT_T0, 0, ATT_T1 - ATT_T0 - 1)
    return _pcall(
        body, name="proj", grid=(1, nj, 1),
        in_specs=[pl.BlockSpec((S, D), lambda i, j, k: (0, 0)), pl.BlockSpec((PROJ_TN, D), lambda i, j, k: (j, 0)),
                  HBM_SPEC],
        out_specs=(pl.BlockSpec((S, PROJ_TN), lambda i, j, k: (0, main_idx(j))),
                   pl.BlockSpec((4, S, 128), lambda i, j, k: (slab_idx(j), 0, 0))),
        out_shape=(jax.ShapeDtypeStruct((S, MAIN_COLS), F32), jax.ShapeDtypeStruct((N_SLABS, S, 128), F32)),
        compiler_params=_params(("arbitrary",) * 3))(h1, w_in_t, after)


TR = 256


def _row_spec(w=D):
    return pl.BlockSpec((TR, w), lambda i: (i, 0))


def _vec_spec(w=D):
    return pl.BlockSpec((1, w), lambda i: (0, 0))


def _norm_mod_fwd(x, g, sh, sc, name):
    def body(x_ref, g_ref, sh_ref, sc_ref, o_ref):
        xv = x_ref[...]
        rstd = lax.rsqrt(jnp.mean(xv * xv, axis=-1, keepdims=True) + RMS_EPS)
        n = xv * rstd * g_ref[...]
        o_ref[...] = (n * (1.0 + sc_ref[...]) + sh_ref[...]).astype(BF16)

    return _pcall(body, name=name, grid=(S // TR,), in_specs=[_row_spec(), _vec_spec(), _vec_spec(), _vec_spec()],
                  out_specs=_row_spec(), out_shape=jax.ShapeDtypeStruct((S, D), BF16),
                  compiler_params=_params(("parallel",)))(x, g, sh, sc)


def _norm_mod_bwd(x, g, sc, dh, dres, name, gate=None, comm=None):
    gated = gate is not None

    def body(x_ref, g_ref, sc_ref, dh_ref, dres_ref, *rest):
        if gated:
            f_ref, gv_ref, dx_ref, dsc_ref, dsh_ref, dg_ref, dz_ref, dgv_ref = rest
        else:
            dx_ref, dsc_ref, dsh_ref, dg_ref = rest
        i = pl.program_id(0)
        xv = x_ref[...]
        dh = dh_ref[...]
        rstd = lax.rsqrt(jnp.mean(xv * xv, axis=-1, keepdims=True) + RMS_EPS)
        xhat = xv * rstd
        gv = g_ref[...]
        dn = dh * (1.0 + sc_ref[...])
        dxhat = dn * gv
        dx = dres_ref[...] + rstd * (dxhat - xhat * jnp.mean(dxhat * xhat, axis=-1, keepdims=True))
        dx_ref[...] = dx
        sums = [(dsc_ref, jnp.sum(dh * (xhat * gv), axis=0, keepdims=True)),
                (dsh_ref, jnp.sum(dh, axis=0, keepdims=True)),
                (dg_ref, jnp.sum(dn * xhat, axis=0, keepdims=True))]
        if gated:
            dz_ref[...] = (dx * gv_ref[...]).astype(BF16)
            sums.append((dgv_ref, jnp.sum(dx * f_ref[...], axis=0, keepdims=True)))

        @pl.when(i == 0)
        def _():
            for ref, p in sums:
                ref[...] = p

        @pl.when(i > 0)
        def _():
            for ref, p in sums:
                ref[...] += p

    vec = jax.ShapeDtypeStruct((1, D), F32)
    in_specs = [_row_spec(), _vec_spec(), _vec_spec(), _row_spec(), _row_spec()]
    out_specs = [_row_spec(), _vec_spec(), _vec_spec(), _vec_spec()]
    out_shape = [jax.ShapeDtypeStruct((S, D), F32), vec, vec, vec]
    args = [x, g, sc, dh, dres]
    if gated:
        in_specs += [_row_spec(), _vec_spec()]
        out_specs += [_row_spec(), _vec_spec()]
        out_shape += [jax.ShapeDtypeStruct((S, D), BF16), vec]
        args += list(gate)
    kw = dict(name=name, grid=(S // TR,), in_specs=in_specs, out_specs=tuple(out_specs), out_shape=tuple(out_shape))
    if comm is not None:
        return _carry(body, comm, **kw)(*args)
    return _pcall(body, compiler_params=_params(("arbitrary",)), **kw)(*args)


def _w_o_norm2(merged, w_o, x, g1, g, sh, sc):
    def body(a_ref, b_ref, x_ref, g1_ref, g_ref, sh_ref, sc_ref, o_ref, x1_ref, h_ref):
        acc = _dot(a_ref[...], b_ref[...], NN)
        o_ref[...] = acc
        xv = x_ref[...] + g1_ref[...] * acc
        x1_ref[...] = xv
        rstd = lax.rsqrt(jnp.mean(xv * xv, axis=-1, keepdims=True) + RMS_EPS)
        h_ref[...] = (xv * rstd * g_ref[...] * (1.0 + sc_ref[...]) + sh_ref[...]).astype(BF16)

    rows = pl.BlockSpec((FF2_TM, D), lambda i: (i, 0))
    f32 = jax.ShapeDtypeStruct((S, D), F32)
    return _pcall(body, name="w_o_norm2", grid=(S // FF2_TM,),
                  in_specs=[rows, pl.BlockSpec((D, D), lambda i: (0, 0)), rows] + [_vec_spec()] * 4,
                  out_specs=(rows, rows, rows), out_shape=(f32, f32, jax.ShapeDtypeStruct((S, D), BF16)),
                  compiler_params=_params(("parallel",)))(merged, w_o, x, g1, g, sh, sc)


FF2_TM = 512


def _ff2_final(act, w_ff2, x1, g2, tgt, g):
    def body(a_ref, b_ref, x1_ref, g2_ref, t_ref, g_ref, loss_ref, dx_ref, dg_ref, df_ref, dg2_ref):
        i = pl.program_id(0)
        f = _dot(a_ref[...], b_ref[...], NN)
        g2v = g2_ref[...]
        xv = x1_ref[...] + g2v * f
        gv = g_ref[...]
        rstd = lax.rsqrt(jnp.mean(xv * xv, axis=-1, keepdims=True) + RMS_EPS)
        xhat = xv * rstd
        err = xhat * gv - t_ref[...]
        dy = err * (1.0 / D)
        dxhat = dy * gv
        dx = rstd * (dxhat - xhat * jnp.mean(dxhat * xhat, axis=-1, keepdims=True))
        dx_ref[...] = dx
        df_ref[...] = (dx * g2v).astype(BF16)
        p_g = jnp.sum(dy * xhat, axis=0, keepdims=True)
        p_g2 = jnp.sum(dx * f, axis=0, keepdims=True)
        p_l = jnp.zeros((1, 128), F32) + 0.5 * jnp.sum(jnp.mean(err * err, axis=-1, keepdims=True))

        @pl.when(i == 0)
        def _():
            dg_ref[...] = p_g
            dg2_ref[...] = p_g2
            loss_ref[...] = p_l

        @pl.when(i > 0)
        def _():
            dg_ref[...] += p_g
            dg2_ref[...] += p_g2
            loss_ref[...] += p_l

    vec = jax.ShapeDtypeStruct((1, D), F32)
    rows = lambda w: pl.BlockSpec((FF2_TM, w), lambda i: (i, 0))
    return _pcall(body, name="ff2_final", grid=(S // FF2_TM,),
                  in_specs=[rows(D_FF), pl.BlockSpec((D_FF, D), lambda i: (0, 0)), rows(D), _vec_spec(), rows(D),
                            _vec_spec()],
                  out_specs=(_vec_spec(128), rows(D), _vec_spec(), rows(D), _vec_spec()),
                  out_shape=(jax.ShapeDtypeStruct((1, 128), F32), jax.ShapeDtypeStruct((S, D), F32), vec,
                             jax.ShapeDtypeStruct((S, D), BF16), vec),
                  compiler_params=_params(("arbitrary",)))(act, w_ff2, x1, g2, tgt, g)


HALF = 512


MERGE_TM = 1024


def _merge_specs():
    blk = lambda off: pl.BlockSpec((MERGE_TM, HALF), lambda i, j: (i, off // HALF + j))
    return blk(OFF_GA), blk(OFF_GB), blk(0)


def _att_out_merge(att, w_att_out, proj, ret_out):
    def body(a_ref, b_ref, ga_ref, gb_ref, r_ref, o_ref, m_ref):
        acc = _dot(a_ref[...], b_ref[...], NN)
        o_ref[...] = acc
        m_ref[...] = (jax.nn.sigmoid(ga_ref[...]) * r_ref[...] + jax.nn.sigmoid(gb_ref[...]) * acc).astype(BF16)

    ga, gb, tile = _merge_specs()
    return _pcall(body, name="att_out", grid=(S // MERGE_TM, D // HALF),
                  in_specs=[pl.BlockSpec((MERGE_TM, AW), lambda i, j: (i, 0)), pl.BlockSpec((AW, HALF), lambda i, j: (0, j)),
                            ga, gb, tile],
                  out_specs=(tile, tile),
                  out_shape=(jax.ShapeDtypeStruct((S, D), F32), jax.ShapeDtypeStruct((S, D), BF16)),
                  compiler_params=_params(("parallel", "parallel")))(att, w_att_out, proj, proj, ret_out)


def _dmerged_split(dmixo, w_o, proj, ret_out, att_out):
    def body(a_ref, b_ref, ga_ref, gb_ref, r_ref, at_ref, dr_ref, da_ref, dga_ref, dgb_ref):
        dm = _dot(a_ref[...], b_ref[...], NT)
        sa = jax.nn.sigmoid(ga_ref[...])
        sb = jax.nn.sigmoid(gb_ref[...])
        dr_ref[...] = (dm * sa).astype(BF16)
        da_ref[...] = (dm * sb).astype(BF16)
        dga_ref[...] = (dm * r_ref[...] * (sa * (1.0 - sa))).astype(BF16)
        dgb_ref[...] = (dm * at_ref[...] * (sb * (1.0 - sb))).astype(BF16)

    ga, gb, tile = _merge_specs()
    o = jax.ShapeDtypeStruct((S, D), BF16)
    return _pcall(body, name="dmerged", grid=(S // MERGE_TM, D // HALF),
                  in_specs=[pl.BlockSpec((MERGE_TM, D), lambda i, j: (i, 0)), pl.BlockSpec((HALF, D), lambda i, j: (j, 0)),
                            ga, gb, tile, tile],
                  out_specs=(tile,) * 4, out_shape=(o, o, o, o),
                  compiler_params=_params(("parallel", "parallel")))(dmixo, w_o, proj, proj, ret_out, att_out)


def _ret_tables():
    H, C = RET_HEADS, CHUNK
    log_g = jnp.log1p(-(2.0 ** (-5.0 - jnp.arange(H, dtype=F32))))
    idx = jnp.arange(C, dtype=F32)
    rel = idx[:, None] - idx[None, :]
    inner = jnp.where(rel >= 0, jnp.exp(log_g[:, None, None] * jnp.maximum(rel, 0.0)), 0.0)
    qd = jnp.exp(log_g[:, None] * (idx + 1.0))[:, :, None]
    kd = jnp.exp(log_g[:, None] * (C - 1.0 - idx))[:, :, None]
    cd = jnp.broadcast_to(jnp.exp(log_g * C)[:, None, None], (H, 1, 128))
    half = RET_DK // 2
    inv = 10000.0 ** (-jnp.arange(half, dtype=F32) / half)
    ang = jnp.arange(S, dtype=F32)[:, None] * inv[None, :]
    return inner, qd, kd, cd, jnp.cos(ang), jnp.sin(ang)


def _rot(x, cos, sin):
    x1, x2 = x[:, :128], x[:, 128:]
    return jnp.concatenate([x1 * cos - x2 * sin, x1 * sin + x2 * cos], axis=1)


def _rot_t(d, cos, sin):
    d1, d2 = d[:, :128], d[:, 128:]
    return jnp.concatenate([d1 * cos + d2 * sin, d2 * cos - d1 * sin], axis=1)


RET_COLS = OFF_ATT
RET_VW = RET_HEADS * RET_DV


def _ret_specs(chunk_of):
    ci = chunk_of
    whole = lambda shape: pl.BlockSpec(shape, lambda t: (0,) * len(shape))
    return [
        pl.BlockSpec((CHUNK, RET_COLS), lambda t: (ci(t), 0)),
        pl.BlockSpec((CHUNK, 128), lambda t: (ci(t), 0)),
        pl.BlockSpec((CHUNK, 128), lambda t: (ci(t), 0)),
        whole((RET_HEADS, CHUNK, CHUNK)), whole((RET_HEADS, CHUNK, 1)), whole((RET_HEADS, CHUNK, 1)),
        whole((RET_HEADS, 1, 128)), whole((1, RET_VW)), whole((1, RET_VW)),
    ]


def _ret_cols(h):
    q = slice(OFF_RQ + h * RET_DK, OFF_RQ + (h + 1) * RET_DK)
    k = slice(OFF_RK + h * RET_DK, OFF_RK + (h + 1) * RET_DK)
    v = slice(OFF_RV + h * RET_DV, OFF_RV + (h + 1) * RET_DV)
    g = slice(OFF_RG + h * RET_DV, OFF_RG + (h + 1) * RET_DV)
    return q, k, v, g, slice(h * RET_DV, (h + 1) * RET_DV)


def _ret_fwd(proj, tables, gn_g, gn_b, comm=None):
    inner, qd, kd, cd, cos, sin = tables

    def body(x_ref, cos_ref, sin_ref, in_ref, qd_ref, kd_ref, cd_ref, g_ref, b_ref,
             gated_ref, ro_ref, st_ref, s_scr):
        i = pl.program_id(0)

        @pl.when(i == 0)
        def _():
            s_scr[...] = jnp.zeros_like(s_scr)

        cosv, sinv = cos_ref[...], sin_ref[...]
        for h in range(RET_HEADS):
            cq, ck, cv, cg, co = _ret_cols(h)
            q = _rot(x_ref[:, cq], cosv, sinv)
            k = _rot(x_ref[:, ck], cosv, sinv) * (RET_DK ** -0.5)
            v = x_ref[:, cv]
            st = s_scr[h]
            st_ref[h] = st.astype(BF16)
            s = _dot(q, k, NT) * in_ref[h]
            o = _dot(s, v, NN) + _dot(q, st, NN) * qd_ref[h]
            s_scr[h] = st * cd_ref[h, :, :1] + _dot(k * kd_ref[h], v, TN)
            ro_ref[:, co] = o
            mu = jnp.mean(o, axis=-1, keepdims=True)
            oc = o - mu
            var = jnp.mean(oc * oc, axis=-1, keepdims=True)
            rn = oc * lax.rsqrt(var + GN_EPS) * g_ref[:, co] + b_ref[:, co]
            rg = x_ref[:, cg]
            gated_ref[:, co] = (rg * jax.nn.sigmoid(rg) * rn).astype(BF16)

    ospec = pl.BlockSpec((CHUNK, RET_VW), lambda t: (t, 0))
    kw = dict(name="ret_fwd", grid=(N_CHUNK,), in_specs=_ret_specs(lambda t: t),
              out_specs=(ospec, ospec, pl.BlockSpec((RET_HEADS, None, RET_DK, RET_DV), lambda t: (0, t, 0, 0))),
              out_shape=(jax.ShapeDtypeStruct((S, RET_VW), BF16), jax.ShapeDtypeStruct((S, RET_VW), F32),
                         jax.ShapeDtypeStruct((RET_HEADS, N_CHUNK, RET_DK, RET_DV), BF16)),
              scratch_shapes=[pltpu.VMEM((RET_HEADS, RET_DK, RET_DV), F32)])
    args = (proj, cos, sin, inner, qd, kd, cd, gn_g, gn_b)
    if comm is not None:
        return _carry(body, comm, **kw)(*args)
    return _pcall(body, compiler_params=_params(("arbitrary",)), **kw)(*args)


def _ret_bwd(proj, tables, gn_g, gn_b, ro, states, dgated, comm=None):
    inner, qd, kd, cd, cos, sin = tables
    last = N_CHUNK - 1

    def body(x_ref, cos_ref, sin_ref, in_ref, qd_ref, kd_ref, cd_ref, g_ref, b_ref, ro_ref, st_ref, dg_ref,
             dx_ref, gg_ref, gb_ref, gs_scr):
        t = pl.program_id(0)

        @pl.when(t == 0)
        def _():
            gs_scr[...] = jnp.zeros_like(gs_scr)
            gg_ref[...] = jnp.zeros_like(gg_ref)
            gb_ref[...] = jnp.zeros_like(gb_ref)

        cosv, sinv = cos_ref[...], sin_ref[...]
        for h in range(RET_HEADS):
            cq, ck, cv, cg, co = _ret_cols(h)
            q = _rot(x_ref[:, cq], cosv, sinv)
            k = _rot(x_ref[:, ck], cosv, sinv) * (RET_DK ** -0.5)
            v = x_ref[:, cv]
            qdv, kdv, dm = qd_ref[h], kd_ref[h], in_ref[h]
            st = st_ref[h]
            o = ro_ref[:, co]
            gv = g_ref[:, co]
            mu = jnp.mean(o, axis=-1, keepdims=True)
            oc = o - mu
            rstd = lax.rsqrt(jnp.mean(oc * oc, axis=-1, keepdims=True) + GN_EPS)
            ohat = oc * rstd
            rn = ohat * gv + b_ref[:, co]
            rg = x_ref[:, cg]
            sg = jax.nn.sigmoid(rg)
            dgt = dg_ref[:, co]
            drn = dgt * (rg * sg)
            dx_ref[:, cg] = (dgt * rn * (sg * (1.0 + rg * (1.0 - sg)))).astype(BF16)
            gg_ref[:, co] += jnp.sum(drn * ohat, axis=0, keepdims=True)
            gb_ref[:, co] += jnp.sum(drn, axis=0, keepdims=True)
            dohat = drn * gv
            do = rstd * (dohat - jnp.mean(dohat, axis=-1, keepdims=True)
                         - ohat * jnp.mean(dohat * ohat, axis=-1, keepdims=True))
            gs = gs_scr[h]
            s = _dot(q, k, NT) * dm
            dsr = _dot(do, v, NT) * dm
            dq = _dot(dsr, k, NN) + _dot(do, st, NT) * qdv
            dk = _dot(dsr, q, TN) + _dot(v, gs, NT) * kdv
            dv = _dot(s, do, TN) + _dot(k * kdv, gs, NN)
            gs_scr[h] = gs * cd_ref[h, :, :1] + _dot(q * qdv, do, TN)
            dx_ref[:, cq] = _rot_t(dq, cosv, sinv).astype(BF16)
            dx_ref[:, ck] = (_rot_t(dk, cosv, sinv) * (RET_DK ** -0.5)).astype(BF16)
            dx_ref[:, cv] = dv.astype(BF16)

    rev = lambda t: last - t
    vblk = pl.BlockSpec((CHUNK, RET_VW), lambda t: (rev(t), 0))
    vspec = pl.BlockSpec((1, RET_VW), lambda t: (0, 0))
    kw = dict(name="ret_bwd", grid=(N_CHUNK,),
              in_specs=_ret_specs(rev) + [vblk, pl.BlockSpec((RET_HEADS, None, RET_DK, RET_DV),
                                                             lambda t: (0, rev(t), 0, 0)), vblk],
              out_specs=(pl.BlockSpec((CHUNK, RET_COLS), lambda t: (rev(t), 0)), vspec, vspec),
              out_shape=(jax.ShapeDtypeStruct((S, RET_COLS), BF16), jax.ShapeDtypeStruct((1, RET_VW), F32),
                         jax.ShapeDtypeStruct((1, RET_VW), F32)),
              scratch_shapes=[pltpu.VMEM((RET_HEADS, RET_DK, RET_DV), F32)])
    args = (proj, cos, sin, inner, qd, kd, cd, gn_g, gn_b, ro, states, dgated)
    if comm is not None:
        return _carry(body, comm, **kw)(*args)
    return _pcall(body, compiler_params=_params(("arbitrary",)), **kw)(*args)


def _bucket_tables():
    qi = np.arange(ATT_BLK)[:, None]
    kj = np.arange(2 * ATT_BLK)[None, :]
    m = ATT_BLK + qi - kj
    out = []
    for win, dil in ATT_GROUPS:
        w = win // dil
        dist = (np.clip(m, 0, w) * dil).astype(np.int32)
        max_exact = N_BUCKETS // 2
        d_f = np.maximum(dist, 1).astype(np.float32)
        large = max_exact + (np.log(d_f / np.float32(max_exact)) / np.float32(math.log(MAX_DIST / max_exact))
                             * np.float32(N_BUCKETS - max_exact)).astype(np.int32)
        large = np.minimum(large, N_BUCKETS - 1)
        out.append(np.where(dist < max_exact, dist, large).astype(np.int32))
    return np.stack(out)


def _bias_build(rel_bias, buckets):
    def body(tab_ref, bk_ref, o_ref):
        hh = pl.program_id(0)
        bk = bk_ref[...]
        acc = jnp.zeros((ATT_BLK, 2 * ATT_BLK), F32)
        for b in range(N_BUCKETS):
            acc = jnp.where(bk == b, tab_ref[b, hh], acc)
        o_ref[...] = acc

    nh = len(ATT_GROUPS) * ATT_HG
    return _pcall(body, name="bias_build", grid=(nh,),
                  in_specs=[pl.BlockSpec(memory_space=pltpu.SMEM),
                            pl.BlockSpec((None, ATT_BLK, 2 * ATT_BLK), lambda hh: (hh // ATT_HG, 0, 0))],
                  out_specs=pl.BlockSpec((None, ATT_BLK, 2 * ATT_BLK), lambda hh: (hh, 0, 0)),
                  out_shape=jax.ShapeDtypeStruct((nh, ATT_BLK, 2 * ATT_BLK), F32),
                  compiler_params=_params(("parallel",)))(rel_bias, buckets)


def _bias_grad(ds_sum, buckets):
    def body(ds_ref, bk_ref, o_ref):
        bk = bk_ref[...]
        ds = ds_ref[...]
        rows = lax.broadcasted_iota(jnp.int32, (N_BUCKETS, 128), 0)
        acc = jnp.zeros((N_BUCKETS, 128), F32)
        for b in range(N_BUCKETS):
            acc = jnp.where(rows == b, jnp.sum(jnp.where(bk == b, ds, 0.0)), acc)
        o_ref[...] = acc

    nh = len(ATT_GROUPS) * ATT_HG
    return _pcall(body, name="bias_grad", grid=(nh,),
                  in_specs=[pl.BlockSpec((None, ATT_BLK, 2 * ATT_BLK), lambda hh: (hh, 0, 0)),
                            pl.BlockSpec((None, ATT_BLK, 2 * ATT_BLK), lambda hh: (hh // ATT_HG, 0, 0))],
                  out_specs=pl.BlockSpec((None, N_BUCKETS, 128), lambda hh: (hh, 0, 0)),
                  out_shape=jax.ShapeDtypeStruct((nh, N_BUCKETS, 128), F32),
                  compiler_params=_params(("parallel",)))(ds_sum, buckets)


def _att_valid(n):
    qi = lax.broadcasted_iota(jnp.int32, (ATT_BLK, 2 * ATT_BLK), 0)
    kj = lax.broadcasted_iota(jnp.int32, (ATT_BLK, 2 * ATT_BLK), 1)
    m = ATT_BLK + qi - kj
    first_key = jnp.where(n > 0, 0, ATT_BLK)
    return (m >= 0) & (m <= ATT_BLK) & (kj >= first_key)


ATT_HP = (1, 2, 2)


def _att_geometry(gi):
    _, dil = ATT_GROUPS[gi]
    return dil, S // dil // ATT_BLK, ATT_HP[gi]


def _blk(dil, r, n):
    if dil == 1:
        return pl.ds(n * ATT_BLK, ATT_BLK)
    return pl.ds(r + n * ATT_BLK * dil, ATT_BLK, stride=dil)


def _slab_specs(gi):
    _, _, hp = _att_geometry(gi)
    per = ATT_HG // hp
    return [pl.BlockSpec((hp, S, ATT_DH), lambda g, r, part=part: ((3 * gi + part) * per + g, 0, 0))
            for part in range(3)]


def _head_specs(gi, count):
    _, _, hp = _att_geometry(gi)
    return [pl.BlockSpec((hp, S, ATT_DH), lambda g, r: (g, 0, 0))] * count


def _bias_spec(gi):
    _, _, hp = _att_geometry(gi)
    return pl.BlockSpec((hp, ATT_BLK, 2 * ATT_BLK), lambda g, r: (gi * (ATT_HG // hp) + g, 0, 0))


def _att_valid_first():
    qi = lax.broadcasted_iota(jnp.int32, (ATT_BLK, ATT_BLK), 0)
    kj = lax.broadcasted_iota(jnp.int32, (ATT_BLK, ATT_BLK), 1)
    return kj <= qi


def _att_fwd(slabs, bias, gi, comm=None):
    dil, nb, hp = _att_geometry(gi)
    scale = ATT_DH ** -0.5

    def body(q_ref, k_ref, v_ref, bias_ref, o_ref, l_ref):
        r = pl.program_id(1)
        for n in range(nb):
            cur = _blk(dil, r, n)
            valid = _att_valid(n) if n > 0 else _att_valid_first()
            for h in range(hp):
                if n > 0:
                    prev = _blk(dil, r, n - 1)
                    kk = jnp.concatenate([k_ref[h, prev, :], k_ref[h, cur, :]], axis=0)
                    vv = jnp.concatenate([v_ref[h, prev, :], v_ref[h, cur, :]], axis=0)
                    bias = bias_ref[h]
                else:
                    kk, vv, bias = k_ref[h, cur, :], v_ref[h, cur, :], bias_ref[h, :, pl.ds(ATT_BLK, ATT_BLK)]
                s = _dot(q_ref[h, cur, :], kk, NT) * scale + bias
                s = jnp.where(valid, s, -1e30)
                mx = jnp.max(s, axis=-1, keepdims=True)
                e = jnp.exp(s - mx)
                den = jnp.sum(e, axis=-1, keepdims=True)
                o_ref[h, cur, :] = _dot(e / den, vv, NN)
                l_ref[h, cur, :] = jnp.broadcast_to(mx + jnp.log(den), (ATT_BLK, ATT_DH))

    osh = jax.ShapeDtypeStruct((ATT_HG, S, ATT_DH), F32)
    kw = dict(name=f"att_fwd{gi}", grid=(ATT_HG // hp, dil), in_specs=_slab_specs(gi) + [_bias_spec(gi)],
              out_specs=tuple(_head_specs(gi, 2)), out_shape=(osh, osh))
    if comm is not None:
        return _carry(body, comm, **kw)(slabs, slabs, slabs, bias)
    return _pcall(body, compiler_params=_params(("parallel", "arbitrary")), **kw)(slabs, slabs, slabs, bias)


def _att_bwd(slabs, bias, o, lse, do, dlse, gi, comm=None):
    dil, nb, hp = _att_geometry(gi)
    per = ATT_HG // hp
    scale = ATT_DH ** -0.5
    wh = hp * ATT_DH
    wide = lambda t: jnp.concatenate([t, t], axis=1)

    def body(q_ref, k_ref, v_ref, bias_ref, o_ref, l_ref, do_ref, dl_ref, dq_ref, dk_ref, dv_ref, ds_ref):
        r = pl.program_id(1)

        @pl.when(r == 0)
        def _():
            ds_ref[...] = jnp.zeros_like(ds_ref)

        for h in range(hp):
            sl = slice(h * ATT_DH, (h + 1) * ATT_DH)
            carry_k = carry_v = None
            for n in range(nb):
                cur = _blk(dil, r, n)
                q = q_ref[h, cur, :]
                dov = do_ref[h, cur, :]
                delta = jnp.sum(dov * o_ref[h, cur, :], axis=-1, keepdims=True)
                out_rows = pl.ds(n * ATT_BLK, ATT_BLK)
                if n == 0:
                    own = pl.ds(ATT_BLK, ATT_BLK)
                    kk, vv = k_ref[h, cur, :], v_ref[h, cur, :]
                    s = _dot(q, kk, NT) * scale + bias_ref[h, :, own]
                    p = jnp.where(_att_valid_first(), jnp.exp(s - l_ref[h, cur, :]), 0.0)
                    ds = p * (_dot(dov, vv, NT) - delta + dl_ref[h, cur, :])
                    ds_ref[h, :, own] += ds
                    dq_ref[out_rows, sl] = (_dot(ds, kk, NN) * scale).astype(BF16)
                    carry_k, carry_v = _dot(ds, q, TN) * scale, _dot(p, dov, TN)
                    continue
                prev = _blk(dil, r, n - 1)
                kk = jnp.concatenate([k_ref[h, prev, :], k_ref[h, cur, :]], axis=0)
                vv = jnp.concatenate([v_ref[h, prev, :], v_ref[h, cur, :]], axis=0)
                s = _dot(q, kk, NT) * scale + bias_ref[h]
                p = jnp.where(_att_valid(n), jnp.exp(s - wide(l_ref[h, cur, :])), 0.0)
                dp = _dot(dov, vv, NT)
                ds = p * (dp - delta + wide(dl_ref[h, cur, :]))
                ds_ref[h] += ds
                dq_ref[out_rows, sl] = (_dot(ds, kk, NN) * scale).astype(BF16)
                dkk = _dot(ds, q, TN) * scale
                dvv = _dot(p, dov, TN)
                before = pl.ds((n - 1) * ATT_BLK, ATT_BLK)
                dk_ref[before, sl] = (carry_k + dkk[:ATT_BLK]).astype(BF16)
                dv_ref[before, sl] = (carry_v + dvv[:ATT_BLK]).astype(BF16)
                carry_k, carry_v = dkk[ATT_BLK:], dvv[ATT_BLK:]
            last = pl.ds((nb - 1) * ATT_BLK, ATT_BLK)
            dk_ref[last, sl] = carry_k.astype(BF16)
            dv_ref[last, sl] = carry_v.astype(BF16)

    out_spec = pl.BlockSpec((S // dil, wh), lambda g, r: (0, r * per + g))
    osh = jax.ShapeDtypeStruct((S // dil, dil * AW), BF16)
    kw = dict(name=f"att_bwd{gi}", grid=(per, dil), in_specs=_slab_specs(gi) + [_bias_spec(gi)] + _head_specs(gi, 4),
              out_specs=(out_spec, out_spec, out_spec,
                         pl.BlockSpec((hp, ATT_BLK, 2 * ATT_BLK), lambda g, r: (g, 0, 0))),
              out_shape=(osh, osh, osh, jax.ShapeDtypeStruct((ATT_HG, ATT_BLK, 2 * ATT_BLK), F32)))
    args = (slabs, slabs, slabs, bias, o, lse, do, dlse)
    if comm is not None:
        return _carry(body, comm, **kw)(*args)
    return _pcall(body, compiler_params=_params(("arbitrary", "arbitrary")), **kw)(*args)


AW = ATT_HG * ATT_DH


def _mix_weights(l0, l1, l2):
    mx = jnp.maximum(jnp.maximum(l0, l1), l2)
    e0, e1, e2 = jnp.exp(l0 - mx), jnp.exp(l1 - mx), jnp.exp(l2 - mx)
    den = e0 + e1 + e2
    return e0 / den, e1 / den, e2 / den


def _heads_spec():
    return pl.BlockSpec((ATT_HG, TR, ATT_DH), lambda i: (0, i, 0))


def _mix_fwd(os_, ls, comm=None):
    def body(o0, o1, o2, l0, l1, l2, att_ref):
        for h in range(ATT_HG):
            w0, w1, w2 = _mix_weights(l0[h], l1[h], l2[h])
            att_ref[:, h * ATT_DH:(h + 1) * ATT_DH] = (w0 * o0[h] + w1 * o1[h] + w2 * o2[h]).astype(BF16)

    kw = dict(name="mix_fwd", grid=(S // TR,), in_specs=[_heads_spec()] * 6, out_specs=_row_spec(AW),
              out_shape=jax.ShapeDtypeStruct((S, AW), BF16))
    if comm is not None:
        return _carry(body, comm, **kw)(*os_, *ls)
    return _pcall(body, compiler_params=_params(("parallel",)), **kw)(*os_, *ls)


def _mix_bwd(os_, ls, datt):
    def body(o0, o1, o2, l0, l1, l2, da_ref, d0, d1, d2, e0, e1, e2):
        for h in range(ATT_HG):
            ws = _mix_weights(l0[h], l1[h], l2[h])
            da = da_ref[:, h * ATT_DH:(h + 1) * ATT_DH]
            dws = []
            for o_ref, w, d_ref in zip((o0, o1, o2), ws, (d0, d1, d2)):
                d_ref[h] = w * da
                dws.append(jnp.broadcast_to(jnp.sum(da * o_ref[h], axis=-1, keepdims=True), (TR, ATT_DH)))
            tot = ws[0] * dws[0] + ws[1] * dws[1] + ws[2] * dws[2]
            for w, dw, e_ref in zip(ws, dws, (e0, e1, e2)):
                e_ref[h] = w * (dw - tot)

    o = jax.ShapeDtypeStruct((ATT_HG, S, ATT_DH), F32)
    return _pcall(body, name="mix_bwd", grid=(S // TR,), in_specs=[_heads_spec()] * 6 + [_row_spec(AW)],
                  out_specs=(_heads_spec(),) * 6, out_shape=(o,) * 6,
                  compiler_params=_params(("parallel",)))(*os_, *ls, datt)


def _ada_fwd(c_all, w_sh, b_sl):
    def body(c_ref, w_ref, b_ref, o_ref):
        cv = c_ref[...]
        o_ref[...] = _dot(cv * jax.nn.sigmoid(cv), w_ref[...], NN) + b_ref[...]

    return _pcall(body, name="ada_fwd", out_shape=jax.ShapeDtypeStruct((N_DEV, w_sh.shape[1]), F32),
                  compiler_params=_params())(c_all, w_sh, b_sl)


def _ada_bwd(c_all, dm_sl):
    def body(c_ref, d_ref, o_ref):
        cv = c_ref[...]
        o_ref[...] = _dot(cv * jax.nn.sigmoid(cv), d_ref[...], TN)

    return _pcall(body, name="ada_bwd", out_shape=jax.ShapeDtypeStruct((D, dm_sl.shape[1]), F32),
                  compiler_params=_params())(c_all, dm_sl)


N_MOD = 6


def _sum_small(gathered):
    n = len(gathered)

    def body(*refs):
        ins, (gb_ref, dm_ref), outs = refs[:n], refs[n:n + 2], refs[n + 2:]

        def total(r):
            acc = r[0]
            for e in range(1, N_DEV):
                acc = acc + r[e]
            return acc

        for i in range(N_MOD):
            cols = slice(i * D, (i + 1) * D)
            gb_ref[:, cols] = total(ins[i])
            for e in range(N_DEV):
                dm_ref[e:e + 1, cols] = ins[i][e]
        for r, o_ref in zip(ins[N_MOD:], outs):
            o_ref[...] = total(r)

    shapes = (jax.ShapeDtypeStruct((1, N_MOD * D), F32), jax.ShapeDtypeStruct((N_DEV, N_MOD * D), F32),
              *[jax.ShapeDtypeStruct(g.shape[1:], F32) for g in gathered[N_MOD:]])
    res = _pcall(body, name="sum_small", out_shape=shapes, compiler_params=_params())(*gathered)
    return res[0], res[1], res[2:]


def _row_tile(m, n):
    t = max(8, min(m, (1 << 19) // n // 8 * 8))
    while m % t:
        t -= 8
    return t


def _pair_sum(full, recv, sel, name, col_block=0):
    _, m, n = recv.shape
    t = _row_tile(m, n)

    def body(sel_ref, a_ref, b_ref, o_ref):
        o_ref[...] = (a_ref[...].astype(F32) + b_ref[...].astype(F32)).astype(o_ref.dtype)

    gs = pltpu.PrefetchScalarGridSpec(
        num_scalar_prefetch=1, grid=(4, m // t),
        in_specs=[pl.BlockSpec((None, None, t, n), lambda q, i, s: (q, s[0], i, col_block)),
                  pl.BlockSpec((None, t, n), lambda q, i, s: (q, i, 0))],
        out_specs=pl.BlockSpec((None, t, n), lambda q, i, s: (q, i, 0)))
    return _pcall(body, name=name, grid_spec=gs, out_shape=jax.ShapeDtypeStruct((4, m, n), full.dtype),
                  compiler_params=_params(("parallel", "parallel")))(sel, full, recv)


def _chip_sum(part, recv, sel, name):
    _, m, n = part.shape
    t = _row_tile(m, n)

    def body(sel_ref, a_ref, r_ref, o_ref):
        o_ref[...] = ((a_ref[...].astype(F32) + r_ref[0].astype(F32)) + r_ref[1].astype(F32)) + r_ref[2].astype(F32)

    gs = pltpu.PrefetchScalarGridSpec(
        num_scalar_prefetch=1, grid=(m // t,),
        in_specs=[pl.BlockSpec((None, t, n), lambda i, s: (s[0], i, 0)),
                  pl.BlockSpec((3, t, n), lambda i, s: (0, i, 0))],
        out_specs=pl.BlockSpec((t, n), lambda i, s: (i, 0)))
    return _pcall(body, name=name, grid_spec=gs, out_shape=jax.ShapeDtypeStruct((m, n), F32),
                  compiler_params=_params(("parallel",)))(sel, part, recv)


def _adamw_math(w, g, m, v):
    nm = ADAM_B1 * m + (1.0 - ADAM_B1) * g
    nv = ADAM_B2 * v + (1.0 - ADAM_B2) * (g * g)
    m_hat = nm / (1.0 - ADAM_B1 ** ADAM_STEP)
    v_hat = nv / (1.0 - ADAM_B2 ** ADAM_STEP)
    return -ADAM_LR * (m_hat / (jnp.sqrt(v_hat) + ADAM_EPS) + ADAM_WD * w), nm, nv


def _adamw(w, g, m, v, name):
    _, rows, cols = w.shape
    t = _row_tile(rows, cols)

    def body(w_ref, g_ref, m_ref, v_ref, d_ref, nm_ref, nv_ref):
        d_ref[...], nm_ref[...], nv_ref[...] = _adamw_math(w_ref[...], g_ref[...], m_ref[...], v_ref[...])

    spec3 = pl.BlockSpec((None, t, cols), lambda i: (0, i, 0))
    spec2 = pl.BlockSpec((t, cols), lambda i: (i, 0))
    o = jax.ShapeDtypeStruct(w.shape, F32)
    return _pcall(body, name=name, grid=(rows // t,), in_specs=[spec3, spec2, spec3, spec3], out_specs=(spec3,) * 3,
                  out_shape=(o, o, o), compiler_params=_params(("parallel",)))(w, g, m, v)


def _adamw_reduced1(w, m, v, part, recv, sel, name):
    _, rows, cols = w.shape
    t = _row_tile(rows, cols)

    def body(sel_ref, w_ref, m_ref, v_ref, p_ref, r_ref, g_ref, d_ref, nm_ref, nv_ref):
        g = ((p_ref[...].astype(F32) + r_ref[0].astype(F32)) + r_ref[1].astype(F32)) + r_ref[2].astype(F32)
        g_ref[...] = g
        d_ref[...], nm_ref[...], nv_ref[...] = _adamw_math(w_ref[...], g, m_ref[...], v_ref[...])

    wspec = pl.BlockSpec((None, t, cols), lambda i, s: (0, i, 0))
    gs = pltpu.PrefetchScalarGridSpec(
        num_scalar_prefetch=1, grid=(rows // t,),
        in_specs=[wspec, wspec, wspec, pl.BlockSpec((None, t, cols), lambda i, s: (s[0], i, 0)),
                  pl.BlockSpec((3, t, cols), lambda i, s: (0, i, 0))],
        out_specs=(wspec,) * 4)
    o = jax.ShapeDtypeStruct(w.shape, F32)
    return _pcall(body, name=name, grid_spec=gs, out_shape=(o, o, o, o),
                  compiler_params=_params(("parallel",)))(sel, w, m, v, part, recv)


def _adamw_sum2(w, m, v, sums, last, name):
    _, rows, cols = w.shape
    t = _row_tile(rows, cols)

    def body(w_ref, m_ref, v_ref, a_ref, b_ref, g_ref, d_ref, nm_ref, nv_ref):
        g = a_ref[...].astype(F32) + b_ref[...].astype(F32)
        g_ref[...] = g
        d_ref[...], nm_ref[...], nv_ref[...] = _adamw_math(w_ref[...], g, m_ref[...], v_ref[...])

    wspec = pl.BlockSpec((None, t, cols), lambda i: (0, i, 0))
    o = jax.ShapeDtypeStruct(w.shape, F32)
    return _pcall(body, name=name, grid=(rows // t,),
                  in_specs=[wspec, wspec, wspec, wspec, pl.BlockSpec((t, cols), lambda i: (i, 0))],
                  out_specs=(wspec,) * 4, out_shape=(o, o, o, o),
                  compiler_params=_params(("parallel",)))(w, m, v, sums, last)


def _adamw_small(ws, gs, ms, vs):
    n = len(ws)

    def body(*refs):
        for i in range(n):
            w_ref, g_ref, m_ref, v_ref = (refs[k * n + i] for k in range(4))
            d, nm, nv = _adamw_math(w_ref[...], g_ref[...], m_ref[...], v_ref[...])
            refs[4 * n + i][...] = d
            refs[5 * n + i][...] = nm
            refs[6 * n + i][...] = nv

    shapes = tuple(jax.ShapeDtypeStruct(w.shape, F32) for w in ws)
    res = _pcall(body, name="adamw_small", out_shape=shapes * 3, compiler_params=_params())(*ws, *gs, *ms, *vs)
    return res[:n], res[n:2 * n], res[2 * n:]


def _mesh_pos():
    return lax.axis_index("x"), lax.axis_index("y"), lax.axis_index("c")


class _Gather:
    def __init__(self, arrs, relay=False, chunks=None):
        self.relay = relay
        self.ins = list(arrs)
        self.out_shape = tuple(jax.ShapeDtypeStruct((N_DEV,) + a.shape, a.dtype) for a in arrs)
        self.pieces = []
        for a, arr in enumerate(arrs):
            n = (chunks or {}).get(a, 1)
            rows = arr.shape[0] // n
            self.pieces += [(a, None if n == 1 else pl.ds(i * rows, rows)) for i in range(n)]
        npc = len(self.pieces)
        self.sems = [pltpu.SemaphoreType.DMA((7 * npc,)), pltpu.SemaphoreType.DMA((7 * npc,)),
                     pltpu.SemaphoreType.DMA((npc,))]

    def _copies(self, ins, outs, sems):
        send_sems, recv_sems, local_sems = sems
        x, y, c = _mesh_pos()
        me, sibling = (x, y, c), (x, y, 1 - c)
        chips = [(1 - x, y), (x, 1 - y), (1 - x, 1 - y)]

        def src_of(p):
            a, rows = self.pieces[p]
            return ins[a] if rows is None else ins[a].at[rows]

        def dst_of(p, block):
            a, rows = self.pieces[p]
            ref = outs[a].at[_slot(block)]
            return ref if rows is None else ref.at[rows]

        def copy(p, k, block, to, from_input=False):
            dst = dst_of(p, block)
            return pltpu.make_async_remote_copy(
                src_ref=src_of(p) if from_input else dst, dst_ref=dst, send_sem=send_sems.at[7 * p + k],
                recv_sem=recv_sems.at[7 * p + k], device_id=to, device_id_type=MESH)

        npc = len(self.pieces)
        mine = [pltpu.make_async_copy(src_of(p), dst_of(p, me), local_sems.at[p]) for p in range(npc)]
        direct = chips[:2] if self.relay else chips
        first = []
        for p in range(npc):
            first.append(copy(p, 0, me, sibling, from_input=True))
            first += [copy(p, 1 + j, me, (*chip, c), from_input=True) for j, chip in enumerate(direct)]
        return me, sibling, chips, c, copy, mine, first

    def start(self, ins, outs, sems):
        *_, mine, first = self._copies(ins, outs, sems)
        for cp in mine + first:
            cp.start()

    def finish(self, ins, outs, sems):
        me, sibling, chips, c, copy, mine, first = self._copies(ins, outs, sems)
        x, y = me[0], me[1]
        npc = len(self.pieces)
        passed = []

        def pass_on(cp):
            cp.start()
            passed.append(cp)

        for p in range(npc):
            for j in range(2):
                copy(p, 1 + j, (*chips[j], c), me).wait_recv()
                pass_on(copy(p, 4 + j, (*chips[j], c), sibling))
            if self.relay:
                owner = ((x + 1 - c) % 2, (y + c) % 2, c)
                pass_on(copy(p, 3, owner, ((x + c) % 2, (y + 1 - c) % 2, c)))
        for p in range(npc):
            copy(p, 3, (*chips[2], c), me).wait_recv()
            pass_on(copy(p, 6, (*chips[2], c), sibling))
        for p in range(npc):
            copy(p, 0, sibling, me).wait_recv()
            for j, chip in enumerate(chips):
                copy(p, 4 + j, (*chip, 1 - c), me).wait_recv()
        for cp in first + passed:
            cp.wait_send()
        for cp in mine:
            cp.wait()


class _ExchangeCore:
    def __init__(self, fulls, cols=None):
        self.ins = list(fulls)
        self.cols = cols
        width = lambda f: f.shape[3] if cols is None else cols[1]
        self.out_shape = tuple(jax.ShapeDtypeStruct((4, f.shape[2], width(f)), f.dtype) for f in fulls)
        self.sems = [pltpu.SemaphoreType.DMA((4 * len(fulls),)), pltpu.SemaphoreType.DMA((4 * len(fulls),))]

    def _copies(self, ins, outs, sems):
        send_sems, recv_sems = sems
        x, y, c = _mesh_pos()

        def src(a, q):
            ref = ins[a].at[q, 1 - c]
            return ref if self.cols is None else ref.at[:, pl.ds(*self.cols)]

        return [pltpu.make_async_remote_copy(
            src_ref=src(a, q), dst_ref=outs[a].at[q], send_sem=send_sems.at[4 * a + q],
            recv_sem=recv_sems.at[4 * a + q], device_id=(x, y, 1 - c), device_id_type=MESH)
            for a in range(len(self.ins)) for q in range(4)]

    def start(self, ins, outs, sems):
        for cp in self._copies(ins, outs, sems):
            cp.start()

    def finish(self, ins, outs, sems):
        for cp in self._copies(ins, outs, sems):
            cp.wait()


class _ExchangeChip:
    def __init__(self, parts):
        self.ins = list(parts)
        self.out_shape = tuple(jax.ShapeDtypeStruct((3,) + p.shape[1:], p.dtype) for p in parts)
        self.sems = [pltpu.SemaphoreType.DMA((3 * len(parts),)), pltpu.SemaphoreType.DMA((3 * len(parts),))]

    def _copies(self, ins, outs, sems):
        send_sems, recv_sems = sems
        x, y, c = _mesh_pos()
        chips = [(1 - x, y), (x, 1 - y), (1 - x, 1 - y)]
        return [pltpu.make_async_remote_copy(
            src_ref=ins[a].at[2 * px + py], dst_ref=outs[a].at[j], send_sem=send_sems.at[3 * a + j],
            recv_sem=recv_sems.at[3 * a + j], device_id=(px, py, c), device_id_type=MESH)
            for a in range(len(self.ins)) for j, (px, py) in enumerate(chips)]

    def start(self, ins, outs, sems):
        for cp in self._copies(ins, outs, sems):
            cp.start()

    def finish(self, ins, outs, sems):
        for cp in self._copies(ins, outs, sems):
            cp.wait()


def _axis_neighbours():
    x, y, c = _mesh_pos()
    return (x, y, c), ((x + 1 - c) % 2, (y + c) % 2), ((x + c) % 2, (y + 1 - c) % 2)


class _ExchangeAxis1:
    def __init__(self, part):
        self.ins = [part]
        self.out_shape = (jax.ShapeDtypeStruct((2,) + part.shape[1:], part.dtype),)
        self.sems = [pltpu.SemaphoreType.DMA((2,)), pltpu.SemaphoreType.DMA((2,))]

    def _copies(self, ins, outs, sems):
        (x, y, c), n1, _ = _axis_neighbours()
        sent = [2 * n1[0] + n1[1], 2 * (1 - x) + (1 - y)]
        return [pltpu.make_async_remote_copy(
            src_ref=ins[0].at[sent[k]], dst_ref=outs[0].at[k], send_sem=sems[0].at[k], recv_sem=sems[1].at[k],
            device_id=(*n1, c), device_id_type=MESH) for k in range(2)]

    def start(self, ins, outs, sems):
        for cp in self._copies(ins, outs, sems):
            cp.start()

    def finish(self, ins, outs, sems):
        for cp in self._copies(ins, outs, sems):
            cp.wait()


class _ExchangeAxis2:
    def __init__(self, sums):
        self.ins = [sums]
        self.out_shape = (jax.ShapeDtypeStruct(sums.shape[1:], sums.dtype),)
        self.sems = [pltpu.SemaphoreType.DMA((1,)), pltpu.SemaphoreType.DMA((1,))]

    def _copy(self, ins, outs, sems):
        (x, y, c), _, n2 = _axis_neighbours()
        return pltpu.make_async_remote_copy(src_ref=ins[0].at[1], dst_ref=outs[0], send_sem=sems[0].at[0],
                                            recv_sem=sems[1].at[0], device_id=(*n2, c), device_id_type=MESH)

    def start(self, ins, outs, sems):
        self._copy(ins, outs, sems).start()

    def finish(self, ins, outs, sems):
        self._copy(ins, outs, sems).wait()


def _axis_sum(part, recv, sel, name):
    _, m, n = part.shape
    t = _row_tile(m, n)

    def body(sel_ref, a_ref, b_ref, o_ref):
        o_ref[...] = (a_ref[...].astype(F32) + b_ref[...].astype(F32)).astype(o_ref.dtype)

    gs = pltpu.PrefetchScalarGridSpec(
        num_scalar_prefetch=1, grid=(2, m // t),
        in_specs=[pl.BlockSpec((None, t, n), lambda j, i, s: (s[j], i, 0)),
                  pl.BlockSpec((None, t, n), lambda j, i, s: (j, i, 0))],
        out_specs=pl.BlockSpec((None, t, n), lambda j, i, s: (j, i, 0)))
    return _pcall(body, name=name, grid_spec=gs, out_shape=jax.ShapeDtypeStruct((2, m, n), part.dtype),
                  compiler_params=_params(("parallel", "parallel")))(sel, part, recv)


HBM_ONLY = pl.BlockSpec(memory_space=pltpu.HBM)
SEM_SPEC = pl.BlockSpec(memory_space=pltpu.SEMAPHORE)
SIDE_EFFECT = pltpu.SideEffectType.DATAFLOW_SIDE_EFFECTING


def _chip_copies(p_refs, land_refs, send_sems, recv_sems):
    x, y, c = _mesh_pos()
    return [pltpu.make_async_remote_copy(
        src_ref=p_refs[a].at[2 * px + py], dst_ref=land_refs[a].at[j], send_sem=send_sems.at[3 * a + j],
        recv_sem=recv_sems.at[3 * a + j], device_id=(px, py, c), device_id_type=MESH)
        for a in range(len(p_refs)) for j, (px, py) in enumerate([(1 - x, y), (x, 1 - y), (1 - x, 1 - y)])]


def _chip_exchange_start(parts, name):
    n = len(parts)
    lands = [lax.empty((3,) + p.shape[1:], p.dtype) for p in parts]

    def body(*refs):
        p_refs, land_refs, (send_sems, recv_sems) = refs[:n], refs[n:2 * n], refs[2 * n:2 * n + 2]
        for cp in _chip_copies(p_refs, land_refs, send_sems, recv_sems):
            cp.start()
        token = refs[-1]
        token[...] = jnp.zeros_like(token)

    hbm = lambda t: pltpu.HBM(t.shape, t.dtype)
    res = pl.pallas_call(
        body, name=name,
        out_shape=(pltpu.SemaphoreType.DMA((3 * n,)), pltpu.SemaphoreType.DMA((3 * n,)), *[hbm(t) for t in parts + lands],
                   jax.ShapeDtypeStruct((8, 128), F32)),
        in_specs=(HBM_ONLY,) * (2 * n),
        out_specs=(SEM_SPEC, SEM_SPEC, *[HBM_ONLY] * (2 * n), pl.BlockSpec(memory_space=pltpu.VMEM)),
        input_output_aliases={i: 2 + i for i in range(2 * n)},
        compiler_params=pltpu.CompilerParams(has_side_effects=SIDE_EFFECT))(
        *[pltpu.with_memory_space_constraint(t, pltpu.HBM) for t in parts + lands])
    return (res[0], res[1], list(res[2:2 + n]), list(res[2 + n:2 + 2 * n])), res[-1]


def _chip_exchange_wait(in_flight, after, name):
    send_sems, recv_sems, parts, lands = in_flight
    n = len(parts)

    def body(*refs):
        p_refs, land_refs, (send_sems, recv_sems) = refs[:n], refs[n:2 * n], refs[2 * n:2 * n + 2]
        for cp in _chip_copies(p_refs, land_refs, send_sems, recv_sems):
            cp.wait_send()
            cp.wait_recv()

    res = pl.pallas_call(
        body, name=name, out_shape=tuple(pltpu.HBM(t.shape, t.dtype) for t in parts + lands),
        in_specs=(*[HBM_ONLY] * (2 * n), SEM_SPEC, SEM_SPEC, pl.BlockSpec(memory_space=pl.ANY)),
        out_specs=(HBM_ONLY,) * (2 * n), input_output_aliases={i: i for i in range(2 * n)},
        compiler_params=pltpu.CompilerParams(has_side_effects=SIDE_EFFECT))(*parts, *lands, send_sems, recv_sems, after)
    return list(res[:n]), list(res[n:])


def _slot(p):
    return 4 * p[0] + 2 * p[1] + p[2]


def _gather_copies(src_refs, out_refs, send_sems, recv_sems):
    x, y, c = _mesh_pos()
    targets = [(x, y, 1 - c), (1 - x, y, c), (x, 1 - y, c), (1 - x, 1 - y, c)]
    return [pltpu.make_async_remote_copy(
        src_ref=src_refs[a], dst_ref=out_refs[a].at[_slot((x, y, c))], send_sem=send_sems.at[4 * a + k],
        recv_sem=recv_sems.at[4 * a + k], device_id=to, device_id_type=MESH)
        for a in range(len(src_refs)) for k, to in enumerate(targets)]


def _gather_start(shards, after, name):
    n = len(shards)
    outs = [lax.empty((N_DEV,) + s.shape, s.dtype) for s in shards]

    def body(*refs):
        for cp in _gather_copies(refs[:n], refs[n:2 * n], refs[2 * n + 1], refs[2 * n + 2]):
            cp.start()
        token = refs[-1]
        token[...] = jnp.zeros_like(token)

    res = pl.pallas_call(
        body, name=name,
        out_shape=(pltpu.SemaphoreType.DMA((4 * n,)), pltpu.SemaphoreType.DMA((4 * n,)),
                   *[pltpu.HBM(t.shape, t.dtype) for t in shards + outs], jax.ShapeDtypeStruct((8, 128), F32)),
        in_specs=(*[HBM_ONLY] * (2 * n), pl.BlockSpec(memory_space=pl.ANY)),
        out_specs=(SEM_SPEC, SEM_SPEC, *[HBM_ONLY] * (2 * n), pl.BlockSpec(memory_space=pltpu.VMEM)),
        input_output_aliases={i: 2 + i for i in range(2 * n)},
        compiler_params=pltpu.CompilerParams(has_side_effects=SIDE_EFFECT))(
        *[pltpu.with_memory_space_constraint(t, pltpu.HBM) for t in shards + outs], after)
    return (res[0], res[1], list(res[2:2 + n]), list(res[2 + n:2 + 2 * n])), res[-1]


def _gather_wait(in_flight, after, name):
    send_sems, recv_sems, shards, outs = in_flight
    n = len(shards)

    def body(*refs):
        for cp in _gather_copies(refs[:n], refs[n:2 * n], refs[2 * n], refs[2 * n + 1]):
            cp.wait_send()
            cp.wait_recv()

    res = pl.pallas_call(
        body, name=name, out_shape=tuple(pltpu.HBM(t.shape, t.dtype) for t in shards + outs),
        in_specs=(*[HBM_ONLY] * (2 * n), SEM_SPEC, SEM_SPEC, pl.BlockSpec(memory_space=pl.ANY)),
        out_specs=(HBM_ONLY,) * (2 * n), input_output_aliases={i: i for i in range(2 * n)},
        compiler_params=pltpu.CompilerParams(has_side_effects=SIDE_EFFECT))(*shards, *outs, send_sems, recv_sems, after)
    return list(res[:n]), list(res[n:])


class _PassToSibling:
    def __init__(self, shards, gathered):
        n = self.n = len(shards)
        self.ins = list(shards) + list(gathered)
        self.out_shape = tuple(jax.ShapeDtypeStruct(g.shape, g.dtype) for g in gathered)
        self.aliases = {n + a: a for a in range(n)}
        self.sems = [pltpu.SemaphoreType.DMA((3 * n,)), pltpu.SemaphoreType.DMA((3 * n,)),
                     pltpu.SemaphoreType.DMA((n,))]

    def _copies(self, ins, outs, sems):
        send_sems, recv_sems, local_sems = sems
        x, y, c = _mesh_pos()
        chips = [(1 - x, y), (x, 1 - y), (1 - x, 1 - y)]
        mine = [pltpu.make_async_copy(ins[a], outs[a].at[_slot((x, y, c))], local_sems.at[a]) for a in range(self.n)]
        passed, awaited = [], []
        for a in range(self.n):
            for j, chip in enumerate(chips):
                sems_j = dict(send_sem=send_sems.at[3 * a + j], recv_sem=recv_sems.at[3 * a + j],
                              device_id=(x, y, 1 - c), device_id_type=MESH)
                blk = outs[a].at[_slot((*chip, c))]
                passed.append(pltpu.make_async_remote_copy(src_ref=blk, dst_ref=blk, **sems_j))
                got = outs[a].at[_slot((*chip, 1 - c))]
                awaited.append(pltpu.make_async_remote_copy(src_ref=got, dst_ref=got, **sems_j))
        return mine, passed, awaited

    def start(self, ins, outs, sems):
        mine, passed, _ = self._copies(ins, outs, sems)
        for cp in mine + passed:
            cp.start()

    def finish(self, ins, outs, sems):
        mine, passed, awaited = self._copies(ins, outs, sems)
        for cp in passed:
            cp.wait_send()
        for cp in awaited:
            cp.wait_recv()
        for cp in mine:
            cp.wait()


def _reduce_sums(fulls, recv_core, core, tag):
    return [_pair_sum(f, r, core, f"rs_pair_{tag}{i}") for i, (f, r) in enumerate(zip(fulls, recv_core))]


def _local_step(x, tgt, mods, w_in_t, shards, small, chip, core, axis_sel):
    sh1, sc1, g1, sh2, sc2, g2 = mods
    norm1_g, rel_bias, gn_g, gn_b, norm2_g, norm_f_g = small
    tables = _ret_tables()
    buckets = jnp.asarray(_bucket_tables())

    h1 = _norm_mod_fwd(x, norm1_g, sh1, sc1, "norm1_fwd")
    flight_w, token_w = _gather_start(list(shards), h1, "gather_w_start")
    proj, slabs = _proj(h1, w_in_t, token_w)
    gated, ro, states = _ret_fwd(proj, tables, gn_g, gn_b)
    bias = _bias_build(rel_bias, buckets)
    outs, lses = [], []
    for gi in range(len(ATT_GROUPS)):
        o, l = _att_fwd(slabs, bias, gi)
        outs.append(o)
        lses.append(l)
    att, gathered = _mix_fwd(outs, lses, comm=_PassToSibling(*_gather_wait(flight_w, lses[2], "gather_w_wait")))
    w_ret_out, w_att_out, w_o, w_ff1, w_ff2 = (_from_slots(g, ax) for g, ax in zip(gathered, BIG_AXES[1:]))
    ret_out = _mm(gated, w_ret_out, 'nn', tm=S, tn=256, tk=2048, name="ret_out")
    att_out, merged = _att_out_merge(att, w_att_out, proj, ret_out)
    mixo, x1, h2 = _w_o_norm2(merged, w_o, x, g1, norm2_g, sh2, sc2)
    u, act = _mm(h2, w_ff1, 'nn', tm=S, tn=512, tk=D, name="ff1", relu2=True)
    loss, dx2, g_normf, df, dg2 = _ff2_final(act, w_ff2, x1, g2, tgt, norm_f_g)

    gw_ff2 = _mm(act, df, 'tn', tm=512, tn=D, tk=S, name="gw_ff2", out_dtype=BF16)
    du = _mm(df, w_ff2, 'nt', tm=S, tn=512, tk=D, name="d_act", out_dtype=BF16, relu2_of=u)
    gw_ff1 = _mm(h2, du, 'tn', tm=D, tn=512, tk=S, name="gw_ff1", out_dtype=BF16)
    fulls_a = [_to_slots(g, ax) for g, ax in zip((gw_ff1, gw_ff2), BIG_AXES[4:])]
    dh2, recv_core_a = _mm(du, w_ff1, 'nt', tm=1024, tn=1024, tk=2048, name="dh2", comm=_ExchangeCore(fulls_a))
    parts_a = _reduce_sums(fulls_a, recv_core_a, core, "a")
    flight_a, token_a = _chip_exchange_start(parts_a, "rs_a_start")
    dx1, dsc2, dsh2, g_norm2, dmixo, dg1 = _norm_mod_bwd(x1, norm2_g, sc2, dh2, dx2, "norm2_bwd", gate=(mixo, g1))

    gw_o = _mm(merged, dmixo, 'tn', tm=D, tn=512, tk=S, name="gw_o", out_dtype=BF16, after=token_a)
    d_ret_out, d_att_out, dga, dgb = _dmerged_split(dmixo, w_o, proj, ret_out, att_out)
    gw_ret_out = _mm(gated, d_ret_out, 'tn', tm=512, tn=D, tk=S, name="gw_ret_out", out_dtype=BF16)
    gw_att_out = _mm(att, d_att_out, 'tn', tm=AW, tn=D, tk=S, name="gw_att_out", out_dtype=BF16)
    fulls_b = [_to_slots(g, ax) for g, ax in zip((gw_ret_out, gw_att_out, gw_o), BIG_AXES[1:4])]
    dgated, recv_core_b = _mm(d_ret_out, w_ret_out, 'nt', tm=S, tn=512, tk=D, name="dgated",
                              comm=_ExchangeCore(fulls_b))
    parts_b = _reduce_sums(fulls_b, recv_core_b, core, "b")
    flight_b, token_b = _chip_exchange_start(parts_b, "rs_b_start")
    datt = _mm(d_att_out, w_att_out, 'nt', tm=S, tn=AW, tk=D, name="datt", after=token_b)
    mix_grads = _mix_bwd(outs, lses, datt)
    datt_parts, ds_sums = [], []
    for gi in range(len(ATT_GROUPS)):
        dq, dk, dv, ds_sum = _att_bwd(slabs, bias, outs[gi], lses[gi], mix_grads[gi], mix_grads[3 + gi], gi)
        datt_parts += [dq.reshape(S, AW), dk.reshape(S, AW), dv.reshape(S, AW)]
        ds_sums.append(ds_sum)
    g_bias = _bias_grad(jnp.concatenate(ds_sums, axis=0), buckets)[:, :, 0].T.reshape(1, -1)
    dret, g_gn_g, g_gn_b = _ret_bwd(proj, tables, gn_g, gn_b, ro, states, dgated)
    parts_a, recv_chip_a = _chip_exchange_wait(flight_a, dret, "rs_a_wait")
    parts_b, recv_chip_b = _chip_exchange_wait(flight_b, dret, "rs_b_wait")
    reduced = list(zip(parts_b + parts_a, recv_chip_b + recv_chip_a))
    dproj = jnp.concatenate([dret] + datt_parts + [dga, dgb], axis=1)
    full_in = _to_slots(_mm(dproj, h1, 'tn', tm=512, tn=D, tk=S, name="gw_in", out_dtype=BF16), 0)
    (recv_core_in,) = _run_comm(_ExchangeCore([full_in]), "rs_core_in")
    part_in = _pair_sum(full_in, recv_core_in, core, "rs_pair_c")
    dh1, (recv_axis1,) = _mm(dproj, w_in_t, 'nn', tm=1024, tn=1024, tk=2560, name="dh1", comm=_ExchangeAxis1(part_in))
    sums_in = _axis_sum(part_in, recv_axis1, axis_sel, "rs_axis_sum")
    (gx, dsc1, dsh1, g_norm1), (last_in,) = _norm_mod_bwd(x, norm1_g, sc1, dh1, dx1, "norm1_bwd",
                                                           comm=_ExchangeAxis2(sums_in))

    dmod = [dsh1, dsc1, dg1, dsh2, dsc2, dg2]
    small_g = [g_norm1, g_bias, g_gn_g, g_gn_b, g_norm2, g_normf]
    return loss, gx, (sums_in, last_in), reduced, small_g, dmod


def _to_slots(g, axis):
    if axis == 0:
        return g.reshape(4, 2, g.shape[0] // N_DEV, g.shape[1])
    return g.reshape(g.shape[0], N_DEV, g.shape[1] // N_DEV).transpose(1, 0, 2).reshape(4, 2, g.shape[0], -1)


def _from_slots(w8, axis):
    if axis == 0:
        return w8.reshape(-1, w8.shape[2])
    return w8.transpose(1, 0, 2).reshape(w8.shape[1], -1)


BIG_AXES = (1, 0, 1, 0, 1, 0)


def kernel(x, c, w_ada, b_ada, norm1_g, w_in, rel_bias, ret_gn_g, ret_gn_b, w_ret_out, w_att_out, w_o, norm2_g, w_ff1, w_ff2, norm_f_g, loss_target, m_w_ada, m_b_ada, m_norm1_g, m_w_in, m_rel_bias, m_ret_gn_g, m_ret_gn_b, m_w_ret_out, m_w_att_out, m_w_o, m_norm2_g, m_w_ff1, m_w_ff2, m_norm_f_g, v_w_ada, v_b_ada, v_norm1_g, v_w_in, v_rel_bias, v_ret_gn_g, v_ret_gn_b, v_w_ret_out, v_w_att_out, v_w_o, v_norm2_g, v_w_ff1, v_w_ff2, v_norm_f_g):
    mx, my, mc = _mesh_pos()
    dev = 4 * mx + 2 * my + mc
    chip = jnp.reshape(2 * mx + my, (1,)).astype(jnp.int32)
    core = jnp.reshape(mc, (1,)).astype(jnp.int32)
    ada_w = D * 6 // N_DEV

    w_in, m_w_in, v_w_in = (jnp.transpose(t, (0, 2, 1)) for t in (w_in, m_w_in, v_w_in))

    shards = [w[0].astype(BF16) for w in (w_in, w_ret_out, w_att_out, w_o, w_ff1, w_ff2)]
    c_all, w_in8 = _run_comm(_Gather([c, shards[0]], relay=True, chunks={1: 4}), "gather_c_w_in")
    c_all = c_all.reshape(N_DEV, D)
    b_sl = lax.dynamic_slice(b_ada, (0, dev * ada_w), (1, ada_w))
    (mod_all,) = _run_comm(_Gather([_ada_fwd(c_all, w_ada[0], b_sl)]), "gather_mod")
    mod = lax.dynamic_index_in_dim(mod_all, dev, axis=1, keepdims=False).reshape(6, D)
    mods = tuple(mod[i:i + 1] for i in range(6))

    small = (norm1_g, rel_bias, ret_gn_g, ret_gn_b, norm2_g, norm_f_g.reshape(1, D))
    axis_sel = jnp.stack([2 * mx + my, 2 * ((mx + mc) % 2) + (my + 1 - mc) % 2]).astype(jnp.int32)
    loss, gx, w_in_sums, big_red, small_g, dmod = _local_step(x[0], loss_target[0], mods, w_in8.reshape(IN_COLS, D),
                                                              shards[1:], small, chip, core, axis_sel)

    gathered = _run_comm(_Gather(dmod + small_g + [loss]), "gather_small")
    g_b_ada, dmod_all, (g_norm1, g_bias, g_gn_g, g_gn_b, g_norm2, g_normf, loss_sum) = _sum_small(gathered)
    loss_out = loss_sum[0, 0]
    g_w_ada = _ada_bwd(c_all, lax.dynamic_slice(dmod_all, (0, dev * ada_w), (N_DEV, ada_w)))

    names = ['w_ada', 'b_ada', 'norm1_g', 'w_in', 'rel_bias', 'ret_gn_g', 'ret_gn_b', 'w_ret_out', 'w_att_out',
             'w_o', 'norm2_g', 'w_ff1', 'w_ff2', 'norm_f_g']
    ws = dict(zip(names, (w_ada, b_ada, norm1_g, w_in, rel_bias, ret_gn_g, ret_gn_b, w_ret_out, w_att_out, w_o,
                          norm2_g, w_ff1, w_ff2, norm_f_g)))
    ms = dict(zip(names, (m_w_ada, m_b_ada, m_norm1_g, m_w_in, m_rel_bias, m_ret_gn_g, m_ret_gn_b, m_w_ret_out,
                          m_w_att_out, m_w_o, m_norm2_g, m_w_ff1, m_w_ff2, m_norm_f_g)))
    vs = dict(zip(names, (v_w_ada, v_b_ada, v_norm1_g, v_w_in, v_rel_bias, v_ret_gn_g, v_ret_gn_b, v_w_ret_out,
                          v_w_att_out, v_w_o, v_norm2_g, v_w_ff1, v_w_ff2, v_norm_f_g)))
    grads = dict(w_ada=g_w_ada, b_ada=g_b_ada, norm1_g=g_norm1, rel_bias=g_bias,
                 ret_gn_g=g_gn_g, ret_gn_b=g_gn_b, norm2_g=g_norm2, norm_f_g=g_normf)
    delta, new_m, new_v = {}, {}, {}
    delta['w_ada'], new_m['w_ada'], new_v['w_ada'] = _adamw(w_ada, g_w_ada, m_w_ada, v_w_ada, "adamw_w_ada")
    grads['w_ada'] = g_w_ada.reshape(w_ada.shape)
    for n, (part, recv) in zip(('w_ret_out', 'w_att_out', 'w_o', 'w_ff1', 'w_ff2'), big_red):
        grads[n], delta[n], new_m[n], new_v[n] = _adamw_reduced1(ws[n], ms[n], vs[n], part, recv, chip, "adamw_" + n)
    small_names = ('b_ada', 'norm1_g', 'rel_bias', 'ret_gn_g', 'ret_gn_b', 'norm2_g', 'norm_f_g')
    two_d = {n: (1, ws[n].size) if ws[n].ndim == 1 else ws[n].shape for n in small_names}
    d_, m_, v_ = _adamw_small(*[[src[n].reshape(two_d[n]) for n in small_names] for src in (ws, grads, ms, vs)])
    for i, n in enumerate(small_names):
        shp = ws[n].shape
        delta[n], new_m[n], new_v[n] = d_[i].reshape(shp), m_[i].reshape(shp), v_[i].reshape(shp)
        grads[n] = grads[n].reshape(shp)

    grads['w_in'], delta['w_in'], new_m['w_in'], new_v['w_in'] = _adamw_sum2(w_in, m_w_in, v_w_in, *w_in_sums,
                                                                            "adamw_w_in")
    for d in (grads, delta, new_m, new_v):
        d['w_in'] = jnp.transpose(d['w_in'], (0, 2, 1))
    return (loss_out, gx[None], *[grads[n] for n in names], *[delta[n] for n in names],
            *[new_m[n] for n in names], *[new_v[n] for n in names])
```

```python
import functools
import math

import numpy as np
import jax
import jax.numpy as jnp
from jax import lax
from jax.experimental import pallas as pl
from jax.experimental.pallas import tpu as pltpu

F32 = jnp.float32
BF16 = jnp.bfloat16
MESH = pl.DeviceIdType.MESH

N_DEV = 8
S = 2048
D = 1024
RET_HEADS = 4
RET_DK = 256
RET_DV = 512
CHUNK = 128
N_CHUNK = S // CHUNK
ATT_GROUPS = ((128, 1), (512, 4), (2048, 16))
ATT_HG = 4
ATT_DH = 128
ATT_BLK = 128
N_BUCKETS = 32
MAX_DIST = 2048
D_FF = 4096
IN_COLS = 12800
OFF_RQ, OFF_RK, OFF_RV, OFF_RG, OFF_ATT = 0, 1024, 2048, 4096, 6144
OFF_GA, OFF_GB = 6144, 7168
RMS_EPS = 1e-6
GN_EPS = 1e-5
ADAM_LR, ADAM_B1, ADAM_B2, ADAM_EPS, ADAM_WD, ADAM_STEP = 0.001, 0.9, 0.999, 1e-08, 0.01, 10
VMEM_LIMIT = 48 * 1024 * 1024


def _pcall(body, **kw):
    return pl.pallas_call(body, **kw)


def _params(sem=None):
    return pltpu.CompilerParams(dimension_semantics=sem, vmem_limit_bytes=VMEM_LIMIT)


HBM_SPEC = pl.BlockSpec(memory_space=pl.ANY)


def _carry(body, comm, *, name, grid, in_specs, out_specs, out_shape, scratch_shapes=()):
    single = not isinstance(out_specs, (tuple, list))
    o_specs = (out_specs,) if single else tuple(out_specs)
    o_shape = (out_shape,) if single else tuple(out_shape)
    n_in, n_out, n_scr = len(in_specs), len(o_specs), len(scratch_shapes)
    nci, nco = len(comm.ins), len(comm.out_shape)
    total = int(np.prod(grid))

    def wrapped(*refs):
        bounds = np.cumsum([0, n_in, nci, n_out, nco, n_scr])
        a, ci, o, co, scr = (refs[bounds[i]:bounds[i + 1]] for i in range(5))
        sems = refs[bounds[5]:]
        flat = 0
        for d, g in enumerate(grid):
            flat = flat * g + pl.program_id(d)

        @pl.when(flat == 0)
        def _():
            comm.start(ci, co, sems)

        body(*a, *o, *scr)

        @pl.when(flat == total - 1)
        def _():
            comm.finish(ci, co, sems)

    aliases = {n_in + i: n_out + o for i, o in getattr(comm, "aliases", {}).items()}
    call = _pcall(wrapped, name=name, grid=grid, in_specs=list(in_specs) + [HBM_SPEC] * nci,
                  out_specs=o_specs + (HBM_SPEC,) * nco, out_shape=o_shape + tuple(comm.out_shape),
                  scratch_shapes=list(scratch_shapes) + list(comm.sems), input_output_aliases=aliases,
                  compiler_params=_params(("arbitrary",) * len(grid)))

    def run(*args):
        res = call(*args, *comm.ins)
        own = res[0] if single else tuple(res[:n_out])
        return own, tuple(res[n_out:])

    return run


def _run_comm(comm, name, after=None):
    nci, nco = len(comm.ins), len(comm.out_shape)
    extra = [] if after is None else [after]

    def body(*refs):
        ci, co, sems = refs[:nci], refs[nci + len(extra):nci + len(extra) + nco], refs[nci + len(extra) + nco:]
        comm.start(ci, co, sems)
        comm.finish(ci, co, sems)

    return _pcall(body, name=name, in_specs=[HBM_SPEC] * (nci + len(extra)), out_specs=(HBM_SPEC,) * nco,
                  out_shape=tuple(comm.out_shape), scratch_shapes=list(comm.sems))(*comm.ins, *extra)


def _dot(a, b, dn):
    return lax.dot_general(a.astype(BF16), b.astype(BF16), (dn, ((), ())), preferred_element_type=F32)


NN = ((1,), (0,))
NT = ((1,), (1,))
TN = ((0,), (0,))


def _mm(a, b, mode, *, tm, tn, tk, name, out_dtype=F32, res=None, gvec=None, relu2=False, relu2_of=None, comm=None,
        after=None):
    if mode == 'nn':
        (M, K), (_, N) = a.shape, b.shape
        a_spec = pl.BlockSpec((tm, tk), lambda i, j, k: (i, k))
        b_spec = pl.BlockSpec((tk, tn), lambda i, j, k: (k, j))
        dn = NN
    elif mode == 'nt':
        (M, K), (N, _) = a.shape, b.shape
        a_spec = pl.BlockSpec((tm, tk), lambda i, j, k: (i, k))
        b_spec = pl.BlockSpec((tn, tk), lambda i, j, k: (j, k))
        dn = NT
    else:
        (K, M), (_, N) = a.shape, b.shape
        a_spec = pl.BlockSpec((tk, tm), lambda i, j, k: (k, i))
        b_spec = pl.BlockSpec((tk, tn), lambda i, j, k: (k, j))
        dn = TN
    assert M % tm == 0 and N % tn == 0 and K % tk == 0, (name, M, N, K)
    nk = K // tk
    fused = res is not None
    o_spec = pl.BlockSpec((tm, tn), lambda i, j, k: (i, j))

    def body(a_ref, b_ref, *rest):
        acc_ref = rest[-1] if nk > 1 else None
        if after is not None:
            rest = rest[1:]
        if fused:
            res_ref, g_ref, o_ref, x_ref = rest[:4]
        elif relu2_of is not None:
            u_ref, o_ref = rest[:2]
        elif relu2:
            o_ref, act_ref = rest[:2]
        else:
            o_ref = rest[0]

        def finish(acc):
            if relu2_of is not None:
                acc = acc * (2.0 * jnp.maximum(u_ref[...], 0.0))
            o_ref[...] = acc.astype(o_ref.dtype)
            if fused:
                x_ref[...] = res_ref[...] + g_ref[...] * acc
            if relu2:
                r = jnp.maximum(acc, 0.0)
                act_ref[...] = (r * r).astype(BF16)

        p = _dot(a_ref[...], b_ref[...], dn)
        if nk == 1:
            finish(p)
        else:
            k = pl.program_id(2)

            @pl.when(k == 0)
            def _():
                acc_ref[...] = p

            @pl.when(k > 0)
            def _():
                acc_ref[...] += p

            @pl.when(k == nk - 1)
            def _():
                finish(acc_ref[...])

    in_specs = [a_spec, b_spec]
    args = [a, b]
    if after is not None:
        in_specs.append(pl.BlockSpec(memory_space=pl.ANY))
        args.append(after)
    out_shape = jax.ShapeDtypeStruct((M, N), out_dtype)
    out_specs = o_spec
    if fused:
        in_specs += [pl.BlockSpec((tm, tn), lambda i, j, k: (i, j)), pl.BlockSpec((1, tn), lambda i, j, k: (0, j))]
        args += [res, gvec]
        out_shape = (out_shape, jax.ShapeDtypeStruct((M, N), F32))
        out_specs = (o_spec, pl.BlockSpec((tm, tn), lambda i, j, k: (i, j)))
    elif relu2_of is not None:
        in_specs.append(pl.BlockSpec((tm, tn), lambda i, j, k: (i, j)))
        args.append(relu2_of)
    elif relu2:
        out_shape = (out_shape, jax.ShapeDtypeStruct((M, N), BF16))
        out_specs = (o_spec, pl.BlockSpec((tm, tn), lambda i, j, k: (i, j)))
    kw = dict(name=name, grid=(M // tm, N // tn, nk), in_specs=in_specs, out_specs=out_specs,
              out_shape=out_shape, scratch_shapes=[pltpu.VMEM((tm, tn), F32)] if nk > 1 else [])
    if comm is not None:
        return _carry(body, comm, **kw)(*args)
    return _pcall(body, compiler_params=_params(("parallel", "parallel", "arbitrary")), **kw)(*args)


PROJ_TN = 512
ATT_T0, ATT_T1 = 6144 // PROJ_TN, 10752 // PROJ_TN
N_SLABS = (ATT_T1 - ATT_T0) * 4
MAIN_COLS = IN_COLS - (ATT_T1 - ATT_T0) * PROJ_TN


PROJ_TILES = IN_COLS // PROJ_TN
SHARD_ROWS = IN_COLS // N_DEV
W_CHUNKS = 4
N_OWN, N_NEAR = 5, 18


def _proj_order():
    out = np.zeros((4, 3, PROJ_TILES), np.int32)
    for q in range(4):
        def hops(t):
            owners = {col // (2 * SHARD_ROWS) for col in (t * PROJ_TN, (t + 1) * PROJ_TN - 1)}
            return max(bin(q ^ p).count("1") for p in owners)
        order = sorted(range(PROJ_TILES), key=lambda t: (hops(t), t))
        assert all(hops(t) == 0 for t in order[:N_OWN]) and all(hops(t) < 2 for t in order[:N_NEAR])
        is_att = [ATT_T0 <= t < ATT_T1 for t in order]
        for row, kind, index in ((1, False, lambda t: t if t < ATT_T0 else t - (ATT_T1 - ATT_T0)),
                                 (2, True, lambda t: t - ATT_T0)):
            own = [index(t) if a == kind else None for t, a in zip(order, is_att)]
            first = next(v for v in own if v is not None)
            last = first
            for j, v in enumerate(own):
                last = last if v is None else v
                out[q, row, j] = last
        out[q, 0] = order
    return out


def _gather_proj(h1, shard, order):
    rows = SHARD_ROWS // W_CHUNKS

    def body(ord_ref, a_ref, sh_ref, main_ref, slab_ref, full_ref, wbuf, fetch_sems, send_sems, recv_sems,
             local_sems):
        j = pl.program_id(0)
        x, y, c = _mesh_pos()
        me, sibling = (x, y, c), (x, y, 1 - c)
        chips = [(1 - x, y), (x, 1 - y), (1 - x, 1 - y)]

        def block(p, owner):
            return full_ref.at[pl.ds(pl.multiple_of(_slot(owner) * SHARD_ROWS + p * rows, 16), rows)]

        def copy(p, k, owner, to, from_input=False):
            dst = block(p, owner)
            return pltpu.make_async_remote_copy(
                src_ref=sh_ref.at[pl.ds(p * rows, rows)] if from_input else dst, dst_ref=dst,
                send_sem=send_sems.at[7 * p + k], recv_sem=recv_sems.at[7 * p + k], device_id=to, device_id_type=MESH)

        pieces = range(W_CHUNKS)
        mine = [pltpu.make_async_copy(sh_ref.at[pl.ds(p * rows, rows)], block(p, me), local_sems.at[p]) for p in pieces]
        first = [copy(p, 0, me, sibling, from_input=True) for p in pieces]
        first += [copy(p, 1 + n, me, (*chips[n], c), from_input=True) for p in pieces for n in range(2)]
        near_pass = [copy(p, 4 + n, (*chips[n], c), sibling) for p in pieces for n in range(2)]
        relay = [copy(p, 3, ((x + 1 - c) % 2, (y + c) % 2, c), ((x + c) % 2, (y + 1 - c) % 2, c)) for p in pieces]
        far_pass = [copy(p, 6, (*chips[2], c), sibling) for p in pieces]

        def fetch(pos):
            slot = lax.rem(pos, 2)
            start = pl.multiple_of(ord_ref[0, pos] * PROJ_TN, PROJ_TN)
            return pltpu.make_async_copy(full_ref.at[pl.ds(start, PROJ_TN)], wbuf.at[slot], fetch_sems.at[slot])

        @pl.when(j == 0)
        def _():
            for cp in mine + first:
                cp.start()
            for cp in mine:
                cp.wait()
            for p in pieces:
                copy(p, 0, sibling, me).wait_recv()
            fetch(j).start()

        @pl.when(j == N_OWN - 1)
        def _():
            for p in pieces:
                for n in range(2):
                    copy(p, 1 + n, (*chips[n], c), me).wait_recv()
                    near_pass[2 * p + n].start()
                relay[p].start()
            for p in pieces:
                for n in range(2):
                    copy(p, 4 + n, (*chips[n], 1 - c), me).wait_recv()

        @pl.when(j == N_NEAR - 1)
        def _():
            for p in pieces:
                copy(p, 3, (*chips[2], c), me).wait_recv()
                far_pass[p].start()
            for p in pieces:
                copy(p, 6, (*chips[2], 1 - c), me).wait_recv()

        @pl.when(j + 1 < PROJ_TILES)
        def _():
            fetch(j + 1).start()

        fetch(j).wait()
        w_ref = wbuf.at[lax.rem(j, 2)]
        tile = ord_ref[0, j]
        is_att = (tile >= ATT_T0) & (tile < ATT_T1)
        chunks = [pl.ds(r * 512, 512) for r in range(S // 512)]

        @pl.when(jnp.logical_not(is_att))
        def _():
            for rws in chunks:
                main_ref[rws, :] = _dot(a_ref[rws, :], w_ref[...], NT)

        @pl.when(is_att)
        def _():
            for rws in chunks:
                p = _dot(a_ref[rws, :], w_ref[...], NT)
                for h in range(4):
                    slab_ref[h, rws, :] = p[:, h * 128:(h + 1) * 128]

        @pl.when(j == PROJ_TILES - 1)
        def _():
            for cp in first + near_pass + relay + far_pass:
                cp.wait_send()

    gs = pltpu.PrefetchScalarGridSpec(
        num_scalar_prefetch=1, grid=(PROJ_TILES,),
        in_specs=[pl.BlockSpec((S, D), lambda j, o: (0, 0)), HBM_SPEC],
        out_specs=(pl.BlockSpec((S, PROJ_TN), lambda j, o: (0, o[1, j])),
                   pl.BlockSpec((4, S, 128), lambda j, o: (o[2, j], 0, 0)), HBM_SPEC),
        scratch_shapes=[pltpu.VMEM((2, PROJ_TN, D), BF16), pltpu.SemaphoreType.DMA((2,)),
                        pltpu.SemaphoreType.DMA((7 * W_CHUNKS,)), pltpu.SemaphoreType.DMA((7 * W_CHUNKS,)),
                        pltpu.SemaphoreType.DMA((W_CHUNKS,))])
    return _pcall(body, name="gather_proj", grid_spec=gs,
                  out_shape=(jax.ShapeDtypeStruct((S, MAIN_COLS), F32), jax.ShapeDtypeStruct((N_SLABS, S, 128), F32),
                             jax.ShapeDtypeStruct((IN_COLS, D), BF16)),
                  compiler_params=_params(("arbitrary",)))(order, h1, shard)


TR = 256


def _row_spec(w=D):
    return pl.BlockSpec((TR, w), lambda i: (i, 0))


def _vec_spec(w=D):
    return pl.BlockSpec((1, w), lambda i: (0, 0))


def _norm_mod_fwd(x, g, sh, sc, name):
    def body(x_ref, g_ref, sh_ref, sc_ref, o_ref):
        xv = x_ref[...]
        rstd = lax.rsqrt(jnp.mean(xv * xv, axis=-1, keepdims=True) + RMS_EPS)
        n = xv * rstd * g_ref[...]
        o_ref[...] = (n * (1.0 + sc_ref[...]) + sh_ref[...]).astype(BF16)

    return _pcall(body, name=name, grid=(S // TR,), in_specs=[_row_spec(), _vec_spec(), _vec_spec(), _vec_spec()],
                  out_specs=_row_spec(), out_shape=jax.ShapeDtypeStruct((S, D), BF16),
                  compiler_params=_params(("parallel",)))(x, g, sh, sc)


def _norm_mod_bwd(x, g, sc, dh, dres, name, gate=None):
    gated = gate is not None

    def body(x_ref, g_ref, sc_ref, dh_ref, dres_ref, *rest):
        if gated:
            f_ref, gv_ref, dx_ref, dsc_ref, dsh_ref, dg_ref, dz_ref, dgv_ref = rest
        else:
            dx_ref, dsc_ref, dsh_ref, dg_ref = rest
        i = pl.program_id(0)
        xv = x_ref[...]
        dh = dh_ref[...]
        rstd = lax.rsqrt(jnp.mean(xv * xv, axis=-1, keepdims=True) + RMS_EPS)
        xhat = xv * rstd
        gv = g_ref[...]
        dn = dh * (1.0 + sc_ref[...])
        dxhat = dn * gv
        dx = dres_ref[...] + rstd * (dxhat - xhat * jnp.mean(dxhat * xhat, axis=-1, keepdims=True))
        dx_ref[...] = dx
        sums = [(dsc_ref, jnp.sum(dh * (xhat * gv), axis=0, keepdims=True)),
                (dsh_ref, jnp.sum(dh, axis=0, keepdims=True)),
                (dg_ref, jnp.sum(dn * xhat, axis=0, keepdims=True))]
        if gated:
            dz_ref[...] = (dx * gv_ref[...]).astype(BF16)
            sums.append((dgv_ref, jnp.sum(dx * f_ref[...], axis=0, keepdims=True)))

        @pl.when(i == 0)
        def _():
            for ref, p in sums:
                ref[...] = p

        @pl.when(i > 0)
        def _():
            for ref, p in sums:
                ref[...] += p

    vec = jax.ShapeDtypeStruct((1, D), F32)
    in_specs = [_row_spec(), _vec_spec(), _vec_spec(), _row_spec(), _row_spec()]
    out_specs = [_row_spec(), _vec_spec(), _vec_spec(), _vec_spec()]
    out_shape = [jax.ShapeDtypeStruct((S, D), F32), vec, vec, vec]
    args = [x, g, sc, dh, dres]
    if gated:
        in_specs += [_row_spec(), _vec_spec()]
        out_specs += [_row_spec(), _vec_spec()]
        out_shape += [jax.ShapeDtypeStruct((S, D), BF16), vec]
        args += list(gate)
    return _pcall(body, name=name, grid=(S // TR,), in_specs=in_specs, out_specs=tuple(out_specs),
                  out_shape=tuple(out_shape), compiler_params=_params(("arbitrary",)))(*args)


def _w_o_norm2(merged, w_o, x, g1, g, sh, sc):
    def body(a_ref, b_ref, x_ref, g1_ref, g_ref, sh_ref, sc_ref, o_ref, x1_ref, h_ref):
        acc = _dot(a_ref[...], b_ref[...], NN)
        o_ref[...] = acc
        xv = x_ref[...] + g1_ref[...] * acc
        x1_ref[...] = xv
        rstd = lax.rsqrt(jnp.mean(xv * xv, axis=-1, keepdims=True) + RMS_EPS)
        h_ref[...] = (xv * rstd * g_ref[...] * (1.0 + sc_ref[...]) + sh_ref[...]).astype(BF16)

    rows = pl.BlockSpec((FF2_TM, D), lambda i: (i, 0))
    f32 = jax.ShapeDtypeStruct((S, D), F32)
    return _pcall(body, name="w_o_norm2", grid=(S // FF2_TM,),
                  in_specs=[rows, pl.BlockSpec((D, D), lambda i: (0, 0)), rows] + [_vec_spec()] * 4,
                  out_specs=(rows, rows, rows), out_shape=(f32, f32, jax.ShapeDtypeStruct((S, D), BF16)),
                  compiler_params=_params(("parallel",)))(merged, w_o, x, g1, g, sh, sc)


FF2_TM = 512


def _ff2_final(act, w_ff2, x1, g2, tgt, g):
    def body(a_ref, b_ref, x1_ref, g2_ref, t_ref, g_ref, loss_ref, dx_ref, dg_ref, df_ref, dg2_ref):
        i = pl.program_id(0)
        f = _dot(a_ref[...], b_ref[...], NN)
        g2v = g2_ref[...]
        xv = x1_ref[...] + g2v * f
        gv = g_ref[...]
        rstd = lax.rsqrt(jnp.mean(xv * xv, axis=-1, keepdims=True) + RMS_EPS)
        xhat = xv * rstd
        err = xhat * gv - t_ref[...]
        dy = err * (1.0 / D)
        dxhat = dy * gv
        dx = rstd * (dxhat - xhat * jnp.mean(dxhat * xhat, axis=-1, keepdims=True))
        dx_ref[...] = dx
        df_ref[...] = (dx * g2v).astype(BF16)
        p_g = jnp.sum(dy * xhat, axis=0, keepdims=True)
        p_g2 = jnp.sum(dx * f, axis=0, keepdims=True)
        p_l = jnp.zeros((1, 128), F32) + 0.5 * jnp.sum(jnp.mean(err * err, axis=-1, keepdims=True))

        @pl.when(i == 0)
        def _():
            dg_ref[...] = p_g
            dg2_ref[...] = p_g2
            loss_ref[...] = p_l

        @pl.when(i > 0)
        def _():
            dg_ref[...] += p_g
            dg2_ref[...] += p_g2
            loss_ref[...] += p_l

    vec = jax.ShapeDtypeStruct((1, D), F32)
    rows = lambda w: pl.BlockSpec((FF2_TM, w), lambda i: (i, 0))
    return _pcall(body, name="ff2_final", grid=(S // FF2_TM,),
                  in_specs=[rows(D_FF), pl.BlockSpec((D_FF, D), lambda i: (0, 0)), rows(D), _vec_spec(), rows(D),
                            _vec_spec()],
                  out_specs=(_vec_spec(128), rows(D), _vec_spec(), rows(D), _vec_spec()),
                  out_shape=(jax.ShapeDtypeStruct((1, 128), F32), jax.ShapeDtypeStruct((S, D), F32), vec,
                             jax.ShapeDtypeStruct((S, D), BF16), vec),
                  compiler_params=_params(("arbitrary",)))(act, w_ff2, x1, g2, tgt, g)


HALF = 512


MERGE_TM = 1024


def _merge_specs():
    blk = lambda off: pl.BlockSpec((MERGE_TM, HALF), lambda i, j: (i, off // HALF + j))
    return blk(OFF_GA), blk(OFF_GB), blk(0)


def _att_out_merge(att, w_att_out, proj, ret_out):
    def body(a_ref, b_ref, ga_ref, gb_ref, r_ref, o_ref, m_ref):
        acc = _dot(a_ref[...], b_ref[...], NN)
        o_ref[...] = acc
        m_ref[...] = (jax.nn.sigmoid(ga_ref[...]) * r_ref[...] + jax.nn.sigmoid(gb_ref[...]) * acc).astype(BF16)

    ga, gb, tile = _merge_specs()
    return _pcall(body, name="att_out", grid=(S // MERGE_TM, D // HALF),
                  in_specs=[pl.BlockSpec((MERGE_TM, AW), lambda i, j: (i, 0)), pl.BlockSpec((AW, HALF), lambda i, j: (0, j)),
                            ga, gb, tile],
                  out_specs=(tile, tile),
                  out_shape=(jax.ShapeDtypeStruct((S, D), F32), jax.ShapeDtypeStruct((S, D), BF16)),
                  compiler_params=_params(("parallel", "parallel")))(att, w_att_out, proj, proj, ret_out)


def _dmerged_split(dmixo, w_o, proj, ret_out, att_out):
    def body(a_ref, b_ref, ga_ref, gb_ref, r_ref, at_ref, dr_ref, da_ref, dga_ref, dgb_ref):
        dm = _dot(a_ref[...], b_ref[...], NT)
        sa = jax.nn.sigmoid(ga_ref[...])
        sb = jax.nn.sigmoid(gb_ref[...])
        dr_ref[...] = (dm * sa).astype(BF16)
        da_ref[...] = (dm * sb).astype(BF16)
        dga_ref[...] = (dm * r_ref[...] * (sa * (1.0 - sa))).astype(BF16)
        dgb_ref[...] = (dm * at_ref[...] * (sb * (1.0 - sb))).astype(BF16)

    ga, gb, tile = _merge_specs()
    o = jax.ShapeDtypeStruct((S, D), BF16)
    return _pcall(body, name="dmerged", grid=(S // MERGE_TM, D // HALF),
                  in_specs=[pl.BlockSpec((MERGE_TM, D), lambda i, j: (i, 0)), pl.BlockSpec((HALF, D), lambda i, j: (j, 0)),
                            ga, gb, tile, tile],
                  out_specs=(tile,) * 4, out_shape=(o, o, o, o),
                  compiler_params=_params(("parallel", "parallel")))(dmixo, w_o, proj, proj, ret_out, att_out)


def _ret_tables():
    H, C = RET_HEADS, CHUNK
    log_g = jnp.log1p(-(2.0 ** (-5.0 - jnp.arange(H, dtype=F32))))
    idx = jnp.arange(C, dtype=F32)
    rel = idx[:, None] - idx[None, :]
    inner = jnp.where(rel >= 0, jnp.exp(log_g[:, None, None] * jnp.maximum(rel, 0.0)), 0.0)
    qd = jnp.exp(log_g[:, None] * (idx + 1.0))[:, :, None]
    kd = jnp.exp(log_g[:, None] * (C - 1.0 - idx))[:, :, None]
    cd = jnp.broadcast_to(jnp.exp(log_g * C)[:, None, None], (H, 1, 128))
    half = RET_DK // 2
    inv = 10000.0 ** (-jnp.arange(half, dtype=F32) / half)
    ang = jnp.arange(S, dtype=F32)[:, None] * inv[None, :]
    return inner, qd, kd, cd, jnp.cos(ang), jnp.sin(ang)


def _rot(x, cos, sin):
    x1, x2 = x[:, :128], x[:, 128:]
    return jnp.concatenate([x1 * cos - x2 * sin, x1 * sin + x2 * cos], axis=1)


def _rot_t(d, cos, sin):
    d1, d2 = d[:, :128], d[:, 128:]
    return jnp.concatenate([d1 * cos + d2 * sin, d2 * cos - d1 * sin], axis=1)


RET_COLS = OFF_ATT
RET_VW = RET_HEADS * RET_DV


def _ret_specs(chunk_of):
    ci = chunk_of
    whole = lambda shape: pl.BlockSpec(shape, lambda t: (0,) * len(shape))
    return [
        pl.BlockSpec((CHUNK, RET_COLS), lambda t: (ci(t), 0)),
        pl.BlockSpec((CHUNK, 128), lambda t: (ci(t), 0)),
        pl.BlockSpec((CHUNK, 128), lambda t: (ci(t), 0)),
        whole((RET_HEADS, CHUNK, CHUNK)), whole((RET_HEADS, CHUNK, 1)), whole((RET_HEADS, CHUNK, 1)),
        whole((RET_HEADS, 1, 128)), whole((1, RET_VW)), whole((1, RET_VW)),
    ]


def _ret_cols(h):
    q = slice(OFF_RQ + h * RET_DK, OFF_RQ + (h + 1) * RET_DK)
    k = slice(OFF_RK + h * RET_DK, OFF_RK + (h + 1) * RET_DK)
    v = slice(OFF_RV + h * RET_DV, OFF_RV + (h + 1) * RET_DV)
    g = slice(OFF_RG + h * RET_DV, OFF_RG + (h + 1) * RET_DV)
    return q, k, v, g, slice(h * RET_DV, (h + 1) * RET_DV)


def _ret_fwd(proj, tables, gn_g, gn_b, after):
    inner, qd, kd, cd, cos, sin = tables

    def body(x_ref, cos_ref, sin_ref, in_ref, qd_ref, kd_ref, cd_ref, g_ref, b_ref, after_ref,
             gated_ref, ro_ref, st_ref, s_scr):
        i = pl.program_id(0)

        @pl.when(i == 0)
        def _():
            s_scr[...] = jnp.zeros_like(s_scr)

        cosv, sinv = cos_ref[...], sin_ref[...]
        for h in range(RET_HEADS):
            cq, ck, cv, cg, co = _ret_cols(h)
            q = _rot(x_ref[:, cq], cosv, sinv)
            k = _rot(x_ref[:, ck], cosv, sinv) * (RET_DK ** -0.5)
            v = x_ref[:, cv]
            st = s_scr[h]
            st_ref[h] = st.astype(BF16)
            s = _dot(q, k, NT) * in_ref[h]
            o = _dot(s, v, NN) + _dot(q, st, NN) * qd_ref[h]
            s_scr[h] = st * cd_ref[h, :, :1] + _dot(k * kd_ref[h], v, TN)
            ro_ref[:, co] = o
            mu = jnp.mean(o, axis=-1, keepdims=True)
            oc = o - mu
            var = jnp.mean(oc * oc, axis=-1, keepdims=True)
            rn = oc * lax.rsqrt(var + GN_EPS) * g_ref[:, co] + b_ref[:, co]
            rg = x_ref[:, cg]
            gated_ref[:, co] = (rg * jax.nn.sigmoid(rg) * rn).astype(BF16)

    ospec = pl.BlockSpec((CHUNK, RET_VW), lambda t: (t, 0))
    return _pcall(
        body, name="ret_fwd", grid=(N_CHUNK,), in_specs=_ret_specs(lambda t: t) + [HBM_SPEC],
        out_specs=(ospec, ospec, pl.BlockSpec((RET_HEADS, None, RET_DK, RET_DV), lambda t: (0, t, 0, 0))),
        out_shape=(jax.ShapeDtypeStruct((S, RET_VW), BF16), jax.ShapeDtypeStruct((S, RET_VW), F32),
                   jax.ShapeDtypeStruct((RET_HEADS, N_CHUNK, RET_DK, RET_DV), BF16)),
        scratch_shapes=[pltpu.VMEM((RET_HEADS, RET_DK, RET_DV), F32)],
        compiler_params=_params(("arbitrary",)))(proj, cos, sin, inner, qd, kd, cd, gn_g, gn_b, after)


def _ret_bwd(proj, tables, gn_g, gn_b, ro, states, dgated, comm=None):
    inner, qd, kd, cd, cos, sin = tables
    last = N_CHUNK - 1

    def body(x_ref, cos_ref, sin_ref, in_ref, qd_ref, kd_ref, cd_ref, g_ref, b_ref, ro_ref, st_ref, dg_ref,
             dx_ref, gg_ref, gb_ref, gs_scr):
        t = pl.program_id(0)

        @pl.when(t == 0)
        def _():
            gs_scr[...] = jnp.zeros_like(gs_scr)
            gg_ref[...] = jnp.zeros_like(gg_ref)
            gb_ref[...] = jnp.zeros_like(gb_ref)

        cosv, sinv = cos_ref[...], sin_ref[...]
        for h in range(RET_HEADS):
            cq, ck, cv, cg, co = _ret_cols(h)
            q = _rot(x_ref[:, cq], cosv, sinv)
            k = _rot(x_ref[:, ck], cosv, sinv) * (RET_DK ** -0.5)
            v = x_ref[:, cv]
            qdv, kdv, dm = qd_ref[h], kd_ref[h], in_ref[h]
            st = st_ref[h]
            o = ro_ref[:, co]
            gv = g_ref[:, co]
            mu = jnp.mean(o, axis=-1, keepdims=True)
            oc = o - mu
            rstd = lax.rsqrt(jnp.mean(oc * oc, axis=-1, keepdims=True) + GN_EPS)
            ohat = oc * rstd
            rn = ohat * gv + b_ref[:, co]
            rg = x_ref[:, cg]
            sg = jax.nn.sigmoid(rg)
            dgt = dg_ref[:, co]
            drn = dgt * (rg * sg)
            dx_ref[:, cg] = (dgt * rn * (sg * (1.0 + rg * (1.0 - sg)))).astype(BF16)
            gg_ref[:, co] += jnp.sum(drn * ohat, axis=0, keepdims=True)
            gb_ref[:, co] += jnp.sum(drn, axis=0, keepdims=True)
            dohat = drn * gv
            do = rstd * (dohat - jnp.mean(dohat, axis=-1, keepdims=True)
                         - ohat * jnp.mean(dohat * ohat, axis=-1, keepdims=True))
            gs = gs_scr[h]
            s = _dot(q, k, NT) * dm
            dsr = _dot(do, v, NT) * dm
            dq = _dot(dsr, k, NN) + _dot(do, st, NT) * qdv
            dk = _dot(dsr, q, TN) + _dot(v, gs, NT) * kdv
            dv = _dot(s, do, TN) + _dot(k * kdv, gs, NN)
            gs_scr[h] = gs * cd_ref[h, :, :1] + _dot(q * qdv, do, TN)
            dx_ref[:, cq] = _rot_t(dq, cosv, sinv).astype(BF16)
            dx_ref[:, ck] = (_rot_t(dk, cosv, sinv) * (RET_DK ** -0.5)).astype(BF16)
            dx_ref[:, cv] = dv.astype(BF16)

    rev = lambda t: last - t
    vblk = pl.BlockSpec((CHUNK, RET_VW), lambda t: (rev(t), 0))
    vspec = pl.BlockSpec((1, RET_VW), lambda t: (0, 0))
    kw = dict(name="ret_bwd", grid=(N_CHUNK,),
              in_specs=_ret_specs(rev) + [vblk, pl.BlockSpec((RET_HEADS, None, RET_DK, RET_DV),
                                                             lambda t: (0, rev(t), 0, 0)), vblk],
              out_specs=(pl.BlockSpec((CHUNK, RET_COLS), lambda t: (rev(t), 0)), vspec, vspec),
              out_shape=(jax.ShapeDtypeStruct((S, RET_COLS), BF16), jax.ShapeDtypeStruct((1, RET_VW), F32),
                         jax.ShapeDtypeStruct((1, RET_VW), F32)),
              scratch_shapes=[pltpu.VMEM((RET_HEADS, RET_DK, RET_DV), F32)])
    args = (proj, cos, sin, inner, qd, kd, cd, gn_g, gn_b, ro, states, dgated)
    if comm is not None:
        return _carry(body, comm, **kw)(*args)
    return _pcall(body, compiler_params=_params(("arbitrary",)), **kw)(*args)


def _bucket_tables():
    qi = np.arange(ATT_BLK)[:, None]
    kj = np.arange(2 * ATT_BLK)[None, :]
    m = ATT_BLK + qi - kj
    out = []
    for win, dil in ATT_GROUPS:
        w = win // dil
        dist = (np.clip(m, 0, w) * dil).astype(np.int32)
        max_exact = N_BUCKETS // 2
        d_f = np.maximum(dist, 1).astype(np.float32)
        large = max_exact + (np.log(d_f / np.float32(max_exact)) / np.float32(math.log(MAX_DIST / max_exact))
                             * np.float32(N_BUCKETS - max_exact)).astype(np.int32)
        large = np.minimum(large, N_BUCKETS - 1)
        out.append(np.where(dist < max_exact, dist, large).astype(np.int32))
    return np.stack(out)


def _bias_build(rel_bias, buckets):
    def body(tab_ref, bk_ref, o_ref):
        hh = pl.program_id(0)
        bk = bk_ref[...]
        acc = jnp.zeros((ATT_BLK, 2 * ATT_BLK), F32)
        for b in range(N_BUCKETS):
            acc = jnp.where(bk == b, tab_ref[b, hh], acc)
        o_ref[...] = acc

    nh = len(ATT_GROUPS) * ATT_HG
    return _pcall(body, name="bias_build", grid=(nh,),
                  in_specs=[pl.BlockSpec(memory_space=pltpu.SMEM),
                            pl.BlockSpec((None, ATT_BLK, 2 * ATT_BLK), lambda hh: (hh // ATT_HG, 0, 0))],
                  out_specs=pl.BlockSpec((None, ATT_BLK, 2 * ATT_BLK), lambda hh: (hh, 0, 0)),
                  out_shape=jax.ShapeDtypeStruct((nh, ATT_BLK, 2 * ATT_BLK), F32),
                  compiler_params=_params(("parallel",)))(rel_bias, buckets)


def _bias_grad(ds_sum, buckets):
    def body(ds_ref, bk_ref, o_ref):
        bk = bk_ref[...]
        ds = ds_ref[...]
        rows = lax.broadcasted_iota(jnp.int32, (N_BUCKETS, 128), 0)
        acc = jnp.zeros((N_BUCKETS, 128), F32)
        for b in range(N_BUCKETS):
            acc = jnp.where(rows == b, jnp.sum(jnp.where(bk == b, ds, 0.0)), acc)
        o_ref[...] = acc

    nh = len(ATT_GROUPS) * ATT_HG
    return _pcall(body, name="bias_grad", grid=(nh,),
                  in_specs=[pl.BlockSpec((None, ATT_BLK, 2 * ATT_BLK), lambda hh: (hh, 0, 0)),
                            pl.BlockSpec((None, ATT_BLK, 2 * ATT_BLK), lambda hh: (hh // ATT_HG, 0, 0))],
                  out_specs=pl.BlockSpec((None, N_BUCKETS, 128), lambda hh: (hh, 0, 0)),
                  out_shape=jax.ShapeDtypeStruct((nh, N_BUCKETS, 128), F32),
                  compiler_params=_params(("parallel",)))(ds_sum, buckets)


def _att_valid(n):
    qi = lax.broadcasted_iota(jnp.int32, (ATT_BLK, 2 * ATT_BLK), 0)
    kj = lax.broadcasted_iota(jnp.int32, (ATT_BLK, 2 * ATT_BLK), 1)
    m = ATT_BLK + qi - kj
    first_key = jnp.where(n > 0, 0, ATT_BLK)
    return (m >= 0) & (m <= ATT_BLK) & (kj >= first_key)


ATT_HP = (1, 2, 2)


def _att_geometry(gi):
    _, dil = ATT_GROUPS[gi]
    return dil, S // dil // ATT_BLK, ATT_HP[gi]


def _blk(dil, r, n):
    if dil == 1:
        return pl.ds(n * ATT_BLK, ATT_BLK)
    return pl.ds(r + n * ATT_BLK * dil, ATT_BLK, stride=dil)


def _slab_specs(gi):
    _, _, hp = _att_geometry(gi)
    per = ATT_HG // hp
    return [pl.BlockSpec((hp, S, ATT_DH), lambda g, r, part=part: ((3 * gi + part) * per + g, 0, 0))
            for part in range(3)]


def _head_specs(gi, count):
    _, _, hp = _att_geometry(gi)
    return [pl.BlockSpec((hp, S, ATT_DH), lambda g, r: (g, 0, 0))] * count


def _bias_spec(gi):
    _, _, hp = _att_geometry(gi)
    return pl.BlockSpec((hp, ATT_BLK, 2 * ATT_BLK), lambda g, r: (gi * (ATT_HG // hp) + g, 0, 0))


def _att_valid_first():
    qi = lax.broadcasted_iota(jnp.int32, (ATT_BLK, ATT_BLK), 0)
    kj = lax.broadcasted_iota(jnp.int32, (ATT_BLK, ATT_BLK), 1)
    return kj <= qi


def _att_fwd(slabs, bias, gi, comm=None):
    dil, nb, hp = _att_geometry(gi)
    scale = ATT_DH ** -0.5

    def body(q_ref, k_ref, v_ref, bias_ref, o_ref, l_ref):
        r = pl.program_id(1)
        for n in range(nb):
            cur = _blk(dil, r, n)
            valid = _att_valid(n) if n > 0 else _att_valid_first()
            for h in range(hp):
                if n > 0:
                    prev = _blk(dil, r, n - 1)
                    kk = jnp.concatenate([k_ref[h, prev, :], k_ref[h, cur, :]], axis=0)
                    vv = jnp.concatenate([v_ref[h, prev, :], v_ref[h, cur, :]], axis=0)
                    bias = bias_ref[h]
                else:
                    kk, vv, bias = k_ref[h, cur, :], v_ref[h, cur, :], bias_ref[h, :, pl.ds(ATT_BLK, ATT_BLK)]
                s = _dot(q_ref[h, cur, :], kk, NT) * scale + bias
                s = jnp.where(valid, s, -1e30)
                mx = jnp.max(s, axis=-1, keepdims=True)
                e = jnp.exp(s - mx)
                den = jnp.sum(e, axis=-1, keepdims=True)
                o_ref[h, cur, :] = _dot(e / den, vv, NN)
                l_ref[h, cur, :] = jnp.broadcast_to(mx + jnp.log(den), (ATT_BLK, ATT_DH))

    osh = jax.ShapeDtypeStruct((ATT_HG, S, ATT_DH), F32)
    kw = dict(name=f"att_fwd{gi}", grid=(ATT_HG // hp, dil), in_specs=_slab_specs(gi) + [_bias_spec(gi)],
              out_specs=tuple(_head_specs(gi, 2)), out_shape=(osh, osh))
    if comm is not None:
        return _carry(body, comm, **kw)(slabs, slabs, slabs, bias)
    return _pcall(body, compiler_params=_params(("parallel", "arbitrary")), **kw)(slabs, slabs, slabs, bias)


def _att_bwd(slabs, bias, o, lse, do, dlse, gi, comm=None):
    dil, nb, hp = _att_geometry(gi)
    per = ATT_HG // hp
    scale = ATT_DH ** -0.5
    wh = hp * ATT_DH
    wide = lambda t: jnp.concatenate([t, t], axis=1)

    def body(q_ref, k_ref, v_ref, bias_ref, o_ref, l_ref, do_ref, dl_ref, dq_ref, dk_ref, dv_ref, ds_ref):
        r = pl.program_id(1)

        @pl.when(r == 0)
        def _():
            ds_ref[...] = jnp.zeros_like(ds_ref)

        for h in range(hp):
            sl = slice(h * ATT_DH, (h + 1) * ATT_DH)
            carry_k = carry_v = None
            for n in range(nb):
                cur = _blk(dil, r, n)
                q = q_ref[h, cur, :]
                dov = do_ref[h, cur, :]
                delta = jnp.sum(dov * o_ref[h, cur, :], axis=-1, keepdims=True)
                out_rows = pl.ds(n * ATT_BLK, ATT_BLK)
                if n == 0:
                    own = pl.ds(ATT_BLK, ATT_BLK)
                    kk, vv = k_ref[h, cur, :], v_ref[h, cur, :]
                    s = _dot(q, kk, NT) * scale + bias_ref[h, :, own]
                    p = jnp.where(_att_valid_first(), jnp.exp(s - l_ref[h, cur, :]), 0.0)
                    ds = p * (_dot(dov, vv, NT) - delta + dl_ref[h, cur, :])
                    ds_ref[h, :, own] += ds
                    dq_ref[out_rows, sl] = (_dot(ds, kk, NN) * scale).astype(BF16)
                    carry_k, carry_v = _dot(ds, q, TN) * scale, _dot(p, dov, TN)
                    continue
                prev = _blk(dil, r, n - 1)
                kk = jnp.concatenate([k_ref[h, prev, :], k_ref[h, cur, :]], axis=0)
                vv = jnp.concatenate([v_ref[h, prev, :], v_ref[h, cur, :]], axis=0)
                s = _dot(q, kk, NT) * scale + bias_ref[h]
                p = jnp.where(_att_valid(n), jnp.exp(s - wide(l_ref[h, cur, :])), 0.0)
                dp = _dot(dov, vv, NT)
                ds = p * (dp - delta + wide(dl_ref[h, cur, :]))
                ds_ref[h] += ds
                dq_ref[out_rows, sl] = (_dot(ds, kk, NN) * scale).astype(BF16)
                dkk = _dot(ds, q, TN) * scale
                dvv = _dot(p, dov, TN)
                before = pl.ds((n - 1) * ATT_BLK, ATT_BLK)
                dk_ref[before, sl] = (carry_k + dkk[:ATT_BLK]).astype(BF16)
                dv_ref[before, sl] = (carry_v + dvv[:ATT_BLK]).astype(BF16)
                carry_k, carry_v = dkk[ATT_BLK:], dvv[ATT_BLK:]
            last = pl.ds((nb - 1) * ATT_BLK, ATT_BLK)
            dk_ref[last, sl] = carry_k.astype(BF16)
            dv_ref[last, sl] = carry_v.astype(BF16)

    out_spec = pl.BlockSpec((S // dil, wh), lambda g, r: (0, r * per + g))
    osh = jax.ShapeDtypeStruct((S // dil, dil * AW), BF16)
    kw = dict(name=f"att_bwd{gi}", grid=(per, dil), in_specs=_slab_specs(gi) + [_bias_spec(gi)] + _head_specs(gi, 4),
              out_specs=(out_spec, out_spec, out_spec,
                         pl.BlockSpec((hp, ATT_BLK, 2 * ATT_BLK), lambda g, r: (g, 0, 0))),
              out_shape=(osh, osh, osh, jax.ShapeDtypeStruct((ATT_HG, ATT_BLK, 2 * ATT_BLK), F32)))
    args = (slabs, slabs, slabs, bias, o, lse, do, dlse)
    if comm is not None:
        return _carry(body, comm, **kw)(*args)
    return _pcall(body, compiler_params=_params(("arbitrary", "arbitrary")), **kw)(*args)


AW = ATT_HG * ATT_DH


def _mix_weights(l0, l1, l2):
    mx = jnp.maximum(jnp.maximum(l0, l1), l2)
    e0, e1, e2 = jnp.exp(l0 - mx), jnp.exp(l1 - mx), jnp.exp(l2 - mx)
    den = e0 + e1 + e2
    return e0 / den, e1 / den, e2 / den


def _heads_spec():
    return pl.BlockSpec((ATT_HG, TR, ATT_DH), lambda i: (0, i, 0))


def _mix_fwd(os_, ls, comm=None):
    def body(o0, o1, o2, l0, l1, l2, att_ref):
        for h in range(ATT_HG):
            w0, w1, w2 = _mix_weights(l0[h], l1[h], l2[h])
            att_ref[:, h * ATT_DH:(h + 1) * ATT_DH] = (w0 * o0[h] + w1 * o1[h] + w2 * o2[h]).astype(BF16)

    kw = dict(name="mix_fwd", grid=(S // TR,), in_specs=[_heads_spec()] * 6, out_specs=_row_spec(AW),
              out_shape=jax.ShapeDtypeStruct((S, AW), BF16))
    if comm is not None:
        return _carry(body, comm, **kw)(*os_, *ls)
    return _pcall(body, compiler_params=_params(("parallel",)), **kw)(*os_, *ls)


def _mix_bwd(os_, ls, datt):
    def body(o0, o1, o2, l0, l1, l2, da_ref, d0, d1, d2, e0, e1, e2):
        for h in range(ATT_HG):
            ws = _mix_weights(l0[h], l1[h], l2[h])
            da = da_ref[:, h * ATT_DH:(h + 1) * ATT_DH]
            dws = []
            for o_ref, w, d_ref in zip((o0, o1, o2), ws, (d0, d1, d2)):
                d_ref[h] = w * da
                dws.append(jnp.broadcast_to(jnp.sum(da * o_ref[h], axis=-1, keepdims=True), (TR, ATT_DH)))
            tot = ws[0] * dws[0] + ws[1] * dws[1] + ws[2] * dws[2]
            for w, dw, e_ref in zip(ws, dws, (e0, e1, e2)):
                e_ref[h] = w * (dw - tot)

    o = jax.ShapeDtypeStruct((ATT_HG, S, ATT_DH), F32)
    return _pcall(body, name="mix_bwd", grid=(S // TR,), in_specs=[_heads_spec()] * 6 + [_row_spec(AW)],
                  out_specs=(_heads_spec(),) * 6, out_shape=(o,) * 6,
                  compiler_params=_params(("parallel",)))(*os_, *ls, datt)


def _ada_fwd(c_all, w_sh, b_sl):
    def body(c_ref, w_ref, b_ref, o_ref):
        cv = c_ref[...]
        o_ref[...] = _dot(cv * jax.nn.sigmoid(cv), w_ref[...], NN) + b_ref[...]

    return _pcall(body, name="ada_fwd", out_shape=jax.ShapeDtypeStruct((N_DEV, w_sh.shape[1]), F32),
                  compiler_params=_params())(c_all, w_sh, b_sl)


def _ada_bwd(c_all, dm_sl):
    def body(c_ref, d_ref, o_ref):
        cv = c_ref[...]
        o_ref[...] = _dot(cv * jax.nn.sigmoid(cv), d_ref[...], TN)

    return _pcall(body, name="ada_bwd", out_shape=jax.ShapeDtypeStruct((D, dm_sl.shape[1]), F32),
                  compiler_params=_params())(c_all, dm_sl)


N_MOD = 6


def _sum_small(gathered):
    n = len(gathered)

    def body(*refs):
        ins, (gb_ref, dm_ref), outs = refs[:n], refs[n:n + 2], refs[n + 2:]

        def total(r):
            acc = r[0]
            for e in range(1, N_DEV):
                acc = acc + r[e]
            return acc

        for i in range(N_MOD):
            cols = slice(i * D, (i + 1) * D)
            gb_ref[:, cols] = total(ins[i])
            for e in range(N_DEV):
                dm_ref[e:e + 1, cols] = ins[i][e]
        for r, o_ref in zip(ins[N_MOD:], outs):
            o_ref[...] = total(r)

    shapes = (jax.ShapeDtypeStruct((1, N_MOD * D), F32), jax.ShapeDtypeStruct((N_DEV, N_MOD * D), F32),
              *[jax.ShapeDtypeStruct(g.shape[1:], F32) for g in gathered[N_MOD:]])
    res = _pcall(body, name="sum_small", out_shape=shapes, compiler_params=_params())(*gathered)
    return res[0], res[1], res[2:]


def _row_tile(m, n):
    t = max(8, min(m, (1 << 19) // n // 8 * 8))
    while m % t:
        t -= 8
    return t


def _pair_sum(full, recv, sel, name, col_block=0):
    _, m, n = recv.shape
    t = _row_tile(m, n)

    def body(sel_ref, a_ref, b_ref, o_ref):
        o_ref[...] = (a_ref[...].astype(F32) + b_ref[...].astype(F32)).astype(o_ref.dtype)

    gs = pltpu.PrefetchScalarGridSpec(
        num_scalar_prefetch=1, grid=(4, m // t),
        in_specs=[pl.BlockSpec((None, None, t, n), lambda q, i, s: (q, s[0], i, col_block)),
                  pl.BlockSpec((None, t, n), lambda q, i, s: (q, i, 0))],
        out_specs=pl.BlockSpec((None, t, n), lambda q, i, s: (q, i, 0)))
    return _pcall(body, name=name, grid_spec=gs, out_shape=jax.ShapeDtypeStruct((4, m, n), full.dtype),
                  compiler_params=_params(("parallel", "parallel")))(sel, full, recv)


def _chip_sum(part, recv, sel, name):
    _, m, n = part.shape
    t = _row_tile(m, n)

    def body(sel_ref, a_ref, r_ref, o_ref):
        o_ref[...] = ((a_ref[...].astype(F32) + r_ref[0].astype(F32)) + r_ref[1].astype(F32)) + r_ref[2].astype(F32)

    gs = pltpu.PrefetchScalarGridSpec(
        num_scalar_prefetch=1, grid=(m // t,),
        in_specs=[pl.BlockSpec((None, t, n), lambda i, s: (s[0], i, 0)),
                  pl.BlockSpec((3, t, n), lambda i, s: (0, i, 0))],
        out_specs=pl.BlockSpec((t, n), lambda i, s: (i, 0)))
    return _pcall(body, name=name, grid_spec=gs, out_shape=jax.ShapeDtypeStruct((m, n), F32),
                  compiler_params=_params(("parallel",)))(sel, part, recv)


def _adamw_math(w, g, m, v):
    nm = ADAM_B1 * m + (1.0 - ADAM_B1) * g
    nv = ADAM_B2 * v + (1.0 - ADAM_B2) * (g * g)
    m_hat = nm / (1.0 - ADAM_B1 ** ADAM_STEP)
    v_hat = nv / (1.0 - ADAM_B2 ** ADAM_STEP)
    return -ADAM_LR * (m_hat / (jnp.sqrt(v_hat) + ADAM_EPS) + ADAM_WD * w), nm, nv


def _adamw(w, g, m, v, name):
    _, rows, cols = w.shape
    t = _row_tile(rows, cols)

    def body(w_ref, g_ref, m_ref, v_ref, d_ref, nm_ref, nv_ref):
        d_ref[...], nm_ref[...], nv_ref[...] = _adamw_math(w_ref[...], g_ref[...], m_ref[...], v_ref[...])

    spec3 = pl.BlockSpec((None, t, cols), lambda i: (0, i, 0))
    spec2 = pl.BlockSpec((t, cols), lambda i: (i, 0))
    o = jax.ShapeDtypeStruct(w.shape, F32)
    return _pcall(body, name=name, grid=(rows // t,), in_specs=[spec3, spec2, spec3, spec3], out_specs=(spec3,) * 3,
                  out_shape=(o, o, o), compiler_params=_params(("parallel",)))(w, g, m, v)


def _adamw_reduced1(w, m, v, part, recv, sel, name):
    _, rows, cols = w.shape
    t = _row_tile(rows, cols)

    def body(sel_ref, w_ref, m_ref, v_ref, p_ref, r_ref, g_ref, d_ref, nm_ref, nv_ref):
        g = ((p_ref[...].astype(F32) + r_ref[0].astype(F32)) + r_ref[1].astype(F32)) + r_ref[2].astype(F32)
        g_ref[...] = g
        d_ref[...], nm_ref[...], nv_ref[...] = _adamw_math(w_ref[...], g, m_ref[...], v_ref[...])

    wspec = pl.BlockSpec((None, t, cols), lambda i, s: (0, i, 0))
    gs = pltpu.PrefetchScalarGridSpec(
        num_scalar_prefetch=1, grid=(rows // t,),
        in_specs=[wspec, wspec, wspec, pl.BlockSpec((None, t, cols), lambda i, s: (s[0], i, 0)),
                  pl.BlockSpec((3, t, cols), lambda i, s: (0, i, 0))],
        out_specs=(wspec,) * 4)
    o = jax.ShapeDtypeStruct(w.shape, F32)
    return _pcall(body, name=name, grid_spec=gs, out_shape=(o, o, o, o),
                  compiler_params=_params(("parallel",)))(sel, w, m, v, part, recv)


def _adamw_reduced(w, m, v, parts, recvs, sel):
    _, rows, cols = w.shape
    half = cols // 2
    t = _row_tile(rows, half)

    def body(sel_ref, w_ref, m_ref, v_ref, pa_ref, pb_ref, ra_ref, rb_ref, g_ref, d_ref, nm_ref, nv_ref):
        total = lambda p_ref, r_ref: ((p_ref[...].astype(F32) + r_ref[0].astype(F32)) + r_ref[1].astype(F32)) \
            + r_ref[2].astype(F32)
        g = jnp.where(pl.program_id(1) == 0, total(pa_ref, ra_ref), total(pb_ref, rb_ref))
        g_ref[...] = g
        d_ref[...], nm_ref[...], nv_ref[...] = _adamw_math(w_ref[...], g, m_ref[...], v_ref[...])

    wspec = pl.BlockSpec((None, t, half), lambda i, j, s: (0, i, j))
    pspec = pl.BlockSpec((None, t, half), lambda i, j, s: (s[0], i, 0))
    rspec = pl.BlockSpec((3, t, half), lambda i, j, s: (0, i, 0))
    gs = pltpu.PrefetchScalarGridSpec(num_scalar_prefetch=1, grid=(rows // t, 2),
                                      in_specs=[wspec, wspec, wspec, pspec, pspec, rspec, rspec],
                                      out_specs=(wspec,) * 4)
    o = jax.ShapeDtypeStruct(w.shape, F32)
    return _pcall(body, name="adamw_w_in", grid_spec=gs, out_shape=(o, o, o, o),
                  compiler_params=_params(("parallel", "arbitrary")))(sel, w, m, v, *parts, *recvs)


def _adamw_small(ws, gs, ms, vs):
    n = len(ws)

    def body(*refs):
        for i in range(n):
            w_ref, g_ref, m_ref, v_ref = (refs[k * n + i] for k in range(4))
            d, nm, nv = _adamw_math(w_ref[...], g_ref[...], m_ref[...], v_ref[...])
            refs[4 * n + i][...] = d
            refs[5 * n + i][...] = nm
            refs[6 * n + i][...] = nv

    shapes = tuple(jax.ShapeDtypeStruct(w.shape, F32) for w in ws)
    res = _pcall(body, name="adamw_small", out_shape=shapes * 3, compiler_params=_params())(*ws, *gs, *ms, *vs)
    return res[:n], res[n:2 * n], res[2 * n:]


def _mesh_pos():
    return lax.axis_index("x"), lax.axis_index("y"), lax.axis_index("c")


class _Gather:
    def __init__(self, arrs):
        self.ins = list(arrs)
        self.out_shape = tuple(jax.ShapeDtypeStruct((N_DEV,) + a.shape, a.dtype) for a in arrs)
        n = len(arrs)
        self.sems = [pltpu.SemaphoreType.DMA((7 * n,)), pltpu.SemaphoreType.DMA((7 * n,)),
                     pltpu.SemaphoreType.DMA((n,))]

    def _copies(self, ins, outs, sems):
        send_sems, recv_sems, local_sems = sems
        x, y, c = _mesh_pos()
        me, sibling = (x, y, c), (x, y, 1 - c)
        chips = [(1 - x, y), (x, 1 - y), (1 - x, 1 - y)]

        def copy(p, k, block, to, from_input=False):
            dst = outs[p].at[_slot(block)]
            return pltpu.make_async_remote_copy(
                src_ref=ins[p] if from_input else dst, dst_ref=dst, send_sem=send_sems.at[7 * p + k],
                recv_sem=recv_sems.at[7 * p + k], device_id=to, device_id_type=MESH)

        npc = len(self.ins)
        mine = [pltpu.make_async_copy(ins[p], outs[p].at[_slot(me)], local_sems.at[p]) for p in range(npc)]
        first = []
        for p in range(npc):
            first.append(copy(p, 0, me, sibling, from_input=True))
            first += [copy(p, 1 + j, me, (*chip, c), from_input=True) for j, chip in enumerate(chips)]
        return me, sibling, chips, c, copy, mine, first

    def start(self, ins, outs, sems):
        *_, mine, first = self._copies(ins, outs, sems)
        for cp in mine + first:
            cp.start()

    def finish(self, ins, outs, sems):
        me, sibling, chips, c, copy, mine, first = self._copies(ins, outs, sems)
        npc = len(self.ins)
        passed = []
        for p in range(npc):
            for j, chip in enumerate(chips):
                copy(p, 1 + j, (*chip, c), me).wait_recv()
                passed.append(copy(p, 4 + j, (*chip, c), sibling))
                passed[-1].start()
        for p in range(npc):
            copy(p, 0, sibling, me).wait_recv()
            for j, chip in enumerate(chips):
                copy(p, 4 + j, (*chip, 1 - c), me).wait_recv()
        for cp in first + passed:
            cp.wait_send()
        for cp in mine:
            cp.wait()


class _ExchangeCore:
    def __init__(self, fulls, cols=None):
        self.ins = list(fulls)
        self.cols = cols
        width = lambda f: f.shape[3] if cols is None else cols[1]
        self.out_shape = tuple(jax.ShapeDtypeStruct((4, f.shape[2], width(f)), f.dtype) for f in fulls)
        self.sems = [pltpu.SemaphoreType.DMA((4 * len(fulls),)), pltpu.SemaphoreType.DMA((4 * len(fulls),))]

    def _copies(self, ins, outs, sems):
        send_sems, recv_sems = sems
        x, y, c = _mesh_pos()

        def src(a, q):
            ref = ins[a].at[q, 1 - c]
            return ref if self.cols is None else ref.at[:, pl.ds(*self.cols)]

        return [pltpu.make_async_remote_copy(
            src_ref=src(a, q), dst_ref=outs[a].at[q], send_sem=send_sems.at[4 * a + q],
            recv_sem=recv_sems.at[4 * a + q], device_id=(x, y, 1 - c), device_id_type=MESH)
            for a in range(len(self.ins)) for q in range(4)]

    def start(self, ins, outs, sems):
        for cp in self._copies(ins, outs, sems):
            cp.start()

    def finish(self, ins, outs, sems):
        for cp in self._copies(ins, outs, sems):
            cp.wait()


class _ExchangeChip:
    def __init__(self, parts):
        self.ins = list(parts)
        self.out_shape = tuple(jax.ShapeDtypeStruct((3,) + p.shape[1:], p.dtype) for p in parts)
        self.sems = [pltpu.SemaphoreType.DMA((3 * len(parts),)), pltpu.SemaphoreType.DMA((3 * len(parts),))]

    def _copies(self, ins, outs, sems):
        send_sems, recv_sems = sems
        x, y, c = _mesh_pos()
        chips = [(1 - x, y), (x, 1 - y), (1 - x, 1 - y)]
        return [pltpu.make_async_remote_copy(
            src_ref=ins[a].at[2 * px + py], dst_ref=outs[a].at[j], send_sem=send_sems.at[3 * a + j],
            recv_sem=recv_sems.at[3 * a + j], device_id=(px, py, c), device_id_type=MESH)
            for a in range(len(self.ins)) for j, (px, py) in enumerate(chips)]

    def start(self, ins, outs, sems):
        for cp in self._copies(ins, outs, sems):
            cp.start()

    def finish(self, ins, outs, sems):
        for cp in self._copies(ins, outs, sems):
            cp.wait()


HBM_ONLY = pl.BlockSpec(memory_space=pltpu.HBM)
SEM_SPEC = pl.BlockSpec(memory_space=pltpu.SEMAPHORE)
SIDE_EFFECT = pltpu.SideEffectType.DATAFLOW_SIDE_EFFECTING


def _chip_copies(p_refs, land_refs, send_sems, recv_sems):
    x, y, c = _mesh_pos()
    return [pltpu.make_async_remote_copy(
        src_ref=p_refs[a].at[2 * px + py], dst_ref=land_refs[a].at[j], send_sem=send_sems.at[3 * a + j],
        recv_sem=recv_sems.at[3 * a + j], device_id=(px, py, c), device_id_type=MESH)
        for a in range(len(p_refs)) for j, (px, py) in enumerate([(1 - x, y), (x, 1 - y), (1 - x, 1 - y)])]


def _chip_exchange_start(parts, name):
    n = len(parts)
    lands = [lax.empty((3,) + p.shape[1:], p.dtype) for p in parts]

    def body(*refs):
        p_refs, land_refs, (send_sems, recv_sems) = refs[:n], refs[n:2 * n], refs[2 * n:2 * n + 2]
        for cp in _chip_copies(p_refs, land_refs, send_sems, recv_sems):
            cp.start()
        token = refs[-1]
        token[...] = jnp.zeros_like(token)

    hbm = lambda t: pltpu.HBM(t.shape, t.dtype)
    res = pl.pallas_call(
        body, name=name,
        out_shape=(pltpu.SemaphoreType.DMA((3 * n,)), pltpu.SemaphoreType.DMA((3 * n,)), *[hbm(t) for t in parts + lands],
                   jax.ShapeDtypeStruct((8, 128), F32)),
        in_specs=(HBM_ONLY,) * (2 * n),
        out_specs=(SEM_SPEC, SEM_SPEC, *[HBM_ONLY] * (2 * n), pl.BlockSpec(memory_space=pltpu.VMEM)),
        input_output_aliases={i: 2 + i for i in range(2 * n)},
        compiler_params=pltpu.CompilerParams(has_side_effects=SIDE_EFFECT))(
        *[pltpu.with_memory_space_constraint(t, pltpu.HBM) for t in parts + lands])
    return (res[0], res[1], list(res[2:2 + n]), list(res[2 + n:2 + 2 * n])), res[-1]


def _chip_exchange_wait(in_flight, after, name):
    send_sems, recv_sems, parts, lands = in_flight
    n = len(parts)

    def body(*refs):
        p_refs, land_refs, (send_sems, recv_sems) = refs[:n], refs[n:2 * n], refs[2 * n:2 * n + 2]
        for cp in _chip_copies(p_refs, land_refs, send_sems, recv_sems):
            cp.wait_send()
            cp.wait_recv()

    res = pl.pallas_call(
        body, name=name, out_shape=tuple(pltpu.HBM(t.shape, t.dtype) for t in parts + lands),
        in_specs=(*[HBM_ONLY] * (2 * n), SEM_SPEC, SEM_SPEC, pl.BlockSpec(memory_space=pl.ANY)),
        out_specs=(HBM_ONLY,) * (2 * n), input_output_aliases={i: i for i in range(2 * n)},
        compiler_params=pltpu.CompilerParams(has_side_effects=SIDE_EFFECT))(*parts, *lands, send_sems, recv_sems, after)
    return list(res[:n]), list(res[n:])


def _slot(p):
    return 4 * p[0] + 2 * p[1] + p[2]


def _gather_copies(src_refs, out_refs, send_sems, recv_sems):
    x, y, c = _mesh_pos()
    targets = [(x, y, 1 - c), (1 - x, y, c), (x, 1 - y, c), (1 - x, 1 - y, c)]
    return [pltpu.make_async_remote_copy(
        src_ref=src_refs[a], dst_ref=out_refs[a].at[_slot((x, y, c))], send_sem=send_sems.at[4 * a + k],
        recv_sem=recv_sems.at[4 * a + k], device_id=to, device_id_type=MESH)
        for a in range(len(src_refs)) for k, to in enumerate(targets)]


def _gather_start(shards, after, name):
    n = len(shards)
    outs = [lax.empty((N_DEV,) + s.shape, s.dtype) for s in shards]

    def body(*refs):
        for cp in _gather_copies(refs[:n], refs[n:2 * n], refs[2 * n + 1], refs[2 * n + 2]):
            cp.start()
        token = refs[-1]
        token[...] = jnp.zeros_like(token)

    res = pl.pallas_call(
        body, name=name,
        out_shape=(pltpu.SemaphoreType.DMA((4 * n,)), pltpu.SemaphoreType.DMA((4 * n,)),
                   *[pltpu.HBM(t.shape, t.dtype) for t in shards + outs], jax.ShapeDtypeStruct((8, 128), F32)),
        in_specs=(*[HBM_ONLY] * (2 * n), pl.BlockSpec(memory_space=pl.ANY)),
        out_specs=(SEM_SPEC, SEM_SPEC, *[HBM_ONLY] * (2 * n), pl.BlockSpec(memory_space=pltpu.VMEM)),
        input_output_aliases={i: 2 + i for i in range(2 * n)},
        compiler_params=pltpu.CompilerParams(has_side_effects=SIDE_EFFECT))(
        *[pltpu.with_memory_space_constraint(t, pltpu.HBM) for t in shards + outs], after)
    return (res[0], res[1], list(res[2:2 + n]), list(res[2 + n:2 + 2 * n])), res[-1]


def _gather_wait(in_flight, after, name):
    send_sems, recv_sems, shards, outs = in_flight
    n = len(shards)

    def body(*refs):
        for cp in _gather_copies(refs[:n], refs[n:2 * n], refs[2 * n], refs[2 * n + 1]):
            cp.wait_send()
            cp.wait_recv()

    res = pl.pallas_call(
        body, name=name, out_shape=tuple(pltpu.HBM(t.shape, t.dtype) for t in shards + outs),
        in_specs=(*[HBM_ONLY] * (2 * n), SEM_SPEC, SEM_SPEC, pl.BlockSpec(memory_space=pl.ANY)),
        out_specs=(HBM_ONLY,) * (2 * n), input_output_aliases={i: i for i in range(2 * n)},
        compiler_params=pltpu.CompilerParams(has_side_effects=SIDE_EFFECT))(*shards, *outs, send_sems, recv_sems, after)
    return list(res[:n]), list(res[n:])


class _PassToSibling:
    def __init__(self, shards, gathered):
        n = self.n = len(shards)
        self.ins = list(shards) + list(gathered)
        self.out_shape = tuple(jax.ShapeDtypeStruct(g.shape, g.dtype) for g in gathered)
        self.aliases = {n + a: a for a in range(n)}
        self.sems = [pltpu.SemaphoreType.DMA((3 * n,)), pltpu.SemaphoreType.DMA((3 * n,)),
                     pltpu.SemaphoreType.DMA((n,))]

    def _copies(self, ins, outs, sems):
        send_sems, recv_sems, local_sems = sems
        x, y, c = _mesh_pos()
        chips = [(1 - x, y), (x, 1 - y), (1 - x, 1 - y)]
        mine = [pltpu.make_async_copy(ins[a], outs[a].at[_slot((x, y, c))], local_sems.at[a]) for a in range(self.n)]
        passed, awaited = [], []
        for a in range(self.n):
            for j, chip in enumerate(chips):
                sems_j = dict(send_sem=send_sems.at[3 * a + j], recv_sem=recv_sems.at[3 * a + j],
                              device_id=(x, y, 1 - c), device_id_type=MESH)
                blk = outs[a].at[_slot((*chip, c))]
                passed.append(pltpu.make_async_remote_copy(src_ref=blk, dst_ref=blk, **sems_j))
                got = outs[a].at[_slot((*chip, 1 - c))]
                awaited.append(pltpu.make_async_remote_copy(src_ref=got, dst_ref=got, **sems_j))
        return mine, passed, awaited

    def start(self, ins, outs, sems):
        mine, passed, _ = self._copies(ins, outs, sems)
        for cp in mine + passed:
            cp.start()

    def finish(self, ins, outs, sems):
        mine, passed, awaited = self._copies(ins, outs, sems)
        for cp in passed:
            cp.wait_send()
        for cp in awaited:
            cp.wait_recv()
        for cp in mine:
            cp.wait()


def _reduce_sums(fulls, recv_core, core, tag):
    return [_pair_sum(f, r, core, f"rs_pair_{tag}{i}") for i, (f, r) in enumerate(zip(fulls, recv_core))]


def _local_step(x, tgt, mods, w_in_shard, order, shards, small, chip, core):
    sh1, sc1, g1, sh2, sc2, g2 = mods
    norm1_g, rel_bias, gn_g, gn_b, norm2_g, norm_f_g = small
    tables = _ret_tables()
    buckets = jnp.asarray(_bucket_tables())

    h1 = _norm_mod_fwd(x, norm1_g, sh1, sc1, "norm1_fwd")
    proj, slabs, w_in_t = _gather_proj(h1, w_in_shard, order)
    flight_w, token_w = _gather_start(list(shards), proj, "gather_w_start")
    gated, ro, states = _ret_fwd(proj, tables, gn_g, gn_b, token_w)
    bias = _bias_build(rel_bias, buckets)
    outs, lses = [], []
    for gi in range(len(ATT_GROUPS)):
        o, l = _att_fwd(slabs, bias, gi)
        outs.append(o)
        lses.append(l)
    att, gathered = _mix_fwd(outs, lses, comm=_PassToSibling(*_gather_wait(flight_w, lses[2], "gather_w_wait")))
    w_ret_out, w_att_out, w_o, w_ff1, w_ff2 = (_from_slots(g, ax) for g, ax in zip(gathered, BIG_AXES[1:]))
    ret_out = _mm(gated, w_ret_out, 'nn', tm=S, tn=256, tk=2048, name="ret_out")
    att_out, merged = _att_out_merge(att, w_att_out, proj, ret_out)
    mixo, x1, h2 = _w_o_norm2(merged, w_o, x, g1, norm2_g, sh2, sc2)
    u, act = _mm(h2, w_ff1, 'nn', tm=S, tn=512, tk=D, name="ff1", relu2=True)
    loss, dx2, g_normf, df, dg2 = _ff2_final(act, w_ff2, x1, g2, tgt, norm_f_g)

    gw_ff2 = _mm(act, df, 'tn', tm=512, tn=D, tk=S, name="gw_ff2", out_dtype=BF16)
    du = _mm(df, w_ff2, 'nt', tm=S, tn=512, tk=D, name="d_act", out_dtype=BF16, relu2_of=u)
    gw_ff1 = _mm(h2, du, 'tn', tm=D, tn=512, tk=S, name="gw_ff1", out_dtype=BF16)
    fulls_a = [_to_slots(g, ax) for g, ax in zip((gw_ff1, gw_ff2), BIG_AXES[4:])]
    dh2, recv_core_a = _mm(du, w_ff1, 'nt', tm=1024, tn=1024, tk=2048, name="dh2", comm=_ExchangeCore(fulls_a))
    parts_a = _reduce_sums(fulls_a, recv_core_a, core, "a")
    flight_a, token_a = _chip_exchange_start(parts_a, "rs_a_start")
    dx1, dsc2, dsh2, g_norm2, dmixo, dg1 = _norm_mod_bwd(x1, norm2_g, sc2, dh2, dx2, "norm2_bwd", gate=(mixo, g1))

    gw_o = _mm(merged, dmixo, 'tn', tm=D, tn=512, tk=S, name="gw_o", out_dtype=BF16, after=token_a)
    d_ret_out, d_att_out, dga, dgb = _dmerged_split(dmixo, w_o, proj, ret_out, att_out)
    gw_ret_out = _mm(gated, d_ret_out, 'tn', tm=512, tn=D, tk=S, name="gw_ret_out", out_dtype=BF16)
    gw_att_out = _mm(att, d_att_out, 'tn', tm=AW, tn=D, tk=S, name="gw_att_out", out_dtype=BF16)
    fulls_b = [_to_slots(g, ax) for g, ax in zip((gw_ret_out, gw_att_out, gw_o), BIG_AXES[1:4])]
    dgated, recv_core_b = _mm(d_ret_out, w_ret_out, 'nt', tm=S, tn=512, tk=D, name="dgated",
                              comm=_ExchangeCore(fulls_b))
    parts_b = _reduce_sums(fulls_b, recv_core_b, core, "b")
    flight_b, token_b = _chip_exchange_start(parts_b, "rs_b_start")
    datt = _mm(d_att_out, w_att_out, 'nt', tm=S, tn=AW, tk=D, name="datt", after=token_b)
    mix_grads = _mix_bwd(outs, lses, datt)
    datt_parts, ds_sums = [], []
    for gi in range(len(ATT_GROUPS)):
        dq, dk, dv, ds_sum = _att_bwd(slabs, bias, outs[gi], lses[gi], mix_grads[gi], mix_grads[3 + gi], gi)
        datt_parts += [dq.reshape(S, AW), dk.reshape(S, AW), dv.reshape(S, AW)]
        ds_sums.append(ds_sum)
    g_bias = _bias_grad(jnp.concatenate(ds_sums, axis=0), buckets)[:, :, 0].T.reshape(1, -1)
    dret, g_gn_g, g_gn_b = _ret_bwd(proj, tables, gn_g, gn_b, ro, states, dgated)
    parts_a, recv_chip_a = _chip_exchange_wait(flight_a, dret, "rs_a_wait")
    parts_b, recv_chip_b = _chip_exchange_wait(flight_b, dret, "rs_b_wait")
    reduced = list(zip(parts_b + parts_a, recv_chip_b + recv_chip_a))
    dproj = jnp.concatenate([dret] + datt_parts + [dga, dgb], axis=1)
    full_in = _to_slots(_mm(dproj, h1, 'tn', tm=512, tn=D, tk=S, name="gw_in", out_dtype=BF16), 0)
    in_flight, token = [], None
    for half in range(2):
        (recv_core_in,) = _run_comm(_ExchangeCore([full_in], cols=(half * (D // 2), D // 2)), f"rs_core_in{half}",
                                    after=token)
        part_in = [_pair_sum(full_in, recv_core_in, core, f"rs_pair_c{half}", col_block=half)]
        flight, token = _chip_exchange_start(part_in, f"rs_in{half}_start")
        in_flight.append(flight)
    dh1 = _mm(dproj, w_in_t, 'nn', tm=1024, tn=1024, tk=2560, name="dh1", after=token)
    gx, dsc1, dsh1, g_norm1 = _norm_mod_bwd(x, norm1_g, sc1, dh1, dx1, "norm1_bwd")

    dmod = [dsh1, dsc1, dg1, dsh2, dsc2, dg2]
    small_g = [g_norm1, g_bias, g_gn_g, g_gn_b, g_norm2, g_normf]
    return loss, gx, in_flight, reduced, small_g, dmod


def _to_slots(g, axis):
    if axis == 0:
        return g.reshape(4, 2, g.shape[0] // N_DEV, g.shape[1])
    return g.reshape(g.shape[0], N_DEV, g.shape[1] // N_DEV).transpose(1, 0, 2).reshape(4, 2, g.shape[0], -1)


def _from_slots(w8, axis):
    if axis == 0:
        return w8.reshape(-1, w8.shape[2])
    return w8.transpose(1, 0, 2).reshape(w8.shape[1], -1)


BIG_AXES = (1, 0, 1, 0, 1, 0)


def kernel(x, c, w_ada, b_ada, norm1_g, w_in, rel_bias, ret_gn_g, ret_gn_b, w_ret_out, w_att_out, w_o, norm2_g, w_ff1, w_ff2, norm_f_g, loss_target, m_w_ada, m_b_ada, m_norm1_g, m_w_in, m_rel_bias, m_ret_gn_g, m_ret_gn_b, m_w_ret_out, m_w_att_out, m_w_o, m_norm2_g, m_w_ff1, m_w_ff2, m_norm_f_g, v_w_ada, v_b_ada, v_norm1_g, v_w_in, v_rel_bias, v_ret_gn_g, v_ret_gn_b, v_w_ret_out, v_w_att_out, v_w_o, v_norm2_g, v_w_ff1, v_w_ff2, v_norm_f_g):
    mx, my, mc = _mesh_pos()
    dev = 4 * mx + 2 * my + mc
    chip = jnp.reshape(2 * mx + my, (1,)).astype(jnp.int32)
    core = jnp.reshape(mc, (1,)).astype(jnp.int32)
    ada_w = D * 6 // N_DEV

    w_in, m_w_in, v_w_in = (jnp.transpose(t, (0, 2, 1)) for t in (w_in, m_w_in, v_w_in))

    shards = [w[0].astype(BF16) for w in (w_in, w_ret_out, w_att_out, w_o, w_ff1, w_ff2)]
    (c_all,) = _run_comm(_Gather([c]), "gather_c")
    c_all = c_all.reshape(N_DEV, D)
    b_sl = lax.dynamic_slice(b_ada, (0, dev * ada_w), (1, ada_w))
    (mod_all,) = _run_comm(_Gather([_ada_fwd(c_all, w_ada[0], b_sl)]), "gather_mod")
    mod = lax.dynamic_index_in_dim(mod_all, dev, axis=1, keepdims=False).reshape(6, D)
    mods = tuple(mod[i:i + 1] for i in range(6))

    small = (norm1_g, rel_bias, ret_gn_g, ret_gn_b, norm2_g, norm_f_g.reshape(1, D))
    order = lax.dynamic_index_in_dim(jnp.asarray(_proj_order()), 2 * mx + my, axis=0, keepdims=False)
    loss, gx, in_flight, big_red, small_g, dmod = _local_step(x[0], loss_target[0], mods, shards[0], order,
                                                              shards[1:], small, chip, core)

    gathered = _run_comm(_Gather(dmod + small_g + [loss]), "gather_small")
    g_b_ada, dmod_all, (g_norm1, g_bias, g_gn_g, g_gn_b, g_norm2, g_normf, loss_sum) = _sum_small(gathered)
    loss_out = loss_sum[0, 0]
    g_w_ada = _ada_bwd(c_all, lax.dynamic_slice(dmod_all, (0, dev * ada_w), (N_DEV, ada_w)))

    names = ['w_ada', 'b_ada', 'norm1_g', 'w_in', 'rel_bias', 'ret_gn_g', 'ret_gn_b', 'w_ret_out', 'w_att_out',
             'w_o', 'norm2_g', 'w_ff1', 'w_ff2', 'norm_f_g']
    ws = dict(zip(names, (w_ada, b_ada, norm1_g, w_in, rel_bias, ret_gn_g, ret_gn_b, w_ret_out, w_att_out, w_o,
                          norm2_g, w_ff1, w_ff2, norm_f_g)))
    ms = dict(zip(names, (m_w_ada, m_b_ada, m_norm1_g, m_w_in, m_rel_bias, m_ret_gn_g, m_ret_gn_b, m_w_ret_out,
                          m_w_att_out, m_w_o, m_norm2_g, m_w_ff1, m_w_ff2, m_norm_f_g)))
    vs = dict(zip(names, (v_w_ada, v_b_ada, v_norm1_g, v_w_in, v_rel_bias, v_ret_gn_g, v_ret_gn_b, v_w_ret_out,
                          v_w_att_out, v_w_o, v_norm2_g, v_w_ff1, v_w_ff2, v_norm_f_g)))
    grads = dict(w_ada=g_w_ada, b_ada=g_b_ada, norm1_g=g_norm1, rel_bias=g_bias,
                 ret_gn_g=g_gn_g, ret_gn_b=g_gn_b, norm2_g=g_norm2, norm_f_g=g_normf)
    delta, new_m, new_v = {}, {}, {}
    delta['w_ada'], new_m['w_ada'], new_v['w_ada'] = _adamw(w_ada, g_w_ada, m_w_ada, v_w_ada, "adamw_w_ada")
    grads['w_ada'] = g_w_ada.reshape(w_ada.shape)
    for n, (part, recv) in zip(('w_ret_out', 'w_att_out', 'w_o', 'w_ff1', 'w_ff2'), big_red):
        grads[n], delta[n], new_m[n], new_v[n] = _adamw_reduced1(ws[n], ms[n], vs[n], part, recv, chip, "adamw_" + n)
    small_names = ('b_ada', 'norm1_g', 'rel_bias', 'ret_gn_g', 'ret_gn_b', 'norm2_g', 'norm_f_g')
    two_d = {n: (1, ws[n].size) if ws[n].ndim == 1 else ws[n].shape for n in small_names}
    d_, m_, v_ = _adamw_small(*[[src[n].reshape(two_d[n]) for n in small_names] for src in (ws, grads, ms, vs)])
    for i, n in enumerate(small_names):
        shp = ws[n].shape
        delta[n], new_m[n], new_v[n] = d_[i].reshape(shp), m_[i].reshape(shp), v_[i].reshape(shp)
        grads[n] = grads[n].reshape(shp)

    done = lax.optimization_barrier((gx, tuple(d_), tuple(delta[n] for n in ('w_ada', 'w_ret_out', 'w_att_out', 'w_o',
                                                                               'w_ff1', 'w_ff2'))))
    parts_in, recvs_in = [], []
    for half, flight in enumerate(in_flight):
        (part_in,), (recv_chip_in,) = _chip_exchange_wait(flight, done[0], f"rs_in{half}_wait")
        parts_in.append(part_in)
        recvs_in.append(recv_chip_in)
    grads['w_in'], delta['w_in'], new_m['w_in'], new_v['w_in'] = _adamw_reduced(w_in, m_w_in, v_w_in, parts_in,
                                                                               recvs_in, chip)
    for d in (grads, delta, new_m, new_v):
        d['w_in'] = jnp.transpose(d['w_in'], (0, 2, 1))
    return (loss_out, gx[None], *[grads[n] for n in names], *[delta[n] for n in names],
            *[new_m[n] for n in names], *[new_v[n] for n in names])
```

```python
import functools
import math

import numpy as np
import jax
import jax.numpy as jnp
from jax import lax
from jax.experimental import pallas as pl
from jax.experimental.pallas import tpu as pltpu

F32 = jnp.float32
BF16 = jnp.bfloat16
MESH = pl.DeviceIdType.MESH

N_DEV = 8
S = 2048
D = 1024
RET_HEADS = 4
RET_DK = 256
RET_DV = 512
CHUNK = 128
N_CHUNK = S // CHUNK
ATT_GROUPS = ((128, 1), (512, 4), (2048, 16))
ATT_HG = 4
ATT_DH = 128
ATT_BLK = 128
N_BUCKETS = 32
MAX_DIST = 2048
D_FF = 4096
IN_COLS = 12800
OFF_RQ, OFF_RK, OFF_RV, OFF_RG, OFF_ATT = 0, 1024, 2048, 4096, 6144
OFF_GA, OFF_GB = 6144, 7168
RMS_EPS = 1e-6
GN_EPS = 1e-5
ADAM_LR, ADAM_B1, ADAM_B2, ADAM_EPS, ADAM_WD, ADAM_STEP = 0.001, 0.9, 0.999, 1e-08, 0.01, 10
VMEM_LIMIT = 48 * 1024 * 1024


def _pcall(body, **kw):
    return pl.pallas_call(body, **kw)


def _params(sem=None):
    return pltpu.CompilerParams(dimension_semantics=sem, vmem_limit_bytes=VMEM_LIMIT)


HBM_SPEC = pl.BlockSpec(memory_space=pl.ANY)


def _carry(body, comm, *, name, grid, in_specs, out_specs, out_shape, scratch_shapes=()):
    single = not isinstance(out_specs, (tuple, list))
    o_specs = (out_specs,) if single else tuple(out_specs)
    o_shape = (out_shape,) if single else tuple(out_shape)
    n_in, n_out, n_scr = len(in_specs), len(o_specs), len(scratch_shapes)
    nci, nco = len(comm.ins), len(comm.out_shape)
    total = int(np.prod(grid))

    def wrapped(*refs):
        bounds = np.cumsum([0, n_in, nci, n_out, nco, n_scr])
        a, ci, o, co, scr = (refs[bounds[i]:bounds[i + 1]] for i in range(5))
        sems = refs[bounds[5]:]
        flat = 0
        for d, g in enumerate(grid):
            flat = flat * g + pl.program_id(d)

        @pl.when(flat == 0)
        def _():
            comm.start(ci, co, sems)

        body(*a, *o, *scr)

        @pl.when(flat == total - 1)
        def _():
            comm.finish(ci, co, sems)

    aliases = {n_in + i: n_out + o for i, o in getattr(comm, "aliases", {}).items()}
    call = _pcall(wrapped, name=name, grid=grid, in_specs=list(in_specs) + [HBM_SPEC] * nci,
                  out_specs=o_specs + (HBM_SPEC,) * nco, out_shape=o_shape + tuple(comm.out_shape),
                  scratch_shapes=list(scratch_shapes) + list(comm.sems), input_output_aliases=aliases,
                  compiler_params=_params(("arbitrary",) * len(grid)))

    def run(*args):
        res = call(*args, *comm.ins)
        own = res[0] if single else tuple(res[:n_out])
        return own, tuple(res[n_out:])

    return run


def _run_comm(comm, name, after=None):
    nci, nco = len(comm.ins), len(comm.out_shape)
    extra = [] if after is None else [after]

    def body(*refs):
        ci, co, sems = refs[:nci], refs[nci + len(extra):nci + len(extra) + nco], refs[nci + len(extra) + nco:]
        comm.start(ci, co, sems)
        comm.finish(ci, co, sems)

    return _pcall(body, name=name, in_specs=[HBM_SPEC] * (nci + len(extra)), out_specs=(HBM_SPEC,) * nco,
                  out_shape=tuple(comm.out_shape), scratch_shapes=list(comm.sems))(*comm.ins, *extra)


def _dot(a, b, dn):
    return lax.dot_general(a.astype(BF16), b.astype(BF16), (dn, ((), ())), preferred_element_type=F32)


NN = ((1,), (0,))
NT = ((1,), (1,))
TN = ((0,), (0,))


def _mm(a, b, mode, *, tm, tn, tk, name, out_dtype=F32, res=None, gvec=None, relu2=False, relu2_of=None, comm=None,
        after=None):
    if mode == 'nn':
        (M, K), (_, N) = a.shape, b.shape
        a_spec = pl.BlockSpec((tm, tk), lambda i, j, k: (i, k))
        b_spec = pl.BlockSpec((tk, tn), lambda i, j, k: (k, j))
        dn = NN
    elif mode == 'nt':
        (M, K), (N, _) = a.shape, b.shape
        a_spec = pl.BlockSpec((tm, tk), lambda i, j, k: (i, k))
        b_spec = pl.BlockSpec((tn, tk), lambda i, j, k: (j, k))
        dn = NT
    else:
        (K, M), (_, N) = a.shape, b.shape
        a_spec = pl.BlockSpec((tk, tm), lambda i, j, k: (k, i))
        b_spec = pl.BlockSpec((tk, tn), lambda i, j, k: (k, j))
        dn = TN
    assert M % tm == 0 and N % tn == 0 and K % tk == 0, (name, M, N, K)
    nk = K // tk
    fused = res is not None
    o_spec = pl.BlockSpec((tm, tn), lambda i, j, k: (i, j))

    def body(a_ref, b_ref, *rest):
        acc_ref = rest[-1] if nk > 1 else None
        if after is not None:
            rest = rest[1:]
        if fused:
            res_ref, g_ref, o_ref, x_ref = rest[:4]
        elif relu2_of is not None:
            u_ref, o_ref = rest[:2]
        elif relu2:
            o_ref, act_ref = rest[:2]
        else:
            o_ref = rest[0]

        def finish(acc):
            if relu2_of is not None:
                acc = acc * (2.0 * jnp.maximum(u_ref[...], 0.0))
            o_ref[...] = acc.astype(o_ref.dtype)
            if fused:
                x_ref[...] = res_ref[...] + g_ref[...] * acc
            if relu2:
                r = jnp.maximum(acc, 0.0)
                act_ref[...] = (r * r).astype(BF16)

        p = _dot(a_ref[...], b_ref[...], dn)
        if nk == 1:
            finish(p)
        else:
            k = pl.program_id(2)

            @pl.when(k == 0)
            def _():
                acc_ref[...] = p

            @pl.when(k > 0)
            def _():
                acc_ref[...] += p

            @pl.when(k == nk - 1)
            def _():
                finish(acc_ref[...])

    in_specs = [a_spec, b_spec]
    args = [a, b]
    if after is not None:
        in_specs.append(pl.BlockSpec(memory_space=pl.ANY))
        args.append(after)
    out_shape = jax.ShapeDtypeStruct((M, N), out_dtype)
    out_specs = o_spec
    if fused:
        in_specs += [pl.BlockSpec((tm, tn), lambda i, j, k: (i, j)), pl.BlockSpec((1, tn), lambda i, j, k: (0, j))]
        args += [res, gvec]
        out_shape = (out_shape, jax.ShapeDtypeStruct((M, N), F32))
        out_specs = (o_spec, pl.BlockSpec((tm, tn), lambda i, j, k: (i, j)))
    elif relu2_of is not None:
        in_specs.append(pl.BlockSpec((tm, tn), lambda i, j, k: (i, j)))
        args.append(relu2_of)
    elif relu2:
        out_shape = (out_shape, jax.ShapeDtypeStruct((M, N), BF16))
        out_specs = (o_spec, pl.BlockSpec((tm, tn), lambda i, j, k: (i, j)))
    kw = dict(name=name, grid=(M // tm, N // tn, nk), in_specs=in_specs, out_specs=out_specs,
              out_shape=out_shape, scratch_shapes=[pltpu.VMEM((tm, tn), F32)] if nk > 1 else [])
    if comm is not None:
        return _carry(body, comm, **kw)(*args)
    return _pcall(body, compiler_params=_params(("parallel", "parallel", "arbitrary")), **kw)(*args)


PROJ_TN = 512
ATT_T0, ATT_T1 = 6144 // PROJ_TN, 10752 // PROJ_TN
N_SLABS = (ATT_T1 - ATT_T0) * 4
MAIN_COLS = IN_COLS - (ATT_T1 - ATT_T0) * PROJ_TN


PROJ_TILES = IN_COLS // PROJ_TN
SHARD_ROWS = IN_COLS // N_DEV
W_CHUNKS = 4
N_OWN, N_NEAR = 5, 18


def _proj_order():
    out = np.zeros((4, 3, PROJ_TILES), np.int32)
    for q in range(4):
        def hops(t):
            owners = {col // (2 * SHARD_ROWS) for col in (t * PROJ_TN, (t + 1) * PROJ_TN - 1)}
            return max(bin(q ^ p).count("1") for p in owners)
        order = sorted(range(PROJ_TILES), key=lambda t: (hops(t), t))
        assert all(hops(t) == 0 for t in order[:N_OWN]) and all(hops(t) < 2 for t in order[:N_NEAR])
        is_att = [ATT_T0 <= t < ATT_T1 for t in order]
        for row, kind, index in ((1, False, lambda t: t if t < ATT_T0 else t - (ATT_T1 - ATT_T0)),
                                 (2, True, lambda t: t - ATT_T0)):
            own = [index(t) if a == kind else None for t, a in zip(order, is_att)]
            first = next(v for v in own if v is not None)
            last = first
            for j, v in enumerate(own):
                last = last if v is None else v
                out[q, row, j] = last
        out[q, 0] = order
    return out


def _gather_proj(h1, shard, order):
    rows = SHARD_ROWS // W_CHUNKS

    def body(ord_ref, a_ref, sh_ref, main_ref, slab_ref, full_ref, wbuf, fetch_sems, send_sems, recv_sems,
             local_sems):
        j = pl.program_id(0)
        x, y, c = _mesh_pos()
        me, sibling = (x, y, c), (x, y, 1 - c)
        chips = [(1 - x, y), (x, 1 - y), (1 - x, 1 - y)]

        def block(p, owner):
            return full_ref.at[pl.ds(pl.multiple_of(_slot(owner) * SHARD_ROWS + p * rows, 16), rows)]

        def copy(p, k, owner, to, from_input=False):
            dst = block(p, owner)
            return pltpu.make_async_remote_copy(
                src_ref=sh_ref.at[pl.ds(p * rows, rows)] if from_input else dst, dst_ref=dst,
                send_sem=send_sems.at[7 * p + k], recv_sem=recv_sems.at[7 * p + k], device_id=to, device_id_type=MESH)

        pieces = range(W_CHUNKS)
        mine = [pltpu.make_async_copy(sh_ref.at[pl.ds(p * rows, rows)], block(p, me), local_sems.at[p]) for p in pieces]
        first = [copy(p, 0, me, sibling, from_input=True) for p in pieces]
        first += [copy(p, 1 + n, me, (*chips[n], c), from_input=True) for p in pieces for n in range(2)]
        near_pass = [copy(p, 4 + n, (*chips[n], c), sibling) for p in pieces for n in range(2)]
        relay = [copy(p, 3, ((x + 1 - c) % 2, (y + c) % 2, c), ((x + c) % 2, (y + 1 - c) % 2, c)) for p in pieces]
        far_pass = [copy(p, 6, (*chips[2], c), sibling) for p in pieces]

        def fetch(pos):
            slot = lax.rem(pos, 2)
            start = pl.multiple_of(ord_ref[0, pos] * PROJ_TN, PROJ_TN)
            return pltpu.make_async_copy(full_ref.at[pl.ds(start, PROJ_TN)], wbuf.at[slot], fetch_sems.at[slot])

        @pl.when(j == 0)
        def _():
            for cp in mine + first:
                cp.start()
            for cp in mine:
                cp.wait()
            for p in pieces:
                copy(p, 0, sibling, me).wait_recv()
            fetch(j).start()

        @pl.when(j == N_OWN - 1)
        def _():
            for p in pieces:
                for n in range(2):
                    copy(p, 1 + n, (*chips[n], c), me).wait_recv()
                    near_pass[2 * p + n].start()
                relay[p].start()
            for p in pieces:
                for n in range(2):
                    copy(p, 4 + n, (*chips[n], 1 - c), me).wait_recv()

        @pl.when(j == N_NEAR - 1)
        def _():
            for p in pieces:
                copy(p, 3, (*chips[2], c), me).wait_recv()
                far_pass[p].start()
            for p in pieces:
                copy(p, 6, (*chips[2], 1 - c), me).wait_recv()

        @pl.when(j + 1 < PROJ_TILES)
        def _():
            fetch(j + 1).start()

        fetch(j).wait()
        w_ref = wbuf.at[lax.rem(j, 2)]
        tile = ord_ref[0, j]
        is_att = (tile >= ATT_T0) & (tile < ATT_T1)
        chunks = [pl.ds(r * 512, 512) for r in range(S // 512)]

        @pl.when(jnp.logical_not(is_att))
        def _():
            for rws in chunks:
                main_ref[rws, :] = _dot(a_ref[rws, :], w_ref[...], NT)

        @pl.when(is_att)
        def _():
            for rws in chunks:
                p = _dot(a_ref[rws, :], w_ref[...], NT)
                for h in range(4):
                    slab_ref[h, rws, :] = p[:, h * 128:(h + 1) * 128]

        @pl.when(j == PROJ_TILES - 1)
        def _():
            for cp in first + near_pass + relay + far_pass:
                cp.wait_send()

    gs = pltpu.PrefetchScalarGridSpec(
        num_scalar_prefetch=1, grid=(PROJ_TILES,),
        in_specs=[pl.BlockSpec((S, D), lambda j, o: (0, 0)), HBM_SPEC],
        out_specs=(pl.BlockSpec((S, PROJ_TN), lambda j, o: (0, o[1, j])),
                   pl.BlockSpec((4, S, 128), lambda j, o: (o[2, j], 0, 0)), HBM_SPEC),
        scratch_shapes=[pltpu.VMEM((2, PROJ_TN, D), BF16), pltpu.SemaphoreType.DMA((2,)),
                        pltpu.SemaphoreType.DMA((7 * W_CHUNKS,)), pltpu.SemaphoreType.DMA((7 * W_CHUNKS,)),
                        pltpu.SemaphoreType.DMA((W_CHUNKS,))])
    return _pcall(body, name="gather_proj", grid_spec=gs,
                  out_shape=(jax.ShapeDtypeStruct((S, MAIN_COLS), F32), jax.ShapeDtypeStruct((N_SLABS, S, 128), F32),
                             jax.ShapeDtypeStruct((IN_COLS, D), BF16)),
                  compiler_params=_params(("arbitrary",)))(order, h1, shard)


TR = 256


def _row_spec(w=D):
    return pl.BlockSpec((TR, w), lambda i: (i, 0))


def _vec_spec(w=D):
    return pl.BlockSpec((1, w), lambda i: (0, 0))


def _norm_mod_fwd(x, g, sh, sc, name):
    def body(x_ref, g_ref, sh_ref, sc_ref, o_ref):
        xv = x_ref[...]
        rstd = lax.rsqrt(jnp.mean(xv * xv, axis=-1, keepdims=True) + RMS_EPS)
        n = xv * rstd * g_ref[...]
        o_ref[...] = (n * (1.0 + sc_ref[...]) + sh_ref[...]).astype(BF16)

    return _pcall(body, name=name, grid=(S // TR,), in_specs=[_row_spec(), _vec_spec(), _vec_spec(), _vec_spec()],
                  out_specs=_row_spec(), out_shape=jax.ShapeDtypeStruct((S, D), BF16),
                  compiler_params=_params(("parallel",)))(x, g, sh, sc)


def _norm_mod_bwd(x, g, sc, dh, dres, name, gate=None):
    gated = gate is not None

    def body(x_ref, g_ref, sc_ref, dh_ref, dres_ref, *rest):
        if gated:
            f_ref, gv_ref, dx_ref, dsc_ref, dsh_ref, dg_ref, dz_ref, dgv_ref = rest
        else:
            dx_ref, dsc_ref, dsh_ref, dg_ref = rest
        i = pl.program_id(0)
        xv = x_ref[...]
        dh = dh_ref[...]
        rstd = lax.rsqrt(jnp.mean(xv * xv, axis=-1, keepdims=True) + RMS_EPS)
        xhat = xv * rstd
        gv = g_ref[...]
        dn = dh * (1.0 + sc_ref[...])
        dxhat = dn * gv
        dx = dres_ref[...] + rstd * (dxhat - xhat * jnp.mean(dxhat * xhat, axis=-1, keepdims=True))
        dx_ref[...] = dx
        sums = [(dsc_ref, jnp.sum(dh * (xhat * gv), axis=0, keepdims=True)),
                (dsh_ref, jnp.sum(dh, axis=0, keepdims=True)),
                (dg_ref, jnp.sum(dn * xhat, axis=0, keepdims=True))]
        if gated:
            dz_ref[...] = (dx * gv_ref[...]).astype(BF16)
            sums.append((dgv_ref, jnp.sum(dx * f_ref[...], axis=0, keepdims=True)))

        @pl.when(i == 0)
        def _():
            for ref, p in sums:
                ref[...] = p

        @pl.when(i > 0)
        def _():
            for ref, p in sums:
                ref[...] += p

    vec = jax.ShapeDtypeStruct((1, D), F32)
    in_specs = [_row_spec(), _vec_spec(), _vec_spec(), _row_spec(), _row_spec()]
    out_specs = [_row_spec(), _vec_spec(), _vec_spec(), _vec_spec()]
    out_shape = [jax.ShapeDtypeStruct((S, D), F32), vec, vec, vec]
    args = [x, g, sc, dh, dres]
    if gated:
        in_specs += [_row_spec(), _vec_spec()]
        out_specs += [_row_spec(), _vec_spec()]
        out_shape += [jax.ShapeDtypeStruct((S, D), BF16), vec]
        args += list(gate)
    return _pcall(body, name=name, grid=(S // TR,), in_specs=in_specs, out_specs=tuple(out_specs),
                  out_shape=tuple(out_shape), compiler_params=_params(("arbitrary",)))(*args)


def _w_o_norm2(merged, w_o, x, g1, g, sh, sc):
    def body(a_ref, b_ref, x_ref, g1_ref, g_ref, sh_ref, sc_ref, o_ref, x1_ref, h_ref):
        acc = _dot(a_ref[...], b_ref[...], NN)
        o_ref[...] = acc
        xv = x_ref[...] + g1_ref[...] * acc
        x1_ref[...] = xv
        rstd = lax.rsqrt(jnp.mean(xv * xv, axis=-1, keepdims=True) + RMS_EPS)
        h_ref[...] = (xv * rstd * g_ref[...] * (1.0 + sc_ref[...]) + sh_ref[...]).astype(BF16)

    rows = pl.BlockSpec((FF2_TM, D), lambda i: (i, 0))
    f32 = jax.ShapeDtypeStruct((S, D), F32)
    return _pcall(body, name="w_o_norm2", grid=(S // FF2_TM,),
                  in_specs=[rows, pl.BlockSpec((D, D), lambda i: (0, 0)), rows] + [_vec_spec()] * 4,
                  out_specs=(rows, rows, rows), out_shape=(f32, f32, jax.ShapeDtypeStruct((S, D), BF16)),
                  compiler_params=_params(("parallel",)))(merged, w_o, x, g1, g, sh, sc)


FF2_TM = 512


def _ff2_final(act, w_ff2, x1, g2, tgt, g):
    def body(a_ref, b_ref, x1_ref, g2_ref, t_ref, g_ref, loss_ref, dx_ref, dg_ref, df_ref, dg2_ref):
        i = pl.program_id(0)
        f = _dot(a_ref[...], b_ref[...], NN)
        g2v = g2_ref[...]
        xv = x1_ref[...] + g2v * f
        gv = g_ref[...]
        rstd = lax.rsqrt(jnp.mean(xv * xv, axis=-1, keepdims=True) + RMS_EPS)
        xhat = xv * rstd
        err = xhat * gv - t_ref[...]
        dy = err * (1.0 / D)
        dxhat = dy * gv
        dx = rstd * (dxhat - xhat * jnp.mean(dxhat * xhat, axis=-1, keepdims=True))
        dx_ref[...] = dx
        df_ref[...] = (dx * g2v).astype(BF16)
        p_g = jnp.sum(dy * xhat, axis=0, keepdims=True)
        p_g2 = jnp.sum(dx * f, axis=0, keepdims=True)
        p_l = jnp.zeros((1, 128), F32) + 0.5 * jnp.sum(jnp.mean(err * err, axis=-1, keepdims=True))

        @pl.when(i == 0)
        def _():
            dg_ref[...] = p_g
            dg2_ref[...] = p_g2
            loss_ref[...] = p_l

        @pl.when(i > 0)
        def _():
            dg_ref[...] += p_g
            dg2_ref[...] += p_g2
            loss_ref[...] += p_l

    vec = jax.ShapeDtypeStruct((1, D), F32)
    rows = lambda w: pl.BlockSpec((FF2_TM, w), lambda i: (i, 0))
    return _pcall(body, name="ff2_final", grid=(S // FF2_TM,),
                  in_specs=[rows(D_FF), pl.BlockSpec((D_FF, D), lambda i: (0, 0)), rows(D), _vec_spec(), rows(D),
                            _vec_spec()],
                  out_specs=(_vec_spec(128), rows(D), _vec_spec(), rows(D), _vec_spec()),
                  out_shape=(jax.ShapeDtypeStruct((1, 128), F32), jax.ShapeDtypeStruct((S, D), F32), vec,
                             jax.ShapeDtypeStruct((S, D), BF16), vec),
                  compiler_params=_params(("arbitrary",)))(act, w_ff2, x1, g2, tgt, g)


HALF = 512


MERGE_TM = 1024


def _merge_specs():
    blk = lambda off: pl.BlockSpec((MERGE_TM, HALF), lambda i, j: (i, off // HALF + j))
    return blk(OFF_GA), blk(OFF_GB), blk(0)


def _att_out_merge(att, w_att_out, proj, ret_out):
    def body(a_ref, b_ref, ga_ref, gb_ref, r_ref, o_ref, m_ref):
        acc = _dot(a_ref[...], b_ref[...], NN)
        o_ref[...] = acc
        m_ref[...] = (jax.nn.sigmoid(ga_ref[...]) * r_ref[...] + jax.nn.sigmoid(gb_ref[...]) * acc).astype(BF16)

    ga, gb, tile = _merge_specs()
    return _pcall(body, name="att_out", grid=(S // MERGE_TM, D // HALF),
                  in_specs=[pl.BlockSpec((MERGE_TM, AW), lambda i, j: (i, 0)), pl.BlockSpec((AW, HALF), lambda i, j: (0, j)),
                            ga, gb, tile],
                  out_specs=(tile, tile),
                  out_shape=(jax.ShapeDtypeStruct((S, D), F32), jax.ShapeDtypeStruct((S, D), BF16)),
                  compiler_params=_params(("parallel", "parallel")))(att, w_att_out, proj, proj, ret_out)


def _dmerged_split(dmixo, w_o, proj, ret_out, att_out):
    def body(a_ref, b_ref, ga_ref, gb_ref, r_ref, at_ref, dr_ref, da_ref, dga_ref, dgb_ref):
        dm = _dot(a_ref[...], b_ref[...], NT)
        sa = jax.nn.sigmoid(ga_ref[...])
        sb = jax.nn.sigmoid(gb_ref[...])
        dr_ref[...] = (dm * sa).astype(BF16)
        da_ref[...] = (dm * sb).astype(BF16)
        dga_ref[...] = (dm * r_ref[...] * (sa * (1.0 - sa))).astype(BF16)
        dgb_ref[...] = (dm * at_ref[...] * (sb * (1.0 - sb))).astype(BF16)

    ga, gb, tile = _merge_specs()
    o = jax.ShapeDtypeStruct((S, D), BF16)
    return _pcall(body, name="dmerged", grid=(S // MERGE_TM, D // HALF),
                  in_specs=[pl.BlockSpec((MERGE_TM, D), lambda i, j: (i, 0)), pl.BlockSpec((HALF, D), lambda i, j: (j, 0)),
                            ga, gb, tile, tile],
                  out_specs=(tile,) * 4, out_shape=(o, o, o, o),
                  compiler_params=_params(("parallel", "parallel")))(dmixo, w_o, proj, proj, ret_out, att_out)


def _ret_tables():
    H, C = RET_HEADS, CHUNK
    log_g = jnp.log1p(-(2.0 ** (-5.0 - jnp.arange(H, dtype=F32))))
    idx = jnp.arange(C, dtype=F32)
    rel = idx[:, None] - idx[None, :]
    inner = jnp.where(rel >= 0, jnp.exp(log_g[:, None, None] * jnp.maximum(rel, 0.0)), 0.0)
    qd = jnp.exp(log_g[:, None] * (idx + 1.0))[:, :, None]
    kd = jnp.exp(log_g[:, None] * (C - 1.0 - idx))[:, :, None]
    cd = jnp.broadcast_to(jnp.exp(log_g * C)[:, None, None], (H, 1, 128))
    half = RET_DK // 2
    inv = 10000.0 ** (-jnp.arange(half, dtype=F32) / half)
    ang = jnp.arange(S, dtype=F32)[:, None] * inv[None, :]
    return inner, qd, kd, cd, jnp.cos(ang), jnp.sin(ang)


def _rot(x, cos, sin):
    x1, x2 = x[:, :128], x[:, 128:]
    return jnp.concatenate([x1 * cos - x2 * sin, x1 * sin + x2 * cos], axis=1)


def _rot_t(d, cos, sin):
    d1, d2 = d[:, :128], d[:, 128:]
    return jnp.concatenate([d1 * cos + d2 * sin, d2 * cos - d1 * sin], axis=1)


RET_COLS = OFF_ATT
RET_VW = RET_HEADS * RET_DV


def _ret_specs(chunk_of):
    ci = chunk_of
    whole = lambda shape: pl.BlockSpec(shape, lambda t: (0,) * len(shape))
    return [
        pl.BlockSpec((CHUNK, RET_COLS), lambda t: (ci(t), 0)),
        pl.BlockSpec((CHUNK, 128), lambda t: (ci(t), 0)),
        pl.BlockSpec((CHUNK, 128), lambda t: (ci(t), 0)),
        whole((RET_HEADS, CHUNK, CHUNK)), whole((RET_HEADS, CHUNK, 1)), whole((RET_HEADS, CHUNK, 1)),
        whole((RET_HEADS, 1, 128)), whole((1, RET_VW)), whole((1, RET_VW)),
    ]


def _ret_cols(h):
    q = slice(OFF_RQ + h * RET_DK, OFF_RQ + (h + 1) * RET_DK)
    k = slice(OFF_RK + h * RET_DK, OFF_RK + (h + 1) * RET_DK)
    v = slice(OFF_RV + h * RET_DV, OFF_RV + (h + 1) * RET_DV)
    g = slice(OFF_RG + h * RET_DV, OFF_RG + (h + 1) * RET_DV)
    return q, k, v, g, slice(h * RET_DV, (h + 1) * RET_DV)


def _ret_fwd(proj, tables, gn_g, gn_b, after):
    inner, qd, kd, cd, cos, sin = tables

    def body(x_ref, cos_ref, sin_ref, in_ref, qd_ref, kd_ref, cd_ref, g_ref, b_ref, after_ref,
             gated_ref, ro_ref, st_ref, s_scr):
        i = pl.program_id(0)

        @pl.when(i == 0)
        def _():
            s_scr[...] = jnp.zeros_like(s_scr)

        cosv, sinv = cos_ref[...], sin_ref[...]
        for h in range(RET_HEADS):
            cq, ck, cv, cg, co = _ret_cols(h)
            q = _rot(x_ref[:, cq], cosv, sinv)
            k = _rot(x_ref[:, ck], cosv, sinv) * (RET_DK ** -0.5)
            v = x_ref[:, cv]
            st = s_scr[h]
            st_ref[h] = st.astype(BF16)
            s = _dot(q, k, NT) * in_ref[h]
            o = _dot(s, v, NN) + _dot(q, st, NN) * qd_ref[h]
            s_scr[h] = st * cd_ref[h, :, :1] + _dot(k * kd_ref[h], v, TN)
            ro_ref[:, co] = o
            mu = jnp.mean(o, axis=-1, keepdims=True)
            oc = o - mu
            var = jnp.mean(oc * oc, axis=-1, keepdims=True)
            rn = oc * lax.rsqrt(var + GN_EPS) * g_ref[:, co] + b_ref[:, co]
            rg = x_ref[:, cg]
            gated_ref[:, co] = (rg * jax.nn.sigmoid(rg) * rn).astype(BF16)

    ospec = pl.BlockSpec((CHUNK, RET_VW), lambda t: (t, 0))
    return _pcall(
        body, name="ret_fwd", grid=(N_CHUNK,), in_specs=_ret_specs(lambda t: t) + [HBM_SPEC],
        out_specs=(ospec, ospec, pl.BlockSpec((RET_HEADS, None, RET_DK, RET_DV), lambda t: (0, t, 0, 0))),
        out_shape=(jax.ShapeDtypeStruct((S, RET_VW), BF16), jax.ShapeDtypeStruct((S, RET_VW), F32),
                   jax.ShapeDtypeStruct((RET_HEADS, N_CHUNK, RET_DK, RET_DV), BF16)),
        scratch_shapes=[pltpu.VMEM((RET_HEADS, RET_DK, RET_DV), F32)],
        compiler_params=_params(("arbitrary",)))(proj, cos, sin, inner, qd, kd, cd, gn_g, gn_b, after)


def _ret_bwd(proj, tables, gn_g, gn_b, ro, states, dgated, comm=None):
    inner, qd, kd, cd, cos, sin = tables
    last = N_CHUNK - 1

    def body(x_ref, cos_ref, sin_ref, in_ref, qd_ref, kd_ref, cd_ref, g_ref, b_ref, ro_ref, st_ref, dg_ref,
             dx_ref, gg_ref, gb_ref, gs_scr):
        t = pl.program_id(0)

        @pl.when(t == 0)
        def _():
            gs_scr[...] = jnp.zeros_like(gs_scr)
            gg_ref[...] = jnp.zeros_like(gg_ref)
            gb_ref[...] = jnp.zeros_like(gb_ref)

        cosv, sinv = cos_ref[...], sin_ref[...]
        for h in range(RET_HEADS):
            cq, ck, cv, cg, co = _ret_cols(h)
            q = _rot(x_ref[:, cq], cosv, sinv)
            k = _rot(x_ref[:, ck], cosv, sinv) * (RET_DK ** -0.5)
            v = x_ref[:, cv]
            qdv, kdv, dm = qd_ref[h], kd_ref[h], in_ref[h]
            st = st_ref[h]
            o = ro_ref[:, co]
            gv = g_ref[:, co]
            mu = jnp.mean(o, axis=-1, keepdims=True)
            oc = o - mu
            rstd = lax.rsqrt(jnp.mean(oc * oc, axis=-1, keepdims=True) + GN_EPS)
            ohat = oc * rstd
            rn = ohat * gv + b_ref[:, co]
            rg = x_ref[:, cg]
            sg = jax.nn.sigmoid(rg)
            dgt = dg_ref[:, co]
            drn = dgt * (rg * sg)
            dx_ref[:, cg] = (dgt * rn * (sg * (1.0 + rg * (1.0 - sg)))).astype(BF16)
            gg_ref[:, co] += jnp.sum(drn * ohat, axis=0, keepdims=True)
            gb_ref[:, co] += jnp.sum(drn, axis=0, keepdims=True)
            dohat = drn * gv
            do = rstd * (dohat - jnp.mean(dohat, axis=-1, keepdims=True)
                         - ohat * jnp.mean(dohat * ohat, axis=-1, keepdims=True))
            gs = gs_scr[h]
            s = _dot(q, k, NT) * dm
            dsr = _dot(do, v, NT) * dm
            dq = _dot(dsr, k, NN) + _dot(do, st, NT) * qdv
            dk = _dot(dsr, q, TN) + _dot(v, gs, NT) * kdv
            dv = _dot(s, do, TN) + _dot(k * kdv, gs, NN)
            gs_scr[h] = gs * cd_ref[h, :, :1] + _dot(q * qdv, do, TN)
            dx_ref[:, cq] = _rot_t(dq, cosv, sinv).astype(BF16)
            dx_ref[:, ck] = (_rot_t(dk, cosv, sinv) * (RET_DK ** -0.5)).astype(BF16)
            dx_ref[:, cv] = dv.astype(BF16)

    rev = lambda t: last - t
    vblk = pl.BlockSpec((CHUNK, RET_VW), lambda t: (rev(t), 0))
    vspec = pl.BlockSpec((1, RET_VW), lambda t: (0, 0))
    kw = dict(name="ret_bwd", grid=(N_CHUNK,),
              in_specs=_ret_specs(rev) + [vblk, pl.BlockSpec((RET_HEADS, None, RET_DK, RET_DV),
                                                             lambda t: (0, rev(t), 0, 0)), vblk],
              out_specs=(pl.BlockSpec((CHUNK, RET_COLS), lambda t: (rev(t), 0)), vspec, vspec),
              out_shape=(jax.ShapeDtypeStruct((S, RET_COLS), BF16), jax.ShapeDtypeStruct((1, RET_VW), F32),
                         jax.ShapeDtypeStruct((1, RET_VW), F32)),
              scratch_shapes=[pltpu.VMEM((RET_HEADS, RET_DK, RET_DV), F32)])
    args = (proj, cos, sin, inner, qd, kd, cd, gn_g, gn_b, ro, states, dgated)
    if comm is not None:
        return _carry(body, comm, **kw)(*args)
    return _pcall(body, compiler_params=_params(("arbitrary",)), **kw)(*args)


def _bucket_tables():
    qi = np.arange(ATT_BLK)[:, None]
    kj = np.arange(2 * ATT_BLK)[None, :]
    m = ATT_BLK + qi - kj
    out = []
    for win, dil in ATT_GROUPS:
        w = win // dil
        dist = (np.clip(m, 0, w) * dil).astype(np.int32)
        max_exact = N_BUCKETS // 2
        d_f = np.maximum(dist, 1).astype(np.float32)
        large = max_exact + (np.log(d_f / np.float32(max_exact)) / np.float32(math.log(MAX_DIST / max_exact))
                             * np.float32(N_BUCKETS - max_exact)).astype(np.int32)
        large = np.minimum(large, N_BUCKETS - 1)
        out.append(np.where(dist < max_exact, dist, large).astype(np.int32))
    return np.stack(out)


def _bias_build(rel_bias, buckets, after):
    def body(tab_ref, bk_ref, after_ref, o_ref):
        hh = pl.program_id(0)
        bk = bk_ref[...]
        acc = jnp.zeros((ATT_BLK, 2 * ATT_BLK), F32)
        for b in range(N_BUCKETS):
            acc = jnp.where(bk == b, tab_ref[b, hh], acc)
        o_ref[...] = acc

    nh = len(ATT_GROUPS) * ATT_HG
    return _pcall(body, name="bias_build", grid=(nh,),
                  in_specs=[pl.BlockSpec(memory_space=pltpu.SMEM),
                            pl.BlockSpec((None, ATT_BLK, 2 * ATT_BLK), lambda hh: (hh // ATT_HG, 0, 0)), HBM_SPEC],
                  out_specs=pl.BlockSpec((None, ATT_BLK, 2 * ATT_BLK), lambda hh: (hh, 0, 0)),
                  out_shape=jax.ShapeDtypeStruct((nh, ATT_BLK, 2 * ATT_BLK), F32),
                  compiler_params=_params(("parallel",)))(rel_bias, buckets, after)


def _bias_grad(ds_sum, buckets):
    def body(ds_ref, bk_ref, o_ref):
        bk = bk_ref[...]
        ds = ds_ref[...]
        rows = lax.broadcasted_iota(jnp.int32, (N_BUCKETS, 128), 0)
        acc = jnp.zeros((N_BUCKETS, 128), F32)
        for b in range(N_BUCKETS):
            acc = jnp.where(rows == b, jnp.sum(jnp.where(bk == b, ds, 0.0)), acc)
        o_ref[...] = acc

    nh = len(ATT_GROUPS) * ATT_HG
    return _pcall(body, name="bias_grad", grid=(nh,),
                  in_specs=[pl.BlockSpec((None, ATT_BLK, 2 * ATT_BLK), lambda hh: (hh, 0, 0)),
                            pl.BlockSpec((None, ATT_BLK, 2 * ATT_BLK), lambda hh: (hh // ATT_HG, 0, 0))],
                  out_specs=pl.BlockSpec((None, N_BUCKETS, 128), lambda hh: (hh, 0, 0)),
                  out_shape=jax.ShapeDtypeStruct((nh, N_BUCKETS, 128), F32),
                  compiler_params=_params(("parallel",)))(ds_sum, buckets)


def _att_valid(n):
    qi = lax.broadcasted_iota(jnp.int32, (ATT_BLK, 2 * ATT_BLK), 0)
    kj = lax.broadcasted_iota(jnp.int32, (ATT_BLK, 2 * ATT_BLK), 1)
    m = ATT_BLK + qi - kj
    first_key = jnp.where(n > 0, 0, ATT_BLK)
    return (m >= 0) & (m <= ATT_BLK) & (kj >= first_key)


ATT_HP = (1, 2, 2)


def _att_geometry(gi):
    _, dil = ATT_GROUPS[gi]
    return dil, S // dil // ATT_BLK, ATT_HP[gi]


def _blk(dil, r, n):
    if dil == 1:
        return pl.ds(n * ATT_BLK, ATT_BLK)
    return pl.ds(r + n * ATT_BLK * dil, ATT_BLK, stride=dil)


def _slab_specs(gi):
    _, _, hp = _att_geometry(gi)
    per = ATT_HG // hp
    return [pl.BlockSpec((hp, S, ATT_DH), lambda g, r, part=part: ((3 * gi + part) * per + g, 0, 0))
            for part in range(3)]


def _head_specs(gi, count):
    _, _, hp = _att_geometry(gi)
    return [pl.BlockSpec((hp, S, ATT_DH), lambda g, r: (g, 0, 0))] * count


def _bias_spec(gi):
    _, _, hp = _att_geometry(gi)
    return pl.BlockSpec((hp, ATT_BLK, 2 * ATT_BLK), lambda g, r: (gi * (ATT_HG // hp) + g, 0, 0))


def _att_valid_first():
    qi = lax.broadcasted_iota(jnp.int32, (ATT_BLK, ATT_BLK), 0)
    kj = lax.broadcasted_iota(jnp.int32, (ATT_BLK, ATT_BLK), 1)
    return kj <= qi


def _att_fwd(slabs, bias, gi, comm=None):
    dil, nb, hp = _att_geometry(gi)
    scale = ATT_DH ** -0.5

    def body(q_ref, k_ref, v_ref, bias_ref, o_ref, l_ref):
        r = pl.program_id(1)
        for n in range(nb):
            cur = _blk(dil, r, n)
            valid = _att_valid(n) if n > 0 else _att_valid_first()
            for h in range(hp):
                if n > 0:
                    prev = _blk(dil, r, n - 1)
                    kk = jnp.concatenate([k_ref[h, prev, :], k_ref[h, cur, :]], axis=0)
                    vv = jnp.concatenate([v_ref[h, prev, :], v_ref[h, cur, :]], axis=0)
                    bias = bias_ref[h]
                else:
                    kk, vv, bias = k_ref[h, cur, :], v_ref[h, cur, :], bias_ref[h, :, pl.ds(ATT_BLK, ATT_BLK)]
                s = _dot(q_ref[h, cur, :], kk, NT) * scale + bias
                s = jnp.where(valid, s, -1e30)
                mx = jnp.max(s, axis=-1, keepdims=True)
                e = jnp.exp(s - mx)
                den = jnp.sum(e, axis=-1, keepdims=True)
                o_ref[h, cur, :] = _dot(e / den, vv, NN)
                l_ref[h, cur, :] = jnp.broadcast_to(mx + jnp.log(den), (ATT_BLK, ATT_DH))

    osh = jax.ShapeDtypeStruct((ATT_HG, S, ATT_DH), F32)
    kw = dict(name=f"att_fwd{gi}", grid=(ATT_HG // hp, dil), in_specs=_slab_specs(gi) + [_bias_spec(gi)],
              out_specs=tuple(_head_specs(gi, 2)), out_shape=(osh, osh))
    if comm is not None:
        return _carry(body, comm, **kw)(slabs, slabs, slabs, bias)
    return _pcall(body, compiler_params=_params(("parallel", "arbitrary")), **kw)(slabs, slabs, slabs, bias)


def _att_bwd(slabs, bias, o, lse, do, dlse, gi, comm=None):
    dil, nb, hp = _att_geometry(gi)
    per = ATT_HG // hp
    scale = ATT_DH ** -0.5
    wh = hp * ATT_DH
    wide = lambda t: jnp.concatenate([t, t], axis=1)

    def body(q_ref, k_ref, v_ref, bias_ref, o_ref, l_ref, do_ref, dl_ref, dq_ref, dk_ref, dv_ref, ds_ref):
        r = pl.program_id(1)

        @pl.when(r == 0)
        def _():
            ds_ref[...] = jnp.zeros_like(ds_ref)

        for h in range(hp):
            sl = slice(h * ATT_DH, (h + 1) * ATT_DH)
            carry_k = carry_v = None
            for n in range(nb):
                cur = _blk(dil, r, n)
                q = q_ref[h, cur, :]
                dov = do_ref[h, cur, :]
                delta = jnp.sum(dov * o_ref[h, cur, :], axis=-1, keepdims=True)
                out_rows = pl.ds(n * ATT_BLK, ATT_BLK)
                if n == 0:
                    own = pl.ds(ATT_BLK, ATT_BLK)
                    kk, vv = k_ref[h, cur, :], v_ref[h, cur, :]
                    s = _dot(q, kk, NT) * scale + bias_ref[h, :, own]
                    p = jnp.where(_att_valid_first(), jnp.exp(s - l_ref[h, cur, :]), 0.0)
                    ds = p * (_dot(dov, vv, NT) - delta + dl_ref[h, cur, :])
                    ds_ref[h, :, own] += ds
                    dq_ref[out_rows, sl] = (_dot(ds, kk, NN) * scale).astype(BF16)
                    carry_k, carry_v = _dot(ds, q, TN) * scale, _dot(p, dov, TN)
                    continue
                prev = _blk(dil, r, n - 1)
                kk = jnp.concatenate([k_ref[h, prev, :], k_ref[h, cur, :]], axis=0)
                vv = jnp.concatenate([v_ref[h, prev, :], v_ref[h, cur, :]], axis=0)
                s = _dot(q, kk, NT) * scale + bias_ref[h]
                p = jnp.where(_att_valid(n), jnp.exp(s - wide(l_ref[h, cur, :])), 0.0)
                dp = _dot(dov, vv, NT)
                ds = p * (dp - delta + wide(dl_ref[h, cur, :]))
                ds_ref[h] += ds
                dq_ref[out_rows, sl] = (_dot(ds, kk, NN) * scale).astype(BF16)
                dkk = _dot(ds, q, TN) * scale
                dvv = _dot(p, dov, TN)
                before = pl.ds((n - 1) * ATT_BLK, ATT_BLK)
                dk_ref[before, sl] = (carry_k + dkk[:ATT_BLK]).astype(BF16)
                dv_ref[before, sl] = (carry_v + dvv[:ATT_BLK]).astype(BF16)
                carry_k, carry_v = dkk[ATT_BLK:], dvv[ATT_BLK:]
            last = pl.ds((nb - 1) * ATT_BLK, ATT_BLK)
            dk_ref[last, sl] = carry_k.astype(BF16)
            dv_ref[last, sl] = carry_v.astype(BF16)

    out_spec = pl.BlockSpec((S // dil, wh), lambda g, r: (0, r * per + g))
    osh = jax.ShapeDtypeStruct((S // dil, dil * AW), BF16)
    kw = dict(name=f"att_bwd{gi}", grid=(per, dil), in_specs=_slab_specs(gi) + [_bias_spec(gi)] + _head_specs(gi, 4),
              out_specs=(out_spec, out_spec, out_spec,
                         pl.BlockSpec((hp, ATT_BLK, 2 * ATT_BLK), lambda g, r: (g, 0, 0))),
              out_shape=(osh, osh, osh, jax.ShapeDtypeStruct((ATT_HG, ATT_BLK, 2 * ATT_BLK), F32)))
    args = (slabs, slabs, slabs, bias, o, lse, do, dlse)
    if comm is not None:
        return _carry(body, comm, **kw)(*args)
    return _pcall(body, compiler_params=_params(("arbitrary", "arbitrary")), **kw)(*args)


AW = ATT_HG * ATT_DH


def _mix_weights(l0, l1, l2):
    mx = jnp.maximum(jnp.maximum(l0, l1), l2)
    e0, e1, e2 = jnp.exp(l0 - mx), jnp.exp(l1 - mx), jnp.exp(l2 - mx)
    den = e0 + e1 + e2
    return e0 / den, e1 / den, e2 / den


def _heads_spec():
    return pl.BlockSpec((ATT_HG, TR, ATT_DH), lambda i: (0, i, 0))


def _mix_fwd(os_, ls, comm=None):
    def body(o0, o1, o2, l0, l1, l2, att_ref):
        for h in range(ATT_HG):
            w0, w1, w2 = _mix_weights(l0[h], l1[h], l2[h])
            att_ref[:, h * ATT_DH:(h + 1) * ATT_DH] = (w0 * o0[h] + w1 * o1[h] + w2 * o2[h]).astype(BF16)

    kw = dict(name="mix_fwd", grid=(S // TR,), in_specs=[_heads_spec()] * 6, out_specs=_row_spec(AW),
              out_shape=jax.ShapeDtypeStruct((S, AW), BF16))
    if comm is not None:
        return _carry(body, comm, **kw)(*os_, *ls)
    return _pcall(body, compiler_params=_params(("parallel",)), **kw)(*os_, *ls)


def _mix_bwd(os_, ls, datt):
    def body(o0, o1, o2, l0, l1, l2, da_ref, d0, d1, d2, e0, e1, e2):
        for h in range(ATT_HG):
            ws = _mix_weights(l0[h], l1[h], l2[h])
            da = da_ref[:, h * ATT_DH:(h + 1) * ATT_DH]
            dws = []
            for o_ref, w, d_ref in zip((o0, o1, o2), ws, (d0, d1, d2)):
                d_ref[h] = w * da
                dws.append(jnp.broadcast_to(jnp.sum(da * o_ref[h], axis=-1, keepdims=True), (TR, ATT_DH)))
            tot = ws[0] * dws[0] + ws[1] * dws[1] + ws[2] * dws[2]
            for w, dw, e_ref in zip(ws, dws, (e0, e1, e2)):
                e_ref[h] = w * (dw - tot)

    o = jax.ShapeDtypeStruct((ATT_HG, S, ATT_DH), F32)
    return _pcall(body, name="mix_bwd", grid=(S // TR,), in_specs=[_heads_spec()] * 6 + [_row_spec(AW)],
                  out_specs=(_heads_spec(),) * 6, out_shape=(o,) * 6,
                  compiler_params=_params(("parallel",)))(*os_, *ls, datt)


def _ada_fwd(c_all, w_sh, b_sl):
    def body(c_ref, w_ref, b_ref, o_ref):
        cv = c_ref[...]
        o_ref[...] = _dot(cv * jax.nn.sigmoid(cv), w_ref[...], NN) + b_ref[...]

    return _pcall(body, name="ada_fwd", out_shape=jax.ShapeDtypeStruct((N_DEV, w_sh.shape[1]), F32),
                  compiler_params=_params())(c_all, w_sh, b_sl)


def _ada_bwd(c_all, dm_sl):
    def body(c_ref, d_ref, o_ref):
        cv = c_ref[...]
        o_ref[...] = _dot(cv * jax.nn.sigmoid(cv), d_ref[...], TN)

    return _pcall(body, name="ada_bwd", out_shape=jax.ShapeDtypeStruct((D, dm_sl.shape[1]), F32),
                  compiler_params=_params())(c_all, dm_sl)


N_MOD = 6


def _sum_small(gathered):
    n = len(gathered)

    def body(*refs):
        ins, (gb_ref, dm_ref), outs = refs[:n], refs[n:n + 2], refs[n + 2:]

        def total(r):
            acc = r[0]
            for e in range(1, N_DEV):
                acc = acc + r[e]
            return acc

        for i in range(N_MOD):
            cols = slice(i * D, (i + 1) * D)
            gb_ref[:, cols] = total(ins[i])
            for e in range(N_DEV):
                dm_ref[e:e + 1, cols] = ins[i][e]
        for r, o_ref in zip(ins[N_MOD:], outs):
            o_ref[...] = total(r)

    shapes = (jax.ShapeDtypeStruct((1, N_MOD * D), F32), jax.ShapeDtypeStruct((N_DEV, N_MOD * D), F32),
              *[jax.ShapeDtypeStruct(g.shape[1:], F32) for g in gathered[N_MOD:]])
    res = _pcall(body, name="sum_small", out_shape=shapes, compiler_params=_params())(*gathered)
    return res[0], res[1], res[2:]


def _row_tile(m, n):
    t = max(8, min(m, (1 << 19) // n // 8 * 8))
    while m % t:
        t -= 8
    return t


def _pair_sum(full, recv, sel, name, col_block=0):
    _, m, n = recv.shape
    t = _row_tile(m, n)

    def body(sel_ref, a_ref, b_ref, o_ref):
        o_ref[...] = (a_ref[...].astype(F32) + b_ref[...].astype(F32)).astype(o_ref.dtype)

    gs = pltpu.PrefetchScalarGridSpec(
        num_scalar_prefetch=1, grid=(4, m // t),
        in_specs=[pl.BlockSpec((None, None, t, n), lambda q, i, s: (q, s[0], i, col_block)),
                  pl.BlockSpec((None, t, n), lambda q, i, s: (q, i, 0))],
        out_specs=pl.BlockSpec((None, t, n), lambda q, i, s: (q, i, 0)))
    return _pcall(body, name=name, grid_spec=gs, out_shape=jax.ShapeDtypeStruct((4, m, n), full.dtype),
                  compiler_params=_params(("parallel", "parallel")))(sel, full, recv)


def _chip_sum(part, recv, sel, name):
    _, m, n = part.shape
    t = _row_tile(m, n)

    def body(sel_ref, a_ref, r_ref, o_ref):
        o_ref[...] = ((a_ref[...].astype(F32) + r_ref[0].astype(F32)) + r_ref[1].astype(F32)) + r_ref[2].astype(F32)

    gs = pltpu.PrefetchScalarGridSpec(
        num_scalar_prefetch=1, grid=(m // t,),
        in_specs=[pl.BlockSpec((None, t, n), lambda i, s: (s[0], i, 0)),
                  pl.BlockSpec((3, t, n), lambda i, s: (0, i, 0))],
        out_specs=pl.BlockSpec((t, n), lambda i, s: (i, 0)))
    return _pcall(body, name=name, grid_spec=gs, out_shape=jax.ShapeDtypeStruct((m, n), F32),
                  compiler_params=_params(("parallel",)))(sel, part, recv)


def _adamw_math(w, g, m, v):
    nm = ADAM_B1 * m + (1.0 - ADAM_B1) * g
    nv = ADAM_B2 * v + (1.0 - ADAM_B2) * (g * g)
    m_hat = nm / (1.0 - ADAM_B1 ** ADAM_STEP)
    v_hat = nv / (1.0 - ADAM_B2 ** ADAM_STEP)
    return -ADAM_LR * (m_hat / (jnp.sqrt(v_hat) + ADAM_EPS) + ADAM_WD * w), nm, nv


def _adamw(w, g, m, v, name):
    _, rows, cols = w.shape
    t = _row_tile(rows, cols)

    def body(w_ref, g_ref, m_ref, v_ref, d_ref, nm_ref, nv_ref):
        d_ref[...], nm_ref[...], nv_ref[...] = _adamw_math(w_ref[...], g_ref[...], m_ref[...], v_ref[...])

    spec3 = pl.BlockSpec((None, t, cols), lambda i: (0, i, 0))
    spec2 = pl.BlockSpec((t, cols), lambda i: (i, 0))
    o = jax.ShapeDtypeStruct(w.shape, F32)
    return _pcall(body, name=name, grid=(rows // t,), in_specs=[spec3, spec2, spec3, spec3], out_specs=(spec3,) * 3,
                  out_shape=(o, o, o), compiler_params=_params(("parallel",)))(w, g, m, v)


def _adamw_reduced1(w, m, v, part, recv, sel, name):
    _, rows, cols = w.shape
    t = _row_tile(rows, cols)

    def body(sel_ref, w_ref, m_ref, v_ref, p_ref, r_ref, g_ref, d_ref, nm_ref, nv_ref):
        g = ((p_ref[...].astype(F32) + r_ref[0].astype(F32)) + r_ref[1].astype(F32)) + r_ref[2].astype(F32)
        g_ref[...] = g
        d_ref[...], nm_ref[...], nv_ref[...] = _adamw_math(w_ref[...], g, m_ref[...], v_ref[...])

    wspec = pl.BlockSpec((None, t, cols), lambda i, s: (0, i, 0))
    gs = pltpu.PrefetchScalarGridSpec(
        num_scalar_prefetch=1, grid=(rows // t,),
        in_specs=[wspec, wspec, wspec, pl.BlockSpec((None, t, cols), lambda i, s: (s[0], i, 0)),
                  pl.BlockSpec((3, t, cols), lambda i, s: (0, i, 0))],
        out_specs=(wspec,) * 4)
    o = jax.ShapeDtypeStruct(w.shape, F32)
    return _pcall(body, name=name, grid_spec=gs, out_shape=(o, o, o, o),
                  compiler_params=_params(("parallel",)))(sel, w, m, v, part, recv)


def _adamw_reduced(w, m, v, parts, recvs, sel):
    _, rows, cols = w.shape
    half = cols // 2
    t = _row_tile(rows, half)

    def body(sel_ref, w_ref, m_ref, v_ref, pa_ref, pb_ref, ra_ref, rb_ref, g_ref, d_ref, nm_ref, nv_ref):
        total = lambda p_ref, r_ref: ((p_ref[...].astype(F32) + r_ref[0].astype(F32)) + r_ref[1].astype(F32)) \
            + r_ref[2].astype(F32)
        g = jnp.where(pl.program_id(1) == 0, total(pa_ref, ra_ref), total(pb_ref, rb_ref))
        g_ref[...] = g
        d_ref[...], nm_ref[...], nv_ref[...] = _adamw_math(w_ref[...], g, m_ref[...], v_ref[...])

    wspec = pl.BlockSpec((None, t, half), lambda i, j, s: (0, i, j))
    pspec = pl.BlockSpec((None, t, half), lambda i, j, s: (s[0], i, 0))
    rspec = pl.BlockSpec((3, t, half), lambda i, j, s: (0, i, 0))
    gs = pltpu.PrefetchScalarGridSpec(num_scalar_prefetch=1, grid=(rows // t, 2),
                                      in_specs=[wspec, wspec, wspec, pspec, pspec, rspec, rspec],
                                      out_specs=(wspec,) * 4)
    o = jax.ShapeDtypeStruct(w.shape, F32)
    return _pcall(body, name="adamw_w_in", grid_spec=gs, out_shape=(o, o, o, o),
                  compiler_params=_params(("parallel", "arbitrary")))(sel, w, m, v, *parts, *recvs)


def _adamw_small(ws, gs, ms, vs):
    n = len(ws)

    def body(*refs):
        for i in range(n):
            w_ref, g_ref, m_ref, v_ref = (refs[k * n + i] for k in range(4))
            d, nm, nv = _adamw_math(w_ref[...], g_ref[...], m_ref[...], v_ref[...])
            refs[4 * n + i][...] = d
            refs[5 * n + i][...] = nm
            refs[6 * n + i][...] = nv

    shapes = tuple(jax.ShapeDtypeStruct(w.shape, F32) for w in ws)
    res = _pcall(body, name="adamw_small", out_shape=shapes * 3, compiler_params=_params())(*ws, *gs, *ms, *vs)
    return res[:n], res[n:2 * n], res[2 * n:]


def _mesh_pos():
    return lax.axis_index("x"), lax.axis_index("y"), lax.axis_index("c")


class _Gather:
    def __init__(self, arrs):
        self.ins = list(arrs)
        self.out_shape = tuple(jax.ShapeDtypeStruct((N_DEV,) + a.shape, a.dtype) for a in arrs)
        n = len(arrs)
        self.sems = [pltpu.SemaphoreType.DMA((7 * n,)), pltpu.SemaphoreType.DMA((7 * n,)),
                     pltpu.SemaphoreType.DMA((n,))]

    def _copies(self, ins, outs, sems):
        send_sems, recv_sems, local_sems = sems
        x, y, c = _mesh_pos()
        me, sibling = (x, y, c), (x, y, 1 - c)
        chips = [(1 - x, y), (x, 1 - y), (1 - x, 1 - y)]

        def copy(p, k, block, to, from_input=False):
            dst = outs[p].at[_slot(block)]
            return pltpu.make_async_remote_copy(
                src_ref=ins[p] if from_input else dst, dst_ref=dst, send_sem=send_sems.at[7 * p + k],
                recv_sem=recv_sems.at[7 * p + k], device_id=to, device_id_type=MESH)

        npc = len(self.ins)
        mine = [pltpu.make_async_copy(ins[p], outs[p].at[_slot(me)], local_sems.at[p]) for p in range(npc)]
        first = []
        for p in range(npc):
            first.append(copy(p, 0, me, sibling, from_input=True))
            first += [copy(p, 1 + j, me, (*chip, c), from_input=True) for j, chip in enumerate(chips)]
        return me, sibling, chips, c, copy, mine, first

    def start(self, ins, outs, sems):
        *_, mine, first = self._copies(ins, outs, sems)
        for cp in mine + first:
            cp.start()

    def finish(self, ins, outs, sems):
        me, sibling, chips, c, copy, mine, first = self._copies(ins, outs, sems)
        npc = len(self.ins)
        passed = []
        for p in range(npc):
            for j, chip in enumerate(chips):
                copy(p, 1 + j, (*chip, c), me).wait_recv()
                passed.append(copy(p, 4 + j, (*chip, c), sibling))
                passed[-1].start()
        for p in range(npc):
            copy(p, 0, sibling, me).wait_recv()
            for j, chip in enumerate(chips):
                copy(p, 4 + j, (*chip, 1 - c), me).wait_recv()
        for cp in first + passed:
            cp.wait_send()
        for cp in mine:
            cp.wait()


class _ExchangeCore:
    def __init__(self, fulls, cols=None):
        self.ins = list(fulls)
        self.cols = cols
        width = lambda f: f.shape[3] if cols is None else cols[1]
        self.out_shape = tuple(jax.ShapeDtypeStruct((4, f.shape[2], width(f)), f.dtype) for f in fulls)
        self.sems = [pltpu.SemaphoreType.DMA((4 * len(fulls),)), pltpu.SemaphoreType.DMA((4 * len(fulls),))]

    def _copies(self, ins, outs, sems):
        send_sems, recv_sems = sems
        x, y, c = _mesh_pos()

        def src(a, q):
            ref = ins[a].at[q, 1 - c]
            return ref if self.cols is None else ref.at[:, pl.ds(*self.cols)]

        return [pltpu.make_async_remote_copy(
            src_ref=src(a, q), dst_ref=outs[a].at[q], send_sem=send_sems.at[4 * a + q],
            recv_sem=recv_sems.at[4 * a + q], device_id=(x, y, 1 - c), device_id_type=MESH)
            for a in range(len(self.ins)) for q in range(4)]

    def start(self, ins, outs, sems):
        for cp in self._copies(ins, outs, sems):
            cp.start()

    def finish(self, ins, outs, sems):
        for cp in self._copies(ins, outs, sems):
            cp.wait()


class _ExchangeChip:
    def __init__(self, parts):
        self.ins = list(parts)
        self.out_shape = tuple(jax.ShapeDtypeStruct((3,) + p.shape[1:], p.dtype) for p in parts)
        self.sems = [pltpu.SemaphoreType.DMA((3 * len(parts),)), pltpu.SemaphoreType.DMA((3 * len(parts),))]

    def _copies(self, ins, outs, sems):
        send_sems, recv_sems = sems
        x, y, c = _mesh_pos()
        chips = [(1 - x, y), (x, 1 - y), (1 - x, 1 - y)]
        return [pltpu.make_async_remote_copy(
            src_ref=ins[a].at[2 * px + py], dst_ref=outs[a].at[j], send_sem=send_sems.at[3 * a + j],
            recv_sem=recv_sems.at[3 * a + j], device_id=(px, py, c), device_id_type=MESH)
            for a in range(len(self.ins)) for j, (px, py) in enumerate(chips)]

    def start(self, ins, outs, sems):
        for cp in self._copies(ins, outs, sems):
            cp.start()

    def finish(self, ins, outs, sems):
        for cp in self._copies(ins, outs, sems):
            cp.wait()


HBM_ONLY = pl.BlockSpec(memory_space=pltpu.HBM)
SEM_SPEC = pl.BlockSpec(memory_space=pltpu.SEMAPHORE)
SIDE_EFFECT = pltpu.SideEffectType.DATAFLOW_SIDE_EFFECTING


def _chip_copies(p_refs, land_refs, send_sems, recv_sems):
    x, y, c = _mesh_pos()
    return [pltpu.make_async_remote_copy(
        src_ref=p_refs[a].at[2 * px + py], dst_ref=land_refs[a].at[j], send_sem=send_sems.at[3 * a + j],
        recv_sem=recv_sems.at[3 * a + j], device_id=(px, py, c), device_id_type=MESH)
        for a in range(len(p_refs)) for j, (px, py) in enumerate([(1 - x, y), (x, 1 - y), (1 - x, 1 - y)])]


def _chip_exchange_start(parts, name):
    n = len(parts)
    lands = [lax.empty((3,) + p.shape[1:], p.dtype) for p in parts]

    def body(*refs):
        p_refs, land_refs, (send_sems, recv_sems) = refs[:n], refs[n:2 * n], refs[2 * n:2 * n + 2]
        for cp in _chip_copies(p_refs, land_refs, send_sems, recv_sems):
            cp.start()
        token = refs[-1]
        token[...] = jnp.zeros_like(token)

    hbm = lambda t: pltpu.HBM(t.shape, t.dtype)
    res = pl.pallas_call(
        body, name=name,
        out_shape=(pltpu.SemaphoreType.DMA((3 * n,)), pltpu.SemaphoreType.DMA((3 * n,)), *[hbm(t) for t in parts + lands],
                   jax.ShapeDtypeStruct((8, 128), F32)),
        in_specs=(HBM_ONLY,) * (2 * n),
        out_specs=(SEM_SPEC, SEM_SPEC, *[HBM_ONLY] * (2 * n), pl.BlockSpec(memory_space=pltpu.VMEM)),
        input_output_aliases={i: 2 + i for i in range(2 * n)},
        compiler_params=pltpu.CompilerParams(has_side_effects=SIDE_EFFECT))(
        *[pltpu.with_memory_space_constraint(t, pltpu.HBM) for t in parts + lands])
    return (res[0], res[1], list(res[2:2 + n]), list(res[2 + n:2 + 2 * n])), res[-1]


def _chip_exchange_wait(in_flight, after, name):
    send_sems, recv_sems, parts, lands = in_flight
    n = len(parts)

    def body(*refs):
        p_refs, land_refs, (send_sems, recv_sems) = refs[:n], refs[n:2 * n], refs[2 * n:2 * n + 2]
        for cp in _chip_copies(p_refs, land_refs, send_sems, recv_sems):
            cp.wait_send()
            cp.wait_recv()

    res = pl.pallas_call(
        body, name=name, out_shape=tuple(pltpu.HBM(t.shape, t.dtype) for t in parts + lands),
        in_specs=(*[HBM_ONLY] * (2 * n), SEM_SPEC, SEM_SPEC, pl.BlockSpec(memory_space=pl.ANY)),
        out_specs=(HBM_ONLY,) * (2 * n), input_output_aliases={i: i for i in range(2 * n)},
        compiler_params=pltpu.CompilerParams(has_side_effects=SIDE_EFFECT))(*parts, *lands, send_sems, recv_sems, after)
    return list(res[:n]), list(res[n:])


def _slot(p):
    return 4 * p[0] + 2 * p[1] + p[2]


def _gather_copies(src_refs, out_refs, send_sems, recv_sems):
    x, y, c = _mesh_pos()
    targets = [(x, y, 1 - c), (1 - x, y, c), (x, 1 - y, c), (1 - x, 1 - y, c)]
    return [pltpu.make_async_remote_copy(
        src_ref=src_refs[a], dst_ref=out_refs[a].at[_slot((x, y, c))], send_sem=send_sems.at[4 * a + k],
        recv_sem=recv_sems.at[4 * a + k], device_id=to, device_id_type=MESH)
        for a in range(len(src_refs)) for k, to in enumerate(targets)]


def _gather_start(shards, after, name):
    n = len(shards)
    outs = [lax.empty((N_DEV,) + s.shape, s.dtype) for s in shards]

    def body(*refs):
        for cp in _gather_copies(refs[:n], refs[n:2 * n], refs[2 * n + 1], refs[2 * n + 2]):
            cp.start()
        token = refs[-1]
        token[...] = jnp.zeros_like(token)

    res = pl.pallas_call(
        body, name=name,
        out_shape=(pltpu.SemaphoreType.DMA((4 * n,)), pltpu.SemaphoreType.DMA((4 * n,)),
                   *[pltpu.HBM(t.shape, t.dtype) for t in shards + outs], jax.ShapeDtypeStruct((8, 128), F32)),
        in_specs=(*[HBM_ONLY] * (2 * n), pl.BlockSpec(memory_space=pl.ANY)),
        out_specs=(SEM_SPEC, SEM_SPEC, *[HBM_ONLY] * (2 * n), pl.BlockSpec(memory_space=pltpu.VMEM)),
        input_output_aliases={i: 2 + i for i in range(2 * n)},
        compiler_params=pltpu.CompilerParams(has_side_effects=SIDE_EFFECT))(
        *[pltpu.with_memory_space_constraint(t, pltpu.HBM) for t in shards + outs], after)
    return (res[0], res[1], list(res[2:2 + n]), list(res[2 + n:2 + 2 * n])), res[-1]


def _gather_wait(in_flight, after, name):
    send_sems, recv_sems, shards, outs = in_flight
    n = len(shards)

    def body(*refs):
        for cp in _gather_copies(refs[:n], refs[n:2 * n], refs[2 * n], refs[2 * n + 1]):
            cp.wait_send()
            cp.wait_recv()

    res = pl.pallas_call(
        body, name=name, out_shape=tuple(pltpu.HBM(t.shape, t.dtype) for t in shards + outs),
        in_specs=(*[HBM_ONLY] * (2 * n), SEM_SPEC, SEM_SPEC, pl.BlockSpec(memory_space=pl.ANY)),
        out_specs=(HBM_ONLY,) * (2 * n), input_output_aliases={i: i for i in range(2 * n)},
        compiler_params=pltpu.CompilerParams(has_side_effects=SIDE_EFFECT))(*shards, *outs, send_sems, recv_sems, after)
    return list(res[:n]), list(res[n:])


class _PassToSibling:
    def __init__(self, shards, gathered):
        n = self.n = len(shards)
        self.ins = list(shards) + list(gathered)
        self.out_shape = tuple(jax.ShapeDtypeStruct(g.shape, g.dtype) for g in gathered)
        self.aliases = {n + a: a for a in range(n)}
        self.sems = [pltpu.SemaphoreType.DMA((3 * n,)), pltpu.SemaphoreType.DMA((3 * n,)),
                     pltpu.SemaphoreType.DMA((n,))]

    def _copies(self, ins, outs, sems):
        send_sems, recv_sems, local_sems = sems
        x, y, c = _mesh_pos()
        chips = [(1 - x, y), (x, 1 - y), (1 - x, 1 - y)]
        mine = [pltpu.make_async_copy(ins[a], outs[a].at[_slot((x, y, c))], local_sems.at[a]) for a in range(self.n)]
        passed, awaited = [], []
        for a in range(self.n):
            for j, chip in enumerate(chips):
                sems_j = dict(send_sem=send_sems.at[3 * a + j], recv_sem=recv_sems.at[3 * a + j],
                              device_id=(x, y, 1 - c), device_id_type=MESH)
                blk = outs[a].at[_slot((*chip, c))]
                passed.append(pltpu.make_async_remote_copy(src_ref=blk, dst_ref=blk, **sems_j))
                got = outs[a].at[_slot((*chip, 1 - c))]
                awaited.append(pltpu.make_async_remote_copy(src_ref=got, dst_ref=got, **sems_j))
        return mine, passed, awaited

    def start(self, ins, outs, sems):
        mine, passed, _ = self._copies(ins, outs, sems)
        for cp in mine + passed:
            cp.start()

    def finish(self, ins, outs, sems):
        mine, passed, awaited = self._copies(ins, outs, sems)
        for cp in passed:
            cp.wait_send()
        for cp in awaited:
            cp.wait_recv()
        for cp in mine:
            cp.wait()


def _reduce_sums(fulls, recv_core, core, tag):
    return [_pair_sum(f, r, core, f"rs_pair_{tag}{i}") for i, (f, r) in enumerate(zip(fulls, recv_core))]


def _local_step(x, tgt, mods, w_in_shard, order, shards, small, chip, core):
    sh1, sc1, g1, sh2, sc2, g2 = mods
    norm1_g, rel_bias, gn_g, gn_b, norm2_g, norm_f_g = small
    tables = _ret_tables()
    buckets = jnp.asarray(_bucket_tables())

    h1 = _norm_mod_fwd(x, norm1_g, sh1, sc1, "norm1_fwd")
    proj, slabs, w_in_t = _gather_proj(h1, w_in_shard, order)
    flight_w1, token_w = _gather_start(list(shards[:3]), proj, "gather_w1_start")
    flight_w2, token_w = _gather_start(list(shards[3:]), token_w, "gather_w2_start")
    gated, ro, states = _ret_fwd(proj, tables, gn_g, gn_b, token_w)
    bias = _bias_build(rel_bias, buckets, token_w)
    outs, lses = [], []
    for gi in range(len(ATT_GROUPS)):
        o, l = _att_fwd(slabs, bias, gi)
        outs.append(o)
        lses.append(l)
    att, gathered = _mix_fwd(outs, lses, comm=_PassToSibling(*_gather_wait(flight_w1, lses[2], "gather_w1_wait")))
    w_ret_out, w_att_out, w_o = (_from_slots(g, ax) for g, ax in zip(gathered, BIG_AXES[1:4]))
    ret_out, gathered = _mm(gated, w_ret_out, 'nn', tm=S, tn=256, tk=2048, name="ret_out",
                            comm=_PassToSibling(*_gather_wait(flight_w2, att, "gather_w2_wait")))
    w_ff1, w_ff2 = (_from_slots(g, ax) for g, ax in zip(gathered, BIG_AXES[4:]))
    att_out, merged = _att_out_merge(att, w_att_out, proj, ret_out)
    mixo, x1, h2 = _w_o_norm2(merged, w_o, x, g1, norm2_g, sh2, sc2)
    u, act = _mm(h2, w_ff1, 'nn', tm=S, tn=512, tk=D, name="ff1", relu2=True)
    loss, dx2, g_normf, df, dg2 = _ff2_final(act, w_ff2, x1, g2, tgt, norm_f_g)

    gw_ff2 = _mm(act, df, 'tn', tm=512, tn=D, tk=S, name="gw_ff2", out_dtype=BF16)
    du = _mm(df, w_ff2, 'nt', tm=S, tn=512, tk=D, name="d_act", out_dtype=BF16, relu2_of=u)
    gw_ff1 = _mm(h2, du, 'tn', tm=D, tn=512, tk=S, name="gw_ff1", out_dtype=BF16)
    fulls_a = [_to_slots(g, ax) for g, ax in zip((gw_ff1, gw_ff2), BIG_AXES[4:])]
    dh2, recv_core_a = _mm(du, w_ff1, 'nt', tm=1024, tn=1024, tk=2048, name="dh2", comm=_ExchangeCore(fulls_a))
    parts_a = _reduce_sums(fulls_a, recv_core_a, core, "a")
    flight_a, token_a = _chip_exchange_start(parts_a, "rs_a_start")
    dx1, dsc2, dsh2, g_norm2, dmixo, dg1 = _norm_mod_bwd(x1, norm2_g, sc2, dh2, dx2, "norm2_bwd", gate=(mixo, g1))

    gw_o = _mm(merged, dmixo, 'tn', tm=D, tn=512, tk=S, name="gw_o", out_dtype=BF16, after=token_a)
    d_ret_out, d_att_out, dga, dgb = _dmerged_split(dmixo, w_o, proj, ret_out, att_out)
    gw_ret_out = _mm(gated, d_ret_out, 'tn', tm=512, tn=D, tk=S, name="gw_ret_out", out_dtype=BF16)
    gw_att_out = _mm(att, d_att_out, 'tn', tm=AW, tn=D, tk=S, name="gw_att_out", out_dtype=BF16)
    fulls_b = [_to_slots(g, ax) for g, ax in zip((gw_ret_out, gw_att_out, gw_o), BIG_AXES[1:4])]
    dgated, recv_core_b = _mm(d_ret_out, w_ret_out, 'nt', tm=S, tn=512, tk=D, name="dgated",
                              comm=_ExchangeCore(fulls_b))
    parts_b = _reduce_sums(fulls_b, recv_core_b, core, "b")
    flight_b, token_b = _chip_exchange_start(parts_b, "rs_b_start")
    datt = _mm(d_att_out, w_att_out, 'nt', tm=S, tn=AW, tk=D, name="datt", after=token_b)
    mix_grads = _mix_bwd(outs, lses, datt)
    datt_parts, ds_sums = [], []
    for gi in range(len(ATT_GROUPS)):
        dq, dk, dv, ds_sum = _att_bwd(slabs, bias, outs[gi], lses[gi], mix_grads[gi], mix_grads[3 + gi], gi)
        datt_parts += [dq.reshape(S, AW), dk.reshape(S, AW), dv.reshape(S, AW)]
        ds_sums.append(ds_sum)
    g_bias = _bias_grad(jnp.concatenate(ds_sums, axis=0), buckets)[:, :, 0].T.reshape(1, -1)
    dret, g_gn_g, g_gn_b = _ret_bwd(proj, tables, gn_g, gn_b, ro, states, dgated)
    parts_a, recv_chip_a = _chip_exchange_wait(flight_a, dret, "rs_a_wait")
    parts_b, recv_chip_b = _chip_exchange_wait(flight_b, dret, "rs_b_wait")
    reduced = list(zip(parts_b + parts_a, recv_chip_b + recv_chip_a))
    dproj = jnp.concatenate([dret] + datt_parts + [dga, dgb], axis=1)
    full_in = _to_slots(_mm(dproj, h1, 'tn', tm=512, tn=D, tk=S, name="gw_in", out_dtype=BF16), 0)
    in_flight, token = [], None
    for half in range(2):
        (recv_core_in,) = _run_comm(_ExchangeCore([full_in], cols=(half * (D // 2), D // 2)), f"rs_core_in{half}",
                                    after=token)
        part_in = [_pair_sum(full_in, recv_core_in, core, f"rs_pair_c{half}", col_block=half)]
        flight, token = _chip_exchange_start(part_in, f"rs_in{half}_start")
        in_flight.append(flight)
    dh1 = _mm(dproj, w_in_t, 'nn', tm=1024, tn=1024, tk=2560, name="dh1", after=token)
    gx, dsc1, dsh1, g_norm1 = _norm_mod_bwd(x, norm1_g, sc1, dh1, dx1, "norm1_bwd")

    dmod = [dsh1, dsc1, dg1, dsh2, dsc2, dg2]
    small_g = [g_norm1, g_bias, g_gn_g, g_gn_b, g_norm2, g_normf]
    return loss, gx, in_flight, reduced, small_g, dmod


def _to_slots(g, axis):
    if axis == 0:
        return g.reshape(4, 2, g.shape[0] // N_DEV, g.shape[1])
    return g.reshape(g.shape[0], N_DEV, g.shape[1] // N_DEV).transpose(1, 0, 2).reshape(4, 2, g.shape[0], -1)


def _from_slots(w8, axis):
    if axis == 0:
        return w8.reshape(-1, w8.shape[2])
    return w8.transpose(1, 0, 2).reshape(w8.shape[1], -1)


BIG_AXES = (1, 0, 1, 0, 1, 0)


def kernel(x, c, w_ada, b_ada, norm1_g, w_in, rel_bias, ret_gn_g, ret_gn_b, w_ret_out, w_att_out, w_o, norm2_g, w_ff1, w_ff2, norm_f_g, loss_target, m_w_ada, m_b_ada, m_norm1_g, m_w_in, m_rel_bias, m_ret_gn_g, m_ret_gn_b, m_w_ret_out, m_w_att_out, m_w_o, m_norm2_g, m_w_ff1, m_w_ff2, m_norm_f_g, v_w_ada, v_b_ada, v_norm1_g, v_w_in, v_rel_bias, v_ret_gn_g, v_ret_gn_b, v_w_ret_out, v_w_att_out, v_w_o, v_norm2_g, v_w_ff1, v_w_ff2, v_norm_f_g):
    mx, my, mc = _mesh_pos()
    dev = 4 * mx + 2 * my + mc
    chip = jnp.reshape(2 * mx + my, (1,)).astype(jnp.int32)
    core = jnp.reshape(mc, (1,)).astype(jnp.int32)
    ada_w = D * 6 // N_DEV

    w_in, m_w_in, v_w_in = (jnp.transpose(t, (0, 2, 1)) for t in (w_in, m_w_in, v_w_in))

    shards = [w[0].astype(BF16) for w in (w_in, w_ret_out, w_att_out, w_o, w_ff1, w_ff2)]
    (c_all,) = _run_comm(_Gather([c]), "gather_c")
    c_all = c_all.reshape(N_DEV, D)
    b_sl = lax.dynamic_slice(b_ada, (0, dev * ada_w), (1, ada_w))
    (mod_all,) = _run_comm(_Gather([_ada_fwd(c_all, w_ada[0], b_sl)]), "gather_mod")
    mod = lax.dynamic_index_in_dim(mod_all, dev, axis=1, keepdims=False).reshape(6, D)
    mods = tuple(mod[i:i + 1] for i in range(6))

    small = (norm1_g, rel_bias, ret_gn_g, ret_gn_b, norm2_g, norm_f_g.reshape(1, D))
    order = lax.dynamic_index_in_dim(jnp.asarray(_proj_order()), 2 * mx + my, axis=0, keepdims=False)
    loss, gx, in_flight, big_red, small_g, dmod = _local_step(x[0], loss_target[0], mods, shards[0], order,
                                                              shards[1:], small, chip, core)

    gathered = _run_comm(_Gather(dmod + small_g + [loss]), "gather_small")
    g_b_ada, dmod_all, (g_norm1, g_bias, g_gn_g, g_gn_b, g_norm2, g_normf, loss_sum) = _sum_small(gathered)
    loss_out = loss_sum[0, 0]
    g_w_ada = _ada_bwd(c_all, lax.dynamic_slice(dmod_all, (0, dev * ada_w), (N_DEV, ada_w)))

    names = ['w_ada', 'b_ada', 'norm1_g', 'w_in', 'rel_bias', 'ret_gn_g', 'ret_gn_b', 'w_ret_out', 'w_att_out',
             'w_o', 'norm2_g', 'w_ff1', 'w_ff2', 'norm_f_g']
    ws = dict(zip(names, (w_ada, b_ada, norm1_g, w_in, rel_bias, ret_gn_g, ret_gn_b, w_ret_out, w_att_out, w_o,
                          norm2_g, w_ff1, w_ff2, norm_f_g)))
    ms = dict(zip(names, (m_w_ada, m_b_ada, m_norm1_g, m_w_in, m_rel_bias, m_ret_gn_g, m_ret_gn_b, m_w_ret_out,
                          m_w_att_out, m_w_o, m_norm2_g, m_w_ff1, m_w_ff2, m_norm_f_g)))
    vs = dict(zip(names, (v_w_ada, v_b_ada, v_norm1_g, v_w_in, v_rel_bias, v_ret_gn_g, v_ret_gn_b, v_w_ret_out,
                          v_w_att_out, v_w_o, v_norm2_g, v_w_ff1, v_w_ff2, v_norm_f_g)))
    grads = dict(w_ada=g_w_ada, b_ada=g_b_ada, norm1_g=g_norm1, rel_bias=g_bias,
                 ret_gn_g=g_gn_g, ret_gn_b=g_gn_b, norm2_g=g_norm2, norm_f_g=g_normf)
    delta, new_m, new_v = {}, {}, {}
    delta['w_ada'], new_m['w_ada'], new_v['w_ada'] = _adamw(w_ada, g_w_ada, m_w_ada, v_w_ada, "adamw_w_ada")
    grads['w_ada'] = g_w_ada.reshape(w_ada.shape)
    for n, (part, recv) in zip(('w_ret_out', 'w_att_out', 'w_o', 'w_ff1', 'w_ff2'), big_red):
        grads[n], delta[n], new_m[n], new_v[n] = _adamw_reduced1(ws[n], ms[n], vs[n], part, recv, chip, "adamw_" + n)
    small_names = ('b_ada', 'norm1_g', 'rel_bias', 'ret_gn_g', 'ret_gn_b', 'norm2_g', 'norm_f_g')
    two_d = {n: (1, ws[n].size) if ws[n].ndim == 1 else ws[n].shape for n in small_names}
    d_, m_, v_ = _adamw_small(*[[src[n].reshape(two_d[n]) for n in small_names] for src in (ws, grads, ms, vs)])
    for i, n in enumerate(small_names):
        shp = ws[n].shape
        delta[n], new_m[n], new_v[n] = d_[i].reshape(shp), m_[i].reshape(shp), v_[i].reshape(shp)
        grads[n] = grads[n].reshape(shp)

    done = lax.optimization_barrier((gx, tuple(d_), tuple(delta[n] for n in ('w_ada', 'w_ret_out', 'w_att_out', 'w_o',
                                                                               'w_ff1', 'w_ff2'))))
    parts_in, recvs_in = [], []
    for half, flight in enumerate(in_flight):
        (part_in,), (recv_chip_in,) = _chip_exchange_wait(flight, done[0], f"rs_in{half}_wait")
        parts_in.append(part_in)
        recvs_in.append(recv_chip_in)
    grads['w_in'], delta['w_in'], new_m['w_in'], new_v['w_in'] = _adamw_reduced(w_in, m_w_in, v_w_in, parts_in,
                                                                               recvs_in, chip)
    for d in (grads, delta, new_m, new_v):
        d['w_in'] = jnp.transpose(d['w_in'], (0, 2, 1))
    return (loss_out, gx[None], *[grads[n] for n in names], *[delta[n] for n in names],
            *[new_m[n] for n in names], *[new_v[n] for n in names])
```

```python
import functools
import math

import numpy as np
import jax
import jax.numpy as jnp
from jax import lax
from jax.experimental import pallas as pl
from jax.experimental.pallas import tpu as pltpu

F32 = jnp.float32
BF16 = jnp.bfloat16
MESH = pl.DeviceIdType.MESH

N_DEV = 8
S = 2048
D = 1024
RET_HEADS = 4
RET_DK = 256
RET_DV = 512
CHUNK = 128
N_CHUNK = S // CHUNK
ATT_GROUPS = ((128, 1), (512, 4), (2048, 16))
ATT_HG = 4
ATT_DH = 128
ATT_BLK = 128
N_BUCKETS = 32
MAX_DIST = 2048
D_FF = 4096
IN_COLS = 12800
OFF_RQ, OFF_RK, OFF_RV, OFF_RG, OFF_ATT = 0, 1024, 2048, 4096, 6144
OFF_GA, OFF_GB = 6144, 7168
RMS_EPS = 1e-6
GN_EPS = 1e-5
ADAM_LR, ADAM_B1, ADAM_B2, ADAM_EPS, ADAM_WD, ADAM_STEP = 0.001, 0.9, 0.999, 1e-08, 0.01, 10
VMEM_LIMIT = 48 * 1024 * 1024


def _pcall(body, **kw):
    return pl.pallas_call(body, **kw)


def _params(sem=None):
    return pltpu.CompilerParams(dimension_semantics=sem, vmem_limit_bytes=VMEM_LIMIT)


HBM_SPEC = pl.BlockSpec(memory_space=pl.ANY)


def _carry(body, comm, *, name, grid, in_specs, out_specs, out_shape, scratch_shapes=()):
    single = not isinstance(out_specs, (tuple, list))
    o_specs = (out_specs,) if single else tuple(out_specs)
    o_shape = (out_shape,) if single else tuple(out_shape)
    n_in, n_out, n_scr = len(in_specs), len(o_specs), len(scratch_shapes)
    nci, nco = len(comm.ins), len(comm.out_shape)
    total = int(np.prod(grid))

    def wrapped(*refs):
        bounds = np.cumsum([0, n_in, nci, n_out, nco, n_scr])
        a, ci, o, co, scr = (refs[bounds[i]:bounds[i + 1]] for i in range(5))
        sems = refs[bounds[5]:]
        flat = 0
        for d, g in enumerate(grid):
            flat = flat * g + pl.program_id(d)

        @pl.when(flat == 0)
        def _():
            comm.start(ci, co, sems)

        body(*a, *o, *scr)

        @pl.when(flat == total - 1)
        def _():
            comm.finish(ci, co, sems)

    aliases = {n_in + i: n_out + o for i, o in getattr(comm, "aliases", {}).items()}
    call = _pcall(wrapped, name=name, grid=grid, in_specs=list(in_specs) + [HBM_SPEC] * nci,
                  out_specs=o_specs + (HBM_SPEC,) * nco, out_shape=o_shape + tuple(comm.out_shape),
                  scratch_shapes=list(scratch_shapes) + list(comm.sems), input_output_aliases=aliases,
                  compiler_params=_params(("arbitrary",) * len(grid)))

    def run(*args):
        res = call(*args, *comm.ins)
        own = res[0] if single else tuple(res[:n_out])
        return own, tuple(res[n_out:])

    return run


def _run_comm(comm, name, after=None):
    nci, nco = len(comm.ins), len(comm.out_shape)
    extra = [] if after is None else [after]

    def body(*refs):
        ci, co, sems = refs[:nci], refs[nci + len(extra):nci + len(extra) + nco], refs[nci + len(extra) + nco:]
        comm.start(ci, co, sems)
        comm.finish(ci, co, sems)

    return _pcall(body, name=name, in_specs=[HBM_SPEC] * (nci + len(extra)), out_specs=(HBM_SPEC,) * nco,
                  out_shape=tuple(comm.out_shape), scratch_shapes=list(comm.sems))(*comm.ins, *extra)


def _dot(a, b, dn):
    return lax.dot_general(a.astype(BF16), b.astype(BF16), (dn, ((), ())), preferred_element_type=F32)


NN = ((1,), (0,))
NT = ((1,), (1,))
TN = ((0,), (0,))


def _mm(a, b, mode, *, tm, tn, tk, name, out_dtype=F32, res=None, gvec=None, relu2=False, relu2_of=None, comm=None,
        after=None):
    if mode == 'nn':
        (M, K), (_, N) = a.shape, b.shape
        a_spec = pl.BlockSpec((tm, tk), lambda i, j, k: (i, k))
        b_spec = pl.BlockSpec((tk, tn), lambda i, j, k: (k, j))
        dn = NN
    elif mode == 'nt':
        (M, K), (N, _) = a.shape, b.shape
        a_spec = pl.BlockSpec((tm, tk), lambda i, j, k: (i, k))
        b_spec = pl.BlockSpec((tn, tk), lambda i, j, k: (j, k))
        dn = NT
    else:
        (K, M), (_, N) = a.shape, b.shape
        a_spec = pl.BlockSpec((tk, tm), lambda i, j, k: (k, i))
        b_spec = pl.BlockSpec((tk, tn), lambda i, j, k: (k, j))
        dn = TN
    assert M % tm == 0 and N % tn == 0 and K % tk == 0, (name, M, N, K)
    nk = K // tk
    fused = res is not None
    o_spec = pl.BlockSpec((tm, tn), lambda i, j, k: (i, j))

    def body(a_ref, b_ref, *rest):
        acc_ref = rest[-1] if nk > 1 else None
        if after is not None:
            rest = rest[1:]
        if fused:
            res_ref, g_ref, o_ref, x_ref = rest[:4]
        elif relu2_of is not None:
            u_ref, o_ref = rest[:2]
        elif relu2:
            o_ref, act_ref = rest[:2]
        else:
            o_ref = rest[0]

        def finish(acc):
            if relu2_of is not None:
                acc = acc * (2.0 * jnp.maximum(u_ref[...], 0.0))
            o_ref[...] = acc.astype(o_ref.dtype)
            if fused:
                x_ref[...] = res_ref[...] + g_ref[...] * acc
            if relu2:
                r = jnp.maximum(acc, 0.0)
                act_ref[...] = (r * r).astype(BF16)

        p = _dot(a_ref[...], b_ref[...], dn)
        if nk == 1:
            finish(p)
        else:
            k = pl.program_id(2)

            @pl.when(k == 0)
            def _():
                acc_ref[...] = p

            @pl.when(k > 0)
            def _():
                acc_ref[...] += p

            @pl.when(k == nk - 1)
            def _():
                finish(acc_ref[...])

    in_specs = [a_spec, b_spec]
    args = [a, b]
    if after is not None:
        in_specs.append(pl.BlockSpec(memory_space=pl.ANY))
        args.append(after)
    out_shape = jax.ShapeDtypeStruct((M, N), out_dtype)
    out_specs = o_spec
    if fused:
        in_specs += [pl.BlockSpec((tm, tn), lambda i, j, k: (i, j)), pl.BlockSpec((1, tn), lambda i, j, k: (0, j))]
        args += [res, gvec]
        out_shape = (out_shape, jax.ShapeDtypeStruct((M, N), F32))
        out_specs = (o_spec, pl.BlockSpec((tm, tn), lambda i, j, k: (i, j)))
    elif relu2_of is not None:
        in_specs.append(pl.BlockSpec((tm, tn), lambda i, j, k: (i, j)))
        args.append(relu2_of)
    elif relu2:
        out_shape = (out_shape, jax.ShapeDtypeStruct((M, N), BF16))
        out_specs = (o_spec, pl.BlockSpec((tm, tn), lambda i, j, k: (i, j)))
    kw = dict(name=name, grid=(M // tm, N // tn, nk), in_specs=in_specs, out_specs=out_specs,
              out_shape=out_shape, scratch_shapes=[pltpu.VMEM((tm, tn), F32)] if nk > 1 else [])
    if comm is not None:
        return _carry(body, comm, **kw)(*args)
    return _pcall(body, compiler_params=_params(("parallel", "parallel", "arbitrary")), **kw)(*args)


PROJ_TN = 512
ATT_T0, ATT_T1 = 6144 // PROJ_TN, 10752 // PROJ_TN
N_SLABS = (ATT_T1 - ATT_T0) * 4
MAIN_COLS = IN_COLS - (ATT_T1 - ATT_T0) * PROJ_TN


PROJ_TILES = IN_COLS // PROJ_TN
SHARD_ROWS = IN_COLS // N_DEV
W_CHUNKS = 4
N_OWN, N_NEAR = 5, 18


def _proj_order():
    out = np.zeros((4, 3, PROJ_TILES), np.int32)
    for q in range(4):
        def hops(t):
            owners = {col // (2 * SHARD_ROWS) for col in (t * PROJ_TN, (t + 1) * PROJ_TN - 1)}
            return max(bin(q ^ p).count("1") for p in owners)
        order = sorted(range(PROJ_TILES), key=lambda t: (hops(t), t))
        assert all(hops(t) == 0 for t in order[:N_OWN]) and all(hops(t) < 2 for t in order[:N_NEAR])
        is_att = [ATT_T0 <= t < ATT_T1 for t in order]
        for row, kind, index in ((1, False, lambda t: t if t < ATT_T0 else t - (ATT_T1 - ATT_T0)),
                                 (2, True, lambda t: t - ATT_T0)):
            own = [index(t) if a == kind else None for t, a in zip(order, is_att)]
            first = next(v for v in own if v is not None)
            last = first
            for j, v in enumerate(own):
                last = last if v is None else v
                out[q, row, j] = last
        out[q, 0] = order
    return out


def _gather_proj(h1, shard, order):
    rows = SHARD_ROWS // W_CHUNKS

    def body(ord_ref, a_ref, sh_ref, main_ref, slab_ref, full_ref, wbuf, fetch_sems, send_sems, recv_sems,
             local_sems):
        j = pl.program_id(0)
        x, y, c = _mesh_pos()
        me, sibling = (x, y, c), (x, y, 1 - c)
        chips = [(1 - x, y), (x, 1 - y), (1 - x, 1 - y)]

        def block(p, owner):
            return full_ref.at[pl.ds(pl.multiple_of(_slot(owner) * SHARD_ROWS + p * rows, 16), rows)]

        def copy(p, k, owner, to, from_input=False):
            dst = block(p, owner)
            return pltpu.make_async_remote_copy(
                src_ref=sh_ref.at[pl.ds(p * rows, rows)] if from_input else dst, dst_ref=dst,
                send_sem=send_sems.at[7 * p + k], recv_sem=recv_sems.at[7 * p + k], device_id=to, device_id_type=MESH)

        pieces = range(W_CHUNKS)
        mine = [pltpu.make_async_copy(sh_ref.at[pl.ds(p * rows, rows)], block(p, me), local_sems.at[p]) for p in pieces]
        first = [copy(p, 0, me, sibling, from_input=True) for p in pieces]
        first += [copy(p, 1 + n, me, (*chips[n], c), from_input=True) for p in pieces for n in range(2)]
        near_pass = [copy(p, 4 + n, (*chips[n], c), sibling) for p in pieces for n in range(2)]
        relay = [copy(p, 3, ((x + 1 - c) % 2, (y + c) % 2, c), ((x + c) % 2, (y + 1 - c) % 2, c)) for p in pieces]
        far_pass = [copy(p, 6, (*chips[2], c), sibling) for p in pieces]

        def fetch(pos):
            slot = lax.rem(pos, 2)
            start = pl.multiple_of(ord_ref[0, pos] * PROJ_TN, PROJ_TN)
            return pltpu.make_async_copy(full_ref.at[pl.ds(start, PROJ_TN)], wbuf.at[slot], fetch_sems.at[slot])

        @pl.when(j == 0)
        def _():
            for cp in mine + first:
                cp.start()
            for cp in mine:
                cp.wait()
            for p in pieces:
                copy(p, 0, sibling, me).wait_recv()
            fetch(j).start()

        @pl.when(j == N_OWN - 1)
        def _():
            for p in pieces:
                for n in range(2):
                    copy(p, 1 + n, (*chips[n], c), me).wait_recv()
                    near_pass[2 * p + n].start()
                relay[p].start()
            for p in pieces:
                for n in range(2):
                    copy(p, 4 + n, (*chips[n], 1 - c), me).wait_recv()

        @pl.when(j == N_NEAR - 1)
        def _():
            for p in pieces:
                copy(p, 3, (*chips[2], c), me).wait_recv()
                far_pass[p].start()
            for p in pieces:
                copy(p, 6, (*chips[2], 1 - c), me).wait_recv()

        @pl.when(j + 1 < PROJ_TILES)
        def _():
            fetch(j + 1).start()

        fetch(j).wait()
        w_ref = wbuf.at[lax.rem(j, 2)]
        tile = ord_ref[0, j]
        is_att = (tile >= ATT_T0) & (tile < ATT_T1)
        chunks = [pl.ds(r * 512, 512) for r in range(S // 512)]

        @pl.when(jnp.logical_not(is_att))
        def _():
            for rws in chunks:
                main_ref[rws, :] = _dot(a_ref[rws, :], w_ref[...], NT)

        @pl.when(is_att)
        def _():
            for rws in chunks:
                p = _dot(a_ref[rws, :], w_ref[...], NT)
                for h in range(4):
                    slab_ref[h, rws, :] = p[:, h * 128:(h + 1) * 128]

        @pl.when(j == PROJ_TILES - 1)
        def _():
            for cp in first + near_pass + relay + far_pass:
                cp.wait_send()

    gs = pltpu.PrefetchScalarGridSpec(
        num_scalar_prefetch=1, grid=(PROJ_TILES,),
        in_specs=[pl.BlockSpec((S, D), lambda j, o: (0, 0)), HBM_SPEC],
        out_specs=(pl.BlockSpec((S, PROJ_TN), lambda j, o: (0, o[1, j])),
                   pl.BlockSpec((4, S, 128), lambda j, o: (o[2, j], 0, 0)), HBM_SPEC),
        scratch_shapes=[pltpu.VMEM((2, PROJ_TN, D), BF16), pltpu.SemaphoreType.DMA((2,)),
                        pltpu.SemaphoreType.DMA((7 * W_CHUNKS,)), pltpu.SemaphoreType.DMA((7 * W_CHUNKS,)),
                        pltpu.SemaphoreType.DMA((W_CHUNKS,))])
    return _pcall(body, name="gather_proj", grid_spec=gs,
                  out_shape=(jax.ShapeDtypeStruct((S, MAIN_COLS), F32), jax.ShapeDtypeStruct((N_SLABS, S, 128), F32),
                             jax.ShapeDtypeStruct((IN_COLS, D), BF16)),
                  compiler_params=_params(("arbitrary",)))(order, h1, shard)


TR = 256


def _row_spec(w=D):
    return pl.BlockSpec((TR, w), lambda i: (i, 0))


def _vec_spec(w=D):
    return pl.BlockSpec((1, w), lambda i: (0, 0))


def _norm_mod_fwd(x, g, sh, sc, name):
    def body(x_ref, g_ref, sh_ref, sc_ref, o_ref):
        xv = x_ref[...]
        rstd = lax.rsqrt(jnp.mean(xv * xv, axis=-1, keepdims=True) + RMS_EPS)
        n = xv * rstd * g_ref[...]
        o_ref[...] = (n * (1.0 + sc_ref[...]) + sh_ref[...]).astype(BF16)

    return _pcall(body, name=name, grid=(S // TR,), in_specs=[_row_spec(), _vec_spec(), _vec_spec(), _vec_spec()],
                  out_specs=_row_spec(), out_shape=jax.ShapeDtypeStruct((S, D), BF16),
                  compiler_params=_params(("parallel",)))(x, g, sh, sc)


def _norm_mod_bwd(x, g, sc, dh, dres, name, gate=None):
    gated = gate is not None

    def body(x_ref, g_ref, sc_ref, dh_ref, dres_ref, *rest):
        if gated:
            f_ref, gv_ref, dx_ref, dsc_ref, dsh_ref, dg_ref, dz_ref, dgv_ref = rest
        else:
            dx_ref, dsc_ref, dsh_ref, dg_ref = rest
        i = pl.program_id(0)
        xv = x_ref[...]
        dh = dh_ref[...]
        rstd = lax.rsqrt(jnp.mean(xv * xv, axis=-1, keepdims=True) + RMS_EPS)
        xhat = xv * rstd
        gv = g_ref[...]
        dn = dh * (1.0 + sc_ref[...])
        dxhat = dn * gv
        dx = dres_ref[...] + rstd * (dxhat - xhat * jnp.mean(dxhat * xhat, axis=-1, keepdims=True))
        dx_ref[...] = dx
        sums = [(dsc_ref, jnp.sum(dh * (xhat * gv), axis=0, keepdims=True)),
                (dsh_ref, jnp.sum(dh, axis=0, keepdims=True)),
                (dg_ref, jnp.sum(dn * xhat, axis=0, keepdims=True))]
        if gated:
            dz_ref[...] = (dx * gv_ref[...]).astype(BF16)
            sums.append((dgv_ref, jnp.sum(dx * f_ref[...], axis=0, keepdims=True)))

        @pl.when(i == 0)
        def _():
            for ref, p in sums:
                ref[...] = p

        @pl.when(i > 0)
        def _():
            for ref, p in sums:
                ref[...] += p

    vec = jax.ShapeDtypeStruct((1, D), F32)
    in_specs = [_row_spec(), _vec_spec(), _vec_spec(), _row_spec(), _row_spec()]
    out_specs = [_row_spec(), _vec_spec(), _vec_spec(), _vec_spec()]
    out_shape = [jax.ShapeDtypeStruct((S, D), F32), vec, vec, vec]
    args = [x, g, sc, dh, dres]
    if gated:
        in_specs += [_row_spec(), _vec_spec()]
        out_specs += [_row_spec(), _vec_spec()]
        out_shape += [jax.ShapeDtypeStruct((S, D), BF16), vec]
        args += list(gate)
    return _pcall(body, name=name, grid=(S // TR,), in_specs=in_specs, out_specs=tuple(out_specs),
                  out_shape=tuple(out_shape), compiler_params=_params(("arbitrary",)))(*args)


def _w_o_norm2(merged, w_o, x, g1, g, sh, sc):
    def body(a_ref, b_ref, x_ref, g1_ref, g_ref, sh_ref, sc_ref, o_ref, x1_ref, h_ref):
        acc = _dot(a_ref[...], b_ref[...], NN)
        o_ref[...] = acc
        xv = x_ref[...] + g1_ref[...] * acc
        x1_ref[...] = xv
        rstd = lax.rsqrt(jnp.mean(xv * xv, axis=-1, keepdims=True) + RMS_EPS)
        h_ref[...] = (xv * rstd * g_ref[...] * (1.0 + sc_ref[...]) + sh_ref[...]).astype(BF16)

    rows = pl.BlockSpec((FF2_TM, D), lambda i: (i, 0))
    f32 = jax.ShapeDtypeStruct((S, D), F32)
    return _pcall(body, name="w_o_norm2", grid=(S // FF2_TM,),
                  in_specs=[rows, pl.BlockSpec((D, D), lambda i: (0, 0)), rows] + [_vec_spec()] * 4,
                  out_specs=(rows, rows, rows), out_shape=(f32, f32, jax.ShapeDtypeStruct((S, D), BF16)),
                  compiler_params=_params(("parallel",)))(merged, w_o, x, g1, g, sh, sc)


FF2_TM = 512


def _ff2_final(act, w_ff2, x1, g2, tgt, g):
    def body(a_ref, b_ref, x1_ref, g2_ref, t_ref, g_ref, loss_ref, dx_ref, dg_ref, df_ref, dg2_ref):
        i = pl.program_id(0)
        f = _dot(a_ref[...], b_ref[...], NN)
        g2v = g2_ref[...]
        xv = x1_ref[...] + g2v * f
        gv = g_ref[...]
        rstd = lax.rsqrt(jnp.mean(xv * xv, axis=-1, keepdims=True) + RMS_EPS)
        xhat = xv * rstd
        err = xhat * gv - t_ref[...]
        dy = err * (1.0 / D)
        dxhat = dy * gv
        dx = rstd * (dxhat - xhat * jnp.mean(dxhat * xhat, axis=-1, keepdims=True))
        dx_ref[...] = dx
        df_ref[...] = (dx * g2v).astype(BF16)
        p_g = jnp.sum(dy * xhat, axis=0, keepdims=True)
        p_g2 = jnp.sum(dx * f, axis=0, keepdims=True)
        p_l = jnp.zeros((1, 128), F32) + 0.5 * jnp.sum(jnp.mean(err * err, axis=-1, keepdims=True))

        @pl.when(i == 0)
        def _():
            dg_ref[...] = p_g
            dg2_ref[...] = p_g2
            loss_ref[...] = p_l

        @pl.when(i > 0)
        def _():
            dg_ref[...] += p_g
            dg2_ref[...] += p_g2
            loss_ref[...] += p_l

    vec = jax.ShapeDtypeStruct((1, D), F32)
    rows = lambda w: pl.BlockSpec((FF2_TM, w), lambda i: (i, 0))
    return _pcall(body, name="ff2_final", grid=(S // FF2_TM,),
                  in_specs=[rows(D_FF), pl.BlockSpec((D_FF, D), lambda i: (0, 0)), rows(D), _vec_spec(), rows(D),
                            _vec_spec()],
                  out_specs=(_vec_spec(128), rows(D), _vec_spec(), rows(D), _vec_spec()),
                  out_shape=(jax.ShapeDtypeStruct((1, 128), F32), jax.ShapeDtypeStruct((S, D), F32), vec,
                             jax.ShapeDtypeStruct((S, D), BF16), vec),
                  compiler_params=_params(("arbitrary",)))(act, w_ff2, x1, g2, tgt, g)


HALF = 512


MERGE_TM = 1024


def _merge_specs():
    blk = lambda off: pl.BlockSpec((MERGE_TM, HALF), lambda i, j: (i, off // HALF + j))
    return blk(OFF_GA), blk(OFF_GB), blk(0)


def _att_out_merge(att, w_att_out, proj, ret_out):
    def body(a_ref, b_ref, ga_ref, gb_ref, r_ref, o_ref, m_ref):
        acc = _dot(a_ref[...], b_ref[...], NN)
        o_ref[...] = acc
        m_ref[...] = (jax.nn.sigmoid(ga_ref[...]) * r_ref[...] + jax.nn.sigmoid(gb_ref[...]) * acc).astype(BF16)

    ga, gb, tile = _merge_specs()
    return _pcall(body, name="att_out", grid=(S // MERGE_TM, D // HALF),
                  in_specs=[pl.BlockSpec((MERGE_TM, AW), lambda i, j: (i, 0)), pl.BlockSpec((AW, HALF), lambda i, j: (0, j)),
                            ga, gb, tile],
                  out_specs=(tile, tile),
                  out_shape=(jax.ShapeDtypeStruct((S, D), F32), jax.ShapeDtypeStruct((S, D), BF16)),
                  compiler_params=_params(("parallel", "parallel")))(att, w_att_out, proj, proj, ret_out)


def _dmerged_split(dmixo, w_o, proj, ret_out, att_out):
    def body(a_ref, b_ref, ga_ref, gb_ref, r_ref, at_ref, dr_ref, da_ref, dga_ref, dgb_ref):
        dm = _dot(a_ref[...], b_ref[...], NT)
        sa = jax.nn.sigmoid(ga_ref[...])
        sb = jax.nn.sigmoid(gb_ref[...])
        dr_ref[...] = (dm * sa).astype(BF16)
        da_ref[...] = (dm * sb).astype(BF16)
        dga_ref[...] = (dm * r_ref[...] * (sa * (1.0 - sa))).astype(BF16)
        dgb_ref[...] = (dm * at_ref[...] * (sb * (1.0 - sb))).astype(BF16)

    ga, gb, tile = _merge_specs()
    o = jax.ShapeDtypeStruct((S, D), BF16)
    return _pcall(body, name="dmerged", grid=(S // MERGE_TM, D // HALF),
                  in_specs=[pl.BlockSpec((MERGE_TM, D), lambda i, j: (i, 0)), pl.BlockSpec((HALF, D), lambda i, j: (j, 0)),
                            ga, gb, tile, tile],
                  out_specs=(tile,) * 4, out_shape=(o, o, o, o),
                  compiler_params=_params(("parallel", "parallel")))(dmixo, w_o, proj, proj, ret_out, att_out)


def _ret_tables():
    H, C = RET_HEADS, CHUNK
    log_g = jnp.log1p(-(2.0 ** (-5.0 - jnp.arange(H, dtype=F32))))
    idx = jnp.arange(C, dtype=F32)
    rel = idx[:, None] - idx[None, :]
    inner = jnp.where(rel >= 0, jnp.exp(log_g[:, None, None] * jnp.maximum(rel, 0.0)), 0.0)
    qd = jnp.exp(log_g[:, None] * (idx + 1.0))[:, :, None]
    kd = jnp.exp(log_g[:, None] * (C - 1.0 - idx))[:, :, None]
    cd = jnp.broadcast_to(jnp.exp(log_g * C)[:, None, None], (H, 1, 128))
    half = RET_DK // 2
    inv = 10000.0 ** (-jnp.arange(half, dtype=F32) / half)
    ang = jnp.arange(S, dtype=F32)[:, None] * inv[None, :]
    return inner, qd, kd, cd, jnp.cos(ang), jnp.sin(ang)


def _rot(x, cos, sin):
    x1, x2 = x[:, :128], x[:, 128:]
    return jnp.concatenate([x1 * cos - x2 * sin, x1 * sin + x2 * cos], axis=1)


def _rot_t(d, cos, sin):
    d1, d2 = d[:, :128], d[:, 128:]
    return jnp.concatenate([d1 * cos + d2 * sin, d2 * cos - d1 * sin], axis=1)


RET_COLS = OFF_ATT
RET_VW = RET_HEADS * RET_DV


def _ret_specs(chunk_of):
    ci = chunk_of
    whole = lambda shape: pl.BlockSpec(shape, lambda t: (0,) * len(shape))
    return [
        pl.BlockSpec((CHUNK, RET_COLS), lambda t: (ci(t), 0)),
        pl.BlockSpec((CHUNK, 128), lambda t: (ci(t), 0)),
        pl.BlockSpec((CHUNK, 128), lambda t: (ci(t), 0)),
        whole((RET_HEADS, CHUNK, CHUNK)), whole((RET_HEADS, CHUNK, 1)), whole((RET_HEADS, CHUNK, 1)),
        whole((RET_HEADS, 1, 128)), whole((1, RET_VW)), whole((1, RET_VW)),
    ]


def _ret_cols(h):
    q = slice(OFF_RQ + h * RET_DK, OFF_RQ + (h + 1) * RET_DK)
    k = slice(OFF_RK + h * RET_DK, OFF_RK + (h + 1) * RET_DK)
    v = slice(OFF_RV + h * RET_DV, OFF_RV + (h + 1) * RET_DV)
    g = slice(OFF_RG + h * RET_DV, OFF_RG + (h + 1) * RET_DV)
    return q, k, v, g, slice(h * RET_DV, (h + 1) * RET_DV)


def _ret_fwd(proj, tables, gn_g, gn_b, after):
    inner, qd, kd, cd, cos, sin = tables

    def body(x_ref, cos_ref, sin_ref, in_ref, qd_ref, kd_ref, cd_ref, g_ref, b_ref, after_ref,
             gated_ref, ro_ref, st_ref, s_scr):
        i = pl.program_id(0)

        @pl.when(i == 0)
        def _():
            s_scr[...] = jnp.zeros_like(s_scr)

        cosv, sinv = cos_ref[...], sin_ref[...]
        for h in range(RET_HEADS):
            cq, ck, cv, cg, co = _ret_cols(h)
            q = _rot(x_ref[:, cq], cosv, sinv)
            k = _rot(x_ref[:, ck], cosv, sinv) * (RET_DK ** -0.5)
            v = x_ref[:, cv]
            st = s_scr[h]
            st_ref[h] = st.astype(BF16)
            s = _dot(q, k, NT) * in_ref[h]
            o = _dot(s, v, NN) + _dot(q, st, NN) * qd_ref[h]
            s_scr[h] = st * cd_ref[h, :, :1] + _dot(k * kd_ref[h], v, TN)
            ro_ref[:, co] = o
            mu = jnp.mean(o, axis=-1, keepdims=True)
            oc = o - mu
            var = jnp.mean(oc * oc, axis=-1, keepdims=True)
            rn = oc * lax.rsqrt(var + GN_EPS) * g_ref[:, co] + b_ref[:, co]
            rg = x_ref[:, cg]
            gated_ref[:, co] = (rg * jax.nn.sigmoid(rg) * rn).astype(BF16)

    ospec = pl.BlockSpec((CHUNK, RET_VW), lambda t: (t, 0))
    return _pcall(
        body, name="ret_fwd", grid=(N_CHUNK,), in_specs=_ret_specs(lambda t: t) + [HBM_SPEC],
        out_specs=(ospec, ospec, pl.BlockSpec((RET_HEADS, None, RET_DK, RET_DV), lambda t: (0, t, 0, 0))),
        out_shape=(jax.ShapeDtypeStruct((S, RET_VW), BF16), jax.ShapeDtypeStruct((S, RET_VW), F32),
                   jax.ShapeDtypeStruct((RET_HEADS, N_CHUNK, RET_DK, RET_DV), BF16)),
        scratch_shapes=[pltpu.VMEM((RET_HEADS, RET_DK, RET_DV), F32)],
        compiler_params=_params(("arbitrary",)))(proj, cos, sin, inner, qd, kd, cd, gn_g, gn_b, after)


def _ret_bwd(proj, tables, gn_g, gn_b, ro, states, dgated, comm=None):
    inner, qd, kd, cd, cos, sin = tables
    last = N_CHUNK - 1

    def body(x_ref, cos_ref, sin_ref, in_ref, qd_ref, kd_ref, cd_ref, g_ref, b_ref, ro_ref, st_ref, dg_ref,
             dx_ref, gg_ref, gb_ref, gs_scr):
        t = pl.program_id(0)

        @pl.when(t == 0)
        def _():
            gs_scr[...] = jnp.zeros_like(gs_scr)
            gg_ref[...] = jnp.zeros_like(gg_ref)
            gb_ref[...] = jnp.zeros_like(gb_ref)

        cosv, sinv = cos_ref[...], sin_ref[...]
        for h in range(RET_HEADS):
            cq, ck, cv, cg, co = _ret_cols(h)
            q = _rot(x_ref[:, cq], cosv, sinv)
            k = _rot(x_ref[:, ck], cosv, sinv) * (RET_DK ** -0.5)
            v = x_ref[:, cv]
            qdv, kdv, dm = qd_ref[h], kd_ref[h], in_ref[h]
            st = st_ref[h]
            o = ro_ref[:, co]
            gv = g_ref[:, co]
            mu = jnp.mean(o, axis=-1, keepdims=True)
            oc = o - mu
            rstd = lax.rsqrt(jnp.mean(oc * oc, axis=-1, keepdims=True) + GN_EPS)
            ohat = oc * rstd
            rn = ohat * gv + b_ref[:, co]
            rg = x_ref[:, cg]
            sg = jax.nn.sigmoid(rg)
            dgt = dg_ref[:, co]
            drn = dgt * (rg * sg)
            dx_ref[:, cg] = (dgt * rn * (sg * (1.0 + rg * (1.0 - sg)))).astype(BF16)
            gg_ref[:, co] += jnp.sum(drn * ohat, axis=0, keepdims=True)
            gb_ref[:, co] += jnp.sum(drn, axis=0, keepdims=True)
            dohat = drn * gv
            do = rstd * (dohat - jnp.mean(dohat, axis=-1, keepdims=True)
                         - ohat * jnp.mean(dohat * ohat, axis=-1, keepdims=True))
            gs = gs_scr[h]
            s = _dot(q, k, NT) * dm
            dsr = _dot(do, v, NT) * dm
            dq = _dot(dsr, k, NN) + _dot(do, st, NT) * qdv
            dk = _dot(dsr, q, TN) + _dot(v, gs, NT) * kdv
            dv = _dot(s, do, TN) + _dot(k * kdv, gs, NN)
            gs_scr[h] = gs * cd_ref[h, :, :1] + _dot(q * qdv, do, TN)
            dx_ref[:, cq] = _rot_t(dq, cosv, sinv).astype(BF16)
            dx_ref[:, ck] = (_rot_t(dk, cosv, sinv) * (RET_DK ** -0.5)).astype(BF16)
            dx_ref[:, cv] = dv.astype(BF16)

    rev = lambda t: last - t
    vblk = pl.BlockSpec((CHUNK, RET_VW), lambda t: (rev(t), 0))
    vspec = pl.BlockSpec((1, RET_VW), lambda t: (0, 0))
    kw = dict(name="ret_bwd", grid=(N_CHUNK,),
              in_specs=_ret_specs(rev) + [vblk, pl.BlockSpec((RET_HEADS, None, RET_DK, RET_DV),
                                                             lambda t: (0, rev(t), 0, 0)), vblk],
              out_specs=(pl.BlockSpec((CHUNK, RET_COLS), lambda t: (rev(t), 0)), vspec, vspec),
              out_shape=(jax.ShapeDtypeStruct((S, RET_COLS), BF16), jax.ShapeDtypeStruct((1, RET_VW), F32),
                         jax.ShapeDtypeStruct((1, RET_VW), F32)),
              scratch_shapes=[pltpu.VMEM((RET_HEADS, RET_DK, RET_DV), F32)])
    args = (proj, cos, sin, inner, qd, kd, cd, gn_g, gn_b, ro, states, dgated)
    if comm is not None:
        return _carry(body, comm, **kw)(*args)
    return _pcall(body, compiler_params=_params(("arbitrary",)), **kw)(*args)


def _bucket_tables():
    qi = np.arange(ATT_BLK)[:, None]
    kj = np.arange(2 * ATT_BLK)[None, :]
    m = ATT_BLK + qi - kj
    out = []
    for win, dil in ATT_GROUPS:
        w = win // dil
        dist = (np.clip(m, 0, w) * dil).astype(np.int32)
        max_exact = N_BUCKETS // 2
        d_f = np.maximum(dist, 1).astype(np.float32)
        large = max_exact + (np.log(d_f / np.float32(max_exact)) / np.float32(math.log(MAX_DIST / max_exact))
                             * np.float32(N_BUCKETS - max_exact)).astype(np.int32)
        large = np.minimum(large, N_BUCKETS - 1)
        out.append(np.where(dist < max_exact, dist, large).astype(np.int32))
    return np.stack(out)


def _bias_build(rel_bias, buckets, after):
    def body(tab_ref, bk_ref, after_ref, o_ref):
        hh = pl.program_id(0)
        bk = bk_ref[...]
        acc = jnp.zeros((ATT_BLK, 2 * ATT_BLK), F32)
        for b in range(N_BUCKETS):
            acc = jnp.where(bk == b, tab_ref[b, hh], acc)
        o_ref[...] = acc

    nh = len(ATT_GROUPS) * ATT_HG
    return _pcall(body, name="bias_build", grid=(nh,),
                  in_specs=[pl.BlockSpec(memory_space=pltpu.SMEM),
                            pl.BlockSpec((None, ATT_BLK, 2 * ATT_BLK), lambda hh: (hh // ATT_HG, 0, 0)), HBM_SPEC],
                  out_specs=pl.BlockSpec((None, ATT_BLK, 2 * ATT_BLK), lambda hh: (hh, 0, 0)),
                  out_shape=jax.ShapeDtypeStruct((nh, ATT_BLK, 2 * ATT_BLK), F32),
                  compiler_params=_params(("parallel",)))(rel_bias, buckets, after)


def _bias_grad(ds_sum, buckets):
    def body(ds_ref, bk_ref, o_ref):
        bk = bk_ref[...]
        ds = ds_ref[...]
        rows = lax.broadcasted_iota(jnp.int32, (N_BUCKETS, 128), 0)
        acc = jnp.zeros((N_BUCKETS, 128), F32)
        for b in range(N_BUCKETS):
            acc = jnp.where(rows == b, jnp.sum(jnp.where(bk == b, ds, 0.0)), acc)
        o_ref[...] = acc

    nh = len(ATT_GROUPS) * ATT_HG
    return _pcall(body, name="bias_grad", grid=(nh,),
                  in_specs=[pl.BlockSpec((None, ATT_BLK, 2 * ATT_BLK), lambda hh: (hh, 0, 0)),
                            pl.BlockSpec((None, ATT_BLK, 2 * ATT_BLK), lambda hh: (hh // ATT_HG, 0, 0))],
                  out_specs=pl.BlockSpec((None, N_BUCKETS, 128), lambda hh: (hh, 0, 0)),
                  out_shape=jax.ShapeDtypeStruct((nh, N_BUCKETS, 128), F32),
                  compiler_params=_params(("parallel",)))(ds_sum, buckets)


def _att_valid(n):
    qi = lax.broadcasted_iota(jnp.int32, (ATT_BLK, 2 * ATT_BLK), 0)
    kj = lax.broadcasted_iota(jnp.int32, (ATT_BLK, 2 * ATT_BLK), 1)
    m = ATT_BLK + qi - kj
    first_key = jnp.where(n > 0, 0, ATT_BLK)
    return (m >= 0) & (m <= ATT_BLK) & (kj >= first_key)


ATT_HP = (1, 2, 2)


def _att_geometry(gi):
    _, dil = ATT_GROUPS[gi]
    return dil, S // dil // ATT_BLK, ATT_HP[gi]


def _blk(dil, r, n):
    if dil == 1:
        return pl.ds(n * ATT_BLK, ATT_BLK)
    return pl.ds(r + n * ATT_BLK * dil, ATT_BLK, stride=dil)


def _slab_specs(gi):
    _, _, hp = _att_geometry(gi)
    per = ATT_HG // hp
    return [pl.BlockSpec((hp, S, ATT_DH), lambda g, r, part=part: ((3 * gi + part) * per + g, 0, 0))
            for part in range(3)]


def _head_specs(gi, count):
    _, _, hp = _att_geometry(gi)
    return [pl.BlockSpec((hp, S, ATT_DH), lambda g, r: (g, 0, 0))] * count


def _bias_spec(gi):
    _, _, hp = _att_geometry(gi)
    return pl.BlockSpec((hp, ATT_BLK, 2 * ATT_BLK), lambda g, r: (gi * (ATT_HG // hp) + g, 0, 0))


def _att_valid_first():
    qi = lax.broadcasted_iota(jnp.int32, (ATT_BLK, ATT_BLK), 0)
    kj = lax.broadcasted_iota(jnp.int32, (ATT_BLK, ATT_BLK), 1)
    return kj <= qi


def _att_fwd(slabs, bias, gi, comm=None):
    dil, nb, hp = _att_geometry(gi)
    scale = ATT_DH ** -0.5

    def body(q_ref, k_ref, v_ref, bias_ref, o_ref, l_ref):
        r = pl.program_id(1)
        for n in range(nb):
            cur = _blk(dil, r, n)
            valid = _att_valid(n) if n > 0 else _att_valid_first()
            for h in range(hp):
                if n > 0:
                    prev = _blk(dil, r, n - 1)
                    kk = jnp.concatenate([k_ref[h, prev, :], k_ref[h, cur, :]], axis=0)
                    vv = jnp.concatenate([v_ref[h, prev, :], v_ref[h, cur, :]], axis=0)
                    bias = bias_ref[h]
                else:
                    kk, vv, bias = k_ref[h, cur, :], v_ref[h, cur, :], bias_ref[h, :, pl.ds(ATT_BLK, ATT_BLK)]
                s = _dot(q_ref[h, cur, :], kk, NT) * scale + bias
                s = jnp.where(valid, s, -1e30)
                mx = jnp.max(s, axis=-1, keepdims=True)
                e = jnp.exp(s - mx)
                den = jnp.sum(e, axis=-1, keepdims=True)
                o_ref[h, cur, :] = _dot(e / den, vv, NN)
                l_ref[h, cur, :] = jnp.broadcast_to(mx + jnp.log(den), (ATT_BLK, ATT_DH))

    osh = jax.ShapeDtypeStruct((ATT_HG, S, ATT_DH), F32)
    kw = dict(name=f"att_fwd{gi}", grid=(ATT_HG // hp, dil), in_specs=_slab_specs(gi) + [_bias_spec(gi)],
              out_specs=tuple(_head_specs(gi, 2)), out_shape=(osh, osh))
    if comm is not None:
        return _carry(body, comm, **kw)(slabs, slabs, slabs, bias)
    return _pcall(body, compiler_params=_params(("parallel", "arbitrary")), **kw)(slabs, slabs, slabs, bias)


def _att_bwd(slabs, bias, o, lse, do, dlse, gi, comm=None):
    dil, nb, hp = _att_geometry(gi)
    per = ATT_HG // hp
    scale = ATT_DH ** -0.5
    wh = hp * ATT_DH
    wide = lambda t: jnp.concatenate([t, t], axis=1)

    def body(q_ref, k_ref, v_ref, bias_ref, o_ref, l_ref, do_ref, dl_ref, dq_ref, dk_ref, dv_ref, ds_ref):
        r = pl.program_id(1)

        @pl.when(r == 0)
        def _():
            ds_ref[...] = jnp.zeros_like(ds_ref)

        for h in range(hp):
            sl = slice(h * ATT_DH, (h + 1) * ATT_DH)
            carry_k = carry_v = None
            for n in range(nb):
                cur = _blk(dil, r, n)
                q = q_ref[h, cur, :]
                dov = do_ref[h, cur, :]
                delta = jnp.sum(dov * o_ref[h, cur, :], axis=-1, keepdims=True)
                out_rows = pl.ds(n * ATT_BLK, ATT_BLK)
                if n == 0:
                    own = pl.ds(ATT_BLK, ATT_BLK)
                    kk, vv = k_ref[h, cur, :], v_ref[h, cur, :]
                    s = _dot(q, kk, NT) * scale + bias_ref[h, :, own]
                    p = jnp.where(_att_valid_first(), jnp.exp(s - l_ref[h, cur, :]), 0.0)
                    ds = p * (_dot(dov, vv, NT) - delta + dl_ref[h, cur, :])
                    ds_ref[h, :, own] += ds
                    dq_ref[out_rows, sl] = (_dot(ds, kk, NN) * scale).astype(BF16)
                    carry_k, carry_v = _dot(ds, q, TN) * scale, _dot(p, dov, TN)
                    continue
                prev = _blk(dil, r, n - 1)
                kk = jnp.concatenate([k_ref[h, prev, :], k_ref[h, cur, :]], axis=0)
                vv = jnp.concatenate([v_ref[h, prev, :], v_ref[h, cur, :]], axis=0)
                s = _dot(q, kk, NT) * scale + bias_ref[h]
                p = jnp.where(_att_valid(n), jnp.exp(s - wide(l_ref[h, cur, :])), 0.0)
                dp = _dot(dov, vv, NT)
                ds = p * (dp - delta + wide(dl_ref[h, cur, :]))
                ds_ref[h] += ds
                dq_ref[out_rows, sl] = (_dot(ds, kk, NN) * scale).astype(BF16)
                dkk = _dot(ds, q, TN) * scale
                dvv = _dot(p, dov, TN)
                before = pl.ds((n - 1) * ATT_BLK, ATT_BLK)
                dk_ref[before, sl] = (carry_k + dkk[:ATT_BLK]).astype(BF16)
                dv_ref[before, sl] = (carry_v + dvv[:ATT_BLK]).astype(BF16)
                carry_k, carry_v = dkk[ATT_BLK:], dvv[ATT_BLK:]
            last = pl.ds((nb - 1) * ATT_BLK, ATT_BLK)
            dk_ref[last, sl] = carry_k.astype(BF16)
            dv_ref[last, sl] = carry_v.astype(BF16)

    out_spec = pl.BlockSpec((S // dil, wh), lambda g, r: (0, r * per + g))
    osh = jax.ShapeDtypeStruct((S // dil, dil * AW), BF16)
    kw = dict(name=f"att_bwd{gi}", grid=(per, dil), in_specs=_slab_specs(gi) + [_bias_spec(gi)] + _head_specs(gi, 4),
              out_specs=(out_spec, out_spec, out_spec,
                         pl.BlockSpec((hp, ATT_BLK, 2 * ATT_BLK), lambda g, r: (g, 0, 0))),
              out_shape=(osh, osh, osh, jax.ShapeDtypeStruct((ATT_HG, ATT_BLK, 2 * ATT_BLK), F32)))
    args = (slabs, slabs, slabs, bias, o, lse, do, dlse)
    if comm is not None:
        return _carry(body, comm, **kw)(*args)
    return _pcall(body, compiler_params=_params(("arbitrary", "arbitrary")), **kw)(*args)


AW = ATT_HG * ATT_DH


def _mix_weights(l0, l1, l2):
    mx = jnp.maximum(jnp.maximum(l0, l1), l2)
    e0, e1, e2 = jnp.exp(l0 - mx), jnp.exp(l1 - mx), jnp.exp(l2 - mx)
    den = e0 + e1 + e2
    return e0 / den, e1 / den, e2 / den


def _heads_spec():
    return pl.BlockSpec((ATT_HG, TR, ATT_DH), lambda i: (0, i, 0))


def _mix_fwd(os_, ls, comm=None):
    def body(o0, o1, o2, l0, l1, l2, att_ref):
        for h in range(ATT_HG):
            w0, w1, w2 = _mix_weights(l0[h], l1[h], l2[h])
            att_ref[:, h * ATT_DH:(h + 1) * ATT_DH] = (w0 * o0[h] + w1 * o1[h] + w2 * o2[h]).astype(BF16)

    kw = dict(name="mix_fwd", grid=(S // TR,), in_specs=[_heads_spec()] * 6, out_specs=_row_spec(AW),
              out_shape=jax.ShapeDtypeStruct((S, AW), BF16))
    if comm is not None:
        return _carry(body, comm, **kw)(*os_, *ls)
    return _pcall(body, compiler_params=_params(("parallel",)), **kw)(*os_, *ls)


def _mix_bwd(os_, ls, datt):
    def body(o0, o1, o2, l0, l1, l2, da_ref, d0, d1, d2, e0, e1, e2):
        for h in range(ATT_HG):
            ws = _mix_weights(l0[h], l1[h], l2[h])
            da = da_ref[:, h * ATT_DH:(h + 1) * ATT_DH]
            dws = []
            for o_ref, w, d_ref in zip((o0, o1, o2), ws, (d0, d1, d2)):
                d_ref[h] = w * da
                dws.append(jnp.broadcast_to(jnp.sum(da * o_ref[h], axis=-1, keepdims=True), (TR, ATT_DH)))
            tot = ws[0] * dws[0] + ws[1] * dws[1] + ws[2] * dws[2]
            for w, dw, e_ref in zip(ws, dws, (e0, e1, e2)):
                e_ref[h] = w * (dw - tot)

    o = jax.ShapeDtypeStruct((ATT_HG, S, ATT_DH), F32)
    return _pcall(body, name="mix_bwd", grid=(S // TR,), in_specs=[_heads_spec()] * 6 + [_row_spec(AW)],
                  out_specs=(_heads_spec(),) * 6, out_shape=(o,) * 6,
                  compiler_params=_params(("parallel",)))(*os_, *ls, datt)


def _ada_fwd(c_all, w_sh, b_sl):
    def body(c_ref, w_ref, b_ref, o_ref):
        cv = c_ref[...]
        o_ref[...] = _dot(cv * jax.nn.sigmoid(cv), w_ref[...], NN) + b_ref[...]

    return _pcall(body, name="ada_fwd", out_shape=jax.ShapeDtypeStruct((N_DEV, w_sh.shape[1]), F32),
                  compiler_params=_params())(c_all, w_sh, b_sl)


def _ada_bwd(c_all, dm_sl):
    def body(c_ref, d_ref, o_ref):
        cv = c_ref[...]
        o_ref[...] = _dot(cv * jax.nn.sigmoid(cv), d_ref[...], TN)

    return _pcall(body, name="ada_bwd", out_shape=jax.ShapeDtypeStruct((D, dm_sl.shape[1]), F32),
                  compiler_params=_params())(c_all, dm_sl)


N_MOD = 6


def _sum_small(gathered):
    n = len(gathered)

    def body(*refs):
        ins, (gb_ref, dm_ref), outs = refs[:n], refs[n:n + 2], refs[n + 2:]

        def total(r):
            acc = r[0]
            for e in range(1, N_DEV):
                acc = acc + r[e]
            return acc

        for i in range(N_MOD):
            cols = slice(i * D, (i + 1) * D)
            gb_ref[:, cols] = total(ins[i])
            for e in range(N_DEV):
                dm_ref[e:e + 1, cols] = ins[i][e]
        for r, o_ref in zip(ins[N_MOD:], outs):
            o_ref[...] = total(r)

    shapes = (jax.ShapeDtypeStruct((1, N_MOD * D), F32), jax.ShapeDtypeStruct((N_DEV, N_MOD * D), F32),
              *[jax.ShapeDtypeStruct(g.shape[1:], F32) for g in gathered[N_MOD:]])
    res = _pcall(body, name="sum_small", out_shape=shapes, compiler_params=_params())(*gathered)
    return res[0], res[1], res[2:]


def _row_tile(m, n):
    t = max(8, min(m, (1 << 19) // n // 8 * 8))
    while m % t:
        t -= 8
    return t


def _pair_sum(full, recv, sel, name, col_block=0):
    _, m, n = recv.shape
    t = _row_tile(m, n)

    def body(sel_ref, a_ref, b_ref, o_ref):
        o_ref[...] = (a_ref[...].astype(F32) + b_ref[...].astype(F32)).astype(o_ref.dtype)

    gs = pltpu.PrefetchScalarGridSpec(
        num_scalar_prefetch=1, grid=(4, m // t),
        in_specs=[pl.BlockSpec((None, None, t, n), lambda q, i, s: (q, s[0], i, col_block)),
                  pl.BlockSpec((None, t, n), lambda q, i, s: (q, i, 0))],
        out_specs=pl.BlockSpec((None, t, n), lambda q, i, s: (q, i, 0)))
    return _pcall(body, name=name, grid_spec=gs, out_shape=jax.ShapeDtypeStruct((4, m, n), full.dtype),
                  compiler_params=_params(("parallel", "parallel")))(sel, full, recv)


def _chip_sum(part, recv, sel, name):
    _, m, n = part.shape
    t = _row_tile(m, n)

    def body(sel_ref, a_ref, r_ref, o_ref):
        o_ref[...] = ((a_ref[...].astype(F32) + r_ref[0].astype(F32)) + r_ref[1].astype(F32)) + r_ref[2].astype(F32)

    gs = pltpu.PrefetchScalarGridSpec(
        num_scalar_prefetch=1, grid=(m // t,),
        in_specs=[pl.BlockSpec((None, t, n), lambda i, s: (s[0], i, 0)),
                  pl.BlockSpec((3, t, n), lambda i, s: (0, i, 0))],
        out_specs=pl.BlockSpec((t, n), lambda i, s: (i, 0)))
    return _pcall(body, name=name, grid_spec=gs, out_shape=jax.ShapeDtypeStruct((m, n), F32),
                  compiler_params=_params(("parallel",)))(sel, part, recv)


def _adamw_math(w, g, m, v):
    nm = ADAM_B1 * m + (1.0 - ADAM_B1) * g
    nv = ADAM_B2 * v + (1.0 - ADAM_B2) * (g * g)
    m_hat = nm / (1.0 - ADAM_B1 ** ADAM_STEP)
    v_hat = nv / (1.0 - ADAM_B2 ** ADAM_STEP)
    return -ADAM_LR * (m_hat / (jnp.sqrt(v_hat) + ADAM_EPS) + ADAM_WD * w), nm, nv


def _adamw(w, g, m, v, name):
    _, rows, cols = w.shape
    t = _row_tile(rows, cols)

    def body(w_ref, g_ref, m_ref, v_ref, d_ref, nm_ref, nv_ref):
        d_ref[...], nm_ref[...], nv_ref[...] = _adamw_math(w_ref[...], g_ref[...], m_ref[...], v_ref[...])

    spec3 = pl.BlockSpec((None, t, cols), lambda i: (0, i, 0))
    spec2 = pl.BlockSpec((t, cols), lambda i: (i, 0))
    o = jax.ShapeDtypeStruct(w.shape, F32)
    return _pcall(body, name=name, grid=(rows // t,), in_specs=[spec3, spec2, spec3, spec3], out_specs=(spec3,) * 3,
                  out_shape=(o, o, o), compiler_params=_params(("parallel",)))(w, g, m, v)


def _adamw_reduced1(w, m, v, part, recv, sel, name):
    _, rows, cols = w.shape
    t = _row_tile(rows, cols)

    def body(sel_ref, w_ref, m_ref, v_ref, p_ref, r_ref, g_ref, d_ref, nm_ref, nv_ref):
        g = ((p_ref[...].astype(F32) + r_ref[0].astype(F32)) + r_ref[1].astype(F32)) + r_ref[2].astype(F32)
        g_ref[...] = g
        d_ref[...], nm_ref[...], nv_ref[...] = _adamw_math(w_ref[...], g, m_ref[...], v_ref[...])

    wspec = pl.BlockSpec((None, t, cols), lambda i, s: (0, i, 0))
    gs = pltpu.PrefetchScalarGridSpec(
        num_scalar_prefetch=1, grid=(rows // t,),
        in_specs=[wspec, wspec, wspec, pl.BlockSpec((None, t, cols), lambda i, s: (s[0], i, 0)),
                  pl.BlockSpec((3, t, cols), lambda i, s: (0, i, 0))],
        out_specs=(wspec,) * 4)
    o = jax.ShapeDtypeStruct(w.shape, F32)
    return _pcall(body, name=name, grid_spec=gs, out_shape=(o, o, o, o),
                  compiler_params=_params(("parallel",)))(sel, w, m, v, part, recv)


def _adamw_reduced(w, m, v, parts, recvs, sel):
    _, rows, cols = w.shape
    half = cols // 2
    t = _row_tile(rows, half)

    def body(sel_ref, w_ref, m_ref, v_ref, pa_ref, pb_ref, ra_ref, rb_ref, g_ref, d_ref, nm_ref, nv_ref):
        total = lambda p_ref, r_ref: ((p_ref[...].astype(F32) + r_ref[0].astype(F32)) + r_ref[1].astype(F32)) \
            + r_ref[2].astype(F32)
        g = jnp.where(pl.program_id(1) == 0, total(pa_ref, ra_ref), total(pb_ref, rb_ref))
        g_ref[...] = g
        d_ref[...], nm_ref[...], nv_ref[...] = _adamw_math(w_ref[...], g, m_ref[...], v_ref[...])

    wspec = pl.BlockSpec((None, t, half), lambda i, j, s: (0, i, j))
    pspec = pl.BlockSpec((None, t, half), lambda i, j, s: (s[0], i, 0))
    rspec = pl.BlockSpec((3, t, half), lambda i, j, s: (0, i, 0))
    gs = pltpu.PrefetchScalarGridSpec(num_scalar_prefetch=1, grid=(rows // t, 2),
                                      in_specs=[wspec, wspec, wspec, pspec, pspec, rspec, rspec],
                                      out_specs=(wspec,) * 4)
    o = jax.ShapeDtypeStruct(w.shape, F32)
    return _pcall(body, name="adamw_w_in", grid_spec=gs, out_shape=(o, o, o, o),
                  compiler_params=_params(("parallel", "arbitrary")))(sel, w, m, v, *parts, *recvs)


def _adamw_small(ws, gs, ms, vs):
    n = len(ws)

    def body(*refs):
        for i in range(n):
            w_ref, g_ref, m_ref, v_ref = (refs[k * n + i] for k in range(4))
            d, nm, nv = _adamw_math(w_ref[...], g_ref[...], m_ref[...], v_ref[...])
            refs[4 * n + i][...] = d
            refs[5 * n + i][...] = nm
            refs[6 * n + i][...] = nv

    shapes = tuple(jax.ShapeDtypeStruct(w.shape, F32) for w in ws)
    res = _pcall(body, name="adamw_small", out_shape=shapes * 3, compiler_params=_params())(*ws, *gs, *ms, *vs)
    return res[:n], res[n:2 * n], res[2 * n:]


def _mesh_pos():
    return lax.axis_index("x"), lax.axis_index("y"), lax.axis_index("c")


class _Gather:
    def __init__(self, arrs):
        self.ins = list(arrs)
        self.out_shape = tuple(jax.ShapeDtypeStruct((N_DEV,) + a.shape, a.dtype) for a in arrs)
        n = len(arrs)
        self.sems = [pltpu.SemaphoreType.DMA((7 * n,)), pltpu.SemaphoreType.DMA((7 * n,)),
                     pltpu.SemaphoreType.DMA((n,))]

    def _copies(self, ins, outs, sems):
        send_sems, recv_sems, local_sems = sems
        x, y, c = _mesh_pos()
        me, sibling = (x, y, c), (x, y, 1 - c)
        chips = [(1 - x, y), (x, 1 - y), (1 - x, 1 - y)]

        def copy(p, k, block, to, from_input=False):
            dst = outs[p].at[_slot(block)]
            return pltpu.make_async_remote_copy(
                src_ref=ins[p] if from_input else dst, dst_ref=dst, send_sem=send_sems.at[7 * p + k],
                recv_sem=recv_sems.at[7 * p + k], device_id=to, device_id_type=MESH)

        npc = len(self.ins)
        mine = [pltpu.make_async_copy(ins[p], outs[p].at[_slot(me)], local_sems.at[p]) for p in range(npc)]
        first = []
        for p in range(npc):
            first.append(copy(p, 0, me, sibling, from_input=True))
            first += [copy(p, 1 + j, me, (*chip, c), from_input=True) for j, chip in enumerate(chips)]
        return me, sibling, chips, c, copy, mine, first

    def start(self, ins, outs, sems):
        *_, mine, first = self._copies(ins, outs, sems)
        for cp in mine + first:
            cp.start()

    def finish(self, ins, outs, sems):
        me, sibling, chips, c, copy, mine, first = self._copies(ins, outs, sems)
        npc = len(self.ins)
        passed = []
        for p in range(npc):
            for j, chip in enumerate(chips):
                copy(p, 1 + j, (*chip, c), me).wait_recv()
                passed.append(copy(p, 4 + j, (*chip, c), sibling))
                passed[-1].start()
        for p in range(npc):
            copy(p, 0, sibling, me).wait_recv()
            for j, chip in enumerate(chips):
                copy(p, 4 + j, (*chip, 1 - c), me).wait_recv()
        for cp in first + passed:
            cp.wait_send()
        for cp in mine:
            cp.wait()


class _ExchangeCore:
    def __init__(self, fulls, cols=None):
        self.ins = list(fulls)
        self.cols = cols
        width = lambda f: f.shape[3] if cols is None else cols[1]
        self.out_shape = tuple(jax.ShapeDtypeStruct((4, f.shape[2], width(f)), f.dtype) for f in fulls)
        self.sems = [pltpu.SemaphoreType.DMA((4 * len(fulls),)), pltpu.SemaphoreType.DMA((4 * len(fulls),))]

    def _copies(self, ins, outs, sems):
        send_sems, recv_sems = sems
        x, y, c = _mesh_pos()

        def src(a, q):
            ref = ins[a].at[q, 1 - c]
            return ref if self.cols is None else ref.at[:, pl.ds(*self.cols)]

        return [pltpu.make_async_remote_copy(
            src_ref=src(a, q), dst_ref=outs[a].at[q], send_sem=send_sems.at[4 * a + q],
            recv_sem=recv_sems.at[4 * a + q], device_id=(x, y, 1 - c), device_id_type=MESH)
            for a in range(len(self.ins)) for q in range(4)]

    def start(self, ins, outs, sems):
        for cp in self._copies(ins, outs, sems):
            cp.start()

    def finish(self, ins, outs, sems):
        for cp in self._copies(ins, outs, sems):
            cp.wait()


class _ExchangeChip:
    def __init__(self, parts):
        self.ins = list(parts)
        self.out_shape = tuple(jax.ShapeDtypeStruct((3,) + p.shape[1:], p.dtype) for p in parts)
        self.sems = [pltpu.SemaphoreType.DMA((3 * len(parts),)), pltpu.SemaphoreType.DMA((3 * len(parts),))]

    def _copies(self, ins, outs, sems):
        send_sems, recv_sems = sems
        x, y, c = _mesh_pos()
        chips = [(1 - x, y), (x, 1 - y), (1 - x, 1 - y)]
        return [pltpu.make_async_remote_copy(
            src_ref=ins[a].at[2 * px + py], dst_ref=outs[a].at[j], send_sem=send_sems.at[3 * a + j],
            recv_sem=recv_sems.at[3 * a + j], device_id=(px, py, c), device_id_type=MESH)
            for a in range(len(self.ins)) for j, (px, py) in enumerate(chips)]

    def start(self, ins, outs, sems):
        for cp in self._copies(ins, outs, sems):
            cp.start()

    def finish(self, ins, outs, sems):
        for cp in self._copies(ins, outs, sems):
            cp.wait()


HBM_ONLY = pl.BlockSpec(memory_space=pltpu.HBM)
SEM_SPEC = pl.BlockSpec(memory_space=pltpu.SEMAPHORE)
SIDE_EFFECT = pltpu.SideEffectType.DATAFLOW_SIDE_EFFECTING


def _chip_copies(p_refs, land_refs, send_sems, recv_sems):
    x, y, c = _mesh_pos()
    return [pltpu.make_async_remote_copy(
        src_ref=p_refs[a].at[2 * px + py], dst_ref=land_refs[a].at[j], send_sem=send_sems.at[3 * a + j],
        recv_sem=recv_sems.at[3 * a + j], device_id=(px, py, c), device_id_type=MESH)
        for a in range(len(p_refs)) for j, (px, py) in enumerate([(1 - x, y), (x, 1 - y), (1 - x, 1 - y)])]


def _chip_exchange_start(parts, name):
    n = len(parts)
    lands = [lax.empty((3,) + p.shape[1:], p.dtype) for p in parts]

    def body(*refs):
        p_refs, land_refs, (send_sems, recv_sems) = refs[:n], refs[n:2 * n], refs[2 * n:2 * n + 2]
        for cp in _chip_copies(p_refs, land_refs, send_sems, recv_sems):
            cp.start()
        token = refs[-1]
        token[...] = jnp.zeros_like(token)

    hbm = lambda t: pltpu.HBM(t.shape, t.dtype)
    res = pl.pallas_call(
        body, name=name,
        out_shape=(pltpu.SemaphoreType.DMA((3 * n,)), pltpu.SemaphoreType.DMA((3 * n,)), *[hbm(t) for t in parts + lands],
                   jax.ShapeDtypeStruct((8, 128), F32)),
        in_specs=(HBM_ONLY,) * (2 * n),
        out_specs=(SEM_SPEC, SEM_SPEC, *[HBM_ONLY] * (2 * n), pl.BlockSpec(memory_space=pltpu.VMEM)),
        input_output_aliases={i: 2 + i for i in range(2 * n)},
        compiler_params=pltpu.CompilerParams(has_side_effects=SIDE_EFFECT))(
        *[pltpu.with_memory_space_constraint(t, pltpu.HBM) for t in parts + lands])
    return (res[0], res[1], list(res[2:2 + n]), list(res[2 + n:2 + 2 * n])), res[-1]


def _chip_exchange_wait(in_flight, after, name):
    send_sems, recv_sems, parts, lands = in_flight
    n = len(parts)

    def body(*refs):
        p_refs, land_refs, (send_sems, recv_sems) = refs[:n], refs[n:2 * n], refs[2 * n:2 * n + 2]
        for cp in _chip_copies(p_refs, land_refs, send_sems, recv_sems):
            cp.wait_send()
            cp.wait_recv()

    res = pl.pallas_call(
        body, name=name, out_shape=tuple(pltpu.HBM(t.shape, t.dtype) for t in parts + lands),
        in_specs=(*[HBM_ONLY] * (2 * n), SEM_SPEC, SEM_SPEC, pl.BlockSpec(memory_space=pl.ANY)),
        out_specs=(HBM_ONLY,) * (2 * n), input_output_aliases={i: i for i in range(2 * n)},
        compiler_params=pltpu.CompilerParams(has_side_effects=SIDE_EFFECT))(*parts, *lands, send_sems, recv_sems, after)
    return list(res[:n]), list(res[n:])


def _slot(p):
    return 4 * p[0] + 2 * p[1] + p[2]


def _gather_copies(src_refs, out_refs, send_sems, recv_sems):
    x, y, c = _mesh_pos()
    targets = [(x, y, 1 - c), (1 - x, y, c), (x, 1 - y, c), (1 - x, 1 - y, c)]
    return [pltpu.make_async_remote_copy(
        src_ref=src_refs[a], dst_ref=out_refs[a].at[_slot((x, y, c))], send_sem=send_sems.at[4 * a + k],
        recv_sem=recv_sems.at[4 * a + k], device_id=to, device_id_type=MESH)
        for a in range(len(src_refs)) for k, to in enumerate(targets)]


def _gather_start(shards, after, name):
    n = len(shards)
    outs = [lax.empty((N_DEV,) + s.shape, s.dtype) for s in shards]

    def body(*refs):
        for cp in _gather_copies(refs[:n], refs[n:2 * n], refs[2 * n + 1], refs[2 * n + 2]):
            cp.start()
        token = refs[-1]
        token[...] = jnp.zeros_like(token)

    res = pl.pallas_call(
        body, name=name,
        out_shape=(pltpu.SemaphoreType.DMA((4 * n,)), pltpu.SemaphoreType.DMA((4 * n,)),
                   *[pltpu.HBM(t.shape, t.dtype) for t in shards + outs], jax.ShapeDtypeStruct((8, 128), F32)),
        in_specs=(*[HBM_ONLY] * (2 * n), pl.BlockSpec(memory_space=pl.ANY)),
        out_specs=(SEM_SPEC, SEM_SPEC, *[HBM_ONLY] * (2 * n), pl.BlockSpec(memory_space=pltpu.VMEM)),
        input_output_aliases={i: 2 + i for i in range(2 * n)},
        compiler_params=pltpu.CompilerParams(has_side_effects=SIDE_EFFECT))(
        *[pltpu.with_memory_space_constraint(t, pltpu.HBM) for t in shards + outs], after)
    return (res[0], res[1], list(res[2:2 + n]), list(res[2 + n:2 + 2 * n])), res[-1]


def _gather_wait(in_flight, after, name):
    send_sems, recv_sems, shards, outs = in_flight
    n = len(shards)

    def body(*refs):
        for cp in _gather_copies(refs[:n], refs[n:2 * n], refs[2 * n], refs[2 * n + 1]):
            cp.wait_send()
            cp.wait_recv()

    res = pl.pallas_call(
        body, name=name, out_shape=tuple(pltpu.HBM(t.shape, t.dtype) for t in shards + outs),
        in_specs=(*[HBM_ONLY] * (2 * n), SEM_SPEC, SEM_SPEC, pl.BlockSpec(memory_space=pl.ANY)),
        out_specs=(HBM_ONLY,) * (2 * n), input_output_aliases={i: i for i in range(2 * n)},
        compiler_params=pltpu.CompilerParams(has_side_effects=SIDE_EFFECT))(*shards, *outs, send_sems, recv_sems, after)
    return list(res[:n]), list(res[n:])


class _PassToSibling:
    def __init__(self, shards, gathered):
        n = self.n = len(shards)
        self.ins = list(shards) + list(gathered)
        self.out_shape = tuple(jax.ShapeDtypeStruct(g.shape, g.dtype) for g in gathered)
        self.aliases = {n + a: a for a in range(n)}
        self.sems = [pltpu.SemaphoreType.DMA((3 * n,)), pltpu.SemaphoreType.DMA((3 * n,)),
                     pltpu.SemaphoreType.DMA((n,))]

    def _copies(self, ins, outs, sems):
        send_sems, recv_sems, local_sems = sems
        x, y, c = _mesh_pos()
        chips = [(1 - x, y), (x, 1 - y), (1 - x, 1 - y)]
        mine = [pltpu.make_async_copy(ins[a], outs[a].at[_slot((x, y, c))], local_sems.at[a]) for a in range(self.n)]
        passed, awaited = [], []
        for a in range(self.n):
            for j, chip in enumerate(chips):
                sems_j = dict(send_sem=send_sems.at[3 * a + j], recv_sem=recv_sems.at[3 * a + j],
                              device_id=(x, y, 1 - c), device_id_type=MESH)
                blk = outs[a].at[_slot((*chip, c))]
                passed.append(pltpu.make_async_remote_copy(src_ref=blk, dst_ref=blk, **sems_j))
                got = outs[a].at[_slot((*chip, 1 - c))]
                awaited.append(pltpu.make_async_remote_copy(src_ref=got, dst_ref=got, **sems_j))
        return mine, passed, awaited

    def start(self, ins, outs, sems):
        mine, passed, _ = self._copies(ins, outs, sems)
        for cp in mine + passed:
            cp.start()

    def finish(self, ins, outs, sems):
        mine, passed, awaited = self._copies(ins, outs, sems)
        for cp in passed:
            cp.wait_send()
        for cp in awaited:
            cp.wait_recv()
        for cp in mine:
            cp.wait()


def _reduce_sums(fulls, recv_core, core, tag):
    return [_pair_sum(f, r, core, f"rs_pair_{tag}{i}") for i, (f, r) in enumerate(zip(fulls, recv_core))]


def _local_step(x, tgt, mods, w_in_shard, order, shards, small, chip, core):
    sh1, sc1, g1, sh2, sc2, g2 = mods
    norm1_g, rel_bias, gn_g, gn_b, norm2_g, norm_f_g = small
    tables = _ret_tables()
    buckets = jnp.asarray(_bucket_tables())

    h1 = _norm_mod_fwd(x, norm1_g, sh1, sc1, "norm1_fwd")
    proj, slabs, w_in_t = _gather_proj(h1, w_in_shard, order)
    flight_w1, token_w = _gather_start(list(shards[:3]), proj, "gather_w1_start")
    flight_w2, token_w = _gather_start(list(shards[3:]), token_w, "gather_w2_start")
    bias = _bias_build(rel_bias, buckets, token_w)
    outs, lses = [], []
    for gi in range(len(ATT_GROUPS)):
        o, l = _att_fwd(slabs, bias, gi)
        outs.append(o)
        lses.append(l)
    att, gathered = _mix_fwd(outs, lses, comm=_PassToSibling(*_gather_wait(flight_w1, lses[2], "gather_w1_wait")))
    w_ret_out, w_att_out, w_o = (_from_slots(g, ax) for g, ax in zip(gathered, BIG_AXES[1:4]))
    gated, ro, states = _ret_fwd(proj, tables, gn_g, gn_b, att)
    ret_out, gathered = _mm(gated, w_ret_out, 'nn', tm=S, tn=256, tk=2048, name="ret_out",
                            comm=_PassToSibling(*_gather_wait(flight_w2, gated, "gather_w2_wait")))
    w_ff1, w_ff2 = (_from_slots(g, ax) for g, ax in zip(gathered, BIG_AXES[4:]))
    att_out, merged = _att_out_merge(att, w_att_out, proj, ret_out)
    mixo, x1, h2 = _w_o_norm2(merged, w_o, x, g1, norm2_g, sh2, sc2)
    u, act = _mm(h2, w_ff1, 'nn', tm=S, tn=512, tk=D, name="ff1", relu2=True)
    loss, dx2, g_normf, df, dg2 = _ff2_final(act, w_ff2, x1, g2, tgt, norm_f_g)

    gw_ff2 = _mm(act, df, 'tn', tm=512, tn=D, tk=S, name="gw_ff2", out_dtype=BF16)
    du = _mm(df, w_ff2, 'nt', tm=S, tn=512, tk=D, name="d_act", out_dtype=BF16, relu2_of=u)
    gw_ff1 = _mm(h2, du, 'tn', tm=D, tn=512, tk=S, name="gw_ff1", out_dtype=BF16)
    fulls_a = [_to_slots(g, ax) for g, ax in zip((gw_ff1, gw_ff2), BIG_AXES[4:])]
    dh2, recv_core_a = _mm(du, w_ff1, 'nt', tm=1024, tn=1024, tk=2048, name="dh2", comm=_ExchangeCore(fulls_a))
    parts_a = _reduce_sums(fulls_a, recv_core_a, core, "a")
    flight_a, token_a = _chip_exchange_start(parts_a, "rs_a_start")
    dx1, dsc2, dsh2, g_norm2, dmixo, dg1 = _norm_mod_bwd(x1, norm2_g, sc2, dh2, dx2, "norm2_bwd", gate=(mixo, g1))

    gw_o = _mm(merged, dmixo, 'tn', tm=D, tn=512, tk=S, name="gw_o", out_dtype=BF16, after=token_a)
    d_ret_out, d_att_out, dga, dgb = _dmerged_split(dmixo, w_o, proj, ret_out, att_out)
    gw_ret_out = _mm(gated, d_ret_out, 'tn', tm=512, tn=D, tk=S, name="gw_ret_out", out_dtype=BF16)
    gw_att_out = _mm(att, d_att_out, 'tn', tm=AW, tn=D, tk=S, name="gw_att_out", out_dtype=BF16)
    fulls_b = [_to_slots(g, ax) for g, ax in zip((gw_ret_out, gw_att_out, gw_o), BIG_AXES[1:4])]
    dgated, recv_core_b = _mm(d_ret_out, w_ret_out, 'nt', tm=S, tn=512, tk=D, name="dgated",
                              comm=_ExchangeCore(fulls_b))
    parts_b = _reduce_sums(fulls_b, recv_core_b, core, "b")
    flight_b, token_b = _chip_exchange_start(parts_b, "rs_b_start")
    datt = _mm(d_att_out, w_att_out, 'nt', tm=S, tn=AW, tk=D, name="datt", after=token_b)
    mix_grads = _mix_bwd(outs, lses, datt)
    datt_parts, ds_sums = [], []
    for gi in range(len(ATT_GROUPS)):
        dq, dk, dv, ds_sum = _att_bwd(slabs, bias, outs[gi], lses[gi], mix_grads[gi], mix_grads[3 + gi], gi)
        datt_parts += [dq.reshape(S, AW), dk.reshape(S, AW), dv.reshape(S, AW)]
        ds_sums.append(ds_sum)
    g_bias = _bias_grad(jnp.concatenate(ds_sums, axis=0), buckets)[:, :, 0].T.reshape(1, -1)
    dret, g_gn_g, g_gn_b = _ret_bwd(proj, tables, gn_g, gn_b, ro, states, dgated)
    parts_a, recv_chip_a = _chip_exchange_wait(flight_a, dret, "rs_a_wait")
    parts_b, recv_chip_b = _chip_exchange_wait(flight_b, dret, "rs_b_wait")
    reduced = list(zip(parts_b + parts_a, recv_chip_b + recv_chip_a))
    dproj = jnp.concatenate([dret] + datt_parts + [dga, dgb], axis=1)
    full_in = _to_slots(_mm(dproj, h1, 'tn', tm=512, tn=D, tk=S, name="gw_in", out_dtype=BF16), 0)
    in_flight, token = [], None
    for half in range(2):
        (recv_core_in,) = _run_comm(_ExchangeCore([full_in], cols=(half * (D // 2), D // 2)), f"rs_core_in{half}",
                                    after=token)
        part_in = [_pair_sum(full_in, recv_core_in, core, f"rs_pair_c{half}", col_block=half)]
        flight, token = _chip_exchange_start(part_in, f"rs_in{half}_start")
        in_flight.append(flight)
    dh1 = _mm(dproj, w_in_t, 'nn', tm=1024, tn=1024, tk=2560, name="dh1", after=token)
    gx, dsc1, dsh1, g_norm1 = _norm_mod_bwd(x, norm1_g, sc1, dh1, dx1, "norm1_bwd")

    dmod = [dsh1, dsc1, dg1, dsh2, dsc2, dg2]
    small_g = [g_norm1, g_bias, g_gn_g, g_gn_b, g_norm2, g_normf]
    return loss, gx, in_flight, reduced, small_g, dmod


def _to_slots(g, axis):
    if axis == 0:
        return g.reshape(4, 2, g.shape[0] // N_DEV, g.shape[1])
    return g.reshape(g.shape[0], N_DEV, g.shape[1] // N_DEV).transpose(1, 0, 2).reshape(4, 2, g.shape[0], -1)


def _from_slots(w8, axis):
    if axis == 0:
        return w8.reshape(-1, w8.shape[2])
    return w8.transpose(1, 0, 2).reshape(w8.shape[1], -1)


BIG_AXES = (1, 0, 1, 0, 1, 0)


def kernel(x, c, w_ada, b_ada, norm1_g, w_in, rel_bias, ret_gn_g, ret_gn_b, w_ret_out, w_att_out, w_o, norm2_g, w_ff1, w_ff2, norm_f_g, loss_target, m_w_ada, m_b_ada, m_norm1_g, m_w_in, m_rel_bias, m_ret_gn_g, m_ret_gn_b, m_w_ret_out, m_w_att_out, m_w_o, m_norm2_g, m_w_ff1, m_w_ff2, m_norm_f_g, v_w_ada, v_b_ada, v_norm1_g, v_w_in, v_rel_bias, v_ret_gn_g, v_ret_gn_b, v_w_ret_out, v_w_att_out, v_w_o, v_norm2_g, v_w_ff1, v_w_ff2, v_norm_f_g):
    mx, my, mc = _mesh_pos()
    dev = 4 * mx + 2 * my + mc
    chip = jnp.reshape(2 * mx + my, (1,)).astype(jnp.int32)
    core = jnp.reshape(mc, (1,)).astype(jnp.int32)
    ada_w = D * 6 // N_DEV

    w_in, m_w_in, v_w_in = (jnp.transpose(t, (0, 2, 1)) for t in (w_in, m_w_in, v_w_in))

    shards = [w[0].astype(BF16) for w in (w_in, w_ret_out, w_att_out, w_o, w_ff1, w_ff2)]
    (c_all,) = _run_comm(_Gather([c]), "gather_c")
    c_all = c_all.reshape(N_DEV, D)
    b_sl = lax.dynamic_slice(b_ada, (0, dev * ada_w), (1, ada_w))
    (mod_all,) = _run_comm(_Gather([_ada_fwd(c_all, w_ada[0], b_sl)]), "gather_mod")
    mod = lax.dynamic_index_in_dim(mod_all, dev, axis=1, keepdims=False).reshape(6, D)
    mods = tuple(mod[i:i + 1] for i in range(6))

    small = (norm1_g, rel_bias, ret_gn_g, ret_gn_b, norm2_g, norm_f_g.reshape(1, D))
    order = lax.dynamic_index_in_dim(jnp.asarray(_proj_order()), 2 * mx + my, axis=0, keepdims=False)
    loss, gx, in_flight, big_red, small_g, dmod = _local_step(x[0], loss_target[0], mods, shards[0], order,
                                                              shards[1:], small, chip, core)

    names = ['w_ada', 'b_ada', 'norm1_g', 'w_in', 'rel_bias', 'ret_gn_g', 'ret_gn_b', 'w_ret_out', 'w_att_out',
             'w_o', 'norm2_g', 'w_ff1', 'w_ff2', 'norm_f_g']
    ws = dict(zip(names, (w_ada, b_ada, norm1_g, w_in, rel_bias, ret_gn_g, ret_gn_b, w_ret_out, w_att_out, w_o,
                          norm2_g, w_ff1, w_ff2, norm_f_g)))
    ms = dict(zip(names, (m_w_ada, m_b_ada, m_norm1_g, m_w_in, m_rel_bias, m_ret_gn_g, m_ret_gn_b, m_w_ret_out,
                          m_w_att_out, m_w_o, m_norm2_g, m_w_ff1, m_w_ff2, m_norm_f_g)))
    vs = dict(zip(names, (v_w_ada, v_b_ada, v_norm1_g, v_w_in, v_rel_bias, v_ret_gn_g, v_ret_gn_b, v_w_ret_out,
                          v_w_att_out, v_w_o, v_norm2_g, v_w_ff1, v_w_ff2, v_norm_f_g)))
    grads, delta, new_m, new_v = {}, {}, {}, {}
    big_names = ('w_ret_out', 'w_att_out', 'w_o', 'w_ff1', 'w_ff2')
    for n, (part, recv) in zip(big_names, big_red):
        grads[n], delta[n], new_m[n], new_v[n] = _adamw_reduced1(ws[n], ms[n], vs[n], part, recv, chip, "adamw_" + n)
    updated = lax.optimization_barrier((gx, tuple(delta[n] for n in big_names)))
    gathered = _run_comm(_Gather(dmod + small_g + [loss]), "gather_small", after=updated[0])
    g_b_ada, dmod_all, (g_norm1, g_bias, g_gn_g, g_gn_b, g_norm2, g_normf, loss_sum) = _sum_small(gathered)
    loss_out = loss_sum[0, 0]
    g_w_ada = _ada_bwd(c_all, lax.dynamic_slice(dmod_all, (0, dev * ada_w), (N_DEV, ada_w)))

    grads.update(w_ada=g_w_ada.reshape(w_ada.shape), b_ada=g_b_ada, norm1_g=g_norm1, rel_bias=g_bias,
                 ret_gn_g=g_gn_g, ret_gn_b=g_gn_b, norm2_g=g_norm2, norm_f_g=g_normf)
    delta['w_ada'], new_m['w_ada'], new_v['w_ada'] = _adamw(w_ada, g_w_ada, m_w_ada, v_w_ada, "adamw_w_ada")
    small_names = ('b_ada', 'norm1_g', 'rel_bias', 'ret_gn_g', 'ret_gn_b', 'norm2_g', 'norm_f_g')
    two_d = {n: (1, ws[n].size) if ws[n].ndim == 1 else ws[n].shape for n in small_names}
    d_, m_, v_ = _adamw_small(*[[src[n].reshape(two_d[n]) for n in small_names] for src in (ws, grads, ms, vs)])
    for i, n in enumerate(small_names):
        shp = ws[n].shape
        delta[n], new_m[n], new_v[n] = d_[i].reshape(shp), m_[i].reshape(shp), v_[i].reshape(shp)
        grads[n] = grads[n].reshape(shp)

    done = lax.optimization_barrier((gx, tuple(d_), tuple(delta[n] for n in ('w_ada', 'w_ret_out', 'w_att_out', 'w_o',
                                                                               'w_ff1', 'w_ff2'))))
    parts_in, recvs_in = [], []
    for half, flight in enumerate(in_flight):
        (part_in,), (recv_chip_in,) = _chip_exchange_wait(flight, done[0], f"rs_in{half}_wait")
        parts_in.append(part_in)
        recvs_in.append(recv_chip_in)
    grads['w_in'], delta['w_in'], new_m['w_in'], new_v['w_in'] = _adamw_reduced(w_in, m_w_in, v_w_in, parts_in,
                                                                               recvs_in, chip)
    for d in (grads, delta, new_m, new_v):
        d['w_in'] = jnp.transpose(d['w_in'], (0, 2, 1))
    return (loss_out, gx[None], *[grads[n] for n in names], *[delta[n] for n in names],
            *[new_m[n] for n in names], *[new_v[n] for n in names])
```

```python
import functools
import math

import numpy as np
import jax
import jax.numpy as jnp
from jax import lax
from jax.experimental import pallas as pl
from jax.experimental.pallas import tpu as pltpu

F32 = jnp.float32
BF16 = jnp.bfloat16
MESH = pl.DeviceIdType.MESH

N_DEV = 8
S = 2048
D = 1024
RET_HEADS = 4
RET_DK = 256
RET_DV = 512
CHUNK = 128
N_CHUNK = S // CHUNK
ATT_GROUPS = ((128, 1), (512, 4), (2048, 16))
ATT_HG = 4
ATT_DH = 128
ATT_BLK = 128
N_BUCKETS = 32
MAX_DIST = 2048
D_FF = 4096
IN_COLS = 12800
OFF_RQ, OFF_RK, OFF_RV, OFF_RG, OFF_ATT = 0, 1024, 2048, 4096, 6144
OFF_GA, OFF_GB = 6144, 7168
RMS_EPS = 1e-6
GN_EPS = 1e-5
ADAM_LR, ADAM_B1, ADAM_B2, ADAM_EPS, ADAM_WD, ADAM_STEP = 0.001, 0.9, 0.999, 1e-08, 0.01, 10
VMEM_LIMIT = 48 * 1024 * 1024


def _pcall(body, **kw):
    return pl.pallas_call(body, **kw)


def _params(sem=None):
    return pltpu.CompilerParams(dimension_semantics=sem, vmem_limit_bytes=VMEM_LIMIT)


HBM_SPEC = pl.BlockSpec(memory_space=pl.ANY)


def _carry(body, comm, *, name, grid, in_specs, out_specs, out_shape, scratch_shapes=()):
    single = not isinstance(out_specs, (tuple, list))
    o_specs = (out_specs,) if single else tuple(out_specs)
    o_shape = (out_shape,) if single else tuple(out_shape)
    n_in, n_out, n_scr = len(in_specs), len(o_specs), len(scratch_shapes)
    nci, nco = len(comm.ins), len(comm.out_shape)
    total = int(np.prod(grid))

    def wrapped(*refs):
        bounds = np.cumsum([0, n_in, nci, n_out, nco, n_scr])
        a, ci, o, co, scr = (refs[bounds[i]:bounds[i + 1]] for i in range(5))
        sems = refs[bounds[5]:]
        flat = 0
        for d, g in enumerate(grid):
            flat = flat * g + pl.program_id(d)

        @pl.when(flat == 0)
        def _():
            comm.start(ci, co, sems)

        body(*a, *o, *scr)

        @pl.when(flat == total - 1)
        def _():
            comm.finish(ci, co, sems)

    aliases = {n_in + i: n_out + o for i, o in getattr(comm, "aliases", {}).items()}
    call = _pcall(wrapped, name=name, grid=grid, in_specs=list(in_specs) + [HBM_SPEC] * nci,
                  out_specs=o_specs + (HBM_SPEC,) * nco, out_shape=o_shape + tuple(comm.out_shape),
                  scratch_shapes=list(scratch_shapes) + list(comm.sems), input_output_aliases=aliases,
                  compiler_params=_params(("arbitrary",) * len(grid)))

    def run(*args):
        res = call(*args, *comm.ins)
        own = res[0] if single else tuple(res[:n_out])
        return own, tuple(res[n_out:])

    return run


def _run_comm(comm, name, after=None):
    nci, nco = len(comm.ins), len(comm.out_shape)
    extra = [] if after is None else [after]

    def body(*refs):
        ci, co, sems = refs[:nci], refs[nci + len(extra):nci + len(extra) + nco], refs[nci + len(extra) + nco:]
        comm.start(ci, co, sems)
        comm.finish(ci, co, sems)

    return _pcall(body, name=name, in_specs=[HBM_SPEC] * (nci + len(extra)), out_specs=(HBM_SPEC,) * nco,
                  out_shape=tuple(comm.out_shape), scratch_shapes=list(comm.sems))(*comm.ins, *extra)


def _dot(a, b, dn):
    return lax.dot_general(a.astype(BF16), b.astype(BF16), (dn, ((), ())), preferred_element_type=F32)


NN = ((1,), (0,))
NT = ((1,), (1,))
TN = ((0,), (0,))


def _mm(a, b, mode, *, tm, tn, tk, name, out_dtype=F32, res=None, gvec=None, relu2=False, relu2_of=None, comm=None,
        after=None):
    if mode == 'nn':
        (M, K), (_, N) = a.shape, b.shape
        a_spec = pl.BlockSpec((tm, tk), lambda i, j, k: (i, k))
        b_spec = pl.BlockSpec((tk, tn), lambda i, j, k: (k, j))
        dn = NN
    elif mode == 'nt':
        (M, K), (N, _) = a.shape, b.shape
        a_spec = pl.BlockSpec((tm, tk), lambda i, j, k: (i, k))
        b_spec = pl.BlockSpec((tn, tk), lambda i, j, k: (j, k))
        dn = NT
    else:
        (K, M), (_, N) = a.shape, b.shape
        a_spec = pl.BlockSpec((tk, tm), lambda i, j, k: (k, i))
        b_spec = pl.BlockSpec((tk, tn), lambda i, j, k: (k, j))
        dn = TN
    assert M % tm == 0 and N % tn == 0 and K % tk == 0, (name, M, N, K)
    nk = K // tk
    fused = res is not None
    o_spec = pl.BlockSpec((tm, tn), lambda i, j, k: (i, j))

    def body(a_ref, b_ref, *rest):
        acc_ref = rest[-1] if nk > 1 else None
        if after is not None:
            rest = rest[1:]
        if fused:
            res_ref, g_ref, o_ref, x_ref = rest[:4]
        elif relu2_of is not None:
            u_ref, o_ref = rest[:2]
        elif relu2:
            o_ref, act_ref = rest[:2]
        else:
            o_ref = rest[0]

        def finish(acc):
            if relu2_of is not None:
                acc = acc * (2.0 * jnp.maximum(u_ref[...], 0.0))
            o_ref[...] = acc.astype(o_ref.dtype)
            if fused:
                x_ref[...] = res_ref[...] + g_ref[...] * acc
            if relu2:
                r = jnp.maximum(acc, 0.0)
                act_ref[...] = (r * r).astype(BF16)

        p = _dot(a_ref[...], b_ref[...], dn)
        if nk == 1:
            finish(p)
        else:
            k = pl.program_id(2)

            @pl.when(k == 0)
            def _():
                acc_ref[...] = p

            @pl.when(k > 0)
            def _():
                acc_ref[...] += p

            @pl.when(k == nk - 1)
            def _():
                finish(acc_ref[...])

    in_specs = [a_spec, b_spec]
    args = [a, b]
    if after is not None:
        in_specs.append(pl.BlockSpec(memory_space=pl.ANY))
        args.append(after)
    out_shape = jax.ShapeDtypeStruct((M, N), out_dtype)
    out_specs = o_spec
    if fused:
        in_specs += [pl.BlockSpec((tm, tn), lambda i, j, k: (i, j)), pl.BlockSpec((1, tn), lambda i, j, k: (0, j))]
        args += [res, gvec]
        out_shape = (out_shape, jax.ShapeDtypeStruct((M, N), F32))
        out_specs = (o_spec, pl.BlockSpec((tm, tn), lambda i, j, k: (i, j)))
    elif relu2_of is not None:
        in_specs.append(pl.BlockSpec((tm, tn), lambda i, j, k: (i, j)))
        args.append(relu2_of)
    elif relu2:
        out_shape = (out_shape, jax.ShapeDtypeStruct((M, N), BF16))
        out_specs = (o_spec, pl.BlockSpec((tm, tn), lambda i, j, k: (i, j)))
    kw = dict(name=name, grid=(M // tm, N // tn, nk), in_specs=in_specs, out_specs=out_specs,
              out_shape=out_shape, scratch_shapes=[pltpu.VMEM((tm, tn), F32)] if nk > 1 else [])
    if comm is not None:
        return _carry(body, comm, **kw)(*args)
    return _pcall(body, compiler_params=_params(("parallel", "parallel", "arbitrary")), **kw)(*args)


PROJ_TN = 512
ATT_T0, ATT_T1 = 6144 // PROJ_TN, 10752 // PROJ_TN
N_SLABS = (ATT_T1 - ATT_T0) * 4
MAIN_COLS = IN_COLS - (ATT_T1 - ATT_T0) * PROJ_TN


PROJ_TILES = IN_COLS // PROJ_TN
SHARD_ROWS = IN_COLS // N_DEV
W_CHUNKS = 4
N_OWN, N_NEAR = 5, 18


def _proj_order():
    out = np.zeros((4, 3, PROJ_TILES), np.int32)
    for q in range(4):
        def hops(t):
            owners = {col // (2 * SHARD_ROWS) for col in (t * PROJ_TN, (t + 1) * PROJ_TN - 1)}
            return max(bin(q ^ p).count("1") for p in owners)
        order = sorted(range(PROJ_TILES), key=lambda t: (hops(t), t))
        assert all(hops(t) == 0 for t in order[:N_OWN]) and all(hops(t) < 2 for t in order[:N_NEAR])
        is_att = [ATT_T0 <= t < ATT_T1 for t in order]
        for row, kind, index in ((1, False, lambda t: t if t < ATT_T0 else t - (ATT_T1 - ATT_T0)),
                                 (2, True, lambda t: t - ATT_T0)):
            own = [index(t) if a == kind else None for t, a in zip(order, is_att)]
            first = next(v for v in own if v is not None)
            last = first
            for j, v in enumerate(own):
                last = last if v is None else v
                out[q, row, j] = last
        out[q, 0] = order
    return out


def _gather_proj(h1, shard, order):
    rows = SHARD_ROWS // W_CHUNKS

    def body(ord_ref, a_ref, sh_ref, main_ref, slab_ref, full_ref, wbuf, fetch_sems, send_sems, recv_sems,
             local_sems):
        j = pl.program_id(0)
        x, y, c = _mesh_pos()
        me, sibling = (x, y, c), (x, y, 1 - c)
        chips = [(1 - x, y), (x, 1 - y), (1 - x, 1 - y)]

        def block(p, owner):
            return full_ref.at[pl.ds(pl.multiple_of(_slot(owner) * SHARD_ROWS + p * rows, 16), rows)]

        def copy(p, k, owner, to, from_input=False):
            dst = block(p, owner)
            return pltpu.make_async_remote_copy(
                src_ref=sh_ref.at[pl.ds(p * rows, rows)] if from_input else dst, dst_ref=dst,
                send_sem=send_sems.at[7 * p + k], recv_sem=recv_sems.at[7 * p + k], device_id=to, device_id_type=MESH)

        pieces = range(W_CHUNKS)
        mine = [pltpu.make_async_copy(sh_ref.at[pl.ds(p * rows, rows)], block(p, me), local_sems.at[p]) for p in pieces]
        first = [copy(p, 0, me, sibling, from_input=True) for p in pieces]
        first += [copy(p, 1 + n, me, (*chips[n], c), from_input=True) for p in pieces for n in range(2)]
        near_pass = [copy(p, 4 + n, (*chips[n], c), sibling) for p in pieces for n in range(2)]
        relay = [copy(p, 3, ((x + 1 - c) % 2, (y + c) % 2, c), ((x + c) % 2, (y + 1 - c) % 2, c)) for p in pieces]
        far_pass = [copy(p, 6, (*chips[2], c), sibling) for p in pieces]

        def fetch(pos):
            slot = lax.rem(pos, 2)
            start = pl.multiple_of(ord_ref[0, pos] * PROJ_TN, PROJ_TN)
            return pltpu.make_async_copy(full_ref.at[pl.ds(start, PROJ_TN)], wbuf.at[slot], fetch_sems.at[slot])

        @pl.when(j == 0)
        def _():
            for cp in mine + first:
                cp.start()
            for cp in mine:
                cp.wait()
            for p in pieces:
                copy(p, 0, sibling, me).wait_recv()
            fetch(j).start()

        @pl.when(j == N_OWN - 1)
        def _():
            for p in pieces:
                for n in range(2):
                    copy(p, 1 + n, (*chips[n], c), me).wait_recv()
                    near_pass[2 * p + n].start()
                relay[p].start()
            for p in pieces:
                for n in range(2):
                    copy(p, 4 + n, (*chips[n], 1 - c), me).wait_recv()

        @pl.when(j == N_NEAR - 1)
        def _():
            for p in pieces:
                copy(p, 3, (*chips[2], c), me).wait_recv()
                far_pass[p].start()
            for p in pieces:
                copy(p, 6, (*chips[2], 1 - c), me).wait_recv()

        @pl.when(j + 1 < PROJ_TILES)
        def _():
            fetch(j + 1).start()

        fetch(j).wait()
        w_ref = wbuf.at[lax.rem(j, 2)]
        tile = ord_ref[0, j]
        is_att = (tile >= ATT_T0) & (tile < ATT_T1)
        chunks = [pl.ds(r * 512, 512) for r in range(S // 512)]

        @pl.when(jnp.logical_not(is_att))
        def _():
            for rws in chunks:
                main_ref[rws, :] = _dot(a_ref[rws, :], w_ref[...], NT)

        @pl.when(is_att)
        def _():
            for rws in chunks:
                p = _dot(a_ref[rws, :], w_ref[...], NT)
                for h in range(4):
                    slab_ref[h, rws, :] = p[:, h * 128:(h + 1) * 128]

        @pl.when(j == PROJ_TILES - 1)
        def _():
            for cp in first + near_pass + relay + far_pass:
                cp.wait_send()

    gs = pltpu.PrefetchScalarGridSpec(
        num_scalar_prefetch=1, grid=(PROJ_TILES,),
        in_specs=[pl.BlockSpec((S, D), lambda j, o: (0, 0)), HBM_SPEC],
        out_specs=(pl.BlockSpec((S, PROJ_TN), lambda j, o: (0, o[1, j])),
                   pl.BlockSpec((4, S, 128), lambda j, o: (o[2, j], 0, 0)), HBM_SPEC),
        scratch_shapes=[pltpu.VMEM((2, PROJ_TN, D), BF16), pltpu.SemaphoreType.DMA((2,)),
                        pltpu.SemaphoreType.DMA((7 * W_CHUNKS,)), pltpu.SemaphoreType.DMA((7 * W_CHUNKS,)),
                        pltpu.SemaphoreType.DMA((W_CHUNKS,))])
    return _pcall(body, name="gather_proj", grid_spec=gs,
                  out_shape=(jax.ShapeDtypeStruct((S, MAIN_COLS), F32), jax.ShapeDtypeStruct((N_SLABS, S, 128), F32),
                             jax.ShapeDtypeStruct((IN_COLS, D), BF16)),
                  compiler_params=_params(("arbitrary",)))(order, h1, shard)


TR = 256


def _row_spec(w=D):
    return pl.BlockSpec((TR, w), lambda i: (i, 0))


def _vec_spec(w=D):
    return pl.BlockSpec((1, w), lambda i: (0, 0))


def _norm_mod_fwd(x, g, sh, sc, name):
    def body(x_ref, g_ref, sh_ref, sc_ref, o_ref):
        xv = x_ref[...]
        rstd = lax.rsqrt(jnp.mean(xv * xv, axis=-1, keepdims=True) + RMS_EPS)
        n = xv * rstd * g_ref[...]
        o_ref[...] = (n * (1.0 + sc_ref[...]) + sh_ref[...]).astype(BF16)

    return _pcall(body, name=name, grid=(S // TR,), in_specs=[_row_spec(), _vec_spec(), _vec_spec(), _vec_spec()],
                  out_specs=_row_spec(), out_shape=jax.ShapeDtypeStruct((S, D), BF16),
                  compiler_params=_params(("parallel",)))(x, g, sh, sc)


def _norm_mod_bwd(x, g, sc, dh, dres, name, gate=None):
    gated = gate is not None

    def body(x_ref, g_ref, sc_ref, dh_ref, dres_ref, *rest):
        if gated:
            f_ref, gv_ref, dx_ref, dsc_ref, dsh_ref, dg_ref, dz_ref, dgv_ref = rest
        else:
            dx_ref, dsc_ref, dsh_ref, dg_ref = rest
        i = pl.program_id(0)
        xv = x_ref[...]
        dh = dh_ref[...]
        rstd = lax.rsqrt(jnp.mean(xv * xv, axis=-1, keepdims=True) + RMS_EPS)
        xhat = xv * rstd
        gv = g_ref[...]
        dn = dh * (1.0 + sc_ref[...])
        dxhat = dn * gv
        dx = dres_ref[...] + rstd * (dxhat - xhat * jnp.mean(dxhat * xhat, axis=-1, keepdims=True))
        dx_ref[...] = dx
        sums = [(dsc_ref, jnp.sum(dh * (xhat * gv), axis=0, keepdims=True)),
                (dsh_ref, jnp.sum(dh, axis=0, keepdims=True)),
                (dg_ref, jnp.sum(dn * xhat, axis=0, keepdims=True))]
        if gated:
            dz_ref[...] = (dx * gv_ref[...]).astype(BF16)
            sums.append((dgv_ref, jnp.sum(dx * f_ref[...], axis=0, keepdims=True)))

        @pl.when(i == 0)
        def _():
            for ref, p in sums:
                ref[...] = p

        @pl.when(i > 0)
        def _():
            for ref, p in sums:
                ref[...] += p

    vec = jax.ShapeDtypeStruct((1, D), F32)
    in_specs = [_row_spec(), _vec_spec(), _vec_spec(), _row_spec(), _row_spec()]
    out_specs = [_row_spec(), _vec_spec(), _vec_spec(), _vec_spec()]
    out_shape = [jax.ShapeDtypeStruct((S, D), F32), vec, vec, vec]
    args = [x, g, sc, dh, dres]
    if gated:
        in_specs += [_row_spec(), _vec_spec()]
        out_specs += [_row_spec(), _vec_spec()]
        out_shape += [jax.ShapeDtypeStruct((S, D), BF16), vec]
        args += list(gate)
    return _pcall(body, name=name, grid=(S // TR,), in_specs=in_specs, out_specs=tuple(out_specs),
                  out_shape=tuple(out_shape), compiler_params=_params(("arbitrary",)))(*args)


def _w_o_norm2(merged, w_o, x, g1, g, sh, sc):
    def body(a_ref, b_ref, x_ref, g1_ref, g_ref, sh_ref, sc_ref, o_ref, x1_ref, h_ref):
        acc = _dot(a_ref[...], b_ref[...], NN)
        o_ref[...] = acc
        xv = x_ref[...] + g1_ref[...] * acc
        x1_ref[...] = xv
        rstd = lax.rsqrt(jnp.mean(xv * xv, axis=-1, keepdims=True) + RMS_EPS)
        h_ref[...] = (xv * rstd * g_ref[...] * (1.0 + sc_ref[...]) + sh_ref[...]).astype(BF16)

    rows = pl.BlockSpec((FF2_TM, D), lambda i: (i, 0))
    f32 = jax.ShapeDtypeStruct((S, D), F32)
    return _pcall(body, name="w_o_norm2", grid=(S // FF2_TM,),
                  in_specs=[rows, pl.BlockSpec((D, D), lambda i: (0, 0)), rows] + [_vec_spec()] * 4,
                  out_specs=(rows, rows, rows), out_shape=(f32, f32, jax.ShapeDtypeStruct((S, D), BF16)),
                  compiler_params=_params(("parallel",)))(merged, w_o, x, g1, g, sh, sc)


FF2_TM = 512


def _ff2_final(act, w_ff2, x1, g2, tgt, g):
    def body(a_ref, b_ref, x1_ref, g2_ref, t_ref, g_ref, loss_ref, dx_ref, dg_ref, df_ref, dg2_ref):
        i = pl.program_id(0)
        f = _dot(a_ref[...], b_ref[...], NN)
        g2v = g2_ref[...]
        xv = x1_ref[...] + g2v * f
        gv = g_ref[...]
        rstd = lax.rsqrt(jnp.mean(xv * xv, axis=-1, keepdims=True) + RMS_EPS)
        xhat = xv * rstd
        err = xhat * gv - t_ref[...]
        dy = err * (1.0 / D)
        dxhat = dy * gv
        dx = rstd * (dxhat - xhat * jnp.mean(dxhat * xhat, axis=-1, keepdims=True))
        dx_ref[...] = dx
        df_ref[...] = (dx * g2v).astype(BF16)
        p_g = jnp.sum(dy * xhat, axis=0, keepdims=True)
        p_g2 = jnp.sum(dx * f, axis=0, keepdims=True)
        p_l = jnp.zeros((1, 128), F32) + 0.5 * jnp.sum(jnp.mean(err * err, axis=-1, keepdims=True))

        @pl.when(i == 0)
        def _():
            dg_ref[...] = p_g
            dg2_ref[...] = p_g2
            loss_ref[...] = p_l

        @pl.when(i > 0)
        def _():
            dg_ref[...] += p_g
            dg2_ref[...] += p_g2
            loss_ref[...] += p_l

    vec = jax.ShapeDtypeStruct((1, D), F32)
    rows = lambda w: pl.BlockSpec((FF2_TM, w), lambda i: (i, 0))
    return _pcall(body, name="ff2_final", grid=(S // FF2_TM,),
                  in_specs=[rows(D_FF), pl.BlockSpec((D_FF, D), lambda i: (0, 0)), rows(D), _vec_spec(), rows(D),
                            _vec_spec()],
                  out_specs=(_vec_spec(128), rows(D), _vec_spec(), rows(D), _vec_spec()),
                  out_shape=(jax.ShapeDtypeStruct((1, 128), F32), jax.ShapeDtypeStruct((S, D), F32), vec,
                             jax.ShapeDtypeStruct((S, D), BF16), vec),
                  compiler_params=_params(("arbitrary",)))(act, w_ff2, x1, g2, tgt, g)


HALF = 512


MERGE_TM = 1024


def _merge_specs():
    blk = lambda off: pl.BlockSpec((MERGE_TM, HALF), lambda i, j: (i, off // HALF + j))
    return blk(OFF_GA), blk(OFF_GB), blk(0)


def _att_out_merge(att, w_att_out, proj, ret_out):
    def body(a_ref, b_ref, ga_ref, gb_ref, r_ref, o_ref, m_ref):
        acc = _dot(a_ref[...], b_ref[...], NN)
        o_ref[...] = acc
        m_ref[...] = (jax.nn.sigmoid(ga_ref[...]) * r_ref[...] + jax.nn.sigmoid(gb_ref[...]) * acc).astype(BF16)

    ga, gb, tile = _merge_specs()
    return _pcall(body, name="att_out", grid=(S // MERGE_TM, D // HALF),
                  in_specs=[pl.BlockSpec((MERGE_TM, AW), lambda i, j: (i, 0)), pl.BlockSpec((AW, HALF), lambda i, j: (0, j)),
                            ga, gb, tile],
                  out_specs=(tile, tile),
                  out_shape=(jax.ShapeDtypeStruct((S, D), F32), jax.ShapeDtypeStruct((S, D), BF16)),
                  compiler_params=_params(("parallel", "parallel")))(att, w_att_out, proj, proj, ret_out)


def _dmerged_split(dmixo, w_o, proj, ret_out, att_out):
    def body(a_ref, b_ref, ga_ref, gb_ref, r_ref, at_ref, dr_ref, da_ref, dga_ref, dgb_ref):
        dm = _dot(a_ref[...], b_ref[...], NT)
        sa = jax.nn.sigmoid(ga_ref[...])
        sb = jax.nn.sigmoid(gb_ref[...])
        dr_ref[...] = (dm * sa).astype(BF16)
        da_ref[...] = (dm * sb).astype(BF16)
        dga_ref[...] = (dm * r_ref[...] * (sa * (1.0 - sa))).astype(BF16)
        dgb_ref[...] = (dm * at_ref[...] * (sb * (1.0 - sb))).astype(BF16)

    ga, gb, tile = _merge_specs()
    o = jax.ShapeDtypeStruct((S, D), BF16)
    return _pcall(body, name="dmerged", grid=(S // MERGE_TM, D // HALF),
                  in_specs=[pl.BlockSpec((MERGE_TM, D), lambda i, j: (i, 0)), pl.BlockSpec((HALF, D), lambda i, j: (j, 0)),
                            ga, gb, tile, tile],
                  out_specs=(tile,) * 4, out_shape=(o, o, o, o),
                  compiler_params=_params(("parallel", "parallel")))(dmixo, w_o, proj, proj, ret_out, att_out)


def _ret_tables():
    H, C = RET_HEADS, CHUNK
    log_g = jnp.log1p(-(2.0 ** (-5.0 - jnp.arange(H, dtype=F32))))
    idx = jnp.arange(C, dtype=F32)
    rel = idx[:, None] - idx[None, :]
    inner = jnp.where(rel >= 0, jnp.exp(log_g[:, None, None] * jnp.maximum(rel, 0.0)), 0.0)
    qd = jnp.exp(log_g[:, None] * (idx + 1.0))[:, :, None]
    kd = jnp.exp(log_g[:, None] * (C - 1.0 - idx))[:, :, None]
    cd = jnp.broadcast_to(jnp.exp(log_g * C)[:, None, None], (H, 1, 128))
    half = RET_DK // 2
    inv = 10000.0 ** (-jnp.arange(half, dtype=F32) / half)
    ang = jnp.arange(S, dtype=F32)[:, None] * inv[None, :]
    return inner, qd, kd, cd, jnp.cos(ang), jnp.sin(ang)


def _rot(x, cos, sin):
    x1, x2 = x[:, :128], x[:, 128:]
    return jnp.concatenate([x1 * cos - x2 * sin, x1 * sin + x2 * cos], axis=1)


def _rot_t(d, cos, sin):
    d1, d2 = d[:, :128], d[:, 128:]
    return jnp.concatenate([d1 * cos + d2 * sin, d2 * cos - d1 * sin], axis=1)


RET_COLS = OFF_ATT
RET_VW = RET_HEADS * RET_DV


def _ret_specs(chunk_of):
    ci = chunk_of
    whole = lambda shape: pl.BlockSpec(shape, lambda t: (0,) * len(shape))
    return [
        pl.BlockSpec((CHUNK, RET_COLS), lambda t: (ci(t), 0)),
        pl.BlockSpec((CHUNK, 128), lambda t: (ci(t), 0)),
        pl.BlockSpec((CHUNK, 128), lambda t: (ci(t), 0)),
        whole((RET_HEADS, CHUNK, CHUNK)), whole((RET_HEADS, CHUNK, 1)), whole((RET_HEADS, CHUNK, 1)),
        whole((RET_HEADS, 1, 128)), whole((1, RET_VW)), whole((1, RET_VW)),
    ]


def _ret_cols(h):
    q = slice(OFF_RQ + h * RET_DK, OFF_RQ + (h + 1) * RET_DK)
    k = slice(OFF_RK + h * RET_DK, OFF_RK + (h + 1) * RET_DK)
    v = slice(OFF_RV + h * RET_DV, OFF_RV + (h + 1) * RET_DV)
    g = slice(OFF_RG + h * RET_DV, OFF_RG + (h + 1) * RET_DV)
    return q, k, v, g, slice(h * RET_DV, (h + 1) * RET_DV)


def _ret_fwd(proj, tables, gn_g, gn_b, after):
    inner, qd, kd, cd, cos, sin = tables

    def body(x_ref, cos_ref, sin_ref, in_ref, qd_ref, kd_ref, cd_ref, g_ref, b_ref, after_ref,
             gated_ref, ro_ref, st_ref, s_scr):
        i = pl.program_id(0)

        @pl.when(i == 0)
        def _():
            s_scr[...] = jnp.zeros_like(s_scr)

        cosv, sinv = cos_ref[...], sin_ref[...]
        for h in range(RET_HEADS):
            cq, ck, cv, cg, co = _ret_cols(h)
            q = _rot(x_ref[:, cq], cosv, sinv)
            k = _rot(x_ref[:, ck], cosv, sinv) * (RET_DK ** -0.5)
            v = x_ref[:, cv]
            st = s_scr[h]
            st_ref[h] = st.astype(BF16)
            s = _dot(q, k, NT) * in_ref[h]
            o = _dot(s, v, NN) + _dot(q, st, NN) * qd_ref[h]
            s_scr[h] = st * cd_ref[h, :, :1] + _dot(k * kd_ref[h], v, TN)
            ro_ref[:, co] = o
            mu = jnp.mean(o, axis=-1, keepdims=True)
            oc = o - mu
            var = jnp.mean(oc * oc, axis=-1, keepdims=True)
            rn = oc * lax.rsqrt(var + GN_EPS) * g_ref[:, co] + b_ref[:, co]
            rg = x_ref[:, cg]
            gated_ref[:, co] = (rg * jax.nn.sigmoid(rg) * rn).astype(BF16)

    ospec = pl.BlockSpec((CHUNK, RET_VW), lambda t: (t, 0))
    return _pcall(
        body, name="ret_fwd", grid=(N_CHUNK,), in_specs=_ret_specs(lambda t: t) + [HBM_SPEC],
        out_specs=(ospec, ospec, pl.BlockSpec((RET_HEADS, None, RET_DK, RET_DV), lambda t: (0, t, 0, 0))),
        out_shape=(jax.ShapeDtypeStruct((S, RET_VW), BF16), jax.ShapeDtypeStruct((S, RET_VW), F32),
                   jax.ShapeDtypeStruct((RET_HEADS, N_CHUNK, RET_DK, RET_DV), BF16)),
        scratch_shapes=[pltpu.VMEM((RET_HEADS, RET_DK, RET_DV), F32)],
        compiler_params=_params(("arbitrary",)))(proj, cos, sin, inner, qd, kd, cd, gn_g, gn_b, after)


def _ret_bwd(proj, tables, gn_g, gn_b, ro, states, dgated, others):
    inner, qd, kd, cd, cos, sin = tables
    last = N_CHUNK - 1
    assert RET_COLS + sum(o.shape[1] for o in others) == IN_COLS

    def body(x_ref, cos_ref, sin_ref, in_ref, qd_ref, kd_ref, cd_ref, g_ref, b_ref, ro_ref, st_ref, dg_ref, *rest):
        other_refs, (dx_ref, gg_ref, gb_ref, gs_scr) = rest[:len(others)], rest[len(others):]
        t = pl.program_id(0)
        col = RET_COLS
        for o_ref in other_refs:
            dx_ref[:, col:col + o_ref.shape[1]] = o_ref[...]
            col += o_ref.shape[1]

        @pl.when(t == 0)
        def _():
            gs_scr[...] = jnp.zeros_like(gs_scr)
            gg_ref[...] = jnp.zeros_like(gg_ref)
            gb_ref[...] = jnp.zeros_like(gb_ref)

        cosv, sinv = cos_ref[...], sin_ref[...]
        for h in range(RET_HEADS):
            cq, ck, cv, cg, co = _ret_cols(h)
            q = _rot(x_ref[:, cq], cosv, sinv)
            k = _rot(x_ref[:, ck], cosv, sinv) * (RET_DK ** -0.5)
            v = x_ref[:, cv]
            qdv, kdv, dm = qd_ref[h], kd_ref[h], in_ref[h]
            st = st_ref[h]
            o = ro_ref[:, co]
            gv = g_ref[:, co]
            mu = jnp.mean(o, axis=-1, keepdims=True)
            oc = o - mu
            rstd = lax.rsqrt(jnp.mean(oc * oc, axis=-1, keepdims=True) + GN_EPS)
            ohat = oc * rstd
            rn = ohat * gv + b_ref[:, co]
            rg = x_ref[:, cg]
            sg = jax.nn.sigmoid(rg)
            dgt = dg_ref[:, co]
            drn = dgt * (rg * sg)
            dx_ref[:, cg] = (dgt * rn * (sg * (1.0 + rg * (1.0 - sg)))).astype(BF16)
            gg_ref[:, co] += jnp.sum(drn * ohat, axis=0, keepdims=True)
            gb_ref[:, co] += jnp.sum(drn, axis=0, keepdims=True)
            dohat = drn * gv
            do = rstd * (dohat - jnp.mean(dohat, axis=-1, keepdims=True)
                         - ohat * jnp.mean(dohat * ohat, axis=-1, keepdims=True))
            gs = gs_scr[h]
            s = _dot(q, k, NT) * dm
            dsr = _dot(do, v, NT) * dm
            dq = _dot(dsr, k, NN) + _dot(do, st, NT) * qdv
            dk = _dot(dsr, q, TN) + _dot(v, gs, NT) * kdv
            dv = _dot(s, do, TN) + _dot(k * kdv, gs, NN)
            gs_scr[h] = gs * cd_ref[h, :, :1] + _dot(q * qdv, do, TN)
            dx_ref[:, cq] = _rot_t(dq, cosv, sinv).astype(BF16)
            dx_ref[:, ck] = (_rot_t(dk, cosv, sinv) * (RET_DK ** -0.5)).astype(BF16)
            dx_ref[:, cv] = dv.astype(BF16)

    rev = lambda t: last - t
    vblk = pl.BlockSpec((CHUNK, RET_VW), lambda t: (rev(t), 0))
    vspec = pl.BlockSpec((1, RET_VW), lambda t: (0, 0))
    rows = lambda w: pl.BlockSpec((CHUNK, w), lambda t: (rev(t), 0))
    return _pcall(
        body, name="ret_bwd", grid=(N_CHUNK,),
        in_specs=_ret_specs(rev) + [vblk, pl.BlockSpec((RET_HEADS, None, RET_DK, RET_DV), lambda t: (0, rev(t), 0, 0)),
                                    vblk] + [rows(o.shape[1]) for o in others],
        out_specs=(rows(IN_COLS), vspec, vspec),
        out_shape=(jax.ShapeDtypeStruct((S, IN_COLS), BF16), jax.ShapeDtypeStruct((1, RET_VW), F32),
                   jax.ShapeDtypeStruct((1, RET_VW), F32)),
        scratch_shapes=[pltpu.VMEM((RET_HEADS, RET_DK, RET_DV), F32)],
        compiler_params=_params(("arbitrary",)))(proj, cos, sin, inner, qd, kd, cd, gn_g, gn_b, ro, states, dgated,
                                                 *others)


def _bucket_tables():
    qi = np.arange(ATT_BLK)[:, None]
    kj = np.arange(2 * ATT_BLK)[None, :]
    m = ATT_BLK + qi - kj
    out = []
    for win, dil in ATT_GROUPS:
        w = win // dil
        dist = (np.clip(m, 0, w) * dil).astype(np.int32)
        max_exact = N_BUCKETS // 2
        d_f = np.maximum(dist, 1).astype(np.float32)
        large = max_exact + (np.log(d_f / np.float32(max_exact)) / np.float32(math.log(MAX_DIST / max_exact))
                             * np.float32(N_BUCKETS - max_exact)).astype(np.int32)
        large = np.minimum(large, N_BUCKETS - 1)
        out.append(np.where(dist < max_exact, dist, large).astype(np.int32))
    return np.stack(out)


def _bias_build(rel_bias, buckets, after):
    def body(tab_ref, bk_ref, after_ref, o_ref):
        hh = pl.program_id(0)
        bk = bk_ref[...]
        acc = jnp.zeros((ATT_BLK, 2 * ATT_BLK), F32)
        for b in range(N_BUCKETS):
            acc = jnp.where(bk == b, tab_ref[b, hh], acc)
        o_ref[...] = acc

    nh = len(ATT_GROUPS) * ATT_HG
    return _pcall(body, name="bias_build", grid=(nh,),
                  in_specs=[pl.BlockSpec(memory_space=pltpu.SMEM),
                            pl.BlockSpec((None, ATT_BLK, 2 * ATT_BLK), lambda hh: (hh // ATT_HG, 0, 0)), HBM_SPEC],
                  out_specs=pl.BlockSpec((None, ATT_BLK, 2 * ATT_BLK), lambda hh: (hh, 0, 0)),
                  out_shape=jax.ShapeDtypeStruct((nh, ATT_BLK, 2 * ATT_BLK), F32),
                  compiler_params=_params(("parallel",)))(rel_bias, buckets, after)


def _bias_grad(ds_sum, buckets):
    def body(ds_ref, bk_ref, o_ref):
        bk = bk_ref[...]
        ds = ds_ref[...]
        rows = lax.broadcasted_iota(jnp.int32, (N_BUCKETS, 128), 0)
        acc = jnp.zeros((N_BUCKETS, 128), F32)
        for b in range(N_BUCKETS):
            acc = jnp.where(rows == b, jnp.sum(jnp.where(bk == b, ds, 0.0)), acc)
        o_ref[...] = acc

    nh = len(ATT_GROUPS) * ATT_HG
    return _pcall(body, name="bias_grad", grid=(nh,),
                  in_specs=[pl.BlockSpec((None, ATT_BLK, 2 * ATT_BLK), lambda hh: (hh, 0, 0)),
                            pl.BlockSpec((None, ATT_BLK, 2 * ATT_BLK), lambda hh: (hh // ATT_HG, 0, 0))],
                  out_specs=pl.BlockSpec((None, N_BUCKETS, 128), lambda hh: (hh, 0, 0)),
                  out_shape=jax.ShapeDtypeStruct((nh, N_BUCKETS, 128), F32),
                  compiler_params=_params(("parallel",)))(ds_sum, buckets)


def _att_valid(n):
    qi = lax.broadcasted_iota(jnp.int32, (ATT_BLK, 2 * ATT_BLK), 0)
    kj = lax.broadcasted_iota(jnp.int32, (ATT_BLK, 2 * ATT_BLK), 1)
    m = ATT_BLK + qi - kj
    first_key = jnp.where(n > 0, 0, ATT_BLK)
    return (m >= 0) & (m <= ATT_BLK) & (kj >= first_key)


ATT_HP = (1, 2, 2)


def _att_geometry(gi):
    _, dil = ATT_GROUPS[gi]
    return dil, S // dil // ATT_BLK, ATT_HP[gi]


def _blk(dil, r, n):
    if dil == 1:
        return pl.ds(n * ATT_BLK, ATT_BLK)
    return pl.ds(r + n * ATT_BLK * dil, ATT_BLK, stride=dil)


def _slab_specs(gi):
    _, _, hp = _att_geometry(gi)
    per = ATT_HG // hp
    return [pl.BlockSpec((hp, S, ATT_DH), lambda g, r, part=part: ((3 * gi + part) * per + g, 0, 0))
            for part in range(3)]


def _head_specs(gi, count):
    _, _, hp = _att_geometry(gi)
    return [pl.BlockSpec((hp, S, ATT_DH), lambda g, r: (g, 0, 0))] * count


def _bias_spec(gi):
    _, _, hp = _att_geometry(gi)
    return pl.BlockSpec((hp, ATT_BLK, 2 * ATT_BLK), lambda g, r: (gi * (ATT_HG // hp) + g, 0, 0))


def _att_valid_first():
    qi = lax.broadcasted_iota(jnp.int32, (ATT_BLK, ATT_BLK), 0)
    kj = lax.broadcasted_iota(jnp.int32, (ATT_BLK, ATT_BLK), 1)
    return kj <= qi


def _att_fwd(slabs, bias, gi, comm=None):
    dil, nb, hp = _att_geometry(gi)
    scale = ATT_DH ** -0.5

    def body(q_ref, k_ref, v_ref, bias_ref, o_ref, l_ref):
        r = pl.program_id(1)
        for n in range(nb):
            cur = _blk(dil, r, n)
            valid = _att_valid(n) if n > 0 else _att_valid_first()
            for h in range(hp):
                if n > 0:
                    prev = _blk(dil, r, n - 1)
                    kk = jnp.concatenate([k_ref[h, prev, :], k_ref[h, cur, :]], axis=0)
                    vv = jnp.concatenate([v_ref[h, prev, :], v_ref[h, cur, :]], axis=0)
                    bias = bias_ref[h]
                else:
                    kk, vv, bias = k_ref[h, cur, :], v_ref[h, cur, :], bias_ref[h, :, pl.ds(ATT_BLK, ATT_BLK)]
                s = _dot(q_ref[h, cur, :], kk, NT) * scale + bias
                s = jnp.where(valid, s, -1e30)
                mx = jnp.max(s, axis=-1, keepdims=True)
                e = jnp.exp(s - mx)
                den = jnp.sum(e, axis=-1, keepdims=True)
                o_ref[h, cur, :] = _dot(e / den, vv, NN)
                l_ref[h, cur, :] = jnp.broadcast_to(mx + jnp.log(den), (ATT_BLK, ATT_DH))

    osh = jax.ShapeDtypeStruct((ATT_HG, S, ATT_DH), F32)
    kw = dict(name=f"att_fwd{gi}", grid=(ATT_HG // hp, dil), in_specs=_slab_specs(gi) + [_bias_spec(gi)],
              out_specs=tuple(_head_specs(gi, 2)), out_shape=(osh, osh))
    if comm is not None:
        return _carry(body, comm, **kw)(slabs, slabs, slabs, bias)
    return _pcall(body, compiler_params=_params(("parallel", "arbitrary")), **kw)(slabs, slabs, slabs, bias)


def _att_bwd(slabs, bias, o, lse, do, dlse, gi, comm=None):
    dil, nb, hp = _att_geometry(gi)
    per = ATT_HG // hp
    scale = ATT_DH ** -0.5
    wh = hp * ATT_DH
    wide = lambda t: jnp.concatenate([t, t], axis=1)

    def body(q_ref, k_ref, v_ref, bias_ref, o_ref, l_ref, do_ref, dl_ref, dq_ref, dk_ref, dv_ref, ds_ref):
        r = pl.program_id(1)

        @pl.when(r == 0)
        def _():
            ds_ref[...] = jnp.zeros_like(ds_ref)

        for h in range(hp):
            sl = slice(h * ATT_DH, (h + 1) * ATT_DH)
            carry_k = carry_v = None
            for n in range(nb):
                cur = _blk(dil, r, n)
                q = q_ref[h, cur, :]
                dov = do_ref[h, cur, :]
                delta = jnp.sum(dov * o_ref[h, cur, :], axis=-1, keepdims=True)
                out_rows = pl.ds(n * ATT_BLK, ATT_BLK)
                if n == 0:
                    own = pl.ds(ATT_BLK, ATT_BLK)
                    kk, vv = k_ref[h, cur, :], v_ref[h, cur, :]
                    s = _dot(q, kk, NT) * scale + bias_ref[h, :, own]
                    p = jnp.where(_att_valid_first(), jnp.exp(s - l_ref[h, cur, :]), 0.0)
                    ds = p * (_dot(dov, vv, NT) - delta + dl_ref[h, cur, :])
                    ds_ref[h, :, own] += ds
                    dq_ref[out_rows, sl] = (_dot(ds, kk, NN) * scale).astype(BF16)
                    carry_k, carry_v = _dot(ds, q, TN) * scale, _dot(p, dov, TN)
                    continue
                prev = _blk(dil, r, n - 1)
                kk = jnp.concatenate([k_ref[h, prev, :], k_ref[h, cur, :]], axis=0)
                vv = jnp.concatenate([v_ref[h, prev, :], v_ref[h, cur, :]], axis=0)
                s = _dot(q, kk, NT) * scale + bias_ref[h]
                p = jnp.where(_att_valid(n), jnp.exp(s - wide(l_ref[h, cur, :])), 0.0)
                dp = _dot(dov, vv, NT)
                ds = p * (dp - delta + wide(dl_ref[h, cur, :]))
                ds_ref[h] += ds
                dq_ref[out_rows, sl] = (_dot(ds, kk, NN) * scale).astype(BF16)
                dkk = _dot(ds, q, TN) * scale
                dvv = _dot(p, dov, TN)
                before = pl.ds((n - 1) * ATT_BLK, ATT_BLK)
                dk_ref[before, sl] = (carry_k + dkk[:ATT_BLK]).astype(BF16)
                dv_ref[before, sl] = (carry_v + dvv[:ATT_BLK]).astype(BF16)
                carry_k, carry_v = dkk[ATT_BLK:], dvv[ATT_BLK:]
            last = pl.ds((nb - 1) * ATT_BLK, ATT_BLK)
            dk_ref[last, sl] = carry_k.astype(BF16)
            dv_ref[last, sl] = carry_v.astype(BF16)

    out_spec = pl.BlockSpec((S // dil, wh), lambda g, r: (0, r * per + g))
    osh = jax.ShapeDtypeStruct((S // dil, dil * AW), BF16)
    kw = dict(name=f"att_bwd{gi}", grid=(per, dil), in_specs=_slab_specs(gi) + [_bias_spec(gi)] + _head_specs(gi, 4),
              out_specs=(out_spec, out_spec, out_spec,
                         pl.BlockSpec((hp, ATT_BLK, 2 * ATT_BLK), lambda g, r: (g, 0, 0))),
              out_shape=(osh, osh, osh, jax.ShapeDtypeStruct((ATT_HG, ATT_BLK, 2 * ATT_BLK), F32)))
    args = (slabs, slabs, slabs, bias, o, lse, do, dlse)
    if comm is not None:
        return _carry(body, comm, **kw)(*args)
    return _pcall(body, compiler_params=_params(("arbitrary", "arbitrary")), **kw)(*args)


AW = ATT_HG * ATT_DH


def _mix_weights(l0, l1, l2):
    mx = jnp.maximum(jnp.maximum(l0, l1), l2)
    e0, e1, e2 = jnp.exp(l0 - mx), jnp.exp(l1 - mx), jnp.exp(l2 - mx)
    den = e0 + e1 + e2
    return e0 / den, e1 / den, e2 / den


def _heads_spec():
    return pl.BlockSpec((ATT_HG, TR, ATT_DH), lambda i: (0, i, 0))


def _mix_fwd(os_, ls, comm=None):
    def body(o0, o1, o2, l0, l1, l2, att_ref):
        for h in range(ATT_HG):
            w0, w1, w2 = _mix_weights(l0[h], l1[h], l2[h])
            att_ref[:, h * ATT_DH:(h + 1) * ATT_DH] = (w0 * o0[h] + w1 * o1[h] + w2 * o2[h]).astype(BF16)

    kw = dict(name="mix_fwd", grid=(S // TR,), in_specs=[_heads_spec()] * 6, out_specs=_row_spec(AW),
              out_shape=jax.ShapeDtypeStruct((S, AW), BF16))
    if comm is not None:
        return _carry(body, comm, **kw)(*os_, *ls)
    return _pcall(body, compiler_params=_params(("parallel",)), **kw)(*os_, *ls)


def _mix_bwd(os_, ls, datt):
    def body(o0, o1, o2, l0, l1, l2, da_ref, d0, d1, d2, e0, e1, e2):
        for h in range(ATT_HG):
            ws = _mix_weights(l0[h], l1[h], l2[h])
            da = da_ref[:, h * ATT_DH:(h + 1) * ATT_DH]
            dws = []
            for o_ref, w, d_ref in zip((o0, o1, o2), ws, (d0, d1, d2)):
                d_ref[h] = w * da
                dws.append(jnp.broadcast_to(jnp.sum(da * o_ref[h], axis=-1, keepdims=True), (TR, ATT_DH)))
            tot = ws[0] * dws[0] + ws[1] * dws[1] + ws[2] * dws[2]
            for w, dw, e_ref in zip(ws, dws, (e0, e1, e2)):
                e_ref[h] = w * (dw - tot)

    o = jax.ShapeDtypeStruct((ATT_HG, S, ATT_DH), F32)
    return _pcall(body, name="mix_bwd", grid=(S // TR,), in_specs=[_heads_spec()] * 6 + [_row_spec(AW)],
                  out_specs=(_heads_spec(),) * 6, out_shape=(o,) * 6,
                  compiler_params=_params(("parallel",)))(*os_, *ls, datt)


def _ada_fwd(c_all, w_sh, b_sl):
    def body(c_ref, w_ref, b_ref, o_ref):
        cv = c_ref[...]
        o_ref[...] = _dot(cv * jax.nn.sigmoid(cv), w_ref[...], NN) + b_ref[...]

    return _pcall(body, name="ada_fwd", out_shape=jax.ShapeDtypeStruct((N_DEV, w_sh.shape[1]), F32),
                  compiler_params=_params())(c_all, w_sh, b_sl)


def _ada_bwd(c_all, dm_sl):
    def body(c_ref, d_ref, o_ref):
        cv = c_ref[...]
        o_ref[...] = _dot(cv * jax.nn.sigmoid(cv), d_ref[...], TN)

    return _pcall(body, name="ada_bwd", out_shape=jax.ShapeDtypeStruct((D, dm_sl.shape[1]), F32),
                  compiler_params=_params())(c_all, dm_sl)


N_MOD = 6


def _sum_small(gathered):
    n = len(gathered)

    def body(*refs):
        ins, (gb_ref, dm_ref), outs = refs[:n], refs[n:n + 2], refs[n + 2:]

        def total(r):
            acc = r[0]
            for e in range(1, N_DEV):
                acc = acc + r[e]
            return acc

        for i in range(N_MOD):
            cols = slice(i * D, (i + 1) * D)
            gb_ref[:, cols] = total(ins[i])
            for e in range(N_DEV):
                dm_ref[e:e + 1, cols] = ins[i][e]
        for r, o_ref in zip(ins[N_MOD:], outs):
            o_ref[...] = total(r)

    shapes = (jax.ShapeDtypeStruct((1, N_MOD * D), F32), jax.ShapeDtypeStruct((N_DEV, N_MOD * D), F32),
              *[jax.ShapeDtypeStruct(g.shape[1:], F32) for g in gathered[N_MOD:]])
    res = _pcall(body, name="sum_small", out_shape=shapes, compiler_params=_params())(*gathered)
    return res[0], res[1], res[2:]


def _row_tile(m, n):
    t = max(8, min(m, (1 << 19) // n // 8 * 8))
    while m % t:
        t -= 8
    return t


def _pair_sum(full, recv, sel, name, col_block=0):
    _, m, n = recv.shape
    t = _row_tile(m, n)

    def body(sel_ref, a_ref, b_ref, o_ref):
        o_ref[...] = (a_ref[...].astype(F32) + b_ref[...].astype(F32)).astype(o_ref.dtype)

    gs = pltpu.PrefetchScalarGridSpec(
        num_scalar_prefetch=1, grid=(4, m // t),
        in_specs=[pl.BlockSpec((None, None, t, n), lambda q, i, s: (q, s[0], i, col_block)),
                  pl.BlockSpec((None, t, n), lambda q, i, s: (q, i, 0))],
        out_specs=pl.BlockSpec((None, t, n), lambda q, i, s: (q, i, 0)))
    return _pcall(body, name=name, grid_spec=gs, out_shape=jax.ShapeDtypeStruct((4, m, n), full.dtype),
                  compiler_params=_params(("parallel", "parallel")))(sel, full, recv)


def _chip_sum(part, recv, sel, name):
    _, m, n = part.shape
    t = _row_tile(m, n)

    def body(sel_ref, a_ref, r_ref, o_ref):
        o_ref[...] = ((a_ref[...].astype(F32) + r_ref[0].astype(F32)) + r_ref[1].astype(F32)) + r_ref[2].astype(F32)

    gs = pltpu.PrefetchScalarGridSpec(
        num_scalar_prefetch=1, grid=(m // t,),
        in_specs=[pl.BlockSpec((None, t, n), lambda i, s: (s[0], i, 0)),
                  pl.BlockSpec((3, t, n), lambda i, s: (0, i, 0))],
        out_specs=pl.BlockSpec((t, n), lambda i, s: (i, 0)))
    return _pcall(body, name=name, grid_spec=gs, out_shape=jax.ShapeDtypeStruct((m, n), F32),
                  compiler_params=_params(("parallel",)))(sel, part, recv)


def _adamw_math(w, g, m, v):
    nm = ADAM_B1 * m + (1.0 - ADAM_B1) * g
    nv = ADAM_B2 * v + (1.0 - ADAM_B2) * (g * g)
    m_hat = nm / (1.0 - ADAM_B1 ** ADAM_STEP)
    v_hat = nv / (1.0 - ADAM_B2 ** ADAM_STEP)
    return -ADAM_LR * (m_hat / (jnp.sqrt(v_hat) + ADAM_EPS) + ADAM_WD * w), nm, nv


def _adamw(w, g, m, v, name):
    _, rows, cols = w.shape
    t = _row_tile(rows, cols)

    def body(w_ref, g_ref, m_ref, v_ref, d_ref, nm_ref, nv_ref):
        d_ref[...], nm_ref[...], nv_ref[...] = _adamw_math(w_ref[...], g_ref[...], m_ref[...], v_ref[...])

    spec3 = pl.BlockSpec((None, t, cols), lambda i: (0, i, 0))
    spec2 = pl.BlockSpec((t, cols), lambda i: (i, 0))
    o = jax.ShapeDtypeStruct(w.shape, F32)
    return _pcall(body, name=name, grid=(rows // t,), in_specs=[spec3, spec2, spec3, spec3], out_specs=(spec3,) * 3,
                  out_shape=(o, o, o), compiler_params=_params(("parallel",)))(w, g, m, v)


def _adamw_reduced1(w, m, v, part, recv, sel, name):
    _, rows, cols = w.shape
    t = _row_tile(rows, cols)

    def body(sel_ref, w_ref, m_ref, v_ref, p_ref, r_ref, g_ref, d_ref, nm_ref, nv_ref):
        g = ((p_ref[...].astype(F32) + r_ref[0].astype(F32)) + r_ref[1].astype(F32)) + r_ref[2].astype(F32)
        g_ref[...] = g
        d_ref[...], nm_ref[...], nv_ref[...] = _adamw_math(w_ref[...], g, m_ref[...], v_ref[...])

    wspec = pl.BlockSpec((None, t, cols), lambda i, s: (0, i, 0))
    gs = pltpu.PrefetchScalarGridSpec(
        num_scalar_prefetch=1, grid=(rows // t,),
        in_specs=[wspec, wspec, wspec, pl.BlockSpec((None, t, cols), lambda i, s: (s[0], i, 0)),
                  pl.BlockSpec((3, t, cols), lambda i, s: (0, i, 0))],
        out_specs=(wspec,) * 4)
    o = jax.ShapeDtypeStruct(w.shape, F32)
    return _pcall(body, name=name, grid_spec=gs, out_shape=(o, o, o, o),
                  compiler_params=_params(("parallel",)))(sel, w, m, v, part, recv)


def _adamw_reduced(w, m, v, parts, recvs, sel):
    _, rows, cols = w.shape
    half = cols // 2
    t = _row_tile(rows, half)

    def body(sel_ref, w_ref, m_ref, v_ref, pa_ref, pb_ref, ra_ref, rb_ref, g_ref, d_ref, nm_ref, nv_ref):
        total = lambda p_ref, r_ref: ((p_ref[...].astype(F32) + r_ref[0].astype(F32)) + r_ref[1].astype(F32)) \
            + r_ref[2].astype(F32)
        g = jnp.where(pl.program_id(1) == 0, total(pa_ref, ra_ref), total(pb_ref, rb_ref))
        g_ref[...] = g
        d_ref[...], nm_ref[...], nv_ref[...] = _adamw_math(w_ref[...], g, m_ref[...], v_ref[...])

    wspec = pl.BlockSpec((None, t, half), lambda i, j, s: (0, i, j))
    pspec = pl.BlockSpec((None, t, half), lambda i, j, s: (s[0], i, 0))
    rspec = pl.BlockSpec((3, t, half), lambda i, j, s: (0, i, 0))
    gs = pltpu.PrefetchScalarGridSpec(num_scalar_prefetch=1, grid=(rows // t, 2),
                                      in_specs=[wspec, wspec, wspec, pspec, pspec, rspec, rspec],
                                      out_specs=(wspec,) * 4)
    o = jax.ShapeDtypeStruct(w.shape, F32)
    return _pcall(body, name="adamw_w_in", grid_spec=gs, out_shape=(o, o, o, o),
                  compiler_params=_params(("parallel", "arbitrary")))(sel, w, m, v, *parts, *recvs)


def _adamw_small(ws, gs, ms, vs):
    n = len(ws)

    def body(*refs):
        for i in range(n):
            w_ref, g_ref, m_ref, v_ref = (refs[k * n + i] for k in range(4))
            d, nm, nv = _adamw_math(w_ref[...], g_ref[...], m_ref[...], v_ref[...])
            refs[4 * n + i][...] = d
            refs[5 * n + i][...] = nm
            refs[6 * n + i][...] = nv

    shapes = tuple(jax.ShapeDtypeStruct(w.shape, F32) for w in ws)
    res = _pcall(body, name="adamw_small", out_shape=shapes * 3, compiler_params=_params())(*ws, *gs, *ms, *vs)
    return res[:n], res[n:2 * n], res[2 * n:]


def _mesh_pos():
    return lax.axis_index("x"), lax.axis_index("y"), lax.axis_index("c")


class _Gather:
    def __init__(self, arrs):
        self.ins = list(arrs)
        self.out_shape = tuple(jax.ShapeDtypeStruct((N_DEV,) + a.shape, a.dtype) for a in arrs)
        n = len(arrs)
        self.sems = [pltpu.SemaphoreType.DMA((7 * n,)), pltpu.SemaphoreType.DMA((7 * n,)),
                     pltpu.SemaphoreType.DMA((n,))]

    def _copies(self, ins, outs, sems):
        send_sems, recv_sems, local_sems = sems
        x, y, c = _mesh_pos()
        me, sibling = (x, y, c), (x, y, 1 - c)
        chips = [(1 - x, y), (x, 1 - y), (1 - x, 1 - y)]

        def copy(p, k, block, to, from_input=False):
            dst = outs[p].at[_slot(block)]
            return pltpu.make_async_remote_copy(
                src_ref=ins[p] if from_input else dst, dst_ref=dst, send_sem=send_sems.at[7 * p + k],
                recv_sem=recv_sems.at[7 * p + k], device_id=to, device_id_type=MESH)

        npc = len(self.ins)
        mine = [pltpu.make_async_copy(ins[p], outs[p].at[_slot(me)], local_sems.at[p]) for p in range(npc)]
        first = []
        for p in range(npc):
            first.append(copy(p, 0, me, sibling, from_input=True))
            first += [copy(p, 1 + j, me, (*chip, c), from_input=True) for j, chip in enumerate(chips)]
        return me, sibling, chips, c, copy, mine, first

    def start(self, ins, outs, sems):
        *_, mine, first = self._copies(ins, outs, sems)
        for cp in mine + first:
            cp.start()

    def finish(self, ins, outs, sems):
        me, sibling, chips, c, copy, mine, first = self._copies(ins, outs, sems)
        npc = len(self.ins)
        passed = []
        for p in range(npc):
            for j, chip in enumerate(chips):
                copy(p, 1 + j, (*chip, c), me).wait_recv()
                passed.append(copy(p, 4 + j, (*chip, c), sibling))
                passed[-1].start()
        for p in range(npc):
            copy(p, 0, sibling, me).wait_recv()
            for j, chip in enumerate(chips):
                copy(p, 4 + j, (*chip, 1 - c), me).wait_recv()
        for cp in first + passed:
            cp.wait_send()
        for cp in mine:
            cp.wait()


class _ExchangeCore:
    def __init__(self, fulls, cols=None):
        self.ins = list(fulls)
        self.cols = cols
        width = lambda f: f.shape[3] if cols is None else cols[1]
        self.out_shape = tuple(jax.ShapeDtypeStruct((4, f.shape[2], width(f)), f.dtype) for f in fulls)
        self.sems = [pltpu.SemaphoreType.DMA((4 * len(fulls),)), pltpu.SemaphoreType.DMA((4 * len(fulls),))]

    def _copies(self, ins, outs, sems):
        send_sems, recv_sems = sems
        x, y, c = _mesh_pos()

        def src(a, q):
            ref = ins[a].at[q, 1 - c]
            return ref if self.cols is None else ref.at[:, pl.ds(*self.cols)]

        return [pltpu.make_async_remote_copy(
            src_ref=src(a, q), dst_ref=outs[a].at[q], send_sem=send_sems.at[4 * a + q],
            recv_sem=recv_sems.at[4 * a + q], device_id=(x, y, 1 - c), device_id_type=MESH)
            for a in range(len(self.ins)) for q in range(4)]

    def start(self, ins, outs, sems):
        for cp in self._copies(ins, outs, sems):
            cp.start()

    def finish(self, ins, outs, sems):
        for cp in self._copies(ins, outs, sems):
            cp.wait()


class _ExchangeChip:
    def __init__(self, parts):
        self.ins = list(parts)
        self.out_shape = tuple(jax.ShapeDtypeStruct((3,) + p.shape[1:], p.dtype) for p in parts)
        self.sems = [pltpu.SemaphoreType.DMA((3 * len(parts),)), pltpu.SemaphoreType.DMA((3 * len(parts),))]

    def _copies(self, ins, outs, sems):
        send_sems, recv_sems = sems
        x, y, c = _mesh_pos()
        chips = [(1 - x, y), (x, 1 - y), (1 - x, 1 - y)]
        return [pltpu.make_async_remote_copy(
            src_ref=ins[a].at[2 * px + py], dst_ref=outs[a].at[j], send_sem=send_sems.at[3 * a + j],
            recv_sem=recv_sems.at[3 * a + j], device_id=(px, py, c), device_id_type=MESH)
            for a in range(len(self.ins)) for j, (px, py) in enumerate(chips)]

    def start(self, ins, outs, sems):
        for cp in self._copies(ins, outs, sems):
            cp.start()

    def finish(self, ins, outs, sems):
        for cp in self._copies(ins, outs, sems):
            cp.wait()


HBM_ONLY = pl.BlockSpec(memory_space=pltpu.HBM)
SEM_SPEC = pl.BlockSpec(memory_space=pltpu.SEMAPHORE)
SIDE_EFFECT = pltpu.SideEffectType.DATAFLOW_SIDE_EFFECTING


def _chip_copies(p_refs, land_refs, send_sems, recv_sems):
    x, y, c = _mesh_pos()
    return [pltpu.make_async_remote_copy(
        src_ref=p_refs[a].at[2 * px + py], dst_ref=land_refs[a].at[j], send_sem=send_sems.at[3 * a + j],
        recv_sem=recv_sems.at[3 * a + j], device_id=(px, py, c), device_id_type=MESH)
        for a in range(len(p_refs)) for j, (px, py) in enumerate([(1 - x, y), (x, 1 - y), (1 - x, 1 - y)])]


def _chip_exchange_start(parts, name):
    n = len(parts)
    lands = [lax.empty((3,) + p.shape[1:], p.dtype) for p in parts]

    def body(*refs):
        p_refs, land_refs, (send_sems, recv_sems) = refs[:n], refs[n:2 * n], refs[2 * n:2 * n + 2]
        for cp in _chip_copies(p_refs, land_refs, send_sems, recv_sems):
            cp.start()
        token = refs[-1]
        token[...] = jnp.zeros_like(token)

    hbm = lambda t: pltpu.HBM(t.shape, t.dtype)
    res = pl.pallas_call(
        body, name=name,
        out_shape=(pltpu.SemaphoreType.DMA((3 * n,)), pltpu.SemaphoreType.DMA((3 * n,)), *[hbm(t) for t in parts + lands],
                   jax.ShapeDtypeStruct((8, 128), F32)),
        in_specs=(HBM_ONLY,) * (2 * n),
        out_specs=(SEM_SPEC, SEM_SPEC, *[HBM_ONLY] * (2 * n), pl.BlockSpec(memory_space=pltpu.VMEM)),
        input_output_aliases={i: 2 + i for i in range(2 * n)},
        compiler_params=pltpu.CompilerParams(has_side_effects=SIDE_EFFECT))(
        *[pltpu.with_memory_space_constraint(t, pltpu.HBM) for t in parts + lands])
    return (res[0], res[1], list(res[2:2 + n]), list(res[2 + n:2 + 2 * n])), res[-1]


def _chip_exchange_wait(in_flight, after, name):
    send_sems, recv_sems, parts, lands = in_flight
    n = len(parts)

    def body(*refs):
        p_refs, land_refs, (send_sems, recv_sems) = refs[:n], refs[n:2 * n], refs[2 * n:2 * n + 2]
        for cp in _chip_copies(p_refs, land_refs, send_sems, recv_sems):
            cp.wait_send()
            cp.wait_recv()

    res = pl.pallas_call(
        body, name=name, out_shape=tuple(pltpu.HBM(t.shape, t.dtype) for t in parts + lands),
        in_specs=(*[HBM_ONLY] * (2 * n), SEM_SPEC, SEM_SPEC, pl.BlockSpec(memory_space=pl.ANY)),
        out_specs=(HBM_ONLY,) * (2 * n), input_output_aliases={i: i for i in range(2 * n)},
        compiler_params=pltpu.CompilerParams(has_side_effects=SIDE_EFFECT))(*parts, *lands, send_sems, recv_sems, after)
    return list(res[:n]), list(res[n:])


def _slot(p):
    return 4 * p[0] + 2 * p[1] + p[2]


def _gather_copies(src_refs, out_refs, send_sems, recv_sems):
    x, y, c = _mesh_pos()
    targets = [(x, y, 1 - c), (1 - x, y, c), (x, 1 - y, c), (1 - x, 1 - y, c)]
    return [pltpu.make_async_remote_copy(
        src_ref=src_refs[a], dst_ref=out_refs[a].at[_slot((x, y, c))], send_sem=send_sems.at[4 * a + k],
        recv_sem=recv_sems.at[4 * a + k], device_id=to, device_id_type=MESH)
        for a in range(len(src_refs)) for k, to in enumerate(targets)]


def _gather_start(shards, after, name):
    n = len(shards)
    outs = [lax.empty((N_DEV,) + s.shape, s.dtype) for s in shards]

    def body(*refs):
        for cp in _gather_copies(refs[:n], refs[n:2 * n], refs[2 * n + 1], refs[2 * n + 2]):
            cp.start()
        token = refs[-1]
        token[...] = jnp.zeros_like(token)

    res = pl.pallas_call(
        body, name=name,
        out_shape=(pltpu.SemaphoreType.DMA((4 * n,)), pltpu.SemaphoreType.DMA((4 * n,)),
                   *[pltpu.HBM(t.shape, t.dtype) for t in shards + outs], jax.ShapeDtypeStruct((8, 128), F32)),
        in_specs=(*[HBM_ONLY] * (2 * n), pl.BlockSpec(memory_space=pl.ANY)),
        out_specs=(SEM_SPEC, SEM_SPEC, *[HBM_ONLY] * (2 * n), pl.BlockSpec(memory_space=pltpu.VMEM)),
        input_output_aliases={i: 2 + i for i in range(2 * n)},
        compiler_params=pltpu.CompilerParams(has_side_effects=SIDE_EFFECT))(
        *[pltpu.with_memory_space_constraint(t, pltpu.HBM) for t in shards + outs], after)
    return (res[0], res[1], list(res[2:2 + n]), list(res[2 + n:2 + 2 * n])), res[-1]


def _gather_wait(in_flight, after, name):
    send_sems, recv_sems, shards, outs = in_flight
    n = len(shards)

    def body(*refs):
        for cp in _gather_copies(refs[:n], refs[n:2 * n], refs[2 * n], refs[2 * n + 1]):
            cp.wait_send()
            cp.wait_recv()

    res = pl.pallas_call(
        body, name=name, out_shape=tuple(pltpu.HBM(t.shape, t.dtype) for t in shards + outs),
        in_specs=(*[HBM_ONLY] * (2 * n), SEM_SPEC, SEM_SPEC, pl.BlockSpec(memory_space=pl.ANY)),
        out_specs=(HBM_ONLY,) * (2 * n), input_output_aliases={i: i for i in range(2 * n)},
        compiler_params=pltpu.CompilerParams(has_side_effects=SIDE_EFFECT))(*shards, *outs, send_sems, recv_sems, after)
    return list(res[:n]), list(res[n:])


class _PassToSibling:
    def __init__(self, shards, gathered):
        n = self.n = len(shards)
        self.ins = list(shards) + list(gathered)
        self.out_shape = tuple(jax.ShapeDtypeStruct(g.shape, g.dtype) for g in gathered)
        self.aliases = {n + a: a for a in range(n)}
        self.sems = [pltpu.SemaphoreType.DMA((3 * n,)), pltpu.SemaphoreType.DMA((3 * n,)),
                     pltpu.SemaphoreType.DMA((n,))]

    def _copies(self, ins, outs, sems):
        send_sems, recv_sems, local_sems = sems
        x, y, c = _mesh_pos()
        chips = [(1 - x, y), (x, 1 - y), (1 - x, 1 - y)]
        mine = [pltpu.make_async_copy(ins[a], outs[a].at[_slot((x, y, c))], local_sems.at[a]) for a in range(self.n)]
        passed, awaited = [], []
        for a in range(self.n):
            for j, chip in enumerate(chips):
                sems_j = dict(send_sem=send_sems.at[3 * a + j], recv_sem=recv_sems.at[3 * a + j],
                              device_id=(x, y, 1 - c), device_id_type=MESH)
                blk = outs[a].at[_slot((*chip, c))]
                passed.append(pltpu.make_async_remote_copy(src_ref=blk, dst_ref=blk, **sems_j))
                got = outs[a].at[_slot((*chip, 1 - c))]
                awaited.append(pltpu.make_async_remote_copy(src_ref=got, dst_ref=got, **sems_j))
        return mine, passed, awaited

    def start(self, ins, outs, sems):
        mine, passed, _ = self._copies(ins, outs, sems)
        for cp in mine + passed:
            cp.start()

    def finish(self, ins, outs, sems):
        mine, passed, awaited = self._copies(ins, outs, sems)
        for cp in passed:
            cp.wait_send()
        for cp in awaited:
            cp.wait_recv()
        for cp in mine:
            cp.wait()


def _reduce_sums(fulls, recv_core, core, tag):
    return [_pair_sum(f, r, core, f"rs_pair_{tag}{i}") for i, (f, r) in enumerate(zip(fulls, recv_core))]


def _local_step(x, tgt, mods, w_in_shard, order, shards, small, chip, core):
    sh1, sc1, g1, sh2, sc2, g2 = mods
    norm1_g, rel_bias, gn_g, gn_b, norm2_g, norm_f_g = small
    tables = _ret_tables()
    buckets = jnp.asarray(_bucket_tables())

    h1 = _norm_mod_fwd(x, norm1_g, sh1, sc1, "norm1_fwd")
    proj, slabs, w_in_t = _gather_proj(h1, w_in_shard, order)
    flight_w1, token_w = _gather_start(list(shards[:3]), proj, "gather_w1_start")
    flight_w2, token_w = _gather_start(list(shards[3:]), token_w, "gather_w2_start")
    bias = _bias_build(rel_bias, buckets, token_w)
    outs, lses = [], []
    for gi in range(len(ATT_GROUPS)):
        o, l = _att_fwd(slabs, bias, gi)
        outs.append(o)
        lses.append(l)
    att, gathered = _mix_fwd(outs, lses, comm=_PassToSibling(*_gather_wait(flight_w1, lses[2], "gather_w1_wait")))
    w_ret_out, w_att_out, w_o = (_from_slots(g, ax) for g, ax in zip(gathered, BIG_AXES[1:4]))
    gated, ro, states = _ret_fwd(proj, tables, gn_g, gn_b, att)
    ret_out, gathered = _mm(gated, w_ret_out, 'nn', tm=S, tn=256, tk=2048, name="ret_out",
                            comm=_PassToSibling(*_gather_wait(flight_w2, gated, "gather_w2_wait")))
    w_ff1, w_ff2 = (_from_slots(g, ax) for g, ax in zip(gathered, BIG_AXES[4:]))
    att_out, merged = _att_out_merge(att, w_att_out, proj, ret_out)
    mixo, x1, h2 = _w_o_norm2(merged, w_o, x, g1, norm2_g, sh2, sc2)
    u, act = _mm(h2, w_ff1, 'nn', tm=S, tn=512, tk=D, name="ff1", relu2=True)
    loss, dx2, g_normf, df, dg2 = _ff2_final(act, w_ff2, x1, g2, tgt, norm_f_g)

    gw_ff2 = _mm(act, df, 'tn', tm=512, tn=D, tk=S, name="gw_ff2", out_dtype=BF16)
    du = _mm(df, w_ff2, 'nt', tm=S, tn=512, tk=D, name="d_act", out_dtype=BF16, relu2_of=u)
    gw_ff1 = _mm(h2, du, 'tn', tm=D, tn=512, tk=S, name="gw_ff1", out_dtype=BF16)
    fulls_a = [_to_slots(g, ax) for g, ax in zip((gw_ff1, gw_ff2), BIG_AXES[4:])]
    dh2, recv_core_a = _mm(du, w_ff1, 'nt', tm=1024, tn=1024, tk=2048, name="dh2", comm=_ExchangeCore(fulls_a))
    parts_a = _reduce_sums(fulls_a, recv_core_a, core, "a")
    flight_a, token_a = _chip_exchange_start(parts_a, "rs_a_start")
    dx1, dsc2, dsh2, g_norm2, dmixo, dg1 = _norm_mod_bwd(x1, norm2_g, sc2, dh2, dx2, "norm2_bwd", gate=(mixo, g1))

    gw_o = _mm(merged, dmixo, 'tn', tm=D, tn=512, tk=S, name="gw_o", out_dtype=BF16, after=token_a)
    d_ret_out, d_att_out, dga, dgb = _dmerged_split(dmixo, w_o, proj, ret_out, att_out)
    gw_ret_out = _mm(gated, d_ret_out, 'tn', tm=512, tn=D, tk=S, name="gw_ret_out", out_dtype=BF16)
    gw_att_out = _mm(att, d_att_out, 'tn', tm=AW, tn=D, tk=S, name="gw_att_out", out_dtype=BF16)
    fulls_b = [_to_slots(g, ax) for g, ax in zip((gw_ret_out, gw_att_out, gw_o), BIG_AXES[1:4])]
    dgated, recv_core_b = _mm(d_ret_out, w_ret_out, 'nt', tm=S, tn=512, tk=D, name="dgated",
                              comm=_ExchangeCore(fulls_b))
    parts_b = _reduce_sums(fulls_b, recv_core_b, core, "b")
    flight_b, token_b = _chip_exchange_start(parts_b, "rs_b_start")
    datt = _mm(d_att_out, w_att_out, 'nt', tm=S, tn=AW, tk=D, name="datt", after=token_b)
    mix_grads = _mix_bwd(outs, lses, datt)
    datt_parts, ds_sums = [], []
    for gi in range(len(ATT_GROUPS)):
        dq, dk, dv, ds_sum = _att_bwd(slabs, bias, outs[gi], lses[gi], mix_grads[gi], mix_grads[3 + gi], gi)
        datt_parts += [dq.reshape(S, AW), dk.reshape(S, AW), dv.reshape(S, AW)]
        ds_sums.append(ds_sum)
    g_bias = _bias_grad(jnp.concatenate(ds_sums, axis=0), buckets)[:, :, 0].T.reshape(1, -1)
    dproj, g_gn_g, g_gn_b = _ret_bwd(proj, tables, gn_g, gn_b, ro, states, dgated, datt_parts + [dga, dgb])
    parts_a, recv_chip_a = _chip_exchange_wait(flight_a, dproj, "rs_a_wait")
    parts_b, recv_chip_b = _chip_exchange_wait(flight_b, dproj, "rs_b_wait")
    reduced = list(zip(parts_b + parts_a, recv_chip_b + recv_chip_a))
    full_in = _to_slots(_mm(dproj, h1, 'tn', tm=512, tn=D, tk=S, name="gw_in", out_dtype=BF16), 0)
    in_flight, token = [], None
    for half in range(2):
        (recv_core_in,) = _run_comm(_ExchangeCore([full_in], cols=(half * (D // 2), D // 2)), f"rs_core_in{half}",
                                    after=token)
        part_in = [_pair_sum(full_in, recv_core_in, core, f"rs_pair_c{half}", col_block=half)]
        flight, token = _chip_exchange_start(part_in, f"rs_in{half}_start")
        in_flight.append(flight)
    dh1 = _mm(dproj, w_in_t, 'nn', tm=1024, tn=1024, tk=2560, name="dh1", after=token)
    gx, dsc1, dsh1, g_norm1 = _norm_mod_bwd(x, norm1_g, sc1, dh1, dx1, "norm1_bwd")

    dmod = [dsh1, dsc1, dg1, dsh2, dsc2, dg2]
    small_g = [g_norm1, g_bias, g_gn_g, g_gn_b, g_norm2, g_normf]
    return loss, gx, in_flight, reduced, small_g, dmod


def _to_slots(g, axis):
    if axis == 0:
        return g.reshape(4, 2, g.shape[0] // N_DEV, g.shape[1])
    return g.reshape(g.shape[0], N_DEV, g.shape[1] // N_DEV).transpose(1, 0, 2).reshape(4, 2, g.shape[0], -1)


def _from_slots(w8, axis):
    if axis == 0:
        return w8.reshape(-1, w8.shape[2])
    return w8.transpose(1, 0, 2).reshape(w8.shape[1], -1)


BIG_AXES = (1, 0, 1, 0, 1, 0)


def kernel(x, c, w_ada, b_ada, norm1_g, w_in, rel_bias, ret_gn_g, ret_gn_b, w_ret_out, w_att_out, w_o, norm2_g, w_ff1, w_ff2, norm_f_g, loss_target, m_w_ada, m_b_ada, m_norm1_g, m_w_in, m_rel_bias, m_ret_gn_g, m_ret_gn_b, m_w_ret_out, m_w_att_out, m_w_o, m_norm2_g, m_w_ff1, m_w_ff2, m_norm_f_g, v_w_ada, v_b_ada, v_norm1_g, v_w_in, v_rel_bias, v_ret_gn_g, v_ret_gn_b, v_w_ret_out, v_w_att_out, v_w_o, v_norm2_g, v_w_ff1, v_w_ff2, v_norm_f_g):
    mx, my, mc = _mesh_pos()
    dev = 4 * mx + 2 * my + mc
    chip = jnp.reshape(2 * mx + my, (1,)).astype(jnp.int32)
    core = jnp.reshape(mc, (1,)).astype(jnp.int32)
    ada_w = D * 6 // N_DEV

    w_in, m_w_in, v_w_in = (jnp.transpose(t, (0, 2, 1)) for t in (w_in, m_w_in, v_w_in))

    shards = [w[0].astype(BF16) for w in (w_in, w_ret_out, w_att_out, w_o, w_ff1, w_ff2)]
    (c_all,) = _run_comm(_Gather([c]), "gather_c")
    c_all = c_all.reshape(N_DEV, D)
    b_sl = lax.dynamic_slice(b_ada, (0, dev * ada_w), (1, ada_w))
    (mod_all,) = _run_comm(_Gather([_ada_fwd(c_all, w_ada[0], b_sl)]), "gather_mod")
    mod = lax.dynamic_index_in_dim(mod_all, dev, axis=1, keepdims=False).reshape(6, D)
    mods = tuple(mod[i:i + 1] for i in range(6))

    small = (norm1_g, rel_bias, ret_gn_g, ret_gn_b, norm2_g, norm_f_g.reshape(1, D))
    order = lax.dynamic_index_in_dim(jnp.asarray(_proj_order()), 2 * mx + my, axis=0, keepdims=False)
    loss, gx, in_flight, big_red, small_g, dmod = _local_step(x[0], loss_target[0], mods, shards[0], order,
                                                              shards[1:], small, chip, core)

    names = ['w_ada', 'b_ada', 'norm1_g', 'w_in', 'rel_bias', 'ret_gn_g', 'ret_gn_b', 'w_ret_out', 'w_att_out',
             'w_o', 'norm2_g', 'w_ff1', 'w_ff2', 'norm_f_g']
    ws = dict(zip(names, (w_ada, b_ada, norm1_g, w_in, rel_bias, ret_gn_g, ret_gn_b, w_ret_out, w_att_out, w_o,
                          norm2_g, w_ff1, w_ff2, norm_f_g)))
    ms = dict(zip(names, (m_w_ada, m_b_ada, m_norm1_g, m_w_in, m_rel_bias, m_ret_gn_g, m_ret_gn_b, m_w_ret_out,
                          m_w_att_out, m_w_o, m_norm2_g, m_w_ff1, m_w_ff2, m_norm_f_g)))
    vs = dict(zip(names, (v_w_ada, v_b_ada, v_norm1_g, v_w_in, v_rel_bias, v_ret_gn_g, v_ret_gn_b, v_w_ret_out,
                          v_w_att_out, v_w_o, v_norm2_g, v_w_ff1, v_w_ff2, v_norm_f_g)))
    grads, delta, new_m, new_v = {}, {}, {}, {}
    big_names = ('w_ret_out', 'w_att_out', 'w_o', 'w_ff1', 'w_ff2')
    for n, (part, recv) in zip(big_names, big_red):
        grads[n], delta[n], new_m[n], new_v[n] = _adamw_reduced1(ws[n], ms[n], vs[n], part, recv, chip, "adamw_" + n)
    updated = lax.optimization_barrier((gx, tuple(delta[n] for n in big_names)))
    gathered = _run_comm(_Gather(dmod + small_g + [loss]), "gather_small", after=updated[0])
    g_b_ada, dmod_all, (g_norm1, g_bias, g_gn_g, g_gn_b, g_norm2, g_normf, loss_sum) = _sum_small(gathered)
    loss_out = loss_sum[0, 0]
    g_w_ada = _ada_bwd(c_all, lax.dynamic_slice(dmod_all, (0, dev * ada_w), (N_DEV, ada_w)))

    grads.update(w_ada=g_w_ada.reshape(w_ada.shape), b_ada=g_b_ada, norm1_g=g_norm1, rel_bias=g_bias,
                 ret_gn_g=g_gn_g, ret_gn_b=g_gn_b, norm2_g=g_norm2, norm_f_g=g_normf)
    delta['w_ada'], new_m['w_ada'], new_v['w_ada'] = _adamw(w_ada, g_w_ada, m_w_ada, v_w_ada, "adamw_w_ada")
    small_names = ('b_ada', 'norm1_g', 'rel_bias', 'ret_gn_g', 'ret_gn_b', 'norm2_g', 'norm_f_g')
    two_d = {n: (1, ws[n].size) if ws[n].ndim == 1 else ws[n].shape for n in small_names}
    d_, m_, v_ = _adamw_small(*[[src[n].reshape(two_d[n]) for n in small_names] for src in (ws, grads, ms, vs)])
    for i, n in enumerate(small_names):
        shp = ws[n].shape
        delta[n], new_m[n], new_v[n] = d_[i].reshape(shp), m_[i].reshape(shp), v_[i].reshape(shp)
        grads[n] = grads[n].reshape(shp)

    done = lax.optimization_barrier((gx, tuple(d_), tuple(delta[n] for n in ('w_ada', 'w_ret_out', 'w_att_out', 'w_o',
                                                                               'w_ff1', 'w_ff2'))))
    parts_in, recvs_in = [], []
    for half, flight in enumerate(in_flight):
        (part_in,), (recv_chip_in,) = _chip_exchange_wait(flight, done[0], f"rs_in{half}_wait")
        parts_in.append(part_in)
        recvs_in.append(recv_chip_in)
    grads['w_in'], delta['w_in'], new_m['w_in'], new_v['w_in'] = _adamw_reduced(w_in, m_w_in, v_w_in, parts_in,
                                                                               recvs_in, chip)
    for d in (grads, delta, new_m, new_v):
        d['w_in'] = jnp.transpose(d['w_in'], (0, 2, 1))
    return (loss_out, gx[None], *[grads[n] for n in names], *[delta[n] for n in names],
            *[new_m[n] for n in names], *[new_v[n] for n in names])
```

```python
import functools
import math

import numpy as np
import jax
import jax.numpy as jnp
from jax import lax
from jax.experimental import pallas as pl
from jax.experimental.pallas import tpu as pltpu

F32 = jnp.float32
BF16 = jnp.bfloat16
MESH = pl.DeviceIdType.MESH

N_DEV = 8
S = 2048
D = 1024
RET_HEADS = 4
RET_DK = 256
RET_DV = 512
CHUNK = 128
N_CHUNK = S // CHUNK
ATT_GROUPS = ((128, 1), (512, 4), (2048, 16))
ATT_HG = 4
ATT_DH = 128
ATT_BLK = 128
N_BUCKETS = 32
MAX_DIST = 2048
D_FF = 4096
IN_COLS = 12800
OFF_RQ, OFF_RK, OFF_RV, OFF_RG, OFF_ATT = 0, 1024, 2048, 4096, 6144
OFF_GA, OFF_GB = 6144, 7168
RMS_EPS = 1e-6
GN_EPS = 1e-5
ADAM_LR, ADAM_B1, ADAM_B2, ADAM_EPS, ADAM_WD, ADAM_STEP = 0.001, 0.9, 0.999, 1e-08, 0.01, 10
VMEM_LIMIT = 48 * 1024 * 1024


def _is_float_array(a):
    return hasattr(a, "dtype") and hasattr(a, "shape") and jnp.issubdtype(a.dtype, jnp.floating)


def _pcall(body, **kw):
    out_shape = kw.pop("out_shape")
    single = not isinstance(out_shape, (tuple, list))
    pinned = [pltpu.HBM(o.shape, o.dtype) if type(o) is jax.ShapeDtypeStruct and _is_float_array(o) else o
              for o in ((out_shape,) if single else out_shape)]
    call = pl.pallas_call(body, out_shape=pinned[0] if single else tuple(pinned), **kw)
    grid_spec = kw.get("grid_spec")
    specs = [None] * grid_spec.num_scalar_prefetch + list(grid_spec.in_specs) if grid_spec else kw.get("in_specs")

    def pin(a, spec):
        in_smem = getattr(spec, "memory_space", None) is pltpu.SMEM
        return pltpu.with_memory_space_constraint(a, pltpu.HBM) if _is_float_array(a) and not in_smem else a

    return lambda *args: call(*[pin(a, spec) for a, spec in zip(args, specs or [None] * len(args), strict=True)])


def _params(sem=None):
    return pltpu.CompilerParams(dimension_semantics=sem, vmem_limit_bytes=VMEM_LIMIT)


HBM_SPEC = pl.BlockSpec(memory_space=pl.ANY)


def _carry(body, comm, *, name, grid, in_specs, out_specs, out_shape, scratch_shapes=()):
    single = not isinstance(out_specs, (tuple, list))
    o_specs = (out_specs,) if single else tuple(out_specs)
    o_shape = (out_shape,) if single else tuple(out_shape)
    n_in, n_out, n_scr = len(in_specs), len(o_specs), len(scratch_shapes)
    nci, nco = len(comm.ins), len(comm.out_shape)
    total = int(np.prod(grid))

    def wrapped(*refs):
        bounds = np.cumsum([0, n_in, nci, n_out, nco, n_scr])
        a, ci, o, co, scr = (refs[bounds[i]:bounds[i + 1]] for i in range(5))
        sems = refs[bounds[5]:]
        flat = 0
        for d, g in enumerate(grid):
            flat = flat * g + pl.program_id(d)

        @pl.when(flat == 0)
        def _():
            comm.start(ci, co, sems)

        body(*a, *o, *scr)

        @pl.when(flat == total - 1)
        def _():
            comm.finish(ci, co, sems)

    aliases = {n_in + i: n_out + o for i, o in getattr(comm, "aliases", {}).items()}
    call = _pcall(wrapped, name=name, grid=grid, in_specs=list(in_specs) + [HBM_SPEC] * nci,
                  out_specs=o_specs + (HBM_SPEC,) * nco, out_shape=o_shape + tuple(comm.out_shape),
                  scratch_shapes=list(scratch_shapes) + list(comm.sems), input_output_aliases=aliases,
                  compiler_params=_params(("arbitrary",) * len(grid)))

    def run(*args):
        res = call(*args, *comm.ins)
        own = res[0] if single else tuple(res[:n_out])
        return own, tuple(res[n_out:])

    return run


def _run_comm(comm, name, after=None):
    nci, nco = len(comm.ins), len(comm.out_shape)
    extra = [] if after is None else [after]

    def body(*refs):
        ci, co, sems = refs[:nci], refs[nci + len(extra):nci + len(extra) + nco], refs[nci + len(extra) + nco:]
        comm.start(ci, co, sems)
        comm.finish(ci, co, sems)

    return _pcall(body, name=name, in_specs=[HBM_SPEC] * (nci + len(extra)), out_specs=(HBM_SPEC,) * nco,
                  out_shape=tuple(comm.out_shape), scratch_shapes=list(comm.sems))(*comm.ins, *extra)


def _dot(a, b, dn):
    return lax.dot_general(a.astype(BF16), b.astype(BF16), (dn, ((), ())), preferred_element_type=F32)


NN = ((1,), (0,))
NT = ((1,), (1,))
TN = ((0,), (0,))


def _mm(a, b, mode, *, tm, tn, tk, name, out_dtype=F32, res=None, gvec=None, relu2=False, relu2_of=None, comm=None,
        after=None):
    if mode == 'nn':
        (M, K), (_, N) = a.shape, b.shape
        a_spec = pl.BlockSpec((tm, tk), lambda i, j, k: (i, k))
        b_spec = pl.BlockSpec((tk, tn), lambda i, j, k: (k, j))
        dn = NN
    elif mode == 'nt':
        (M, K), (N, _) = a.shape, b.shape
        a_spec = pl.BlockSpec((tm, tk), lambda i, j, k: (i, k))
        b_spec = pl.BlockSpec((tn, tk), lambda i, j, k: (j, k))
        dn = NT
    else:
        (K, M), (_, N) = a.shape, b.shape
        a_spec = pl.BlockSpec((tk, tm), lambda i, j, k: (k, i))
        b_spec = pl.BlockSpec((tk, tn), lambda i, j, k: (k, j))
        dn = TN
    assert M % tm == 0 and N % tn == 0 and K % tk == 0, (name, M, N, K)
    nk = K // tk
    fused = res is not None
    o_spec = pl.BlockSpec((tm, tn), lambda i, j, k: (i, j))

    def body(a_ref, b_ref, *rest):
        acc_ref = rest[-1] if nk > 1 else None
        if after is not None:
            rest = rest[1:]
        if fused:
            res_ref, g_ref, o_ref, x_ref = rest[:4]
        elif relu2_of is not None:
            u_ref, o_ref = rest[:2]
        elif relu2:
            o_ref, act_ref = rest[:2]
        else:
            o_ref = rest[0]

        def finish(acc):
            if relu2_of is not None:
                acc = acc * (2.0 * jnp.maximum(u_ref[...], 0.0))
            o_ref[...] = acc.astype(o_ref.dtype)
            if fused:
                x_ref[...] = res_ref[...] + g_ref[...] * acc
            if relu2:
                r = jnp.maximum(acc, 0.0)
                act_ref[...] = (r * r).astype(BF16)

        p = _dot(a_ref[...], b_ref[...], dn)
        if nk == 1:
            finish(p)
        else:
            k = pl.program_id(2)

            @pl.when(k == 0)
            def _():
                acc_ref[...] = p

            @pl.when(k > 0)
            def _():
                acc_ref[...] += p

            @pl.when(k == nk - 1)
            def _():
                finish(acc_ref[...])

    in_specs = [a_spec, b_spec]
    args = [a, b]
    if after is not None:
        in_specs.append(pl.BlockSpec(memory_space=pl.ANY))
        args.append(after)
    out_shape = jax.ShapeDtypeStruct((M, N), out_dtype)
    out_specs = o_spec
    if fused:
        in_specs += [pl.BlockSpec((tm, tn), lambda i, j, k: (i, j)), pl.BlockSpec((1, tn), lambda i, j, k: (0, j))]
        args += [res, gvec]
        out_shape = (out_shape, jax.ShapeDtypeStruct((M, N), F32))
        out_specs = (o_spec, pl.BlockSpec((tm, tn), lambda i, j, k: (i, j)))
    elif relu2_of is not None:
        in_specs.append(pl.BlockSpec((tm, tn), lambda i, j, k: (i, j)))
        args.append(relu2_of)
    elif relu2:
        out_shape = (out_shape, jax.ShapeDtypeStruct((M, N), BF16))
        out_specs = (o_spec, pl.BlockSpec((tm, tn), lambda i, j, k: (i, j)))
    kw = dict(name=name, grid=(M // tm, N // tn, nk), in_specs=in_specs, out_specs=out_specs,
              out_shape=out_shape, scratch_shapes=[pltpu.VMEM((tm, tn), F32)] if nk > 1 else [])
    if comm is not None:
        return _carry(body, comm, **kw)(*args)
    return _pcall(body, compiler_params=_params(("parallel", "parallel", "arbitrary")), **kw)(*args)


PROJ_TN = 512
ATT_T0, ATT_T1 = 6144 // PROJ_TN, 10752 // PROJ_TN
N_SLABS = (ATT_T1 - ATT_T0) * 4
MAIN_COLS = IN_COLS - (ATT_T1 - ATT_T0) * PROJ_TN


PROJ_TILES = IN_COLS // PROJ_TN
SHARD_ROWS = IN_COLS // N_DEV
W_CHUNKS = 4
N_OWN, N_NEAR = 5, 18


def _proj_order():
    out = np.zeros((4, 3, PROJ_TILES), np.int32)
    for q in range(4):
        def hops(t):
            owners = {col // (2 * SHARD_ROWS) for col in (t * PROJ_TN, (t + 1) * PROJ_TN - 1)}
            return max(bin(q ^ p).count("1") for p in owners)
        order = sorted(range(PROJ_TILES), key=lambda t: (hops(t), t))
        assert all(hops(t) == 0 for t in order[:N_OWN]) and all(hops(t) < 2 for t in order[:N_NEAR])
        is_att = [ATT_T0 <= t < ATT_T1 for t in order]
        for row, kind, index in ((1, False, lambda t: t if t < ATT_T0 else t - (ATT_T1 - ATT_T0)),
                                 (2, True, lambda t: t - ATT_T0)):
            own = [index(t) if a == kind else None for t, a in zip(order, is_att)]
            first = next(v for v in own if v is not None)
            last = first
            for j, v in enumerate(own):
                last = last if v is None else v
                out[q, row, j] = last
        out[q, 0] = order
    return out


def _gather_proj(h1, shard, order):
    rows = SHARD_ROWS // W_CHUNKS

    def body(ord_ref, a_ref, sh_ref, main_ref, slab_ref, full_ref, wbuf, fetch_sems, send_sems, recv_sems,
             local_sems):
        j = pl.program_id(0)
        x, y, c = _mesh_pos()
        me, sibling = (x, y, c), (x, y, 1 - c)
        chips = [(1 - x, y), (x, 1 - y), (1 - x, 1 - y)]

        def block(p, owner):
            return full_ref.at[pl.ds(pl.multiple_of(_slot(owner) * SHARD_ROWS + p * rows, 16), rows)]

        def copy(p, k, owner, to, from_input=False):
            dst = block(p, owner)
            return pltpu.make_async_remote_copy(
                src_ref=sh_ref.at[pl.ds(p * rows, rows)] if from_input else dst, dst_ref=dst,
                send_sem=send_sems.at[7 * p + k], recv_sem=recv_sems.at[7 * p + k], device_id=to, device_id_type=MESH)

        pieces = range(W_CHUNKS)
        mine = [pltpu.make_async_copy(sh_ref.at[pl.ds(p * rows, rows)], block(p, me), local_sems.at[p]) for p in pieces]
        first = [copy(p, 0, me, sibling, from_input=True) for p in pieces]
        first += [copy(p, 1 + n, me, (*chips[n], c), from_input=True) for p in pieces for n in range(2)]
        near_pass = [copy(p, 4 + n, (*chips[n], c), sibling) for p in pieces for n in range(2)]
        relay = [copy(p, 3, ((x + 1 - c) % 2, (y + c) % 2, c), ((x + c) % 2, (y + 1 - c) % 2, c)) for p in pieces]
        far_pass = [copy(p, 6, (*chips[2], c), sibling) for p in pieces]

        def fetch(pos):
            slot = lax.rem(pos, 2)
            start = pl.multiple_of(ord_ref[0, pos] * PROJ_TN, PROJ_TN)
            return pltpu.make_async_copy(full_ref.at[pl.ds(start, PROJ_TN)], wbuf.at[slot], fetch_sems.at[slot])

        @pl.when(j == 0)
        def _():
            for cp in mine + first:
                cp.start()
            for cp in mine:
                cp.wait()
            for p in pieces:
                copy(p, 0, sibling, me).wait_recv()
            fetch(j).start()

        @pl.when(j == N_OWN - 1)
        def _():
            for p in pieces:
                for n in range(2):
                    copy(p, 1 + n, (*chips[n], c), me).wait_recv()
                    near_pass[2 * p + n].start()
                relay[p].start()
            for p in pieces:
                for n in range(2):
                    copy(p, 4 + n, (*chips[n], 1 - c), me).wait_recv()

        @pl.when(j == N_NEAR - 1)
        def _():
            for p in pieces:
                copy(p, 3, (*chips[2], c), me).wait_recv()
                far_pass[p].start()
            for p in pieces:
                copy(p, 6, (*chips[2], 1 - c), me).wait_recv()

        @pl.when(j + 1 < PROJ_TILES)
        def _():
            fetch(j + 1).start()

        fetch(j).wait()
        w_ref = wbuf.at[lax.rem(j, 2)]
        tile = ord_ref[0, j]
        is_att = (tile >= ATT_T0) & (tile < ATT_T1)
        chunks = [pl.ds(r * 512, 512) for r in range(S // 512)]

        @pl.when(jnp.logical_not(is_att))
        def _():
            for rws in chunks:
                main_ref[rws, :] = _dot(a_ref[rws, :], w_ref[...], NT)

        @pl.when(is_att)
        def _():
            for rws in chunks:
                p = _dot(a_ref[rws, :], w_ref[...], NT)
                for h in range(4):
                    slab_ref[h, rws, :] = p[:, h * 128:(h + 1) * 128]

        @pl.when(j == PROJ_TILES - 1)
        def _():
            for cp in first + near_pass + relay + far_pass:
                cp.wait_send()

    gs = pltpu.PrefetchScalarGridSpec(
        num_scalar_prefetch=1, grid=(PROJ_TILES,),
        in_specs=[pl.BlockSpec((S, D), lambda j, o: (0, 0)), HBM_SPEC],
        out_specs=(pl.BlockSpec((S, PROJ_TN), lambda j, o: (0, o[1, j])),
                   pl.BlockSpec((4, S, 128), lambda j, o: (o[2, j], 0, 0)), HBM_SPEC),
        scratch_shapes=[pltpu.VMEM((2, PROJ_TN, D), BF16), pltpu.SemaphoreType.DMA((2,)),
                        pltpu.SemaphoreType.DMA((7 * W_CHUNKS,)), pltpu.SemaphoreType.DMA((7 * W_CHUNKS,)),
                        pltpu.SemaphoreType.DMA((W_CHUNKS,))])
    return _pcall(body, name="gather_proj", grid_spec=gs,
                  out_shape=(jax.ShapeDtypeStruct((S, MAIN_COLS), F32), jax.ShapeDtypeStruct((N_SLABS, S, 128), F32),
                             jax.ShapeDtypeStruct((IN_COLS, D), BF16)),
                  compiler_params=_params(("arbitrary",)))(order, h1, shard)


TR = 256


def _row_spec(w=D):
    return pl.BlockSpec((TR, w), lambda i: (i, 0))


def _vec_spec(w=D):
    return pl.BlockSpec((1, w), lambda i: (0, 0))


def _norm_mod_fwd(x, g, sh, sc, name):
    def body(x_ref, g_ref, sh_ref, sc_ref, o_ref):
        xv = x_ref[...]
        rstd = lax.rsqrt(jnp.mean(xv * xv, axis=-1, keepdims=True) + RMS_EPS)
        n = xv * rstd * g_ref[...]
        o_ref[...] = (n * (1.0 + sc_ref[...]) + sh_ref[...]).astype(BF16)

    return _pcall(body, name=name, grid=(S // TR,), in_specs=[_row_spec(), _vec_spec(), _vec_spec(), _vec_spec()],
                  out_specs=_row_spec(), out_shape=jax.ShapeDtypeStruct((S, D), BF16),
                  compiler_params=_params(("parallel",)))(x, g, sh, sc)


def _norm_mod_bwd(x, g, sc, dh, dres, name, gate=None):
    gated = gate is not None

    def body(x_ref, g_ref, sc_ref, dh_ref, dres_ref, *rest):
        if gated:
            f_ref, gv_ref, dx_ref, dsc_ref, dsh_ref, dg_ref, dz_ref, dgv_ref = rest
        else:
            dx_ref, dsc_ref, dsh_ref, dg_ref = rest
        i = pl.program_id(0)
        xv = x_ref[...]
        dh = dh_ref[...]
        rstd = lax.rsqrt(jnp.mean(xv * xv, axis=-1, keepdims=True) + RMS_EPS)
        xhat = xv * rstd
        gv = g_ref[...]
        dn = dh * (1.0 + sc_ref[...])
        dxhat = dn * gv
        dx = dres_ref[...] + rstd * (dxhat - xhat * jnp.mean(dxhat * xhat, axis=-1, keepdims=True))
        dx_ref[...] = dx
        sums = [(dsc_ref, jnp.sum(dh * (xhat * gv), axis=0, keepdims=True)),
                (dsh_ref, jnp.sum(dh, axis=0, keepdims=True)),
                (dg_ref, jnp.sum(dn * xhat, axis=0, keepdims=True))]
        if gated:
            dz_ref[...] = (dx * gv_ref[...]).astype(BF16)
            sums.append((dgv_ref, jnp.sum(dx * f_ref[...], axis=0, keepdims=True)))

        @pl.when(i == 0)
        def _():
            for ref, p in sums:
                ref[...] = p

        @pl.when(i > 0)
        def _():
            for ref, p in sums:
                ref[...] += p

    vec = jax.ShapeDtypeStruct((1, D), F32)
    in_specs = [_row_spec(), _vec_spec(), _vec_spec(), _row_spec(), _row_spec()]
    out_specs = [_row_spec(), _vec_spec(), _vec_spec(), _vec_spec()]
    out_shape = [jax.ShapeDtypeStruct((S, D), F32), vec, vec, vec]
    args = [x, g, sc, dh, dres]
    if gated:
        in_specs += [_row_spec(), _vec_spec()]
        out_specs += [_row_spec(), _vec_spec()]
        out_shape += [jax.ShapeDtypeStruct((S, D), BF16), vec]
        args += list(gate)
    return _pcall(body, name=name, grid=(S // TR,), in_specs=in_specs, out_specs=tuple(out_specs),
                  out_shape=tuple(out_shape), compiler_params=_params(("arbitrary",)))(*args)


def _w_o_norm2(merged, w_o, x, g1, g, sh, sc):
    def body(a_ref, b_ref, x_ref, g1_ref, g_ref, sh_ref, sc_ref, o_ref, x1_ref, h_ref):
        acc = _dot(a_ref[...], b_ref[...], NN)
        o_ref[...] = acc
        xv = x_ref[...] + g1_ref[...] * acc
        x1_ref[...] = xv
        rstd = lax.rsqrt(jnp.mean(xv * xv, axis=-1, keepdims=True) + RMS_EPS)
        h_ref[...] = (xv * rstd * g_ref[...] * (1.0 + sc_ref[...]) + sh_ref[...]).astype(BF16)

    rows = pl.BlockSpec((FF2_TM, D), lambda i: (i, 0))
    f32 = jax.ShapeDtypeStruct((S, D), F32)
    return _pcall(body, name="w_o_norm2", grid=(S // FF2_TM,),
                  in_specs=[rows, pl.BlockSpec((D, D), lambda i: (0, 0)), rows] + [_vec_spec()] * 4,
                  out_specs=(rows, rows, rows), out_shape=(f32, f32, jax.ShapeDtypeStruct((S, D), BF16)),
                  compiler_params=_params(("parallel",)))(merged, w_o, x, g1, g, sh, sc)


FF2_TM = 512


def _ff2_final(act, w_ff2, x1, g2, tgt, g):
    def body(a_ref, b_ref, x1_ref, g2_ref, t_ref, g_ref, loss_ref, dx_ref, dg_ref, df_ref, dg2_ref):
        i = pl.program_id(0)
        f = _dot(a_ref[...], b_ref[...], NN)
        g2v = g2_ref[...]
        xv = x1_ref[...] + g2v * f
        gv = g_ref[...]
        rstd = lax.rsqrt(jnp.mean(xv * xv, axis=-1, keepdims=True) + RMS_EPS)
        xhat = xv * rstd
        err = xhat * gv - t_ref[...]
        dy = err * (1.0 / D)
        dxhat = dy * gv
        dx = rstd * (dxhat - xhat * jnp.mean(dxhat * xhat, axis=-1, keepdims=True))
        dx_ref[...] = dx
        df_ref[...] = (dx * g2v).astype(BF16)
        p_g = jnp.sum(dy * xhat, axis=0, keepdims=True)
        p_g2 = jnp.sum(dx * f, axis=0, keepdims=True)
        p_l = jnp.zeros((1, 128), F32) + 0.5 * jnp.sum(jnp.mean(err * err, axis=-1, keepdims=True))

        @pl.when(i == 0)
        def _():
            dg_ref[...] = p_g
            dg2_ref[...] = p_g2
            loss_ref[...] = p_l

        @pl.when(i > 0)
        def _():
            dg_ref[...] += p_g
            dg2_ref[...] += p_g2
            loss_ref[...] += p_l

    vec = jax.ShapeDtypeStruct((1, D), F32)
    rows = lambda w: pl.BlockSpec((FF2_TM, w), lambda i: (i, 0))
    return _pcall(body, name="ff2_final", grid=(S // FF2_TM,),
                  in_specs=[rows(D_FF), pl.BlockSpec((D_FF, D), lambda i: (0, 0)), rows(D), _vec_spec(), rows(D),
                            _vec_spec()],
                  out_specs=(_vec_spec(128), rows(D), _vec_spec(), rows(D), _vec_spec()),
                  out_shape=(jax.ShapeDtypeStruct((1, 128), F32), jax.ShapeDtypeStruct((S, D), F32), vec,
                             jax.ShapeDtypeStruct((S, D), BF16), vec),
                  compiler_params=_params(("arbitrary",)))(act, w_ff2, x1, g2, tgt, g)


HALF = 512


MERGE_TM = 1024


def _merge_specs():
    blk = lambda off: pl.BlockSpec((MERGE_TM, HALF), lambda i, j: (i, off // HALF + j))
    return blk(OFF_GA), blk(OFF_GB), blk(0)


def _att_out_merge(att, w_att_out, proj, ret_out):
    def body(a_ref, b_ref, ga_ref, gb_ref, r_ref, o_ref, m_ref):
        acc = _dot(a_ref[...], b_ref[...], NN)
        o_ref[...] = acc
        m_ref[...] = (jax.nn.sigmoid(ga_ref[...]) * r_ref[...] + jax.nn.sigmoid(gb_ref[...]) * acc).astype(BF16)

    ga, gb, tile = _merge_specs()
    return _pcall(body, name="att_out", grid=(S // MERGE_TM, D // HALF),
                  in_specs=[pl.BlockSpec((MERGE_TM, AW), lambda i, j: (i, 0)), pl.BlockSpec((AW, HALF), lambda i, j: (0, j)),
                            ga, gb, tile],
                  out_specs=(tile, tile),
                  out_shape=(jax.ShapeDtypeStruct((S, D), F32), jax.ShapeDtypeStruct((S, D), BF16)),
                  compiler_params=_params(("parallel", "parallel")))(att, w_att_out, proj, proj, ret_out)


def _dmerged_split(dmixo, w_o, proj, ret_out, att_out):
    def body(a_ref, b_ref, ga_ref, gb_ref, r_ref, at_ref, dr_ref, da_ref, dga_ref, dgb_ref):
        dm = _dot(a_ref[...], b_ref[...], NT)
        sa = jax.nn.sigmoid(ga_ref[...])
        sb = jax.nn.sigmoid(gb_ref[...])
        dr_ref[...] = (dm * sa).astype(BF16)
        da_ref[...] = (dm * sb).astype(BF16)
        dga_ref[...] = (dm * r_ref[...] * (sa * (1.0 - sa))).astype(BF16)
        dgb_ref[...] = (dm * at_ref[...] * (sb * (1.0 - sb))).astype(BF16)

    ga, gb, tile = _merge_specs()
    o = jax.ShapeDtypeStruct((S, D), BF16)
    return _pcall(body, name="dmerged", grid=(S // MERGE_TM, D // HALF),
                  in_specs=[pl.BlockSpec((MERGE_TM, D), lambda i, j: (i, 0)), pl.BlockSpec((HALF, D), lambda i, j: (j, 0)),
                            ga, gb, tile, tile],
                  out_specs=(tile,) * 4, out_shape=(o, o, o, o),
                  compiler_params=_params(("parallel", "parallel")))(dmixo, w_o, proj, proj, ret_out, att_out)


def _ret_tables():
    H, C = RET_HEADS, CHUNK
    log_g = jnp.log1p(-(2.0 ** (-5.0 - jnp.arange(H, dtype=F32))))
    idx = jnp.arange(C, dtype=F32)
    rel = idx[:, None] - idx[None, :]
    inner = jnp.where(rel >= 0, jnp.exp(log_g[:, None, None] * jnp.maximum(rel, 0.0)), 0.0)
    qd = jnp.exp(log_g[:, None] * (idx + 1.0))[:, :, None]
    kd = jnp.exp(log_g[:, None] * (C - 1.0 - idx))[:, :, None]
    cd = jnp.broadcast_to(jnp.exp(log_g * C)[:, None, None], (H, 1, 128))
    half = RET_DK // 2
    inv = 10000.0 ** (-jnp.arange(half, dtype=F32) / half)
    ang = jnp.arange(S, dtype=F32)[:, None] * inv[None, :]
    return inner, qd, kd, cd, jnp.cos(ang), jnp.sin(ang)


def _rot(x, cos, sin):
    x1, x2 = x[:, :128], x[:, 128:]
    return jnp.concatenate([x1 * cos - x2 * sin, x1 * sin + x2 * cos], axis=1)


def _rot_t(d, cos, sin):
    d1, d2 = d[:, :128], d[:, 128:]
    return jnp.concatenate([d1 * cos + d2 * sin, d2 * cos - d1 * sin], axis=1)


RET_COLS = OFF_ATT
RET_VW = RET_HEADS * RET_DV


def _ret_specs(chunk_of):
    ci = chunk_of
    whole = lambda shape: pl.BlockSpec(shape, lambda t: (0,) * len(shape))
    return [
        pl.BlockSpec((CHUNK, RET_COLS), lambda t: (ci(t), 0)),
        pl.BlockSpec((CHUNK, 128), lambda t: (ci(t), 0)),
        pl.BlockSpec((CHUNK, 128), lambda t: (ci(t), 0)),
        whole((RET_HEADS, CHUNK, CHUNK)), whole((RET_HEADS, CHUNK, 1)), whole((RET_HEADS, CHUNK, 1)),
        whole((RET_HEADS, 1, 128)), whole((1, RET_VW)), whole((1, RET_VW)),
    ]


def _ret_cols(h):
    q = slice(OFF_RQ + h * RET_DK, OFF_RQ + (h + 1) * RET_DK)
    k = slice(OFF_RK + h * RET_DK, OFF_RK + (h + 1) * RET_DK)
    v = slice(OFF_RV + h * RET_DV, OFF_RV + (h + 1) * RET_DV)
    g = slice(OFF_RG + h * RET_DV, OFF_RG + (h + 1) * RET_DV)
    return q, k, v, g, slice(h * RET_DV, (h + 1) * RET_DV)


def _ret_fwd(proj, tables, gn_g, gn_b, after):
    inner, qd, kd, cd, cos, sin = tables

    def body(x_ref, cos_ref, sin_ref, in_ref, qd_ref, kd_ref, cd_ref, g_ref, b_ref, after_ref,
             gated_ref, ro_ref, st_ref, s_scr):
        i = pl.program_id(0)

        @pl.when(i == 0)
        def _():
            s_scr[...] = jnp.zeros_like(s_scr)

        cosv, sinv = cos_ref[...], sin_ref[...]
        for h in range(RET_HEADS):
            cq, ck, cv, cg, co = _ret_cols(h)
            q = _rot(x_ref[:, cq], cosv, sinv)
            k = _rot(x_ref[:, ck], cosv, sinv) * (RET_DK ** -0.5)
            v = x_ref[:, cv]
            st = s_scr[h]
            st_ref[h] = st.astype(BF16)
            s = _dot(q, k, NT) * in_ref[h]
            o = _dot(s, v, NN) + _dot(q, st, NN) * qd_ref[h]
            s_scr[h] = st * cd_ref[h, :, :1] + _dot(k * kd_ref[h], v, TN)
            ro_ref[:, co] = o
            mu = jnp.mean(o, axis=-1, keepdims=True)
            oc = o - mu
            var = jnp.mean(oc * oc, axis=-1, keepdims=True)
            rn = oc * lax.rsqrt(var + GN_EPS) * g_ref[:, co] + b_ref[:, co]
            rg = x_ref[:, cg]
            gated_ref[:, co] = (rg * jax.nn.sigmoid(rg) * rn).astype(BF16)

    ospec = pl.BlockSpec((CHUNK, RET_VW), lambda t: (t, 0))
    return _pcall(
        body, name="ret_fwd", grid=(N_CHUNK,), in_specs=_ret_specs(lambda t: t) + [HBM_SPEC],
        out_specs=(ospec, ospec, pl.BlockSpec((RET_HEADS, None, RET_DK, RET_DV), lambda t: (0, t, 0, 0))),
        out_shape=(jax.ShapeDtypeStruct((S, RET_VW), BF16), jax.ShapeDtypeStruct((S, RET_VW), F32),
                   jax.ShapeDtypeStruct((RET_HEADS, N_CHUNK, RET_DK, RET_DV), BF16)),
        scratch_shapes=[pltpu.VMEM((RET_HEADS, RET_DK, RET_DV), F32)],
        compiler_params=_params(("arbitrary",)))(proj, cos, sin, inner, qd, kd, cd, gn_g, gn_b, after)


def _ret_bwd(proj, tables, gn_g, gn_b, ro, states, dgated, others):
    inner, qd, kd, cd, cos, sin = tables
    last = N_CHUNK - 1
    assert RET_COLS + sum(o.shape[1] for o in others) == IN_COLS

    def body(x_ref, cos_ref, sin_ref, in_ref, qd_ref, kd_ref, cd_ref, g_ref, b_ref, ro_ref, st_ref, dg_ref, *rest):
        other_refs, (dx_ref, gg_ref, gb_ref, gs_scr) = rest[:len(others)], rest[len(others):]
        t = pl.program_id(0)
        col = RET_COLS
        for o_ref in other_refs:
            dx_ref[:, col:col + o_ref.shape[1]] = o_ref[...]
            col += o_ref.shape[1]

        @pl.when(t == 0)
        def _():
            gs_scr[...] = jnp.zeros_like(gs_scr)
            gg_ref[...] = jnp.zeros_like(gg_ref)
            gb_ref[...] = jnp.zeros_like(gb_ref)

        cosv, sinv = cos_ref[...], sin_ref[...]
        for h in range(RET_HEADS):
            cq, ck, cv, cg, co = _ret_cols(h)
            q = _rot(x_ref[:, cq], cosv, sinv)
            k = _rot(x_ref[:, ck], cosv, sinv) * (RET_DK ** -0.5)
            v = x_ref[:, cv]
            qdv, kdv, dm = qd_ref[h], kd_ref[h], in_ref[h]
            st = st_ref[h]
            o = ro_ref[:, co]
            gv = g_ref[:, co]
            mu = jnp.mean(o, axis=-1, keepdims=True)
            oc = o - mu
            rstd = lax.rsqrt(jnp.mean(oc * oc, axis=-1, keepdims=True) + GN_EPS)
            ohat = oc * rstd
            rn = ohat * gv + b_ref[:, co]
            rg = x_ref[:, cg]
            sg = jax.nn.sigmoid(rg)
            dgt = dg_ref[:, co]
            drn = dgt * (rg * sg)
            dx_ref[:, cg] = (dgt * rn * (sg * (1.0 + rg * (1.0 - sg)))).astype(BF16)
            gg_ref[:, co] += jnp.sum(drn * ohat, axis=0, keepdims=True)
            gb_ref[:, co] += jnp.sum(drn, axis=0, keepdims=True)
            dohat = drn * gv
            do = rstd * (dohat - jnp.mean(dohat, axis=-1, keepdims=True)
                         - ohat * jnp.mean(dohat * ohat, axis=-1, keepdims=True))
            gs = gs_scr[h]
            s = _dot(q, k, NT) * dm
            dsr = _dot(do, v, NT) * dm
            dq = _dot(dsr, k, NN) + _dot(do, st, NT) * qdv
            dk = _dot(dsr, q, TN) + _dot(v, gs, NT) * kdv
            dv = _dot(s, do, TN) + _dot(k * kdv, gs, NN)
            gs_scr[h] = gs * cd_ref[h, :, :1] + _dot(q * qdv, do, TN)
            dx_ref[:, cq] = _rot_t(dq, cosv, sinv).astype(BF16)
            dx_ref[:, ck] = (_rot_t(dk, cosv, sinv) * (RET_DK ** -0.5)).astype(BF16)
            dx_ref[:, cv] = dv.astype(BF16)

    rev = lambda t: last - t
    vblk = pl.BlockSpec((CHUNK, RET_VW), lambda t: (rev(t), 0))
    vspec = pl.BlockSpec((1, RET_VW), lambda t: (0, 0))
    rows = lambda w: pl.BlockSpec((CHUNK, w), lambda t: (rev(t), 0))
    return _pcall(
        body, name="ret_bwd", grid=(N_CHUNK,),
        in_specs=_ret_specs(rev) + [vblk, pl.BlockSpec((RET_HEADS, None, RET_DK, RET_DV), lambda t: (0, rev(t), 0, 0)),
                                    vblk] + [rows(o.shape[1]) for o in others],
        out_specs=(rows(IN_COLS), vspec, vspec),
        out_shape=(jax.ShapeDtypeStruct((S, IN_COLS), BF16), jax.ShapeDtypeStruct((1, RET_VW), F32),
                   jax.ShapeDtypeStruct((1, RET_VW), F32)),
        scratch_shapes=[pltpu.VMEM((RET_HEADS, RET_DK, RET_DV), F32)],
        compiler_params=_params(("arbitrary",)))(proj, cos, sin, inner, qd, kd, cd, gn_g, gn_b, ro, states, dgated,
                                                 *others)


def _bucket_tables():
    qi = np.arange(ATT_BLK)[:, None]
    kj = np.arange(2 * ATT_BLK)[None, :]
    m = ATT_BLK + qi - kj
    out = []
    for win, dil in ATT_GROUPS:
        w = win // dil
        dist = (np.clip(m, 0, w) * dil).astype(np.int32)
        max_exact = N_BUCKETS // 2
        d_f = np.maximum(dist, 1).astype(np.float32)
        large = max_exact + (np.log(d_f / np.float32(max_exact)) / np.float32(math.log(MAX_DIST / max_exact))
                             * np.float32(N_BUCKETS - max_exact)).astype(np.int32)
        large = np.minimum(large, N_BUCKETS - 1)
        out.append(np.where(dist < max_exact, dist, large).astype(np.int32))
    return np.stack(out)


def _bias_build(rel_bias, buckets, after):
    def body(tab_ref, bk_ref, after_ref, o_ref):
        hh = pl.program_id(0)
        bk = bk_ref[...]
        acc = jnp.zeros((ATT_BLK, 2 * ATT_BLK), F32)
        for b in range(N_BUCKETS):
            acc = jnp.where(bk == b, tab_ref[b, hh], acc)
        o_ref[...] = acc

    nh = len(ATT_GROUPS) * ATT_HG
    return _pcall(body, name="bias_build", grid=(nh,),
                  in_specs=[pl.BlockSpec(memory_space=pltpu.SMEM),
                            pl.BlockSpec((None, ATT_BLK, 2 * ATT_BLK), lambda hh: (hh // ATT_HG, 0, 0)), HBM_SPEC],
                  out_specs=pl.BlockSpec((None, ATT_BLK, 2 * ATT_BLK), lambda hh: (hh, 0, 0)),
                  out_shape=jax.ShapeDtypeStruct((nh, ATT_BLK, 2 * ATT_BLK), F32),
                  compiler_params=_params(("parallel",)))(rel_bias, buckets, after)


def _bias_grad(ds_sum, buckets):
    def body(ds_ref, bk_ref, o_ref):
        bk = bk_ref[...]
        ds = ds_ref[...]
        rows = lax.broadcasted_iota(jnp.int32, (N_BUCKETS, 128), 0)
        acc = jnp.zeros((N_BUCKETS, 128), F32)
        for b in range(N_BUCKETS):
            acc = jnp.where(rows == b, jnp.sum(jnp.where(bk == b, ds, 0.0)), acc)
        o_ref[...] = acc

    nh = len(ATT_GROUPS) * ATT_HG
    return _pcall(body, name="bias_grad", grid=(nh,),
                  in_specs=[pl.BlockSpec((None, ATT_BLK, 2 * ATT_BLK), lambda hh: (hh, 0, 0)),
                            pl.BlockSpec((None, ATT_BLK, 2 * ATT_BLK), lambda hh: (hh // ATT_HG, 0, 0))],
                  out_specs=pl.BlockSpec((None, N_BUCKETS, 128), lambda hh: (hh, 0, 0)),
                  out_shape=jax.ShapeDtypeStruct((nh, N_BUCKETS, 128), F32),
                  compiler_params=_params(("parallel",)))(ds_sum, buckets)


def _att_valid(n):
    qi = lax.broadcasted_iota(jnp.int32, (ATT_BLK, 2 * ATT_BLK), 0)
    kj = lax.broadcasted_iota(jnp.int32, (ATT_BLK, 2 * ATT_BLK), 1)
    m = ATT_BLK + qi - kj
    first_key = jnp.where(n > 0, 0, ATT_BLK)
    return (m >= 0) & (m <= ATT_BLK) & (kj >= first_key)


ATT_HP = (1, 2, 2)


def _att_geometry(gi):
    _, dil = ATT_GROUPS[gi]
    return dil, S // dil // ATT_BLK, ATT_HP[gi]


def _blk(dil, r, n):
    if dil == 1:
        return pl.ds(n * ATT_BLK, ATT_BLK)
    return pl.ds(r + n * ATT_BLK * dil, ATT_BLK, stride=dil)


def _slab_specs(gi):
    _, _, hp = _att_geometry(gi)
    per = ATT_HG // hp
    return [pl.BlockSpec((hp, S, ATT_DH), lambda g, r, part=part: ((3 * gi + part) * per + g, 0, 0))
            for part in range(3)]


def _head_specs(gi, count):
    _, _, hp = _att_geometry(gi)
    return [pl.BlockSpec((hp, S, ATT_DH), lambda g, r: (g, 0, 0))] * count


def _bias_spec(gi):
    _, _, hp = _att_geometry(gi)
    return pl.BlockSpec((hp, ATT_BLK, 2 * ATT_BLK), lambda g, r: (gi * (ATT_HG // hp) + g, 0, 0))


def _att_valid_first():
    qi = lax.broadcasted_iota(jnp.int32, (ATT_BLK, ATT_BLK), 0)
    kj = lax.broadcasted_iota(jnp.int32, (ATT_BLK, ATT_BLK), 1)
    return kj <= qi


def _att_fwd(slabs, bias, gi, comm=None):
    dil, nb, hp = _att_geometry(gi)
    scale = ATT_DH ** -0.5

    def body(q_ref, k_ref, v_ref, bias_ref, o_ref, l_ref):
        r = pl.program_id(1)
        for n in range(nb):
            cur = _blk(dil, r, n)
            valid = _att_valid(n) if n > 0 else _att_valid_first()
            for h in range(hp):
                if n > 0:
                    prev = _blk(dil, r, n - 1)
                    kk = jnp.concatenate([k_ref[h, prev, :], k_ref[h, cur, :]], axis=0)
                    vv = jnp.concatenate([v_ref[h, prev, :], v_ref[h, cur, :]], axis=0)
                    bias = bias_ref[h]
                else:
                    kk, vv, bias = k_ref[h, cur, :], v_ref[h, cur, :], bias_ref[h, :, pl.ds(ATT_BLK, ATT_BLK)]
                s = _dot(q_ref[h, cur, :], kk, NT) * scale + bias
                s = jnp.where(valid, s, -1e30)
                mx = jnp.max(s, axis=-1, keepdims=True)
                e = jnp.exp(s - mx)
                den = jnp.sum(e, axis=-1, keepdims=True)
                o_ref[h, cur, :] = _dot(e / den, vv, NN)
                l_ref[h, cur, :] = jnp.broadcast_to(mx + jnp.log(den), (ATT_BLK, ATT_DH))

    osh = jax.ShapeDtypeStruct((ATT_HG, S, ATT_DH), F32)
    kw = dict(name=f"att_fwd{gi}", grid=(ATT_HG // hp, dil), in_specs=_slab_specs(gi) + [_bias_spec(gi)],
              out_specs=tuple(_head_specs(gi, 2)), out_shape=(osh, osh))
    if comm is not None:
        return _carry(body, comm, **kw)(slabs, slabs, slabs, bias)
    return _pcall(body, compiler_params=_params(("parallel", "arbitrary")), **kw)(slabs, slabs, slabs, bias)


def _att_bwd(slabs, bias, o, lse, do, dlse, gi, comm=None):
    dil, nb, hp = _att_geometry(gi)
    per = ATT_HG // hp
    scale = ATT_DH ** -0.5
    wh = hp * ATT_DH
    wide = lambda t: jnp.concatenate([t, t], axis=1)

    def body(q_ref, k_ref, v_ref, bias_ref, o_ref, l_ref, do_ref, dl_ref, dq_ref, dk_ref, dv_ref, ds_ref):
        r = pl.program_id(1)

        @pl.when(r == 0)
        def _():
            ds_ref[...] = jnp.zeros_like(ds_ref)

        for h in range(hp):
            sl = slice(h * ATT_DH, (h + 1) * ATT_DH)
            carry_k = carry_v = None
            for n in range(nb):
                cur = _blk(dil, r, n)
                q = q_ref[h, cur, :]
                dov = do_ref[h, cur, :]
                delta = jnp.sum(dov * o_ref[h, cur, :], axis=-1, keepdims=True)
                out_rows = pl.ds(n * ATT_BLK, ATT_BLK)
                if n == 0:
                    own = pl.ds(ATT_BLK, ATT_BLK)
                    kk, vv = k_ref[h, cur, :], v_ref[h, cur, :]
                    s = _dot(q, kk, NT) * scale + bias_ref[h, :, own]
                    p = jnp.where(_att_valid_first(), jnp.exp(s - l_ref[h, cur, :]), 0.0)
                    ds = p * (_dot(dov, vv, NT) - delta + dl_ref[h, cur, :])
                    ds_ref[h, :, own] += ds
                    dq_ref[out_rows, sl] = (_dot(ds, kk, NN) * scale).astype(BF16)
                    carry_k, carry_v = _dot(ds, q, TN) * scale, _dot(p, dov, TN)
                    continue
                prev = _blk(dil, r, n - 1)
                kk = jnp.concatenate([k_ref[h, prev, :], k_ref[h, cur, :]], axis=0)
                vv = jnp.concatenate([v_ref[h, prev, :], v_ref[h, cur, :]], axis=0)
                s = _dot(q, kk, NT) * scale + bias_ref[h]
                p = jnp.where(_att_valid(n), jnp.exp(s - wide(l_ref[h, cur, :])), 0.0)
                dp = _dot(dov, vv, NT)
                ds = p * (dp - delta + wide(dl_ref[h, cur, :]))
                ds_ref[h] += ds
                dq_ref[out_rows, sl] = (_dot(ds, kk, NN) * scale).astype(BF16)
                dkk = _dot(ds, q, TN) * scale
                dvv = _dot(p, dov, TN)
                before = pl.ds((n - 1) * ATT_BLK, ATT_BLK)
                dk_ref[before, sl] = (carry_k + dkk[:ATT_BLK]).astype(BF16)
                dv_ref[before, sl] = (carry_v + dvv[:ATT_BLK]).astype(BF16)
                carry_k, carry_v = dkk[ATT_BLK:], dvv[ATT_BLK:]
            last = pl.ds((nb - 1) * ATT_BLK, ATT_BLK)
            dk_ref[last, sl] = carry_k.astype(BF16)
            dv_ref[last, sl] = carry_v.astype(BF16)

    out_spec = pl.BlockSpec((S // dil, wh), lambda g, r: (0, r * per + g))
    osh = jax.ShapeDtypeStruct((S // dil, dil * AW), BF16)
    kw = dict(name=f"att_bwd{gi}", grid=(per, dil), in_specs=_slab_specs(gi) + [_bias_spec(gi)] + _head_specs(gi, 4),
              out_specs=(out_spec, out_spec, out_spec,
                         pl.BlockSpec((hp, ATT_BLK, 2 * ATT_BLK), lambda g, r: (g, 0, 0))),
              out_shape=(osh, osh, osh, jax.ShapeDtypeStruct((ATT_HG, ATT_BLK, 2 * ATT_BLK), F32)))
    args = (slabs, slabs, slabs, bias, o, lse, do, dlse)
    if comm is not None:
        return _carry(body, comm, **kw)(*args)
    return _pcall(body, compiler_params=_params(("arbitrary", "arbitrary")), **kw)(*args)


AW = ATT_HG * ATT_DH


def _mix_weights(l0, l1, l2):
    mx = jnp.maximum(jnp.maximum(l0, l1), l2)
    e0, e1, e2 = jnp.exp(l0 - mx), jnp.exp(l1 - mx), jnp.exp(l2 - mx)
    den = e0 + e1 + e2
    return e0 / den, e1 / den, e2 / den


def _heads_spec():
    return pl.BlockSpec((ATT_HG, TR, ATT_DH), lambda i: (0, i, 0))


def _mix_fwd(os_, ls, comm=None):
    def body(o0, o1, o2, l0, l1, l2, att_ref):
        for h in range(ATT_HG):
            w0, w1, w2 = _mix_weights(l0[h], l1[h], l2[h])
            att_ref[:, h * ATT_DH:(h + 1) * ATT_DH] = (w0 * o0[h] + w1 * o1[h] + w2 * o2[h]).astype(BF16)

    kw = dict(name="mix_fwd", grid=(S // TR,), in_specs=[_heads_spec()] * 6, out_specs=_row_spec(AW),
              out_shape=jax.ShapeDtypeStruct((S, AW), BF16))
    if comm is not None:
        return _carry(body, comm, **kw)(*os_, *ls)
    return _pcall(body, compiler_params=_params(("parallel",)), **kw)(*os_, *ls)


def _mix_bwd(os_, ls, datt):
    def body(o0, o1, o2, l0, l1, l2, da_ref, d0, d1, d2, e0, e1, e2):
        for h in range(ATT_HG):
            ws = _mix_weights(l0[h], l1[h], l2[h])
            da = da_ref[:, h * ATT_DH:(h + 1) * ATT_DH]
            dws = []
            for o_ref, w, d_ref in zip((o0, o1, o2), ws, (d0, d1, d2)):
                d_ref[h] = w * da
                dws.append(jnp.broadcast_to(jnp.sum(da * o_ref[h], axis=-1, keepdims=True), (TR, ATT_DH)))
            tot = ws[0] * dws[0] + ws[1] * dws[1] + ws[2] * dws[2]
            for w, dw, e_ref in zip(ws, dws, (e0, e1, e2)):
                e_ref[h] = w * (dw - tot)

    o = jax.ShapeDtypeStruct((ATT_HG, S, ATT_DH), F32)
    return _pcall(body, name="mix_bwd", grid=(S // TR,), in_specs=[_heads_spec()] * 6 + [_row_spec(AW)],
                  out_specs=(_heads_spec(),) * 6, out_shape=(o,) * 6,
                  compiler_params=_params(("parallel",)))(*os_, *ls, datt)


def _ada_fwd(c_all, w_sh, b_sl):
    def body(c_ref, w_ref, b_ref, o_ref):
        cv = c_ref[...]
        o_ref[...] = _dot(cv * jax.nn.sigmoid(cv), w_ref[...], NN) + b_ref[...]

    return _pcall(body, name="ada_fwd", out_shape=jax.ShapeDtypeStruct((N_DEV, w_sh.shape[1]), F32),
                  compiler_params=_params())(c_all, w_sh, b_sl)


def _ada_bwd(c_all, dm_sl):
    def body(c_ref, d_ref, o_ref):
        cv = c_ref[...]
        o_ref[...] = _dot(cv * jax.nn.sigmoid(cv), d_ref[...], TN)

    return _pcall(body, name="ada_bwd", out_shape=jax.ShapeDtypeStruct((D, dm_sl.shape[1]), F32),
                  compiler_params=_params())(c_all, dm_sl)


N_MOD = 6


def _sum_small(gathered):
    n = len(gathered)

    def body(*refs):
        ins, (gb_ref, dm_ref), outs = refs[:n], refs[n:n + 2], refs[n + 2:]

        def total(r):
            acc = r[0]
            for e in range(1, N_DEV):
                acc = acc + r[e]
            return acc

        for i in range(N_MOD):
            cols = slice(i * D, (i + 1) * D)
            gb_ref[:, cols] = total(ins[i])
            for e in range(N_DEV):
                dm_ref[e:e + 1, cols] = ins[i][e]
        for r, o_ref in zip(ins[N_MOD:], outs):
            o_ref[...] = total(r)

    shapes = (jax.ShapeDtypeStruct((1, N_MOD * D), F32), jax.ShapeDtypeStruct((N_DEV, N_MOD * D), F32),
              *[jax.ShapeDtypeStruct(g.shape[1:], F32) for g in gathered[N_MOD:]])
    res = _pcall(body, name="sum_small", out_shape=shapes, compiler_params=_params())(*gathered)
    return res[0], res[1], res[2:]


def _row_tile(m, n):
    t = max(8, min(m, (1 << 19) // n // 8 * 8))
    while m % t:
        t -= 8
    return t


def _pair_sum(full, recv, sel, name, col_block=0):
    _, m, n = recv.shape
    t = _row_tile(m, n)

    def body(sel_ref, a_ref, b_ref, o_ref):
        o_ref[...] = (a_ref[...].astype(F32) + b_ref[...].astype(F32)).astype(o_ref.dtype)

    gs = pltpu.PrefetchScalarGridSpec(
        num_scalar_prefetch=1, grid=(4, m // t),
        in_specs=[pl.BlockSpec((None, None, t, n), lambda q, i, s: (q, s[0], i, col_block)),
                  pl.BlockSpec((None, t, n), lambda q, i, s: (q, i, 0))],
        out_specs=pl.BlockSpec((None, t, n), lambda q, i, s: (q, i, 0)))
    return _pcall(body, name=name, grid_spec=gs, out_shape=jax.ShapeDtypeStruct((4, m, n), full.dtype),
                  compiler_params=_params(("parallel", "parallel")))(sel, full, recv)


def _chip_sum(part, recv, sel, name):
    _, m, n = part.shape
    t = _row_tile(m, n)

    def body(sel_ref, a_ref, r_ref, o_ref):
        o_ref[...] = ((a_ref[...].astype(F32) + r_ref[0].astype(F32)) + r_ref[1].astype(F32)) + r_ref[2].astype(F32)

    gs = pltpu.PrefetchScalarGridSpec(
        num_scalar_prefetch=1, grid=(m // t,),
        in_specs=[pl.BlockSpec((None, t, n), lambda i, s: (s[0], i, 0)),
                  pl.BlockSpec((3, t, n), lambda i, s: (0, i, 0))],
        out_specs=pl.BlockSpec((t, n), lambda i, s: (i, 0)))
    return _pcall(body, name=name, grid_spec=gs, out_shape=jax.ShapeDtypeStruct((m, n), F32),
                  compiler_params=_params(("parallel",)))(sel, part, recv)


def _adamw_math(w, g, m, v):
    nm = ADAM_B1 * m + (1.0 - ADAM_B1) * g
    nv = ADAM_B2 * v + (1.0 - ADAM_B2) * (g * g)
    m_hat = nm / (1.0 - ADAM_B1 ** ADAM_STEP)
    v_hat = nv / (1.0 - ADAM_B2 ** ADAM_STEP)
    return -ADAM_LR * (m_hat / (jnp.sqrt(v_hat) + ADAM_EPS) + ADAM_WD * w), nm, nv


def _adamw(w, g, m, v, name):
    _, rows, cols = w.shape
    t = _row_tile(rows, cols)

    def body(w_ref, g_ref, m_ref, v_ref, d_ref, nm_ref, nv_ref):
        d_ref[...], nm_ref[...], nv_ref[...] = _adamw_math(w_ref[...], g_ref[...], m_ref[...], v_ref[...])

    spec3 = pl.BlockSpec((None, t, cols), lambda i: (0, i, 0))
    spec2 = pl.BlockSpec((t, cols), lambda i: (i, 0))
    o = jax.ShapeDtypeStruct(w.shape, F32)
    return _pcall(body, name=name, grid=(rows // t,), in_specs=[spec3, spec2, spec3, spec3], out_specs=(spec3,) * 3,
                  out_shape=(o, o, o), compiler_params=_params(("parallel",)))(w, g, m, v)


def _adamw_reduced1(w, m, v, part, recv, sel, name):
    _, rows, cols = w.shape
    t = _row_tile(rows, cols)

    def body(sel_ref, w_ref, m_ref, v_ref, p_ref, r_ref, g_ref, d_ref, nm_ref, nv_ref):
        g = ((p_ref[...].astype(F32) + r_ref[0].astype(F32)) + r_ref[1].astype(F32)) + r_ref[2].astype(F32)
        g_ref[...] = g
        d_ref[...], nm_ref[...], nv_ref[...] = _adamw_math(w_ref[...], g, m_ref[...], v_ref[...])

    wspec = pl.BlockSpec((None, t, cols), lambda i, s: (0, i, 0))
    gs = pltpu.PrefetchScalarGridSpec(
        num_scalar_prefetch=1, grid=(rows // t,),
        in_specs=[wspec, wspec, wspec, pl.BlockSpec((None, t, cols), lambda i, s: (s[0], i, 0)),
                  pl.BlockSpec((3, t, cols), lambda i, s: (0, i, 0))],
        out_specs=(wspec,) * 4)
    o = jax.ShapeDtypeStruct(w.shape, F32)
    return _pcall(body, name=name, grid_spec=gs, out_shape=(o, o, o, o),
                  compiler_params=_params(("parallel",)))(sel, w, m, v, part, recv)


def _adamw_reduced(w, m, v, parts, recvs, sel):
    _, rows, cols = w.shape
    half = cols // 2
    t = _row_tile(rows, half)

    def body(sel_ref, w_ref, m_ref, v_ref, pa_ref, pb_ref, ra_ref, rb_ref, g_ref, d_ref, nm_ref, nv_ref):
        total = lambda p_ref, r_ref: ((p_ref[...].astype(F32) + r_ref[0].astype(F32)) + r_ref[1].astype(F32)) \
            + r_ref[2].astype(F32)
        g = jnp.where(pl.program_id(1) == 0, total(pa_ref, ra_ref), total(pb_ref, rb_ref))
        g_ref[...] = g
        d_ref[...], nm_ref[...], nv_ref[...] = _adamw_math(w_ref[...], g, m_ref[...], v_ref[...])

    wspec = pl.BlockSpec((None, t, half), lambda i, j, s: (0, i, j))
    pspec = pl.BlockSpec((None, t, half), lambda i, j, s: (s[0], i, 0))
    rspec = pl.BlockSpec((3, t, half), lambda i, j, s: (0, i, 0))
    gs = pltpu.PrefetchScalarGridSpec(num_scalar_prefetch=1, grid=(rows // t, 2),
                                      in_specs=[wspec, wspec, wspec, pspec, pspec, rspec, rspec],
                                      out_specs=(wspec,) * 4)
    o = jax.ShapeDtypeStruct(w.shape, F32)
    return _pcall(body, name="adamw_w_in", grid_spec=gs, out_shape=(o, o, o, o),
                  compiler_params=_params(("parallel", "arbitrary")))(sel, w, m, v, *parts, *recvs)


def _adamw_small(ws, gs, ms, vs):
    n = len(ws)

    def body(*refs):
        for i in range(n):
            w_ref, g_ref, m_ref, v_ref = (refs[k * n + i] for k in range(4))
            d, nm, nv = _adamw_math(w_ref[...], g_ref[...], m_ref[...], v_ref[...])
            refs[4 * n + i][...] = d
            refs[5 * n + i][...] = nm
            refs[6 * n + i][...] = nv

    shapes = tuple(jax.ShapeDtypeStruct(w.shape, F32) for w in ws)
    res = _pcall(body, name="adamw_small", out_shape=shapes * 3, compiler_params=_params())(*ws, *gs, *ms, *vs)
    return res[:n], res[n:2 * n], res[2 * n:]


def _mesh_pos():
    return lax.axis_index("x"), lax.axis_index("y"), lax.axis_index("c")


class _Gather:
    def __init__(self, arrs):
        self.ins = list(arrs)
        self.out_shape = tuple(jax.ShapeDtypeStruct((N_DEV,) + a.shape, a.dtype) for a in arrs)
        n = len(arrs)
        self.sems = [pltpu.SemaphoreType.DMA((7 * n,)), pltpu.SemaphoreType.DMA((7 * n,)),
                     pltpu.SemaphoreType.DMA((n,))]

    def _copies(self, ins, outs, sems):
        send_sems, recv_sems, local_sems = sems
        x, y, c = _mesh_pos()
        me, sibling = (x, y, c), (x, y, 1 - c)
        chips = [(1 - x, y), (x, 1 - y), (1 - x, 1 - y)]

        def copy(p, k, block, to, from_input=False):
            dst = outs[p].at[_slot(block)]
            return pltpu.make_async_remote_copy(
                src_ref=ins[p] if from_input else dst, dst_ref=dst, send_sem=send_sems.at[7 * p + k],
                recv_sem=recv_sems.at[7 * p + k], device_id=to, device_id_type=MESH)

        npc = len(self.ins)
        mine = [pltpu.make_async_copy(ins[p], outs[p].at[_slot(me)], local_sems.at[p]) for p in range(npc)]
        first = []
        for p in range(npc):
            first.append(copy(p, 0, me, sibling, from_input=True))
            first += [copy(p, 1 + j, me, (*chip, c), from_input=True) for j, chip in enumerate(chips)]
        return me, sibling, chips, c, copy, mine, first

    def start(self, ins, outs, sems):
        *_, mine, first = self._copies(ins, outs, sems)
        for cp in mine + first:
            cp.start()

    def finish(self, ins, outs, sems):
        me, sibling, chips, c, copy, mine, first = self._copies(ins, outs, sems)
        npc = len(self.ins)
        passed = []
        for p in range(npc):
            for j, chip in enumerate(chips):
                copy(p, 1 + j, (*chip, c), me).wait_recv()
                passed.append(copy(p, 4 + j, (*chip, c), sibling))
                passed[-1].start()
        for p in range(npc):
            copy(p, 0, sibling, me).wait_recv()
            for j, chip in enumerate(chips):
                copy(p, 4 + j, (*chip, 1 - c), me).wait_recv()
        for cp in first + passed:
            cp.wait_send()
        for cp in mine:
            cp.wait()


class _ExchangeCore:
    def __init__(self, fulls, cols=None):
        self.ins = list(fulls)
        self.cols = cols
        width = lambda f: f.shape[3] if cols is None else cols[1]
        self.out_shape = tuple(jax.ShapeDtypeStruct((4, f.shape[2], width(f)), f.dtype) for f in fulls)
        self.sems = [pltpu.SemaphoreType.DMA((4 * len(fulls),)), pltpu.SemaphoreType.DMA((4 * len(fulls),))]

    def _copies(self, ins, outs, sems):
        send_sems, recv_sems = sems
        x, y, c = _mesh_pos()

        def src(a, q):
            ref = ins[a].at[q, 1 - c]
            return ref if self.cols is None else ref.at[:, pl.ds(*self.cols)]

        return [pltpu.make_async_remote_copy(
            src_ref=src(a, q), dst_ref=outs[a].at[q], send_sem=send_sems.at[4 * a + q],
            recv_sem=recv_sems.at[4 * a + q], device_id=(x, y, 1 - c), device_id_type=MESH)
            for a in range(len(self.ins)) for q in range(4)]

    def start(self, ins, outs, sems):
        for cp in self._copies(ins, outs, sems):
            cp.start()

    def finish(self, ins, outs, sems):
        for cp in self._copies(ins, outs, sems):
            cp.wait()


class _ExchangeChip:
    def __init__(self, parts):
        self.ins = list(parts)
        self.out_shape = tuple(jax.ShapeDtypeStruct((3,) + p.shape[1:], p.dtype) for p in parts)
        self.sems = [pltpu.SemaphoreType.DMA((3 * len(parts),)), pltpu.SemaphoreType.DMA((3 * len(parts),))]

    def _copies(self, ins, outs, sems):
        send_sems, recv_sems = sems
        x, y, c = _mesh_pos()
        chips = [(1 - x, y), (x, 1 - y), (1 - x, 1 - y)]
        return [pltpu.make_async_remote_copy(
            src_ref=ins[a].at[2 * px + py], dst_ref=outs[a].at[j], send_sem=send_sems.at[3 * a + j],
            recv_sem=recv_sems.at[3 * a + j], device_id=(px, py, c), device_id_type=MESH)
            for a in range(len(self.ins)) for j, (px, py) in enumerate(chips)]

    def start(self, ins, outs, sems):
        for cp in self._copies(ins, outs, sems):
            cp.start()

    def finish(self, ins, outs, sems):
        for cp in self._copies(ins, outs, sems):
            cp.wait()


HBM_ONLY = pl.BlockSpec(memory_space=pltpu.HBM)
SEM_SPEC = pl.BlockSpec(memory_space=pltpu.SEMAPHORE)
SIDE_EFFECT = pltpu.SideEffectType.DATAFLOW_SIDE_EFFECTING


def _chip_copies(p_refs, land_refs, send_sems, recv_sems):
    x, y, c = _mesh_pos()
    return [pltpu.make_async_remote_copy(
        src_ref=p_refs[a].at[2 * px + py], dst_ref=land_refs[a].at[j], send_sem=send_sems.at[3 * a + j],
        recv_sem=recv_sems.at[3 * a + j], device_id=(px, py, c), device_id_type=MESH)
        for a in range(len(p_refs)) for j, (px, py) in enumerate([(1 - x, y), (x, 1 - y), (1 - x, 1 - y)])]


def _chip_exchange_start(parts, name):
    n = len(parts)
    lands = [lax.empty((3,) + p.shape[1:], p.dtype) for p in parts]

    def body(*refs):
        p_refs, land_refs, (send_sems, recv_sems) = refs[:n], refs[n:2 * n], refs[2 * n:2 * n + 2]
        for cp in _chip_copies(p_refs, land_refs, send_sems, recv_sems):
            cp.start()
        token = refs[-1]
        token[...] = jnp.zeros_like(token)

    hbm = lambda t: pltpu.HBM(t.shape, t.dtype)
    res = pl.pallas_call(
        body, name=name,
        out_shape=(pltpu.SemaphoreType.DMA((3 * n,)), pltpu.SemaphoreType.DMA((3 * n,)), *[hbm(t) for t in parts + lands],
                   jax.ShapeDtypeStruct((8, 128), F32)),
        in_specs=(HBM_ONLY,) * (2 * n),
        out_specs=(SEM_SPEC, SEM_SPEC, *[HBM_ONLY] * (2 * n), pl.BlockSpec(memory_space=pltpu.VMEM)),
        input_output_aliases={i: 2 + i for i in range(2 * n)},
        compiler_params=pltpu.CompilerParams(has_side_effects=SIDE_EFFECT))(
        *[pltpu.with_memory_space_constraint(t, pltpu.HBM) for t in parts + lands])
    return (res[0], res[1], list(res[2:2 + n]), list(res[2 + n:2 + 2 * n])), res[-1]


def _chip_exchange_wait(in_flight, after, name):
    send_sems, recv_sems, parts, lands = in_flight
    n = len(parts)

    def body(*refs):
        p_refs, land_refs, (send_sems, recv_sems) = refs[:n], refs[n:2 * n], refs[2 * n:2 * n + 2]
        for cp in _chip_copies(p_refs, land_refs, send_sems, recv_sems):
            cp.wait_send()
            cp.wait_recv()

    res = pl.pallas_call(
        body, name=name, out_shape=tuple(pltpu.HBM(t.shape, t.dtype) for t in parts + lands),
        in_specs=(*[HBM_ONLY] * (2 * n), SEM_SPEC, SEM_SPEC, pl.BlockSpec(memory_space=pl.ANY)),
        out_specs=(HBM_ONLY,) * (2 * n), input_output_aliases={i: i for i in range(2 * n)},
        compiler_params=pltpu.CompilerParams(has_side_effects=SIDE_EFFECT))(*parts, *lands, send_sems, recv_sems, after)
    return list(res[:n]), list(res[n:])


def _slot(p):
    return 4 * p[0] + 2 * p[1] + p[2]


def _gather_copies(src_refs, out_refs, send_sems, recv_sems):
    x, y, c = _mesh_pos()
    targets = [(x, y, 1 - c), (1 - x, y, c), (x, 1 - y, c), (1 - x, 1 - y, c)]
    return [pltpu.make_async_remote_copy(
        src_ref=src_refs[a], dst_ref=out_refs[a].at[_slot((x, y, c))], send_sem=send_sems.at[4 * a + k],
        recv_sem=recv_sems.at[4 * a + k], device_id=to, device_id_type=MESH)
        for a in range(len(src_refs)) for k, to in enumerate(targets)]


def _gather_start(shards, after, name):
    n = len(shards)
    outs = [lax.empty((N_DEV,) + s.shape, s.dtype) for s in shards]

    def body(*refs):
        for cp in _gather_copies(refs[:n], refs[n:2 * n], refs[2 * n + 1], refs[2 * n + 2]):
            cp.start()
        token = refs[-1]
        token[...] = jnp.zeros_like(token)

    res = pl.pallas_call(
        body, name=name,
        out_shape=(pltpu.SemaphoreType.DMA((4 * n,)), pltpu.SemaphoreType.DMA((4 * n,)),
                   *[pltpu.HBM(t.shape, t.dtype) for t in shards + outs], jax.ShapeDtypeStruct((8, 128), F32)),
        in_specs=(*[HBM_ONLY] * (2 * n), pl.BlockSpec(memory_space=pl.ANY)),
        out_specs=(SEM_SPEC, SEM_SPEC, *[HBM_ONLY] * (2 * n), pl.BlockSpec(memory_space=pltpu.VMEM)),
        input_output_aliases={i: 2 + i for i in range(2 * n)},
        compiler_params=pltpu.CompilerParams(has_side_effects=SIDE_EFFECT))(
        *[pltpu.with_memory_space_constraint(t, pltpu.HBM) for t in shards + outs], after)
    return (res[0], res[1], list(res[2:2 + n]), list(res[2 + n:2 + 2 * n])), res[-1]


def _gather_wait(in_flight, after, name):
    send_sems, recv_sems, shards, outs = in_flight
    n = len(shards)

    def body(*refs):
        for cp in _gather_copies(refs[:n], refs[n:2 * n], refs[2 * n], refs[2 * n + 1]):
            cp.wait_send()
            cp.wait_recv()

    res = pl.pallas_call(
        body, name=name, out_shape=tuple(pltpu.HBM(t.shape, t.dtype) for t in shards + outs),
        in_specs=(*[HBM_ONLY] * (2 * n), SEM_SPEC, SEM_SPEC, pl.BlockSpec(memory_space=pl.ANY)),
        out_specs=(HBM_ONLY,) * (2 * n), input_output_aliases={i: i for i in range(2 * n)},
        compiler_params=pltpu.CompilerParams(has_side_effects=SIDE_EFFECT))(*shards, *outs, send_sems, recv_sems, after)
    return list(res[:n]), list(res[n:])


class _PassToSibling:
    def __init__(self, shards, gathered):
        n = self.n = len(shards)
        self.ins = list(shards) + list(gathered)
        self.out_shape = tuple(jax.ShapeDtypeStruct(g.shape, g.dtype) for g in gathered)
        self.aliases = {n + a: a for a in range(n)}
        self.sems = [pltpu.SemaphoreType.DMA((3 * n,)), pltpu.SemaphoreType.DMA((3 * n,)),
                     pltpu.SemaphoreType.DMA((n,))]

    def _copies(self, ins, outs, sems):
        send_sems, recv_sems, local_sems = sems
        x, y, c = _mesh_pos()
        chips = [(1 - x, y), (x, 1 - y), (1 - x, 1 - y)]
        mine = [pltpu.make_async_copy(ins[a], outs[a].at[_slot((x, y, c))], local_sems.at[a]) for a in range(self.n)]
        passed, awaited = [], []
        for a in range(self.n):
            for j, chip in enumerate(chips):
                sems_j = dict(send_sem=send_sems.at[3 * a + j], recv_sem=recv_sems.at[3 * a + j],
                              device_id=(x, y, 1 - c), device_id_type=MESH)
                blk = outs[a].at[_slot((*chip, c))]
                passed.append(pltpu.make_async_remote_copy(src_ref=blk, dst_ref=blk, **sems_j))
                got = outs[a].at[_slot((*chip, 1 - c))]
                awaited.append(pltpu.make_async_remote_copy(src_ref=got, dst_ref=got, **sems_j))
        return mine, passed, awaited

    def start(self, ins, outs, sems):
        mine, passed, _ = self._copies(ins, outs, sems)
        for cp in mine + passed:
            cp.start()

    def finish(self, ins, outs, sems):
        mine, passed, awaited = self._copies(ins, outs, sems)
        for cp in passed:
            cp.wait_send()
        for cp in awaited:
            cp.wait_recv()
        for cp in mine:
            cp.wait()


def _reduce_sums(fulls, recv_core, core, tag):
    return [_pair_sum(f, r, core, f"rs_pair_{tag}{i}") for i, (f, r) in enumerate(zip(fulls, recv_core))]


def _local_step(x, tgt, mods, w_in_shard, order, shards, small, chip, core):
    sh1, sc1, g1, sh2, sc2, g2 = mods
    norm1_g, rel_bias, gn_g, gn_b, norm2_g, norm_f_g = small
    tables = _ret_tables()
    buckets = jnp.asarray(_bucket_tables())

    h1 = _norm_mod_fwd(x, norm1_g, sh1, sc1, "norm1_fwd")
    proj, slabs, w_in_t = _gather_proj(h1, w_in_shard, order)
    flight_w1, token_w = _gather_start(list(shards[:3]), proj, "gather_w1_start")
    flight_w2, token_w = _gather_start(list(shards[3:]), token_w, "gather_w2_start")
    bias = _bias_build(rel_bias, buckets, token_w)
    outs, lses = [], []
    for gi in range(len(ATT_GROUPS)):
        o, l = _att_fwd(slabs, bias, gi)
        outs.append(o)
        lses.append(l)
    att, gathered = _mix_fwd(outs, lses, comm=_PassToSibling(*_gather_wait(flight_w1, lses[2], "gather_w1_wait")))
    w_ret_out, w_att_out, w_o = (_from_slots(g, ax) for g, ax in zip(gathered, BIG_AXES[1:4]))
    gated, ro, states = _ret_fwd(proj, tables, gn_g, gn_b, att)
    ret_out, gathered = _mm(gated, w_ret_out, 'nn', tm=S, tn=256, tk=2048, name="ret_out",
                            comm=_PassToSibling(*_gather_wait(flight_w2, gated, "gather_w2_wait")))
    w_ff1, w_ff2 = (_from_slots(g, ax) for g, ax in zip(gathered, BIG_AXES[4:]))
    att_out, merged = _att_out_merge(att, w_att_out, proj, ret_out)
    mixo, x1, h2 = _w_o_norm2(merged, w_o, x, g1, norm2_g, sh2, sc2)
    u, act = _mm(h2, w_ff1, 'nn', tm=S, tn=512, tk=D, name="ff1", relu2=True)
    loss, dx2, g_normf, df, dg2 = _ff2_final(act, w_ff2, x1, g2, tgt, norm_f_g)

    gw_ff2 = _mm(act, df, 'tn', tm=512, tn=D, tk=S, name="gw_ff2", out_dtype=BF16)
    du = _mm(df, w_ff2, 'nt', tm=S, tn=512, tk=D, name="d_act", out_dtype=BF16, relu2_of=u)
    gw_ff1 = _mm(h2, du, 'tn', tm=D, tn=512, tk=S, name="gw_ff1", out_dtype=BF16)
    fulls_a = [_to_slots(g, ax) for g, ax in zip((gw_ff1, gw_ff2), BIG_AXES[4:])]
    dh2, recv_core_a = _mm(du, w_ff1, 'nt', tm=1024, tn=1024, tk=2048, name="dh2", comm=_ExchangeCore(fulls_a))
    parts_a = _reduce_sums(fulls_a, recv_core_a, core, "a")
    flight_a, token_a = _chip_exchange_start(parts_a, "rs_a_start")
    dx1, dsc2, dsh2, g_norm2, dmixo, dg1 = _norm_mod_bwd(x1, norm2_g, sc2, dh2, dx2, "norm2_bwd", gate=(mixo, g1))

    gw_o = _mm(merged, dmixo, 'tn', tm=D, tn=512, tk=S, name="gw_o", out_dtype=BF16, after=token_a)
    d_ret_out, d_att_out, dga, dgb = _dmerged_split(dmixo, w_o, proj, ret_out, att_out)
    gw_ret_out = _mm(gated, d_ret_out, 'tn', tm=512, tn=D, tk=S, name="gw_ret_out", out_dtype=BF16)
    gw_att_out = _mm(att, d_att_out, 'tn', tm=AW, tn=D, tk=S, name="gw_att_out", out_dtype=BF16)
    fulls_b = [_to_slots(g, ax) for g, ax in zip((gw_ret_out, gw_att_out, gw_o), BIG_AXES[1:4])]
    dgated, recv_core_b = _mm(d_ret_out, w_ret_out, 'nt', tm=S, tn=512, tk=D, name="dgated",
                              comm=_ExchangeCore(fulls_b))
    parts_b = _reduce_sums(fulls_b, recv_core_b, core, "b")
    flight_b, token_b = _chip_exchange_start(parts_b, "rs_b_start")
    datt = _mm(d_att_out, w_att_out, 'nt', tm=S, tn=AW, tk=D, name="datt", after=token_b)
    mix_grads = _mix_bwd(outs, lses, datt)
    datt_parts, ds_sums = [], []
    for gi in range(len(ATT_GROUPS)):
        dq, dk, dv, ds_sum = _att_bwd(slabs, bias, outs[gi], lses[gi], mix_grads[gi], mix_grads[3 + gi], gi)
        datt_parts += [dq.reshape(S, AW), dk.reshape(S, AW), dv.reshape(S, AW)]
        ds_sums.append(ds_sum)
    g_bias = _bias_grad(jnp.concatenate(ds_sums, axis=0), buckets)[:, :, 0].T.reshape(1, -1)
    dproj, g_gn_g, g_gn_b = _ret_bwd(proj, tables, gn_g, gn_b, ro, states, dgated, datt_parts + [dga, dgb])
    parts_a, recv_chip_a = _chip_exchange_wait(flight_a, dproj, "rs_a_wait")
    parts_b, recv_chip_b = _chip_exchange_wait(flight_b, dproj, "rs_b_wait")
    reduced = list(zip(parts_b + parts_a, recv_chip_b + recv_chip_a))
    full_in = _to_slots(_mm(dproj, h1, 'tn', tm=512, tn=D, tk=S, name="gw_in", out_dtype=BF16), 0)
    in_flight, token = [], None
    for half in range(2):
        (recv_core_in,) = _run_comm(_ExchangeCore([full_in], cols=(half * (D // 2), D // 2)), f"rs_core_in{half}",
                                    after=token)
        part_in = [_pair_sum(full_in, recv_core_in, core, f"rs_pair_c{half}", col_block=half)]
        flight, token = _chip_exchange_start(part_in, f"rs_in{half}_start")
        in_flight.append(flight)
    dh1 = _mm(dproj, w_in_t, 'nn', tm=1024, tn=1024, tk=2560, name="dh1", after=token)
    gx, dsc1, dsh1, g_norm1 = _norm_mod_bwd(x, norm1_g, sc1, dh1, dx1, "norm1_bwd")

    dmod = [dsh1, dsc1, dg1, dsh2, dsc2, dg2]
    small_g = [g_norm1, g_bias, g_gn_g, g_gn_b, g_norm2, g_normf]
    return loss, gx, in_flight, reduced, small_g, dmod


def _to_slots(g, axis):
    if axis == 0:
        return g.reshape(4, 2, g.shape[0] // N_DEV, g.shape[1])
    return g.reshape(g.shape[0], N_DEV, g.shape[1] // N_DEV).transpose(1, 0, 2).reshape(4, 2, g.shape[0], -1)


def _from_slots(w8, axis):
    if axis == 0:
        return w8.reshape(-1, w8.shape[2])
    return w8.transpose(1, 0, 2).reshape(w8.shape[1], -1)


BIG_AXES = (1, 0, 1, 0, 1, 0)


def kernel(x, c, w_ada, b_ada, norm1_g, w_in, rel_bias, ret_gn_g, ret_gn_b, w_ret_out, w_att_out, w_o, norm2_g, w_ff1, w_ff2, norm_f_g, loss_target, m_w_ada, m_b_ada, m_norm1_g, m_w_in, m_rel_bias, m_ret_gn_g, m_ret_gn_b, m_w_ret_out, m_w_att_out, m_w_o, m_norm2_g, m_w_ff1, m_w_ff2, m_norm_f_g, v_w_ada, v_b_ada, v_norm1_g, v_w_in, v_rel_bias, v_ret_gn_g, v_ret_gn_b, v_w_ret_out, v_w_att_out, v_w_o, v_norm2_g, v_w_ff1, v_w_ff2, v_norm_f_g):
    mx, my, mc = _mesh_pos()
    dev = 4 * mx + 2 * my + mc
    chip = jnp.reshape(2 * mx + my, (1,)).astype(jnp.int32)
    core = jnp.reshape(mc, (1,)).astype(jnp.int32)
    ada_w = D * 6 // N_DEV

    w_in, m_w_in, v_w_in = (jnp.transpose(t, (0, 2, 1)) for t in (w_in, m_w_in, v_w_in))

    shards = [w[0].astype(BF16) for w in (w_in, w_ret_out, w_att_out, w_o, w_ff1, w_ff2)]
    (c_all,) = _run_comm(_Gather([c]), "gather_c")
    c_all = c_all.reshape(N_DEV, D)
    b_sl = lax.dynamic_slice(b_ada, (0, dev * ada_w), (1, ada_w))
    (mod_all,) = _run_comm(_Gather([_ada_fwd(c_all, w_ada[0], b_sl)]), "gather_mod")
    mod = lax.dynamic_index_in_dim(mod_all, dev, axis=1, keepdims=False).reshape(6, D)
    mods = tuple(mod[i:i + 1] for i in range(6))

    small = (norm1_g, rel_bias, ret_gn_g, ret_gn_b, norm2_g, norm_f_g.reshape(1, D))
    order = lax.dynamic_index_in_dim(jnp.asarray(_proj_order()), 2 * mx + my, axis=0, keepdims=False)
    loss, gx, in_flight, big_red, small_g, dmod = _local_step(x[0], loss_target[0], mods, shards[0], order,
                                                              shards[1:], small, chip, core)

    names = ['w_ada', 'b_ada', 'norm1_g', 'w_in', 'rel_bias', 'ret_gn_g', 'ret_gn_b', 'w_ret_out', 'w_att_out',
             'w_o', 'norm2_g', 'w_ff1', 'w_ff2', 'norm_f_g']
    ws = dict(zip(names, (w_ada, b_ada, norm1_g, w_in, rel_bias, ret_gn_g, ret_gn_b, w_ret_out, w_att_out, w_o,
                          norm2_g, w_ff1, w_ff2, norm_f_g)))
    ms = dict(zip(names, (m_w_ada, m_b_ada, m_norm1_g, m_w_in, m_rel_bias, m_ret_gn_g, m_ret_gn_b, m_w_ret_out,
                          m_w_att_out, m_w_o, m_norm2_g, m_w_ff1, m_w_ff2, m_norm_f_g)))
    vs = dict(zip(names, (v_w_ada, v_b_ada, v_norm1_g, v_w_in, v_rel_bias, v_ret_gn_g, v_ret_gn_b, v_w_ret_out,
                          v_w_att_out, v_w_o, v_norm2_g, v_w_ff1, v_w_ff2, v_norm_f_g)))
    grads, delta, new_m, new_v = {}, {}, {}, {}
    big_names = ('w_ret_out', 'w_att_out', 'w_o', 'w_ff1', 'w_ff2')
    for n, (part, recv) in zip(big_names, big_red):
        grads[n], delta[n], new_m[n], new_v[n] = _adamw_reduced1(ws[n], ms[n], vs[n], part, recv, chip, "adamw_" + n)
    updated = lax.optimization_barrier((gx, tuple(delta[n] for n in big_names)))
    gathered = _run_comm(_Gather(dmod + small_g + [loss]), "gather_small", after=updated[0])
    g_b_ada, dmod_all, (g_norm1, g_bias, g_gn_g, g_gn_b, g_norm2, g_normf, loss_sum) = _sum_small(gathered)
    loss_out = loss_sum[0, 0]
    g_w_ada = _ada_bwd(c_all, lax.dynamic_slice(dmod_all, (0, dev * ada_w), (N_DEV, ada_w)))

    grads.update(w_ada=g_w_ada.reshape(w_ada.shape), b_ada=g_b_ada, norm1_g=g_norm1, rel_bias=g_bias,
                 ret_gn_g=g_gn_g, ret_gn_b=g_gn_b, norm2_g=g_norm2, norm_f_g=g_normf)
    delta['w_ada'], new_m['w_ada'], new_v['w_ada'] = _adamw(w_ada, g_w_ada, m_w_ada, v_w_ada, "adamw_w_ada")
    small_names = ('b_ada', 'norm1_g', 'rel_bias', 'ret_gn_g', 'ret_gn_b', 'norm2_g', 'norm_f_g')
    two_d = {n: (1, ws[n].size) if ws[n].ndim == 1 else ws[n].shape for n in small_names}
    d_, m_, v_ = _adamw_small(*[[src[n].reshape(two_d[n]) for n in small_names] for src in (ws, grads, ms, vs)])
    for i, n in enumerate(small_names):
        shp = ws[n].shape
        delta[n], new_m[n], new_v[n] = d_[i].reshape(shp), m_[i].reshape(shp), v_[i].reshape(shp)
        grads[n] = grads[n].reshape(shp)

    done = lax.optimization_barrier((gx, tuple(d_), tuple(delta[n] for n in ('w_ada', 'w_ret_out', 'w_att_out', 'w_o',
                                                                               'w_ff1', 'w_ff2'))))
    parts_in, recvs_in = [], []
    for half, flight in enumerate(in_flight):
        (part_in,), (recv_chip_in,) = _chip_exchange_wait(flight, done[0], f"rs_in{half}_wait")
        parts_in.append(part_in)
        recvs_in.append(recv_chip_in)
    grads['w_in'], delta['w_in'], new_m['w_in'], new_v['w_in'] = _adamw_reduced(w_in, m_w_in, v_w_in, parts_in,
                                                                               recvs_in, chip)
    for d in (grads, delta, new_m, new_v):
        d['w_in'] = jnp.transpose(d['w_in'], (0, 2, 1))
    return (loss_out, gx[None], *[grads[n] for n in names], *[delta[n] for n in names],
            *[new_m[n] for n in names], *[new_v[n] for n in names])
```

```python
import functools
import math

import numpy as np
import jax
import jax.numpy as jnp
from jax import lax
from jax.experimental import pallas as pl
from jax.experimental.pallas import tpu as pltpu

F32 = jnp.float32
BF16 = jnp.bfloat16
MESH = pl.DeviceIdType.MESH

N_DEV = 8
S = 2048
D = 1024
RET_HEADS = 4
RET_DK = 256
RET_DV = 512
CHUNK = 128
N_CHUNK = S // CHUNK
ATT_GROUPS = ((128, 1), (512, 4), (2048, 16))
ATT_HG = 4
ATT_DH = 128
ATT_BLK = 128
N_BUCKETS = 32
MAX_DIST = 2048
D_FF = 4096
IN_COLS = 12800
OFF_RQ, OFF_RK, OFF_RV, OFF_RG, OFF_ATT = 0, 1024, 2048, 4096, 6144
OFF_GA, OFF_GB = 6144, 7168
RMS_EPS = 1e-6
GN_EPS = 1e-5
ADAM_LR, ADAM_B1, ADAM_B2, ADAM_EPS, ADAM_WD, ADAM_STEP = 0.001, 0.9, 0.999, 1e-08, 0.01, 10
VMEM_LIMIT = 48 * 1024 * 1024


def _is_float_array(a):
    return hasattr(a, "dtype") and hasattr(a, "shape") and jnp.issubdtype(a.dtype, jnp.floating)


def _pcall(body, pin=True, **kw):
    if not pin:
        return pl.pallas_call(body, **kw)
    out_shape = kw.pop("out_shape")
    single = not isinstance(out_shape, (tuple, list))
    pinned = [pltpu.HBM(o.shape, o.dtype) if type(o) is jax.ShapeDtypeStruct and _is_float_array(o) else o
              for o in ((out_shape,) if single else out_shape)]
    call = pl.pallas_call(body, out_shape=pinned[0] if single else tuple(pinned), **kw)
    grid_spec = kw.get("grid_spec")
    specs = [None] * grid_spec.num_scalar_prefetch + list(grid_spec.in_specs) if grid_spec else kw.get("in_specs")

    def pin(a, spec):
        in_smem = getattr(spec, "memory_space", None) is pltpu.SMEM
        return pltpu.with_memory_space_constraint(a, pltpu.HBM) if _is_float_array(a) and not in_smem else a

    return lambda *args: call(*[pin(a, spec) for a, spec in zip(args, specs or [None] * len(args), strict=True)])


def _params(sem=None):
    return pltpu.CompilerParams(dimension_semantics=sem, vmem_limit_bytes=VMEM_LIMIT)


HBM_SPEC = pl.BlockSpec(memory_space=pl.ANY)


def _carry(body, comm, *, name, grid, in_specs, out_specs, out_shape, scratch_shapes=()):
    single = not isinstance(out_specs, (tuple, list))
    o_specs = (out_specs,) if single else tuple(out_specs)
    o_shape = (out_shape,) if single else tuple(out_shape)
    n_in, n_out, n_scr = len(in_specs), len(o_specs), len(scratch_shapes)
    nci, nco = len(comm.ins), len(comm.out_shape)
    total = int(np.prod(grid))

    def wrapped(*refs):
        bounds = np.cumsum([0, n_in, nci, n_out, nco, n_scr])
        a, ci, o, co, scr = (refs[bounds[i]:bounds[i + 1]] for i in range(5))
        sems = refs[bounds[5]:]
        flat = 0
        for d, g in enumerate(grid):
            flat = flat * g + pl.program_id(d)

        @pl.when(flat == 0)
        def _():
            comm.start(ci, co, sems)

        body(*a, *o, *scr)

        @pl.when(flat == total - 1)
        def _():
            comm.finish(ci, co, sems)

    aliases = {n_in + i: n_out + o for i, o in getattr(comm, "aliases", {}).items()}
    call = _pcall(wrapped, pin=False, name=name, grid=grid, in_specs=list(in_specs) + [HBM_SPEC] * nci,
                  out_specs=o_specs + (HBM_SPEC,) * nco, out_shape=o_shape + tuple(comm.out_shape),
                  scratch_shapes=list(scratch_shapes) + list(comm.sems), input_output_aliases=aliases,
                  compiler_params=_params(("arbitrary",) * len(grid)))

    def run(*args):
        res = call(*args, *comm.ins)
        own = res[0] if single else tuple(res[:n_out])
        return own, tuple(res[n_out:])

    return run


def _run_comm(comm, name, after=None):
    nci, nco = len(comm.ins), len(comm.out_shape)
    extra = [] if after is None else [after]

    def body(*refs):
        ci, co, sems = refs[:nci], refs[nci + len(extra):nci + len(extra) + nco], refs[nci + len(extra) + nco:]
        comm.start(ci, co, sems)
        comm.finish(ci, co, sems)

    return _pcall(body, pin=False, name=name, in_specs=[HBM_SPEC] * (nci + len(extra)), out_specs=(HBM_SPEC,) * nco,
                  out_shape=tuple(comm.out_shape), scratch_shapes=list(comm.sems))(*comm.ins, *extra)


def _dot(a, b, dn):
    return lax.dot_general(a.astype(BF16), b.astype(BF16), (dn, ((), ())), preferred_element_type=F32)


NN = ((1,), (0,))
NT = ((1,), (1,))
TN = ((0,), (0,))


def _mm(a, b, mode, *, tm, tn, tk, name, out_dtype=F32, res=None, gvec=None, relu2=False, relu2_of=None, comm=None,
        after=None):
    if mode == 'nn':
        (M, K), (_, N) = a.shape, b.shape
        a_spec = pl.BlockSpec((tm, tk), lambda i, j, k: (i, k))
        b_spec = pl.BlockSpec((tk, tn), lambda i, j, k: (k, j))
        dn = NN
    elif mode == 'nt':
        (M, K), (N, _) = a.shape, b.shape
        a_spec = pl.BlockSpec((tm, tk), lambda i, j, k: (i, k))
        b_spec = pl.BlockSpec((tn, tk), lambda i, j, k: (j, k))
        dn = NT
    else:
        (K, M), (_, N) = a.shape, b.shape
        a_spec = pl.BlockSpec((tk, tm), lambda i, j, k: (k, i))
        b_spec = pl.BlockSpec((tk, tn), lambda i, j, k: (k, j))
        dn = TN
    assert M % tm == 0 and N % tn == 0 and K % tk == 0, (name, M, N, K)
    nk = K // tk
    fused = res is not None
    o_spec = pl.BlockSpec((tm, tn), lambda i, j, k: (i, j))

    def body(a_ref, b_ref, *rest):
        acc_ref = rest[-1] if nk > 1 else None
        if after is not None:
            rest = rest[1:]
        if fused:
            res_ref, g_ref, o_ref, x_ref = rest[:4]
        elif relu2_of is not None:
            u_ref, o_ref = rest[:2]
        elif relu2:
            o_ref, act_ref = rest[:2]
        else:
            o_ref = rest[0]

        def finish(acc):
            if relu2_of is not None:
                acc = acc * (2.0 * jnp.maximum(u_ref[...], 0.0))
            o_ref[...] = acc.astype(o_ref.dtype)
            if fused:
                x_ref[...] = res_ref[...] + g_ref[...] * acc
            if relu2:
                r = jnp.maximum(acc, 0.0)
                act_ref[...] = (r * r).astype(BF16)

        p = _dot(a_ref[...], b_ref[...], dn)
        if nk == 1:
            finish(p)
        else:
            k = pl.program_id(2)

            @pl.when(k == 0)
            def _():
                acc_ref[...] = p

            @pl.when(k > 0)
            def _():
                acc_ref[...] += p

            @pl.when(k == nk - 1)
            def _():
                finish(acc_ref[...])

    in_specs = [a_spec, b_spec]
    args = [a, b]
    if after is not None:
        in_specs.append(pl.BlockSpec(memory_space=pl.ANY))
        args.append(after)
    out_shape = jax.ShapeDtypeStruct((M, N), out_dtype)
    out_specs = o_spec
    if fused:
        in_specs += [pl.BlockSpec((tm, tn), lambda i, j, k: (i, j)), pl.BlockSpec((1, tn), lambda i, j, k: (0, j))]
        args += [res, gvec]
        out_shape = (out_shape, jax.ShapeDtypeStruct((M, N), F32))
        out_specs = (o_spec, pl.BlockSpec((tm, tn), lambda i, j, k: (i, j)))
    elif relu2_of is not None:
        in_specs.append(pl.BlockSpec((tm, tn), lambda i, j, k: (i, j)))
        args.append(relu2_of)
    elif relu2:
        out_shape = (out_shape, jax.ShapeDtypeStruct((M, N), BF16))
        out_specs = (o_spec, pl.BlockSpec((tm, tn), lambda i, j, k: (i, j)))
    kw = dict(name=name, grid=(M // tm, N // tn, nk), in_specs=in_specs, out_specs=out_specs,
              out_shape=out_shape, scratch_shapes=[pltpu.VMEM((tm, tn), F32)] if nk > 1 else [])
    if comm is not None:
        return _carry(body, comm, **kw)(*args)
    return _pcall(body, compiler_params=_params(("parallel", "parallel", "arbitrary")), **kw)(*args)


PROJ_TN = 512
ATT_T0, ATT_T1 = 6144 // PROJ_TN, 10752 // PROJ_TN
N_SLABS = (ATT_T1 - ATT_T0) * 4
MAIN_COLS = IN_COLS - (ATT_T1 - ATT_T0) * PROJ_TN


PROJ_TILES = IN_COLS // PROJ_TN
SHARD_ROWS = IN_COLS // N_DEV
W_CHUNKS = 4
N_OWN, N_NEAR = 5, 18


def _proj_order():
    out = np.zeros((4, 3, PROJ_TILES), np.int32)
    for q in range(4):
        def hops(t):
            owners = {col // (2 * SHARD_ROWS) for col in (t * PROJ_TN, (t + 1) * PROJ_TN - 1)}
            return max(bin(q ^ p).count("1") for p in owners)
        order = sorted(range(PROJ_TILES), key=lambda t: (hops(t), t))
        assert all(hops(t) == 0 for t in order[:N_OWN]) and all(hops(t) < 2 for t in order[:N_NEAR])
        is_att = [ATT_T0 <= t < ATT_T1 for t in order]
        for row, kind, index in ((1, False, lambda t: t if t < ATT_T0 else t - (ATT_T1 - ATT_T0)),
                                 (2, True, lambda t: t - ATT_T0)):
            own = [index(t) if a == kind else None for t, a in zip(order, is_att)]
            first = next(v for v in own if v is not None)
            last = first
            for j, v in enumerate(own):
                last = last if v is None else v
                out[q, row, j] = last
        out[q, 0] = order
    return out


def _gather_proj(h1, shard, order):
    rows = SHARD_ROWS // W_CHUNKS

    def body(ord_ref, a_ref, sh_ref, main_ref, slab_ref, full_ref, wbuf, fetch_sems, send_sems, recv_sems,
             local_sems):
        j = pl.program_id(0)
        x, y, c = _mesh_pos()
        me, sibling = (x, y, c), (x, y, 1 - c)
        chips = [(1 - x, y), (x, 1 - y), (1 - x, 1 - y)]

        def block(p, owner):
            return full_ref.at[pl.ds(pl.multiple_of(_slot(owner) * SHARD_ROWS + p * rows, 16), rows)]

        def copy(p, k, owner, to, from_input=False):
            dst = block(p, owner)
            return pltpu.make_async_remote_copy(
                src_ref=sh_ref.at[pl.ds(p * rows, rows)] if from_input else dst, dst_ref=dst,
                send_sem=send_sems.at[7 * p + k], recv_sem=recv_sems.at[7 * p + k], device_id=to, device_id_type=MESH)

        pieces = range(W_CHUNKS)
        mine = [pltpu.make_async_copy(sh_ref.at[pl.ds(p * rows, rows)], block(p, me), local_sems.at[p]) for p in pieces]
        first = [copy(p, 0, me, sibling, from_input=True) for p in pieces]
        first += [copy(p, 1 + n, me, (*chips[n], c), from_input=True) for p in pieces for n in range(2)]
        near_pass = [copy(p, 4 + n, (*chips[n], c), sibling) for p in pieces for n in range(2)]
        relay = [copy(p, 3, ((x + 1 - c) % 2, (y + c) % 2, c), ((x + c) % 2, (y + 1 - c) % 2, c)) for p in pieces]
        far_pass = [copy(p, 6, (*chips[2], c), sibling) for p in pieces]

        def fetch(pos):
            slot = lax.rem(pos, 2)
            start = pl.multiple_of(ord_ref[0, pos] * PROJ_TN, PROJ_TN)
            return pltpu.make_async_copy(full_ref.at[pl.ds(start, PROJ_TN)], wbuf.at[slot], fetch_sems.at[slot])

        @pl.when(j == 0)
        def _():
            for cp in mine + first:
                cp.start()
            for cp in mine:
                cp.wait()
            for p in pieces:
                copy(p, 0, sibling, me).wait_recv()
            fetch(j).start()

        @pl.when(j == N_OWN - 1)
        def _():
            for p in pieces:
                for n in range(2):
                    copy(p, 1 + n, (*chips[n], c), me).wait_recv()
                    near_pass[2 * p + n].start()
                relay[p].start()
            for p in pieces:
                for n in range(2):
                    copy(p, 4 + n, (*chips[n], 1 - c), me).wait_recv()

        @pl.when(j == N_NEAR - 1)
        def _():
            for p in pieces:
                copy(p, 3, (*chips[2], c), me).wait_recv()
                far_pass[p].start()
            for p in pieces:
                copy(p, 6, (*chips[2], 1 - c), me).wait_recv()

        @pl.when(j + 1 < PROJ_TILES)
        def _():
            fetch(j + 1).start()

        fetch(j).wait()
        w_ref = wbuf.at[lax.rem(j, 2)]
        tile = ord_ref[0, j]
        is_att = (tile >= ATT_T0) & (tile < ATT_T1)
        chunks = [pl.ds(r * 512, 512) for r in range(S // 512)]

        @pl.when(jnp.logical_not(is_att))
        def _():
            for rws in chunks:
                main_ref[rws, :] = _dot(a_ref[rws, :], w_ref[...], NT)

        @pl.when(is_att)
        def _():
            for rws in chunks:
                p = _dot(a_ref[rws, :], w_ref[...], NT)
                for h in range(4):
                    slab_ref[h, rws, :] = p[:, h * 128:(h + 1) * 128]

        @pl.when(j == PROJ_TILES - 1)
        def _():
            for cp in first + near_pass + relay + far_pass:
                cp.wait_send()

    gs = pltpu.PrefetchScalarGridSpec(
        num_scalar_prefetch=1, grid=(PROJ_TILES,),
        in_specs=[pl.BlockSpec((S, D), lambda j, o: (0, 0)), HBM_SPEC],
        out_specs=(pl.BlockSpec((S, PROJ_TN), lambda j, o: (0, o[1, j])),
                   pl.BlockSpec((4, S, 128), lambda j, o: (o[2, j], 0, 0)), HBM_SPEC),
        scratch_shapes=[pltpu.VMEM((2, PROJ_TN, D), BF16), pltpu.SemaphoreType.DMA((2,)),
                        pltpu.SemaphoreType.DMA((7 * W_CHUNKS,)), pltpu.SemaphoreType.DMA((7 * W_CHUNKS,)),
                        pltpu.SemaphoreType.DMA((W_CHUNKS,))])
    return _pcall(body, pin=False, name="gather_proj", grid_spec=gs,
                  out_shape=(jax.ShapeDtypeStruct((S, MAIN_COLS), F32), jax.ShapeDtypeStruct((N_SLABS, S, 128), F32),
                             jax.ShapeDtypeStruct((IN_COLS, D), BF16)),
                  compiler_params=_params(("arbitrary",)))(order, h1, shard)


TR = 256


def _row_spec(w=D):
    return pl.BlockSpec((TR, w), lambda i: (i, 0))


def _vec_spec(w=D):
    return pl.BlockSpec((1, w), lambda i: (0, 0))


def _norm_mod_fwd(x, g, sh, sc, name):
    def body(x_ref, g_ref, sh_ref, sc_ref, o_ref):
        xv = x_ref[...]
        rstd = lax.rsqrt(jnp.mean(xv * xv, axis=-1, keepdims=True) + RMS_EPS)
        n = xv * rstd * g_ref[...]
        o_ref[...] = (n * (1.0 + sc_ref[...]) + sh_ref[...]).astype(BF16)

    return _pcall(body, name=name, grid=(S // TR,), in_specs=[_row_spec(), _vec_spec(), _vec_spec(), _vec_spec()],
                  out_specs=_row_spec(), out_shape=jax.ShapeDtypeStruct((S, D), BF16),
                  compiler_params=_params(("parallel",)))(x, g, sh, sc)


def _norm_mod_bwd(x, g, sc, dh, dres, name, gate=None):
    gated = gate is not None

    def body(x_ref, g_ref, sc_ref, dh_ref, dres_ref, *rest):
        if gated:
            f_ref, gv_ref, dx_ref, dsc_ref, dsh_ref, dg_ref, dz_ref, dgv_ref = rest
        else:
            dx_ref, dsc_ref, dsh_ref, dg_ref = rest
        i = pl.program_id(0)
        xv = x_ref[...]
        dh = dh_ref[...]
        rstd = lax.rsqrt(jnp.mean(xv * xv, axis=-1, keepdims=True) + RMS_EPS)
        xhat = xv * rstd
        gv = g_ref[...]
        dn = dh * (1.0 + sc_ref[...])
        dxhat = dn * gv
        dx = dres_ref[...] + rstd * (dxhat - xhat * jnp.mean(dxhat * xhat, axis=-1, keepdims=True))
        dx_ref[...] = dx
        sums = [(dsc_ref, jnp.sum(dh * (xhat * gv), axis=0, keepdims=True)),
                (dsh_ref, jnp.sum(dh, axis=0, keepdims=True)),
                (dg_ref, jnp.sum(dn * xhat, axis=0, keepdims=True))]
        if gated:
            dz_ref[...] = (dx * gv_ref[...]).astype(BF16)
            sums.append((dgv_ref, jnp.sum(dx * f_ref[...], axis=0, keepdims=True)))

        @pl.when(i == 0)
        def _():
            for ref, p in sums:
                ref[...] = p

        @pl.when(i > 0)
        def _():
            for ref, p in sums:
                ref[...] += p

    vec = jax.ShapeDtypeStruct((1, D), F32)
    in_specs = [_row_spec(), _vec_spec(), _vec_spec(), _row_spec(), _row_spec()]
    out_specs = [_row_spec(), _vec_spec(), _vec_spec(), _vec_spec()]
    out_shape = [jax.ShapeDtypeStruct((S, D), F32), vec, vec, vec]
    args = [x, g, sc, dh, dres]
    if gated:
        in_specs += [_row_spec(), _vec_spec()]
        out_specs += [_row_spec(), _vec_spec()]
        out_shape += [jax.ShapeDtypeStruct((S, D), BF16), vec]
        args += list(gate)
    return _pcall(body, name=name, grid=(S // TR,), in_specs=in_specs, out_specs=tuple(out_specs),
                  out_shape=tuple(out_shape), compiler_params=_params(("arbitrary",)))(*args)


def _w_o_norm2(merged, w_o, x, g1, g, sh, sc):
    def body(a_ref, b_ref, x_ref, g1_ref, g_ref, sh_ref, sc_ref, o_ref, x1_ref, h_ref):
        acc = _dot(a_ref[...], b_ref[...], NN)
        o_ref[...] = acc
        xv = x_ref[...] + g1_ref[...] * acc
        x1_ref[...] = xv
        rstd = lax.rsqrt(jnp.mean(xv * xv, axis=-1, keepdims=True) + RMS_EPS)
        h_ref[...] = (xv * rstd * g_ref[...] * (1.0 + sc_ref[...]) + sh_ref[...]).astype(BF16)

    rows = pl.BlockSpec((FF2_TM, D), lambda i: (i, 0))
    f32 = jax.ShapeDtypeStruct((S, D), F32)
    return _pcall(body, name="w_o_norm2", grid=(S // FF2_TM,),
                  in_specs=[rows, pl.BlockSpec((D, D), lambda i: (0, 0)), rows] + [_vec_spec()] * 4,
                  out_specs=(rows, rows, rows), out_shape=(f32, f32, jax.ShapeDtypeStruct((S, D), BF16)),
                  compiler_params=_params(("parallel",)))(merged, w_o, x, g1, g, sh, sc)


FF2_TM = 512


def _ff2_final(act, w_ff2, x1, g2, tgt, g):
    def body(a_ref, b_ref, x1_ref, g2_ref, t_ref, g_ref, loss_ref, dx_ref, dg_ref, df_ref, dg2_ref):
        i = pl.program_id(0)
        f = _dot(a_ref[...], b_ref[...], NN)
        g2v = g2_ref[...]
        xv = x1_ref[...] + g2v * f
        gv = g_ref[...]
        rstd = lax.rsqrt(jnp.mean(xv * xv, axis=-1, keepdims=True) + RMS_EPS)
        xhat = xv * rstd
        err = xhat * gv - t_ref[...]
        dy = err * (1.0 / D)
        dxhat = dy * gv
        dx = rstd * (dxhat - xhat * jnp.mean(dxhat * xhat, axis=-1, keepdims=True))
        dx_ref[...] = dx
        df_ref[...] = (dx * g2v).astype(BF16)
        p_g = jnp.sum(dy * xhat, axis=0, keepdims=True)
        p_g2 = jnp.sum(dx * f, axis=0, keepdims=True)
        p_l = jnp.zeros((1, 128), F32) + 0.5 * jnp.sum(jnp.mean(err * err, axis=-1, keepdims=True))

        @pl.when(i == 0)
        def _():
            dg_ref[...] = p_g
            dg2_ref[...] = p_g2
            loss_ref[...] = p_l

        @pl.when(i > 0)
        def _():
            dg_ref[...] += p_g
            dg2_ref[...] += p_g2
            loss_ref[...] += p_l

    vec = jax.ShapeDtypeStruct((1, D), F32)
    rows = lambda w: pl.BlockSpec((FF2_TM, w), lambda i: (i, 0))
    return _pcall(body, name="ff2_final", grid=(S // FF2_TM,),
                  in_specs=[rows(D_FF), pl.BlockSpec((D_FF, D), lambda i: (0, 0)), rows(D), _vec_spec(), rows(D),
                            _vec_spec()],
                  out_specs=(_vec_spec(128), rows(D), _vec_spec(), rows(D), _vec_spec()),
                  out_shape=(jax.ShapeDtypeStruct((1, 128), F32), jax.ShapeDtypeStruct((S, D), F32), vec,
                             jax.ShapeDtypeStruct((S, D), BF16), vec),
                  compiler_params=_params(("arbitrary",)))(act, w_ff2, x1, g2, tgt, g)


HALF = 512


MERGE_TM = 1024


def _merge_specs():
    blk = lambda off: pl.BlockSpec((MERGE_TM, HALF), lambda i, j: (i, off // HALF + j))
    return blk(OFF_GA), blk(OFF_GB), blk(0)


def _att_out_merge(att, w_att_out, proj, ret_out):
    def body(a_ref, b_ref, ga_ref, gb_ref, r_ref, o_ref, m_ref):
        acc = _dot(a_ref[...], b_ref[...], NN)
        o_ref[...] = acc
        m_ref[...] = (jax.nn.sigmoid(ga_ref[...]) * r_ref[...] + jax.nn.sigmoid(gb_ref[...]) * acc).astype(BF16)

    ga, gb, tile = _merge_specs()
    return _pcall(body, name="att_out", grid=(S // MERGE_TM, D // HALF),
                  in_specs=[pl.BlockSpec((MERGE_TM, AW), lambda i, j: (i, 0)), pl.BlockSpec((AW, HALF), lambda i, j: (0, j)),
                            ga, gb, tile],
                  out_specs=(tile, tile),
                  out_shape=(jax.ShapeDtypeStruct((S, D), F32), jax.ShapeDtypeStruct((S, D), BF16)),
                  compiler_params=_params(("parallel", "parallel")))(att, w_att_out, proj, proj, ret_out)


def _dmerged_split(dmixo, w_o, proj, ret_out, att_out):
    def body(a_ref, b_ref, ga_ref, gb_ref, r_ref, at_ref, dr_ref, da_ref, dga_ref, dgb_ref):
        dm = _dot(a_ref[...], b_ref[...], NT)
        sa = jax.nn.sigmoid(ga_ref[...])
        sb = jax.nn.sigmoid(gb_ref[...])
        dr_ref[...] = (dm * sa).astype(BF16)
        da_ref[...] = (dm * sb).astype(BF16)
        dga_ref[...] = (dm * r_ref[...] * (sa * (1.0 - sa))).astype(BF16)
        dgb_ref[...] = (dm * at_ref[...] * (sb * (1.0 - sb))).astype(BF16)

    ga, gb, tile = _merge_specs()
    o = jax.ShapeDtypeStruct((S, D), BF16)
    return _pcall(body, name="dmerged", grid=(S // MERGE_TM, D // HALF),
                  in_specs=[pl.BlockSpec((MERGE_TM, D), lambda i, j: (i, 0)), pl.BlockSpec((HALF, D), lambda i, j: (j, 0)),
                            ga, gb, tile, tile],
                  out_specs=(tile,) * 4, out_shape=(o, o, o, o),
                  compiler_params=_params(("parallel", "parallel")))(dmixo, w_o, proj, proj, ret_out, att_out)


def _ret_tables():
    H, C = RET_HEADS, CHUNK
    log_g = jnp.log1p(-(2.0 ** (-5.0 - jnp.arange(H, dtype=F32))))
    idx = jnp.arange(C, dtype=F32)
    rel = idx[:, None] - idx[None, :]
    inner = jnp.where(rel >= 0, jnp.exp(log_g[:, None, None] * jnp.maximum(rel, 0.0)), 0.0)
    qd = jnp.exp(log_g[:, None] * (idx + 1.0))[:, :, None]
    kd = jnp.exp(log_g[:, None] * (C - 1.0 - idx))[:, :, None]
    cd = jnp.broadcast_to(jnp.exp(log_g * C)[:, None, None], (H, 1, 128))
    half = RET_DK // 2
    inv = 10000.0 ** (-jnp.arange(half, dtype=F32) / half)
    ang = jnp.arange(S, dtype=F32)[:, None] * inv[None, :]
    return inner, qd, kd, cd, jnp.cos(ang), jnp.sin(ang)


def _rot(x, cos, sin):
    x1, x2 = x[:, :128], x[:, 128:]
    return jnp.concatenate([x1 * cos - x2 * sin, x1 * sin + x2 * cos], axis=1)


def _rot_t(d, cos, sin):
    d1, d2 = d[:, :128], d[:, 128:]
    return jnp.concatenate([d1 * cos + d2 * sin, d2 * cos - d1 * sin], axis=1)


RET_COLS = OFF_ATT
RET_VW = RET_HEADS * RET_DV


def _ret_specs(chunk_of):
    ci = chunk_of
    whole = lambda shape: pl.BlockSpec(shape, lambda t: (0,) * len(shape))
    return [
        pl.BlockSpec((CHUNK, RET_COLS), lambda t: (ci(t), 0)),
        pl.BlockSpec((CHUNK, 128), lambda t: (ci(t), 0)),
        pl.BlockSpec((CHUNK, 128), lambda t: (ci(t), 0)),
        whole((RET_HEADS, CHUNK, CHUNK)), whole((RET_HEADS, CHUNK, 1)), whole((RET_HEADS, CHUNK, 1)),
        whole((RET_HEADS, 1, 128)), whole((1, RET_VW)), whole((1, RET_VW)),
    ]


def _ret_cols(h):
    q = slice(OFF_RQ + h * RET_DK, OFF_RQ + (h + 1) * RET_DK)
    k = slice(OFF_RK + h * RET_DK, OFF_RK + (h + 1) * RET_DK)
    v = slice(OFF_RV + h * RET_DV, OFF_RV + (h + 1) * RET_DV)
    g = slice(OFF_RG + h * RET_DV, OFF_RG + (h + 1) * RET_DV)
    return q, k, v, g, slice(h * RET_DV, (h + 1) * RET_DV)


def _ret_fwd(proj, tables, gn_g, gn_b, after):
    inner, qd, kd, cd, cos, sin = tables

    def body(x_ref, cos_ref, sin_ref, in_ref, qd_ref, kd_ref, cd_ref, g_ref, b_ref, after_ref,
             gated_ref, ro_ref, st_ref, s_scr):
        i = pl.program_id(0)

        @pl.when(i == 0)
        def _():
            s_scr[...] = jnp.zeros_like(s_scr)

        cosv, sinv = cos_ref[...], sin_ref[...]
        for h in range(RET_HEADS):
            cq, ck, cv, cg, co = _ret_cols(h)
            q = _rot(x_ref[:, cq], cosv, sinv)
            k = _rot(x_ref[:, ck], cosv, sinv) * (RET_DK ** -0.5)
            v = x_ref[:, cv]
            st = s_scr[h]
            st_ref[h] = st.astype(BF16)
            s = _dot(q, k, NT) * in_ref[h]
            o = _dot(s, v, NN) + _dot(q, st, NN) * qd_ref[h]
            s_scr[h] = st * cd_ref[h, :, :1] + _dot(k * kd_ref[h], v, TN)
            ro_ref[:, co] = o
            mu = jnp.mean(o, axis=-1, keepdims=True)
            oc = o - mu
            var = jnp.mean(oc * oc, axis=-1, keepdims=True)
            rn = oc * lax.rsqrt(var + GN_EPS) * g_ref[:, co] + b_ref[:, co]
            rg = x_ref[:, cg]
            gated_ref[:, co] = (rg * jax.nn.sigmoid(rg) * rn).astype(BF16)

    ospec = pl.BlockSpec((CHUNK, RET_VW), lambda t: (t, 0))
    return _pcall(
        body, name="ret_fwd", grid=(N_CHUNK,), in_specs=_ret_specs(lambda t: t) + [HBM_SPEC],
        out_specs=(ospec, ospec, pl.BlockSpec((RET_HEADS, None, RET_DK, RET_DV), lambda t: (0, t, 0, 0))),
        out_shape=(jax.ShapeDtypeStruct((S, RET_VW), BF16), jax.ShapeDtypeStruct((S, RET_VW), F32),
                   jax.ShapeDtypeStruct((RET_HEADS, N_CHUNK, RET_DK, RET_DV), BF16)),
        scratch_shapes=[pltpu.VMEM((RET_HEADS, RET_DK, RET_DV), F32)],
        compiler_params=_params(("arbitrary",)))(proj, cos, sin, inner, qd, kd, cd, gn_g, gn_b, after)


def _ret_bwd(proj, tables, gn_g, gn_b, ro, states, dgated, others):
    inner, qd, kd, cd, cos, sin = tables
    last = N_CHUNK - 1
    assert RET_COLS + sum(o.shape[1] for o in others) == IN_COLS

    def body(x_ref, cos_ref, sin_ref, in_ref, qd_ref, kd_ref, cd_ref, g_ref, b_ref, ro_ref, st_ref, dg_ref, *rest):
        other_refs, (dx_ref, gg_ref, gb_ref, gs_scr) = rest[:len(others)], rest[len(others):]
        t = pl.program_id(0)
        col = RET_COLS
        for o_ref in other_refs:
            dx_ref[:, col:col + o_ref.shape[1]] = o_ref[...]
            col += o_ref.shape[1]

        @pl.when(t == 0)
        def _():
            gs_scr[...] = jnp.zeros_like(gs_scr)
            gg_ref[...] = jnp.zeros_like(gg_ref)
            gb_ref[...] = jnp.zeros_like(gb_ref)

        cosv, sinv = cos_ref[...], sin_ref[...]
        for h in range(RET_HEADS):
            cq, ck, cv, cg, co = _ret_cols(h)
            q = _rot(x_ref[:, cq], cosv, sinv)
            k = _rot(x_ref[:, ck], cosv, sinv) * (RET_DK ** -0.5)
            v = x_ref[:, cv]
            qdv, kdv, dm = qd_ref[h], kd_ref[h], in_ref[h]
            st = st_ref[h]
            o = ro_ref[:, co]
            gv = g_ref[:, co]
            mu = jnp.mean(o, axis=-1, keepdims=True)
            oc = o - mu
            rstd = lax.rsqrt(jnp.mean(oc * oc, axis=-1, keepdims=True) + GN_EPS)
            ohat = oc * rstd
            rn = ohat * gv + b_ref[:, co]
            rg = x_ref[:, cg]
            sg = jax.nn.sigmoid(rg)
            dgt = dg_ref[:, co]
            drn = dgt * (rg * sg)
            dx_ref[:, cg] = (dgt * rn * (sg * (1.0 + rg * (1.0 - sg)))).astype(BF16)
            gg_ref[:, co] += jnp.sum(drn * ohat, axis=0, keepdims=True)
            gb_ref[:, co] += jnp.sum(drn, axis=0, keepdims=True)
            dohat = drn * gv
            do = rstd * (dohat - jnp.mean(dohat, axis=-1, keepdims=True)
                         - ohat * jnp.mean(dohat * ohat, axis=-1, keepdims=True))
            gs = gs_scr[h]
            s = _dot(q, k, NT) * dm
            dsr = _dot(do, v, NT) * dm
            dq = _dot(dsr, k, NN) + _dot(do, st, NT) * qdv
            dk = _dot(dsr, q, TN) + _dot(v, gs, NT) * kdv
            dv = _dot(s, do, TN) + _dot(k * kdv, gs, NN)
            gs_scr[h] = gs * cd_ref[h, :, :1] + _dot(q * qdv, do, TN)
            dx_ref[:, cq] = _rot_t(dq, cosv, sinv).astype(BF16)
            dx_ref[:, ck] = (_rot_t(dk, cosv, sinv) * (RET_DK ** -0.5)).astype(BF16)
            dx_ref[:, cv] = dv.astype(BF16)

    rev = lambda t: last - t
    vblk = pl.BlockSpec((CHUNK, RET_VW), lambda t: (rev(t), 0))
    vspec = pl.BlockSpec((1, RET_VW), lambda t: (0, 0))
    rows = lambda w: pl.BlockSpec((CHUNK, w), lambda t: (rev(t), 0))
    return _pcall(
        body, name="ret_bwd", grid=(N_CHUNK,),
        in_specs=_ret_specs(rev) + [vblk, pl.BlockSpec((RET_HEADS, None, RET_DK, RET_DV), lambda t: (0, rev(t), 0, 0)),
                                    vblk] + [rows(o.shape[1]) for o in others],
        out_specs=(rows(IN_COLS), vspec, vspec),
        out_shape=(jax.ShapeDtypeStruct((S, IN_COLS), BF16), jax.ShapeDtypeStruct((1, RET_VW), F32),
                   jax.ShapeDtypeStruct((1, RET_VW), F32)),
        scratch_shapes=[pltpu.VMEM((RET_HEADS, RET_DK, RET_DV), F32)],
        compiler_params=_params(("arbitrary",)))(proj, cos, sin, inner, qd, kd, cd, gn_g, gn_b, ro, states, dgated,
                                                 *others)


def _bucket_tables():
    qi = np.arange(ATT_BLK)[:, None]
    kj = np.arange(2 * ATT_BLK)[None, :]
    m = ATT_BLK + qi - kj
    out = []
    for win, dil in ATT_GROUPS:
        w = win // dil
        dist = (np.clip(m, 0, w) * dil).astype(np.int32)
        max_exact = N_BUCKETS // 2
        d_f = np.maximum(dist, 1).astype(np.float32)
        large = max_exact + (np.log(d_f / np.float32(max_exact)) / np.float32(math.log(MAX_DIST / max_exact))
                             * np.float32(N_BUCKETS - max_exact)).astype(np.int32)
        large = np.minimum(large, N_BUCKETS - 1)
        out.append(np.where(dist < max_exact, dist, large).astype(np.int32))
    return np.stack(out)


def _bias_build(rel_bias, buckets, after):
    def body(tab_ref, bk_ref, after_ref, o_ref):
        hh = pl.program_id(0)
        bk = bk_ref[...]
        acc = jnp.zeros((ATT_BLK, 2 * ATT_BLK), F32)
        for b in range(N_BUCKETS):
            acc = jnp.where(bk == b, tab_ref[b, hh], acc)
        o_ref[...] = acc

    nh = len(ATT_GROUPS) * ATT_HG
    return _pcall(body, name="bias_build", grid=(nh,),
                  in_specs=[pl.BlockSpec(memory_space=pltpu.SMEM),
                            pl.BlockSpec((None, ATT_BLK, 2 * ATT_BLK), lambda hh: (hh // ATT_HG, 0, 0)), HBM_SPEC],
                  out_specs=pl.BlockSpec((None, ATT_BLK, 2 * ATT_BLK), lambda hh: (hh, 0, 0)),
                  out_shape=jax.ShapeDtypeStruct((nh, ATT_BLK, 2 * ATT_BLK), F32),
                  compiler_params=_params(("parallel",)))(rel_bias, buckets, after)


def _bias_grad(ds_sum, buckets):
    def body(ds_ref, bk_ref, o_ref):
        bk = bk_ref[...]
        ds = ds_ref[...]
        rows = lax.broadcasted_iota(jnp.int32, (N_BUCKETS, 128), 0)
        acc = jnp.zeros((N_BUCKETS, 128), F32)
        for b in range(N_BUCKETS):
            acc = jnp.where(rows == b, jnp.sum(jnp.where(bk == b, ds, 0.0)), acc)
        o_ref[...] = acc

    nh = len(ATT_GROUPS) * ATT_HG
    return _pcall(body, name="bias_grad", grid=(nh,),
                  in_specs=[pl.BlockSpec((None, ATT_BLK, 2 * ATT_BLK), lambda hh: (hh, 0, 0)),
                            pl.BlockSpec((None, ATT_BLK, 2 * ATT_BLK), lambda hh: (hh // ATT_HG, 0, 0))],
                  out_specs=pl.BlockSpec((None, N_BUCKETS, 128), lambda hh: (hh, 0, 0)),
                  out_shape=jax.ShapeDtypeStruct((nh, N_BUCKETS, 128), F32),
                  compiler_params=_params(("parallel",)))(ds_sum, buckets)


def _att_valid(n):
    qi = lax.broadcasted_iota(jnp.int32, (ATT_BLK, 2 * ATT_BLK), 0)
    kj = lax.broadcasted_iota(jnp.int32, (ATT_BLK, 2 * ATT_BLK), 1)
    m = ATT_BLK + qi - kj
    first_key = jnp.where(n > 0, 0, ATT_BLK)
    return (m >= 0) & (m <= ATT_BLK) & (kj >= first_key)


ATT_HP = (1, 2, 2)


def _att_geometry(gi):
    _, dil = ATT_GROUPS[gi]
    return dil, S // dil // ATT_BLK, ATT_HP[gi]


def _blk(dil, r, n):
    if dil == 1:
        return pl.ds(n * ATT_BLK, ATT_BLK)
    return pl.ds(r + n * ATT_BLK * dil, ATT_BLK, stride=dil)


def _slab_specs(gi):
    _, _, hp = _att_geometry(gi)
    per = ATT_HG // hp
    return [pl.BlockSpec((hp, S, ATT_DH), lambda g, r, part=part: ((3 * gi + part) * per + g, 0, 0))
            for part in range(3)]


def _head_specs(gi, count):
    _, _, hp = _att_geometry(gi)
    return [pl.BlockSpec((hp, S, ATT_DH), lambda g, r: (g, 0, 0))] * count


def _bias_spec(gi):
    _, _, hp = _att_geometry(gi)
    return pl.BlockSpec((hp, ATT_BLK, 2 * ATT_BLK), lambda g, r: (gi * (ATT_HG // hp) + g, 0, 0))


def _att_valid_first():
    qi = lax.broadcasted_iota(jnp.int32, (ATT_BLK, ATT_BLK), 0)
    kj = lax.broadcasted_iota(jnp.int32, (ATT_BLK, ATT_BLK), 1)
    return kj <= qi


def _att_fwd(slabs, bias, gi, comm=None):
    dil, nb, hp = _att_geometry(gi)
    scale = ATT_DH ** -0.5

    def body(q_ref, k_ref, v_ref, bias_ref, o_ref, l_ref):
        r = pl.program_id(1)
        for n in range(nb):
            cur = _blk(dil, r, n)
            valid = _att_valid(n) if n > 0 else _att_valid_first()
            for h in range(hp):
                if n > 0:
                    prev = _blk(dil, r, n - 1)
                    kk = jnp.concatenate([k_ref[h, prev, :], k_ref[h, cur, :]], axis=0)
                    vv = jnp.concatenate([v_ref[h, prev, :], v_ref[h, cur, :]], axis=0)
                    bias = bias_ref[h]
                else:
                    kk, vv, bias = k_ref[h, cur, :], v_ref[h, cur, :], bias_ref[h, :, pl.ds(ATT_BLK, ATT_BLK)]
                s = _dot(q_ref[h, cur, :], kk, NT) * scale + bias
                s = jnp.where(valid, s, -1e30)
                mx = jnp.max(s, axis=-1, keepdims=True)
                e = jnp.exp(s - mx)
                den = jnp.sum(e, axis=-1, keepdims=True)
                o_ref[h, cur, :] = _dot(e / den, vv, NN)
                l_ref[h, cur, :] = jnp.broadcast_to(mx + jnp.log(den), (ATT_BLK, ATT_DH))

    osh = jax.ShapeDtypeStruct((ATT_HG, S, ATT_DH), F32)
    kw = dict(name=f"att_fwd{gi}", grid=(ATT_HG // hp, dil), in_specs=_slab_specs(gi) + [_bias_spec(gi)],
              out_specs=tuple(_head_specs(gi, 2)), out_shape=(osh, osh))
    if comm is not None:
        return _carry(body, comm, **kw)(slabs, slabs, slabs, bias)
    return _pcall(body, compiler_params=_params(("parallel", "arbitrary")), **kw)(slabs, slabs, slabs, bias)


def _att_bwd(slabs, bias, o, lse, do, dlse, gi, comm=None):
    dil, nb, hp = _att_geometry(gi)
    per = ATT_HG // hp
    scale = ATT_DH ** -0.5
    wh = hp * ATT_DH
    wide = lambda t: jnp.concatenate([t, t], axis=1)

    def body(q_ref, k_ref, v_ref, bias_ref, o_ref, l_ref, do_ref, dl_ref, dq_ref, dk_ref, dv_ref, ds_ref):
        r = pl.program_id(1)

        @pl.when(r == 0)
        def _():
            ds_ref[...] = jnp.zeros_like(ds_ref)

        for h in range(hp):
            sl = slice(h * ATT_DH, (h + 1) * ATT_DH)
            carry_k = carry_v = None
            for n in range(nb):
                cur = _blk(dil, r, n)
                q = q_ref[h, cur, :]
                dov = do_ref[h, cur, :]
                delta = jnp.sum(dov * o_ref[h, cur, :], axis=-1, keepdims=True)
                out_rows = pl.ds(n * ATT_BLK, ATT_BLK)
                if n == 0:
                    own = pl.ds(ATT_BLK, ATT_BLK)
                    kk, vv = k_ref[h, cur, :], v_ref[h, cur, :]
                    s = _dot(q, kk, NT) * scale + bias_ref[h, :, own]
                    p = jnp.where(_att_valid_first(), jnp.exp(s - l_ref[h, cur, :]), 0.0)
                    ds = p * (_dot(dov, vv, NT) - delta + dl_ref[h, cur, :])
                    ds_ref[h, :, own] += ds
                    dq_ref[out_rows, sl] = (_dot(ds, kk, NN) * scale).astype(BF16)
                    carry_k, carry_v = _dot(ds, q, TN) * scale, _dot(p, dov, TN)
                    continue
                prev = _blk(dil, r, n - 1)
                kk = jnp.concatenate([k_ref[h, prev, :], k_ref[h, cur, :]], axis=0)
                vv = jnp.concatenate([v_ref[h, prev, :], v_ref[h, cur, :]], axis=0)
                s = _dot(q, kk, NT) * scale + bias_ref[h]
                p = jnp.where(_att_valid(n), jnp.exp(s - wide(l_ref[h, cur, :])), 0.0)
                dp = _dot(dov, vv, NT)
                ds = p * (dp - delta + wide(dl_ref[h, cur, :]))
                ds_ref[h] += ds
                dq_ref[out_rows, sl] = (_dot(ds, kk, NN) * scale).astype(BF16)
                dkk = _dot(ds, q, TN) * scale
                dvv = _dot(p, dov, TN)
                before = pl.ds((n - 1) * ATT_BLK, ATT_BLK)
                dk_ref[before, sl] = (carry_k + dkk[:ATT_BLK]).astype(BF16)
                dv_ref[before, sl] = (carry_v + dvv[:ATT_BLK]).astype(BF16)
                carry_k, carry_v = dkk[ATT_BLK:], dvv[ATT_BLK:]
            last = pl.ds((nb - 1) * ATT_BLK, ATT_BLK)
            dk_ref[last, sl] = carry_k.astype(BF16)
            dv_ref[last, sl] = carry_v.astype(BF16)

    out_spec = pl.BlockSpec((S // dil, wh), lambda g, r: (0, r * per + g))
    osh = jax.ShapeDtypeStruct((S // dil, dil * AW), BF16)
    kw = dict(name=f"att_bwd{gi}", grid=(per, dil), in_specs=_slab_specs(gi) + [_bias_spec(gi)] + _head_specs(gi, 4),
              out_specs=(out_spec, out_spec, out_spec,
                         pl.BlockSpec((hp, ATT_BLK, 2 * ATT_BLK), lambda g, r: (g, 0, 0))),
              out_shape=(osh, osh, osh, jax.ShapeDtypeStruct((ATT_HG, ATT_BLK, 2 * ATT_BLK), F32)))
    args = (slabs, slabs, slabs, bias, o, lse, do, dlse)
    if comm is not None:
        return _carry(body, comm, **kw)(*args)
    return _pcall(body, compiler_params=_params(("arbitrary", "arbitrary")), **kw)(*args)


AW = ATT_HG * ATT_DH


def _mix_weights(l0, l1, l2):
    mx = jnp.maximum(jnp.maximum(l0, l1), l2)
    e0, e1, e2 = jnp.exp(l0 - mx), jnp.exp(l1 - mx), jnp.exp(l2 - mx)
    den = e0 + e1 + e2
    return e0 / den, e1 / den, e2 / den


def _heads_spec():
    return pl.BlockSpec((ATT_HG, TR, ATT_DH), lambda i: (0, i, 0))


def _mix_fwd(os_, ls, comm=None):
    def body(o0, o1, o2, l0, l1, l2, att_ref):
        for h in range(ATT_HG):
            w0, w1, w2 = _mix_weights(l0[h], l1[h], l2[h])
            att_ref[:, h * ATT_DH:(h + 1) * ATT_DH] = (w0 * o0[h] + w1 * o1[h] + w2 * o2[h]).astype(BF16)

    kw = dict(name="mix_fwd", grid=(S // TR,), in_specs=[_heads_spec()] * 6, out_specs=_row_spec(AW),
              out_shape=jax.ShapeDtypeStruct((S, AW), BF16))
    if comm is not None:
        return _carry(body, comm, **kw)(*os_, *ls)
    return _pcall(body, compiler_params=_params(("parallel",)), **kw)(*os_, *ls)


def _mix_bwd(os_, ls, datt):
    def body(o0, o1, o2, l0, l1, l2, da_ref, d0, d1, d2, e0, e1, e2):
        for h in range(ATT_HG):
            ws = _mix_weights(l0[h], l1[h], l2[h])
            da = da_ref[:, h * ATT_DH:(h + 1) * ATT_DH]
            dws = []
            for o_ref, w, d_ref in zip((o0, o1, o2), ws, (d0, d1, d2)):
                d_ref[h] = w * da
                dws.append(jnp.broadcast_to(jnp.sum(da * o_ref[h], axis=-1, keepdims=True), (TR, ATT_DH)))
            tot = ws[0] * dws[0] + ws[1] * dws[1] + ws[2] * dws[2]
            for w, dw, e_ref in zip(ws, dws, (e0, e1, e2)):
                e_ref[h] = w * (dw - tot)

    o = jax.ShapeDtypeStruct((ATT_HG, S, ATT_DH), F32)
    return _pcall(body, name="mix_bwd", grid=(S // TR,), in_specs=[_heads_spec()] * 6 + [_row_spec(AW)],
                  out_specs=(_heads_spec(),) * 6, out_shape=(o,) * 6,
                  compiler_params=_params(("parallel",)))(*os_, *ls, datt)


def _ada_fwd(c_all, w_sh, b_sl):
    def body(c_ref, w_ref, b_ref, o_ref):
        cv = c_ref[...]
        o_ref[...] = _dot(cv * jax.nn.sigmoid(cv), w_ref[...], NN) + b_ref[...]

    return _pcall(body, name="ada_fwd", out_shape=jax.ShapeDtypeStruct((N_DEV, w_sh.shape[1]), F32),
                  compiler_params=_params())(c_all, w_sh, b_sl)


def _ada_bwd(c_all, dm_sl):
    def body(c_ref, d_ref, o_ref):
        cv = c_ref[...]
        o_ref[...] = _dot(cv * jax.nn.sigmoid(cv), d_ref[...], TN)

    return _pcall(body, name="ada_bwd", out_shape=jax.ShapeDtypeStruct((D, dm_sl.shape[1]), F32),
                  compiler_params=_params())(c_all, dm_sl)


N_MOD = 6


def _sum_small(gathered):
    n = len(gathered)

    def body(*refs):
        ins, (gb_ref, dm_ref), outs = refs[:n], refs[n:n + 2], refs[n + 2:]

        def total(r):
            acc = r[0]
            for e in range(1, N_DEV):
                acc = acc + r[e]
            return acc

        for i in range(N_MOD):
            cols = slice(i * D, (i + 1) * D)
            gb_ref[:, cols] = total(ins[i])
            for e in range(N_DEV):
                dm_ref[e:e + 1, cols] = ins[i][e]
        for r, o_ref in zip(ins[N_MOD:], outs):
            o_ref[...] = total(r)

    shapes = (jax.ShapeDtypeStruct((1, N_MOD * D), F32), jax.ShapeDtypeStruct((N_DEV, N_MOD * D), F32),
              *[jax.ShapeDtypeStruct(g.shape[1:], F32) for g in gathered[N_MOD:]])
    res = _pcall(body, name="sum_small", out_shape=shapes, compiler_params=_params())(*gathered)
    return res[0], res[1], res[2:]


def _row_tile(m, n):
    t = max(8, min(m, (1 << 19) // n // 8 * 8))
    while m % t:
        t -= 8
    return t


def _pair_sum(full, recv, sel, name, col_block=0):
    _, m, n = recv.shape
    t = _row_tile(m, n)

    def body(sel_ref, a_ref, b_ref, o_ref):
        o_ref[...] = (a_ref[...].astype(F32) + b_ref[...].astype(F32)).astype(o_ref.dtype)

    gs = pltpu.PrefetchScalarGridSpec(
        num_scalar_prefetch=1, grid=(4, m // t),
        in_specs=[pl.BlockSpec((None, None, t, n), lambda q, i, s: (q, s[0], i, col_block)),
                  pl.BlockSpec((None, t, n), lambda q, i, s: (q, i, 0))],
        out_specs=pl.BlockSpec((None, t, n), lambda q, i, s: (q, i, 0)))
    return _pcall(body, name=name, grid_spec=gs, out_shape=jax.ShapeDtypeStruct((4, m, n), full.dtype),
                  compiler_params=_params(("parallel", "parallel")))(sel, full, recv)


def _chip_sum(part, recv, sel, name):
    _, m, n = part.shape
    t = _row_tile(m, n)

    def body(sel_ref, a_ref, r_ref, o_ref):
        o_ref[...] = ((a_ref[...].astype(F32) + r_ref[0].astype(F32)) + r_ref[1].astype(F32)) + r_ref[2].astype(F32)

    gs = pltpu.PrefetchScalarGridSpec(
        num_scalar_prefetch=1, grid=(m // t,),
        in_specs=[pl.BlockSpec((None, t, n), lambda i, s: (s[0], i, 0)),
                  pl.BlockSpec((3, t, n), lambda i, s: (0, i, 0))],
        out_specs=pl.BlockSpec((t, n), lambda i, s: (i, 0)))
    return _pcall(body, name=name, grid_spec=gs, out_shape=jax.ShapeDtypeStruct((m, n), F32),
                  compiler_params=_params(("parallel",)))(sel, part, recv)


def _adamw_math(w, g, m, v):
    nm = ADAM_B1 * m + (1.0 - ADAM_B1) * g
    nv = ADAM_B2 * v + (1.0 - ADAM_B2) * (g * g)
    m_hat = nm / (1.0 - ADAM_B1 ** ADAM_STEP)
    v_hat = nv / (1.0 - ADAM_B2 ** ADAM_STEP)
    return -ADAM_LR * (m_hat / (jnp.sqrt(v_hat) + ADAM_EPS) + ADAM_WD * w), nm, nv


def _adamw(w, g, m, v, name):
    _, rows, cols = w.shape
    t = _row_tile(rows, cols)

    def body(w_ref, g_ref, m_ref, v_ref, d_ref, nm_ref, nv_ref):
        d_ref[...], nm_ref[...], nv_ref[...] = _adamw_math(w_ref[...], g_ref[...], m_ref[...], v_ref[...])

    spec3 = pl.BlockSpec((None, t, cols), lambda i: (0, i, 0))
    spec2 = pl.BlockSpec((t, cols), lambda i: (i, 0))
    o = jax.ShapeDtypeStruct(w.shape, F32)
    return _pcall(body, name=name, grid=(rows // t,), in_specs=[spec3, spec2, spec3, spec3], out_specs=(spec3,) * 3,
                  out_shape=(o, o, o), compiler_params=_params(("parallel",)))(w, g, m, v)


def _adamw_reduced1(w, m, v, part, recv, sel, name):
    _, rows, cols = w.shape
    t = _row_tile(rows, cols)

    def body(sel_ref, w_ref, m_ref, v_ref, p_ref, r_ref, g_ref, d_ref, nm_ref, nv_ref):
        g = ((p_ref[...].astype(F32) + r_ref[0].astype(F32)) + r_ref[1].astype(F32)) + r_ref[2].astype(F32)
        g_ref[...] = g
        d_ref[...], nm_ref[...], nv_ref[...] = _adamw_math(w_ref[...], g, m_ref[...], v_ref[...])

    wspec = pl.BlockSpec((None, t, cols), lambda i, s: (0, i, 0))
    gs = pltpu.PrefetchScalarGridSpec(
        num_scalar_prefetch=1, grid=(rows // t,),
        in_specs=[wspec, wspec, wspec, pl.BlockSpec((None, t, cols), lambda i, s: (s[0], i, 0)),
                  pl.BlockSpec((3, t, cols), lambda i, s: (0, i, 0))],
        out_specs=(wspec,) * 4)
    o = jax.ShapeDtypeStruct(w.shape, F32)
    return _pcall(body, name=name, grid_spec=gs, out_shape=(o, o, o, o),
                  compiler_params=_params(("parallel",)))(sel, w, m, v, part, recv)


def _adamw_reduced(w, m, v, parts, recvs, sel):
    _, rows, cols = w.shape
    half = cols // 2
    t = _row_tile(rows, half)

    def body(sel_ref, w_ref, m_ref, v_ref, pa_ref, pb_ref, ra_ref, rb_ref, g_ref, d_ref, nm_ref, nv_ref):
        total = lambda p_ref, r_ref: ((p_ref[...].astype(F32) + r_ref[0].astype(F32)) + r_ref[1].astype(F32)) \
            + r_ref[2].astype(F32)
        g = jnp.where(pl.program_id(1) == 0, total(pa_ref, ra_ref), total(pb_ref, rb_ref))
        g_ref[...] = g
        d_ref[...], nm_ref[...], nv_ref[...] = _adamw_math(w_ref[...], g, m_ref[...], v_ref[...])

    wspec = pl.BlockSpec((None, t, half), lambda i, j, s: (0, i, j))
    pspec = pl.BlockSpec((None, t, half), lambda i, j, s: (s[0], i, 0))
    rspec = pl.BlockSpec((3, t, half), lambda i, j, s: (0, i, 0))
    gs = pltpu.PrefetchScalarGridSpec(num_scalar_prefetch=1, grid=(rows // t, 2),
                                      in_specs=[wspec, wspec, wspec, pspec, pspec, rspec, rspec],
                                      out_specs=(wspec,) * 4)
    o = jax.ShapeDtypeStruct(w.shape, F32)
    return _pcall(body, name="adamw_w_in", grid_spec=gs, out_shape=(o, o, o, o),
                  compiler_params=_params(("parallel", "arbitrary")))(sel, w, m, v, *parts, *recvs)


def _adamw_small(ws, gs, ms, vs):
    n = len(ws)

    def body(*refs):
        for i in range(n):
            w_ref, g_ref, m_ref, v_ref = (refs[k * n + i] for k in range(4))
            d, nm, nv = _adamw_math(w_ref[...], g_ref[...], m_ref[...], v_ref[...])
            refs[4 * n + i][...] = d
            refs[5 * n + i][...] = nm
            refs[6 * n + i][...] = nv

    shapes = tuple(jax.ShapeDtypeStruct(w.shape, F32) for w in ws)
    res = _pcall(body, name="adamw_small", out_shape=shapes * 3, compiler_params=_params())(*ws, *gs, *ms, *vs)
    return res[:n], res[n:2 * n], res[2 * n:]


def _mesh_pos():
    return lax.axis_index("x"), lax.axis_index("y"), lax.axis_index("c")


class _Gather:
    def __init__(self, arrs):
        self.ins = list(arrs)
        self.out_shape = tuple(jax.ShapeDtypeStruct((N_DEV,) + a.shape, a.dtype) for a in arrs)
        n = len(arrs)
        self.sems = [pltpu.SemaphoreType.DMA((7 * n,)), pltpu.SemaphoreType.DMA((7 * n,)),
                     pltpu.SemaphoreType.DMA((n,))]

    def _copies(self, ins, outs, sems):
        send_sems, recv_sems, local_sems = sems
        x, y, c = _mesh_pos()
        me, sibling = (x, y, c), (x, y, 1 - c)
        chips = [(1 - x, y), (x, 1 - y), (1 - x, 1 - y)]

        def copy(p, k, block, to, from_input=False):
            dst = outs[p].at[_slot(block)]
            return pltpu.make_async_remote_copy(
                src_ref=ins[p] if from_input else dst, dst_ref=dst, send_sem=send_sems.at[7 * p + k],
                recv_sem=recv_sems.at[7 * p + k], device_id=to, device_id_type=MESH)

        npc = len(self.ins)
        mine = [pltpu.make_async_copy(ins[p], outs[p].at[_slot(me)], local_sems.at[p]) for p in range(npc)]
        first = []
        for p in range(npc):
            first.append(copy(p, 0, me, sibling, from_input=True))
            first += [copy(p, 1 + j, me, (*chip, c), from_input=True) for j, chip in enumerate(chips)]
        return me, sibling, chips, c, copy, mine, first

    def start(self, ins, outs, sems):
        *_, mine, first = self._copies(ins, outs, sems)
        for cp in mine + first:
            cp.start()

    def finish(self, ins, outs, sems):
        me, sibling, chips, c, copy, mine, first = self._copies(ins, outs, sems)
        npc = len(self.ins)
        passed = []
        for p in range(npc):
            for j, chip in enumerate(chips):
                copy(p, 1 + j, (*chip, c), me).wait_recv()
                passed.append(copy(p, 4 + j, (*chip, c), sibling))
                passed[-1].start()
        for p in range(npc):
            copy(p, 0, sibling, me).wait_recv()
            for j, chip in enumerate(chips):
                copy(p, 4 + j, (*chip, 1 - c), me).wait_recv()
        for cp in first + passed:
            cp.wait_send()
        for cp in mine:
            cp.wait()


class _ExchangeCore:
    def __init__(self, fulls, cols=None):
        self.ins = list(fulls)
        self.cols = cols
        width = lambda f: f.shape[3] if cols is None else cols[1]
        self.out_shape = tuple(jax.ShapeDtypeStruct((4, f.shape[2], width(f)), f.dtype) for f in fulls)
        self.sems = [pltpu.SemaphoreType.DMA((4 * len(fulls),)), pltpu.SemaphoreType.DMA((4 * len(fulls),))]

    def _copies(self, ins, outs, sems):
        send_sems, recv_sems = sems
        x, y, c = _mesh_pos()

        def src(a, q):
            ref = ins[a].at[q, 1 - c]
            return ref if self.cols is None else ref.at[:, pl.ds(*self.cols)]

        return [pltpu.make_async_remote_copy(
            src_ref=src(a, q), dst_ref=outs[a].at[q], send_sem=send_sems.at[4 * a + q],
            recv_sem=recv_sems.at[4 * a + q], device_id=(x, y, 1 - c), device_id_type=MESH)
            for a in range(len(self.ins)) for q in range(4)]

    def start(self, ins, outs, sems):
        for cp in self._copies(ins, outs, sems):
            cp.start()

    def finish(self, ins, outs, sems):
        for cp in self._copies(ins, outs, sems):
            cp.wait()


class _ExchangeChip:
    def __init__(self, parts):
        self.ins = list(parts)
        self.out_shape = tuple(jax.ShapeDtypeStruct((3,) + p.shape[1:], p.dtype) for p in parts)
        self.sems = [pltpu.SemaphoreType.DMA((3 * len(parts),)), pltpu.SemaphoreType.DMA((3 * len(parts),))]

    def _copies(self, ins, outs, sems):
        send_sems, recv_sems = sems
        x, y, c = _mesh_pos()
        chips = [(1 - x, y), (x, 1 - y), (1 - x, 1 - y)]
        return [pltpu.make_async_remote_copy(
            src_ref=ins[a].at[2 * px + py], dst_ref=outs[a].at[j], send_sem=send_sems.at[3 * a + j],
            recv_sem=recv_sems.at[3 * a + j], device_id=(px, py, c), device_id_type=MESH)
            for a in range(len(self.ins)) for j, (px, py) in enumerate(chips)]

    def start(self, ins, outs, sems):
        for cp in self._copies(ins, outs, sems):
            cp.start()

    def finish(self, ins, outs, sems):
        for cp in self._copies(ins, outs, sems):
            cp.wait()


HBM_ONLY = pl.BlockSpec(memory_space=pltpu.HBM)
SEM_SPEC = pl.BlockSpec(memory_space=pltpu.SEMAPHORE)
SIDE_EFFECT = pltpu.SideEffectType.DATAFLOW_SIDE_EFFECTING


def _chip_copies(p_refs, land_refs, send_sems, recv_sems):
    x, y, c = _mesh_pos()
    return [pltpu.make_async_remote_copy(
        src_ref=p_refs[a].at[2 * px + py], dst_ref=land_refs[a].at[j], send_sem=send_sems.at[3 * a + j],
        recv_sem=recv_sems.at[3 * a + j], device_id=(px, py, c), device_id_type=MESH)
        for a in range(len(p_refs)) for j, (px, py) in enumerate([(1 - x, y), (x, 1 - y), (1 - x, 1 - y)])]


def _chip_exchange_start(parts, name):
    n = len(parts)
    lands = [lax.empty((3,) + p.shape[1:], p.dtype) for p in parts]

    def body(*refs):
        p_refs, land_refs, (send_sems, recv_sems) = refs[:n], refs[n:2 * n], refs[2 * n:2 * n + 2]
        for cp in _chip_copies(p_refs, land_refs, send_sems, recv_sems):
            cp.start()
        token = refs[-1]
        token[...] = jnp.zeros_like(token)

    hbm = lambda t: pltpu.HBM(t.shape, t.dtype)
    res = pl.pallas_call(
        body, name=name,
        out_shape=(pltpu.SemaphoreType.DMA((3 * n,)), pltpu.SemaphoreType.DMA((3 * n,)), *[hbm(t) for t in parts + lands],
                   jax.ShapeDtypeStruct((8, 128), F32)),
        in_specs=(HBM_ONLY,) * (2 * n),
        out_specs=(SEM_SPEC, SEM_SPEC, *[HBM_ONLY] * (2 * n), pl.BlockSpec(memory_space=pltpu.VMEM)),
        input_output_aliases={i: 2 + i for i in range(2 * n)},
        compiler_params=pltpu.CompilerParams(has_side_effects=SIDE_EFFECT))(
        *[pltpu.with_memory_space_constraint(t, pltpu.HBM) for t in parts + lands])
    return (res[0], res[1], list(res[2:2 + n]), list(res[2 + n:2 + 2 * n])), res[-1]


def _chip_exchange_wait(in_flight, after, name):
    send_sems, recv_sems, parts, lands = in_flight
    n = len(parts)

    def body(*refs):
        p_refs, land_refs, (send_sems, recv_sems) = refs[:n], refs[n:2 * n], refs[2 * n:2 * n + 2]
        for cp in _chip_copies(p_refs, land_refs, send_sems, recv_sems):
            cp.wait_send()
            cp.wait_recv()

    res = pl.pallas_call(
        body, name=name, out_shape=tuple(pltpu.HBM(t.shape, t.dtype) for t in parts + lands),
        in_specs=(*[HBM_ONLY] * (2 * n), SEM_SPEC, SEM_SPEC, pl.BlockSpec(memory_space=pl.ANY)),
        out_specs=(HBM_ONLY,) * (2 * n), input_output_aliases={i: i for i in range(2 * n)},
        compiler_params=pltpu.CompilerParams(has_side_effects=SIDE_EFFECT))(*parts, *lands, send_sems, recv_sems, after)
    return list(res[:n]), list(res[n:])


def _slot(p):
    return 4 * p[0] + 2 * p[1] + p[2]


def _gather_copies(src_refs, out_refs, send_sems, recv_sems):
    x, y, c = _mesh_pos()
    targets = [(x, y, 1 - c), (1 - x, y, c), (x, 1 - y, c), (1 - x, 1 - y, c)]
    return [pltpu.make_async_remote_copy(
        src_ref=src_refs[a], dst_ref=out_refs[a].at[_slot((x, y, c))], send_sem=send_sems.at[4 * a + k],
        recv_sem=recv_sems.at[4 * a + k], device_id=to, device_id_type=MESH)
        for a in range(len(src_refs)) for k, to in enumerate(targets)]


def _gather_start(shards, after, name):
    n = len(shards)
    outs = [lax.empty((N_DEV,) + s.shape, s.dtype) for s in shards]

    def body(*refs):
        for cp in _gather_copies(refs[:n], refs[n:2 * n], refs[2 * n + 1], refs[2 * n + 2]):
            cp.start()
        token = refs[-1]
        token[...] = jnp.zeros_like(token)

    res = pl.pallas_call(
        body, name=name,
        out_shape=(pltpu.SemaphoreType.DMA((4 * n,)), pltpu.SemaphoreType.DMA((4 * n,)),
                   *[pltpu.HBM(t.shape, t.dtype) for t in shards + outs], jax.ShapeDtypeStruct((8, 128), F32)),
        in_specs=(*[HBM_ONLY] * (2 * n), pl.BlockSpec(memory_space=pl.ANY)),
        out_specs=(SEM_SPEC, SEM_SPEC, *[HBM_ONLY] * (2 * n), pl.BlockSpec(memory_space=pltpu.VMEM)),
        input_output_aliases={i: 2 + i for i in range(2 * n)},
        compiler_params=pltpu.CompilerParams(has_side_effects=SIDE_EFFECT))(
        *[pltpu.with_memory_space_constraint(t, pltpu.HBM) for t in shards + outs], after)
    return (res[0], res[1], list(res[2:2 + n]), list(res[2 + n:2 + 2 * n])), res[-1]


def _gather_wait(in_flight, after, name):
    send_sems, recv_sems, shards, outs = in_flight
    n = len(shards)

    def body(*refs):
        for cp in _gather_copies(refs[:n], refs[n:2 * n], refs[2 * n], refs[2 * n + 1]):
            cp.wait_send()
            cp.wait_recv()

    res = pl.pallas_call(
        body, name=name, out_shape=tuple(pltpu.HBM(t.shape, t.dtype) for t in shards + outs),
        in_specs=(*[HBM_ONLY] * (2 * n), SEM_SPEC, SEM_SPEC, pl.BlockSpec(memory_space=pl.ANY)),
        out_specs=(HBM_ONLY,) * (2 * n), input_output_aliases={i: i for i in range(2 * n)},
        compiler_params=pltpu.CompilerParams(has_side_effects=SIDE_EFFECT))(*shards, *outs, send_sems, recv_sems, after)
    return list(res[:n]), list(res[n:])


class _PassToSibling:
    def __init__(self, shards, gathered):
        n = self.n = len(shards)
        self.ins = list(shards) + list(gathered)
        self.out_shape = tuple(jax.ShapeDtypeStruct(g.shape, g.dtype) for g in gathered)
        self.aliases = {n + a: a for a in range(n)}
        self.sems = [pltpu.SemaphoreType.DMA((3 * n,)), pltpu.SemaphoreType.DMA((3 * n,)),
                     pltpu.SemaphoreType.DMA((n,))]

    def _copies(self, ins, outs, sems):
        send_sems, recv_sems, local_sems = sems
        x, y, c = _mesh_pos()
        chips = [(1 - x, y), (x, 1 - y), (1 - x, 1 - y)]
        mine = [pltpu.make_async_copy(ins[a], outs[a].at[_slot((x, y, c))], local_sems.at[a]) for a in range(self.n)]
        passed, awaited = [], []
        for a in range(self.n):
            for j, chip in enumerate(chips):
                sems_j = dict(send_sem=send_sems.at[3 * a + j], recv_sem=recv_sems.at[3 * a + j],
                              device_id=(x, y, 1 - c), device_id_type=MESH)
                blk = outs[a].at[_slot((*chip, c))]
                passed.append(pltpu.make_async_remote_copy(src_ref=blk, dst_ref=blk, **sems_j))
                got = outs[a].at[_slot((*chip, 1 - c))]
                awaited.append(pltpu.make_async_remote_copy(src_ref=got, dst_ref=got, **sems_j))
        return mine, passed, awaited

    def start(self, ins, outs, sems):
        mine, passed, _ = self._copies(ins, outs, sems)
        for cp in mine + passed:
            cp.start()

    def finish(self, ins, outs, sems):
        mine, passed, awaited = self._copies(ins, outs, sems)
        for cp in passed:
            cp.wait_send()
        for cp in awaited:
            cp.wait_recv()
        for cp in mine:
            cp.wait()


def _reduce_sums(fulls, recv_core, core, tag):
    return [_pair_sum(f, r, core, f"rs_pair_{tag}{i}") for i, (f, r) in enumerate(zip(fulls, recv_core))]


def _local_step(x, tgt, mods, w_in_shard, order, shards, small, chip, core):
    sh1, sc1, g1, sh2, sc2, g2 = mods
    norm1_g, rel_bias, gn_g, gn_b, norm2_g, norm_f_g = small
    tables = _ret_tables()
    buckets = jnp.asarray(_bucket_tables())

    h1 = _norm_mod_fwd(x, norm1_g, sh1, sc1, "norm1_fwd")
    proj, slabs, w_in_t = _gather_proj(h1, w_in_shard, order)
    flight_w1, token_w = _gather_start(list(shards[:3]), proj, "gather_w1_start")
    flight_w2, token_w = _gather_start(list(shards[3:]), token_w, "gather_w2_start")
    bias = _bias_build(rel_bias, buckets, token_w)
    outs, lses = [], []
    for gi in range(len(ATT_GROUPS)):
        o, l = _att_fwd(slabs, bias, gi)
        outs.append(o)
        lses.append(l)
    att, gathered = _mix_fwd(outs, lses, comm=_PassToSibling(*_gather_wait(flight_w1, lses[2], "gather_w1_wait")))
    w_ret_out, w_att_out, w_o = (_from_slots(g, ax) for g, ax in zip(gathered, BIG_AXES[1:4]))
    gated, ro, states = _ret_fwd(proj, tables, gn_g, gn_b, att)
    ret_out, gathered = _mm(gated, w_ret_out, 'nn', tm=S, tn=256, tk=2048, name="ret_out",
                            comm=_PassToSibling(*_gather_wait(flight_w2, gated, "gather_w2_wait")))
    w_ff1, w_ff2 = (_from_slots(g, ax) for g, ax in zip(gathered, BIG_AXES[4:]))
    att_out, merged = _att_out_merge(att, w_att_out, proj, ret_out)
    mixo, x1, h2 = _w_o_norm2(merged, w_o, x, g1, norm2_g, sh2, sc2)
    u, act = _mm(h2, w_ff1, 'nn', tm=S, tn=512, tk=D, name="ff1", relu2=True)
    loss, dx2, g_normf, df, dg2 = _ff2_final(act, w_ff2, x1, g2, tgt, norm_f_g)

    gw_ff2 = _mm(act, df, 'tn', tm=512, tn=D, tk=S, name="gw_ff2", out_dtype=BF16)
    du = _mm(df, w_ff2, 'nt', tm=S, tn=512, tk=D, name="d_act", out_dtype=BF16, relu2_of=u)
    gw_ff1 = _mm(h2, du, 'tn', tm=D, tn=512, tk=S, name="gw_ff1", out_dtype=BF16)
    fulls_a = [_to_slots(g, ax) for g, ax in zip((gw_ff1, gw_ff2), BIG_AXES[4:])]
    dh2, recv_core_a = _mm(du, w_ff1, 'nt', tm=1024, tn=1024, tk=2048, name="dh2", comm=_ExchangeCore(fulls_a))
    parts_a = _reduce_sums(fulls_a, recv_core_a, core, "a")
    flight_a, token_a = _chip_exchange_start(parts_a, "rs_a_start")
    dx1, dsc2, dsh2, g_norm2, dmixo, dg1 = _norm_mod_bwd(x1, norm2_g, sc2, dh2, dx2, "norm2_bwd", gate=(mixo, g1))

    gw_o = _mm(merged, dmixo, 'tn', tm=D, tn=512, tk=S, name="gw_o", out_dtype=BF16, after=token_a)
    d_ret_out, d_att_out, dga, dgb = _dmerged_split(dmixo, w_o, proj, ret_out, att_out)
    gw_ret_out = _mm(gated, d_ret_out, 'tn', tm=512, tn=D, tk=S, name="gw_ret_out", out_dtype=BF16)
    gw_att_out = _mm(att, d_att_out, 'tn', tm=AW, tn=D, tk=S, name="gw_att_out", out_dtype=BF16)
    fulls_b = [_to_slots(g, ax) for g, ax in zip((gw_ret_out, gw_att_out, gw_o), BIG_AXES[1:4])]
    dgated, recv_core_b = _mm(d_ret_out, w_ret_out, 'nt', tm=S, tn=512, tk=D, name="dgated",
                              comm=_ExchangeCore(fulls_b))
    parts_b = _reduce_sums(fulls_b, recv_core_b, core, "b")
    flight_b, token_b = _chip_exchange_start(parts_b, "rs_b_start")
    datt = _mm(d_att_out, w_att_out, 'nt', tm=S, tn=AW, tk=D, name="datt", after=token_b)
    mix_grads = _mix_bwd(outs, lses, datt)
    datt_parts, ds_sums = [], []
    for gi in range(len(ATT_GROUPS)):
        dq, dk, dv, ds_sum = _att_bwd(slabs, bias, outs[gi], lses[gi], mix_grads[gi], mix_grads[3 + gi], gi)
        datt_parts += [dq.reshape(S, AW), dk.reshape(S, AW), dv.reshape(S, AW)]
        ds_sums.append(ds_sum)
    g_bias = _bias_grad(jnp.concatenate(ds_sums, axis=0), buckets)[:, :, 0].T.reshape(1, -1)
    dproj, g_gn_g, g_gn_b = _ret_bwd(proj, tables, gn_g, gn_b, ro, states, dgated, datt_parts + [dga, dgb])
    parts_a, recv_chip_a = _chip_exchange_wait(flight_a, dproj, "rs_a_wait")
    parts_b, recv_chip_b = _chip_exchange_wait(flight_b, dproj, "rs_b_wait")
    reduced = list(zip(parts_b + parts_a, recv_chip_b + recv_chip_a))
    full_in = _to_slots(_mm(dproj, h1, 'tn', tm=512, tn=D, tk=S, name="gw_in", out_dtype=BF16), 0)
    in_flight, token = [], None
    for half in range(2):
        (recv_core_in,) = _run_comm(_ExchangeCore([full_in], cols=(half * (D // 2), D // 2)), f"rs_core_in{half}",
                                    after=token)
        part_in = [_pair_sum(full_in, recv_core_in, core, f"rs_pair_c{half}", col_block=half)]
        flight, token = _chip_exchange_start(part_in, f"rs_in{half}_start")
        in_flight.append(flight)
    dh1 = _mm(dproj, w_in_t, 'nn', tm=1024, tn=1024, tk=2560, name="dh1", after=token)
    gx, dsc1, dsh1, g_norm1 = _norm_mod_bwd(x, norm1_g, sc1, dh1, dx1, "norm1_bwd")

    dmod = [dsh1, dsc1, dg1, dsh2, dsc2, dg2]
    small_g = [g_norm1, g_bias, g_gn_g, g_gn_b, g_norm2, g_normf]
    return loss, gx, in_flight, reduced, small_g, dmod


def _to_slots(g, axis):
    if axis == 0:
        return g.reshape(4, 2, g.shape[0] // N_DEV, g.shape[1])
    return g.reshape(g.shape[0], N_DEV, g.shape[1] // N_DEV).transpose(1, 0, 2).reshape(4, 2, g.shape[0], -1)


def _from_slots(w8, axis):
    if axis == 0:
        return w8.reshape(-1, w8.shape[2])
    return w8.transpose(1, 0, 2).reshape(w8.shape[1], -1)


BIG_AXES = (1, 0, 1, 0, 1, 0)


def kernel(x, c, w_ada, b_ada, norm1_g, w_in, rel_bias, ret_gn_g, ret_gn_b, w_ret_out, w_att_out, w_o, norm2_g, w_ff1, w_ff2, norm_f_g, loss_target, m_w_ada, m_b_ada, m_norm1_g, m_w_in, m_rel_bias, m_ret_gn_g, m_ret_gn_b, m_w_ret_out, m_w_att_out, m_w_o, m_norm2_g, m_w_ff1, m_w_ff2, m_norm_f_g, v_w_ada, v_b_ada, v_norm1_g, v_w_in, v_rel_bias, v_ret_gn_g, v_ret_gn_b, v_w_ret_out, v_w_att_out, v_w_o, v_norm2_g, v_w_ff1, v_w_ff2, v_norm_f_g):
    mx, my, mc = _mesh_pos()
    dev = 4 * mx + 2 * my + mc
    chip = jnp.reshape(2 * mx + my, (1,)).astype(jnp.int32)
    core = jnp.reshape(mc, (1,)).astype(jnp.int32)
    ada_w = D * 6 // N_DEV

    w_in, m_w_in, v_w_in = (jnp.transpose(t, (0, 2, 1)) for t in (w_in, m_w_in, v_w_in))

    shards = [w[0].astype(BF16) for w in (w_in, w_ret_out, w_att_out, w_o, w_ff1, w_ff2)]
    (c_all,) = _run_comm(_Gather([c]), "gather_c")
    c_all = c_all.reshape(N_DEV, D)
    b_sl = lax.dynamic_slice(b_ada, (0, dev * ada_w), (1, ada_w))
    (mod_all,) = _run_comm(_Gather([_ada_fwd(c_all, w_ada[0], b_sl)]), "gather_mod")
    mod = lax.dynamic_index_in_dim(mod_all, dev, axis=1, keepdims=False).reshape(6, D)
    mods = tuple(mod[i:i + 1] for i in range(6))

    small = (norm1_g, rel_bias, ret_gn_g, ret_gn_b, norm2_g, norm_f_g.reshape(1, D))
    order = lax.dynamic_index_in_dim(jnp.asarray(_proj_order()), 2 * mx + my, axis=0, keepdims=False)
    loss, gx, in_flight, big_red, small_g, dmod = _local_step(x[0], loss_target[0], mods, shards[0], order,
                                                              shards[1:], small, chip, core)

    names = ['w_ada', 'b_ada', 'norm1_g', 'w_in', 'rel_bias', 'ret_gn_g', 'ret_gn_b', 'w_ret_out', 'w_att_out',
             'w_o', 'norm2_g', 'w_ff1', 'w_ff2', 'norm_f_g']
    ws = dict(zip(names, (w_ada, b_ada, norm1_g, w_in, rel_bias, ret_gn_g, ret_gn_b, w_ret_out, w_att_out, w_o,
                          norm2_g, w_ff1, w_ff2, norm_f_g)))
    ms = dict(zip(names, (m_w_ada, m_b_ada, m_norm1_g, m_w_in, m_rel_bias, m_ret_gn_g, m_ret_gn_b, m_w_ret_out,
                          m_w_att_out, m_w_o, m_norm2_g, m_w_ff1, m_w_ff2, m_norm_f_g)))
    vs = dict(zip(names, (v_w_ada, v_b_ada, v_norm1_g, v_w_in, v_rel_bias, v_ret_gn_g, v_ret_gn_b, v_w_ret_out,
                          v_w_att_out, v_w_o, v_norm2_g, v_w_ff1, v_w_ff2, v_norm_f_g)))
    grads, delta, new_m, new_v = {}, {}, {}, {}
    big_names = ('w_ret_out', 'w_att_out', 'w_o', 'w_ff1', 'w_ff2')
    for n, (part, recv) in zip(big_names, big_red):
        grads[n], delta[n], new_m[n], new_v[n] = _adamw_reduced1(ws[n], ms[n], vs[n], part, recv, chip, "adamw_" + n)
    updated = lax.optimization_barrier((gx, tuple(delta[n] for n in big_names)))
    gathered = _run_comm(_Gather(dmod + small_g + [loss]), "gather_small", after=updated[0])
    g_b_ada, dmod_all, (g_norm1, g_bias, g_gn_g, g_gn_b, g_norm2, g_normf, loss_sum) = _sum_small(gathered)
    loss_out = loss_sum[0, 0]
    g_w_ada = _ada_bwd(c_all, lax.dynamic_slice(dmod_all, (0, dev * ada_w), (N_DEV, ada_w)))

    grads.update(w_ada=g_w_ada.reshape(w_ada.shape), b_ada=g_b_ada, norm1_g=g_norm1, rel_bias=g_bias,
                 ret_gn_g=g_gn_g, ret_gn_b=g_gn_b, norm2_g=g_norm2, norm_f_g=g_normf)
    delta['w_ada'], new_m['w_ada'], new_v['w_ada'] = _adamw(w_ada, g_w_ada, m_w_ada, v_w_ada, "adamw_w_ada")
    small_names = ('b_ada', 'norm1_g', 'rel_bias', 'ret_gn_g', 'ret_gn_b', 'norm2_g', 'norm_f_g')
    two_d = {n: (1, ws[n].size) if ws[n].ndim == 1 else ws[n].shape for n in small_names}
    d_, m_, v_ = _adamw_small(*[[src[n].reshape(two_d[n]) for n in small_names] for src in (ws, grads, ms, vs)])
    for i, n in enumerate(small_names):
        shp = ws[n].shape
        delta[n], new_m[n], new_v[n] = d_[i].reshape(shp), m_[i].reshape(shp), v_[i].reshape(shp)
        grads[n] = grads[n].reshape(shp)

    done = lax.optimization_barrier((gx, tuple(d_), tuple(delta[n] for n in ('w_ada', 'w_ret_out', 'w_att_out', 'w_o',
                                                                               'w_ff1', 'w_ff2'))))
    parts_in, recvs_in = [], []
    for half, flight in enumerate(in_flight):
        (part_in,), (recv_chip_in,) = _chip_exchange_wait(flight, done[0], f"rs_in{half}_wait")
        parts_in.append(part_in)
        recvs_in.append(recv_chip_in)
    grads['w_in'], delta['w_in'], new_m['w_in'], new_v['w_in'] = _adamw_reduced(w_in, m_w_in, v_w_in, parts_in,
                                                                               recvs_in, chip)
    for d in (grads, delta, new_m, new_v):
        d['w_in'] = jnp.transpose(d['w_in'], (0, 2, 1))
    return (loss_out, gx[None], *[grads[n] for n in names], *[delta[n] for n in names],
            *[new_m[n] for n in names], *[new_v[n] for n in names])
```

```python
import functools
import math

import numpy as np
import jax
import jax.numpy as jnp
from jax import lax
from jax.experimental import pallas as pl
from jax.experimental.pallas import tpu as pltpu

F32 = jnp.float32
BF16 = jnp.bfloat16
MESH = pl.DeviceIdType.MESH

N_DEV = 8
S = 2048
D = 1024
RET_HEADS = 4
RET_DK = 256
RET_DV = 512
CHUNK = 128
N_CHUNK = S // CHUNK
ATT_GROUPS = ((128, 1), (512, 4), (2048, 16))
ATT_HG = 4
ATT_DH = 128
ATT_BLK = 128
N_BUCKETS = 32
MAX_DIST = 2048
D_FF = 4096
IN_COLS = 12800
OFF_RQ, OFF_RK, OFF_RV, OFF_RG, OFF_ATT = 0, 1024, 2048, 4096, 6144
OFF_GA, OFF_GB = 6144, 7168
RMS_EPS = 1e-6
GN_EPS = 1e-5
ADAM_LR, ADAM_B1, ADAM_B2, ADAM_EPS, ADAM_WD, ADAM_STEP = 0.001, 0.9, 0.999, 1e-08, 0.01, 10
VMEM_LIMIT = 48 * 1024 * 1024


def _pcall(body, **kw):
    return pl.pallas_call(body, **kw)


def _params(sem=None):
    return pltpu.CompilerParams(dimension_semantics=sem, vmem_limit_bytes=VMEM_LIMIT)


HBM_SPEC = pl.BlockSpec(memory_space=pl.ANY)


def _carry(body, comm, *, name, grid, in_specs, out_specs, out_shape, scratch_shapes=()):
    single = not isinstance(out_specs, (tuple, list))
    o_specs = (out_specs,) if single else tuple(out_specs)
    o_shape = (out_shape,) if single else tuple(out_shape)
    n_in, n_out, n_scr = len(in_specs), len(o_specs), len(scratch_shapes)
    nci, nco = len(comm.ins), len(comm.out_shape)
    total = int(np.prod(grid))

    def wrapped(*refs):
        bounds = np.cumsum([0, n_in, nci, n_out, nco, n_scr])
        a, ci, o, co, scr = (refs[bounds[i]:bounds[i + 1]] for i in range(5))
        sems = refs[bounds[5]:]
        flat = 0
        for d, g in enumerate(grid):
            flat = flat * g + pl.program_id(d)

        @pl.when(flat == 0)
        def _():
            comm.start(ci, co, sems)

        body(*a, *o, *scr)

        @pl.when(flat == total - 1)
        def _():
            comm.finish(ci, co, sems)

    aliases = {n_in + i: n_out + o for i, o in getattr(comm, "aliases", {}).items()}
    call = _pcall(wrapped, name=name, grid=grid, in_specs=list(in_specs) + [HBM_SPEC] * nci,
                  out_specs=o_specs + (HBM_SPEC,) * nco, out_shape=o_shape + tuple(comm.out_shape),
                  scratch_shapes=list(scratch_shapes) + list(comm.sems), input_output_aliases=aliases,
                  compiler_params=_params(("arbitrary",) * len(grid)))

    def run(*args):
        res = call(*args, *comm.ins)
        own = res[0] if single else tuple(res[:n_out])
        return own, tuple(res[n_out:])

    return run


def _run_comm(comm, name, after=None):
    nci, nco = len(comm.ins), len(comm.out_shape)
    extra = [] if after is None else [after]

    def body(*refs):
        ci, co, sems = refs[:nci], refs[nci + len(extra):nci + len(extra) + nco], refs[nci + len(extra) + nco:]
        comm.start(ci, co, sems)
        comm.finish(ci, co, sems)

    return _pcall(body, name=name, in_specs=[HBM_SPEC] * (nci + len(extra)), out_specs=(HBM_SPEC,) * nco,
                  out_shape=tuple(comm.out_shape), scratch_shapes=list(comm.sems))(*comm.ins, *extra)


def _dot(a, b, dn):
    return lax.dot_general(a.astype(BF16), b.astype(BF16), (dn, ((), ())), preferred_element_type=F32)


NN = ((1,), (0,))
NT = ((1,), (1,))
TN = ((0,), (0,))


def _mm(a, b, mode, *, tm, tn, tk, name, out_dtype=F32, res=None, gvec=None, relu2=False, relu2_of=None, comm=None,
        after=None):
    if mode == 'nn':
        (M, K), (_, N) = a.shape, b.shape
        a_spec = pl.BlockSpec((tm, tk), lambda i, j, k: (i, k))
        b_spec = pl.BlockSpec((tk, tn), lambda i, j, k: (k, j))
        dn = NN
    elif mode == 'nt':
        (M, K), (N, _) = a.shape, b.shape
        a_spec = pl.BlockSpec((tm, tk), lambda i, j, k: (i, k))
        b_spec = pl.BlockSpec((tn, tk), lambda i, j, k: (j, k))
        dn = NT
    else:
        (K, M), (_, N) = a.shape, b.shape
        a_spec = pl.BlockSpec((tk, tm), lambda i, j, k: (k, i))
        b_spec = pl.BlockSpec((tk, tn), lambda i, j, k: (k, j))
        dn = TN
    assert M % tm == 0 and N % tn == 0 and K % tk == 0, (name, M, N, K)
    nk = K // tk
    fused = res is not None
    o_spec = pl.BlockSpec((tm, tn), lambda i, j, k: (i, j))

    def body(a_ref, b_ref, *rest):
        acc_ref = rest[-1] if nk > 1 else None
        if after is not None:
            rest = rest[1:]
        if fused:
            res_ref, g_ref, o_ref, x_ref = rest[:4]
        elif relu2_of is not None:
            u_ref, o_ref = rest[:2]
        elif relu2:
            o_ref, act_ref = rest[:2]
        else:
            o_ref = rest[0]

        def finish(acc):
            if relu2_of is not None:
                acc = acc * (2.0 * jnp.maximum(u_ref[...], 0.0))
            o_ref[...] = acc.astype(o_ref.dtype)
            if fused:
                x_ref[...] = res_ref[...] + g_ref[...] * acc
            if relu2:
                r = jnp.maximum(acc, 0.0)
                act_ref[...] = (r * r).astype(BF16)

        p = _dot(a_ref[...], b_ref[...], dn)
        if nk == 1:
            finish(p)
        else:
            k = pl.program_id(2)

            @pl.when(k == 0)
            def _():
                acc_ref[...] = p

            @pl.when(k > 0)
            def _():
                acc_ref[...] += p

            @pl.when(k == nk - 1)
            def _():
                finish(acc_ref[...])

    in_specs = [a_spec, b_spec]
    args = [a, b]
    if after is not None:
        in_specs.append(pl.BlockSpec(memory_space=pl.ANY))
        args.append(after)
    out_shape = jax.ShapeDtypeStruct((M, N), out_dtype)
    out_specs = o_spec
    if fused:
        in_specs += [pl.BlockSpec((tm, tn), lambda i, j, k: (i, j)), pl.BlockSpec((1, tn), lambda i, j, k: (0, j))]
        args += [res, gvec]
        out_shape = (out_shape, jax.ShapeDtypeStruct((M, N), F32))
        out_specs = (o_spec, pl.BlockSpec((tm, tn), lambda i, j, k: (i, j)))
    elif relu2_of is not None:
        in_specs.append(pl.BlockSpec((tm, tn), lambda i, j, k: (i, j)))
        args.append(relu2_of)
    elif relu2:
        out_shape = (out_shape, jax.ShapeDtypeStruct((M, N), BF16))
        out_specs = (o_spec, pl.BlockSpec((tm, tn), lambda i, j, k: (i, j)))
    kw = dict(name=name, grid=(M // tm, N // tn, nk), in_specs=in_specs, out_specs=out_specs,
              out_shape=out_shape, scratch_shapes=[pltpu.VMEM((tm, tn), F32)] if nk > 1 else [])
    if comm is not None:
        return _carry(body, comm, **kw)(*args)
    return _pcall(body, compiler_params=_params(("parallel", "parallel", "arbitrary")), **kw)(*args)


PROJ_TN = 512
ATT_T0, ATT_T1 = 6144 // PROJ_TN, 10752 // PROJ_TN
N_SLABS = (ATT_T1 - ATT_T0) * 4
MAIN_COLS = IN_COLS - (ATT_T1 - ATT_T0) * PROJ_TN


PROJ_TILES = IN_COLS // PROJ_TN
SHARD_ROWS = IN_COLS // N_DEV
W_CHUNKS = 4
N_OWN, N_NEAR = 5, 18


def _proj_order():
    out = np.zeros((4, 3, PROJ_TILES), np.int32)
    for q in range(4):
        def hops(t):
            owners = {col // (2 * SHARD_ROWS) for col in (t * PROJ_TN, (t + 1) * PROJ_TN - 1)}
            return max(bin(q ^ p).count("1") for p in owners)
        order = sorted(range(PROJ_TILES), key=lambda t: (hops(t), t))
        assert all(hops(t) == 0 for t in order[:N_OWN]) and all(hops(t) < 2 for t in order[:N_NEAR])
        is_att = [ATT_T0 <= t < ATT_T1 for t in order]
        for row, kind, index in ((1, False, lambda t: t if t < ATT_T0 else t - (ATT_T1 - ATT_T0)),
                                 (2, True, lambda t: t - ATT_T0)):
            own = [index(t) if a == kind else None for t, a in zip(order, is_att)]
            first = next(v for v in own if v is not None)
            last = first
            for j, v in enumerate(own):
                last = last if v is None else v
                out[q, row, j] = last
        out[q, 0] = order
    return out


def _gather_proj(h1, shard, order):
    rows = SHARD_ROWS // W_CHUNKS

    def body(ord_ref, a_ref, sh_ref, main_ref, slab_ref, full_ref, wbuf, fetch_sems, send_sems, recv_sems,
             local_sems):
        j = pl.program_id(0)
        x, y, c = _mesh_pos()
        me, sibling = (x, y, c), (x, y, 1 - c)
        chips = [(1 - x, y), (x, 1 - y), (1 - x, 1 - y)]

        def block(p, owner):
            return full_ref.at[pl.ds(pl.multiple_of(_slot(owner) * SHARD_ROWS + p * rows, 16), rows)]

        def copy(p, k, owner, to, from_input=False):
            dst = block(p, owner)
            return pltpu.make_async_remote_copy(
                src_ref=sh_ref.at[pl.ds(p * rows, rows)] if from_input else dst, dst_ref=dst,
                send_sem=send_sems.at[7 * p + k], recv_sem=recv_sems.at[7 * p + k], device_id=to, device_id_type=MESH)

        pieces = range(W_CHUNKS)
        mine = [pltpu.make_async_copy(sh_ref.at[pl.ds(p * rows, rows)], block(p, me), local_sems.at[p]) for p in pieces]
        first = [copy(p, 0, me, sibling, from_input=True) for p in pieces]
        first += [copy(p, 1 + n, me, (*chips[n], c), from_input=True) for p in pieces for n in range(2)]
        near_pass = [copy(p, 4 + n, (*chips[n], c), sibling) for p in pieces for n in range(2)]
        relay = [copy(p, 3, ((x + 1 - c) % 2, (y + c) % 2, c), ((x + c) % 2, (y + 1 - c) % 2, c)) for p in pieces]
        far_pass = [copy(p, 6, (*chips[2], c), sibling) for p in pieces]

        def fetch(pos):
            slot = lax.rem(pos, 2)
            start = pl.multiple_of(ord_ref[0, pos] * PROJ_TN, PROJ_TN)
            return pltpu.make_async_copy(full_ref.at[pl.ds(start, PROJ_TN)], wbuf.at[slot], fetch_sems.at[slot])

        @pl.when(j == 0)
        def _():
            for cp in mine + first:
                cp.start()
            for cp in mine:
                cp.wait()
            for p in pieces:
                copy(p, 0, sibling, me).wait_recv()
            fetch(j).start()

        @pl.when(j == N_OWN - 1)
        def _():
            for p in pieces:
                for n in range(2):
                    copy(p, 1 + n, (*chips[n], c), me).wait_recv()
                    near_pass[2 * p + n].start()
                relay[p].start()
            for p in pieces:
                for n in range(2):
                    copy(p, 4 + n, (*chips[n], 1 - c), me).wait_recv()

        @pl.when(j == N_NEAR - 1)
        def _():
            for p in pieces:
                copy(p, 3, (*chips[2], c), me).wait_recv()
                far_pass[p].start()
            for p in pieces:
                copy(p, 6, (*chips[2], 1 - c), me).wait_recv()

        @pl.when(j + 1 < PROJ_TILES)
        def _():
            fetch(j + 1).start()

        fetch(j).wait()
        w_ref = wbuf.at[lax.rem(j, 2)]
        tile = ord_ref[0, j]
        is_att = (tile >= ATT_T0) & (tile < ATT_T1)
        chunks = [pl.ds(r * 512, 512) for r in range(S // 512)]

        @pl.when(jnp.logical_not(is_att))
        def _():
            for rws in chunks:
                main_ref[rws, :] = _dot(a_ref[rws, :], w_ref[...], NT)

        @pl.when(is_att)
        def _():
            for rws in chunks:
                p = _dot(a_ref[rws, :], w_ref[...], NT)
                for h in range(4):
                    slab_ref[h, rws, :] = p[:, h * 128:(h + 1) * 128]

        @pl.when(j == PROJ_TILES - 1)
        def _():
            for cp in first + near_pass + relay + far_pass:
                cp.wait_send()

    gs = pltpu.PrefetchScalarGridSpec(
        num_scalar_prefetch=1, grid=(PROJ_TILES,),
        in_specs=[pl.BlockSpec((S, D), lambda j, o: (0, 0)), HBM_SPEC],
        out_specs=(pl.BlockSpec((S, PROJ_TN), lambda j, o: (0, o[1, j])),
                   pl.BlockSpec((4, S, 128), lambda j, o: (o[2, j], 0, 0)), HBM_SPEC),
        scratch_shapes=[pltpu.VMEM((2, PROJ_TN, D), BF16), pltpu.SemaphoreType.DMA((2,)),
                        pltpu.SemaphoreType.DMA((7 * W_CHUNKS,)), pltpu.SemaphoreType.DMA((7 * W_CHUNKS,)),
                        pltpu.SemaphoreType.DMA((W_CHUNKS,))])
    return _pcall(body, name="gather_proj", grid_spec=gs,
                  out_shape=(jax.ShapeDtypeStruct((S, MAIN_COLS), F32), jax.ShapeDtypeStruct((N_SLABS, S, 128), F32),
                             jax.ShapeDtypeStruct((IN_COLS, D), BF16)),
                  compiler_params=_params(("arbitrary",)))(order, h1, shard)


TR = 256


def _row_spec(w=D):
    return pl.BlockSpec((TR, w), lambda i: (i, 0))


def _vec_spec(w=D):
    return pl.BlockSpec((1, w), lambda i: (0, 0))


def _norm_mod_fwd(x, g, sh, sc, name):
    def body(x_ref, g_ref, sh_ref, sc_ref, o_ref):
        xv = x_ref[...]
        rstd = lax.rsqrt(jnp.mean(xv * xv, axis=-1, keepdims=True) + RMS_EPS)
        n = xv * rstd * g_ref[...]
        o_ref[...] = (n * (1.0 + sc_ref[...]) + sh_ref[...]).astype(BF16)

    return _pcall(body, name=name, grid=(S // TR,), in_specs=[_row_spec(), _vec_spec(), _vec_spec(), _vec_spec()],
                  out_specs=_row_spec(), out_shape=jax.ShapeDtypeStruct((S, D), BF16),
                  compiler_params=_params(("parallel",)))(x, g, sh, sc)


def _norm_mod_bwd(x, g, sc, dh, dres, name, gate=None, after=None):
    gated = gate is not None

    def body(x_ref, g_ref, sc_ref, dh_ref, dres_ref, *rest):
        if after is not None:
            rest = rest[1:]
        if gated:
            f_ref, gv_ref, dx_ref, dsc_ref, dsh_ref, dg_ref, dz_ref, dgv_ref = rest
        else:
            dx_ref, dsc_ref, dsh_ref, dg_ref = rest
        i = pl.program_id(0)
        xv = x_ref[...]
        dh = dh_ref[...]
        rstd = lax.rsqrt(jnp.mean(xv * xv, axis=-1, keepdims=True) + RMS_EPS)
        xhat = xv * rstd
        gv = g_ref[...]
        dn = dh * (1.0 + sc_ref[...])
        dxhat = dn * gv
        dx = dres_ref[...] + rstd * (dxhat - xhat * jnp.mean(dxhat * xhat, axis=-1, keepdims=True))
        dx_ref[...] = dx
        sums = [(dsc_ref, jnp.sum(dh * (xhat * gv), axis=0, keepdims=True)),
                (dsh_ref, jnp.sum(dh, axis=0, keepdims=True)),
                (dg_ref, jnp.sum(dn * xhat, axis=0, keepdims=True))]
        if gated:
            dz_ref[...] = (dx * gv_ref[...]).astype(BF16)
            sums.append((dgv_ref, jnp.sum(dx * f_ref[...], axis=0, keepdims=True)))

        @pl.when(i == 0)
        def _():
            for ref, p in sums:
                ref[...] = p

        @pl.when(i > 0)
        def _():
            for ref, p in sums:
                ref[...] += p

    vec = jax.ShapeDtypeStruct((1, D), F32)
    in_specs = [_row_spec(), _vec_spec(), _vec_spec(), _row_spec(), _row_spec()]
    out_specs = [_row_spec(), _vec_spec(), _vec_spec(), _vec_spec()]
    out_shape = [jax.ShapeDtypeStruct((S, D), F32), vec, vec, vec]
    args = [x, g, sc, dh, dres]
    if after is not None:
        in_specs.append(HBM_SPEC)
        args.append(after)
    if gated:
        in_specs += [_row_spec(), _vec_spec()]
        out_specs += [_row_spec(), _vec_spec()]
        out_shape += [jax.ShapeDtypeStruct((S, D), BF16), vec]
        args += list(gate)
    return _pcall(body, name=name, grid=(S // TR,), in_specs=in_specs, out_specs=tuple(out_specs),
                  out_shape=tuple(out_shape), compiler_params=_params(("arbitrary",)))(*args)


def _w_o_norm2(merged, w_o, x, g1, g, sh, sc):
    def body(a_ref, b_ref, x_ref, g1_ref, g_ref, sh_ref, sc_ref, o_ref, x1_ref, h_ref):
        acc = _dot(a_ref[...], b_ref[...], NN)
        o_ref[...] = acc
        xv = x_ref[...] + g1_ref[...] * acc
        x1_ref[...] = xv
        rstd = lax.rsqrt(jnp.mean(xv * xv, axis=-1, keepdims=True) + RMS_EPS)
        h_ref[...] = (xv * rstd * g_ref[...] * (1.0 + sc_ref[...]) + sh_ref[...]).astype(BF16)

    rows = pl.BlockSpec((FF2_TM, D), lambda i: (i, 0))
    f32 = jax.ShapeDtypeStruct((S, D), F32)
    return _pcall(body, name="w_o_norm2", grid=(S // FF2_TM,),
                  in_specs=[rows, pl.BlockSpec((D, D), lambda i: (0, 0)), rows] + [_vec_spec()] * 4,
                  out_specs=(rows, rows, rows), out_shape=(f32, f32, jax.ShapeDtypeStruct((S, D), BF16)),
                  compiler_params=_params(("parallel",)))(merged, w_o, x, g1, g, sh, sc)


FF2_TM = 512


def _ff2_final(act, w_ff2, x1, g2, tgt, g):
    def body(a_ref, b_ref, x1_ref, g2_ref, t_ref, g_ref, loss_ref, dx_ref, dg_ref, df_ref, dg2_ref):
        i = pl.program_id(0)
        f = _dot(a_ref[...], b_ref[...], NN)
        g2v = g2_ref[...]
        xv = x1_ref[...] + g2v * f
        gv = g_ref[...]
        rstd = lax.rsqrt(jnp.mean(xv * xv, axis=-1, keepdims=True) + RMS_EPS)
        xhat = xv * rstd
        err = xhat * gv - t_ref[...]
        dy = err * (1.0 / D)
        dxhat = dy * gv
        dx = rstd * (dxhat - xhat * jnp.mean(dxhat * xhat, axis=-1, keepdims=True))
        dx_ref[...] = dx
        df_ref[...] = (dx * g2v).astype(BF16)
        p_g = jnp.sum(dy * xhat, axis=0, keepdims=True)
        p_g2 = jnp.sum(dx * f, axis=0, keepdims=True)
        p_l = jnp.zeros((1, 128), F32) + 0.5 * jnp.sum(jnp.mean(err * err, axis=-1, keepdims=True))

        @pl.when(i == 0)
        def _():
            dg_ref[...] = p_g
            dg2_ref[...] = p_g2
            loss_ref[...] = p_l

        @pl.when(i > 0)
        def _():
            dg_ref[...] += p_g
            dg2_ref[...] += p_g2
            loss_ref[...] += p_l

    vec = jax.ShapeDtypeStruct((1, D), F32)
    rows = lambda w: pl.BlockSpec((FF2_TM, w), lambda i: (i, 0))
    return _pcall(body, name="ff2_final", grid=(S // FF2_TM,),
                  in_specs=[rows(D_FF), pl.BlockSpec((D_FF, D), lambda i: (0, 0)), rows(D), _vec_spec(), rows(D),
                            _vec_spec()],
                  out_specs=(_vec_spec(128), rows(D), _vec_spec(), rows(D), _vec_spec()),
                  out_shape=(jax.ShapeDtypeStruct((1, 128), F32), jax.ShapeDtypeStruct((S, D), F32), vec,
                             jax.ShapeDtypeStruct((S, D), BF16), vec),
                  compiler_params=_params(("arbitrary",)))(act, w_ff2, x1, g2, tgt, g)


HALF = 512


MERGE_TM = 1024


def _merge_specs():
    blk = lambda off: pl.BlockSpec((MERGE_TM, HALF), lambda i, j: (i, off // HALF + j))
    return blk(OFF_GA), blk(OFF_GB), blk(0)


def _att_out_merge(att, w_att_out, proj, ret_out):
    def body(a_ref, b_ref, ga_ref, gb_ref, r_ref, o_ref, m_ref):
        acc = _dot(a_ref[...], b_ref[...], NN)
        o_ref[...] = acc
        m_ref[...] = (jax.nn.sigmoid(ga_ref[...]) * r_ref[...] + jax.nn.sigmoid(gb_ref[...]) * acc).astype(BF16)

    ga, gb, tile = _merge_specs()
    return _pcall(body, name="att_out", grid=(S // MERGE_TM, D // HALF),
                  in_specs=[pl.BlockSpec((MERGE_TM, AW), lambda i, j: (i, 0)), pl.BlockSpec((AW, HALF), lambda i, j: (0, j)),
                            ga, gb, tile],
                  out_specs=(tile, tile),
                  out_shape=(jax.ShapeDtypeStruct((S, D), F32), jax.ShapeDtypeStruct((S, D), BF16)),
                  compiler_params=_params(("parallel", "parallel")))(att, w_att_out, proj, proj, ret_out)


def _dmerged_split(dmixo, w_o, proj, ret_out, att_out):
    def body(a_ref, b_ref, ga_ref, gb_ref, r_ref, at_ref, dr_ref, da_ref, dga_ref, dgb_ref):
        dm = _dot(a_ref[...], b_ref[...], NT)
        sa = jax.nn.sigmoid(ga_ref[...])
        sb = jax.nn.sigmoid(gb_ref[...])
        dr_ref[...] = (dm * sa).astype(BF16)
        da_ref[...] = (dm * sb).astype(BF16)
        dga_ref[...] = (dm * r_ref[...] * (sa * (1.0 - sa))).astype(BF16)
        dgb_ref[...] = (dm * at_ref[...] * (sb * (1.0 - sb))).astype(BF16)

    ga, gb, tile = _merge_specs()
    o = jax.ShapeDtypeStruct((S, D), BF16)
    return _pcall(body, name="dmerged", grid=(S // MERGE_TM, D // HALF),
                  in_specs=[pl.BlockSpec((MERGE_TM, D), lambda i, j: (i, 0)), pl.BlockSpec((HALF, D), lambda i, j: (j, 0)),
                            ga, gb, tile, tile],
                  out_specs=(tile,) * 4, out_shape=(o, o, o, o),
                  compiler_params=_params(("parallel", "parallel")))(dmixo, w_o, proj, proj, ret_out, att_out)


def _ret_tables():
    H, C = RET_HEADS, CHUNK
    log_g = jnp.log1p(-(2.0 ** (-5.0 - jnp.arange(H, dtype=F32))))
    idx = jnp.arange(C, dtype=F32)
    rel = idx[:, None] - idx[None, :]
    inner = jnp.where(rel >= 0, jnp.exp(log_g[:, None, None] * jnp.maximum(rel, 0.0)), 0.0)
    qd = jnp.exp(log_g[:, None] * (idx + 1.0))[:, :, None]
    kd = jnp.exp(log_g[:, None] * (C - 1.0 - idx))[:, :, None]
    cd = jnp.broadcast_to(jnp.exp(log_g * C)[:, None, None], (H, 1, 128))
    half = RET_DK // 2
    inv = 10000.0 ** (-jnp.arange(half, dtype=F32) / half)
    ang = jnp.arange(S, dtype=F32)[:, None] * inv[None, :]
    return inner, qd, kd, cd, jnp.cos(ang), jnp.sin(ang)


def _rot(x, cos, sin):
    x1, x2 = x[:, :128], x[:, 128:]
    return jnp.concatenate([x1 * cos - x2 * sin, x1 * sin + x2 * cos], axis=1)


def _rot_t(d, cos, sin):
    d1, d2 = d[:, :128], d[:, 128:]
    return jnp.concatenate([d1 * cos + d2 * sin, d2 * cos - d1 * sin], axis=1)


RET_COLS = OFF_ATT
RET_VW = RET_HEADS * RET_DV


def _ret_specs(chunk_of):
    ci = chunk_of
    whole = lambda shape: pl.BlockSpec(shape, lambda t: (0,) * len(shape))
    return [
        pl.BlockSpec((CHUNK, RET_COLS), lambda t: (ci(t), 0)),
        pl.BlockSpec((CHUNK, 128), lambda t: (ci(t), 0)),
        pl.BlockSpec((CHUNK, 128), lambda t: (ci(t), 0)),
        whole((RET_HEADS, CHUNK, CHUNK)), whole((RET_HEADS, CHUNK, 1)), whole((RET_HEADS, CHUNK, 1)),
        whole((RET_HEADS, 1, 128)), whole((1, RET_VW)), whole((1, RET_VW)),
    ]


def _ret_cols(h):
    q = slice(OFF_RQ + h * RET_DK, OFF_RQ + (h + 1) * RET_DK)
    k = slice(OFF_RK + h * RET_DK, OFF_RK + (h + 1) * RET_DK)
    v = slice(OFF_RV + h * RET_DV, OFF_RV + (h + 1) * RET_DV)
    g = slice(OFF_RG + h * RET_DV, OFF_RG + (h + 1) * RET_DV)
    return q, k, v, g, slice(h * RET_DV, (h + 1) * RET_DV)


def _ret_fwd(proj, tables, gn_g, gn_b, after):
    inner, qd, kd, cd, cos, sin = tables

    def body(x_ref, cos_ref, sin_ref, in_ref, qd_ref, kd_ref, cd_ref, g_ref, b_ref, after_ref,
             gated_ref, ro_ref, st_ref, s_scr):
        i = pl.program_id(0)

        @pl.when(i == 0)
        def _():
            s_scr[...] = jnp.zeros_like(s_scr)

        cosv, sinv = cos_ref[...], sin_ref[...]
        for h in range(RET_HEADS):
            cq, ck, cv, cg, co = _ret_cols(h)
            q = _rot(x_ref[:, cq], cosv, sinv)
            k = _rot(x_ref[:, ck], cosv, sinv) * (RET_DK ** -0.5)
            v = x_ref[:, cv]
            st = s_scr[h]
            st_ref[h] = st.astype(BF16)
            s = _dot(q, k, NT) * in_ref[h]
            o = _dot(s, v, NN) + _dot(q, st, NN) * qd_ref[h]
            s_scr[h] = st * cd_ref[h, :, :1] + _dot(k * kd_ref[h], v, TN)
            ro_ref[:, co] = o
            mu = jnp.mean(o, axis=-1, keepdims=True)
            oc = o - mu
            var = jnp.mean(oc * oc, axis=-1, keepdims=True)
            rn = oc * lax.rsqrt(var + GN_EPS) * g_ref[:, co] + b_ref[:, co]
            rg = x_ref[:, cg]
            gated_ref[:, co] = (rg * jax.nn.sigmoid(rg) * rn).astype(BF16)

    ospec = pl.BlockSpec((CHUNK, RET_VW), lambda t: (t, 0))
    return _pcall(
        body, name="ret_fwd", grid=(N_CHUNK,), in_specs=_ret_specs(lambda t: t) + [HBM_SPEC],
        out_specs=(ospec, ospec, pl.BlockSpec((RET_HEADS, None, RET_DK, RET_DV), lambda t: (0, t, 0, 0))),
        out_shape=(jax.ShapeDtypeStruct((S, RET_VW), BF16), jax.ShapeDtypeStruct((S, RET_VW), F32),
                   jax.ShapeDtypeStruct((RET_HEADS, N_CHUNK, RET_DK, RET_DV), BF16)),
        scratch_shapes=[pltpu.VMEM((RET_HEADS, RET_DK, RET_DV), F32)],
        compiler_params=_params(("arbitrary",)))(proj, cos, sin, inner, qd, kd, cd, gn_g, gn_b, after)


def _ret_bwd(proj, tables, gn_g, gn_b, ro, states, dgated, others):
    inner, qd, kd, cd, cos, sin = tables
    last = N_CHUNK - 1
    assert RET_COLS + sum(o.shape[1] for o in others) == IN_COLS

    def body(x_ref, cos_ref, sin_ref, in_ref, qd_ref, kd_ref, cd_ref, g_ref, b_ref, ro_ref, st_ref, dg_ref, *rest):
        other_refs, (dx_ref, gg_ref, gb_ref, gs_scr) = rest[:len(others)], rest[len(others):]
        t = pl.program_id(0)
        col = RET_COLS
        for o_ref in other_refs:
            dx_ref[:, col:col + o_ref.shape[1]] = o_ref[...]
            col += o_ref.shape[1]

        @pl.when(t == 0)
        def _():
            gs_scr[...] = jnp.zeros_like(gs_scr)
            gg_ref[...] = jnp.zeros_like(gg_ref)
            gb_ref[...] = jnp.zeros_like(gb_ref)

        cosv, sinv = cos_ref[...], sin_ref[...]
        for h in range(RET_HEADS):
            cq, ck, cv, cg, co = _ret_cols(h)
            q = _rot(x_ref[:, cq], cosv, sinv)
            k = _rot(x_ref[:, ck], cosv, sinv) * (RET_DK ** -0.5)
            v = x_ref[:, cv]
            qdv, kdv, dm = qd_ref[h], kd_ref[h], in_ref[h]
            st = st_ref[h]
            o = ro_ref[:, co]
            gv = g_ref[:, co]
            mu = jnp.mean(o, axis=-1, keepdims=True)
            oc = o - mu
            rstd = lax.rsqrt(jnp.mean(oc * oc, axis=-1, keepdims=True) + GN_EPS)
            ohat = oc * rstd
            rn = ohat * gv + b_ref[:, co]
            rg = x_ref[:, cg]
            sg = jax.nn.sigmoid(rg)
            dgt = dg_ref[:, co]
            drn = dgt * (rg * sg)
            dx_ref[:, cg] = (dgt * rn * (sg * (1.0 + rg * (1.0 - sg)))).astype(BF16)
            gg_ref[:, co] += jnp.sum(drn * ohat, axis=0, keepdims=True)
            gb_ref[:, co] += jnp.sum(drn, axis=0, keepdims=True)
            dohat = drn * gv
            do = rstd * (dohat - jnp.mean(dohat, axis=-1, keepdims=True)
                         - ohat * jnp.mean(dohat * ohat, axis=-1, keepdims=True))
            gs = gs_scr[h]
            s = _dot(q, k, NT) * dm
            dsr = _dot(do, v, NT) * dm
            dq = _dot(dsr, k, NN) + _dot(do, st, NT) * qdv
            dk = _dot(dsr, q, TN) + _dot(v, gs, NT) * kdv
            dv = _dot(s, do, TN) + _dot(k * kdv, gs, NN)
            gs_scr[h] = gs * cd_ref[h, :, :1] + _dot(q * qdv, do, TN)
            dx_ref[:, cq] = _rot_t(dq, cosv, sinv).astype(BF16)
            dx_ref[:, ck] = (_rot_t(dk, cosv, sinv) * (RET_DK ** -0.5)).astype(BF16)
            dx_ref[:, cv] = dv.astype(BF16)

    rev = lambda t: last - t
    vblk = pl.BlockSpec((CHUNK, RET_VW), lambda t: (rev(t), 0))
    vspec = pl.BlockSpec((1, RET_VW), lambda t: (0, 0))
    rows = lambda w: pl.BlockSpec((CHUNK, w), lambda t: (rev(t), 0))
    return _pcall(
        body, name="ret_bwd", grid=(N_CHUNK,),
        in_specs=_ret_specs(rev) + [vblk, pl.BlockSpec((RET_HEADS, None, RET_DK, RET_DV), lambda t: (0, rev(t), 0, 0)),
                                    vblk] + [rows(o.shape[1]) for o in others],
        out_specs=(rows(IN_COLS), vspec, vspec),
        out_shape=(jax.ShapeDtypeStruct((S, IN_COLS), BF16), jax.ShapeDtypeStruct((1, RET_VW), F32),
                   jax.ShapeDtypeStruct((1, RET_VW), F32)),
        scratch_shapes=[pltpu.VMEM((RET_HEADS, RET_DK, RET_DV), F32)],
        compiler_params=_params(("arbitrary",)))(proj, cos, sin, inner, qd, kd, cd, gn_g, gn_b, ro, states, dgated,
                                                 *others)


def _bucket_tables():
    qi = np.arange(ATT_BLK)[:, None]
    kj = np.arange(2 * ATT_BLK)[None, :]
    m = ATT_BLK + qi - kj
    out = []
    for win, dil in ATT_GROUPS:
        w = win // dil
        dist = (np.clip(m, 0, w) * dil).astype(np.int32)
        max_exact = N_BUCKETS // 2
        d_f = np.maximum(dist, 1).astype(np.float32)
        large = max_exact + (np.log(d_f / np.float32(max_exact)) / np.float32(math.log(MAX_DIST / max_exact))
                             * np.float32(N_BUCKETS - max_exact)).astype(np.int32)
        large = np.minimum(large, N_BUCKETS - 1)
        out.append(np.where(dist < max_exact, dist, large).astype(np.int32))
    return np.stack(out)


def _bias_build(rel_bias, buckets, after):
    def body(tab_ref, bk_ref, after_ref, o_ref):
        hh = pl.program_id(0)
        bk = bk_ref[...]
        acc = jnp.zeros((ATT_BLK, 2 * ATT_BLK), F32)
        for b in range(N_BUCKETS):
            acc = jnp.where(bk == b, tab_ref[b, hh], acc)
        o_ref[...] = acc

    nh = len(ATT_GROUPS) * ATT_HG
    return _pcall(body, name="bias_build", grid=(nh,),
                  in_specs=[pl.BlockSpec(memory_space=pltpu.SMEM),
                            pl.BlockSpec((None, ATT_BLK, 2 * ATT_BLK), lambda hh: (hh // ATT_HG, 0, 0)), HBM_SPEC],
                  out_specs=pl.BlockSpec((None, ATT_BLK, 2 * ATT_BLK), lambda hh: (hh, 0, 0)),
                  out_shape=jax.ShapeDtypeStruct((nh, ATT_BLK, 2 * ATT_BLK), F32),
                  compiler_params=_params(("parallel",)))(rel_bias, buckets, after)


def _bias_grad(ds_sum, buckets):
    def body(ds_ref, bk_ref, o_ref):
        bk = bk_ref[...]
        ds = ds_ref[...]
        rows = lax.broadcasted_iota(jnp.int32, (N_BUCKETS, 128), 0)
        acc = jnp.zeros((N_BUCKETS, 128), F32)
        for b in range(N_BUCKETS):
            acc = jnp.where(rows == b, jnp.sum(jnp.where(bk == b, ds, 0.0)), acc)
        o_ref[...] = acc

    nh = len(ATT_GROUPS) * ATT_HG
    return _pcall(body, name="bias_grad", grid=(nh,),
                  in_specs=[pl.BlockSpec((None, ATT_BLK, 2 * ATT_BLK), lambda hh: (hh, 0, 0)),
                            pl.BlockSpec((None, ATT_BLK, 2 * ATT_BLK), lambda hh: (hh // ATT_HG, 0, 0))],
                  out_specs=pl.BlockSpec((None, N_BUCKETS, 128), lambda hh: (hh, 0, 0)),
                  out_shape=jax.ShapeDtypeStruct((nh, N_BUCKETS, 128), F32),
                  compiler_params=_params(("parallel",)))(ds_sum, buckets)


def _att_valid(n):
    qi = lax.broadcasted_iota(jnp.int32, (ATT_BLK, 2 * ATT_BLK), 0)
    kj = lax.broadcasted_iota(jnp.int32, (ATT_BLK, 2 * ATT_BLK), 1)
    m = ATT_BLK + qi - kj
    first_key = jnp.where(n > 0, 0, ATT_BLK)
    return (m >= 0) & (m <= ATT_BLK) & (kj >= first_key)


ATT_HP = (1, 2, 2)


def _att_geometry(gi):
    _, dil = ATT_GROUPS[gi]
    return dil, S // dil // ATT_BLK, ATT_HP[gi]


def _blk(dil, r, n):
    if dil == 1:
        return pl.ds(n * ATT_BLK, ATT_BLK)
    return pl.ds(r + n * ATT_BLK * dil, ATT_BLK, stride=dil)


def _slab_specs(gi):
    _, _, hp = _att_geometry(gi)
    per = ATT_HG // hp
    return [pl.BlockSpec((hp, S, ATT_DH), lambda g, r, part=part: ((3 * gi + part) * per + g, 0, 0))
            for part in range(3)]


def _head_specs(gi, count):
    _, _, hp = _att_geometry(gi)
    return [pl.BlockSpec((hp, S, ATT_DH), lambda g, r: (g, 0, 0))] * count


def _bias_spec(gi):
    _, _, hp = _att_geometry(gi)
    return pl.BlockSpec((hp, ATT_BLK, 2 * ATT_BLK), lambda g, r: (gi * (ATT_HG // hp) + g, 0, 0))


def _att_valid_first():
    qi = lax.broadcasted_iota(jnp.int32, (ATT_BLK, ATT_BLK), 0)
    kj = lax.broadcasted_iota(jnp.int32, (ATT_BLK, ATT_BLK), 1)
    return kj <= qi


def _att_fwd(slabs, bias, gi, comm=None):
    dil, nb, hp = _att_geometry(gi)
    scale = ATT_DH ** -0.5

    def body(q_ref, k_ref, v_ref, bias_ref, o_ref, l_ref):
        r = pl.program_id(1)
        for n in range(nb):
            cur = _blk(dil, r, n)
            valid = _att_valid(n) if n > 0 else _att_valid_first()
            for h in range(hp):
                if n > 0:
                    prev = _blk(dil, r, n - 1)
                    kk = jnp.concatenate([k_ref[h, prev, :], k_ref[h, cur, :]], axis=0)
                    vv = jnp.concatenate([v_ref[h, prev, :], v_ref[h, cur, :]], axis=0)
                    bias = bias_ref[h]
                else:
                    kk, vv, bias = k_ref[h, cur, :], v_ref[h, cur, :], bias_ref[h, :, pl.ds(ATT_BLK, ATT_BLK)]
                s = _dot(q_ref[h, cur, :], kk, NT) * scale + bias
                s = jnp.where(valid, s, -1e30)
                mx = jnp.max(s, axis=-1, keepdims=True)
                e = jnp.exp(s - mx)
                den = jnp.sum(e, axis=-1, keepdims=True)
                o_ref[h, cur, :] = _dot(e / den, vv, NN)
                l_ref[h, cur, :] = jnp.broadcast_to(mx + jnp.log(den), (ATT_BLK, ATT_DH))

    osh = jax.ShapeDtypeStruct((ATT_HG, S, ATT_DH), F32)
    kw = dict(name=f"att_fwd{gi}", grid=(ATT_HG // hp, dil), in_specs=_slab_specs(gi) + [_bias_spec(gi)],
              out_specs=tuple(_head_specs(gi, 2)), out_shape=(osh, osh))
    if comm is not None:
        return _carry(body, comm, **kw)(slabs, slabs, slabs, bias)
    return _pcall(body, compiler_params=_params(("parallel", "arbitrary")), **kw)(slabs, slabs, slabs, bias)


def _att_bwd(slabs, bias, o, lse, do, dlse, gi, comm=None):
    dil, nb, hp = _att_geometry(gi)
    per = ATT_HG // hp
    scale = ATT_DH ** -0.5
    wh = hp * ATT_DH
    wide = lambda t: jnp.concatenate([t, t], axis=1)

    def body(q_ref, k_ref, v_ref, bias_ref, o_ref, l_ref, do_ref, dl_ref, dq_ref, dk_ref, dv_ref, ds_ref):
        r = pl.program_id(1)

        @pl.when(r == 0)
        def _():
            ds_ref[...] = jnp.zeros_like(ds_ref)

        for h in range(hp):
            sl = slice(h * ATT_DH, (h + 1) * ATT_DH)
            carry_k = carry_v = None
            for n in range(nb):
                cur = _blk(dil, r, n)
                q = q_ref[h, cur, :]
                dov = do_ref[h, cur, :]
                delta = jnp.sum(dov * o_ref[h, cur, :], axis=-1, keepdims=True)
                out_rows = pl.ds(n * ATT_BLK, ATT_BLK)
                if n == 0:
                    own = pl.ds(ATT_BLK, ATT_BLK)
                    kk, vv = k_ref[h, cur, :], v_ref[h, cur, :]
                    s = _dot(q, kk, NT) * scale + bias_ref[h, :, own]
                    p = jnp.where(_att_valid_first(), jnp.exp(s - l_ref[h, cur, :]), 0.0)
                    ds = p * (_dot(dov, vv, NT) - delta + dl_ref[h, cur, :])
                    ds_ref[h, :, own] += ds
                    dq_ref[out_rows, sl] = (_dot(ds, kk, NN) * scale).astype(BF16)
                    carry_k, carry_v = _dot(ds, q, TN) * scale, _dot(p, dov, TN)
                    continue
                prev = _blk(dil, r, n - 1)
                kk = jnp.concatenate([k_ref[h, prev, :], k_ref[h, cur, :]], axis=0)
                vv = jnp.concatenate([v_ref[h, prev, :], v_ref[h, cur, :]], axis=0)
                s = _dot(q, kk, NT) * scale + bias_ref[h]
                p = jnp.where(_att_valid(n), jnp.exp(s - wide(l_ref[h, cur, :])), 0.0)
                dp = _dot(dov, vv, NT)
                ds = p * (dp - delta + wide(dl_ref[h, cur, :]))
                ds_ref[h] += ds
                dq_ref[out_rows, sl] = (_dot(ds, kk, NN) * scale).astype(BF16)
                dkk = _dot(ds, q, TN) * scale
                dvv = _dot(p, dov, TN)
                before = pl.ds((n - 1) * ATT_BLK, ATT_BLK)
                dk_ref[before, sl] = (carry_k + dkk[:ATT_BLK]).astype(BF16)
                dv_ref[before, sl] = (carry_v + dvv[:ATT_BLK]).astype(BF16)
                carry_k, carry_v = dkk[ATT_BLK:], dvv[ATT_BLK:]
            last = pl.ds((nb - 1) * ATT_BLK, ATT_BLK)
            dk_ref[last, sl] = carry_k.astype(BF16)
            dv_ref[last, sl] = carry_v.astype(BF16)

    out_spec = pl.BlockSpec((S // dil, wh), lambda g, r: (0, r * per + g))
    osh = jax.ShapeDtypeStruct((S // dil, dil * AW), BF16)
    kw = dict(name=f"att_bwd{gi}", grid=(per, dil), in_specs=_slab_specs(gi) + [_bias_spec(gi)] + _head_specs(gi, 4),
              out_specs=(out_spec, out_spec, out_spec,
                         pl.BlockSpec((hp, ATT_BLK, 2 * ATT_BLK), lambda g, r: (g, 0, 0))),
              out_shape=(osh, osh, osh, jax.ShapeDtypeStruct((ATT_HG, ATT_BLK, 2 * ATT_BLK), F32)))
    args = (slabs, slabs, slabs, bias, o, lse, do, dlse)
    if comm is not None:
        return _carry(body, comm, **kw)(*args)
    return _pcall(body, compiler_params=_params(("arbitrary", "arbitrary")), **kw)(*args)


AW = ATT_HG * ATT_DH


def _mix_weights(l0, l1, l2):
    mx = jnp.maximum(jnp.maximum(l0, l1), l2)
    e0, e1, e2 = jnp.exp(l0 - mx), jnp.exp(l1 - mx), jnp.exp(l2 - mx)
    den = e0 + e1 + e2
    return e0 / den, e1 / den, e2 / den


def _heads_spec():
    return pl.BlockSpec((ATT_HG, TR, ATT_DH), lambda i: (0, i, 0))


def _mix_fwd(os_, ls, comm=None):
    def body(o0, o1, o2, l0, l1, l2, att_ref):
        for h in range(ATT_HG):
            w0, w1, w2 = _mix_weights(l0[h], l1[h], l2[h])
            att_ref[:, h * ATT_DH:(h + 1) * ATT_DH] = (w0 * o0[h] + w1 * o1[h] + w2 * o2[h]).astype(BF16)

    kw = dict(name="mix_fwd", grid=(S // TR,), in_specs=[_heads_spec()] * 6, out_specs=_row_spec(AW),
              out_shape=jax.ShapeDtypeStruct((S, AW), BF16))
    if comm is not None:
        return _carry(body, comm, **kw)(*os_, *ls)
    return _pcall(body, compiler_params=_params(("parallel",)), **kw)(*os_, *ls)


def _mix_bwd(os_, ls, datt):
    def body(o0, o1, o2, l0, l1, l2, da_ref, d0, d1, d2, e0, e1, e2):
        for h in range(ATT_HG):
            ws = _mix_weights(l0[h], l1[h], l2[h])
            da = da_ref[:, h * ATT_DH:(h + 1) * ATT_DH]
            dws = []
            for o_ref, w, d_ref in zip((o0, o1, o2), ws, (d0, d1, d2)):
                d_ref[h] = w * da
                dws.append(jnp.broadcast_to(jnp.sum(da * o_ref[h], axis=-1, keepdims=True), (TR, ATT_DH)))
            tot = ws[0] * dws[0] + ws[1] * dws[1] + ws[2] * dws[2]
            for w, dw, e_ref in zip(ws, dws, (e0, e1, e2)):
                e_ref[h] = w * (dw - tot)

    o = jax.ShapeDtypeStruct((ATT_HG, S, ATT_DH), F32)
    return _pcall(body, name="mix_bwd", grid=(S // TR,), in_specs=[_heads_spec()] * 6 + [_row_spec(AW)],
                  out_specs=(_heads_spec(),) * 6, out_shape=(o,) * 6,
                  compiler_params=_params(("parallel",)))(*os_, *ls, datt)


def _ada_fwd(c_all, w_sh, b_sl):
    def body(c_ref, w_ref, b_ref, o_ref):
        cv = c_ref[...]
        o_ref[...] = _dot(cv * jax.nn.sigmoid(cv), w_ref[...], NN) + b_ref[...]

    return _pcall(body, name="ada_fwd", out_shape=jax.ShapeDtypeStruct((N_DEV, w_sh.shape[1]), F32),
                  compiler_params=_params())(c_all, w_sh, b_sl)


def _ada_bwd(c_all, dm_sl):
    def body(c_ref, d_ref, o_ref):
        cv = c_ref[...]
        o_ref[...] = _dot(cv * jax.nn.sigmoid(cv), d_ref[...], TN)

    return _pcall(body, name="ada_bwd", out_shape=jax.ShapeDtypeStruct((D, dm_sl.shape[1]), F32),
                  compiler_params=_params())(c_all, dm_sl)


N_MOD = 6


def _sum_small(gathered):
    n = len(gathered)

    def body(*refs):
        ins, (gb_ref, dm_ref), outs = refs[:n], refs[n:n + 2], refs[n + 2:]

        def total(r):
            acc = r[0]
            for e in range(1, N_DEV):
                acc = acc + r[e]
            return acc

        for i in range(N_MOD):
            cols = slice(i * D, (i + 1) * D)
            gb_ref[:, cols] = total(ins[i])
            for e in range(N_DEV):
                dm_ref[e:e + 1, cols] = ins[i][e]
        for r, o_ref in zip(ins[N_MOD:], outs):
            o_ref[...] = total(r)

    shapes = (jax.ShapeDtypeStruct((1, N_MOD * D), F32), jax.ShapeDtypeStruct((N_DEV, N_MOD * D), F32),
              *[jax.ShapeDtypeStruct(g.shape[1:], F32) for g in gathered[N_MOD:]])
    res = _pcall(body, name="sum_small", out_shape=shapes, compiler_params=_params())(*gathered)
    return res[0], res[1], res[2:]


def _row_tile(m, n):
    t = max(8, min(m, (1 << 19) // n // 8 * 8))
    while m % t:
        t -= 8
    return t


def _pair_sum(full, recv, sel, name):
    _, m, n = recv.shape
    t = _row_tile(m, n)

    def body(sel_ref, a_ref, b_ref, o_ref):
        o_ref[...] = (a_ref[...].astype(F32) + b_ref[...].astype(F32)).astype(o_ref.dtype)

    gs = pltpu.PrefetchScalarGridSpec(
        num_scalar_prefetch=1, grid=(4, m // t),
        in_specs=[pl.BlockSpec((None, None, t, n), lambda q, i, s: (q, s[0], i, 0)),
                  pl.BlockSpec((None, t, n), lambda q, i, s: (q, i, 0))],
        out_specs=pl.BlockSpec((None, t, n), lambda q, i, s: (q, i, 0)))
    return _pcall(body, name=name, grid_spec=gs, out_shape=jax.ShapeDtypeStruct((4, m, n), full.dtype),
                  compiler_params=_params(("parallel", "parallel")))(sel, full, recv)


def _chip_sum(part, recv, sel, name):
    _, m, n = part.shape
    t = _row_tile(m, n)

    def body(sel_ref, a_ref, r_ref, o_ref):
        o_ref[...] = ((a_ref[...].astype(F32) + r_ref[0].astype(F32)) + r_ref[1].astype(F32)) + r_ref[2].astype(F32)

    gs = pltpu.PrefetchScalarGridSpec(
        num_scalar_prefetch=1, grid=(m // t,),
        in_specs=[pl.BlockSpec((None, t, n), lambda i, s: (s[0], i, 0)),
                  pl.BlockSpec((3, t, n), lambda i, s: (0, i, 0))],
        out_specs=pl.BlockSpec((t, n), lambda i, s: (i, 0)))
    return _pcall(body, name=name, grid_spec=gs, out_shape=jax.ShapeDtypeStruct((m, n), F32),
                  compiler_params=_params(("parallel",)))(sel, part, recv)


def _adamw_math(w, g, m, v):
    nm = ADAM_B1 * m + (1.0 - ADAM_B1) * g
    nv = ADAM_B2 * v + (1.0 - ADAM_B2) * (g * g)
    m_hat = nm / (1.0 - ADAM_B1 ** ADAM_STEP)
    v_hat = nv / (1.0 - ADAM_B2 ** ADAM_STEP)
    return -ADAM_LR * (m_hat / (jnp.sqrt(v_hat) + ADAM_EPS) + ADAM_WD * w), nm, nv


def _adamw(w, g, m, v, name):
    _, rows, cols = w.shape
    t = _row_tile(rows, cols)

    def body(w_ref, g_ref, m_ref, v_ref, d_ref, nm_ref, nv_ref):
        d_ref[...], nm_ref[...], nv_ref[...] = _adamw_math(w_ref[...], g_ref[...], m_ref[...], v_ref[...])

    spec3 = pl.BlockSpec((None, t, cols), lambda i: (0, i, 0))
    spec2 = pl.BlockSpec((t, cols), lambda i: (i, 0))
    o = jax.ShapeDtypeStruct(w.shape, F32)
    return _pcall(body, name=name, grid=(rows // t,), in_specs=[spec3, spec2, spec3, spec3], out_specs=(spec3,) * 3,
                  out_shape=(o, o, o), compiler_params=_params(("parallel",)))(w, g, m, v)


def _adamw_reduced1(w, m, v, part, recv, sel, name):
    _, rows, cols = w.shape
    t = _row_tile(rows, cols)

    def body(sel_ref, w_ref, m_ref, v_ref, p_ref, r_ref, g_ref, d_ref, nm_ref, nv_ref):
        g = ((p_ref[...].astype(F32) + r_ref[0].astype(F32)) + r_ref[1].astype(F32)) + r_ref[2].astype(F32)
        g_ref[...] = g
        d_ref[...], nm_ref[...], nv_ref[...] = _adamw_math(w_ref[...], g, m_ref[...], v_ref[...])

    wspec = pl.BlockSpec((None, t, cols), lambda i, s: (0, i, 0))
    gs = pltpu.PrefetchScalarGridSpec(
        num_scalar_prefetch=1, grid=(rows // t,),
        in_specs=[wspec, wspec, wspec, pl.BlockSpec((None, t, cols), lambda i, s: (s[0], i, 0)),
                  pl.BlockSpec((3, t, cols), lambda i, s: (0, i, 0))],
        out_specs=(wspec,) * 4)
    o = jax.ShapeDtypeStruct(w.shape, F32)
    return _pcall(body, name=name, grid_spec=gs, out_shape=(o, o, o, o),
                  compiler_params=_params(("parallel",)))(sel, w, m, v, part, recv)


def _adamw_reduced(w, m, v, parts, recvs, sel):
    _, rows, cols = w.shape
    half = cols // 2
    t = _row_tile(rows, half)

    def body(sel_ref, w_ref, m_ref, v_ref, pa_ref, pb_ref, ra_ref, rb_ref, g_ref, d_ref, nm_ref, nv_ref):
        total = lambda p_ref, r_ref: ((p_ref[...].astype(F32) + r_ref[0].astype(F32)) + r_ref[1].astype(F32)) \
            + r_ref[2].astype(F32)
        g = jnp.where(pl.program_id(1) == 0, total(pa_ref, ra_ref), total(pb_ref, rb_ref))
        g_ref[...] = g
        d_ref[...], nm_ref[...], nv_ref[...] = _adamw_math(w_ref[...], g, m_ref[...], v_ref[...])

    wspec = pl.BlockSpec((None, t, half), lambda i, j, s: (0, i, j))
    pspec = pl.BlockSpec((None, t, half), lambda i, j, s: (s[0], i, 0))
    rspec = pl.BlockSpec((3, t, half), lambda i, j, s: (0, i, 0))
    gs = pltpu.PrefetchScalarGridSpec(num_scalar_prefetch=1, grid=(rows // t, 2),
                                      in_specs=[wspec, wspec, wspec, pspec, pspec, rspec, rspec],
                                      out_specs=(wspec,) * 4)
    o = jax.ShapeDtypeStruct(w.shape, F32)
    return _pcall(body, name="adamw_w_in", grid_spec=gs, out_shape=(o, o, o, o),
                  compiler_params=_params(("parallel", "arbitrary")))(sel, w, m, v, *parts, *recvs)


def _adamw_small(ws, gs, ms, vs):
    n = len(ws)

    def body(*refs):
        for i in range(n):
            w_ref, g_ref, m_ref, v_ref = (refs[k * n + i] for k in range(4))
            d, nm, nv = _adamw_math(w_ref[...], g_ref[...], m_ref[...], v_ref[...])
            refs[4 * n + i][...] = d
            refs[5 * n + i][...] = nm
            refs[6 * n + i][...] = nv

    shapes = tuple(jax.ShapeDtypeStruct(w.shape, F32) for w in ws)
    res = _pcall(body, name="adamw_small", out_shape=shapes * 3, compiler_params=_params())(*ws, *gs, *ms, *vs)
    return res[:n], res[n:2 * n], res[2 * n:]


def _mesh_pos():
    return lax.axis_index("x"), lax.axis_index("y"), lax.axis_index("c")


class _Gather:
    def __init__(self, arrs):
        self.ins = list(arrs)
        self.out_shape = tuple(jax.ShapeDtypeStruct((N_DEV,) + a.shape, a.dtype) for a in arrs)
        n = len(arrs)
        self.sems = [pltpu.SemaphoreType.DMA((7 * n,)), pltpu.SemaphoreType.DMA((7 * n,)),
                     pltpu.SemaphoreType.DMA((n,))]

    def _copies(self, ins, outs, sems):
        send_sems, recv_sems, local_sems = sems
        x, y, c = _mesh_pos()
        me, sibling = (x, y, c), (x, y, 1 - c)
        chips = [(1 - x, y), (x, 1 - y), (1 - x, 1 - y)]

        def copy(p, k, block, to, from_input=False):
            dst = outs[p].at[_slot(block)]
            return pltpu.make_async_remote_copy(
                src_ref=ins[p] if from_input else dst, dst_ref=dst, send_sem=send_sems.at[7 * p + k],
                recv_sem=recv_sems.at[7 * p + k], device_id=to, device_id_type=MESH)

        npc = len(self.ins)
        mine = [pltpu.make_async_copy(ins[p], outs[p].at[_slot(me)], local_sems.at[p]) for p in range(npc)]
        first = []
        for p in range(npc):
            first.append(copy(p, 0, me, sibling, from_input=True))
            first += [copy(p, 1 + j, me, (*chip, c), from_input=True) for j, chip in enumerate(chips)]
        return me, sibling, chips, c, copy, mine, first

    def start(self, ins, outs, sems):
        *_, mine, first = self._copies(ins, outs, sems)
        for cp in mine + first:
            cp.start()

    def finish(self, ins, outs, sems):
        me, sibling, chips, c, copy, mine, first = self._copies(ins, outs, sems)
        npc = len(self.ins)
        passed = []
        for p in range(npc):
            for j, chip in enumerate(chips):
                copy(p, 1 + j, (*chip, c), me).wait_recv()
                passed.append(copy(p, 4 + j, (*chip, c), sibling))
                passed[-1].start()
        for p in range(npc):
            copy(p, 0, sibling, me).wait_recv()
            for j, chip in enumerate(chips):
                copy(p, 4 + j, (*chip, 1 - c), me).wait_recv()
        for cp in first + passed:
            cp.wait_send()
        for cp in mine:
            cp.wait()


class _ExchangeCore:
    def __init__(self, fulls):
        self.ins = list(fulls)
        self.out_shape = tuple(jax.ShapeDtypeStruct((4,) + f.shape[2:], f.dtype) for f in fulls)
        self.sems = [pltpu.SemaphoreType.DMA((4 * len(fulls),)), pltpu.SemaphoreType.DMA((4 * len(fulls),))]

    def _copies(self, ins, outs, sems):
        send_sems, recv_sems = sems
        x, y, c = _mesh_pos()
        return [pltpu.make_async_remote_copy(
            src_ref=ins[a].at[q, 1 - c], dst_ref=outs[a].at[q], send_sem=send_sems.at[4 * a + q],
            recv_sem=recv_sems.at[4 * a + q], device_id=(x, y, 1 - c), device_id_type=MESH)
            for a in range(len(self.ins)) for q in range(4)]

    def start(self, ins, outs, sems):
        for cp in self._copies(ins, outs, sems):
            cp.start()

    def finish(self, ins, outs, sems):
        for cp in self._copies(ins, outs, sems):
            cp.wait()


class _ExchangeChip:
    def __init__(self, parts):
        self.ins = list(parts)
        self.out_shape = tuple(jax.ShapeDtypeStruct((3,) + p.shape[1:], p.dtype) for p in parts)
        self.sems = [pltpu.SemaphoreType.DMA((3 * len(parts),)), pltpu.SemaphoreType.DMA((3 * len(parts),))]

    def _copies(self, ins, outs, sems):
        send_sems, recv_sems = sems
        x, y, c = _mesh_pos()
        chips = [(1 - x, y), (x, 1 - y), (1 - x, 1 - y)]
        return [pltpu.make_async_remote_copy(
            src_ref=ins[a].at[2 * px + py], dst_ref=outs[a].at[j], send_sem=send_sems.at[3 * a + j],
            recv_sem=recv_sems.at[3 * a + j], device_id=(px, py, c), device_id_type=MESH)
            for a in range(len(self.ins)) for j, (px, py) in enumerate(chips)]

    def start(self, ins, outs, sems):
        for cp in self._copies(ins, outs, sems):
            cp.start()

    def finish(self, ins, outs, sems):
        for cp in self._copies(ins, outs, sems):
            cp.wait()


HBM_ONLY = pl.BlockSpec(memory_space=pltpu.HBM)
SEM_SPEC = pl.BlockSpec(memory_space=pltpu.SEMAPHORE)
SIDE_EFFECT = pltpu.SideEffectType.DATAFLOW_SIDE_EFFECTING


def _chip_copies(p_refs, land_refs, send_sems, recv_sems):
    x, y, c = _mesh_pos()
    return [pltpu.make_async_remote_copy(
        src_ref=p_refs[a].at[2 * px + py], dst_ref=land_refs[a].at[j], send_sem=send_sems.at[3 * a + j],
        recv_sem=recv_sems.at[3 * a + j], device_id=(px, py, c), device_id_type=MESH)
        for a in range(len(p_refs)) for j, (px, py) in enumerate([(1 - x, y), (x, 1 - y), (1 - x, 1 - y)])]


def _chip_exchange_start(parts, name):
    n = len(parts)
    lands = [lax.empty((3,) + p.shape[1:], p.dtype) for p in parts]

    def body(*refs):
        p_refs, land_refs, (send_sems, recv_sems) = refs[:n], refs[n:2 * n], refs[2 * n:2 * n + 2]
        for cp in _chip_copies(p_refs, land_refs, send_sems, recv_sems):
            cp.start()
        token = refs[-1]
        token[...] = jnp.zeros_like(token)

    hbm = lambda t: pltpu.HBM(t.shape, t.dtype)
    res = pl.pallas_call(
        body, name=name,
        out_shape=(pltpu.SemaphoreType.DMA((3 * n,)), pltpu.SemaphoreType.DMA((3 * n,)), *[hbm(t) for t in parts + lands],
                   jax.ShapeDtypeStruct((8, 128), F32)),
        in_specs=(HBM_ONLY,) * (2 * n),
        out_specs=(SEM_SPEC, SEM_SPEC, *[HBM_ONLY] * (2 * n), pl.BlockSpec(memory_space=pltpu.VMEM)),
        input_output_aliases={i: 2 + i for i in range(2 * n)},
        compiler_params=pltpu.CompilerParams(has_side_effects=SIDE_EFFECT))(
        *[pltpu.with_memory_space_constraint(t, pltpu.HBM) for t in parts + lands])
    return (res[0], res[1], list(res[2:2 + n]), list(res[2 + n:2 + 2 * n])), res[-1]


def _chip_exchange_wait(in_flight, after, name):
    send_sems, recv_sems, parts, lands = in_flight
    n = len(parts)

    def body(*refs):
        p_refs, land_refs, (send_sems, recv_sems) = refs[:n], refs[n:2 * n], refs[2 * n:2 * n + 2]
        for cp in _chip_copies(p_refs, land_refs, send_sems, recv_sems):
            cp.wait_send()
            cp.wait_recv()

    res = pl.pallas_call(
        body, name=name, out_shape=tuple(pltpu.HBM(t.shape, t.dtype) for t in parts + lands),
        in_specs=(*[HBM_ONLY] * (2 * n), SEM_SPEC, SEM_SPEC, pl.BlockSpec(memory_space=pl.ANY)),
        out_specs=(HBM_ONLY,) * (2 * n), input_output_aliases={i: i for i in range(2 * n)},
        compiler_params=pltpu.CompilerParams(has_side_effects=SIDE_EFFECT))(*parts, *lands, send_sems, recv_sems, after)
    return list(res[:n]), list(res[n:])


def _slot(p):
    return 4 * p[0] + 2 * p[1] + p[2]


def _gather_copies(src_refs, out_refs, send_sems, recv_sems):
    x, y, c = _mesh_pos()
    targets = [(x, y, 1 - c), (1 - x, y, c), (x, 1 - y, c), (1 - x, 1 - y, c)]
    return [pltpu.make_async_remote_copy(
        src_ref=src_refs[a], dst_ref=out_refs[a].at[_slot((x, y, c))], send_sem=send_sems.at[4 * a + k],
        recv_sem=recv_sems.at[4 * a + k], device_id=to, device_id_type=MESH)
        for a in range(len(src_refs)) for k, to in enumerate(targets)]


def _gather_start(shards, after, name):
    n = len(shards)
    outs = [lax.empty((N_DEV,) + s.shape, s.dtype) for s in shards]

    def body(*refs):
        for cp in _gather_copies(refs[:n], refs[n:2 * n], refs[2 * n + 1], refs[2 * n + 2]):
            cp.start()
        token = refs[-1]
        token[...] = jnp.zeros_like(token)

    res = pl.pallas_call(
        body, name=name,
        out_shape=(pltpu.SemaphoreType.DMA((4 * n,)), pltpu.SemaphoreType.DMA((4 * n,)),
                   *[pltpu.HBM(t.shape, t.dtype) for t in shards + outs], jax.ShapeDtypeStruct((8, 128), F32)),
        in_specs=(*[HBM_ONLY] * (2 * n), pl.BlockSpec(memory_space=pl.ANY)),
        out_specs=(SEM_SPEC, SEM_SPEC, *[HBM_ONLY] * (2 * n), pl.BlockSpec(memory_space=pltpu.VMEM)),
        input_output_aliases={i: 2 + i for i in range(2 * n)},
        compiler_params=pltpu.CompilerParams(has_side_effects=SIDE_EFFECT))(
        *[pltpu.with_memory_space_constraint(t, pltpu.HBM) for t in shards + outs], after)
    return (res[0], res[1], list(res[2:2 + n]), list(res[2 + n:2 + 2 * n])), res[-1]


def _gather_wait(in_flight, after, name):
    send_sems, recv_sems, shards, outs = in_flight
    n = len(shards)

    def body(*refs):
        for cp in _gather_copies(refs[:n], refs[n:2 * n], refs[2 * n], refs[2 * n + 1]):
            cp.wait_send()
            cp.wait_recv()

    res = pl.pallas_call(
        body, name=name, out_shape=tuple(pltpu.HBM(t.shape, t.dtype) for t in shards + outs),
        in_specs=(*[HBM_ONLY] * (2 * n), SEM_SPEC, SEM_SPEC, pl.BlockSpec(memory_space=pl.ANY)),
        out_specs=(HBM_ONLY,) * (2 * n), input_output_aliases={i: i for i in range(2 * n)},
        compiler_params=pltpu.CompilerParams(has_side_effects=SIDE_EFFECT))(*shards, *outs, send_sems, recv_sems, after)
    return list(res[:n]), list(res[n:])


class _PassToSibling:
    def __init__(self, shards, gathered):
        n = self.n = len(shards)
        self.ins = list(shards) + list(gathered)
        self.out_shape = tuple(jax.ShapeDtypeStruct(g.shape, g.dtype) for g in gathered)
        self.aliases = {n + a: a for a in range(n)}
        self.sems = [pltpu.SemaphoreType.DMA((3 * n,)), pltpu.SemaphoreType.DMA((3 * n,)),
                     pltpu.SemaphoreType.DMA((n,))]

    def _copies(self, ins, outs, sems):
        send_sems, recv_sems, local_sems = sems
        x, y, c = _mesh_pos()
        chips = [(1 - x, y), (x, 1 - y), (1 - x, 1 - y)]
        mine = [pltpu.make_async_copy(ins[a], outs[a].at[_slot((x, y, c))], local_sems.at[a]) for a in range(self.n)]
        passed, awaited = [], []
        for a in range(self.n):
            for j, chip in enumerate(chips):
                sems_j = dict(send_sem=send_sems.at[3 * a + j], recv_sem=recv_sems.at[3 * a + j],
                              device_id=(x, y, 1 - c), device_id_type=MESH)
                blk = outs[a].at[_slot((*chip, c))]
                passed.append(pltpu.make_async_remote_copy(src_ref=blk, dst_ref=blk, **sems_j))
                got = outs[a].at[_slot((*chip, 1 - c))]
                awaited.append(pltpu.make_async_remote_copy(src_ref=got, dst_ref=got, **sems_j))
        return mine, passed, awaited

    def start(self, ins, outs, sems):
        mine, passed, _ = self._copies(ins, outs, sems)
        for cp in mine + passed:
            cp.start()

    def finish(self, ins, outs, sems):
        mine, passed, awaited = self._copies(ins, outs, sems)
        for cp in passed:
            cp.wait_send()
        for cp in awaited:
            cp.wait_recv()
        for cp in mine:
            cp.wait()


def _reduce_sums(fulls, recv_core, core, tag):
    return [_pair_sum(f, r, core, f"rs_pair_{tag}{i}") for i, (f, r) in enumerate(zip(fulls, recv_core))]


def _local_step(x, tgt, mods, w_in_shard, order, shards, small, chip, core):
    sh1, sc1, g1, sh2, sc2, g2 = mods
    norm1_g, rel_bias, gn_g, gn_b, norm2_g, norm_f_g = small
    tables = _ret_tables()
    buckets = jnp.asarray(_bucket_tables())

    h1 = _norm_mod_fwd(x, norm1_g, sh1, sc1, "norm1_fwd")
    proj, slabs, w_in_t = _gather_proj(h1, w_in_shard, order)
    flight_w1, token_w = _gather_start(list(shards[:3]), proj, "gather_w1_start")
    flight_w2, token_w = _gather_start(list(shards[3:]), token_w, "gather_w2_start")
    bias = _bias_build(rel_bias, buckets, token_w)
    outs, lses = [], []
    for gi in range(len(ATT_GROUPS)):
        o, l = _att_fwd(slabs, bias, gi)
        outs.append(o)
        lses.append(l)
    att, gathered = _mix_fwd(outs, lses, comm=_PassToSibling(*_gather_wait(flight_w1, lses[2], "gather_w1_wait")))
    w_ret_out, w_att_out, w_o = (_from_slots(g, ax) for g, ax in zip(gathered, BIG_AXES[1:4]))
    gated, ro, states = _ret_fwd(proj, tables, gn_g, gn_b, att)
    ret_out, gathered = _mm(gated, w_ret_out, 'nn', tm=S, tn=256, tk=2048, name="ret_out",
                            comm=_PassToSibling(*_gather_wait(flight_w2, gated, "gather_w2_wait")))
    w_ff1, w_ff2 = (_from_slots(g, ax) for g, ax in zip(gathered, BIG_AXES[4:]))
    att_out, merged = _att_out_merge(att, w_att_out, proj, ret_out)
    mixo, x1, h2 = _w_o_norm2(merged, w_o, x, g1, norm2_g, sh2, sc2)
    u, act = _mm(h2, w_ff1, 'nn', tm=S, tn=512, tk=D, name="ff1", relu2=True)
    loss, dx2, g_normf, df, dg2 = _ff2_final(act, w_ff2, x1, g2, tgt, norm_f_g)

    gw_ff2 = _mm(act, df, 'tn', tm=512, tn=D, tk=S, name="gw_ff2", out_dtype=BF16)
    du = _mm(df, w_ff2, 'nt', tm=S, tn=512, tk=D, name="d_act", out_dtype=BF16, relu2_of=u)
    gw_ff1 = _mm(h2, du, 'tn', tm=D, tn=512, tk=S, name="gw_ff1", out_dtype=BF16)
    fulls_a = [_to_slots(g, ax) for g, ax in zip((gw_ff1, gw_ff2), BIG_AXES[4:])]
    dh2, recv_core_a = _mm(du, w_ff1, 'nt', tm=1024, tn=1024, tk=2048, name="dh2", comm=_ExchangeCore(fulls_a))
    parts_a = _reduce_sums(fulls_a, recv_core_a, core, "a")
    flight_a, token_a = _chip_exchange_start(parts_a, "rs_a_start")
    dx1, dsc2, dsh2, g_norm2, dmixo, dg1 = _norm_mod_bwd(x1, norm2_g, sc2, dh2, dx2, "norm2_bwd", gate=(mixo, g1))

    gw_o = _mm(merged, dmixo, 'tn', tm=D, tn=512, tk=S, name="gw_o", out_dtype=BF16, after=token_a)
    d_ret_out, d_att_out, dga, dgb = _dmerged_split(dmixo, w_o, proj, ret_out, att_out)
    gw_ret_out = _mm(gated, d_ret_out, 'tn', tm=512, tn=D, tk=S, name="gw_ret_out", out_dtype=BF16)
    gw_att_out = _mm(att, d_att_out, 'tn', tm=AW, tn=D, tk=S, name="gw_att_out", out_dtype=BF16)
    fulls_b = [_to_slots(g, ax) for g, ax in zip((gw_ret_out, gw_att_out, gw_o), BIG_AXES[1:4])]
    dgated, recv_core_b = _mm(d_ret_out, w_ret_out, 'nt', tm=S, tn=512, tk=D, name="dgated",
                              comm=_ExchangeCore(fulls_b))
    parts_b = _reduce_sums(fulls_b, recv_core_b, core, "b")
    flight_b, token_b = _chip_exchange_start(parts_b, "rs_b_start")
    datt = _mm(d_att_out, w_att_out, 'nt', tm=S, tn=AW, tk=D, name="datt", after=token_b)
    mix_grads = _mix_bwd(outs, lses, datt)
    datt_parts, ds_sums = [], []
    for gi in range(len(ATT_GROUPS)):
        dq, dk, dv, ds_sum = _att_bwd(slabs, bias, outs[gi], lses[gi], mix_grads[gi], mix_grads[3 + gi], gi)
        datt_parts += [dq.reshape(S, AW), dk.reshape(S, AW), dv.reshape(S, AW)]
        ds_sums.append(ds_sum)
    g_bias = _bias_grad(jnp.concatenate(ds_sums, axis=0), buckets)[:, :, 0].T.reshape(1, -1)
    dproj, g_gn_g, g_gn_b = _ret_bwd(proj, tables, gn_g, gn_b, ro, states, dgated, datt_parts + [dga, dgb])
    parts_a, recv_chip_a = _chip_exchange_wait(flight_a, dproj, "rs_a_wait")
    parts_b, recv_chip_b = _chip_exchange_wait(flight_b, dproj, "rs_b_wait")
    reduced = list(zip(parts_b + parts_a, recv_chip_b + recv_chip_a))
    gw_in = lambda half, **kw: _mm(dproj, h1[:, half * (D // 2):(half + 1) * (D // 2)], 'tn', tm=512, tn=D // 2, tk=S,
                                   name=f"gw_in{half}", out_dtype=BF16, **kw)
    full_in0 = _to_slots(gw_in(0), 0)
    gw_in1, (recv_core_in0,) = gw_in(1, comm=_ExchangeCore([full_in0]))
    flight0, token = _chip_exchange_start([_pair_sum(full_in0, recv_core_in0, core, "rs_pair_c0")], "rs_in0_start")
    full_in1 = _to_slots(gw_in1, 0)
    dh1, (recv_core_in1,) = _mm(dproj, w_in_t, 'nn', tm=1024, tn=1024, tk=2560, name="dh1",
                                comm=_ExchangeCore([full_in1]), after=token)
    flight1, token = _chip_exchange_start([_pair_sum(full_in1, recv_core_in1, core, "rs_pair_c1")], "rs_in1_start")
    in_flight = [flight0, flight1]
    gx, dsc1, dsh1, g_norm1 = _norm_mod_bwd(x, norm1_g, sc1, dh1, dx1, "norm1_bwd", after=token)

    dmod = [dsh1, dsc1, dg1, dsh2, dsc2, dg2]
    small_g = [g_norm1, g_bias, g_gn_g, g_gn_b, g_norm2, g_normf]
    return loss, gx, in_flight, reduced, small_g, dmod


def _to_slots(g, axis):
    if axis == 0:
        return g.reshape(4, 2, g.shape[0] // N_DEV, g.shape[1])
    return g.reshape(g.shape[0], N_DEV, g.shape[1] // N_DEV).transpose(1, 0, 2).reshape(4, 2, g.shape[0], -1)


def _from_slots(w8, axis):
    if axis == 0:
        return w8.reshape(-1, w8.shape[2])
    return w8.transpose(1, 0, 2).reshape(w8.shape[1], -1)


BIG_AXES = (1, 0, 1, 0, 1, 0)


def kernel(x, c, w_ada, b_ada, norm1_g, w_in, rel_bias, ret_gn_g, ret_gn_b, w_ret_out, w_att_out, w_o, norm2_g, w_ff1, w_ff2, norm_f_g, loss_target, m_w_ada, m_b_ada, m_norm1_g, m_w_in, m_rel_bias, m_ret_gn_g, m_ret_gn_b, m_w_ret_out, m_w_att_out, m_w_o, m_norm2_g, m_w_ff1, m_w_ff2, m_norm_f_g, v_w_ada, v_b_ada, v_norm1_g, v_w_in, v_rel_bias, v_ret_gn_g, v_ret_gn_b, v_w_ret_out, v_w_att_out, v_w_o, v_norm2_g, v_w_ff1, v_w_ff2, v_norm_f_g):
    mx, my, mc = _mesh_pos()
    dev = 4 * mx + 2 * my + mc
    chip = jnp.reshape(2 * mx + my, (1,)).astype(jnp.int32)
    core = jnp.reshape(mc, (1,)).astype(jnp.int32)
    ada_w = D * 6 // N_DEV

    w_in, m_w_in, v_w_in = (jnp.transpose(t, (0, 2, 1)) for t in (w_in, m_w_in, v_w_in))

    shards = [w[0].astype(BF16) for w in (w_in, w_ret_out, w_att_out, w_o, w_ff1, w_ff2)]
    (c_all,) = _run_comm(_Gather([c]), "gather_c")
    c_all = c_all.reshape(N_DEV, D)
    b_sl = lax.dynamic_slice(b_ada, (0, dev * ada_w), (1, ada_w))
    (mod_all,) = _run_comm(_Gather([_ada_fwd(c_all, w_ada[0], b_sl)]), "gather_mod")
    mod = lax.dynamic_index_in_dim(mod_all, dev, axis=1, keepdims=False).reshape(6, D)
    mods = tuple(mod[i:i + 1] for i in range(6))

    small = (norm1_g, rel_bias, ret_gn_g, ret_gn_b, norm2_g, norm_f_g.reshape(1, D))
    order = lax.dynamic_index_in_dim(jnp.asarray(_proj_order()), 2 * mx + my, axis=0, keepdims=False)
    loss, gx, in_flight, big_red, small_g, dmod = _local_step(x[0], loss_target[0], mods, shards[0], order,
                                                              shards[1:], small, chip, core)

    names = ['w_ada', 'b_ada', 'norm1_g', 'w_in', 'rel_bias', 'ret_gn_g', 'ret_gn_b', 'w_ret_out', 'w_att_out',
             'w_o', 'norm2_g', 'w_ff1', 'w_ff2', 'norm_f_g']
    ws = dict(zip(names, (w_ada, b_ada, norm1_g, w_in, rel_bias, ret_gn_g, ret_gn_b, w_ret_out, w_att_out, w_o,
                          norm2_g, w_ff1, w_ff2, norm_f_g)))
    ms = dict(zip(names, (m_w_ada, m_b_ada, m_norm1_g, m_w_in, m_rel_bias, m_ret_gn_g, m_ret_gn_b, m_w_ret_out,
                          m_w_att_out, m_w_o, m_norm2_g, m_w_ff1, m_w_ff2, m_norm_f_g)))
    vs = dict(zip(names, (v_w_ada, v_b_ada, v_norm1_g, v_w_in, v_rel_bias, v_ret_gn_g, v_ret_gn_b, v_w_ret_out,
                          v_w_att_out, v_w_o, v_norm2_g, v_w_ff1, v_w_ff2, v_norm_f_g)))
    grads, delta, new_m, new_v = {}, {}, {}, {}
    big_names = ('w_ret_out', 'w_att_out', 'w_o', 'w_ff1', 'w_ff2')
    for n, (part, recv) in zip(big_names, big_red):
        grads[n], delta[n], new_m[n], new_v[n] = _adamw_reduced1(ws[n], ms[n], vs[n], part, recv, chip, "adamw_" + n)
    updated = lax.optimization_barrier((gx, tuple(delta[n] for n in big_names)))
    gathered = _run_comm(_Gather(dmod + small_g + [loss]), "gather_small", after=updated[0])
    g_b_ada, dmod_all, (g_norm1, g_bias, g_gn_g, g_gn_b, g_norm2, g_normf, loss_sum) = _sum_small(gathered)
    loss_out = loss_sum[0, 0]
    g_w_ada = _ada_bwd(c_all, lax.dynamic_slice(dmod_all, (0, dev * ada_w), (N_DEV, ada_w)))

    grads.update(w_ada=g_w_ada.reshape(w_ada.shape), b_ada=g_b_ada, norm1_g=g_norm1, rel_bias=g_bias,
                 ret_gn_g=g_gn_g, ret_gn_b=g_gn_b, norm2_g=g_norm2, norm_f_g=g_normf)
    delta['w_ada'], new_m['w_ada'], new_v['w_ada'] = _adamw(w_ada, g_w_ada, m_w_ada, v_w_ada, "adamw_w_ada")
    small_names = ('b_ada', 'norm1_g', 'rel_bias', 'ret_gn_g', 'ret_gn_b', 'norm2_g', 'norm_f_g')
    two_d = {n: (1, ws[n].size) if ws[n].ndim == 1 else ws[n].shape for n in small_names}
    d_, m_, v_ = _adamw_small(*[[src[n].reshape(two_d[n]) for n in small_names] for src in (ws, grads, ms, vs)])
    for i, n in enumerate(small_names):
        shp = ws[n].shape
        delta[n], new_m[n], new_v[n] = d_[i].reshape(shp), m_[i].reshape(shp), v_[i].reshape(shp)
        grads[n] = grads[n].reshape(shp)

    done = lax.optimization_barrier((gx, tuple(d_), tuple(delta[n] for n in ('w_ada', 'w_ret_out', 'w_att_out', 'w_o',
                                                                               'w_ff1', 'w_ff2'))))
    parts_in, recvs_in = [], []
    for half, flight in enumerate(in_flight):
        (part_in,), (recv_chip_in,) = _chip_exchange_wait(flight, done[0], f"rs_in{half}_wait")
        parts_in.append(part_in)
        recvs_in.append(recv_chip_in)
    grads['w_in'], delta['w_in'], new_m['w_in'], new_v['w_in'] = _adamw_reduced(w_in, m_w_in, v_w_in, parts_in,
                                                                               recvs_in, chip)
    for d in (grads, delta, new_m, new_v):
        d['w_in'] = jnp.transpose(d['w_in'], (0, 2, 1))
    return (loss_out, gx[None], *[grads[n] for n in names], *[delta[n] for n in names],
            *[new_m[n] for n in names], *[new_v[n] for n in names])
```

```python
import functools
import math

import numpy as np
import jax
import jax.numpy as jnp
from jax import lax
from jax.experimental import pallas as pl
from jax.experimental.pallas import tpu as pltpu

F32 = jnp.float32
BF16 = jnp.bfloat16
MESH = pl.DeviceIdType.MESH

N_DEV = 8
S = 2048
D = 1024
RET_HEADS = 4
RET_DK = 256
RET_DV = 512
CHUNK = 128
N_CHUNK = S // CHUNK
ATT_GROUPS = ((128, 1), (512, 4), (2048, 16))
ATT_HG = 4
ATT_DH = 128
ATT_BLK = 128
N_BUCKETS = 32
MAX_DIST = 2048
D_FF = 4096
IN_COLS = 12800
OFF_RQ, OFF_RK, OFF_RV, OFF_RG, OFF_ATT = 0, 1024, 2048, 4096, 6144
OFF_GA, OFF_GB = 6144, 7168
RMS_EPS = 1e-6
GN_EPS = 1e-5
ADAM_LR, ADAM_B1, ADAM_B2, ADAM_EPS, ADAM_WD, ADAM_STEP = 0.001, 0.9, 0.999, 1e-08, 0.01, 10
VMEM_LIMIT = 48 * 1024 * 1024


def _pcall(body, **kw):
    return pl.pallas_call(body, **kw)


def _params(sem=None):
    return pltpu.CompilerParams(dimension_semantics=sem, vmem_limit_bytes=VMEM_LIMIT)


HBM_SPEC = pl.BlockSpec(memory_space=pl.ANY)


def _carry(body, comm, *, name, grid, in_specs, out_specs, out_shape, scratch_shapes=()):
    single = not isinstance(out_specs, (tuple, list))
    o_specs = (out_specs,) if single else tuple(out_specs)
    o_shape = (out_shape,) if single else tuple(out_shape)
    n_in, n_out, n_scr = len(in_specs), len(o_specs), len(scratch_shapes)
    nci, nco = len(comm.ins), len(comm.out_shape)
    total = int(np.prod(grid))

    def wrapped(*refs):
        bounds = np.cumsum([0, n_in, nci, n_out, nco, n_scr])
        a, ci, o, co, scr = (refs[bounds[i]:bounds[i + 1]] for i in range(5))
        sems = refs[bounds[5]:]
        flat = 0
        for d, g in enumerate(grid):
            flat = flat * g + pl.program_id(d)

        @pl.when(flat == 0)
        def _():
            comm.start(ci, co, sems)

        body(*a, *o, *scr)

        @pl.when(flat == total - 1)
        def _():
            comm.finish(ci, co, sems)

    aliases = {n_in + i: n_out + o for i, o in getattr(comm, "aliases", {}).items()}
    call = _pcall(wrapped, name=name, grid=grid, in_specs=list(in_specs) + [HBM_SPEC] * nci,
                  out_specs=o_specs + (HBM_SPEC,) * nco, out_shape=o_shape + tuple(comm.out_shape),
                  scratch_shapes=list(scratch_shapes) + list(comm.sems), input_output_aliases=aliases,
                  compiler_params=_params(("arbitrary",) * len(grid)))

    def run(*args):
        res = call(*args, *comm.ins)
        own = res[0] if single else tuple(res[:n_out])
        return own, tuple(res[n_out:])

    return run


def _run_comm(comm, name, after=None):
    nci, nco = len(comm.ins), len(comm.out_shape)
    extra = [] if after is None else [after]

    def body(*refs):
        ci, co, sems = refs[:nci], refs[nci + len(extra):nci + len(extra) + nco], refs[nci + len(extra) + nco:]
        comm.start(ci, co, sems)
        comm.finish(ci, co, sems)

    return _pcall(body, name=name, in_specs=[HBM_SPEC] * (nci + len(extra)), out_specs=(HBM_SPEC,) * nco,
                  out_shape=tuple(comm.out_shape), scratch_shapes=list(comm.sems))(*comm.ins, *extra)


def _dot(a, b, dn):
    return lax.dot_general(a.astype(BF16), b.astype(BF16), (dn, ((), ())), preferred_element_type=F32)


NN = ((1,), (0,))
NT = ((1,), (1,))
TN = ((0,), (0,))


def _mm(a, b, mode, *, tm, tn, tk, name, out_dtype=F32, res=None, gvec=None, relu2=False, relu2_of=None, comm=None,
        after=None):
    if mode == 'nn':
        (M, K), (_, N) = a.shape, b.shape
        a_spec = pl.BlockSpec((tm, tk), lambda i, j, k: (i, k))
        b_spec = pl.BlockSpec((tk, tn), lambda i, j, k: (k, j))
        dn = NN
    elif mode == 'nt':
        (M, K), (N, _) = a.shape, b.shape
        a_spec = pl.BlockSpec((tm, tk), lambda i, j, k: (i, k))
        b_spec = pl.BlockSpec((tn, tk), lambda i, j, k: (j, k))
        dn = NT
    else:
        (K, M), (_, N) = a.shape, b.shape
        a_spec = pl.BlockSpec((tk, tm), lambda i, j, k: (k, i))
        b_spec = pl.BlockSpec((tk, tn), lambda i, j, k: (k, j))
        dn = TN
    assert M % tm == 0 and N % tn == 0 and K % tk == 0, (name, M, N, K)
    nk = K // tk
    fused = res is not None
    o_spec = pl.BlockSpec((tm, tn), lambda i, j, k: (i, j))

    def body(a_ref, b_ref, *rest):
        acc_ref = rest[-1] if nk > 1 else None
        if after is not None:
            rest = rest[1:]
        if fused:
            res_ref, g_ref, o_ref, x_ref = rest[:4]
        elif relu2_of is not None:
            u_ref, o_ref = rest[:2]
        elif relu2:
            o_ref, act_ref = rest[:2]
        else:
            o_ref = rest[0]

        def finish(acc):
            if relu2_of is not None:
                acc = acc * (2.0 * jnp.maximum(u_ref[...], 0.0))
            o_ref[...] = acc.astype(o_ref.dtype)
            if fused:
                x_ref[...] = res_ref[...] + g_ref[...] * acc
            if relu2:
                r = jnp.maximum(acc, 0.0)
                act_ref[...] = (r * r).astype(BF16)

        p = _dot(a_ref[...], b_ref[...], dn)
        if nk == 1:
            finish(p)
        else:
            k = pl.program_id(2)

            @pl.when(k == 0)
            def _():
                acc_ref[...] = p

            @pl.when(k > 0)
            def _():
                acc_ref[...] += p

            @pl.when(k == nk - 1)
            def _():
                finish(acc_ref[...])

    in_specs = [a_spec, b_spec]
    args = [a, b]
    if after is not None:
        in_specs.append(pl.BlockSpec(memory_space=pl.ANY))
        args.append(after)
    out_shape = jax.ShapeDtypeStruct((M, N), out_dtype)
    out_specs = o_spec
    if fused:
        in_specs += [pl.BlockSpec((tm, tn), lambda i, j, k: (i, j)), pl.BlockSpec((1, tn), lambda i, j, k: (0, j))]
        args += [res, gvec]
        out_shape = (out_shape, jax.ShapeDtypeStruct((M, N), F32))
        out_specs = (o_spec, pl.BlockSpec((tm, tn), lambda i, j, k: (i, j)))
    elif relu2_of is not None:
        in_specs.append(pl.BlockSpec((tm, tn), lambda i, j, k: (i, j)))
        args.append(relu2_of)
    elif relu2:
        out_shape = (out_shape, jax.ShapeDtypeStruct((M, N), BF16))
        out_specs = (o_spec, pl.BlockSpec((tm, tn), lambda i, j, k: (i, j)))
    kw = dict(name=name, grid=(M // tm, N // tn, nk), in_specs=in_specs, out_specs=out_specs,
              out_shape=out_shape, scratch_shapes=[pltpu.VMEM((tm, tn), F32)] if nk > 1 else [])
    if comm is not None:
        return _carry(body, comm, **kw)(*args)
    return _pcall(body, compiler_params=_params(("parallel", "parallel", "arbitrary")), **kw)(*args)


PROJ_TN = 512
ATT_T0, ATT_T1 = 6144 // PROJ_TN, 10752 // PROJ_TN
N_SLABS = (ATT_T1 - ATT_T0) * 4
MAIN_COLS = IN_COLS - (ATT_T1 - ATT_T0) * PROJ_TN


PROJ_TILES = IN_COLS // PROJ_TN
SHARD_ROWS = IN_COLS // N_DEV
W_CHUNKS = 4
N_OWN, N_NEAR = 5, 18


def _proj_order():
    out = np.zeros((4, 3, PROJ_TILES), np.int32)
    for q in range(4):
        def hops(t):
            owners = {col // (2 * SHARD_ROWS) for col in (t * PROJ_TN, (t + 1) * PROJ_TN - 1)}
            return max(bin(q ^ p).count("1") for p in owners)
        order = sorted(range(PROJ_TILES), key=lambda t: (hops(t), t))
        assert all(hops(t) == 0 for t in order[:N_OWN]) and all(hops(t) < 2 for t in order[:N_NEAR])
        is_att = [ATT_T0 <= t < ATT_T1 for t in order]
        for row, kind, index in ((1, False, lambda t: t if t < ATT_T0 else t - (ATT_T1 - ATT_T0)),
                                 (2, True, lambda t: t - ATT_T0)):
            own = [index(t) if a == kind else None for t, a in zip(order, is_att)]
            first = next(v for v in own if v is not None)
            last = first
            for j, v in enumerate(own):
                last = last if v is None else v
                out[q, row, j] = last
        out[q, 0] = order
    return out


def _gather_proj(h1, shard, order):
    rows = SHARD_ROWS // W_CHUNKS

    def body(ord_ref, a_ref, sh_ref, main_ref, slab_ref, full_ref, wbuf, fetch_sems, send_sems, recv_sems,
             local_sems):
        j = pl.program_id(0)
        x, y, c = _mesh_pos()
        me, sibling = (x, y, c), (x, y, 1 - c)
        chips = [(1 - x, y), (x, 1 - y), (1 - x, 1 - y)]

        def block(p, owner):
            return full_ref.at[pl.ds(pl.multiple_of(_slot(owner) * SHARD_ROWS + p * rows, 16), rows)]

        def copy(p, k, owner, to, from_input=False):
            dst = block(p, owner)
            return pltpu.make_async_remote_copy(
                src_ref=sh_ref.at[pl.ds(p * rows, rows)] if from_input else dst, dst_ref=dst,
                send_sem=send_sems.at[7 * p + k], recv_sem=recv_sems.at[7 * p + k], device_id=to, device_id_type=MESH)

        pieces = range(W_CHUNKS)
        mine = [pltpu.make_async_copy(sh_ref.at[pl.ds(p * rows, rows)], block(p, me), local_sems.at[p]) for p in pieces]
        first = [copy(p, 0, me, sibling, from_input=True) for p in pieces]
        first += [copy(p, 1 + n, me, (*chips[n], c), from_input=True) for p in pieces for n in range(2)]
        near_pass = [copy(p, 4 + n, (*chips[n], c), sibling) for p in pieces for n in range(2)]
        relay = [copy(p, 3, ((x + 1 - c) % 2, (y + c) % 2, c), ((x + c) % 2, (y + 1 - c) % 2, c)) for p in pieces]
        far_pass = [copy(p, 6, (*chips[2], c), sibling) for p in pieces]

        def fetch(pos):
            slot = lax.rem(pos, 2)
            start = pl.multiple_of(ord_ref[0, pos] * PROJ_TN, PROJ_TN)
            return pltpu.make_async_copy(full_ref.at[pl.ds(start, PROJ_TN)], wbuf.at[slot], fetch_sems.at[slot])

        @pl.when(j == 0)
        def _():
            for cp in mine + first:
                cp.start()
            for cp in mine:
                cp.wait()
            for p in pieces:
                copy(p, 0, sibling, me).wait_recv()
            fetch(j).start()

        @pl.when(j == N_OWN - 1)
        def _():
            for p in pieces:
                for n in range(2):
                    copy(p, 1 + n, (*chips[n], c), me).wait_recv()
                    near_pass[2 * p + n].start()
                relay[p].start()
            for p in pieces:
                for n in range(2):
                    copy(p, 4 + n, (*chips[n], 1 - c), me).wait_recv()

        @pl.when(j == N_NEAR - 1)
        def _():
            for p in pieces:
                copy(p, 3, (*chips[2], c), me).wait_recv()
                far_pass[p].start()
            for p in pieces:
                copy(p, 6, (*chips[2], 1 - c), me).wait_recv()

        @pl.when(j + 1 < PROJ_TILES)
        def _():
            fetch(j + 1).start()

        fetch(j).wait()
        w_ref = wbuf.at[lax.rem(j, 2)]
        tile = ord_ref[0, j]
        is_att = (tile >= ATT_T0) & (tile < ATT_T1)
        chunks = [pl.ds(r * 512, 512) for r in range(S // 512)]

        @pl.when(jnp.logical_not(is_att))
        def _():
            for rws in chunks:
                main_ref[rws, :] = _dot(a_ref[rws, :], w_ref[...], NT)

        @pl.when(is_att)
        def _():
            for rws in chunks:
                p = _dot(a_ref[rws, :], w_ref[...], NT)
                for h in range(4):
                    slab_ref[h, rws, :] = p[:, h * 128:(h + 1) * 128]

        @pl.when(j == PROJ_TILES - 1)
        def _():
            for cp in first + near_pass + relay + far_pass:
                cp.wait_send()

    gs = pltpu.PrefetchScalarGridSpec(
        num_scalar_prefetch=1, grid=(PROJ_TILES,),
        in_specs=[pl.BlockSpec((S, D), lambda j, o: (0, 0)), HBM_SPEC],
        out_specs=(pl.BlockSpec((S, PROJ_TN), lambda j, o: (0, o[1, j])),
                   pl.BlockSpec((4, S, 128), lambda j, o: (o[2, j], 0, 0)), HBM_SPEC),
        scratch_shapes=[pltpu.VMEM((2, PROJ_TN, D), BF16), pltpu.SemaphoreType.DMA((2,)),
                        pltpu.SemaphoreType.DMA((7 * W_CHUNKS,)), pltpu.SemaphoreType.DMA((7 * W_CHUNKS,)),
                        pltpu.SemaphoreType.DMA((W_CHUNKS,))])
    return _pcall(body, name="gather_proj", grid_spec=gs,
                  out_shape=(jax.ShapeDtypeStruct((S, MAIN_COLS), F32), jax.ShapeDtypeStruct((N_SLABS, S, 128), F32),
                             jax.ShapeDtypeStruct((IN_COLS, D), BF16)),
                  compiler_params=_params(("arbitrary",)))(order, h1, shard)


TR = 256


def _row_spec(w=D):
    return pl.BlockSpec((TR, w), lambda i: (i, 0))


def _vec_spec(w=D):
    return pl.BlockSpec((1, w), lambda i: (0, 0))


def _norm_mod_fwd(x, g, sh, sc, name):
    def body(x_ref, g_ref, sh_ref, sc_ref, o_ref):
        xv = x_ref[...]
        rstd = lax.rsqrt(jnp.mean(xv * xv, axis=-1, keepdims=True) + RMS_EPS)
        n = xv * rstd * g_ref[...]
        o_ref[...] = (n * (1.0 + sc_ref[...]) + sh_ref[...]).astype(BF16)

    return _pcall(body, name=name, grid=(S // TR,), in_specs=[_row_spec(), _vec_spec(), _vec_spec(), _vec_spec()],
                  out_specs=_row_spec(), out_shape=jax.ShapeDtypeStruct((S, D), BF16),
                  compiler_params=_params(("parallel",)))(x, g, sh, sc)


def _norm_mod_bwd(x, g, sc, dh, dres, name, gate=None, after=None):
    gated = gate is not None

    def body(x_ref, g_ref, sc_ref, dh_ref, dres_ref, *rest):
        if after is not None:
            rest = rest[1:]
        if gated:
            f_ref, gv_ref, dx_ref, dsc_ref, dsh_ref, dg_ref, dz_ref, dgv_ref = rest
        else:
            dx_ref, dsc_ref, dsh_ref, dg_ref = rest
        i = pl.program_id(0)
        xv = x_ref[...]
        dh = dh_ref[...]
        rstd = lax.rsqrt(jnp.mean(xv * xv, axis=-1, keepdims=True) + RMS_EPS)
        xhat = xv * rstd
        gv = g_ref[...]
        dn = dh * (1.0 + sc_ref[...])
        dxhat = dn * gv
        dx = dres_ref[...] + rstd * (dxhat - xhat * jnp.mean(dxhat * xhat, axis=-1, keepdims=True))
        dx_ref[...] = dx
        sums = [(dsc_ref, jnp.sum(dh * (xhat * gv), axis=0, keepdims=True)),
                (dsh_ref, jnp.sum(dh, axis=0, keepdims=True)),
                (dg_ref, jnp.sum(dn * xhat, axis=0, keepdims=True))]
        if gated:
            dz_ref[...] = (dx * gv_ref[...]).astype(BF16)
            sums.append((dgv_ref, jnp.sum(dx * f_ref[...], axis=0, keepdims=True)))

        @pl.when(i == 0)
        def _():
            for ref, p in sums:
                ref[...] = p

        @pl.when(i > 0)
        def _():
            for ref, p in sums:
                ref[...] += p

    vec = jax.ShapeDtypeStruct((1, D), F32)
    in_specs = [_row_spec(), _vec_spec(), _vec_spec(), _row_spec(), _row_spec()]
    out_specs = [_row_spec(), _vec_spec(), _vec_spec(), _vec_spec()]
    out_shape = [jax.ShapeDtypeStruct((S, D), F32), vec, vec, vec]
    args = [x, g, sc, dh, dres]
    if after is not None:
        in_specs.append(HBM_SPEC)
        args.append(after)
    if gated:
        in_specs += [_row_spec(), _vec_spec()]
        out_specs += [_row_spec(), _vec_spec()]
        out_shape += [jax.ShapeDtypeStruct((S, D), BF16), vec]
        args += list(gate)
    return _pcall(body, name=name, grid=(S // TR,), in_specs=in_specs, out_specs=tuple(out_specs),
                  out_shape=tuple(out_shape), compiler_params=_params(("arbitrary",)))(*args)


def _w_o_norm2(merged, w_o, x, g1, g, sh, sc):
    def body(a_ref, b_ref, x_ref, g1_ref, g_ref, sh_ref, sc_ref, o_ref, x1_ref, h_ref):
        acc = _dot(a_ref[...], b_ref[...], NN)
        o_ref[...] = acc
        xv = x_ref[...] + g1_ref[...] * acc
        x1_ref[...] = xv
        rstd = lax.rsqrt(jnp.mean(xv * xv, axis=-1, keepdims=True) + RMS_EPS)
        h_ref[...] = (xv * rstd * g_ref[...] * (1.0 + sc_ref[...]) + sh_ref[...]).astype(BF16)

    rows = pl.BlockSpec((FF2_TM, D), lambda i: (i, 0))
    f32 = jax.ShapeDtypeStruct((S, D), F32)
    return _pcall(body, name="w_o_norm2", grid=(S // FF2_TM,),
                  in_specs=[rows, pl.BlockSpec((D, D), lambda i: (0, 0)), rows] + [_vec_spec()] * 4,
                  out_specs=(rows, rows, rows), out_shape=(f32, f32, jax.ShapeDtypeStruct((S, D), BF16)),
                  compiler_params=_params(("parallel",)))(merged, w_o, x, g1, g, sh, sc)


FF2_TM = 512


def _ff2_final(act, w_ff2, x1, g2, tgt, g):
    def body(a_ref, b_ref, x1_ref, g2_ref, t_ref, g_ref, loss_ref, dx_ref, dg_ref, df_ref, dg2_ref):
        i = pl.program_id(0)
        f = _dot(a_ref[...], b_ref[...], NN)
        g2v = g2_ref[...]
        xv = x1_ref[...] + g2v * f
        gv = g_ref[...]
        rstd = lax.rsqrt(jnp.mean(xv * xv, axis=-1, keepdims=True) + RMS_EPS)
        xhat = xv * rstd
        err = xhat * gv - t_ref[...]
        dy = err * (1.0 / D)
        dxhat = dy * gv
        dx = rstd * (dxhat - xhat * jnp.mean(dxhat * xhat, axis=-1, keepdims=True))
        dx_ref[...] = dx
        df_ref[...] = (dx * g2v).astype(BF16)
        p_g = jnp.sum(dy * xhat, axis=0, keepdims=True)
        p_g2 = jnp.sum(dx * f, axis=0, keepdims=True)
        p_l = jnp.zeros((1, 128), F32) + 0.5 * jnp.sum(jnp.mean(err * err, axis=-1, keepdims=True))

        @pl.when(i == 0)
        def _():
            dg_ref[...] = p_g
            dg2_ref[...] = p_g2
            loss_ref[...] = p_l

        @pl.when(i > 0)
        def _():
            dg_ref[...] += p_g
            dg2_ref[...] += p_g2
            loss_ref[...] += p_l

    vec = jax.ShapeDtypeStruct((1, D), F32)
    rows = lambda w: pl.BlockSpec((FF2_TM, w), lambda i: (i, 0))
    return _pcall(body, name="ff2_final", grid=(S // FF2_TM,),
                  in_specs=[rows(D_FF), pl.BlockSpec((D_FF, D), lambda i: (0, 0)), rows(D), _vec_spec(), rows(D),
                            _vec_spec()],
                  out_specs=(_vec_spec(128), rows(D), _vec_spec(), rows(D), _vec_spec()),
                  out_shape=(jax.ShapeDtypeStruct((1, 128), F32), jax.ShapeDtypeStruct((S, D), F32), vec,
                             jax.ShapeDtypeStruct((S, D), BF16), vec),
                  compiler_params=_params(("arbitrary",)))(act, w_ff2, x1, g2, tgt, g)


HALF = 512


MERGE_TM = 1024


def _merge_specs():
    blk = lambda off: pl.BlockSpec((MERGE_TM, HALF), lambda i, j: (i, off // HALF + j))
    return blk(OFF_GA), blk(OFF_GB), blk(0)


def _att_out_merge(att, w_att_out, proj, ret_out):
    def body(a_ref, b_ref, ga_ref, gb_ref, r_ref, o_ref, m_ref):
        acc = _dot(a_ref[...], b_ref[...], NN)
        o_ref[...] = acc
        m_ref[...] = (jax.nn.sigmoid(ga_ref[...]) * r_ref[...] + jax.nn.sigmoid(gb_ref[...]) * acc).astype(BF16)

    ga, gb, tile = _merge_specs()
    return _pcall(body, name="att_out", grid=(S // MERGE_TM, D // HALF),
                  in_specs=[pl.BlockSpec((MERGE_TM, AW), lambda i, j: (i, 0)), pl.BlockSpec((AW, HALF), lambda i, j: (0, j)),
                            ga, gb, tile],
                  out_specs=(tile, tile),
                  out_shape=(jax.ShapeDtypeStruct((S, D), F32), jax.ShapeDtypeStruct((S, D), BF16)),
                  compiler_params=_params(("parallel", "parallel")))(att, w_att_out, proj, proj, ret_out)


def _dmerged_split(dmixo, w_o, proj, ret_out, att_out):
    def body(a_ref, b_ref, ga_ref, gb_ref, r_ref, at_ref, dr_ref, da_ref, dga_ref, dgb_ref):
        dm = _dot(a_ref[...], b_ref[...], NT)
        sa = jax.nn.sigmoid(ga_ref[...])
        sb = jax.nn.sigmoid(gb_ref[...])
        dr_ref[...] = (dm * sa).astype(BF16)
        da_ref[...] = (dm * sb).astype(BF16)
        dga_ref[...] = (dm * r_ref[...] * (sa * (1.0 - sa))).astype(BF16)
        dgb_ref[...] = (dm * at_ref[...] * (sb * (1.0 - sb))).astype(BF16)

    ga, gb, tile = _merge_specs()
    o = jax.ShapeDtypeStruct((S, D), BF16)
    return _pcall(body, name="dmerged", grid=(S // MERGE_TM, D // HALF),
                  in_specs=[pl.BlockSpec((MERGE_TM, D), lambda i, j: (i, 0)), pl.BlockSpec((HALF, D), lambda i, j: (j, 0)),
                            ga, gb, tile, tile],
                  out_specs=(tile,) * 4, out_shape=(o, o, o, o),
                  compiler_params=_params(("parallel", "parallel")))(dmixo, w_o, proj, proj, ret_out, att_out)


def _ret_tables():
    H, C = RET_HEADS, CHUNK
    log_g = jnp.log1p(-(2.0 ** (-5.0 - jnp.arange(H, dtype=F32))))
    idx = jnp.arange(C, dtype=F32)
    rel = idx[:, None] - idx[None, :]
    inner = jnp.where(rel >= 0, jnp.exp(log_g[:, None, None] * jnp.maximum(rel, 0.0)), 0.0)
    qd = jnp.exp(log_g[:, None] * (idx + 1.0))[:, :, None]
    kd = jnp.exp(log_g[:, None] * (C - 1.0 - idx))[:, :, None]
    cd = jnp.broadcast_to(jnp.exp(log_g * C)[:, None, None], (H, 1, 128))
    half = RET_DK // 2
    inv = 10000.0 ** (-jnp.arange(half, dtype=F32) / half)
    ang = jnp.arange(S, dtype=F32)[:, None] * inv[None, :]
    return inner, qd, kd, cd, jnp.cos(ang), jnp.sin(ang)


def _rot(x, cos, sin):
    x1, x2 = x[:, :128], x[:, 128:]
    return jnp.concatenate([x1 * cos - x2 * sin, x1 * sin + x2 * cos], axis=1)


def _rot_t(d, cos, sin):
    d1, d2 = d[:, :128], d[:, 128:]
    return jnp.concatenate([d1 * cos + d2 * sin, d2 * cos - d1 * sin], axis=1)


RET_COLS = OFF_ATT
RET_VW = RET_HEADS * RET_DV


def _ret_specs(chunk_of):
    ci = chunk_of
    whole = lambda shape: pl.BlockSpec(shape, lambda t: (0,) * len(shape))
    return [
        pl.BlockSpec((CHUNK, RET_COLS), lambda t: (ci(t), 0)),
        pl.BlockSpec((CHUNK, 128), lambda t: (ci(t), 0)),
        pl.BlockSpec((CHUNK, 128), lambda t: (ci(t), 0)),
        whole((RET_HEADS, CHUNK, CHUNK)), whole((RET_HEADS, CHUNK, 1)), whole((RET_HEADS, CHUNK, 1)),
        whole((RET_HEADS, 1, 128)), whole((1, RET_VW)), whole((1, RET_VW)),
    ]


def _ret_cols(h):
    q = slice(OFF_RQ + h * RET_DK, OFF_RQ + (h + 1) * RET_DK)
    k = slice(OFF_RK + h * RET_DK, OFF_RK + (h + 1) * RET_DK)
    v = slice(OFF_RV + h * RET_DV, OFF_RV + (h + 1) * RET_DV)
    g = slice(OFF_RG + h * RET_DV, OFF_RG + (h + 1) * RET_DV)
    return q, k, v, g, slice(h * RET_DV, (h + 1) * RET_DV)


def _ret_fwd(proj, tables, gn_g, gn_b, after):
    inner, qd, kd, cd, cos, sin = tables

    def body(x_ref, cos_ref, sin_ref, in_ref, qd_ref, kd_ref, cd_ref, g_ref, b_ref, after_ref,
             gated_ref, ro_ref, st_ref, s_scr):
        i = pl.program_id(0)

        @pl.when(i == 0)
        def _():
            s_scr[...] = jnp.zeros_like(s_scr)

        cosv, sinv = cos_ref[...], sin_ref[...]
        for h in range(RET_HEADS):
            cq, ck, cv, cg, co = _ret_cols(h)
            q = _rot(x_ref[:, cq], cosv, sinv)
            k = _rot(x_ref[:, ck], cosv, sinv) * (RET_DK ** -0.5)
            v = x_ref[:, cv]
            st = s_scr[h]
            st_ref[h] = st.astype(BF16)
            s = _dot(q, k, NT) * in_ref[h]
            o = _dot(s, v, NN) + _dot(q, st, NN) * qd_ref[h]
            s_scr[h] = st * cd_ref[h, :, :1] + _dot(k * kd_ref[h], v, TN)
            ro_ref[:, co] = o
            mu = jnp.mean(o, axis=-1, keepdims=True)
            oc = o - mu
            var = jnp.mean(oc * oc, axis=-1, keepdims=True)
            rn = oc * lax.rsqrt(var + GN_EPS) * g_ref[:, co] + b_ref[:, co]
            rg = x_ref[:, cg]
            gated_ref[:, co] = (rg * jax.nn.sigmoid(rg) * rn).astype(BF16)

    ospec = pl.BlockSpec((CHUNK, RET_VW), lambda t: (t, 0))
    return _pcall(
        body, name="ret_fwd", grid=(N_CHUNK,), in_specs=_ret_specs(lambda t: t) + [HBM_SPEC],
        out_specs=(ospec, ospec, pl.BlockSpec((RET_HEADS, None, RET_DK, RET_DV), lambda t: (0, t, 0, 0))),
        out_shape=(jax.ShapeDtypeStruct((S, RET_VW), BF16), jax.ShapeDtypeStruct((S, RET_VW), F32),
                   jax.ShapeDtypeStruct((RET_HEADS, N_CHUNK, RET_DK, RET_DV), BF16)),
        scratch_shapes=[pltpu.VMEM((RET_HEADS, RET_DK, RET_DV), F32)],
        compiler_params=_params(("arbitrary",)))(proj, cos, sin, inner, qd, kd, cd, gn_g, gn_b, after)


def _ret_bwd(proj, tables, gn_g, gn_b, ro, states, dgated, others):
    inner, qd, kd, cd, cos, sin = tables
    last = N_CHUNK - 1
    assert RET_COLS + sum(o.shape[1] for o in others) == IN_COLS

    def body(x_ref, cos_ref, sin_ref, in_ref, qd_ref, kd_ref, cd_ref, g_ref, b_ref, ro_ref, st_ref, dg_ref, *rest):
        other_refs, (dx_ref, gg_ref, gb_ref, gs_scr) = rest[:len(others)], rest[len(others):]
        t = pl.program_id(0)
        col = RET_COLS
        for o_ref in other_refs:
            dx_ref[:, col:col + o_ref.shape[1]] = o_ref[...]
            col += o_ref.shape[1]

        @pl.when(t == 0)
        def _():
            gs_scr[...] = jnp.zeros_like(gs_scr)
            gg_ref[...] = jnp.zeros_like(gg_ref)
            gb_ref[...] = jnp.zeros_like(gb_ref)

        cosv, sinv = cos_ref[...], sin_ref[...]
        for h in range(RET_HEADS):
            cq, ck, cv, cg, co = _ret_cols(h)
            q = _rot(x_ref[:, cq], cosv, sinv)
            k = _rot(x_ref[:, ck], cosv, sinv) * (RET_DK ** -0.5)
            v = x_ref[:, cv]
            qdv, kdv, dm = qd_ref[h], kd_ref[h], in_ref[h]
            st = st_ref[h]
            o = ro_ref[:, co]
            gv = g_ref[:, co]
            mu = jnp.mean(o, axis=-1, keepdims=True)
            oc = o - mu
            rstd = lax.rsqrt(jnp.mean(oc * oc, axis=-1, keepdims=True) + GN_EPS)
            ohat = oc * rstd
            rn = ohat * gv + b_ref[:, co]
            rg = x_ref[:, cg]
            sg = jax.nn.sigmoid(rg)
            dgt = dg_ref[:, co]
            drn = dgt * (rg * sg)
            dx_ref[:, cg] = (dgt * rn * (sg * (1.0 + rg * (1.0 - sg)))).astype(BF16)
            gg_ref[:, co] += jnp.sum(drn * ohat, axis=0, keepdims=True)
            gb_ref[:, co] += jnp.sum(drn, axis=0, keepdims=True)
            dohat = drn * gv
            do = rstd * (dohat - jnp.mean(dohat, axis=-1, keepdims=True)
                         - ohat * jnp.mean(dohat * ohat, axis=-1, keepdims=True))
            gs = gs_scr[h]
            s = _dot(q, k, NT) * dm
            dsr = _dot(do, v, NT) * dm
            dq = _dot(dsr, k, NN) + _dot(do, st, NT) * qdv
            dk = _dot(dsr, q, TN) + _dot(v, gs, NT) * kdv
            dv = _dot(s, do, TN) + _dot(k * kdv, gs, NN)
            gs_scr[h] = gs * cd_ref[h, :, :1] + _dot(q * qdv, do, TN)
            dx_ref[:, cq] = _rot_t(dq, cosv, sinv).astype(BF16)
            dx_ref[:, ck] = (_rot_t(dk, cosv, sinv) * (RET_DK ** -0.5)).astype(BF16)
            dx_ref[:, cv] = dv.astype(BF16)

    rev = lambda t: last - t
    vblk = pl.BlockSpec((CHUNK, RET_VW), lambda t: (rev(t), 0))
    vspec = pl.BlockSpec((1, RET_VW), lambda t: (0, 0))
    rows = lambda w: pl.BlockSpec((CHUNK, w), lambda t: (rev(t), 0))
    return _pcall(
        body, name="ret_bwd", grid=(N_CHUNK,),
        in_specs=_ret_specs(rev) + [vblk, pl.BlockSpec((RET_HEADS, None, RET_DK, RET_DV), lambda t: (0, rev(t), 0, 0)),
                                    vblk] + [rows(o.shape[1]) for o in others],
        out_specs=(rows(IN_COLS), vspec, vspec),
        out_shape=(jax.ShapeDtypeStruct((S, IN_COLS), BF16), jax.ShapeDtypeStruct((1, RET_VW), F32),
                   jax.ShapeDtypeStruct((1, RET_VW), F32)),
        scratch_shapes=[pltpu.VMEM((RET_HEADS, RET_DK, RET_DV), F32)],
        compiler_params=_params(("arbitrary",)))(proj, cos, sin, inner, qd, kd, cd, gn_g, gn_b, ro, states, dgated,
                                                 *others)


def _bucket_tables():
    qi = np.arange(ATT_BLK)[:, None]
    kj = np.arange(2 * ATT_BLK)[None, :]
    m = ATT_BLK + qi - kj
    out = []
    for win, dil in ATT_GROUPS:
        w = win // dil
        dist = (np.clip(m, 0, w) * dil).astype(np.int32)
        max_exact = N_BUCKETS // 2
        d_f = np.maximum(dist, 1).astype(np.float32)
        large = max_exact + (np.log(d_f / np.float32(max_exact)) / np.float32(math.log(MAX_DIST / max_exact))
                             * np.float32(N_BUCKETS - max_exact)).astype(np.int32)
        large = np.minimum(large, N_BUCKETS - 1)
        out.append(np.where(dist < max_exact, dist, large).astype(np.int32))
    return np.stack(out)


def _bias_build(rel_bias, buckets, after):
    def body(tab_ref, bk_ref, after_ref, o_ref):
        hh = pl.program_id(0)
        bk = bk_ref[...]
        acc = jnp.zeros((ATT_BLK, 2 * ATT_BLK), F32)
        for b in range(N_BUCKETS):
            acc = jnp.where(bk == b, tab_ref[b, hh], acc)
        o_ref[...] = acc

    nh = len(ATT_GROUPS) * ATT_HG
    return _pcall(body, name="bias_build", grid=(nh,),
                  in_specs=[pl.BlockSpec(memory_space=pltpu.SMEM),
                            pl.BlockSpec((None, ATT_BLK, 2 * ATT_BLK), lambda hh: (hh // ATT_HG, 0, 0)), HBM_SPEC],
                  out_specs=pl.BlockSpec((None, ATT_BLK, 2 * ATT_BLK), lambda hh: (hh, 0, 0)),
                  out_shape=jax.ShapeDtypeStruct((nh, ATT_BLK, 2 * ATT_BLK), F32),
                  compiler_params=_params(("parallel",)))(rel_bias, buckets, after)


def _bias_grad(ds_sum, buckets):
    def body(ds_ref, bk_ref, o_ref):
        bk = bk_ref[...]
        ds = ds_ref[...]
        rows = lax.broadcasted_iota(jnp.int32, (N_BUCKETS, 128), 0)
        acc = jnp.zeros((N_BUCKETS, 128), F32)
        for b in range(N_BUCKETS):
            acc = jnp.where(rows == b, jnp.sum(jnp.where(bk == b, ds, 0.0)), acc)
        o_ref[...] = acc

    nh = len(ATT_GROUPS) * ATT_HG
    return _pcall(body, name="bias_grad", grid=(nh,),
                  in_specs=[pl.BlockSpec((None, ATT_BLK, 2 * ATT_BLK), lambda hh: (hh, 0, 0)),
                            pl.BlockSpec((None, ATT_BLK, 2 * ATT_BLK), lambda hh: (hh // ATT_HG, 0, 0))],
                  out_specs=pl.BlockSpec((None, N_BUCKETS, 128), lambda hh: (hh, 0, 0)),
                  out_shape=jax.ShapeDtypeStruct((nh, N_BUCKETS, 128), F32),
                  compiler_params=_params(("parallel",)))(ds_sum, buckets)


def _att_valid(n):
    qi = lax.broadcasted_iota(jnp.int32, (ATT_BLK, 2 * ATT_BLK), 0)
    kj = lax.broadcasted_iota(jnp.int32, (ATT_BLK, 2 * ATT_BLK), 1)
    m = ATT_BLK + qi - kj
    first_key = jnp.where(n > 0, 0, ATT_BLK)
    return (m >= 0) & (m <= ATT_BLK) & (kj >= first_key)


ATT_HP = (1, 2, 2)


def _att_geometry(gi):
    _, dil = ATT_GROUPS[gi]
    return dil, S // dil // ATT_BLK, ATT_HP[gi]


def _blk(dil, r, n):
    if dil == 1:
        return pl.ds(n * ATT_BLK, ATT_BLK)
    return pl.ds(r + n * ATT_BLK * dil, ATT_BLK, stride=dil)


def _slab_specs(gi):
    _, _, hp = _att_geometry(gi)
    per = ATT_HG // hp
    return [pl.BlockSpec((hp, S, ATT_DH), lambda g, r, part=part: ((3 * gi + part) * per + g, 0, 0))
            for part in range(3)]


def _head_specs(gi, count):
    _, _, hp = _att_geometry(gi)
    return [pl.BlockSpec((hp, S, ATT_DH), lambda g, r: (g, 0, 0))] * count


def _bias_spec(gi):
    _, _, hp = _att_geometry(gi)
    return pl.BlockSpec((hp, ATT_BLK, 2 * ATT_BLK), lambda g, r: (gi * (ATT_HG // hp) + g, 0, 0))


def _att_valid_first():
    qi = lax.broadcasted_iota(jnp.int32, (ATT_BLK, ATT_BLK), 0)
    kj = lax.broadcasted_iota(jnp.int32, (ATT_BLK, ATT_BLK), 1)
    return kj <= qi


def _att_fwd(slabs, bias, gi, comm=None):
    dil, nb, hp = _att_geometry(gi)
    scale = ATT_DH ** -0.5

    def body(q_ref, k_ref, v_ref, bias_ref, o_ref, l_ref):
        r = pl.program_id(1)
        for n in range(nb):
            cur = _blk(dil, r, n)
            valid = _att_valid(n) if n > 0 else _att_valid_first()
            for h in range(hp):
                if n > 0:
                    prev = _blk(dil, r, n - 1)
                    kk = jnp.concatenate([k_ref[h, prev, :], k_ref[h, cur, :]], axis=0)
                    vv = jnp.concatenate([v_ref[h, prev, :], v_ref[h, cur, :]], axis=0)
                    bias = bias_ref[h]
                else:
                    kk, vv, bias = k_ref[h, cur, :], v_ref[h, cur, :], bias_ref[h, :, pl.ds(ATT_BLK, ATT_BLK)]
                s = _dot(q_ref[h, cur, :], kk, NT) * scale + bias
                s = jnp.where(valid, s, -1e30)
                mx = jnp.max(s, axis=-1, keepdims=True)
                e = jnp.exp(s - mx)
                den = jnp.sum(e, axis=-1, keepdims=True)
                o_ref[h, cur, :] = _dot(e / den, vv, NN)
                l_ref[h, cur, :] = jnp.broadcast_to(mx + jnp.log(den), (ATT_BLK, ATT_DH))

    osh = jax.ShapeDtypeStruct((ATT_HG, S, ATT_DH), F32)
    kw = dict(name=f"att_fwd{gi}", grid=(ATT_HG // hp, dil), in_specs=_slab_specs(gi) + [_bias_spec(gi)],
              out_specs=tuple(_head_specs(gi, 2)), out_shape=(osh, osh))
    if comm is not None:
        return _carry(body, comm, **kw)(slabs, slabs, slabs, bias)
    return _pcall(body, compiler_params=_params(("parallel", "arbitrary")), **kw)(slabs, slabs, slabs, bias)


def _att_bwd(slabs, bias, o, lse, do, dlse, gi, comm=None):
    dil, nb, hp = _att_geometry(gi)
    per = ATT_HG // hp
    scale = ATT_DH ** -0.5
    wh = hp * ATT_DH
    wide = lambda t: jnp.concatenate([t, t], axis=1)

    def body(q_ref, k_ref, v_ref, bias_ref, o_ref, l_ref, do_ref, dl_ref, dq_ref, dk_ref, dv_ref, ds_ref):
        r = pl.program_id(1)

        @pl.when(r == 0)
        def _():
            ds_ref[...] = jnp.zeros_like(ds_ref)

        for h in range(hp):
            sl = slice(h * ATT_DH, (h + 1) * ATT_DH)
            carry_k = carry_v = None
            for n in range(nb):
                cur = _blk(dil, r, n)
                q = q_ref[h, cur, :]
                dov = do_ref[h, cur, :]
                delta = jnp.sum(dov * o_ref[h, cur, :], axis=-1, keepdims=True)
                out_rows = pl.ds(n * ATT_BLK, ATT_BLK)
                if n == 0:
                    own = pl.ds(ATT_BLK, ATT_BLK)
                    kk, vv = k_ref[h, cur, :], v_ref[h, cur, :]
                    s = _dot(q, kk, NT) * scale + bias_ref[h, :, own]
                    p = jnp.where(_att_valid_first(), jnp.exp(s - l_ref[h, cur, :]), 0.0)
                    ds = p * (_dot(dov, vv, NT) - delta + dl_ref[h, cur, :])
                    ds_ref[h, :, own] += ds
                    dq_ref[out_rows, sl] = (_dot(ds, kk, NN) * scale).astype(BF16)
                    carry_k, carry_v = _dot(ds, q, TN) * scale, _dot(p, dov, TN)
                    continue
                prev = _blk(dil, r, n - 1)
                kk = jnp.concatenate([k_ref[h, prev, :], k_ref[h, cur, :]], axis=0)
                vv = jnp.concatenate([v_ref[h, prev, :], v_ref[h, cur, :]], axis=0)
                s = _dot(q, kk, NT) * scale + bias_ref[h]
                p = jnp.where(_att_valid(n), jnp.exp(s - wide(l_ref[h, cur, :])), 0.0)
                dp = _dot(dov, vv, NT)
                ds = p * (dp - delta + wide(dl_ref[h, cur, :]))
                ds_ref[h] += ds
                dq_ref[out_rows, sl] = (_dot(ds, kk, NN) * scale).astype(BF16)
                dkk = _dot(ds, q, TN) * scale
                dvv = _dot(p, dov, TN)
                before = pl.ds((n - 1) * ATT_BLK, ATT_BLK)
                dk_ref[before, sl] = (carry_k + dkk[:ATT_BLK]).astype(BF16)
                dv_ref[before, sl] = (carry_v + dvv[:ATT_BLK]).astype(BF16)
                carry_k, carry_v = dkk[ATT_BLK:], dvv[ATT_BLK:]
            last = pl.ds((nb - 1) * ATT_BLK, ATT_BLK)
            dk_ref[last, sl] = carry_k.astype(BF16)
            dv_ref[last, sl] = carry_v.astype(BF16)

    out_spec = pl.BlockSpec((S // dil, wh), lambda g, r: (0, r * per + g))
    osh = jax.ShapeDtypeStruct((S // dil, dil * AW), BF16)
    kw = dict(name=f"att_bwd{gi}", grid=(per, dil), in_specs=_slab_specs(gi) + [_bias_spec(gi)] + _head_specs(gi, 4),
              out_specs=(out_spec, out_spec, out_spec,
                         pl.BlockSpec((hp, ATT_BLK, 2 * ATT_BLK), lambda g, r: (g, 0, 0))),
              out_shape=(osh, osh, osh, jax.ShapeDtypeStruct((ATT_HG, ATT_BLK, 2 * ATT_BLK), F32)))
    args = (slabs, slabs, slabs, bias, o, lse, do, dlse)
    if comm is not None:
        return _carry(body, comm, **kw)(*args)
    return _pcall(body, compiler_params=_params(("arbitrary", "arbitrary")), **kw)(*args)


AW = ATT_HG * ATT_DH


def _mix_weights(l0, l1, l2):
    mx = jnp.maximum(jnp.maximum(l0, l1), l2)
    e0, e1, e2 = jnp.exp(l0 - mx), jnp.exp(l1 - mx), jnp.exp(l2 - mx)
    den = e0 + e1 + e2
    return e0 / den, e1 / den, e2 / den


def _heads_spec():
    return pl.BlockSpec((ATT_HG, TR, ATT_DH), lambda i: (0, i, 0))


def _mix_fwd(os_, ls, comm=None):
    def body(o0, o1, o2, l0, l1, l2, att_ref):
        for h in range(ATT_HG):
            w0, w1, w2 = _mix_weights(l0[h], l1[h], l2[h])
            att_ref[:, h * ATT_DH:(h + 1) * ATT_DH] = (w0 * o0[h] + w1 * o1[h] + w2 * o2[h]).astype(BF16)

    kw = dict(name="mix_fwd", grid=(S // TR,), in_specs=[_heads_spec()] * 6, out_specs=_row_spec(AW),
              out_shape=jax.ShapeDtypeStruct((S, AW), BF16))
    if comm is not None:
        return _carry(body, comm, **kw)(*os_, *ls)
    return _pcall(body, compiler_params=_params(("parallel",)), **kw)(*os_, *ls)


def _mix_bwd(os_, ls, datt):
    def body(o0, o1, o2, l0, l1, l2, da_ref, d0, d1, d2, e0, e1, e2):
        for h in range(ATT_HG):
            ws = _mix_weights(l0[h], l1[h], l2[h])
            da = da_ref[:, h * ATT_DH:(h + 1) * ATT_DH]
            dws = []
            for o_ref, w, d_ref in zip((o0, o1, o2), ws, (d0, d1, d2)):
                d_ref[h] = w * da
                dws.append(jnp.broadcast_to(jnp.sum(da * o_ref[h], axis=-1, keepdims=True), (TR, ATT_DH)))
            tot = ws[0] * dws[0] + ws[1] * dws[1] + ws[2] * dws[2]
            for w, dw, e_ref in zip(ws, dws, (e0, e1, e2)):
                e_ref[h] = w * (dw - tot)

    o = jax.ShapeDtypeStruct((ATT_HG, S, ATT_DH), F32)
    return _pcall(body, name="mix_bwd", grid=(S // TR,), in_specs=[_heads_spec()] * 6 + [_row_spec(AW)],
                  out_specs=(_heads_spec(),) * 6, out_shape=(o,) * 6,
                  compiler_params=_params(("parallel",)))(*os_, *ls, datt)


def _ada_fwd(c_all, w_sh, b_sl):
    def body(c_ref, w_ref, b_ref, o_ref):
        cv = c_ref[...]
        o_ref[...] = _dot(cv * jax.nn.sigmoid(cv), w_ref[...], NN) + b_ref[...]

    return _pcall(body, name="ada_fwd", out_shape=jax.ShapeDtypeStruct((N_DEV, w_sh.shape[1]), F32),
                  compiler_params=_params())(c_all, w_sh, b_sl)


def _ada_bwd(c_all, dm_sl):
    def body(c_ref, d_ref, o_ref):
        cv = c_ref[...]
        o_ref[...] = _dot(cv * jax.nn.sigmoid(cv), d_ref[...], TN)

    return _pcall(body, name="ada_bwd", out_shape=jax.ShapeDtypeStruct((D, dm_sl.shape[1]), F32),
                  compiler_params=_params())(c_all, dm_sl)


N_MOD = 6


def _sum_small(gathered):
    n = len(gathered)

    def body(*refs):
        ins, (gb_ref, dm_ref), outs = refs[:n], refs[n:n + 2], refs[n + 2:]

        def total(r):
            acc = r[0]
            for e in range(1, N_DEV):
                acc = acc + r[e]
            return acc

        for i in range(N_MOD):
            cols = slice(i * D, (i + 1) * D)
            gb_ref[:, cols] = total(ins[i])
            for e in range(N_DEV):
                dm_ref[e:e + 1, cols] = ins[i][e]
        for r, o_ref in zip(ins[N_MOD:], outs):
            o_ref[...] = total(r)

    shapes = (jax.ShapeDtypeStruct((1, N_MOD * D), F32), jax.ShapeDtypeStruct((N_DEV, N_MOD * D), F32),
              *[jax.ShapeDtypeStruct(g.shape[1:], F32) for g in gathered[N_MOD:]])
    res = _pcall(body, name="sum_small", out_shape=shapes, compiler_params=_params())(*gathered)
    return res[0], res[1], res[2:]


def _row_tile(m, n):
    t = max(8, min(m, (1 << 19) // n // 8 * 8))
    while m % t:
        t -= 8
    return t


def _pair_sum(full, recv, sel, name, col_block=0):
    _, m, n = recv.shape
    t = _row_tile(m, n)

    def body(sel_ref, a_ref, b_ref, o_ref):
        o_ref[...] = (a_ref[...].astype(F32) + b_ref[...].astype(F32)).astype(o_ref.dtype)

    gs = pltpu.PrefetchScalarGridSpec(
        num_scalar_prefetch=1, grid=(4, m // t),
        in_specs=[pl.BlockSpec((None, None, t, n), lambda q, i, s: (q, s[0], i, col_block)),
                  pl.BlockSpec((None, t, n), lambda q, i, s: (q, i, 0))],
        out_specs=pl.BlockSpec((None, t, n), lambda q, i, s: (q, i, 0)))
    return _pcall(body, name=name, grid_spec=gs, out_shape=jax.ShapeDtypeStruct((4, m, n), full.dtype),
                  compiler_params=_params(("parallel", "parallel")))(sel, full, recv)


def _chip_sum(part, recv, sel, name):
    _, m, n = part.shape
    t = _row_tile(m, n)

    def body(sel_ref, a_ref, r_ref, o_ref):
        o_ref[...] = ((a_ref[...].astype(F32) + r_ref[0].astype(F32)) + r_ref[1].astype(F32)) + r_ref[2].astype(F32)

    gs = pltpu.PrefetchScalarGridSpec(
        num_scalar_prefetch=1, grid=(m // t,),
        in_specs=[pl.BlockSpec((None, t, n), lambda i, s: (s[0], i, 0)),
                  pl.BlockSpec((3, t, n), lambda i, s: (0, i, 0))],
        out_specs=pl.BlockSpec((t, n), lambda i, s: (i, 0)))
    return _pcall(body, name=name, grid_spec=gs, out_shape=jax.ShapeDtypeStruct((m, n), F32),
                  compiler_params=_params(("parallel",)))(sel, part, recv)


def _adamw_math(w, g, m, v):
    nm = ADAM_B1 * m + (1.0 - ADAM_B1) * g
    nv = ADAM_B2 * v + (1.0 - ADAM_B2) * (g * g)
    m_hat = nm / (1.0 - ADAM_B1 ** ADAM_STEP)
    v_hat = nv / (1.0 - ADAM_B2 ** ADAM_STEP)
    return -ADAM_LR * (m_hat / (jnp.sqrt(v_hat) + ADAM_EPS) + ADAM_WD * w), nm, nv


def _adamw(w, g, m, v, name):
    _, rows, cols = w.shape
    t = _row_tile(rows, cols)

    def body(w_ref, g_ref, m_ref, v_ref, d_ref, nm_ref, nv_ref):
        d_ref[...], nm_ref[...], nv_ref[...] = _adamw_math(w_ref[...], g_ref[...], m_ref[...], v_ref[...])

    spec3 = pl.BlockSpec((None, t, cols), lambda i: (0, i, 0))
    spec2 = pl.BlockSpec((t, cols), lambda i: (i, 0))
    o = jax.ShapeDtypeStruct(w.shape, F32)
    return _pcall(body, name=name, grid=(rows // t,), in_specs=[spec3, spec2, spec3, spec3], out_specs=(spec3,) * 3,
                  out_shape=(o, o, o), compiler_params=_params(("parallel",)))(w, g, m, v)


def _adamw_reduced1(w, m, v, part, recv, sel, name):
    _, rows, cols = w.shape
    t = _row_tile(rows, cols)

    def body(sel_ref, w_ref, m_ref, v_ref, p_ref, r_ref, g_ref, d_ref, nm_ref, nv_ref):
        g = ((p_ref[...].astype(F32) + r_ref[0].astype(F32)) + r_ref[1].astype(F32)) + r_ref[2].astype(F32)
        g_ref[...] = g
        d_ref[...], nm_ref[...], nv_ref[...] = _adamw_math(w_ref[...], g, m_ref[...], v_ref[...])

    wspec = pl.BlockSpec((None, t, cols), lambda i, s: (0, i, 0))
    gs = pltpu.PrefetchScalarGridSpec(
        num_scalar_prefetch=1, grid=(rows // t,),
        in_specs=[wspec, wspec, wspec, pl.BlockSpec((None, t, cols), lambda i, s: (s[0], i, 0)),
                  pl.BlockSpec((3, t, cols), lambda i, s: (0, i, 0))],
        out_specs=(wspec,) * 4)
    o = jax.ShapeDtypeStruct(w.shape, F32)
    return _pcall(body, name=name, grid_spec=gs, out_shape=(o, o, o, o),
                  compiler_params=_params(("parallel",)))(sel, w, m, v, part, recv)


def _adamw_reduced(w, m, v, parts, recvs, sel):
    _, rows, cols = w.shape
    half = cols // 2
    t = _row_tile(rows, half)

    def body(sel_ref, w_ref, m_ref, v_ref, pa_ref, pb_ref, ra_ref, rb_ref, g_ref, d_ref, nm_ref, nv_ref):
        total = lambda p_ref, r_ref: ((p_ref[...].astype(F32) + r_ref[0].astype(F32)) + r_ref[1].astype(F32)) \
            + r_ref[2].astype(F32)
        g = jnp.where(pl.program_id(1) == 0, total(pa_ref, ra_ref), total(pb_ref, rb_ref))
        g_ref[...] = g
        d_ref[...], nm_ref[...], nv_ref[...] = _adamw_math(w_ref[...], g, m_ref[...], v_ref[...])

    wspec = pl.BlockSpec((None, t, half), lambda i, j, s: (0, i, j))
    pspec = pl.BlockSpec((None, t, half), lambda i, j, s: (s[0], i, 0))
    rspec = pl.BlockSpec((3, t, half), lambda i, j, s: (0, i, 0))
    gs = pltpu.PrefetchScalarGridSpec(num_scalar_prefetch=1, grid=(rows // t, 2),
                                      in_specs=[wspec, wspec, wspec, pspec, pspec, rspec, rspec],
                                      out_specs=(wspec,) * 4)
    o = jax.ShapeDtypeStruct(w.shape, F32)
    return _pcall(body, name="adamw_w_in", grid_spec=gs, out_shape=(o, o, o, o),
                  compiler_params=_params(("parallel", "arbitrary")))(sel, w, m, v, *parts, *recvs)


def _adamw_small(ws, gs, ms, vs):
    n = len(ws)

    def body(*refs):
        for i in range(n):
            w_ref, g_ref, m_ref, v_ref = (refs[k * n + i] for k in range(4))
            d, nm, nv = _adamw_math(w_ref[...], g_ref[...], m_ref[...], v_ref[...])
            refs[4 * n + i][...] = d
            refs[5 * n + i][...] = nm
            refs[6 * n + i][...] = nv

    shapes = tuple(jax.ShapeDtypeStruct(w.shape, F32) for w in ws)
    res = _pcall(body, name="adamw_small", out_shape=shapes * 3, compiler_params=_params())(*ws, *gs, *ms, *vs)
    return res[:n], res[n:2 * n], res[2 * n:]


def _mesh_pos():
    return lax.axis_index("x"), lax.axis_index("y"), lax.axis_index("c")


class _Gather:
    def __init__(self, arrs):
        self.ins = list(arrs)
        self.out_shape = tuple(jax.ShapeDtypeStruct((N_DEV,) + a.shape, a.dtype) for a in arrs)
        n = len(arrs)
        self.sems = [pltpu.SemaphoreType.DMA((7 * n,)), pltpu.SemaphoreType.DMA((7 * n,)),
                     pltpu.SemaphoreType.DMA((n,))]

    def _copies(self, ins, outs, sems):
        send_sems, recv_sems, local_sems = sems
        x, y, c = _mesh_pos()
        me, sibling = (x, y, c), (x, y, 1 - c)
        chips = [(1 - x, y), (x, 1 - y), (1 - x, 1 - y)]

        def copy(p, k, block, to, from_input=False):
            dst = outs[p].at[_slot(block)]
            return pltpu.make_async_remote_copy(
                src_ref=ins[p] if from_input else dst, dst_ref=dst, send_sem=send_sems.at[7 * p + k],
                recv_sem=recv_sems.at[7 * p + k], device_id=to, device_id_type=MESH)

        npc = len(self.ins)
        mine = [pltpu.make_async_copy(ins[p], outs[p].at[_slot(me)], local_sems.at[p]) for p in range(npc)]
        first = []
        for p in range(npc):
            first.append(copy(p, 0, me, sibling, from_input=True))
            first += [copy(p, 1 + j, me, (*chip, c), from_input=True) for j, chip in enumerate(chips)]
        return me, sibling, chips, c, copy, mine, first

    def start(self, ins, outs, sems):
        *_, mine, first = self._copies(ins, outs, sems)
        for cp in mine + first:
            cp.start()

    def finish(self, ins, outs, sems):
        me, sibling, chips, c, copy, mine, first = self._copies(ins, outs, sems)
        npc = len(self.ins)
        passed = []
        for p in range(npc):
            for j, chip in enumerate(chips):
                copy(p, 1 + j, (*chip, c), me).wait_recv()
                passed.append(copy(p, 4 + j, (*chip, c), sibling))
                passed[-1].start()
        for p in range(npc):
            copy(p, 0, sibling, me).wait_recv()
            for j, chip in enumerate(chips):
                copy(p, 4 + j, (*chip, 1 - c), me).wait_recv()
        for cp in first + passed:
            cp.wait_send()
        for cp in mine:
            cp.wait()


class _ExchangeCore:
    def __init__(self, fulls, cols=None):
        self.ins = list(fulls)
        self.cols = cols
        width = lambda f: f.shape[3] if cols is None else cols[1]
        self.out_shape = tuple(jax.ShapeDtypeStruct((4, f.shape[2], width(f)), f.dtype) for f in fulls)
        self.sems = [pltpu.SemaphoreType.DMA((4 * len(fulls),)), pltpu.SemaphoreType.DMA((4 * len(fulls),))]

    def _copies(self, ins, outs, sems):
        send_sems, recv_sems = sems
        x, y, c = _mesh_pos()

        def src(a, q):
            ref = ins[a].at[q, 1 - c]
            return ref if self.cols is None else ref.at[:, pl.ds(*self.cols)]

        return [pltpu.make_async_remote_copy(
            src_ref=src(a, q), dst_ref=outs[a].at[q], send_sem=send_sems.at[4 * a + q],
            recv_sem=recv_sems.at[4 * a + q], device_id=(x, y, 1 - c), device_id_type=MESH)
            for a in range(len(self.ins)) for q in range(4)]

    def start(self, ins, outs, sems):
        for cp in self._copies(ins, outs, sems):
            cp.start()

    def finish(self, ins, outs, sems):
        for cp in self._copies(ins, outs, sems):
            cp.wait()


class _ExchangeChip:
    def __init__(self, parts):
        self.ins = list(parts)
        self.out_shape = tuple(jax.ShapeDtypeStruct((3,) + p.shape[1:], p.dtype) for p in parts)
        self.sems = [pltpu.SemaphoreType.DMA((3 * len(parts),)), pltpu.SemaphoreType.DMA((3 * len(parts),))]

    def _copies(self, ins, outs, sems):
        send_sems, recv_sems = sems
        x, y, c = _mesh_pos()
        chips = [(1 - x, y), (x, 1 - y), (1 - x, 1 - y)]
        return [pltpu.make_async_remote_copy(
            src_ref=ins[a].at[2 * px + py], dst_ref=outs[a].at[j], send_sem=send_sems.at[3 * a + j],
            recv_sem=recv_sems.at[3 * a + j], device_id=(px, py, c), device_id_type=MESH)
            for a in range(len(self.ins)) for j, (px, py) in enumerate(chips)]

    def start(self, ins, outs, sems):
        for cp in self._copies(ins, outs, sems):
            cp.start()

    def finish(self, ins, outs, sems):
        for cp in self._copies(ins, outs, sems):
            cp.wait()


HBM_ONLY = pl.BlockSpec(memory_space=pltpu.HBM)
SEM_SPEC = pl.BlockSpec(memory_space=pltpu.SEMAPHORE)
SIDE_EFFECT = pltpu.SideEffectType.DATAFLOW_SIDE_EFFECTING


def _chip_copies(p_refs, land_refs, send_sems, recv_sems):
    x, y, c = _mesh_pos()
    return [pltpu.make_async_remote_copy(
        src_ref=p_refs[a].at[2 * px + py], dst_ref=land_refs[a].at[j], send_sem=send_sems.at[3 * a + j],
        recv_sem=recv_sems.at[3 * a + j], device_id=(px, py, c), device_id_type=MESH)
        for a in range(len(p_refs)) for j, (px, py) in enumerate([(1 - x, y), (x, 1 - y), (1 - x, 1 - y)])]


def _chip_exchange_start(parts, name):
    n = len(parts)
    lands = [lax.empty((3,) + p.shape[1:], p.dtype) for p in parts]

    def body(*refs):
        p_refs, land_refs, (send_sems, recv_sems) = refs[:n], refs[n:2 * n], refs[2 * n:2 * n + 2]
        for cp in _chip_copies(p_refs, land_refs, send_sems, recv_sems):
            cp.start()
        token = refs[-1]
        token[...] = jnp.zeros_like(token)

    hbm = lambda t: pltpu.HBM(t.shape, t.dtype)
    res = pl.pallas_call(
        body, name=name,
        out_shape=(pltpu.SemaphoreType.DMA((3 * n,)), pltpu.SemaphoreType.DMA((3 * n,)), *[hbm(t) for t in parts + lands],
                   jax.ShapeDtypeStruct((8, 128), F32)),
        in_specs=(HBM_ONLY,) * (2 * n),
        out_specs=(SEM_SPEC, SEM_SPEC, *[HBM_ONLY] * (2 * n), pl.BlockSpec(memory_space=pltpu.VMEM)),
        input_output_aliases={i: 2 + i for i in range(2 * n)},
        compiler_params=pltpu.CompilerParams(has_side_effects=SIDE_EFFECT))(
        *[pltpu.with_memory_space_constraint(t, pltpu.HBM) for t in parts + lands])
    return (res[0], res[1], list(res[2:2 + n]), list(res[2 + n:2 + 2 * n])), res[-1]


def _chip_exchange_wait(in_flight, after, name):
    send_sems, recv_sems, parts, lands = in_flight
    n = len(parts)

    def body(*refs):
        p_refs, land_refs, (send_sems, recv_sems) = refs[:n], refs[n:2 * n], refs[2 * n:2 * n + 2]
        for cp in _chip_copies(p_refs, land_refs, send_sems, recv_sems):
            cp.wait_send()
            cp.wait_recv()

    res = pl.pallas_call(
        body, name=name, out_shape=tuple(pltpu.HBM(t.shape, t.dtype) for t in parts + lands),
        in_specs=(*[HBM_ONLY] * (2 * n), SEM_SPEC, SEM_SPEC, pl.BlockSpec(memory_space=pl.ANY)),
        out_specs=(HBM_ONLY,) * (2 * n), input_output_aliases={i: i for i in range(2 * n)},
        compiler_params=pltpu.CompilerParams(has_side_effects=SIDE_EFFECT))(*parts, *lands, send_sems, recv_sems, after)
    return list(res[:n]), list(res[n:])


def _slot(p):
    return 4 * p[0] + 2 * p[1] + p[2]


def _gather_copies(src_refs, out_refs, send_sems, recv_sems):
    x, y, c = _mesh_pos()
    targets = [(x, y, 1 - c), (1 - x, y, c), (x, 1 - y, c), (1 - x, 1 - y, c)]
    return [pltpu.make_async_remote_copy(
        src_ref=src_refs[a], dst_ref=out_refs[a].at[_slot((x, y, c))], send_sem=send_sems.at[4 * a + k],
        recv_sem=recv_sems.at[4 * a + k], device_id=to, device_id_type=MESH)
        for a in range(len(src_refs)) for k, to in enumerate(targets)]


def _gather_start(shards, after, name):
    n = len(shards)
    outs = [lax.empty((N_DEV,) + s.shape, s.dtype) for s in shards]

    def body(*refs):
        for cp in _gather_copies(refs[:n], refs[n:2 * n], refs[2 * n + 1], refs[2 * n + 2]):
            cp.start()
        token = refs[-1]
        token[...] = jnp.zeros_like(token)

    res = pl.pallas_call(
        body, name=name,
        out_shape=(pltpu.SemaphoreType.DMA((4 * n,)), pltpu.SemaphoreType.DMA((4 * n,)),
                   *[pltpu.HBM(t.shape, t.dtype) for t in shards + outs], jax.ShapeDtypeStruct((8, 128), F32)),
        in_specs=(*[HBM_ONLY] * (2 * n), pl.BlockSpec(memory_space=pl.ANY)),
        out_specs=(SEM_SPEC, SEM_SPEC, *[HBM_ONLY] * (2 * n), pl.BlockSpec(memory_space=pltpu.VMEM)),
        input_output_aliases={i: 2 + i for i in range(2 * n)},
        compiler_params=pltpu.CompilerParams(has_side_effects=SIDE_EFFECT))(
        *[pltpu.with_memory_space_constraint(t, pltpu.HBM) for t in shards + outs], after)
    return (res[0], res[1], list(res[2:2 + n]), list(res[2 + n:2 + 2 * n])), res[-1]


def _gather_wait(in_flight, after, name):
    send_sems, recv_sems, shards, outs = in_flight
    n = len(shards)

    def body(*refs):
        for cp in _gather_copies(refs[:n], refs[n:2 * n], refs[2 * n], refs[2 * n + 1]):
            cp.wait_send()
            cp.wait_recv()

    res = pl.pallas_call(
        body, name=name, out_shape=tuple(pltpu.HBM(t.shape, t.dtype) for t in shards + outs),
        in_specs=(*[HBM_ONLY] * (2 * n), SEM_SPEC, SEM_SPEC, pl.BlockSpec(memory_space=pl.ANY)),
        out_specs=(HBM_ONLY,) * (2 * n), input_output_aliases={i: i for i in range(2 * n)},
        compiler_params=pltpu.CompilerParams(has_side_effects=SIDE_EFFECT))(*shards, *outs, send_sems, recv_sems, after)
    return list(res[:n]), list(res[n:])


class _PassToSibling:
    def __init__(self, shards, gathered):
        n = self.n = len(shards)
        self.ins = list(shards) + list(gathered)
        self.out_shape = tuple(jax.ShapeDtypeStruct(g.shape, g.dtype) for g in gathered)
        self.aliases = {n + a: a for a in range(n)}
        self.sems = [pltpu.SemaphoreType.DMA((3 * n,)), pltpu.SemaphoreType.DMA((3 * n,)),
                     pltpu.SemaphoreType.DMA((n,))]

    def _copies(self, ins, outs, sems):
        send_sems, recv_sems, local_sems = sems
        x, y, c = _mesh_pos()
        chips = [(1 - x, y), (x, 1 - y), (1 - x, 1 - y)]
        mine = [pltpu.make_async_copy(ins[a], outs[a].at[_slot((x, y, c))], local_sems.at[a]) for a in range(self.n)]
        passed, awaited = [], []
        for a in range(self.n):
            for j, chip in enumerate(chips):
                sems_j = dict(send_sem=send_sems.at[3 * a + j], recv_sem=recv_sems.at[3 * a + j],
                              device_id=(x, y, 1 - c), device_id_type=MESH)
                blk = outs[a].at[_slot((*chip, c))]
                passed.append(pltpu.make_async_remote_copy(src_ref=blk, dst_ref=blk, **sems_j))
                got = outs[a].at[_slot((*chip, 1 - c))]
                awaited.append(pltpu.make_async_remote_copy(src_ref=got, dst_ref=got, **sems_j))
        return mine, passed, awaited

    def start(self, ins, outs, sems):
        mine, passed, _ = self._copies(ins, outs, sems)
        for cp in mine + passed:
            cp.start()

    def finish(self, ins, outs, sems):
        mine, passed, awaited = self._copies(ins, outs, sems)
        for cp in passed:
            cp.wait_send()
        for cp in awaited:
            cp.wait_recv()
        for cp in mine:
            cp.wait()


def _reduce_sums(fulls, recv_core, core, tag):
    return [_pair_sum(f, r, core, f"rs_pair_{tag}{i}") for i, (f, r) in enumerate(zip(fulls, recv_core))]


def _local_step(x, tgt, mods, w_in_shard, order, shards, small, chip, core):
    sh1, sc1, g1, sh2, sc2, g2 = mods
    norm1_g, rel_bias, gn_g, gn_b, norm2_g, norm_f_g = small
    tables = _ret_tables()
    buckets = jnp.asarray(_bucket_tables())

    h1 = _norm_mod_fwd(x, norm1_g, sh1, sc1, "norm1_fwd")
    proj, slabs, w_in_t = _gather_proj(h1, w_in_shard, order)
    flight_w1, token_w = _gather_start(list(shards[:3]), proj, "gather_w1_start")
    flight_w2, token_w = _gather_start(list(shards[3:]), token_w, "gather_w2_start")
    bias = _bias_build(rel_bias, buckets, token_w)
    outs, lses = [], []
    for gi in range(len(ATT_GROUPS)):
        o, l = _att_fwd(slabs, bias, gi)
        outs.append(o)
        lses.append(l)
    att, gathered = _mix_fwd(outs, lses, comm=_PassToSibling(*_gather_wait(flight_w1, lses[2], "gather_w1_wait")))
    w_ret_out, w_att_out, w_o = (_from_slots(g, ax) for g, ax in zip(gathered, BIG_AXES[1:4]))
    gated, ro, states = _ret_fwd(proj, tables, gn_g, gn_b, att)
    ret_out, gathered = _mm(gated, w_ret_out, 'nn', tm=S, tn=256, tk=2048, name="ret_out",
                            comm=_PassToSibling(*_gather_wait(flight_w2, gated, "gather_w2_wait")))
    w_ff1, w_ff2 = (_from_slots(g, ax) for g, ax in zip(gathered, BIG_AXES[4:]))
    att_out, merged = _att_out_merge(att, w_att_out, proj, ret_out)
    mixo, x1, h2 = _w_o_norm2(merged, w_o, x, g1, norm2_g, sh2, sc2)
    u, act = _mm(h2, w_ff1, 'nn', tm=S, tn=512, tk=D, name="ff1", relu2=True)
    loss, dx2, g_normf, df, dg2 = _ff2_final(act, w_ff2, x1, g2, tgt, norm_f_g)

    gw_ff2 = _mm(act, df, 'tn', tm=512, tn=D, tk=S, name="gw_ff2", out_dtype=BF16)
    du = _mm(df, w_ff2, 'nt', tm=S, tn=512, tk=D, name="d_act", out_dtype=BF16, relu2_of=u)
    gw_ff1 = _mm(h2, du, 'tn', tm=D, tn=512, tk=S, name="gw_ff1", out_dtype=BF16)
    fulls_a = [_to_slots(g, ax) for g, ax in zip((gw_ff1, gw_ff2), BIG_AXES[4:])]
    dh2, recv_core_a = _mm(du, w_ff1, 'nt', tm=1024, tn=1024, tk=2048, name="dh2", comm=_ExchangeCore(fulls_a))
    parts_a = _reduce_sums(fulls_a, recv_core_a, core, "a")
    flight_a, token_a = _chip_exchange_start(parts_a, "rs_a_start")
    dx1, dsc2, dsh2, g_norm2, dmixo, dg1 = _norm_mod_bwd(x1, norm2_g, sc2, dh2, dx2, "norm2_bwd", gate=(mixo, g1))

    gw_o = _mm(merged, dmixo, 'tn', tm=D, tn=512, tk=S, name="gw_o", out_dtype=BF16, after=token_a)
    d_ret_out, d_att_out, dga, dgb = _dmerged_split(dmixo, w_o, proj, ret_out, att_out)
    gw_ret_out = _mm(gated, d_ret_out, 'tn', tm=512, tn=D, tk=S, name="gw_ret_out", out_dtype=BF16)
    gw_att_out = _mm(att, d_att_out, 'tn', tm=AW, tn=D, tk=S, name="gw_att_out", out_dtype=BF16)
    fulls_b = [_to_slots(g, ax) for g, ax in zip((gw_ret_out, gw_att_out, gw_o), BIG_AXES[1:4])]
    dgated, recv_core_b = _mm(d_ret_out, w_ret_out, 'nt', tm=S, tn=512, tk=D, name="dgated",
                              comm=_ExchangeCore(fulls_b))
    parts_b = _reduce_sums(fulls_b, recv_core_b, core, "b")
    flight_b, token_b = _chip_exchange_start(parts_b, "rs_b_start")
    datt = _mm(d_att_out, w_att_out, 'nt', tm=S, tn=AW, tk=D, name="datt", after=token_b)
    mix_grads = _mix_bwd(outs, lses, datt)
    datt_parts, ds_sums = [], []
    for gi in range(len(ATT_GROUPS)):
        dq, dk, dv, ds_sum = _att_bwd(slabs, bias, outs[gi], lses[gi], mix_grads[gi], mix_grads[3 + gi], gi)
        datt_parts += [dq.reshape(S, AW), dk.reshape(S, AW), dv.reshape(S, AW)]
        ds_sums.append(ds_sum)
    g_bias = _bias_grad(jnp.concatenate(ds_sums, axis=0), buckets)[:, :, 0].T.reshape(1, -1)
    dproj, g_gn_g, g_gn_b = _ret_bwd(proj, tables, gn_g, gn_b, ro, states, dgated, datt_parts + [dga, dgb])
    parts_a, recv_chip_a = _chip_exchange_wait(flight_a, dproj, "rs_a_wait")
    parts_b, recv_chip_b = _chip_exchange_wait(flight_b, dproj, "rs_b_wait")
    reduced = list(zip(parts_b + parts_a, recv_chip_b + recv_chip_a))
    full_in = _to_slots(_mm(dproj, h1, 'tn', tm=512, tn=D, tk=S, name="gw_in", out_dtype=BF16), 0)
    halves = [(half * (D // 2), D // 2) for half in range(2)]
    (recv_core_in0,) = _run_comm(_ExchangeCore([full_in], cols=halves[0]), "rs_core_in0")
    flight0, token = _chip_exchange_start([_pair_sum(full_in, recv_core_in0, core, "rs_pair_c0", col_block=0)],
                                          "rs_in0_start")
    dh1, (recv_core_in1,) = _mm(dproj, w_in_t, 'nn', tm=1024, tn=1024, tk=2560, name="dh1",
                                comm=_ExchangeCore([full_in], cols=halves[1]), after=token)
    flight1, token = _chip_exchange_start([_pair_sum(full_in, recv_core_in1, core, "rs_pair_c1", col_block=1)],
                                          "rs_in1_start")
    in_flight = [flight0, flight1]
    gx, dsc1, dsh1, g_norm1 = _norm_mod_bwd(x, norm1_g, sc1, dh1, dx1, "norm1_bwd", after=token)

    dmod = [dsh1, dsc1, dg1, dsh2, dsc2, dg2]
    small_g = [g_norm1, g_bias, g_gn_g, g_gn_b, g_norm2, g_normf]
    return loss, gx, in_flight, reduced, small_g, dmod


def _to_slots(g, axis):
    if axis == 0:
        return g.reshape(4, 2, g.shape[0] // N_DEV, g.shape[1])
    return g.reshape(g.shape[0], N_DEV, g.shape[1] // N_DEV).transpose(1, 0, 2).reshape(4, 2, g.shape[0], -1)


def _from_slots(w8, axis):
    if axis == 0:
        return w8.reshape(-1, w8.shape[2])
    return w8.transpose(1, 0, 2).reshape(w8.shape[1], -1)


BIG_AXES = (1, 0, 1, 0, 1, 0)


def kernel(x, c, w_ada, b_ada, norm1_g, w_in, rel_bias, ret_gn_g, ret_gn_b, w_ret_out, w_att_out, w_o, norm2_g, w_ff1, w_ff2, norm_f_g, loss_target, m_w_ada, m_b_ada, m_norm1_g, m_w_in, m_rel_bias, m_ret_gn_g, m_ret_gn_b, m_w_ret_out, m_w_att_out, m_w_o, m_norm2_g, m_w_ff1, m_w_ff2, m_norm_f_g, v_w_ada, v_b_ada, v_norm1_g, v_w_in, v_rel_bias, v_ret_gn_g, v_ret_gn_b, v_w_ret_out, v_w_att_out, v_w_o, v_norm2_g, v_w_ff1, v_w_ff2, v_norm_f_g):
    mx, my, mc = _mesh_pos()
    dev = 4 * mx + 2 * my + mc
    chip = jnp.reshape(2 * mx + my, (1,)).astype(jnp.int32)
    core = jnp.reshape(mc, (1,)).astype(jnp.int32)
    ada_w = D * 6 // N_DEV

    w_in, m_w_in, v_w_in = (jnp.transpose(t, (0, 2, 1)) for t in (w_in, m_w_in, v_w_in))

    shards = [w[0].astype(BF16) for w in (w_in, w_ret_out, w_att_out, w_o, w_ff1, w_ff2)]
    (c_all,) = _run_comm(_Gather([c]), "gather_c")
    c_all = c_all.reshape(N_DEV, D)
    b_sl = lax.dynamic_slice(b_ada, (0, dev * ada_w), (1, ada_w))
    (mod_all,) = _run_comm(_Gather([_ada_fwd(c_all, w_ada[0], b_sl)]), "gather_mod")
    mod = lax.dynamic_index_in_dim(mod_all, dev, axis=1, keepdims=False).reshape(6, D)
    mods = tuple(mod[i:i + 1] for i in range(6))

    small = (norm1_g, rel_bias, ret_gn_g, ret_gn_b, norm2_g, norm_f_g.reshape(1, D))
    order = lax.dynamic_index_in_dim(jnp.asarray(_proj_order()), 2 * mx + my, axis=0, keepdims=False)
    loss, gx, in_flight, big_red, small_g, dmod = _local_step(x[0], loss_target[0], mods, shards[0], order,
                                                              shards[1:], small, chip, core)

    names = ['w_ada', 'b_ada', 'norm1_g', 'w_in', 'rel_bias', 'ret_gn_g', 'ret_gn_b', 'w_ret_out', 'w_att_out',
             'w_o', 'norm2_g', 'w_ff1', 'w_ff2', 'norm_f_g']
    ws = dict(zip(names, (w_ada, b_ada, norm1_g, w_in, rel_bias, ret_gn_g, ret_gn_b, w_ret_out, w_att_out, w_o,
                          norm2_g, w_ff1, w_ff2, norm_f_g)))
    ms = dict(zip(names, (m_w_ada, m_b_ada, m_norm1_g, m_w_in, m_rel_bias, m_ret_gn_g, m_ret_gn_b, m_w_ret_out,
                          m_w_att_out, m_w_o, m_norm2_g, m_w_ff1, m_w_ff2, m_norm_f_g)))
    vs = dict(zip(names, (v_w_ada, v_b_ada, v_norm1_g, v_w_in, v_rel_bias, v_ret_gn_g, v_ret_gn_b, v_w_ret_out,
                          v_w_att_out, v_w_o, v_norm2_g, v_w_ff1, v_w_ff2, v_norm_f_g)))
    grads, delta, new_m, new_v = {}, {}, {}, {}
    big_names = ('w_ret_out', 'w_att_out', 'w_o', 'w_ff1', 'w_ff2')
    for n, (part, recv) in zip(big_names, big_red):
        grads[n], delta[n], new_m[n], new_v[n] = _adamw_reduced1(ws[n], ms[n], vs[n], part, recv, chip, "adamw_" + n)
    updated = lax.optimization_barrier((gx, tuple(delta[n] for n in big_names)))
    gathered = _run_comm(_Gather(dmod + small_g + [loss]), "gather_small", after=updated[0])
    g_b_ada, dmod_all, (g_norm1, g_bias, g_gn_g, g_gn_b, g_norm2, g_normf, loss_sum) = _sum_small(gathered)
    loss_out = loss_sum[0, 0]
    g_w_ada = _ada_bwd(c_all, lax.dynamic_slice(dmod_all, (0, dev * ada_w), (N_DEV, ada_w)))

    grads.update(w_ada=g_w_ada.reshape(w_ada.shape), b_ada=g_b_ada, norm1_g=g_norm1, rel_bias=g_bias,
                 ret_gn_g=g_gn_g, ret_gn_b=g_gn_b, norm2_g=g_norm2, norm_f_g=g_normf)
    delta['w_ada'], new_m['w_ada'], new_v['w_ada'] = _adamw(w_ada, g_w_ada, m_w_ada, v_w_ada, "adamw_w_ada")
    small_names = ('b_ada', 'norm1_g', 'rel_bias', 'ret_gn_g', 'ret_gn_b', 'norm2_g', 'norm_f_g')
    two_d = {n: (1, ws[n].size) if ws[n].ndim == 1 else ws[n].shape for n in small_names}
    d_, m_, v_ = _adamw_small(*[[src[n].reshape(two_d[n]) for n in small_names] for src in (ws, grads, ms, vs)])
    for i, n in enumerate(small_names):
        shp = ws[n].shape
        delta[n], new_m[n], new_v[n] = d_[i].reshape(shp), m_[i].reshape(shp), v_[i].reshape(shp)
        grads[n] = grads[n].reshape(shp)

    done = lax.optimization_barrier((gx, tuple(d_), tuple(delta[n] for n in ('w_ada', 'w_ret_out', 'w_att_out', 'w_o',
                                                                               'w_ff1', 'w_ff2'))))
    parts_in, recvs_in = [], []
    for half, flight in enumerate(in_flight):
        (part_in,), (recv_chip_in,) = _chip_exchange_wait(flight, done[0], f"rs_in{half}_wait")
        parts_in.append(part_in)
        recvs_in.append(recv_chip_in)
    grads['w_in'], delta['w_in'], new_m['w_in'], new_v['w_in'] = _adamw_reduced(w_in, m_w_in, v_w_in, parts_in,
                                                                               recvs_in, chip)
    for d in (grads, delta, new_m, new_v):
        d['w_in'] = jnp.transpose(d['w_in'], (0, 2, 1))
    return (loss_out, gx[None], *[grads[n] for n in names], *[delta[n] for n in names],
            *[new_m[n] for n in names], *[new_v[n] for n in names])
```

```python
import functools
import math

import numpy as np
import jax
import jax.numpy as jnp
from jax import lax
from jax.experimental import pallas as pl
from jax.experimental.pallas import tpu as pltpu

F32 = jnp.float32
BF16 = jnp.bfloat16
MESH = pl.DeviceIdType.MESH

N_DEV = 8
S = 2048
D = 1024
RET_HEADS = 4
RET_DK = 256
RET_DV = 512
CHUNK = 128
N_CHUNK = S // CHUNK
ATT_GROUPS = ((128, 1), (512, 4), (2048, 16))
ATT_HG = 4
ATT_DH = 128
ATT_BLK = 128
N_BUCKETS = 32
MAX_DIST = 2048
D_FF = 4096
IN_COLS = 12800
OFF_RQ, OFF_RK, OFF_RV, OFF_RG, OFF_ATT = 0, 1024, 2048, 4096, 6144
OFF_GA, OFF_GB = 6144, 7168
RMS_EPS = 1e-6
GN_EPS = 1e-5
ADAM_LR, ADAM_B1, ADAM_B2, ADAM_EPS, ADAM_WD, ADAM_STEP = 0.001, 0.9, 0.999, 1e-08, 0.01, 10
VMEM_LIMIT = 48 * 1024 * 1024


def _pcall(body, **kw):
    return pl.pallas_call(body, **kw)


def _params(sem=None):
    return pltpu.CompilerParams(dimension_semantics=sem, vmem_limit_bytes=VMEM_LIMIT)


HBM_SPEC = pl.BlockSpec(memory_space=pl.ANY)


def _carry(body, comm, *, name, grid, in_specs, out_specs, out_shape, scratch_shapes=()):
    single = not isinstance(out_specs, (tuple, list))
    o_specs = (out_specs,) if single else tuple(out_specs)
    o_shape = (out_shape,) if single else tuple(out_shape)
    n_in, n_out, n_scr = len(in_specs), len(o_specs), len(scratch_shapes)
    nci, nco = len(comm.ins), len(comm.out_shape)
    total = int(np.prod(grid))

    def wrapped(*refs):
        bounds = np.cumsum([0, n_in, nci, n_out, nco, n_scr])
        a, ci, o, co, scr = (refs[bounds[i]:bounds[i + 1]] for i in range(5))
        sems = refs[bounds[5]:]
        flat = 0
        for d, g in enumerate(grid):
            flat = flat * g + pl.program_id(d)

        @pl.when(flat == 0)
        def _():
            comm.start(ci, co, sems)

        body(*a, *o, *scr)

        @pl.when(flat == total - 1)
        def _():
            comm.finish(ci, co, sems)

    aliases = {n_in + i: n_out + o for i, o in getattr(comm, "aliases", {}).items()}
    call = _pcall(wrapped, name=name, grid=grid, in_specs=list(in_specs) + [HBM_SPEC] * nci,
                  out_specs=o_specs + (HBM_SPEC,) * nco, out_shape=o_shape + tuple(comm.out_shape),
                  scratch_shapes=list(scratch_shapes) + list(comm.sems), input_output_aliases=aliases,
                  compiler_params=_params(("arbitrary",) * len(grid)))

    def run(*args):
        res = call(*args, *comm.ins)
        own = res[0] if single else tuple(res[:n_out])
        return own, tuple(res[n_out:])

    return run


def _run_comm(comm, name, after=None):
    nci, nco = len(comm.ins), len(comm.out_shape)
    extra = [] if after is None else [after]

    def body(*refs):
        ci, co, sems = refs[:nci], refs[nci + len(extra):nci + len(extra) + nco], refs[nci + len(extra) + nco:]
        comm.start(ci, co, sems)
        comm.finish(ci, co, sems)

    return _pcall(body, name=name, in_specs=[HBM_SPEC] * (nci + len(extra)), out_specs=(HBM_SPEC,) * nco,
                  out_shape=tuple(comm.out_shape), scratch_shapes=list(comm.sems))(*comm.ins, *extra)


def _dot(a, b, dn):
    return lax.dot_general(a.astype(BF16), b.astype(BF16), (dn, ((), ())), preferred_element_type=F32)


NN = ((1,), (0,))
NT = ((1,), (1,))
TN = ((0,), (0,))


def _mm(a, b, mode, *, tm, tn, tk, name, out_dtype=F32, res=None, gvec=None, relu2=False, relu2_of=None, comm=None,
        after=None):
    if mode == 'nn':
        (M, K), (_, N) = a.shape, b.shape
        a_spec = pl.BlockSpec((tm, tk), lambda i, j, k: (i, k))
        b_spec = pl.BlockSpec((tk, tn), lambda i, j, k: (k, j))
        dn = NN
    elif mode == 'nt':
        (M, K), (N, _) = a.shape, b.shape
        a_spec = pl.BlockSpec((tm, tk), lambda i, j, k: (i, k))
        b_spec = pl.BlockSpec((tn, tk), lambda i, j, k: (j, k))
        dn = NT
    else:
        (K, M), (_, N) = a.shape, b.shape
        a_spec = pl.BlockSpec((tk, tm), lambda i, j, k: (k, i))
        b_spec = pl.BlockSpec((tk, tn), lambda i, j, k: (k, j))
        dn = TN
    assert M % tm == 0 and N % tn == 0 and K % tk == 0, (name, M, N, K)
    nk = K // tk
    fused = res is not None
    o_spec = pl.BlockSpec((tm, tn), lambda i, j, k: (i, j))

    def body(a_ref, b_ref, *rest):
        acc_ref = rest[-1] if nk > 1 else None
        if after is not None:
            rest = rest[1:]
        if fused:
            res_ref, g_ref, o_ref, x_ref = rest[:4]
        elif relu2_of is not None:
            u_ref, o_ref = rest[:2]
        elif relu2:
            o_ref, act_ref = rest[:2]
        else:
            o_ref = rest[0]

        def finish(acc):
            if relu2_of is not None:
                acc = acc * (2.0 * jnp.maximum(u_ref[...], 0.0))
            o_ref[...] = acc.astype(o_ref.dtype)
            if fused:
                x_ref[...] = res_ref[...] + g_ref[...] * acc
            if relu2:
                r = jnp.maximum(acc, 0.0)
                act_ref[...] = (r * r).astype(BF16)

        p = _dot(a_ref[...], b_ref[...], dn)
        if nk == 1:
            finish(p)
        else:
            k = pl.program_id(2)

            @pl.when(k == 0)
            def _():
                acc_ref[...] = p

            @pl.when(k > 0)
            def _():
                acc_ref[...] += p

            @pl.when(k == nk - 1)
            def _():
                finish(acc_ref[...])

    in_specs = [a_spec, b_spec]
    args = [a, b]
    if after is not None:
        in_specs.append(pl.BlockSpec(memory_space=pl.ANY))
        args.append(after)
    out_shape = jax.ShapeDtypeStruct((M, N), out_dtype)
    out_specs = o_spec
    if fused:
        in_specs += [pl.BlockSpec((tm, tn), lambda i, j, k: (i, j)), pl.BlockSpec((1, tn), lambda i, j, k: (0, j))]
        args += [res, gvec]
        out_shape = (out_shape, jax.ShapeDtypeStruct((M, N), F32))
        out_specs = (o_spec, pl.BlockSpec((tm, tn), lambda i, j, k: (i, j)))
    elif relu2_of is not None:
        in_specs.append(pl.BlockSpec((tm, tn), lambda i, j, k: (i, j)))
        args.append(relu2_of)
    elif relu2:
        out_shape = (out_shape, jax.ShapeDtypeStruct((M, N), BF16))
        out_specs = (o_spec, pl.BlockSpec((tm, tn), lambda i, j, k: (i, j)))
    kw = dict(name=name, grid=(M // tm, N // tn, nk), in_specs=in_specs, out_specs=out_specs,
              out_shape=out_shape, scratch_shapes=[pltpu.VMEM((tm, tn), F32)] if nk > 1 else [])
    if comm is not None:
        return _carry(body, comm, **kw)(*args)
    return _pcall(body, compiler_params=_params(("parallel", "parallel", "arbitrary")), **kw)(*args)


PROJ_TN = 512
ATT_T0, ATT_T1 = 6144 // PROJ_TN, 10752 // PROJ_TN
N_SLABS = (ATT_T1 - ATT_T0) * 4
MAIN_COLS = IN_COLS - (ATT_T1 - ATT_T0) * PROJ_TN


PROJ_TILES = IN_COLS // PROJ_TN
SHARD_ROWS = IN_COLS // N_DEV
W_CHUNKS = 4
N_OWN, N_NEAR = 5, 18


def _proj_order():
    out = np.zeros((4, 3, PROJ_TILES), np.int32)
    for q in range(4):
        def hops(t):
            owners = {col // (2 * SHARD_ROWS) for col in (t * PROJ_TN, (t + 1) * PROJ_TN - 1)}
            return max(bin(q ^ p).count("1") for p in owners)
        order = sorted(range(PROJ_TILES), key=lambda t: (hops(t), t))
        assert all(hops(t) == 0 for t in order[:N_OWN]) and all(hops(t) < 2 for t in order[:N_NEAR])
        is_att = [ATT_T0 <= t < ATT_T1 for t in order]
        for row, kind, index in ((1, False, lambda t: t if t < ATT_T0 else t - (ATT_T1 - ATT_T0)),
                                 (2, True, lambda t: t - ATT_T0)):
            own = [index(t) if a == kind else None for t, a in zip(order, is_att)]
            first = next(v for v in own if v is not None)
            last = first
            for j, v in enumerate(own):
                last = last if v is None else v
                out[q, row, j] = last
        out[q, 0] = order
    return out


def _gather_proj(h1, shard, order):
    rows = SHARD_ROWS // W_CHUNKS

    def body(ord_ref, a_ref, sh_ref, main_ref, slab_ref, full_ref, wbuf, fetch_sems, send_sems, recv_sems,
             local_sems):
        j = pl.program_id(0)
        x, y, c = _mesh_pos()
        me, sibling = (x, y, c), (x, y, 1 - c)
        chips = [(1 - x, y), (x, 1 - y), (1 - x, 1 - y)]

        def block(p, owner):
            return full_ref.at[pl.ds(pl.multiple_of(_slot(owner) * SHARD_ROWS + p * rows, 16), rows)]

        def copy(p, k, owner, to, from_input=False):
            dst = block(p, owner)
            return pltpu.make_async_remote_copy(
                src_ref=sh_ref.at[pl.ds(p * rows, rows)] if from_input else dst, dst_ref=dst,
                send_sem=send_sems.at[7 * p + k], recv_sem=recv_sems.at[7 * p + k], device_id=to, device_id_type=MESH)

        pieces = range(W_CHUNKS)
        mine = [pltpu.make_async_copy(sh_ref.at[pl.ds(p * rows, rows)], block(p, me), local_sems.at[p]) for p in pieces]
        first = [copy(p, 0, me, sibling, from_input=True) for p in pieces]
        first += [copy(p, 1 + n, me, (*chips[n], c), from_input=True) for p in pieces for n in range(2)]
        near_pass = [copy(p, 4 + n, (*chips[n], c), sibling) for p in pieces for n in range(2)]
        relay = [copy(p, 3, ((x + 1 - c) % 2, (y + c) % 2, c), ((x + c) % 2, (y + 1 - c) % 2, c)) for p in pieces]
        far_pass = [copy(p, 6, (*chips[2], c), sibling) for p in pieces]

        def fetch(pos):
            slot = lax.rem(pos, 2)
            start = pl.multiple_of(ord_ref[0, pos] * PROJ_TN, PROJ_TN)
            return pltpu.make_async_copy(full_ref.at[pl.ds(start, PROJ_TN)], wbuf.at[slot], fetch_sems.at[slot])

        @pl.when(j == 0)
        def _():
            for cp in mine + first:
                cp.start()
            for cp in mine:
                cp.wait()
            for p in pieces:
                copy(p, 0, sibling, me).wait_recv()
            fetch(j).start()

        @pl.when(j == N_OWN - 1)
        def _():
            for p in pieces:
                for n in range(2):
                    copy(p, 1 + n, (*chips[n], c), me).wait_recv()
                    near_pass[2 * p + n].start()
                relay[p].start()
            for p in pieces:
                for n in range(2):
                    copy(p, 4 + n, (*chips[n], 1 - c), me).wait_recv()

        @pl.when(j == N_NEAR - 1)
        def _():
            for p in pieces:
                copy(p, 3, (*chips[2], c), me).wait_recv()
                far_pass[p].start()
            for p in pieces:
                copy(p, 6, (*chips[2], 1 - c), me).wait_recv()

        @pl.when(j + 1 < PROJ_TILES)
        def _():
            fetch(j + 1).start()

        fetch(j).wait()
        w_ref = wbuf.at[lax.rem(j, 2)]
        tile = ord_ref[0, j]
        is_att = (tile >= ATT_T0) & (tile < ATT_T1)
        chunks = [pl.ds(r * 512, 512) for r in range(S // 512)]

        @pl.when(jnp.logical_not(is_att))
        def _():
            for rws in chunks:
                main_ref[rws, :] = _dot(a_ref[rws, :], w_ref[...], NT)

        @pl.when(is_att)
        def _():
            for rws in chunks:
                p = _dot(a_ref[rws, :], w_ref[...], NT)
                for h in range(4):
                    slab_ref[h, rws, :] = p[:, h * 128:(h + 1) * 128]

        @pl.when(j == PROJ_TILES - 1)
        def _():
            for cp in first + near_pass + relay + far_pass:
                cp.wait_send()

    gs = pltpu.PrefetchScalarGridSpec(
        num_scalar_prefetch=1, grid=(PROJ_TILES,),
        in_specs=[pl.BlockSpec((S, D), lambda j, o: (0, 0)), HBM_SPEC],
        out_specs=(pl.BlockSpec((S, PROJ_TN), lambda j, o: (0, o[1, j])),
                   pl.BlockSpec((4, S, 128), lambda j, o: (o[2, j], 0, 0)), HBM_SPEC),
        scratch_shapes=[pltpu.VMEM((2, PROJ_TN, D), BF16), pltpu.SemaphoreType.DMA((2,)),
                        pltpu.SemaphoreType.DMA((7 * W_CHUNKS,)), pltpu.SemaphoreType.DMA((7 * W_CHUNKS,)),
                        pltpu.SemaphoreType.DMA((W_CHUNKS,))])
    return _pcall(body, name="gather_proj", grid_spec=gs,
                  out_shape=(jax.ShapeDtypeStruct((S, MAIN_COLS), F32), jax.ShapeDtypeStruct((N_SLABS, S, 128), F32),
                             jax.ShapeDtypeStruct((IN_COLS, D), BF16)),
                  compiler_params=_params(("arbitrary",)))(order, h1, shard)


TR = 256


def _row_spec(w=D):
    return pl.BlockSpec((TR, w), lambda i: (i, 0))


def _vec_spec(w=D):
    return pl.BlockSpec((1, w), lambda i: (0, 0))


def _norm_mod_fwd(x, g, sh, sc, name):
    def body(x_ref, g_ref, sh_ref, sc_ref, o_ref):
        xv = x_ref[...]
        rstd = lax.rsqrt(jnp.mean(xv * xv, axis=-1, keepdims=True) + RMS_EPS)
        n = xv * rstd * g_ref[...]
        o_ref[...] = (n * (1.0 + sc_ref[...]) + sh_ref[...]).astype(BF16)

    return _pcall(body, name=name, grid=(S // TR,), in_specs=[_row_spec(), _vec_spec(), _vec_spec(), _vec_spec()],
                  out_specs=_row_spec(), out_shape=jax.ShapeDtypeStruct((S, D), BF16),
                  compiler_params=_params(("parallel",)))(x, g, sh, sc)


def _norm_mod_bwd(x, g, sc, dh, dres, name, gate=None, after=None):
    gated = gate is not None

    def body(x_ref, g_ref, sc_ref, dh_ref, dres_ref, *rest):
        if after is not None:
            rest = rest[1:]
        if gated:
            f_ref, gv_ref, dx_ref, dsc_ref, dsh_ref, dg_ref, dz_ref, dgv_ref = rest
        else:
            dx_ref, dsc_ref, dsh_ref, dg_ref = rest
        i = pl.program_id(0)
        xv = x_ref[...]
        dh = dh_ref[...]
        rstd = lax.rsqrt(jnp.mean(xv * xv, axis=-1, keepdims=True) + RMS_EPS)
        xhat = xv * rstd
        gv = g_ref[...]
        dn = dh * (1.0 + sc_ref[...])
        dxhat = dn * gv
        dx = dres_ref[...] + rstd * (dxhat - xhat * jnp.mean(dxhat * xhat, axis=-1, keepdims=True))
        dx_ref[...] = dx
        sums = [(dsc_ref, jnp.sum(dh * (xhat * gv), axis=0, keepdims=True)),
                (dsh_ref, jnp.sum(dh, axis=0, keepdims=True)),
                (dg_ref, jnp.sum(dn * xhat, axis=0, keepdims=True))]
        if gated:
            dz_ref[...] = (dx * gv_ref[...]).astype(BF16)
            sums.append((dgv_ref, jnp.sum(dx * f_ref[...], axis=0, keepdims=True)))

        @pl.when(i == 0)
        def _():
            for ref, p in sums:
                ref[...] = p

        @pl.when(i > 0)
        def _():
            for ref, p in sums:
                ref[...] += p

    vec = jax.ShapeDtypeStruct((1, D), F32)
    in_specs = [_row_spec(), _vec_spec(), _vec_spec(), _row_spec(), _row_spec()]
    out_specs = [_row_spec(), _vec_spec(), _vec_spec(), _vec_spec()]
    out_shape = [jax.ShapeDtypeStruct((S, D), F32), vec, vec, vec]
    args = [x, g, sc, dh, dres]
    if after is not None:
        in_specs.append(HBM_SPEC)
        args.append(after)
    if gated:
        in_specs += [_row_spec(), _vec_spec()]
        out_specs += [_row_spec(), _vec_spec()]
        out_shape += [jax.ShapeDtypeStruct((S, D), BF16), vec]
        args += list(gate)
    return _pcall(body, name=name, grid=(S // TR,), in_specs=in_specs, out_specs=tuple(out_specs),
                  out_shape=tuple(out_shape), compiler_params=_params(("arbitrary",)))(*args)


def _w_o_norm2(merged, w_o, x, g1, g, sh, sc):
    def body(a_ref, b_ref, x_ref, g1_ref, g_ref, sh_ref, sc_ref, o_ref, x1_ref, h_ref):
        acc = _dot(a_ref[...], b_ref[...], NN)
        o_ref[...] = acc
        xv = x_ref[...] + g1_ref[...] * acc
        x1_ref[...] = xv
        rstd = lax.rsqrt(jnp.mean(xv * xv, axis=-1, keepdims=True) + RMS_EPS)
        h_ref[...] = (xv * rstd * g_ref[...] * (1.0 + sc_ref[...]) + sh_ref[...]).astype(BF16)

    rows = pl.BlockSpec((FF2_TM, D), lambda i: (i, 0))
    f32 = jax.ShapeDtypeStruct((S, D), F32)
    return _pcall(body, name="w_o_norm2", grid=(S // FF2_TM,),
                  in_specs=[rows, pl.BlockSpec((D, D), lambda i: (0, 0)), rows] + [_vec_spec()] * 4,
                  out_specs=(rows, rows, rows), out_shape=(f32, f32, jax.ShapeDtypeStruct((S, D), BF16)),
                  compiler_params=_params(("parallel",)))(merged, w_o, x, g1, g, sh, sc)


FF2_TM = 512


def _ff2_final(act, w_ff2, x1, g2, tgt, g):
    def body(a_ref, b_ref, x1_ref, g2_ref, t_ref, g_ref, loss_ref, dx_ref, dg_ref, df_ref, dg2_ref):
        i = pl.program_id(0)
        f = _dot(a_ref[...], b_ref[...], NN)
        g2v = g2_ref[...]
        xv = x1_ref[...] + g2v * f
        gv = g_ref[...]
        rstd = lax.rsqrt(jnp.mean(xv * xv, axis=-1, keepdims=True) + RMS_EPS)
        xhat = xv * rstd
        err = xhat * gv - t_ref[...]
        dy = err * (1.0 / D)
        dxhat = dy * gv
        dx = rstd * (dxhat - xhat * jnp.mean(dxhat * xhat, axis=-1, keepdims=True))
        dx_ref[...] = dx
        df_ref[...] = (dx * g2v).astype(BF16)
        p_g = jnp.sum(dy * xhat, axis=0, keepdims=True)
        p_g2 = jnp.sum(dx * f, axis=0, keepdims=True)
        p_l = jnp.zeros((1, 128), F32) + 0.5 * jnp.sum(jnp.mean(err * err, axis=-1, keepdims=True))

        @pl.when(i == 0)
        def _():
            dg_ref[...] = p_g
            dg2_ref[...] = p_g2
            loss_ref[...] = p_l

        @pl.when(i > 0)
        def _():
            dg_ref[...] += p_g
            dg2_ref[...] += p_g2
            loss_ref[...] += p_l

    vec = jax.ShapeDtypeStruct((1, D), F32)
    rows = lambda w: pl.BlockSpec((FF2_TM, w), lambda i: (i, 0))
    return _pcall(body, name="ff2_final", grid=(S // FF2_TM,),
                  in_specs=[rows(D_FF), pl.BlockSpec((D_FF, D), lambda i: (0, 0)), rows(D), _vec_spec(), rows(D),
                            _vec_spec()],
                  out_specs=(_vec_spec(128), rows(D), _vec_spec(), rows(D), _vec_spec()),
                  out_shape=(jax.ShapeDtypeStruct((1, 128), F32), jax.ShapeDtypeStruct((S, D), F32), vec,
                             jax.ShapeDtypeStruct((S, D), BF16), vec),
                  compiler_params=_params(("arbitrary",)))(act, w_ff2, x1, g2, tgt, g)


HALF = 512


MERGE_TM = 1024


def _merge_specs():
    blk = lambda off: pl.BlockSpec((MERGE_TM, HALF), lambda i, j: (i, off // HALF + j))
    return blk(OFF_GA), blk(OFF_GB), blk(0)


def _att_out_merge(att, w_att_out, proj, ret_out):
    def body(a_ref, b_ref, ga_ref, gb_ref, r_ref, o_ref, m_ref):
        acc = _dot(a_ref[...], b_ref[...], NN)
        o_ref[...] = acc
        m_ref[...] = (jax.nn.sigmoid(ga_ref[...]) * r_ref[...] + jax.nn.sigmoid(gb_ref[...]) * acc).astype(BF16)

    ga, gb, tile = _merge_specs()
    return _pcall(body, name="att_out", grid=(S // MERGE_TM, D // HALF),
                  in_specs=[pl.BlockSpec((MERGE_TM, AW), lambda i, j: (i, 0)), pl.BlockSpec((AW, HALF), lambda i, j: (0, j)),
                            ga, gb, tile],
                  out_specs=(tile, tile),
                  out_shape=(jax.ShapeDtypeStruct((S, D), F32), jax.ShapeDtypeStruct((S, D), BF16)),
                  compiler_params=_params(("parallel", "parallel")))(att, w_att_out, proj, proj, ret_out)


def _dmerged_split(dmixo, w_o, proj, ret_out, att_out):
    def body(a_ref, b_ref, ga_ref, gb_ref, r_ref, at_ref, dr_ref, da_ref, dga_ref, dgb_ref):
        dm = _dot(a_ref[...], b_ref[...], NT)
        sa = jax.nn.sigmoid(ga_ref[...])
        sb = jax.nn.sigmoid(gb_ref[...])
        dr_ref[...] = (dm * sa).astype(BF16)
        da_ref[...] = (dm * sb).astype(BF16)
        dga_ref[...] = (dm * r_ref[...] * (sa * (1.0 - sa))).astype(BF16)
        dgb_ref[...] = (dm * at_ref[...] * (sb * (1.0 - sb))).astype(BF16)

    ga, gb, tile = _merge_specs()
    o = jax.ShapeDtypeStruct((S, D), BF16)
    return _pcall(body, name="dmerged", grid=(S // MERGE_TM, D // HALF),
                  in_specs=[pl.BlockSpec((MERGE_TM, D), lambda i, j: (i, 0)), pl.BlockSpec((HALF, D), lambda i, j: (j, 0)),
                            ga, gb, tile, tile],
                  out_specs=(tile,) * 4, out_shape=(o, o, o, o),
                  compiler_params=_params(("parallel", "parallel")))(dmixo, w_o, proj, proj, ret_out, att_out)


def _ret_tables():
    H, C = RET_HEADS, CHUNK
    log_g = jnp.log1p(-(2.0 ** (-5.0 - jnp.arange(H, dtype=F32))))
    idx = jnp.arange(C, dtype=F32)
    rel = idx[:, None] - idx[None, :]
    inner = jnp.where(rel >= 0, jnp.exp(log_g[:, None, None] * jnp.maximum(rel, 0.0)), 0.0)
    qd = jnp.exp(log_g[:, None] * (idx + 1.0))[:, :, None]
    kd = jnp.exp(log_g[:, None] * (C - 1.0 - idx))[:, :, None]
    cd = jnp.broadcast_to(jnp.exp(log_g * C)[:, None, None], (H, 1, 128))
    half = RET_DK // 2
    inv = 10000.0 ** (-jnp.arange(half, dtype=F32) / half)
    ang = jnp.arange(S, dtype=F32)[:, None] * inv[None, :]
    return inner, qd, kd, cd, jnp.cos(ang), jnp.sin(ang)


def _rot(x, cos, sin):
    x1, x2 = x[:, :128], x[:, 128:]
    return jnp.concatenate([x1 * cos - x2 * sin, x1 * sin + x2 * cos], axis=1)


def _rot_t(d, cos, sin):
    d1, d2 = d[:, :128], d[:, 128:]
    return jnp.concatenate([d1 * cos + d2 * sin, d2 * cos - d1 * sin], axis=1)


RET_COLS = OFF_ATT
RET_VW = RET_HEADS * RET_DV


def _ret_specs(chunk_of):
    ci = chunk_of
    whole = lambda shape: pl.BlockSpec(shape, lambda t: (0,) * len(shape))
    return [
        pl.BlockSpec((CHUNK, RET_COLS), lambda t: (ci(t), 0)),
        pl.BlockSpec((CHUNK, 128), lambda t: (ci(t), 0)),
        pl.BlockSpec((CHUNK, 128), lambda t: (ci(t), 0)),
        whole((RET_HEADS, CHUNK, CHUNK)), whole((RET_HEADS, CHUNK, 1)), whole((RET_HEADS, CHUNK, 1)),
        whole((RET_HEADS, 1, 128)), whole((1, RET_VW)), whole((1, RET_VW)),
    ]


def _ret_cols(h):
    q = slice(OFF_RQ + h * RET_DK, OFF_RQ + (h + 1) * RET_DK)
    k = slice(OFF_RK + h * RET_DK, OFF_RK + (h + 1) * RET_DK)
    v = slice(OFF_RV + h * RET_DV, OFF_RV + (h + 1) * RET_DV)
    g = slice(OFF_RG + h * RET_DV, OFF_RG + (h + 1) * RET_DV)
    return q, k, v, g, slice(h * RET_DV, (h + 1) * RET_DV)


def _ret_fwd(proj, tables, gn_g, gn_b, after):
    inner, qd, kd, cd, cos, sin = tables

    def body(x_ref, cos_ref, sin_ref, in_ref, qd_ref, kd_ref, cd_ref, g_ref, b_ref, after_ref,
             gated_ref, ro_ref, st_ref, s_scr):
        i = pl.program_id(0)

        @pl.when(i == 0)
        def _():
            s_scr[...] = jnp.zeros_like(s_scr)

        cosv, sinv = cos_ref[...], sin_ref[...]
        for h in range(RET_HEADS):
            cq, ck, cv, cg, co = _ret_cols(h)
            q = _rot(x_ref[:, cq], cosv, sinv)
            k = _rot(x_ref[:, ck], cosv, sinv) * (RET_DK ** -0.5)
            v = x_ref[:, cv]
            st = s_scr[h]
            st_ref[h] = st.astype(BF16)
            s = _dot(q, k, NT) * in_ref[h]
            o = _dot(s, v, NN) + _dot(q, st, NN) * qd_ref[h]
            s_scr[h] = st * cd_ref[h, :, :1] + _dot(k * kd_ref[h], v, TN)
            ro_ref[:, co] = o
            mu = jnp.mean(o, axis=-1, keepdims=True)
            oc = o - mu
            var = jnp.mean(oc * oc, axis=-1, keepdims=True)
            rn = oc * lax.rsqrt(var + GN_EPS) * g_ref[:, co] + b_ref[:, co]
            rg = x_ref[:, cg]
            gated_ref[:, co] = (rg * jax.nn.sigmoid(rg) * rn).astype(BF16)

    ospec = pl.BlockSpec((CHUNK, RET_VW), lambda t: (t, 0))
    return _pcall(
        body, name="ret_fwd", grid=(N_CHUNK,), in_specs=_ret_specs(lambda t: t) + [HBM_SPEC],
        out_specs=(ospec, ospec, pl.BlockSpec((RET_HEADS, None, RET_DK, RET_DV), lambda t: (0, t, 0, 0))),
        out_shape=(jax.ShapeDtypeStruct((S, RET_VW), BF16), jax.ShapeDtypeStruct((S, RET_VW), F32),
                   jax.ShapeDtypeStruct((RET_HEADS, N_CHUNK, RET_DK, RET_DV), BF16)),
        scratch_shapes=[pltpu.VMEM((RET_HEADS, RET_DK, RET_DV), F32)],
        compiler_params=_params(("arbitrary",)))(proj, cos, sin, inner, qd, kd, cd, gn_g, gn_b, after)


def _ret_bwd(proj, tables, gn_g, gn_b, ro, states, dgated, others):
    inner, qd, kd, cd, cos, sin = tables
    last = N_CHUNK - 1
    pieces = lambda o: [(h, o.shape[2]) for h in range(o.shape[0])] if len(o.shape) == 3 else [(None, o.shape[1])]
    assert RET_COLS + sum(w for o in others for _, w in pieces(o)) == IN_COLS

    def body(x_ref, cos_ref, sin_ref, in_ref, qd_ref, kd_ref, cd_ref, g_ref, b_ref, ro_ref, st_ref, dg_ref, *rest):
        other_refs, (dx_ref, gg_ref, gb_ref, gs_scr) = rest[:len(others)], rest[len(others):]
        t = pl.program_id(0)
        col = RET_COLS
        for o_ref in other_refs:
            for h, w in pieces(o_ref):
                dx_ref[:, col:col + w] = (o_ref[...] if h is None else o_ref[h]).astype(BF16)
                col += w

        @pl.when(t == 0)
        def _():
            gs_scr[...] = jnp.zeros_like(gs_scr)
            gg_ref[...] = jnp.zeros_like(gg_ref)
            gb_ref[...] = jnp.zeros_like(gb_ref)

        cosv, sinv = cos_ref[...], sin_ref[...]
        for h in range(RET_HEADS):
            cq, ck, cv, cg, co = _ret_cols(h)
            q = _rot(x_ref[:, cq], cosv, sinv)
            k = _rot(x_ref[:, ck], cosv, sinv) * (RET_DK ** -0.5)
            v = x_ref[:, cv]
            qdv, kdv, dm = qd_ref[h], kd_ref[h], in_ref[h]
            st = st_ref[h]
            o = ro_ref[:, co]
            gv = g_ref[:, co]
            mu = jnp.mean(o, axis=-1, keepdims=True)
            oc = o - mu
            rstd = lax.rsqrt(jnp.mean(oc * oc, axis=-1, keepdims=True) + GN_EPS)
            ohat = oc * rstd
            rn = ohat * gv + b_ref[:, co]
            rg = x_ref[:, cg]
            sg = jax.nn.sigmoid(rg)
            dgt = dg_ref[:, co]
            drn = dgt * (rg * sg)
            dx_ref[:, cg] = (dgt * rn * (sg * (1.0 + rg * (1.0 - sg)))).astype(BF16)
            gg_ref[:, co] += jnp.sum(drn * ohat, axis=0, keepdims=True)
            gb_ref[:, co] += jnp.sum(drn, axis=0, keepdims=True)
            dohat = drn * gv
            do = rstd * (dohat - jnp.mean(dohat, axis=-1, keepdims=True)
                         - ohat * jnp.mean(dohat * ohat, axis=-1, keepdims=True))
            gs = gs_scr[h]
            s = _dot(q, k, NT) * dm
            dsr = _dot(do, v, NT) * dm
            dq = _dot(dsr, k, NN) + _dot(do, st, NT) * qdv
            dk = _dot(dsr, q, TN) + _dot(v, gs, NT) * kdv
            dv = _dot(s, do, TN) + _dot(k * kdv, gs, NN)
            gs_scr[h] = gs * cd_ref[h, :, :1] + _dot(q * qdv, do, TN)
            dx_ref[:, cq] = _rot_t(dq, cosv, sinv).astype(BF16)
            dx_ref[:, ck] = (_rot_t(dk, cosv, sinv) * (RET_DK ** -0.5)).astype(BF16)
            dx_ref[:, cv] = dv.astype(BF16)

    rev = lambda t: last - t
    vblk = pl.BlockSpec((CHUNK, RET_VW), lambda t: (rev(t), 0))
    vspec = pl.BlockSpec((1, RET_VW), lambda t: (0, 0))
    rows = lambda w: pl.BlockSpec((CHUNK, w), lambda t: (rev(t), 0))
    return _pcall(
        body, name="ret_bwd", grid=(N_CHUNK,),
        in_specs=_ret_specs(rev) + [vblk, pl.BlockSpec((RET_HEADS, None, RET_DK, RET_DV), lambda t: (0, rev(t), 0, 0)),
                                    vblk] + [rows(o.shape[1]) if o.ndim == 2 else
                                             pl.BlockSpec((o.shape[0], CHUNK, o.shape[2]), lambda t: (0, rev(t), 0))
                                             for o in others],
        out_specs=(rows(IN_COLS), vspec, vspec),
        out_shape=(jax.ShapeDtypeStruct((S, IN_COLS), BF16), jax.ShapeDtypeStruct((1, RET_VW), F32),
                   jax.ShapeDtypeStruct((1, RET_VW), F32)),
        scratch_shapes=[pltpu.VMEM((RET_HEADS, RET_DK, RET_DV), F32)],
        compiler_params=_params(("arbitrary",)))(proj, cos, sin, inner, qd, kd, cd, gn_g, gn_b, ro, states, dgated,
                                                 *others)


def _bucket_tables():
    qi = np.arange(ATT_BLK)[:, None]
    kj = np.arange(2 * ATT_BLK)[None, :]
    m = ATT_BLK + qi - kj
    out = []
    for win, dil in ATT_GROUPS:
        w = win // dil
        dist = (np.clip(m, 0, w) * dil).astype(np.int32)
        max_exact = N_BUCKETS // 2
        d_f = np.maximum(dist, 1).astype(np.float32)
        large = max_exact + (np.log(d_f / np.float32(max_exact)) / np.float32(math.log(MAX_DIST / max_exact))
                             * np.float32(N_BUCKETS - max_exact)).astype(np.int32)
        large = np.minimum(large, N_BUCKETS - 1)
        out.append(np.where(dist < max_exact, dist, large).astype(np.int32))
    return np.stack(out)


def _bias_build(rel_bias, buckets, after):
    def body(tab_ref, bk_ref, after_ref, o_ref):
        hh = pl.program_id(0)
        bk = bk_ref[...]
        acc = jnp.zeros((ATT_BLK, 2 * ATT_BLK), F32)
        for b in range(N_BUCKETS):
            acc = jnp.where(bk == b, tab_ref[b, hh], acc)
        o_ref[...] = acc

    nh = len(ATT_GROUPS) * ATT_HG
    return _pcall(body, name="bias_build", grid=(nh,),
                  in_specs=[pl.BlockSpec(memory_space=pltpu.SMEM),
                            pl.BlockSpec((None, ATT_BLK, 2 * ATT_BLK), lambda hh: (hh // ATT_HG, 0, 0)), HBM_SPEC],
                  out_specs=pl.BlockSpec((None, ATT_BLK, 2 * ATT_BLK), lambda hh: (hh, 0, 0)),
                  out_shape=jax.ShapeDtypeStruct((nh, ATT_BLK, 2 * ATT_BLK), F32),
                  compiler_params=_params(("parallel",)))(rel_bias, buckets, after)


def _bias_grad(ds_sum, buckets):
    def body(ds_ref, bk_ref, o_ref):
        bk = bk_ref[...]
        ds = ds_ref[...]
        rows = lax.broadcasted_iota(jnp.int32, (N_BUCKETS, 128), 0)
        acc = jnp.zeros((N_BUCKETS, 128), F32)
        for b in range(N_BUCKETS):
            acc = jnp.where(rows == b, jnp.sum(jnp.where(bk == b, ds, 0.0)), acc)
        o_ref[...] = acc

    nh = len(ATT_GROUPS) * ATT_HG
    return _pcall(body, name="bias_grad", grid=(nh,),
                  in_specs=[pl.BlockSpec((None, ATT_BLK, 2 * ATT_BLK), lambda hh: (hh, 0, 0)),
                            pl.BlockSpec((None, ATT_BLK, 2 * ATT_BLK), lambda hh: (hh // ATT_HG, 0, 0))],
                  out_specs=pl.BlockSpec((None, N_BUCKETS, 128), lambda hh: (hh, 0, 0)),
                  out_shape=jax.ShapeDtypeStruct((nh, N_BUCKETS, 128), F32),
                  compiler_params=_params(("parallel",)))(ds_sum, buckets)


def _att_valid(n):
    qi = lax.broadcasted_iota(jnp.int32, (ATT_BLK, 2 * ATT_BLK), 0)
    kj = lax.broadcasted_iota(jnp.int32, (ATT_BLK, 2 * ATT_BLK), 1)
    m = ATT_BLK + qi - kj
    first_key = jnp.where(n > 0, 0, ATT_BLK)
    return (m >= 0) & (m <= ATT_BLK) & (kj >= first_key)


ATT_HP = (1, 2, 2)


def _att_geometry(gi):
    _, dil = ATT_GROUPS[gi]
    return dil, S // dil // ATT_BLK, ATT_HP[gi]


def _blk(dil, r, n):
    if dil == 1:
        return pl.ds(n * ATT_BLK, ATT_BLK)
    return pl.ds(r + n * ATT_BLK * dil, ATT_BLK, stride=dil)


def _slab_specs(gi):
    _, _, hp = _att_geometry(gi)
    per = ATT_HG // hp
    return [pl.BlockSpec((hp, S, ATT_DH), lambda g, r, part=part: ((3 * gi + part) * per + g, 0, 0))
            for part in range(3)]


def _head_specs(gi, count):
    _, _, hp = _att_geometry(gi)
    return [pl.BlockSpec((hp, S, ATT_DH), lambda g, r: (g, 0, 0))] * count


def _bias_spec(gi):
    _, _, hp = _att_geometry(gi)
    return pl.BlockSpec((hp, ATT_BLK, 2 * ATT_BLK), lambda g, r: (gi * (ATT_HG // hp) + g, 0, 0))


def _att_valid_first():
    qi = lax.broadcasted_iota(jnp.int32, (ATT_BLK, ATT_BLK), 0)
    kj = lax.broadcasted_iota(jnp.int32, (ATT_BLK, ATT_BLK), 1)
    return kj <= qi


def _att_fwd(slabs, bias, gi, comm=None):
    dil, nb, hp = _att_geometry(gi)
    scale = ATT_DH ** -0.5

    def body(q_ref, k_ref, v_ref, bias_ref, o_ref, l_ref):
        r = pl.program_id(1)
        for n in range(nb):
            cur = _blk(dil, r, n)
            valid = _att_valid(n) if n > 0 else _att_valid_first()
            for h in range(hp):
                if n > 0:
                    prev = _blk(dil, r, n - 1)
                    kk = jnp.concatenate([k_ref[h, prev, :], k_ref[h, cur, :]], axis=0)
                    vv = jnp.concatenate([v_ref[h, prev, :], v_ref[h, cur, :]], axis=0)
                    bias = bias_ref[h]
                else:
                    kk, vv, bias = k_ref[h, cur, :], v_ref[h, cur, :], bias_ref[h, :, pl.ds(ATT_BLK, ATT_BLK)]
                s = _dot(q_ref[h, cur, :], kk, NT) * scale + bias
                s = jnp.where(valid, s, -1e30)
                mx = jnp.max(s, axis=-1, keepdims=True)
                e = jnp.exp(s - mx)
                den = jnp.sum(e, axis=-1, keepdims=True)
                o_ref[h, cur, :] = _dot(e / den, vv, NN)
                l_ref[h, cur, :] = jnp.broadcast_to(mx + jnp.log(den), (ATT_BLK, ATT_DH))

    osh = jax.ShapeDtypeStruct((ATT_HG, S, ATT_DH), F32)
    kw = dict(name=f"att_fwd{gi}", grid=(ATT_HG // hp, dil), in_specs=_slab_specs(gi) + [_bias_spec(gi)],
              out_specs=tuple(_head_specs(gi, 2)), out_shape=(osh, osh))
    if comm is not None:
        return _carry(body, comm, **kw)(slabs, slabs, slabs, bias)
    return _pcall(body, compiler_params=_params(("parallel", "arbitrary")), **kw)(slabs, slabs, slabs, bias)


def _att_bwd(slabs, bias, o, lse, do, dlse, gi, comm=None):
    dil, nb, hp = _att_geometry(gi)
    per = ATT_HG // hp
    scale = ATT_DH ** -0.5
    wide = lambda t: jnp.concatenate([t, t], axis=1)

    def body(q_ref, k_ref, v_ref, bias_ref, o_ref, l_ref, do_ref, dl_ref, dq_ref, dk_ref, dv_ref, ds_ref):
        r = pl.program_id(1)

        @pl.when(r == 0)
        def _():
            ds_ref[...] = jnp.zeros_like(ds_ref)

        for h in range(hp):
            carry_k = carry_v = None
            for n in range(nb):
                cur = _blk(dil, r, n)
                q = q_ref[h, cur, :]
                dov = do_ref[h, cur, :]
                delta = jnp.sum(dov * o_ref[h, cur, :], axis=-1, keepdims=True)
                if n == 0:
                    own = pl.ds(ATT_BLK, ATT_BLK)
                    kk, vv = k_ref[h, cur, :], v_ref[h, cur, :]
                    s = _dot(q, kk, NT) * scale + bias_ref[h, :, own]
                    p = jnp.where(_att_valid_first(), jnp.exp(s - l_ref[h, cur, :]), 0.0)
                    ds = p * (_dot(dov, vv, NT) - delta + dl_ref[h, cur, :])
                    ds_ref[h, :, own] += ds
                    dq_ref[h, cur, :] = _dot(ds, kk, NN) * scale
                    carry_k, carry_v = _dot(ds, q, TN) * scale, _dot(p, dov, TN)
                    continue
                prev = _blk(dil, r, n - 1)
                kk = jnp.concatenate([k_ref[h, prev, :], k_ref[h, cur, :]], axis=0)
                vv = jnp.concatenate([v_ref[h, prev, :], v_ref[h, cur, :]], axis=0)
                s = _dot(q, kk, NT) * scale + bias_ref[h]
                p = jnp.where(_att_valid(n), jnp.exp(s - wide(l_ref[h, cur, :])), 0.0)
                dp = _dot(dov, vv, NT)
                ds = p * (dp - delta + wide(dl_ref[h, cur, :]))
                ds_ref[h] += ds
                dq_ref[h, cur, :] = _dot(ds, kk, NN) * scale
                dkk = _dot(ds, q, TN) * scale
                dvv = _dot(p, dov, TN)
                dk_ref[h, prev, :] = carry_k + dkk[:ATT_BLK]
                dv_ref[h, prev, :] = carry_v + dvv[:ATT_BLK]
                carry_k, carry_v = dkk[ATT_BLK:], dvv[ATT_BLK:]
            last = _blk(dil, r, nb - 1)
            dk_ref[h, last, :] = carry_k
            dv_ref[h, last, :] = carry_v

    osh = jax.ShapeDtypeStruct((ATT_HG, S, ATT_DH), F32)
    kw = dict(name=f"att_bwd{gi}", grid=(per, dil), in_specs=_slab_specs(gi) + [_bias_spec(gi)] + _head_specs(gi, 4),
              out_specs=(*_head_specs(gi, 3), pl.BlockSpec((hp, ATT_BLK, 2 * ATT_BLK), lambda g, r: (g, 0, 0))),
              out_shape=(osh, osh, osh, jax.ShapeDtypeStruct((ATT_HG, ATT_BLK, 2 * ATT_BLK), F32)))
    args = (slabs, slabs, slabs, bias, o, lse, do, dlse)
    if comm is not None:
        return _carry(body, comm, **kw)(*args)
    return _pcall(body, compiler_params=_params(("arbitrary", "arbitrary")), **kw)(*args)


AW = ATT_HG * ATT_DH


def _mix_weights(l0, l1, l2):
    mx = jnp.maximum(jnp.maximum(l0, l1), l2)
    e0, e1, e2 = jnp.exp(l0 - mx), jnp.exp(l1 - mx), jnp.exp(l2 - mx)
    den = e0 + e1 + e2
    return e0 / den, e1 / den, e2 / den


def _heads_spec():
    return pl.BlockSpec((ATT_HG, TR, ATT_DH), lambda i: (0, i, 0))


def _mix_fwd(os_, ls, comm=None):
    def body(o0, o1, o2, l0, l1, l2, att_ref):
        for h in range(ATT_HG):
            w0, w1, w2 = _mix_weights(l0[h], l1[h], l2[h])
            att_ref[:, h * ATT_DH:(h + 1) * ATT_DH] = (w0 * o0[h] + w1 * o1[h] + w2 * o2[h]).astype(BF16)

    kw = dict(name="mix_fwd", grid=(S // TR,), in_specs=[_heads_spec()] * 6, out_specs=_row_spec(AW),
              out_shape=jax.ShapeDtypeStruct((S, AW), BF16))
    if comm is not None:
        return _carry(body, comm, **kw)(*os_, *ls)
    return _pcall(body, compiler_params=_params(("parallel",)), **kw)(*os_, *ls)


def _mix_bwd(os_, ls, datt):
    def body(o0, o1, o2, l0, l1, l2, da_ref, d0, d1, d2, e0, e1, e2):
        for h in range(ATT_HG):
            ws = _mix_weights(l0[h], l1[h], l2[h])
            da = da_ref[:, h * ATT_DH:(h + 1) * ATT_DH]
            dws = []
            for o_ref, w, d_ref in zip((o0, o1, o2), ws, (d0, d1, d2)):
                d_ref[h] = w * da
                dws.append(jnp.broadcast_to(jnp.sum(da * o_ref[h], axis=-1, keepdims=True), (TR, ATT_DH)))
            tot = ws[0] * dws[0] + ws[1] * dws[1] + ws[2] * dws[2]
            for w, dw, e_ref in zip(ws, dws, (e0, e1, e2)):
                e_ref[h] = w * (dw - tot)

    o = jax.ShapeDtypeStruct((ATT_HG, S, ATT_DH), F32)
    return _pcall(body, name="mix_bwd", grid=(S // TR,), in_specs=[_heads_spec()] * 6 + [_row_spec(AW)],
                  out_specs=(_heads_spec(),) * 6, out_shape=(o,) * 6,
                  compiler_params=_params(("parallel",)))(*os_, *ls, datt)


def _ada_fwd(c_all, w_sh, b_sl):
    def body(c_ref, w_ref, b_ref, o_ref):
        cv = c_ref[...]
        o_ref[...] = _dot(cv * jax.nn.sigmoid(cv), w_ref[...], NN) + b_ref[...]

    return _pcall(body, name="ada_fwd", out_shape=jax.ShapeDtypeStruct((N_DEV, w_sh.shape[1]), F32),
                  compiler_params=_params())(c_all, w_sh, b_sl)


def _ada_bwd(c_all, dm_sl):
    def body(c_ref, d_ref, o_ref):
        cv = c_ref[...]
        o_ref[...] = _dot(cv * jax.nn.sigmoid(cv), d_ref[...], TN)

    return _pcall(body, name="ada_bwd", out_shape=jax.ShapeDtypeStruct((D, dm_sl.shape[1]), F32),
                  compiler_params=_params())(c_all, dm_sl)


N_MOD = 6


def _sum_small(gathered):
    n = len(gathered)

    def body(*refs):
        ins, (gb_ref, dm_ref), outs = refs[:n], refs[n:n + 2], refs[n + 2:]

        def total(r):
            acc = r[0]
            for e in range(1, N_DEV):
                acc = acc + r[e]
            return acc

        for i in range(N_MOD):
            cols = slice(i * D, (i + 1) * D)
            gb_ref[:, cols] = total(ins[i])
            for e in range(N_DEV):
                dm_ref[e:e + 1, cols] = ins[i][e]
        for r, o_ref in zip(ins[N_MOD:], outs):
            o_ref[...] = total(r)

    shapes = (jax.ShapeDtypeStruct((1, N_MOD * D), F32), jax.ShapeDtypeStruct((N_DEV, N_MOD * D), F32),
              *[jax.ShapeDtypeStruct(g.shape[1:], F32) for g in gathered[N_MOD:]])
    res = _pcall(body, name="sum_small", out_shape=shapes, compiler_params=_params())(*gathered)
    return res[0], res[1], res[2:]


def _row_tile(m, n):
    t = max(8, min(m, (1 << 19) // n // 8 * 8))
    while m % t:
        t -= 8
    return t


def _pair_sum(full, recv, sel, name, col_block=0):
    _, m, n = recv.shape
    t = _row_tile(m, n)

    def body(sel_ref, a_ref, b_ref, o_ref):
        o_ref[...] = (a_ref[...].astype(F32) + b_ref[...].astype(F32)).astype(o_ref.dtype)

    gs = pltpu.PrefetchScalarGridSpec(
        num_scalar_prefetch=1, grid=(4, m // t),
        in_specs=[pl.BlockSpec((None, None, t, n), lambda q, i, s: (q, s[0], i, col_block)),
                  pl.BlockSpec((None, t, n), lambda q, i, s: (q, i, 0))],
        out_specs=pl.BlockSpec((None, t, n), lambda q, i, s: (q, i, 0)))
    return _pcall(body, name=name, grid_spec=gs, out_shape=jax.ShapeDtypeStruct((4, m, n), full.dtype),
                  compiler_params=_params(("parallel", "parallel")))(sel, full, recv)


def _chip_sum(part, recv, sel, name):
    _, m, n = part.shape
    t = _row_tile(m, n)

    def body(sel_ref, a_ref, r_ref, o_ref):
        o_ref[...] = ((a_ref[...].astype(F32) + r_ref[0].astype(F32)) + r_ref[1].astype(F32)) + r_ref[2].astype(F32)

    gs = pltpu.PrefetchScalarGridSpec(
        num_scalar_prefetch=1, grid=(m // t,),
        in_specs=[pl.BlockSpec((None, t, n), lambda i, s: (s[0], i, 0)),
                  pl.BlockSpec((3, t, n), lambda i, s: (0, i, 0))],
        out_specs=pl.BlockSpec((t, n), lambda i, s: (i, 0)))
    return _pcall(body, name=name, grid_spec=gs, out_shape=jax.ShapeDtypeStruct((m, n), F32),
                  compiler_params=_params(("parallel",)))(sel, part, recv)


def _adamw_math(w, g, m, v):
    nm = ADAM_B1 * m + (1.0 - ADAM_B1) * g
    nv = ADAM_B2 * v + (1.0 - ADAM_B2) * (g * g)
    m_hat = nm / (1.0 - ADAM_B1 ** ADAM_STEP)
    v_hat = nv / (1.0 - ADAM_B2 ** ADAM_STEP)
    return -ADAM_LR * (m_hat / (jnp.sqrt(v_hat) + ADAM_EPS) + ADAM_WD * w), nm, nv


def _adamw(w, g, m, v, name):
    _, rows, cols = w.shape
    t = _row_tile(rows, cols)

    def body(w_ref, g_ref, m_ref, v_ref, d_ref, nm_ref, nv_ref):
        d_ref[...], nm_ref[...], nv_ref[...] = _adamw_math(w_ref[...], g_ref[...], m_ref[...], v_ref[...])

    spec3 = pl.BlockSpec((None, t, cols), lambda i: (0, i, 0))
    spec2 = pl.BlockSpec((t, cols), lambda i: (i, 0))
    o = jax.ShapeDtypeStruct(w.shape, F32)
    return _pcall(body, name=name, grid=(rows // t,), in_specs=[spec3, spec2, spec3, spec3], out_specs=(spec3,) * 3,
                  out_shape=(o, o, o), compiler_params=_params(("parallel",)))(w, g, m, v)


def _adamw_reduced1(w, m, v, part, recv, sel, name):
    _, rows, cols = w.shape
    t = _row_tile(rows, cols)

    def body(sel_ref, w_ref, m_ref, v_ref, p_ref, r_ref, g_ref, d_ref, nm_ref, nv_ref):
        g = ((p_ref[...].astype(F32) + r_ref[0].astype(F32)) + r_ref[1].astype(F32)) + r_ref[2].astype(F32)
        g_ref[...] = g
        d_ref[...], nm_ref[...], nv_ref[...] = _adamw_math(w_ref[...], g, m_ref[...], v_ref[...])

    wspec = pl.BlockSpec((None, t, cols), lambda i, s: (0, i, 0))
    gs = pltpu.PrefetchScalarGridSpec(
        num_scalar_prefetch=1, grid=(rows // t,),
        in_specs=[wspec, wspec, wspec, pl.BlockSpec((None, t, cols), lambda i, s: (s[0], i, 0)),
                  pl.BlockSpec((3, t, cols), lambda i, s: (0, i, 0))],
        out_specs=(wspec,) * 4)
    o = jax.ShapeDtypeStruct(w.shape, F32)
    return _pcall(body, name=name, grid_spec=gs, out_shape=(o, o, o, o),
                  compiler_params=_params(("parallel",)))(sel, w, m, v, part, recv)


def _adamw_reduced(w, m, v, parts, recvs, sel):
    _, rows, cols = w.shape
    half = cols // 2
    t = _row_tile(rows, half)

    def body(sel_ref, w_ref, m_ref, v_ref, pa_ref, pb_ref, ra_ref, rb_ref, g_ref, d_ref, nm_ref, nv_ref):
        total = lambda p_ref, r_ref: ((p_ref[...].astype(F32) + r_ref[0].astype(F32)) + r_ref[1].astype(F32)) \
            + r_ref[2].astype(F32)
        g = jnp.where(pl.program_id(1) == 0, total(pa_ref, ra_ref), total(pb_ref, rb_ref))
        g_ref[...] = g
        d_ref[...], nm_ref[...], nv_ref[...] = _adamw_math(w_ref[...], g, m_ref[...], v_ref[...])

    wspec = pl.BlockSpec((None, t, half), lambda i, j, s: (0, i, j))
    pspec = pl.BlockSpec((None, t, half), lambda i, j, s: (s[0], i, 0))
    rspec = pl.BlockSpec((3, t, half), lambda i, j, s: (0, i, 0))
    gs = pltpu.PrefetchScalarGridSpec(num_scalar_prefetch=1, grid=(rows // t, 2),
                                      in_specs=[wspec, wspec, wspec, pspec, pspec, rspec, rspec],
                                      out_specs=(wspec,) * 4)
    o = jax.ShapeDtypeStruct(w.shape, F32)
    return _pcall(body, name="adamw_w_in", grid_spec=gs, out_shape=(o, o, o, o),
                  compiler_params=_params(("parallel", "arbitrary")))(sel, w, m, v, *parts, *recvs)


def _adamw_small(ws, gs, ms, vs):
    n = len(ws)

    def body(*refs):
        for i in range(n):
            w_ref, g_ref, m_ref, v_ref = (refs[k * n + i] for k in range(4))
            d, nm, nv = _adamw_math(w_ref[...], g_ref[...], m_ref[...], v_ref[...])
            refs[4 * n + i][...] = d
            refs[5 * n + i][...] = nm
            refs[6 * n + i][...] = nv

    shapes = tuple(jax.ShapeDtypeStruct(w.shape, F32) for w in ws)
    res = _pcall(body, name="adamw_small", out_shape=shapes * 3, compiler_params=_params())(*ws, *gs, *ms, *vs)
    return res[:n], res[n:2 * n], res[2 * n:]


def _mesh_pos():
    return lax.axis_index("x"), lax.axis_index("y"), lax.axis_index("c")


class _Gather:
    def __init__(self, arrs):
        self.ins = list(arrs)
        self.out_shape = tuple(jax.ShapeDtypeStruct((N_DEV,) + a.shape, a.dtype) for a in arrs)
        n = len(arrs)
        self.sems = [pltpu.SemaphoreType.DMA((7 * n,)), pltpu.SemaphoreType.DMA((7 * n,)),
                     pltpu.SemaphoreType.DMA((n,))]

    def _copies(self, ins, outs, sems):
        send_sems, recv_sems, local_sems = sems
        x, y, c = _mesh_pos()
        me, sibling = (x, y, c), (x, y, 1 - c)
        chips = [(1 - x, y), (x, 1 - y), (1 - x, 1 - y)]

        def copy(p, k, block, to, from_input=False):
            dst = outs[p].at[_slot(block)]
            return pltpu.make_async_remote_copy(
                src_ref=ins[p] if from_input else dst, dst_ref=dst, send_sem=send_sems.at[7 * p + k],
                recv_sem=recv_sems.at[7 * p + k], device_id=to, device_id_type=MESH)

        npc = len(self.ins)
        mine = [pltpu.make_async_copy(ins[p], outs[p].at[_slot(me)], local_sems.at[p]) for p in range(npc)]
        first = []
        for p in range(npc):
            first.append(copy(p, 0, me, sibling, from_input=True))
            first += [copy(p, 1 + j, me, (*chip, c), from_input=True) for j, chip in enumerate(chips)]
        return me, sibling, chips, c, copy, mine, first

    def start(self, ins, outs, sems):
        *_, mine, first = self._copies(ins, outs, sems)
        for cp in mine + first:
            cp.start()

    def finish(self, ins, outs, sems):
        me, sibling, chips, c, copy, mine, first = self._copies(ins, outs, sems)
        npc = len(self.ins)
        passed = []
        for p in range(npc):
            for j, chip in enumerate(chips):
                copy(p, 1 + j, (*chip, c), me).wait_recv()
                passed.append(copy(p, 4 + j, (*chip, c), sibling))
                passed[-1].start()
        for p in range(npc):
            copy(p, 0, sibling, me).wait_recv()
            for j, chip in enumerate(chips):
                copy(p, 4 + j, (*chip, 1 - c), me).wait_recv()
        for cp in first + passed:
            cp.wait_send()
        for cp in mine:
            cp.wait()


class _ExchangeCore:
    def __init__(self, fulls, cols=None):
        self.ins = list(fulls)
        self.cols = cols
        width = lambda f: f.shape[3] if cols is None else cols[1]
        self.out_shape = tuple(jax.ShapeDtypeStruct((4, f.shape[2], width(f)), f.dtype) for f in fulls)
        self.sems = [pltpu.SemaphoreType.DMA((4 * len(fulls),)), pltpu.SemaphoreType.DMA((4 * len(fulls),))]

    def _copies(self, ins, outs, sems):
        send_sems, recv_sems = sems
        x, y, c = _mesh_pos()

        def src(a, q):
            ref = ins[a].at[q, 1 - c]
            return ref if self.cols is None else ref.at[:, pl.ds(*self.cols)]

        return [pltpu.make_async_remote_copy(
            src_ref=src(a, q), dst_ref=outs[a].at[q], send_sem=send_sems.at[4 * a + q],
            recv_sem=recv_sems.at[4 * a + q], device_id=(x, y, 1 - c), device_id_type=MESH)
            for a in range(len(self.ins)) for q in range(4)]

    def start(self, ins, outs, sems):
        for cp in self._copies(ins, outs, sems):
            cp.start()

    def finish(self, ins, outs, sems):
        for cp in self._copies(ins, outs, sems):
            cp.wait()


class _ExchangeChip:
    def __init__(self, parts):
        self.ins = list(parts)
        self.out_shape = tuple(jax.ShapeDtypeStruct((3,) + p.shape[1:], p.dtype) for p in parts)
        self.sems = [pltpu.SemaphoreType.DMA((3 * len(parts),)), pltpu.SemaphoreType.DMA((3 * len(parts),))]

    def _copies(self, ins, outs, sems):
        send_sems, recv_sems = sems
        x, y, c = _mesh_pos()
        chips = [(1 - x, y), (x, 1 - y), (1 - x, 1 - y)]
        return [pltpu.make_async_remote_copy(
            src_ref=ins[a].at[2 * px + py], dst_ref=outs[a].at[j], send_sem=send_sems.at[3 * a + j],
            recv_sem=recv_sems.at[3 * a + j], device_id=(px, py, c), device_id_type=MESH)
            for a in range(len(self.ins)) for j, (px, py) in enumerate(chips)]

    def start(self, ins, outs, sems):
        for cp in self._copies(ins, outs, sems):
            cp.start()

    def finish(self, ins, outs, sems):
        for cp in self._copies(ins, outs, sems):
            cp.wait()


HBM_ONLY = pl.BlockSpec(memory_space=pltpu.HBM)
SEM_SPEC = pl.BlockSpec(memory_space=pltpu.SEMAPHORE)
SIDE_EFFECT = pltpu.SideEffectType.DATAFLOW_SIDE_EFFECTING


def _chip_copies(p_refs, land_refs, send_sems, recv_sems):
    x, y, c = _mesh_pos()
    return [pltpu.make_async_remote_copy(
        src_ref=p_refs[a].at[2 * px + py], dst_ref=land_refs[a].at[j], send_sem=send_sems.at[3 * a + j],
        recv_sem=recv_sems.at[3 * a + j], device_id=(px, py, c), device_id_type=MESH)
        for a in range(len(p_refs)) for j, (px, py) in enumerate([(1 - x, y), (x, 1 - y), (1 - x, 1 - y)])]


def _chip_exchange_start(parts, name):
    n = len(parts)
    lands = [lax.empty((3,) + p.shape[1:], p.dtype) for p in parts]

    def body(*refs):
        p_refs, land_refs, (send_sems, recv_sems) = refs[:n], refs[n:2 * n], refs[2 * n:2 * n + 2]
        for cp in _chip_copies(p_refs, land_refs, send_sems, recv_sems):
            cp.start()
        token = refs[-1]
        token[...] = jnp.zeros_like(token)

    hbm = lambda t: pltpu.HBM(t.shape, t.dtype)
    res = pl.pallas_call(
        body, name=name,
        out_shape=(pltpu.SemaphoreType.DMA((3 * n,)), pltpu.SemaphoreType.DMA((3 * n,)), *[hbm(t) for t in parts + lands],
                   jax.ShapeDtypeStruct((8, 128), F32)),
        in_specs=(HBM_ONLY,) * (2 * n),
        out_specs=(SEM_SPEC, SEM_SPEC, *[HBM_ONLY] * (2 * n), pl.BlockSpec(memory_space=pltpu.VMEM)),
        input_output_aliases={i: 2 + i for i in range(2 * n)},
        compiler_params=pltpu.CompilerParams(has_side_effects=SIDE_EFFECT))(
        *[pltpu.with_memory_space_constraint(t, pltpu.HBM) for t in parts + lands])
    return (res[0], res[1], list(res[2:2 + n]), list(res[2 + n:2 + 2 * n])), res[-1]


def _chip_exchange_wait(in_flight, after, name):
    send_sems, recv_sems, parts, lands = in_flight
    n = len(parts)

    def body(*refs):
        p_refs, land_refs, (send_sems, recv_sems) = refs[:n], refs[n:2 * n], refs[2 * n:2 * n + 2]
        for cp in _chip_copies(p_refs, land_refs, send_sems, recv_sems):
            cp.wait_send()
            cp.wait_recv()

    res = pl.pallas_call(
        body, name=name, out_shape=tuple(pltpu.HBM(t.shape, t.dtype) for t in parts + lands),
        in_specs=(*[HBM_ONLY] * (2 * n), SEM_SPEC, SEM_SPEC, pl.BlockSpec(memory_space=pl.ANY)),
        out_specs=(HBM_ONLY,) * (2 * n), input_output_aliases={i: i for i in range(2 * n)},
        compiler_params=pltpu.CompilerParams(has_side_effects=SIDE_EFFECT))(*parts, *lands, send_sems, recv_sems, after)
    return list(res[:n]), list(res[n:])


def _slot(p):
    return 4 * p[0] + 2 * p[1] + p[2]


def _gather_copies(src_refs, out_refs, send_sems, recv_sems):
    x, y, c = _mesh_pos()
    targets = [(x, y, 1 - c), (1 - x, y, c), (x, 1 - y, c), (1 - x, 1 - y, c)]
    return [pltpu.make_async_remote_copy(
        src_ref=src_refs[a], dst_ref=out_refs[a].at[_slot((x, y, c))], send_sem=send_sems.at[4 * a + k],
        recv_sem=recv_sems.at[4 * a + k], device_id=to, device_id_type=MESH)
        for a in range(len(src_refs)) for k, to in enumerate(targets)]


def _gather_start(shards, after, name):
    n = len(shards)
    outs = [lax.empty((N_DEV,) + s.shape, s.dtype) for s in shards]

    def body(*refs):
        for cp in _gather_copies(refs[:n], refs[n:2 * n], refs[2 * n + 1], refs[2 * n + 2]):
            cp.start()
        token = refs[-1]
        token[...] = jnp.zeros_like(token)

    res = pl.pallas_call(
        body, name=name,
        out_shape=(pltpu.SemaphoreType.DMA((4 * n,)), pltpu.SemaphoreType.DMA((4 * n,)),
                   *[pltpu.HBM(t.shape, t.dtype) for t in shards + outs], jax.ShapeDtypeStruct((8, 128), F32)),
        in_specs=(*[HBM_ONLY] * (2 * n), pl.BlockSpec(memory_space=pl.ANY)),
        out_specs=(SEM_SPEC, SEM_SPEC, *[HBM_ONLY] * (2 * n), pl.BlockSpec(memory_space=pltpu.VMEM)),
        input_output_aliases={i: 2 + i for i in range(2 * n)},
        compiler_params=pltpu.CompilerParams(has_side_effects=SIDE_EFFECT))(
        *[pltpu.with_memory_space_constraint(t, pltpu.HBM) for t in shards + outs], after)
    return (res[0], res[1], list(res[2:2 + n]), list(res[2 + n:2 + 2 * n])), res[-1]


def _gather_wait(in_flight, after, name):
    send_sems, recv_sems, shards, outs = in_flight
    n = len(shards)

    def body(*refs):
        for cp in _gather_copies(refs[:n], refs[n:2 * n], refs[2 * n], refs[2 * n + 1]):
            cp.wait_send()
            cp.wait_recv()

    res = pl.pallas_call(
        body, name=name, out_shape=tuple(pltpu.HBM(t.shape, t.dtype) for t in shards + outs),
        in_specs=(*[HBM_ONLY] * (2 * n), SEM_SPEC, SEM_SPEC, pl.BlockSpec(memory_space=pl.ANY)),
        out_specs=(HBM_ONLY,) * (2 * n), input_output_aliases={i: i for i in range(2 * n)},
        compiler_params=pltpu.CompilerParams(has_side_effects=SIDE_EFFECT))(*shards, *outs, send_sems, recv_sems, after)
    return list(res[:n]), list(res[n:])


class _PassToSibling:
    def __init__(self, shards, gathered):
        n = self.n = len(shards)
        self.ins = list(shards) + list(gathered)
        self.out_shape = tuple(jax.ShapeDtypeStruct(g.shape, g.dtype) for g in gathered)
        self.aliases = {n + a: a for a in range(n)}
        self.sems = [pltpu.SemaphoreType.DMA((3 * n,)), pltpu.SemaphoreType.DMA((3 * n,)),
                     pltpu.SemaphoreType.DMA((n,))]

    def _copies(self, ins, outs, sems):
        send_sems, recv_sems, local_sems = sems
        x, y, c = _mesh_pos()
        chips = [(1 - x, y), (x, 1 - y), (1 - x, 1 - y)]
        mine = [pltpu.make_async_copy(ins[a], outs[a].at[_slot((x, y, c))], local_sems.at[a]) for a in range(self.n)]
        passed, awaited = [], []
        for a in range(self.n):
            for j, chip in enumerate(chips):
                sems_j = dict(send_sem=send_sems.at[3 * a + j], recv_sem=recv_sems.at[3 * a + j],
                              device_id=(x, y, 1 - c), device_id_type=MESH)
                blk = outs[a].at[_slot((*chip, c))]
                passed.append(pltpu.make_async_remote_copy(src_ref=blk, dst_ref=blk, **sems_j))
                got = outs[a].at[_slot((*chip, 1 - c))]
                awaited.append(pltpu.make_async_remote_copy(src_ref=got, dst_ref=got, **sems_j))
        return mine, passed, awaited

    def start(self, ins, outs, sems):
        mine, passed, _ = self._copies(ins, outs, sems)
        for cp in mine + passed:
            cp.start()

    def finish(self, ins, outs, sems):
        mine, passed, awaited = self._copies(ins, outs, sems)
        for cp in passed:
            cp.wait_send()
        for cp in awaited:
            cp.wait_recv()
        for cp in mine:
            cp.wait()


def _reduce_sums(fulls, recv_core, core, tag):
    return [_pair_sum(f, r, core, f"rs_pair_{tag}{i}") for i, (f, r) in enumerate(zip(fulls, recv_core))]


def _local_step(x, tgt, mods, w_in_shard, order, shards, small, chip, core):
    sh1, sc1, g1, sh2, sc2, g2 = mods
    norm1_g, rel_bias, gn_g, gn_b, norm2_g, norm_f_g = small
    tables = _ret_tables()
    buckets = jnp.asarray(_bucket_tables())

    h1 = _norm_mod_fwd(x, norm1_g, sh1, sc1, "norm1_fwd")
    proj, slabs, w_in_t = _gather_proj(h1, w_in_shard, order)
    flight_w1, token_w = _gather_start(list(shards[:3]), proj, "gather_w1_start")
    flight_w2, token_w = _gather_start(list(shards[3:]), token_w, "gather_w2_start")
    bias = _bias_build(rel_bias, buckets, token_w)
    outs, lses = [], []
    for gi in range(len(ATT_GROUPS)):
        o, l = _att_fwd(slabs, bias, gi)
        outs.append(o)
        lses.append(l)
    att, gathered = _mix_fwd(outs, lses, comm=_PassToSibling(*_gather_wait(flight_w1, lses[2], "gather_w1_wait")))
    w_ret_out, w_att_out, w_o = (_from_slots(g, ax) for g, ax in zip(gathered, BIG_AXES[1:4]))
    gated, ro, states = _ret_fwd(proj, tables, gn_g, gn_b, att)
    ret_out, gathered = _mm(gated, w_ret_out, 'nn', tm=S, tn=256, tk=2048, name="ret_out",
                            comm=_PassToSibling(*_gather_wait(flight_w2, gated, "gather_w2_wait")))
    w_ff1, w_ff2 = (_from_slots(g, ax) for g, ax in zip(gathered, BIG_AXES[4:]))
    att_out, merged = _att_out_merge(att, w_att_out, proj, ret_out)
    mixo, x1, h2 = _w_o_norm2(merged, w_o, x, g1, norm2_g, sh2, sc2)
    u, act = _mm(h2, w_ff1, 'nn', tm=S, tn=512, tk=D, name="ff1", relu2=True)
    loss, dx2, g_normf, df, dg2 = _ff2_final(act, w_ff2, x1, g2, tgt, norm_f_g)

    gw_ff2 = _mm(act, df, 'tn', tm=512, tn=D, tk=S, name="gw_ff2", out_dtype=BF16)
    du = _mm(df, w_ff2, 'nt', tm=S, tn=512, tk=D, name="d_act", out_dtype=BF16, relu2_of=u)
    gw_ff1 = _mm(h2, du, 'tn', tm=D, tn=512, tk=S, name="gw_ff1", out_dtype=BF16)
    fulls_a = [_to_slots(g, ax) for g, ax in zip((gw_ff1, gw_ff2), BIG_AXES[4:])]
    dh2, recv_core_a = _mm(du, w_ff1, 'nt', tm=1024, tn=1024, tk=2048, name="dh2", comm=_ExchangeCore(fulls_a))
    parts_a = _reduce_sums(fulls_a, recv_core_a, core, "a")
    flight_a, token_a = _chip_exchange_start(parts_a, "rs_a_start")
    dx1, dsc2, dsh2, g_norm2, dmixo, dg1 = _norm_mod_bwd(x1, norm2_g, sc2, dh2, dx2, "norm2_bwd", gate=(mixo, g1))

    gw_o = _mm(merged, dmixo, 'tn', tm=D, tn=512, tk=S, name="gw_o", out_dtype=BF16, after=token_a)
    d_ret_out, d_att_out, dga, dgb = _dmerged_split(dmixo, w_o, proj, ret_out, att_out)
    gw_ret_out = _mm(gated, d_ret_out, 'tn', tm=512, tn=D, tk=S, name="gw_ret_out", out_dtype=BF16)
    gw_att_out = _mm(att, d_att_out, 'tn', tm=AW, tn=D, tk=S, name="gw_att_out", out_dtype=BF16)
    fulls_b = [_to_slots(g, ax) for g, ax in zip((gw_ret_out, gw_att_out, gw_o), BIG_AXES[1:4])]
    dgated, recv_core_b = _mm(d_ret_out, w_ret_out, 'nt', tm=S, tn=512, tk=D, name="dgated",
                              comm=_ExchangeCore(fulls_b))
    parts_b = _reduce_sums(fulls_b, recv_core_b, core, "b")
    flight_b, token_b = _chip_exchange_start(parts_b, "rs_b_start")
    datt = _mm(d_att_out, w_att_out, 'nt', tm=S, tn=AW, tk=D, name="datt", after=token_b)
    mix_grads = _mix_bwd(outs, lses, datt)
    datt_parts, ds_sums = [], []
    for gi in range(len(ATT_GROUPS)):
        dq, dk, dv, ds_sum = _att_bwd(slabs, bias, outs[gi], lses[gi], mix_grads[gi], mix_grads[3 + gi], gi)
        datt_parts += [dq, dk, dv]
        ds_sums.append(ds_sum)
    g_bias = _bias_grad(jnp.concatenate(ds_sums, axis=0), buckets)[:, :, 0].T.reshape(1, -1)
    dproj, g_gn_g, g_gn_b = _ret_bwd(proj, tables, gn_g, gn_b, ro, states, dgated, datt_parts + [dga, dgb])
    parts_a, recv_chip_a = _chip_exchange_wait(flight_a, dproj, "rs_a_wait")
    parts_b, recv_chip_b = _chip_exchange_wait(flight_b, dproj, "rs_b_wait")
    reduced = list(zip(parts_b + parts_a, recv_chip_b + recv_chip_a))
    full_in = _to_slots(_mm(dproj, h1, 'tn', tm=512, tn=D, tk=S, name="gw_in", out_dtype=BF16), 0)
    halves = [(half * (D // 2), D // 2) for half in range(2)]
    (recv_core_in0,) = _run_comm(_ExchangeCore([full_in], cols=halves[0]), "rs_core_in0")
    flight0, token = _chip_exchange_start([_pair_sum(full_in, recv_core_in0, core, "rs_pair_c0", col_block=0)],
                                          "rs_in0_start")
    dh1, (recv_core_in1,) = _mm(dproj, w_in_t, 'nn', tm=1024, tn=1024, tk=2560, name="dh1",
                                comm=_ExchangeCore([full_in], cols=halves[1]), after=token)
    flight1, token = _chip_exchange_start([_pair_sum(full_in, recv_core_in1, core, "rs_pair_c1", col_block=1)],
                                          "rs_in1_start")
    in_flight = [flight0, flight1]
    gx, dsc1, dsh1, g_norm1 = _norm_mod_bwd(x, norm1_g, sc1, dh1, dx1, "norm1_bwd", after=token)

    dmod = [dsh1, dsc1, dg1, dsh2, dsc2, dg2]
    small_g = [g_norm1, g_bias, g_gn_g, g_gn_b, g_norm2, g_normf]
    return loss, gx, in_flight, reduced, small_g, dmod


def _to_slots(g, axis):
    if axis == 0:
        return g.reshape(4, 2, g.shape[0] // N_DEV, g.shape[1])
    return g.reshape(g.shape[0], N_DEV, g.shape[1] // N_DEV).transpose(1, 0, 2).reshape(4, 2, g.shape[0], -1)


def _from_slots(w8, axis):
    if axis == 0:
        return w8.reshape(-1, w8.shape[2])
    return w8.transpose(1, 0, 2).reshape(w8.shape[1], -1)


BIG_AXES = (1, 0, 1, 0, 1, 0)


def kernel(x, c, w_ada, b_ada, norm1_g, w_in, rel_bias, ret_gn_g, ret_gn_b, w_ret_out, w_att_out, w_o, norm2_g, w_ff1, w_ff2, norm_f_g, loss_target, m_w_ada, m_b_ada, m_norm1_g, m_w_in, m_rel_bias, m_ret_gn_g, m_ret_gn_b, m_w_ret_out, m_w_att_out, m_w_o, m_norm2_g, m_w_ff1, m_w_ff2, m_norm_f_g, v_w_ada, v_b_ada, v_norm1_g, v_w_in, v_rel_bias, v_ret_gn_g, v_ret_gn_b, v_w_ret_out, v_w_att_out, v_w_o, v_norm2_g, v_w_ff1, v_w_ff2, v_norm_f_g):
    mx, my, mc = _mesh_pos()
    dev = 4 * mx + 2 * my + mc
    chip = jnp.reshape(2 * mx + my, (1,)).astype(jnp.int32)
    core = jnp.reshape(mc, (1,)).astype(jnp.int32)
    ada_w = D * 6 // N_DEV

    w_in, m_w_in, v_w_in = (jnp.transpose(t, (0, 2, 1)) for t in (w_in, m_w_in, v_w_in))

    shards = [w[0].astype(BF16) for w in (w_in, w_ret_out, w_att_out, w_o, w_ff1, w_ff2)]
    (c_all,) = _run_comm(_Gather([c]), "gather_c")
    c_all = c_all.reshape(N_DEV, D)
    b_sl = lax.dynamic_slice(b_ada, (0, dev * ada_w), (1, ada_w))
    (mod_all,) = _run_comm(_Gather([_ada_fwd(c_all, w_ada[0], b_sl)]), "gather_mod")
    mod = lax.dynamic_index_in_dim(mod_all, dev, axis=1, keepdims=False).reshape(6, D)
    mods = tuple(mod[i:i + 1] for i in range(6))

    small = (norm1_g, rel_bias, ret_gn_g, ret_gn_b, norm2_g, norm_f_g.reshape(1, D))
    order = lax.dynamic_index_in_dim(jnp.asarray(_proj_order()), 2 * mx + my, axis=0, keepdims=False)
    loss, gx, in_flight, big_red, small_g, dmod = _local_step(x[0], loss_target[0], mods, shards[0], order,
                                                              shards[1:], small, chip, core)

    names = ['w_ada', 'b_ada', 'norm1_g', 'w_in', 'rel_bias', 'ret_gn_g', 'ret_gn_b', 'w_ret_out', 'w_att_out',
             'w_o', 'norm2_g', 'w_ff1', 'w_ff2', 'norm_f_g']
    ws = dict(zip(names, (w_ada, b_ada, norm1_g, w_in, rel_bias, ret_gn_g, ret_gn_b, w_ret_out, w_att_out, w_o,
                          norm2_g, w_ff1, w_ff2, norm_f_g)))
    ms = dict(zip(names, (m_w_ada, m_b_ada, m_norm1_g, m_w_in, m_rel_bias, m_ret_gn_g, m_ret_gn_b, m_w_ret_out,
                          m_w_att_out, m_w_o, m_norm2_g, m_w_ff1, m_w_ff2, m_norm_f_g)))
    vs = dict(zip(names, (v_w_ada, v_b_ada, v_norm1_g, v_w_in, v_rel_bias, v_ret_gn_g, v_ret_gn_b, v_w_ret_out,
                          v_w_att_out, v_w_o, v_norm2_g, v_w_ff1, v_w_ff2, v_norm_f_g)))
    grads, delta, new_m, new_v = {}, {}, {}, {}
    big_names = ('w_ret_out', 'w_att_out', 'w_o', 'w_ff1', 'w_ff2')
    for n, (part, recv) in zip(big_names, big_red):
        grads[n], delta[n], new_m[n], new_v[n] = _adamw_reduced1(ws[n], ms[n], vs[n], part, recv, chip, "adamw_" + n)
    updated = lax.optimization_barrier((gx, tuple(delta[n] for n in big_names)))
    gathered = _run_comm(_Gather(dmod + small_g + [loss]), "gather_small", after=updated[0])
    g_b_ada, dmod_all, (g_norm1, g_bias, g_gn_g, g_gn_b, g_norm2, g_normf, loss_sum) = _sum_small(gathered)
    loss_out = loss_sum[0, 0]
    g_w_ada = _ada_bwd(c_all, lax.dynamic_slice(dmod_all, (0, dev * ada_w), (N_DEV, ada_w)))

    grads.update(w_ada=g_w_ada.reshape(w_ada.shape), b_ada=g_b_ada, norm1_g=g_norm1, rel_bias=g_bias,
                 ret_gn_g=g_gn_g, ret_gn_b=g_gn_b, norm2_g=g_norm2, norm_f_g=g_normf)
    delta['w_ada'], new_m['w_ada'], new_v['w_ada'] = _adamw(w_ada, g_w_ada, m_w_ada, v_w_ada, "adamw_w_ada")
    small_names = ('b_ada', 'norm1_g', 'rel_bias', 'ret_gn_g', 'ret_gn_b', 'norm2_g', 'norm_f_g')
    two_d = {n: (1, ws[n].size) if ws[n].ndim == 1 else ws[n].shape for n in small_names}
    d_, m_, v_ = _adamw_small(*[[src[n].reshape(two_d[n]) for n in small_names] for src in (ws, grads, ms, vs)])
    for i, n in enumerate(small_names):
        shp = ws[n].shape
        delta[n], new_m[n], new_v[n] = d_[i].reshape(shp), m_[i].reshape(shp), v_[i].reshape(shp)
        grads[n] = grads[n].reshape(shp)

    done = lax.optimization_barrier((gx, tuple(d_), tuple(delta[n] for n in ('w_ada', 'w_ret_out', 'w_att_out', 'w_o',
                                                                               'w_ff1', 'w_ff2'))))
    parts_in, recvs_in = [], []
    for half, flight in enumerate(in_flight):
        (part_in,), (recv_chip_in,) = _chip_exchange_wait(flight, done[0], f"rs_in{half}_wait")
        parts_in.append(part_in)
        recvs_in.append(recv_chip_in)
    grads['w_in'], delta['w_in'], new_m['w_in'], new_v['w_in'] = _adamw_reduced(w_in, m_w_in, v_w_in, parts_in,
                                                                               recvs_in, chip)
    for d in (grads, delta, new_m, new_v):
        d['w_in'] = jnp.transpose(d['w_in'], (0, 2, 1))
    return (loss_out, gx[None], *[grads[n] for n in names], *[delta[n] for n in names],
            *[new_m[n] for n in names], *[new_v[n] for n in names])
```

```python
import functools
import math

import numpy as np
import jax
import jax.numpy as jnp
from jax import lax
from jax.experimental import pallas as pl
from jax.experimental.pallas import tpu as pltpu

F32 = jnp.float32
BF16 = jnp.bfloat16
MESH = pl.DeviceIdType.MESH

N_DEV = 8
S = 2048
D = 1024
RET_HEADS = 4
RET_DK = 256
RET_DV = 512
CHUNK = 128
N_CHUNK = S // CHUNK
ATT_GROUPS = ((128, 1), (512, 4), (2048, 16))
ATT_HG = 4
ATT_DH = 128
ATT_BLK = 128
N_BUCKETS = 32
MAX_DIST = 2048
D_FF = 4096
IN_COLS = 12800
OFF_RQ, OFF_RK, OFF_RV, OFF_RG, OFF_ATT = 0, 1024, 2048, 4096, 6144
OFF_GA, OFF_GB = 6144, 7168
RMS_EPS = 1e-6
GN_EPS = 1e-5
ADAM_LR, ADAM_B1, ADAM_B2, ADAM_EPS, ADAM_WD, ADAM_STEP = 0.001, 0.9, 0.999, 1e-08, 0.01, 10
VMEM_LIMIT = 48 * 1024 * 1024


def _pcall(body, **kw):
    return pl.pallas_call(body, **kw)


def _params(sem=None):
    return pltpu.CompilerParams(dimension_semantics=sem, vmem_limit_bytes=VMEM_LIMIT)


HBM_SPEC = pl.BlockSpec(memory_space=pl.ANY)


def _carry(body, comm, *, name, grid, in_specs, out_specs, out_shape, scratch_shapes=()):
    single = not isinstance(out_specs, (tuple, list))
    o_specs = (out_specs,) if single else tuple(out_specs)
    o_shape = (out_shape,) if single else tuple(out_shape)
    n_in, n_out, n_scr = len(in_specs), len(o_specs), len(scratch_shapes)
    nci, nco = len(comm.ins), len(comm.out_shape)
    total = int(np.prod(grid))

    def wrapped(*refs):
        bounds = np.cumsum([0, n_in, nci, n_out, nco, n_scr])
        a, ci, o, co, scr = (refs[bounds[i]:bounds[i + 1]] for i in range(5))
        sems = refs[bounds[5]:]
        flat = 0
        for d, g in enumerate(grid):
            flat = flat * g + pl.program_id(d)

        @pl.when(flat == 0)
        def _():
            comm.start(ci, co, sems)

        body(*a, *o, *scr)

        @pl.when(flat == total - 1)
        def _():
            comm.finish(ci, co, sems)

    aliases = {n_in + i: n_out + o for i, o in getattr(comm, "aliases", {}).items()}
    call = _pcall(wrapped, name=name, grid=grid, in_specs=list(in_specs) + [HBM_SPEC] * nci,
                  out_specs=o_specs + (HBM_SPEC,) * nco, out_shape=o_shape + tuple(comm.out_shape),
                  scratch_shapes=list(scratch_shapes) + list(comm.sems), input_output_aliases=aliases,
                  compiler_params=_params(("arbitrary",) * len(grid)))

    def run(*args):
        res = call(*args, *comm.ins)
        own = res[0] if single else tuple(res[:n_out])
        return own, tuple(res[n_out:])

    return run


def _run_comm(comm, name, after=None):
    nci, nco = len(comm.ins), len(comm.out_shape)
    extra = [] if after is None else [after]

    def body(*refs):
        ci, co, sems = refs[:nci], refs[nci + len(extra):nci + len(extra) + nco], refs[nci + len(extra) + nco:]
        comm.start(ci, co, sems)
        comm.finish(ci, co, sems)

    return _pcall(body, name=name, in_specs=[HBM_SPEC] * (nci + len(extra)), out_specs=(HBM_SPEC,) * nco,
                  out_shape=tuple(comm.out_shape), scratch_shapes=list(comm.sems))(*comm.ins, *extra)


def _dot(a, b, dn):
    return lax.dot_general(a.astype(BF16), b.astype(BF16), (dn, ((), ())), preferred_element_type=F32)


NN = ((1,), (0,))
NT = ((1,), (1,))
TN = ((0,), (0,))


def _mm(a, b, mode, *, tm, tn, tk, name, out_dtype=F32, res=None, gvec=None, relu2=False, relu2_of=None, comm=None,
        after=None):
    if mode == 'nn':
        (M, K), (_, N) = a.shape, b.shape
        a_spec = pl.BlockSpec((tm, tk), lambda i, j, k: (i, k))
        b_spec = pl.BlockSpec((tk, tn), lambda i, j, k: (k, j))
        dn = NN
    elif mode == 'nt':
        (M, K), (N, _) = a.shape, b.shape
        a_spec = pl.BlockSpec((tm, tk), lambda i, j, k: (i, k))
        b_spec = pl.BlockSpec((tn, tk), lambda i, j, k: (j, k))
        dn = NT
    else:
        (K, M), (_, N) = a.shape, b.shape
        a_spec = pl.BlockSpec((tk, tm), lambda i, j, k: (k, i))
        b_spec = pl.BlockSpec((tk, tn), lambda i, j, k: (k, j))
        dn = TN
    assert M % tm == 0 and N % tn == 0 and K % tk == 0, (name, M, N, K)
    nk = K // tk
    fused = res is not None
    o_spec = pl.BlockSpec((tm, tn), lambda i, j, k: (i, j))

    def body(a_ref, b_ref, *rest):
        acc_ref = rest[-1] if nk > 1 else None
        if after is not None:
            rest = rest[1:]
        if fused:
            res_ref, g_ref, o_ref, x_ref = rest[:4]
        elif relu2_of is not None:
            u_ref, o_ref = rest[:2]
        elif relu2:
            o_ref, act_ref = rest[:2]
        else:
            o_ref = rest[0]

        def finish(acc):
            if relu2_of is not None:
                acc = acc * (2.0 * jnp.maximum(u_ref[...], 0.0))
            o_ref[...] = acc.astype(o_ref.dtype)
            if fused:
                x_ref[...] = res_ref[...] + g_ref[...] * acc
            if relu2:
                r = jnp.maximum(acc, 0.0)
                act_ref[...] = (r * r).astype(BF16)

        p = _dot(a_ref[...], b_ref[...], dn)
        if nk == 1:
            finish(p)
        else:
            k = pl.program_id(2)

            @pl.when(k == 0)
            def _():
                acc_ref[...] = p

            @pl.when(k > 0)
            def _():
                acc_ref[...] += p

            @pl.when(k == nk - 1)
            def _():
                finish(acc_ref[...])

    in_specs = [a_spec, b_spec]
    args = [a, b]
    if after is not None:
        in_specs.append(pl.BlockSpec(memory_space=pl.ANY))
        args.append(after)
    out_shape = jax.ShapeDtypeStruct((M, N), out_dtype)
    out_specs = o_spec
    if fused:
        in_specs += [pl.BlockSpec((tm, tn), lambda i, j, k: (i, j)), pl.BlockSpec((1, tn), lambda i, j, k: (0, j))]
        args += [res, gvec]
        out_shape = (out_shape, jax.ShapeDtypeStruct((M, N), F32))
        out_specs = (o_spec, pl.BlockSpec((tm, tn), lambda i, j, k: (i, j)))
    elif relu2_of is not None:
        in_specs.append(pl.BlockSpec((tm, tn), lambda i, j, k: (i, j)))
        args.append(relu2_of)
    elif relu2:
        out_shape = (out_shape, jax.ShapeDtypeStruct((M, N), BF16))
        out_specs = (o_spec, pl.BlockSpec((tm, tn), lambda i, j, k: (i, j)))
    kw = dict(name=name, grid=(M // tm, N // tn, nk), in_specs=in_specs, out_specs=out_specs,
              out_shape=out_shape, scratch_shapes=[pltpu.VMEM((tm, tn), F32)] if nk > 1 else [])
    if comm is not None:
        return _carry(body, comm, **kw)(*args)
    return _pcall(body, compiler_params=_params(("parallel", "parallel", "arbitrary")), **kw)(*args)


PROJ_TN = 512
ATT_T0, ATT_T1 = 6144 // PROJ_TN, 10752 // PROJ_TN
N_SLABS = (ATT_T1 - ATT_T0) * 4
MAIN_COLS = IN_COLS - (ATT_T1 - ATT_T0) * PROJ_TN


PROJ_TILES = IN_COLS // PROJ_TN
SHARD_ROWS = IN_COLS // N_DEV
W_CHUNKS = 4
N_OWN, N_NEAR = 5, 18


def _proj_order():
    out = np.zeros((4, 3, PROJ_TILES), np.int32)
    for q in range(4):
        def hops(t):
            owners = {col // (2 * SHARD_ROWS) for col in (t * PROJ_TN, (t + 1) * PROJ_TN - 1)}
            return max(bin(q ^ p).count("1") for p in owners)
        order = sorted(range(PROJ_TILES), key=lambda t: (hops(t), t))
        assert all(hops(t) == 0 for t in order[:N_OWN]) and all(hops(t) < 2 for t in order[:N_NEAR])
        is_att = [ATT_T0 <= t < ATT_T1 for t in order]
        for row, kind, index in ((1, False, lambda t: t if t < ATT_T0 else t - (ATT_T1 - ATT_T0)),
                                 (2, True, lambda t: t - ATT_T0)):
            own = [index(t) if a == kind else None for t, a in zip(order, is_att)]
            first = next(v for v in own if v is not None)
            last = first
            for j, v in enumerate(own):
                last = last if v is None else v
                out[q, row, j] = last
        out[q, 0] = order
    return out


def _gather_proj(x, g, sh, sc, shard, order):
    rows = SHARD_ROWS // W_CHUNKS

    def body(ord_ref, x_ref, g_ref, shift_ref, scale_ref, sh_ref, main_ref, slab_ref, full_ref, a_ref, wbuf, xbuf,
             fetch_sems, send_sems, recv_sems, local_sems, x_sem):
        j = pl.program_id(0)
        x, y, c = _mesh_pos()
        me, sibling = (x, y, c), (x, y, 1 - c)
        chips = [(1 - x, y), (x, 1 - y), (1 - x, 1 - y)]

        def block(p, owner):
            return full_ref.at[pl.ds(pl.multiple_of(_slot(owner) * SHARD_ROWS + p * rows, 16), rows)]

        def copy(p, k, owner, to, from_input=False):
            dst = block(p, owner)
            return pltpu.make_async_remote_copy(
                src_ref=sh_ref.at[pl.ds(p * rows, rows)] if from_input else dst, dst_ref=dst,
                send_sem=send_sems.at[7 * p + k], recv_sem=recv_sems.at[7 * p + k], device_id=to, device_id_type=MESH)

        pieces = range(W_CHUNKS)
        mine = [pltpu.make_async_copy(sh_ref.at[pl.ds(p * rows, rows)], block(p, me), local_sems.at[p]) for p in pieces]
        first = [copy(p, 0, me, sibling, from_input=True) for p in pieces]
        first += [copy(p, 1 + n, me, (*chips[n], c), from_input=True) for p in pieces for n in range(2)]
        near_pass = [copy(p, 4 + n, (*chips[n], c), sibling) for p in pieces for n in range(2)]
        relay = [copy(p, 3, ((x + 1 - c) % 2, (y + c) % 2, c), ((x + c) % 2, (y + 1 - c) % 2, c)) for p in pieces]
        far_pass = [copy(p, 6, (*chips[2], c), sibling) for p in pieces]

        def fetch(pos):
            slot = lax.rem(pos, 2)
            start = pl.multiple_of(ord_ref[0, pos] * PROJ_TN, PROJ_TN)
            return pltpu.make_async_copy(full_ref.at[pl.ds(start, PROJ_TN)], wbuf.at[slot], fetch_sems.at[slot])

        @pl.when(j == 0)
        def _():
            x_copy = pltpu.make_async_copy(x_ref, xbuf, x_sem.at[0])
            x_copy.start()
            for cp in mine + first:
                cp.start()
            x_copy.wait()
            for r in range(S // TR):
                rws = pl.ds(r * TR, TR)
                xv = xbuf[rws, :]
                rstd = lax.rsqrt(jnp.mean(xv * xv, axis=-1, keepdims=True) + RMS_EPS)
                n = xv * rstd * g_ref[...]
                a_ref[rws, :] = (n * (1.0 + scale_ref[...]) + shift_ref[...]).astype(BF16)
            for cp in mine:
                cp.wait()
            for p in pieces:
                copy(p, 0, sibling, me).wait_recv()
            fetch(j).start()

        @pl.when(j == N_OWN - 1)
        def _():
            for p in pieces:
                for n in range(2):
                    copy(p, 1 + n, (*chips[n], c), me).wait_recv()
                    near_pass[2 * p + n].start()
                relay[p].start()
            for p in pieces:
                for n in range(2):
                    copy(p, 4 + n, (*chips[n], 1 - c), me).wait_recv()

        @pl.when(j == N_NEAR - 1)
        def _():
            for p in pieces:
                copy(p, 3, (*chips[2], c), me).wait_recv()
                far_pass[p].start()
            for p in pieces:
                copy(p, 6, (*chips[2], 1 - c), me).wait_recv()

        @pl.when(j + 1 < PROJ_TILES)
        def _():
            fetch(j + 1).start()

        fetch(j).wait()
        w_ref = wbuf.at[lax.rem(j, 2)]
        tile = ord_ref[0, j]
        is_att = (tile >= ATT_T0) & (tile < ATT_T1)
        chunks = [pl.ds(r * 512, 512) for r in range(S // 512)]

        @pl.when(jnp.logical_not(is_att))
        def _():
            for rws in chunks:
                main_ref[rws, :] = _dot(a_ref[rws, :], w_ref[...], NT)

        @pl.when(is_att)
        def _():
            for rws in chunks:
                p = _dot(a_ref[rws, :], w_ref[...], NT)
                for h in range(4):
                    slab_ref[h, rws, :] = p[:, h * 128:(h + 1) * 128]

        @pl.when(j == PROJ_TILES - 1)
        def _():
            for cp in first + near_pass + relay + far_pass:
                cp.wait_send()

    vec = pl.BlockSpec((1, D), lambda j, o: (0, 0))
    gs = pltpu.PrefetchScalarGridSpec(
        num_scalar_prefetch=1, grid=(PROJ_TILES,),
        in_specs=[HBM_SPEC, vec, vec, vec, HBM_SPEC],
        out_specs=(pl.BlockSpec((S, PROJ_TN), lambda j, o: (0, o[1, j])),
                   pl.BlockSpec((4, S, 128), lambda j, o: (o[2, j], 0, 0)), HBM_SPEC,
                   pl.BlockSpec((S, D), lambda j, o: (0, 0))),
        scratch_shapes=[pltpu.VMEM((2, PROJ_TN, D), BF16), pltpu.VMEM((S, D), F32), pltpu.SemaphoreType.DMA((2,)),
                        pltpu.SemaphoreType.DMA((7 * W_CHUNKS,)), pltpu.SemaphoreType.DMA((7 * W_CHUNKS,)),
                        pltpu.SemaphoreType.DMA((W_CHUNKS,)), pltpu.SemaphoreType.DMA((1,))])
    return _pcall(body, name="gather_proj", grid_spec=gs,
                  out_shape=(jax.ShapeDtypeStruct((S, MAIN_COLS), F32), jax.ShapeDtypeStruct((N_SLABS, S, 128), F32),
                             jax.ShapeDtypeStruct((IN_COLS, D), BF16), jax.ShapeDtypeStruct((S, D), BF16)),
                  compiler_params=_params(("arbitrary",)))(order, x, g, sh, sc, shard)


TR = 256


def _row_spec(w=D):
    return pl.BlockSpec((TR, w), lambda i: (i, 0))


def _vec_spec(w=D):
    return pl.BlockSpec((1, w), lambda i: (0, 0))


def _norm_mod_bwd(x, g, sc, dh, dres, name, gate=None, after=None):
    gated = gate is not None

    def body(x_ref, g_ref, sc_ref, dh_ref, dres_ref, *rest):
        if after is not None:
            rest = rest[1:]
        if gated:
            f_ref, gv_ref, dx_ref, dsc_ref, dsh_ref, dg_ref, dz_ref, dgv_ref = rest
        else:
            dx_ref, dsc_ref, dsh_ref, dg_ref = rest
        i = pl.program_id(0)
        xv = x_ref[...]
        dh = dh_ref[...]
        rstd = lax.rsqrt(jnp.mean(xv * xv, axis=-1, keepdims=True) + RMS_EPS)
        xhat = xv * rstd
        gv = g_ref[...]
        dn = dh * (1.0 + sc_ref[...])
        dxhat = dn * gv
        dx = dres_ref[...] + rstd * (dxhat - xhat * jnp.mean(dxhat * xhat, axis=-1, keepdims=True))
        dx_ref[...] = dx
        sums = [(dsc_ref, jnp.sum(dh * (xhat * gv), axis=0, keepdims=True)),
                (dsh_ref, jnp.sum(dh, axis=0, keepdims=True)),
                (dg_ref, jnp.sum(dn * xhat, axis=0, keepdims=True))]
        if gated:
            dz_ref[...] = (dx * gv_ref[...]).astype(BF16)
            sums.append((dgv_ref, jnp.sum(dx * f_ref[...], axis=0, keepdims=True)))

        @pl.when(i == 0)
        def _():
            for ref, p in sums:
                ref[...] = p

        @pl.when(i > 0)
        def _():
            for ref, p in sums:
                ref[...] += p

    vec = jax.ShapeDtypeStruct((1, D), F32)
    in_specs = [_row_spec(), _vec_spec(), _vec_spec(), _row_spec(), _row_spec()]
    out_specs = [_row_spec(), _vec_spec(), _vec_spec(), _vec_spec()]
    out_shape = [jax.ShapeDtypeStruct((S, D), F32), vec, vec, vec]
    args = [x, g, sc, dh, dres]
    if after is not None:
        in_specs.append(HBM_SPEC)
        args.append(after)
    if gated:
        in_specs += [_row_spec(), _vec_spec()]
        out_specs += [_row_spec(), _vec_spec()]
        out_shape += [jax.ShapeDtypeStruct((S, D), BF16), vec]
        args += list(gate)
    return _pcall(body, name=name, grid=(S // TR,), in_specs=in_specs, out_specs=tuple(out_specs),
                  out_shape=tuple(out_shape), compiler_params=_params(("arbitrary",)))(*args)


def _w_o_norm2(merged, w_o, x, g1, g, sh, sc):
    def body(a_ref, b_ref, x_ref, g1_ref, g_ref, sh_ref, sc_ref, o_ref, x1_ref, h_ref):
        acc = _dot(a_ref[...], b_ref[...], NN)
        o_ref[...] = acc
        xv = x_ref[...] + g1_ref[...] * acc
        x1_ref[...] = xv
        rstd = lax.rsqrt(jnp.mean(xv * xv, axis=-1, keepdims=True) + RMS_EPS)
        h_ref[...] = (xv * rstd * g_ref[...] * (1.0 + sc_ref[...]) + sh_ref[...]).astype(BF16)

    rows = pl.BlockSpec((FF2_TM, D), lambda i: (i, 0))
    f32 = jax.ShapeDtypeStruct((S, D), F32)
    return _pcall(body, name="w_o_norm2", grid=(S // FF2_TM,),
                  in_specs=[rows, pl.BlockSpec((D, D), lambda i: (0, 0)), rows] + [_vec_spec()] * 4,
                  out_specs=(rows, rows, rows), out_shape=(f32, f32, jax.ShapeDtypeStruct((S, D), BF16)),
                  compiler_params=_params(("parallel",)))(merged, w_o, x, g1, g, sh, sc)


FF2_TM = 512


def _ff2_final(act, w_ff2, x1, g2, tgt, g):
    def body(a_ref, b_ref, x1_ref, g2_ref, t_ref, g_ref, loss_ref, dx_ref, dg_ref, df_ref, dg2_ref):
        i = pl.program_id(0)
        f = _dot(a_ref[...], b_ref[...], NN)
        g2v = g2_ref[...]
        xv = x1_ref[...] + g2v * f
        gv = g_ref[...]
        rstd = lax.rsqrt(jnp.mean(xv * xv, axis=-1, keepdims=True) + RMS_EPS)
        xhat = xv * rstd
        err = xhat * gv - t_ref[...]
        dy = err * (1.0 / D)
        dxhat = dy * gv
        dx = rstd * (dxhat - xhat * jnp.mean(dxhat * xhat, axis=-1, keepdims=True))
        dx_ref[...] = dx
        df_ref[...] = (dx * g2v).astype(BF16)
        p_g = jnp.sum(dy * xhat, axis=0, keepdims=True)
        p_g2 = jnp.sum(dx * f, axis=0, keepdims=True)
        p_l = jnp.zeros((1, 128), F32) + 0.5 * jnp.sum(jnp.mean(err * err, axis=-1, keepdims=True))

        @pl.when(i == 0)
        def _():
            dg_ref[...] = p_g
            dg2_ref[...] = p_g2
            loss_ref[...] = p_l

        @pl.when(i > 0)
        def _():
            dg_ref[...] += p_g
            dg2_ref[...] += p_g2
            loss_ref[...] += p_l

    vec = jax.ShapeDtypeStruct((1, D), F32)
    rows = lambda w: pl.BlockSpec((FF2_TM, w), lambda i: (i, 0))
    return _pcall(body, name="ff2_final", grid=(S // FF2_TM,),
                  in_specs=[rows(D_FF), pl.BlockSpec((D_FF, D), lambda i: (0, 0)), rows(D), _vec_spec(), rows(D),
                            _vec_spec()],
                  out_specs=(_vec_spec(128), rows(D), _vec_spec(), rows(D), _vec_spec()),
                  out_shape=(jax.ShapeDtypeStruct((1, 128), F32), jax.ShapeDtypeStruct((S, D), F32), vec,
                             jax.ShapeDtypeStruct((S, D), BF16), vec),
                  compiler_params=_params(("arbitrary",)))(act, w_ff2, x1, g2, tgt, g)


HALF = 512


MERGE_TM = 1024


def _merge_specs():
    blk = lambda off: pl.BlockSpec((MERGE_TM, HALF), lambda i, j: (i, off // HALF + j))
    return blk(OFF_GA), blk(OFF_GB), blk(0)


def _att_out_merge(att, w_att_out, proj, ret_out):
    def body(a_ref, b_ref, ga_ref, gb_ref, r_ref, o_ref, m_ref):
        acc = _dot(a_ref[...], b_ref[...], NN)
        o_ref[...] = acc
        m_ref[...] = (jax.nn.sigmoid(ga_ref[...]) * r_ref[...] + jax.nn.sigmoid(gb_ref[...]) * acc).astype(BF16)

    ga, gb, tile = _merge_specs()
    return _pcall(body, name="att_out", grid=(S // MERGE_TM, D // HALF),
                  in_specs=[pl.BlockSpec((MERGE_TM, AW), lambda i, j: (i, 0)), pl.BlockSpec((AW, HALF), lambda i, j: (0, j)),
                            ga, gb, tile],
                  out_specs=(tile, tile),
                  out_shape=(jax.ShapeDtypeStruct((S, D), F32), jax.ShapeDtypeStruct((S, D), BF16)),
                  compiler_params=_params(("parallel", "parallel")))(att, w_att_out, proj, proj, ret_out)


def _dmerged_split(dmixo, w_o, proj, ret_out, att_out):
    def body(a_ref, b_ref, ga_ref, gb_ref, r_ref, at_ref, dr_ref, da_ref, dga_ref, dgb_ref):
        dm = _dot(a_ref[...], b_ref[...], NT)
        sa = jax.nn.sigmoid(ga_ref[...])
        sb = jax.nn.sigmoid(gb_ref[...])
        dr_ref[...] = (dm * sa).astype(BF16)
        da_ref[...] = (dm * sb).astype(BF16)
        dga_ref[...] = (dm * r_ref[...] * (sa * (1.0 - sa))).astype(BF16)
        dgb_ref[...] = (dm * at_ref[...] * (sb * (1.0 - sb))).astype(BF16)

    ga, gb, tile = _merge_specs()
    o = jax.ShapeDtypeStruct((S, D), BF16)
    return _pcall(body, name="dmerged", grid=(S // MERGE_TM, D // HALF),
                  in_specs=[pl.BlockSpec((MERGE_TM, D), lambda i, j: (i, 0)), pl.BlockSpec((HALF, D), lambda i, j: (j, 0)),
                            ga, gb, tile, tile],
                  out_specs=(tile,) * 4, out_shape=(o, o, o, o),
                  compiler_params=_params(("parallel", "parallel")))(dmixo, w_o, proj, proj, ret_out, att_out)


def _ret_tables():
    H, C = RET_HEADS, CHUNK
    log_g = jnp.log1p(-(2.0 ** (-5.0 - jnp.arange(H, dtype=F32))))
    idx = jnp.arange(C, dtype=F32)
    rel = idx[:, None] - idx[None, :]
    inner = jnp.where(rel >= 0, jnp.exp(log_g[:, None, None] * jnp.maximum(rel, 0.0)), 0.0)
    qd = jnp.exp(log_g[:, None] * (idx + 1.0))[:, :, None]
    kd = jnp.exp(log_g[:, None] * (C - 1.0 - idx))[:, :, None]
    cd = jnp.broadcast_to(jnp.exp(log_g * C)[:, None, None], (H, 1, 128))
    half = RET_DK // 2
    inv = 10000.0 ** (-jnp.arange(half, dtype=F32) / half)
    ang = jnp.arange(S, dtype=F32)[:, None] * inv[None, :]
    return inner, qd, kd, cd, jnp.cos(ang), jnp.sin(ang)


def _rot(x, cos, sin):
    x1, x2 = x[:, :128], x[:, 128:]
    return jnp.concatenate([x1 * cos - x2 * sin, x1 * sin + x2 * cos], axis=1)


def _rot_t(d, cos, sin):
    d1, d2 = d[:, :128], d[:, 128:]
    return jnp.concatenate([d1 * cos + d2 * sin, d2 * cos - d1 * sin], axis=1)


RET_COLS = OFF_ATT
RET_VW = RET_HEADS * RET_DV


def _ret_specs(chunk_of):
    ci = chunk_of
    whole = lambda shape: pl.BlockSpec(shape, lambda t: (0,) * len(shape))
    return [
        pl.BlockSpec((CHUNK, RET_COLS), lambda t: (ci(t), 0)),
        pl.BlockSpec((CHUNK, 128), lambda t: (ci(t), 0)),
        pl.BlockSpec((CHUNK, 128), lambda t: (ci(t), 0)),
        whole((RET_HEADS, CHUNK, CHUNK)), whole((RET_HEADS, CHUNK, 1)), whole((RET_HEADS, CHUNK, 1)),
        whole((RET_HEADS, 1, 128)), whole((1, RET_VW)), whole((1, RET_VW)),
    ]


def _ret_cols(h):
    q = slice(OFF_RQ + h * RET_DK, OFF_RQ + (h + 1) * RET_DK)
    k = slice(OFF_RK + h * RET_DK, OFF_RK + (h + 1) * RET_DK)
    v = slice(OFF_RV + h * RET_DV, OFF_RV + (h + 1) * RET_DV)
    g = slice(OFF_RG + h * RET_DV, OFF_RG + (h + 1) * RET_DV)
    return q, k, v, g, slice(h * RET_DV, (h + 1) * RET_DV)


def _ret_fwd(proj, tables, gn_g, gn_b, after):
    inner, qd, kd, cd, cos, sin = tables

    def body(x_ref, cos_ref, sin_ref, in_ref, qd_ref, kd_ref, cd_ref, g_ref, b_ref, after_ref,
             gated_ref, ro_ref, st_ref, s_scr):
        i = pl.program_id(0)

        @pl.when(i == 0)
        def _():
            s_scr[...] = jnp.zeros_like(s_scr)

        cosv, sinv = cos_ref[...], sin_ref[...]
        for h in range(RET_HEADS):
            cq, ck, cv, cg, co = _ret_cols(h)
            q = _rot(x_ref[:, cq], cosv, sinv)
            k = _rot(x_ref[:, ck], cosv, sinv) * (RET_DK ** -0.5)
            v = x_ref[:, cv]
            st = s_scr[h]
            st_ref[h] = st.astype(BF16)
            s = _dot(q, k, NT) * in_ref[h]
            o = _dot(s, v, NN) + _dot(q, st, NN) * qd_ref[h]
            s_scr[h] = st * cd_ref[h, :, :1] + _dot(k * kd_ref[h], v, TN)
            ro_ref[:, co] = o
            mu = jnp.mean(o, axis=-1, keepdims=True)
            oc = o - mu
            var = jnp.mean(oc * oc, axis=-1, keepdims=True)
            rn = oc * lax.rsqrt(var + GN_EPS) * g_ref[:, co] + b_ref[:, co]
            rg = x_ref[:, cg]
            gated_ref[:, co] = (rg * jax.nn.sigmoid(rg) * rn).astype(BF16)

    ospec = pl.BlockSpec((CHUNK, RET_VW), lambda t: (t, 0))
    return _pcall(
        body, name="ret_fwd", grid=(N_CHUNK,), in_specs=_ret_specs(lambda t: t) + [HBM_SPEC],
        out_specs=(ospec, ospec, pl.BlockSpec((RET_HEADS, None, RET_DK, RET_DV), lambda t: (0, t, 0, 0))),
        out_shape=(jax.ShapeDtypeStruct((S, RET_VW), BF16), jax.ShapeDtypeStruct((S, RET_VW), F32),
                   jax.ShapeDtypeStruct((RET_HEADS, N_CHUNK, RET_DK, RET_DV), BF16)),
        scratch_shapes=[pltpu.VMEM((RET_HEADS, RET_DK, RET_DV), F32)],
        compiler_params=_params(("arbitrary",)))(proj, cos, sin, inner, qd, kd, cd, gn_g, gn_b, after)


def _ret_bwd(proj, tables, gn_g, gn_b, ro, states, dgated, others):
    inner, qd, kd, cd, cos, sin = tables
    last = N_CHUNK - 1
    pieces = lambda o: [(h, o.shape[2]) for h in range(o.shape[0])] if len(o.shape) == 3 else [(None, o.shape[1])]
    assert RET_COLS + sum(w for o in others for _, w in pieces(o)) == IN_COLS

    def body(x_ref, cos_ref, sin_ref, in_ref, qd_ref, kd_ref, cd_ref, g_ref, b_ref, ro_ref, st_ref, dg_ref, *rest):
        other_refs, (dx_ref, gg_ref, gb_ref, gs_scr) = rest[:len(others)], rest[len(others):]
        t = pl.program_id(0)
        col = RET_COLS
        for o_ref in other_refs:
            for h, w in pieces(o_ref):
                dx_ref[:, col:col + w] = (o_ref[...] if h is None else o_ref[h]).astype(BF16)
                col += w

        @pl.when(t == 0)
        def _():
            gs_scr[...] = jnp.zeros_like(gs_scr)
            gg_ref[...] = jnp.zeros_like(gg_ref)
            gb_ref[...] = jnp.zeros_like(gb_ref)

        cosv, sinv = cos_ref[...], sin_ref[...]
        for h in range(RET_HEADS):
            cq, ck, cv, cg, co = _ret_cols(h)
            q = _rot(x_ref[:, cq], cosv, sinv)
            k = _rot(x_ref[:, ck], cosv, sinv) * (RET_DK ** -0.5)
            v = x_ref[:, cv]
            qdv, kdv, dm = qd_ref[h], kd_ref[h], in_ref[h]
            st = st_ref[h]
            o = ro_ref[:, co]
            gv = g_ref[:, co]
            mu = jnp.mean(o, axis=-1, keepdims=True)
            oc = o - mu
            rstd = lax.rsqrt(jnp.mean(oc * oc, axis=-1, keepdims=True) + GN_EPS)
            ohat = oc * rstd
            rn = ohat * gv + b_ref[:, co]
            rg = x_ref[:, cg]
            sg = jax.nn.sigmoid(rg)
            dgt = dg_ref[:, co]
            drn = dgt * (rg * sg)
            dx_ref[:, cg] = (dgt * rn * (sg * (1.0 + rg * (1.0 - sg)))).astype(BF16)
            gg_ref[:, co] += jnp.sum(drn * ohat, axis=0, keepdims=True)
            gb_ref[:, co] += jnp.sum(drn, axis=0, keepdims=True)
            dohat = drn * gv
            do = rstd * (dohat - jnp.mean(dohat, axis=-1, keepdims=True)
                         - ohat * jnp.mean(dohat * ohat, axis=-1, keepdims=True))
            gs = gs_scr[h]
            s = _dot(q, k, NT) * dm
            dsr = _dot(do, v, NT) * dm
            dq = _dot(dsr, k, NN) + _dot(do, st, NT) * qdv
            dk = _dot(dsr, q, TN) + _dot(v, gs, NT) * kdv
            dv = _dot(s, do, TN) + _dot(k * kdv, gs, NN)
            gs_scr[h] = gs * cd_ref[h, :, :1] + _dot(q * qdv, do, TN)
            dx_ref[:, cq] = _rot_t(dq, cosv, sinv).astype(BF16)
            dx_ref[:, ck] = (_rot_t(dk, cosv, sinv) * (RET_DK ** -0.5)).astype(BF16)
            dx_ref[:, cv] = dv.astype(BF16)

    rev = lambda t: last - t
    vblk = pl.BlockSpec((CHUNK, RET_VW), lambda t: (rev(t), 0))
    vspec = pl.BlockSpec((1, RET_VW), lambda t: (0, 0))
    rows = lambda w: pl.BlockSpec((CHUNK, w), lambda t: (rev(t), 0))
    return _pcall(
        body, name="ret_bwd", grid=(N_CHUNK,),
        in_specs=_ret_specs(rev) + [vblk, pl.BlockSpec((RET_HEADS, None, RET_DK, RET_DV), lambda t: (0, rev(t), 0, 0)),
                                    vblk] + [rows(o.shape[1]) if o.ndim == 2 else
                                             pl.BlockSpec((o.shape[0], CHUNK, o.shape[2]), lambda t: (0, rev(t), 0))
                                             for o in others],
        out_specs=(rows(IN_COLS), vspec, vspec),
        out_shape=(jax.ShapeDtypeStruct((S, IN_COLS), BF16), jax.ShapeDtypeStruct((1, RET_VW), F32),
                   jax.ShapeDtypeStruct((1, RET_VW), F32)),
        scratch_shapes=[pltpu.VMEM((RET_HEADS, RET_DK, RET_DV), F32)],
        compiler_params=_params(("arbitrary",)))(proj, cos, sin, inner, qd, kd, cd, gn_g, gn_b, ro, states, dgated,
                                                 *others)


def _bucket_tables():
    qi = np.arange(ATT_BLK)[:, None]
    kj = np.arange(2 * ATT_BLK)[None, :]
    m = ATT_BLK + qi - kj
    out = []
    for win, dil in ATT_GROUPS:
        w = win // dil
        dist = (np.clip(m, 0, w) * dil).astype(np.int32)
        max_exact = N_BUCKETS // 2
        d_f = np.maximum(dist, 1).astype(np.float32)
        large = max_exact + (np.log(d_f / np.float32(max_exact)) / np.float32(math.log(MAX_DIST / max_exact))
                             * np.float32(N_BUCKETS - max_exact)).astype(np.int32)
        large = np.minimum(large, N_BUCKETS - 1)
        out.append(np.where(dist < max_exact, dist, large).astype(np.int32))
    return np.stack(out)


def _bias_build(rel_bias, buckets, after):
    def body(tab_ref, bk_ref, after_ref, o_ref):
        hh = pl.program_id(0)
        bk = bk_ref[...]
        acc = jnp.zeros((ATT_BLK, 2 * ATT_BLK), F32)
        for b in range(N_BUCKETS):
            acc = jnp.where(bk == b, tab_ref[b, hh], acc)
        o_ref[...] = acc

    nh = len(ATT_GROUPS) * ATT_HG
    return _pcall(body, name="bias_build", grid=(nh,),
                  in_specs=[pl.BlockSpec(memory_space=pltpu.SMEM),
                            pl.BlockSpec((None, ATT_BLK, 2 * ATT_BLK), lambda hh: (hh // ATT_HG, 0, 0)), HBM_SPEC],
                  out_specs=pl.BlockSpec((None, ATT_BLK, 2 * ATT_BLK), lambda hh: (hh, 0, 0)),
                  out_shape=jax.ShapeDtypeStruct((nh, ATT_BLK, 2 * ATT_BLK), F32),
                  compiler_params=_params(("parallel",)))(rel_bias, buckets, after)


def _bias_grad(ds_sum, buckets):
    def body(ds_ref, bk_ref, o_ref):
        bk = bk_ref[...]
        ds = ds_ref[...]
        rows = lax.broadcasted_iota(jnp.int32, (N_BUCKETS, 128), 0)
        acc = jnp.zeros((N_BUCKETS, 128), F32)
        for b in range(N_BUCKETS):
            acc = jnp.where(rows == b, jnp.sum(jnp.where(bk == b, ds, 0.0)), acc)
        o_ref[...] = acc

    nh = len(ATT_GROUPS) * ATT_HG
    return _pcall(body, name="bias_grad", grid=(nh,),
                  in_specs=[pl.BlockSpec((None, ATT_BLK, 2 * ATT_BLK), lambda hh: (hh, 0, 0)),
                            pl.BlockSpec((None, ATT_BLK, 2 * ATT_BLK), lambda hh: (hh // ATT_HG, 0, 0))],
                  out_specs=pl.BlockSpec((None, N_BUCKETS, 128), lambda hh: (hh, 0, 0)),
                  out_shape=jax.ShapeDtypeStruct((nh, N_BUCKETS, 128), F32),
                  compiler_params=_params(("parallel",)))(ds_sum, buckets)


def _att_valid(n):
    qi = lax.broadcasted_iota(jnp.int32, (ATT_BLK, 2 * ATT_BLK), 0)
    kj = lax.broadcasted_iota(jnp.int32, (ATT_BLK, 2 * ATT_BLK), 1)
    m = ATT_BLK + qi - kj
    first_key = jnp.where(n > 0, 0, ATT_BLK)
    return (m >= 0) & (m <= ATT_BLK) & (kj >= first_key)


ATT_HP = (1, 2, 2)


def _att_geometry(gi):
    _, dil = ATT_GROUPS[gi]
    return dil, S // dil // ATT_BLK, ATT_HP[gi]


def _blk(dil, r, n):
    if dil == 1:
        return pl.ds(n * ATT_BLK, ATT_BLK)
    return pl.ds(r + n * ATT_BLK * dil, ATT_BLK, stride=dil)


def _slab_specs(gi):
    _, _, hp = _att_geometry(gi)
    per = ATT_HG // hp
    return [pl.BlockSpec((hp, S, ATT_DH), lambda g, r, part=part: ((3 * gi + part) * per + g, 0, 0))
            for part in range(3)]


def _head_specs(gi, count):
    _, _, hp = _att_geometry(gi)
    return [pl.BlockSpec((hp, S, ATT_DH), lambda g, r: (g, 0, 0))] * count


def _bias_spec(gi):
    _, _, hp = _att_geometry(gi)
    return pl.BlockSpec((hp, ATT_BLK, 2 * ATT_BLK), lambda g, r: (gi * (ATT_HG // hp) + g, 0, 0))


def _att_valid_first():
    qi = lax.broadcasted_iota(jnp.int32, (ATT_BLK, ATT_BLK), 0)
    kj = lax.broadcasted_iota(jnp.int32, (ATT_BLK, ATT_BLK), 1)
    return kj <= qi


def _att_fwd(slabs, bias, gi, comm=None):
    dil, nb, hp = _att_geometry(gi)
    scale = ATT_DH ** -0.5

    def body(q_ref, k_ref, v_ref, bias_ref, o_ref, l_ref):
        r = pl.program_id(1)
        for n in range(nb):
            cur = _blk(dil, r, n)
            valid = _att_valid(n) if n > 0 else _att_valid_first()
            for h in range(hp):
                if n > 0:
                    prev = _blk(dil, r, n - 1)
                    kk = jnp.concatenate([k_ref[h, prev, :], k_ref[h, cur, :]], axis=0)
                    vv = jnp.concatenate([v_ref[h, prev, :], v_ref[h, cur, :]], axis=0)
                    bias = bias_ref[h]
                else:
                    kk, vv, bias = k_ref[h, cur, :], v_ref[h, cur, :], bias_ref[h, :, pl.ds(ATT_BLK, ATT_BLK)]
                s = _dot(q_ref[h, cur, :], kk, NT) * scale + bias
                s = jnp.where(valid, s, -1e30)
                mx = jnp.max(s, axis=-1, keepdims=True)
                e = jnp.exp(s - mx)
                den = jnp.sum(e, axis=-1, keepdims=True)
                o_ref[h, cur, :] = _dot(e / den, vv, NN)
                l_ref[h, cur, :] = jnp.broadcast_to(mx + jnp.log(den), (ATT_BLK, ATT_DH))

    osh = jax.ShapeDtypeStruct((ATT_HG, S, ATT_DH), F32)
    kw = dict(name=f"att_fwd{gi}", grid=(ATT_HG // hp, dil), in_specs=_slab_specs(gi) + [_bias_spec(gi)],
              out_specs=tuple(_head_specs(gi, 2)), out_shape=(osh, osh))
    if comm is not None:
        return _carry(body, comm, **kw)(slabs, slabs, slabs, bias)
    return _pcall(body, compiler_params=_params(("parallel", "arbitrary")), **kw)(slabs, slabs, slabs, bias)


def _att_bwd(slabs, bias, o, lse, do, dlse, gi, comm=None):
    dil, nb, hp = _att_geometry(gi)
    per = ATT_HG // hp
    scale = ATT_DH ** -0.5
    wide = lambda t: jnp.concatenate([t, t], axis=1)

    def body(q_ref, k_ref, v_ref, bias_ref, o_ref, l_ref, do_ref, dl_ref, dq_ref, dk_ref, dv_ref, ds_ref):
        r = pl.program_id(1)

        @pl.when(r == 0)
        def _():
            ds_ref[...] = jnp.zeros_like(ds_ref)

        for h in range(hp):
            carry_k = carry_v = None
            for n in range(nb):
                cur = _blk(dil, r, n)
                q = q_ref[h, cur, :]
                dov = do_ref[h, cur, :]
                delta = jnp.sum(dov * o_ref[h, cur, :], axis=-1, keepdims=True)
                if n == 0:
                    own = pl.ds(ATT_BLK, ATT_BLK)
                    kk, vv = k_ref[h, cur, :], v_ref[h, cur, :]
                    s = _dot(q, kk, NT) * scale + bias_ref[h, :, own]
                    p = jnp.where(_att_valid_first(), jnp.exp(s - l_ref[h, cur, :]), 0.0)
                    ds = p * (_dot(dov, vv, NT) - delta + dl_ref[h, cur, :])
                    ds_ref[h, :, own] += ds
                    dq_ref[h, cur, :] = _dot(ds, kk, NN) * scale
                    carry_k, carry_v = _dot(ds, q, TN) * scale, _dot(p, dov, TN)
                    continue
                prev = _blk(dil, r, n - 1)
                kk = jnp.concatenate([k_ref[h, prev, :], k_ref[h, cur, :]], axis=0)
                vv = jnp.concatenate([v_ref[h, prev, :], v_ref[h, cur, :]], axis=0)
                s = _dot(q, kk, NT) * scale + bias_ref[h]
                p = jnp.where(_att_valid(n), jnp.exp(s - wide(l_ref[h, cur, :])), 0.0)
                dp = _dot(dov, vv, NT)
                ds = p * (dp - delta + wide(dl_ref[h, cur, :]))
                ds_ref[h] += ds
                dq_ref[h, cur, :] = _dot(ds, kk, NN) * scale
                dkk = _dot(ds, q, TN) * scale
                dvv = _dot(p, dov, TN)
                dk_ref[h, prev, :] = carry_k + dkk[:ATT_BLK]
                dv_ref[h, prev, :] = carry_v + dvv[:ATT_BLK]
                carry_k, carry_v = dkk[ATT_BLK:], dvv[ATT_BLK:]
            last = _blk(dil, r, nb - 1)
            dk_ref[h, last, :] = carry_k
            dv_ref[h, last, :] = carry_v

    osh = jax.ShapeDtypeStruct((ATT_HG, S, ATT_DH), F32)
    kw = dict(name=f"att_bwd{gi}", grid=(per, dil), in_specs=_slab_specs(gi) + [_bias_spec(gi)] + _head_specs(gi, 4),
              out_specs=(*_head_specs(gi, 3), pl.BlockSpec((hp, ATT_BLK, 2 * ATT_BLK), lambda g, r: (g, 0, 0))),
              out_shape=(osh, osh, osh, jax.ShapeDtypeStruct((ATT_HG, ATT_BLK, 2 * ATT_BLK), F32)))
    args = (slabs, slabs, slabs, bias, o, lse, do, dlse)
    if comm is not None:
        return _carry(body, comm, **kw)(*args)
    return _pcall(body, compiler_params=_params(("arbitrary", "arbitrary")), **kw)(*args)


AW = ATT_HG * ATT_DH


def _mix_weights(l0, l1, l2):
    mx = jnp.maximum(jnp.maximum(l0, l1), l2)
    e0, e1, e2 = jnp.exp(l0 - mx), jnp.exp(l1 - mx), jnp.exp(l2 - mx)
    den = e0 + e1 + e2
    return e0 / den, e1 / den, e2 / den


def _heads_spec():
    return pl.BlockSpec((ATT_HG, TR, ATT_DH), lambda i: (0, i, 0))


def _mix_fwd(os_, ls, comm=None):
    def body(o0, o1, o2, l0, l1, l2, att_ref):
        for h in range(ATT_HG):
            w0, w1, w2 = _mix_weights(l0[h], l1[h], l2[h])
            att_ref[:, h * ATT_DH:(h + 1) * ATT_DH] = (w0 * o0[h] + w1 * o1[h] + w2 * o2[h]).astype(BF16)

    kw = dict(name="mix_fwd", grid=(S // TR,), in_specs=[_heads_spec()] * 6, out_specs=_row_spec(AW),
              out_shape=jax.ShapeDtypeStruct((S, AW), BF16))
    if comm is not None:
        return _carry(body, comm, **kw)(*os_, *ls)
    return _pcall(body, compiler_params=_params(("parallel",)), **kw)(*os_, *ls)


def _mix_bwd(os_, ls, datt):
    def body(o0, o1, o2, l0, l1, l2, da_ref, d0, d1, d2, e0, e1, e2):
        for h in range(ATT_HG):
            ws = _mix_weights(l0[h], l1[h], l2[h])
            da = da_ref[:, h * ATT_DH:(h + 1) * ATT_DH]
            dws = []
            for o_ref, w, d_ref in zip((o0, o1, o2), ws, (d0, d1, d2)):
                d_ref[h] = w * da
                dws.append(jnp.broadcast_to(jnp.sum(da * o_ref[h], axis=-1, keepdims=True), (TR, ATT_DH)))
            tot = ws[0] * dws[0] + ws[1] * dws[1] + ws[2] * dws[2]
            for w, dw, e_ref in zip(ws, dws, (e0, e1, e2)):
                e_ref[h] = w * (dw - tot)

    o = jax.ShapeDtypeStruct((ATT_HG, S, ATT_DH), F32)
    return _pcall(body, name="mix_bwd", grid=(S // TR,), in_specs=[_heads_spec()] * 6 + [_row_spec(AW)],
                  out_specs=(_heads_spec(),) * 6, out_shape=(o,) * 6,
                  compiler_params=_params(("parallel",)))(*os_, *ls, datt)


def _ada_fwd(c_all, w_sh, b_sl):
    def body(c_ref, w_ref, b_ref, o_ref):
        cv = c_ref[...]
        o_ref[...] = _dot(cv * jax.nn.sigmoid(cv), w_ref[...], NN) + b_ref[...]

    return _pcall(body, name="ada_fwd", out_shape=jax.ShapeDtypeStruct((N_DEV, w_sh.shape[1]), F32),
                  compiler_params=_params())(c_all, w_sh, b_sl)


def _ada_bwd(c_all, dm_sl):
    def body(c_ref, d_ref, o_ref):
        cv = c_ref[...]
        o_ref[...] = _dot(cv * jax.nn.sigmoid(cv), d_ref[...], TN)

    return _pcall(body, name="ada_bwd", out_shape=jax.ShapeDtypeStruct((D, dm_sl.shape[1]), F32),
                  compiler_params=_params())(c_all, dm_sl)


N_MOD = 6


def _sum_small(gathered):
    n = len(gathered)

    def body(*refs):
        ins, (gb_ref, dm_ref), outs = refs[:n], refs[n:n + 2], refs[n + 2:]

        def total(r):
            acc = r[0]
            for e in range(1, N_DEV):
                acc = acc + r[e]
            return acc

        for i in range(N_MOD):
            cols = slice(i * D, (i + 1) * D)
            gb_ref[:, cols] = total(ins[i])
            for e in range(N_DEV):
                dm_ref[e:e + 1, cols] = ins[i][e]
        for r, o_ref in zip(ins[N_MOD:], outs):
            o_ref[...] = total(r)

    shapes = (jax.ShapeDtypeStruct((1, N_MOD * D), F32), jax.ShapeDtypeStruct((N_DEV, N_MOD * D), F32),
              *[jax.ShapeDtypeStruct(g.shape[1:], F32) for g in gathered[N_MOD:]])
    res = _pcall(body, name="sum_small", out_shape=shapes, compiler_params=_params())(*gathered)
    return res[0], res[1], res[2:]


def _row_tile(m, n):
    t = max(8, min(m, (1 << 19) // n // 8 * 8))
    while m % t:
        t -= 8
    return t


def _pair_sum(full, recv, sel, name, col_block=0):
    _, m, n = recv.shape
    t = _row_tile(m, n)

    def body(sel_ref, a_ref, b_ref, o_ref):
        o_ref[...] = (a_ref[...].astype(F32) + b_ref[...].astype(F32)).astype(o_ref.dtype)

    gs = pltpu.PrefetchScalarGridSpec(
        num_scalar_prefetch=1, grid=(4, m // t),
        in_specs=[pl.BlockSpec((None, None, t, n), lambda q, i, s: (q, s[0], i, col_block)),
                  pl.BlockSpec((None, t, n), lambda q, i, s: (q, i, 0))],
        out_specs=pl.BlockSpec((None, t, n), lambda q, i, s: (q, i, 0)))
    return _pcall(body, name=name, grid_spec=gs, out_shape=jax.ShapeDtypeStruct((4, m, n), full.dtype),
                  compiler_params=_params(("parallel", "parallel")))(sel, full, recv)


def _chip_sum(part, recv, sel, name):
    _, m, n = part.shape
    t = _row_tile(m, n)

    def body(sel_ref, a_ref, r_ref, o_ref):
        o_ref[...] = ((a_ref[...].astype(F32) + r_ref[0].astype(F32)) + r_ref[1].astype(F32)) + r_ref[2].astype(F32)

    gs = pltpu.PrefetchScalarGridSpec(
        num_scalar_prefetch=1, grid=(m // t,),
        in_specs=[pl.BlockSpec((None, t, n), lambda i, s: (s[0], i, 0)),
                  pl.BlockSpec((3, t, n), lambda i, s: (0, i, 0))],
        out_specs=pl.BlockSpec((t, n), lambda i, s: (i, 0)))
    return _pcall(body, name=name, grid_spec=gs, out_shape=jax.ShapeDtypeStruct((m, n), F32),
                  compiler_params=_params(("parallel",)))(sel, part, recv)


def _adamw_math(w, g, m, v):
    nm = ADAM_B1 * m + (1.0 - ADAM_B1) * g
    nv = ADAM_B2 * v + (1.0 - ADAM_B2) * (g * g)
    m_hat = nm / (1.0 - ADAM_B1 ** ADAM_STEP)
    v_hat = nv / (1.0 - ADAM_B2 ** ADAM_STEP)
    return -ADAM_LR * (m_hat / (jnp.sqrt(v_hat) + ADAM_EPS) + ADAM_WD * w), nm, nv


def _adamw(w, g, m, v, name):
    _, rows, cols = w.shape
    t = _row_tile(rows, cols)

    def body(w_ref, g_ref, m_ref, v_ref, d_ref, nm_ref, nv_ref):
        d_ref[...], nm_ref[...], nv_ref[...] = _adamw_math(w_ref[...], g_ref[...], m_ref[...], v_ref[...])

    spec3 = pl.BlockSpec((None, t, cols), lambda i: (0, i, 0))
    spec2 = pl.BlockSpec((t, cols), lambda i: (i, 0))
    o = jax.ShapeDtypeStruct(w.shape, F32)
    return _pcall(body, name=name, grid=(rows // t,), in_specs=[spec3, spec2, spec3, spec3], out_specs=(spec3,) * 3,
                  out_shape=(o, o, o), compiler_params=_params(("parallel",)))(w, g, m, v)


def _adamw_reduced1(w, m, v, part, recv, sel, name):
    _, rows, cols = w.shape
    t = _row_tile(rows, cols)

    def body(sel_ref, w_ref, m_ref, v_ref, p_ref, r_ref, g_ref, d_ref, nm_ref, nv_ref):
        g = ((p_ref[...].astype(F32) + r_ref[0].astype(F32)) + r_ref[1].astype(F32)) + r_ref[2].astype(F32)
        g_ref[...] = g
        d_ref[...], nm_ref[...], nv_ref[...] = _adamw_math(w_ref[...], g, m_ref[...], v_ref[...])

    wspec = pl.BlockSpec((None, t, cols), lambda i, s: (0, i, 0))
    gs = pltpu.PrefetchScalarGridSpec(
        num_scalar_prefetch=1, grid=(rows // t,),
        in_specs=[wspec, wspec, wspec, pl.BlockSpec((None, t, cols), lambda i, s: (s[0], i, 0)),
                  pl.BlockSpec((3, t, cols), lambda i, s: (0, i, 0))],
        out_specs=(wspec,) * 4)
    o = jax.ShapeDtypeStruct(w.shape, F32)
    return _pcall(body, name=name, grid_spec=gs, out_shape=(o, o, o, o),
                  compiler_params=_params(("parallel",)))(sel, w, m, v, part, recv)


def _adamw_reduced(w, m, v, parts, recvs, sel):
    _, rows, cols = w.shape
    half = cols // 2
    t = _row_tile(rows, half)

    def body(sel_ref, w_ref, m_ref, v_ref, pa_ref, pb_ref, ra_ref, rb_ref, g_ref, d_ref, nm_ref, nv_ref):
        total = lambda p_ref, r_ref: ((p_ref[...].astype(F32) + r_ref[0].astype(F32)) + r_ref[1].astype(F32)) \
            + r_ref[2].astype(F32)
        g = jnp.where(pl.program_id(1) == 0, total(pa_ref, ra_ref), total(pb_ref, rb_ref))
        g_ref[...] = g
        d_ref[...], nm_ref[...], nv_ref[...] = _adamw_math(w_ref[...], g, m_ref[...], v_ref[...])

    wspec = pl.BlockSpec((None, t, half), lambda i, j, s: (0, i, j))
    pspec = pl.BlockSpec((None, t, half), lambda i, j, s: (s[0], i, 0))
    rspec = pl.BlockSpec((3, t, half), lambda i, j, s: (0, i, 0))
    gs = pltpu.PrefetchScalarGridSpec(num_scalar_prefetch=1, grid=(rows // t, 2),
                                      in_specs=[wspec, wspec, wspec, pspec, pspec, rspec, rspec],
                                      out_specs=(wspec,) * 4)
    o = jax.ShapeDtypeStruct(w.shape, F32)
    return _pcall(body, name="adamw_w_in", grid_spec=gs, out_shape=(o, o, o, o),
                  compiler_params=_params(("parallel", "arbitrary")))(sel, w, m, v, *parts, *recvs)


def _adamw_small(ws, gs, ms, vs):
    n = len(ws)

    def body(*refs):
        for i in range(n):
            w_ref, g_ref, m_ref, v_ref = (refs[k * n + i] for k in range(4))
            d, nm, nv = _adamw_math(w_ref[...], g_ref[...], m_ref[...], v_ref[...])
            refs[4 * n + i][...] = d
            refs[5 * n + i][...] = nm
            refs[6 * n + i][...] = nv

    shapes = tuple(jax.ShapeDtypeStruct(w.shape, F32) for w in ws)
    res = _pcall(body, name="adamw_small", out_shape=shapes * 3, compiler_params=_params())(*ws, *gs, *ms, *vs)
    return res[:n], res[n:2 * n], res[2 * n:]


def _mesh_pos():
    return lax.axis_index("x"), lax.axis_index("y"), lax.axis_index("c")


class _Gather:
    def __init__(self, arrs):
        self.ins = list(arrs)
        self.out_shape = tuple(jax.ShapeDtypeStruct((N_DEV,) + a.shape, a.dtype) for a in arrs)
        n = len(arrs)
        self.sems = [pltpu.SemaphoreType.DMA((7 * n,)), pltpu.SemaphoreType.DMA((7 * n,)),
                     pltpu.SemaphoreType.DMA((n,))]

    def _copies(self, ins, outs, sems):
        send_sems, recv_sems, local_sems = sems
        x, y, c = _mesh_pos()
        me, sibling = (x, y, c), (x, y, 1 - c)
        chips = [(1 - x, y), (x, 1 - y), (1 - x, 1 - y)]

        def copy(p, k, block, to, from_input=False):
            dst = outs[p].at[_slot(block)]
            return pltpu.make_async_remote_copy(
                src_ref=ins[p] if from_input else dst, dst_ref=dst, send_sem=send_sems.at[7 * p + k],
                recv_sem=recv_sems.at[7 * p + k], device_id=to, device_id_type=MESH)

        npc = len(self.ins)
        mine = [pltpu.make_async_copy(ins[p], outs[p].at[_slot(me)], local_sems.at[p]) for p in range(npc)]
        first = []
        for p in range(npc):
            first.append(copy(p, 0, me, sibling, from_input=True))
            first += [copy(p, 1 + j, me, (*chip, c), from_input=True) for j, chip in enumerate(chips)]
        return me, sibling, chips, c, copy, mine, first

    def start(self, ins, outs, sems):
        *_, mine, first = self._copies(ins, outs, sems)
        for cp in mine + first:
            cp.start()

    def finish(self, ins, outs, sems):
        me, sibling, chips, c, copy, mine, first = self._copies(ins, outs, sems)
        npc = len(self.ins)
        passed = []
        for p in range(npc):
            for j, chip in enumerate(chips):
                copy(p, 1 + j, (*chip, c), me).wait_recv()
                passed.append(copy(p, 4 + j, (*chip, c), sibling))
                passed[-1].start()
        for p in range(npc):
            copy(p, 0, sibling, me).wait_recv()
            for j, chip in enumerate(chips):
                copy(p, 4 + j, (*chip, 1 - c), me).wait_recv()
        for cp in first + passed:
            cp.wait_send()
        for cp in mine:
            cp.wait()


class _ExchangeCore:
    def __init__(self, fulls, cols=None):
        self.ins = list(fulls)
        self.cols = cols
        width = lambda f: f.shape[3] if cols is None else cols[1]
        self.out_shape = tuple(jax.ShapeDtypeStruct((4, f.shape[2], width(f)), f.dtype) for f in fulls)
        self.sems = [pltpu.SemaphoreType.DMA((4 * len(fulls),)), pltpu.SemaphoreType.DMA((4 * len(fulls),))]

    def _copies(self, ins, outs, sems):
        send_sems, recv_sems = sems
        x, y, c = _mesh_pos()

        def src(a, q):
            ref = ins[a].at[q, 1 - c]
            return ref if self.cols is None else ref.at[:, pl.ds(*self.cols)]

        return [pltpu.make_async_remote_copy(
            src_ref=src(a, q), dst_ref=outs[a].at[q], send_sem=send_sems.at[4 * a + q],
            recv_sem=recv_sems.at[4 * a + q], device_id=(x, y, 1 - c), device_id_type=MESH)
            for a in range(len(self.ins)) for q in range(4)]

    def start(self, ins, outs, sems):
        for cp in self._copies(ins, outs, sems):
            cp.start()

    def finish(self, ins, outs, sems):
        for cp in self._copies(ins, outs, sems):
            cp.wait()


class _ExchangeChip:
    def __init__(self, parts):
        self.ins = list(parts)
        self.out_shape = tuple(jax.ShapeDtypeStruct((3,) + p.shape[1:], p.dtype) for p in parts)
        self.sems = [pltpu.SemaphoreType.DMA((3 * len(parts),)), pltpu.SemaphoreType.DMA((3 * len(parts),))]

    def _copies(self, ins, outs, sems):
        send_sems, recv_sems = sems
        x, y, c = _mesh_pos()
        chips = [(1 - x, y), (x, 1 - y), (1 - x, 1 - y)]
        return [pltpu.make_async_remote_copy(
            src_ref=ins[a].at[2 * px + py], dst_ref=outs[a].at[j], send_sem=send_sems.at[3 * a + j],
            recv_sem=recv_sems.at[3 * a + j], device_id=(px, py, c), device_id_type=MESH)
            for a in range(len(self.ins)) for j, (px, py) in enumerate(chips)]

    def start(self, ins, outs, sems):
        for cp in self._copies(ins, outs, sems):
            cp.start()

    def finish(self, ins, outs, sems):
        for cp in self._copies(ins, outs, sems):
            cp.wait()


HBM_ONLY = pl.BlockSpec(memory_space=pltpu.HBM)
SEM_SPEC = pl.BlockSpec(memory_space=pltpu.SEMAPHORE)
SIDE_EFFECT = pltpu.SideEffectType.DATAFLOW_SIDE_EFFECTING


def _chip_copies(p_refs, land_refs, send_sems, recv_sems):
    x, y, c = _mesh_pos()
    return [pltpu.make_async_remote_copy(
        src_ref=p_refs[a].at[2 * px + py], dst_ref=land_refs[a].at[j], send_sem=send_sems.at[3 * a + j],
        recv_sem=recv_sems.at[3 * a + j], device_id=(px, py, c), device_id_type=MESH)
        for a in range(len(p_refs)) for j, (px, py) in enumerate([(1 - x, y), (x, 1 - y), (1 - x, 1 - y)])]


def _chip_exchange_start(parts, name):
    n = len(parts)
    lands = [lax.empty((3,) + p.shape[1:], p.dtype) for p in parts]

    def body(*refs):
        p_refs, land_refs, (send_sems, recv_sems) = refs[:n], refs[n:2 * n], refs[2 * n:2 * n + 2]
        for cp in _chip_copies(p_refs, land_refs, send_sems, recv_sems):
            cp.start()
        token = refs[-1]
        token[...] = jnp.zeros_like(token)

    hbm = lambda t: pltpu.HBM(t.shape, t.dtype)
    res = pl.pallas_call(
        body, name=name,
        out_shape=(pltpu.SemaphoreType.DMA((3 * n,)), pltpu.SemaphoreType.DMA((3 * n,)), *[hbm(t) for t in parts + lands],
                   jax.ShapeDtypeStruct((8, 128), F32)),
        in_specs=(HBM_ONLY,) * (2 * n),
        out_specs=(SEM_SPEC, SEM_SPEC, *[HBM_ONLY] * (2 * n), pl.BlockSpec(memory_space=pltpu.VMEM)),
        input_output_aliases={i: 2 + i for i in range(2 * n)},
        compiler_params=pltpu.CompilerParams(has_side_effects=SIDE_EFFECT))(
        *[pltpu.with_memory_space_constraint(t, pltpu.HBM) for t in parts + lands])
    return (res[0], res[1], list(res[2:2 + n]), list(res[2 + n:2 + 2 * n])), res[-1]


def _chip_exchange_wait(in_flight, after, name):
    send_sems, recv_sems, parts, lands = in_flight
    n = len(parts)

    def body(*refs):
        p_refs, land_refs, (send_sems, recv_sems) = refs[:n], refs[n:2 * n], refs[2 * n:2 * n + 2]
        for cp in _chip_copies(p_refs, land_refs, send_sems, recv_sems):
            cp.wait_send()
            cp.wait_recv()

    res = pl.pallas_call(
        body, name=name, out_shape=tuple(pltpu.HBM(t.shape, t.dtype) for t in parts + lands),
        in_specs=(*[HBM_ONLY] * (2 * n), SEM_SPEC, SEM_SPEC, pl.BlockSpec(memory_space=pl.ANY)),
        out_specs=(HBM_ONLY,) * (2 * n), input_output_aliases={i: i for i in range(2 * n)},
        compiler_params=pltpu.CompilerParams(has_side_effects=SIDE_EFFECT))(*parts, *lands, send_sems, recv_sems, after)
    return list(res[:n]), list(res[n:])


def _slot(p):
    return 4 * p[0] + 2 * p[1] + p[2]


def _gather_copies(src_refs, out_refs, send_sems, recv_sems):
    x, y, c = _mesh_pos()
    targets = [(x, y, 1 - c), (1 - x, y, c), (x, 1 - y, c), (1 - x, 1 - y, c)]
    return [pltpu.make_async_remote_copy(
        src_ref=src_refs[a], dst_ref=out_refs[a].at[_slot((x, y, c))], send_sem=send_sems.at[4 * a + k],
        recv_sem=recv_sems.at[4 * a + k], device_id=to, device_id_type=MESH)
        for a in range(len(src_refs)) for k, to in enumerate(targets)]


def _gather_start(shards, after, name):
    n = len(shards)
    outs = [lax.empty((N_DEV,) + s.shape, s.dtype) for s in shards]

    def body(*refs):
        for cp in _gather_copies(refs[:n], refs[n:2 * n], refs[2 * n + 1], refs[2 * n + 2]):
            cp.start()
        token = refs[-1]
        token[...] = jnp.zeros_like(token)

    res = pl.pallas_call(
        body, name=name,
        out_shape=(pltpu.SemaphoreType.DMA((4 * n,)), pltpu.SemaphoreType.DMA((4 * n,)),
                   *[pltpu.HBM(t.shape, t.dtype) for t in shards + outs], jax.ShapeDtypeStruct((8, 128), F32)),
        in_specs=(*[HBM_ONLY] * (2 * n), pl.BlockSpec(memory_space=pl.ANY)),
        out_specs=(SEM_SPEC, SEM_SPEC, *[HBM_ONLY] * (2 * n), pl.BlockSpec(memory_space=pltpu.VMEM)),
        input_output_aliases={i: 2 + i for i in range(2 * n)},
        compiler_params=pltpu.CompilerParams(has_side_effects=SIDE_EFFECT))(
        *[pltpu.with_memory_space_constraint(t, pltpu.HBM) for t in shards + outs], after)
    return (res[0], res[1], list(res[2:2 + n]), list(res[2 + n:2 + 2 * n])), res[-1]


def _gather_wait(in_flight, after, name):
    send_sems, recv_sems, shards, outs = in_flight
    n = len(shards)

    def body(*refs):
        for cp in _gather_copies(refs[:n], refs[n:2 * n], refs[2 * n], refs[2 * n + 1]):
            cp.wait_send()
            cp.wait_recv()

    res = pl.pallas_call(
        body, name=name, out_shape=tuple(pltpu.HBM(t.shape, t.dtype) for t in shards + outs),
        in_specs=(*[HBM_ONLY] * (2 * n), SEM_SPEC, SEM_SPEC, pl.BlockSpec(memory_space=pl.ANY)),
        out_specs=(HBM_ONLY,) * (2 * n), input_output_aliases={i: i for i in range(2 * n)},
        compiler_params=pltpu.CompilerParams(has_side_effects=SIDE_EFFECT))(*shards, *outs, send_sems, recv_sems, after)
    return list(res[:n]), list(res[n:])


class _PassToSibling:
    def __init__(self, shards, gathered):
        n = self.n = len(shards)
        self.ins = list(shards) + list(gathered)
        self.out_shape = tuple(jax.ShapeDtypeStruct(g.shape, g.dtype) for g in gathered)
        self.aliases = {n + a: a for a in range(n)}
        self.sems = [pltpu.SemaphoreType.DMA((3 * n,)), pltpu.SemaphoreType.DMA((3 * n,)),
                     pltpu.SemaphoreType.DMA((n,))]

    def _copies(self, ins, outs, sems):
        send_sems, recv_sems, local_sems = sems
        x, y, c = _mesh_pos()
        chips = [(1 - x, y), (x, 1 - y), (1 - x, 1 - y)]
        mine = [pltpu.make_async_copy(ins[a], outs[a].at[_slot((x, y, c))], local_sems.at[a]) for a in range(self.n)]
        passed, awaited = [], []
        for a in range(self.n):
            for j, chip in enumerate(chips):
                sems_j = dict(send_sem=send_sems.at[3 * a + j], recv_sem=recv_sems.at[3 * a + j],
                              device_id=(x, y, 1 - c), device_id_type=MESH)
                blk = outs[a].at[_slot((*chip, c))]
                passed.append(pltpu.make_async_remote_copy(src_ref=blk, dst_ref=blk, **sems_j))
                got = outs[a].at[_slot((*chip, 1 - c))]
                awaited.append(pltpu.make_async_remote_copy(src_ref=got, dst_ref=got, **sems_j))
        return mine, passed, awaited

    def start(self, ins, outs, sems):
        mine, passed, _ = self._copies(ins, outs, sems)
        for cp in mine + passed:
            cp.start()

    def finish(self, ins, outs, sems):
        mine, passed, awaited = self._copies(ins, outs, sems)
        for cp in passed:
            cp.wait_send()
        for cp in awaited:
            cp.wait_recv()
        for cp in mine:
            cp.wait()


def _reduce_sums(fulls, recv_core, core, tag):
    return [_pair_sum(f, r, core, f"rs_pair_{tag}{i}") for i, (f, r) in enumerate(zip(fulls, recv_core))]


def _local_step(x, tgt, mods, w_in_shard, order, shards, small, chip, core):
    sh1, sc1, g1, sh2, sc2, g2 = mods
    norm1_g, rel_bias, gn_g, gn_b, norm2_g, norm_f_g = small
    tables = _ret_tables()
    buckets = jnp.asarray(_bucket_tables())

    proj, slabs, w_in_t, h1 = _gather_proj(x, norm1_g, sh1, sc1, w_in_shard, order)
    flight_w1, token_w = _gather_start(list(shards[:3]), proj, "gather_w1_start")
    flight_w2, token_w = _gather_start(list(shards[3:]), token_w, "gather_w2_start")
    bias = _bias_build(rel_bias, buckets, token_w)
    outs, lses = [], []
    for gi in range(len(ATT_GROUPS)):
        o, l = _att_fwd(slabs, bias, gi)
        outs.append(o)
        lses.append(l)
    att, gathered = _mix_fwd(outs, lses, comm=_PassToSibling(*_gather_wait(flight_w1, lses[2], "gather_w1_wait")))
    w_ret_out, w_att_out, w_o = (_from_slots(g, ax) for g, ax in zip(gathered, BIG_AXES[1:4]))
    gated, ro, states = _ret_fwd(proj, tables, gn_g, gn_b, att)
    ret_out, gathered = _mm(gated, w_ret_out, 'nn', tm=S, tn=256, tk=2048, name="ret_out",
                            comm=_PassToSibling(*_gather_wait(flight_w2, gated, "gather_w2_wait")))
    w_ff1, w_ff2 = (_from_slots(g, ax) for g, ax in zip(gathered, BIG_AXES[4:]))
    att_out, merged = _att_out_merge(att, w_att_out, proj, ret_out)
    mixo, x1, h2 = _w_o_norm2(merged, w_o, x, g1, norm2_g, sh2, sc2)
    u, act = _mm(h2, w_ff1, 'nn', tm=S, tn=512, tk=D, name="ff1", relu2=True)
    loss, dx2, g_normf, df, dg2 = _ff2_final(act, w_ff2, x1, g2, tgt, norm_f_g)

    gw_ff2 = _mm(act, df, 'tn', tm=512, tn=D, tk=S, name="gw_ff2", out_dtype=BF16)
    du = _mm(df, w_ff2, 'nt', tm=S, tn=512, tk=D, name="d_act", out_dtype=BF16, relu2_of=u)
    gw_ff1 = _mm(h2, du, 'tn', tm=D, tn=512, tk=S, name="gw_ff1", out_dtype=BF16)
    fulls_a = [_to_slots(g, ax) for g, ax in zip((gw_ff1, gw_ff2), BIG_AXES[4:])]
    dh2, recv_core_a = _mm(du, w_ff1, 'nt', tm=1024, tn=1024, tk=2048, name="dh2", comm=_ExchangeCore(fulls_a))
    parts_a = _reduce_sums(fulls_a, recv_core_a, core, "a")
    flight_a, token_a = _chip_exchange_start(parts_a, "rs_a_start")
    dx1, dsc2, dsh2, g_norm2, dmixo, dg1 = _norm_mod_bwd(x1, norm2_g, sc2, dh2, dx2, "norm2_bwd", gate=(mixo, g1))

    gw_o = _mm(merged, dmixo, 'tn', tm=D, tn=512, tk=S, name="gw_o", out_dtype=BF16, after=token_a)
    d_ret_out, d_att_out, dga, dgb = _dmerged_split(dmixo, w_o, proj, ret_out, att_out)
    gw_ret_out = _mm(gated, d_ret_out, 'tn', tm=512, tn=D, tk=S, name="gw_ret_out", out_dtype=BF16)
    gw_att_out = _mm(att, d_att_out, 'tn', tm=AW, tn=D, tk=S, name="gw_att_out", out_dtype=BF16)
    fulls_b = [_to_slots(g, ax) for g, ax in zip((gw_ret_out, gw_att_out, gw_o), BIG_AXES[1:4])]
    dgated, recv_core_b = _mm(d_ret_out, w_ret_out, 'nt', tm=S, tn=512, tk=D, name="dgated",
                              comm=_ExchangeCore(fulls_b))
    parts_b = _reduce_sums(fulls_b, recv_core_b, core, "b")
    flight_b, token_b = _chip_exchange_start(parts_b, "rs_b_start")
    datt = _mm(d_att_out, w_att_out, 'nt', tm=S, tn=AW, tk=D, name="datt", after=token_b)
    mix_grads = _mix_bwd(outs, lses, datt)
    datt_parts, ds_sums = [], []
    for gi in range(len(ATT_GROUPS)):
        dq, dk, dv, ds_sum = _att_bwd(slabs, bias, outs[gi], lses[gi], mix_grads[gi], mix_grads[3 + gi], gi)
        datt_parts += [dq, dk, dv]
        ds_sums.append(ds_sum)
    g_bias = _bias_grad(jnp.concatenate(ds_sums, axis=0), buckets)[:, :, 0].T.reshape(1, -1)
    dproj, g_gn_g, g_gn_b = _ret_bwd(proj, tables, gn_g, gn_b, ro, states, dgated, datt_parts + [dga, dgb])
    parts_a, recv_chip_a = _chip_exchange_wait(flight_a, dproj, "rs_a_wait")
    parts_b, recv_chip_b = _chip_exchange_wait(flight_b, dproj, "rs_b_wait")
    reduced = list(zip(parts_b + parts_a, recv_chip_b + recv_chip_a))
    full_in = _to_slots(_mm(dproj, h1, 'tn', tm=512, tn=D, tk=S, name="gw_in", out_dtype=BF16), 0)
    halves = [(half * (D // 2), D // 2) for half in range(2)]
    (recv_core_in0,) = _run_comm(_ExchangeCore([full_in], cols=halves[0]), "rs_core_in0")
    flight0, token = _chip_exchange_start([_pair_sum(full_in, recv_core_in0, core, "rs_pair_c0", col_block=0)],
                                          "rs_in0_start")
    dh1, (recv_core_in1,) = _mm(dproj, w_in_t, 'nn', tm=1024, tn=1024, tk=2560, name="dh1",
                                comm=_ExchangeCore([full_in], cols=halves[1]), after=token)
    flight1, token = _chip_exchange_start([_pair_sum(full_in, recv_core_in1, core, "rs_pair_c1", col_block=1)],
                                          "rs_in1_start")
    in_flight = [flight0, flight1]
    gx, dsc1, dsh1, g_norm1 = _norm_mod_bwd(x, norm1_g, sc1, dh1, dx1, "norm1_bwd", after=token)

    dmod = [dsh1, dsc1, dg1, dsh2, dsc2, dg2]
    small_g = [g_norm1, g_bias, g_gn_g, g_gn_b, g_norm2, g_normf]
    return loss, gx, in_flight, reduced, small_g, dmod


def _to_slots(g, axis):
    if axis == 0:
        return g.reshape(4, 2, g.shape[0] // N_DEV, g.shape[1])
    return g.reshape(g.shape[0], N_DEV, g.shape[1] // N_DEV).transpose(1, 0, 2).reshape(4, 2, g.shape[0], -1)


def _from_slots(w8, axis):
    if axis == 0:
        return w8.reshape(-1, w8.shape[2])
    return w8.transpose(1, 0, 2).reshape(w8.shape[1], -1)


BIG_AXES = (1, 0, 1, 0, 1, 0)


def kernel(x, c, w_ada, b_ada, norm1_g, w_in, rel_bias, ret_gn_g, ret_gn_b, w_ret_out, w_att_out, w_o, norm2_g, w_ff1, w_ff2, norm_f_g, loss_target, m_w_ada, m_b_ada, m_norm1_g, m_w_in, m_rel_bias, m_ret_gn_g, m_ret_gn_b, m_w_ret_out, m_w_att_out, m_w_o, m_norm2_g, m_w_ff1, m_w_ff2, m_norm_f_g, v_w_ada, v_b_ada, v_norm1_g, v_w_in, v_rel_bias, v_ret_gn_g, v_ret_gn_b, v_w_ret_out, v_w_att_out, v_w_o, v_norm2_g, v_w_ff1, v_w_ff2, v_norm_f_g):
    mx, my, mc = _mesh_pos()
    dev = 4 * mx + 2 * my + mc
    chip = jnp.reshape(2 * mx + my, (1,)).astype(jnp.int32)
    core = jnp.reshape(mc, (1,)).astype(jnp.int32)
    ada_w = D * 6 // N_DEV

    w_in, m_w_in, v_w_in = (jnp.transpose(t, (0, 2, 1)) for t in (w_in, m_w_in, v_w_in))

    shards = [w[0].astype(BF16) for w in (w_in, w_ret_out, w_att_out, w_o, w_ff1, w_ff2)]
    (c_all,) = _run_comm(_Gather([c]), "gather_c")
    c_all = c_all.reshape(N_DEV, D)
    b_sl = lax.dynamic_slice(b_ada, (0, dev * ada_w), (1, ada_w))
    (mod_all,) = _run_comm(_Gather([_ada_fwd(c_all, w_ada[0], b_sl)]), "gather_mod")
    mod = lax.dynamic_index_in_dim(mod_all, dev, axis=1, keepdims=False).reshape(6, D)
    mods = tuple(mod[i:i + 1] for i in range(6))

    small = (norm1_g, rel_bias, ret_gn_g, ret_gn_b, norm2_g, norm_f_g.reshape(1, D))
    order = lax.dynamic_index_in_dim(jnp.asarray(_proj_order()), 2 * mx + my, axis=0, keepdims=False)
    loss, gx, in_flight, big_red, small_g, dmod = _local_step(x[0], loss_target[0], mods, shards[0], order,
                                                              shards[1:], small, chip, core)

    names = ['w_ada', 'b_ada', 'norm1_g', 'w_in', 'rel_bias', 'ret_gn_g', 'ret_gn_b', 'w_ret_out', 'w_att_out',
             'w_o', 'norm2_g', 'w_ff1', 'w_ff2', 'norm_f_g']
    ws = dict(zip(names, (w_ada, b_ada, norm1_g, w_in, rel_bias, ret_gn_g, ret_gn_b, w_ret_out, w_att_out, w_o,
                          norm2_g, w_ff1, w_ff2, norm_f_g)))
    ms = dict(zip(names, (m_w_ada, m_b_ada, m_norm1_g, m_w_in, m_rel_bias, m_ret_gn_g, m_ret_gn_b, m_w_ret_out,
                          m_w_att_out, m_w_o, m_norm2_g, m_w_ff1, m_w_ff2, m_norm_f_g)))
    vs = dict(zip(names, (v_w_ada, v_b_ada, v_norm1_g, v_w_in, v_rel_bias, v_ret_gn_g, v_ret_gn_b, v_w_ret_out,
                          v_w_att_out, v_w_o, v_norm2_g, v_w_ff1, v_w_ff2, v_norm_f_g)))
    grads, delta, new_m, new_v = {}, {}, {}, {}
    big_names = ('w_ret_out', 'w_att_out', 'w_o', 'w_ff1', 'w_ff2')
    for n, (part, recv) in zip(big_names, big_red):
        grads[n], delta[n], new_m[n], new_v[n] = _adamw_reduced1(ws[n], ms[n], vs[n], part, recv, chip, "adamw_" + n)
    updated = lax.optimization_barrier((gx, tuple(delta[n] for n in big_names)))
    gathered = _run_comm(_Gather(dmod + small_g + [loss]), "gather_small", after=updated[0])
    g_b_ada, dmod_all, (g_norm1, g_bias, g_gn_g, g_gn_b, g_norm2, g_normf, loss_sum) = _sum_small(gathered)
    loss_out = loss_sum[0, 0]
    g_w_ada = _ada_bwd(c_all, lax.dynamic_slice(dmod_all, (0, dev * ada_w), (N_DEV, ada_w)))

    grads.update(w_ada=g_w_ada.reshape(w_ada.shape), b_ada=g_b_ada, norm1_g=g_norm1, rel_bias=g_bias,
                 ret_gn_g=g_gn_g, ret_gn_b=g_gn_b, norm2_g=g_norm2, norm_f_g=g_normf)
    delta['w_ada'], new_m['w_ada'], new_v['w_ada'] = _adamw(w_ada, g_w_ada, m_w_ada, v_w_ada, "adamw_w_ada")
    small_names = ('b_ada', 'norm1_g', 'rel_bias', 'ret_gn_g', 'ret_gn_b', 'norm2_g', 'norm_f_g')
    two_d = {n: (1, ws[n].size) if ws[n].ndim == 1 else ws[n].shape for n in small_names}
    d_, m_, v_ = _adamw_small(*[[src[n].reshape(two_d[n]) for n in small_names] for src in (ws, grads, ms, vs)])
    for i, n in enumerate(small_names):
        shp = ws[n].shape
        delta[n], new_m[n], new_v[n] = d_[i].reshape(shp), m_[i].reshape(shp), v_[i].reshape(shp)
        grads[n] = grads[n].reshape(shp)

    done = lax.optimization_barrier((gx, tuple(d_), tuple(delta[n] for n in ('w_ada', 'w_ret_out', 'w_att_out', 'w_o',
                                                                               'w_ff1', 'w_ff2'))))
    parts_in, recvs_in = [], []
    for half, flight in enumerate(in_flight):
        (part_in,), (recv_chip_in,) = _chip_exchange_wait(flight, done[0], f"rs_in{half}_wait")
        parts_in.append(part_in)
        recvs_in.append(recv_chip_in)
    grads['w_in'], delta['w_in'], new_m['w_in'], new_v['w_in'] = _adamw_reduced(w_in, m_w_in, v_w_in, parts_in,
                                                                               recvs_in, chip)
    for d in (grads, delta, new_m, new_v):
        d['w_in'] = jnp.transpose(d['w_in'], (0, 2, 1))
    return (loss_out, gx[None], *[grads[n] for n in names], *[delta[n] for n in names],
            *[new_m[n] for n in names], *[new_v[n] for n in names])
```

```python
import functools
import math

import numpy as np
import jax
import jax.numpy as jnp
from jax import lax
from jax.experimental import pallas as pl
from jax.experimental.pallas import tpu as pltpu

F32 = jnp.float32
BF16 = jnp.bfloat16
MESH = pl.DeviceIdType.MESH

N_DEV = 8
S = 2048
D = 1024
RET_HEADS = 4
RET_DK = 256
RET_DV = 512
CHUNK = 128
N_CHUNK = S // CHUNK
ATT_GROUPS = ((128, 1), (512, 4), (2048, 16))
ATT_HG = 4
ATT_DH = 128
ATT_BLK = 128
N_BUCKETS = 32
MAX_DIST = 2048
D_FF = 4096
IN_COLS = 12800
OFF_RQ, OFF_RK, OFF_RV, OFF_RG, OFF_ATT = 0, 1024, 2048, 4096, 6144
OFF_GA, OFF_GB = 6144, 7168
RMS_EPS = 1e-6
GN_EPS = 1e-5
ADAM_LR, ADAM_B1, ADAM_B2, ADAM_EPS, ADAM_WD, ADAM_STEP = 0.001, 0.9, 0.999, 1e-08, 0.01, 10
VMEM_LIMIT = 48 * 1024 * 1024


def _pcall(body, **kw):
    return pl.pallas_call(body, **kw)


def _params(sem=None):
    return pltpu.CompilerParams(dimension_semantics=sem, vmem_limit_bytes=VMEM_LIMIT)


HBM_SPEC = pl.BlockSpec(memory_space=pl.ANY)


def _carry(body, comm, *, name, grid, in_specs, out_specs, out_shape, scratch_shapes=()):
    single = not isinstance(out_specs, (tuple, list))
    o_specs = (out_specs,) if single else tuple(out_specs)
    o_shape = (out_shape,) if single else tuple(out_shape)
    n_in, n_out, n_scr = len(in_specs), len(o_specs), len(scratch_shapes)
    nci, nco = len(comm.ins), len(comm.out_shape)
    total = int(np.prod(grid))

    def wrapped(*refs):
        bounds = np.cumsum([0, n_in, nci, n_out, nco, n_scr])
        a, ci, o, co, scr = (refs[bounds[i]:bounds[i + 1]] for i in range(5))
        sems = refs[bounds[5]:]
        flat = 0
        for d, g in enumerate(grid):
            flat = flat * g + pl.program_id(d)

        @pl.when(flat == 0)
        def _():
            comm.start(ci, co, sems)

        body(*a, *o, *scr)

        @pl.when(flat == total - 1)
        def _():
            comm.finish(ci, co, sems)

    aliases = {n_in + i: n_out + o for i, o in getattr(comm, "aliases", {}).items()}
    call = _pcall(wrapped, name=name, grid=grid, in_specs=list(in_specs) + [HBM_SPEC] * nci,
                  out_specs=o_specs + (HBM_SPEC,) * nco, out_shape=o_shape + tuple(comm.out_shape),
                  scratch_shapes=list(scratch_shapes) + list(comm.sems), input_output_aliases=aliases,
                  compiler_params=_params(("arbitrary",) * len(grid)))

    def run(*args):
        res = call(*args, *comm.ins)
        own = res[0] if single else tuple(res[:n_out])
        return own, tuple(res[n_out:])

    return run


def _run_comm(comm, name, after=None):
    nci, nco = len(comm.ins), len(comm.out_shape)
    extra = [] if after is None else [after]

    def body(*refs):
        ci, co, sems = refs[:nci], refs[nci + len(extra):nci + len(extra) + nco], refs[nci + len(extra) + nco:]
        comm.start(ci, co, sems)
        comm.finish(ci, co, sems)

    return _pcall(body, name=name, in_specs=[HBM_SPEC] * (nci + len(extra)), out_specs=(HBM_SPEC,) * nco,
                  out_shape=tuple(comm.out_shape), scratch_shapes=list(comm.sems))(*comm.ins, *extra)


def _dot(a, b, dn):
    return lax.dot_general(a.astype(BF16), b.astype(BF16), (dn, ((), ())), preferred_element_type=F32)


NN = ((1,), (0,))
NT = ((1,), (1,))
TN = ((0,), (0,))


def _mm(a, b, mode, *, tm, tn, tk, name, out_dtype=F32, res=None, gvec=None, relu2=False, relu2_of=None, comm=None,
        after=None):
    if mode == 'nn':
        (M, K), (_, N) = a.shape, b.shape
        a_spec = pl.BlockSpec((tm, tk), lambda i, j, k: (i, k))
        b_spec = pl.BlockSpec((tk, tn), lambda i, j, k: (k, j))
        dn = NN
    elif mode == 'nt':
        (M, K), (N, _) = a.shape, b.shape
        a_spec = pl.BlockSpec((tm, tk), lambda i, j, k: (i, k))
        b_spec = pl.BlockSpec((tn, tk), lambda i, j, k: (j, k))
        dn = NT
    else:
        (K, M), (_, N) = a.shape, b.shape
        a_spec = pl.BlockSpec((tk, tm), lambda i, j, k: (k, i))
        b_spec = pl.BlockSpec((tk, tn), lambda i, j, k: (k, j))
        dn = TN
    assert M % tm == 0 and N % tn == 0 and K % tk == 0, (name, M, N, K)
    nk = K // tk
    fused = res is not None
    o_spec = pl.BlockSpec((tm, tn), lambda i, j, k: (i, j))

    def body(a_ref, b_ref, *rest):
        acc_ref = rest[-1] if nk > 1 else None
        if after is not None:
            rest = rest[1:]
        if fused:
            res_ref, g_ref, o_ref, x_ref = rest[:4]
        elif relu2_of is not None:
            u_ref, o_ref = rest[:2]
        elif relu2:
            o_ref, act_ref = rest[:2]
        else:
            o_ref = rest[0]

        def finish(acc):
            if relu2_of is not None:
                acc = acc * (2.0 * jnp.maximum(u_ref[...], 0.0))
            o_ref[...] = acc.astype(o_ref.dtype)
            if fused:
                x_ref[...] = res_ref[...] + g_ref[...] * acc
            if relu2:
                r = jnp.maximum(acc, 0.0)
                act_ref[...] = (r * r).astype(BF16)

        p = _dot(a_ref[...], b_ref[...], dn)
        if nk == 1:
            finish(p)
        else:
            k = pl.program_id(2)

            @pl.when(k == 0)
            def _():
                acc_ref[...] = p

            @pl.when(k > 0)
            def _():
                acc_ref[...] += p

            @pl.when(k == nk - 1)
            def _():
                finish(acc_ref[...])

    in_specs = [a_spec, b_spec]
    args = [a, b]
    if after is not None:
        in_specs.append(pl.BlockSpec(memory_space=pl.ANY))
        args.append(after)
    out_shape = jax.ShapeDtypeStruct((M, N), out_dtype)
    out_specs = o_spec
    if fused:
        in_specs += [pl.BlockSpec((tm, tn), lambda i, j, k: (i, j)), pl.BlockSpec((1, tn), lambda i, j, k: (0, j))]
        args += [res, gvec]
        out_shape = (out_shape, jax.ShapeDtypeStruct((M, N), F32))
        out_specs = (o_spec, pl.BlockSpec((tm, tn), lambda i, j, k: (i, j)))
    elif relu2_of is not None:
        in_specs.append(pl.BlockSpec((tm, tn), lambda i, j, k: (i, j)))
        args.append(relu2_of)
    elif relu2:
        out_shape = (out_shape, jax.ShapeDtypeStruct((M, N), BF16))
        out_specs = (o_spec, pl.BlockSpec((tm, tn), lambda i, j, k: (i, j)))
    kw = dict(name=name, grid=(M // tm, N // tn, nk), in_specs=in_specs, out_specs=out_specs,
              out_shape=out_shape, scratch_shapes=[pltpu.VMEM((tm, tn), F32)] if nk > 1 else [])
    if comm is not None:
        return _carry(body, comm, **kw)(*args)
    return _pcall(body, compiler_params=_params(("parallel", "parallel", "arbitrary")), **kw)(*args)


PROJ_TN = 512
ATT_T0, ATT_T1 = 6144 // PROJ_TN, 10752 // PROJ_TN
N_SLABS = (ATT_T1 - ATT_T0) * 4
MAIN_COLS = IN_COLS - (ATT_T1 - ATT_T0) * PROJ_TN


PROJ_TILES = IN_COLS // PROJ_TN
SHARD_ROWS = IN_COLS // N_DEV
W_CHUNKS = 4
N_OWN, N_NEAR = 5, 18


def _proj_order():
    out = np.zeros((4, 3, PROJ_TILES), np.int32)
    for q in range(4):
        def hops(t):
            owners = {col // (2 * SHARD_ROWS) for col in (t * PROJ_TN, (t + 1) * PROJ_TN - 1)}
            return max(bin(q ^ p).count("1") for p in owners)
        order = sorted(range(PROJ_TILES), key=lambda t: (hops(t), t))
        assert all(hops(t) == 0 for t in order[:N_OWN]) and all(hops(t) < 2 for t in order[:N_NEAR])
        is_att = [ATT_T0 <= t < ATT_T1 for t in order]
        for row, kind, index in ((1, False, lambda t: t if t < ATT_T0 else t - (ATT_T1 - ATT_T0)),
                                 (2, True, lambda t: t - ATT_T0)):
            own = [index(t) if a == kind else None for t, a in zip(order, is_att)]
            first = next(v for v in own if v is not None)
            last = first
            for j, v in enumerate(own):
                last = last if v is None else v
                out[q, row, j] = last
        out[q, 0] = order
    return out


def _gather_proj(x, g, sh, sc, shard, order):
    rows = SHARD_ROWS // W_CHUNKS

    def body(ord_ref, x_ref, g_ref, shift_ref, scale_ref, sh_ref, main_ref, slab_ref, full_ref, a_ref, wbuf, xbuf,
             fetch_sems, send_sems, recv_sems, local_sems, x_sem):
        j = pl.program_id(0)
        x, y, c = _mesh_pos()
        me, sibling = (x, y, c), (x, y, 1 - c)
        chips = [(1 - x, y), (x, 1 - y), (1 - x, 1 - y)]

        def block(p, owner):
            return full_ref.at[pl.ds(pl.multiple_of(_slot(owner) * SHARD_ROWS + p * rows, 16), rows)]

        def copy(p, k, owner, to, from_input=False):
            dst = block(p, owner)
            return pltpu.make_async_remote_copy(
                src_ref=sh_ref.at[pl.ds(p * rows, rows)] if from_input else dst, dst_ref=dst,
                send_sem=send_sems.at[7 * p + k], recv_sem=recv_sems.at[7 * p + k], device_id=to, device_id_type=MESH)

        pieces = range(W_CHUNKS)
        mine = [pltpu.make_async_copy(sh_ref.at[pl.ds(p * rows, rows)], block(p, me), local_sems.at[p]) for p in pieces]
        first = [copy(p, 0, me, sibling, from_input=True) for p in pieces]
        first += [copy(p, 1 + n, me, (*chips[n], c), from_input=True) for p in pieces for n in range(2)]
        near_pass = [copy(p, 4 + n, (*chips[n], c), sibling) for p in pieces for n in range(2)]
        relay = [copy(p, 3, ((x + 1 - c) % 2, (y + c) % 2, c), ((x + c) % 2, (y + 1 - c) % 2, c)) for p in pieces]
        far_pass = [copy(p, 6, (*chips[2], c), sibling) for p in pieces]

        def fetch(pos):
            slot = lax.rem(pos, 2)
            start = pl.multiple_of(ord_ref[0, pos] * PROJ_TN, PROJ_TN)
            return pltpu.make_async_copy(full_ref.at[pl.ds(start, PROJ_TN)], wbuf.at[slot], fetch_sems.at[slot])

        @pl.when(j == 0)
        def _():
            x_copy = pltpu.make_async_copy(x_ref, xbuf, x_sem.at[0])
            x_copy.start()
            for cp in mine + first:
                cp.start()
            x_copy.wait()
            for r in range(S // TR):
                rws = pl.ds(r * TR, TR)
                xv = xbuf[rws, :]
                rstd = lax.rsqrt(jnp.mean(xv * xv, axis=-1, keepdims=True) + RMS_EPS)
                n = xv * rstd * g_ref[...]
                a_ref[rws, :] = (n * (1.0 + scale_ref[...]) + shift_ref[...]).astype(BF16)
            for cp in mine:
                cp.wait()
            for p in pieces:
                copy(p, 0, sibling, me).wait_recv()
            fetch(j).start()

        @pl.when(j == N_OWN - 1)
        def _():
            for p in pieces:
                for n in range(2):
                    copy(p, 1 + n, (*chips[n], c), me).wait_recv()
                    near_pass[2 * p + n].start()
                relay[p].start()
            for p in pieces:
                for n in range(2):
                    copy(p, 4 + n, (*chips[n], 1 - c), me).wait_recv()

        @pl.when(j == N_NEAR - 1)
        def _():
            for p in pieces:
                copy(p, 3, (*chips[2], c), me).wait_recv()
                far_pass[p].start()
            for p in pieces:
                copy(p, 6, (*chips[2], 1 - c), me).wait_recv()

        @pl.when(j + 1 < PROJ_TILES)
        def _():
            fetch(j + 1).start()

        fetch(j).wait()
        w_ref = wbuf.at[lax.rem(j, 2)]
        tile = ord_ref[0, j]
        is_att = (tile >= ATT_T0) & (tile < ATT_T1)
        chunks = [pl.ds(r * 512, 512) for r in range(S // 512)]

        @pl.when(jnp.logical_not(is_att))
        def _():
            for rws in chunks:
                main_ref[rws, :] = _dot(a_ref[rws, :], w_ref[...], NT)

        @pl.when(is_att)
        def _():
            for rws in chunks:
                p = _dot(a_ref[rws, :], w_ref[...], NT)
                for h in range(4):
                    slab_ref[h, rws, :] = p[:, h * 128:(h + 1) * 128]

        @pl.when(j == PROJ_TILES - 1)
        def _():
            for cp in first + near_pass + relay + far_pass:
                cp.wait_send()

    vec = pl.BlockSpec((1, D), lambda j, o: (0, 0))
    gs = pltpu.PrefetchScalarGridSpec(
        num_scalar_prefetch=1, grid=(PROJ_TILES,),
        in_specs=[HBM_SPEC, vec, vec, vec, HBM_SPEC],
        out_specs=(pl.BlockSpec((S, PROJ_TN), lambda j, o: (0, o[1, j])),
                   pl.BlockSpec((4, S, 128), lambda j, o: (o[2, j], 0, 0)), HBM_SPEC,
                   pl.BlockSpec((S, D), lambda j, o: (0, 0))),
        scratch_shapes=[pltpu.VMEM((2, PROJ_TN, D), BF16), pltpu.VMEM((S, D), F32), pltpu.SemaphoreType.DMA((2,)),
                        pltpu.SemaphoreType.DMA((7 * W_CHUNKS,)), pltpu.SemaphoreType.DMA((7 * W_CHUNKS,)),
                        pltpu.SemaphoreType.DMA((W_CHUNKS,)), pltpu.SemaphoreType.DMA((1,))])
    return _pcall(body, name="gather_proj", grid_spec=gs,
                  out_shape=(jax.ShapeDtypeStruct((S, MAIN_COLS), F32), jax.ShapeDtypeStruct((N_SLABS, S, 128), F32),
                             jax.ShapeDtypeStruct((IN_COLS, D), BF16), jax.ShapeDtypeStruct((S, D), BF16)),
                  compiler_params=_params(("arbitrary",)))(order, x, g, sh, sc, shard)


TR = 256


def _row_spec(w=D):
    return pl.BlockSpec((TR, w), lambda i: (i, 0))


def _vec_spec(w=D):
    return pl.BlockSpec((1, w), lambda i: (0, 0))


def _norm_mod_bwd(x, g, sc, dh, dres, name, gate=None, after=None):
    gated = gate is not None

    def body(x_ref, g_ref, sc_ref, dh_ref, dres_ref, *rest):
        if after is not None:
            rest = rest[1:]
        if gated:
            f_ref, gv_ref, dx_ref, dsc_ref, dsh_ref, dg_ref, dz_ref, dgv_ref = rest
        else:
            dx_ref, dsc_ref, dsh_ref, dg_ref = rest
        i = pl.program_id(0)
        xv = x_ref[...]
        dh = dh_ref[...]
        rstd = lax.rsqrt(jnp.mean(xv * xv, axis=-1, keepdims=True) + RMS_EPS)
        xhat = xv * rstd
        gv = g_ref[...]
        dn = dh * (1.0 + sc_ref[...])
        dxhat = dn * gv
        dx = dres_ref[...] + rstd * (dxhat - xhat * jnp.mean(dxhat * xhat, axis=-1, keepdims=True))
        dx_ref[...] = dx
        sums = [(dsc_ref, jnp.sum(dh * (xhat * gv), axis=0, keepdims=True)),
                (dsh_ref, jnp.sum(dh, axis=0, keepdims=True)),
                (dg_ref, jnp.sum(dn * xhat, axis=0, keepdims=True))]
        if gated:
            dz_ref[...] = (dx * gv_ref[...]).astype(BF16)
            sums.append((dgv_ref, jnp.sum(dx * f_ref[...], axis=0, keepdims=True)))

        @pl.when(i == 0)
        def _():
            for ref, p in sums:
                ref[...] = p

        @pl.when(i > 0)
        def _():
            for ref, p in sums:
                ref[...] += p

    vec = jax.ShapeDtypeStruct((1, D), F32)
    in_specs = [_row_spec(), _vec_spec(), _vec_spec(), _row_spec(), _row_spec()]
    out_specs = [_row_spec(), _vec_spec(), _vec_spec(), _vec_spec()]
    out_shape = [jax.ShapeDtypeStruct((S, D), F32), vec, vec, vec]
    args = [x, g, sc, dh, dres]
    if after is not None:
        in_specs.append(HBM_SPEC)
        args.append(after)
    if gated:
        in_specs += [_row_spec(), _vec_spec()]
        out_specs += [_row_spec(), _vec_spec()]
        out_shape += [jax.ShapeDtypeStruct((S, D), BF16), vec]
        args += list(gate)
    return _pcall(body, name=name, grid=(S // TR,), in_specs=in_specs, out_specs=tuple(out_specs),
                  out_shape=tuple(out_shape), compiler_params=_params(("arbitrary",)))(*args)


def _w_o_norm2(merged, w_o, x, g1, g, sh, sc):
    def body(a_ref, b_ref, x_ref, g1_ref, g_ref, sh_ref, sc_ref, o_ref, x1_ref, h_ref):
        acc = _dot(a_ref[...], b_ref[...], NN)
        o_ref[...] = acc
        xv = x_ref[...] + g1_ref[...] * acc
        x1_ref[...] = xv
        rstd = lax.rsqrt(jnp.mean(xv * xv, axis=-1, keepdims=True) + RMS_EPS)
        h_ref[...] = (xv * rstd * g_ref[...] * (1.0 + sc_ref[...]) + sh_ref[...]).astype(BF16)

    rows = pl.BlockSpec((FF2_TM, D), lambda i: (i, 0))
    f32 = jax.ShapeDtypeStruct((S, D), F32)
    return _pcall(body, name="w_o_norm2", grid=(S // FF2_TM,),
                  in_specs=[rows, pl.BlockSpec((D, D), lambda i: (0, 0)), rows] + [_vec_spec()] * 4,
                  out_specs=(rows, rows, rows), out_shape=(f32, f32, jax.ShapeDtypeStruct((S, D), BF16)),
                  compiler_params=_params(("parallel",)))(merged, w_o, x, g1, g, sh, sc)


FF2_TM = 512


def _ff2_final(act, w_ff2, x1, g2, tgt, g):
    def body(a_ref, b_ref, x1_ref, g2_ref, t_ref, g_ref, loss_ref, dx_ref, dg_ref, df_ref, dg2_ref):
        i = pl.program_id(0)
        f = _dot(a_ref[...], b_ref[...], NN)
        g2v = g2_ref[...]
        xv = x1_ref[...] + g2v * f
        gv = g_ref[...]
        rstd = lax.rsqrt(jnp.mean(xv * xv, axis=-1, keepdims=True) + RMS_EPS)
        xhat = xv * rstd
        err = xhat * gv - t_ref[...]
        dy = err * (1.0 / D)
        dxhat = dy * gv
        dx = rstd * (dxhat - xhat * jnp.mean(dxhat * xhat, axis=-1, keepdims=True))
        dx_ref[...] = dx
        df_ref[...] = (dx * g2v).astype(BF16)
        p_g = jnp.sum(dy * xhat, axis=0, keepdims=True)
        p_g2 = jnp.sum(dx * f, axis=0, keepdims=True)
        p_l = jnp.zeros((1, 128), F32) + 0.5 * jnp.sum(jnp.mean(err * err, axis=-1, keepdims=True))

        @pl.when(i == 0)
        def _():
            dg_ref[...] = p_g
            dg2_ref[...] = p_g2
            loss_ref[...] = p_l

        @pl.when(i > 0)
        def _():
            dg_ref[...] += p_g
            dg2_ref[...] += p_g2
            loss_ref[...] += p_l

    vec = jax.ShapeDtypeStruct((1, D), F32)
    rows = lambda w: pl.BlockSpec((FF2_TM, w), lambda i: (i, 0))
    return _pcall(body, name="ff2_final", grid=(S // FF2_TM,),
                  in_specs=[rows(D_FF), pl.BlockSpec((D_FF, D), lambda i: (0, 0)), rows(D), _vec_spec(), rows(D),
                            _vec_spec()],
                  out_specs=(_vec_spec(128), rows(D), _vec_spec(), rows(D), _vec_spec()),
                  out_shape=(jax.ShapeDtypeStruct((1, 128), F32), jax.ShapeDtypeStruct((S, D), F32), vec,
                             jax.ShapeDtypeStruct((S, D), BF16), vec),
                  compiler_params=_params(("arbitrary",)))(act, w_ff2, x1, g2, tgt, g)


HALF = 512


MERGE_TM = 1024


def _merge_specs():
    blk = lambda off: pl.BlockSpec((MERGE_TM, HALF), lambda i, j: (i, off // HALF + j))
    return blk(OFF_GA), blk(OFF_GB), blk(0)


def _att_out_merge(att, w_att_out, proj, ret_out):
    def body(a_ref, b_ref, ga_ref, gb_ref, r_ref, o_ref, m_ref):
        acc = _dot(a_ref[...], b_ref[...], NN)
        o_ref[...] = acc
        m_ref[...] = (jax.nn.sigmoid(ga_ref[...]) * r_ref[...] + jax.nn.sigmoid(gb_ref[...]) * acc).astype(BF16)

    ga, gb, tile = _merge_specs()
    return _pcall(body, name="att_out", grid=(S // MERGE_TM, D // HALF),
                  in_specs=[pl.BlockSpec((MERGE_TM, AW), lambda i, j: (i, 0)), pl.BlockSpec((AW, HALF), lambda i, j: (0, j)),
                            ga, gb, tile],
                  out_specs=(tile, tile),
                  out_shape=(jax.ShapeDtypeStruct((S, D), F32), jax.ShapeDtypeStruct((S, D), BF16)),
                  compiler_params=_params(("parallel", "parallel")))(att, w_att_out, proj, proj, ret_out)


def _dmerged_split(dmixo, w_o, proj, ret_out, att_out):
    def body(a_ref, b_ref, ga_ref, gb_ref, r_ref, at_ref, dr_ref, da_ref, dga_ref, dgb_ref):
        dm = _dot(a_ref[...], b_ref[...], NT)
        sa = jax.nn.sigmoid(ga_ref[...])
        sb = jax.nn.sigmoid(gb_ref[...])
        dr_ref[...] = (dm * sa).astype(BF16)
        da_ref[...] = (dm * sb).astype(BF16)
        dga_ref[...] = (dm * r_ref[...] * (sa * (1.0 - sa))).astype(BF16)
        dgb_ref[...] = (dm * at_ref[...] * (sb * (1.0 - sb))).astype(BF16)

    ga, gb, tile = _merge_specs()
    o = jax.ShapeDtypeStruct((S, D), BF16)
    return _pcall(body, name="dmerged", grid=(S // MERGE_TM, D // HALF),
                  in_specs=[pl.BlockSpec((MERGE_TM, D), lambda i, j: (i, 0)), pl.BlockSpec((HALF, D), lambda i, j: (j, 0)),
                            ga, gb, tile, tile],
                  out_specs=(tile,) * 4, out_shape=(o, o, o, o),
                  compiler_params=_params(("parallel", "parallel")))(dmixo, w_o, proj, proj, ret_out, att_out)


def _ret_tables():
    H, C = RET_HEADS, CHUNK
    log_g = jnp.log1p(-(2.0 ** (-5.0 - jnp.arange(H, dtype=F32))))
    idx = jnp.arange(C, dtype=F32)
    rel = idx[:, None] - idx[None, :]
    inner = jnp.where(rel >= 0, jnp.exp(log_g[:, None, None] * jnp.maximum(rel, 0.0)), 0.0)
    qd = jnp.exp(log_g[:, None] * (idx + 1.0))[:, :, None]
    kd = jnp.exp(log_g[:, None] * (C - 1.0 - idx))[:, :, None]
    cd = jnp.broadcast_to(jnp.exp(log_g * C)[:, None, None], (H, 1, 128))
    half = RET_DK // 2
    inv = 10000.0 ** (-jnp.arange(half, dtype=F32) / half)
    ang = jnp.arange(S, dtype=F32)[:, None] * inv[None, :]
    return inner, qd, kd, cd, jnp.cos(ang), jnp.sin(ang)


def _rot(x, cos, sin):
    x1, x2 = x[:, :128], x[:, 128:]
    return jnp.concatenate([x1 * cos - x2 * sin, x1 * sin + x2 * cos], axis=1)


def _rot_t(d, cos, sin):
    d1, d2 = d[:, :128], d[:, 128:]
    return jnp.concatenate([d1 * cos + d2 * sin, d2 * cos - d1 * sin], axis=1)


RET_COLS = OFF_ATT
RET_VW = RET_HEADS * RET_DV


def _ret_specs(chunk_of):
    ci = chunk_of
    whole = lambda shape: pl.BlockSpec(shape, lambda t: (0,) * len(shape))
    return [
        pl.BlockSpec((CHUNK, RET_COLS), lambda t: (ci(t), 0)),
        pl.BlockSpec((CHUNK, 128), lambda t: (ci(t), 0)),
        pl.BlockSpec((CHUNK, 128), lambda t: (ci(t), 0)),
        whole((RET_HEADS, CHUNK, CHUNK)), whole((RET_HEADS, CHUNK, 1)), whole((RET_HEADS, CHUNK, 1)),
        whole((RET_HEADS, 1, 128)), whole((1, RET_VW)), whole((1, RET_VW)),
    ]


def _ret_cols(h):
    q = slice(OFF_RQ + h * RET_DK, OFF_RQ + (h + 1) * RET_DK)
    k = slice(OFF_RK + h * RET_DK, OFF_RK + (h + 1) * RET_DK)
    v = slice(OFF_RV + h * RET_DV, OFF_RV + (h + 1) * RET_DV)
    g = slice(OFF_RG + h * RET_DV, OFF_RG + (h + 1) * RET_DV)
    return q, k, v, g, slice(h * RET_DV, (h + 1) * RET_DV)


def _ret_fwd(proj, tables, gn_g, gn_b, after):
    inner, qd, kd, cd, cos, sin = tables

    def body(x_ref, cos_ref, sin_ref, in_ref, qd_ref, kd_ref, cd_ref, g_ref, b_ref, after_ref,
             gated_ref, ro_ref, st_ref, s_scr):
        i = pl.program_id(0)

        @pl.when(i == 0)
        def _():
            s_scr[...] = jnp.zeros_like(s_scr)

        cosv, sinv = cos_ref[...], sin_ref[...]
        for h in range(RET_HEADS):
            cq, ck, cv, cg, co = _ret_cols(h)
            q = _rot(x_ref[:, cq], cosv, sinv)
            k = _rot(x_ref[:, ck], cosv, sinv) * (RET_DK ** -0.5)
            v = x_ref[:, cv]
            st = s_scr[h]
            st_ref[h] = st.astype(BF16)
            s = _dot(q, k, NT) * in_ref[h]
            o = _dot(s, v, NN) + _dot(q, st, NN) * qd_ref[h]
            s_scr[h] = st * cd_ref[h, :, :1] + _dot(k * kd_ref[h], v, TN)
            ro_ref[:, co] = o
            mu = jnp.mean(o, axis=-1, keepdims=True)
            oc = o - mu
            var = jnp.mean(oc * oc, axis=-1, keepdims=True)
            rn = oc * lax.rsqrt(var + GN_EPS) * g_ref[:, co] + b_ref[:, co]
            rg = x_ref[:, cg]
            gated_ref[:, co] = (rg * jax.nn.sigmoid(rg) * rn).astype(BF16)

    ospec = pl.BlockSpec((CHUNK, RET_VW), lambda t: (t, 0))
    return _pcall(
        body, name="ret_fwd", grid=(N_CHUNK,), in_specs=_ret_specs(lambda t: t) + [HBM_SPEC],
        out_specs=(ospec, ospec, pl.BlockSpec((RET_HEADS, None, RET_DK, RET_DV), lambda t: (0, t, 0, 0))),
        out_shape=(jax.ShapeDtypeStruct((S, RET_VW), BF16), jax.ShapeDtypeStruct((S, RET_VW), F32),
                   jax.ShapeDtypeStruct((RET_HEADS, N_CHUNK, RET_DK, RET_DV), BF16)),
        scratch_shapes=[pltpu.VMEM((RET_HEADS, RET_DK, RET_DV), F32)],
        compiler_params=_params(("arbitrary",)))(proj, cos, sin, inner, qd, kd, cd, gn_g, gn_b, after)


def _ret_bwd(proj, tables, gn_g, gn_b, ro, states, dgated, others):
    inner, qd, kd, cd, cos, sin = tables
    last = N_CHUNK - 1
    pieces = lambda o: [(h, o.shape[2]) for h in range(o.shape[0])] if len(o.shape) == 3 else [(None, o.shape[1])]
    assert RET_COLS + sum(w for o in others for _, w in pieces(o)) == IN_COLS

    def body(x_ref, cos_ref, sin_ref, in_ref, qd_ref, kd_ref, cd_ref, g_ref, b_ref, ro_ref, st_ref, dg_ref, *rest):
        other_refs, (dx_ref, gg_ref, gb_ref, gs_scr) = rest[:len(others)], rest[len(others):]
        t = pl.program_id(0)
        col = RET_COLS
        for o_ref in other_refs:
            for h, w in pieces(o_ref):
                dx_ref[:, col:col + w] = (o_ref[...] if h is None else o_ref[h]).astype(BF16)
                col += w

        @pl.when(t == 0)
        def _():
            gs_scr[...] = jnp.zeros_like(gs_scr)
            gg_ref[...] = jnp.zeros_like(gg_ref)
            gb_ref[...] = jnp.zeros_like(gb_ref)

        cosv, sinv = cos_ref[...], sin_ref[...]
        for h in range(RET_HEADS):
            cq, ck, cv, cg, co = _ret_cols(h)
            q = _rot(x_ref[:, cq], cosv, sinv)
            k = _rot(x_ref[:, ck], cosv, sinv) * (RET_DK ** -0.5)
            v = x_ref[:, cv]
            qdv, kdv, dm = qd_ref[h], kd_ref[h], in_ref[h]
            st = st_ref[h]
            o = ro_ref[:, co]
            gv = g_ref[:, co]
            mu = jnp.mean(o, axis=-1, keepdims=True)
            oc = o - mu
            rstd = lax.rsqrt(jnp.mean(oc * oc, axis=-1, keepdims=True) + GN_EPS)
            ohat = oc * rstd
            rn = ohat * gv + b_ref[:, co]
            rg = x_ref[:, cg]
            sg = jax.nn.sigmoid(rg)
            dgt = dg_ref[:, co]
            drn = dgt * (rg * sg)
            dx_ref[:, cg] = (dgt * rn * (sg * (1.0 + rg * (1.0 - sg)))).astype(BF16)
            gg_ref[:, co] += jnp.sum(drn * ohat, axis=0, keepdims=True)
            gb_ref[:, co] += jnp.sum(drn, axis=0, keepdims=True)
            dohat = drn * gv
            do = rstd * (dohat - jnp.mean(dohat, axis=-1, keepdims=True)
                         - ohat * jnp.mean(dohat * ohat, axis=-1, keepdims=True))
            gs = gs_scr[h]
            s = _dot(q, k, NT) * dm
            dsr = _dot(do, v, NT) * dm
            dq = _dot(dsr, k, NN) + _dot(do, st, NT) * qdv
            dk = _dot(dsr, q, TN) + _dot(v, gs, NT) * kdv
            dv = _dot(s, do, TN) + _dot(k * kdv, gs, NN)
            gs_scr[h] = gs * cd_ref[h, :, :1] + _dot(q * qdv, do, TN)
            dx_ref[:, cq] = _rot_t(dq, cosv, sinv).astype(BF16)
            dx_ref[:, ck] = (_rot_t(dk, cosv, sinv) * (RET_DK ** -0.5)).astype(BF16)
            dx_ref[:, cv] = dv.astype(BF16)

    rev = lambda t: last - t
    vblk = pl.BlockSpec((CHUNK, RET_VW), lambda t: (rev(t), 0))
    vspec = pl.BlockSpec((1, RET_VW), lambda t: (0, 0))
    rows = lambda w: pl.BlockSpec((CHUNK, w), lambda t: (rev(t), 0))
    return _pcall(
        body, name="ret_bwd", grid=(N_CHUNK,),
        in_specs=_ret_specs(rev) + [vblk, pl.BlockSpec((RET_HEADS, None, RET_DK, RET_DV), lambda t: (0, rev(t), 0, 0)),
                                    vblk] + [rows(o.shape[1]) if o.ndim == 2 else
                                             pl.BlockSpec((o.shape[0], CHUNK, o.shape[2]), lambda t: (0, rev(t), 0))
                                             for o in others],
        out_specs=(rows(IN_COLS), vspec, vspec),
        out_shape=(jax.ShapeDtypeStruct((S, IN_COLS), BF16), jax.ShapeDtypeStruct((1, RET_VW), F32),
                   jax.ShapeDtypeStruct((1, RET_VW), F32)),
        scratch_shapes=[pltpu.VMEM((RET_HEADS, RET_DK, RET_DV), F32)],
        compiler_params=_params(("arbitrary",)))(proj, cos, sin, inner, qd, kd, cd, gn_g, gn_b, ro, states, dgated,
                                                 *others)


def _bucket_tables():
    qi = np.arange(ATT_BLK)[:, None]
    kj = np.arange(2 * ATT_BLK)[None, :]
    m = ATT_BLK + qi - kj
    out = []
    for win, dil in ATT_GROUPS:
        w = win // dil
        dist = (np.clip(m, 0, w) * dil).astype(np.int32)
        max_exact = N_BUCKETS // 2
        d_f = np.maximum(dist, 1).astype(np.float32)
        large = max_exact + (np.log(d_f / np.float32(max_exact)) / np.float32(math.log(MAX_DIST / max_exact))
                             * np.float32(N_BUCKETS - max_exact)).astype(np.int32)
        large = np.minimum(large, N_BUCKETS - 1)
        out.append(np.where(dist < max_exact, dist, large).astype(np.int32))
    return np.stack(out)


def _bias_build(rel_bias, buckets, after):
    def body(tab_ref, bk_ref, after_ref, o_ref):
        hh = pl.program_id(0)
        bk = bk_ref[...]
        acc = jnp.zeros((ATT_BLK, 2 * ATT_BLK), F32)
        for b in range(N_BUCKETS):
            acc = jnp.where(bk == b, tab_ref[b, hh], acc)
        o_ref[...] = acc

    nh = len(ATT_GROUPS) * ATT_HG
    return _pcall(body, name="bias_build", grid=(nh,),
                  in_specs=[pl.BlockSpec(memory_space=pltpu.SMEM),
                            pl.BlockSpec((None, ATT_BLK, 2 * ATT_BLK), lambda hh: (hh // ATT_HG, 0, 0)), HBM_SPEC],
                  out_specs=pl.BlockSpec((None, ATT_BLK, 2 * ATT_BLK), lambda hh: (hh, 0, 0)),
                  out_shape=jax.ShapeDtypeStruct((nh, ATT_BLK, 2 * ATT_BLK), F32),
                  compiler_params=_params(("parallel",)))(rel_bias, buckets, after)


def _bias_grad(ds_sum, buckets):
    def body(ds_ref, bk_ref, o_ref):
        bk = bk_ref[...]
        ds = ds_ref[...]
        rows = lax.broadcasted_iota(jnp.int32, (N_BUCKETS, 128), 0)
        acc = jnp.zeros((N_BUCKETS, 128), F32)
        for b in range(N_BUCKETS):
            acc = jnp.where(rows == b, jnp.sum(jnp.where(bk == b, ds, 0.0)), acc)
        o_ref[...] = acc

    nh = len(ATT_GROUPS) * ATT_HG
    return _pcall(body, name="bias_grad", grid=(nh,),
                  in_specs=[pl.BlockSpec((None, ATT_BLK, 2 * ATT_BLK), lambda hh: (hh, 0, 0)),
                            pl.BlockSpec((None, ATT_BLK, 2 * ATT_BLK), lambda hh: (hh // ATT_HG, 0, 0))],
                  out_specs=pl.BlockSpec((None, N_BUCKETS, 128), lambda hh: (hh, 0, 0)),
                  out_shape=jax.ShapeDtypeStruct((nh, N_BUCKETS, 128), F32),
                  compiler_params=_params(("parallel",)))(ds_sum, buckets)


def _att_valid(n):
    qi = lax.broadcasted_iota(jnp.int32, (ATT_BLK, 2 * ATT_BLK), 0)
    kj = lax.broadcasted_iota(jnp.int32, (ATT_BLK, 2 * ATT_BLK), 1)
    m = ATT_BLK + qi - kj
    first_key = jnp.where(n > 0, 0, ATT_BLK)
    return (m >= 0) & (m <= ATT_BLK) & (kj >= first_key)


ATT_HP = (1, 2, 2)


def _att_geometry(gi):
    _, dil = ATT_GROUPS[gi]
    return dil, S // dil // ATT_BLK, ATT_HP[gi]


def _blk(dil, r, n):
    if dil == 1:
        return pl.ds(n * ATT_BLK, ATT_BLK)
    return pl.ds(r + n * ATT_BLK * dil, ATT_BLK, stride=dil)


def _slab_specs(gi):
    _, _, hp = _att_geometry(gi)
    per = ATT_HG // hp
    return [pl.BlockSpec((hp, S, ATT_DH), lambda g, r, part=part: ((3 * gi + part) * per + g, 0, 0))
            for part in range(3)]


def _head_specs(gi, count):
    _, _, hp = _att_geometry(gi)
    return [pl.BlockSpec((hp, S, ATT_DH), lambda g, r: (g, 0, 0))] * count


def _bias_spec(gi):
    _, _, hp = _att_geometry(gi)
    return pl.BlockSpec((hp, ATT_BLK, 2 * ATT_BLK), lambda g, r: (gi * (ATT_HG // hp) + g, 0, 0))


def _att_valid_first():
    qi = lax.broadcasted_iota(jnp.int32, (ATT_BLK, ATT_BLK), 0)
    kj = lax.broadcasted_iota(jnp.int32, (ATT_BLK, ATT_BLK), 1)
    return kj <= qi


def _att_fwd(slabs, bias, gi, comm=None):
    dil, nb, hp = _att_geometry(gi)
    scale = ATT_DH ** -0.5

    def body(q_ref, k_ref, v_ref, bias_ref, o_ref, l_ref):
        r = pl.program_id(1)
        for n in range(nb):
            cur = _blk(dil, r, n)
            valid = _att_valid(n) if n > 0 else _att_valid_first()
            for h in range(hp):
                if n > 0:
                    prev = _blk(dil, r, n - 1)
                    kk = jnp.concatenate([k_ref[h, prev, :], k_ref[h, cur, :]], axis=0)
                    vv = jnp.concatenate([v_ref[h, prev, :], v_ref[h, cur, :]], axis=0)
                    bias = bias_ref[h]
                else:
                    kk, vv, bias = k_ref[h, cur, :], v_ref[h, cur, :], bias_ref[h, :, pl.ds(ATT_BLK, ATT_BLK)]
                s = _dot(q_ref[h, cur, :], kk, NT) * scale + bias
                s = jnp.where(valid, s, -1e30)
                mx = jnp.max(s, axis=-1, keepdims=True)
                e = jnp.exp(s - mx)
                den = jnp.sum(e, axis=-1, keepdims=True)
                o_ref[h, cur, :] = _dot(e / den, vv, NN)
                l_ref[h, cur, :] = jnp.broadcast_to(mx + jnp.log(den), (ATT_BLK, ATT_DH))

    osh = jax.ShapeDtypeStruct((ATT_HG, S, ATT_DH), F32)
    kw = dict(name=f"att_fwd{gi}", grid=(ATT_HG // hp, dil), in_specs=_slab_specs(gi) + [_bias_spec(gi)],
              out_specs=tuple(_head_specs(gi, 2)), out_shape=(osh, osh))
    if comm is not None:
        return _carry(body, comm, **kw)(slabs, slabs, slabs, bias)
    return _pcall(body, compiler_params=_params(("parallel", "arbitrary")), **kw)(slabs, slabs, slabs, bias)


def _att_bwd(slabs, bias, o, lse, do, dlse, gi, comm=None):
    dil, nb, hp = _att_geometry(gi)
    per = ATT_HG // hp
    scale = ATT_DH ** -0.5
    wide = lambda t: jnp.concatenate([t, t], axis=1)

    def body(q_ref, k_ref, v_ref, bias_ref, o_ref, l_ref, do_ref, dl_ref, dq_ref, dk_ref, dv_ref, ds_ref):
        r = pl.program_id(1)

        @pl.when(r == 0)
        def _():
            ds_ref[...] = jnp.zeros_like(ds_ref)

        for h in range(hp):
            carry_k = carry_v = None
            for n in range(nb):
                cur = _blk(dil, r, n)
                q = q_ref[h, cur, :]
                dov = do_ref[h, cur, :]
                delta = jnp.sum(dov * o_ref[h, cur, :], axis=-1, keepdims=True)
                if n == 0:
                    own = pl.ds(ATT_BLK, ATT_BLK)
                    kk, vv = k_ref[h, cur, :], v_ref[h, cur, :]
                    s = _dot(q, kk, NT) * scale + bias_ref[h, :, own]
                    p = jnp.where(_att_valid_first(), jnp.exp(s - l_ref[h, cur, :]), 0.0)
                    ds = p * (_dot(dov, vv, NT) - delta + dl_ref[h, cur, :])
                    ds_ref[h, :, own] += ds
                    dq_ref[h, cur, :] = _dot(ds, kk, NN) * scale
                    carry_k, carry_v = _dot(ds, q, TN) * scale, _dot(p, dov, TN)
                    continue
                prev = _blk(dil, r, n - 1)
                kk = jnp.concatenate([k_ref[h, prev, :], k_ref[h, cur, :]], axis=0)
                vv = jnp.concatenate([v_ref[h, prev, :], v_ref[h, cur, :]], axis=0)
                s = _dot(q, kk, NT) * scale + bias_ref[h]
                p = jnp.where(_att_valid(n), jnp.exp(s - wide(l_ref[h, cur, :])), 0.0)
                dp = _dot(dov, vv, NT)
                ds = p * (dp - delta + wide(dl_ref[h, cur, :]))
                ds_ref[h] += ds
                dq_ref[h, cur, :] = _dot(ds, kk, NN) * scale
                dkk = _dot(ds, q, TN) * scale
                dvv = _dot(p, dov, TN)
                dk_ref[h, prev, :] = carry_k + dkk[:ATT_BLK]
                dv_ref[h, prev, :] = carry_v + dvv[:ATT_BLK]
                carry_k, carry_v = dkk[ATT_BLK:], dvv[ATT_BLK:]
            last = _blk(dil, r, nb - 1)
            dk_ref[h, last, :] = carry_k
            dv_ref[h, last, :] = carry_v

    osh = jax.ShapeDtypeStruct((ATT_HG, S, ATT_DH), F32)
    kw = dict(name=f"att_bwd{gi}", grid=(per, dil), in_specs=_slab_specs(gi) + [_bias_spec(gi)] + _head_specs(gi, 4),
              out_specs=(*_head_specs(gi, 3), pl.BlockSpec((hp, ATT_BLK, 2 * ATT_BLK), lambda g, r: (g, 0, 0))),
              out_shape=(osh, osh, osh, jax.ShapeDtypeStruct((ATT_HG, ATT_BLK, 2 * ATT_BLK), F32)))
    args = (slabs, slabs, slabs, bias, o, lse, do, dlse)
    if comm is not None:
        return _carry(body, comm, **kw)(*args)
    return _pcall(body, compiler_params=_params(("arbitrary", "arbitrary")), **kw)(*args)


AW = ATT_HG * ATT_DH


def _mix_weights(l0, l1, l2):
    mx = jnp.maximum(jnp.maximum(l0, l1), l2)
    e0, e1, e2 = jnp.exp(l0 - mx), jnp.exp(l1 - mx), jnp.exp(l2 - mx)
    den = e0 + e1 + e2
    return e0 / den, e1 / den, e2 / den


def _heads_spec():
    return pl.BlockSpec((ATT_HG, TR, ATT_DH), lambda i: (0, i, 0))


def _mix_fwd(os_, ls, comm=None):
    def body(o0, o1, o2, l0, l1, l2, att_ref):
        for h in range(ATT_HG):
            w0, w1, w2 = _mix_weights(l0[h], l1[h], l2[h])
            att_ref[:, h * ATT_DH:(h + 1) * ATT_DH] = (w0 * o0[h] + w1 * o1[h] + w2 * o2[h]).astype(BF16)

    kw = dict(name="mix_fwd", grid=(S // TR,), in_specs=[_heads_spec()] * 6, out_specs=_row_spec(AW),
              out_shape=jax.ShapeDtypeStruct((S, AW), BF16))
    if comm is not None:
        return _carry(body, comm, **kw)(*os_, *ls)
    return _pcall(body, compiler_params=_params(("parallel",)), **kw)(*os_, *ls)


def _mix_bwd(os_, ls, datt):
    def body(o0, o1, o2, l0, l1, l2, da_ref, d0, d1, d2, e0, e1, e2):
        for h in range(ATT_HG):
            ws = _mix_weights(l0[h], l1[h], l2[h])
            da = da_ref[:, h * ATT_DH:(h + 1) * ATT_DH]
            dws = []
            for o_ref, w, d_ref in zip((o0, o1, o2), ws, (d0, d1, d2)):
                d_ref[h] = w * da
                dws.append(jnp.broadcast_to(jnp.sum(da * o_ref[h], axis=-1, keepdims=True), (TR, ATT_DH)))
            tot = ws[0] * dws[0] + ws[1] * dws[1] + ws[2] * dws[2]
            for w, dw, e_ref in zip(ws, dws, (e0, e1, e2)):
                e_ref[h] = w * (dw - tot)

    o = jax.ShapeDtypeStruct((ATT_HG, S, ATT_DH), F32)
    return _pcall(body, name="mix_bwd", grid=(S // TR,), in_specs=[_heads_spec()] * 6 + [_row_spec(AW)],
                  out_specs=(_heads_spec(),) * 6, out_shape=(o,) * 6,
                  compiler_params=_params(("parallel",)))(*os_, *ls, datt)


def _ada_fwd(c_all, w_sh, b_sl):
    def body(c_ref, w_ref, b_ref, o_ref):
        cv = c_ref[...]
        o_ref[...] = _dot(cv * jax.nn.sigmoid(cv), w_ref[...], NN) + b_ref[...]

    return _pcall(body, name="ada_fwd", out_shape=jax.ShapeDtypeStruct((N_DEV, w_sh.shape[1]), F32),
                  compiler_params=_params())(c_all, w_sh, b_sl)


CAST_STEPS = 4


def _to_bf16(arrs, comm, name):
    n = len(arrs)

    def body(*refs):
        for src, dst in zip(refs[:n], refs[n:]):
            dst[...] = src[...].astype(BF16)

    blocks = [pl.BlockSpec((a.shape[0] // CAST_STEPS, a.shape[1]), lambda i: (i, 0)) for a in arrs]
    return _carry(body, comm, name=name, grid=(CAST_STEPS,), in_specs=blocks, out_specs=tuple(blocks),
                  out_shape=tuple(jax.ShapeDtypeStruct(a.shape, BF16) for a in arrs))(*arrs)


def _ada_bwd(c_all, dm_sl):
    def body(c_ref, d_ref, o_ref):
        cv = c_ref[...]
        o_ref[...] = _dot(cv * jax.nn.sigmoid(cv), d_ref[...], TN)

    return _pcall(body, name="ada_bwd", out_shape=jax.ShapeDtypeStruct((D, dm_sl.shape[1]), F32),
                  compiler_params=_params())(c_all, dm_sl)


N_MOD = 6


def _sum_small(gathered):
    n = len(gathered)

    def body(*refs):
        ins, (gb_ref, dm_ref), outs = refs[:n], refs[n:n + 2], refs[n + 2:]

        def total(r):
            acc = r[0]
            for e in range(1, N_DEV):
                acc = acc + r[e]
            return acc

        for i in range(N_MOD):
            cols = slice(i * D, (i + 1) * D)
            gb_ref[:, cols] = total(ins[i])
            for e in range(N_DEV):
                dm_ref[e:e + 1, cols] = ins[i][e]
        for r, o_ref in zip(ins[N_MOD:], outs):
            o_ref[...] = total(r)

    shapes = (jax.ShapeDtypeStruct((1, N_MOD * D), F32), jax.ShapeDtypeStruct((N_DEV, N_MOD * D), F32),
              *[jax.ShapeDtypeStruct(g.shape[1:], F32) for g in gathered[N_MOD:]])
    res = _pcall(body, name="sum_small", out_shape=shapes, compiler_params=_params())(*gathered)
    return res[0], res[1], res[2:]


def _row_tile(m, n):
    t = max(8, min(m, (1 << 19) // n // 8 * 8))
    while m % t:
        t -= 8
    return t


def _pair_sum(full, recv, sel, name, col_block=0):
    _, m, n = recv.shape
    t = _row_tile(m, n)

    def body(sel_ref, a_ref, b_ref, o_ref):
        o_ref[...] = (a_ref[...].astype(F32) + b_ref[...].astype(F32)).astype(o_ref.dtype)

    gs = pltpu.PrefetchScalarGridSpec(
        num_scalar_prefetch=1, grid=(4, m // t),
        in_specs=[pl.BlockSpec((None, None, t, n), lambda q, i, s: (q, s[0], i, col_block)),
                  pl.BlockSpec((None, t, n), lambda q, i, s: (q, i, 0))],
        out_specs=pl.BlockSpec((None, t, n), lambda q, i, s: (q, i, 0)))
    return _pcall(body, name=name, grid_spec=gs, out_shape=jax.ShapeDtypeStruct((4, m, n), full.dtype),
                  compiler_params=_params(("parallel", "parallel")))(sel, full, recv)


def _chip_sum(part, recv, sel, name):
    _, m, n = part.shape
    t = _row_tile(m, n)

    def body(sel_ref, a_ref, r_ref, o_ref):
        o_ref[...] = ((a_ref[...].astype(F32) + r_ref[0].astype(F32)) + r_ref[1].astype(F32)) + r_ref[2].astype(F32)

    gs = pltpu.PrefetchScalarGridSpec(
        num_scalar_prefetch=1, grid=(m // t,),
        in_specs=[pl.BlockSpec((None, t, n), lambda i, s: (s[0], i, 0)),
                  pl.BlockSpec((3, t, n), lambda i, s: (0, i, 0))],
        out_specs=pl.BlockSpec((t, n), lambda i, s: (i, 0)))
    return _pcall(body, name=name, grid_spec=gs, out_shape=jax.ShapeDtypeStruct((m, n), F32),
                  compiler_params=_params(("parallel",)))(sel, part, recv)


def _adamw_math(w, g, m, v):
    nm = ADAM_B1 * m + (1.0 - ADAM_B1) * g
    nv = ADAM_B2 * v + (1.0 - ADAM_B2) * (g * g)
    m_hat = nm / (1.0 - ADAM_B1 ** ADAM_STEP)
    v_hat = nv / (1.0 - ADAM_B2 ** ADAM_STEP)
    return -ADAM_LR * (m_hat / (jnp.sqrt(v_hat) + ADAM_EPS) + ADAM_WD * w), nm, nv


def _adamw(w, g, m, v, name):
    _, rows, cols = w.shape
    t = _row_tile(rows, cols)

    def body(w_ref, g_ref, m_ref, v_ref, d_ref, nm_ref, nv_ref):
        d_ref[...], nm_ref[...], nv_ref[...] = _adamw_math(w_ref[...], g_ref[...], m_ref[...], v_ref[...])

    spec3 = pl.BlockSpec((None, t, cols), lambda i: (0, i, 0))
    spec2 = pl.BlockSpec((t, cols), lambda i: (i, 0))
    o = jax.ShapeDtypeStruct(w.shape, F32)
    return _pcall(body, name=name, grid=(rows // t,), in_specs=[spec3, spec2, spec3, spec3], out_specs=(spec3,) * 3,
                  out_shape=(o, o, o), compiler_params=_params(("parallel",)))(w, g, m, v)


def _adamw_reduced1(w, m, v, part, recv, sel, name):
    _, rows, cols = w.shape
    t = _row_tile(rows, cols)

    def body(sel_ref, w_ref, m_ref, v_ref, p_ref, r_ref, g_ref, d_ref, nm_ref, nv_ref):
        g = ((p_ref[...].astype(F32) + r_ref[0].astype(F32)) + r_ref[1].astype(F32)) + r_ref[2].astype(F32)
        g_ref[...] = g
        d_ref[...], nm_ref[...], nv_ref[...] = _adamw_math(w_ref[...], g, m_ref[...], v_ref[...])

    wspec = pl.BlockSpec((None, t, cols), lambda i, s: (0, i, 0))
    gs = pltpu.PrefetchScalarGridSpec(
        num_scalar_prefetch=1, grid=(rows // t,),
        in_specs=[wspec, wspec, wspec, pl.BlockSpec((None, t, cols), lambda i, s: (s[0], i, 0)),
                  pl.BlockSpec((3, t, cols), lambda i, s: (0, i, 0))],
        out_specs=(wspec,) * 4)
    o = jax.ShapeDtypeStruct(w.shape, F32)
    return _pcall(body, name=name, grid_spec=gs, out_shape=(o, o, o, o),
                  compiler_params=_params(("parallel",)))(sel, w, m, v, part, recv)


def _adamw_reduced(w, m, v, parts, recvs, sel):
    _, rows, cols = w.shape
    half = cols // 2
    t = _row_tile(rows, half)

    def body(sel_ref, w_ref, m_ref, v_ref, pa_ref, pb_ref, ra_ref, rb_ref, g_ref, d_ref, nm_ref, nv_ref):
        total = lambda p_ref, r_ref: ((p_ref[...].astype(F32) + r_ref[0].astype(F32)) + r_ref[1].astype(F32)) \
            + r_ref[2].astype(F32)
        g = jnp.where(pl.program_id(1) == 0, total(pa_ref, ra_ref), total(pb_ref, rb_ref))
        g_ref[...] = g
        d_ref[...], nm_ref[...], nv_ref[...] = _adamw_math(w_ref[...], g, m_ref[...], v_ref[...])

    wspec = pl.BlockSpec((None, t, half), lambda i, j, s: (0, i, j))
    pspec = pl.BlockSpec((None, t, half), lambda i, j, s: (s[0], i, 0))
    rspec = pl.BlockSpec((3, t, half), lambda i, j, s: (0, i, 0))
    gs = pltpu.PrefetchScalarGridSpec(num_scalar_prefetch=1, grid=(rows // t, 2),
                                      in_specs=[wspec, wspec, wspec, pspec, pspec, rspec, rspec],
                                      out_specs=(wspec,) * 4)
    o = jax.ShapeDtypeStruct(w.shape, F32)
    return _pcall(body, name="adamw_w_in", grid_spec=gs, out_shape=(o, o, o, o),
                  compiler_params=_params(("parallel", "arbitrary")))(sel, w, m, v, *parts, *recvs)


def _adamw_small(ws, gs, ms, vs):
    n = len(ws)

    def body(*refs):
        for i in range(n):
            w_ref, g_ref, m_ref, v_ref = (refs[k * n + i] for k in range(4))
            d, nm, nv = _adamw_math(w_ref[...], g_ref[...], m_ref[...], v_ref[...])
            refs[4 * n + i][...] = d
            refs[5 * n + i][...] = nm
            refs[6 * n + i][...] = nv

    shapes = tuple(jax.ShapeDtypeStruct(w.shape, F32) for w in ws)
    res = _pcall(body, name="adamw_small", out_shape=shapes * 3, compiler_params=_params())(*ws, *gs, *ms, *vs)
    return res[:n], res[n:2 * n], res[2 * n:]


def _mesh_pos():
    return lax.axis_index("x"), lax.axis_index("y"), lax.axis_index("c")


class _Gather:
    def __init__(self, arrs):
        self.ins = list(arrs)
        self.out_shape = tuple(jax.ShapeDtypeStruct((N_DEV,) + a.shape, a.dtype) for a in arrs)
        n = len(arrs)
        self.sems = [pltpu.SemaphoreType.DMA((7 * n,)), pltpu.SemaphoreType.DMA((7 * n,)),
                     pltpu.SemaphoreType.DMA((n,))]

    def _copies(self, ins, outs, sems):
        send_sems, recv_sems, local_sems = sems
        x, y, c = _mesh_pos()
        me, sibling = (x, y, c), (x, y, 1 - c)
        chips = [(1 - x, y), (x, 1 - y), (1 - x, 1 - y)]

        def copy(p, k, block, to, from_input=False):
            dst = outs[p].at[_slot(block)]
            return pltpu.make_async_remote_copy(
                src_ref=ins[p] if from_input else dst, dst_ref=dst, send_sem=send_sems.at[7 * p + k],
                recv_sem=recv_sems.at[7 * p + k], device_id=to, device_id_type=MESH)

        npc = len(self.ins)
        mine = [pltpu.make_async_copy(ins[p], outs[p].at[_slot(me)], local_sems.at[p]) for p in range(npc)]
        first = []
        for p in range(npc):
            first.append(copy(p, 0, me, sibling, from_input=True))
            first += [copy(p, 1 + j, me, (*chip, c), from_input=True) for j, chip in enumerate(chips)]
        return me, sibling, chips, c, copy, mine, first

    def start(self, ins, outs, sems):
        *_, mine, first = self._copies(ins, outs, sems)
        for cp in mine + first:
            cp.start()

    def finish(self, ins, outs, sems):
        me, sibling, chips, c, copy, mine, first = self._copies(ins, outs, sems)
        npc = len(self.ins)
        passed = []
        for p in range(npc):
            for j, chip in enumerate(chips):
                copy(p, 1 + j, (*chip, c), me).wait_recv()
                passed.append(copy(p, 4 + j, (*chip, c), sibling))
                passed[-1].start()
        for p in range(npc):
            copy(p, 0, sibling, me).wait_recv()
            for j, chip in enumerate(chips):
                copy(p, 4 + j, (*chip, 1 - c), me).wait_recv()
        for cp in first + passed:
            cp.wait_send()
        for cp in mine:
            cp.wait()


class _ExchangeCore:
    def __init__(self, fulls, cols=None):
        self.ins = list(fulls)
        self.cols = cols
        width = lambda f: f.shape[3] if cols is None else cols[1]
        self.out_shape = tuple(jax.ShapeDtypeStruct((4, f.shape[2], width(f)), f.dtype) for f in fulls)
        self.sems = [pltpu.SemaphoreType.DMA((4 * len(fulls),)), pltpu.SemaphoreType.DMA((4 * len(fulls),))]

    def _copies(self, ins, outs, sems):
        send_sems, recv_sems = sems
        x, y, c = _mesh_pos()

        def src(a, q):
            ref = ins[a].at[q, 1 - c]
            return ref if self.cols is None else ref.at[:, pl.ds(*self.cols)]

        return [pltpu.make_async_remote_copy(
            src_ref=src(a, q), dst_ref=outs[a].at[q], send_sem=send_sems.at[4 * a + q],
            recv_sem=recv_sems.at[4 * a + q], device_id=(x, y, 1 - c), device_id_type=MESH)
            for a in range(len(self.ins)) for q in range(4)]

    def start(self, ins, outs, sems):
        for cp in self._copies(ins, outs, sems):
            cp.start()

    def finish(self, ins, outs, sems):
        for cp in self._copies(ins, outs, sems):
            cp.wait()


class _ExchangeChip:
    def __init__(self, parts):
        self.ins = list(parts)
        self.out_shape = tuple(jax.ShapeDtypeStruct((3,) + p.shape[1:], p.dtype) for p in parts)
        self.sems = [pltpu.SemaphoreType.DMA((3 * len(parts),)), pltpu.SemaphoreType.DMA((3 * len(parts),))]

    def _copies(self, ins, outs, sems):
        send_sems, recv_sems = sems
        x, y, c = _mesh_pos()
        chips = [(1 - x, y), (x, 1 - y), (1 - x, 1 - y)]
        return [pltpu.make_async_remote_copy(
            src_ref=ins[a].at[2 * px + py], dst_ref=outs[a].at[j], send_sem=send_sems.at[3 * a + j],
            recv_sem=recv_sems.at[3 * a + j], device_id=(px, py, c), device_id_type=MESH)
            for a in range(len(self.ins)) for j, (px, py) in enumerate(chips)]

    def start(self, ins, outs, sems):
        for cp in self._copies(ins, outs, sems):
            cp.start()

    def finish(self, ins, outs, sems):
        for cp in self._copies(ins, outs, sems):
            cp.wait()


HBM_ONLY = pl.BlockSpec(memory_space=pltpu.HBM)
SEM_SPEC = pl.BlockSpec(memory_space=pltpu.SEMAPHORE)
SIDE_EFFECT = pltpu.SideEffectType.DATAFLOW_SIDE_EFFECTING


def _chip_copies(p_refs, land_refs, send_sems, recv_sems):
    x, y, c = _mesh_pos()
    return [pltpu.make_async_remote_copy(
        src_ref=p_refs[a].at[2 * px + py], dst_ref=land_refs[a].at[j], send_sem=send_sems.at[3 * a + j],
        recv_sem=recv_sems.at[3 * a + j], device_id=(px, py, c), device_id_type=MESH)
        for a in range(len(p_refs)) for j, (px, py) in enumerate([(1 - x, y), (x, 1 - y), (1 - x, 1 - y)])]


def _chip_exchange_start(parts, name):
    n = len(parts)
    lands = [lax.empty((3,) + p.shape[1:], p.dtype) for p in parts]

    def body(*refs):
        p_refs, land_refs, (send_sems, recv_sems) = refs[:n], refs[n:2 * n], refs[2 * n:2 * n + 2]
        for cp in _chip_copies(p_refs, land_refs, send_sems, recv_sems):
            cp.start()
        token = refs[-1]
        token[...] = jnp.zeros_like(token)

    hbm = lambda t: pltpu.HBM(t.shape, t.dtype)
    res = pl.pallas_call(
        body, name=name,
        out_shape=(pltpu.SemaphoreType.DMA((3 * n,)), pltpu.SemaphoreType.DMA((3 * n,)), *[hbm(t) for t in parts + lands],
                   jax.ShapeDtypeStruct((8, 128), F32)),
        in_specs=(HBM_ONLY,) * (2 * n),
        out_specs=(SEM_SPEC, SEM_SPEC, *[HBM_ONLY] * (2 * n), pl.BlockSpec(memory_space=pltpu.VMEM)),
        input_output_aliases={i: 2 + i for i in range(2 * n)},
        compiler_params=pltpu.CompilerParams(has_side_effects=SIDE_EFFECT))(
        *[pltpu.with_memory_space_constraint(t, pltpu.HBM) for t in parts + lands])
    return (res[0], res[1], list(res[2:2 + n]), list(res[2 + n:2 + 2 * n])), res[-1]


def _chip_exchange_wait(in_flight, after, name):
    send_sems, recv_sems, parts, lands = in_flight
    n = len(parts)

    def body(*refs):
        p_refs, land_refs, (send_sems, recv_sems) = refs[:n], refs[n:2 * n], refs[2 * n:2 * n + 2]
        for cp in _chip_copies(p_refs, land_refs, send_sems, recv_sems):
            cp.wait_send()
            cp.wait_recv()

    res = pl.pallas_call(
        body, name=name, out_shape=tuple(pltpu.HBM(t.shape, t.dtype) for t in parts + lands),
        in_specs=(*[HBM_ONLY] * (2 * n), SEM_SPEC, SEM_SPEC, pl.BlockSpec(memory_space=pl.ANY)),
        out_specs=(HBM_ONLY,) * (2 * n), input_output_aliases={i: i for i in range(2 * n)},
        compiler_params=pltpu.CompilerParams(has_side_effects=SIDE_EFFECT))(*parts, *lands, send_sems, recv_sems, after)
    return list(res[:n]), list(res[n:])


def _slot(p):
    return 4 * p[0] + 2 * p[1] + p[2]


def _gather_copies(src_refs, out_refs, send_sems, recv_sems):
    x, y, c = _mesh_pos()
    targets = [(x, y, 1 - c), (1 - x, y, c), (x, 1 - y, c), (1 - x, 1 - y, c)]
    return [pltpu.make_async_remote_copy(
        src_ref=src_refs[a], dst_ref=out_refs[a].at[_slot((x, y, c))], send_sem=send_sems.at[4 * a + k],
        recv_sem=recv_sems.at[4 * a + k], device_id=to, device_id_type=MESH)
        for a in range(len(src_refs)) for k, to in enumerate(targets)]


def _gather_start(shards, after, name):
    n = len(shards)
    outs = [lax.empty((N_DEV,) + s.shape, s.dtype) for s in shards]

    def body(*refs):
        for cp in _gather_copies(refs[:n], refs[n:2 * n], refs[2 * n + 1], refs[2 * n + 2]):
            cp.start()
        token = refs[-1]
        token[...] = jnp.zeros_like(token)

    res = pl.pallas_call(
        body, name=name,
        out_shape=(pltpu.SemaphoreType.DMA((4 * n,)), pltpu.SemaphoreType.DMA((4 * n,)),
                   *[pltpu.HBM(t.shape, t.dtype) for t in shards + outs], jax.ShapeDtypeStruct((8, 128), F32)),
        in_specs=(*[HBM_ONLY] * (2 * n), pl.BlockSpec(memory_space=pl.ANY)),
        out_specs=(SEM_SPEC, SEM_SPEC, *[HBM_ONLY] * (2 * n), pl.BlockSpec(memory_space=pltpu.VMEM)),
        input_output_aliases={i: 2 + i for i in range(2 * n)},
        compiler_params=pltpu.CompilerParams(has_side_effects=SIDE_EFFECT))(
        *[pltpu.with_memory_space_constraint(t, pltpu.HBM) for t in shards + outs], after)
    return (res[0], res[1], list(res[2:2 + n]), list(res[2 + n:2 + 2 * n])), res[-1]


def _gather_wait(in_flight, after, name):
    send_sems, recv_sems, shards, outs = in_flight
    n = len(shards)

    def body(*refs):
        for cp in _gather_copies(refs[:n], refs[n:2 * n], refs[2 * n], refs[2 * n + 1]):
            cp.wait_send()
            cp.wait_recv()

    res = pl.pallas_call(
        body, name=name, out_shape=tuple(pltpu.HBM(t.shape, t.dtype) for t in shards + outs),
        in_specs=(*[HBM_ONLY] * (2 * n), SEM_SPEC, SEM_SPEC, pl.BlockSpec(memory_space=pl.ANY)),
        out_specs=(HBM_ONLY,) * (2 * n), input_output_aliases={i: i for i in range(2 * n)},
        compiler_params=pltpu.CompilerParams(has_side_effects=SIDE_EFFECT))(*shards, *outs, send_sems, recv_sems, after)
    return list(res[:n]), list(res[n:])


class _PassToSibling:
    def __init__(self, shards, gathered):
        n = self.n = len(shards)
        self.ins = list(shards) + list(gathered)
        self.out_shape = tuple(jax.ShapeDtypeStruct(g.shape, g.dtype) for g in gathered)
        self.aliases = {n + a: a for a in range(n)}
        self.sems = [pltpu.SemaphoreType.DMA((3 * n,)), pltpu.SemaphoreType.DMA((3 * n,)),
                     pltpu.SemaphoreType.DMA((n,))]

    def _copies(self, ins, outs, sems):
        send_sems, recv_sems, local_sems = sems
        x, y, c = _mesh_pos()
        chips = [(1 - x, y), (x, 1 - y), (1 - x, 1 - y)]
        mine = [pltpu.make_async_copy(ins[a], outs[a].at[_slot((x, y, c))], local_sems.at[a]) for a in range(self.n)]
        passed, awaited = [], []
        for a in range(self.n):
            for j, chip in enumerate(chips):
                sems_j = dict(send_sem=send_sems.at[3 * a + j], recv_sem=recv_sems.at[3 * a + j],
                              device_id=(x, y, 1 - c), device_id_type=MESH)
                blk = outs[a].at[_slot((*chip, c))]
                passed.append(pltpu.make_async_remote_copy(src_ref=blk, dst_ref=blk, **sems_j))
                got = outs[a].at[_slot((*chip, 1 - c))]
                awaited.append(pltpu.make_async_remote_copy(src_ref=got, dst_ref=got, **sems_j))
        return mine, passed, awaited

    def start(self, ins, outs, sems):
        mine, passed, _ = self._copies(ins, outs, sems)
        for cp in mine + passed:
            cp.start()

    def finish(self, ins, outs, sems):
        mine, passed, awaited = self._copies(ins, outs, sems)
        for cp in passed:
            cp.wait_send()
        for cp in awaited:
            cp.wait_recv()
        for cp in mine:
            cp.wait()


def _reduce_sums(fulls, recv_core, core, tag):
    return [_pair_sum(f, r, core, f"rs_pair_{tag}{i}") for i, (f, r) in enumerate(zip(fulls, recv_core))]


def _local_step(x, tgt, mods, w_in_shard, order, shards, small, chip, core):
    sh1, sc1, g1, sh2, sc2, g2 = mods
    norm1_g, rel_bias, gn_g, gn_b, norm2_g, norm_f_g = small
    tables = _ret_tables()
    buckets = jnp.asarray(_bucket_tables())

    proj, slabs, w_in_t, h1 = _gather_proj(x, norm1_g, sh1, sc1, w_in_shard, order)
    flight_w1, token_w = _gather_start(list(shards[:3]), proj, "gather_w1_start")
    flight_w2, token_w = _gather_start(list(shards[3:]), token_w, "gather_w2_start")
    bias = _bias_build(rel_bias, buckets, token_w)
    outs, lses = [], []
    for gi in range(len(ATT_GROUPS)):
        o, l = _att_fwd(slabs, bias, gi)
        outs.append(o)
        lses.append(l)
    att, gathered = _mix_fwd(outs, lses, comm=_PassToSibling(*_gather_wait(flight_w1, lses[2], "gather_w1_wait")))
    w_ret_out, w_att_out, w_o = (_from_slots(g, ax) for g, ax in zip(gathered, BIG_AXES[1:4]))
    gated, ro, states = _ret_fwd(proj, tables, gn_g, gn_b, att)
    ret_out, gathered = _mm(gated, w_ret_out, 'nn', tm=S, tn=256, tk=2048, name="ret_out",
                            comm=_PassToSibling(*_gather_wait(flight_w2, gated, "gather_w2_wait")))
    w_ff1, w_ff2 = (_from_slots(g, ax) for g, ax in zip(gathered, BIG_AXES[4:]))
    att_out, merged = _att_out_merge(att, w_att_out, proj, ret_out)
    mixo, x1, h2 = _w_o_norm2(merged, w_o, x, g1, norm2_g, sh2, sc2)
    u, act = _mm(h2, w_ff1, 'nn', tm=S, tn=512, tk=D, name="ff1", relu2=True)
    loss, dx2, g_normf, df, dg2 = _ff2_final(act, w_ff2, x1, g2, tgt, norm_f_g)

    gw_ff2 = _mm(act, df, 'tn', tm=512, tn=D, tk=S, name="gw_ff2", out_dtype=BF16)
    du = _mm(df, w_ff2, 'nt', tm=S, tn=512, tk=D, name="d_act", out_dtype=BF16, relu2_of=u)
    gw_ff1 = _mm(h2, du, 'tn', tm=D, tn=512, tk=S, name="gw_ff1", out_dtype=BF16)
    fulls_a = [_to_slots(g, ax) for g, ax in zip((gw_ff1, gw_ff2), BIG_AXES[4:])]
    dh2, recv_core_a = _mm(du, w_ff1, 'nt', tm=1024, tn=1024, tk=2048, name="dh2", comm=_ExchangeCore(fulls_a))
    parts_a = _reduce_sums(fulls_a, recv_core_a, core, "a")
    flight_a, token_a = _chip_exchange_start(parts_a, "rs_a_start")
    dx1, dsc2, dsh2, g_norm2, dmixo, dg1 = _norm_mod_bwd(x1, norm2_g, sc2, dh2, dx2, "norm2_bwd", gate=(mixo, g1))

    gw_o = _mm(merged, dmixo, 'tn', tm=D, tn=512, tk=S, name="gw_o", out_dtype=BF16, after=token_a)
    d_ret_out, d_att_out, dga, dgb = _dmerged_split(dmixo, w_o, proj, ret_out, att_out)
    gw_ret_out = _mm(gated, d_ret_out, 'tn', tm=512, tn=D, tk=S, name="gw_ret_out", out_dtype=BF16)
    gw_att_out = _mm(att, d_att_out, 'tn', tm=AW, tn=D, tk=S, name="gw_att_out", out_dtype=BF16)
    fulls_b = [_to_slots(g, ax) for g, ax in zip((gw_ret_out, gw_att_out, gw_o), BIG_AXES[1:4])]
    dgated, recv_core_b = _mm(d_ret_out, w_ret_out, 'nt', tm=S, tn=512, tk=D, name="dgated",
                              comm=_ExchangeCore(fulls_b))
    parts_b = _reduce_sums(fulls_b, recv_core_b, core, "b")
    flight_b, token_b = _chip_exchange_start(parts_b, "rs_b_start")
    datt = _mm(d_att_out, w_att_out, 'nt', tm=S, tn=AW, tk=D, name="datt", after=token_b)
    mix_grads = _mix_bwd(outs, lses, datt)
    datt_parts, ds_sums = [], []
    for gi in range(len(ATT_GROUPS)):
        dq, dk, dv, ds_sum = _att_bwd(slabs, bias, outs[gi], lses[gi], mix_grads[gi], mix_grads[3 + gi], gi)
        datt_parts += [dq, dk, dv]
        ds_sums.append(ds_sum)
    g_bias = _bias_grad(jnp.concatenate(ds_sums, axis=0), buckets)[:, :, 0].T.reshape(1, -1)
    dproj, g_gn_g, g_gn_b = _ret_bwd(proj, tables, gn_g, gn_b, ro, states, dgated, datt_parts + [dga, dgb])
    parts_a, recv_chip_a = _chip_exchange_wait(flight_a, dproj, "rs_a_wait")
    parts_b, recv_chip_b = _chip_exchange_wait(flight_b, dproj, "rs_b_wait")
    reduced = list(zip(parts_b + parts_a, recv_chip_b + recv_chip_a))
    full_in = _to_slots(_mm(dproj, h1, 'tn', tm=512, tn=D, tk=S, name="gw_in", out_dtype=BF16), 0)
    halves = [(half * (D // 2), D // 2) for half in range(2)]
    (recv_core_in0,) = _run_comm(_ExchangeCore([full_in], cols=halves[0]), "rs_core_in0")
    flight0, token = _chip_exchange_start([_pair_sum(full_in, recv_core_in0, core, "rs_pair_c0", col_block=0)],
                                          "rs_in0_start")
    dh1, (recv_core_in1,) = _mm(dproj, w_in_t, 'nn', tm=1024, tn=1024, tk=2560, name="dh1",
                                comm=_ExchangeCore([full_in], cols=halves[1]), after=token)
    flight1, token = _chip_exchange_start([_pair_sum(full_in, recv_core_in1, core, "rs_pair_c1", col_block=1)],
                                          "rs_in1_start")
    in_flight = [flight0, flight1]
    gx, dsc1, dsh1, g_norm1 = _norm_mod_bwd(x, norm1_g, sc1, dh1, dx1, "norm1_bwd", after=token)

    dmod = [dsh1, dsc1, dg1, dsh2, dsc2, dg2]
    small_g = [g_norm1, g_bias, g_gn_g, g_gn_b, g_norm2, g_normf]
    return loss, gx, in_flight, reduced, small_g, dmod


def _to_slots(g, axis):
    if axis == 0:
        return g.reshape(4, 2, g.shape[0] // N_DEV, g.shape[1])
    return g.reshape(g.shape[0], N_DEV, g.shape[1] // N_DEV).transpose(1, 0, 2).reshape(4, 2, g.shape[0], -1)


def _from_slots(w8, axis):
    if axis == 0:
        return w8.reshape(-1, w8.shape[2])
    return w8.transpose(1, 0, 2).reshape(w8.shape[1], -1)


BIG_AXES = (1, 0, 1, 0, 1, 0)


def kernel(x, c, w_ada, b_ada, norm1_g, w_in, rel_bias, ret_gn_g, ret_gn_b, w_ret_out, w_att_out, w_o, norm2_g, w_ff1, w_ff2, norm_f_g, loss_target, m_w_ada, m_b_ada, m_norm1_g, m_w_in, m_rel_bias, m_ret_gn_g, m_ret_gn_b, m_w_ret_out, m_w_att_out, m_w_o, m_norm2_g, m_w_ff1, m_w_ff2, m_norm_f_g, v_w_ada, v_b_ada, v_norm1_g, v_w_in, v_rel_bias, v_ret_gn_g, v_ret_gn_b, v_w_ret_out, v_w_att_out, v_w_o, v_norm2_g, v_w_ff1, v_w_ff2, v_norm_f_g):
    mx, my, mc = _mesh_pos()
    dev = 4 * mx + 2 * my + mc
    chip = jnp.reshape(2 * mx + my, (1,)).astype(jnp.int32)
    core = jnp.reshape(mc, (1,)).astype(jnp.int32)
    ada_w = D * 6 // N_DEV

    w_in, m_w_in, v_w_in = (jnp.transpose(t, (0, 2, 1)) for t in (w_in, m_w_in, v_w_in))

    (w_in_shard,), (c_all,) = _to_bf16([w_in[0]], _Gather([c]), "gather_c")
    c_all = c_all.reshape(N_DEV, D)
    b_sl = lax.dynamic_slice(b_ada, (0, dev * ada_w), (1, ada_w))
    other_shards, (mod_all,) = _to_bf16([w[0] for w in (w_ret_out, w_att_out, w_o, w_ff1, w_ff2)],
                                        _Gather([_ada_fwd(c_all, w_ada[0], b_sl)]), "gather_mod")
    mod = lax.dynamic_index_in_dim(mod_all, dev, axis=1, keepdims=False).reshape(6, D)
    mods = tuple(mod[i:i + 1] for i in range(6))

    small = (norm1_g, rel_bias, ret_gn_g, ret_gn_b, norm2_g, norm_f_g.reshape(1, D))
    order = lax.dynamic_index_in_dim(jnp.asarray(_proj_order()), 2 * mx + my, axis=0, keepdims=False)
    loss, gx, in_flight, big_red, small_g, dmod = _local_step(x[0], loss_target[0], mods, w_in_shard, order,
                                                              list(other_shards), small, chip, core)

    names = ['w_ada', 'b_ada', 'norm1_g', 'w_in', 'rel_bias', 'ret_gn_g', 'ret_gn_b', 'w_ret_out', 'w_att_out',
             'w_o', 'norm2_g', 'w_ff1', 'w_ff2', 'norm_f_g']
    ws = dict(zip(names, (w_ada, b_ada, norm1_g, w_in, rel_bias, ret_gn_g, ret_gn_b, w_ret_out, w_att_out, w_o,
                          norm2_g, w_ff1, w_ff2, norm_f_g)))
    ms = dict(zip(names, (m_w_ada, m_b_ada, m_norm1_g, m_w_in, m_rel_bias, m_ret_gn_g, m_ret_gn_b, m_w_ret_out,
                          m_w_att_out, m_w_o, m_norm2_g, m_w_ff1, m_w_ff2, m_norm_f_g)))
    vs = dict(zip(names, (v_w_ada, v_b_ada, v_norm1_g, v_w_in, v_rel_bias, v_ret_gn_g, v_ret_gn_b, v_w_ret_out,
                          v_w_att_out, v_w_o, v_norm2_g, v_w_ff1, v_w_ff2, v_norm_f_g)))
    grads, delta, new_m, new_v = {}, {}, {}, {}
    big_names = ('w_ret_out', 'w_att_out', 'w_o', 'w_ff1', 'w_ff2')
    for n, (part, recv) in zip(big_names, big_red):
        grads[n], delta[n], new_m[n], new_v[n] = _adamw_reduced1(ws[n], ms[n], vs[n], part, recv, chip, "adamw_" + n)
    updated = lax.optimization_barrier((gx, tuple(delta[n] for n in big_names)))
    gathered = _run_comm(_Gather(dmod + small_g + [loss]), "gather_small", after=updated[0])
    g_b_ada, dmod_all, (g_norm1, g_bias, g_gn_g, g_gn_b, g_norm2, g_normf, loss_sum) = _sum_small(gathered)
    loss_out = loss_sum[0, 0]
    g_w_ada = _ada_bwd(c_all, lax.dynamic_slice(dmod_all, (0, dev * ada_w), (N_DEV, ada_w)))

    grads.update(w_ada=g_w_ada.reshape(w_ada.shape), b_ada=g_b_ada, norm1_g=g_norm1, rel_bias=g_bias,
                 ret_gn_g=g_gn_g, ret_gn_b=g_gn_b, norm2_g=g_norm2, norm_f_g=g_normf)
    delta['w_ada'], new_m['w_ada'], new_v['w_ada'] = _adamw(w_ada, g_w_ada, m_w_ada, v_w_ada, "adamw_w_ada")
    small_names = ('b_ada', 'norm1_g', 'rel_bias', 'ret_gn_g', 'ret_gn_b', 'norm2_g', 'norm_f_g')
    two_d = {n: (1, ws[n].size) if ws[n].ndim == 1 else ws[n].shape for n in small_names}
    d_, m_, v_ = _adamw_small(*[[src[n].reshape(two_d[n]) for n in small_names] for src in (ws, grads, ms, vs)])
    for i, n in enumerate(small_names):
        shp = ws[n].shape
        delta[n], new_m[n], new_v[n] = d_[i].reshape(shp), m_[i].reshape(shp), v_[i].reshape(shp)
        grads[n] = grads[n].reshape(shp)

    done = lax.optimization_barrier((gx, tuple(d_), tuple(delta[n] for n in ('w_ada', 'w_ret_out', 'w_att_out', 'w_o',
                                                                               'w_ff1', 'w_ff2'))))
    parts_in, recvs_in = [], []
    for half, flight in enumerate(in_flight):
        (part_in,), (recv_chip_in,) = _chip_exchange_wait(flight, done[0], f"rs_in{half}_wait")
        parts_in.append(part_in)
        recvs_in.append(recv_chip_in)
    grads['w_in'], delta['w_in'], new_m['w_in'], new_v['w_in'] = _adamw_reduced(w_in, m_w_in, v_w_in, parts_in,
                                                                               recvs_in, chip)
    for d in (grads, delta, new_m, new_v):
        d['w_in'] = jnp.transpose(d['w_in'], (0, 2, 1))
    return (loss_out, gx[None], *[grads[n] for n in names], *[delta[n] for n in names],
            *[new_m[n] for n in names], *[new_v[n] for n in names])
```

```python
import functools
import math

import numpy as np
import jax
import jax.numpy as jnp
from jax import lax
from jax.experimental import pallas as pl
from jax.experimental.pallas import tpu as pltpu

F32 = jnp.float32
BF16 = jnp.bfloat16
MESH = pl.DeviceIdType.MESH

N_DEV = 8
S = 2048
D = 1024
RET_HEADS = 4
RET_DK = 256
RET_DV = 512
CHUNK = 128
N_CHUNK = S // CHUNK
ATT_GROUPS = ((128, 1), (512, 4), (2048, 16))
ATT_HG = 4
ATT_DH = 128
ATT_BLK = 128
N_BUCKETS = 32
MAX_DIST = 2048
D_FF = 4096
IN_COLS = 12800
OFF_RQ, OFF_RK, OFF_RV, OFF_RG, OFF_ATT = 0, 1024, 2048, 4096, 6144
OFF_GA, OFF_GB = 6144, 7168
RMS_EPS = 1e-6
GN_EPS = 1e-5
ADAM_LR, ADAM_B1, ADAM_B2, ADAM_EPS, ADAM_WD, ADAM_STEP = 0.001, 0.9, 0.999, 1e-08, 0.01, 10
VMEM_LIMIT = 48 * 1024 * 1024


def _pcall(body, **kw):
    return pl.pallas_call(body, **kw)


def _params(sem=None):
    return pltpu.CompilerParams(dimension_semantics=sem, vmem_limit_bytes=VMEM_LIMIT)


HBM_SPEC = pl.BlockSpec(memory_space=pl.ANY)


def _carry(body, comm, *, name, grid, in_specs, out_specs, out_shape, scratch_shapes=()):
    single = not isinstance(out_specs, (tuple, list))
    o_specs = (out_specs,) if single else tuple(out_specs)
    o_shape = (out_shape,) if single else tuple(out_shape)
    n_in, n_out, n_scr = len(in_specs), len(o_specs), len(scratch_shapes)
    nci, nco = len(comm.ins), len(comm.out_shape)
    total = int(np.prod(grid))

    def wrapped(*refs):
        bounds = np.cumsum([0, n_in, nci, n_out, nco, n_scr])
        a, ci, o, co, scr = (refs[bounds[i]:bounds[i + 1]] for i in range(5))
        sems = refs[bounds[5]:]
        flat = 0
        for d, g in enumerate(grid):
            flat = flat * g + pl.program_id(d)

        @pl.when(flat == 0)
        def _():
            comm.start(ci, co, sems)

        body(*a, *o, *scr)

        @pl.when(flat == total - 1)
        def _():
            comm.finish(ci, co, sems)

    aliases = {n_in + i: n_out + o for i, o in getattr(comm, "aliases", {}).items()}
    call = _pcall(wrapped, name=name, grid=grid, in_specs=list(in_specs) + [HBM_SPEC] * nci,
                  out_specs=o_specs + (HBM_SPEC,) * nco, out_shape=o_shape + tuple(comm.out_shape),
                  scratch_shapes=list(scratch_shapes) + list(comm.sems), input_output_aliases=aliases,
                  compiler_params=_params(("arbitrary",) * len(grid)))

    def run(*args):
        res = call(*args, *comm.ins)
        own = res[0] if single else tuple(res[:n_out])
        return own, tuple(res[n_out:])

    return run


def _run_comm(comm, name, after=None):
    nci, nco = len(comm.ins), len(comm.out_shape)
    extra = [] if after is None else [after]

    def body(*refs):
        ci, co, sems = refs[:nci], refs[nci + len(extra):nci + len(extra) + nco], refs[nci + len(extra) + nco:]
        comm.start(ci, co, sems)
        comm.finish(ci, co, sems)

    return _pcall(body, name=name, in_specs=[HBM_SPEC] * (nci + len(extra)), out_specs=(HBM_SPEC,) * nco,
                  out_shape=tuple(comm.out_shape), scratch_shapes=list(comm.sems))(*comm.ins, *extra)


def _dot(a, b, dn):
    return lax.dot_general(a.astype(BF16), b.astype(BF16), (dn, ((), ())), preferred_element_type=F32)


NN = ((1,), (0,))
NT = ((1,), (1,))
TN = ((0,), (0,))


def _mm(a, b, mode, *, tm, tn, tk, name, out_dtype=F32, res=None, gvec=None, relu2=False, relu2_of=None, comm=None,
        after=None):
    if mode == 'nn':
        (M, K), (_, N) = a.shape, b.shape
        a_spec = pl.BlockSpec((tm, tk), lambda i, j, k: (i, k))
        b_spec = pl.BlockSpec((tk, tn), lambda i, j, k: (k, j))
        dn = NN
    elif mode == 'nt':
        (M, K), (N, _) = a.shape, b.shape
        a_spec = pl.BlockSpec((tm, tk), lambda i, j, k: (i, k))
        b_spec = pl.BlockSpec((tn, tk), lambda i, j, k: (j, k))
        dn = NT
    else:
        (K, M), (_, N) = a.shape, b.shape
        a_spec = pl.BlockSpec((tk, tm), lambda i, j, k: (k, i))
        b_spec = pl.BlockSpec((tk, tn), lambda i, j, k: (k, j))
        dn = TN
    assert M % tm == 0 and N % tn == 0 and K % tk == 0, (name, M, N, K)
    nk = K // tk
    fused = res is not None
    o_spec = pl.BlockSpec((tm, tn), lambda i, j, k: (i, j))

    def body(a_ref, b_ref, *rest):
        acc_ref = rest[-1] if nk > 1 else None
        if after is not None:
            rest = rest[1:]
        if fused:
            res_ref, g_ref, o_ref, x_ref = rest[:4]
        elif relu2_of is not None:
            u_ref, o_ref = rest[:2]
        elif relu2:
            o_ref, act_ref = rest[:2]
        else:
            o_ref = rest[0]

        def finish(acc):
            if relu2_of is not None:
                acc = acc * (2.0 * jnp.maximum(u_ref[...], 0.0))
            o_ref[...] = acc.astype(o_ref.dtype)
            if fused:
                x_ref[...] = res_ref[...] + g_ref[...] * acc
            if relu2:
                r = jnp.maximum(acc, 0.0)
                act_ref[...] = (r * r).astype(BF16)

        p = _dot(a_ref[...], b_ref[...], dn)
        if nk == 1:
            finish(p)
        else:
            k = pl.program_id(2)

            @pl.when(k == 0)
            def _():
                acc_ref[...] = p

            @pl.when(k > 0)
            def _():
                acc_ref[...] += p

            @pl.when(k == nk - 1)
            def _():
                finish(acc_ref[...])

    in_specs = [a_spec, b_spec]
    args = [a, b]
    if after is not None:
        in_specs.append(pl.BlockSpec(memory_space=pl.ANY))
        args.append(after)
    out_shape = jax.ShapeDtypeStruct((M, N), out_dtype)
    out_specs = o_spec
    if fused:
        in_specs += [pl.BlockSpec((tm, tn), lambda i, j, k: (i, j)), pl.BlockSpec((1, tn), lambda i, j, k: (0, j))]
        args += [res, gvec]
        out_shape = (out_shape, jax.ShapeDtypeStruct((M, N), F32))
        out_specs = (o_spec, pl.BlockSpec((tm, tn), lambda i, j, k: (i, j)))
    elif relu2_of is not None:
        in_specs.append(pl.BlockSpec((tm, tn), lambda i, j, k: (i, j)))
        args.append(relu2_of)
    elif relu2:
        out_shape = (out_shape, jax.ShapeDtypeStruct((M, N), BF16))
        out_specs = (o_spec, pl.BlockSpec((tm, tn), lambda i, j, k: (i, j)))
    kw = dict(name=name, grid=(M // tm, N // tn, nk), in_specs=in_specs, out_specs=out_specs,
              out_shape=out_shape, scratch_shapes=[pltpu.VMEM((tm, tn), F32)] if nk > 1 else [])
    if comm is not None:
        return _carry(body, comm, **kw)(*args)
    return _pcall(body, compiler_params=_params(("parallel", "parallel", "arbitrary")), **kw)(*args)


PROJ_TN = 512
ATT_T0, ATT_T1 = 6144 // PROJ_TN, 10752 // PROJ_TN
N_SLABS = (ATT_T1 - ATT_T0) * 4
MAIN_COLS = IN_COLS - (ATT_T1 - ATT_T0) * PROJ_TN


PROJ_TILES = IN_COLS // PROJ_TN
SHARD_ROWS = IN_COLS // N_DEV
W_CHUNKS = 4
N_OWN, N_NEAR = 5, 18


def _proj_order():
    out = np.zeros((4, 3, PROJ_TILES), np.int32)
    for q in range(4):
        def hops(t):
            owners = {col // (2 * SHARD_ROWS) for col in (t * PROJ_TN, (t + 1) * PROJ_TN - 1)}
            return max(bin(q ^ p).count("1") for p in owners)
        order = sorted(range(PROJ_TILES), key=lambda t: (hops(t), t))
        assert all(hops(t) == 0 for t in order[:N_OWN]) and all(hops(t) < 2 for t in order[:N_NEAR])
        is_att = [ATT_T0 <= t < ATT_T1 for t in order]
        for row, kind, index in ((1, False, lambda t: t if t < ATT_T0 else t - (ATT_T1 - ATT_T0)),
                                 (2, True, lambda t: t - ATT_T0)):
            own = [index(t) if a == kind else None for t, a in zip(order, is_att)]
            first = next(v for v in own if v is not None)
            last = first
            for j, v in enumerate(own):
                last = last if v is None else v
                out[q, row, j] = last
        out[q, 0] = order
    return out


def _gather_proj(x, g, sh, sc, shard, order):
    rows = SHARD_ROWS // W_CHUNKS

    def body(ord_ref, x_ref, g_ref, shift_ref, scale_ref, sh_ref, main_ref, slab_ref, full_ref, a_ref, wbuf, xbuf,
             fetch_sems, send_sems, recv_sems, local_sems, x_sem):
        j = pl.program_id(0)
        x, y, c = _mesh_pos()
        me, sibling = (x, y, c), (x, y, 1 - c)
        chips = [(1 - x, y), (x, 1 - y), (1 - x, 1 - y)]

        def block(p, owner):
            return full_ref.at[pl.ds(pl.multiple_of(_slot(owner) * SHARD_ROWS + p * rows, 16), rows)]

        def copy(p, k, owner, to, from_input=False):
            dst = block(p, owner)
            return pltpu.make_async_remote_copy(
                src_ref=sh_ref.at[pl.ds(p * rows, rows)] if from_input else dst, dst_ref=dst,
                send_sem=send_sems.at[7 * p + k], recv_sem=recv_sems.at[7 * p + k], device_id=to, device_id_type=MESH)

        pieces = range(W_CHUNKS)
        mine = [pltpu.make_async_copy(sh_ref.at[pl.ds(p * rows, rows)], block(p, me), local_sems.at[p]) for p in pieces]
        first = [copy(p, 0, me, sibling, from_input=True) for p in pieces]
        first += [copy(p, 1 + n, me, (*chips[n], c), from_input=True) for p in pieces for n in range(2)]
        near_pass = [copy(p, 4 + n, (*chips[n], c), sibling) for p in pieces for n in range(2)]
        relay = [copy(p, 3, ((x + 1 - c) % 2, (y + c) % 2, c), ((x + c) % 2, (y + 1 - c) % 2, c)) for p in pieces]
        far_pass = [copy(p, 6, (*chips[2], c), sibling) for p in pieces]

        def fetch(pos):
            slot = lax.rem(pos, 2)
            start = pl.multiple_of(ord_ref[0, pos] * PROJ_TN, PROJ_TN)
            return pltpu.make_async_copy(full_ref.at[pl.ds(start, PROJ_TN)], wbuf.at[slot], fetch_sems.at[slot])

        @pl.when(j == 0)
        def _():
            x_copy = pltpu.make_async_copy(x_ref, xbuf, x_sem.at[0])
            x_copy.start()
            for cp in mine + first:
                cp.start()
            x_copy.wait()
            for r in range(S // TR):
                rws = pl.ds(r * TR, TR)
                xv = xbuf[rws, :]
                rstd = lax.rsqrt(jnp.mean(xv * xv, axis=-1, keepdims=True) + RMS_EPS)
                n = xv * rstd * g_ref[...]
                a_ref[rws, :] = (n * (1.0 + scale_ref[...]) + shift_ref[...]).astype(BF16)
            for cp in mine:
                cp.wait()
            for p in pieces:
                copy(p, 0, sibling, me).wait_recv()
            fetch(j).start()

        @pl.when(j == N_OWN - 1)
        def _():
            for p in pieces:
                for n in range(2):
                    copy(p, 1 + n, (*chips[n], c), me).wait_recv()
                    near_pass[2 * p + n].start()
                relay[p].start()
            for p in pieces:
                for n in range(2):
                    copy(p, 4 + n, (*chips[n], 1 - c), me).wait_recv()

        @pl.when(j == N_NEAR - 1)
        def _():
            for p in pieces:
                copy(p, 3, (*chips[2], c), me).wait_recv()
                far_pass[p].start()
            for p in pieces:
                copy(p, 6, (*chips[2], 1 - c), me).wait_recv()

        @pl.when(j + 1 < PROJ_TILES)
        def _():
            fetch(j + 1).start()

        fetch(j).wait()
        w_ref = wbuf.at[lax.rem(j, 2)]
        tile = ord_ref[0, j]
        is_att = (tile >= ATT_T0) & (tile < ATT_T1)
        chunks = [pl.ds(r * 512, 512) for r in range(S // 512)]

        @pl.when(jnp.logical_not(is_att))
        def _():
            for rws in chunks:
                main_ref[rws, :] = _dot(a_ref[rws, :], w_ref[...], NT)

        @pl.when(is_att)
        def _():
            for rws in chunks:
                p = _dot(a_ref[rws, :], w_ref[...], NT)
                for h in range(4):
                    slab_ref[h, rws, :] = p[:, h * 128:(h + 1) * 128]

        @pl.when(j == PROJ_TILES - 1)
        def _():
            for cp in first + near_pass + relay + far_pass:
                cp.wait_send()

    vec = pl.BlockSpec((1, D), lambda j, o: (0, 0))
    gs = pltpu.PrefetchScalarGridSpec(
        num_scalar_prefetch=1, grid=(PROJ_TILES,),
        in_specs=[HBM_SPEC, vec, vec, vec, HBM_SPEC],
        out_specs=(pl.BlockSpec((S, PROJ_TN), lambda j, o: (0, o[1, j])),
                   pl.BlockSpec((4, S, 128), lambda j, o: (o[2, j], 0, 0)), HBM_SPEC,
                   pl.BlockSpec((S, D), lambda j, o: (0, 0))),
        scratch_shapes=[pltpu.VMEM((2, PROJ_TN, D), BF16), pltpu.VMEM((S, D), F32), pltpu.SemaphoreType.DMA((2,)),
                        pltpu.SemaphoreType.DMA((7 * W_CHUNKS,)), pltpu.SemaphoreType.DMA((7 * W_CHUNKS,)),
                        pltpu.SemaphoreType.DMA((W_CHUNKS,)), pltpu.SemaphoreType.DMA((1,))])
    return _pcall(body, name="gather_proj", grid_spec=gs,
                  out_shape=(jax.ShapeDtypeStruct((S, MAIN_COLS), F32), jax.ShapeDtypeStruct((N_SLABS, S, 128), F32),
                             jax.ShapeDtypeStruct((IN_COLS, D), BF16), jax.ShapeDtypeStruct((S, D), BF16)),
                  compiler_params=_params(("arbitrary",)))(order, x, g, sh, sc, shard)


TR = 256


def _row_spec(w=D):
    return pl.BlockSpec((TR, w), lambda i: (i, 0))


def _vec_spec(w=D):
    return pl.BlockSpec((1, w), lambda i: (0, 0))


def _norm_mod_bwd(x, g, sc, dh, dres, name, gate=None, after=None):
    gated = gate is not None

    def body(x_ref, g_ref, sc_ref, dh_ref, dres_ref, *rest):
        if after is not None:
            rest = rest[1:]
        if gated:
            f_ref, gv_ref, dx_ref, dsc_ref, dsh_ref, dg_ref, dz_ref, dgv_ref = rest
        else:
            dx_ref, dsc_ref, dsh_ref, dg_ref = rest
        i = pl.program_id(0)
        xv = x_ref[...]
        dh = dh_ref[...]
        rstd = lax.rsqrt(jnp.mean(xv * xv, axis=-1, keepdims=True) + RMS_EPS)
        xhat = xv * rstd
        gv = g_ref[...]
        dn = dh * (1.0 + sc_ref[...])
        dxhat = dn * gv
        dx = dres_ref[...] + rstd * (dxhat - xhat * jnp.mean(dxhat * xhat, axis=-1, keepdims=True))
        dx_ref[...] = dx
        sums = [(dsc_ref, jnp.sum(dh * (xhat * gv), axis=0, keepdims=True)),
                (dsh_ref, jnp.sum(dh, axis=0, keepdims=True)),
                (dg_ref, jnp.sum(dn * xhat, axis=0, keepdims=True))]
        if gated:
            dz_ref[...] = (dx * gv_ref[...]).astype(BF16)
            sums.append((dgv_ref, jnp.sum(dx * f_ref[...], axis=0, keepdims=True)))

        @pl.when(i == 0)
        def _():
            for ref, p in sums:
                ref[...] = p

        @pl.when(i > 0)
        def _():
            for ref, p in sums:
                ref[...] += p

    vec = jax.ShapeDtypeStruct((1, D), F32)
    in_specs = [_row_spec(), _vec_spec(), _vec_spec(), _row_spec(), _row_spec()]
    out_specs = [_row_spec(), _vec_spec(), _vec_spec(), _vec_spec()]
    out_shape = [jax.ShapeDtypeStruct((S, D), F32), vec, vec, vec]
    args = [x, g, sc, dh, dres]
    if after is not None:
        in_specs.append(HBM_SPEC)
        args.append(after)
    if gated:
        in_specs += [_row_spec(), _vec_spec()]
        out_specs += [_row_spec(), _vec_spec()]
        out_shape += [jax.ShapeDtypeStruct((S, D), BF16), vec]
        args += list(gate)
    return _pcall(body, name=name, grid=(S // TR,), in_specs=in_specs, out_specs=tuple(out_specs),
                  out_shape=tuple(out_shape), compiler_params=_params(("arbitrary",)))(*args)


def _w_o_norm2(merged, w_o, x, g1, g, sh, sc):
    def body(a_ref, b_ref, x_ref, g1_ref, g_ref, sh_ref, sc_ref, o_ref, x1_ref, h_ref):
        acc = _dot(a_ref[...], b_ref[...], NN)
        o_ref[...] = acc
        xv = x_ref[...] + g1_ref[...] * acc
        x1_ref[...] = xv
        rstd = lax.rsqrt(jnp.mean(xv * xv, axis=-1, keepdims=True) + RMS_EPS)
        h_ref[...] = (xv * rstd * g_ref[...] * (1.0 + sc_ref[...]) + sh_ref[...]).astype(BF16)

    rows = pl.BlockSpec((FF2_TM, D), lambda i: (i, 0))
    f32 = jax.ShapeDtypeStruct((S, D), F32)
    return _pcall(body, name="w_o_norm2", grid=(S // FF2_TM,),
                  in_specs=[rows, pl.BlockSpec((D, D), lambda i: (0, 0)), rows] + [_vec_spec()] * 4,
                  out_specs=(rows, rows, rows), out_shape=(f32, f32, jax.ShapeDtypeStruct((S, D), BF16)),
                  compiler_params=_params(("parallel",)))(merged, w_o, x, g1, g, sh, sc)


FF2_TM = 512


def _ff2_final(act, w_ff2, x1, g2, tgt, g):
    def body(a_ref, b_ref, x1_ref, g2_ref, t_ref, g_ref, loss_ref, dx_ref, dg_ref, df_ref, dg2_ref):
        i = pl.program_id(0)
        f = _dot(a_ref[...], b_ref[...], NN)
        g2v = g2_ref[...]
        xv = x1_ref[...] + g2v * f
        gv = g_ref[...]
        rstd = lax.rsqrt(jnp.mean(xv * xv, axis=-1, keepdims=True) + RMS_EPS)
        xhat = xv * rstd
        err = xhat * gv - t_ref[...]
        dy = err * (1.0 / D)
        dxhat = dy * gv
        dx = rstd * (dxhat - xhat * jnp.mean(dxhat * xhat, axis=-1, keepdims=True))
        dx_ref[...] = dx
        df_ref[...] = (dx * g2v).astype(BF16)
        p_g = jnp.sum(dy * xhat, axis=0, keepdims=True)
        p_g2 = jnp.sum(dx * f, axis=0, keepdims=True)
        p_l = jnp.zeros((1, 128), F32) + 0.5 * jnp.sum(jnp.mean(err * err, axis=-1, keepdims=True))

        @pl.when(i == 0)
        def _():
            dg_ref[...] = p_g
            dg2_ref[...] = p_g2
            loss_ref[...] = p_l

        @pl.when(i > 0)
        def _():
            dg_ref[...] += p_g
            dg2_ref[...] += p_g2
            loss_ref[...] += p_l

    vec = jax.ShapeDtypeStruct((1, D), F32)
    rows = lambda w: pl.BlockSpec((FF2_TM, w), lambda i: (i, 0))
    return _pcall(body, name="ff2_final", grid=(S // FF2_TM,),
                  in_specs=[rows(D_FF), pl.BlockSpec((D_FF, D), lambda i: (0, 0)), rows(D), _vec_spec(), rows(D),
                            _vec_spec()],
                  out_specs=(_vec_spec(128), rows(D), _vec_spec(), rows(D), _vec_spec()),
                  out_shape=(jax.ShapeDtypeStruct((1, 128), F32), jax.ShapeDtypeStruct((S, D), F32), vec,
                             jax.ShapeDtypeStruct((S, D), BF16), vec),
                  compiler_params=_params(("arbitrary",)))(act, w_ff2, x1, g2, tgt, g)


HALF = 512


MERGE_TM = 1024


def _merge_specs():
    blk = lambda off: pl.BlockSpec((MERGE_TM, HALF), lambda i, j: (i, off // HALF + j))
    return blk(OFF_GA), blk(OFF_GB), blk(0)


def _att_out_merge(att, w_att_out, proj, ret_out):
    def body(a_ref, b_ref, ga_ref, gb_ref, r_ref, o_ref, m_ref):
        acc = _dot(a_ref[...], b_ref[...], NN)
        o_ref[...] = acc
        m_ref[...] = (jax.nn.sigmoid(ga_ref[...]) * r_ref[...] + jax.nn.sigmoid(gb_ref[...]) * acc).astype(BF16)

    ga, gb, tile = _merge_specs()
    return _pcall(body, name="att_out", grid=(S // MERGE_TM, D // HALF),
                  in_specs=[pl.BlockSpec((MERGE_TM, AW), lambda i, j: (i, 0)), pl.BlockSpec((AW, HALF), lambda i, j: (0, j)),
                            ga, gb, tile],
                  out_specs=(tile, tile),
                  out_shape=(jax.ShapeDtypeStruct((S, D), F32), jax.ShapeDtypeStruct((S, D), BF16)),
                  compiler_params=_params(("parallel", "parallel")))(att, w_att_out, proj, proj, ret_out)


def _dmerged_split(dmixo, w_o, proj, ret_out, att_out):
    def body(a_ref, b_ref, ga_ref, gb_ref, r_ref, at_ref, dr_ref, da_ref, dga_ref, dgb_ref):
        dm = _dot(a_ref[...], b_ref[...], NT)
        sa = jax.nn.sigmoid(ga_ref[...])
        sb = jax.nn.sigmoid(gb_ref[...])
        dr_ref[...] = (dm * sa).astype(BF16)
        da_ref[...] = (dm * sb).astype(BF16)
        dga_ref[...] = (dm * r_ref[...] * (sa * (1.0 - sa))).astype(BF16)
        dgb_ref[...] = (dm * at_ref[...] * (sb * (1.0 - sb))).astype(BF16)

    ga, gb, tile = _merge_specs()
    o = jax.ShapeDtypeStruct((S, D), BF16)
    return _pcall(body, name="dmerged", grid=(S // MERGE_TM, D // HALF),
                  in_specs=[pl.BlockSpec((MERGE_TM, D), lambda i, j: (i, 0)), pl.BlockSpec((HALF, D), lambda i, j: (j, 0)),
                            ga, gb, tile, tile],
                  out_specs=(tile,) * 4, out_shape=(o, o, o, o),
                  compiler_params=_params(("parallel", "parallel")))(dmixo, w_o, proj, proj, ret_out, att_out)


def _ret_tables():
    H, C = RET_HEADS, CHUNK
    log_g = jnp.log1p(-(2.0 ** (-5.0 - jnp.arange(H, dtype=F32))))
    idx = jnp.arange(C, dtype=F32)
    rel = idx[:, None] - idx[None, :]
    inner = jnp.where(rel >= 0, jnp.exp(log_g[:, None, None] * jnp.maximum(rel, 0.0)), 0.0)
    qd = jnp.exp(log_g[:, None] * (idx + 1.0))[:, :, None]
    kd = jnp.exp(log_g[:, None] * (C - 1.0 - idx))[:, :, None]
    cd = jnp.broadcast_to(jnp.exp(log_g * C)[:, None, None], (H, 1, 128))
    half = RET_DK // 2
    inv = 10000.0 ** (-jnp.arange(half, dtype=F32) / half)
    ang = jnp.arange(S, dtype=F32)[:, None] * inv[None, :]
    return inner, qd, kd, cd, jnp.cos(ang), jnp.sin(ang)


def _rot(x, cos, sin):
    x1, x2 = x[:, :128], x[:, 128:]
    return jnp.concatenate([x1 * cos - x2 * sin, x1 * sin + x2 * cos], axis=1)


def _rot_t(d, cos, sin):
    d1, d2 = d[:, :128], d[:, 128:]
    return jnp.concatenate([d1 * cos + d2 * sin, d2 * cos - d1 * sin], axis=1)


RET_COLS = OFF_ATT
RET_VW = RET_HEADS * RET_DV


def _ret_specs(chunk_of):
    ci = chunk_of
    whole = lambda shape: pl.BlockSpec(shape, lambda t: (0,) * len(shape))
    return [
        pl.BlockSpec((CHUNK, RET_COLS), lambda t: (ci(t), 0)),
        pl.BlockSpec((CHUNK, 128), lambda t: (ci(t), 0)),
        pl.BlockSpec((CHUNK, 128), lambda t: (ci(t), 0)),
        whole((RET_HEADS, CHUNK, CHUNK)), whole((RET_HEADS, CHUNK, 1)), whole((RET_HEADS, CHUNK, 1)),
        whole((RET_HEADS, 1, 128)), whole((1, RET_VW)), whole((1, RET_VW)),
    ]


def _ret_cols(h):
    q = slice(OFF_RQ + h * RET_DK, OFF_RQ + (h + 1) * RET_DK)
    k = slice(OFF_RK + h * RET_DK, OFF_RK + (h + 1) * RET_DK)
    v = slice(OFF_RV + h * RET_DV, OFF_RV + (h + 1) * RET_DV)
    g = slice(OFF_RG + h * RET_DV, OFF_RG + (h + 1) * RET_DV)
    return q, k, v, g, slice(h * RET_DV, (h + 1) * RET_DV)


def _ret_fwd(proj, tables, gn_g, gn_b, after):
    inner, qd, kd, cd, cos, sin = tables

    def body(x_ref, cos_ref, sin_ref, in_ref, qd_ref, kd_ref, cd_ref, g_ref, b_ref, after_ref,
             gated_ref, ro_ref, st_ref, s_scr):
        i = pl.program_id(0)

        @pl.when(i == 0)
        def _():
            s_scr[...] = jnp.zeros_like(s_scr)

        cosv, sinv = cos_ref[...], sin_ref[...]
        for h in range(RET_HEADS):
            cq, ck, cv, cg, co = _ret_cols(h)
            q = _rot(x_ref[:, cq], cosv, sinv)
            k = _rot(x_ref[:, ck], cosv, sinv) * (RET_DK ** -0.5)
            v = x_ref[:, cv]
            st = s_scr[h]
            st_ref[h] = st.astype(BF16)
            s = _dot(q, k, NT) * in_ref[h]
            o = _dot(s, v, NN) + _dot(q, st, NN) * qd_ref[h]
            s_scr[h] = st * cd_ref[h, :, :1] + _dot(k * kd_ref[h], v, TN)
            ro_ref[:, co] = o
            mu = jnp.mean(o, axis=-1, keepdims=True)
            oc = o - mu
            var = jnp.mean(oc * oc, axis=-1, keepdims=True)
            rn = oc * lax.rsqrt(var + GN_EPS) * g_ref[:, co] + b_ref[:, co]
            rg = x_ref[:, cg]
            gated_ref[:, co] = (rg * jax.nn.sigmoid(rg) * rn).astype(BF16)

    ospec = pl.BlockSpec((CHUNK, RET_VW), lambda t: (t, 0))
    return _pcall(
        body, name="ret_fwd", grid=(N_CHUNK,), in_specs=_ret_specs(lambda t: t) + [HBM_SPEC],
        out_specs=(ospec, ospec, pl.BlockSpec((RET_HEADS, None, RET_DK, RET_DV), lambda t: (0, t, 0, 0))),
        out_shape=(jax.ShapeDtypeStruct((S, RET_VW), BF16), jax.ShapeDtypeStruct((S, RET_VW), F32),
                   jax.ShapeDtypeStruct((RET_HEADS, N_CHUNK, RET_DK, RET_DV), BF16)),
        scratch_shapes=[pltpu.VMEM((RET_HEADS, RET_DK, RET_DV), F32)],
        compiler_params=_params(("arbitrary",)))(proj, cos, sin, inner, qd, kd, cd, gn_g, gn_b, after)


def _ret_bwd(proj, tables, gn_g, gn_b, ro, states, dgated, others):
    inner, qd, kd, cd, cos, sin = tables
    last = N_CHUNK - 1
    pieces = lambda o: [(h, o.shape[2]) for h in range(o.shape[0])] if len(o.shape) == 3 else [(None, o.shape[1])]
    assert RET_COLS + sum(w for o in others for _, w in pieces(o)) == IN_COLS

    def body(x_ref, cos_ref, sin_ref, in_ref, qd_ref, kd_ref, cd_ref, g_ref, b_ref, ro_ref, st_ref, dg_ref, *rest):
        other_refs, (dx_ref, gg_ref, gb_ref, gs_scr) = rest[:len(others)], rest[len(others):]
        t = pl.program_id(0)
        col = RET_COLS
        for o_ref in other_refs:
            for h, w in pieces(o_ref):
                dx_ref[:, col:col + w] = (o_ref[...] if h is None else o_ref[h]).astype(BF16)
                col += w

        @pl.when(t == 0)
        def _():
            gs_scr[...] = jnp.zeros_like(gs_scr)
            gg_ref[...] = jnp.zeros_like(gg_ref)
            gb_ref[...] = jnp.zeros_like(gb_ref)

        cosv, sinv = cos_ref[...], sin_ref[...]
        for h in range(RET_HEADS):
            cq, ck, cv, cg, co = _ret_cols(h)
            q = _rot(x_ref[:, cq], cosv, sinv)
            k = _rot(x_ref[:, ck], cosv, sinv) * (RET_DK ** -0.5)
            v = x_ref[:, cv]
            qdv, kdv, dm = qd_ref[h], kd_ref[h], in_ref[h]
            st = st_ref[h]
            o = ro_ref[:, co]
            gv = g_ref[:, co]
            mu = jnp.mean(o, axis=-1, keepdims=True)
            oc = o - mu
            rstd = lax.rsqrt(jnp.mean(oc * oc, axis=-1, keepdims=True) + GN_EPS)
            ohat = oc * rstd
            rn = ohat * gv + b_ref[:, co]
            rg = x_ref[:, cg]
            sg = jax.nn.sigmoid(rg)
            dgt = dg_ref[:, co]
            drn = dgt * (rg * sg)
            dx_ref[:, cg] = (dgt * rn * (sg * (1.0 + rg * (1.0 - sg)))).astype(BF16)
            gg_ref[:, co] += jnp.sum(drn * ohat, axis=0, keepdims=True)
            gb_ref[:, co] += jnp.sum(drn, axis=0, keepdims=True)
            dohat = drn * gv
            do = rstd * (dohat - jnp.mean(dohat, axis=-1, keepdims=True)
                         - ohat * jnp.mean(dohat * ohat, axis=-1, keepdims=True))
            gs = gs_scr[h]
            s = _dot(q, k, NT) * dm
            dsr = _dot(do, v, NT) * dm
            dq = _dot(dsr, k, NN) + _dot(do, st, NT) * qdv
            dk = _dot(dsr, q, TN) + _dot(v, gs, NT) * kdv
            dv = _dot(s, do, TN) + _dot(k * kdv, gs, NN)
            gs_scr[h] = gs * cd_ref[h, :, :1] + _dot(q * qdv, do, TN)
            dx_ref[:, cq] = _rot_t(dq, cosv, sinv).astype(BF16)
            dx_ref[:, ck] = (_rot_t(dk, cosv, sinv) * (RET_DK ** -0.5)).astype(BF16)
            dx_ref[:, cv] = dv.astype(BF16)

    rev = lambda t: last - t
    vblk = pl.BlockSpec((CHUNK, RET_VW), lambda t: (rev(t), 0))
    vspec = pl.BlockSpec((1, RET_VW), lambda t: (0, 0))
    rows = lambda w: pl.BlockSpec((CHUNK, w), lambda t: (rev(t), 0))
    return _pcall(
        body, name="ret_bwd", grid=(N_CHUNK,),
        in_specs=_ret_specs(rev) + [vblk, pl.BlockSpec((RET_HEADS, None, RET_DK, RET_DV), lambda t: (0, rev(t), 0, 0)),
                                    vblk] + [rows(o.shape[1]) if o.ndim == 2 else
                                             pl.BlockSpec((o.shape[0], CHUNK, o.shape[2]), lambda t: (0, rev(t), 0))
                                             for o in others],
        out_specs=(rows(IN_COLS), vspec, vspec),
        out_shape=(jax.ShapeDtypeStruct((S, IN_COLS), BF16), jax.ShapeDtypeStruct((1, RET_VW), F32),
                   jax.ShapeDtypeStruct((1, RET_VW), F32)),
        scratch_shapes=[pltpu.VMEM((RET_HEADS, RET_DK, RET_DV), F32)],
        compiler_params=_params(("arbitrary",)))(proj, cos, sin, inner, qd, kd, cd, gn_g, gn_b, ro, states, dgated,
                                                 *others)


def _bucket_tables():
    qi = np.arange(ATT_BLK)[:, None]
    kj = np.arange(2 * ATT_BLK)[None, :]
    m = ATT_BLK + qi - kj
    out = []
    for win, dil in ATT_GROUPS:
        w = win // dil
        dist = (np.clip(m, 0, w) * dil).astype(np.int32)
        max_exact = N_BUCKETS // 2
        d_f = np.maximum(dist, 1).astype(np.float32)
        large = max_exact + (np.log(d_f / np.float32(max_exact)) / np.float32(math.log(MAX_DIST / max_exact))
                             * np.float32(N_BUCKETS - max_exact)).astype(np.int32)
        large = np.minimum(large, N_BUCKETS - 1)
        out.append(np.where(dist < max_exact, dist, large).astype(np.int32))
    return np.stack(out)


def _bias_build(rel_bias, buckets, after):
    def body(tab_ref, bk_ref, after_ref, o_ref):
        hh = pl.program_id(0)
        bk = bk_ref[...]
        acc = jnp.zeros((ATT_BLK, 2 * ATT_BLK), F32)
        for b in range(N_BUCKETS):
            acc = jnp.where(bk == b, tab_ref[b, hh], acc)
        o_ref[...] = acc

    nh = len(ATT_GROUPS) * ATT_HG
    return _pcall(body, name="bias_build", grid=(nh,),
                  in_specs=[pl.BlockSpec(memory_space=pltpu.SMEM),
                            pl.BlockSpec((None, ATT_BLK, 2 * ATT_BLK), lambda hh: (hh // ATT_HG, 0, 0)), HBM_SPEC],
                  out_specs=pl.BlockSpec((None, ATT_BLK, 2 * ATT_BLK), lambda hh: (hh, 0, 0)),
                  out_shape=jax.ShapeDtypeStruct((nh, ATT_BLK, 2 * ATT_BLK), F32),
                  compiler_params=_params(("parallel",)))(rel_bias, buckets, after)


def _bias_grad(ds_sum, buckets):
    def body(ds_ref, bk_ref, o_ref):
        bk = bk_ref[...]
        ds = ds_ref[...]
        rows = lax.broadcasted_iota(jnp.int32, (N_BUCKETS, 128), 0)
        acc = jnp.zeros((N_BUCKETS, 128), F32)
        for b in range(N_BUCKETS):
            acc = jnp.where(rows == b, jnp.sum(jnp.where(bk == b, ds, 0.0)), acc)
        o_ref[...] = acc

    nh = len(ATT_GROUPS) * ATT_HG
    return _pcall(body, name="bias_grad", grid=(nh,),
                  in_specs=[pl.BlockSpec((None, ATT_BLK, 2 * ATT_BLK), lambda hh: (hh, 0, 0)),
                            pl.BlockSpec((None, ATT_BLK, 2 * ATT_BLK), lambda hh: (hh // ATT_HG, 0, 0))],
                  out_specs=pl.BlockSpec((None, N_BUCKETS, 128), lambda hh: (hh, 0, 0)),
                  out_shape=jax.ShapeDtypeStruct((nh, N_BUCKETS, 128), F32),
                  compiler_params=_params(("parallel",)))(ds_sum, buckets)


def _att_valid(n):
    qi = lax.broadcasted_iota(jnp.int32, (ATT_BLK, 2 * ATT_BLK), 0)
    kj = lax.broadcasted_iota(jnp.int32, (ATT_BLK, 2 * ATT_BLK), 1)
    m = ATT_BLK + qi - kj
    first_key = jnp.where(n > 0, 0, ATT_BLK)
    return (m >= 0) & (m <= ATT_BLK) & (kj >= first_key)


ATT_HP = (1, 2, 2)


def _att_geometry(gi):
    _, dil = ATT_GROUPS[gi]
    return dil, S // dil // ATT_BLK, ATT_HP[gi]


def _blk(dil, r, n):
    if dil == 1:
        return pl.ds(n * ATT_BLK, ATT_BLK)
    return pl.ds(r + n * ATT_BLK * dil, ATT_BLK, stride=dil)


def _slab_specs(gi):
    _, _, hp = _att_geometry(gi)
    per = ATT_HG // hp
    return [pl.BlockSpec((hp, S, ATT_DH), lambda g, r, part=part: ((3 * gi + part) * per + g, 0, 0))
            for part in range(3)]


def _head_specs(gi, count):
    _, _, hp = _att_geometry(gi)
    return [pl.BlockSpec((hp, S, ATT_DH), lambda g, r: (g, 0, 0))] * count


def _bias_spec(gi):
    _, _, hp = _att_geometry(gi)
    return pl.BlockSpec((hp, ATT_BLK, 2 * ATT_BLK), lambda g, r: (gi * (ATT_HG // hp) + g, 0, 0))


def _att_valid_first():
    qi = lax.broadcasted_iota(jnp.int32, (ATT_BLK, ATT_BLK), 0)
    kj = lax.broadcasted_iota(jnp.int32, (ATT_BLK, ATT_BLK), 1)
    return kj <= qi


def _att_fwd(slabs, bias, gi, comm=None):
    dil, nb, hp = _att_geometry(gi)
    scale = ATT_DH ** -0.5

    def body(q_ref, k_ref, v_ref, bias_ref, o_ref, l_ref):
        r = pl.program_id(1)
        for n in range(nb):
            cur = _blk(dil, r, n)
            valid = _att_valid(n) if n > 0 else _att_valid_first()
            for h in range(hp):
                if n > 0:
                    prev = _blk(dil, r, n - 1)
                    kk = jnp.concatenate([k_ref[h, prev, :], k_ref[h, cur, :]], axis=0)
                    vv = jnp.concatenate([v_ref[h, prev, :], v_ref[h, cur, :]], axis=0)
                    bias = bias_ref[h]
                else:
                    kk, vv, bias = k_ref[h, cur, :], v_ref[h, cur, :], bias_ref[h, :, pl.ds(ATT_BLK, ATT_BLK)]
                s = _dot(q_ref[h, cur, :], kk, NT) * scale + bias
                s = jnp.where(valid, s, -1e30)
                mx = jnp.max(s, axis=-1, keepdims=True)
                e = jnp.exp(s - mx)
                den = jnp.sum(e, axis=-1, keepdims=True)
                o_ref[h, cur, :] = _dot(e / den, vv, NN)
                l_ref[h, cur, :] = jnp.broadcast_to(mx + jnp.log(den), (ATT_BLK, ATT_DH))

    osh = jax.ShapeDtypeStruct((ATT_HG, S, ATT_DH), F32)
    kw = dict(name=f"att_fwd{gi}", grid=(ATT_HG // hp, dil), in_specs=_slab_specs(gi) + [_bias_spec(gi)],
              out_specs=tuple(_head_specs(gi, 2)), out_shape=(osh, osh))
    if comm is not None:
        return _carry(body, comm, **kw)(slabs, slabs, slabs, bias)
    return _pcall(body, compiler_params=_params(("parallel", "arbitrary")), **kw)(slabs, slabs, slabs, bias)


def _att_bwd(slabs, bias, o, lse, do, dlse, gi, comm=None):
    dil, nb, hp = _att_geometry(gi)
    per = ATT_HG // hp
    scale = ATT_DH ** -0.5
    wide = lambda t: jnp.concatenate([t, t], axis=1)

    def body(q_ref, k_ref, v_ref, bias_ref, o_ref, l_ref, do_ref, dl_ref, dq_ref, dk_ref, dv_ref, ds_ref):
        r = pl.program_id(1)

        @pl.when(r == 0)
        def _():
            ds_ref[...] = jnp.zeros_like(ds_ref)

        for h in range(hp):
            carry_k = carry_v = None
            for n in range(nb):
                cur = _blk(dil, r, n)
                q = q_ref[h, cur, :]
                dov = do_ref[h, cur, :]
                delta = jnp.sum(dov * o_ref[h, cur, :], axis=-1, keepdims=True)
                if n == 0:
                    own = pl.ds(ATT_BLK, ATT_BLK)
                    kk, vv = k_ref[h, cur, :], v_ref[h, cur, :]
                    s = _dot(q, kk, NT) * scale + bias_ref[h, :, own]
                    p = jnp.where(_att_valid_first(), jnp.exp(s - l_ref[h, cur, :]), 0.0)
                    ds = p * (_dot(dov, vv, NT) - delta + dl_ref[h, cur, :])
                    ds_ref[h, :, own] += ds
                    dq_ref[h, cur, :] = _dot(ds, kk, NN) * scale
                    carry_k, carry_v = _dot(ds, q, TN) * scale, _dot(p, dov, TN)
                    continue
                prev = _blk(dil, r, n - 1)
                kk = jnp.concatenate([k_ref[h, prev, :], k_ref[h, cur, :]], axis=0)
                vv = jnp.concatenate([v_ref[h, prev, :], v_ref[h, cur, :]], axis=0)
                s = _dot(q, kk, NT) * scale + bias_ref[h]
                p = jnp.where(_att_valid(n), jnp.exp(s - wide(l_ref[h, cur, :])), 0.0)
                dp = _dot(dov, vv, NT)
                ds = p * (dp - delta + wide(dl_ref[h, cur, :]))
                ds_ref[h] += ds
                dq_ref[h, cur, :] = _dot(ds, kk, NN) * scale
                dkk = _dot(ds, q, TN) * scale
                dvv = _dot(p, dov, TN)
                dk_ref[h, prev, :] = carry_k + dkk[:ATT_BLK]
                dv_ref[h, prev, :] = carry_v + dvv[:ATT_BLK]
                carry_k, carry_v = dkk[ATT_BLK:], dvv[ATT_BLK:]
            last = _blk(dil, r, nb - 1)
            dk_ref[h, last, :] = carry_k
            dv_ref[h, last, :] = carry_v

    osh = jax.ShapeDtypeStruct((ATT_HG, S, ATT_DH), F32)
    kw = dict(name=f"att_bwd{gi}", grid=(per, dil), in_specs=_slab_specs(gi) + [_bias_spec(gi)] + _head_specs(gi, 4),
              out_specs=(*_head_specs(gi, 3), pl.BlockSpec((hp, ATT_BLK, 2 * ATT_BLK), lambda g, r: (g, 0, 0))),
              out_shape=(osh, osh, osh, jax.ShapeDtypeStruct((ATT_HG, ATT_BLK, 2 * ATT_BLK), F32)))
    args = (slabs, slabs, slabs, bias, o, lse, do, dlse)
    if comm is not None:
        return _carry(body, comm, **kw)(*args)
    return _pcall(body, compiler_params=_params(("arbitrary", "arbitrary")), **kw)(*args)


AW = ATT_HG * ATT_DH


def _mix_weights(l0, l1, l2):
    mx = jnp.maximum(jnp.maximum(l0, l1), l2)
    e0, e1, e2 = jnp.exp(l0 - mx), jnp.exp(l1 - mx), jnp.exp(l2 - mx)
    den = e0 + e1 + e2
    return e0 / den, e1 / den, e2 / den


def _heads_spec():
    return pl.BlockSpec((ATT_HG, TR, ATT_DH), lambda i: (0, i, 0))


def _mix_fwd(os_, ls, comm=None):
    def body(o0, o1, o2, l0, l1, l2, att_ref):
        for h in range(ATT_HG):
            w0, w1, w2 = _mix_weights(l0[h], l1[h], l2[h])
            att_ref[:, h * ATT_DH:(h + 1) * ATT_DH] = (w0 * o0[h] + w1 * o1[h] + w2 * o2[h]).astype(BF16)

    kw = dict(name="mix_fwd", grid=(S // TR,), in_specs=[_heads_spec()] * 6, out_specs=_row_spec(AW),
              out_shape=jax.ShapeDtypeStruct((S, AW), BF16))
    if comm is not None:
        return _carry(body, comm, **kw)(*os_, *ls)
    return _pcall(body, compiler_params=_params(("parallel",)), **kw)(*os_, *ls)


def _mix_bwd(os_, ls, datt):
    def body(o0, o1, o2, l0, l1, l2, da_ref, d0, d1, d2, e0, e1, e2):
        for h in range(ATT_HG):
            ws = _mix_weights(l0[h], l1[h], l2[h])
            da = da_ref[:, h * ATT_DH:(h + 1) * ATT_DH]
            dws = []
            for o_ref, w, d_ref in zip((o0, o1, o2), ws, (d0, d1, d2)):
                d_ref[h] = w * da
                dws.append(jnp.broadcast_to(jnp.sum(da * o_ref[h], axis=-1, keepdims=True), (TR, ATT_DH)))
            tot = ws[0] * dws[0] + ws[1] * dws[1] + ws[2] * dws[2]
            for w, dw, e_ref in zip(ws, dws, (e0, e1, e2)):
                e_ref[h] = w * (dw - tot)

    o = jax.ShapeDtypeStruct((ATT_HG, S, ATT_DH), F32)
    return _pcall(body, name="mix_bwd", grid=(S // TR,), in_specs=[_heads_spec()] * 6 + [_row_spec(AW)],
                  out_specs=(_heads_spec(),) * 6, out_shape=(o,) * 6,
                  compiler_params=_params(("parallel",)))(*os_, *ls, datt)


def _ada_fwd(c_all, w_sh, b_sl):
    def body(c_ref, w_ref, b_ref, o_ref):
        cv = c_ref[...]
        o_ref[...] = _dot(cv * jax.nn.sigmoid(cv), w_ref[...], NN) + b_ref[...]

    return _pcall(body, name="ada_fwd", out_shape=jax.ShapeDtypeStruct((N_DEV, w_sh.shape[1]), F32),
                  compiler_params=_params())(c_all, w_sh, b_sl)


CAST_STEPS = 4


def _to_bf16(arrs, comm, name):
    n = len(arrs)

    def body(*refs):
        for src, dst in zip(refs[:n], refs[n:]):
            dst[...] = src[...].astype(BF16)

    blocks = [pl.BlockSpec((a.shape[0] // CAST_STEPS, a.shape[1]), lambda i: (i, 0)) for a in arrs]
    return _carry(body, comm, name=name, grid=(CAST_STEPS,), in_specs=blocks, out_specs=tuple(blocks),
                  out_shape=tuple(pltpu.HBM(a.shape, BF16) for a in arrs))(*arrs)


def _ada_bwd(c_all, dm_sl):
    def body(c_ref, d_ref, o_ref):
        cv = c_ref[...]
        o_ref[...] = _dot(cv * jax.nn.sigmoid(cv), d_ref[...], TN)

    return _pcall(body, name="ada_bwd", out_shape=jax.ShapeDtypeStruct((D, dm_sl.shape[1]), F32),
                  compiler_params=_params())(c_all, dm_sl)


N_MOD = 6


def _sum_small(gathered):
    n = len(gathered)

    def body(*refs):
        ins, (gb_ref, dm_ref), outs = refs[:n], refs[n:n + 2], refs[n + 2:]

        def total(r):
            acc = r[0]
            for e in range(1, N_DEV):
                acc = acc + r[e]
            return acc

        for i in range(N_MOD):
            cols = slice(i * D, (i + 1) * D)
            gb_ref[:, cols] = total(ins[i])
            for e in range(N_DEV):
                dm_ref[e:e + 1, cols] = ins[i][e]
        for r, o_ref in zip(ins[N_MOD:], outs):
            o_ref[...] = total(r)

    shapes = (jax.ShapeDtypeStruct((1, N_MOD * D), F32), jax.ShapeDtypeStruct((N_DEV, N_MOD * D), F32),
              *[jax.ShapeDtypeStruct(g.shape[1:], F32) for g in gathered[N_MOD:]])
    res = _pcall(body, name="sum_small", out_shape=shapes, compiler_params=_params())(*gathered)
    return res[0], res[1], res[2:]


def _row_tile(m, n):
    t = max(8, min(m, (1 << 19) // n // 8 * 8))
    while m % t:
        t -= 8
    return t


def _pair_sum(full, recv, sel, name, col_block=0):
    _, m, n = recv.shape
    t = _row_tile(m, n)

    def body(sel_ref, a_ref, b_ref, o_ref):
        o_ref[...] = (a_ref[...].astype(F32) + b_ref[...].astype(F32)).astype(o_ref.dtype)

    gs = pltpu.PrefetchScalarGridSpec(
        num_scalar_prefetch=1, grid=(4, m // t),
        in_specs=[pl.BlockSpec((None, None, t, n), lambda q, i, s: (q, s[0], i, col_block)),
                  pl.BlockSpec((None, t, n), lambda q, i, s: (q, i, 0))],
        out_specs=pl.BlockSpec((None, t, n), lambda q, i, s: (q, i, 0)))
    return _pcall(body, name=name, grid_spec=gs, out_shape=jax.ShapeDtypeStruct((4, m, n), full.dtype),
                  compiler_params=_params(("parallel", "parallel")))(sel, full, recv)


def _chip_sum(part, recv, sel, name):
    _, m, n = part.shape
    t = _row_tile(m, n)

    def body(sel_ref, a_ref, r_ref, o_ref):
        o_ref[...] = ((a_ref[...].astype(F32) + r_ref[0].astype(F32)) + r_ref[1].astype(F32)) + r_ref[2].astype(F32)

    gs = pltpu.PrefetchScalarGridSpec(
        num_scalar_prefetch=1, grid=(m // t,),
        in_specs=[pl.BlockSpec((None, t, n), lambda i, s: (s[0], i, 0)),
                  pl.BlockSpec((3, t, n), lambda i, s: (0, i, 0))],
        out_specs=pl.BlockSpec((t, n), lambda i, s: (i, 0)))
    return _pcall(body, name=name, grid_spec=gs, out_shape=jax.ShapeDtypeStruct((m, n), F32),
                  compiler_params=_params(("parallel",)))(sel, part, recv)


def _adamw_math(w, g, m, v):
    nm = ADAM_B1 * m + (1.0 - ADAM_B1) * g
    nv = ADAM_B2 * v + (1.0 - ADAM_B2) * (g * g)
    m_hat = nm / (1.0 - ADAM_B1 ** ADAM_STEP)
    v_hat = nv / (1.0 - ADAM_B2 ** ADAM_STEP)
    return -ADAM_LR * (m_hat / (jnp.sqrt(v_hat) + ADAM_EPS) + ADAM_WD * w), nm, nv


def _adamw(w, g, m, v, name):
    _, rows, cols = w.shape
    t = _row_tile(rows, cols)

    def body(w_ref, g_ref, m_ref, v_ref, d_ref, nm_ref, nv_ref):
        d_ref[...], nm_ref[...], nv_ref[...] = _adamw_math(w_ref[...], g_ref[...], m_ref[...], v_ref[...])

    spec3 = pl.BlockSpec((None, t, cols), lambda i: (0, i, 0))
    spec2 = pl.BlockSpec((t, cols), lambda i: (i, 0))
    o = jax.ShapeDtypeStruct(w.shape, F32)
    return _pcall(body, name=name, grid=(rows // t,), in_specs=[spec3, spec2, spec3, spec3], out_specs=(spec3,) * 3,
                  out_shape=(o, o, o), compiler_params=_params(("parallel",)))(w, g, m, v)


def _adamw_reduced1(w, m, v, part, recv, sel, name):
    _, rows, cols = w.shape
    t = _row_tile(rows, cols)

    def body(sel_ref, w_ref, m_ref, v_ref, p_ref, r_ref, g_ref, d_ref, nm_ref, nv_ref):
        g = ((p_ref[...].astype(F32) + r_ref[0].astype(F32)) + r_ref[1].astype(F32)) + r_ref[2].astype(F32)
        g_ref[...] = g
        d_ref[...], nm_ref[...], nv_ref[...] = _adamw_math(w_ref[...], g, m_ref[...], v_ref[...])

    wspec = pl.BlockSpec((None, t, cols), lambda i, s: (0, i, 0))
    gs = pltpu.PrefetchScalarGridSpec(
        num_scalar_prefetch=1, grid=(rows // t,),
        in_specs=[wspec, wspec, wspec, pl.BlockSpec((None, t, cols), lambda i, s: (s[0], i, 0)),
                  pl.BlockSpec((3, t, cols), lambda i, s: (0, i, 0))],
        out_specs=(wspec,) * 4)
    o = jax.ShapeDtypeStruct(w.shape, F32)
    return _pcall(body, name=name, grid_spec=gs, out_shape=(o, o, o, o),
                  compiler_params=_params(("parallel",)))(sel, w, m, v, part, recv)


def _adamw_reduced(w, m, v, parts, recvs, sel):
    _, rows, cols = w.shape
    half = cols // 2
    t = _row_tile(rows, half)

    def body(sel_ref, w_ref, m_ref, v_ref, pa_ref, pb_ref, ra_ref, rb_ref, g_ref, d_ref, nm_ref, nv_ref):
        total = lambda p_ref, r_ref: ((p_ref[...].astype(F32) + r_ref[0].astype(F32)) + r_ref[1].astype(F32)) \
            + r_ref[2].astype(F32)
        g = jnp.where(pl.program_id(1) == 0, total(pa_ref, ra_ref), total(pb_ref, rb_ref))
        g_ref[...] = g
        d_ref[...], nm_ref[...], nv_ref[...] = _adamw_math(w_ref[...], g, m_ref[...], v_ref[...])

    wspec = pl.BlockSpec((None, t, half), lambda i, j, s: (0, i, j))
    pspec = pl.BlockSpec((None, t, half), lambda i, j, s: (s[0], i, 0))
    rspec = pl.BlockSpec((3, t, half), lambda i, j, s: (0, i, 0))
    gs = pltpu.PrefetchScalarGridSpec(num_scalar_prefetch=1, grid=(rows // t, 2),
                                      in_specs=[wspec, wspec, wspec, pspec, pspec, rspec, rspec],
                                      out_specs=(wspec,) * 4)
    o = jax.ShapeDtypeStruct(w.shape, F32)
    return _pcall(body, name="adamw_w_in", grid_spec=gs, out_shape=(o, o, o, o),
                  compiler_params=_params(("parallel", "arbitrary")))(sel, w, m, v, *parts, *recvs)


def _adamw_small(ws, gs, ms, vs):
    n = len(ws)

    def body(*refs):
        for i in range(n):
            w_ref, g_ref, m_ref, v_ref = (refs[k * n + i] for k in range(4))
            d, nm, nv = _adamw_math(w_ref[...], g_ref[...], m_ref[...], v_ref[...])
            refs[4 * n + i][...] = d
            refs[5 * n + i][...] = nm
            refs[6 * n + i][...] = nv

    shapes = tuple(jax.ShapeDtypeStruct(w.shape, F32) for w in ws)
    res = _pcall(body, name="adamw_small", out_shape=shapes * 3, compiler_params=_params())(*ws, *gs, *ms, *vs)
    return res[:n], res[n:2 * n], res[2 * n:]


def _mesh_pos():
    return lax.axis_index("x"), lax.axis_index("y"), lax.axis_index("c")


class _Gather:
    def __init__(self, arrs):
        self.ins = list(arrs)
        self.out_shape = tuple(jax.ShapeDtypeStruct((N_DEV,) + a.shape, a.dtype) for a in arrs)
        n = len(arrs)
        self.sems = [pltpu.SemaphoreType.DMA((7 * n,)), pltpu.SemaphoreType.DMA((7 * n,)),
                     pltpu.SemaphoreType.DMA((n,))]

    def _copies(self, ins, outs, sems):
        send_sems, recv_sems, local_sems = sems
        x, y, c = _mesh_pos()
        me, sibling = (x, y, c), (x, y, 1 - c)
        chips = [(1 - x, y), (x, 1 - y), (1 - x, 1 - y)]

        def copy(p, k, block, to, from_input=False):
            dst = outs[p].at[_slot(block)]
            return pltpu.make_async_remote_copy(
                src_ref=ins[p] if from_input else dst, dst_ref=dst, send_sem=send_sems.at[7 * p + k],
                recv_sem=recv_sems.at[7 * p + k], device_id=to, device_id_type=MESH)

        npc = len(self.ins)
        mine = [pltpu.make_async_copy(ins[p], outs[p].at[_slot(me)], local_sems.at[p]) for p in range(npc)]
        first = []
        for p in range(npc):
            first.append(copy(p, 0, me, sibling, from_input=True))
            first += [copy(p, 1 + j, me, (*chip, c), from_input=True) for j, chip in enumerate(chips)]
        return me, sibling, chips, c, copy, mine, first

    def start(self, ins, outs, sems):
        *_, mine, first = self._copies(ins, outs, sems)
        for cp in mine + first:
            cp.start()

    def finish(self, ins, outs, sems):
        me, sibling, chips, c, copy, mine, first = self._copies(ins, outs, sems)
        npc = len(self.ins)
        passed = []
        for p in range(npc):
            for j, chip in enumerate(chips):
                copy(p, 1 + j, (*chip, c), me).wait_recv()
                passed.append(copy(p, 4 + j, (*chip, c), sibling))
                passed[-1].start()
        for p in range(npc):
            copy(p, 0, sibling, me).wait_recv()
            for j, chip in enumerate(chips):
                copy(p, 4 + j, (*chip, 1 - c), me).wait_recv()
        for cp in first + passed:
            cp.wait_send()
        for cp in mine:
            cp.wait()


class _ExchangeCore:
    def __init__(self, fulls, cols=None):
        self.ins = list(fulls)
        self.cols = cols
        width = lambda f: f.shape[3] if cols is None else cols[1]
        self.out_shape = tuple(jax.ShapeDtypeStruct((4, f.shape[2], width(f)), f.dtype) for f in fulls)
        self.sems = [pltpu.SemaphoreType.DMA((4 * len(fulls),)), pltpu.SemaphoreType.DMA((4 * len(fulls),))]

    def _copies(self, ins, outs, sems):
        send_sems, recv_sems = sems
        x, y, c = _mesh_pos()

        def src(a, q):
            ref = ins[a].at[q, 1 - c]
            return ref if self.cols is None else ref.at[:, pl.ds(*self.cols)]

        return [pltpu.make_async_remote_copy(
            src_ref=src(a, q), dst_ref=outs[a].at[q], send_sem=send_sems.at[4 * a + q],
            recv_sem=recv_sems.at[4 * a + q], device_id=(x, y, 1 - c), device_id_type=MESH)
            for a in range(len(self.ins)) for q in range(4)]

    def start(self, ins, outs, sems):
        for cp in self._copies(ins, outs, sems):
            cp.start()

    def finish(self, ins, outs, sems):
        for cp in self._copies(ins, outs, sems):
            cp.wait()


class _ExchangeChip:
    def __init__(self, parts):
        self.ins = list(parts)
        self.out_shape = tuple(jax.ShapeDtypeStruct((3,) + p.shape[1:], p.dtype) for p in parts)
        self.sems = [pltpu.SemaphoreType.DMA((3 * len(parts),)), pltpu.SemaphoreType.DMA((3 * len(parts),))]

    def _copies(self, ins, outs, sems):
        send_sems, recv_sems = sems
        x, y, c = _mesh_pos()
        chips = [(1 - x, y), (x, 1 - y), (1 - x, 1 - y)]
        return [pltpu.make_async_remote_copy(
            src_ref=ins[a].at[2 * px + py], dst_ref=outs[a].at[j], send_sem=send_sems.at[3 * a + j],
            recv_sem=recv_sems.at[3 * a + j], device_id=(px, py, c), device_id_type=MESH)
            for a in range(len(self.ins)) for j, (px, py) in enumerate(chips)]

    def start(self, ins, outs, sems):
        for cp in self._copies(ins, outs, sems):
            cp.start()

    def finish(self, ins, outs, sems):
        for cp in self._copies(ins, outs, sems):
            cp.wait()


HBM_ONLY = pl.BlockSpec(memory_space=pltpu.HBM)
SEM_SPEC = pl.BlockSpec(memory_space=pltpu.SEMAPHORE)
SIDE_EFFECT = pltpu.SideEffectType.DATAFLOW_SIDE_EFFECTING


def _chip_copies(p_refs, land_refs, send_sems, recv_sems):
    x, y, c = _mesh_pos()
    return [pltpu.make_async_remote_copy(
        src_ref=p_refs[a].at[2 * px + py], dst_ref=land_refs[a].at[j], send_sem=send_sems.at[3 * a + j],
        recv_sem=recv_sems.at[3 * a + j], device_id=(px, py, c), device_id_type=MESH)
        for a in range(len(p_refs)) for j, (px, py) in enumerate([(1 - x, y), (x, 1 - y), (1 - x, 1 - y)])]


def _chip_exchange_start(parts, name):
    n = len(parts)
    lands = [lax.empty((3,) + p.shape[1:], p.dtype) for p in parts]

    def body(*refs):
        p_refs, land_refs, (send_sems, recv_sems) = refs[:n], refs[n:2 * n], refs[2 * n:2 * n + 2]
        for cp in _chip_copies(p_refs, land_refs, send_sems, recv_sems):
            cp.start()
        token = refs[-1]
        token[...] = jnp.zeros_like(token)

    hbm = lambda t: pltpu.HBM(t.shape, t.dtype)
    res = pl.pallas_call(
        body, name=name,
        out_shape=(pltpu.SemaphoreType.DMA((3 * n,)), pltpu.SemaphoreType.DMA((3 * n,)), *[hbm(t) for t in parts + lands],
                   jax.ShapeDtypeStruct((8, 128), F32)),
        in_specs=(HBM_ONLY,) * (2 * n),
        out_specs=(SEM_SPEC, SEM_SPEC, *[HBM_ONLY] * (2 * n), pl.BlockSpec(memory_space=pltpu.VMEM)),
        input_output_aliases={i: 2 + i for i in range(2 * n)},
        compiler_params=pltpu.CompilerParams(has_side_effects=SIDE_EFFECT))(
        *[pltpu.with_memory_space_constraint(t, pltpu.HBM) for t in parts + lands])
    return (res[0], res[1], list(res[2:2 + n]), list(res[2 + n:2 + 2 * n])), res[-1]


def _chip_exchange_wait(in_flight, after, name):
    send_sems, recv_sems, parts, lands = in_flight
    n = len(parts)

    def body(*refs):
        p_refs, land_refs, (send_sems, recv_sems) = refs[:n], refs[n:2 * n], refs[2 * n:2 * n + 2]
        for cp in _chip_copies(p_refs, land_refs, send_sems, recv_sems):
            cp.wait_send()
            cp.wait_recv()

    res = pl.pallas_call(
        body, name=name, out_shape=tuple(pltpu.HBM(t.shape, t.dtype) for t in parts + lands),
        in_specs=(*[HBM_ONLY] * (2 * n), SEM_SPEC, SEM_SPEC, pl.BlockSpec(memory_space=pl.ANY)),
        out_specs=(HBM_ONLY,) * (2 * n), input_output_aliases={i: i for i in range(2 * n)},
        compiler_params=pltpu.CompilerParams(has_side_effects=SIDE_EFFECT))(*parts, *lands, send_sems, recv_sems, after)
    return list(res[:n]), list(res[n:])


def _slot(p):
    return 4 * p[0] + 2 * p[1] + p[2]


def _gather_copies(src_refs, out_refs, send_sems, recv_sems):
    x, y, c = _mesh_pos()
    targets = [(x, y, 1 - c), (1 - x, y, c), (x, 1 - y, c), (1 - x, 1 - y, c)]
    return [pltpu.make_async_remote_copy(
        src_ref=src_refs[a], dst_ref=out_refs[a].at[_slot((x, y, c))], send_sem=send_sems.at[4 * a + k],
        recv_sem=recv_sems.at[4 * a + k], device_id=to, device_id_type=MESH)
        for a in range(len(src_refs)) for k, to in enumerate(targets)]


def _gather_start(shards, after, name):
    n = len(shards)
    outs = [lax.empty((N_DEV,) + s.shape, s.dtype) for s in shards]

    def body(*refs):
        for cp in _gather_copies(refs[:n], refs[n:2 * n], refs[2 * n + 1], refs[2 * n + 2]):
            cp.start()
        token = refs[-1]
        token[...] = jnp.zeros_like(token)

    res = pl.pallas_call(
        body, name=name,
        out_shape=(pltpu.SemaphoreType.DMA((4 * n,)), pltpu.SemaphoreType.DMA((4 * n,)),
                   *[pltpu.HBM(t.shape, t.dtype) for t in shards + outs], jax.ShapeDtypeStruct((8, 128), F32)),
        in_specs=(*[HBM_ONLY] * (2 * n), pl.BlockSpec(memory_space=pl.ANY)),
        out_specs=(SEM_SPEC, SEM_SPEC, *[HBM_ONLY] * (2 * n), pl.BlockSpec(memory_space=pltpu.VMEM)),
        input_output_aliases={i: 2 + i for i in range(2 * n)},
        compiler_params=pltpu.CompilerParams(has_side_effects=SIDE_EFFECT))(
        *[pltpu.with_memory_space_constraint(t, pltpu.HBM) for t in shards + outs], after)
    return (res[0], res[1], list(res[2:2 + n]), list(res[2 + n:2 + 2 * n])), res[-1]


def _gather_wait(in_flight, after, name):
    send_sems, recv_sems, shards, outs = in_flight
    n = len(shards)

    def body(*refs):
        for cp in _gather_copies(refs[:n], refs[n:2 * n], refs[2 * n], refs[2 * n + 1]):
            cp.wait_send()
            cp.wait_recv()

    res = pl.pallas_call(
        body, name=name, out_shape=tuple(pltpu.HBM(t.shape, t.dtype) for t in shards + outs),
        in_specs=(*[HBM_ONLY] * (2 * n), SEM_SPEC, SEM_SPEC, pl.BlockSpec(memory_space=pl.ANY)),
        out_specs=(HBM_ONLY,) * (2 * n), input_output_aliases={i: i for i in range(2 * n)},
        compiler_params=pltpu.CompilerParams(has_side_effects=SIDE_EFFECT))(*shards, *outs, send_sems, recv_sems, after)
    return list(res[:n]), list(res[n:])


class _PassToSibling:
    def __init__(self, shards, gathered):
        n = self.n = len(shards)
        self.ins = list(shards) + list(gathered)
        self.out_shape = tuple(jax.ShapeDtypeStruct(g.shape, g.dtype) for g in gathered)
        self.aliases = {n + a: a for a in range(n)}
        self.sems = [pltpu.SemaphoreType.DMA((3 * n,)), pltpu.SemaphoreType.DMA((3 * n,)),
                     pltpu.SemaphoreType.DMA((n,))]

    def _copies(self, ins, outs, sems):
        send_sems, recv_sems, local_sems = sems
        x, y, c = _mesh_pos()
        chips = [(1 - x, y), (x, 1 - y), (1 - x, 1 - y)]
        mine = [pltpu.make_async_copy(ins[a], outs[a].at[_slot((x, y, c))], local_sems.at[a]) for a in range(self.n)]
        passed, awaited = [], []
        for a in range(self.n):
            for j, chip in enumerate(chips):
                sems_j = dict(send_sem=send_sems.at[3 * a + j], recv_sem=recv_sems.at[3 * a + j],
                              device_id=(x, y, 1 - c), device_id_type=MESH)
                blk = outs[a].at[_slot((*chip, c))]
                passed.append(pltpu.make_async_remote_copy(src_ref=blk, dst_ref=blk, **sems_j))
                got = outs[a].at[_slot((*chip, 1 - c))]
                awaited.append(pltpu.make_async_remote_copy(src_ref=got, dst_ref=got, **sems_j))
        return mine, passed, awaited

    def start(self, ins, outs, sems):
        mine, passed, _ = self._copies(ins, outs, sems)
        for cp in mine + passed:
            cp.start()

    def finish(self, ins, outs, sems):
        mine, passed, awaited = self._copies(ins, outs, sems)
        for cp in passed:
            cp.wait_send()
        for cp in awaited:
            cp.wait_recv()
        for cp in mine:
            cp.wait()


def _reduce_sums(fulls, recv_core, core, tag):
    return [_pair_sum(f, r, core, f"rs_pair_{tag}{i}") for i, (f, r) in enumerate(zip(fulls, recv_core))]


def _local_step(x, tgt, mods, w_in_shard, order, shards, small, chip, core):
    sh1, sc1, g1, sh2, sc2, g2 = mods
    norm1_g, rel_bias, gn_g, gn_b, norm2_g, norm_f_g = small
    tables = _ret_tables()
    buckets = jnp.asarray(_bucket_tables())

    proj, slabs, w_in_t, h1 = _gather_proj(x, norm1_g, sh1, sc1, w_in_shard, order)
    flight_w1, token_w = _gather_start(list(shards[:3]), proj, "gather_w1_start")
    flight_w2, token_w = _gather_start(list(shards[3:]), token_w, "gather_w2_start")
    bias = _bias_build(rel_bias, buckets, token_w)
    outs, lses = [], []
    for gi in range(len(ATT_GROUPS)):
        o, l = _att_fwd(slabs, bias, gi)
        outs.append(o)
        lses.append(l)
    att, gathered = _mix_fwd(outs, lses, comm=_PassToSibling(*_gather_wait(flight_w1, lses[2], "gather_w1_wait")))
    w_ret_out, w_att_out, w_o = (_from_slots(g, ax) for g, ax in zip(gathered, BIG_AXES[1:4]))
    gated, ro, states = _ret_fwd(proj, tables, gn_g, gn_b, att)
    ret_out, gathered = _mm(gated, w_ret_out, 'nn', tm=S, tn=256, tk=2048, name="ret_out",
                            comm=_PassToSibling(*_gather_wait(flight_w2, gated, "gather_w2_wait")))
    w_ff1, w_ff2 = (_from_slots(g, ax) for g, ax in zip(gathered, BIG_AXES[4:]))
    att_out, merged = _att_out_merge(att, w_att_out, proj, ret_out)
    mixo, x1, h2 = _w_o_norm2(merged, w_o, x, g1, norm2_g, sh2, sc2)
    u, act = _mm(h2, w_ff1, 'nn', tm=S, tn=512, tk=D, name="ff1", relu2=True)
    loss, dx2, g_normf, df, dg2 = _ff2_final(act, w_ff2, x1, g2, tgt, norm_f_g)

    gw_ff2 = _mm(act, df, 'tn', tm=512, tn=D, tk=S, name="gw_ff2", out_dtype=BF16)
    du = _mm(df, w_ff2, 'nt', tm=S, tn=512, tk=D, name="d_act", out_dtype=BF16, relu2_of=u)
    gw_ff1 = _mm(h2, du, 'tn', tm=D, tn=512, tk=S, name="gw_ff1", out_dtype=BF16)
    fulls_a = [_to_slots(g, ax) for g, ax in zip((gw_ff1, gw_ff2), BIG_AXES[4:])]
    dh2, recv_core_a = _mm(du, w_ff1, 'nt', tm=1024, tn=1024, tk=2048, name="dh2", comm=_ExchangeCore(fulls_a))
    parts_a = _reduce_sums(fulls_a, recv_core_a, core, "a")
    flight_a, token_a = _chip_exchange_start(parts_a, "rs_a_start")
    dx1, dsc2, dsh2, g_norm2, dmixo, dg1 = _norm_mod_bwd(x1, norm2_g, sc2, dh2, dx2, "norm2_bwd", gate=(mixo, g1))

    gw_o = _mm(merged, dmixo, 'tn', tm=D, tn=512, tk=S, name="gw_o", out_dtype=BF16, after=token_a)
    d_ret_out, d_att_out, dga, dgb = _dmerged_split(dmixo, w_o, proj, ret_out, att_out)
    gw_ret_out = _mm(gated, d_ret_out, 'tn', tm=512, tn=D, tk=S, name="gw_ret_out", out_dtype=BF16)
    gw_att_out = _mm(att, d_att_out, 'tn', tm=AW, tn=D, tk=S, name="gw_att_out", out_dtype=BF16)
    fulls_b = [_to_slots(g, ax) for g, ax in zip((gw_ret_out, gw_att_out, gw_o), BIG_AXES[1:4])]
    dgated, recv_core_b = _mm(d_ret_out, w_ret_out, 'nt', tm=S, tn=512, tk=D, name="dgated",
                              comm=_ExchangeCore(fulls_b))
    parts_b = _reduce_sums(fulls_b, recv_core_b, core, "b")
    flight_b, token_b = _chip_exchange_start(parts_b, "rs_b_start")
    datt = _mm(d_att_out, w_att_out, 'nt', tm=S, tn=AW, tk=D, name="datt", after=token_b)
    mix_grads = _mix_bwd(outs, lses, datt)
    datt_parts, ds_sums = [], []
    for gi in range(len(ATT_GROUPS)):
        dq, dk, dv, ds_sum = _att_bwd(slabs, bias, outs[gi], lses[gi], mix_grads[gi], mix_grads[3 + gi], gi)
        datt_parts += [dq, dk, dv]
        ds_sums.append(ds_sum)
    g_bias = _bias_grad(jnp.concatenate(ds_sums, axis=0), buckets)[:, :, 0].T.reshape(1, -1)
    dproj, g_gn_g, g_gn_b = _ret_bwd(proj, tables, gn_g, gn_b, ro, states, dgated, datt_parts + [dga, dgb])
    parts_a, recv_chip_a = _chip_exchange_wait(flight_a, dproj, "rs_a_wait")
    parts_b, recv_chip_b = _chip_exchange_wait(flight_b, dproj, "rs_b_wait")
    reduced = list(zip(parts_b + parts_a, recv_chip_b + recv_chip_a))
    full_in = _to_slots(_mm(dproj, h1, 'tn', tm=512, tn=D, tk=S, name="gw_in", out_dtype=BF16), 0)
    halves = [(half * (D // 2), D // 2) for half in range(2)]
    (recv_core_in0,) = _run_comm(_ExchangeCore([full_in], cols=halves[0]), "rs_core_in0")
    flight0, token = _chip_exchange_start([_pair_sum(full_in, recv_core_in0, core, "rs_pair_c0", col_block=0)],
                                          "rs_in0_start")
    dh1, (recv_core_in1,) = _mm(dproj, w_in_t, 'nn', tm=1024, tn=1024, tk=2560, name="dh1",
                                comm=_ExchangeCore([full_in], cols=halves[1]), after=token)
    flight1, token = _chip_exchange_start([_pair_sum(full_in, recv_core_in1, core, "rs_pair_c1", col_block=1)],
                                          "rs_in1_start")
    in_flight = [flight0, flight1]
    gx, dsc1, dsh1, g_norm1 = _norm_mod_bwd(x, norm1_g, sc1, dh1, dx1, "norm1_bwd", after=token)

    dmod = [dsh1, dsc1, dg1, dsh2, dsc2, dg2]
    small_g = [g_norm1, g_bias, g_gn_g, g_gn_b, g_norm2, g_normf]
    return loss, gx, in_flight, reduced, small_g, dmod


def _to_slots(g, axis):
    if axis == 0:
        return g.reshape(4, 2, g.shape[0] // N_DEV, g.shape[1])
    return g.reshape(g.shape[0], N_DEV, g.shape[1] // N_DEV).transpose(1, 0, 2).reshape(4, 2, g.shape[0], -1)


def _from_slots(w8, axis):
    if axis == 0:
        return w8.reshape(-1, w8.shape[2])
    return w8.transpose(1, 0, 2).reshape(w8.shape[1], -1)


BIG_AXES = (1, 0, 1, 0, 1, 0)


def kernel(x, c, w_ada, b_ada, norm1_g, w_in, rel_bias, ret_gn_g, ret_gn_b, w_ret_out, w_att_out, w_o, norm2_g, w_ff1, w_ff2, norm_f_g, loss_target, m_w_ada, m_b_ada, m_norm1_g, m_w_in, m_rel_bias, m_ret_gn_g, m_ret_gn_b, m_w_ret_out, m_w_att_out, m_w_o, m_norm2_g, m_w_ff1, m_w_ff2, m_norm_f_g, v_w_ada, v_b_ada, v_norm1_g, v_w_in, v_rel_bias, v_ret_gn_g, v_ret_gn_b, v_w_ret_out, v_w_att_out, v_w_o, v_norm2_g, v_w_ff1, v_w_ff2, v_norm_f_g):
    mx, my, mc = _mesh_pos()
    dev = 4 * mx + 2 * my + mc
    chip = jnp.reshape(2 * mx + my, (1,)).astype(jnp.int32)
    core = jnp.reshape(mc, (1,)).astype(jnp.int32)
    ada_w = D * 6 // N_DEV

    w_in, m_w_in, v_w_in = (jnp.transpose(t, (0, 2, 1)) for t in (w_in, m_w_in, v_w_in))

    (w_in_shard,), (c_all,) = _to_bf16([w_in[0]], _Gather([c]), "gather_c")
    c_all = c_all.reshape(N_DEV, D)
    b_sl = lax.dynamic_slice(b_ada, (0, dev * ada_w), (1, ada_w))
    other_shards, (mod_all,) = _to_bf16([w[0] for w in (w_ret_out, w_att_out, w_o, w_ff1, w_ff2)],
                                        _Gather([_ada_fwd(c_all, w_ada[0], b_sl)]), "gather_mod")
    mod = lax.dynamic_index_in_dim(mod_all, dev, axis=1, keepdims=False).reshape(6, D)
    mods = tuple(mod[i:i + 1] for i in range(6))

    small = (norm1_g, rel_bias, ret_gn_g, ret_gn_b, norm2_g, norm_f_g.reshape(1, D))
    order = lax.dynamic_index_in_dim(jnp.asarray(_proj_order()), 2 * mx + my, axis=0, keepdims=False)
    loss, gx, in_flight, big_red, small_g, dmod = _local_step(x[0], loss_target[0], mods, w_in_shard, order,
                                                              list(other_shards), small, chip, core)

    names = ['w_ada', 'b_ada', 'norm1_g', 'w_in', 'rel_bias', 'ret_gn_g', 'ret_gn_b', 'w_ret_out', 'w_att_out',
             'w_o', 'norm2_g', 'w_ff1', 'w_ff2', 'norm_f_g']
    ws = dict(zip(names, (w_ada, b_ada, norm1_g, w_in, rel_bias, ret_gn_g, ret_gn_b, w_ret_out, w_att_out, w_o,
                          norm2_g, w_ff1, w_ff2, norm_f_g)))
    ms = dict(zip(names, (m_w_ada, m_b_ada, m_norm1_g, m_w_in, m_rel_bias, m_ret_gn_g, m_ret_gn_b, m_w_ret_out,
                          m_w_att_out, m_w_o, m_norm2_g, m_w_ff1, m_w_ff2, m_norm_f_g)))
    vs = dict(zip(names, (v_w_ada, v_b_ada, v_norm1_g, v_w_in, v_rel_bias, v_ret_gn_g, v_ret_gn_b, v_w_ret_out,
                          v_w_att_out, v_w_o, v_norm2_g, v_w_ff1, v_w_ff2, v_norm_f_g)))
    grads, delta, new_m, new_v = {}, {}, {}, {}
    big_names = ('w_ret_out', 'w_att_out', 'w_o', 'w_ff1', 'w_ff2')
    for n, (part, recv) in zip(big_names, big_red):
        grads[n], delta[n], new_m[n], new_v[n] = _adamw_reduced1(ws[n], ms[n], vs[n], part, recv, chip, "adamw_" + n)
    updated = lax.optimization_barrier((gx, tuple(delta[n] for n in big_names)))
    gathered = _run_comm(_Gather(dmod + small_g + [loss]), "gather_small", after=updated[0])
    g_b_ada, dmod_all, (g_norm1, g_bias, g_gn_g, g_gn_b, g_norm2, g_normf, loss_sum) = _sum_small(gathered)
    loss_out = loss_sum[0, 0]
    g_w_ada = _ada_bwd(c_all, lax.dynamic_slice(dmod_all, (0, dev * ada_w), (N_DEV, ada_w)))

    grads.update(w_ada=g_w_ada.reshape(w_ada.shape), b_ada=g_b_ada, norm1_g=g_norm1, rel_bias=g_bias,
                 ret_gn_g=g_gn_g, ret_gn_b=g_gn_b, norm2_g=g_norm2, norm_f_g=g_normf)
    delta['w_ada'], new_m['w_ada'], new_v['w_ada'] = _adamw(w_ada, g_w_ada, m_w_ada, v_w_ada, "adamw_w_ada")
    small_names = ('b_ada', 'norm1_g', 'rel_bias', 'ret_gn_g', 'ret_gn_b', 'norm2_g', 'norm_f_g')
    two_d = {n: (1, ws[n].size) if ws[n].ndim == 1 else ws[n].shape for n in small_names}
    d_, m_, v_ = _adamw_small(*[[src[n].reshape(two_d[n]) for n in small_names] for src in (ws, grads, ms, vs)])
    for i, n in enumerate(small_names):
        shp = ws[n].shape
        delta[n], new_m[n], new_v[n] = d_[i].reshape(shp), m_[i].reshape(shp), v_[i].reshape(shp)
        grads[n] = grads[n].reshape(shp)

    done = lax.optimization_barrier((gx, tuple(d_), tuple(delta[n] for n in ('w_ada', 'w_ret_out', 'w_att_out', 'w_o',
                                                                               'w_ff1', 'w_ff2'))))
    parts_in, recvs_in = [], []
    for half, flight in enumerate(in_flight):
        (part_in,), (recv_chip_in,) = _chip_exchange_wait(flight, done[0], f"rs_in{half}_wait")
        parts_in.append(part_in)
        recvs_in.append(recv_chip_in)
    grads['w_in'], delta['w_in'], new_m['w_in'], new_v['w_in'] = _adamw_reduced(w_in, m_w_in, v_w_in, parts_in,
                                                                               recvs_in, chip)
    for d in (grads, delta, new_m, new_v):
        d['w_in'] = jnp.transpose(d['w_in'], (0, 2, 1))
    return (loss_out, gx[None], *[grads[n] for n in names], *[delta[n] for n in names],
            *[new_m[n] for n in names], *[new_v[n] for n in names])
```

```python
import functools
import math

import numpy as np
import jax
import jax.numpy as jnp
from jax import lax
from jax.experimental import pallas as pl
from jax.experimental.pallas import tpu as pltpu

F32 = jnp.float32
BF16 = jnp.bfloat16
MESH = pl.DeviceIdType.MESH

N_DEV = 8
S = 2048
D = 1024
RET_HEADS = 4
RET_DK = 256
RET_DV = 512
CHUNK = 128
N_CHUNK = S // CHUNK
ATT_GROUPS = ((128, 1), (512, 4), (2048, 16))
ATT_HG = 4
ATT_DH = 128
ATT_BLK = 128
N_BUCKETS = 32
MAX_DIST = 2048
D_FF = 4096
IN_COLS = 12800
OFF_RQ, OFF_RK, OFF_RV, OFF_RG, OFF_ATT = 0, 1024, 2048, 4096, 6144
OFF_GA, OFF_GB = 6144, 7168
RMS_EPS = 1e-6
GN_EPS = 1e-5
ADAM_LR, ADAM_B1, ADAM_B2, ADAM_EPS, ADAM_WD, ADAM_STEP = 0.001, 0.9, 0.999, 1e-08, 0.01, 10
VMEM_LIMIT = 48 * 1024 * 1024


def _pcall(body, **kw):
    return pl.pallas_call(body, **kw)


def _params(sem=None):
    return pltpu.CompilerParams(dimension_semantics=sem, vmem_limit_bytes=VMEM_LIMIT)


HBM_SPEC = pl.BlockSpec(memory_space=pl.ANY)


def _carry(body, comm, *, name, grid, in_specs, out_specs, out_shape, scratch_shapes=()):
    single = not isinstance(out_specs, (tuple, list))
    o_specs = (out_specs,) if single else tuple(out_specs)
    o_shape = (out_shape,) if single else tuple(out_shape)
    n_in, n_out, n_scr = len(in_specs), len(o_specs), len(scratch_shapes)
    nci, nco = len(comm.ins), len(comm.out_shape)
    total = int(np.prod(grid))

    def wrapped(*refs):
        bounds = np.cumsum([0, n_in, nci, n_out, nco, n_scr])
        a, ci, o, co, scr = (refs[bounds[i]:bounds[i + 1]] for i in range(5))
        sems = refs[bounds[5]:]
        flat = 0
        for d, g in enumerate(grid):
            flat = flat * g + pl.program_id(d)

        @pl.when(flat == 0)
        def _():
            comm.start(ci, co, sems)

        body(*a, *o, *scr)

        @pl.when(flat == total - 1)
        def _():
            comm.finish(ci, co, sems)

    aliases = {n_in + i: n_out + o for i, o in getattr(comm, "aliases", {}).items()}
    call = _pcall(wrapped, name=name, grid=grid, in_specs=list(in_specs) + [HBM_SPEC] * nci,
                  out_specs=o_specs + (HBM_SPEC,) * nco, out_shape=o_shape + tuple(comm.out_shape),
                  scratch_shapes=list(scratch_shapes) + list(comm.sems), input_output_aliases=aliases,
                  compiler_params=_params(("arbitrary",) * len(grid)))

    def run(*args):
        res = call(*args, *comm.ins)
        own = res[0] if single else tuple(res[:n_out])
        return own, tuple(res[n_out:])

    return run


def _run_comm(comm, name, after=None):
    nci, nco = len(comm.ins), len(comm.out_shape)
    extra = [] if after is None else [after]

    def body(*refs):
        ci, co, sems = refs[:nci], refs[nci + len(extra):nci + len(extra) + nco], refs[nci + len(extra) + nco:]
        comm.start(ci, co, sems)
        comm.finish(ci, co, sems)

    return _pcall(body, name=name, in_specs=[HBM_SPEC] * (nci + len(extra)), out_specs=(HBM_SPEC,) * nco,
                  out_shape=tuple(comm.out_shape), scratch_shapes=list(comm.sems))(*comm.ins, *extra)


def _dot(a, b, dn):
    return lax.dot_general(a.astype(BF16), b.astype(BF16), (dn, ((), ())), preferred_element_type=F32)


NN = ((1,), (0,))
NT = ((1,), (1,))
TN = ((0,), (0,))


def _mm(a, b, mode, *, tm, tn, tk, name, out_dtype=F32, res=None, gvec=None, relu2=False, relu2_of=None, comm=None,
        after=None):
    if mode == 'nn':
        (M, K), (_, N) = a.shape, b.shape
        a_spec = pl.BlockSpec((tm, tk), lambda i, j, k: (i, k))
        b_spec = pl.BlockSpec((tk, tn), lambda i, j, k: (k, j))
        dn = NN
    elif mode == 'nt':
        (M, K), (N, _) = a.shape, b.shape
        a_spec = pl.BlockSpec((tm, tk), lambda i, j, k: (i, k))
        b_spec = pl.BlockSpec((tn, tk), lambda i, j, k: (j, k))
        dn = NT
    else:
        (K, M), (_, N) = a.shape, b.shape
        a_spec = pl.BlockSpec((tk, tm), lambda i, j, k: (k, i))
        b_spec = pl.BlockSpec((tk, tn), lambda i, j, k: (k, j))
        dn = TN
    assert M % tm == 0 and N % tn == 0 and K % tk == 0, (name, M, N, K)
    nk = K // tk
    fused = res is not None
    o_spec = pl.BlockSpec((tm, tn), lambda i, j, k: (i, j))

    def body(a_ref, b_ref, *rest):
        acc_ref = rest[-1] if nk > 1 else None
        if after is not None:
            rest = rest[1:]
        if fused:
            res_ref, g_ref, o_ref, x_ref = rest[:4]
        elif relu2_of is not None:
            u_ref, o_ref = rest[:2]
        elif relu2:
            o_ref, act_ref = rest[:2]
        else:
            o_ref = rest[0]

        def finish(acc):
            if relu2_of is not None:
                acc = acc * (2.0 * jnp.maximum(u_ref[...], 0.0))
            o_ref[...] = acc.astype(o_ref.dtype)
            if fused:
                x_ref[...] = res_ref[...] + g_ref[...] * acc
            if relu2:
                r = jnp.maximum(acc, 0.0)
                act_ref[...] = (r * r).astype(BF16)

        p = _dot(a_ref[...], b_ref[...], dn)
        if nk == 1:
            finish(p)
        else:
            k = pl.program_id(2)

            @pl.when(k == 0)
            def _():
                acc_ref[...] = p

            @pl.when(k > 0)
            def _():
                acc_ref[...] += p

            @pl.when(k == nk - 1)
            def _():
                finish(acc_ref[...])

    in_specs = [a_spec, b_spec]
    args = [a, b]
    if after is not None:
        in_specs.append(pl.BlockSpec(memory_space=pl.ANY))
        args.append(after)
    out_shape = jax.ShapeDtypeStruct((M, N), out_dtype)
    out_specs = o_spec
    if fused:
        in_specs += [pl.BlockSpec((tm, tn), lambda i, j, k: (i, j)), pl.BlockSpec((1, tn), lambda i, j, k: (0, j))]
        args += [res, gvec]
        out_shape = (out_shape, jax.ShapeDtypeStruct((M, N), F32))
        out_specs = (o_spec, pl.BlockSpec((tm, tn), lambda i, j, k: (i, j)))
    elif relu2_of is not None:
        in_specs.append(pl.BlockSpec((tm, tn), lambda i, j, k: (i, j)))
        args.append(relu2_of)
    elif relu2:
        out_shape = (out_shape, jax.ShapeDtypeStruct((M, N), BF16))
        out_specs = (o_spec, pl.BlockSpec((tm, tn), lambda i, j, k: (i, j)))
    kw = dict(name=name, grid=(M // tm, N // tn, nk), in_specs=in_specs, out_specs=out_specs,
              out_shape=out_shape, scratch_shapes=[pltpu.VMEM((tm, tn), F32)] if nk > 1 else [])
    if comm is not None:
        return _carry(body, comm, **kw)(*args)
    return _pcall(body, compiler_params=_params(("parallel", "parallel", "arbitrary")), **kw)(*args)


PROJ_TN = 512
ATT_T0, ATT_T1 = 6144 // PROJ_TN, 10752 // PROJ_TN
N_SLABS = (ATT_T1 - ATT_T0) * 4
MAIN_COLS = IN_COLS - (ATT_T1 - ATT_T0) * PROJ_TN


PROJ_TILES = IN_COLS // PROJ_TN
SHARD_ROWS = IN_COLS // N_DEV
W_CHUNKS = 4
N_OWN, N_NEAR = 5, 18


def _proj_order():
    out = np.zeros((4, 3, PROJ_TILES), np.int32)
    for q in range(4):
        def hops(t):
            owners = {col // (2 * SHARD_ROWS) for col in (t * PROJ_TN, (t + 1) * PROJ_TN - 1)}
            return max(bin(q ^ p).count("1") for p in owners)
        order = sorted(range(PROJ_TILES), key=lambda t: (hops(t), t))
        assert all(hops(t) == 0 for t in order[:N_OWN]) and all(hops(t) < 2 for t in order[:N_NEAR])
        is_att = [ATT_T0 <= t < ATT_T1 for t in order]
        for row, kind, index in ((1, False, lambda t: t if t < ATT_T0 else t - (ATT_T1 - ATT_T0)),
                                 (2, True, lambda t: t - ATT_T0)):
            own = [index(t) if a == kind else None for t, a in zip(order, is_att)]
            first = next(v for v in own if v is not None)
            last = first
            for j, v in enumerate(own):
                last = last if v is None else v
                out[q, row, j] = last
        out[q, 0] = order
    return out


def _gather_proj(x, g, sh, sc, shard, order):
    rows = SHARD_ROWS // W_CHUNKS

    def body(ord_ref, x_ref, g_ref, shift_ref, scale_ref, sh_ref, main_ref, slab_ref, full_ref, a_ref, wbuf, xbuf,
             fetch_sems, send_sems, recv_sems, local_sems, x_sem):
        j = pl.program_id(0)
        x, y, c = _mesh_pos()
        me, sibling = (x, y, c), (x, y, 1 - c)
        chips = [(1 - x, y), (x, 1 - y), (1 - x, 1 - y)]

        def block(p, owner):
            return full_ref.at[pl.ds(pl.multiple_of(_slot(owner) * SHARD_ROWS + p * rows, 16), rows)]

        def copy(p, k, owner, to, from_input=False):
            dst = block(p, owner)
            return pltpu.make_async_remote_copy(
                src_ref=sh_ref.at[pl.ds(p * rows, rows)] if from_input else dst, dst_ref=dst,
                send_sem=send_sems.at[7 * p + k], recv_sem=recv_sems.at[7 * p + k], device_id=to, device_id_type=MESH)

        pieces = range(W_CHUNKS)
        mine = [pltpu.make_async_copy(sh_ref.at[pl.ds(p * rows, rows)], block(p, me), local_sems.at[p]) for p in pieces]
        first = [copy(p, 0, me, sibling, from_input=True) for p in pieces]
        first += [copy(p, 1 + n, me, (*chips[n], c), from_input=True) for p in pieces for n in range(2)]
        near_pass = [copy(p, 4 + n, (*chips[n], c), sibling) for p in pieces for n in range(2)]
        relay = [copy(p, 3, ((x + 1 - c) % 2, (y + c) % 2, c), ((x + c) % 2, (y + 1 - c) % 2, c)) for p in pieces]
        far_pass = [copy(p, 6, (*chips[2], c), sibling) for p in pieces]

        def fetch(pos):
            slot = lax.rem(pos, 2)
            start = pl.multiple_of(ord_ref[0, pos] * PROJ_TN, PROJ_TN)
            return pltpu.make_async_copy(full_ref.at[pl.ds(start, PROJ_TN)], wbuf.at[slot], fetch_sems.at[slot])

        @pl.when(j == 0)
        def _():
            x_copy = pltpu.make_async_copy(x_ref, xbuf, x_sem.at[0])
            x_copy.start()
            for cp in mine + first:
                cp.start()
            x_copy.wait()
            for r in range(S // TR):
                rws = pl.ds(r * TR, TR)
                xv = xbuf[rws, :]
                rstd = lax.rsqrt(jnp.mean(xv * xv, axis=-1, keepdims=True) + RMS_EPS)
                n = xv * rstd * g_ref[...]
                a_ref[rws, :] = (n * (1.0 + scale_ref[...]) + shift_ref[...]).astype(BF16)
            for cp in mine:
                cp.wait()
            for p in pieces:
                copy(p, 0, sibling, me).wait_recv()
            fetch(j).start()

        @pl.when(j == N_OWN - 1)
        def _():
            for p in pieces:
                for n in range(2):
                    copy(p, 1 + n, (*chips[n], c), me).wait_recv()
                    near_pass[2 * p + n].start()
                relay[p].start()
            for p in pieces:
                for n in range(2):
                    copy(p, 4 + n, (*chips[n], 1 - c), me).wait_recv()

        @pl.when(j == N_NEAR - 1)
        def _():
            for p in pieces:
                copy(p, 3, (*chips[2], c), me).wait_recv()
                far_pass[p].start()
            for p in pieces:
                copy(p, 6, (*chips[2], 1 - c), me).wait_recv()

        @pl.when(j + 1 < PROJ_TILES)
        def _():
            fetch(j + 1).start()

        fetch(j).wait()
        w_ref = wbuf.at[lax.rem(j, 2)]
        tile = ord_ref[0, j]
        is_att = (tile >= ATT_T0) & (tile < ATT_T1)
        chunks = [pl.ds(r * 512, 512) for r in range(S // 512)]

        @pl.when(jnp.logical_not(is_att))
        def _():
            for rws in chunks:
                main_ref[rws, :] = _dot(a_ref[rws, :], w_ref[...], NT)

        @pl.when(is_att)
        def _():
            for rws in chunks:
                p = _dot(a_ref[rws, :], w_ref[...], NT)
                for h in range(4):
                    slab_ref[h, rws, :] = p[:, h * 128:(h + 1) * 128]

        @pl.when(j == PROJ_TILES - 1)
        def _():
            for cp in first + near_pass + relay + far_pass:
                cp.wait_send()

    vec = pl.BlockSpec((1, D), lambda j, o: (0, 0))
    gs = pltpu.PrefetchScalarGridSpec(
        num_scalar_prefetch=1, grid=(PROJ_TILES,),
        in_specs=[HBM_SPEC, vec, vec, vec, HBM_SPEC],
        out_specs=(pl.BlockSpec((S, PROJ_TN), lambda j, o: (0, o[1, j])),
                   pl.BlockSpec((4, S, 128), lambda j, o: (o[2, j], 0, 0)), HBM_SPEC,
                   pl.BlockSpec((S, D), lambda j, o: (0, 0))),
        scratch_shapes=[pltpu.VMEM((2, PROJ_TN, D), BF16), pltpu.VMEM((S, D), F32), pltpu.SemaphoreType.DMA((2,)),
                        pltpu.SemaphoreType.DMA((7 * W_CHUNKS,)), pltpu.SemaphoreType.DMA((7 * W_CHUNKS,)),
                        pltpu.SemaphoreType.DMA((W_CHUNKS,)), pltpu.SemaphoreType.DMA((1,))])
    return _pcall(body, name="gather_proj", grid_spec=gs,
                  out_shape=(jax.ShapeDtypeStruct((S, MAIN_COLS), F32), jax.ShapeDtypeStruct((N_SLABS, S, 128), F32),
                             jax.ShapeDtypeStruct((IN_COLS, D), BF16), jax.ShapeDtypeStruct((S, D), BF16)),
                  compiler_params=_params(("arbitrary",)))(order, x, g, sh, sc, shard)


TR = 256


def _row_spec(w=D):
    return pl.BlockSpec((TR, w), lambda i: (i, 0))


def _vec_spec(w=D):
    return pl.BlockSpec((1, w), lambda i: (0, 0))


def _norm_mod_bwd(x, g, sc, dh, dres, name, gate=None, after=None):
    gated = gate is not None

    def body(x_ref, g_ref, sc_ref, dh_ref, dres_ref, *rest):
        if after is not None:
            rest = rest[1:]
        if gated:
            f_ref, gv_ref, dx_ref, dsc_ref, dsh_ref, dg_ref, dz_ref, dgv_ref = rest
        else:
            dx_ref, dsc_ref, dsh_ref, dg_ref = rest
        i = pl.program_id(0)
        xv = x_ref[...]
        dh = dh_ref[...]
        rstd = lax.rsqrt(jnp.mean(xv * xv, axis=-1, keepdims=True) + RMS_EPS)
        xhat = xv * rstd
        gv = g_ref[...]
        dn = dh * (1.0 + sc_ref[...])
        dxhat = dn * gv
        dx = dres_ref[...] + rstd * (dxhat - xhat * jnp.mean(dxhat * xhat, axis=-1, keepdims=True))
        dx_ref[...] = dx
        sums = [(dsc_ref, jnp.sum(dh * (xhat * gv), axis=0, keepdims=True)),
                (dsh_ref, jnp.sum(dh, axis=0, keepdims=True)),
                (dg_ref, jnp.sum(dn * xhat, axis=0, keepdims=True))]
        if gated:
            dz_ref[...] = (dx * gv_ref[...]).astype(BF16)
            sums.append((dgv_ref, jnp.sum(dx * f_ref[...], axis=0, keepdims=True)))

        @pl.when(i == 0)
        def _():
            for ref, p in sums:
                ref[...] = p

        @pl.when(i > 0)
        def _():
            for ref, p in sums:
                ref[...] += p

    vec = jax.ShapeDtypeStruct((1, D), F32)
    in_specs = [_row_spec(), _vec_spec(), _vec_spec(), _row_spec(), _row_spec()]
    out_specs = [_row_spec(), _vec_spec(), _vec_spec(), _vec_spec()]
    out_shape = [jax.ShapeDtypeStruct((S, D), F32), vec, vec, vec]
    args = [x, g, sc, dh, dres]
    if after is not None:
        in_specs.append(HBM_SPEC)
        args.append(after)
    if gated:
        in_specs += [_row_spec(), _vec_spec()]
        out_specs += [_row_spec(), _vec_spec()]
        out_shape += [jax.ShapeDtypeStruct((S, D), BF16), vec]
        args += list(gate)
    return _pcall(body, name=name, grid=(S // TR,), in_specs=in_specs, out_specs=tuple(out_specs),
                  out_shape=tuple(out_shape), compiler_params=_params(("arbitrary",)))(*args)


def _w_o_norm2(merged, w_o, x, g1, g, sh, sc):
    def body(a_ref, b_ref, x_ref, g1_ref, g_ref, sh_ref, sc_ref, o_ref, x1_ref, h_ref):
        acc = _dot(a_ref[...], b_ref[...], NN)
        o_ref[...] = acc
        xv = x_ref[...] + g1_ref[...] * acc
        x1_ref[...] = xv
        rstd = lax.rsqrt(jnp.mean(xv * xv, axis=-1, keepdims=True) + RMS_EPS)
        h_ref[...] = (xv * rstd * g_ref[...] * (1.0 + sc_ref[...]) + sh_ref[...]).astype(BF16)

    rows = pl.BlockSpec((FF2_TM, D), lambda i: (i, 0))
    f32 = jax.ShapeDtypeStruct((S, D), F32)
    return _pcall(body, name="w_o_norm2", grid=(S // FF2_TM,),
                  in_specs=[rows, pl.BlockSpec((D, D), lambda i: (0, 0)), rows] + [_vec_spec()] * 4,
                  out_specs=(rows, rows, rows), out_shape=(f32, f32, jax.ShapeDtypeStruct((S, D), BF16)),
                  compiler_params=_params(("parallel",)))(merged, w_o, x, g1, g, sh, sc)


FF2_TM = 512


def _ff2_final(act, w_ff2, x1, g2, tgt, g):
    def body(a_ref, b_ref, x1_ref, g2_ref, t_ref, g_ref, loss_ref, dx_ref, dg_ref, df_ref, dg2_ref):
        i = pl.program_id(0)
        f = _dot(a_ref[...], b_ref[...], NN)
        g2v = g2_ref[...]
        xv = x1_ref[...] + g2v * f
        gv = g_ref[...]
        rstd = lax.rsqrt(jnp.mean(xv * xv, axis=-1, keepdims=True) + RMS_EPS)
        xhat = xv * rstd
        err = xhat * gv - t_ref[...]
        dy = err * (1.0 / D)
        dxhat = dy * gv
        dx = rstd * (dxhat - xhat * jnp.mean(dxhat * xhat, axis=-1, keepdims=True))
        dx_ref[...] = dx
        df_ref[...] = (dx * g2v).astype(BF16)
        p_g = jnp.sum(dy * xhat, axis=0, keepdims=True)
        p_g2 = jnp.sum(dx * f, axis=0, keepdims=True)
        p_l = jnp.zeros((1, 128), F32) + 0.5 * jnp.sum(jnp.mean(err * err, axis=-1, keepdims=True))

        @pl.when(i == 0)
        def _():
            dg_ref[...] = p_g
            dg2_ref[...] = p_g2
            loss_ref[...] = p_l

        @pl.when(i > 0)
        def _():
            dg_ref[...] += p_g
            dg2_ref[...] += p_g2
            loss_ref[...] += p_l

    vec = jax.ShapeDtypeStruct((1, D), F32)
    rows = lambda w: pl.BlockSpec((FF2_TM, w), lambda i: (i, 0))
    return _pcall(body, name="ff2_final", grid=(S // FF2_TM,),
                  in_specs=[rows(D_FF), pl.BlockSpec((D_FF, D), lambda i: (0, 0)), rows(D), _vec_spec(), rows(D),
                            _vec_spec()],
                  out_specs=(_vec_spec(128), rows(D), _vec_spec(), rows(D), _vec_spec()),
                  out_shape=(jax.ShapeDtypeStruct((1, 128), F32), jax.ShapeDtypeStruct((S, D), F32), vec,
                             jax.ShapeDtypeStruct((S, D), BF16), vec),
                  compiler_params=_params(("arbitrary",)))(act, w_ff2, x1, g2, tgt, g)


HALF = 512


MERGE_TM = 1024


def _merge_specs():
    blk = lambda off: pl.BlockSpec((MERGE_TM, HALF), lambda i, j: (i, off // HALF + j))
    return blk(OFF_GA), blk(OFF_GB), blk(0)


def _att_out_merge(att, w_att_out, proj, ret_out):
    def body(a_ref, b_ref, ga_ref, gb_ref, r_ref, o_ref, m_ref):
        acc = _dot(a_ref[...], b_ref[...], NN)
        o_ref[...] = acc
        m_ref[...] = (jax.nn.sigmoid(ga_ref[...]) * r_ref[...] + jax.nn.sigmoid(gb_ref[...]) * acc).astype(BF16)

    ga, gb, tile = _merge_specs()
    return _pcall(body, name="att_out", grid=(S // MERGE_TM, D // HALF),
                  in_specs=[pl.BlockSpec((MERGE_TM, AW), lambda i, j: (i, 0)), pl.BlockSpec((AW, HALF), lambda i, j: (0, j)),
                            ga, gb, tile],
                  out_specs=(tile, tile),
                  out_shape=(jax.ShapeDtypeStruct((S, D), F32), jax.ShapeDtypeStruct((S, D), BF16)),
                  compiler_params=_params(("parallel", "parallel")))(att, w_att_out, proj, proj, ret_out)


def _dmerged_split(dmixo, w_o, proj, ret_out, att_out):
    def body(a_ref, b_ref, ga_ref, gb_ref, r_ref, at_ref, dr_ref, da_ref, dga_ref, dgb_ref):
        dm = _dot(a_ref[...], b_ref[...], NT)
        sa = jax.nn.sigmoid(ga_ref[...])
        sb = jax.nn.sigmoid(gb_ref[...])
        dr_ref[...] = (dm * sa).astype(BF16)
        da_ref[...] = (dm * sb).astype(BF16)
        dga_ref[...] = (dm * r_ref[...] * (sa * (1.0 - sa))).astype(BF16)
        dgb_ref[...] = (dm * at_ref[...] * (sb * (1.0 - sb))).astype(BF16)

    ga, gb, tile = _merge_specs()
    o = jax.ShapeDtypeStruct((S, D), BF16)
    return _pcall(body, name="dmerged", grid=(S // MERGE_TM, D // HALF),
                  in_specs=[pl.BlockSpec((MERGE_TM, D), lambda i, j: (i, 0)), pl.BlockSpec((HALF, D), lambda i, j: (j, 0)),
                            ga, gb, tile, tile],
                  out_specs=(tile,) * 4, out_shape=(o, o, o, o),
                  compiler_params=_params(("parallel", "parallel")))(dmixo, w_o, proj, proj, ret_out, att_out)


def _ret_tables():
    H, C = RET_HEADS, CHUNK
    log_g = jnp.log1p(-(2.0 ** (-5.0 - jnp.arange(H, dtype=F32))))
    idx = jnp.arange(C, dtype=F32)
    rel = idx[:, None] - idx[None, :]
    inner = jnp.where(rel >= 0, jnp.exp(log_g[:, None, None] * jnp.maximum(rel, 0.0)), 0.0)
    qd = jnp.exp(log_g[:, None] * (idx + 1.0))[:, :, None]
    kd = jnp.exp(log_g[:, None] * (C - 1.0 - idx))[:, :, None]
    cd = jnp.broadcast_to(jnp.exp(log_g * C)[:, None, None], (H, 1, 128))
    half = RET_DK // 2
    inv = 10000.0 ** (-jnp.arange(half, dtype=F32) / half)
    ang = jnp.arange(S, dtype=F32)[:, None] * inv[None, :]
    return inner, qd, kd, cd, jnp.cos(ang), jnp.sin(ang)


def _rot(x, cos, sin):
    x1, x2 = x[:, :128], x[:, 128:]
    return jnp.concatenate([x1 * cos - x2 * sin, x1 * sin + x2 * cos], axis=1)


def _rot_t(d, cos, sin):
    d1, d2 = d[:, :128], d[:, 128:]
    return jnp.concatenate([d1 * cos + d2 * sin, d2 * cos - d1 * sin], axis=1)


RET_COLS = OFF_ATT
RET_VW = RET_HEADS * RET_DV


def _ret_specs(chunk_of):
    ci = chunk_of
    whole = lambda shape: pl.BlockSpec(shape, lambda t: (0,) * len(shape))
    return [
        pl.BlockSpec((CHUNK, RET_COLS), lambda t: (ci(t), 0)),
        pl.BlockSpec((CHUNK, 128), lambda t: (ci(t), 0)),
        pl.BlockSpec((CHUNK, 128), lambda t: (ci(t), 0)),
        whole((RET_HEADS, CHUNK, CHUNK)), whole((RET_HEADS, CHUNK, 1)), whole((RET_HEADS, CHUNK, 1)),
        whole((RET_HEADS, 1, 128)), whole((1, RET_VW)), whole((1, RET_VW)),
    ]


def _ret_cols(h):
    q = slice(OFF_RQ + h * RET_DK, OFF_RQ + (h + 1) * RET_DK)
    k = slice(OFF_RK + h * RET_DK, OFF_RK + (h + 1) * RET_DK)
    v = slice(OFF_RV + h * RET_DV, OFF_RV + (h + 1) * RET_DV)
    g = slice(OFF_RG + h * RET_DV, OFF_RG + (h + 1) * RET_DV)
    return q, k, v, g, slice(h * RET_DV, (h + 1) * RET_DV)


def _ret_fwd(proj, tables, gn_g, gn_b, after):
    inner, qd, kd, cd, cos, sin = tables

    def body(x_ref, cos_ref, sin_ref, in_ref, qd_ref, kd_ref, cd_ref, g_ref, b_ref, after_ref,
             gated_ref, ro_ref, st_ref, s_scr):
        i = pl.program_id(0)

        @pl.when(i == 0)
        def _():
            s_scr[...] = jnp.zeros_like(s_scr)

        cosv, sinv = cos_ref[...], sin_ref[...]
        for h in range(RET_HEADS):
            cq, ck, cv, cg, co = _ret_cols(h)
            q = _rot(x_ref[:, cq], cosv, sinv)
            k = _rot(x_ref[:, ck], cosv, sinv) * (RET_DK ** -0.5)
            v = x_ref[:, cv]
            st = s_scr[h]
            st_ref[h] = st.astype(BF16)
            s = _dot(q, k, NT) * in_ref[h]
            o = _dot(s, v, NN) + _dot(q, st, NN) * qd_ref[h]
            s_scr[h] = st * cd_ref[h, :, :1] + _dot(k * kd_ref[h], v, TN)
            ro_ref[:, co] = o
            mu = jnp.mean(o, axis=-1, keepdims=True)
            oc = o - mu
            var = jnp.mean(oc * oc, axis=-1, keepdims=True)
            rn = oc * lax.rsqrt(var + GN_EPS) * g_ref[:, co] + b_ref[:, co]
            rg = x_ref[:, cg]
            gated_ref[:, co] = (rg * jax.nn.sigmoid(rg) * rn).astype(BF16)

    ospec = pl.BlockSpec((CHUNK, RET_VW), lambda t: (t, 0))
    return _pcall(
        body, name="ret_fwd", grid=(N_CHUNK,), in_specs=_ret_specs(lambda t: t) + [HBM_SPEC],
        out_specs=(ospec, ospec, pl.BlockSpec((RET_HEADS, None, RET_DK, RET_DV), lambda t: (0, t, 0, 0))),
        out_shape=(jax.ShapeDtypeStruct((S, RET_VW), BF16), jax.ShapeDtypeStruct((S, RET_VW), F32),
                   jax.ShapeDtypeStruct((RET_HEADS, N_CHUNK, RET_DK, RET_DV), BF16)),
        scratch_shapes=[pltpu.VMEM((RET_HEADS, RET_DK, RET_DV), F32)],
        compiler_params=_params(("arbitrary",)))(proj, cos, sin, inner, qd, kd, cd, gn_g, gn_b, after)


def _ret_bwd(proj, tables, gn_g, gn_b, ro, states, dgated, others):
    inner, qd, kd, cd, cos, sin = tables
    last = N_CHUNK - 1
    pieces = lambda o: [(h, o.shape[2]) for h in range(o.shape[0])] if len(o.shape) == 3 else [(None, o.shape[1])]
    assert RET_COLS + sum(w for o in others for _, w in pieces(o)) == IN_COLS

    def body(x_ref, cos_ref, sin_ref, in_ref, qd_ref, kd_ref, cd_ref, g_ref, b_ref, ro_ref, st_ref, dg_ref, *rest):
        other_refs, (dx_ref, gg_ref, gb_ref, gs_scr) = rest[:len(others)], rest[len(others):]
        t = pl.program_id(0)
        col = RET_COLS
        for o_ref in other_refs:
            for h, w in pieces(o_ref):
                dx_ref[:, col:col + w] = (o_ref[...] if h is None else o_ref[h]).astype(BF16)
                col += w

        @pl.when(t == 0)
        def _():
            gs_scr[...] = jnp.zeros_like(gs_scr)
            gg_ref[...] = jnp.zeros_like(gg_ref)
            gb_ref[...] = jnp.zeros_like(gb_ref)

        cosv, sinv = cos_ref[...], sin_ref[...]
        for h in range(RET_HEADS):
            cq, ck, cv, cg, co = _ret_cols(h)
            q = _rot(x_ref[:, cq], cosv, sinv)
            k = _rot(x_ref[:, ck], cosv, sinv) * (RET_DK ** -0.5)
            v = x_ref[:, cv]
            qdv, kdv, dm = qd_ref[h], kd_ref[h], in_ref[h]
            st = st_ref[h]
            o = ro_ref[:, co]
            gv = g_ref[:, co]
            mu = jnp.mean(o, axis=-1, keepdims=True)
            oc = o - mu
            rstd = lax.rsqrt(jnp.mean(oc * oc, axis=-1, keepdims=True) + GN_EPS)
            ohat = oc * rstd
            rn = ohat * gv + b_ref[:, co]
            rg = x_ref[:, cg]
            sg = jax.nn.sigmoid(rg)
            dgt = dg_ref[:, co]
            drn = dgt * (rg * sg)
            dx_ref[:, cg] = (dgt * rn * (sg * (1.0 + rg * (1.0 - sg)))).astype(BF16)
            gg_ref[:, co] += jnp.sum(drn * ohat, axis=0, keepdims=True)
            gb_ref[:, co] += jnp.sum(drn, axis=0, keepdims=True)
            dohat = drn * gv
            do = rstd * (dohat - jnp.mean(dohat, axis=-1, keepdims=True)
                         - ohat * jnp.mean(dohat * ohat, axis=-1, keepdims=True))
            gs = gs_scr[h]
            s = _dot(q, k, NT) * dm
            dsr = _dot(do, v, NT) * dm
            dq = _dot(dsr, k, NN) + _dot(do, st, NT) * qdv
            dk = _dot(dsr, q, TN) + _dot(v, gs, NT) * kdv
            dv = _dot(s, do, TN) + _dot(k * kdv, gs, NN)
            gs_scr[h] = gs * cd_ref[h, :, :1] + _dot(q * qdv, do, TN)
            dx_ref[:, cq] = _rot_t(dq, cosv, sinv).astype(BF16)
            dx_ref[:, ck] = (_rot_t(dk, cosv, sinv) * (RET_DK ** -0.5)).astype(BF16)
            dx_ref[:, cv] = dv.astype(BF16)

    rev = lambda t: last - t
    vblk = pl.BlockSpec((CHUNK, RET_VW), lambda t: (rev(t), 0))
    vspec = pl.BlockSpec((1, RET_VW), lambda t: (0, 0))
    rows = lambda w: pl.BlockSpec((CHUNK, w), lambda t: (rev(t), 0))
    return _pcall(
        body, name="ret_bwd", grid=(N_CHUNK,),
        in_specs=_ret_specs(rev) + [vblk, pl.BlockSpec((RET_HEADS, None, RET_DK, RET_DV), lambda t: (0, rev(t), 0, 0)),
                                    vblk] + [rows(o.shape[1]) if o.ndim == 2 else
                                             pl.BlockSpec((o.shape[0], CHUNK, o.shape[2]), lambda t: (0, rev(t), 0))
                                             for o in others],
        out_specs=(rows(IN_COLS), vspec, vspec),
        out_shape=(jax.ShapeDtypeStruct((S, IN_COLS), BF16), jax.ShapeDtypeStruct((1, RET_VW), F32),
                   jax.ShapeDtypeStruct((1, RET_VW), F32)),
        scratch_shapes=[pltpu.VMEM((RET_HEADS, RET_DK, RET_DV), F32)],
        compiler_params=_params(("arbitrary",)))(proj, cos, sin, inner, qd, kd, cd, gn_g, gn_b, ro, states, dgated,
                                                 *others)


def _bucket_tables():
    qi = np.arange(ATT_BLK)[:, None]
    kj = np.arange(2 * ATT_BLK)[None, :]
    m = ATT_BLK + qi - kj
    out = []
    for win, dil in ATT_GROUPS:
        w = win // dil
        dist = (np.clip(m, 0, w) * dil).astype(np.int32)
        max_exact = N_BUCKETS // 2
        d_f = np.maximum(dist, 1).astype(np.float32)
        large = max_exact + (np.log(d_f / np.float32(max_exact)) / np.float32(math.log(MAX_DIST / max_exact))
                             * np.float32(N_BUCKETS - max_exact)).astype(np.int32)
        large = np.minimum(large, N_BUCKETS - 1)
        out.append(np.where(dist < max_exact, dist, large).astype(np.int32))
    return np.stack(out)


def _bias_build(rel_bias, buckets, after):
    def body(tab_ref, bk_ref, after_ref, o_ref):
        hh = pl.program_id(0)
        bk = bk_ref[...]
        acc = jnp.zeros((ATT_BLK, 2 * ATT_BLK), F32)
        for b in range(N_BUCKETS):
            acc = jnp.where(bk == b, tab_ref[b, hh], acc)
        o_ref[...] = acc

    nh = len(ATT_GROUPS) * ATT_HG
    return _pcall(body, name="bias_build", grid=(nh,),
                  in_specs=[pl.BlockSpec(memory_space=pltpu.SMEM),
                            pl.BlockSpec((None, ATT_BLK, 2 * ATT_BLK), lambda hh: (hh // ATT_HG, 0, 0)), HBM_SPEC],
                  out_specs=pl.BlockSpec((None, ATT_BLK, 2 * ATT_BLK), lambda hh: (hh, 0, 0)),
                  out_shape=jax.ShapeDtypeStruct((nh, ATT_BLK, 2 * ATT_BLK), F32),
                  compiler_params=_params(("parallel",)))(rel_bias, buckets, after)


def _bias_grad(ds_sum, buckets):
    def body(ds_ref, bk_ref, o_ref):
        bk = bk_ref[...]
        ds = ds_ref[...]
        rows = lax.broadcasted_iota(jnp.int32, (N_BUCKETS, 128), 0)
        acc = jnp.zeros((N_BUCKETS, 128), F32)
        for b in range(N_BUCKETS):
            acc = jnp.where(rows == b, jnp.sum(jnp.where(bk == b, ds, 0.0)), acc)
        o_ref[...] = acc

    nh = len(ATT_GROUPS) * ATT_HG
    return _pcall(body, name="bias_grad", grid=(nh,),
                  in_specs=[pl.BlockSpec((None, ATT_BLK, 2 * ATT_BLK), lambda hh: (hh, 0, 0)),
                            pl.BlockSpec((None, ATT_BLK, 2 * ATT_BLK), lambda hh: (hh // ATT_HG, 0, 0))],
                  out_specs=pl.BlockSpec((None, N_BUCKETS, 128), lambda hh: (hh, 0, 0)),
                  out_shape=jax.ShapeDtypeStruct((nh, N_BUCKETS, 128), F32),
                  compiler_params=_params(("parallel",)))(ds_sum, buckets)


def _att_valid(n):
    qi = lax.broadcasted_iota(jnp.int32, (ATT_BLK, 2 * ATT_BLK), 0)
    kj = lax.broadcasted_iota(jnp.int32, (ATT_BLK, 2 * ATT_BLK), 1)
    m = ATT_BLK + qi - kj
    first_key = jnp.where(n > 0, 0, ATT_BLK)
    return (m >= 0) & (m <= ATT_BLK) & (kj >= first_key)


ATT_HP = (1, 2, 2)


def _att_geometry(gi):
    _, dil = ATT_GROUPS[gi]
    return dil, S // dil // ATT_BLK, ATT_HP[gi]


def _blk(dil, r, n):
    if dil == 1:
        return pl.ds(n * ATT_BLK, ATT_BLK)
    return pl.ds(r + n * ATT_BLK * dil, ATT_BLK, stride=dil)


def _slab_specs(gi):
    _, _, hp = _att_geometry(gi)
    per = ATT_HG // hp
    return [pl.BlockSpec((hp, S, ATT_DH), lambda g, r, part=part: ((3 * gi + part) * per + g, 0, 0))
            for part in range(3)]


def _head_specs(gi, count):
    _, _, hp = _att_geometry(gi)
    return [pl.BlockSpec((hp, S, ATT_DH), lambda g, r: (g, 0, 0))] * count


def _bias_spec(gi):
    _, _, hp = _att_geometry(gi)
    return pl.BlockSpec((hp, ATT_BLK, 2 * ATT_BLK), lambda g, r: (gi * (ATT_HG // hp) + g, 0, 0))


def _att_valid_first():
    qi = lax.broadcasted_iota(jnp.int32, (ATT_BLK, ATT_BLK), 0)
    kj = lax.broadcasted_iota(jnp.int32, (ATT_BLK, ATT_BLK), 1)
    return kj <= qi


def _att_fwd(slabs, bias, gi, comm=None):
    dil, nb, hp = _att_geometry(gi)
    scale = ATT_DH ** -0.5

    def body(q_ref, k_ref, v_ref, bias_ref, o_ref, l_ref):
        r = pl.program_id(1)
        for n in range(nb):
            cur = _blk(dil, r, n)
            valid = _att_valid(n) if n > 0 else _att_valid_first()
            for h in range(hp):
                if n > 0:
                    prev = _blk(dil, r, n - 1)
                    kk = jnp.concatenate([k_ref[h, prev, :], k_ref[h, cur, :]], axis=0)
                    vv = jnp.concatenate([v_ref[h, prev, :], v_ref[h, cur, :]], axis=0)
                    bias = bias_ref[h]
                else:
                    kk, vv, bias = k_ref[h, cur, :], v_ref[h, cur, :], bias_ref[h, :, pl.ds(ATT_BLK, ATT_BLK)]
                s = _dot(q_ref[h, cur, :], kk, NT) * scale + bias
                s = jnp.where(valid, s, -1e30)
                mx = jnp.max(s, axis=-1, keepdims=True)
                e = jnp.exp(s - mx)
                den = jnp.sum(e, axis=-1, keepdims=True)
                o_ref[h, cur, :] = _dot(e / den, vv, NN)
                l_ref[h, cur, :] = jnp.broadcast_to(mx + jnp.log(den), (ATT_BLK, ATT_DH))

    osh = pltpu.HBM((ATT_HG, S, ATT_DH), F32)
    kw = dict(name=f"att_fwd{gi}", grid=(ATT_HG // hp, dil), in_specs=_slab_specs(gi) + [_bias_spec(gi)],
              out_specs=tuple(_head_specs(gi, 2)), out_shape=(osh, osh))
    if comm is not None:
        return _carry(body, comm, **kw)(slabs, slabs, slabs, bias)
    return _pcall(body, compiler_params=_params(("parallel", "arbitrary")), **kw)(slabs, slabs, slabs, bias)


def _att_bwd(slabs, bias, o, lse, do, dlse, gi, comm=None):
    dil, nb, hp = _att_geometry(gi)
    per = ATT_HG // hp
    scale = ATT_DH ** -0.5
    wide = lambda t: jnp.concatenate([t, t], axis=1)

    def body(q_ref, k_ref, v_ref, bias_ref, o_ref, l_ref, do_ref, dl_ref, dq_ref, dk_ref, dv_ref, ds_ref):
        r = pl.program_id(1)

        @pl.when(r == 0)
        def _():
            ds_ref[...] = jnp.zeros_like(ds_ref)

        for h in range(hp):
            carry_k = carry_v = None
            for n in range(nb):
                cur = _blk(dil, r, n)
                q = q_ref[h, cur, :]
                dov = do_ref[h, cur, :]
                delta = jnp.sum(dov * o_ref[h, cur, :], axis=-1, keepdims=True)
                if n == 0:
                    own = pl.ds(ATT_BLK, ATT_BLK)
                    kk, vv = k_ref[h, cur, :], v_ref[h, cur, :]
                    s = _dot(q, kk, NT) * scale + bias_ref[h, :, own]
                    p = jnp.where(_att_valid_first(), jnp.exp(s - l_ref[h, cur, :]), 0.0)
                    ds = p * (_dot(dov, vv, NT) - delta + dl_ref[h, cur, :])
                    ds_ref[h, :, own] += ds
                    dq_ref[h, cur, :] = _dot(ds, kk, NN) * scale
                    carry_k, carry_v = _dot(ds, q, TN) * scale, _dot(p, dov, TN)
                    continue
                prev = _blk(dil, r, n - 1)
                kk = jnp.concatenate([k_ref[h, prev, :], k_ref[h, cur, :]], axis=0)
                vv = jnp.concatenate([v_ref[h, prev, :], v_ref[h, cur, :]], axis=0)
                s = _dot(q, kk, NT) * scale + bias_ref[h]
                p = jnp.where(_att_valid(n), jnp.exp(s - wide(l_ref[h, cur, :])), 0.0)
                dp = _dot(dov, vv, NT)
                ds = p * (dp - delta + wide(dl_ref[h, cur, :]))
                ds_ref[h] += ds
                dq_ref[h, cur, :] = _dot(ds, kk, NN) * scale
                dkk = _dot(ds, q, TN) * scale
                dvv = _dot(p, dov, TN)
                dk_ref[h, prev, :] = carry_k + dkk[:ATT_BLK]
                dv_ref[h, prev, :] = carry_v + dvv[:ATT_BLK]
                carry_k, carry_v = dkk[ATT_BLK:], dvv[ATT_BLK:]
            last = _blk(dil, r, nb - 1)
            dk_ref[h, last, :] = carry_k
            dv_ref[h, last, :] = carry_v

    osh = jax.ShapeDtypeStruct((ATT_HG, S, ATT_DH), F32)
    kw = dict(name=f"att_bwd{gi}", grid=(per, dil), in_specs=_slab_specs(gi) + [_bias_spec(gi)] + _head_specs(gi, 4),
              out_specs=(*_head_specs(gi, 3), pl.BlockSpec((hp, ATT_BLK, 2 * ATT_BLK), lambda g, r: (g, 0, 0))),
              out_shape=(osh, osh, osh, jax.ShapeDtypeStruct((ATT_HG, ATT_BLK, 2 * ATT_BLK), F32)))
    args = (slabs, slabs, slabs, bias, o, lse, do, dlse)
    if comm is not None:
        return _carry(body, comm, **kw)(*args)
    return _pcall(body, compiler_params=_params(("arbitrary", "arbitrary")), **kw)(*args)


AW = ATT_HG * ATT_DH


def _mix_weights(l0, l1, l2):
    mx = jnp.maximum(jnp.maximum(l0, l1), l2)
    e0, e1, e2 = jnp.exp(l0 - mx), jnp.exp(l1 - mx), jnp.exp(l2 - mx)
    den = e0 + e1 + e2
    return e0 / den, e1 / den, e2 / den


def _heads_spec():
    return pl.BlockSpec((ATT_HG, TR, ATT_DH), lambda i: (0, i, 0))


def _mix_fwd(os_, ls, comm=None):
    def body(o0, o1, o2, l0, l1, l2, att_ref):
        for h in range(ATT_HG):
            w0, w1, w2 = _mix_weights(l0[h], l1[h], l2[h])
            att_ref[:, h * ATT_DH:(h + 1) * ATT_DH] = (w0 * o0[h] + w1 * o1[h] + w2 * o2[h]).astype(BF16)

    kw = dict(name="mix_fwd", grid=(S // TR,), in_specs=[_heads_spec()] * 6, out_specs=_row_spec(AW),
              out_shape=jax.ShapeDtypeStruct((S, AW), BF16))
    if comm is not None:
        return _carry(body, comm, **kw)(*os_, *ls)
    return _pcall(body, compiler_params=_params(("parallel",)), **kw)(*os_, *ls)


def _mix_bwd(os_, ls, datt):
    def body(o0, o1, o2, l0, l1, l2, da_ref, d0, d1, d2, e0, e1, e2):
        for h in range(ATT_HG):
            ws = _mix_weights(l0[h], l1[h], l2[h])
            da = da_ref[:, h * ATT_DH:(h + 1) * ATT_DH]
            dws = []
            for o_ref, w, d_ref in zip((o0, o1, o2), ws, (d0, d1, d2)):
                d_ref[h] = w * da
                dws.append(jnp.broadcast_to(jnp.sum(da * o_ref[h], axis=-1, keepdims=True), (TR, ATT_DH)))
            tot = ws[0] * dws[0] + ws[1] * dws[1] + ws[2] * dws[2]
            for w, dw, e_ref in zip(ws, dws, (e0, e1, e2)):
                e_ref[h] = w * (dw - tot)

    o = pltpu.HBM((ATT_HG, S, ATT_DH), F32)
    return _pcall(body, name="mix_bwd", grid=(S // TR,), in_specs=[_heads_spec()] * 6 + [_row_spec(AW)],
                  out_specs=(_heads_spec(),) * 6, out_shape=(o,) * 6,
                  compiler_params=_params(("parallel",)))(*os_, *ls, datt)


def _ada_fwd(c_all, w_sh, b_sl):
    def body(c_ref, w_ref, b_ref, o_ref):
        cv = c_ref[...]
        o_ref[...] = _dot(cv * jax.nn.sigmoid(cv), w_ref[...], NN) + b_ref[...]

    return _pcall(body, name="ada_fwd", out_shape=jax.ShapeDtypeStruct((N_DEV, w_sh.shape[1]), F32),
                  compiler_params=_params())(c_all, w_sh, b_sl)


CAST_STEPS = 4


def _to_bf16(arrs, comm, name):
    n = len(arrs)

    def body(*refs):
        for src, dst in zip(refs[:n], refs[n:]):
            dst[...] = src[...].astype(BF16)

    blocks = [pl.BlockSpec((a.shape[0] // CAST_STEPS, a.shape[1]), lambda i: (i, 0)) for a in arrs]
    return _carry(body, comm, name=name, grid=(CAST_STEPS,), in_specs=blocks, out_specs=tuple(blocks),
                  out_shape=tuple(pltpu.HBM(a.shape, BF16) for a in arrs))(*arrs)


def _ada_bwd(c_all, dm_sl):
    def body(c_ref, d_ref, o_ref):
        cv = c_ref[...]
        o_ref[...] = _dot(cv * jax.nn.sigmoid(cv), d_ref[...], TN)

    return _pcall(body, name="ada_bwd", out_shape=jax.ShapeDtypeStruct((D, dm_sl.shape[1]), F32),
                  compiler_params=_params())(c_all, dm_sl)


N_MOD = 6


def _sum_small(gathered):
    n = len(gathered)

    def body(*refs):
        ins, (gb_ref, dm_ref), outs = refs[:n], refs[n:n + 2], refs[n + 2:]

        def total(r):
            acc = r[0]
            for e in range(1, N_DEV):
                acc = acc + r[e]
            return acc

        for i in range(N_MOD):
            cols = slice(i * D, (i + 1) * D)
            gb_ref[:, cols] = total(ins[i])
            for e in range(N_DEV):
                dm_ref[e:e + 1, cols] = ins[i][e]
        for r, o_ref in zip(ins[N_MOD:], outs):
            o_ref[...] = total(r)

    shapes = (jax.ShapeDtypeStruct((1, N_MOD * D), F32), jax.ShapeDtypeStruct((N_DEV, N_MOD * D), F32),
              *[jax.ShapeDtypeStruct(g.shape[1:], F32) for g in gathered[N_MOD:]])
    res = _pcall(body, name="sum_small", out_shape=shapes, compiler_params=_params())(*gathered)
    return res[0], res[1], res[2:]


def _row_tile(m, n):
    t = max(8, min(m, (1 << 19) // n // 8 * 8))
    while m % t:
        t -= 8
    return t


def _pair_sum(full, recv, sel, name, col_block=0):
    _, m, n = recv.shape
    t = _row_tile(m, n)

    def body(sel_ref, a_ref, b_ref, o_ref):
        o_ref[...] = (a_ref[...].astype(F32) + b_ref[...].astype(F32)).astype(o_ref.dtype)

    gs = pltpu.PrefetchScalarGridSpec(
        num_scalar_prefetch=1, grid=(4, m // t),
        in_specs=[pl.BlockSpec((None, None, t, n), lambda q, i, s: (q, s[0], i, col_block)),
                  pl.BlockSpec((None, t, n), lambda q, i, s: (q, i, 0))],
        out_specs=pl.BlockSpec((None, t, n), lambda q, i, s: (q, i, 0)))
    return _pcall(body, name=name, grid_spec=gs, out_shape=pltpu.HBM((4, m, n), full.dtype),
                  compiler_params=_params(("parallel", "parallel")))(sel, full, recv)


def _chip_sum(part, recv, sel, name):
    _, m, n = part.shape
    t = _row_tile(m, n)

    def body(sel_ref, a_ref, r_ref, o_ref):
        o_ref[...] = ((a_ref[...].astype(F32) + r_ref[0].astype(F32)) + r_ref[1].astype(F32)) + r_ref[2].astype(F32)

    gs = pltpu.PrefetchScalarGridSpec(
        num_scalar_prefetch=1, grid=(m // t,),
        in_specs=[pl.BlockSpec((None, t, n), lambda i, s: (s[0], i, 0)),
                  pl.BlockSpec((3, t, n), lambda i, s: (0, i, 0))],
        out_specs=pl.BlockSpec((t, n), lambda i, s: (i, 0)))
    return _pcall(body, name=name, grid_spec=gs, out_shape=jax.ShapeDtypeStruct((m, n), F32),
                  compiler_params=_params(("parallel",)))(sel, part, recv)


def _adamw_math(w, g, m, v):
    nm = ADAM_B1 * m + (1.0 - ADAM_B1) * g
    nv = ADAM_B2 * v + (1.0 - ADAM_B2) * (g * g)
    m_hat = nm / (1.0 - ADAM_B1 ** ADAM_STEP)
    v_hat = nv / (1.0 - ADAM_B2 ** ADAM_STEP)
    return -ADAM_LR * (m_hat / (jnp.sqrt(v_hat) + ADAM_EPS) + ADAM_WD * w), nm, nv


def _adamw(w, g, m, v, name):
    _, rows, cols = w.shape
    t = _row_tile(rows, cols)

    def body(w_ref, g_ref, m_ref, v_ref, d_ref, nm_ref, nv_ref):
        d_ref[...], nm_ref[...], nv_ref[...] = _adamw_math(w_ref[...], g_ref[...], m_ref[...], v_ref[...])

    spec3 = pl.BlockSpec((None, t, cols), lambda i: (0, i, 0))
    spec2 = pl.BlockSpec((t, cols), lambda i: (i, 0))
    o = jax.ShapeDtypeStruct(w.shape, F32)
    return _pcall(body, name=name, grid=(rows // t,), in_specs=[spec3, spec2, spec3, spec3], out_specs=(spec3,) * 3,
                  out_shape=(o, o, o), compiler_params=_params(("parallel",)))(w, g, m, v)


def _adamw_reduced1(w, m, v, part, recv, sel, name):
    _, rows, cols = w.shape
    t = _row_tile(rows, cols)

    def body(sel_ref, w_ref, m_ref, v_ref, p_ref, r_ref, g_ref, d_ref, nm_ref, nv_ref):
        g = ((p_ref[...].astype(F32) + r_ref[0].astype(F32)) + r_ref[1].astype(F32)) + r_ref[2].astype(F32)
        g_ref[...] = g
        d_ref[...], nm_ref[...], nv_ref[...] = _adamw_math(w_ref[...], g, m_ref[...], v_ref[...])

    wspec = pl.BlockSpec((None, t, cols), lambda i, s: (0, i, 0))
    gs = pltpu.PrefetchScalarGridSpec(
        num_scalar_prefetch=1, grid=(rows // t,),
        in_specs=[wspec, wspec, wspec, pl.BlockSpec((None, t, cols), lambda i, s: (s[0], i, 0)),
                  pl.BlockSpec((3, t, cols), lambda i, s: (0, i, 0))],
        out_specs=(wspec,) * 4)
    o = jax.ShapeDtypeStruct(w.shape, F32)
    return _pcall(body, name=name, grid_spec=gs, out_shape=(o, o, o, o),
                  compiler_params=_params(("parallel",)))(sel, w, m, v, part, recv)


def _adamw_reduced(w, m, v, parts, recvs, sel):
    _, rows, cols = w.shape
    half = cols // 2
    t = _row_tile(rows, half)

    def body(sel_ref, w_ref, m_ref, v_ref, pa_ref, pb_ref, ra_ref, rb_ref, g_ref, d_ref, nm_ref, nv_ref):
        total = lambda p_ref, r_ref: ((p_ref[...].astype(F32) + r_ref[0].astype(F32)) + r_ref[1].astype(F32)) \
            + r_ref[2].astype(F32)
        g = jnp.where(pl.program_id(1) == 0, total(pa_ref, ra_ref), total(pb_ref, rb_ref))
        g_ref[...] = g
        d_ref[...], nm_ref[...], nv_ref[...] = _adamw_math(w_ref[...], g, m_ref[...], v_ref[...])

    wspec = pl.BlockSpec((None, t, half), lambda i, j, s: (0, i, j))
    pspec = pl.BlockSpec((None, t, half), lambda i, j, s: (s[0], i, 0))
    rspec = pl.BlockSpec((3, t, half), lambda i, j, s: (0, i, 0))
    gs = pltpu.PrefetchScalarGridSpec(num_scalar_prefetch=1, grid=(rows // t, 2),
                                      in_specs=[wspec, wspec, wspec, pspec, pspec, rspec, rspec],
                                      out_specs=(wspec,) * 4)
    o = jax.ShapeDtypeStruct(w.shape, F32)
    return _pcall(body, name="adamw_w_in", grid_spec=gs, out_shape=(o, o, o, o),
                  compiler_params=_params(("parallel", "arbitrary")))(sel, w, m, v, *parts, *recvs)


def _adamw_small(ws, gs, ms, vs):
    n = len(ws)

    def body(*refs):
        for i in range(n):
            w_ref, g_ref, m_ref, v_ref = (refs[k * n + i] for k in range(4))
            d, nm, nv = _adamw_math(w_ref[...], g_ref[...], m_ref[...], v_ref[...])
            refs[4 * n + i][...] = d
            refs[5 * n + i][...] = nm
            refs[6 * n + i][...] = nv

    shapes = tuple(jax.ShapeDtypeStruct(w.shape, F32) for w in ws)
    res = _pcall(body, name="adamw_small", out_shape=shapes * 3, compiler_params=_params())(*ws, *gs, *ms, *vs)
    return res[:n], res[n:2 * n], res[2 * n:]


def _mesh_pos():
    return lax.axis_index("x"), lax.axis_index("y"), lax.axis_index("c")


class _Gather:
    def __init__(self, arrs):
        self.ins = list(arrs)
        self.out_shape = tuple(jax.ShapeDtypeStruct((N_DEV,) + a.shape, a.dtype) for a in arrs)
        n = len(arrs)
        self.sems = [pltpu.SemaphoreType.DMA((7 * n,)), pltpu.SemaphoreType.DMA((7 * n,)),
                     pltpu.SemaphoreType.DMA((n,))]

    def _copies(self, ins, outs, sems):
        send_sems, recv_sems, local_sems = sems
        x, y, c = _mesh_pos()
        me, sibling = (x, y, c), (x, y, 1 - c)
        chips = [(1 - x, y), (x, 1 - y), (1 - x, 1 - y)]

        def copy(p, k, block, to, from_input=False):
            dst = outs[p].at[_slot(block)]
            return pltpu.make_async_remote_copy(
                src_ref=ins[p] if from_input else dst, dst_ref=dst, send_sem=send_sems.at[7 * p + k],
                recv_sem=recv_sems.at[7 * p + k], device_id=to, device_id_type=MESH)

        npc = len(self.ins)
        mine = [pltpu.make_async_copy(ins[p], outs[p].at[_slot(me)], local_sems.at[p]) for p in range(npc)]
        first = []
        for p in range(npc):
            first.append(copy(p, 0, me, sibling, from_input=True))
            first += [copy(p, 1 + j, me, (*chip, c), from_input=True) for j, chip in enumerate(chips)]
        return me, sibling, chips, c, copy, mine, first

    def start(self, ins, outs, sems):
        *_, mine, first = self._copies(ins, outs, sems)
        for cp in mine + first:
            cp.start()

    def finish(self, ins, outs, sems):
        me, sibling, chips, c, copy, mine, first = self._copies(ins, outs, sems)
        npc = len(self.ins)
        passed = []
        for p in range(npc):
            for j, chip in enumerate(chips):
                copy(p, 1 + j, (*chip, c), me).wait_recv()
                passed.append(copy(p, 4 + j, (*chip, c), sibling))
                passed[-1].start()
        for p in range(npc):
            copy(p, 0, sibling, me).wait_recv()
            for j, chip in enumerate(chips):
                copy(p, 4 + j, (*chip, 1 - c), me).wait_recv()
        for cp in first + passed:
            cp.wait_send()
        for cp in mine:
            cp.wait()


class _ExchangeCore:
    def __init__(self, fulls, cols=None):
        self.ins = list(fulls)
        self.cols = cols
        width = lambda f: f.shape[3] if cols is None else cols[1]
        self.out_shape = tuple(jax.ShapeDtypeStruct((4, f.shape[2], width(f)), f.dtype) for f in fulls)
        self.sems = [pltpu.SemaphoreType.DMA((4 * len(fulls),)), pltpu.SemaphoreType.DMA((4 * len(fulls),))]

    def _copies(self, ins, outs, sems):
        send_sems, recv_sems = sems
        x, y, c = _mesh_pos()

        def src(a, q):
            ref = ins[a].at[q, 1 - c]
            return ref if self.cols is None else ref.at[:, pl.ds(*self.cols)]

        return [pltpu.make_async_remote_copy(
            src_ref=src(a, q), dst_ref=outs[a].at[q], send_sem=send_sems.at[4 * a + q],
            recv_sem=recv_sems.at[4 * a + q], device_id=(x, y, 1 - c), device_id_type=MESH)
            for a in range(len(self.ins)) for q in range(4)]

    def start(self, ins, outs, sems):
        for cp in self._copies(ins, outs, sems):
            cp.start()

    def finish(self, ins, outs, sems):
        for cp in self._copies(ins, outs, sems):
            cp.wait()


class _ExchangeChip:
    def __init__(self, parts):
        self.ins = list(parts)
        self.out_shape = tuple(jax.ShapeDtypeStruct((3,) + p.shape[1:], p.dtype) for p in parts)
        self.sems = [pltpu.SemaphoreType.DMA((3 * len(parts),)), pltpu.SemaphoreType.DMA((3 * len(parts),))]

    def _copies(self, ins, outs, sems):
        send_sems, recv_sems = sems
        x, y, c = _mesh_pos()
        chips = [(1 - x, y), (x, 1 - y), (1 - x, 1 - y)]
        return [pltpu.make_async_remote_copy(
            src_ref=ins[a].at[2 * px + py], dst_ref=outs[a].at[j], send_sem=send_sems.at[3 * a + j],
            recv_sem=recv_sems.at[3 * a + j], device_id=(px, py, c), device_id_type=MESH)
            for a in range(len(self.ins)) for j, (px, py) in enumerate(chips)]

    def start(self, ins, outs, sems):
        for cp in self._copies(ins, outs, sems):
            cp.start()

    def finish(self, ins, outs, sems):
        for cp in self._copies(ins, outs, sems):
            cp.wait()


HBM_ONLY = pl.BlockSpec(memory_space=pltpu.HBM)
SEM_SPEC = pl.BlockSpec(memory_space=pltpu.SEMAPHORE)
SIDE_EFFECT = pltpu.SideEffectType.DATAFLOW_SIDE_EFFECTING


def _chip_copies(p_refs, land_refs, send_sems, recv_sems):
    x, y, c = _mesh_pos()
    return [pltpu.make_async_remote_copy(
        src_ref=p_refs[a].at[2 * px + py], dst_ref=land_refs[a].at[j], send_sem=send_sems.at[3 * a + j],
        recv_sem=recv_sems.at[3 * a + j], device_id=(px, py, c), device_id_type=MESH)
        for a in range(len(p_refs)) for j, (px, py) in enumerate([(1 - x, y), (x, 1 - y), (1 - x, 1 - y)])]


def _chip_exchange_start(parts, name):
    n = len(parts)
    lands = [lax.empty((3,) + p.shape[1:], p.dtype) for p in parts]

    def body(*refs):
        p_refs, land_refs, (send_sems, recv_sems) = refs[:n], refs[n:2 * n], refs[2 * n:2 * n + 2]
        for cp in _chip_copies(p_refs, land_refs, send_sems, recv_sems):
            cp.start()
        token = refs[-1]
        token[...] = jnp.zeros_like(token)

    hbm = lambda t: pltpu.HBM(t.shape, t.dtype)
    res = pl.pallas_call(
        body, name=name,
        out_shape=(pltpu.SemaphoreType.DMA((3 * n,)), pltpu.SemaphoreType.DMA((3 * n,)), *[hbm(t) for t in parts + lands],
                   jax.ShapeDtypeStruct((8, 128), F32)),
        in_specs=(HBM_ONLY,) * (2 * n),
        out_specs=(SEM_SPEC, SEM_SPEC, *[HBM_ONLY] * (2 * n), pl.BlockSpec(memory_space=pltpu.VMEM)),
        input_output_aliases={i: 2 + i for i in range(2 * n)},
        compiler_params=pltpu.CompilerParams(has_side_effects=SIDE_EFFECT))(
        *[pltpu.with_memory_space_constraint(t, pltpu.HBM) for t in parts + lands])
    return (res[0], res[1], list(res[2:2 + n]), list(res[2 + n:2 + 2 * n])), res[-1]


def _chip_exchange_wait(in_flight, after, name):
    send_sems, recv_sems, parts, lands = in_flight
    n = len(parts)

    def body(*refs):
        p_refs, land_refs, (send_sems, recv_sems) = refs[:n], refs[n:2 * n], refs[2 * n:2 * n + 2]
        for cp in _chip_copies(p_refs, land_refs, send_sems, recv_sems):
            cp.wait_send()
            cp.wait_recv()

    res = pl.pallas_call(
        body, name=name, out_shape=tuple(pltpu.HBM(t.shape, t.dtype) for t in parts + lands),
        in_specs=(*[HBM_ONLY] * (2 * n), SEM_SPEC, SEM_SPEC, pl.BlockSpec(memory_space=pl.ANY)),
        out_specs=(HBM_ONLY,) * (2 * n), input_output_aliases={i: i for i in range(2 * n)},
        compiler_params=pltpu.CompilerParams(has_side_effects=SIDE_EFFECT))(*parts, *lands, send_sems, recv_sems, after)
    return list(res[:n]), list(res[n:])


def _slot(p):
    return 4 * p[0] + 2 * p[1] + p[2]


def _gather_copies(src_refs, out_refs, send_sems, recv_sems):
    x, y, c = _mesh_pos()
    targets = [(x, y, 1 - c), (1 - x, y, c), (x, 1 - y, c), (1 - x, 1 - y, c)]
    return [pltpu.make_async_remote_copy(
        src_ref=src_refs[a], dst_ref=out_refs[a].at[_slot((x, y, c))], send_sem=send_sems.at[4 * a + k],
        recv_sem=recv_sems.at[4 * a + k], device_id=to, device_id_type=MESH)
        for a in range(len(src_refs)) for k, to in enumerate(targets)]


def _gather_start(shards, after, name):
    n = len(shards)
    outs = [lax.empty((N_DEV,) + s.shape, s.dtype) for s in shards]

    def body(*refs):
        for cp in _gather_copies(refs[:n], refs[n:2 * n], refs[2 * n + 1], refs[2 * n + 2]):
            cp.start()
        token = refs[-1]
        token[...] = jnp.zeros_like(token)

    res = pl.pallas_call(
        body, name=name,
        out_shape=(pltpu.SemaphoreType.DMA((4 * n,)), pltpu.SemaphoreType.DMA((4 * n,)),
                   *[pltpu.HBM(t.shape, t.dtype) for t in shards + outs], jax.ShapeDtypeStruct((8, 128), F32)),
        in_specs=(*[HBM_ONLY] * (2 * n), pl.BlockSpec(memory_space=pl.ANY)),
        out_specs=(SEM_SPEC, SEM_SPEC, *[HBM_ONLY] * (2 * n), pl.BlockSpec(memory_space=pltpu.VMEM)),
        input_output_aliases={i: 2 + i for i in range(2 * n)},
        compiler_params=pltpu.CompilerParams(has_side_effects=SIDE_EFFECT))(
        *[pltpu.with_memory_space_constraint(t, pltpu.HBM) for t in shards + outs], after)
    return (res[0], res[1], list(res[2:2 + n]), list(res[2 + n:2 + 2 * n])), res[-1]


def _gather_wait(in_flight, after, name):
    send_sems, recv_sems, shards, outs = in_flight
    n = len(shards)

    def body(*refs):
        for cp in _gather_copies(refs[:n], refs[n:2 * n], refs[2 * n], refs[2 * n + 1]):
            cp.wait_send()
            cp.wait_recv()

    res = pl.pallas_call(
        body, name=name, out_shape=tuple(pltpu.HBM(t.shape, t.dtype) for t in shards + outs),
        in_specs=(*[HBM_ONLY] * (2 * n), SEM_SPEC, SEM_SPEC, pl.BlockSpec(memory_space=pl.ANY)),
        out_specs=(HBM_ONLY,) * (2 * n), input_output_aliases={i: i for i in range(2 * n)},
        compiler_params=pltpu.CompilerParams(has_side_effects=SIDE_EFFECT))(*shards, *outs, send_sems, recv_sems, after)
    return list(res[:n]), list(res[n:])


class _PassToSibling:
    def __init__(self, shards, gathered):
        n = self.n = len(shards)
        self.ins = list(shards) + list(gathered)
        self.out_shape = tuple(jax.ShapeDtypeStruct(g.shape, g.dtype) for g in gathered)
        self.aliases = {n + a: a for a in range(n)}
        self.sems = [pltpu.SemaphoreType.DMA((3 * n,)), pltpu.SemaphoreType.DMA((3 * n,)),
                     pltpu.SemaphoreType.DMA((n,))]

    def _copies(self, ins, outs, sems):
        send_sems, recv_sems, local_sems = sems
        x, y, c = _mesh_pos()
        chips = [(1 - x, y), (x, 1 - y), (1 - x, 1 - y)]
        mine = [pltpu.make_async_copy(ins[a], outs[a].at[_slot((x, y, c))], local_sems.at[a]) for a in range(self.n)]
        passed, awaited = [], []
        for a in range(self.n):
            for j, chip in enumerate(chips):
                sems_j = dict(send_sem=send_sems.at[3 * a + j], recv_sem=recv_sems.at[3 * a + j],
                              device_id=(x, y, 1 - c), device_id_type=MESH)
                blk = outs[a].at[_slot((*chip, c))]
                passed.append(pltpu.make_async_remote_copy(src_ref=blk, dst_ref=blk, **sems_j))
                got = outs[a].at[_slot((*chip, 1 - c))]
                awaited.append(pltpu.make_async_remote_copy(src_ref=got, dst_ref=got, **sems_j))
        return mine, passed, awaited

    def start(self, ins, outs, sems):
        mine, passed, _ = self._copies(ins, outs, sems)
        for cp in mine + passed:
            cp.start()

    def finish(self, ins, outs, sems):
        mine, passed, awaited = self._copies(ins, outs, sems)
        for cp in passed:
            cp.wait_send()
        for cp in awaited:
            cp.wait_recv()
        for cp in mine:
            cp.wait()


def _reduce_sums(fulls, recv_core, core, tag):
    return [_pair_sum(f, r, core, f"rs_pair_{tag}{i}") for i, (f, r) in enumerate(zip(fulls, recv_core))]


def _local_step(x, tgt, mods, w_in_shard, order, shards, small, chip, core):
    sh1, sc1, g1, sh2, sc2, g2 = mods
    norm1_g, rel_bias, gn_g, gn_b, norm2_g, norm_f_g = small
    tables = _ret_tables()
    buckets = jnp.asarray(_bucket_tables())

    proj, slabs, w_in_t, h1 = _gather_proj(x, norm1_g, sh1, sc1, w_in_shard, order)
    flight_w1, token_w = _gather_start(list(shards[:3]), proj, "gather_w1_start")
    flight_w2, token_w = _gather_start(list(shards[3:]), token_w, "gather_w2_start")
    bias = _bias_build(rel_bias, buckets, token_w)
    outs, lses = [], []
    for gi in range(len(ATT_GROUPS)):
        o, l = _att_fwd(slabs, bias, gi)
        outs.append(o)
        lses.append(l)
    att, gathered = _mix_fwd(outs, lses, comm=_PassToSibling(*_gather_wait(flight_w1, lses[2], "gather_w1_wait")))
    w_ret_out, w_att_out, w_o = (_from_slots(g, ax) for g, ax in zip(gathered, BIG_AXES[1:4]))
    gated, ro, states = _ret_fwd(proj, tables, gn_g, gn_b, att)
    ret_out, gathered = _mm(gated, w_ret_out, 'nn', tm=S, tn=256, tk=2048, name="ret_out",
                            comm=_PassToSibling(*_gather_wait(flight_w2, gated, "gather_w2_wait")))
    w_ff1, w_ff2 = (_from_slots(g, ax) for g, ax in zip(gathered, BIG_AXES[4:]))
    att_out, merged = _att_out_merge(att, w_att_out, proj, ret_out)
    mixo, x1, h2 = _w_o_norm2(merged, w_o, x, g1, norm2_g, sh2, sc2)
    u, act = _mm(h2, w_ff1, 'nn', tm=S, tn=512, tk=D, name="ff1", relu2=True)
    loss, dx2, g_normf, df, dg2 = _ff2_final(act, w_ff2, x1, g2, tgt, norm_f_g)

    gw_ff2 = _mm(act, df, 'tn', tm=512, tn=D, tk=S, name="gw_ff2", out_dtype=BF16)
    du = _mm(df, w_ff2, 'nt', tm=S, tn=512, tk=D, name="d_act", out_dtype=BF16, relu2_of=u)
    gw_ff1 = _mm(h2, du, 'tn', tm=D, tn=512, tk=S, name="gw_ff1", out_dtype=BF16)
    fulls_a = [_to_slots(g, ax) for g, ax in zip((gw_ff1, gw_ff2), BIG_AXES[4:])]
    dh2, recv_core_a = _mm(du, w_ff1, 'nt', tm=1024, tn=1024, tk=2048, name="dh2", comm=_ExchangeCore(fulls_a))
    parts_a = _reduce_sums(fulls_a, recv_core_a, core, "a")
    flight_a, token_a = _chip_exchange_start(parts_a, "rs_a_start")
    dx1, dsc2, dsh2, g_norm2, dmixo, dg1 = _norm_mod_bwd(x1, norm2_g, sc2, dh2, dx2, "norm2_bwd", gate=(mixo, g1))

    gw_o = _mm(merged, dmixo, 'tn', tm=D, tn=512, tk=S, name="gw_o", out_dtype=BF16, after=token_a)
    d_ret_out, d_att_out, dga, dgb = _dmerged_split(dmixo, w_o, proj, ret_out, att_out)
    gw_ret_out = _mm(gated, d_ret_out, 'tn', tm=512, tn=D, tk=S, name="gw_ret_out", out_dtype=BF16)
    gw_att_out = _mm(att, d_att_out, 'tn', tm=AW, tn=D, tk=S, name="gw_att_out", out_dtype=BF16)
    fulls_b = [_to_slots(g, ax) for g, ax in zip((gw_ret_out, gw_att_out, gw_o), BIG_AXES[1:4])]
    dgated, recv_core_b = _mm(d_ret_out, w_ret_out, 'nt', tm=S, tn=512, tk=D, name="dgated",
                              comm=_ExchangeCore(fulls_b))
    parts_b = _reduce_sums(fulls_b, recv_core_b, core, "b")
    flight_b, token_b = _chip_exchange_start(parts_b, "rs_b_start")
    datt = _mm(d_att_out, w_att_out, 'nt', tm=S, tn=AW, tk=D, name="datt", after=token_b)
    mix_grads = _mix_bwd(outs, lses, datt)
    datt_parts, ds_sums = [], []
    for gi in range(len(ATT_GROUPS)):
        dq, dk, dv, ds_sum = _att_bwd(slabs, bias, outs[gi], lses[gi], mix_grads[gi], mix_grads[3 + gi], gi)
        datt_parts += [dq, dk, dv]
        ds_sums.append(ds_sum)
    g_bias = _bias_grad(jnp.concatenate(ds_sums, axis=0), buckets)[:, :, 0].T.reshape(1, -1)
    dproj, g_gn_g, g_gn_b = _ret_bwd(proj, tables, gn_g, gn_b, ro, states, dgated, datt_parts + [dga, dgb])
    parts_a, recv_chip_a = _chip_exchange_wait(flight_a, dproj, "rs_a_wait")
    parts_b, recv_chip_b = _chip_exchange_wait(flight_b, dproj, "rs_b_wait")
    reduced = list(zip(parts_b + parts_a, recv_chip_b + recv_chip_a))
    full_in = _to_slots(_mm(dproj, h1, 'tn', tm=512, tn=D, tk=S, name="gw_in", out_dtype=BF16), 0)
    halves = [(half * (D // 2), D // 2) for half in range(2)]
    (recv_core_in0,) = _run_comm(_ExchangeCore([full_in], cols=halves[0]), "rs_core_in0")
    flight0, token = _chip_exchange_start([_pair_sum(full_in, recv_core_in0, core, "rs_pair_c0", col_block=0)],
                                          "rs_in0_start")
    dh1, (recv_core_in1,) = _mm(dproj, w_in_t, 'nn', tm=1024, tn=1024, tk=2560, name="dh1",
                                comm=_ExchangeCore([full_in], cols=halves[1]), after=token)
    flight1, token = _chip_exchange_start([_pair_sum(full_in, recv_core_in1, core, "rs_pair_c1", col_block=1)],
                                          "rs_in1_start")
    in_flight = [flight0, flight1]
    gx, dsc1, dsh1, g_norm1 = _norm_mod_bwd(x, norm1_g, sc1, dh1, dx1, "norm1_bwd", after=token)

    dmod = [dsh1, dsc1, dg1, dsh2, dsc2, dg2]
    small_g = [g_norm1, g_bias, g_gn_g, g_gn_b, g_norm2, g_normf]
    return loss, gx, in_flight, reduced, small_g, dmod


def _to_slots(g, axis):
    if axis == 0:
        return g.reshape(4, 2, g.shape[0] // N_DEV, g.shape[1])
    return g.reshape(g.shape[0], N_DEV, g.shape[1] // N_DEV).transpose(1, 0, 2).reshape(4, 2, g.shape[0], -1)


def _from_slots(w8, axis):
    if axis == 0:
        return w8.reshape(-1, w8.shape[2])
    return w8.transpose(1, 0, 2).reshape(w8.shape[1], -1)


BIG_AXES = (1, 0, 1, 0, 1, 0)


def kernel(x, c, w_ada, b_ada, norm1_g, w_in, rel_bias, ret_gn_g, ret_gn_b, w_ret_out, w_att_out, w_o, norm2_g, w_ff1, w_ff2, norm_f_g, loss_target, m_w_ada, m_b_ada, m_norm1_g, m_w_in, m_rel_bias, m_ret_gn_g, m_ret_gn_b, m_w_ret_out, m_w_att_out, m_w_o, m_norm2_g, m_w_ff1, m_w_ff2, m_norm_f_g, v_w_ada, v_b_ada, v_norm1_g, v_w_in, v_rel_bias, v_ret_gn_g, v_ret_gn_b, v_w_ret_out, v_w_att_out, v_w_o, v_norm2_g, v_w_ff1, v_w_ff2, v_norm_f_g):
    mx, my, mc = _mesh_pos()
    dev = 4 * mx + 2 * my + mc
    chip = jnp.reshape(2 * mx + my, (1,)).astype(jnp.int32)
    core = jnp.reshape(mc, (1,)).astype(jnp.int32)
    ada_w = D * 6 // N_DEV

    w_in, m_w_in, v_w_in = (jnp.transpose(t, (0, 2, 1)) for t in (w_in, m_w_in, v_w_in))

    (w_in_shard,), (c_all,) = _to_bf16([w_in[0]], _Gather([c]), "gather_c")
    c_all = c_all.reshape(N_DEV, D)
    b_sl = lax.dynamic_slice(b_ada, (0, dev * ada_w), (1, ada_w))
    other_shards, (mod_all,) = _to_bf16([w[0] for w in (w_ret_out, w_att_out, w_o, w_ff1, w_ff2)],
                                        _Gather([_ada_fwd(c_all, w_ada[0], b_sl)]), "gather_mod")
    mod = lax.dynamic_index_in_dim(mod_all, dev, axis=1, keepdims=False).reshape(6, D)
    mods = tuple(mod[i:i + 1] for i in range(6))

    small = (norm1_g, rel_bias, ret_gn_g, ret_gn_b, norm2_g, norm_f_g.reshape(1, D))
    order = lax.dynamic_index_in_dim(jnp.asarray(_proj_order()), 2 * mx + my, axis=0, keepdims=False)
    loss, gx, in_flight, big_red, small_g, dmod = _local_step(x[0], loss_target[0], mods, w_in_shard, order,
                                                              list(other_shards), small, chip, core)

    names = ['w_ada', 'b_ada', 'norm1_g', 'w_in', 'rel_bias', 'ret_gn_g', 'ret_gn_b', 'w_ret_out', 'w_att_out',
             'w_o', 'norm2_g', 'w_ff1', 'w_ff2', 'norm_f_g']
    ws = dict(zip(names, (w_ada, b_ada, norm1_g, w_in, rel_bias, ret_gn_g, ret_gn_b, w_ret_out, w_att_out, w_o,
                          norm2_g, w_ff1, w_ff2, norm_f_g)))
    ms = dict(zip(names, (m_w_ada, m_b_ada, m_norm1_g, m_w_in, m_rel_bias, m_ret_gn_g, m_ret_gn_b, m_w_ret_out,
                          m_w_att_out, m_w_o, m_norm2_g, m_w_ff1, m_w_ff2, m_norm_f_g)))
    vs = dict(zip(names, (v_w_ada, v_b_ada, v_norm1_g, v_w_in, v_rel_bias, v_ret_gn_g, v_ret_gn_b, v_w_ret_out,
                          v_w_att_out, v_w_o, v_norm2_g, v_w_ff1, v_w_ff2, v_norm_f_g)))
    grads, delta, new_m, new_v = {}, {}, {}, {}
    big_names = ('w_ret_out', 'w_att_out', 'w_o', 'w_ff1', 'w_ff2')
    for n, (part, recv) in zip(big_names, big_red):
        grads[n], delta[n], new_m[n], new_v[n] = _adamw_reduced1(ws[n], ms[n], vs[n], part, recv, chip, "adamw_" + n)
    updated = lax.optimization_barrier((gx, tuple(delta[n] for n in big_names)))
    gathered = _run_comm(_Gather(dmod + small_g + [loss]), "gather_small", after=updated[0])
    g_b_ada, dmod_all, (g_norm1, g_bias, g_gn_g, g_gn_b, g_norm2, g_normf, loss_sum) = _sum_small(gathered)
    loss_out = loss_sum[0, 0]
    g_w_ada = _ada_bwd(c_all, lax.dynamic_slice(dmod_all, (0, dev * ada_w), (N_DEV, ada_w)))

    grads.update(w_ada=g_w_ada.reshape(w_ada.shape), b_ada=g_b_ada, norm1_g=g_norm1, rel_bias=g_bias,
                 ret_gn_g=g_gn_g, ret_gn_b=g_gn_b, norm2_g=g_norm2, norm_f_g=g_normf)
    delta['w_ada'], new_m['w_ada'], new_v['w_ada'] = _adamw(w_ada, g_w_ada, m_w_ada, v_w_ada, "adamw_w_ada")
    small_names = ('b_ada', 'norm1_g', 'rel_bias', 'ret_gn_g', 'ret_gn_b', 'norm2_g', 'norm_f_g')
    two_d = {n: (1, ws[n].size) if ws[n].ndim == 1 else ws[n].shape for n in small_names}
    d_, m_, v_ = _adamw_small(*[[src[n].reshape(two_d[n]) for n in small_names] for src in (ws, grads, ms, vs)])
    for i, n in enumerate(small_names):
        shp = ws[n].shape
        delta[n], new_m[n], new_v[n] = d_[i].reshape(shp), m_[i].reshape(shp), v_[i].reshape(shp)
        grads[n] = grads[n].reshape(shp)

    done = lax.optimization_barrier((gx, tuple(d_), tuple(delta[n] for n in ('w_ada', 'w_ret_out', 'w_att_out', 'w_o',
                                                                               'w_ff1', 'w_ff2'))))
    parts_in, recvs_in = [], []
    for half, flight in enumerate(in_flight):
        (part_in,), (recv_chip_in,) = _chip_exchange_wait(flight, done[0], f"rs_in{half}_wait")
        parts_in.append(part_in)
        recvs_in.append(recv_chip_in)
    grads['w_in'], delta['w_in'], new_m['w_in'], new_v['w_in'] = _adamw_reduced(w_in, m_w_in, v_w_in, parts_in,
                                                                               recvs_in, chip)
    for d in (grads, delta, new_m, new_v):
        d['w_in'] = jnp.transpose(d['w_in'], (0, 2, 1))
    return (loss_out, gx[None], *[grads[n] for n in names], *[delta[n] for n in names],
            *[new_m[n] for n in names], *[new_v[n] for n in names])
```

```python
import functools
import math

import numpy as np
import jax
import jax.numpy as jnp
from jax import lax
from jax.experimental import pallas as pl
from jax.experimental.pallas import tpu as pltpu

F32 = jnp.float32
BF16 = jnp.bfloat16
MESH = pl.DeviceIdType.MESH

N_DEV = 8
S = 2048
D = 1024
RET_HEADS = 4
RET_DK = 256
RET_DV = 512
CHUNK = 128
N_CHUNK = S // CHUNK
ATT_GROUPS = ((128, 1), (512, 4), (2048, 16))
ATT_HG = 4
ATT_DH = 128
ATT_BLK = 128
N_BUCKETS = 32
MAX_DIST = 2048
D_FF = 4096
IN_COLS = 12800
OFF_RQ, OFF_RK, OFF_RV, OFF_RG, OFF_ATT = 0, 1024, 2048, 4096, 6144
OFF_GA, OFF_GB = 6144, 7168
RMS_EPS = 1e-6
GN_EPS = 1e-5
ADAM_LR, ADAM_B1, ADAM_B2, ADAM_EPS, ADAM_WD, ADAM_STEP = 0.001, 0.9, 0.999, 1e-08, 0.01, 10
VMEM_LIMIT = 48 * 1024 * 1024


def _pcall(body, **kw):
    return pl.pallas_call(body, **kw)


def _params(sem=None):
    return pltpu.CompilerParams(dimension_semantics=sem, vmem_limit_bytes=VMEM_LIMIT)


HBM_SPEC = pl.BlockSpec(memory_space=pl.ANY)


def _carry(body, comm, *, name, grid, in_specs, out_specs, out_shape, scratch_shapes=()):
    single = not isinstance(out_specs, (tuple, list))
    o_specs = (out_specs,) if single else tuple(out_specs)
    o_shape = (out_shape,) if single else tuple(out_shape)
    n_in, n_out, n_scr = len(in_specs), len(o_specs), len(scratch_shapes)
    nci, nco = len(comm.ins), len(comm.out_shape)
    total = int(np.prod(grid))

    def wrapped(*refs):
        bounds = np.cumsum([0, n_in, nci, n_out, nco, n_scr])
        a, ci, o, co, scr = (refs[bounds[i]:bounds[i + 1]] for i in range(5))
        sems = refs[bounds[5]:]
        flat = 0
        for d, g in enumerate(grid):
            flat = flat * g + pl.program_id(d)

        @pl.when(flat == 0)
        def _():
            comm.start(ci, co, sems)

        body(*a, *o, *scr)

        @pl.when(flat == total - 1)
        def _():
            comm.finish(ci, co, sems)

    aliases = {n_in + i: n_out + o for i, o in getattr(comm, "aliases", {}).items()}
    call = _pcall(wrapped, name=name, grid=grid, in_specs=list(in_specs) + [HBM_SPEC] * nci,
                  out_specs=o_specs + (HBM_SPEC,) * nco, out_shape=o_shape + tuple(comm.out_shape),
                  scratch_shapes=list(scratch_shapes) + list(comm.sems), input_output_aliases=aliases,
                  compiler_params=_params(("arbitrary",) * len(grid)))

    def run(*args):
        res = call(*args, *comm.ins)
        own = res[0] if single else tuple(res[:n_out])
        return own, tuple(res[n_out:])

    return run


def _run_comm(comm, name, after=None):
    nci, nco = len(comm.ins), len(comm.out_shape)
    extra = [] if after is None else [after]

    def body(*refs):
        ci, co, sems = refs[:nci], refs[nci + len(extra):nci + len(extra) + nco], refs[nci + len(extra) + nco:]
        comm.start(ci, co, sems)
        comm.finish(ci, co, sems)

    return _pcall(body, name=name, in_specs=[HBM_SPEC] * (nci + len(extra)), out_specs=(HBM_SPEC,) * nco,
                  out_shape=tuple(comm.out_shape), scratch_shapes=list(comm.sems))(*comm.ins, *extra)


def _dot(a, b, dn):
    return lax.dot_general(a.astype(BF16), b.astype(BF16), (dn, ((), ())), preferred_element_type=F32)


NN = ((1,), (0,))
NT = ((1,), (1,))
TN = ((0,), (0,))


def _mm(a, b, mode, *, tm, tn, tk, name, out_dtype=F32, res=None, gvec=None, relu2=False, relu2_of=None, comm=None,
        after=None):
    if mode == 'nn':
        (M, K), (_, N) = a.shape, b.shape
        a_spec = pl.BlockSpec((tm, tk), lambda i, j, k: (i, k))
        b_spec = pl.BlockSpec((tk, tn), lambda i, j, k: (k, j))
        dn = NN
    elif mode == 'nt':
        (M, K), (N, _) = a.shape, b.shape
        a_spec = pl.BlockSpec((tm, tk), lambda i, j, k: (i, k))
        b_spec = pl.BlockSpec((tn, tk), lambda i, j, k: (j, k))
        dn = NT
    else:
        (K, M), (_, N) = a.shape, b.shape
        a_spec = pl.BlockSpec((tk, tm), lambda i, j, k: (k, i))
        b_spec = pl.BlockSpec((tk, tn), lambda i, j, k: (k, j))
        dn = TN
    assert M % tm == 0 and N % tn == 0 and K % tk == 0, (name, M, N, K)
    nk = K // tk
    fused = res is not None
    o_spec = pl.BlockSpec((tm, tn), lambda i, j, k: (i, j))

    def body(a_ref, b_ref, *rest):
        acc_ref = rest[-1] if nk > 1 else None
        if after is not None:
            rest = rest[1:]
        if fused:
            res_ref, g_ref, o_ref, x_ref = rest[:4]
        elif relu2_of is not None:
            u_ref, o_ref = rest[:2]
        elif relu2:
            o_ref, act_ref = rest[:2]
        else:
            o_ref = rest[0]

        def finish(acc):
            if relu2_of is not None:
                acc = acc * (2.0 * jnp.maximum(u_ref[...], 0.0))
            o_ref[...] = acc.astype(o_ref.dtype)
            if fused:
                x_ref[...] = res_ref[...] + g_ref[...] * acc
            if relu2:
                r = jnp.maximum(acc, 0.0)
                act_ref[...] = (r * r).astype(BF16)

        p = _dot(a_ref[...], b_ref[...], dn)
        if nk == 1:
            finish(p)
        else:
            k = pl.program_id(2)

            @pl.when(k == 0)
            def _():
                acc_ref[...] = p

            @pl.when(k > 0)
            def _():
                acc_ref[...] += p

            @pl.when(k == nk - 1)
            def _():
                finish(acc_ref[...])

    in_specs = [a_spec, b_spec]
    args = [a, b]
    if after is not None:
        in_specs.append(pl.BlockSpec(memory_space=pl.ANY))
        args.append(after)
    out_shape = jax.ShapeDtypeStruct((M, N), out_dtype)
    out_specs = o_spec
    if fused:
        in_specs += [pl.BlockSpec((tm, tn), lambda i, j, k: (i, j)), pl.BlockSpec((1, tn), lambda i, j, k: (0, j))]
        args += [res, gvec]
        out_shape = (out_shape, jax.ShapeDtypeStruct((M, N), F32))
        out_specs = (o_spec, pl.BlockSpec((tm, tn), lambda i, j, k: (i, j)))
    elif relu2_of is not None:
        in_specs.append(pl.BlockSpec((tm, tn), lambda i, j, k: (i, j)))
        args.append(relu2_of)
    elif relu2:
        out_shape = (out_shape, jax.ShapeDtypeStruct((M, N), BF16))
        out_specs = (o_spec, pl.BlockSpec((tm, tn), lambda i, j, k: (i, j)))
    kw = dict(name=name, grid=(M // tm, N // tn, nk), in_specs=in_specs, out_specs=out_specs,
              out_shape=out_shape, scratch_shapes=[pltpu.VMEM((tm, tn), F32)] if nk > 1 else [])
    if comm is not None:
        return _carry(body, comm, **kw)(*args)
    return _pcall(body, compiler_params=_params(("parallel", "parallel", "arbitrary")), **kw)(*args)


PROJ_TN = 512
ATT_T0, ATT_T1 = 6144 // PROJ_TN, 10752 // PROJ_TN
N_SLABS = (ATT_T1 - ATT_T0) * 4
MAIN_COLS = IN_COLS - (ATT_T1 - ATT_T0) * PROJ_TN


PROJ_TILES = IN_COLS // PROJ_TN
SHARD_ROWS = IN_COLS // N_DEV
W_CHUNKS = 4
N_OWN, N_NEAR = 5, 18


def _proj_order():
    out = np.zeros((4, 3, PROJ_TILES), np.int32)
    for q in range(4):
        def hops(t):
            owners = {col // (2 * SHARD_ROWS) for col in (t * PROJ_TN, (t + 1) * PROJ_TN - 1)}
            return max(bin(q ^ p).count("1") for p in owners)
        order = sorted(range(PROJ_TILES), key=lambda t: (hops(t), t))
        assert all(hops(t) == 0 for t in order[:N_OWN]) and all(hops(t) < 2 for t in order[:N_NEAR])
        is_att = [ATT_T0 <= t < ATT_T1 for t in order]
        for row, kind, index in ((1, False, lambda t: t if t < ATT_T0 else t - (ATT_T1 - ATT_T0)),
                                 (2, True, lambda t: t - ATT_T0)):
            own = [index(t) if a == kind else None for t, a in zip(order, is_att)]
            first = next(v for v in own if v is not None)
            last = first
            for j, v in enumerate(own):
                last = last if v is None else v
                out[q, row, j] = last
        out[q, 0] = order
    return out


def _gather_proj(x, g, sh, sc, shard, order):
    rows = SHARD_ROWS // W_CHUNKS

    def body(ord_ref, x_ref, g_ref, shift_ref, scale_ref, sh_ref, main_ref, slab_ref, full_ref, a_ref, wbuf, xbuf,
             fetch_sems, send_sems, recv_sems, local_sems, x_sem):
        j = pl.program_id(0)
        x, y, c = _mesh_pos()
        me, sibling = (x, y, c), (x, y, 1 - c)
        chips = [(1 - x, y), (x, 1 - y), (1 - x, 1 - y)]

        def block(p, owner):
            return full_ref.at[pl.ds(pl.multiple_of(_slot(owner) * SHARD_ROWS + p * rows, 16), rows)]

        def copy(p, k, owner, to, from_input=False):
            dst = block(p, owner)
            return pltpu.make_async_remote_copy(
                src_ref=sh_ref.at[pl.ds(p * rows, rows)] if from_input else dst, dst_ref=dst,
                send_sem=send_sems.at[7 * p + k], recv_sem=recv_sems.at[7 * p + k], device_id=to, device_id_type=MESH)

        pieces = range(W_CHUNKS)
        mine = [pltpu.make_async_copy(sh_ref.at[pl.ds(p * rows, rows)], block(p, me), local_sems.at[p]) for p in pieces]
        first = [copy(p, 0, me, sibling, from_input=True) for p in pieces]
        first += [copy(p, 1 + n, me, (*chips[n], c), from_input=True) for p in pieces for n in range(2)]
        near_pass = [copy(p, 4 + n, (*chips[n], c), sibling) for p in pieces for n in range(2)]
        relay = [copy(p, 3, ((x + 1 - c) % 2, (y + c) % 2, c), ((x + c) % 2, (y + 1 - c) % 2, c)) for p in pieces]
        far_pass = [copy(p, 6, (*chips[2], c), sibling) for p in pieces]

        def fetch(pos):
            slot = lax.rem(pos, 2)
            start = pl.multiple_of(ord_ref[0, pos] * PROJ_TN, PROJ_TN)
            return pltpu.make_async_copy(full_ref.at[pl.ds(start, PROJ_TN)], wbuf.at[slot], fetch_sems.at[slot])

        @pl.when(j == 0)
        def _():
            x_copy = pltpu.make_async_copy(x_ref, xbuf, x_sem.at[0])
            x_copy.start()
            for cp in mine + first:
                cp.start()
            x_copy.wait()
            for r in range(S // TR):
                rws = pl.ds(r * TR, TR)
                xv = xbuf[rws, :]
                rstd = lax.rsqrt(jnp.mean(xv * xv, axis=-1, keepdims=True) + RMS_EPS)
                n = xv * rstd * g_ref[...]
                a_ref[rws, :] = (n * (1.0 + scale_ref[...]) + shift_ref[...]).astype(BF16)
            for cp in mine:
                cp.wait()
            for p in pieces:
                copy(p, 0, sibling, me).wait_recv()
            fetch(j).start()

        @pl.when(j == N_OWN - 1)
        def _():
            for p in pieces:
                for n in range(2):
                    copy(p, 1 + n, (*chips[n], c), me).wait_recv()
                    near_pass[2 * p + n].start()
                relay[p].start()
            for p in pieces:
                for n in range(2):
                    copy(p, 4 + n, (*chips[n], 1 - c), me).wait_recv()

        @pl.when(j == N_NEAR - 1)
        def _():
            for p in pieces:
                copy(p, 3, (*chips[2], c), me).wait_recv()
                far_pass[p].start()
            for p in pieces:
                copy(p, 6, (*chips[2], 1 - c), me).wait_recv()

        @pl.when(j + 1 < PROJ_TILES)
        def _():
            fetch(j + 1).start()

        fetch(j).wait()
        w_ref = wbuf.at[lax.rem(j, 2)]
        tile = ord_ref[0, j]
        is_att = (tile >= ATT_T0) & (tile < ATT_T1)
        chunks = [pl.ds(r * 512, 512) for r in range(S // 512)]

        @pl.when(jnp.logical_not(is_att))
        def _():
            for rws in chunks:
                main_ref[rws, :] = _dot(a_ref[rws, :], w_ref[...], NT)

        @pl.when(is_att)
        def _():
            for rws in chunks:
                p = _dot(a_ref[rws, :], w_ref[...], NT)
                for h in range(4):
                    slab_ref[h, rws, :] = p[:, h * 128:(h + 1) * 128]

        @pl.when(j == PROJ_TILES - 1)
        def _():
            for cp in first + near_pass + relay + far_pass:
                cp.wait_send()

    vec = pl.BlockSpec((1, D), lambda j, o: (0, 0))
    gs = pltpu.PrefetchScalarGridSpec(
        num_scalar_prefetch=1, grid=(PROJ_TILES,),
        in_specs=[HBM_SPEC, vec, vec, vec, HBM_SPEC],
        out_specs=(pl.BlockSpec((S, PROJ_TN), lambda j, o: (0, o[1, j])),
                   pl.BlockSpec((4, S, 128), lambda j, o: (o[2, j], 0, 0)), HBM_SPEC,
                   pl.BlockSpec((S, D), lambda j, o: (0, 0))),
        scratch_shapes=[pltpu.VMEM((2, PROJ_TN, D), BF16), pltpu.VMEM((S, D), F32), pltpu.SemaphoreType.DMA((2,)),
                        pltpu.SemaphoreType.DMA((7 * W_CHUNKS,)), pltpu.SemaphoreType.DMA((7 * W_CHUNKS,)),
                        pltpu.SemaphoreType.DMA((W_CHUNKS,)), pltpu.SemaphoreType.DMA((1,))])
    return _pcall(body, name="gather_proj", grid_spec=gs,
                  out_shape=(jax.ShapeDtypeStruct((S, MAIN_COLS), F32), jax.ShapeDtypeStruct((N_SLABS, S, 128), F32),
                             jax.ShapeDtypeStruct((IN_COLS, D), BF16), jax.ShapeDtypeStruct((S, D), BF16)),
                  compiler_params=_params(("arbitrary",)))(order, x, g, sh, sc, shard)


TR = 256


def _row_spec(w=D):
    return pl.BlockSpec((TR, w), lambda i: (i, 0))


def _vec_spec(w=D):
    return pl.BlockSpec((1, w), lambda i: (0, 0))


def _norm_mod_bwd(x, g, sc, dh, dres, name, gate=None, after=None):
    gated = gate is not None

    def body(x_ref, g_ref, sc_ref, dh_ref, dres_ref, *rest):
        if after is not None:
            rest = rest[1:]
        if gated:
            f_ref, gv_ref, dx_ref, dsc_ref, dsh_ref, dg_ref, dz_ref, dgv_ref = rest
        else:
            dx_ref, dsc_ref, dsh_ref, dg_ref = rest
        i = pl.program_id(0)
        xv = x_ref[...]
        dh = dh_ref[...]
        rstd = lax.rsqrt(jnp.mean(xv * xv, axis=-1, keepdims=True) + RMS_EPS)
        xhat = xv * rstd
        gv = g_ref[...]
        dn = dh * (1.0 + sc_ref[...])
        dxhat = dn * gv
        dx = dres_ref[...] + rstd * (dxhat - xhat * jnp.mean(dxhat * xhat, axis=-1, keepdims=True))
        dx_ref[...] = dx
        sums = [(dsc_ref, jnp.sum(dh * (xhat * gv), axis=0, keepdims=True)),
                (dsh_ref, jnp.sum(dh, axis=0, keepdims=True)),
                (dg_ref, jnp.sum(dn * xhat, axis=0, keepdims=True))]
        if gated:
            dz_ref[...] = (dx * gv_ref[...]).astype(BF16)
            sums.append((dgv_ref, jnp.sum(dx * f_ref[...], axis=0, keepdims=True)))

        @pl.when(i == 0)
        def _():
            for ref, p in sums:
                ref[...] = p

        @pl.when(i > 0)
        def _():
            for ref, p in sums:
                ref[...] += p

    vec = jax.ShapeDtypeStruct((1, D), F32)
    in_specs = [_row_spec(), _vec_spec(), _vec_spec(), _row_spec(), _row_spec()]
    out_specs = [_row_spec(), _vec_spec(), _vec_spec(), _vec_spec()]
    out_shape = [jax.ShapeDtypeStruct((S, D), F32), vec, vec, vec]
    args = [x, g, sc, dh, dres]
    if after is not None:
        in_specs.append(HBM_SPEC)
        args.append(after)
    if gated:
        in_specs += [_row_spec(), _vec_spec()]
        out_specs += [_row_spec(), _vec_spec()]
        out_shape += [jax.ShapeDtypeStruct((S, D), BF16), vec]
        args += list(gate)
    return _pcall(body, name=name, grid=(S // TR,), in_specs=in_specs, out_specs=tuple(out_specs),
                  out_shape=tuple(out_shape), compiler_params=_params(("arbitrary",)))(*args)


def _w_o_norm2(merged, w_o, x, g1, g, sh, sc):
    def body(a_ref, b_ref, x_ref, g1_ref, g_ref, sh_ref, sc_ref, o_ref, x1_ref, h_ref):
        acc = _dot(a_ref[...], b_ref[...], NN)
        o_ref[...] = acc
        xv = x_ref[...] + g1_ref[...] * acc
        x1_ref[...] = xv
        rstd = lax.rsqrt(jnp.mean(xv * xv, axis=-1, keepdims=True) + RMS_EPS)
        h_ref[...] = (xv * rstd * g_ref[...] * (1.0 + sc_ref[...]) + sh_ref[...]).astype(BF16)

    rows = pl.BlockSpec((FF2_TM, D), lambda i: (i, 0))
    f32 = jax.ShapeDtypeStruct((S, D), F32)
    return _pcall(body, name="w_o_norm2", grid=(S // FF2_TM,),
                  in_specs=[rows, pl.BlockSpec((D, D), lambda i: (0, 0)), rows] + [_vec_spec()] * 4,
                  out_specs=(rows, rows, rows), out_shape=(f32, f32, jax.ShapeDtypeStruct((S, D), BF16)),
                  compiler_params=_params(("parallel",)))(merged, w_o, x, g1, g, sh, sc)


FF2_TM = 512


def _ff2_final(act, w_ff2, x1, g2, tgt, g):
    def body(a_ref, b_ref, x1_ref, g2_ref, t_ref, g_ref, loss_ref, dx_ref, dg_ref, df_ref, dg2_ref):
        i = pl.program_id(0)
        f = _dot(a_ref[...], b_ref[...], NN)
        g2v = g2_ref[...]
        xv = x1_ref[...] + g2v * f
        gv = g_ref[...]
        rstd = lax.rsqrt(jnp.mean(xv * xv, axis=-1, keepdims=True) + RMS_EPS)
        xhat = xv * rstd
        err = xhat * gv - t_ref[...]
        dy = err * (1.0 / D)
        dxhat = dy * gv
        dx = rstd * (dxhat - xhat * jnp.mean(dxhat * xhat, axis=-1, keepdims=True))
        dx_ref[...] = dx
        df_ref[...] = (dx * g2v).astype(BF16)
        p_g = jnp.sum(dy * xhat, axis=0, keepdims=True)
        p_g2 = jnp.sum(dx * f, axis=0, keepdims=True)
        p_l = jnp.zeros((1, 128), F32) + 0.5 * jnp.sum(jnp.mean(err * err, axis=-1, keepdims=True))

        @pl.when(i == 0)
        def _():
            dg_ref[...] = p_g
            dg2_ref[...] = p_g2
            loss_ref[...] = p_l

        @pl.when(i > 0)
        def _():
            dg_ref[...] += p_g
            dg2_ref[...] += p_g2
            loss_ref[...] += p_l

    vec = jax.ShapeDtypeStruct((1, D), F32)
    rows = lambda w: pl.BlockSpec((FF2_TM, w), lambda i: (i, 0))
    return _pcall(body, name="ff2_final", grid=(S // FF2_TM,),
                  in_specs=[rows(D_FF), pl.BlockSpec((D_FF, D), lambda i: (0, 0)), rows(D), _vec_spec(), rows(D),
                            _vec_spec()],
                  out_specs=(_vec_spec(128), rows(D), _vec_spec(), rows(D), _vec_spec()),
                  out_shape=(jax.ShapeDtypeStruct((1, 128), F32), jax.ShapeDtypeStruct((S, D), F32), vec,
                             jax.ShapeDtypeStruct((S, D), BF16), vec),
                  compiler_params=_params(("arbitrary",)))(act, w_ff2, x1, g2, tgt, g)


HALF = 512


MERGE_TM = 1024


def _merge_specs():
    blk = lambda off: pl.BlockSpec((MERGE_TM, HALF), lambda i, j: (i, off // HALF + j))
    return blk(OFF_GA), blk(OFF_GB), blk(0)


def _att_out_merge(att, w_att_out, proj, ret_out):
    def body(a_ref, b_ref, ga_ref, gb_ref, r_ref, o_ref, m_ref):
        acc = _dot(a_ref[...], b_ref[...], NN)
        o_ref[...] = acc
        m_ref[...] = (jax.nn.sigmoid(ga_ref[...]) * r_ref[...] + jax.nn.sigmoid(gb_ref[...]) * acc).astype(BF16)

    ga, gb, tile = _merge_specs()
    return _pcall(body, name="att_out", grid=(S // MERGE_TM, D // HALF),
                  in_specs=[pl.BlockSpec((MERGE_TM, AW), lambda i, j: (i, 0)), pl.BlockSpec((AW, HALF), lambda i, j: (0, j)),
                            ga, gb, tile],
                  out_specs=(tile, tile),
                  out_shape=(jax.ShapeDtypeStruct((S, D), F32), jax.ShapeDtypeStruct((S, D), BF16)),
                  compiler_params=_params(("parallel", "parallel")))(att, w_att_out, proj, proj, ret_out)


def _dmerged_split(dmixo, w_o, proj, ret_out, att_out):
    def body(a_ref, b_ref, ga_ref, gb_ref, r_ref, at_ref, dr_ref, da_ref, dga_ref, dgb_ref):
        dm = _dot(a_ref[...], b_ref[...], NT)
        sa = jax.nn.sigmoid(ga_ref[...])
        sb = jax.nn.sigmoid(gb_ref[...])
        dr_ref[...] = (dm * sa).astype(BF16)
        da_ref[...] = (dm * sb).astype(BF16)
        dga_ref[...] = (dm * r_ref[...] * (sa * (1.0 - sa))).astype(BF16)
        dgb_ref[...] = (dm * at_ref[...] * (sb * (1.0 - sb))).astype(BF16)

    ga, gb, tile = _merge_specs()
    o = jax.ShapeDtypeStruct((S, D), BF16)
    return _pcall(body, name="dmerged", grid=(S // MERGE_TM, D // HALF),
                  in_specs=[pl.BlockSpec((MERGE_TM, D), lambda i, j: (i, 0)), pl.BlockSpec((HALF, D), lambda i, j: (j, 0)),
                            ga, gb, tile, tile],
                  out_specs=(tile,) * 4, out_shape=(o, o, o, o),
                  compiler_params=_params(("parallel", "parallel")))(dmixo, w_o, proj, proj, ret_out, att_out)


def _ret_tables():
    H, C = RET_HEADS, CHUNK
    log_g = jnp.log1p(-(2.0 ** (-5.0 - jnp.arange(H, dtype=F32))))
    idx = jnp.arange(C, dtype=F32)
    rel = idx[:, None] - idx[None, :]
    inner = jnp.where(rel >= 0, jnp.exp(log_g[:, None, None] * jnp.maximum(rel, 0.0)), 0.0)
    qd = jnp.exp(log_g[:, None] * (idx + 1.0))[:, :, None]
    kd = jnp.exp(log_g[:, None] * (C - 1.0 - idx))[:, :, None]
    cd = jnp.broadcast_to(jnp.exp(log_g * C)[:, None, None], (H, 1, 128))
    half = RET_DK // 2
    inv = 10000.0 ** (-jnp.arange(half, dtype=F32) / half)
    ang = jnp.arange(S, dtype=F32)[:, None] * inv[None, :]
    return inner, qd, kd, cd, jnp.cos(ang), jnp.sin(ang)


def _rot(x, cos, sin):
    x1, x2 = x[:, :128], x[:, 128:]
    return jnp.concatenate([x1 * cos - x2 * sin, x1 * sin + x2 * cos], axis=1)


def _rot_t(d, cos, sin):
    d1, d2 = d[:, :128], d[:, 128:]
    return jnp.concatenate([d1 * cos + d2 * sin, d2 * cos - d1 * sin], axis=1)


RET_COLS = OFF_ATT
RET_VW = RET_HEADS * RET_DV


def _ret_specs(chunk_of):
    ci = chunk_of
    whole = lambda shape: pl.BlockSpec(shape, lambda t: (0,) * len(shape))
    return [
        pl.BlockSpec((CHUNK, RET_COLS), lambda t: (ci(t), 0)),
        pl.BlockSpec((CHUNK, 128), lambda t: (ci(t), 0)),
        pl.BlockSpec((CHUNK, 128), lambda t: (ci(t), 0)),
        whole((RET_HEADS, CHUNK, CHUNK)), whole((RET_HEADS, CHUNK, 1)), whole((RET_HEADS, CHUNK, 1)),
        whole((RET_HEADS, 1, 128)), whole((1, RET_VW)), whole((1, RET_VW)),
    ]


def _ret_cols(h):
    q = slice(OFF_RQ + h * RET_DK, OFF_RQ + (h + 1) * RET_DK)
    k = slice(OFF_RK + h * RET_DK, OFF_RK + (h + 1) * RET_DK)
    v = slice(OFF_RV + h * RET_DV, OFF_RV + (h + 1) * RET_DV)
    g = slice(OFF_RG + h * RET_DV, OFF_RG + (h + 1) * RET_DV)
    return q, k, v, g, slice(h * RET_DV, (h + 1) * RET_DV)


def _ret_fwd(proj, tables, gn_g, gn_b, after):
    inner, qd, kd, cd, cos, sin = tables

    def body(x_ref, cos_ref, sin_ref, in_ref, qd_ref, kd_ref, cd_ref, g_ref, b_ref, after_ref,
             gated_ref, ro_ref, st_ref, s_scr):
        i = pl.program_id(0)

        @pl.when(i == 0)
        def _():
            s_scr[...] = jnp.zeros_like(s_scr)

        cosv, sinv = cos_ref[...], sin_ref[...]
        for h in range(RET_HEADS):
            cq, ck, cv, cg, co = _ret_cols(h)
            q = _rot(x_ref[:, cq], cosv, sinv)
            k = _rot(x_ref[:, ck], cosv, sinv) * (RET_DK ** -0.5)
            v = x_ref[:, cv]
            st = s_scr[h]
            st_ref[h] = st.astype(BF16)
            s = _dot(q, k, NT) * in_ref[h]
            o = _dot(s, v, NN) + _dot(q, st, NN) * qd_ref[h]
            s_scr[h] = st * cd_ref[h, :, :1] + _dot(k * kd_ref[h], v, TN)
            ro_ref[:, co] = o
            mu = jnp.mean(o, axis=-1, keepdims=True)
            oc = o - mu
            var = jnp.mean(oc * oc, axis=-1, keepdims=True)
            rn = oc * lax.rsqrt(var + GN_EPS) * g_ref[:, co] + b_ref[:, co]
            rg = x_ref[:, cg]
            gated_ref[:, co] = (rg * jax.nn.sigmoid(rg) * rn).astype(BF16)

    ospec = pl.BlockSpec((CHUNK, RET_VW), lambda t: (t, 0))
    return _pcall(
        body, name="ret_fwd", grid=(N_CHUNK,), in_specs=_ret_specs(lambda t: t) + [HBM_SPEC],
        out_specs=(ospec, ospec, pl.BlockSpec((RET_HEADS, None, RET_DK, RET_DV), lambda t: (0, t, 0, 0))),
        out_shape=(jax.ShapeDtypeStruct((S, RET_VW), BF16), jax.ShapeDtypeStruct((S, RET_VW), F32),
                   jax.ShapeDtypeStruct((RET_HEADS, N_CHUNK, RET_DK, RET_DV), BF16)),
        scratch_shapes=[pltpu.VMEM((RET_HEADS, RET_DK, RET_DV), F32)],
        compiler_params=_params(("arbitrary",)))(proj, cos, sin, inner, qd, kd, cd, gn_g, gn_b, after)


def _ret_bwd(proj, tables, gn_g, gn_b, ro, states, dgated, others):
    inner, qd, kd, cd, cos, sin = tables
    last = N_CHUNK - 1
    pieces = lambda o: [(h, o.shape[2]) for h in range(o.shape[0])] if len(o.shape) == 3 else [(None, o.shape[1])]
    assert RET_COLS + sum(w for o in others for _, w in pieces(o)) == IN_COLS

    def body(x_ref, cos_ref, sin_ref, in_ref, qd_ref, kd_ref, cd_ref, g_ref, b_ref, ro_ref, st_ref, dg_ref, *rest):
        other_refs, (dx_ref, gg_ref, gb_ref, gs_scr) = rest[:len(others)], rest[len(others):]
        t = pl.program_id(0)
        col = RET_COLS
        for o_ref in other_refs:
            for h, w in pieces(o_ref):
                dx_ref[:, col:col + w] = (o_ref[...] if h is None else o_ref[h]).astype(BF16)
                col += w

        @pl.when(t == 0)
        def _():
            gs_scr[...] = jnp.zeros_like(gs_scr)
            gg_ref[...] = jnp.zeros_like(gg_ref)
            gb_ref[...] = jnp.zeros_like(gb_ref)

        cosv, sinv = cos_ref[...], sin_ref[...]
        for h in range(RET_HEADS):
            cq, ck, cv, cg, co = _ret_cols(h)
            q = _rot(x_ref[:, cq], cosv, sinv)
            k = _rot(x_ref[:, ck], cosv, sinv) * (RET_DK ** -0.5)
            v = x_ref[:, cv]
            qdv, kdv, dm = qd_ref[h], kd_ref[h], in_ref[h]
            st = st_ref[h]
            o = ro_ref[:, co]
            gv = g_ref[:, co]
            mu = jnp.mean(o, axis=-1, keepdims=True)
            oc = o - mu
            rstd = lax.rsqrt(jnp.mean(oc * oc, axis=-1, keepdims=True) + GN_EPS)
            ohat = oc * rstd
            rn = ohat * gv + b_ref[:, co]
            rg = x_ref[:, cg]
            sg = jax.nn.sigmoid(rg)
            dgt = dg_ref[:, co]
            drn = dgt * (rg * sg)
            dx_ref[:, cg] = (dgt * rn * (sg * (1.0 + rg * (1.0 - sg)))).astype(BF16)
            gg_ref[:, co] += jnp.sum(drn * ohat, axis=0, keepdims=True)
            gb_ref[:, co] += jnp.sum(drn, axis=0, keepdims=True)
            dohat = drn * gv
            do = rstd * (dohat - jnp.mean(dohat, axis=-1, keepdims=True)
                         - ohat * jnp.mean(dohat * ohat, axis=-1, keepdims=True))
            gs = gs_scr[h]
            s = _dot(q, k, NT) * dm
            dsr = _dot(do, v, NT) * dm
            dq = _dot(dsr, k, NN) + _dot(do, st, NT) * qdv
            dk = _dot(dsr, q, TN) + _dot(v, gs, NT) * kdv
            dv = _dot(s, do, TN) + _dot(k * kdv, gs, NN)
            gs_scr[h] = gs * cd_ref[h, :, :1] + _dot(q * qdv, do, TN)
            dx_ref[:, cq] = _rot_t(dq, cosv, sinv).astype(BF16)
            dx_ref[:, ck] = (_rot_t(dk, cosv, sinv) * (RET_DK ** -0.5)).astype(BF16)
            dx_ref[:, cv] = dv.astype(BF16)

    rev = lambda t: last - t
    vblk = pl.BlockSpec((CHUNK, RET_VW), lambda t: (rev(t), 0))
    vspec = pl.BlockSpec((1, RET_VW), lambda t: (0, 0))
    rows = lambda w: pl.BlockSpec((CHUNK, w), lambda t: (rev(t), 0))
    return _pcall(
        body, name="ret_bwd", grid=(N_CHUNK,),
        in_specs=_ret_specs(rev) + [vblk, pl.BlockSpec((RET_HEADS, None, RET_DK, RET_DV), lambda t: (0, rev(t), 0, 0)),
                                    vblk] + [rows(o.shape[1]) if o.ndim == 2 else
                                             pl.BlockSpec((o.shape[0], CHUNK, o.shape[2]), lambda t: (0, rev(t), 0))
                                             for o in others],
        out_specs=(rows(IN_COLS), vspec, vspec),
        out_shape=(jax.ShapeDtypeStruct((S, IN_COLS), BF16), jax.ShapeDtypeStruct((1, RET_VW), F32),
                   jax.ShapeDtypeStruct((1, RET_VW), F32)),
        scratch_shapes=[pltpu.VMEM((RET_HEADS, RET_DK, RET_DV), F32)],
        compiler_params=_params(("arbitrary",)))(proj, cos, sin, inner, qd, kd, cd, gn_g, gn_b, ro, states, dgated,
                                                 *others)


def _bucket_tables():
    qi = np.arange(ATT_BLK)[:, None]
    kj = np.arange(2 * ATT_BLK)[None, :]
    m = ATT_BLK + qi - kj
    out = []
    for win, dil in ATT_GROUPS:
        w = win // dil
        dist = (np.clip(m, 0, w) * dil).astype(np.int32)
        max_exact = N_BUCKETS // 2
        d_f = np.maximum(dist, 1).astype(np.float32)
        large = max_exact + (np.log(d_f / np.float32(max_exact)) / np.float32(math.log(MAX_DIST / max_exact))
                             * np.float32(N_BUCKETS - max_exact)).astype(np.int32)
        large = np.minimum(large, N_BUCKETS - 1)
        out.append(np.where(dist < max_exact, dist, large).astype(np.int32))
    return np.stack(out)


def _bias_build(rel_bias, buckets, after):
    def body(tab_ref, bk_ref, after_ref, o_ref):
        hh = pl.program_id(0)
        bk = bk_ref[...]
        acc = jnp.zeros((ATT_BLK, 2 * ATT_BLK), F32)
        for b in range(N_BUCKETS):
            acc = jnp.where(bk == b, tab_ref[b, hh], acc)
        o_ref[...] = acc

    nh = len(ATT_GROUPS) * ATT_HG
    return _pcall(body, name="bias_build", grid=(nh,),
                  in_specs=[pl.BlockSpec(memory_space=pltpu.SMEM),
                            pl.BlockSpec((None, ATT_BLK, 2 * ATT_BLK), lambda hh: (hh // ATT_HG, 0, 0)), HBM_SPEC],
                  out_specs=pl.BlockSpec((None, ATT_BLK, 2 * ATT_BLK), lambda hh: (hh, 0, 0)),
                  out_shape=jax.ShapeDtypeStruct((nh, ATT_BLK, 2 * ATT_BLK), F32),
                  compiler_params=_params(("parallel",)))(rel_bias, buckets, after)


def _bias_grad(ds_sum, buckets):
    def body(ds_ref, bk_ref, o_ref):
        bk = bk_ref[...]
        ds = ds_ref[...]
        rows = lax.broadcasted_iota(jnp.int32, (N_BUCKETS, 128), 0)
        acc = jnp.zeros((N_BUCKETS, 128), F32)
        for b in range(N_BUCKETS):
            acc = jnp.where(rows == b, jnp.sum(jnp.where(bk == b, ds, 0.0)), acc)
        o_ref[...] = acc

    nh = len(ATT_GROUPS) * ATT_HG
    return _pcall(body, name="bias_grad", grid=(nh,),
                  in_specs=[pl.BlockSpec((None, ATT_BLK, 2 * ATT_BLK), lambda hh: (hh, 0, 0)),
                            pl.BlockSpec((None, ATT_BLK, 2 * ATT_BLK), lambda hh: (hh // ATT_HG, 0, 0))],
                  out_specs=pl.BlockSpec((None, N_BUCKETS, 128), lambda hh: (hh, 0, 0)),
                  out_shape=jax.ShapeDtypeStruct((nh, N_BUCKETS, 128), F32),
                  compiler_params=_params(("parallel",)))(ds_sum, buckets)


def _att_valid(n):
    qi = lax.broadcasted_iota(jnp.int32, (ATT_BLK, 2 * ATT_BLK), 0)
    kj = lax.broadcasted_iota(jnp.int32, (ATT_BLK, 2 * ATT_BLK), 1)
    m = ATT_BLK + qi - kj
    first_key = jnp.where(n > 0, 0, ATT_BLK)
    return (m >= 0) & (m <= ATT_BLK) & (kj >= first_key)


ATT_HP = (1, 2, 2)


def _att_geometry(gi):
    _, dil = ATT_GROUPS[gi]
    return dil, S // dil // ATT_BLK, ATT_HP[gi]


def _blk(dil, r, n):
    if dil == 1:
        return pl.ds(n * ATT_BLK, ATT_BLK)
    return pl.ds(r + n * ATT_BLK * dil, ATT_BLK, stride=dil)


def _slab_specs(gi):
    _, _, hp = _att_geometry(gi)
    per = ATT_HG // hp
    return [pl.BlockSpec((hp, S, ATT_DH), lambda g, r, part=part: ((3 * gi + part) * per + g, 0, 0))
            for part in range(3)]


def _head_specs(gi, count):
    _, _, hp = _att_geometry(gi)
    return [pl.BlockSpec((hp, S, ATT_DH), lambda g, r: (g, 0, 0))] * count


def _bias_spec(gi):
    _, _, hp = _att_geometry(gi)
    return pl.BlockSpec((hp, ATT_BLK, 2 * ATT_BLK), lambda g, r: (gi * (ATT_HG // hp) + g, 0, 0))


def _att_valid_first():
    qi = lax.broadcasted_iota(jnp.int32, (ATT_BLK, ATT_BLK), 0)
    kj = lax.broadcasted_iota(jnp.int32, (ATT_BLK, ATT_BLK), 1)
    return kj <= qi


def _att_fwd(slabs, bias, gi, comm=None):
    dil, nb, hp = _att_geometry(gi)
    scale = ATT_DH ** -0.5

    def body(q_ref, k_ref, v_ref, bias_ref, o_ref, l_ref):
        r = pl.program_id(1)
        for n in range(nb):
            cur = _blk(dil, r, n)
            valid = _att_valid(n) if n > 0 else _att_valid_first()
            for h in range(hp):
                if n > 0:
                    prev = _blk(dil, r, n - 1)
                    kk = jnp.concatenate([k_ref[h, prev, :], k_ref[h, cur, :]], axis=0)
                    vv = jnp.concatenate([v_ref[h, prev, :], v_ref[h, cur, :]], axis=0)
                    bias = bias_ref[h]
                else:
                    kk, vv, bias = k_ref[h, cur, :], v_ref[h, cur, :], bias_ref[h, :, pl.ds(ATT_BLK, ATT_BLK)]
                s = _dot(q_ref[h, cur, :], kk, NT) * scale + bias
                s = jnp.where(valid, s, -1e30)
                mx = jnp.max(s, axis=-1, keepdims=True)
                e = jnp.exp(s - mx)
                den = jnp.sum(e, axis=-1, keepdims=True)
                o_ref[h, cur, :] = _dot(e / den, vv, NN)
                l_ref[h, cur, :] = jnp.broadcast_to(mx + jnp.log(den), (ATT_BLK, ATT_DH))

    osh = pltpu.HBM((ATT_HG, S, ATT_DH), F32)
    kw = dict(name=f"att_fwd{gi}", grid=(ATT_HG // hp, dil), in_specs=_slab_specs(gi) + [_bias_spec(gi)],
              out_specs=tuple(_head_specs(gi, 2)), out_shape=(osh, osh))
    if comm is not None:
        return _carry(body, comm, **kw)(slabs, slabs, slabs, bias)
    return _pcall(body, compiler_params=_params(("parallel", "arbitrary")), **kw)(slabs, slabs, slabs, bias)


def _att_bwd(slabs, bias, o, lse, do, dlse, gi, comm=None):
    dil, nb, hp = _att_geometry(gi)
    per = ATT_HG // hp
    scale = ATT_DH ** -0.5
    wide = lambda t: jnp.concatenate([t, t], axis=1)

    def body(q_ref, k_ref, v_ref, bias_ref, o_ref, l_ref, do_ref, dl_ref, dq_ref, dk_ref, dv_ref, ds_ref):
        r = pl.program_id(1)

        @pl.when(r == 0)
        def _():
            ds_ref[...] = jnp.zeros_like(ds_ref)

        for h in range(hp):
            carry_k = carry_v = None
            for n in range(nb):
                cur = _blk(dil, r, n)
                q = q_ref[h, cur, :]
                dov = do_ref[h, cur, :]
                delta = jnp.sum(dov * o_ref[h, cur, :], axis=-1, keepdims=True)
                if n == 0:
                    own = pl.ds(ATT_BLK, ATT_BLK)
                    kk, vv = k_ref[h, cur, :], v_ref[h, cur, :]
                    s = _dot(q, kk, NT) * scale + bias_ref[h, :, own]
                    p = jnp.where(_att_valid_first(), jnp.exp(s - l_ref[h, cur, :]), 0.0)
                    ds = p * (_dot(dov, vv, NT) - delta + dl_ref[h, cur, :])
                    ds_ref[h, :, own] += ds
                    dq_ref[h, cur, :] = _dot(ds, kk, NN) * scale
                    carry_k, carry_v = _dot(ds, q, TN) * scale, _dot(p, dov, TN)
                    continue
                prev = _blk(dil, r, n - 1)
                kk = jnp.concatenate([k_ref[h, prev, :], k_ref[h, cur, :]], axis=0)
                vv = jnp.concatenate([v_ref[h, prev, :], v_ref[h, cur, :]], axis=0)
                s = _dot(q, kk, NT) * scale + bias_ref[h]
                p = jnp.where(_att_valid(n), jnp.exp(s - wide(l_ref[h, cur, :])), 0.0)
                dp = _dot(dov, vv, NT)
                ds = p * (dp - delta + wide(dl_ref[h, cur, :]))
                ds_ref[h] += ds
                dq_ref[h, cur, :] = _dot(ds, kk, NN) * scale
                dkk = _dot(ds, q, TN) * scale
                dvv = _dot(p, dov, TN)
                dk_ref[h, prev, :] = carry_k + dkk[:ATT_BLK]
                dv_ref[h, prev, :] = carry_v + dvv[:ATT_BLK]
                carry_k, carry_v = dkk[ATT_BLK:], dvv[ATT_BLK:]
            last = _blk(dil, r, nb - 1)
            dk_ref[h, last, :] = carry_k
            dv_ref[h, last, :] = carry_v

    osh = jax.ShapeDtypeStruct((ATT_HG, S, ATT_DH), F32)
    kw = dict(name=f"att_bwd{gi}", grid=(per, dil), in_specs=_slab_specs(gi) + [_bias_spec(gi)] + _head_specs(gi, 4),
              out_specs=(*_head_specs(gi, 3), pl.BlockSpec((hp, ATT_BLK, 2 * ATT_BLK), lambda g, r: (g, 0, 0))),
              out_shape=(osh, osh, osh, jax.ShapeDtypeStruct((ATT_HG, ATT_BLK, 2 * ATT_BLK), F32)))
    args = (slabs, slabs, slabs, bias, o, lse, do, dlse)
    if comm is not None:
        return _carry(body, comm, **kw)(*args)
    return _pcall(body, compiler_params=_params(("arbitrary", "arbitrary")), **kw)(*args)


AW = ATT_HG * ATT_DH


def _mix_weights(l0, l1, l2):
    mx = jnp.maximum(jnp.maximum(l0, l1), l2)
    e0, e1, e2 = jnp.exp(l0 - mx), jnp.exp(l1 - mx), jnp.exp(l2 - mx)
    den = e0 + e1 + e2
    return e0 / den, e1 / den, e2 / den


def _heads_spec():
    return pl.BlockSpec((ATT_HG, TR, ATT_DH), lambda i: (0, i, 0))


def _mix_fwd(os_, ls, comm=None):
    def body(o0, o1, o2, l0, l1, l2, att_ref):
        for h in range(ATT_HG):
            w0, w1, w2 = _mix_weights(l0[h], l1[h], l2[h])
            att_ref[:, h * ATT_DH:(h + 1) * ATT_DH] = (w0 * o0[h] + w1 * o1[h] + w2 * o2[h]).astype(BF16)

    kw = dict(name="mix_fwd", grid=(S // TR,), in_specs=[_heads_spec()] * 6, out_specs=_row_spec(AW),
              out_shape=jax.ShapeDtypeStruct((S, AW), BF16))
    if comm is not None:
        return _carry(body, comm, **kw)(*os_, *ls)
    return _pcall(body, compiler_params=_params(("parallel",)), **kw)(*os_, *ls)


def _mix_bwd(os_, ls, datt):
    def body(o0, o1, o2, l0, l1, l2, da_ref, d0, d1, d2, e0, e1, e2):
        for h in range(ATT_HG):
            ws = _mix_weights(l0[h], l1[h], l2[h])
            da = da_ref[:, h * ATT_DH:(h + 1) * ATT_DH]
            dws = []
            for o_ref, w, d_ref in zip((o0, o1, o2), ws, (d0, d1, d2)):
                d_ref[h] = w * da
                dws.append(jnp.broadcast_to(jnp.sum(da * o_ref[h], axis=-1, keepdims=True), (TR, ATT_DH)))
            tot = ws[0] * dws[0] + ws[1] * dws[1] + ws[2] * dws[2]
            for w, dw, e_ref in zip(ws, dws, (e0, e1, e2)):
                e_ref[h] = w * (dw - tot)

    o = pltpu.HBM((ATT_HG, S, ATT_DH), F32)
    return _pcall(body, name="mix_bwd", grid=(S // TR,), in_specs=[_heads_spec()] * 6 + [_row_spec(AW)],
                  out_specs=(_heads_spec(),) * 6, out_shape=(o,) * 6,
                  compiler_params=_params(("parallel",)))(*os_, *ls, datt)


def _ada_fwd(c_all, w_sh, b_sl):
    def body(c_ref, w_ref, b_ref, o_ref):
        cv = c_ref[...]
        o_ref[...] = _dot(cv * jax.nn.sigmoid(cv), w_ref[...], NN) + b_ref[...]

    return _pcall(body, name="ada_fwd", out_shape=jax.ShapeDtypeStruct((N_DEV, w_sh.shape[1]), F32),
                  compiler_params=_params())(c_all, w_sh, b_sl)


CAST_STEPS = 4


def _to_bf16(arrs, comm, name):
    n = len(arrs)

    def body(*refs):
        for src, dst in zip(refs[:n], refs[n:]):
            dst[...] = src[...].astype(BF16)

    blocks = [pl.BlockSpec((a.shape[0] // CAST_STEPS, a.shape[1]), lambda i: (i, 0)) for a in arrs]
    return _carry(body, comm, name=name, grid=(CAST_STEPS,), in_specs=blocks, out_specs=tuple(blocks),
                  out_shape=tuple(pltpu.HBM(a.shape, BF16) for a in arrs))(*arrs)


def _ada_bwd(c_all, dm_sl):
    def body(c_ref, d_ref, o_ref):
        cv = c_ref[...]
        o_ref[...] = _dot(cv * jax.nn.sigmoid(cv), d_ref[...], TN)

    return _pcall(body, name="ada_bwd", out_shape=jax.ShapeDtypeStruct((D, dm_sl.shape[1]), F32),
                  compiler_params=_params())(c_all, dm_sl)


N_MOD = 6


def _sum_small(gathered, widths):
    def body(g_ref, gb_ref, dm_ref, *outs):
        def total(cols):
            acc = g_ref[0, :, cols]
            for e in range(1, N_DEV):
                acc = acc + g_ref[e, :, cols]
            return acc

        gb_ref[...] = total(slice(0, N_MOD * D))
        for e in range(N_DEV):
            dm_ref[e:e + 1, :] = g_ref[e, :, :N_MOD * D]
        col = N_MOD * D
        for w, o_ref in zip(widths, outs):
            o_ref[...] = total(slice(col, col + w))
            col += w

    assert gathered.shape == (N_DEV, 1, N_MOD * D + sum(widths))
    shapes = (jax.ShapeDtypeStruct((1, N_MOD * D), F32), jax.ShapeDtypeStruct((N_DEV, N_MOD * D), F32),
              *[jax.ShapeDtypeStruct((1, w), F32) for w in widths])
    res = _pcall(body, name="sum_small", out_shape=shapes, compiler_params=_params())(gathered)
    return res[0], res[1], res[2:]


def _row_tile(m, n):
    t = max(8, min(m, (1 << 19) // n // 8 * 8))
    while m % t:
        t -= 8
    return t


def _pair_sum(full, recv, sel, name, col_block=0):
    _, m, n = recv.shape
    t = _row_tile(m, n)

    def body(sel_ref, a_ref, b_ref, o_ref):
        o_ref[...] = (a_ref[...].astype(F32) + b_ref[...].astype(F32)).astype(o_ref.dtype)

    gs = pltpu.PrefetchScalarGridSpec(
        num_scalar_prefetch=1, grid=(4, m // t),
        in_specs=[pl.BlockSpec((None, None, t, n), lambda q, i, s: (q, s[0], i, col_block)),
                  pl.BlockSpec((None, t, n), lambda q, i, s: (q, i, 0))],
        out_specs=pl.BlockSpec((None, t, n), lambda q, i, s: (q, i, 0)))
    return _pcall(body, name=name, grid_spec=gs, out_shape=pltpu.HBM((4, m, n), full.dtype),
                  compiler_params=_params(("parallel", "parallel")))(sel, full, recv)


def _chip_sum(part, recv, sel, name):
    _, m, n = part.shape
    t = _row_tile(m, n)

    def body(sel_ref, a_ref, r_ref, o_ref):
        o_ref[...] = ((a_ref[...].astype(F32) + r_ref[0].astype(F32)) + r_ref[1].astype(F32)) + r_ref[2].astype(F32)

    gs = pltpu.PrefetchScalarGridSpec(
        num_scalar_prefetch=1, grid=(m // t,),
        in_specs=[pl.BlockSpec((None, t, n), lambda i, s: (s[0], i, 0)),
                  pl.BlockSpec((3, t, n), lambda i, s: (0, i, 0))],
        out_specs=pl.BlockSpec((t, n), lambda i, s: (i, 0)))
    return _pcall(body, name=name, grid_spec=gs, out_shape=jax.ShapeDtypeStruct((m, n), F32),
                  compiler_params=_params(("parallel",)))(sel, part, recv)


def _adamw_math(w, g, m, v):
    nm = ADAM_B1 * m + (1.0 - ADAM_B1) * g
    nv = ADAM_B2 * v + (1.0 - ADAM_B2) * (g * g)
    m_hat = nm / (1.0 - ADAM_B1 ** ADAM_STEP)
    v_hat = nv / (1.0 - ADAM_B2 ** ADAM_STEP)
    return -ADAM_LR * (m_hat / (jnp.sqrt(v_hat) + ADAM_EPS) + ADAM_WD * w), nm, nv


def _adamw(w, g, m, v, name):
    _, rows, cols = w.shape
    t = _row_tile(rows, cols)

    def body(w_ref, g_ref, m_ref, v_ref, d_ref, nm_ref, nv_ref):
        d_ref[...], nm_ref[...], nv_ref[...] = _adamw_math(w_ref[...], g_ref[...], m_ref[...], v_ref[...])

    spec3 = pl.BlockSpec((None, t, cols), lambda i: (0, i, 0))
    spec2 = pl.BlockSpec((t, cols), lambda i: (i, 0))
    o = jax.ShapeDtypeStruct(w.shape, F32)
    return _pcall(body, name=name, grid=(rows // t,), in_specs=[spec3, spec2, spec3, spec3], out_specs=(spec3,) * 3,
                  out_shape=(o, o, o), compiler_params=_params(("parallel",)))(w, g, m, v)


def _adamw_reduced1(w, m, v, part, recv, sel, name):
    _, rows, cols = w.shape
    t = _row_tile(rows, cols)

    def body(sel_ref, w_ref, m_ref, v_ref, p_ref, r_ref, g_ref, d_ref, nm_ref, nv_ref):
        g = ((p_ref[...].astype(F32) + r_ref[0].astype(F32)) + r_ref[1].astype(F32)) + r_ref[2].astype(F32)
        g_ref[...] = g
        d_ref[...], nm_ref[...], nv_ref[...] = _adamw_math(w_ref[...], g, m_ref[...], v_ref[...])

    wspec = pl.BlockSpec((None, t, cols), lambda i, s: (0, i, 0))
    gs = pltpu.PrefetchScalarGridSpec(
        num_scalar_prefetch=1, grid=(rows // t,),
        in_specs=[wspec, wspec, wspec, pl.BlockSpec((None, t, cols), lambda i, s: (s[0], i, 0)),
                  pl.BlockSpec((3, t, cols), lambda i, s: (0, i, 0))],
        out_specs=(wspec,) * 4)
    o = jax.ShapeDtypeStruct(w.shape, F32)
    return _pcall(body, name=name, grid_spec=gs, out_shape=(o, o, o, o),
                  compiler_params=_params(("parallel",)))(sel, w, m, v, part, recv)


def _adamw_reduced(w, m, v, parts, recvs, sel):
    _, rows, cols = w.shape
    half = cols // 2
    t = _row_tile(rows, half)

    def body(sel_ref, w_ref, m_ref, v_ref, pa_ref, pb_ref, ra_ref, rb_ref, g_ref, d_ref, nm_ref, nv_ref):
        total = lambda p_ref, r_ref: ((p_ref[...].astype(F32) + r_ref[0].astype(F32)) + r_ref[1].astype(F32)) \
            + r_ref[2].astype(F32)
        g = jnp.where(pl.program_id(1) == 0, total(pa_ref, ra_ref), total(pb_ref, rb_ref))
        g_ref[...] = g
        d_ref[...], nm_ref[...], nv_ref[...] = _adamw_math(w_ref[...], g, m_ref[...], v_ref[...])

    wspec = pl.BlockSpec((None, t, half), lambda i, j, s: (0, i, j))
    pspec = pl.BlockSpec((None, t, half), lambda i, j, s: (s[0], i, 0))
    rspec = pl.BlockSpec((3, t, half), lambda i, j, s: (0, i, 0))
    gs = pltpu.PrefetchScalarGridSpec(num_scalar_prefetch=1, grid=(rows // t, 2),
                                      in_specs=[wspec, wspec, wspec, pspec, pspec, rspec, rspec],
                                      out_specs=(wspec,) * 4)
    o = jax.ShapeDtypeStruct(w.shape, F32)
    return _pcall(body, name="adamw_w_in", grid_spec=gs, out_shape=(o, o, o, o),
                  compiler_params=_params(("parallel", "arbitrary")))(sel, w, m, v, *parts, *recvs)


def _adamw_small(ws, gs, ms, vs):
    n = len(ws)

    def body(*refs):
        for i in range(n):
            w_ref, g_ref, m_ref, v_ref = (refs[k * n + i] for k in range(4))
            d, nm, nv = _adamw_math(w_ref[...], g_ref[...], m_ref[...], v_ref[...])
            refs[4 * n + i][...] = d
            refs[5 * n + i][...] = nm
            refs[6 * n + i][...] = nv

    shapes = tuple(jax.ShapeDtypeStruct(w.shape, F32) for w in ws)
    res = _pcall(body, name="adamw_small", out_shape=shapes * 3, compiler_params=_params())(*ws, *gs, *ms, *vs)
    return res[:n], res[n:2 * n], res[2 * n:]


def _mesh_pos():
    return lax.axis_index("x"), lax.axis_index("y"), lax.axis_index("c")


class _Gather:
    def __init__(self, arrs):
        self.ins = list(arrs)
        self.out_shape = tuple(jax.ShapeDtypeStruct((N_DEV,) + a.shape, a.dtype) for a in arrs)
        n = len(arrs)
        self.sems = [pltpu.SemaphoreType.DMA((7 * n,)), pltpu.SemaphoreType.DMA((7 * n,)),
                     pltpu.SemaphoreType.DMA((n,))]

    def _copies(self, ins, outs, sems):
        send_sems, recv_sems, local_sems = sems
        x, y, c = _mesh_pos()
        me, sibling = (x, y, c), (x, y, 1 - c)
        chips = [(1 - x, y), (x, 1 - y), (1 - x, 1 - y)]

        def copy(p, k, block, to, from_input=False):
            dst = outs[p].at[_slot(block)]
            return pltpu.make_async_remote_copy(
                src_ref=ins[p] if from_input else dst, dst_ref=dst, send_sem=send_sems.at[7 * p + k],
                recv_sem=recv_sems.at[7 * p + k], device_id=to, device_id_type=MESH)

        npc = len(self.ins)
        mine = [pltpu.make_async_copy(ins[p], outs[p].at[_slot(me)], local_sems.at[p]) for p in range(npc)]
        first = []
        for p in range(npc):
            first.append(copy(p, 0, me, sibling, from_input=True))
            first += [copy(p, 1 + j, me, (*chip, c), from_input=True) for j, chip in enumerate(chips)]
        return me, sibling, chips, c, copy, mine, first

    def start(self, ins, outs, sems):
        *_, mine, first = self._copies(ins, outs, sems)
        for cp in mine + first:
            cp.start()

    def finish(self, ins, outs, sems):
        me, sibling, chips, c, copy, mine, first = self._copies(ins, outs, sems)
        npc = len(self.ins)
        passed = []
        for p in range(npc):
            for j, chip in enumerate(chips):
                copy(p, 1 + j, (*chip, c), me).wait_recv()
                passed.append(copy(p, 4 + j, (*chip, c), sibling))
                passed[-1].start()
        for p in range(npc):
            copy(p, 0, sibling, me).wait_recv()
            for j, chip in enumerate(chips):
                copy(p, 4 + j, (*chip, 1 - c), me).wait_recv()
        for cp in first + passed:
            cp.wait_send()
        for cp in mine:
            cp.wait()


class _ExchangeCore:
    def __init__(self, fulls, cols=None):
        self.ins = list(fulls)
        self.cols = cols
        width = lambda f: f.shape[3] if cols is None else cols[1]
        self.out_shape = tuple(jax.ShapeDtypeStruct((4, f.shape[2], width(f)), f.dtype) for f in fulls)
        self.sems = [pltpu.SemaphoreType.DMA((4 * len(fulls),)), pltpu.SemaphoreType.DMA((4 * len(fulls),))]

    def _copies(self, ins, outs, sems):
        send_sems, recv_sems = sems
        x, y, c = _mesh_pos()

        def src(a, q):
            ref = ins[a].at[q, 1 - c]
            return ref if self.cols is None else ref.at[:, pl.ds(*self.cols)]

        return [pltpu.make_async_remote_copy(
            src_ref=src(a, q), dst_ref=outs[a].at[q], send_sem=send_sems.at[4 * a + q],
            recv_sem=recv_sems.at[4 * a + q], device_id=(x, y, 1 - c), device_id_type=MESH)
            for a in range(len(self.ins)) for q in range(4)]

    def start(self, ins, outs, sems):
        for cp in self._copies(ins, outs, sems):
            cp.start()

    def finish(self, ins, outs, sems):
        for cp in self._copies(ins, outs, sems):
            cp.wait()


class _ExchangeChip:
    def __init__(self, parts):
        self.ins = list(parts)
        self.out_shape = tuple(jax.ShapeDtypeStruct((3,) + p.shape[1:], p.dtype) for p in parts)
        self.sems = [pltpu.SemaphoreType.DMA((3 * len(parts),)), pltpu.SemaphoreType.DMA((3 * len(parts),))]

    def _copies(self, ins, outs, sems):
        send_sems, recv_sems = sems
        x, y, c = _mesh_pos()
        chips = [(1 - x, y), (x, 1 - y), (1 - x, 1 - y)]
        return [pltpu.make_async_remote_copy(
            src_ref=ins[a].at[2 * px + py], dst_ref=outs[a].at[j], send_sem=send_sems.at[3 * a + j],
            recv_sem=recv_sems.at[3 * a + j], device_id=(px, py, c), device_id_type=MESH)
            for a in range(len(self.ins)) for j, (px, py) in enumerate(chips)]

    def start(self, ins, outs, sems):
        for cp in self._copies(ins, outs, sems):
            cp.start()

    def finish(self, ins, outs, sems):
        for cp in self._copies(ins, outs, sems):
            cp.wait()


HBM_ONLY = pl.BlockSpec(memory_space=pltpu.HBM)
SEM_SPEC = pl.BlockSpec(memory_space=pltpu.SEMAPHORE)
SIDE_EFFECT = pltpu.SideEffectType.DATAFLOW_SIDE_EFFECTING


def _chip_copies(p_refs, land_refs, send_sems, recv_sems):
    x, y, c = _mesh_pos()
    return [pltpu.make_async_remote_copy(
        src_ref=p_refs[a].at[2 * px + py], dst_ref=land_refs[a].at[j], send_sem=send_sems.at[3 * a + j],
        recv_sem=recv_sems.at[3 * a + j], device_id=(px, py, c), device_id_type=MESH)
        for a in range(len(p_refs)) for j, (px, py) in enumerate([(1 - x, y), (x, 1 - y), (1 - x, 1 - y)])]


def _chip_exchange_start(parts, name):
    n = len(parts)
    lands = [lax.empty((3,) + p.shape[1:], p.dtype) for p in parts]

    def body(*refs):
        p_refs, land_refs, (send_sems, recv_sems) = refs[:n], refs[n:2 * n], refs[2 * n:2 * n + 2]
        for cp in _chip_copies(p_refs, land_refs, send_sems, recv_sems):
            cp.start()
        token = refs[-1]
        token[...] = jnp.zeros_like(token)

    hbm = lambda t: pltpu.HBM(t.shape, t.dtype)
    res = pl.pallas_call(
        body, name=name,
        out_shape=(pltpu.SemaphoreType.DMA((3 * n,)), pltpu.SemaphoreType.DMA((3 * n,)), *[hbm(t) for t in parts + lands],
                   jax.ShapeDtypeStruct((8, 128), F32)),
        in_specs=(HBM_ONLY,) * (2 * n),
        out_specs=(SEM_SPEC, SEM_SPEC, *[HBM_ONLY] * (2 * n), pl.BlockSpec(memory_space=pltpu.VMEM)),
        input_output_aliases={i: 2 + i for i in range(2 * n)},
        compiler_params=pltpu.CompilerParams(has_side_effects=SIDE_EFFECT))(
        *[pltpu.with_memory_space_constraint(t, pltpu.HBM) for t in parts + lands])
    return (res[0], res[1], list(res[2:2 + n]), list(res[2 + n:2 + 2 * n])), res[-1]


def _chip_exchange_wait(in_flight, after, name):
    send_sems, recv_sems, parts, lands = in_flight
    n = len(parts)

    def body(*refs):
        p_refs, land_refs, (send_sems, recv_sems) = refs[:n], refs[n:2 * n], refs[2 * n:2 * n + 2]
        for cp in _chip_copies(p_refs, land_refs, send_sems, recv_sems):
            cp.wait_send()
            cp.wait_recv()

    res = pl.pallas_call(
        body, name=name, out_shape=tuple(pltpu.HBM(t.shape, t.dtype) for t in parts + lands),
        in_specs=(*[HBM_ONLY] * (2 * n), SEM_SPEC, SEM_SPEC, pl.BlockSpec(memory_space=pl.ANY)),
        out_specs=(HBM_ONLY,) * (2 * n), input_output_aliases={i: i for i in range(2 * n)},
        compiler_params=pltpu.CompilerParams(has_side_effects=SIDE_EFFECT))(*parts, *lands, send_sems, recv_sems, after)
    return list(res[:n]), list(res[n:])


def _slot(p):
    return 4 * p[0] + 2 * p[1] + p[2]


def _gather_copies(src_refs, out_refs, send_sems, recv_sems):
    x, y, c = _mesh_pos()
    targets = [(x, y, 1 - c), (1 - x, y, c), (x, 1 - y, c), (1 - x, 1 - y, c)]
    return [pltpu.make_async_remote_copy(
        src_ref=src_refs[a], dst_ref=out_refs[a].at[_slot((x, y, c))], send_sem=send_sems.at[4 * a + k],
        recv_sem=recv_sems.at[4 * a + k], device_id=to, device_id_type=MESH)
        for a in range(len(src_refs)) for k, to in enumerate(targets)]


def _gather_start(shards, after, name):
    n = len(shards)
    outs = [lax.empty((N_DEV,) + s.shape, s.dtype) for s in shards]

    def body(*refs):
        for cp in _gather_copies(refs[:n], refs[n:2 * n], refs[2 * n + 1], refs[2 * n + 2]):
            cp.start()
        token = refs[-1]
        token[...] = jnp.zeros_like(token)

    res = pl.pallas_call(
        body, name=name,
        out_shape=(pltpu.SemaphoreType.DMA((4 * n,)), pltpu.SemaphoreType.DMA((4 * n,)),
                   *[pltpu.HBM(t.shape, t.dtype) for t in shards + outs], jax.ShapeDtypeStruct((8, 128), F32)),
        in_specs=(*[HBM_ONLY] * (2 * n), pl.BlockSpec(memory_space=pl.ANY)),
        out_specs=(SEM_SPEC, SEM_SPEC, *[HBM_ONLY] * (2 * n), pl.BlockSpec(memory_space=pltpu.VMEM)),
        input_output_aliases={i: 2 + i for i in range(2 * n)},
        compiler_params=pltpu.CompilerParams(has_side_effects=SIDE_EFFECT))(
        *[pltpu.with_memory_space_constraint(t, pltpu.HBM) for t in shards + outs], after)
    return (res[0], res[1], list(res[2:2 + n]), list(res[2 + n:2 + 2 * n])), res[-1]


def _gather_wait(in_flight, after, name):
    send_sems, recv_sems, shards, outs = in_flight
    n = len(shards)

    def body(*refs):
        for cp in _gather_copies(refs[:n], refs[n:2 * n], refs[2 * n], refs[2 * n + 1]):
            cp.wait_send()
            cp.wait_recv()

    res = pl.pallas_call(
        body, name=name, out_shape=tuple(pltpu.HBM(t.shape, t.dtype) for t in shards + outs),
        in_specs=(*[HBM_ONLY] * (2 * n), SEM_SPEC, SEM_SPEC, pl.BlockSpec(memory_space=pl.ANY)),
        out_specs=(HBM_ONLY,) * (2 * n), input_output_aliases={i: i for i in range(2 * n)},
        compiler_params=pltpu.CompilerParams(has_side_effects=SIDE_EFFECT))(*shards, *outs, send_sems, recv_sems, after)
    return list(res[:n]), list(res[n:])


class _PassToSibling:
    def __init__(self, shards, gathered):
        n = self.n = len(shards)
        self.ins = list(shards) + list(gathered)
        self.out_shape = tuple(jax.ShapeDtypeStruct(g.shape, g.dtype) for g in gathered)
        self.aliases = {n + a: a for a in range(n)}
        self.sems = [pltpu.SemaphoreType.DMA((3 * n,)), pltpu.SemaphoreType.DMA((3 * n,)),
                     pltpu.SemaphoreType.DMA((n,))]

    def _copies(self, ins, outs, sems):
        send_sems, recv_sems, local_sems = sems
        x, y, c = _mesh_pos()
        chips = [(1 - x, y), (x, 1 - y), (1 - x, 1 - y)]
        mine = [pltpu.make_async_copy(ins[a], outs[a].at[_slot((x, y, c))], local_sems.at[a]) for a in range(self.n)]
        passed, awaited = [], []
        for a in range(self.n):
            for j, chip in enumerate(chips):
                sems_j = dict(send_sem=send_sems.at[3 * a + j], recv_sem=recv_sems.at[3 * a + j],
                              device_id=(x, y, 1 - c), device_id_type=MESH)
                blk = outs[a].at[_slot((*chip, c))]
                passed.append(pltpu.make_async_remote_copy(src_ref=blk, dst_ref=blk, **sems_j))
                got = outs[a].at[_slot((*chip, 1 - c))]
                awaited.append(pltpu.make_async_remote_copy(src_ref=got, dst_ref=got, **sems_j))
        return mine, passed, awaited

    def start(self, ins, outs, sems):
        mine, passed, _ = self._copies(ins, outs, sems)
        for cp in mine + passed:
            cp.start()

    def finish(self, ins, outs, sems):
        mine, passed, awaited = self._copies(ins, outs, sems)
        for cp in passed:
            cp.wait_send()
        for cp in awaited:
            cp.wait_recv()
        for cp in mine:
            cp.wait()


def _reduce_sums(fulls, recv_core, core, tag):
    return [_pair_sum(f, r, core, f"rs_pair_{tag}{i}") for i, (f, r) in enumerate(zip(fulls, recv_core))]


def _local_step(x, tgt, mods, w_in_shard, order, shards, small, chip, core):
    sh1, sc1, g1, sh2, sc2, g2 = mods
    norm1_g, rel_bias, gn_g, gn_b, norm2_g, norm_f_g = small
    tables = _ret_tables()
    buckets = jnp.asarray(_bucket_tables())

    proj, slabs, w_in_t, h1 = _gather_proj(x, norm1_g, sh1, sc1, w_in_shard, order)
    flight_w1, token_w = _gather_start(list(shards[:3]), proj, "gather_w1_start")
    flight_w2, token_w = _gather_start(list(shards[3:]), token_w, "gather_w2_start")
    bias = _bias_build(rel_bias, buckets, token_w)
    outs, lses = [], []
    for gi in range(len(ATT_GROUPS)):
        o, l = _att_fwd(slabs, bias, gi)
        outs.append(o)
        lses.append(l)
    att, gathered = _mix_fwd(outs, lses, comm=_PassToSibling(*_gather_wait(flight_w1, lses[2], "gather_w1_wait")))
    w_ret_out, w_att_out, w_o = (_from_slots(g, ax) for g, ax in zip(gathered, BIG_AXES[1:4]))
    gated, ro, states = _ret_fwd(proj, tables, gn_g, gn_b, att)
    ret_out, gathered = _mm(gated, w_ret_out, 'nn', tm=S, tn=256, tk=2048, name="ret_out",
                            comm=_PassToSibling(*_gather_wait(flight_w2, gated, "gather_w2_wait")))
    w_ff1, w_ff2 = (_from_slots(g, ax) for g, ax in zip(gathered, BIG_AXES[4:]))
    att_out, merged = _att_out_merge(att, w_att_out, proj, ret_out)
    mixo, x1, h2 = _w_o_norm2(merged, w_o, x, g1, norm2_g, sh2, sc2)
    u, act = _mm(h2, w_ff1, 'nn', tm=S, tn=512, tk=D, name="ff1", relu2=True)
    loss, dx2, g_normf, df, dg2 = _ff2_final(act, w_ff2, x1, g2, tgt, norm_f_g)

    gw_ff2 = _mm(act, df, 'tn', tm=512, tn=D, tk=S, name="gw_ff2", out_dtype=BF16)
    du = _mm(df, w_ff2, 'nt', tm=S, tn=512, tk=D, name="d_act", out_dtype=BF16, relu2_of=u)
    gw_ff1 = _mm(h2, du, 'tn', tm=D, tn=512, tk=S, name="gw_ff1", out_dtype=BF16)
    fulls_a = [_to_slots(g, ax) for g, ax in zip((gw_ff1, gw_ff2), BIG_AXES[4:])]
    dh2, recv_core_a = _mm(du, w_ff1, 'nt', tm=1024, tn=1024, tk=2048, name="dh2", comm=_ExchangeCore(fulls_a))
    parts_a = _reduce_sums(fulls_a, recv_core_a, core, "a")
    flight_a, token_a = _chip_exchange_start(parts_a, "rs_a_start")
    dx1, dsc2, dsh2, g_norm2, dmixo, dg1 = _norm_mod_bwd(x1, norm2_g, sc2, dh2, dx2, "norm2_bwd", gate=(mixo, g1))

    gw_o = _mm(merged, dmixo, 'tn', tm=D, tn=512, tk=S, name="gw_o", out_dtype=BF16, after=token_a)
    d_ret_out, d_att_out, dga, dgb = _dmerged_split(dmixo, w_o, proj, ret_out, att_out)
    gw_ret_out = _mm(gated, d_ret_out, 'tn', tm=512, tn=D, tk=S, name="gw_ret_out", out_dtype=BF16)
    gw_att_out = _mm(att, d_att_out, 'tn', tm=AW, tn=D, tk=S, name="gw_att_out", out_dtype=BF16)
    fulls_b = [_to_slots(g, ax) for g, ax in zip((gw_ret_out, gw_att_out, gw_o), BIG_AXES[1:4])]
    dgated, recv_core_b = _mm(d_ret_out, w_ret_out, 'nt', tm=S, tn=512, tk=D, name="dgated",
                              comm=_ExchangeCore(fulls_b))
    parts_b = _reduce_sums(fulls_b, recv_core_b, core, "b")
    flight_b, token_b = _chip_exchange_start(parts_b, "rs_b_start")
    datt = _mm(d_att_out, w_att_out, 'nt', tm=S, tn=AW, tk=D, name="datt", after=token_b)
    mix_grads = _mix_bwd(outs, lses, datt)
    datt_parts, ds_sums = [], []
    for gi in range(len(ATT_GROUPS)):
        dq, dk, dv, ds_sum = _att_bwd(slabs, bias, outs[gi], lses[gi], mix_grads[gi], mix_grads[3 + gi], gi)
        datt_parts += [dq, dk, dv]
        ds_sums.append(ds_sum)
    g_bias = _bias_grad(jnp.concatenate(ds_sums, axis=0), buckets)[:, :, 0].T.reshape(1, -1)
    dproj, g_gn_g, g_gn_b = _ret_bwd(proj, tables, gn_g, gn_b, ro, states, dgated, datt_parts + [dga, dgb])
    parts_a, recv_chip_a = _chip_exchange_wait(flight_a, dproj, "rs_a_wait")
    parts_b, recv_chip_b = _chip_exchange_wait(flight_b, dproj, "rs_b_wait")
    reduced = list(zip(parts_b + parts_a, recv_chip_b + recv_chip_a))
    full_in = _to_slots(_mm(dproj, h1, 'tn', tm=512, tn=D, tk=S, name="gw_in", out_dtype=BF16), 0)
    halves = [(half * (D // 2), D // 2) for half in range(2)]
    (recv_core_in0,) = _run_comm(_ExchangeCore([full_in], cols=halves[0]), "rs_core_in0")
    flight0, token = _chip_exchange_start([_pair_sum(full_in, recv_core_in0, core, "rs_pair_c0", col_block=0)],
                                          "rs_in0_start")
    dh1, (recv_core_in1,) = _mm(dproj, w_in_t, 'nn', tm=1024, tn=1024, tk=2560, name="dh1",
                                comm=_ExchangeCore([full_in], cols=halves[1]), after=token)
    flight1, token = _chip_exchange_start([_pair_sum(full_in, recv_core_in1, core, "rs_pair_c1", col_block=1)],
                                          "rs_in1_start")
    in_flight = [flight0, flight1]
    gx, dsc1, dsh1, g_norm1 = _norm_mod_bwd(x, norm1_g, sc1, dh1, dx1, "norm1_bwd", after=token)

    dmod = [dsh1, dsc1, dg1, dsh2, dsc2, dg2]
    small_g = [g_norm1, g_bias, g_gn_g, g_gn_b, g_norm2, g_normf]
    return loss, gx, in_flight, reduced, small_g, dmod


def _to_slots(g, axis):
    if axis == 0:
        return g.reshape(4, 2, g.shape[0] // N_DEV, g.shape[1])
    return g.reshape(g.shape[0], N_DEV, g.shape[1] // N_DEV).transpose(1, 0, 2).reshape(4, 2, g.shape[0], -1)


def _from_slots(w8, axis):
    if axis == 0:
        return w8.reshape(-1, w8.shape[2])
    return w8.transpose(1, 0, 2).reshape(w8.shape[1], -1)


BIG_AXES = (1, 0, 1, 0, 1, 0)


def kernel(x, c, w_ada, b_ada, norm1_g, w_in, rel_bias, ret_gn_g, ret_gn_b, w_ret_out, w_att_out, w_o, norm2_g, w_ff1, w_ff2, norm_f_g, loss_target, m_w_ada, m_b_ada, m_norm1_g, m_w_in, m_rel_bias, m_ret_gn_g, m_ret_gn_b, m_w_ret_out, m_w_att_out, m_w_o, m_norm2_g, m_w_ff1, m_w_ff2, m_norm_f_g, v_w_ada, v_b_ada, v_norm1_g, v_w_in, v_rel_bias, v_ret_gn_g, v_ret_gn_b, v_w_ret_out, v_w_att_out, v_w_o, v_norm2_g, v_w_ff1, v_w_ff2, v_norm_f_g):
    mx, my, mc = _mesh_pos()
    dev = 4 * mx + 2 * my + mc
    chip = jnp.reshape(2 * mx + my, (1,)).astype(jnp.int32)
    core = jnp.reshape(mc, (1,)).astype(jnp.int32)
    ada_w = D * 6 // N_DEV

    w_in, m_w_in, v_w_in = (jnp.transpose(t, (0, 2, 1)) for t in (w_in, m_w_in, v_w_in))

    (w_in_shard,), (c_all,) = _to_bf16([w_in[0]], _Gather([c]), "gather_c")
    c_all = c_all.reshape(N_DEV, D)
    b_sl = lax.dynamic_slice(b_ada, (0, dev * ada_w), (1, ada_w))
    other_shards, (mod_all,) = _to_bf16([w[0] for w in (w_ret_out, w_att_out, w_o, w_ff1, w_ff2)],
                                        _Gather([_ada_fwd(c_all, w_ada[0], b_sl)]), "gather_mod")
    mod = lax.dynamic_index_in_dim(mod_all, dev, axis=1, keepdims=False).reshape(6, D)
    mods = tuple(mod[i:i + 1] for i in range(6))

    small = (norm1_g, rel_bias, ret_gn_g, ret_gn_b, norm2_g, norm_f_g.reshape(1, D))
    order = lax.dynamic_index_in_dim(jnp.asarray(_proj_order()), 2 * mx + my, axis=0, keepdims=False)
    loss, gx, in_flight, big_red, small_g, dmod = _local_step(x[0], loss_target[0], mods, w_in_shard, order,
                                                              list(other_shards), small, chip, core)

    names = ['w_ada', 'b_ada', 'norm1_g', 'w_in', 'rel_bias', 'ret_gn_g', 'ret_gn_b', 'w_ret_out', 'w_att_out',
             'w_o', 'norm2_g', 'w_ff1', 'w_ff2', 'norm_f_g']
    ws = dict(zip(names, (w_ada, b_ada, norm1_g, w_in, rel_bias, ret_gn_g, ret_gn_b, w_ret_out, w_att_out, w_o,
                          norm2_g, w_ff1, w_ff2, norm_f_g)))
    ms = dict(zip(names, (m_w_ada, m_b_ada, m_norm1_g, m_w_in, m_rel_bias, m_ret_gn_g, m_ret_gn_b, m_w_ret_out,
                          m_w_att_out, m_w_o, m_norm2_g, m_w_ff1, m_w_ff2, m_norm_f_g)))
    vs = dict(zip(names, (v_w_ada, v_b_ada, v_norm1_g, v_w_in, v_rel_bias, v_ret_gn_g, v_ret_gn_b, v_w_ret_out,
                          v_w_att_out, v_w_o, v_norm2_g, v_w_ff1, v_w_ff2, v_norm_f_g)))
    grads, delta, new_m, new_v = {}, {}, {}, {}
    big_names = ('w_ret_out', 'w_att_out', 'w_o', 'w_ff1', 'w_ff2')
    for n, (part, recv) in zip(big_names, big_red):
        grads[n], delta[n], new_m[n], new_v[n] = _adamw_reduced1(ws[n], ms[n], vs[n], part, recv, chip, "adamw_" + n)
    updated = lax.optimization_barrier((gx, tuple(delta[n] for n in big_names)))
    rows = dmod + small_g + [loss]
    (gathered,) = _run_comm(_Gather([jnp.concatenate(rows, axis=1)]), "gather_small", after=updated[0])
    g_b_ada, dmod_all, (g_norm1, g_bias, g_gn_g, g_gn_b, g_norm2, g_normf, loss_sum) = _sum_small(
        gathered, [r.shape[1] for r in rows[N_MOD:]])
    loss_out = loss_sum[0, 0]
    g_w_ada = _ada_bwd(c_all, lax.dynamic_slice(dmod_all, (0, dev * ada_w), (N_DEV, ada_w)))

    grads.update(w_ada=g_w_ada.reshape(w_ada.shape), b_ada=g_b_ada, norm1_g=g_norm1, rel_bias=g_bias,
                 ret_gn_g=g_gn_g, ret_gn_b=g_gn_b, norm2_g=g_norm2, norm_f_g=g_normf)
    delta['w_ada'], new_m['w_ada'], new_v['w_ada'] = _adamw(w_ada, g_w_ada, m_w_ada, v_w_ada, "adamw_w_ada")
    small_names = ('b_ada', 'norm1_g', 'rel_bias', 'ret_gn_g', 'ret_gn_b', 'norm2_g', 'norm_f_g')
    two_d = {n: (1, ws[n].size) if ws[n].ndim == 1 else ws[n].shape for n in small_names}
    d_, m_, v_ = _adamw_small(*[[src[n].reshape(two_d[n]) for n in small_names] for src in (ws, grads, ms, vs)])
    for i, n in enumerate(small_names):
        shp = ws[n].shape
        delta[n], new_m[n], new_v[n] = d_[i].reshape(shp), m_[i].reshape(shp), v_[i].reshape(shp)
        grads[n] = grads[n].reshape(shp)

    done = lax.optimization_barrier((gx, tuple(d_), tuple(delta[n] for n in ('w_ada', 'w_ret_out', 'w_att_out', 'w_o',
                                                                               'w_ff1', 'w_ff2'))))
    parts_in, recvs_in = [], []
    for half, flight in enumerate(in_flight):
        (part_in,), (recv_chip_in,) = _chip_exchange_wait(flight, done[0], f"rs_in{half}_wait")
        parts_in.append(part_in)
        recvs_in.append(recv_chip_in)
    grads['w_in'], delta['w_in'], new_m['w_in'], new_v['w_in'] = _adamw_reduced(w_in, m_w_in, v_w_in, parts_in,
                                                                               recvs_in, chip)
    for d in (grads, delta, new_m, new_v):
        d['w_in'] = jnp.transpose(d['w_in'], (0, 2, 1))
    return (loss_out, gx[None], *[grads[n] for n in names], *[delta[n] for n in names],
            *[new_m[n] for n in names], *[new_v[n] for n in names])
```

```python
import functools
import math

import numpy as np
import jax
import jax.numpy as jnp
from jax import lax
from jax.experimental import pallas as pl
from jax.experimental.pallas import tpu as pltpu

F32 = jnp.float32
BF16 = jnp.bfloat16
MESH = pl.DeviceIdType.MESH

N_DEV = 8
S = 2048
D = 1024
RET_HEADS = 4
RET_DK = 256
RET_DV = 512
CHUNK = 128
N_CHUNK = S // CHUNK
ATT_GROUPS = ((128, 1), (512, 4), (2048, 16))
ATT_HG = 4
ATT_DH = 128
ATT_BLK = 128
N_BUCKETS = 32
MAX_DIST = 2048
D_FF = 4096
IN_COLS = 12800
OFF_RQ, OFF_RK, OFF_RV, OFF_RG, OFF_ATT = 0, 1024, 2048, 4096, 6144
OFF_GA, OFF_GB = 6144, 7168
RMS_EPS = 1e-6
GN_EPS = 1e-5
ADAM_LR, ADAM_B1, ADAM_B2, ADAM_EPS, ADAM_WD, ADAM_STEP = 0.001, 0.9, 0.999, 1e-08, 0.01, 10
VMEM_LIMIT = 48 * 1024 * 1024


def _pcall(body, **kw):
    return pl.pallas_call(body, **kw)


def _params(sem=None):
    return pltpu.CompilerParams(dimension_semantics=sem, vmem_limit_bytes=VMEM_LIMIT)


HBM_SPEC = pl.BlockSpec(memory_space=pl.ANY)


def _carry(body, comm, *, name, grid, in_specs, out_specs, out_shape, scratch_shapes=()):
    single = not isinstance(out_specs, (tuple, list))
    o_specs = (out_specs,) if single else tuple(out_specs)
    o_shape = (out_shape,) if single else tuple(out_shape)
    n_in, n_out, n_scr = len(in_specs), len(o_specs), len(scratch_shapes)
    nci, nco = len(comm.ins), len(comm.out_shape)
    total = int(np.prod(grid))

    def wrapped(*refs):
        bounds = np.cumsum([0, n_in, nci, n_out, nco, n_scr])
        a, ci, o, co, scr = (refs[bounds[i]:bounds[i + 1]] for i in range(5))
        sems = refs[bounds[5]:]
        flat = 0
        for d, g in enumerate(grid):
            flat = flat * g + pl.program_id(d)

        @pl.when(flat == 0)
        def _():
            comm.start(ci, co, sems)

        body(*a, *o, *scr)

        @pl.when(flat == total - 1)
        def _():
            comm.finish(ci, co, sems)

    aliases = {n_in + i: n_out + o for i, o in getattr(comm, "aliases", {}).items()}
    call = _pcall(wrapped, name=name, grid=grid, in_specs=list(in_specs) + [HBM_SPEC] * nci,
                  out_specs=o_specs + (HBM_SPEC,) * nco, out_shape=o_shape + tuple(comm.out_shape),
                  scratch_shapes=list(scratch_shapes) + list(comm.sems), input_output_aliases=aliases,
                  compiler_params=_params(("arbitrary",) * len(grid)))

    def run(*args):
        res = call(*args, *comm.ins)
        own = res[0] if single else tuple(res[:n_out])
        return own, tuple(res[n_out:])

    return run


def _run_comm(comm, name, after=None):
    nci, nco = len(comm.ins), len(comm.out_shape)
    extra = [] if after is None else [after]

    def body(*refs):
        ci, co, sems = refs[:nci], refs[nci + len(extra):nci + len(extra) + nco], refs[nci + len(extra) + nco:]
        comm.start(ci, co, sems)
        comm.finish(ci, co, sems)

    in_spec = pl.BlockSpec(memory_space=pltpu.VMEM) if getattr(comm, "ins_in_vmem", False) else HBM_SPEC
    return _pcall(body, name=name, in_specs=[in_spec] * nci + [HBM_SPEC] * len(extra), out_specs=(HBM_SPEC,) * nco,
                  out_shape=tuple(comm.out_shape), scratch_shapes=list(comm.sems))(*comm.ins, *extra)


def _dot(a, b, dn):
    return lax.dot_general(a.astype(BF16), b.astype(BF16), (dn, ((), ())), preferred_element_type=F32)


NN = ((1,), (0,))
NT = ((1,), (1,))
TN = ((0,), (0,))


def _mm(a, b, mode, *, tm, tn, tk, name, out_dtype=F32, res=None, gvec=None, relu2=False, relu2_of=None, comm=None,
        after=None):
    if mode == 'nn':
        (M, K), (_, N) = a.shape, b.shape
        a_spec = pl.BlockSpec((tm, tk), lambda i, j, k: (i, k))
        b_spec = pl.BlockSpec((tk, tn), lambda i, j, k: (k, j))
        dn = NN
    elif mode == 'nt':
        (M, K), (N, _) = a.shape, b.shape
        a_spec = pl.BlockSpec((tm, tk), lambda i, j, k: (i, k))
        b_spec = pl.BlockSpec((tn, tk), lambda i, j, k: (j, k))
        dn = NT
    else:
        (K, M), (_, N) = a.shape, b.shape
        a_spec = pl.BlockSpec((tk, tm), lambda i, j, k: (k, i))
        b_spec = pl.BlockSpec((tk, tn), lambda i, j, k: (k, j))
        dn = TN
    assert M % tm == 0 and N % tn == 0 and K % tk == 0, (name, M, N, K)
    nk = K // tk
    fused = res is not None
    o_spec = pl.BlockSpec((tm, tn), lambda i, j, k: (i, j))

    def body(a_ref, b_ref, *rest):
        acc_ref = rest[-1] if nk > 1 else None
        if after is not None:
            rest = rest[1:]
        if fused:
            res_ref, g_ref, o_ref, x_ref = rest[:4]
        elif relu2_of is not None:
            u_ref, o_ref = rest[:2]
        elif relu2:
            o_ref, act_ref = rest[:2]
        else:
            o_ref = rest[0]

        def finish(acc):
            if relu2_of is not None:
                acc = acc * (2.0 * jnp.maximum(u_ref[...], 0.0))
            o_ref[...] = acc.astype(o_ref.dtype)
            if fused:
                x_ref[...] = res_ref[...] + g_ref[...] * acc
            if relu2:
                r = jnp.maximum(acc, 0.0)
                act_ref[...] = (r * r).astype(BF16)

        p = _dot(a_ref[...], b_ref[...], dn)
        if nk == 1:
            finish(p)
        else:
            k = pl.program_id(2)

            @pl.when(k == 0)
            def _():
                acc_ref[...] = p

            @pl.when(k > 0)
            def _():
                acc_ref[...] += p

            @pl.when(k == nk - 1)
            def _():
                finish(acc_ref[...])

    in_specs = [a_spec, b_spec]
    args = [a, b]
    if after is not None:
        in_specs.append(pl.BlockSpec(memory_space=pl.ANY))
        args.append(after)
    out_shape = jax.ShapeDtypeStruct((M, N), out_dtype)
    out_specs = o_spec
    if fused:
        in_specs += [pl.BlockSpec((tm, tn), lambda i, j, k: (i, j)), pl.BlockSpec((1, tn), lambda i, j, k: (0, j))]
        args += [res, gvec]
        out_shape = (out_shape, jax.ShapeDtypeStruct((M, N), F32))
        out_specs = (o_spec, pl.BlockSpec((tm, tn), lambda i, j, k: (i, j)))
    elif relu2_of is not None:
        in_specs.append(pl.BlockSpec((tm, tn), lambda i, j, k: (i, j)))
        args.append(relu2_of)
    elif relu2:
        out_shape = (out_shape, jax.ShapeDtypeStruct((M, N), BF16))
        out_specs = (o_spec, pl.BlockSpec((tm, tn), lambda i, j, k: (i, j)))
    kw = dict(name=name, grid=(M // tm, N // tn, nk), in_specs=in_specs, out_specs=out_specs,
              out_shape=out_shape, scratch_shapes=[pltpu.VMEM((tm, tn), F32)] if nk > 1 else [])
    if comm is not None:
        return _carry(body, comm, **kw)(*args)
    return _pcall(body, compiler_params=_params(("parallel", "parallel", "arbitrary")), **kw)(*args)


PROJ_TN = 512
ATT_T0, ATT_T1 = 6144 // PROJ_TN, 10752 // PROJ_TN
N_SLABS = (ATT_T1 - ATT_T0) * 4
MAIN_COLS = IN_COLS - (ATT_T1 - ATT_T0) * PROJ_TN


PROJ_TILES = IN_COLS // PROJ_TN
SHARD_ROWS = IN_COLS // N_DEV
W_CHUNKS = 4
N_OWN, N_NEAR = 5, 18


def _proj_order():
    out = np.zeros((4, 3, PROJ_TILES), np.int32)
    for q in range(4):
        def hops(t):
            owners = {col // (2 * SHARD_ROWS) for col in (t * PROJ_TN, (t + 1) * PROJ_TN - 1)}
            return max(bin(q ^ p).count("1") for p in owners)
        order = sorted(range(PROJ_TILES), key=lambda t: (hops(t), t))
        assert all(hops(t) == 0 for t in order[:N_OWN]) and all(hops(t) < 2 for t in order[:N_NEAR])
        is_att = [ATT_T0 <= t < ATT_T1 for t in order]
        for row, kind, index in ((1, False, lambda t: t if t < ATT_T0 else t - (ATT_T1 - ATT_T0)),
                                 (2, True, lambda t: t - ATT_T0)):
            own = [index(t) if a == kind else None for t, a in zip(order, is_att)]
            first = next(v for v in own if v is not None)
            last = first
            for j, v in enumerate(own):
                last = last if v is None else v
                out[q, row, j] = last
        out[q, 0] = order
    return out


def _gather_proj(x, g, sh, sc, shard, order):
    rows = SHARD_ROWS // W_CHUNKS

    def body(ord_ref, x_ref, g_ref, shift_ref, scale_ref, sh_ref, main_ref, slab_ref, full_ref, a_ref, wbuf, xbuf,
             fetch_sems, send_sems, recv_sems, local_sems, x_sem):
        j = pl.program_id(0)
        x, y, c = _mesh_pos()
        me, sibling = (x, y, c), (x, y, 1 - c)
        chips = [(1 - x, y), (x, 1 - y), (1 - x, 1 - y)]

        def block(p, owner):
            return full_ref.at[pl.ds(pl.multiple_of(_slot(owner) * SHARD_ROWS + p * rows, 16), rows)]

        def copy(p, k, owner, to, from_input=False):
            dst = block(p, owner)
            return pltpu.make_async_remote_copy(
                src_ref=sh_ref.at[pl.ds(p * rows, rows)] if from_input else dst, dst_ref=dst,
                send_sem=send_sems.at[7 * p + k], recv_sem=recv_sems.at[7 * p + k], device_id=to, device_id_type=MESH)

        pieces = range(W_CHUNKS)
        mine = [pltpu.make_async_copy(sh_ref.at[pl.ds(p * rows, rows)], block(p, me), local_sems.at[p]) for p in pieces]
        first = [copy(p, 0, me, sibling, from_input=True) for p in pieces]
        first += [copy(p, 1 + n, me, (*chips[n], c), from_input=True) for p in pieces for n in range(2)]
        near_pass = [copy(p, 4 + n, (*chips[n], c), sibling) for p in pieces for n in range(2)]
        relay = [copy(p, 3, ((x + 1 - c) % 2, (y + c) % 2, c), ((x + c) % 2, (y + 1 - c) % 2, c)) for p in pieces]
        far_pass = [copy(p, 6, (*chips[2], c), sibling) for p in pieces]

        def fetch(pos):
            slot = lax.rem(pos, 2)
            start = pl.multiple_of(ord_ref[0, pos] * PROJ_TN, PROJ_TN)
            return pltpu.make_async_copy(full_ref.at[pl.ds(start, PROJ_TN)], wbuf.at[slot], fetch_sems.at[slot])

        @pl.when(j == 0)
        def _():
            x_copy = pltpu.make_async_copy(x_ref, xbuf, x_sem.at[0])
            x_copy.start()
            for cp in mine + first:
                cp.start()
            x_copy.wait()
            for r in range(S // TR):
                rws = pl.ds(r * TR, TR)
                xv = xbuf[rws, :]
                rstd = lax.rsqrt(jnp.mean(xv * xv, axis=-1, keepdims=True) + RMS_EPS)
                n = xv * rstd * g_ref[...]
                a_ref[rws, :] = (n * (1.0 + scale_ref[...]) + shift_ref[...]).astype(BF16)
            for cp in mine:
                cp.wait()
            for p in pieces:
                copy(p, 0, sibling, me).wait_recv()
            fetch(j).start()

        @pl.when(j == N_OWN - 1)
        def _():
            for p in pieces:
                for n in range(2):
                    copy(p, 1 + n, (*chips[n], c), me).wait_recv()
                    near_pass[2 * p + n].start()
                relay[p].start()
            for p in pieces:
                for n in range(2):
                    copy(p, 4 + n, (*chips[n], 1 - c), me).wait_recv()

        @pl.when(j == N_NEAR - 1)
        def _():
            for p in pieces:
                copy(p, 3, (*chips[2], c), me).wait_recv()
                far_pass[p].start()
            for p in pieces:
                copy(p, 6, (*chips[2], 1 - c), me).wait_recv()

        @pl.when(j + 1 < PROJ_TILES)
        def _():
            fetch(j + 1).start()

        fetch(j).wait()
        w_ref = wbuf.at[lax.rem(j, 2)]
        tile = ord_ref[0, j]
        is_att = (tile >= ATT_T0) & (tile < ATT_T1)
        chunks = [pl.ds(r * 512, 512) for r in range(S // 512)]

        @pl.when(jnp.logical_not(is_att))
        def _():
            for rws in chunks:
                main_ref[rws, :] = _dot(a_ref[rws, :], w_ref[...], NT)

        @pl.when(is_att)
        def _():
            for rws in chunks:
                p = _dot(a_ref[rws, :], w_ref[...], NT)
                for h in range(4):
                    slab_ref[h, rws, :] = p[:, h * 128:(h + 1) * 128]

        @pl.when(j == PROJ_TILES - 1)
        def _():
            for cp in first + near_pass + relay + far_pass:
                cp.wait_send()

    vec = pl.BlockSpec((1, D), lambda j, o: (0, 0))
    gs = pltpu.PrefetchScalarGridSpec(
        num_scalar_prefetch=1, grid=(PROJ_TILES,),
        in_specs=[HBM_SPEC, vec, vec, vec, HBM_SPEC],
        out_specs=(pl.BlockSpec((S, PROJ_TN), lambda j, o: (0, o[1, j])),
                   pl.BlockSpec((4, S, 128), lambda j, o: (o[2, j], 0, 0)), HBM_SPEC,
                   pl.BlockSpec((S, D), lambda j, o: (0, 0))),
        scratch_shapes=[pltpu.VMEM((2, PROJ_TN, D), BF16), pltpu.VMEM((S, D), F32), pltpu.SemaphoreType.DMA((2,)),
                        pltpu.SemaphoreType.DMA((7 * W_CHUNKS,)), pltpu.SemaphoreType.DMA((7 * W_CHUNKS,)),
                        pltpu.SemaphoreType.DMA((W_CHUNKS,)), pltpu.SemaphoreType.DMA((1,))])
    return _pcall(body, name="gather_proj", grid_spec=gs,
                  out_shape=(jax.ShapeDtypeStruct((S, MAIN_COLS), F32), jax.ShapeDtypeStruct((N_SLABS, S, 128), F32),
                             jax.ShapeDtypeStruct((IN_COLS, D), BF16), jax.ShapeDtypeStruct((S, D), BF16)),
                  compiler_params=_params(("arbitrary",)))(order, x, g, sh, sc, shard)


TR = 256


def _row_spec(w=D):
    return pl.BlockSpec((TR, w), lambda i: (i, 0))


def _vec_spec(w=D):
    return pl.BlockSpec((1, w), lambda i: (0, 0))


def _norm_mod_bwd(x, g, sc, dh, dres, name, gate=None, after=None):
    gated = gate is not None

    def body(x_ref, g_ref, sc_ref, dh_ref, dres_ref, *rest):
        if after is not None:
            rest = rest[1:]
        if gated:
            f_ref, gv_ref, dx_ref, dsc_ref, dsh_ref, dg_ref, dz_ref, dgv_ref = rest
        else:
            dx_ref, dsc_ref, dsh_ref, dg_ref = rest
        i = pl.program_id(0)
        xv = x_ref[...]
        dh = dh_ref[...]
        rstd = lax.rsqrt(jnp.mean(xv * xv, axis=-1, keepdims=True) + RMS_EPS)
        xhat = xv * rstd
        gv = g_ref[...]
        dn = dh * (1.0 + sc_ref[...])
        dxhat = dn * gv
        dx = dres_ref[...] + rstd * (dxhat - xhat * jnp.mean(dxhat * xhat, axis=-1, keepdims=True))
        dx_ref[...] = dx
        sums = [(dsc_ref, jnp.sum(dh * (xhat * gv), axis=0, keepdims=True)),
                (dsh_ref, jnp.sum(dh, axis=0, keepdims=True)),
                (dg_ref, jnp.sum(dn * xhat, axis=0, keepdims=True))]
        if gated:
            dz_ref[...] = (dx * gv_ref[...]).astype(BF16)
            sums.append((dgv_ref, jnp.sum(dx * f_ref[...], axis=0, keepdims=True)))

        @pl.when(i == 0)
        def _():
            for ref, p in sums:
                ref[...] = p

        @pl.when(i > 0)
        def _():
            for ref, p in sums:
                ref[...] += p

    vec = jax.ShapeDtypeStruct((1, D), F32)
    in_specs = [_row_spec(), _vec_spec(), _vec_spec(), _row_spec(), _row_spec()]
    out_specs = [_row_spec(), _vec_spec(), _vec_spec(), _vec_spec()]
    out_shape = [jax.ShapeDtypeStruct((S, D), F32), vec, vec, vec]
    args = [x, g, sc, dh, dres]
    if after is not None:
        in_specs.append(HBM_SPEC)
        args.append(after)
    if gated:
        in_specs += [_row_spec(), _vec_spec()]
        out_specs += [_row_spec(), _vec_spec()]
        out_shape += [jax.ShapeDtypeStruct((S, D), BF16), vec]
        args += list(gate)
    return _pcall(body, name=name, grid=(S // TR,), in_specs=in_specs, out_specs=tuple(out_specs),
                  out_shape=tuple(out_shape), compiler_params=_params(("arbitrary",)))(*args)


def _w_o_norm2(merged, w_o, x, g1, g, sh, sc):
    def body(a_ref, b_ref, x_ref, g1_ref, g_ref, sh_ref, sc_ref, o_ref, x1_ref, h_ref):
        acc = _dot(a_ref[...], b_ref[...], NN)
        o_ref[...] = acc
        xv = x_ref[...] + g1_ref[...] * acc
        x1_ref[...] = xv
        rstd = lax.rsqrt(jnp.mean(xv * xv, axis=-1, keepdims=True) + RMS_EPS)
        h_ref[...] = (xv * rstd * g_ref[...] * (1.0 + sc_ref[...]) + sh_ref[...]).astype(BF16)

    rows = pl.BlockSpec((FF2_TM, D), lambda i: (i, 0))
    f32 = jax.ShapeDtypeStruct((S, D), F32)
    return _pcall(body, name="w_o_norm2", grid=(S // FF2_TM,),
                  in_specs=[rows, pl.BlockSpec((D, D), lambda i: (0, 0)), rows] + [_vec_spec()] * 4,
                  out_specs=(rows, rows, rows), out_shape=(f32, f32, jax.ShapeDtypeStruct((S, D), BF16)),
                  compiler_params=_params(("parallel",)))(merged, w_o, x, g1, g, sh, sc)


FF2_TM = 512


def _ff2_final(act, w_ff2, x1, g2, tgt, g):
    def body(a_ref, b_ref, x1_ref, g2_ref, t_ref, g_ref, loss_ref, dx_ref, dg_ref, df_ref, dg2_ref):
        i = pl.program_id(0)
        f = _dot(a_ref[...], b_ref[...], NN)
        g2v = g2_ref[...]
        xv = x1_ref[...] + g2v * f
        gv = g_ref[...]
        rstd = lax.rsqrt(jnp.mean(xv * xv, axis=-1, keepdims=True) + RMS_EPS)
        xhat = xv * rstd
        err = xhat * gv - t_ref[...]
        dy = err * (1.0 / D)
        dxhat = dy * gv
        dx = rstd * (dxhat - xhat * jnp.mean(dxhat * xhat, axis=-1, keepdims=True))
        dx_ref[...] = dx
        df_ref[...] = (dx * g2v).astype(BF16)
        p_g = jnp.sum(dy * xhat, axis=0, keepdims=True)
        p_g2 = jnp.sum(dx * f, axis=0, keepdims=True)
        p_l = jnp.zeros((1, 128), F32) + 0.5 * jnp.sum(jnp.mean(err * err, axis=-1, keepdims=True))

        @pl.when(i == 0)
        def _():
            dg_ref[...] = p_g
            dg2_ref[...] = p_g2
            loss_ref[...] = p_l

        @pl.when(i > 0)
        def _():
            dg_ref[...] += p_g
            dg2_ref[...] += p_g2
            loss_ref[...] += p_l

    vec = jax.ShapeDtypeStruct((1, D), F32)
    rows = lambda w: pl.BlockSpec((FF2_TM, w), lambda i: (i, 0))
    return _pcall(body, name="ff2_final", grid=(S // FF2_TM,),
                  in_specs=[rows(D_FF), pl.BlockSpec((D_FF, D), lambda i: (0, 0)), rows(D), _vec_spec(), rows(D),
                            _vec_spec()],
                  out_specs=(_vec_spec(128), rows(D), _vec_spec(), rows(D), _vec_spec()),
                  out_shape=(jax.ShapeDtypeStruct((1, 128), F32), jax.ShapeDtypeStruct((S, D), F32), vec,
                             jax.ShapeDtypeStruct((S, D), BF16), vec),
                  compiler_params=_params(("arbitrary",)))(act, w_ff2, x1, g2, tgt, g)


HALF = 512


MERGE_TM = 1024


def _merge_specs():
    blk = lambda off: pl.BlockSpec((MERGE_TM, HALF), lambda i, j: (i, off // HALF + j))
    return blk(OFF_GA), blk(OFF_GB), blk(0)


def _att_out_merge(att, w_att_out, proj, ret_out):
    def body(a_ref, b_ref, ga_ref, gb_ref, r_ref, o_ref, m_ref):
        acc = _dot(a_ref[...], b_ref[...], NN)
        o_ref[...] = acc
        m_ref[...] = (jax.nn.sigmoid(ga_ref[...]) * r_ref[...] + jax.nn.sigmoid(gb_ref[...]) * acc).astype(BF16)

    ga, gb, tile = _merge_specs()
    return _pcall(body, name="att_out", grid=(S // MERGE_TM, D // HALF),
                  in_specs=[pl.BlockSpec((MERGE_TM, AW), lambda i, j: (i, 0)), pl.BlockSpec((AW, HALF), lambda i, j: (0, j)),
                            ga, gb, tile],
                  out_specs=(tile, tile),
                  out_shape=(jax.ShapeDtypeStruct((S, D), F32), jax.ShapeDtypeStruct((S, D), BF16)),
                  compiler_params=_params(("parallel", "parallel")))(att, w_att_out, proj, proj, ret_out)


def _dmerged_split(dmixo, w_o, proj, ret_out, att_out):
    def body(a_ref, b_ref, ga_ref, gb_ref, r_ref, at_ref, dr_ref, da_ref, dga_ref, dgb_ref):
        dm = _dot(a_ref[...], b_ref[...], NT)
        sa = jax.nn.sigmoid(ga_ref[...])
        sb = jax.nn.sigmoid(gb_ref[...])
        dr_ref[...] = (dm * sa).astype(BF16)
        da_ref[...] = (dm * sb).astype(BF16)
        dga_ref[...] = (dm * r_ref[...] * (sa * (1.0 - sa))).astype(BF16)
        dgb_ref[...] = (dm * at_ref[...] * (sb * (1.0 - sb))).astype(BF16)

    ga, gb, tile = _merge_specs()
    o = jax.ShapeDtypeStruct((S, D), BF16)
    return _pcall(body, name="dmerged", grid=(S // MERGE_TM, D // HALF),
                  in_specs=[pl.BlockSpec((MERGE_TM, D), lambda i, j: (i, 0)), pl.BlockSpec((HALF, D), lambda i, j: (j, 0)),
                            ga, gb, tile, tile],
                  out_specs=(tile,) * 4, out_shape=(o, o, o, o),
                  compiler_params=_params(("parallel", "parallel")))(dmixo, w_o, proj, proj, ret_out, att_out)


def _ret_tables():
    H, C = RET_HEADS, CHUNK
    log_g = jnp.log1p(-(2.0 ** (-5.0 - jnp.arange(H, dtype=F32))))
    idx = jnp.arange(C, dtype=F32)
    rel = idx[:, None] - idx[None, :]
    inner = jnp.where(rel >= 0, jnp.exp(log_g[:, None, None] * jnp.maximum(rel, 0.0)), 0.0)
    qd = jnp.exp(log_g[:, None] * (idx + 1.0))[:, :, None]
    kd = jnp.exp(log_g[:, None] * (C - 1.0 - idx))[:, :, None]
    cd = jnp.broadcast_to(jnp.exp(log_g * C)[:, None, None], (H, 1, 128))
    half = RET_DK // 2
    inv = 10000.0 ** (-jnp.arange(half, dtype=F32) / half)
    ang = jnp.arange(S, dtype=F32)[:, None] * inv[None, :]
    return inner, qd, kd, cd, jnp.cos(ang), jnp.sin(ang)


def _rot(x, cos, sin):
    x1, x2 = x[:, :128], x[:, 128:]
    return jnp.concatenate([x1 * cos - x2 * sin, x1 * sin + x2 * cos], axis=1)


def _rot_t(d, cos, sin):
    d1, d2 = d[:, :128], d[:, 128:]
    return jnp.concatenate([d1 * cos + d2 * sin, d2 * cos - d1 * sin], axis=1)


RET_COLS = OFF_ATT
RET_VW = RET_HEADS * RET_DV


def _ret_specs(chunk_of):
    ci = chunk_of
    whole = lambda shape: pl.BlockSpec(shape, lambda t: (0,) * len(shape))
    return [
        pl.BlockSpec((CHUNK, RET_COLS), lambda t: (ci(t), 0)),
        pl.BlockSpec((CHUNK, 128), lambda t: (ci(t), 0)),
        pl.BlockSpec((CHUNK, 128), lambda t: (ci(t), 0)),
        whole((RET_HEADS, CHUNK, CHUNK)), whole((RET_HEADS, CHUNK, 1)), whole((RET_HEADS, CHUNK, 1)),
        whole((RET_HEADS, 1, 128)), whole((1, RET_VW)), whole((1, RET_VW)),
    ]


def _ret_cols(h):
    q = slice(OFF_RQ + h * RET_DK, OFF_RQ + (h + 1) * RET_DK)
    k = slice(OFF_RK + h * RET_DK, OFF_RK + (h + 1) * RET_DK)
    v = slice(OFF_RV + h * RET_DV, OFF_RV + (h + 1) * RET_DV)
    g = slice(OFF_RG + h * RET_DV, OFF_RG + (h + 1) * RET_DV)
    return q, k, v, g, slice(h * RET_DV, (h + 1) * RET_DV)


def _ret_fwd(proj, tables, gn_g, gn_b, after):
    inner, qd, kd, cd, cos, sin = tables

    def body(x_ref, cos_ref, sin_ref, in_ref, qd_ref, kd_ref, cd_ref, g_ref, b_ref, after_ref,
             gated_ref, ro_ref, st_ref, s_scr):
        i = pl.program_id(0)

        @pl.when(i == 0)
        def _():
            s_scr[...] = jnp.zeros_like(s_scr)

        cosv, sinv = cos_ref[...], sin_ref[...]
        for h in range(RET_HEADS):
            cq, ck, cv, cg, co = _ret_cols(h)
            q = _rot(x_ref[:, cq], cosv, sinv)
            k = _rot(x_ref[:, ck], cosv, sinv) * (RET_DK ** -0.5)
            v = x_ref[:, cv]
            st = s_scr[h]
            st_ref[h] = st.astype(BF16)
            s = _dot(q, k, NT) * in_ref[h]
            o = _dot(s, v, NN) + _dot(q, st, NN) * qd_ref[h]
            s_scr[h] = st * cd_ref[h, :, :1] + _dot(k * kd_ref[h], v, TN)
            ro_ref[:, co] = o
            mu = jnp.mean(o, axis=-1, keepdims=True)
            oc = o - mu
            var = jnp.mean(oc * oc, axis=-1, keepdims=True)
            rn = oc * lax.rsqrt(var + GN_EPS) * g_ref[:, co] + b_ref[:, co]
            rg = x_ref[:, cg]
            gated_ref[:, co] = (rg * jax.nn.sigmoid(rg) * rn).astype(BF16)

    ospec = pl.BlockSpec((CHUNK, RET_VW), lambda t: (t, 0))
    return _pcall(
        body, name="ret_fwd", grid=(N_CHUNK,), in_specs=_ret_specs(lambda t: t) + [HBM_SPEC],
        out_specs=(ospec, ospec, pl.BlockSpec((RET_HEADS, None, RET_DK, RET_DV), lambda t: (0, t, 0, 0))),
        out_shape=(jax.ShapeDtypeStruct((S, RET_VW), BF16), jax.ShapeDtypeStruct((S, RET_VW), F32),
                   jax.ShapeDtypeStruct((RET_HEADS, N_CHUNK, RET_DK, RET_DV), BF16)),
        scratch_shapes=[pltpu.VMEM((RET_HEADS, RET_DK, RET_DV), F32)],
        compiler_params=_params(("arbitrary",)))(proj, cos, sin, inner, qd, kd, cd, gn_g, gn_b, after)


def _ret_bwd(proj, tables, gn_g, gn_b, ro, states, dgated, others):
    inner, qd, kd, cd, cos, sin = tables
    last = N_CHUNK - 1
    pieces = lambda o: [(h, o.shape[2]) for h in range(o.shape[0])] if len(o.shape) == 3 else [(None, o.shape[1])]
    assert RET_COLS + sum(w for o in others for _, w in pieces(o)) == IN_COLS

    def body(x_ref, cos_ref, sin_ref, in_ref, qd_ref, kd_ref, cd_ref, g_ref, b_ref, ro_ref, st_ref, dg_ref, *rest):
        other_refs, (dx_ref, gg_ref, gb_ref, gs_scr) = rest[:len(others)], rest[len(others):]
        t = pl.program_id(0)
        col = RET_COLS
        for o_ref in other_refs:
            for h, w in pieces(o_ref):
                dx_ref[:, col:col + w] = (o_ref[...] if h is None else o_ref[h]).astype(BF16)
                col += w

        @pl.when(t == 0)
        def _():
            gs_scr[...] = jnp.zeros_like(gs_scr)
            gg_ref[...] = jnp.zeros_like(gg_ref)
            gb_ref[...] = jnp.zeros_like(gb_ref)

        cosv, sinv = cos_ref[...], sin_ref[...]
        for h in range(RET_HEADS):
            cq, ck, cv, cg, co = _ret_cols(h)
            q = _rot(x_ref[:, cq], cosv, sinv)
            k = _rot(x_ref[:, ck], cosv, sinv) * (RET_DK ** -0.5)
            v = x_ref[:, cv]
            qdv, kdv, dm = qd_ref[h], kd_ref[h], in_ref[h]
            st = st_ref[h]
            o = ro_ref[:, co]
            gv = g_ref[:, co]
            mu = jnp.mean(o, axis=-1, keepdims=True)
            oc = o - mu
            rstd = lax.rsqrt(jnp.mean(oc * oc, axis=-1, keepdims=True) + GN_EPS)
            ohat = oc * rstd
            rn = ohat * gv + b_ref[:, co]
            rg = x_ref[:, cg]
            sg = jax.nn.sigmoid(rg)
            dgt = dg_ref[:, co]
            drn = dgt * (rg * sg)
            dx_ref[:, cg] = (dgt * rn * (sg * (1.0 + rg * (1.0 - sg)))).astype(BF16)
            gg_ref[:, co] += jnp.sum(drn * ohat, axis=0, keepdims=True)
            gb_ref[:, co] += jnp.sum(drn, axis=0, keepdims=True)
            dohat = drn * gv
            do = rstd * (dohat - jnp.mean(dohat, axis=-1, keepdims=True)
                         - ohat * jnp.mean(dohat * ohat, axis=-1, keepdims=True))
            gs = gs_scr[h]
            s = _dot(q, k, NT) * dm
            dsr = _dot(do, v, NT) * dm
            dq = _dot(dsr, k, NN) + _dot(do, st, NT) * qdv
            dk = _dot(dsr, q, TN) + _dot(v, gs, NT) * kdv
            dv = _dot(s, do, TN) + _dot(k * kdv, gs, NN)
            gs_scr[h] = gs * cd_ref[h, :, :1] + _dot(q * qdv, do, TN)
            dx_ref[:, cq] = _rot_t(dq, cosv, sinv).astype(BF16)
            dx_ref[:, ck] = (_rot_t(dk, cosv, sinv) * (RET_DK ** -0.5)).astype(BF16)
            dx_ref[:, cv] = dv.astype(BF16)

    rev = lambda t: last - t
    vblk = pl.BlockSpec((CHUNK, RET_VW), lambda t: (rev(t), 0))
    vspec = pl.BlockSpec((1, RET_VW), lambda t: (0, 0))
    rows = lambda w: pl.BlockSpec((CHUNK, w), lambda t: (rev(t), 0))
    return _pcall(
        body, name="ret_bwd", grid=(N_CHUNK,),
        in_specs=_ret_specs(rev) + [vblk, pl.BlockSpec((RET_HEADS, None, RET_DK, RET_DV), lambda t: (0, rev(t), 0, 0)),
                                    vblk] + [rows(o.shape[1]) if o.ndim == 2 else
                                             pl.BlockSpec((o.shape[0], CHUNK, o.shape[2]), lambda t: (0, rev(t), 0))
                                             for o in others],
        out_specs=(rows(IN_COLS), vspec, vspec),
        out_shape=(jax.ShapeDtypeStruct((S, IN_COLS), BF16), jax.ShapeDtypeStruct((1, RET_VW), F32),
                   jax.ShapeDtypeStruct((1, RET_VW), F32)),
        scratch_shapes=[pltpu.VMEM((RET_HEADS, RET_DK, RET_DV), F32)],
        compiler_params=_params(("arbitrary",)))(proj, cos, sin, inner, qd, kd, cd, gn_g, gn_b, ro, states, dgated,
                                                 *others)


def _bucket_tables():
    qi = np.arange(ATT_BLK)[:, None]
    kj = np.arange(2 * ATT_BLK)[None, :]
    m = ATT_BLK + qi - kj
    out = []
    for win, dil in ATT_GROUPS:
        w = win // dil
        dist = (np.clip(m, 0, w) * dil).astype(np.int32)
        max_exact = N_BUCKETS // 2
        d_f = np.maximum(dist, 1).astype(np.float32)
        large = max_exact + (np.log(d_f / np.float32(max_exact)) / np.float32(math.log(MAX_DIST / max_exact))
                             * np.float32(N_BUCKETS - max_exact)).astype(np.int32)
        large = np.minimum(large, N_BUCKETS - 1)
        out.append(np.where(dist < max_exact, dist, large).astype(np.int32))
    return np.stack(out)


def _bias_build(rel_bias, buckets, after):
    def body(tab_ref, bk_ref, after_ref, o_ref):
        hh = pl.program_id(0)
        bk = bk_ref[...]
        acc = jnp.zeros((ATT_BLK, 2 * ATT_BLK), F32)
        for b in range(N_BUCKETS):
            acc = jnp.where(bk == b, tab_ref[b, hh], acc)
        o_ref[...] = acc

    nh = len(ATT_GROUPS) * ATT_HG
    return _pcall(body, name="bias_build", grid=(nh,),
                  in_specs=[pl.BlockSpec(memory_space=pltpu.SMEM),
                            pl.BlockSpec((None, ATT_BLK, 2 * ATT_BLK), lambda hh: (hh // ATT_HG, 0, 0)), HBM_SPEC],
                  out_specs=pl.BlockSpec((None, ATT_BLK, 2 * ATT_BLK), lambda hh: (hh, 0, 0)),
                  out_shape=jax.ShapeDtypeStruct((nh, ATT_BLK, 2 * ATT_BLK), F32),
                  compiler_params=_params(("parallel",)))(rel_bias, buckets, after)


def _bias_grad(ds_sum, buckets):
    def body(ds_ref, bk_ref, o_ref):
        bk = bk_ref[...]
        ds = ds_ref[...]
        rows = lax.broadcasted_iota(jnp.int32, (N_BUCKETS, 128), 0)
        acc = jnp.zeros((N_BUCKETS, 128), F32)
        for b in range(N_BUCKETS):
            acc = jnp.where(rows == b, jnp.sum(jnp.where(bk == b, ds, 0.0)), acc)
        o_ref[...] = acc

    nh = len(ATT_GROUPS) * ATT_HG
    return _pcall(body, name="bias_grad", grid=(nh,),
                  in_specs=[pl.BlockSpec((None, ATT_BLK, 2 * ATT_BLK), lambda hh: (hh, 0, 0)),
                            pl.BlockSpec((None, ATT_BLK, 2 * ATT_BLK), lambda hh: (hh // ATT_HG, 0, 0))],
                  out_specs=pl.BlockSpec((None, N_BUCKETS, 128), lambda hh: (hh, 0, 0)),
                  out_shape=jax.ShapeDtypeStruct((nh, N_BUCKETS, 128), F32),
                  compiler_params=_params(("parallel",)))(ds_sum, buckets)


def _att_valid(n):
    qi = lax.broadcasted_iota(jnp.int32, (ATT_BLK, 2 * ATT_BLK), 0)
    kj = lax.broadcasted_iota(jnp.int32, (ATT_BLK, 2 * ATT_BLK), 1)
    m = ATT_BLK + qi - kj
    first_key = jnp.where(n > 0, 0, ATT_BLK)
    return (m >= 0) & (m <= ATT_BLK) & (kj >= first_key)


ATT_HP = (1, 2, 2)


def _att_geometry(gi):
    _, dil = ATT_GROUPS[gi]
    return dil, S // dil // ATT_BLK, ATT_HP[gi]


def _blk(dil, r, n):
    if dil == 1:
        return pl.ds(n * ATT_BLK, ATT_BLK)
    return pl.ds(r + n * ATT_BLK * dil, ATT_BLK, stride=dil)


def _slab_specs(gi):
    _, _, hp = _att_geometry(gi)
    per = ATT_HG // hp
    return [pl.BlockSpec((hp, S, ATT_DH), lambda g, r, part=part: ((3 * gi + part) * per + g, 0, 0))
            for part in range(3)]


def _head_specs(gi, count):
    _, _, hp = _att_geometry(gi)
    return [pl.BlockSpec((hp, S, ATT_DH), lambda g, r: (g, 0, 0))] * count


def _bias_spec(gi):
    _, _, hp = _att_geometry(gi)
    return pl.BlockSpec((hp, ATT_BLK, 2 * ATT_BLK), lambda g, r: (gi * (ATT_HG // hp) + g, 0, 0))


def _att_valid_first():
    qi = lax.broadcasted_iota(jnp.int32, (ATT_BLK, ATT_BLK), 0)
    kj = lax.broadcasted_iota(jnp.int32, (ATT_BLK, ATT_BLK), 1)
    return kj <= qi


def _att_fwd(slabs, bias, gi, comm=None):
    dil, nb, hp = _att_geometry(gi)
    scale = ATT_DH ** -0.5

    def body(q_ref, k_ref, v_ref, bias_ref, o_ref, l_ref):
        r = pl.program_id(1)
        for n in range(nb):
            cur = _blk(dil, r, n)
            valid = _att_valid(n) if n > 0 else _att_valid_first()
            for h in range(hp):
                if n > 0:
                    prev = _blk(dil, r, n - 1)
                    kk = jnp.concatenate([k_ref[h, prev, :], k_ref[h, cur, :]], axis=0)
                    vv = jnp.concatenate([v_ref[h, prev, :], v_ref[h, cur, :]], axis=0)
                    bias = bias_ref[h]
                else:
                    kk, vv, bias = k_ref[h, cur, :], v_ref[h, cur, :], bias_ref[h, :, pl.ds(ATT_BLK, ATT_BLK)]
                s = _dot(q_ref[h, cur, :], kk, NT) * scale + bias
                s = jnp.where(valid, s, -1e30)
                mx = jnp.max(s, axis=-1, keepdims=True)
                e = jnp.exp(s - mx)
                den = jnp.sum(e, axis=-1, keepdims=True)
                o_ref[h, cur, :] = _dot(e / den, vv, NN)
                l_ref[h, cur, :] = jnp.broadcast_to(mx + jnp.log(den), (ATT_BLK, ATT_DH))

    osh = pltpu.HBM((ATT_HG, S, ATT_DH), F32)
    kw = dict(name=f"att_fwd{gi}", grid=(ATT_HG // hp, dil), in_specs=_slab_specs(gi) + [_bias_spec(gi)],
              out_specs=tuple(_head_specs(gi, 2)), out_shape=(osh, osh))
    if comm is not None:
        return _carry(body, comm, **kw)(slabs, slabs, slabs, bias)
    return _pcall(body, compiler_params=_params(("parallel", "arbitrary")), **kw)(slabs, slabs, slabs, bias)


def _att_bwd(slabs, bias, o, lse, do, dlse, gi, comm=None):
    dil, nb, hp = _att_geometry(gi)
    per = ATT_HG // hp
    scale = ATT_DH ** -0.5
    wide = lambda t: jnp.concatenate([t, t], axis=1)

    def body(q_ref, k_ref, v_ref, bias_ref, o_ref, l_ref, do_ref, dl_ref, dq_ref, dk_ref, dv_ref, ds_ref):
        r = pl.program_id(1)

        @pl.when(r == 0)
        def _():
            ds_ref[...] = jnp.zeros_like(ds_ref)

        for h in range(hp):
            carry_k = carry_v = None
            for n in range(nb):
                cur = _blk(dil, r, n)
                q = q_ref[h, cur, :]
                dov = do_ref[h, cur, :]
                delta = jnp.sum(dov * o_ref[h, cur, :], axis=-1, keepdims=True)
                if n == 0:
                    own = pl.ds(ATT_BLK, ATT_BLK)
                    kk, vv = k_ref[h, cur, :], v_ref[h, cur, :]
                    s = _dot(q, kk, NT) * scale + bias_ref[h, :, own]
                    p = jnp.where(_att_valid_first(), jnp.exp(s - l_ref[h, cur, :]), 0.0)
                    ds = p * (_dot(dov, vv, NT) - delta + dl_ref[h, cur, :])
                    ds_ref[h, :, own] += ds
                    dq_ref[h, cur, :] = _dot(ds, kk, NN) * scale
                    carry_k, carry_v = _dot(ds, q, TN) * scale, _dot(p, dov, TN)
                    continue
                prev = _blk(dil, r, n - 1)
                kk = jnp.concatenate([k_ref[h, prev, :], k_ref[h, cur, :]], axis=0)
                vv = jnp.concatenate([v_ref[h, prev, :], v_ref[h, cur, :]], axis=0)
                s = _dot(q, kk, NT) * scale + bias_ref[h]
                p = jnp.where(_att_valid(n), jnp.exp(s - wide(l_ref[h, cur, :])), 0.0)
                dp = _dot(dov, vv, NT)
                ds = p * (dp - delta + wide(dl_ref[h, cur, :]))
                ds_ref[h] += ds
                dq_ref[h, cur, :] = _dot(ds, kk, NN) * scale
                dkk = _dot(ds, q, TN) * scale
                dvv = _dot(p, dov, TN)
                dk_ref[h, prev, :] = carry_k + dkk[:ATT_BLK]
                dv_ref[h, prev, :] = carry_v + dvv[:ATT_BLK]
                carry_k, carry_v = dkk[ATT_BLK:], dvv[ATT_BLK:]
            last = _blk(dil, r, nb - 1)
            dk_ref[h, last, :] = carry_k
            dv_ref[h, last, :] = carry_v

    osh = jax.ShapeDtypeStruct((ATT_HG, S, ATT_DH), F32)
    kw = dict(name=f"att_bwd{gi}", grid=(per, dil), in_specs=_slab_specs(gi) + [_bias_spec(gi)] + _head_specs(gi, 4),
              out_specs=(*_head_specs(gi, 3), pl.BlockSpec((hp, ATT_BLK, 2 * ATT_BLK), lambda g, r: (g, 0, 0))),
              out_shape=(osh, osh, osh, jax.ShapeDtypeStruct((ATT_HG, ATT_BLK, 2 * ATT_BLK), F32)))
    args = (slabs, slabs, slabs, bias, o, lse, do, dlse)
    if comm is not None:
        return _carry(body, comm, **kw)(*args)
    return _pcall(body, compiler_params=_params(("arbitrary", "arbitrary")), **kw)(*args)


AW = ATT_HG * ATT_DH


def _mix_weights(l0, l1, l2):
    mx = jnp.maximum(jnp.maximum(l0, l1), l2)
    e0, e1, e2 = jnp.exp(l0 - mx), jnp.exp(l1 - mx), jnp.exp(l2 - mx)
    den = e0 + e1 + e2
    return e0 / den, e1 / den, e2 / den


def _heads_spec():
    return pl.BlockSpec((ATT_HG, TR, ATT_DH), lambda i: (0, i, 0))


def _mix_fwd(os_, ls, comm=None):
    def body(o0, o1, o2, l0, l1, l2, att_ref):
        for h in range(ATT_HG):
            w0, w1, w2 = _mix_weights(l0[h], l1[h], l2[h])
            att_ref[:, h * ATT_DH:(h + 1) * ATT_DH] = (w0 * o0[h] + w1 * o1[h] + w2 * o2[h]).astype(BF16)

    kw = dict(name="mix_fwd", grid=(S // TR,), in_specs=[_heads_spec()] * 6, out_specs=_row_spec(AW),
              out_shape=jax.ShapeDtypeStruct((S, AW), BF16))
    if comm is not None:
        return _carry(body, comm, **kw)(*os_, *ls)
    return _pcall(body, compiler_params=_params(("parallel",)), **kw)(*os_, *ls)


def _mix_bwd(os_, ls, datt):
    def body(o0, o1, o2, l0, l1, l2, da_ref, d0, d1, d2, e0, e1, e2):
        for h in range(ATT_HG):
            ws = _mix_weights(l0[h], l1[h], l2[h])
            da = da_ref[:, h * ATT_DH:(h + 1) * ATT_DH]
            dws = []
            for o_ref, w, d_ref in zip((o0, o1, o2), ws, (d0, d1, d2)):
                d_ref[h] = w * da
                dws.append(jnp.broadcast_to(jnp.sum(da * o_ref[h], axis=-1, keepdims=True), (TR, ATT_DH)))
            tot = ws[0] * dws[0] + ws[1] * dws[1] + ws[2] * dws[2]
            for w, dw, e_ref in zip(ws, dws, (e0, e1, e2)):
                e_ref[h] = w * (dw - tot)

    o = pltpu.HBM((ATT_HG, S, ATT_DH), F32)
    return _pcall(body, name="mix_bwd", grid=(S // TR,), in_specs=[_heads_spec()] * 6 + [_row_spec(AW)],
                  out_specs=(_heads_spec(),) * 6, out_shape=(o,) * 6,
                  compiler_params=_params(("parallel",)))(*os_, *ls, datt)


def _ada_fwd(c_all, w_sh, b_sl):
    def body(c_ref, w_ref, b_ref, o_ref):
        cv = c_ref[...]
        o_ref[...] = _dot(cv * jax.nn.sigmoid(cv), w_ref[...], NN) + b_ref[...]

    return _pcall(body, name="ada_fwd", out_shape=jax.ShapeDtypeStruct((N_DEV, w_sh.shape[1]), F32),
                  compiler_params=_params())(c_all, w_sh, b_sl)


CAST_STEPS = 4


def _to_bf16(arrs, comm, name):
    n = len(arrs)

    def body(*refs):
        for src, dst in zip(refs[:n], refs[n:]):
            dst[...] = src[...].astype(BF16)

    blocks = [pl.BlockSpec((a.shape[0] // CAST_STEPS, a.shape[1]), lambda i: (i, 0)) for a in arrs]
    return _carry(body, comm, name=name, grid=(CAST_STEPS,), in_specs=blocks, out_specs=tuple(blocks),
                  out_shape=tuple(pltpu.HBM(a.shape, BF16) for a in arrs))(*arrs)


def _ada_bwd(c_all, dm_sl):
    def body(c_ref, d_ref, o_ref):
        cv = c_ref[...]
        o_ref[...] = _dot(cv * jax.nn.sigmoid(cv), d_ref[...], TN)

    return _pcall(body, name="ada_bwd", out_shape=jax.ShapeDtypeStruct((D, dm_sl.shape[1]), F32),
                  compiler_params=_params())(c_all, dm_sl)


N_MOD = 6


def _sum_small(gathered, widths):
    def body(g_ref, gb_ref, dm_ref, *outs):
        def total(cols):
            acc = g_ref[0, :, cols]
            for e in range(1, N_DEV):
                acc = acc + g_ref[e, :, cols]
            return acc

        gb_ref[...] = total(slice(0, N_MOD * D))
        for e in range(N_DEV):
            dm_ref[e:e + 1, :] = g_ref[e, :, :N_MOD * D]
        col = N_MOD * D
        for w, o_ref in zip(widths, outs):
            o_ref[...] = total(slice(col, col + w))
            col += w

    assert gathered.shape == (N_DEV, 1, N_MOD * D + sum(widths))
    shapes = (jax.ShapeDtypeStruct((1, N_MOD * D), F32), jax.ShapeDtypeStruct((N_DEV, N_MOD * D), F32),
              *[jax.ShapeDtypeStruct((1, w), F32) for w in widths])
    res = _pcall(body, name="sum_small", out_shape=shapes, compiler_params=_params())(gathered)
    return res[0], res[1], res[2:]


def _row_tile(m, n):
    t = max(8, min(m, (1 << 19) // n // 8 * 8))
    while m % t:
        t -= 8
    return t


def _pair_sum(full, recv, sel, name, col_block=0):
    _, m, n = recv.shape
    t = _row_tile(m, n)

    def body(sel_ref, a_ref, b_ref, o_ref):
        o_ref[...] = (a_ref[...].astype(F32) + b_ref[...].astype(F32)).astype(o_ref.dtype)

    gs = pltpu.PrefetchScalarGridSpec(
        num_scalar_prefetch=1, grid=(4, m // t),
        in_specs=[pl.BlockSpec((None, None, t, n), lambda q, i, s: (q, s[0], i, col_block)),
                  pl.BlockSpec((None, t, n), lambda q, i, s: (q, i, 0))],
        out_specs=pl.BlockSpec((None, t, n), lambda q, i, s: (q, i, 0)))
    return _pcall(body, name=name, grid_spec=gs, out_shape=pltpu.HBM((4, m, n), full.dtype),
                  compiler_params=_params(("parallel", "parallel")))(sel, full, recv)


def _chip_sum(part, recv, sel, name):
    _, m, n = part.shape
    t = _row_tile(m, n)

    def body(sel_ref, a_ref, r_ref, o_ref):
        o_ref[...] = ((a_ref[...].astype(F32) + r_ref[0].astype(F32)) + r_ref[1].astype(F32)) + r_ref[2].astype(F32)

    gs = pltpu.PrefetchScalarGridSpec(
        num_scalar_prefetch=1, grid=(m // t,),
        in_specs=[pl.BlockSpec((None, t, n), lambda i, s: (s[0], i, 0)),
                  pl.BlockSpec((3, t, n), lambda i, s: (0, i, 0))],
        out_specs=pl.BlockSpec((t, n), lambda i, s: (i, 0)))
    return _pcall(body, name=name, grid_spec=gs, out_shape=jax.ShapeDtypeStruct((m, n), F32),
                  compiler_params=_params(("parallel",)))(sel, part, recv)


def _adamw_math(w, g, m, v):
    nm = ADAM_B1 * m + (1.0 - ADAM_B1) * g
    nv = ADAM_B2 * v + (1.0 - ADAM_B2) * (g * g)
    m_hat = nm / (1.0 - ADAM_B1 ** ADAM_STEP)
    v_hat = nv / (1.0 - ADAM_B2 ** ADAM_STEP)
    return -ADAM_LR * (m_hat / (jnp.sqrt(v_hat) + ADAM_EPS) + ADAM_WD * w), nm, nv


def _adamw(w, g, m, v, name):
    _, rows, cols = w.shape
    t = _row_tile(rows, cols)

    def body(w_ref, g_ref, m_ref, v_ref, d_ref, nm_ref, nv_ref):
        d_ref[...], nm_ref[...], nv_ref[...] = _adamw_math(w_ref[...], g_ref[...], m_ref[...], v_ref[...])

    spec3 = pl.BlockSpec((None, t, cols), lambda i: (0, i, 0))
    spec2 = pl.BlockSpec((t, cols), lambda i: (i, 0))
    o = jax.ShapeDtypeStruct(w.shape, F32)
    return _pcall(body, name=name, grid=(rows // t,), in_specs=[spec3, spec2, spec3, spec3], out_specs=(spec3,) * 3,
                  out_shape=(o, o, o), compiler_params=_params(("parallel",)))(w, g, m, v)


def _adamw_reduced1(w, m, v, part, recv, sel, name):
    _, rows, cols = w.shape
    t = _row_tile(rows, cols)

    def body(sel_ref, w_ref, m_ref, v_ref, p_ref, r_ref, g_ref, d_ref, nm_ref, nv_ref):
        g = ((p_ref[...].astype(F32) + r_ref[0].astype(F32)) + r_ref[1].astype(F32)) + r_ref[2].astype(F32)
        g_ref[...] = g
        d_ref[...], nm_ref[...], nv_ref[...] = _adamw_math(w_ref[...], g, m_ref[...], v_ref[...])

    wspec = pl.BlockSpec((None, t, cols), lambda i, s: (0, i, 0))
    gs = pltpu.PrefetchScalarGridSpec(
        num_scalar_prefetch=1, grid=(rows // t,),
        in_specs=[wspec, wspec, wspec, pl.BlockSpec((None, t, cols), lambda i, s: (s[0], i, 0)),
                  pl.BlockSpec((3, t, cols), lambda i, s: (0, i, 0))],
        out_specs=(wspec,) * 4)
    o = jax.ShapeDtypeStruct(w.shape, F32)
    return _pcall(body, name=name, grid_spec=gs, out_shape=(o, o, o, o),
                  compiler_params=_params(("parallel",)))(sel, w, m, v, part, recv)


def _adamw_reduced(w, m, v, parts, recvs, sel):
    _, rows, cols = w.shape
    half = cols // 2
    t = _row_tile(rows, half)

    def body(sel_ref, w_ref, m_ref, v_ref, pa_ref, pb_ref, ra_ref, rb_ref, g_ref, d_ref, nm_ref, nv_ref):
        total = lambda p_ref, r_ref: ((p_ref[...].astype(F32) + r_ref[0].astype(F32)) + r_ref[1].astype(F32)) \
            + r_ref[2].astype(F32)
        g = jnp.where(pl.program_id(1) == 0, total(pa_ref, ra_ref), total(pb_ref, rb_ref))
        g_ref[...] = g
        d_ref[...], nm_ref[...], nv_ref[...] = _adamw_math(w_ref[...], g, m_ref[...], v_ref[...])

    wspec = pl.BlockSpec((None, t, half), lambda i, j, s: (0, i, j))
    pspec = pl.BlockSpec((None, t, half), lambda i, j, s: (s[0], i, 0))
    rspec = pl.BlockSpec((3, t, half), lambda i, j, s: (0, i, 0))
    gs = pltpu.PrefetchScalarGridSpec(num_scalar_prefetch=1, grid=(rows // t, 2),
                                      in_specs=[wspec, wspec, wspec, pspec, pspec, rspec, rspec],
                                      out_specs=(wspec,) * 4)
    o = jax.ShapeDtypeStruct(w.shape, F32)
    return _pcall(body, name="adamw_w_in", grid_spec=gs, out_shape=(o, o, o, o),
                  compiler_params=_params(("parallel", "arbitrary")))(sel, w, m, v, *parts, *recvs)


def _adamw_small(ws, gs, ms, vs):
    n = len(ws)

    def body(*refs):
        for i in range(n):
            w_ref, g_ref, m_ref, v_ref = (refs[k * n + i] for k in range(4))
            d, nm, nv = _adamw_math(w_ref[...], g_ref[...], m_ref[...], v_ref[...])
            refs[4 * n + i][...] = d
            refs[5 * n + i][...] = nm
            refs[6 * n + i][...] = nv

    shapes = tuple(jax.ShapeDtypeStruct(w.shape, F32) for w in ws)
    res = _pcall(body, name="adamw_small", out_shape=shapes * 3, compiler_params=_params())(*ws, *gs, *ms, *vs)
    return res[:n], res[n:2 * n], res[2 * n:]


def _mesh_pos():
    return lax.axis_index("x"), lax.axis_index("y"), lax.axis_index("c")


class _Gather:
    def __init__(self, arrs):
        self.ins = list(arrs)
        self.out_shape = tuple(jax.ShapeDtypeStruct((N_DEV,) + a.shape, a.dtype) for a in arrs)
        n = len(arrs)
        self.sems = [pltpu.SemaphoreType.DMA((7 * n,)), pltpu.SemaphoreType.DMA((7 * n,)),
                     pltpu.SemaphoreType.DMA((n,))]

    def _copies(self, ins, outs, sems):
        send_sems, recv_sems, local_sems = sems
        x, y, c = _mesh_pos()
        me, sibling = (x, y, c), (x, y, 1 - c)
        chips = [(1 - x, y), (x, 1 - y), (1 - x, 1 - y)]

        def copy(p, k, block, to, from_input=False):
            dst = outs[p].at[_slot(block)]
            return pltpu.make_async_remote_copy(
                src_ref=ins[p] if from_input else dst, dst_ref=dst, send_sem=send_sems.at[7 * p + k],
                recv_sem=recv_sems.at[7 * p + k], device_id=to, device_id_type=MESH)

        npc = len(self.ins)
        mine = [pltpu.make_async_copy(ins[p], outs[p].at[_slot(me)], local_sems.at[p]) for p in range(npc)]
        first = []
        for p in range(npc):
            first.append(copy(p, 0, me, sibling, from_input=True))
            first += [copy(p, 1 + j, me, (*chip, c), from_input=True) for j, chip in enumerate(chips)]
        return me, sibling, chips, c, copy, mine, first

    def start(self, ins, outs, sems):
        *_, mine, first = self._copies(ins, outs, sems)
        for cp in mine + first:
            cp.start()

    def finish(self, ins, outs, sems):
        me, sibling, chips, c, copy, mine, first = self._copies(ins, outs, sems)
        npc = len(self.ins)
        passed = []
        for p in range(npc):
            for j, chip in enumerate(chips):
                copy(p, 1 + j, (*chip, c), me).wait_recv()
                passed.append(copy(p, 4 + j, (*chip, c), sibling))
                passed[-1].start()
        for p in range(npc):
            copy(p, 0, sibling, me).wait_recv()
            for j, chip in enumerate(chips):
                copy(p, 4 + j, (*chip, 1 - c), me).wait_recv()
        for cp in first + passed:
            cp.wait_send()
        for cp in mine:
            cp.wait()


class _GatherSmallDirect:
    ins_in_vmem = True

    def __init__(self, arrs):
        self.ins = list(arrs)
        self.out_shape = tuple(jax.ShapeDtypeStruct((N_DEV,) + a.shape, a.dtype) for a in arrs)
        n = len(arrs)
        self.sems = [pltpu.SemaphoreType.DMA((7 * n,)), pltpu.SemaphoreType.DMA((7 * n,)),
                     pltpu.SemaphoreType.DMA((n,))]

    def _copies(self, ins, outs, sems):
        send_sems, recv_sems, local_sems = sems
        x, y, c = _mesh_pos()
        me = _slot((x, y, c))
        mine = [pltpu.make_async_copy(ins[p], outs[p].at[me], local_sems.at[p]) for p in range(len(self.ins))]
        sends, arrivals = [], []
        for p in range(len(self.ins)):
            for j in range(1, N_DEV):
                peer = (me + j) % N_DEV
                to = (peer // 4, (peer // 2) % 2, peer % 2)
                sends.append(pltpu.make_async_remote_copy(
                    src_ref=ins[p], dst_ref=outs[p].at[me], send_sem=send_sems.at[7 * p + j - 1],
                    recv_sem=recv_sems.at[7 * p + (N_DEV - j) - 1], device_id=to, device_id_type=MESH))
                arrivals.append(pltpu.make_async_remote_copy(
                    src_ref=ins[p], dst_ref=outs[p].at[peer], send_sem=send_sems.at[7 * p + j - 1],
                    recv_sem=recv_sems.at[7 * p + j - 1], device_id=to, device_id_type=MESH))
        return mine, sends, arrivals

    def start(self, ins, outs, sems):
        mine, sends, _ = self._copies(ins, outs, sems)
        for cp in mine + sends:
            cp.start()

    def finish(self, ins, outs, sems):
        mine, sends, arrivals = self._copies(ins, outs, sems)
        for cp in arrivals:
            cp.wait_recv()
        for cp in sends:
            cp.wait_send()
        for cp in mine:
            cp.wait()


class _ExchangeCore:
    def __init__(self, fulls, cols=None):
        self.ins = list(fulls)
        self.cols = cols
        width = lambda f: f.shape[3] if cols is None else cols[1]
        self.out_shape = tuple(jax.ShapeDtypeStruct((4, f.shape[2], width(f)), f.dtype) for f in fulls)
        self.sems = [pltpu.SemaphoreType.DMA((4 * len(fulls),)), pltpu.SemaphoreType.DMA((4 * len(fulls),))]

    def _copies(self, ins, outs, sems):
        send_sems, recv_sems = sems
        x, y, c = _mesh_pos()

        def src(a, q):
            ref = ins[a].at[q, 1 - c]
            return ref if self.cols is None else ref.at[:, pl.ds(*self.cols)]

        return [pltpu.make_async_remote_copy(
            src_ref=src(a, q), dst_ref=outs[a].at[q], send_sem=send_sems.at[4 * a + q],
            recv_sem=recv_sems.at[4 * a + q], device_id=(x, y, 1 - c), device_id_type=MESH)
            for a in range(len(self.ins)) for q in range(4)]

    def start(self, ins, outs, sems):
        for cp in self._copies(ins, outs, sems):
            cp.start()

    def finish(self, ins, outs, sems):
        for cp in self._copies(ins, outs, sems):
            cp.wait()


class _ExchangeChip:
    def __init__(self, parts):
        self.ins = list(parts)
        self.out_shape = tuple(jax.ShapeDtypeStruct((3,) + p.shape[1:], p.dtype) for p in parts)
        self.sems = [pltpu.SemaphoreType.DMA((3 * len(parts),)), pltpu.SemaphoreType.DMA((3 * len(parts),))]

    def _copies(self, ins, outs, sems):
        send_sems, recv_sems = sems
        x, y, c = _mesh_pos()
        chips = [(1 - x, y), (x, 1 - y), (1 - x, 1 - y)]
        return [pltpu.make_async_remote_copy(
            src_ref=ins[a].at[2 * px + py], dst_ref=outs[a].at[j], send_sem=send_sems.at[3 * a + j],
            recv_sem=recv_sems.at[3 * a + j], device_id=(px, py, c), device_id_type=MESH)
            for a in range(len(self.ins)) for j, (px, py) in enumerate(chips)]

    def start(self, ins, outs, sems):
        for cp in self._copies(ins, outs, sems):
            cp.start()

    def finish(self, ins, outs, sems):
        for cp in self._copies(ins, outs, sems):
            cp.wait()


HBM_ONLY = pl.BlockSpec(memory_space=pltpu.HBM)
SEM_SPEC = pl.BlockSpec(memory_space=pltpu.SEMAPHORE)
SIDE_EFFECT = pltpu.SideEffectType.DATAFLOW_SIDE_EFFECTING


def _chip_copies(p_refs, land_refs, send_sems, recv_sems):
    x, y, c = _mesh_pos()
    return [pltpu.make_async_remote_copy(
        src_ref=p_refs[a].at[2 * px + py], dst_ref=land_refs[a].at[j], send_sem=send_sems.at[3 * a + j],
        recv_sem=recv_sems.at[3 * a + j], device_id=(px, py, c), device_id_type=MESH)
        for a in range(len(p_refs)) for j, (px, py) in enumerate([(1 - x, y), (x, 1 - y), (1 - x, 1 - y)])]


def _chip_exchange_start(parts, name):
    n = len(parts)
    lands = [lax.empty((3,) + p.shape[1:], p.dtype) for p in parts]

    def body(*refs):
        p_refs, land_refs, (send_sems, recv_sems) = refs[:n], refs[n:2 * n], refs[2 * n:2 * n + 2]
        for cp in _chip_copies(p_refs, land_refs, send_sems, recv_sems):
            cp.start()
        token = refs[-1]
        token[...] = jnp.zeros_like(token)

    hbm = lambda t: pltpu.HBM(t.shape, t.dtype)
    res = pl.pallas_call(
        body, name=name,
        out_shape=(pltpu.SemaphoreType.DMA((3 * n,)), pltpu.SemaphoreType.DMA((3 * n,)), *[hbm(t) for t in parts + lands],
                   jax.ShapeDtypeStruct((8, 128), F32)),
        in_specs=(HBM_ONLY,) * (2 * n),
        out_specs=(SEM_SPEC, SEM_SPEC, *[HBM_ONLY] * (2 * n), pl.BlockSpec(memory_space=pltpu.VMEM)),
        input_output_aliases={i: 2 + i for i in range(2 * n)},
        compiler_params=pltpu.CompilerParams(has_side_effects=SIDE_EFFECT))(
        *[pltpu.with_memory_space_constraint(t, pltpu.HBM) for t in parts + lands])
    return (res[0], res[1], list(res[2:2 + n]), list(res[2 + n:2 + 2 * n])), res[-1]


def _chip_exchange_wait(in_flight, after, name):
    send_sems, recv_sems, parts, lands = in_flight
    n = len(parts)

    def body(*refs):
        p_refs, land_refs, (send_sems, recv_sems) = refs[:n], refs[n:2 * n], refs[2 * n:2 * n + 2]
        for cp in _chip_copies(p_refs, land_refs, send_sems, recv_sems):
            cp.wait_send()
            cp.wait_recv()

    res = pl.pallas_call(
        body, name=name, out_shape=tuple(pltpu.HBM(t.shape, t.dtype) for t in parts + lands),
        in_specs=(*[HBM_ONLY] * (2 * n), SEM_SPEC, SEM_SPEC, pl.BlockSpec(memory_space=pl.ANY)),
        out_specs=(HBM_ONLY,) * (2 * n), input_output_aliases={i: i for i in range(2 * n)},
        compiler_params=pltpu.CompilerParams(has_side_effects=SIDE_EFFECT))(*parts, *lands, send_sems, recv_sems, after)
    return list(res[:n]), list(res[n:])


def _slot(p):
    return 4 * p[0] + 2 * p[1] + p[2]


def _gather_copies(src_refs, out_refs, send_sems, recv_sems):
    x, y, c = _mesh_pos()
    targets = [(x, y, 1 - c), (1 - x, y, c), (x, 1 - y, c), (1 - x, 1 - y, c)]
    return [pltpu.make_async_remote_copy(
        src_ref=src_refs[a], dst_ref=out_refs[a].at[_slot((x, y, c))], send_sem=send_sems.at[4 * a + k],
        recv_sem=recv_sems.at[4 * a + k], device_id=to, device_id_type=MESH)
        for a in range(len(src_refs)) for k, to in enumerate(targets)]


def _gather_start(shards, after, name):
    n = len(shards)
    outs = [lax.empty((N_DEV,) + s.shape, s.dtype) for s in shards]

    def body(*refs):
        for cp in _gather_copies(refs[:n], refs[n:2 * n], refs[2 * n + 1], refs[2 * n + 2]):
            cp.start()
        token = refs[-1]
        token[...] = jnp.zeros_like(token)

    res = pl.pallas_call(
        body, name=name,
        out_shape=(pltpu.SemaphoreType.DMA((4 * n,)), pltpu.SemaphoreType.DMA((4 * n,)),
                   *[pltpu.HBM(t.shape, t.dtype) for t in shards + outs], jax.ShapeDtypeStruct((8, 128), F32)),
        in_specs=(*[HBM_ONLY] * (2 * n), pl.BlockSpec(memory_space=pl.ANY)),
        out_specs=(SEM_SPEC, SEM_SPEC, *[HBM_ONLY] * (2 * n), pl.BlockSpec(memory_space=pltpu.VMEM)),
        input_output_aliases={i: 2 + i for i in range(2 * n)},
        compiler_params=pltpu.CompilerParams(has_side_effects=SIDE_EFFECT))(
        *[pltpu.with_memory_space_constraint(t, pltpu.HBM) for t in shards + outs], after)
    return (res[0], res[1], list(res[2:2 + n]), list(res[2 + n:2 + 2 * n])), res[-1]


def _gather_wait(in_flight, after, name):
    send_sems, recv_sems, shards, outs = in_flight
    n = len(shards)

    def body(*refs):
        for cp in _gather_copies(refs[:n], refs[n:2 * n], refs[2 * n], refs[2 * n + 1]):
            cp.wait_send()
            cp.wait_recv()

    res = pl.pallas_call(
        body, name=name, out_shape=tuple(pltpu.HBM(t.shape, t.dtype) for t in shards + outs),
        in_specs=(*[HBM_ONLY] * (2 * n), SEM_SPEC, SEM_SPEC, pl.BlockSpec(memory_space=pl.ANY)),
        out_specs=(HBM_ONLY,) * (2 * n), input_output_aliases={i: i for i in range(2 * n)},
        compiler_params=pltpu.CompilerParams(has_side_effects=SIDE_EFFECT))(*shards, *outs, send_sems, recv_sems, after)
    return list(res[:n]), list(res[n:])


class _PassToSibling:
    def __init__(self, shards, gathered):
        n = self.n = len(shards)
        self.ins = list(shards) + list(gathered)
        self.out_shape = tuple(jax.ShapeDtypeStruct(g.shape, g.dtype) for g in gathered)
        self.aliases = {n + a: a for a in range(n)}
        self.sems = [pltpu.SemaphoreType.DMA((3 * n,)), pltpu.SemaphoreType.DMA((3 * n,)),
                     pltpu.SemaphoreType.DMA((n,))]

    def _copies(self, ins, outs, sems):
        send_sems, recv_sems, local_sems = sems
        x, y, c = _mesh_pos()
        chips = [(1 - x, y), (x, 1 - y), (1 - x, 1 - y)]
        mine = [pltpu.make_async_copy(ins[a], outs[a].at[_slot((x, y, c))], local_sems.at[a]) for a in range(self.n)]
        passed, awaited = [], []
        for a in range(self.n):
            for j, chip in enumerate(chips):
                sems_j = dict(send_sem=send_sems.at[3 * a + j], recv_sem=recv_sems.at[3 * a + j],
                              device_id=(x, y, 1 - c), device_id_type=MESH)
                blk = outs[a].at[_slot((*chip, c))]
                passed.append(pltpu.make_async_remote_copy(src_ref=blk, dst_ref=blk, **sems_j))
                got = outs[a].at[_slot((*chip, 1 - c))]
                awaited.append(pltpu.make_async_remote_copy(src_ref=got, dst_ref=got, **sems_j))
        return mine, passed, awaited

    def start(self, ins, outs, sems):
        mine, passed, _ = self._copies(ins, outs, sems)
        for cp in mine + passed:
            cp.start()

    def finish(self, ins, outs, sems):
        mine, passed, awaited = self._copies(ins, outs, sems)
        for cp in passed:
            cp.wait_send()
        for cp in awaited:
            cp.wait_recv()
        for cp in mine:
            cp.wait()


def _reduce_sums(fulls, recv_core, core, tag):
    return [_pair_sum(f, r, core, f"rs_pair_{tag}{i}") for i, (f, r) in enumerate(zip(fulls, recv_core))]


def _local_step(x, tgt, mods, w_in_shard, order, shards, small, chip, core):
    sh1, sc1, g1, sh2, sc2, g2 = mods
    norm1_g, rel_bias, gn_g, gn_b, norm2_g, norm_f_g = small
    tables = _ret_tables()
    buckets = jnp.asarray(_bucket_tables())

    proj, slabs, w_in_t, h1 = _gather_proj(x, norm1_g, sh1, sc1, w_in_shard, order)
    flight_w1, token_w = _gather_start(list(shards[:3]), proj, "gather_w1_start")
    flight_w2, token_w = _gather_start(list(shards[3:]), token_w, "gather_w2_start")
    bias = _bias_build(rel_bias, buckets, token_w)
    outs, lses = [], []
    for gi in range(len(ATT_GROUPS)):
        o, l = _att_fwd(slabs, bias, gi)
        outs.append(o)
        lses.append(l)
    att, gathered = _mix_fwd(outs, lses, comm=_PassToSibling(*_gather_wait(flight_w1, lses[2], "gather_w1_wait")))
    w_ret_out, w_att_out, w_o = (_from_slots(g, ax) for g, ax in zip(gathered, BIG_AXES[1:4]))
    gated, ro, states = _ret_fwd(proj, tables, gn_g, gn_b, att)
    ret_out, gathered = _mm(gated, w_ret_out, 'nn', tm=S, tn=256, tk=2048, name="ret_out",
                            comm=_PassToSibling(*_gather_wait(flight_w2, gated, "gather_w2_wait")))
    w_ff1, w_ff2 = (_from_slots(g, ax) for g, ax in zip(gathered, BIG_AXES[4:]))
    att_out, merged = _att_out_merge(att, w_att_out, proj, ret_out)
    mixo, x1, h2 = _w_o_norm2(merged, w_o, x, g1, norm2_g, sh2, sc2)
    u, act = _mm(h2, w_ff1, 'nn', tm=S, tn=512, tk=D, name="ff1", relu2=True)
    loss, dx2, g_normf, df, dg2 = _ff2_final(act, w_ff2, x1, g2, tgt, norm_f_g)

    gw_ff2 = _mm(act, df, 'tn', tm=512, tn=D, tk=S, name="gw_ff2", out_dtype=BF16)
    du = _mm(df, w_ff2, 'nt', tm=S, tn=512, tk=D, name="d_act", out_dtype=BF16, relu2_of=u)
    gw_ff1 = _mm(h2, du, 'tn', tm=D, tn=512, tk=S, name="gw_ff1", out_dtype=BF16)
    fulls_a = [_to_slots(g, ax) for g, ax in zip((gw_ff1, gw_ff2), BIG_AXES[4:])]
    dh2, recv_core_a = _mm(du, w_ff1, 'nt', tm=1024, tn=1024, tk=2048, name="dh2", comm=_ExchangeCore(fulls_a))
    parts_a = _reduce_sums(fulls_a, recv_core_a, core, "a")
    flight_a, token_a = _chip_exchange_start(parts_a, "rs_a_start")
    dx1, dsc2, dsh2, g_norm2, dmixo, dg1 = _norm_mod_bwd(x1, norm2_g, sc2, dh2, dx2, "norm2_bwd", gate=(mixo, g1))

    gw_o = _mm(merged, dmixo, 'tn', tm=D, tn=512, tk=S, name="gw_o", out_dtype=BF16, after=token_a)
    d_ret_out, d_att_out, dga, dgb = _dmerged_split(dmixo, w_o, proj, ret_out, att_out)
    gw_ret_out = _mm(gated, d_ret_out, 'tn', tm=512, tn=D, tk=S, name="gw_ret_out", out_dtype=BF16)
    gw_att_out = _mm(att, d_att_out, 'tn', tm=AW, tn=D, tk=S, name="gw_att_out", out_dtype=BF16)
    fulls_b = [_to_slots(g, ax) for g, ax in zip((gw_ret_out, gw_att_out, gw_o), BIG_AXES[1:4])]
    dgated, recv_core_b = _mm(d_ret_out, w_ret_out, 'nt', tm=S, tn=512, tk=D, name="dgated",
                              comm=_ExchangeCore(fulls_b))
    parts_b = _reduce_sums(fulls_b, recv_core_b, core, "b")
    flight_b, token_b = _chip_exchange_start(parts_b, "rs_b_start")
    datt = _mm(d_att_out, w_att_out, 'nt', tm=S, tn=AW, tk=D, name="datt", after=token_b)
    mix_grads = _mix_bwd(outs, lses, datt)
    datt_parts, ds_sums = [], []
    for gi in range(len(ATT_GROUPS)):
        dq, dk, dv, ds_sum = _att_bwd(slabs, bias, outs[gi], lses[gi], mix_grads[gi], mix_grads[3 + gi], gi)
        datt_parts += [dq, dk, dv]
        ds_sums.append(ds_sum)
    g_bias = _bias_grad(jnp.concatenate(ds_sums, axis=0), buckets)[:, :, 0].T.reshape(1, -1)
    dproj, g_gn_g, g_gn_b = _ret_bwd(proj, tables, gn_g, gn_b, ro, states, dgated, datt_parts + [dga, dgb])
    parts_a, recv_chip_a = _chip_exchange_wait(flight_a, dproj, "rs_a_wait")
    parts_b, recv_chip_b = _chip_exchange_wait(flight_b, dproj, "rs_b_wait")
    reduced = list(zip(parts_b + parts_a, recv_chip_b + recv_chip_a))
    full_in = _to_slots(_mm(dproj, h1, 'tn', tm=512, tn=D, tk=S, name="gw_in", out_dtype=BF16), 0)
    halves = [(half * (D // 2), D // 2) for half in range(2)]
    (recv_core_in0,) = _run_comm(_ExchangeCore([full_in], cols=halves[0]), "rs_core_in0")
    flight0, token = _chip_exchange_start([_pair_sum(full_in, recv_core_in0, core, "rs_pair_c0", col_block=0)],
                                          "rs_in0_start")
    dh1, (recv_core_in1,) = _mm(dproj, w_in_t, 'nn', tm=1024, tn=1024, tk=2560, name="dh1",
                                comm=_ExchangeCore([full_in], cols=halves[1]), after=token)
    flight1, token = _chip_exchange_start([_pair_sum(full_in, recv_core_in1, core, "rs_pair_c1", col_block=1)],
                                          "rs_in1_start")
    in_flight = [flight0, flight1]
    gx, dsc1, dsh1, g_norm1 = _norm_mod_bwd(x, norm1_g, sc1, dh1, dx1, "norm1_bwd", after=token)

    dmod = [dsh1, dsc1, dg1, dsh2, dsc2, dg2]
    small_g = [g_norm1, g_bias, g_gn_g, g_gn_b, g_norm2, g_normf]
    return loss, gx, in_flight, reduced, small_g, dmod


def _to_slots(g, axis):
    if axis == 0:
        return g.reshape(4, 2, g.shape[0] // N_DEV, g.shape[1])
    return g.reshape(g.shape[0], N_DEV, g.shape[1] // N_DEV).transpose(1, 0, 2).reshape(4, 2, g.shape[0], -1)


def _from_slots(w8, axis):
    if axis == 0:
        return w8.reshape(-1, w8.shape[2])
    return w8.transpose(1, 0, 2).reshape(w8.shape[1], -1)


BIG_AXES = (1, 0, 1, 0, 1, 0)


def kernel(x, c, w_ada, b_ada, norm1_g, w_in, rel_bias, ret_gn_g, ret_gn_b, w_ret_out, w_att_out, w_o, norm2_g, w_ff1, w_ff2, norm_f_g, loss_target, m_w_ada, m_b_ada, m_norm1_g, m_w_in, m_rel_bias, m_ret_gn_g, m_ret_gn_b, m_w_ret_out, m_w_att_out, m_w_o, m_norm2_g, m_w_ff1, m_w_ff2, m_norm_f_g, v_w_ada, v_b_ada, v_norm1_g, v_w_in, v_rel_bias, v_ret_gn_g, v_ret_gn_b, v_w_ret_out, v_w_att_out, v_w_o, v_norm2_g, v_w_ff1, v_w_ff2, v_norm_f_g):
    mx, my, mc = _mesh_pos()
    dev = 4 * mx + 2 * my + mc
    chip = jnp.reshape(2 * mx + my, (1,)).astype(jnp.int32)
    core = jnp.reshape(mc, (1,)).astype(jnp.int32)
    ada_w = D * 6 // N_DEV

    w_in, m_w_in, v_w_in = (jnp.transpose(t, (0, 2, 1)) for t in (w_in, m_w_in, v_w_in))

    (w_in_shard,), (c_all,) = _to_bf16([w_in[0]], _Gather([c]), "gather_c")
    c_all = c_all.reshape(N_DEV, D)
    b_sl = lax.dynamic_slice(b_ada, (0, dev * ada_w), (1, ada_w))
    other_shards, (mod_all,) = _to_bf16([w[0] for w in (w_ret_out, w_att_out, w_o, w_ff1, w_ff2)],
                                        _Gather([_ada_fwd(c_all, w_ada[0], b_sl)]), "gather_mod")
    mod = lax.dynamic_index_in_dim(mod_all, dev, axis=1, keepdims=False).reshape(6, D)
    mods = tuple(mod[i:i + 1] for i in range(6))

    small = (norm1_g, rel_bias, ret_gn_g, ret_gn_b, norm2_g, norm_f_g.reshape(1, D))
    order = lax.dynamic_index_in_dim(jnp.asarray(_proj_order()), 2 * mx + my, axis=0, keepdims=False)
    loss, gx, in_flight, big_red, small_g, dmod = _local_step(x[0], loss_target[0], mods, w_in_shard, order,
                                                              list(other_shards), small, chip, core)

    names = ['w_ada', 'b_ada', 'norm1_g', 'w_in', 'rel_bias', 'ret_gn_g', 'ret_gn_b', 'w_ret_out', 'w_att_out',
             'w_o', 'norm2_g', 'w_ff1', 'w_ff2', 'norm_f_g']
    ws = dict(zip(names, (w_ada, b_ada, norm1_g, w_in, rel_bias, ret_gn_g, ret_gn_b, w_ret_out, w_att_out, w_o,
                          norm2_g, w_ff1, w_ff2, norm_f_g)))
    ms = dict(zip(names, (m_w_ada, m_b_ada, m_norm1_g, m_w_in, m_rel_bias, m_ret_gn_g, m_ret_gn_b, m_w_ret_out,
                          m_w_att_out, m_w_o, m_norm2_g, m_w_ff1, m_w_ff2, m_norm_f_g)))
    vs = dict(zip(names, (v_w_ada, v_b_ada, v_norm1_g, v_w_in, v_rel_bias, v_ret_gn_g, v_ret_gn_b, v_w_ret_out,
                          v_w_att_out, v_w_o, v_norm2_g, v_w_ff1, v_w_ff2, v_norm_f_g)))
    grads, delta, new_m, new_v = {}, {}, {}, {}
    big_names = ('w_ret_out', 'w_att_out', 'w_o', 'w_ff1', 'w_ff2')
    for n, (part, recv) in zip(big_names, big_red):
        grads[n], delta[n], new_m[n], new_v[n] = _adamw_reduced1(ws[n], ms[n], vs[n], part, recv, chip, "adamw_" + n)
    updated = lax.optimization_barrier((gx, tuple(delta[n] for n in big_names)))
    rows = dmod + small_g + [loss]
    (gathered,) = _run_comm(_GatherSmallDirect([jnp.concatenate(rows, axis=1)]), "gather_small", after=updated[0])
    g_b_ada, dmod_all, (g_norm1, g_bias, g_gn_g, g_gn_b, g_norm2, g_normf, loss_sum) = _sum_small(
        gathered, [r.shape[1] for r in rows[N_MOD:]])
    loss_out = loss_sum[0, 0]
    g_w_ada = _ada_bwd(c_all, lax.dynamic_slice(dmod_all, (0, dev * ada_w), (N_DEV, ada_w)))

    grads.update(w_ada=g_w_ada.reshape(w_ada.shape), b_ada=g_b_ada, norm1_g=g_norm1, rel_bias=g_bias,
                 ret_gn_g=g_gn_g, ret_gn_b=g_gn_b, norm2_g=g_norm2, norm_f_g=g_normf)
    delta['w_ada'], new_m['w_ada'], new_v['w_ada'] = _adamw(w_ada, g_w_ada, m_w_ada, v_w_ada, "adamw_w_ada")
    small_names = ('b_ada', 'norm1_g', 'rel_bias', 'ret_gn_g', 'ret_gn_b', 'norm2_g', 'norm_f_g')
    two_d = {n: (1, ws[n].size) if ws[n].ndim == 1 else ws[n].shape for n in small_names}
    d_, m_, v_ = _adamw_small(*[[src[n].reshape(two_d[n]) for n in small_names] for src in (ws, grads, ms, vs)])
    for i, n in enumerate(small_names):
        shp = ws[n].shape
        delta[n], new_m[n], new_v[n] = d_[i].reshape(shp), m_[i].reshape(shp), v_[i].reshape(shp)
        grads[n] = grads[n].reshape(shp)

    done = lax.optimization_barrier((gx, tuple(d_), tuple(delta[n] for n in ('w_ada', 'w_ret_out', 'w_att_out', 'w_o',
                                                                               'w_ff1', 'w_ff2'))))
    parts_in, recvs_in = [], []
    for half, flight in enumerate(in_flight):
        (part_in,), (recv_chip_in,) = _chip_exchange_wait(flight, done[0], f"rs_in{half}_wait")
        parts_in.append(part_in)
        recvs_in.append(recv_chip_in)
    grads['w_in'], delta['w_in'], new_m['w_in'], new_v['w_in'] = _adamw_reduced(w_in, m_w_in, v_w_in, parts_in,
                                                                               recvs_in, chip)
    for d in (grads, delta, new_m, new_v):
        d['w_in'] = jnp.transpose(d['w_in'], (0, 2, 1))
    return (loss_out, gx[None], *[grads[n] for n in names], *[delta[n] for n in names],
            *[new_m[n] for n in names], *[new_v[n] for n in names])
```

```python
import functools
import math

import numpy as np
import jax
import jax.numpy as jnp
from jax import lax
from jax.experimental import pallas as pl
from jax.experimental.pallas import tpu as pltpu

F32 = jnp.float32
BF16 = jnp.bfloat16
MESH = pl.DeviceIdType.MESH

N_DEV = 8
S = 2048
D = 1024
RET_HEADS = 4
RET_DK = 256
RET_DV = 512
CHUNK = 128
N_CHUNK = S // CHUNK
ATT_GROUPS = ((128, 1), (512, 4), (2048, 16))
ATT_HG = 4
ATT_DH = 128
ATT_BLK = 128
N_BUCKETS = 32
MAX_DIST = 2048
D_FF = 4096
IN_COLS = 12800
OFF_RQ, OFF_RK, OFF_RV, OFF_RG, OFF_ATT = 0, 1024, 2048, 4096, 6144
OFF_GA, OFF_GB = 6144, 7168
RMS_EPS = 1e-6
GN_EPS = 1e-5
ADAM_LR, ADAM_B1, ADAM_B2, ADAM_EPS, ADAM_WD, ADAM_STEP = 0.001, 0.9, 0.999, 1e-08, 0.01, 10
VMEM_LIMIT = 48 * 1024 * 1024


def _pcall(body, **kw):
    return pl.pallas_call(body, **kw)


def _params(sem=None):
    return pltpu.CompilerParams(dimension_semantics=sem, vmem_limit_bytes=VMEM_LIMIT)


HBM_SPEC = pl.BlockSpec(memory_space=pl.ANY)


def _carry(body, comm, *, name, grid, in_specs, out_specs, out_shape, scratch_shapes=()):
    single = not isinstance(out_specs, (tuple, list))
    o_specs = (out_specs,) if single else tuple(out_specs)
    o_shape = (out_shape,) if single else tuple(out_shape)
    n_in, n_out, n_scr = len(in_specs), len(o_specs), len(scratch_shapes)
    nci, nco = len(comm.ins), len(comm.out_shape)
    total = int(np.prod(grid))

    def wrapped(*refs):
        bounds = np.cumsum([0, n_in, nci, n_out, nco, n_scr])
        a, ci, o, co, scr = (refs[bounds[i]:bounds[i + 1]] for i in range(5))
        sems = refs[bounds[5]:]
        flat = 0
        for d, g in enumerate(grid):
            flat = flat * g + pl.program_id(d)

        @pl.when(flat == 0)
        def _():
            comm.start(ci, co, sems)

        body(*a, *o, *scr)

        @pl.when(flat == total - 1)
        def _():
            comm.finish(ci, co, sems)

    aliases = {n_in + i: n_out + o for i, o in getattr(comm, "aliases", {}).items()}
    call = _pcall(wrapped, name=name, grid=grid, in_specs=list(in_specs) + [HBM_SPEC] * nci,
                  out_specs=o_specs + (HBM_SPEC,) * nco, out_shape=o_shape + tuple(comm.out_shape),
                  scratch_shapes=list(scratch_shapes) + list(comm.sems), input_output_aliases=aliases,
                  compiler_params=_params(("arbitrary",) * len(grid)))

    def run(*args):
        res = call(*args, *comm.ins)
        own = res[0] if single else tuple(res[:n_out])
        return own, tuple(res[n_out:])

    return run


def _run_comm(comm, name, after=None):
    nci, nco = len(comm.ins), len(comm.out_shape)
    extra = [] if after is None else [after]

    def body(*refs):
        ci, co, sems = refs[:nci], refs[nci + len(extra):nci + len(extra) + nco], refs[nci + len(extra) + nco:]
        comm.start(ci, co, sems)
        comm.finish(ci, co, sems)

    in_spec = pl.BlockSpec(memory_space=pltpu.VMEM) if getattr(comm, "ins_in_vmem", False) else HBM_SPEC
    return _pcall(body, name=name, in_specs=[in_spec] * nci + [HBM_SPEC] * len(extra), out_specs=(HBM_SPEC,) * nco,
                  out_shape=tuple(comm.out_shape), scratch_shapes=list(comm.sems))(*comm.ins, *extra)


def _dot(a, b, dn):
    return lax.dot_general(a.astype(BF16), b.astype(BF16), (dn, ((), ())), preferred_element_type=F32)


NN = ((1,), (0,))
NT = ((1,), (1,))
TN = ((0,), (0,))


def _mm(a, b, mode, *, tm, tn, tk, name, out_dtype=F32, res=None, gvec=None, relu2=False, relu2_of=None, comm=None,
        after=None):
    if mode == 'nn':
        (M, K), (_, N) = a.shape, b.shape
        a_spec = pl.BlockSpec((tm, tk), lambda i, j, k: (i, k))
        b_spec = pl.BlockSpec((tk, tn), lambda i, j, k: (k, j))
        dn = NN
    elif mode == 'nt':
        (M, K), (N, _) = a.shape, b.shape
        a_spec = pl.BlockSpec((tm, tk), lambda i, j, k: (i, k))
        b_spec = pl.BlockSpec((tn, tk), lambda i, j, k: (j, k))
        dn = NT
    else:
        (K, M), (_, N) = a.shape, b.shape
        a_spec = pl.BlockSpec((tk, tm), lambda i, j, k: (k, i))
        b_spec = pl.BlockSpec((tk, tn), lambda i, j, k: (k, j))
        dn = TN
    assert M % tm == 0 and N % tn == 0 and K % tk == 0, (name, M, N, K)
    nk = K // tk
    fused = res is not None
    o_spec = pl.BlockSpec((tm, tn), lambda i, j, k: (i, j))

    def body(a_ref, b_ref, *rest):
        acc_ref = rest[-1] if nk > 1 else None
        if after is not None:
            rest = rest[1:]
        if fused:
            res_ref, g_ref, o_ref, x_ref = rest[:4]
        elif relu2_of is not None:
            u_ref, o_ref = rest[:2]
        elif relu2:
            o_ref, act_ref = rest[:2]
        else:
            o_ref = rest[0]

        def finish(acc):
            if relu2_of is not None:
                acc = acc * (2.0 * jnp.maximum(u_ref[...], 0.0))
            o_ref[...] = acc.astype(o_ref.dtype)
            if fused:
                x_ref[...] = res_ref[...] + g_ref[...] * acc
            if relu2:
                r = jnp.maximum(acc, 0.0)
                act_ref[...] = (r * r).astype(BF16)

        p = _dot(a_ref[...], b_ref[...], dn)
        if nk == 1:
            finish(p)
        else:
            k = pl.program_id(2)

            @pl.when(k == 0)
            def _():
                acc_ref[...] = p

            @pl.when(k > 0)
            def _():
                acc_ref[...] += p

            @pl.when(k == nk - 1)
            def _():
                finish(acc_ref[...])

    in_specs = [a_spec, b_spec]
    args = [a, b]
    if after is not None:
        in_specs.append(pl.BlockSpec(memory_space=pl.ANY))
        args.append(after)
    out_shape = jax.ShapeDtypeStruct((M, N), out_dtype)
    out_specs = o_spec
    if fused:
        in_specs += [pl.BlockSpec((tm, tn), lambda i, j, k: (i, j)), pl.BlockSpec((1, tn), lambda i, j, k: (0, j))]
        args += [res, gvec]
        out_shape = (out_shape, jax.ShapeDtypeStruct((M, N), F32))
        out_specs = (o_spec, pl.BlockSpec((tm, tn), lambda i, j, k: (i, j)))
    elif relu2_of is not None:
        in_specs.append(pl.BlockSpec((tm, tn), lambda i, j, k: (i, j)))
        args.append(relu2_of)
    elif relu2:
        out_shape = (out_shape, jax.ShapeDtypeStruct((M, N), BF16))
        out_specs = (o_spec, pl.BlockSpec((tm, tn), lambda i, j, k: (i, j)))
    kw = dict(name=name, grid=(M // tm, N // tn, nk), in_specs=in_specs, out_specs=out_specs,
              out_shape=out_shape, scratch_shapes=[pltpu.VMEM((tm, tn), F32)] if nk > 1 else [])
    if comm is not None:
        return _carry(body, comm, **kw)(*args)
    return _pcall(body, compiler_params=_params(("parallel", "parallel", "arbitrary")), **kw)(*args)


PROJ_TN = 512
ATT_T0, ATT_T1 = 6144 // PROJ_TN, 10752 // PROJ_TN
N_SLABS = (ATT_T1 - ATT_T0) * 4
MAIN_COLS = IN_COLS - (ATT_T1 - ATT_T0) * PROJ_TN


PROJ_TILES = IN_COLS // PROJ_TN
SHARD_ROWS = IN_COLS // N_DEV
W_CHUNKS = 4
N_OWN, N_NEAR = 5, 18


def _proj_order():
    out = np.zeros((4, 3, PROJ_TILES), np.int32)
    for q in range(4):
        def hops(t):
            owners = {col // (2 * SHARD_ROWS) for col in (t * PROJ_TN, (t + 1) * PROJ_TN - 1)}
            return max(bin(q ^ p).count("1") for p in owners)
        order = sorted(range(PROJ_TILES), key=lambda t: (hops(t), t))
        assert all(hops(t) == 0 for t in order[:N_OWN]) and all(hops(t) < 2 for t in order[:N_NEAR])
        is_att = [ATT_T0 <= t < ATT_T1 for t in order]
        for row, kind, index in ((1, False, lambda t: t if t < ATT_T0 else t - (ATT_T1 - ATT_T0)),
                                 (2, True, lambda t: t - ATT_T0)):
            own = [index(t) if a == kind else None for t, a in zip(order, is_att)]
            first = next(v for v in own if v is not None)
            last = first
            for j, v in enumerate(own):
                last = last if v is None else v
                out[q, row, j] = last
        out[q, 0] = order
    return out


def _gather_proj(x, g, sh, sc, shard, order):
    rows = SHARD_ROWS // W_CHUNKS

    def body(ord_ref, x_ref, g_ref, shift_ref, scale_ref, sh_ref, main_ref, slab_ref, full_ref, a_ref, wbuf, xbuf,
             fetch_sems, send_sems, recv_sems, local_sems, x_sem):
        j = pl.program_id(0)
        x, y, c = _mesh_pos()
        me, sibling = (x, y, c), (x, y, 1 - c)
        chips = [(1 - x, y), (x, 1 - y), (1 - x, 1 - y)]

        def block(p, owner):
            return full_ref.at[pl.ds(pl.multiple_of(_slot(owner) * SHARD_ROWS + p * rows, 16), rows)]

        def copy(p, k, owner, to, from_input=False):
            dst = block(p, owner)
            return pltpu.make_async_remote_copy(
                src_ref=sh_ref.at[pl.ds(p * rows, rows)] if from_input else dst, dst_ref=dst,
                send_sem=send_sems.at[7 * p + k], recv_sem=recv_sems.at[7 * p + k], device_id=to, device_id_type=MESH)

        pieces = range(W_CHUNKS)
        mine = [pltpu.make_async_copy(sh_ref.at[pl.ds(p * rows, rows)], block(p, me), local_sems.at[p]) for p in pieces]
        first = [copy(p, 0, me, sibling, from_input=True) for p in pieces]
        first += [copy(p, 1 + n, me, (*chips[n], c), from_input=True) for p in pieces for n in range(2)]
        near_pass = [copy(p, 4 + n, (*chips[n], c), sibling) for p in pieces for n in range(2)]
        relay = [copy(p, 3, ((x + 1 - c) % 2, (y + c) % 2, c), ((x + c) % 2, (y + 1 - c) % 2, c)) for p in pieces]
        far_pass = [copy(p, 6, (*chips[2], c), sibling) for p in pieces]

        def fetch(pos):
            slot = lax.rem(pos, 2)
            start = pl.multiple_of(ord_ref[0, pos] * PROJ_TN, PROJ_TN)
            return pltpu.make_async_copy(full_ref.at[pl.ds(start, PROJ_TN)], wbuf.at[slot], fetch_sems.at[slot])

        @pl.when(j == 0)
        def _():
            x_copy = pltpu.make_async_copy(x_ref, xbuf, x_sem.at[0])
            x_copy.start()
            for cp in mine + first:
                cp.start()
            x_copy.wait()
            for r in range(S // TR):
                rws = pl.ds(r * TR, TR)
                xv = xbuf[rws, :]
                rstd = lax.rsqrt(jnp.mean(xv * xv, axis=-1, keepdims=True) + RMS_EPS)
                n = xv * rstd * g_ref[...]
                a_ref[rws, :] = (n * (1.0 + scale_ref[...]) + shift_ref[...]).astype(BF16)
            for cp in mine:
                cp.wait()
            for p in pieces:
                copy(p, 0, sibling, me).wait_recv()
            fetch(j).start()

        @pl.when(j == N_OWN - 1)
        def _():
            for p in pieces:
                for n in range(2):
                    copy(p, 1 + n, (*chips[n], c), me).wait_recv()
                    near_pass[2 * p + n].start()
                relay[p].start()
            for p in pieces:
                for n in range(2):
                    copy(p, 4 + n, (*chips[n], 1 - c), me).wait_recv()

        @pl.when(j == N_NEAR - 1)
        def _():
            for p in pieces:
                copy(p, 3, (*chips[2], c), me).wait_recv()
                far_pass[p].start()
            for p in pieces:
                copy(p, 6, (*chips[2], 1 - c), me).wait_recv()

        @pl.when(j + 1 < PROJ_TILES)
        def _():
            fetch(j + 1).start()

        fetch(j).wait()
        w_ref = wbuf.at[lax.rem(j, 2)]
        tile = ord_ref[0, j]
        is_att = (tile >= ATT_T0) & (tile < ATT_T1)
        chunks = [pl.ds(r * 512, 512) for r in range(S // 512)]

        @pl.when(jnp.logical_not(is_att))
        def _():
            for rws in chunks:
                main_ref[rws, :] = _dot(a_ref[rws, :], w_ref[...], NT)

        @pl.when(is_att)
        def _():
            for rws in chunks:
                p = _dot(a_ref[rws, :], w_ref[...], NT)
                for h in range(4):
                    slab_ref[h, rws, :] = p[:, h * 128:(h + 1) * 128]

        @pl.when(j == PROJ_TILES - 1)
        def _():
            for cp in first + near_pass + relay + far_pass:
                cp.wait_send()

    vec = pl.BlockSpec((1, D), lambda j, o: (0, 0))
    gs = pltpu.PrefetchScalarGridSpec(
        num_scalar_prefetch=1, grid=(PROJ_TILES,),
        in_specs=[HBM_SPEC, vec, vec, vec, HBM_SPEC],
        out_specs=(pl.BlockSpec((S, PROJ_TN), lambda j, o: (0, o[1, j])),
                   pl.BlockSpec((4, S, 128), lambda j, o: (o[2, j], 0, 0)), HBM_SPEC,
                   pl.BlockSpec((S, D), lambda j, o: (0, 0))),
        scratch_shapes=[pltpu.VMEM((2, PROJ_TN, D), BF16), pltpu.VMEM((S, D), F32), pltpu.SemaphoreType.DMA((2,)),
                        pltpu.SemaphoreType.DMA((7 * W_CHUNKS,)), pltpu.SemaphoreType.DMA((7 * W_CHUNKS,)),
                        pltpu.SemaphoreType.DMA((W_CHUNKS,)), pltpu.SemaphoreType.DMA((1,))])
    return _pcall(body, name="gather_proj", grid_spec=gs,
                  out_shape=(jax.ShapeDtypeStruct((S, MAIN_COLS), F32), jax.ShapeDtypeStruct((N_SLABS, S, 128), F32),
                             jax.ShapeDtypeStruct((IN_COLS, D), BF16), jax.ShapeDtypeStruct((S, D), BF16)),
                  compiler_params=_params(("arbitrary",)))(order, x, g, sh, sc, shard)


TR = 256


def _row_spec(w=D):
    return pl.BlockSpec((TR, w), lambda i: (i, 0))


def _vec_spec(w=D):
    return pl.BlockSpec((1, w), lambda i: (0, 0))


def _norm_mod_bwd(x, g, sc, dh, dres, name, gate=None, after=None):
    gated = gate is not None

    def body(x_ref, g_ref, sc_ref, dh_ref, dres_ref, *rest):
        if after is not None:
            rest = rest[1:]
        if gated:
            f_ref, gv_ref, dx_ref, dsc_ref, dsh_ref, dg_ref, dz_ref, dgv_ref = rest
        else:
            dx_ref, dsc_ref, dsh_ref, dg_ref = rest
        i = pl.program_id(0)
        xv = x_ref[...]
        dh = dh_ref[...]
        rstd = lax.rsqrt(jnp.mean(xv * xv, axis=-1, keepdims=True) + RMS_EPS)
        xhat = xv * rstd
        gv = g_ref[...]
        dn = dh * (1.0 + sc_ref[...])
        dxhat = dn * gv
        dx = dres_ref[...] + rstd * (dxhat - xhat * jnp.mean(dxhat * xhat, axis=-1, keepdims=True))
        dx_ref[...] = dx
        sums = [(dsc_ref, jnp.sum(dh * (xhat * gv), axis=0, keepdims=True)),
                (dsh_ref, jnp.sum(dh, axis=0, keepdims=True)),
                (dg_ref, jnp.sum(dn * xhat, axis=0, keepdims=True))]
        if gated:
            dz_ref[...] = (dx * gv_ref[...]).astype(BF16)
            sums.append((dgv_ref, jnp.sum(dx * f_ref[...], axis=0, keepdims=True)))

        @pl.when(i == 0)
        def _():
            for ref, p in sums:
                ref[...] = p

        @pl.when(i > 0)
        def _():
            for ref, p in sums:
                ref[...] += p

    vec = jax.ShapeDtypeStruct((1, D), F32)
    in_specs = [_row_spec(), _vec_spec(), _vec_spec(), _row_spec(), _row_spec()]
    out_specs = [_row_spec(), _vec_spec(), _vec_spec(), _vec_spec()]
    out_shape = [jax.ShapeDtypeStruct((S, D), F32), vec, vec, vec]
    args = [x, g, sc, dh, dres]
    if after is not None:
        in_specs.append(HBM_SPEC)
        args.append(after)
    if gated:
        in_specs += [_row_spec(), _vec_spec()]
        out_specs += [_row_spec(), _vec_spec()]
        out_shape += [jax.ShapeDtypeStruct((S, D), BF16), vec]
        args += list(gate)
    return _pcall(body, name=name, grid=(S // TR,), in_specs=in_specs, out_specs=tuple(out_specs),
                  out_shape=tuple(out_shape), compiler_params=_params(("arbitrary",)))(*args)


def _w_o_norm2(merged, w_o, x, g1, g, sh, sc):
    def body(a_ref, b_ref, x_ref, g1_ref, g_ref, sh_ref, sc_ref, o_ref, x1_ref, h_ref):
        acc = _dot(a_ref[...], b_ref[...], NN)
        o_ref[...] = acc
        xv = x_ref[...] + g1_ref[...] * acc
        x1_ref[...] = xv
        rstd = lax.rsqrt(jnp.mean(xv * xv, axis=-1, keepdims=True) + RMS_EPS)
        h_ref[...] = (xv * rstd * g_ref[...] * (1.0 + sc_ref[...]) + sh_ref[...]).astype(BF16)

    rows = pl.BlockSpec((FF2_TM, D), lambda i: (i, 0))
    f32 = jax.ShapeDtypeStruct((S, D), F32)
    return _pcall(body, name="w_o_norm2", grid=(S // FF2_TM,),
                  in_specs=[rows, pl.BlockSpec((D, D), lambda i: (0, 0)), rows] + [_vec_spec()] * 4,
                  out_specs=(rows, rows, rows), out_shape=(f32, f32, jax.ShapeDtypeStruct((S, D), BF16)),
                  compiler_params=_params(("parallel",)))(merged, w_o, x, g1, g, sh, sc)


FF2_TM = 512


def _ff2_final(act, w_ff2, x1, g2, tgt, g):
    def body(a_ref, b_ref, x1_ref, g2_ref, t_ref, g_ref, loss_ref, dx_ref, dg_ref, df_ref, dg2_ref):
        i = pl.program_id(0)
        f = _dot(a_ref[...], b_ref[...], NN)
        g2v = g2_ref[...]
        xv = x1_ref[...] + g2v * f
        gv = g_ref[...]
        rstd = lax.rsqrt(jnp.mean(xv * xv, axis=-1, keepdims=True) + RMS_EPS)
        xhat = xv * rstd
        err = xhat * gv - t_ref[...]
        dy = err * (1.0 / D)
        dxhat = dy * gv
        dx = rstd * (dxhat - xhat * jnp.mean(dxhat * xhat, axis=-1, keepdims=True))
        dx_ref[...] = dx
        df_ref[...] = (dx * g2v).astype(BF16)
        p_g = jnp.sum(dy * xhat, axis=0, keepdims=True)
        p_g2 = jnp.sum(dx * f, axis=0, keepdims=True)
        p_l = jnp.zeros((1, 128), F32) + 0.5 * jnp.sum(jnp.mean(err * err, axis=-1, keepdims=True))

        @pl.when(i == 0)
        def _():
            dg_ref[...] = p_g
            dg2_ref[...] = p_g2
            loss_ref[...] = p_l

        @pl.when(i > 0)
        def _():
            dg_ref[...] += p_g
            dg2_ref[...] += p_g2
            loss_ref[...] += p_l

    vec = jax.ShapeDtypeStruct((1, D), F32)
    rows = lambda w: pl.BlockSpec((FF2_TM, w), lambda i: (i, 0))
    return _pcall(body, name="ff2_final", grid=(S // FF2_TM,),
                  in_specs=[rows(D_FF), pl.BlockSpec((D_FF, D), lambda i: (0, 0)), rows(D), _vec_spec(), rows(D),
                            _vec_spec()],
                  out_specs=(_vec_spec(128), rows(D), _vec_spec(), rows(D), _vec_spec()),
                  out_shape=(jax.ShapeDtypeStruct((1, 128), F32), jax.ShapeDtypeStruct((S, D), F32), vec,
                             jax.ShapeDtypeStruct((S, D), BF16), vec),
                  compiler_params=_params(("arbitrary",)))(act, w_ff2, x1, g2, tgt, g)


HALF = 512


MERGE_TM = 1024


def _merge_specs():
    blk = lambda off: pl.BlockSpec((MERGE_TM, HALF), lambda i, j: (i, off // HALF + j))
    return blk(OFF_GA), blk(OFF_GB), blk(0)


def _att_out_merge(att, w_att_out, proj, ret_out):
    def body(a_ref, b_ref, ga_ref, gb_ref, r_ref, o_ref, m_ref):
        acc = _dot(a_ref[...], b_ref[...], NN)
        o_ref[...] = acc
        m_ref[...] = (jax.nn.sigmoid(ga_ref[...]) * r_ref[...] + jax.nn.sigmoid(gb_ref[...]) * acc).astype(BF16)

    ga, gb, tile = _merge_specs()
    return _pcall(body, name="att_out", grid=(S // MERGE_TM, D // HALF),
                  in_specs=[pl.BlockSpec((MERGE_TM, AW), lambda i, j: (i, 0)), pl.BlockSpec((AW, HALF), lambda i, j: (0, j)),
                            ga, gb, tile],
                  out_specs=(tile, tile),
                  out_shape=(jax.ShapeDtypeStruct((S, D), F32), jax.ShapeDtypeStruct((S, D), BF16)),
                  compiler_params=_params(("parallel", "parallel")))(att, w_att_out, proj, proj, ret_out)


def _dmerged_split(dmixo, w_o, proj, ret_out, att_out):
    def body(a_ref, b_ref, ga_ref, gb_ref, r_ref, at_ref, dr_ref, da_ref, dga_ref, dgb_ref):
        dm = _dot(a_ref[...], b_ref[...], NT)
        sa = jax.nn.sigmoid(ga_ref[...])
        sb = jax.nn.sigmoid(gb_ref[...])
        dr_ref[...] = (dm * sa).astype(BF16)
        da_ref[...] = (dm * sb).astype(BF16)
        dga_ref[...] = (dm * r_ref[...] * (sa * (1.0 - sa))).astype(BF16)
        dgb_ref[...] = (dm * at_ref[...] * (sb * (1.0 - sb))).astype(BF16)

    ga, gb, tile = _merge_specs()
    o = jax.ShapeDtypeStruct((S, D), BF16)
    return _pcall(body, name="dmerged", grid=(S // MERGE_TM, D // HALF),
                  in_specs=[pl.BlockSpec((MERGE_TM, D), lambda i, j: (i, 0)), pl.BlockSpec((HALF, D), lambda i, j: (j, 0)),
                            ga, gb, tile, tile],
                  out_specs=(tile,) * 4, out_shape=(o, o, o, o),
                  compiler_params=_params(("parallel", "parallel")))(dmixo, w_o, proj, proj, ret_out, att_out)


def _ret_tables():
    H, C = RET_HEADS, CHUNK
    log_g = jnp.log1p(-(2.0 ** (-5.0 - jnp.arange(H, dtype=F32))))
    idx = jnp.arange(C, dtype=F32)
    rel = idx[:, None] - idx[None, :]
    inner = jnp.where(rel >= 0, jnp.exp(log_g[:, None, None] * jnp.maximum(rel, 0.0)), 0.0)
    qd = jnp.exp(log_g[:, None] * (idx + 1.0))[:, :, None]
    kd = jnp.exp(log_g[:, None] * (C - 1.0 - idx))[:, :, None]
    cd = jnp.broadcast_to(jnp.exp(log_g * C)[:, None, None], (H, 1, 128))
    half = RET_DK // 2
    inv = 10000.0 ** (-jnp.arange(half, dtype=F32) / half)
    ang = jnp.arange(S, dtype=F32)[:, None] * inv[None, :]
    return inner, qd, kd, cd, jnp.cos(ang), jnp.sin(ang)


def _rot(x, cos, sin):
    x1, x2 = x[:, :128], x[:, 128:]
    return jnp.concatenate([x1 * cos - x2 * sin, x1 * sin + x2 * cos], axis=1)


def _rot_t(d, cos, sin):
    d1, d2 = d[:, :128], d[:, 128:]
    return jnp.concatenate([d1 * cos + d2 * sin, d2 * cos - d1 * sin], axis=1)


RET_COLS = OFF_ATT
RET_VW = RET_HEADS * RET_DV


def _ret_specs(chunk_of):
    ci = chunk_of
    whole = lambda shape: pl.BlockSpec(shape, lambda t: (0,) * len(shape))
    return [
        pl.BlockSpec((CHUNK, RET_COLS), lambda t: (ci(t), 0)),
        pl.BlockSpec((CHUNK, 128), lambda t: (ci(t), 0)),
        pl.BlockSpec((CHUNK, 128), lambda t: (ci(t), 0)),
        whole((RET_HEADS, CHUNK, CHUNK)), whole((RET_HEADS, CHUNK, 1)), whole((RET_HEADS, CHUNK, 1)),
        whole((RET_HEADS, 1, 128)), whole((1, RET_VW)), whole((1, RET_VW)),
    ]


def _ret_cols(h):
    q = slice(OFF_RQ + h * RET_DK, OFF_RQ + (h + 1) * RET_DK)
    k = slice(OFF_RK + h * RET_DK, OFF_RK + (h + 1) * RET_DK)
    v = slice(OFF_RV + h * RET_DV, OFF_RV + (h + 1) * RET_DV)
    g = slice(OFF_RG + h * RET_DV, OFF_RG + (h + 1) * RET_DV)
    return q, k, v, g, slice(h * RET_DV, (h + 1) * RET_DV)


def _ret_fwd(proj, tables, gn_g, gn_b, after):
    inner, qd, kd, cd, cos, sin = tables

    def body(x_ref, cos_ref, sin_ref, in_ref, qd_ref, kd_ref, cd_ref, g_ref, b_ref, after_ref,
             gated_ref, ro_ref, st_ref, s_scr):
        i = pl.program_id(0)

        @pl.when(i == 0)
        def _():
            s_scr[...] = jnp.zeros_like(s_scr)

        cosv, sinv = cos_ref[...], sin_ref[...]
        for h in range(RET_HEADS):
            cq, ck, cv, cg, co = _ret_cols(h)
            q = _rot(x_ref[:, cq], cosv, sinv)
            k = _rot(x_ref[:, ck], cosv, sinv) * (RET_DK ** -0.5)
            v = x_ref[:, cv]
            st = s_scr[h]
            st_ref[h] = st.astype(BF16)
            s = _dot(q, k, NT) * in_ref[h]
            o = _dot(s, v, NN) + _dot(q, st, NN) * qd_ref[h]
            s_scr[h] = st * cd_ref[h, :, :1] + _dot(k * kd_ref[h], v, TN)
            ro_ref[:, co] = o
            mu = jnp.mean(o, axis=-1, keepdims=True)
            oc = o - mu
            var = jnp.mean(oc * oc, axis=-1, keepdims=True)
            rn = oc * lax.rsqrt(var + GN_EPS) * g_ref[:, co] + b_ref[:, co]
            rg = x_ref[:, cg]
            gated_ref[:, co] = (rg * jax.nn.sigmoid(rg) * rn).astype(BF16)

    ospec = pl.BlockSpec((CHUNK, RET_VW), lambda t: (t, 0))
    return _pcall(
        body, name="ret_fwd", grid=(N_CHUNK,), in_specs=_ret_specs(lambda t: t) + [HBM_SPEC],
        out_specs=(ospec, ospec, pl.BlockSpec((RET_HEADS, None, RET_DK, RET_DV), lambda t: (0, t, 0, 0))),
        out_shape=(jax.ShapeDtypeStruct((S, RET_VW), BF16), jax.ShapeDtypeStruct((S, RET_VW), F32),
                   jax.ShapeDtypeStruct((RET_HEADS, N_CHUNK, RET_DK, RET_DV), BF16)),
        scratch_shapes=[pltpu.VMEM((RET_HEADS, RET_DK, RET_DV), F32)],
        compiler_params=_params(("arbitrary",)))(proj, cos, sin, inner, qd, kd, cd, gn_g, gn_b, after)


def _ret_bwd(proj, tables, gn_g, gn_b, ro, states, dgated, others):
    inner, qd, kd, cd, cos, sin = tables
    last = N_CHUNK - 1
    pieces = lambda o: [(h, o.shape[2]) for h in range(o.shape[0])] if len(o.shape) == 3 else [(None, o.shape[1])]
    assert RET_COLS + sum(w for o in others for _, w in pieces(o)) == IN_COLS

    def body(x_ref, cos_ref, sin_ref, in_ref, qd_ref, kd_ref, cd_ref, g_ref, b_ref, ro_ref, st_ref, dg_ref, *rest):
        other_refs, (dx_ref, gg_ref, gb_ref, gs_scr) = rest[:len(others)], rest[len(others):]
        t = pl.program_id(0)
        col = RET_COLS
        for o_ref in other_refs:
            for h, w in pieces(o_ref):
                dx_ref[:, col:col + w] = (o_ref[...] if h is None else o_ref[h]).astype(BF16)
                col += w

        @pl.when(t == 0)
        def _():
            gs_scr[...] = jnp.zeros_like(gs_scr)
            gg_ref[...] = jnp.zeros_like(gg_ref)
            gb_ref[...] = jnp.zeros_like(gb_ref)

        cosv, sinv = cos_ref[...], sin_ref[...]
        for h in range(RET_HEADS):
            cq, ck, cv, cg, co = _ret_cols(h)
            q = _rot(x_ref[:, cq], cosv, sinv)
            k = _rot(x_ref[:, ck], cosv, sinv) * (RET_DK ** -0.5)
            v = x_ref[:, cv]
            qdv, kdv, dm = qd_ref[h], kd_ref[h], in_ref[h]
            st = st_ref[h]
            o = ro_ref[:, co]
            gv = g_ref[:, co]
            mu = jnp.mean(o, axis=-1, keepdims=True)
            oc = o - mu
            rstd = lax.rsqrt(jnp.mean(oc * oc, axis=-1, keepdims=True) + GN_EPS)
            ohat = oc * rstd
            rn = ohat * gv + b_ref[:, co]
            rg = x_ref[:, cg]
            sg = jax.nn.sigmoid(rg)
            dgt = dg_ref[:, co]
            drn = dgt * (rg * sg)
            dx_ref[:, cg] = (dgt * rn * (sg * (1.0 + rg * (1.0 - sg)))).astype(BF16)
            gg_ref[:, co] += jnp.sum(drn * ohat, axis=0, keepdims=True)
            gb_ref[:, co] += jnp.sum(drn, axis=0, keepdims=True)
            dohat = drn * gv
            do = rstd * (dohat - jnp.mean(dohat, axis=-1, keepdims=True)
                         - ohat * jnp.mean(dohat * ohat, axis=-1, keepdims=True))
            gs = gs_scr[h]
            s = _dot(q, k, NT) * dm
            dsr = _dot(do, v, NT) * dm
            dq = _dot(dsr, k, NN) + _dot(do, st, NT) * qdv
            dk = _dot(dsr, q, TN) + _dot(v, gs, NT) * kdv
            dv = _dot(s, do, TN) + _dot(k * kdv, gs, NN)
            gs_scr[h] = gs * cd_ref[h, :, :1] + _dot(q * qdv, do, TN)
            dx_ref[:, cq] = _rot_t(dq, cosv, sinv).astype(BF16)
            dx_ref[:, ck] = (_rot_t(dk, cosv, sinv) * (RET_DK ** -0.5)).astype(BF16)
            dx_ref[:, cv] = dv.astype(BF16)

    rev = lambda t: last - t
    vblk = pl.BlockSpec((CHUNK, RET_VW), lambda t: (rev(t), 0))
    vspec = pl.BlockSpec((1, RET_VW), lambda t: (0, 0))
    rows = lambda w: pl.BlockSpec((CHUNK, w), lambda t: (rev(t), 0))
    return _pcall(
        body, name="ret_bwd", grid=(N_CHUNK,),
        in_specs=_ret_specs(rev) + [vblk, pl.BlockSpec((RET_HEADS, None, RET_DK, RET_DV), lambda t: (0, rev(t), 0, 0)),
                                    vblk] + [rows(o.shape[1]) if o.ndim == 2 else
                                             pl.BlockSpec((o.shape[0], CHUNK, o.shape[2]), lambda t: (0, rev(t), 0))
                                             for o in others],
        out_specs=(rows(IN_COLS), vspec, vspec),
        out_shape=(jax.ShapeDtypeStruct((S, IN_COLS), BF16), jax.ShapeDtypeStruct((1, RET_VW), F32),
                   jax.ShapeDtypeStruct((1, RET_VW), F32)),
        scratch_shapes=[pltpu.VMEM((RET_HEADS, RET_DK, RET_DV), F32)],
        compiler_params=_params(("arbitrary",)))(proj, cos, sin, inner, qd, kd, cd, gn_g, gn_b, ro, states, dgated,
                                                 *others)


def _bucket_tables():
    qi = np.arange(ATT_BLK)[:, None]
    kj = np.arange(2 * ATT_BLK)[None, :]
    m = ATT_BLK + qi - kj
    out = []
    for win, dil in ATT_GROUPS:
        w = win // dil
        dist = (np.clip(m, 0, w) * dil).astype(np.int32)
        max_exact = N_BUCKETS // 2
        d_f = np.maximum(dist, 1).astype(np.float32)
        large = max_exact + (np.log(d_f / np.float32(max_exact)) / np.float32(math.log(MAX_DIST / max_exact))
                             * np.float32(N_BUCKETS - max_exact)).astype(np.int32)
        large = np.minimum(large, N_BUCKETS - 1)
        out.append(np.where(dist < max_exact, dist, large).astype(np.int32))
    return np.stack(out)


def _bias_build(rel_bias, buckets, after):
    def body(tab_ref, bk_ref, after_ref, o_ref):
        hh = pl.program_id(0)
        bk = bk_ref[...]
        acc = jnp.zeros((ATT_BLK, 2 * ATT_BLK), F32)
        for b in range(N_BUCKETS):
            acc = jnp.where(bk == b, tab_ref[b, hh], acc)
        o_ref[...] = acc

    nh = len(ATT_GROUPS) * ATT_HG
    return _pcall(body, name="bias_build", grid=(nh,),
                  in_specs=[pl.BlockSpec(memory_space=pltpu.SMEM),
                            pl.BlockSpec((None, ATT_BLK, 2 * ATT_BLK), lambda hh: (hh // ATT_HG, 0, 0)), HBM_SPEC],
                  out_specs=pl.BlockSpec((None, ATT_BLK, 2 * ATT_BLK), lambda hh: (hh, 0, 0)),
                  out_shape=jax.ShapeDtypeStruct((nh, ATT_BLK, 2 * ATT_BLK), F32),
                  compiler_params=_params(("parallel",)))(rel_bias, buckets, after)


def _bias_grad(ds_sum, buckets):
    def body(ds_ref, bk_ref, o_ref):
        bk = bk_ref[...]
        ds = ds_ref[...]
        rows = lax.broadcasted_iota(jnp.int32, (N_BUCKETS, 128), 0)
        acc = jnp.zeros((N_BUCKETS, 128), F32)
        for b in range(N_BUCKETS):
            acc = jnp.where(rows == b, jnp.sum(jnp.where(bk == b, ds, 0.0)), acc)
        o_ref[...] = acc

    nh = len(ATT_GROUPS) * ATT_HG
    return _pcall(body, name="bias_grad", grid=(nh,),
                  in_specs=[pl.BlockSpec((None, ATT_BLK, 2 * ATT_BLK), lambda hh: (hh, 0, 0)),
                            pl.BlockSpec((None, ATT_BLK, 2 * ATT_BLK), lambda hh: (hh // ATT_HG, 0, 0))],
                  out_specs=pl.BlockSpec((None, N_BUCKETS, 128), lambda hh: (hh, 0, 0)),
                  out_shape=jax.ShapeDtypeStruct((nh, N_BUCKETS, 128), F32),
                  compiler_params=_params(("parallel",)))(ds_sum, buckets)


def _att_valid(n):
    qi = lax.broadcasted_iota(jnp.int32, (ATT_BLK, 2 * ATT_BLK), 0)
    kj = lax.broadcasted_iota(jnp.int32, (ATT_BLK, 2 * ATT_BLK), 1)
    m = ATT_BLK + qi - kj
    first_key = jnp.where(n > 0, 0, ATT_BLK)
    return (m >= 0) & (m <= ATT_BLK) & (kj >= first_key)


ATT_HP = (1, 2, 2)


def _att_geometry(gi):
    _, dil = ATT_GROUPS[gi]
    return dil, S // dil // ATT_BLK, ATT_HP[gi]


def _blk(dil, r, n):
    if dil == 1:
        return pl.ds(n * ATT_BLK, ATT_BLK)
    return pl.ds(r + n * ATT_BLK * dil, ATT_BLK, stride=dil)


def _slab_specs(gi):
    _, _, hp = _att_geometry(gi)
    per = ATT_HG // hp
    return [pl.BlockSpec((hp, S, ATT_DH), lambda g, r, part=part: ((3 * gi + part) * per + g, 0, 0))
            for part in range(3)]


def _head_specs(gi, count):
    _, _, hp = _att_geometry(gi)
    return [pl.BlockSpec((hp, S, ATT_DH), lambda g, r: (g, 0, 0))] * count


def _bias_spec(gi):
    _, _, hp = _att_geometry(gi)
    return pl.BlockSpec((hp, ATT_BLK, 2 * ATT_BLK), lambda g, r: (gi * (ATT_HG // hp) + g, 0, 0))


def _att_valid_first():
    qi = lax.broadcasted_iota(jnp.int32, (ATT_BLK, ATT_BLK), 0)
    kj = lax.broadcasted_iota(jnp.int32, (ATT_BLK, ATT_BLK), 1)
    return kj <= qi


def _att_fwd(slabs, bias, gi, comm=None):
    dil, nb, hp = _att_geometry(gi)
    scale = ATT_DH ** -0.5

    def body(q_ref, k_ref, v_ref, bias_ref, o_ref, l_ref):
        r = pl.program_id(1)
        for n in range(nb):
            cur = _blk(dil, r, n)
            valid = _att_valid(n) if n > 0 else _att_valid_first()
            for h in range(hp):
                if n > 0:
                    prev = _blk(dil, r, n - 1)
                    kk = jnp.concatenate([k_ref[h, prev, :], k_ref[h, cur, :]], axis=0)
                    vv = jnp.concatenate([v_ref[h, prev, :], v_ref[h, cur, :]], axis=0)
                    bias = bias_ref[h]
                else:
                    kk, vv, bias = k_ref[h, cur, :], v_ref[h, cur, :], bias_ref[h, :, pl.ds(ATT_BLK, ATT_BLK)]
                s = _dot(q_ref[h, cur, :], kk, NT) * scale + bias
                s = jnp.where(valid, s, -1e30)
                mx = jnp.max(s, axis=-1, keepdims=True)
                e = jnp.exp(s - mx)
                den = jnp.sum(e, axis=-1, keepdims=True)
                o_ref[h, cur, :] = _dot(e / den, vv, NN)
                l_ref[h, cur, :] = jnp.broadcast_to(mx + jnp.log(den), (ATT_BLK, ATT_DH))

    osh = pltpu.HBM((ATT_HG, S, ATT_DH), F32)
    kw = dict(name=f"att_fwd{gi}", grid=(ATT_HG // hp, dil), in_specs=_slab_specs(gi) + [_bias_spec(gi)],
              out_specs=tuple(_head_specs(gi, 2)), out_shape=(osh, osh))
    if comm is not None:
        return _carry(body, comm, **kw)(slabs, slabs, slabs, bias)
    return _pcall(body, compiler_params=_params(("parallel", "arbitrary")), **kw)(slabs, slabs, slabs, bias)


def _att_bwd(slabs, bias, o, lse, do, dlse, gi, comm=None):
    dil, nb, hp = _att_geometry(gi)
    per = ATT_HG // hp
    scale = ATT_DH ** -0.5
    wide = lambda t: jnp.concatenate([t, t], axis=1)

    def body(q_ref, k_ref, v_ref, bias_ref, o_ref, l_ref, do_ref, dl_ref, dq_ref, dk_ref, dv_ref, ds_ref):
        r = pl.program_id(1)

        @pl.when(r == 0)
        def _():
            ds_ref[...] = jnp.zeros_like(ds_ref)

        for h in range(hp):
            carry_k = carry_v = None
            for n in range(nb):
                cur = _blk(dil, r, n)
                q = q_ref[h, cur, :]
                dov = do_ref[h, cur, :]
                delta = jnp.sum(dov * o_ref[h, cur, :], axis=-1, keepdims=True)
                if n == 0:
                    own = pl.ds(ATT_BLK, ATT_BLK)
                    kk, vv = k_ref[h, cur, :], v_ref[h, cur, :]
                    s = _dot(q, kk, NT) * scale + bias_ref[h, :, own]
                    p = jnp.where(_att_valid_first(), jnp.exp(s - l_ref[h, cur, :]), 0.0)
                    ds = p * (_dot(dov, vv, NT) - delta + dl_ref[h, cur, :])
                    ds_ref[h, :, own] += ds
                    dq_ref[h, cur, :] = _dot(ds, kk, NN) * scale
                    carry_k, carry_v = _dot(ds, q, TN) * scale, _dot(p, dov, TN)
                    continue
                prev = _blk(dil, r, n - 1)
                kk = jnp.concatenate([k_ref[h, prev, :], k_ref[h, cur, :]], axis=0)
                vv = jnp.concatenate([v_ref[h, prev, :], v_ref[h, cur, :]], axis=0)
                s = _dot(q, kk, NT) * scale + bias_ref[h]
                p = jnp.where(_att_valid(n), jnp.exp(s - wide(l_ref[h, cur, :])), 0.0)
                dp = _dot(dov, vv, NT)
                ds = p * (dp - delta + wide(dl_ref[h, cur, :]))
                ds_ref[h] += ds
                dq_ref[h, cur, :] = _dot(ds, kk, NN) * scale
                dkk = _dot(ds, q, TN) * scale
                dvv = _dot(p, dov, TN)
                dk_ref[h, prev, :] = carry_k + dkk[:ATT_BLK]
                dv_ref[h, prev, :] = carry_v + dvv[:ATT_BLK]
                carry_k, carry_v = dkk[ATT_BLK:], dvv[ATT_BLK:]
            last = _blk(dil, r, nb - 1)
            dk_ref[h, last, :] = carry_k
            dv_ref[h, last, :] = carry_v

    osh = jax.ShapeDtypeStruct((ATT_HG, S, ATT_DH), F32)
    kw = dict(name=f"att_bwd{gi}", grid=(per, dil), in_specs=_slab_specs(gi) + [_bias_spec(gi)] + _head_specs(gi, 4),
              out_specs=(*_head_specs(gi, 3), pl.BlockSpec((hp, ATT_BLK, 2 * ATT_BLK), lambda g, r: (g, 0, 0))),
              out_shape=(osh, osh, osh, jax.ShapeDtypeStruct((ATT_HG, ATT_BLK, 2 * ATT_BLK), F32)))
    args = (slabs, slabs, slabs, bias, o, lse, do, dlse)
    if comm is not None:
        return _carry(body, comm, **kw)(*args)
    return _pcall(body, compiler_params=_params(("arbitrary", "arbitrary")), **kw)(*args)


AW = ATT_HG * ATT_DH


def _mix_weights(l0, l1, l2):
    mx = jnp.maximum(jnp.maximum(l0, l1), l2)
    e0, e1, e2 = jnp.exp(l0 - mx), jnp.exp(l1 - mx), jnp.exp(l2 - mx)
    den = e0 + e1 + e2
    return e0 / den, e1 / den, e2 / den


def _heads_spec():
    return pl.BlockSpec((ATT_HG, TR, ATT_DH), lambda i: (0, i, 0))


def _mix_fwd(os_, ls, comm=None):
    def body(o0, o1, o2, l0, l1, l2, att_ref):
        for h in range(ATT_HG):
            w0, w1, w2 = _mix_weights(l0[h], l1[h], l2[h])
            att_ref[:, h * ATT_DH:(h + 1) * ATT_DH] = (w0 * o0[h] + w1 * o1[h] + w2 * o2[h]).astype(BF16)

    kw = dict(name="mix_fwd", grid=(S // TR,), in_specs=[_heads_spec()] * 6, out_specs=_row_spec(AW),
              out_shape=jax.ShapeDtypeStruct((S, AW), BF16))
    if comm is not None:
        return _carry(body, comm, **kw)(*os_, *ls)
    return _pcall(body, compiler_params=_params(("parallel",)), **kw)(*os_, *ls)


def _mix_bwd(os_, ls, datt):
    def body(o0, o1, o2, l0, l1, l2, da_ref, d0, d1, d2, e0, e1, e2):
        for h in range(ATT_HG):
            ws = _mix_weights(l0[h], l1[h], l2[h])
            da = da_ref[:, h * ATT_DH:(h + 1) * ATT_DH]
            dws = []
            for o_ref, w, d_ref in zip((o0, o1, o2), ws, (d0, d1, d2)):
                d_ref[h] = w * da
                dws.append(jnp.broadcast_to(jnp.sum(da * o_ref[h], axis=-1, keepdims=True), (TR, ATT_DH)))
            tot = ws[0] * dws[0] + ws[1] * dws[1] + ws[2] * dws[2]
            for w, dw, e_ref in zip(ws, dws, (e0, e1, e2)):
                e_ref[h] = w * (dw - tot)

    o = pltpu.HBM((ATT_HG, S, ATT_DH), F32)
    return _pcall(body, name="mix_bwd", grid=(S // TR,), in_specs=[_heads_spec()] * 6 + [_row_spec(AW)],
                  out_specs=(_heads_spec(),) * 6, out_shape=(o,) * 6,
                  compiler_params=_params(("parallel",)))(*os_, *ls, datt)


def _ada_fwd(c_all, w_sh, b_sl):
    def body(c_ref, w_ref, b_ref, o_ref):
        cv = c_ref[...]
        o_ref[...] = _dot(cv * jax.nn.sigmoid(cv), w_ref[...], NN) + b_ref[...]

    return _pcall(body, name="ada_fwd", out_shape=jax.ShapeDtypeStruct((N_DEV, w_sh.shape[1]), F32),
                  compiler_params=_params())(c_all, w_sh, b_sl)


CAST_STEPS = 4


def _to_bf16(arrs, comm, name):
    n = len(arrs)

    def body(*refs):
        for src, dst in zip(refs[:n], refs[n:]):
            dst[...] = src[...].astype(BF16)

    blocks = [pl.BlockSpec((a.shape[0] // CAST_STEPS, a.shape[1]), lambda i: (i, 0)) for a in arrs]
    return _carry(body, comm, name=name, grid=(CAST_STEPS,), in_specs=blocks, out_specs=tuple(blocks),
                  out_shape=tuple(pltpu.HBM(a.shape, BF16) for a in arrs))(*arrs)


def _ada_bwd(c_all, dm_sl):
    def body(c_ref, d_ref, o_ref):
        cv = c_ref[...]
        o_ref[...] = _dot(cv * jax.nn.sigmoid(cv), d_ref[...], TN)

    return _pcall(body, name="ada_bwd", out_shape=jax.ShapeDtypeStruct((D, dm_sl.shape[1]), F32),
                  compiler_params=_params())(c_all, dm_sl)


N_MOD = 6


def _sum_small(gathered, widths):
    def body(g_ref, gb_ref, dm_ref, *outs):
        def total(cols):
            acc = g_ref[0, :, cols]
            for e in range(1, N_DEV):
                acc = acc + g_ref[e, :, cols]
            return acc

        gb_ref[...] = total(slice(0, N_MOD * D))
        for e in range(N_DEV):
            dm_ref[e:e + 1, :] = g_ref[e, :, :N_MOD * D]
        col = N_MOD * D
        for w, o_ref in zip(widths, outs):
            o_ref[...] = total(slice(col, col + w))
            col += w

    assert gathered.shape == (N_DEV, 1, N_MOD * D + sum(widths))
    shapes = (jax.ShapeDtypeStruct((1, N_MOD * D), F32), jax.ShapeDtypeStruct((N_DEV, N_MOD * D), F32),
              *[jax.ShapeDtypeStruct((1, w), F32) for w in widths])
    res = _pcall(body, name="sum_small", out_shape=shapes, compiler_params=_params())(gathered)
    return res[0], res[1], res[2:]


def _row_tile(m, n):
    t = max(8, min(m, (1 << 19) // n // 8 * 8))
    while m % t:
        t -= 8
    return t


def _pair_sum(full, recv, sel, name, col_block=0):
    _, m, n = recv.shape
    t = _row_tile(m, n)

    def body(sel_ref, a_ref, b_ref, o_ref):
        o_ref[...] = (a_ref[...].astype(F32) + b_ref[...].astype(F32)).astype(o_ref.dtype)

    gs = pltpu.PrefetchScalarGridSpec(
        num_scalar_prefetch=1, grid=(4, m // t),
        in_specs=[pl.BlockSpec((None, None, t, n), lambda q, i, s: (q, s[0], i, col_block)),
                  pl.BlockSpec((None, t, n), lambda q, i, s: (q, i, 0))],
        out_specs=pl.BlockSpec((None, t, n), lambda q, i, s: (q, i, 0)))
    return _pcall(body, name=name, grid_spec=gs, out_shape=pltpu.HBM((4, m, n), full.dtype),
                  compiler_params=_params(("parallel", "parallel")))(sel, full, recv)


def _chip_sum(part, recv, sel, name):
    _, m, n = part.shape
    t = _row_tile(m, n)

    def body(sel_ref, a_ref, r_ref, o_ref):
        o_ref[...] = ((a_ref[...].astype(F32) + r_ref[0].astype(F32)) + r_ref[1].astype(F32)) + r_ref[2].astype(F32)

    gs = pltpu.PrefetchScalarGridSpec(
        num_scalar_prefetch=1, grid=(m // t,),
        in_specs=[pl.BlockSpec((None, t, n), lambda i, s: (s[0], i, 0)),
                  pl.BlockSpec((3, t, n), lambda i, s: (0, i, 0))],
        out_specs=pl.BlockSpec((t, n), lambda i, s: (i, 0)))
    return _pcall(body, name=name, grid_spec=gs, out_shape=jax.ShapeDtypeStruct((m, n), F32),
                  compiler_params=_params(("parallel",)))(sel, part, recv)


def _adamw_math(w, g, m, v):
    nm = ADAM_B1 * m + (1.0 - ADAM_B1) * g
    nv = ADAM_B2 * v + (1.0 - ADAM_B2) * (g * g)
    m_hat = nm / (1.0 - ADAM_B1 ** ADAM_STEP)
    v_hat = nv / (1.0 - ADAM_B2 ** ADAM_STEP)
    return -ADAM_LR * (m_hat / (jnp.sqrt(v_hat) + ADAM_EPS) + ADAM_WD * w), nm, nv


def _adamw(w, g, m, v, name):
    _, rows, cols = w.shape
    t = _row_tile(rows, cols)

    def body(w_ref, g_ref, m_ref, v_ref, d_ref, nm_ref, nv_ref):
        d_ref[...], nm_ref[...], nv_ref[...] = _adamw_math(w_ref[...], g_ref[...], m_ref[...], v_ref[...])

    spec3 = pl.BlockSpec((None, t, cols), lambda i: (0, i, 0))
    spec2 = pl.BlockSpec((t, cols), lambda i: (i, 0))
    o = jax.ShapeDtypeStruct(w.shape, F32)
    return _pcall(body, name=name, grid=(rows // t,), in_specs=[spec3, spec2, spec3, spec3], out_specs=(spec3,) * 3,
                  out_shape=(o, o, o), compiler_params=_params(("parallel",)))(w, g, m, v)


def _adamw_reduced1(w, m, v, part, recv, sel, name):
    _, rows, cols = w.shape
    t = _row_tile(rows, cols)

    def body(sel_ref, w_ref, m_ref, v_ref, p_ref, r_ref, g_ref, d_ref, nm_ref, nv_ref):
        g = ((p_ref[...].astype(F32) + r_ref[0].astype(F32)) + r_ref[1].astype(F32)) + r_ref[2].astype(F32)
        g_ref[...] = g
        d_ref[...], nm_ref[...], nv_ref[...] = _adamw_math(w_ref[...], g, m_ref[...], v_ref[...])

    wspec = pl.BlockSpec((None, t, cols), lambda i, s: (0, i, 0))
    gs = pltpu.PrefetchScalarGridSpec(
        num_scalar_prefetch=1, grid=(rows // t,),
        in_specs=[wspec, wspec, wspec, pl.BlockSpec((None, t, cols), lambda i, s: (s[0], i, 0)),
                  pl.BlockSpec((3, t, cols), lambda i, s: (0, i, 0))],
        out_specs=(wspec,) * 4)
    o = jax.ShapeDtypeStruct(w.shape, F32)
    return _pcall(body, name=name, grid_spec=gs, out_shape=(o, o, o, o),
                  compiler_params=_params(("parallel",)))(sel, w, m, v, part, recv)


def _adamw_reduced(w, m, v, parts, recvs, sel):
    _, rows, cols = w.shape
    half = cols // 2
    t = _row_tile(rows, half)

    def body(sel_ref, w_ref, m_ref, v_ref, pa_ref, pb_ref, ra_ref, rb_ref, g_ref, d_ref, nm_ref, nv_ref):
        total = lambda p_ref, r_ref: ((p_ref[...].astype(F32) + r_ref[0].astype(F32)) + r_ref[1].astype(F32)) \
            + r_ref[2].astype(F32)
        g = jnp.where(pl.program_id(1) == 0, total(pa_ref, ra_ref), total(pb_ref, rb_ref))
        g_ref[...] = g
        d_ref[...], nm_ref[...], nv_ref[...] = _adamw_math(w_ref[...], g, m_ref[...], v_ref[...])

    wspec = pl.BlockSpec((None, t, half), lambda i, j, s: (0, i, j))
    pspec = pl.BlockSpec((None, t, half), lambda i, j, s: (s[0], i, 0))
    rspec = pl.BlockSpec((3, t, half), lambda i, j, s: (0, i, 0))
    gs = pltpu.PrefetchScalarGridSpec(num_scalar_prefetch=1, grid=(rows // t, 2),
                                      in_specs=[wspec, wspec, wspec, pspec, pspec, rspec, rspec],
                                      out_specs=(wspec,) * 4)
    o = jax.ShapeDtypeStruct(w.shape, F32)
    return _pcall(body, name="adamw_w_in", grid_spec=gs, out_shape=(o, o, o, o),
                  compiler_params=_params(("parallel", "arbitrary")))(sel, w, m, v, *parts, *recvs)


def _adamw_small(ws, gs, ms, vs):
    n = len(ws)

    def body(*refs):
        for i in range(n):
            w_ref, g_ref, m_ref, v_ref = (refs[k * n + i] for k in range(4))
            d, nm, nv = _adamw_math(w_ref[...], g_ref[...], m_ref[...], v_ref[...])
            refs[4 * n + i][...] = d
            refs[5 * n + i][...] = nm
            refs[6 * n + i][...] = nv

    shapes = tuple(jax.ShapeDtypeStruct(w.shape, F32) for w in ws)
    res = _pcall(body, name="adamw_small", out_shape=shapes * 3, compiler_params=_params())(*ws, *gs, *ms, *vs)
    return res[:n], res[n:2 * n], res[2 * n:]


def _mesh_pos():
    return lax.axis_index("x"), lax.axis_index("y"), lax.axis_index("c")


class _GatherSmallDirect:
    ins_in_vmem = True

    def __init__(self, arrs):
        self.ins = list(arrs)
        self.out_shape = tuple(jax.ShapeDtypeStruct((N_DEV,) + a.shape, a.dtype) for a in arrs)
        n = len(arrs)
        self.sems = [pltpu.SemaphoreType.DMA((7 * n,)), pltpu.SemaphoreType.DMA((7 * n,)),
                     pltpu.SemaphoreType.DMA((n,))]

    def _copies(self, ins, outs, sems):
        send_sems, recv_sems, local_sems = sems
        x, y, c = _mesh_pos()
        me = _slot((x, y, c))
        mine = [pltpu.make_async_copy(ins[p], outs[p].at[me], local_sems.at[p]) for p in range(len(self.ins))]
        sends, arrivals = [], []
        for p in range(len(self.ins)):
            for j in range(1, N_DEV):
                peer = (me + j) % N_DEV
                to = (peer // 4, (peer // 2) % 2, peer % 2)
                sends.append(pltpu.make_async_remote_copy(
                    src_ref=ins[p], dst_ref=outs[p].at[me], send_sem=send_sems.at[7 * p + j - 1],
                    recv_sem=recv_sems.at[7 * p + (N_DEV - j) - 1], device_id=to, device_id_type=MESH))
                arrivals.append(pltpu.make_async_remote_copy(
                    src_ref=ins[p], dst_ref=outs[p].at[peer], send_sem=send_sems.at[7 * p + j - 1],
                    recv_sem=recv_sems.at[7 * p + j - 1], device_id=to, device_id_type=MESH))
        return mine, sends, arrivals

    def start(self, ins, outs, sems):
        mine, sends, _ = self._copies(ins, outs, sems)
        for cp in mine + sends:
            cp.start()

    def finish(self, ins, outs, sems):
        mine, sends, arrivals = self._copies(ins, outs, sems)
        for cp in arrivals:
            cp.wait_recv()
        for cp in sends:
            cp.wait_send()
        for cp in mine:
            cp.wait()


class _ExchangeCore:
    def __init__(self, fulls, cols=None):
        self.ins = list(fulls)
        self.cols = cols
        width = lambda f: f.shape[3] if cols is None else cols[1]
        self.out_shape = tuple(jax.ShapeDtypeStruct((4, f.shape[2], width(f)), f.dtype) for f in fulls)
        self.sems = [pltpu.SemaphoreType.DMA((4 * len(fulls),)), pltpu.SemaphoreType.DMA((4 * len(fulls),))]

    def _copies(self, ins, outs, sems):
        send_sems, recv_sems = sems
        x, y, c = _mesh_pos()

        def src(a, q):
            ref = ins[a].at[q, 1 - c]
            return ref if self.cols is None else ref.at[:, pl.ds(*self.cols)]

        return [pltpu.make_async_remote_copy(
            src_ref=src(a, q), dst_ref=outs[a].at[q], send_sem=send_sems.at[4 * a + q],
            recv_sem=recv_sems.at[4 * a + q], device_id=(x, y, 1 - c), device_id_type=MESH)
            for a in range(len(self.ins)) for q in range(4)]

    def start(self, ins, outs, sems):
        for cp in self._copies(ins, outs, sems):
            cp.start()

    def finish(self, ins, outs, sems):
        for cp in self._copies(ins, outs, sems):
            cp.wait()


class _ExchangeChip:
    def __init__(self, parts):
        self.ins = list(parts)
        self.out_shape = tuple(jax.ShapeDtypeStruct((3,) + p.shape[1:], p.dtype) for p in parts)
        self.sems = [pltpu.SemaphoreType.DMA((3 * len(parts),)), pltpu.SemaphoreType.DMA((3 * len(parts),))]

    def _copies(self, ins, outs, sems):
        send_sems, recv_sems = sems
        x, y, c = _mesh_pos()
        chips = [(1 - x, y), (x, 1 - y), (1 - x, 1 - y)]
        return [pltpu.make_async_remote_copy(
            src_ref=ins[a].at[2 * px + py], dst_ref=outs[a].at[j], send_sem=send_sems.at[3 * a + j],
            recv_sem=recv_sems.at[3 * a + j], device_id=(px, py, c), device_id_type=MESH)
            for a in range(len(self.ins)) for j, (px, py) in enumerate(chips)]

    def start(self, ins, outs, sems):
        for cp in self._copies(ins, outs, sems):
            cp.start()

    def finish(self, ins, outs, sems):
        for cp in self._copies(ins, outs, sems):
            cp.wait()


HBM_ONLY = pl.BlockSpec(memory_space=pltpu.HBM)
SEM_SPEC = pl.BlockSpec(memory_space=pltpu.SEMAPHORE)
SIDE_EFFECT = pltpu.SideEffectType.DATAFLOW_SIDE_EFFECTING


def _chip_copies(p_refs, land_refs, send_sems, recv_sems):
    x, y, c = _mesh_pos()
    return [pltpu.make_async_remote_copy(
        src_ref=p_refs[a].at[2 * px + py], dst_ref=land_refs[a].at[j], send_sem=send_sems.at[3 * a + j],
        recv_sem=recv_sems.at[3 * a + j], device_id=(px, py, c), device_id_type=MESH)
        for a in range(len(p_refs)) for j, (px, py) in enumerate([(1 - x, y), (x, 1 - y), (1 - x, 1 - y)])]


def _chip_exchange_start(parts, name):
    n = len(parts)
    lands = [lax.empty((3,) + p.shape[1:], p.dtype) for p in parts]

    def body(*refs):
        p_refs, land_refs, (send_sems, recv_sems) = refs[:n], refs[n:2 * n], refs[2 * n:2 * n + 2]
        for cp in _chip_copies(p_refs, land_refs, send_sems, recv_sems):
            cp.start()
        token = refs[-1]
        token[...] = jnp.zeros_like(token)

    hbm = lambda t: pltpu.HBM(t.shape, t.dtype)
    res = pl.pallas_call(
        body, name=name,
        out_shape=(pltpu.SemaphoreType.DMA((3 * n,)), pltpu.SemaphoreType.DMA((3 * n,)), *[hbm(t) for t in parts + lands],
                   jax.ShapeDtypeStruct((8, 128), F32)),
        in_specs=(HBM_ONLY,) * (2 * n),
        out_specs=(SEM_SPEC, SEM_SPEC, *[HBM_ONLY] * (2 * n), pl.BlockSpec(memory_space=pltpu.VMEM)),
        input_output_aliases={i: 2 + i for i in range(2 * n)},
        compiler_params=pltpu.CompilerParams(has_side_effects=SIDE_EFFECT))(
        *[pltpu.with_memory_space_constraint(t, pltpu.HBM) for t in parts + lands])
    return (res[0], res[1], list(res[2:2 + n]), list(res[2 + n:2 + 2 * n])), res[-1]


def _chip_exchange_wait(in_flight, after, name):
    send_sems, recv_sems, parts, lands = in_flight
    n = len(parts)

    def body(*refs):
        p_refs, land_refs, (send_sems, recv_sems) = refs[:n], refs[n:2 * n], refs[2 * n:2 * n + 2]
        for cp in _chip_copies(p_refs, land_refs, send_sems, recv_sems):
            cp.wait_send()
            cp.wait_recv()

    res = pl.pallas_call(
        body, name=name, out_shape=tuple(pltpu.HBM(t.shape, t.dtype) for t in parts + lands),
        in_specs=(*[HBM_ONLY] * (2 * n), SEM_SPEC, SEM_SPEC, pl.BlockSpec(memory_space=pl.ANY)),
        out_specs=(HBM_ONLY,) * (2 * n), input_output_aliases={i: i for i in range(2 * n)},
        compiler_params=pltpu.CompilerParams(has_side_effects=SIDE_EFFECT))(*parts, *lands, send_sems, recv_sems, after)
    return list(res[:n]), list(res[n:])


def _slot(p):
    return 4 * p[0] + 2 * p[1] + p[2]


def _gather_copies(src_refs, out_refs, send_sems, recv_sems):
    x, y, c = _mesh_pos()
    targets = [(x, y, 1 - c), (1 - x, y, c), (x, 1 - y, c), (1 - x, 1 - y, c)]
    return [pltpu.make_async_remote_copy(
        src_ref=src_refs[a], dst_ref=out_refs[a].at[_slot((x, y, c))], send_sem=send_sems.at[4 * a + k],
        recv_sem=recv_sems.at[4 * a + k], device_id=to, device_id_type=MESH)
        for a in range(len(src_refs)) for k, to in enumerate(targets)]


def _gather_start(shards, after, name):
    n = len(shards)
    outs = [lax.empty((N_DEV,) + s.shape, s.dtype) for s in shards]

    def body(*refs):
        for cp in _gather_copies(refs[:n], refs[n:2 * n], refs[2 * n + 1], refs[2 * n + 2]):
            cp.start()
        token = refs[-1]
        token[...] = jnp.zeros_like(token)

    res = pl.pallas_call(
        body, name=name,
        out_shape=(pltpu.SemaphoreType.DMA((4 * n,)), pltpu.SemaphoreType.DMA((4 * n,)),
                   *[pltpu.HBM(t.shape, t.dtype) for t in shards + outs], jax.ShapeDtypeStruct((8, 128), F32)),
        in_specs=(*[HBM_ONLY] * (2 * n), pl.BlockSpec(memory_space=pl.ANY)),
        out_specs=(SEM_SPEC, SEM_SPEC, *[HBM_ONLY] * (2 * n), pl.BlockSpec(memory_space=pltpu.VMEM)),
        input_output_aliases={i: 2 + i for i in range(2 * n)},
        compiler_params=pltpu.CompilerParams(has_side_effects=SIDE_EFFECT))(
        *[pltpu.with_memory_space_constraint(t, pltpu.HBM) for t in shards + outs], after)
    return (res[0], res[1], list(res[2:2 + n]), list(res[2 + n:2 + 2 * n])), res[-1]


def _gather_wait(in_flight, after, name):
    send_sems, recv_sems, shards, outs = in_flight
    n = len(shards)

    def body(*refs):
        for cp in _gather_copies(refs[:n], refs[n:2 * n], refs[2 * n], refs[2 * n + 1]):
            cp.wait_send()
            cp.wait_recv()

    res = pl.pallas_call(
        body, name=name, out_shape=tuple(pltpu.HBM(t.shape, t.dtype) for t in shards + outs),
        in_specs=(*[HBM_ONLY] * (2 * n), SEM_SPEC, SEM_SPEC, pl.BlockSpec(memory_space=pl.ANY)),
        out_specs=(HBM_ONLY,) * (2 * n), input_output_aliases={i: i for i in range(2 * n)},
        compiler_params=pltpu.CompilerParams(has_side_effects=SIDE_EFFECT))(*shards, *outs, send_sems, recv_sems, after)
    return list(res[:n]), list(res[n:])


class _PassToSibling:
    def __init__(self, shards, gathered):
        n = self.n = len(shards)
        self.ins = list(shards) + list(gathered)
        self.out_shape = tuple(jax.ShapeDtypeStruct(g.shape, g.dtype) for g in gathered)
        self.aliases = {n + a: a for a in range(n)}
        self.sems = [pltpu.SemaphoreType.DMA((3 * n,)), pltpu.SemaphoreType.DMA((3 * n,)),
                     pltpu.SemaphoreType.DMA((n,))]

    def _copies(self, ins, outs, sems):
        send_sems, recv_sems, local_sems = sems
        x, y, c = _mesh_pos()
        chips = [(1 - x, y), (x, 1 - y), (1 - x, 1 - y)]
        mine = [pltpu.make_async_copy(ins[a], outs[a].at[_slot((x, y, c))], local_sems.at[a]) for a in range(self.n)]
        passed, awaited = [], []
        for a in range(self.n):
            for j, chip in enumerate(chips):
                sems_j = dict(send_sem=send_sems.at[3 * a + j], recv_sem=recv_sems.at[3 * a + j],
                              device_id=(x, y, 1 - c), device_id_type=MESH)
                blk = outs[a].at[_slot((*chip, c))]
                passed.append(pltpu.make_async_remote_copy(src_ref=blk, dst_ref=blk, **sems_j))
                got = outs[a].at[_slot((*chip, 1 - c))]
                awaited.append(pltpu.make_async_remote_copy(src_ref=got, dst_ref=got, **sems_j))
        return mine, passed, awaited

    def start(self, ins, outs, sems):
        mine, passed, _ = self._copies(ins, outs, sems)
        for cp in mine + passed:
            cp.start()

    def finish(self, ins, outs, sems):
        mine, passed, awaited = self._copies(ins, outs, sems)
        for cp in passed:
            cp.wait_send()
        for cp in awaited:
            cp.wait_recv()
        for cp in mine:
            cp.wait()


def _reduce_sums(fulls, recv_core, core, tag):
    return [_pair_sum(f, r, core, f"rs_pair_{tag}{i}") for i, (f, r) in enumerate(zip(fulls, recv_core))]


def _local_step(x, tgt, mods, w_in_shard, order, shards, small, chip, core):
    sh1, sc1, g1, sh2, sc2, g2 = mods
    norm1_g, rel_bias, gn_g, gn_b, norm2_g, norm_f_g = small
    tables = _ret_tables()
    buckets = jnp.asarray(_bucket_tables())

    proj, slabs, w_in_t, h1 = _gather_proj(x, norm1_g, sh1, sc1, w_in_shard, order)
    flight_w1, token_w = _gather_start(list(shards[:3]), proj, "gather_w1_start")
    flight_w2, token_w = _gather_start(list(shards[3:]), token_w, "gather_w2_start")
    bias = _bias_build(rel_bias, buckets, token_w)
    outs, lses = [], []
    for gi in range(len(ATT_GROUPS)):
        o, l = _att_fwd(slabs, bias, gi)
        outs.append(o)
        lses.append(l)
    att, gathered = _mix_fwd(outs, lses, comm=_PassToSibling(*_gather_wait(flight_w1, lses[2], "gather_w1_wait")))
    w_ret_out, w_att_out, w_o = (_from_slots(g, ax) for g, ax in zip(gathered, BIG_AXES[1:4]))
    gated, ro, states = _ret_fwd(proj, tables, gn_g, gn_b, att)
    ret_out, gathered = _mm(gated, w_ret_out, 'nn', tm=S, tn=256, tk=2048, name="ret_out",
                            comm=_PassToSibling(*_gather_wait(flight_w2, gated, "gather_w2_wait")))
    w_ff1, w_ff2 = (_from_slots(g, ax) for g, ax in zip(gathered, BIG_AXES[4:]))
    att_out, merged = _att_out_merge(att, w_att_out, proj, ret_out)
    mixo, x1, h2 = _w_o_norm2(merged, w_o, x, g1, norm2_g, sh2, sc2)
    u, act = _mm(h2, w_ff1, 'nn', tm=S, tn=512, tk=D, name="ff1", relu2=True)
    loss, dx2, g_normf, df, dg2 = _ff2_final(act, w_ff2, x1, g2, tgt, norm_f_g)

    gw_ff2 = _mm(act, df, 'tn', tm=512, tn=D, tk=S, name="gw_ff2", out_dtype=BF16)
    du = _mm(df, w_ff2, 'nt', tm=S, tn=512, tk=D, name="d_act", out_dtype=BF16, relu2_of=u)
    gw_ff1 = _mm(h2, du, 'tn', tm=D, tn=512, tk=S, name="gw_ff1", out_dtype=BF16)
    fulls_a = [_to_slots(g, ax) for g, ax in zip((gw_ff1, gw_ff2), BIG_AXES[4:])]
    dh2, recv_core_a = _mm(du, w_ff1, 'nt', tm=1024, tn=1024, tk=2048, name="dh2", comm=_ExchangeCore(fulls_a))
    parts_a = _reduce_sums(fulls_a, recv_core_a, core, "a")
    flight_a, token_a = _chip_exchange_start(parts_a, "rs_a_start")
    dx1, dsc2, dsh2, g_norm2, dmixo, dg1 = _norm_mod_bwd(x1, norm2_g, sc2, dh2, dx2, "norm2_bwd", gate=(mixo, g1))

    gw_o = _mm(merged, dmixo, 'tn', tm=D, tn=512, tk=S, name="gw_o", out_dtype=BF16, after=token_a)
    d_ret_out, d_att_out, dga, dgb = _dmerged_split(dmixo, w_o, proj, ret_out, att_out)
    gw_ret_out = _mm(gated, d_ret_out, 'tn', tm=512, tn=D, tk=S, name="gw_ret_out", out_dtype=BF16)
    gw_att_out = _mm(att, d_att_out, 'tn', tm=AW, tn=D, tk=S, name="gw_att_out", out_dtype=BF16)
    fulls_b = [_to_slots(g, ax) for g, ax in zip((gw_ret_out, gw_att_out, gw_o), BIG_AXES[1:4])]
    dgated, recv_core_b = _mm(d_ret_out, w_ret_out, 'nt', tm=S, tn=512, tk=D, name="dgated",
                              comm=_ExchangeCore(fulls_b))
    parts_b = _reduce_sums(fulls_b, recv_core_b, core, "b")
    flight_b, token_b = _chip_exchange_start(parts_b, "rs_b_start")
    datt = _mm(d_att_out, w_att_out, 'nt', tm=S, tn=AW, tk=D, name="datt", after=token_b)
    mix_grads = _mix_bwd(outs, lses, datt)
    datt_parts, ds_sums = [], []
    for gi in range(len(ATT_GROUPS)):
        dq, dk, dv, ds_sum = _att_bwd(slabs, bias, outs[gi], lses[gi], mix_grads[gi], mix_grads[3 + gi], gi)
        datt_parts += [dq, dk, dv]
        ds_sums.append(ds_sum)
    g_bias = _bias_grad(jnp.concatenate(ds_sums, axis=0), buckets)[:, :, 0].T.reshape(1, -1)
    dproj, g_gn_g, g_gn_b = _ret_bwd(proj, tables, gn_g, gn_b, ro, states, dgated, datt_parts + [dga, dgb])
    parts_a, recv_chip_a = _chip_exchange_wait(flight_a, dproj, "rs_a_wait")
    parts_b, recv_chip_b = _chip_exchange_wait(flight_b, dproj, "rs_b_wait")
    reduced = list(zip(parts_b + parts_a, recv_chip_b + recv_chip_a))
    full_in = _to_slots(_mm(dproj, h1, 'tn', tm=512, tn=D, tk=S, name="gw_in", out_dtype=BF16), 0)
    halves = [(half * (D // 2), D // 2) for half in range(2)]
    (recv_core_in0,) = _run_comm(_ExchangeCore([full_in], cols=halves[0]), "rs_core_in0")
    flight0, token = _chip_exchange_start([_pair_sum(full_in, recv_core_in0, core, "rs_pair_c0", col_block=0)],
                                          "rs_in0_start")
    dh1, (recv_core_in1,) = _mm(dproj, w_in_t, 'nn', tm=1024, tn=1024, tk=2560, name="dh1",
                                comm=_ExchangeCore([full_in], cols=halves[1]), after=token)
    flight1, token = _chip_exchange_start([_pair_sum(full_in, recv_core_in1, core, "rs_pair_c1", col_block=1)],
                                          "rs_in1_start")
    in_flight = [flight0, flight1]
    gx, dsc1, dsh1, g_norm1 = _norm_mod_bwd(x, norm1_g, sc1, dh1, dx1, "norm1_bwd", after=token)

    dmod = [dsh1, dsc1, dg1, dsh2, dsc2, dg2]
    small_g = [g_norm1, g_bias, g_gn_g, g_gn_b, g_norm2, g_normf]
    return loss, gx, in_flight, reduced, small_g, dmod


def _to_slots(g, axis):
    if axis == 0:
        return g.reshape(4, 2, g.shape[0] // N_DEV, g.shape[1])
    return g.reshape(g.shape[0], N_DEV, g.shape[1] // N_DEV).transpose(1, 0, 2).reshape(4, 2, g.shape[0], -1)


def _from_slots(w8, axis):
    if axis == 0:
        return w8.reshape(-1, w8.shape[2])
    return w8.transpose(1, 0, 2).reshape(w8.shape[1], -1)


BIG_AXES = (1, 0, 1, 0, 1, 0)


def kernel(x, c, w_ada, b_ada, norm1_g, w_in, rel_bias, ret_gn_g, ret_gn_b, w_ret_out, w_att_out, w_o, norm2_g, w_ff1, w_ff2, norm_f_g, loss_target, m_w_ada, m_b_ada, m_norm1_g, m_w_in, m_rel_bias, m_ret_gn_g, m_ret_gn_b, m_w_ret_out, m_w_att_out, m_w_o, m_norm2_g, m_w_ff1, m_w_ff2, m_norm_f_g, v_w_ada, v_b_ada, v_norm1_g, v_w_in, v_rel_bias, v_ret_gn_g, v_ret_gn_b, v_w_ret_out, v_w_att_out, v_w_o, v_norm2_g, v_w_ff1, v_w_ff2, v_norm_f_g):
    mx, my, mc = _mesh_pos()
    dev = 4 * mx + 2 * my + mc
    chip = jnp.reshape(2 * mx + my, (1,)).astype(jnp.int32)
    core = jnp.reshape(mc, (1,)).astype(jnp.int32)
    ada_w = D * 6 // N_DEV

    w_in, m_w_in, v_w_in = (jnp.transpose(t, (0, 2, 1)) for t in (w_in, m_w_in, v_w_in))

    (w_in_shard,), (c_all,) = _to_bf16([w_in[0]], _GatherSmallDirect([c]), "gather_c")
    c_all = c_all.reshape(N_DEV, D)
    b_sl = lax.dynamic_slice(b_ada, (0, dev * ada_w), (1, ada_w))
    other_shards, (mod_all,) = _to_bf16([w[0] for w in (w_ret_out, w_att_out, w_o, w_ff1, w_ff2)],
                                        _GatherSmallDirect([_ada_fwd(c_all, w_ada[0], b_sl)]), "gather_mod")
    mod = lax.dynamic_index_in_dim(mod_all, dev, axis=1, keepdims=False).reshape(6, D)
    mods = tuple(mod[i:i + 1] for i in range(6))

    small = (norm1_g, rel_bias, ret_gn_g, ret_gn_b, norm2_g, norm_f_g.reshape(1, D))
    order = lax.dynamic_index_in_dim(jnp.asarray(_proj_order()), 2 * mx + my, axis=0, keepdims=False)
    loss, gx, in_flight, big_red, small_g, dmod = _local_step(x[0], loss_target[0], mods, w_in_shard, order,
                                                              list(other_shards), small, chip, core)

    names = ['w_ada', 'b_ada', 'norm1_g', 'w_in', 'rel_bias', 'ret_gn_g', 'ret_gn_b', 'w_ret_out', 'w_att_out',
             'w_o', 'norm2_g', 'w_ff1', 'w_ff2', 'norm_f_g']
    ws = dict(zip(names, (w_ada, b_ada, norm1_g, w_in, rel_bias, ret_gn_g, ret_gn_b, w_ret_out, w_att_out, w_o,
                          norm2_g, w_ff1, w_ff2, norm_f_g)))
    ms = dict(zip(names, (m_w_ada, m_b_ada, m_norm1_g, m_w_in, m_rel_bias, m_ret_gn_g, m_ret_gn_b, m_w_ret_out,
                          m_w_att_out, m_w_o, m_norm2_g, m_w_ff1, m_w_ff2, m_norm_f_g)))
    vs = dict(zip(names, (v_w_ada, v_b_ada, v_norm1_g, v_w_in, v_rel_bias, v_ret_gn_g, v_ret_gn_b, v_w_ret_out,
                          v_w_att_out, v_w_o, v_norm2_g, v_w_ff1, v_w_ff2, v_norm_f_g)))
    grads, delta, new_m, new_v = {}, {}, {}, {}
    big_names = ('w_ret_out', 'w_att_out', 'w_o', 'w_ff1', 'w_ff2')
    for n, (part, recv) in zip(big_names, big_red):
        grads[n], delta[n], new_m[n], new_v[n] = _adamw_reduced1(ws[n], ms[n], vs[n], part, recv, chip, "adamw_" + n)
    updated = lax.optimization_barrier((gx, tuple(delta[n] for n in big_names)))
    rows = dmod + small_g + [loss]
    (gathered,) = _run_comm(_GatherSmallDirect([jnp.concatenate(rows, axis=1)]), "gather_small", after=updated[0])
    g_b_ada, dmod_all, (g_norm1, g_bias, g_gn_g, g_gn_b, g_norm2, g_normf, loss_sum) = _sum_small(
        gathered, [r.shape[1] for r in rows[N_MOD:]])
    loss_out = loss_sum[0, 0]
    g_w_ada = _ada_bwd(c_all, lax.dynamic_slice(dmod_all, (0, dev * ada_w), (N_DEV, ada_w)))

    grads.update(w_ada=g_w_ada.reshape(w_ada.shape), b_ada=g_b_ada, norm1_g=g_norm1, rel_bias=g_bias,
                 ret_gn_g=g_gn_g, ret_gn_b=g_gn_b, norm2_g=g_norm2, norm_f_g=g_normf)
    delta['w_ada'], new_m['w_ada'], new_v['w_ada'] = _adamw(w_ada, g_w_ada, m_w_ada, v_w_ada, "adamw_w_ada")
    small_names = ('b_ada', 'norm1_g', 'rel_bias', 'ret_gn_g', 'ret_gn_b', 'norm2_g', 'norm_f_g')
    two_d = {n: (1, ws[n].size) if ws[n].ndim == 1 else ws[n].shape for n in small_names}
    d_, m_, v_ = _adamw_small(*[[src[n].reshape(two_d[n]) for n in small_names] for src in (ws, grads, ms, vs)])
    for i, n in enumerate(small_names):
        shp = ws[n].shape
        delta[n], new_m[n], new_v[n] = d_[i].reshape(shp), m_[i].reshape(shp), v_[i].reshape(shp)
        grads[n] = grads[n].reshape(shp)

    done = lax.optimization_barrier((gx, tuple(d_), tuple(delta[n] for n in ('w_ada', 'w_ret_out', 'w_att_out', 'w_o',
                                                                               'w_ff1', 'w_ff2'))))
    parts_in, recvs_in = [], []
    for half, flight in enumerate(in_flight):
        (part_in,), (recv_chip_in,) = _chip_exchange_wait(flight, done[0], f"rs_in{half}_wait")
        parts_in.append(part_in)
        recvs_in.append(recv_chip_in)
    grads['w_in'], delta['w_in'], new_m['w_in'], new_v['w_in'] = _adamw_reduced(w_in, m_w_in, v_w_in, parts_in,
                                                                               recvs_in, chip)
    for d in (grads, delta, new_m, new_v):
        d['w_in'] = jnp.transpose(d['w_in'], (0, 2, 1))
    return (loss_out, gx[None], *[grads[n] for n in names], *[delta[n] for n in names],
            *[new_m[n] for n in names], *[new_v[n] for n in names])
```

```python
import functools
import math

import numpy as np
import jax
import jax.numpy as jnp
from jax import lax
from jax.experimental import pallas as pl
from jax.experimental.pallas import tpu as pltpu

F32 = jnp.float32
BF16 = jnp.bfloat16
MESH = pl.DeviceIdType.MESH

N_DEV = 8
S = 2048
D = 1024
RET_HEADS = 4
RET_DK = 256
RET_DV = 512
CHUNK = 128
N_CHUNK = S // CHUNK
ATT_GROUPS = ((128, 1), (512, 4), (2048, 16))
ATT_HG = 4
ATT_DH = 128
ATT_BLK = 128
N_BUCKETS = 32
MAX_DIST = 2048
D_FF = 4096
IN_COLS = 12800
OFF_RQ, OFF_RK, OFF_RV, OFF_RG, OFF_ATT = 0, 1024, 2048, 4096, 6144
OFF_GA, OFF_GB = 6144, 7168
RMS_EPS = 1e-6
GN_EPS = 1e-5
ADAM_LR, ADAM_B1, ADAM_B2, ADAM_EPS, ADAM_WD, ADAM_STEP = 0.001, 0.9, 0.999, 1e-08, 0.01, 10
VMEM_LIMIT = 48 * 1024 * 1024


def _pcall(body, **kw):
    return pl.pallas_call(body, **kw)


def _params(sem=None):
    return pltpu.CompilerParams(dimension_semantics=sem, vmem_limit_bytes=VMEM_LIMIT)


HBM_SPEC = pl.BlockSpec(memory_space=pl.ANY)


def _carry(body, comm, *, name, grid, in_specs, out_specs, out_shape, scratch_shapes=()):
    single = not isinstance(out_specs, (tuple, list))
    o_specs = (out_specs,) if single else tuple(out_specs)
    o_shape = (out_shape,) if single else tuple(out_shape)
    n_in, n_out, n_scr = len(in_specs), len(o_specs), len(scratch_shapes)
    nci, nco = len(comm.ins), len(comm.out_shape)
    total = int(np.prod(grid))

    def wrapped(*refs):
        bounds = np.cumsum([0, n_in, nci, n_out, nco, n_scr])
        a, ci, o, co, scr = (refs[bounds[i]:bounds[i + 1]] for i in range(5))
        sems = refs[bounds[5]:]
        flat = 0
        for d, g in enumerate(grid):
            flat = flat * g + pl.program_id(d)

        @pl.when(flat == 0)
        def _():
            comm.start(ci, co, sems)

        body(*a, *o, *scr)

        @pl.when(flat == total - 1)
        def _():
            comm.finish(ci, co, sems)

    aliases = {n_in + i: n_out + o for i, o in getattr(comm, "aliases", {}).items()}
    call = _pcall(wrapped, name=name, grid=grid, in_specs=list(in_specs) + [HBM_SPEC] * nci,
                  out_specs=o_specs + (HBM_SPEC,) * nco, out_shape=o_shape + tuple(comm.out_shape),
                  scratch_shapes=list(scratch_shapes) + list(comm.sems), input_output_aliases=aliases,
                  compiler_params=_params(("arbitrary",) * len(grid)))

    def run(*args):
        res = call(*args, *comm.ins)
        own = res[0] if single else tuple(res[:n_out])
        return own, tuple(res[n_out:])

    return run


def _run_comm(comm, name, after=None):
    nci, nco = len(comm.ins), len(comm.out_shape)
    extra = [] if after is None else [after]

    def body(*refs):
        ci, co, sems = refs[:nci], refs[nci + len(extra):nci + len(extra) + nco], refs[nci + len(extra) + nco:]
        comm.start(ci, co, sems)
        comm.finish(ci, co, sems)

    in_spec = pl.BlockSpec(memory_space=pltpu.VMEM) if getattr(comm, "ins_in_vmem", False) else HBM_SPEC
    return _pcall(body, name=name, in_specs=[in_spec] * nci + [HBM_SPEC] * len(extra), out_specs=(HBM_SPEC,) * nco,
                  out_shape=tuple(comm.out_shape), scratch_shapes=list(comm.sems))(*comm.ins, *extra)


def _dot(a, b, dn):
    return lax.dot_general(a.astype(BF16), b.astype(BF16), (dn, ((), ())), preferred_element_type=F32)


NN = ((1,), (0,))
NT = ((1,), (1,))
TN = ((0,), (0,))


def _mm(a, b, mode, *, tm, tn, tk, name, out_dtype=F32, res=None, gvec=None, relu2=False, relu2_of=None, comm=None,
        after=None):
    if mode == 'nn':
        (M, K), (_, N) = a.shape, b.shape
        a_spec = pl.BlockSpec((tm, tk), lambda i, j, k: (i, k))
        b_spec = pl.BlockSpec((tk, tn), lambda i, j, k: (k, j))
        dn = NN
    elif mode == 'nt':
        (M, K), (N, _) = a.shape, b.shape
        a_spec = pl.BlockSpec((tm, tk), lambda i, j, k: (i, k))
        b_spec = pl.BlockSpec((tn, tk), lambda i, j, k: (j, k))
        dn = NT
    else:
        (K, M), (_, N) = a.shape, b.shape
        a_spec = pl.BlockSpec((tk, tm), lambda i, j, k: (k, i))
        b_spec = pl.BlockSpec((tk, tn), lambda i, j, k: (k, j))
        dn = TN
    assert M % tm == 0 and N % tn == 0 and K % tk == 0, (name, M, N, K)
    nk = K // tk
    fused = res is not None
    o_spec = pl.BlockSpec((tm, tn), lambda i, j, k: (i, j))

    def body(a_ref, b_ref, *rest):
        acc_ref = rest[-1] if nk > 1 else None
        if after is not None:
            rest = rest[1:]
        if fused:
            res_ref, g_ref, o_ref, x_ref = rest[:4]
        elif relu2_of is not None:
            u_ref, o_ref = rest[:2]
        elif relu2:
            o_ref, act_ref = rest[:2]
        else:
            o_ref = rest[0]

        def finish(acc):
            if relu2_of is not None:
                acc = acc * (2.0 * jnp.maximum(u_ref[...], 0.0))
            o_ref[...] = acc.astype(o_ref.dtype)
            if fused:
                x_ref[...] = res_ref[...] + g_ref[...] * acc
            if relu2:
                r = jnp.maximum(acc, 0.0)
                act_ref[...] = (r * r).astype(BF16)

        p = _dot(a_ref[...], b_ref[...], dn)
        if nk == 1:
            finish(p)
        else:
            k = pl.program_id(2)

            @pl.when(k == 0)
            def _():
                acc_ref[...] = p

            @pl.when(k > 0)
            def _():
                acc_ref[...] += p

            @pl.when(k == nk - 1)
            def _():
                finish(acc_ref[...])

    in_specs = [a_spec, b_spec]
    args = [a, b]
    if after is not None:
        in_specs.append(pl.BlockSpec(memory_space=pl.ANY))
        args.append(after)
    out_shape = jax.ShapeDtypeStruct((M, N), out_dtype)
    out_specs = o_spec
    if fused:
        in_specs += [pl.BlockSpec((tm, tn), lambda i, j, k: (i, j)), pl.BlockSpec((1, tn), lambda i, j, k: (0, j))]
        args += [res, gvec]
        out_shape = (out_shape, jax.ShapeDtypeStruct((M, N), F32))
        out_specs = (o_spec, pl.BlockSpec((tm, tn), lambda i, j, k: (i, j)))
    elif relu2_of is not None:
        in_specs.append(pl.BlockSpec((tm, tn), lambda i, j, k: (i, j)))
        args.append(relu2_of)
    elif relu2:
        out_shape = (out_shape, jax.ShapeDtypeStruct((M, N), BF16))
        out_specs = (o_spec, pl.BlockSpec((tm, tn), lambda i, j, k: (i, j)))
    kw = dict(name=name, grid=(M // tm, N // tn, nk), in_specs=in_specs, out_specs=out_specs,
              out_shape=out_shape, scratch_shapes=[pltpu.VMEM((tm, tn), F32)] if nk > 1 else [])
    if comm is not None:
        return _carry(body, comm, **kw)(*args)
    return _pcall(body, compiler_params=_params(("parallel", "parallel", "arbitrary")), **kw)(*args)


PROJ_TN = 512
ATT_T0, ATT_T1 = 6144 // PROJ_TN, 10752 // PROJ_TN
N_SLABS = (ATT_T1 - ATT_T0) * 4
MAIN_COLS = IN_COLS - (ATT_T1 - ATT_T0) * PROJ_TN


PROJ_TILES = IN_COLS // PROJ_TN
SHARD_ROWS = IN_COLS // N_DEV
W_CHUNKS = 4
N_OWN, N_NEAR = 5, 18


def _proj_order():
    out = np.zeros((4, 3, PROJ_TILES), np.int32)
    for q in range(4):
        def hops(t):
            owners = {col // (2 * SHARD_ROWS) for col in (t * PROJ_TN, (t + 1) * PROJ_TN - 1)}
            return max(bin(q ^ p).count("1") for p in owners)
        order = sorted(range(PROJ_TILES), key=lambda t: (hops(t), t))
        assert all(hops(t) == 0 for t in order[:N_OWN]) and all(hops(t) < 2 for t in order[:N_NEAR])
        is_att = [ATT_T0 <= t < ATT_T1 for t in order]
        for row, kind, index in ((1, False, lambda t: t if t < ATT_T0 else t - (ATT_T1 - ATT_T0)),
                                 (2, True, lambda t: t - ATT_T0)):
            own = [index(t) if a == kind else None for t, a in zip(order, is_att)]
            first = next(v for v in own if v is not None)
            last = first
            for j, v in enumerate(own):
                last = last if v is None else v
                out[q, row, j] = last
        out[q, 0] = order
    return out


def _gather_proj(x, g, sh, sc, shard, order):
    rows = SHARD_ROWS // W_CHUNKS

    def body(ord_ref, x_ref, g_ref, shift_ref, scale_ref, sh_ref, main_ref, slab_ref, full_ref, a_ref, wbuf, xbuf,
             fetch_sems, send_sems, recv_sems, local_sems, x_sem):
        j = pl.program_id(0)
        x, y, c = _mesh_pos()
        me, sibling = (x, y, c), (x, y, 1 - c)
        chips = [(1 - x, y), (x, 1 - y), (1 - x, 1 - y)]

        def block(p, owner):
            return full_ref.at[pl.ds(pl.multiple_of(_slot(owner) * SHARD_ROWS + p * rows, 16), rows)]

        def copy(p, k, owner, to, from_input=False):
            dst = block(p, owner)
            return pltpu.make_async_remote_copy(
                src_ref=sh_ref.at[pl.ds(p * rows, rows)] if from_input else dst, dst_ref=dst,
                send_sem=send_sems.at[7 * p + k], recv_sem=recv_sems.at[7 * p + k], device_id=to, device_id_type=MESH)

        pieces = range(W_CHUNKS)
        mine = [pltpu.make_async_copy(sh_ref.at[pl.ds(p * rows, rows)], block(p, me), local_sems.at[p]) for p in pieces]
        first = [copy(p, 0, me, sibling, from_input=True) for p in pieces]
        first += [copy(p, 1 + n, me, (*chips[n], c), from_input=True) for p in pieces for n in range(2)]
        near_pass = [copy(p, 4 + n, (*chips[n], c), sibling) for p in pieces for n in range(2)]
        relay = [copy(p, 3, ((x + 1 - c) % 2, (y + c) % 2, c), ((x + c) % 2, (y + 1 - c) % 2, c)) for p in pieces]
        far_pass = [copy(p, 6, (*chips[2], c), sibling) for p in pieces]

        def fetch(pos):
            slot = lax.rem(pos, 2)
            start = pl.multiple_of(ord_ref[0, pos] * PROJ_TN, PROJ_TN)
            return pltpu.make_async_copy(full_ref.at[pl.ds(start, PROJ_TN)], wbuf.at[slot], fetch_sems.at[slot])

        @pl.when(j == 0)
        def _():
            x_copy = pltpu.make_async_copy(x_ref, xbuf, x_sem.at[0])
            x_copy.start()
            for cp in mine + first:
                cp.start()
            x_copy.wait()
            for r in range(S // TR):
                rws = pl.ds(r * TR, TR)
                xv = xbuf[rws, :]
                rstd = lax.rsqrt(jnp.mean(xv * xv, axis=-1, keepdims=True) + RMS_EPS)
                n = xv * rstd * g_ref[...]
                a_ref[rws, :] = (n * (1.0 + scale_ref[...]) + shift_ref[...]).astype(BF16)
            for cp in mine:
                cp.wait()
            for p in pieces:
                copy(p, 0, sibling, me).wait_recv()
            fetch(j).start()

        @pl.when(j == N_OWN - 1)
        def _():
            for p in pieces:
                for n in range(2):
                    copy(p, 1 + n, (*chips[n], c), me).wait_recv()
                    near_pass[2 * p + n].start()
                relay[p].start()
            for p in pieces:
                for n in range(2):
                    copy(p, 4 + n, (*chips[n], 1 - c), me).wait_recv()

        @pl.when(j == N_NEAR - 1)
        def _():
            for p in pieces:
                copy(p, 3, (*chips[2], c), me).wait_recv()
                far_pass[p].start()
            for p in pieces:
                copy(p, 6, (*chips[2], 1 - c), me).wait_recv()

        @pl.when(j + 1 < PROJ_TILES)
        def _():
            fetch(j + 1).start()

        fetch(j).wait()
        w_ref = wbuf.at[lax.rem(j, 2)]
        tile = ord_ref[0, j]
        is_att = (tile >= ATT_T0) & (tile < ATT_T1)
        chunks = [pl.ds(r * 512, 512) for r in range(S // 512)]

        @pl.when(jnp.logical_not(is_att))
        def _():
            for rws in chunks:
                main_ref[rws, :] = _dot(a_ref[rws, :], w_ref[...], NT)

        @pl.when(is_att)
        def _():
            for rws in chunks:
                p = _dot(a_ref[rws, :], w_ref[...], NT)
                for h in range(4):
                    slab_ref[h, rws, :] = p[:, h * 128:(h + 1) * 128]

        @pl.when(j == PROJ_TILES - 1)
        def _():
            for cp in first + near_pass + relay + far_pass:
                cp.wait_send()

    vec = pl.BlockSpec((1, D), lambda j, o: (0, 0))
    gs = pltpu.PrefetchScalarGridSpec(
        num_scalar_prefetch=1, grid=(PROJ_TILES,),
        in_specs=[HBM_SPEC, vec, vec, vec, HBM_SPEC],
        out_specs=(pl.BlockSpec((S, PROJ_TN), lambda j, o: (0, o[1, j])),
                   pl.BlockSpec((4, S, 128), lambda j, o: (o[2, j], 0, 0)), HBM_SPEC,
                   pl.BlockSpec((S, D), lambda j, o: (0, 0))),
        scratch_shapes=[pltpu.VMEM((2, PROJ_TN, D), BF16), pltpu.VMEM((S, D), F32), pltpu.SemaphoreType.DMA((2,)),
                        pltpu.SemaphoreType.DMA((7 * W_CHUNKS,)), pltpu.SemaphoreType.DMA((7 * W_CHUNKS,)),
                        pltpu.SemaphoreType.DMA((W_CHUNKS,)), pltpu.SemaphoreType.DMA((1,))])
    return _pcall(body, name="gather_proj", grid_spec=gs,
                  out_shape=(jax.ShapeDtypeStruct((S, MAIN_COLS), F32), jax.ShapeDtypeStruct((N_SLABS, S, 128), F32),
                             jax.ShapeDtypeStruct((IN_COLS, D), BF16), jax.ShapeDtypeStruct((S, D), BF16)),
                  compiler_params=_params(("arbitrary",)))(order, x, g, sh, sc, shard)


TR = 256


def _row_spec(w=D):
    return pl.BlockSpec((TR, w), lambda i: (i, 0))


def _vec_spec(w=D):
    return pl.BlockSpec((1, w), lambda i: (0, 0))


def _norm_mod_bwd(x, g, sc, dh, dres, name, gate=None, after=None):
    gated = gate is not None

    def body(x_ref, g_ref, sc_ref, dh_ref, dres_ref, *rest):
        if after is not None:
            rest = rest[1:]
        if gated:
            f_ref, gv_ref, dx_ref, dsc_ref, dsh_ref, dg_ref, dz_ref, dgv_ref = rest
        else:
            dx_ref, dsc_ref, dsh_ref, dg_ref = rest
        i = pl.program_id(0)
        xv = x_ref[...]
        dh = dh_ref[...]
        rstd = lax.rsqrt(jnp.mean(xv * xv, axis=-1, keepdims=True) + RMS_EPS)
        xhat = xv * rstd
        gv = g_ref[...]
        dn = dh * (1.0 + sc_ref[...])
        dxhat = dn * gv
        dx = dres_ref[...] + rstd * (dxhat - xhat * jnp.mean(dxhat * xhat, axis=-1, keepdims=True))
        dx_ref[...] = dx
        sums = [(dsc_ref, jnp.sum(dh * (xhat * gv), axis=0, keepdims=True)),
                (dsh_ref, jnp.sum(dh, axis=0, keepdims=True)),
                (dg_ref, jnp.sum(dn * xhat, axis=0, keepdims=True))]
        if gated:
            dz_ref[...] = (dx * gv_ref[...]).astype(BF16)
            sums.append((dgv_ref, jnp.sum(dx * f_ref[...], axis=0, keepdims=True)))

        @pl.when(i == 0)
        def _():
            for ref, p in sums:
                ref[...] = p

        @pl.when(i > 0)
        def _():
            for ref, p in sums:
                ref[...] += p

    vec = jax.ShapeDtypeStruct((1, D), F32)
    in_specs = [_row_spec(), _vec_spec(), _vec_spec(), _row_spec(), _row_spec()]
    out_specs = [_row_spec(), _vec_spec(), _vec_spec(), _vec_spec()]
    out_shape = [jax.ShapeDtypeStruct((S, D), F32), vec, vec, vec]
    args = [x, g, sc, dh, dres]
    if after is not None:
        in_specs.append(HBM_SPEC)
        args.append(after)
    if gated:
        in_specs += [_row_spec(), _vec_spec()]
        out_specs += [_row_spec(), _vec_spec()]
        out_shape += [jax.ShapeDtypeStruct((S, D), BF16), vec]
        args += list(gate)
    return _pcall(body, name=name, grid=(S // TR,), in_specs=in_specs, out_specs=tuple(out_specs),
                  out_shape=tuple(out_shape), compiler_params=_params(("arbitrary",)))(*args)


def _w_o_norm2(merged, w_o, x, g1, g, sh, sc):
    def body(a_ref, b_ref, x_ref, g1_ref, g_ref, sh_ref, sc_ref, o_ref, x1_ref, h_ref):
        acc = _dot(a_ref[...], b_ref[...], NN)
        o_ref[...] = acc
        xv = x_ref[...] + g1_ref[...] * acc
        x1_ref[...] = xv
        rstd = lax.rsqrt(jnp.mean(xv * xv, axis=-1, keepdims=True) + RMS_EPS)
        h_ref[...] = (xv * rstd * g_ref[...] * (1.0 + sc_ref[...]) + sh_ref[...]).astype(BF16)

    rows = pl.BlockSpec((FF2_TM, D), lambda i: (i, 0))
    f32 = jax.ShapeDtypeStruct((S, D), F32)
    return _pcall(body, name="w_o_norm2", grid=(S // FF2_TM,),
                  in_specs=[rows, pl.BlockSpec((D, D), lambda i: (0, 0)), rows] + [_vec_spec()] * 4,
                  out_specs=(rows, rows, rows), out_shape=(f32, f32, jax.ShapeDtypeStruct((S, D), BF16)),
                  compiler_params=_params(("parallel",)))(merged, w_o, x, g1, g, sh, sc)


FF2_TM = 512


def _ff2_final(act, w_ff2, x1, g2, tgt, g):
    def body(a_ref, b_ref, x1_ref, g2_ref, t_ref, g_ref, loss_ref, dx_ref, dg_ref, df_ref, dg2_ref):
        i = pl.program_id(0)
        f = _dot(a_ref[...], b_ref[...], NN)
        g2v = g2_ref[...]
        xv = x1_ref[...] + g2v * f
        gv = g_ref[...]
        rstd = lax.rsqrt(jnp.mean(xv * xv, axis=-1, keepdims=True) + RMS_EPS)
        xhat = xv * rstd
        err = xhat * gv - t_ref[...]
        dy = err * (1.0 / D)
        dxhat = dy * gv
        dx = rstd * (dxhat - xhat * jnp.mean(dxhat * xhat, axis=-1, keepdims=True))
        dx_ref[...] = dx
        df_ref[...] = (dx * g2v).astype(BF16)
        p_g = jnp.sum(dy * xhat, axis=0, keepdims=True)
        p_g2 = jnp.sum(dx * f, axis=0, keepdims=True)
        p_l = jnp.zeros((1, 128), F32) + 0.5 * jnp.sum(jnp.mean(err * err, axis=-1, keepdims=True))

        @pl.when(i == 0)
        def _():
            dg_ref[...] = p_g
            dg2_ref[...] = p_g2
            loss_ref[...] = p_l

        @pl.when(i > 0)
        def _():
            dg_ref[...] += p_g
            dg2_ref[...] += p_g2
            loss_ref[...] += p_l

    vec = jax.ShapeDtypeStruct((1, D), F32)
    rows = lambda w: pl.BlockSpec((FF2_TM, w), lambda i: (i, 0))
    return _pcall(body, name="ff2_final", grid=(S // FF2_TM,),
                  in_specs=[rows(D_FF), pl.BlockSpec((D_FF, D), lambda i: (0, 0)), rows(D), _vec_spec(), rows(D),
                            _vec_spec()],
                  out_specs=(_vec_spec(128), rows(D), _vec_spec(), rows(D), _vec_spec()),
                  out_shape=(jax.ShapeDtypeStruct((1, 128), F32), jax.ShapeDtypeStruct((S, D), F32), vec,
                             jax.ShapeDtypeStruct((S, D), BF16), vec),
                  compiler_params=_params(("arbitrary",)))(act, w_ff2, x1, g2, tgt, g)


HALF = 512


MERGE_TM = 1024


def _merge_specs():
    blk = lambda off: pl.BlockSpec((MERGE_TM, HALF), lambda i, j: (i, off // HALF + j))
    return blk(OFF_GA), blk(OFF_GB), blk(0)


def _att_out_merge(att, w_att_out, proj, ret_out):
    def body(a_ref, b_ref, ga_ref, gb_ref, r_ref, o_ref, m_ref):
        acc = _dot(a_ref[...], b_ref[...], NN)
        o_ref[...] = acc
        m_ref[...] = (jax.nn.sigmoid(ga_ref[...]) * r_ref[...] + jax.nn.sigmoid(gb_ref[...]) * acc).astype(BF16)

    ga, gb, tile = _merge_specs()
    return _pcall(body, name="att_out", grid=(S // MERGE_TM, D // HALF),
                  in_specs=[pl.BlockSpec((MERGE_TM, AW), lambda i, j: (i, 0)), pl.BlockSpec((AW, HALF), lambda i, j: (0, j)),
                            ga, gb, tile],
                  out_specs=(tile, tile),
                  out_shape=(jax.ShapeDtypeStruct((S, D), F32), jax.ShapeDtypeStruct((S, D), BF16)),
                  compiler_params=_params(("parallel", "parallel")))(att, w_att_out, proj, proj, ret_out)


def _dmerged_split(dmixo, w_o, proj, ret_out, att_out):
    def body(a_ref, b_ref, ga_ref, gb_ref, r_ref, at_ref, dr_ref, da_ref, dga_ref, dgb_ref):
        dm = _dot(a_ref[...], b_ref[...], NT)
        sa = jax.nn.sigmoid(ga_ref[...])
        sb = jax.nn.sigmoid(gb_ref[...])
        dr_ref[...] = (dm * sa).astype(BF16)
        da_ref[...] = (dm * sb).astype(BF16)
        dga_ref[...] = (dm * r_ref[...] * (sa * (1.0 - sa))).astype(BF16)
        dgb_ref[...] = (dm * at_ref[...] * (sb * (1.0 - sb))).astype(BF16)

    ga, gb, tile = _merge_specs()
    o = jax.ShapeDtypeStruct((S, D), BF16)
    return _pcall(body, name="dmerged", grid=(S // MERGE_TM, D // HALF),
                  in_specs=[pl.BlockSpec((MERGE_TM, D), lambda i, j: (i, 0)), pl.BlockSpec((HALF, D), lambda i, j: (j, 0)),
                            ga, gb, tile, tile],
                  out_specs=(tile,) * 4, out_shape=(o, o, o, o),
                  compiler_params=_params(("parallel", "parallel")))(dmixo, w_o, proj, proj, ret_out, att_out)


def _ret_tables():
    H, C = RET_HEADS, CHUNK
    log_g = jnp.log1p(-(2.0 ** (-5.0 - jnp.arange(H, dtype=F32))))
    idx = jnp.arange(C, dtype=F32)
    rel = idx[:, None] - idx[None, :]
    inner = jnp.where(rel >= 0, jnp.exp(log_g[:, None, None] * jnp.maximum(rel, 0.0)), 0.0)
    qd = jnp.exp(log_g[:, None] * (idx + 1.0))[:, :, None]
    kd = jnp.exp(log_g[:, None] * (C - 1.0 - idx))[:, :, None]
    cd = jnp.broadcast_to(jnp.exp(log_g * C)[:, None, None], (H, 1, 128))
    half = RET_DK // 2
    inv = 10000.0 ** (-jnp.arange(half, dtype=F32) / half)
    ang = jnp.arange(S, dtype=F32)[:, None] * inv[None, :]
    return inner, qd, kd, cd, jnp.cos(ang), jnp.sin(ang)


def _rot(x, cos, sin):
    x1, x2 = x[:, :128], x[:, 128:]
    return jnp.concatenate([x1 * cos - x2 * sin, x1 * sin + x2 * cos], axis=1)


def _rot_t(d, cos, sin):
    d1, d2 = d[:, :128], d[:, 128:]
    return jnp.concatenate([d1 * cos + d2 * sin, d2 * cos - d1 * sin], axis=1)


RET_COLS = OFF_ATT
RET_VW = RET_HEADS * RET_DV


def _ret_specs(chunk_of):
    ci = chunk_of
    whole = lambda shape: pl.BlockSpec(shape, lambda t: (0,) * len(shape))
    return [
        pl.BlockSpec((CHUNK, RET_COLS), lambda t: (ci(t), 0)),
        pl.BlockSpec((CHUNK, 128), lambda t: (ci(t), 0)),
        pl.BlockSpec((CHUNK, 128), lambda t: (ci(t), 0)),
        whole((RET_HEADS, CHUNK, CHUNK)), whole((RET_HEADS, CHUNK, 1)), whole((RET_HEADS, CHUNK, 1)),
        whole((RET_HEADS, 1, 128)), whole((1, RET_VW)), whole((1, RET_VW)),
    ]


def _ret_cols(h):
    q = slice(OFF_RQ + h * RET_DK, OFF_RQ + (h + 1) * RET_DK)
    k = slice(OFF_RK + h * RET_DK, OFF_RK + (h + 1) * RET_DK)
    v = slice(OFF_RV + h * RET_DV, OFF_RV + (h + 1) * RET_DV)
    g = slice(OFF_RG + h * RET_DV, OFF_RG + (h + 1) * RET_DV)
    return q, k, v, g, slice(h * RET_DV, (h + 1) * RET_DV)


def _ret_fwd(proj, tables, gn_g, gn_b, after):
    inner, qd, kd, cd, cos, sin = tables

    def body(x_ref, cos_ref, sin_ref, in_ref, qd_ref, kd_ref, cd_ref, g_ref, b_ref, after_ref,
             gated_ref, ro_ref, st_ref, s_scr):
        i = pl.program_id(0)

        @pl.when(i == 0)
        def _():
            s_scr[...] = jnp.zeros_like(s_scr)

        cosv, sinv = cos_ref[...], sin_ref[...]
        for h in range(RET_HEADS):
            cq, ck, cv, cg, co = _ret_cols(h)
            q = _rot(x_ref[:, cq], cosv, sinv)
            k = _rot(x_ref[:, ck], cosv, sinv) * (RET_DK ** -0.5)
            v = x_ref[:, cv]
            st = s_scr[h]
            st_ref[h] = st.astype(BF16)
            s = _dot(q, k, NT) * in_ref[h]
            o = _dot(s, v, NN) + _dot(q, st, NN) * qd_ref[h]
            s_scr[h] = st * cd_ref[h, :, :1] + _dot(k * kd_ref[h], v, TN)
            ro_ref[:, co] = o
            mu = jnp.mean(o, axis=-1, keepdims=True)
            oc = o - mu
            var = jnp.mean(oc * oc, axis=-1, keepdims=True)
            rn = oc * lax.rsqrt(var + GN_EPS) * g_ref[:, co] + b_ref[:, co]
            rg = x_ref[:, cg]
            gated_ref[:, co] = (rg * jax.nn.sigmoid(rg) * rn).astype(BF16)

    ospec = pl.BlockSpec((CHUNK, RET_VW), lambda t: (t, 0))
    return _pcall(
        body, name="ret_fwd", grid=(N_CHUNK,), in_specs=_ret_specs(lambda t: t) + [HBM_SPEC],
        out_specs=(ospec, ospec, pl.BlockSpec((RET_HEADS, None, RET_DK, RET_DV), lambda t: (0, t, 0, 0))),
        out_shape=(jax.ShapeDtypeStruct((S, RET_VW), BF16), jax.ShapeDtypeStruct((S, RET_VW), F32),
                   jax.ShapeDtypeStruct((RET_HEADS, N_CHUNK, RET_DK, RET_DV), BF16)),
        scratch_shapes=[pltpu.VMEM((RET_HEADS, RET_DK, RET_DV), F32)],
        compiler_params=_params(("arbitrary",)))(proj, cos, sin, inner, qd, kd, cd, gn_g, gn_b, after)


def _ret_bwd(proj, tables, gn_g, gn_b, ro, states, dgated, others):
    inner, qd, kd, cd, cos, sin = tables
    last = N_CHUNK - 1
    pieces = lambda o: [(h, o.shape[2]) for h in range(o.shape[0])] if len(o.shape) == 3 else [(None, o.shape[1])]
    assert RET_COLS + sum(w for o in others for _, w in pieces(o)) == IN_COLS

    def body(x_ref, cos_ref, sin_ref, in_ref, qd_ref, kd_ref, cd_ref, g_ref, b_ref, ro_ref, st_ref, dg_ref, *rest):
        other_refs, (dx_ref, gg_ref, gb_ref, gs_scr) = rest[:len(others)], rest[len(others):]
        t = pl.program_id(0)
        col = RET_COLS
        for o_ref in other_refs:
            for h, w in pieces(o_ref):
                dx_ref[:, col:col + w] = (o_ref[...] if h is None else o_ref[h]).astype(BF16)
                col += w

        @pl.when(t == 0)
        def _():
            gs_scr[...] = jnp.zeros_like(gs_scr)
            gg_ref[...] = jnp.zeros_like(gg_ref)
            gb_ref[...] = jnp.zeros_like(gb_ref)

        cosv, sinv = cos_ref[...], sin_ref[...]
        for h in range(RET_HEADS):
            cq, ck, cv, cg, co = _ret_cols(h)
            q = _rot(x_ref[:, cq], cosv, sinv)
            k = _rot(x_ref[:, ck], cosv, sinv) * (RET_DK ** -0.5)
            v = x_ref[:, cv]
            qdv, kdv, dm = qd_ref[h], kd_ref[h], in_ref[h]
            st = st_ref[h]
            o = ro_ref[:, co]
            gv = g_ref[:, co]
            mu = jnp.mean(o, axis=-1, keepdims=True)
            oc = o - mu
            rstd = lax.rsqrt(jnp.mean(oc * oc, axis=-1, keepdims=True) + GN_EPS)
            ohat = oc * rstd
            rn = ohat * gv + b_ref[:, co]
            rg = x_ref[:, cg]
            sg = jax.nn.sigmoid(rg)
            dgt = dg_ref[:, co]
            drn = dgt * (rg * sg)
            dx_ref[:, cg] = (dgt * rn * (sg * (1.0 + rg * (1.0 - sg)))).astype(BF16)
            gg_ref[:, co] += jnp.sum(drn * ohat, axis=0, keepdims=True)
            gb_ref[:, co] += jnp.sum(drn, axis=0, keepdims=True)
            dohat = drn * gv
            do = rstd * (dohat - jnp.mean(dohat, axis=-1, keepdims=True)
                         - ohat * jnp.mean(dohat * ohat, axis=-1, keepdims=True))
            gs = gs_scr[h]
            s = _dot(q, k, NT) * dm
            dsr = _dot(do, v, NT) * dm
            dq = _dot(dsr, k, NN) + _dot(do, st, NT) * qdv
            dk = _dot(dsr, q, TN) + _dot(v, gs, NT) * kdv
            dv = _dot(s, do, TN) + _dot(k * kdv, gs, NN)
            gs_scr[h] = gs * cd_ref[h, :, :1] + _dot(q * qdv, do, TN)
            dx_ref[:, cq] = _rot_t(dq, cosv, sinv).astype(BF16)
            dx_ref[:, ck] = (_rot_t(dk, cosv, sinv) * (RET_DK ** -0.5)).astype(BF16)
            dx_ref[:, cv] = dv.astype(BF16)

    rev = lambda t: last - t
    vblk = pl.BlockSpec((CHUNK, RET_VW), lambda t: (rev(t), 0))
    vspec = pl.BlockSpec((1, RET_VW), lambda t: (0, 0))
    rows = lambda w: pl.BlockSpec((CHUNK, w), lambda t: (rev(t), 0))
    return _pcall(
        body, name="ret_bwd", grid=(N_CHUNK,),
        in_specs=_ret_specs(rev) + [vblk, pl.BlockSpec((RET_HEADS, None, RET_DK, RET_DV), lambda t: (0, rev(t), 0, 0)),
                                    vblk] + [rows(o.shape[1]) if o.ndim == 2 else
                                             pl.BlockSpec((o.shape[0], CHUNK, o.shape[2]), lambda t: (0, rev(t), 0))
                                             for o in others],
        out_specs=(rows(IN_COLS), vspec, vspec),
        out_shape=(jax.ShapeDtypeStruct((S, IN_COLS), BF16), jax.ShapeDtypeStruct((1, RET_VW), F32),
                   jax.ShapeDtypeStruct((1, RET_VW), F32)),
        scratch_shapes=[pltpu.VMEM((RET_HEADS, RET_DK, RET_DV), F32)],
        compiler_params=_params(("arbitrary",)))(proj, cos, sin, inner, qd, kd, cd, gn_g, gn_b, ro, states, dgated,
                                                 *others)


def _bucket_tables():
    qi = np.arange(ATT_BLK)[:, None]
    kj = np.arange(2 * ATT_BLK)[None, :]
    m = ATT_BLK + qi - kj
    out = []
    for win, dil in ATT_GROUPS:
        w = win // dil
        dist = (np.clip(m, 0, w) * dil).astype(np.int32)
        max_exact = N_BUCKETS // 2
        d_f = np.maximum(dist, 1).astype(np.float32)
        large = max_exact + (np.log(d_f / np.float32(max_exact)) / np.float32(math.log(MAX_DIST / max_exact))
                             * np.float32(N_BUCKETS - max_exact)).astype(np.int32)
        large = np.minimum(large, N_BUCKETS - 1)
        out.append(np.where(dist < max_exact, dist, large).astype(np.int32))
    return np.stack(out)


def _bias_build(rel_bias, buckets, after):
    def body(tab_ref, bk_ref, after_ref, o_ref):
        hh = pl.program_id(0)
        bk = bk_ref[...]
        acc = jnp.zeros((ATT_BLK, 2 * ATT_BLK), F32)
        for b in range(N_BUCKETS):
            acc = jnp.where(bk == b, tab_ref[b, hh], acc)
        o_ref[...] = acc

    nh = len(ATT_GROUPS) * ATT_HG
    return _pcall(body, name="bias_build", grid=(nh,),
                  in_specs=[pl.BlockSpec(memory_space=pltpu.SMEM),
                            pl.BlockSpec((None, ATT_BLK, 2 * ATT_BLK), lambda hh: (hh // ATT_HG, 0, 0)), HBM_SPEC],
                  out_specs=pl.BlockSpec((None, ATT_BLK, 2 * ATT_BLK), lambda hh: (hh, 0, 0)),
                  out_shape=jax.ShapeDtypeStruct((nh, ATT_BLK, 2 * ATT_BLK), F32),
                  compiler_params=_params(("parallel",)))(rel_bias, buckets, after)


def _bias_grad(ds_sum, buckets):
    def body(ds_ref, bk_ref, o_ref):
        bk = bk_ref[...]
        ds = ds_ref[...]
        rows = lax.broadcasted_iota(jnp.int32, (N_BUCKETS, 128), 0)
        acc = jnp.zeros((N_BUCKETS, 128), F32)
        for b in range(N_BUCKETS):
            acc = jnp.where(rows == b, jnp.sum(jnp.where(bk == b, ds, 0.0)), acc)
        o_ref[...] = acc

    nh = len(ATT_GROUPS) * ATT_HG
    return _pcall(body, name="bias_grad", grid=(nh,),
                  in_specs=[pl.BlockSpec((None, ATT_BLK, 2 * ATT_BLK), lambda hh: (hh, 0, 0)),
                            pl.BlockSpec((None, ATT_BLK, 2 * ATT_BLK), lambda hh: (hh // ATT_HG, 0, 0))],
                  out_specs=pl.BlockSpec((None, N_BUCKETS, 128), lambda hh: (hh, 0, 0)),
                  out_shape=jax.ShapeDtypeStruct((nh, N_BUCKETS, 128), F32),
                  compiler_params=_params(("parallel",)))(ds_sum, buckets)


def _att_valid(n):
    qi = lax.broadcasted_iota(jnp.int32, (ATT_BLK, 2 * ATT_BLK), 0)
    kj = lax.broadcasted_iota(jnp.int32, (ATT_BLK, 2 * ATT_BLK), 1)
    m = ATT_BLK + qi - kj
    first_key = jnp.where(n > 0, 0, ATT_BLK)
    return (m >= 0) & (m <= ATT_BLK) & (kj >= first_key)


ATT_HP = (1, 2, 2)


def _att_geometry(gi):
    _, dil = ATT_GROUPS[gi]
    return dil, S // dil // ATT_BLK, ATT_HP[gi]


def _blk(dil, r, n):
    if dil == 1:
        return pl.ds(n * ATT_BLK, ATT_BLK)
    return pl.ds(r + n * ATT_BLK * dil, ATT_BLK, stride=dil)


def _slab_specs(gi):
    _, _, hp = _att_geometry(gi)
    per = ATT_HG // hp
    return [pl.BlockSpec((hp, S, ATT_DH), lambda g, r, part=part: ((3 * gi + part) * per + g, 0, 0))
            for part in range(3)]


def _head_specs(gi, count):
    _, _, hp = _att_geometry(gi)
    return [pl.BlockSpec((hp, S, ATT_DH), lambda g, r: (g, 0, 0))] * count


def _bias_spec(gi):
    _, _, hp = _att_geometry(gi)
    return pl.BlockSpec((hp, ATT_BLK, 2 * ATT_BLK), lambda g, r: (gi * (ATT_HG // hp) + g, 0, 0))


def _att_valid_first():
    qi = lax.broadcasted_iota(jnp.int32, (ATT_BLK, ATT_BLK), 0)
    kj = lax.broadcasted_iota(jnp.int32, (ATT_BLK, ATT_BLK), 1)
    return kj <= qi


def _att_fwd(slabs, bias, gi, comm=None):
    dil, nb, hp = _att_geometry(gi)
    scale = ATT_DH ** -0.5

    def body(q_ref, k_ref, v_ref, bias_ref, o_ref, l_ref):
        r = pl.program_id(1)
        for n in range(nb):
            cur = _blk(dil, r, n)
            valid = _att_valid(n) if n > 0 else _att_valid_first()
            for h in range(hp):
                if n > 0:
                    prev = _blk(dil, r, n - 1)
                    kk = jnp.concatenate([k_ref[h, prev, :], k_ref[h, cur, :]], axis=0)
                    vv = jnp.concatenate([v_ref[h, prev, :], v_ref[h, cur, :]], axis=0)
                    bias = bias_ref[h]
                else:
                    kk, vv, bias = k_ref[h, cur, :], v_ref[h, cur, :], bias_ref[h, :, pl.ds(ATT_BLK, ATT_BLK)]
                s = _dot(q_ref[h, cur, :], kk, NT) * scale + bias
                s = jnp.where(valid, s, -1e30)
                mx = jnp.max(s, axis=-1, keepdims=True)
                e = jnp.exp(s - mx)
                den = jnp.sum(e, axis=-1, keepdims=True)
                o_ref[h, cur, :] = _dot(e / den, vv, NN)
                l_ref[h, cur, :] = jnp.broadcast_to(mx + jnp.log(den), (ATT_BLK, ATT_DH))

    osh = pltpu.HBM((ATT_HG, S, ATT_DH), F32)
    kw = dict(name=f"att_fwd{gi}", grid=(ATT_HG // hp, dil), in_specs=_slab_specs(gi) + [_bias_spec(gi)],
              out_specs=tuple(_head_specs(gi, 2)), out_shape=(osh, osh))
    if comm is not None:
        return _carry(body, comm, **kw)(slabs, slabs, slabs, bias)
    return _pcall(body, compiler_params=_params(("parallel", "arbitrary")), **kw)(slabs, slabs, slabs, bias)


def _att_bwd(slabs, bias, o, lse, do, dlse, gi, comm=None):
    dil, nb, hp = _att_geometry(gi)
    per = ATT_HG // hp
    scale = ATT_DH ** -0.5
    wide = lambda t: jnp.concatenate([t, t], axis=1)

    def body(q_ref, k_ref, v_ref, bias_ref, o_ref, l_ref, do_ref, dl_ref, dq_ref, dk_ref, dv_ref, ds_ref):
        r = pl.program_id(1)

        @pl.when(r == 0)
        def _():
            ds_ref[...] = jnp.zeros_like(ds_ref)

        for h in range(hp):
            carry_k = carry_v = None
            for n in range(nb):
                cur = _blk(dil, r, n)
                q = q_ref[h, cur, :]
                dov = do_ref[h, cur, :]
                delta = jnp.sum(dov * o_ref[h, cur, :], axis=-1, keepdims=True)
                if n == 0:
                    own = pl.ds(ATT_BLK, ATT_BLK)
                    kk, vv = k_ref[h, cur, :], v_ref[h, cur, :]
                    s = _dot(q, kk, NT) * scale + bias_ref[h, :, own]
                    p = jnp.where(_att_valid_first(), jnp.exp(s - l_ref[h, cur, :]), 0.0)
                    ds = p * (_dot(dov, vv, NT) - delta + dl_ref[h, cur, :])
                    ds_ref[h, :, own] += ds
                    dq_ref[h, cur, :] = _dot(ds, kk, NN) * scale
                    carry_k, carry_v = _dot(ds, q, TN) * scale, _dot(p, dov, TN)
                    continue
                prev = _blk(dil, r, n - 1)
                kk = jnp.concatenate([k_ref[h, prev, :], k_ref[h, cur, :]], axis=0)
                vv = jnp.concatenate([v_ref[h, prev, :], v_ref[h, cur, :]], axis=0)
                s = _dot(q, kk, NT) * scale + bias_ref[h]
                p = jnp.where(_att_valid(n), jnp.exp(s - wide(l_ref[h, cur, :])), 0.0)
                dp = _dot(dov, vv, NT)
                ds = p * (dp - delta + wide(dl_ref[h, cur, :]))
                ds_ref[h] += ds
                dq_ref[h, cur, :] = _dot(ds, kk, NN) * scale
                dkk = _dot(ds, q, TN) * scale
                dvv = _dot(p, dov, TN)
                dk_ref[h, prev, :] = carry_k + dkk[:ATT_BLK]
                dv_ref[h, prev, :] = carry_v + dvv[:ATT_BLK]
                carry_k, carry_v = dkk[ATT_BLK:], dvv[ATT_BLK:]
            last = _blk(dil, r, nb - 1)
            dk_ref[h, last, :] = carry_k
            dv_ref[h, last, :] = carry_v

    osh = jax.ShapeDtypeStruct((ATT_HG, S, ATT_DH), F32)
    kw = dict(name=f"att_bwd{gi}", grid=(per, dil), in_specs=_slab_specs(gi) + [_bias_spec(gi)] + _head_specs(gi, 4),
              out_specs=(*_head_specs(gi, 3), pl.BlockSpec((hp, ATT_BLK, 2 * ATT_BLK), lambda g, r: (g, 0, 0))),
              out_shape=(osh, osh, osh, jax.ShapeDtypeStruct((ATT_HG, ATT_BLK, 2 * ATT_BLK), F32)))
    args = (slabs, slabs, slabs, bias, o, lse, do, dlse)
    if comm is not None:
        return _carry(body, comm, **kw)(*args)
    return _pcall(body, compiler_params=_params(("arbitrary", "arbitrary")), **kw)(*args)


AW = ATT_HG * ATT_DH


def _mix_weights(l0, l1, l2):
    mx = jnp.maximum(jnp.maximum(l0, l1), l2)
    e0, e1, e2 = jnp.exp(l0 - mx), jnp.exp(l1 - mx), jnp.exp(l2 - mx)
    den = e0 + e1 + e2
    return e0 / den, e1 / den, e2 / den


def _heads_spec():
    return pl.BlockSpec((ATT_HG, TR, ATT_DH), lambda i: (0, i, 0))


def _mix_fwd(os_, ls, comm=None):
    def body(o0, o1, o2, l0, l1, l2, att_ref):
        for h in range(ATT_HG):
            w0, w1, w2 = _mix_weights(l0[h], l1[h], l2[h])
            att_ref[:, h * ATT_DH:(h + 1) * ATT_DH] = (w0 * o0[h] + w1 * o1[h] + w2 * o2[h]).astype(BF16)

    kw = dict(name="mix_fwd", grid=(S // TR,), in_specs=[_heads_spec()] * 6, out_specs=_row_spec(AW),
              out_shape=jax.ShapeDtypeStruct((S, AW), BF16))
    if comm is not None:
        return _carry(body, comm, **kw)(*os_, *ls)
    return _pcall(body, compiler_params=_params(("parallel",)), **kw)(*os_, *ls)


def _mix_bwd(os_, ls, datt):
    def body(o0, o1, o2, l0, l1, l2, da_ref, d0, d1, d2, e0, e1, e2):
        for h in range(ATT_HG):
            ws = _mix_weights(l0[h], l1[h], l2[h])
            da = da_ref[:, h * ATT_DH:(h + 1) * ATT_DH]
            dws = []
            for o_ref, w, d_ref in zip((o0, o1, o2), ws, (d0, d1, d2)):
                d_ref[h] = w * da
                dws.append(jnp.broadcast_to(jnp.sum(da * o_ref[h], axis=-1, keepdims=True), (TR, ATT_DH)))
            tot = ws[0] * dws[0] + ws[1] * dws[1] + ws[2] * dws[2]
            for w, dw, e_ref in zip(ws, dws, (e0, e1, e2)):
                e_ref[h] = w * (dw - tot)

    o = pltpu.HBM((ATT_HG, S, ATT_DH), F32)
    return _pcall(body, name="mix_bwd", grid=(S // TR,), in_specs=[_heads_spec()] * 6 + [_row_spec(AW)],
                  out_specs=(_heads_spec(),) * 6, out_shape=(o,) * 6,
                  compiler_params=_params(("parallel",)))(*os_, *ls, datt)


def _ada_fwd(c_all, w_sh, b_sl):
    def body(c_ref, w_ref, b_ref, o_ref):
        cv = c_ref[...]
        o_ref[...] = _dot(cv * jax.nn.sigmoid(cv), w_ref[...], NN) + b_ref[...]

    return _pcall(body, name="ada_fwd", out_shape=jax.ShapeDtypeStruct((N_DEV, w_sh.shape[1]), F32),
                  compiler_params=_params())(c_all, w_sh, b_sl)


CAST_STEPS = 4


def _to_bf16(arrs, comm, name):
    n = len(arrs)

    def body(*refs):
        for src, dst in zip(refs[:n], refs[n:]):
            dst[...] = src[...].astype(BF16)

    blocks = [pl.BlockSpec((a.shape[0] // CAST_STEPS, a.shape[1]), lambda i: (i, 0)) for a in arrs]
    return _carry(body, comm, name=name, grid=(CAST_STEPS,), in_specs=blocks, out_specs=tuple(blocks),
                  out_shape=tuple(pltpu.HBM(a.shape, BF16) for a in arrs))(*arrs)


def _ada_bwd(c_all, dm_sl):
    def body(c_ref, d_ref, o_ref):
        cv = c_ref[...]
        o_ref[...] = _dot(cv * jax.nn.sigmoid(cv), d_ref[...], TN)

    return _pcall(body, name="ada_bwd", out_shape=jax.ShapeDtypeStruct((D, dm_sl.shape[1]), F32),
                  compiler_params=_params())(c_all, dm_sl)


N_MOD = 6


def _sum_small(gathered, widths):
    def body(g_ref, gb_ref, dm_ref, *outs):
        def total(cols):
            acc = g_ref[0, :, cols]
            for e in range(1, N_DEV):
                acc = acc + g_ref[e, :, cols]
            return acc

        gb_ref[...] = total(slice(0, N_MOD * D))
        for e in range(N_DEV):
            dm_ref[e:e + 1, :] = g_ref[e, :, :N_MOD * D]
        col = N_MOD * D
        for w, o_ref in zip(widths, outs):
            o_ref[...] = total(slice(col, col + w))
            col += w

    assert gathered.shape == (N_DEV, 1, N_MOD * D + sum(widths))
    shapes = (jax.ShapeDtypeStruct((1, N_MOD * D), F32), jax.ShapeDtypeStruct((N_DEV, N_MOD * D), F32),
              *[jax.ShapeDtypeStruct((1, w), F32) for w in widths])
    res = _pcall(body, name="sum_small", out_shape=shapes, compiler_params=_params())(gathered)
    return res[0], res[1], res[2:]


ADAMW_STEPS = 4


def _row_tile(m, n, min_steps=1):
    t = max(8, min(m // min_steps, (1 << 19) // n // 8 * 8))
    while m % t:
        t -= 8
    return t


def _pair_sum(full, recv, sel, name, col_block=0):
    _, m, n = recv.shape
    t = _row_tile(m, n)

    def body(sel_ref, a_ref, b_ref, o_ref):
        o_ref[...] = (a_ref[...].astype(F32) + b_ref[...].astype(F32)).astype(o_ref.dtype)

    gs = pltpu.PrefetchScalarGridSpec(
        num_scalar_prefetch=1, grid=(4, m // t),
        in_specs=[pl.BlockSpec((None, None, t, n), lambda q, i, s: (q, s[0], i, col_block)),
                  pl.BlockSpec((None, t, n), lambda q, i, s: (q, i, 0))],
        out_specs=pl.BlockSpec((None, t, n), lambda q, i, s: (q, i, 0)))
    return _pcall(body, name=name, grid_spec=gs, out_shape=pltpu.HBM((4, m, n), full.dtype),
                  compiler_params=_params(("parallel", "parallel")))(sel, full, recv)


def _chip_sum(part, recv, sel, name):
    _, m, n = part.shape
    t = _row_tile(m, n)

    def body(sel_ref, a_ref, r_ref, o_ref):
        o_ref[...] = ((a_ref[...].astype(F32) + r_ref[0].astype(F32)) + r_ref[1].astype(F32)) + r_ref[2].astype(F32)

    gs = pltpu.PrefetchScalarGridSpec(
        num_scalar_prefetch=1, grid=(m // t,),
        in_specs=[pl.BlockSpec((None, t, n), lambda i, s: (s[0], i, 0)),
                  pl.BlockSpec((3, t, n), lambda i, s: (0, i, 0))],
        out_specs=pl.BlockSpec((t, n), lambda i, s: (i, 0)))
    return _pcall(body, name=name, grid_spec=gs, out_shape=jax.ShapeDtypeStruct((m, n), F32),
                  compiler_params=_params(("parallel",)))(sel, part, recv)


def _adamw_math(w, g, m, v):
    nm = ADAM_B1 * m + (1.0 - ADAM_B1) * g
    nv = ADAM_B2 * v + (1.0 - ADAM_B2) * (g * g)
    m_hat = nm / (1.0 - ADAM_B1 ** ADAM_STEP)
    v_hat = nv / (1.0 - ADAM_B2 ** ADAM_STEP)
    return -ADAM_LR * (m_hat / (jnp.sqrt(v_hat) + ADAM_EPS) + ADAM_WD * w), nm, nv


def _adamw(w, g, m, v, name):
    _, rows, cols = w.shape
    t = _row_tile(rows, cols, ADAMW_STEPS)

    def body(w_ref, g_ref, m_ref, v_ref, d_ref, nm_ref, nv_ref):
        d_ref[...], nm_ref[...], nv_ref[...] = _adamw_math(w_ref[...], g_ref[...], m_ref[...], v_ref[...])

    spec3 = pl.BlockSpec((None, t, cols), lambda i: (0, i, 0))
    spec2 = pl.BlockSpec((t, cols), lambda i: (i, 0))
    o = jax.ShapeDtypeStruct(w.shape, F32)
    return _pcall(body, name=name, grid=(rows // t,), in_specs=[spec3, spec2, spec3, spec3], out_specs=(spec3,) * 3,
                  out_shape=(o, o, o), compiler_params=_params(("parallel",)))(w, g, m, v)


def _adamw_reduced1(w, m, v, part, recv, sel, name):
    _, rows, cols = w.shape
    t = _row_tile(rows, cols, ADAMW_STEPS)

    def body(sel_ref, w_ref, m_ref, v_ref, p_ref, r_ref, g_ref, d_ref, nm_ref, nv_ref):
        g = ((p_ref[...].astype(F32) + r_ref[0].astype(F32)) + r_ref[1].astype(F32)) + r_ref[2].astype(F32)
        g_ref[...] = g
        d_ref[...], nm_ref[...], nv_ref[...] = _adamw_math(w_ref[...], g, m_ref[...], v_ref[...])

    wspec = pl.BlockSpec((None, t, cols), lambda i, s: (0, i, 0))
    gs = pltpu.PrefetchScalarGridSpec(
        num_scalar_prefetch=1, grid=(rows // t,),
        in_specs=[wspec, wspec, wspec, pl.BlockSpec((None, t, cols), lambda i, s: (s[0], i, 0)),
                  pl.BlockSpec((3, t, cols), lambda i, s: (0, i, 0))],
        out_specs=(wspec,) * 4)
    o = jax.ShapeDtypeStruct(w.shape, F32)
    return _pcall(body, name=name, grid_spec=gs, out_shape=(o, o, o, o),
                  compiler_params=_params(("parallel",)))(sel, w, m, v, part, recv)


def _adamw_reduced(w, m, v, parts, recvs, sel):
    _, rows, cols = w.shape
    half = cols // 2
    t = _row_tile(rows, half, ADAMW_STEPS)

    def body(sel_ref, w_ref, m_ref, v_ref, pa_ref, pb_ref, ra_ref, rb_ref, g_ref, d_ref, nm_ref, nv_ref):
        total = lambda p_ref, r_ref: ((p_ref[...].astype(F32) + r_ref[0].astype(F32)) + r_ref[1].astype(F32)) \
            + r_ref[2].astype(F32)
        g = jnp.where(pl.program_id(1) == 0, total(pa_ref, ra_ref), total(pb_ref, rb_ref))
        g_ref[...] = g
        d_ref[...], nm_ref[...], nv_ref[...] = _adamw_math(w_ref[...], g, m_ref[...], v_ref[...])

    wspec = pl.BlockSpec((None, t, half), lambda i, j, s: (0, i, j))
    pspec = pl.BlockSpec((None, t, half), lambda i, j, s: (s[0], i, 0))
    rspec = pl.BlockSpec((3, t, half), lambda i, j, s: (0, i, 0))
    gs = pltpu.PrefetchScalarGridSpec(num_scalar_prefetch=1, grid=(rows // t, 2),
                                      in_specs=[wspec, wspec, wspec, pspec, pspec, rspec, rspec],
                                      out_specs=(wspec,) * 4)
    o = jax.ShapeDtypeStruct(w.shape, F32)
    return _pcall(body, name="adamw_w_in", grid_spec=gs, out_shape=(o, o, o, o),
                  compiler_params=_params(("parallel", "arbitrary")))(sel, w, m, v, *parts, *recvs)


def _adamw_small(ws, gs, ms, vs):
    n = len(ws)

    def body(*refs):
        for i in range(n):
            w_ref, g_ref, m_ref, v_ref = (refs[k * n + i] for k in range(4))
            d, nm, nv = _adamw_math(w_ref[...], g_ref[...], m_ref[...], v_ref[...])
            refs[4 * n + i][...] = d
            refs[5 * n + i][...] = nm
            refs[6 * n + i][...] = nv

    shapes = tuple(jax.ShapeDtypeStruct(w.shape, F32) for w in ws)
    res = _pcall(body, name="adamw_small", out_shape=shapes * 3, compiler_params=_params())(*ws, *gs, *ms, *vs)
    return res[:n], res[n:2 * n], res[2 * n:]


def _mesh_pos():
    return lax.axis_index("x"), lax.axis_index("y"), lax.axis_index("c")


class _GatherSmallDirect:
    ins_in_vmem = True

    def __init__(self, arrs):
        self.ins = list(arrs)
        self.out_shape = tuple(jax.ShapeDtypeStruct((N_DEV,) + a.shape, a.dtype) for a in arrs)
        n = len(arrs)
        self.sems = [pltpu.SemaphoreType.DMA((7 * n,)), pltpu.SemaphoreType.DMA((7 * n,)),
                     pltpu.SemaphoreType.DMA((n,))]

    def _copies(self, ins, outs, sems):
        send_sems, recv_sems, local_sems = sems
        x, y, c = _mesh_pos()
        me = _slot((x, y, c))
        mine = [pltpu.make_async_copy(ins[p], outs[p].at[me], local_sems.at[p]) for p in range(len(self.ins))]
        sends, arrivals = [], []
        for p in range(len(self.ins)):
            for j in range(1, N_DEV):
                peer = (me + j) % N_DEV
                to = (peer // 4, (peer // 2) % 2, peer % 2)
                sends.append(pltpu.make_async_remote_copy(
                    src_ref=ins[p], dst_ref=outs[p].at[me], send_sem=send_sems.at[7 * p + j - 1],
                    recv_sem=recv_sems.at[7 * p + (N_DEV - j) - 1], device_id=to, device_id_type=MESH))
                arrivals.append(pltpu.make_async_remote_copy(
                    src_ref=ins[p], dst_ref=outs[p].at[peer], send_sem=send_sems.at[7 * p + j - 1],
                    recv_sem=recv_sems.at[7 * p + j - 1], device_id=to, device_id_type=MESH))
        return mine, sends, arrivals

    def start(self, ins, outs, sems):
        mine, sends, _ = self._copies(ins, outs, sems)
        for cp in mine + sends:
            cp.start()

    def finish(self, ins, outs, sems):
        mine, sends, arrivals = self._copies(ins, outs, sems)
        for cp in arrivals:
            cp.wait_recv()
        for cp in sends:
            cp.wait_send()
        for cp in mine:
            cp.wait()


class _ExchangeCore:
    def __init__(self, fulls, cols=None):
        self.ins = list(fulls)
        self.cols = cols
        width = lambda f: f.shape[3] if cols is None else cols[1]
        self.out_shape = tuple(jax.ShapeDtypeStruct((4, f.shape[2], width(f)), f.dtype) for f in fulls)
        self.sems = [pltpu.SemaphoreType.DMA((4 * len(fulls),)), pltpu.SemaphoreType.DMA((4 * len(fulls),))]

    def _copies(self, ins, outs, sems):
        send_sems, recv_sems = sems
        x, y, c = _mesh_pos()

        def src(a, q):
            ref = ins[a].at[q, 1 - c]
            return ref if self.cols is None else ref.at[:, pl.ds(*self.cols)]

        return [pltpu.make_async_remote_copy(
            src_ref=src(a, q), dst_ref=outs[a].at[q], send_sem=send_sems.at[4 * a + q],
            recv_sem=recv_sems.at[4 * a + q], device_id=(x, y, 1 - c), device_id_type=MESH)
            for a in range(len(self.ins)) for q in range(4)]

    def start(self, ins, outs, sems):
        for cp in self._copies(ins, outs, sems):
            cp.start()

    def finish(self, ins, outs, sems):
        for cp in self._copies(ins, outs, sems):
            cp.wait()


class _ExchangeChip:
    def __init__(self, parts):
        self.ins = list(parts)
        self.out_shape = tuple(jax.ShapeDtypeStruct((3,) + p.shape[1:], p.dtype) for p in parts)
        self.sems = [pltpu.SemaphoreType.DMA((3 * len(parts),)), pltpu.SemaphoreType.DMA((3 * len(parts),))]

    def _copies(self, ins, outs, sems):
        send_sems, recv_sems = sems
        x, y, c = _mesh_pos()
        chips = [(1 - x, y), (x, 1 - y), (1 - x, 1 - y)]
        return [pltpu.make_async_remote_copy(
            src_ref=ins[a].at[2 * px + py], dst_ref=outs[a].at[j], send_sem=send_sems.at[3 * a + j],
            recv_sem=recv_sems.at[3 * a + j], device_id=(px, py, c), device_id_type=MESH)
            for a in range(len(self.ins)) for j, (px, py) in enumerate(chips)]

    def start(self, ins, outs, sems):
        for cp in self._copies(ins, outs, sems):
            cp.start()

    def finish(self, ins, outs, sems):
        for cp in self._copies(ins, outs, sems):
            cp.wait()


HBM_ONLY = pl.BlockSpec(memory_space=pltpu.HBM)
SEM_SPEC = pl.BlockSpec(memory_space=pltpu.SEMAPHORE)
SIDE_EFFECT = pltpu.SideEffectType.DATAFLOW_SIDE_EFFECTING


def _chip_copies(p_refs, land_refs, send_sems, recv_sems):
    x, y, c = _mesh_pos()
    return [pltpu.make_async_remote_copy(
        src_ref=p_refs[a].at[2 * px + py], dst_ref=land_refs[a].at[j], send_sem=send_sems.at[3 * a + j],
        recv_sem=recv_sems.at[3 * a + j], device_id=(px, py, c), device_id_type=MESH)
        for a in range(len(p_refs)) for j, (px, py) in enumerate([(1 - x, y), (x, 1 - y), (1 - x, 1 - y)])]


def _chip_exchange_start(parts, name):
    n = len(parts)
    lands = [lax.empty((3,) + p.shape[1:], p.dtype) for p in parts]

    def body(*refs):
        p_refs, land_refs, (send_sems, recv_sems) = refs[:n], refs[n:2 * n], refs[2 * n:2 * n + 2]
        for cp in _chip_copies(p_refs, land_refs, send_sems, recv_sems):
            cp.start()
        token = refs[-1]
        token[...] = jnp.zeros_like(token)

    hbm = lambda t: pltpu.HBM(t.shape, t.dtype)
    res = pl.pallas_call(
        body, name=name,
        out_shape=(pltpu.SemaphoreType.DMA((3 * n,)), pltpu.SemaphoreType.DMA((3 * n,)), *[hbm(t) for t in parts + lands],
                   jax.ShapeDtypeStruct((8, 128), F32)),
        in_specs=(HBM_ONLY,) * (2 * n),
        out_specs=(SEM_SPEC, SEM_SPEC, *[HBM_ONLY] * (2 * n), pl.BlockSpec(memory_space=pltpu.VMEM)),
        input_output_aliases={i: 2 + i for i in range(2 * n)},
        compiler_params=pltpu.CompilerParams(has_side_effects=SIDE_EFFECT))(
        *[pltpu.with_memory_space_constraint(t, pltpu.HBM) for t in parts + lands])
    return (res[0], res[1], list(res[2:2 + n]), list(res[2 + n:2 + 2 * n])), res[-1]


def _chip_exchange_wait(in_flight, after, name):
    send_sems, recv_sems, parts, lands = in_flight
    n = len(parts)

    def body(*refs):
        p_refs, land_refs, (send_sems, recv_sems) = refs[:n], refs[n:2 * n], refs[2 * n:2 * n + 2]
        for cp in _chip_copies(p_refs, land_refs, send_sems, recv_sems):
            cp.wait_send()
            cp.wait_recv()

    res = pl.pallas_call(
        body, name=name, out_shape=tuple(pltpu.HBM(t.shape, t.dtype) for t in parts + lands),
        in_specs=(*[HBM_ONLY] * (2 * n), SEM_SPEC, SEM_SPEC, pl.BlockSpec(memory_space=pl.ANY)),
        out_specs=(HBM_ONLY,) * (2 * n), input_output_aliases={i: i for i in range(2 * n)},
        compiler_params=pltpu.CompilerParams(has_side_effects=SIDE_EFFECT))(*parts, *lands, send_sems, recv_sems, after)
    return list(res[:n]), list(res[n:])


def _slot(p):
    return 4 * p[0] + 2 * p[1] + p[2]


def _gather_copies(src_refs, out_refs, send_sems, recv_sems):
    x, y, c = _mesh_pos()
    targets = [(x, y, 1 - c), (1 - x, y, c), (x, 1 - y, c), (1 - x, 1 - y, c)]
    return [pltpu.make_async_remote_copy(
        src_ref=src_refs[a], dst_ref=out_refs[a].at[_slot((x, y, c))], send_sem=send_sems.at[4 * a + k],
        recv_sem=recv_sems.at[4 * a + k], device_id=to, device_id_type=MESH)
        for a in range(len(src_refs)) for k, to in enumerate(targets)]


def _gather_start(shards, after, name):
    n = len(shards)
    outs = [lax.empty((N_DEV,) + s.shape, s.dtype) for s in shards]

    def body(*refs):
        for cp in _gather_copies(refs[:n], refs[n:2 * n], refs[2 * n + 1], refs[2 * n + 2]):
            cp.start()
        token = refs[-1]
        token[...] = jnp.zeros_like(token)

    res = pl.pallas_call(
        body, name=name,
        out_shape=(pltpu.SemaphoreType.DMA((4 * n,)), pltpu.SemaphoreType.DMA((4 * n,)),
                   *[pltpu.HBM(t.shape, t.dtype) for t in shards + outs], jax.ShapeDtypeStruct((8, 128), F32)),
        in_specs=(*[HBM_ONLY] * (2 * n), pl.BlockSpec(memory_space=pl.ANY)),
        out_specs=(SEM_SPEC, SEM_SPEC, *[HBM_ONLY] * (2 * n), pl.BlockSpec(memory_space=pltpu.VMEM)),
        input_output_aliases={i: 2 + i for i in range(2 * n)},
        compiler_params=pltpu.CompilerParams(has_side_effects=SIDE_EFFECT))(
        *[pltpu.with_memory_space_constraint(t, pltpu.HBM) for t in shards + outs], after)
    return (res[0], res[1], list(res[2:2 + n]), list(res[2 + n:2 + 2 * n])), res[-1]


def _gather_wait(in_flight, after, name):
    send_sems, recv_sems, shards, outs = in_flight
    n = len(shards)

    def body(*refs):
        for cp in _gather_copies(refs[:n], refs[n:2 * n], refs[2 * n], refs[2 * n + 1]):
            cp.wait_send()
            cp.wait_recv()

    res = pl.pallas_call(
        body, name=name, out_shape=tuple(pltpu.HBM(t.shape, t.dtype) for t in shards + outs),
        in_specs=(*[HBM_ONLY] * (2 * n), SEM_SPEC, SEM_SPEC, pl.BlockSpec(memory_space=pl.ANY)),
        out_specs=(HBM_ONLY,) * (2 * n), input_output_aliases={i: i for i in range(2 * n)},
        compiler_params=pltpu.CompilerParams(has_side_effects=SIDE_EFFECT))(*shards, *outs, send_sems, recv_sems, after)
    return list(res[:n]), list(res[n:])


class _PassToSibling:
    def __init__(self, shards, gathered):
        n = self.n = len(shards)
        self.ins = list(shards) + list(gathered)
        self.out_shape = tuple(jax.ShapeDtypeStruct(g.shape, g.dtype) for g in gathered)
        self.aliases = {n + a: a for a in range(n)}
        self.sems = [pltpu.SemaphoreType.DMA((3 * n,)), pltpu.SemaphoreType.DMA((3 * n,)),
                     pltpu.SemaphoreType.DMA((n,))]

    def _copies(self, ins, outs, sems):
        send_sems, recv_sems, local_sems = sems
        x, y, c = _mesh_pos()
        chips = [(1 - x, y), (x, 1 - y), (1 - x, 1 - y)]
        mine = [pltpu.make_async_copy(ins[a], outs[a].at[_slot((x, y, c))], local_sems.at[a]) for a in range(self.n)]
        passed, awaited = [], []
        for a in range(self.n):
            for j, chip in enumerate(chips):
                sems_j = dict(send_sem=send_sems.at[3 * a + j], recv_sem=recv_sems.at[3 * a + j],
                              device_id=(x, y, 1 - c), device_id_type=MESH)
                blk = outs[a].at[_slot((*chip, c))]
                passed.append(pltpu.make_async_remote_copy(src_ref=blk, dst_ref=blk, **sems_j))
                got = outs[a].at[_slot((*chip, 1 - c))]
                awaited.append(pltpu.make_async_remote_copy(src_ref=got, dst_ref=got, **sems_j))
        return mine, passed, awaited

    def start(self, ins, outs, sems):
        mine, passed, _ = self._copies(ins, outs, sems)
        for cp in mine + passed:
            cp.start()

    def finish(self, ins, outs, sems):
        mine, passed, awaited = self._copies(ins, outs, sems)
        for cp in passed:
            cp.wait_send()
        for cp in awaited:
            cp.wait_recv()
        for cp in mine:
            cp.wait()


def _reduce_sums(fulls, recv_core, core, tag):
    return [_pair_sum(f, r, core, f"rs_pair_{tag}{i}") for i, (f, r) in enumerate(zip(fulls, recv_core))]


def _local_step(x, tgt, mods, w_in_shard, order, shards, small, chip, core):
    sh1, sc1, g1, sh2, sc2, g2 = mods
    norm1_g, rel_bias, gn_g, gn_b, norm2_g, norm_f_g = small
    tables = _ret_tables()
    buckets = jnp.asarray(_bucket_tables())

    proj, slabs, w_in_t, h1 = _gather_proj(x, norm1_g, sh1, sc1, w_in_shard, order)
    flight_w1, token_w = _gather_start(list(shards[:3]), proj, "gather_w1_start")
    flight_w2, token_w = _gather_start(list(shards[3:]), token_w, "gather_w2_start")
    bias = _bias_build(rel_bias, buckets, token_w)
    outs, lses = [], []
    for gi in range(len(ATT_GROUPS)):
        o, l = _att_fwd(slabs, bias, gi)
        outs.append(o)
        lses.append(l)
    att, gathered = _mix_fwd(outs, lses, comm=_PassToSibling(*_gather_wait(flight_w1, lses[2], "gather_w1_wait")))
    w_ret_out, w_att_out, w_o = (_from_slots(g, ax) for g, ax in zip(gathered, BIG_AXES[1:4]))
    gated, ro, states = _ret_fwd(proj, tables, gn_g, gn_b, att)
    ret_out, gathered = _mm(gated, w_ret_out, 'nn', tm=S, tn=256, tk=2048, name="ret_out",
                            comm=_PassToSibling(*_gather_wait(flight_w2, gated, "gather_w2_wait")))
    w_ff1, w_ff2 = (_from_slots(g, ax) for g, ax in zip(gathered, BIG_AXES[4:]))
    att_out, merged = _att_out_merge(att, w_att_out, proj, ret_out)
    mixo, x1, h2 = _w_o_norm2(merged, w_o, x, g1, norm2_g, sh2, sc2)
    u, act = _mm(h2, w_ff1, 'nn', tm=S, tn=512, tk=D, name="ff1", relu2=True)
    loss, dx2, g_normf, df, dg2 = _ff2_final(act, w_ff2, x1, g2, tgt, norm_f_g)

    gw_ff2 = _mm(act, df, 'tn', tm=512, tn=D, tk=S, name="gw_ff2", out_dtype=BF16)
    du = _mm(df, w_ff2, 'nt', tm=S, tn=512, tk=D, name="d_act", out_dtype=BF16, relu2_of=u)
    gw_ff1 = _mm(h2, du, 'tn', tm=D, tn=512, tk=S, name="gw_ff1", out_dtype=BF16)
    fulls_a = [_to_slots(g, ax) for g, ax in zip((gw_ff1, gw_ff2), BIG_AXES[4:])]
    dh2, recv_core_a = _mm(du, w_ff1, 'nt', tm=1024, tn=1024, tk=2048, name="dh2", comm=_ExchangeCore(fulls_a))
    parts_a = _reduce_sums(fulls_a, recv_core_a, core, "a")
    flight_a, token_a = _chip_exchange_start(parts_a, "rs_a_start")
    dx1, dsc2, dsh2, g_norm2, dmixo, dg1 = _norm_mod_bwd(x1, norm2_g, sc2, dh2, dx2, "norm2_bwd", gate=(mixo, g1))

    gw_o = _mm(merged, dmixo, 'tn', tm=D, tn=512, tk=S, name="gw_o", out_dtype=BF16, after=token_a)
    d_ret_out, d_att_out, dga, dgb = _dmerged_split(dmixo, w_o, proj, ret_out, att_out)
    gw_ret_out = _mm(gated, d_ret_out, 'tn', tm=512, tn=D, tk=S, name="gw_ret_out", out_dtype=BF16)
    gw_att_out = _mm(att, d_att_out, 'tn', tm=AW, tn=D, tk=S, name="gw_att_out", out_dtype=BF16)
    fulls_b = [_to_slots(g, ax) for g, ax in zip((gw_ret_out, gw_att_out, gw_o), BIG_AXES[1:4])]
    dgated, recv_core_b = _mm(d_ret_out, w_ret_out, 'nt', tm=S, tn=512, tk=D, name="dgated",
                              comm=_ExchangeCore(fulls_b))
    parts_b = _reduce_sums(fulls_b, recv_core_b, core, "b")
    flight_b, token_b = _chip_exchange_start(parts_b, "rs_b_start")
    datt = _mm(d_att_out, w_att_out, 'nt', tm=S, tn=AW, tk=D, name="datt", after=token_b)
    mix_grads = _mix_bwd(outs, lses, datt)
    datt_parts, ds_sums = [], []
    for gi in range(len(ATT_GROUPS)):
        dq, dk, dv, ds_sum = _att_bwd(slabs, bias, outs[gi], lses[gi], mix_grads[gi], mix_grads[3 + gi], gi)
        datt_parts += [dq, dk, dv]
        ds_sums.append(ds_sum)
    g_bias = _bias_grad(jnp.concatenate(ds_sums, axis=0), buckets)[:, :, 0].T.reshape(1, -1)
    dproj, g_gn_g, g_gn_b = _ret_bwd(proj, tables, gn_g, gn_b, ro, states, dgated, datt_parts + [dga, dgb])
    parts_a, recv_chip_a = _chip_exchange_wait(flight_a, dproj, "rs_a_wait")
    parts_b, recv_chip_b = _chip_exchange_wait(flight_b, dproj, "rs_b_wait")
    reduced = list(zip(parts_b + parts_a, recv_chip_b + recv_chip_a))
    full_in = _to_slots(_mm(dproj, h1, 'tn', tm=512, tn=D, tk=S, name="gw_in", out_dtype=BF16), 0)
    halves = [(half * (D // 2), D // 2) for half in range(2)]
    (recv_core_in0,) = _run_comm(_ExchangeCore([full_in], cols=halves[0]), "rs_core_in0")
    flight0, token = _chip_exchange_start([_pair_sum(full_in, recv_core_in0, core, "rs_pair_c0", col_block=0)],
                                          "rs_in0_start")
    dh1, (recv_core_in1,) = _mm(dproj, w_in_t, 'nn', tm=1024, tn=1024, tk=2560, name="dh1",
                                comm=_ExchangeCore([full_in], cols=halves[1]), after=token)
    flight1, token = _chip_exchange_start([_pair_sum(full_in, recv_core_in1, core, "rs_pair_c1", col_block=1)],
                                          "rs_in1_start")
    in_flight = [flight0, flight1]
    gx, dsc1, dsh1, g_norm1 = _norm_mod_bwd(x, norm1_g, sc1, dh1, dx1, "norm1_bwd", after=token)

    dmod = [dsh1, dsc1, dg1, dsh2, dsc2, dg2]
    small_g = [g_norm1, g_bias, g_gn_g, g_gn_b, g_norm2, g_normf]
    return loss, gx, in_flight, reduced, small_g, dmod


def _to_slots(g, axis):
    if axis == 0:
        return g.reshape(4, 2, g.shape[0] // N_DEV, g.shape[1])
    return g.reshape(g.shape[0], N_DEV, g.shape[1] // N_DEV).transpose(1, 0, 2).reshape(4, 2, g.shape[0], -1)


def _from_slots(w8, axis):
    if axis == 0:
        return w8.reshape(-1, w8.shape[2])
    return w8.transpose(1, 0, 2).reshape(w8.shape[1], -1)


BIG_AXES = (1, 0, 1, 0, 1, 0)


def kernel(x, c, w_ada, b_ada, norm1_g, w_in, rel_bias, ret_gn_g, ret_gn_b, w_ret_out, w_att_out, w_o, norm2_g, w_ff1, w_ff2, norm_f_g, loss_target, m_w_ada, m_b_ada, m_norm1_g, m_w_in, m_rel_bias, m_ret_gn_g, m_ret_gn_b, m_w_ret_out, m_w_att_out, m_w_o, m_norm2_g, m_w_ff1, m_w_ff2, m_norm_f_g, v_w_ada, v_b_ada, v_norm1_g, v_w_in, v_rel_bias, v_ret_gn_g, v_ret_gn_b, v_w_ret_out, v_w_att_out, v_w_o, v_norm2_g, v_w_ff1, v_w_ff2, v_norm_f_g):
    mx, my, mc = _mesh_pos()
    dev = 4 * mx + 2 * my + mc
    chip = jnp.reshape(2 * mx + my, (1,)).astype(jnp.int32)
    core = jnp.reshape(mc, (1,)).astype(jnp.int32)
    ada_w = D * 6 // N_DEV

    w_in, m_w_in, v_w_in = (jnp.transpose(t, (0, 2, 1)) for t in (w_in, m_w_in, v_w_in))

    (w_in_shard,), (c_all,) = _to_bf16([w_in[0]], _GatherSmallDirect([c]), "gather_c")
    c_all = c_all.reshape(N_DEV, D)
    b_sl = lax.dynamic_slice(b_ada, (0, dev * ada_w), (1, ada_w))
    other_shards, (mod_all,) = _to_bf16([w[0] for w in (w_ret_out, w_att_out, w_o, w_ff1, w_ff2)],
                                        _GatherSmallDirect([_ada_fwd(c_all, w_ada[0], b_sl)]), "gather_mod")
    mod = lax.dynamic_index_in_dim(mod_all, dev, axis=1, keepdims=False).reshape(6, D)
    mods = tuple(mod[i:i + 1] for i in range(6))

    small = (norm1_g, rel_bias, ret_gn_g, ret_gn_b, norm2_g, norm_f_g.reshape(1, D))
    order = lax.dynamic_index_in_dim(jnp.asarray(_proj_order()), 2 * mx + my, axis=0, keepdims=False)
    loss, gx, in_flight, big_red, small_g, dmod = _local_step(x[0], loss_target[0], mods, w_in_shard, order,
                                                              list(other_shards), small, chip, core)

    names = ['w_ada', 'b_ada', 'norm1_g', 'w_in', 'rel_bias', 'ret_gn_g', 'ret_gn_b', 'w_ret_out', 'w_att_out',
             'w_o', 'norm2_g', 'w_ff1', 'w_ff2', 'norm_f_g']
    ws = dict(zip(names, (w_ada, b_ada, norm1_g, w_in, rel_bias, ret_gn_g, ret_gn_b, w_ret_out, w_att_out, w_o,
                          norm2_g, w_ff1, w_ff2, norm_f_g)))
    ms = dict(zip(names, (m_w_ada, m_b_ada, m_norm1_g, m_w_in, m_rel_bias, m_ret_gn_g, m_ret_gn_b, m_w_ret_out,
                          m_w_att_out, m_w_o, m_norm2_g, m_w_ff1, m_w_ff2, m_norm_f_g)))
    vs = dict(zip(names, (v_w_ada, v_b_ada, v_norm1_g, v_w_in, v_rel_bias, v_ret_gn_g, v_ret_gn_b, v_w_ret_out,
                          v_w_att_out, v_w_o, v_norm2_g, v_w_ff1, v_w_ff2, v_norm_f_g)))
    grads, delta, new_m, new_v = {}, {}, {}, {}
    big_names = ('w_ret_out', 'w_att_out', 'w_o', 'w_ff1', 'w_ff2')
    for n, (part, recv) in zip(big_names, big_red):
        grads[n], delta[n], new_m[n], new_v[n] = _adamw_reduced1(ws[n], ms[n], vs[n], part, recv, chip, "adamw_" + n)
    updated = lax.optimization_barrier((gx, tuple(delta[n] for n in big_names)))
    rows = dmod + small_g + [loss]
    (gathered,) = _run_comm(_GatherSmallDirect([jnp.concatenate(rows, axis=1)]), "gather_small", after=updated[0])
    g_b_ada, dmod_all, (g_norm1, g_bias, g_gn_g, g_gn_b, g_norm2, g_normf, loss_sum) = _sum_small(
        gathered, [r.shape[1] for r in rows[N_MOD:]])
    loss_out = loss_sum[0, 0]
    g_w_ada = _ada_bwd(c_all, lax.dynamic_slice(dmod_all, (0, dev * ada_w), (N_DEV, ada_w)))

    grads.update(w_ada=g_w_ada.reshape(w_ada.shape), b_ada=g_b_ada, norm1_g=g_norm1, rel_bias=g_bias,
                 ret_gn_g=g_gn_g, ret_gn_b=g_gn_b, norm2_g=g_norm2, norm_f_g=g_normf)
    delta['w_ada'], new_m['w_ada'], new_v['w_ada'] = _adamw(w_ada, g_w_ada, m_w_ada, v_w_ada, "adamw_w_ada")
    small_names = ('b_ada', 'norm1_g', 'rel_bias', 'ret_gn_g', 'ret_gn_b', 'norm2_g', 'norm_f_g')
    two_d = {n: (1, ws[n].size) if ws[n].ndim == 1 else ws[n].shape for n in small_names}
    d_, m_, v_ = _adamw_small(*[[src[n].reshape(two_d[n]) for n in small_names] for src in (ws, grads, ms, vs)])
    for i, n in enumerate(small_names):
        shp = ws[n].shape
        delta[n], new_m[n], new_v[n] = d_[i].reshape(shp), m_[i].reshape(shp), v_[i].reshape(shp)
        grads[n] = grads[n].reshape(shp)

    done = lax.optimization_barrier((gx, tuple(d_), tuple(delta[n] for n in ('w_ada', 'w_ret_out', 'w_att_out', 'w_o',
                                                                               'w_ff1', 'w_ff2'))))
    parts_in, recvs_in = [], []
    for half, flight in enumerate(in_flight):
        (part_in,), (recv_chip_in,) = _chip_exchange_wait(flight, done[0], f"rs_in{half}_wait")
        parts_in.append(part_in)
        recvs_in.append(recv_chip_in)
    grads['w_in'], delta['w_in'], new_m['w_in'], new_v['w_in'] = _adamw_reduced(w_in, m_w_in, v_w_in, parts_in,
                                                                               recvs_in, chip)
    for d in (grads, delta, new_m, new_v):
        d['w_in'] = jnp.transpose(d['w_in'], (0, 2, 1))
    return (loss_out, gx[None], *[grads[n] for n in names], *[delta[n] for n in names],
            *[new_m[n] for n in names], *[new_v[n] for n in names])
```

```python
import functools
import math

import numpy as np
import jax
import jax.numpy as jnp
from jax import lax
from jax.experimental import pallas as pl
from jax.experimental.pallas import tpu as pltpu

F32 = jnp.float32
BF16 = jnp.bfloat16
MESH = pl.DeviceIdType.MESH

N_DEV = 8
S = 2048
D = 1024
RET_HEADS = 4
RET_DK = 256
RET_DV = 512
CHUNK = 128
N_CHUNK = S // CHUNK
ATT_GROUPS = ((128, 1), (512, 4), (2048, 16))
ATT_HG = 4
ATT_DH = 128
ATT_BLK = 128
N_BUCKETS = 32
MAX_DIST = 2048
D_FF = 4096
IN_COLS = 12800
OFF_RQ, OFF_RK, OFF_RV, OFF_RG, OFF_ATT = 0, 1024, 2048, 4096, 6144
OFF_GA, OFF_GB = 6144, 7168
RMS_EPS = 1e-6
GN_EPS = 1e-5
ADAM_LR, ADAM_B1, ADAM_B2, ADAM_EPS, ADAM_WD, ADAM_STEP = 0.001, 0.9, 0.999, 1e-08, 0.01, 10
VMEM_LIMIT = 48 * 1024 * 1024


def _pcall(body, **kw):
    return pl.pallas_call(body, **kw)


def _params(sem=None):
    return pltpu.CompilerParams(dimension_semantics=sem, vmem_limit_bytes=VMEM_LIMIT)


HBM_SPEC = pl.BlockSpec(memory_space=pl.ANY)


def _carry(body, comm, *, name, grid, in_specs, out_specs, out_shape, scratch_shapes=()):
    single = not isinstance(out_specs, (tuple, list))
    o_specs = (out_specs,) if single else tuple(out_specs)
    o_shape = (out_shape,) if single else tuple(out_shape)
    n_in, n_out, n_scr = len(in_specs), len(o_specs), len(scratch_shapes)
    nci, nco = len(comm.ins), len(comm.out_shape)
    total = int(np.prod(grid))

    def wrapped(*refs):
        bounds = np.cumsum([0, n_in, nci, n_out, nco, n_scr])
        a, ci, o, co, scr = (refs[bounds[i]:bounds[i + 1]] for i in range(5))
        sems = refs[bounds[5]:]
        flat = 0
        for d, g in enumerate(grid):
            flat = flat * g + pl.program_id(d)

        @pl.when(flat == 0)
        def _():
            comm.start(ci, co, sems)

        body(*a, *o, *scr)

        @pl.when(flat == total - 1)
        def _():
            comm.finish(ci, co, sems)

    aliases = {n_in + i: n_out + o for i, o in getattr(comm, "aliases", {}).items()}
    call = _pcall(wrapped, name=name, grid=grid, in_specs=list(in_specs) + [HBM_SPEC] * nci,
                  out_specs=o_specs + (HBM_SPEC,) * nco, out_shape=o_shape + tuple(comm.out_shape),
                  scratch_shapes=list(scratch_shapes) + list(comm.sems), input_output_aliases=aliases,
                  compiler_params=_params(("arbitrary",) * len(grid)))

    def run(*args):
        res = call(*args, *comm.ins)
        own = res[0] if single else tuple(res[:n_out])
        return own, tuple(res[n_out:])

    return run


def _run_comm(comm, name, after=None):
    nci, nco = len(comm.ins), len(comm.out_shape)
    extra = [] if after is None else [after]

    def body(*refs):
        ci, co, sems = refs[:nci], refs[nci + len(extra):nci + len(extra) + nco], refs[nci + len(extra) + nco:]
        comm.start(ci, co, sems)
        comm.finish(ci, co, sems)

    in_spec = pl.BlockSpec(memory_space=pltpu.VMEM) if getattr(comm, "ins_in_vmem", False) else HBM_SPEC
    return _pcall(body, name=name, in_specs=[in_spec] * nci + [HBM_SPEC] * len(extra), out_specs=(HBM_SPEC,) * nco,
                  out_shape=tuple(comm.out_shape), scratch_shapes=list(comm.sems))(*comm.ins, *extra)


def _dot(a, b, dn):
    return lax.dot_general(a.astype(BF16), b.astype(BF16), (dn, ((), ())), preferred_element_type=F32)


NN = ((1,), (0,))
NT = ((1,), (1,))
TN = ((0,), (0,))


def _mm(a, b, mode, *, tm, tn, tk, name, out_dtype=F32, res=None, gvec=None, relu2=False, relu2_of=None, comm=None,
        after=None):
    if mode == 'nn':
        (M, K), (_, N) = a.shape, b.shape
        a_spec = pl.BlockSpec((tm, tk), lambda i, j, k: (i, k))
        b_spec = pl.BlockSpec((tk, tn), lambda i, j, k: (k, j))
        dn = NN
    elif mode == 'nt':
        (M, K), (N, _) = a.shape, b.shape
        a_spec = pl.BlockSpec((tm, tk), lambda i, j, k: (i, k))
        b_spec = pl.BlockSpec((tn, tk), lambda i, j, k: (j, k))
        dn = NT
    else:
        (K, M), (_, N) = a.shape, b.shape
        a_spec = pl.BlockSpec((tk, tm), lambda i, j, k: (k, i))
        b_spec = pl.BlockSpec((tk, tn), lambda i, j, k: (k, j))
        dn = TN
    assert M % tm == 0 and N % tn == 0 and K % tk == 0, (name, M, N, K)
    nk = K // tk
    fused = res is not None
    o_spec = pl.BlockSpec((tm, tn), lambda i, j, k: (i, j))

    def body(a_ref, b_ref, *rest):
        acc_ref = rest[-1] if nk > 1 else None
        if after is not None:
            rest = rest[1:]
        if fused:
            res_ref, g_ref, o_ref, x_ref = rest[:4]
        elif relu2_of is not None:
            u_ref, o_ref = rest[:2]
        elif relu2:
            o_ref, act_ref = rest[:2]
        else:
            o_ref = rest[0]

        def finish(acc):
            if relu2_of is not None:
                acc = acc * (2.0 * jnp.maximum(u_ref[...], 0.0))
            o_ref[...] = acc.astype(o_ref.dtype)
            if fused:
                x_ref[...] = res_ref[...] + g_ref[...] * acc
            if relu2:
                r = jnp.maximum(acc, 0.0)
                act_ref[...] = (r * r).astype(BF16)

        p = _dot(a_ref[...], b_ref[...], dn)
        if nk == 1:
            finish(p)
        else:
            k = pl.program_id(2)

            @pl.when(k == 0)
            def _():
                acc_ref[...] = p

            @pl.when(k > 0)
            def _():
                acc_ref[...] += p

            @pl.when(k == nk - 1)
            def _():
                finish(acc_ref[...])

    in_specs = [a_spec, b_spec]
    args = [a, b]
    if after is not None:
        in_specs.append(pl.BlockSpec(memory_space=pl.ANY))
        args.append(after)
    out_shape = jax.ShapeDtypeStruct((M, N), out_dtype)
    out_specs = o_spec
    if fused:
        in_specs += [pl.BlockSpec((tm, tn), lambda i, j, k: (i, j)), pl.BlockSpec((1, tn), lambda i, j, k: (0, j))]
        args += [res, gvec]
        out_shape = (out_shape, jax.ShapeDtypeStruct((M, N), F32))
        out_specs = (o_spec, pl.BlockSpec((tm, tn), lambda i, j, k: (i, j)))
    elif relu2_of is not None:
        in_specs.append(pl.BlockSpec((tm, tn), lambda i, j, k: (i, j)))
        args.append(relu2_of)
    elif relu2:
        out_shape = (out_shape, jax.ShapeDtypeStruct((M, N), BF16))
        out_specs = (o_spec, pl.BlockSpec((tm, tn), lambda i, j, k: (i, j)))
    kw = dict(name=name, grid=(M // tm, N // tn, nk), in_specs=in_specs, out_specs=out_specs,
              out_shape=out_shape, scratch_shapes=[pltpu.VMEM((tm, tn), F32)] if nk > 1 else [])
    if comm is not None:
        return _carry(body, comm, **kw)(*args)
    return _pcall(body, compiler_params=_params(("parallel", "parallel", "arbitrary")), **kw)(*args)


PROJ_TN = 512
ATT_T0, ATT_T1 = 6144 // PROJ_TN, 10752 // PROJ_TN
N_SLABS = (ATT_T1 - ATT_T0) * 4
MAIN_COLS = IN_COLS - (ATT_T1 - ATT_T0) * PROJ_TN


PROJ_TILES = IN_COLS // PROJ_TN
SHARD_ROWS = IN_COLS // N_DEV
W_CHUNKS = 4
N_OWN, N_NEAR = 5, 18


def _proj_order():
    out = np.zeros((4, 3, PROJ_TILES), np.int32)
    for q in range(4):
        def hops(t):
            owners = {col // (2 * SHARD_ROWS) for col in (t * PROJ_TN, (t + 1) * PROJ_TN - 1)}
            return max(bin(q ^ p).count("1") for p in owners)
        order = sorted(range(PROJ_TILES), key=lambda t: (hops(t), t))
        assert all(hops(t) == 0 for t in order[:N_OWN]) and all(hops(t) < 2 for t in order[:N_NEAR])
        is_att = [ATT_T0 <= t < ATT_T1 for t in order]
        for row, kind, index in ((1, False, lambda t: t if t < ATT_T0 else t - (ATT_T1 - ATT_T0)),
                                 (2, True, lambda t: t - ATT_T0)):
            own = [index(t) if a == kind else None for t, a in zip(order, is_att)]
            first = next(v for v in own if v is not None)
            last = first
            for j, v in enumerate(own):
                last = last if v is None else v
                out[q, row, j] = last
        out[q, 0] = order
    return out


def _gather_proj(x, g, sh, sc, shard, order):
    rows = SHARD_ROWS // W_CHUNKS

    def body(ord_ref, x_ref, g_ref, shift_ref, scale_ref, sh_ref, main_ref, slab_ref, full_ref, a_ref, wbuf, xbuf,
             fetch_sems, send_sems, recv_sems, local_sems, x_sem):
        j = pl.program_id(0)
        x, y, c = _mesh_pos()
        me, sibling = (x, y, c), (x, y, 1 - c)
        chips = [(1 - x, y), (x, 1 - y), (1 - x, 1 - y)]

        def block(p, owner):
            return full_ref.at[pl.ds(pl.multiple_of(_slot(owner) * SHARD_ROWS + p * rows, 16), rows)]

        def copy(p, k, owner, to, from_input=False):
            dst = block(p, owner)
            return pltpu.make_async_remote_copy(
                src_ref=sh_ref.at[pl.ds(p * rows, rows)] if from_input else dst, dst_ref=dst,
                send_sem=send_sems.at[7 * p + k], recv_sem=recv_sems.at[7 * p + k], device_id=to, device_id_type=MESH)

        pieces = range(W_CHUNKS)
        mine = [pltpu.make_async_copy(sh_ref.at[pl.ds(p * rows, rows)], block(p, me), local_sems.at[p]) for p in pieces]
        first = [copy(p, 0, me, sibling, from_input=True) for p in pieces]
        first += [copy(p, 1 + n, me, (*chips[n], c), from_input=True) for p in pieces for n in range(2)]
        near_pass = [copy(p, 4 + n, (*chips[n], c), sibling) for p in pieces for n in range(2)]
        relay = [copy(p, 3, ((x + 1 - c) % 2, (y + c) % 2, c), ((x + c) % 2, (y + 1 - c) % 2, c)) for p in pieces]
        far_pass = [copy(p, 6, (*chips[2], c), sibling) for p in pieces]

        def fetch(pos):
            slot = lax.rem(pos, 2)
            start = pl.multiple_of(ord_ref[0, pos] * PROJ_TN, PROJ_TN)
            return pltpu.make_async_copy(full_ref.at[pl.ds(start, PROJ_TN)], wbuf.at[slot], fetch_sems.at[slot])

        @pl.when(j == 0)
        def _():
            x_copy = pltpu.make_async_copy(x_ref, xbuf, x_sem.at[0])
            x_copy.start()
            for cp in mine + first:
                cp.start()
            x_copy.wait()
            for r in range(S // TR):
                rws = pl.ds(r * TR, TR)
                xv = xbuf[rws, :]
                rstd = lax.rsqrt(jnp.mean(xv * xv, axis=-1, keepdims=True) + RMS_EPS)
                n = xv * rstd * g_ref[...]
                a_ref[rws, :] = (n * (1.0 + scale_ref[...]) + shift_ref[...]).astype(BF16)
            for cp in mine:
                cp.wait()
            for p in pieces:
                copy(p, 0, sibling, me).wait_recv()
            fetch(j).start()

        @pl.when(j == N_OWN - 1)
        def _():
            for p in pieces:
                for n in range(2):
                    copy(p, 1 + n, (*chips[n], c), me).wait_recv()
                    near_pass[2 * p + n].start()
                relay[p].start()
            for p in pieces:
                for n in range(2):
                    copy(p, 4 + n, (*chips[n], 1 - c), me).wait_recv()

        @pl.when(j == N_NEAR - 1)
        def _():
            for p in pieces:
                copy(p, 3, (*chips[2], c), me).wait_recv()
                far_pass[p].start()
            for p in pieces:
                copy(p, 6, (*chips[2], 1 - c), me).wait_recv()

        @pl.when(j + 1 < PROJ_TILES)
        def _():
            fetch(j + 1).start()

        fetch(j).wait()
        w_ref = wbuf.at[lax.rem(j, 2)]
        tile = ord_ref[0, j]
        is_att = (tile >= ATT_T0) & (tile < ATT_T1)
        chunks = [pl.ds(r * 512, 512) for r in range(S // 512)]

        @pl.when(jnp.logical_not(is_att))
        def _():
            for rws in chunks:
                main_ref[rws, :] = _dot(a_ref[rws, :], w_ref[...], NT)

        @pl.when(is_att)
        def _():
            for rws in chunks:
                p = _dot(a_ref[rws, :], w_ref[...], NT)
                for h in range(4):
                    slab_ref[h, rws, :] = p[:, h * 128:(h + 1) * 128]

        @pl.when(j == PROJ_TILES - 1)
        def _():
            for cp in first + near_pass + relay + far_pass:
                cp.wait_send()

    vec = pl.BlockSpec((1, D), lambda j, o: (0, 0))
    gs = pltpu.PrefetchScalarGridSpec(
        num_scalar_prefetch=1, grid=(PROJ_TILES,),
        in_specs=[HBM_SPEC, vec, vec, vec, HBM_SPEC],
        out_specs=(pl.BlockSpec((S, PROJ_TN), lambda j, o: (0, o[1, j])),
                   pl.BlockSpec((4, S, 128), lambda j, o: (o[2, j], 0, 0)), HBM_SPEC,
                   pl.BlockSpec((S, D), lambda j, o: (0, 0))),
        scratch_shapes=[pltpu.VMEM((2, PROJ_TN, D), BF16), pltpu.VMEM((S, D), F32), pltpu.SemaphoreType.DMA((2,)),
                        pltpu.SemaphoreType.DMA((7 * W_CHUNKS,)), pltpu.SemaphoreType.DMA((7 * W_CHUNKS,)),
                        pltpu.SemaphoreType.DMA((W_CHUNKS,)), pltpu.SemaphoreType.DMA((1,))])
    return _pcall(body, name="gather_proj", grid_spec=gs,
                  out_shape=(jax.ShapeDtypeStruct((S, MAIN_COLS), F32), jax.ShapeDtypeStruct((N_SLABS, S, 128), F32),
                             jax.ShapeDtypeStruct((IN_COLS, D), BF16), jax.ShapeDtypeStruct((S, D), BF16)),
                  compiler_params=_params(("arbitrary",)))(order, x, g, sh, sc, shard)


TR = 256


def _row_spec(w=D):
    return pl.BlockSpec((TR, w), lambda i: (i, 0))


def _vec_spec(w=D):
    return pl.BlockSpec((1, w), lambda i: (0, 0))


def _norm_mod_bwd(x, g, sc, dh, dres, name, gate=None, after=None):
    gated = gate is not None

    def body(x_ref, g_ref, sc_ref, dh_ref, dres_ref, *rest):
        if after is not None:
            rest = rest[1:]
        if gated:
            f_ref, gv_ref, dx_ref, dsc_ref, dsh_ref, dg_ref, dz_ref, dgv_ref = rest
        else:
            dx_ref, dsc_ref, dsh_ref, dg_ref = rest
        i = pl.program_id(0)
        xv = x_ref[...]
        dh = dh_ref[...]
        rstd = lax.rsqrt(jnp.mean(xv * xv, axis=-1, keepdims=True) + RMS_EPS)
        xhat = xv * rstd
        gv = g_ref[...]
        dn = dh * (1.0 + sc_ref[...])
        dxhat = dn * gv
        dx = dres_ref[...] + rstd * (dxhat - xhat * jnp.mean(dxhat * xhat, axis=-1, keepdims=True))
        dx_ref[...] = dx
        sums = [(dsc_ref, jnp.sum(dh * (xhat * gv), axis=0, keepdims=True)),
                (dsh_ref, jnp.sum(dh, axis=0, keepdims=True)),
                (dg_ref, jnp.sum(dn * xhat, axis=0, keepdims=True))]
        if gated:
            dz_ref[...] = (dx * gv_ref[...]).astype(BF16)
            sums.append((dgv_ref, jnp.sum(dx * f_ref[...], axis=0, keepdims=True)))

        @pl.when(i == 0)
        def _():
            for ref, p in sums:
                ref[...] = p

        @pl.when(i > 0)
        def _():
            for ref, p in sums:
                ref[...] += p

    vec = jax.ShapeDtypeStruct((1, D), F32)
    in_specs = [_row_spec(), _vec_spec(), _vec_spec(), _row_spec(), _row_spec()]
    out_specs = [_row_spec(), _vec_spec(), _vec_spec(), _vec_spec()]
    out_shape = [jax.ShapeDtypeStruct((S, D), F32), vec, vec, vec]
    args = [x, g, sc, dh, dres]
    if after is not None:
        in_specs.append(HBM_SPEC)
        args.append(after)
    if gated:
        in_specs += [_row_spec(), _vec_spec()]
        out_specs += [_row_spec(), _vec_spec()]
        out_shape += [jax.ShapeDtypeStruct((S, D), BF16), vec]
        args += list(gate)
    return _pcall(body, name=name, grid=(S // TR,), in_specs=in_specs, out_specs=tuple(out_specs),
                  out_shape=tuple(out_shape), compiler_params=_params(("arbitrary",)))(*args)


def _w_o_norm2(merged, w_o, x, g1, g, sh, sc):
    def body(a_ref, b_ref, x_ref, g1_ref, g_ref, sh_ref, sc_ref, o_ref, x1_ref, h_ref):
        acc = _dot(a_ref[...], b_ref[...], NN)
        o_ref[...] = acc
        xv = x_ref[...] + g1_ref[...] * acc
        x1_ref[...] = xv
        rstd = lax.rsqrt(jnp.mean(xv * xv, axis=-1, keepdims=True) + RMS_EPS)
        h_ref[...] = (xv * rstd * g_ref[...] * (1.0 + sc_ref[...]) + sh_ref[...]).astype(BF16)

    rows = pl.BlockSpec((FF2_TM, D), lambda i: (i, 0))
    f32 = jax.ShapeDtypeStruct((S, D), F32)
    return _pcall(body, name="w_o_norm2", grid=(S // FF2_TM,),
                  in_specs=[rows, pl.BlockSpec((D, D), lambda i: (0, 0)), rows] + [_vec_spec()] * 4,
                  out_specs=(rows, rows, rows), out_shape=(f32, f32, jax.ShapeDtypeStruct((S, D), BF16)),
                  compiler_params=_params(("parallel",)))(merged, w_o, x, g1, g, sh, sc)


FF2_TM = 512


def _ff2_final(act, w_ff2, x1, g2, tgt, g):
    def body(a_ref, b_ref, x1_ref, g2_ref, t_ref, g_ref, loss_ref, dx_ref, dg_ref, df_ref, dg2_ref):
        i = pl.program_id(0)
        f = _dot(a_ref[...], b_ref[...], NN)
        g2v = g2_ref[...]
        xv = x1_ref[...] + g2v * f
        gv = g_ref[...]
        rstd = lax.rsqrt(jnp.mean(xv * xv, axis=-1, keepdims=True) + RMS_EPS)
        xhat = xv * rstd
        err = xhat * gv - t_ref[...]
        dy = err * (1.0 / D)
        dxhat = dy * gv
        dx = rstd * (dxhat - xhat * jnp.mean(dxhat * xhat, axis=-1, keepdims=True))
        dx_ref[...] = dx
        df_ref[...] = (dx * g2v).astype(BF16)
        p_g = jnp.sum(dy * xhat, axis=0, keepdims=True)
        p_g2 = jnp.sum(dx * f, axis=0, keepdims=True)
        p_l = jnp.zeros((1, 128), F32) + 0.5 * jnp.sum(jnp.mean(err * err, axis=-1, keepdims=True))

        @pl.when(i == 0)
        def _():
            dg_ref[...] = p_g
            dg2_ref[...] = p_g2
            loss_ref[...] = p_l

        @pl.when(i > 0)
        def _():
            dg_ref[...] += p_g
            dg2_ref[...] += p_g2
            loss_ref[...] += p_l

    vec = jax.ShapeDtypeStruct((1, D), F32)
    rows = lambda w: pl.BlockSpec((FF2_TM, w), lambda i: (i, 0))
    return _pcall(body, name="ff2_final", grid=(S // FF2_TM,),
                  in_specs=[rows(D_FF), pl.BlockSpec((D_FF, D), lambda i: (0, 0)), rows(D), _vec_spec(), rows(D),
                            _vec_spec()],
                  out_specs=(_vec_spec(128), rows(D), _vec_spec(), rows(D), _vec_spec()),
                  out_shape=(jax.ShapeDtypeStruct((1, 128), F32), jax.ShapeDtypeStruct((S, D), F32), vec,
                             jax.ShapeDtypeStruct((S, D), BF16), vec),
                  compiler_params=_params(("arbitrary",)))(act, w_ff2, x1, g2, tgt, g)


HALF = 512


MERGE_TM = 1024


def _merge_specs():
    blk = lambda off: pl.BlockSpec((MERGE_TM, HALF), lambda i, j: (i, off // HALF + j))
    return blk(OFF_GA), blk(OFF_GB), blk(0)


def _att_out_merge(att, w_att_out, proj, ret_out):
    def body(a_ref, b_ref, ga_ref, gb_ref, r_ref, o_ref, m_ref):
        acc = _dot(a_ref[...], b_ref[...], NN)
        o_ref[...] = acc
        m_ref[...] = (jax.nn.sigmoid(ga_ref[...]) * r_ref[...] + jax.nn.sigmoid(gb_ref[...]) * acc).astype(BF16)

    ga, gb, tile = _merge_specs()
    return _pcall(body, name="att_out", grid=(S // MERGE_TM, D // HALF),
                  in_specs=[pl.BlockSpec((MERGE_TM, AW), lambda i, j: (i, 0)), pl.BlockSpec((AW, HALF), lambda i, j: (0, j)),
                            ga, gb, tile],
                  out_specs=(tile, tile),
                  out_shape=(jax.ShapeDtypeStruct((S, D), F32), jax.ShapeDtypeStruct((S, D), BF16)),
                  compiler_params=_params(("parallel", "parallel")))(att, w_att_out, proj, proj, ret_out)


def _dmerged_split(dmixo, w_o, proj, ret_out, att_out):
    def body(a_ref, b_ref, ga_ref, gb_ref, r_ref, at_ref, dr_ref, da_ref, dga_ref, dgb_ref):
        dm = _dot(a_ref[...], b_ref[...], NT)
        sa = jax.nn.sigmoid(ga_ref[...])
        sb = jax.nn.sigmoid(gb_ref[...])
        dr_ref[...] = (dm * sa).astype(BF16)
        da_ref[...] = (dm * sb).astype(BF16)
        dga_ref[...] = (dm * r_ref[...] * (sa * (1.0 - sa))).astype(BF16)
        dgb_ref[...] = (dm * at_ref[...] * (sb * (1.0 - sb))).astype(BF16)

    ga, gb, tile = _merge_specs()
    o = jax.ShapeDtypeStruct((S, D), BF16)
    return _pcall(body, name="dmerged", grid=(S // MERGE_TM, D // HALF),
                  in_specs=[pl.BlockSpec((MERGE_TM, D), lambda i, j: (i, 0)), pl.BlockSpec((HALF, D), lambda i, j: (j, 0)),
                            ga, gb, tile, tile],
                  out_specs=(tile,) * 4, out_shape=(o, o, o, o),
                  compiler_params=_params(("parallel", "parallel")))(dmixo, w_o, proj, proj, ret_out, att_out)


def _ret_tables():
    H, C = RET_HEADS, CHUNK
    log_g = jnp.log1p(-(2.0 ** (-5.0 - jnp.arange(H, dtype=F32))))
    idx = jnp.arange(C, dtype=F32)
    rel = idx[:, None] - idx[None, :]
    inner = jnp.where(rel >= 0, jnp.exp(log_g[:, None, None] * jnp.maximum(rel, 0.0)), 0.0)
    qd = jnp.exp(log_g[:, None] * (idx + 1.0))[:, :, None]
    kd = jnp.exp(log_g[:, None] * (C - 1.0 - idx))[:, :, None]
    cd = jnp.broadcast_to(jnp.exp(log_g * C)[:, None, None], (H, 1, 128))
    half = RET_DK // 2
    inv = 10000.0 ** (-jnp.arange(half, dtype=F32) / half)
    ang = jnp.arange(S, dtype=F32)[:, None] * inv[None, :]
    return inner, qd, kd, cd, jnp.cos(ang), jnp.sin(ang)


def _rot(x, cos, sin):
    x1, x2 = x[:, :128], x[:, 128:]
    return jnp.concatenate([x1 * cos - x2 * sin, x1 * sin + x2 * cos], axis=1)


def _rot_t(d, cos, sin):
    d1, d2 = d[:, :128], d[:, 128:]
    return jnp.concatenate([d1 * cos + d2 * sin, d2 * cos - d1 * sin], axis=1)


RET_COLS = OFF_ATT
RET_VW = RET_HEADS * RET_DV


def _ret_specs(chunk_of):
    ci = chunk_of
    whole = lambda shape: pl.BlockSpec(shape, lambda t: (0,) * len(shape))
    return [
        pl.BlockSpec((CHUNK, RET_COLS), lambda t: (ci(t), 0)),
        pl.BlockSpec((CHUNK, 128), lambda t: (ci(t), 0)),
        pl.BlockSpec((CHUNK, 128), lambda t: (ci(t), 0)),
        whole((RET_HEADS, CHUNK, CHUNK)), whole((RET_HEADS, CHUNK, 1)), whole((RET_HEADS, CHUNK, 1)),
        whole((RET_HEADS, 1, 128)), whole((1, RET_VW)), whole((1, RET_VW)),
    ]


def _ret_cols(h):
    q = slice(OFF_RQ + h * RET_DK, OFF_RQ + (h + 1) * RET_DK)
    k = slice(OFF_RK + h * RET_DK, OFF_RK + (h + 1) * RET_DK)
    v = slice(OFF_RV + h * RET_DV, OFF_RV + (h + 1) * RET_DV)
    g = slice(OFF_RG + h * RET_DV, OFF_RG + (h + 1) * RET_DV)
    return q, k, v, g, slice(h * RET_DV, (h + 1) * RET_DV)


def _ret_fwd(proj, tables, gn_g, gn_b, after):
    inner, qd, kd, cd, cos, sin = tables

    def body(x_ref, cos_ref, sin_ref, in_ref, qd_ref, kd_ref, cd_ref, g_ref, b_ref, after_ref,
             gated_ref, ro_ref, st_ref, s_scr):
        i = pl.program_id(0)

        @pl.when(i == 0)
        def _():
            s_scr[...] = jnp.zeros_like(s_scr)

        cosv, sinv = cos_ref[...], sin_ref[...]
        for h in range(RET_HEADS):
            cq, ck, cv, cg, co = _ret_cols(h)
            q = _rot(x_ref[:, cq], cosv, sinv)
            k = _rot(x_ref[:, ck], cosv, sinv) * (RET_DK ** -0.5)
            v = x_ref[:, cv]
            st = s_scr[h]
            st_ref[h] = st.astype(BF16)
            s = _dot(q, k, NT) * in_ref[h]
            o = _dot(s, v, NN) + _dot(q, st, NN) * qd_ref[h]
            s_scr[h] = st * cd_ref[h, :, :1] + _dot(k * kd_ref[h], v, TN)
            ro_ref[:, co] = o
            mu = jnp.mean(o, axis=-1, keepdims=True)
            oc = o - mu
            var = jnp.mean(oc * oc, axis=-1, keepdims=True)
            rn = oc * lax.rsqrt(var + GN_EPS) * g_ref[:, co] + b_ref[:, co]
            rg = x_ref[:, cg]
            gated_ref[:, co] = (rg * jax.nn.sigmoid(rg) * rn).astype(BF16)

    ospec = pl.BlockSpec((CHUNK, RET_VW), lambda t: (t, 0))
    return _pcall(
        body, name="ret_fwd", grid=(N_CHUNK,), in_specs=_ret_specs(lambda t: t) + [HBM_SPEC],
        out_specs=(ospec, ospec, pl.BlockSpec((RET_HEADS, None, RET_DK, RET_DV), lambda t: (0, t, 0, 0))),
        out_shape=(jax.ShapeDtypeStruct((S, RET_VW), BF16), jax.ShapeDtypeStruct((S, RET_VW), F32),
                   jax.ShapeDtypeStruct((RET_HEADS, N_CHUNK, RET_DK, RET_DV), BF16)),
        scratch_shapes=[pltpu.VMEM((RET_HEADS, RET_DK, RET_DV), F32)],
        compiler_params=_params(("arbitrary",)))(proj, cos, sin, inner, qd, kd, cd, gn_g, gn_b, after)


def _ret_bwd(proj, tables, gn_g, gn_b, ro, states, dgated, others):
    inner, qd, kd, cd, cos, sin = tables
    last = N_CHUNK - 1
    pieces = lambda o: [(h, o.shape[2]) for h in range(o.shape[0])] if len(o.shape) == 3 else [(None, o.shape[1])]
    assert RET_COLS + sum(w for o in others for _, w in pieces(o)) == IN_COLS

    def body(x_ref, cos_ref, sin_ref, in_ref, qd_ref, kd_ref, cd_ref, g_ref, b_ref, ro_ref, st_ref, dg_ref, *rest):
        other_refs, (dx_ref, gg_ref, gb_ref, gs_scr) = rest[:len(others)], rest[len(others):]
        t = pl.program_id(0)
        col = RET_COLS
        for o_ref in other_refs:
            for h, w in pieces(o_ref):
                dx_ref[:, col:col + w] = (o_ref[...] if h is None else o_ref[h]).astype(BF16)
                col += w

        @pl.when(t == 0)
        def _():
            gs_scr[...] = jnp.zeros_like(gs_scr)
            gg_ref[...] = jnp.zeros_like(gg_ref)
            gb_ref[...] = jnp.zeros_like(gb_ref)

        cosv, sinv = cos_ref[...], sin_ref[...]
        for h in range(RET_HEADS):
            cq, ck, cv, cg, co = _ret_cols(h)
            q = _rot(x_ref[:, cq], cosv, sinv)
            k = _rot(x_ref[:, ck], cosv, sinv) * (RET_DK ** -0.5)
            v = x_ref[:, cv]
            qdv, kdv, dm = qd_ref[h], kd_ref[h], in_ref[h]
            st = st_ref[h]
            o = ro_ref[:, co]
            gv = g_ref[:, co]
            mu = jnp.mean(o, axis=-1, keepdims=True)
            oc = o - mu
            rstd = lax.rsqrt(jnp.mean(oc * oc, axis=-1, keepdims=True) + GN_EPS)
            ohat = oc * rstd
            rn = ohat * gv + b_ref[:, co]
            rg = x_ref[:, cg]
            sg = jax.nn.sigmoid(rg)
            dgt = dg_ref[:, co]
            drn = dgt * (rg * sg)
            dx_ref[:, cg] = (dgt * rn * (sg * (1.0 + rg * (1.0 - sg)))).astype(BF16)
            gg_ref[:, co] += jnp.sum(drn * ohat, axis=0, keepdims=True)
            gb_ref[:, co] += jnp.sum(drn, axis=0, keepdims=True)
            dohat = drn * gv
            do = rstd * (dohat - jnp.mean(dohat, axis=-1, keepdims=True)
                         - ohat * jnp.mean(dohat * ohat, axis=-1, keepdims=True))
            gs = gs_scr[h]
            s = _dot(q, k, NT) * dm
            dsr = _dot(do, v, NT) * dm
            dq = _dot(dsr, k, NN) + _dot(do, st, NT) * qdv
            dk = _dot(dsr, q, TN) + _dot(v, gs, NT) * kdv
            dv = _dot(s, do, TN) + _dot(k * kdv, gs, NN)
            gs_scr[h] = gs * cd_ref[h, :, :1] + _dot(q * qdv, do, TN)
            dx_ref[:, cq] = _rot_t(dq, cosv, sinv).astype(BF16)
            dx_ref[:, ck] = (_rot_t(dk, cosv, sinv) * (RET_DK ** -0.5)).astype(BF16)
            dx_ref[:, cv] = dv.astype(BF16)

    rev = lambda t: last - t
    vblk = pl.BlockSpec((CHUNK, RET_VW), lambda t: (rev(t), 0))
    vspec = pl.BlockSpec((1, RET_VW), lambda t: (0, 0))
    rows = lambda w: pl.BlockSpec((CHUNK, w), lambda t: (rev(t), 0))
    return _pcall(
        body, name="ret_bwd", grid=(N_CHUNK,),
        in_specs=_ret_specs(rev) + [vblk, pl.BlockSpec((RET_HEADS, None, RET_DK, RET_DV), lambda t: (0, rev(t), 0, 0)),
                                    vblk] + [rows(o.shape[1]) if o.ndim == 2 else
                                             pl.BlockSpec((o.shape[0], CHUNK, o.shape[2]), lambda t: (0, rev(t), 0))
                                             for o in others],
        out_specs=(rows(IN_COLS), vspec, vspec),
        out_shape=(jax.ShapeDtypeStruct((S, IN_COLS), BF16), jax.ShapeDtypeStruct((1, RET_VW), F32),
                   jax.ShapeDtypeStruct((1, RET_VW), F32)),
        scratch_shapes=[pltpu.VMEM((RET_HEADS, RET_DK, RET_DV), F32)],
        compiler_params=_params(("arbitrary",)))(proj, cos, sin, inner, qd, kd, cd, gn_g, gn_b, ro, states, dgated,
                                                 *others)


def _bucket_tables():
    qi = np.arange(ATT_BLK)[:, None]
    kj = np.arange(2 * ATT_BLK)[None, :]
    m = ATT_BLK + qi - kj
    out = []
    for win, dil in ATT_GROUPS:
        w = win // dil
        dist = (np.clip(m, 0, w) * dil).astype(np.int32)
        max_exact = N_BUCKETS // 2
        d_f = np.maximum(dist, 1).astype(np.float32)
        large = max_exact + (np.log(d_f / np.float32(max_exact)) / np.float32(math.log(MAX_DIST / max_exact))
                             * np.float32(N_BUCKETS - max_exact)).astype(np.int32)
        large = np.minimum(large, N_BUCKETS - 1)
        out.append(np.where(dist < max_exact, dist, large).astype(np.int32))
    return np.stack(out)


def _bias_build(rel_bias, buckets, after):
    def body(tab_ref, bk_ref, after_ref, o_ref):
        hh = pl.program_id(0)
        bk = bk_ref[...]
        acc = jnp.zeros((ATT_BLK, 2 * ATT_BLK), F32)
        for b in range(N_BUCKETS):
            acc = jnp.where(bk == b, tab_ref[b, hh], acc)
        o_ref[...] = acc

    nh = len(ATT_GROUPS) * ATT_HG
    return _pcall(body, name="bias_build", grid=(nh,),
                  in_specs=[pl.BlockSpec(memory_space=pltpu.SMEM),
                            pl.BlockSpec((None, ATT_BLK, 2 * ATT_BLK), lambda hh: (hh // ATT_HG, 0, 0)), HBM_SPEC],
                  out_specs=pl.BlockSpec((None, ATT_BLK, 2 * ATT_BLK), lambda hh: (hh, 0, 0)),
                  out_shape=jax.ShapeDtypeStruct((nh, ATT_BLK, 2 * ATT_BLK), F32),
                  compiler_params=_params(("parallel",)))(rel_bias, buckets, after)


def _bias_grad(ds_sum, buckets):
    def body(ds_ref, bk_ref, o_ref):
        bk = bk_ref[...]
        ds = ds_ref[...]
        rows = lax.broadcasted_iota(jnp.int32, (N_BUCKETS, 128), 0)
        acc = jnp.zeros((N_BUCKETS, 128), F32)
        for b in range(N_BUCKETS):
            acc = jnp.where(rows == b, jnp.sum(jnp.where(bk == b, ds, 0.0)), acc)
        o_ref[...] = acc

    nh = len(ATT_GROUPS) * ATT_HG
    return _pcall(body, name="bias_grad", grid=(nh,),
                  in_specs=[pl.BlockSpec((None, ATT_BLK, 2 * ATT_BLK), lambda hh: (hh, 0, 0)),
                            pl.BlockSpec((None, ATT_BLK, 2 * ATT_BLK), lambda hh: (hh // ATT_HG, 0, 0))],
                  out_specs=pl.BlockSpec((None, N_BUCKETS, 128), lambda hh: (hh, 0, 0)),
                  out_shape=jax.ShapeDtypeStruct((nh, N_BUCKETS, 128), F32),
                  compiler_params=_params(("parallel",)))(ds_sum, buckets)


def _att_valid(n):
    qi = lax.broadcasted_iota(jnp.int32, (ATT_BLK, 2 * ATT_BLK), 0)
    kj = lax.broadcasted_iota(jnp.int32, (ATT_BLK, 2 * ATT_BLK), 1)
    m = ATT_BLK + qi - kj
    first_key = jnp.where(n > 0, 0, ATT_BLK)
    return (m >= 0) & (m <= ATT_BLK) & (kj >= first_key)


ATT_HP = (1, 2, 2)


def _att_geometry(gi):
    _, dil = ATT_GROUPS[gi]
    return dil, S // dil // ATT_BLK, ATT_HP[gi]


def _blk(dil, r, n):
    if dil == 1:
        return pl.ds(n * ATT_BLK, ATT_BLK)
    return pl.ds(r + n * ATT_BLK * dil, ATT_BLK, stride=dil)


def _slab_specs(gi):
    _, _, hp = _att_geometry(gi)
    per = ATT_HG // hp
    return [pl.BlockSpec((hp, S, ATT_DH), lambda g, r, part=part: ((3 * gi + part) * per + g, 0, 0))
            for part in range(3)]


def _head_specs(gi, count):
    _, _, hp = _att_geometry(gi)
    return [pl.BlockSpec((hp, S, ATT_DH), lambda g, r: (g, 0, 0))] * count


def _bias_spec(gi):
    _, _, hp = _att_geometry(gi)
    return pl.BlockSpec((hp, ATT_BLK, 2 * ATT_BLK), lambda g, r: (gi * (ATT_HG // hp) + g, 0, 0))


def _att_valid_first():
    qi = lax.broadcasted_iota(jnp.int32, (ATT_BLK, ATT_BLK), 0)
    kj = lax.broadcasted_iota(jnp.int32, (ATT_BLK, ATT_BLK), 1)
    return kj <= qi


def _att_fwd(slabs, bias, gi, comm=None):
    dil, nb, hp = _att_geometry(gi)
    scale = ATT_DH ** -0.5

    def body(q_ref, k_ref, v_ref, bias_ref, o_ref, l_ref):
        r = pl.program_id(1)
        for n in range(nb):
            cur = _blk(dil, r, n)
            valid = _att_valid(n) if n > 0 else _att_valid_first()
            for h in range(hp):
                if n > 0:
                    prev = _blk(dil, r, n - 1)
                    kk = jnp.concatenate([k_ref[h, prev, :], k_ref[h, cur, :]], axis=0)
                    vv = jnp.concatenate([v_ref[h, prev, :], v_ref[h, cur, :]], axis=0)
                    bias = bias_ref[h]
                else:
                    kk, vv, bias = k_ref[h, cur, :], v_ref[h, cur, :], bias_ref[h, :, pl.ds(ATT_BLK, ATT_BLK)]
                s = _dot(q_ref[h, cur, :], kk, NT) * scale + bias
                s = jnp.where(valid, s, -1e30)
                mx = jnp.max(s, axis=-1, keepdims=True)
                e = jnp.exp(s - mx)
                den = jnp.sum(e, axis=-1, keepdims=True)
                o_ref[h, cur, :] = _dot(e / den, vv, NN)
                l_ref[h, cur, :] = jnp.broadcast_to(mx + jnp.log(den), (ATT_BLK, ATT_DH))

    osh = pltpu.HBM((ATT_HG, S, ATT_DH), F32)
    kw = dict(name=f"att_fwd{gi}", grid=(ATT_HG // hp, dil), in_specs=_slab_specs(gi) + [_bias_spec(gi)],
              out_specs=tuple(_head_specs(gi, 2)), out_shape=(osh, osh))
    if comm is not None:
        return _carry(body, comm, **kw)(slabs, slabs, slabs, bias)
    return _pcall(body, compiler_params=_params(("parallel", "arbitrary")), **kw)(slabs, slabs, slabs, bias)


def _att_bwd(slabs, bias, o, lse, do, dlse, gi, comm=None):
    dil, nb, hp = _att_geometry(gi)
    per = ATT_HG // hp
    scale = ATT_DH ** -0.5
    wide = lambda t: jnp.concatenate([t, t], axis=1)

    def body(q_ref, k_ref, v_ref, bias_ref, o_ref, l_ref, do_ref, dl_ref, dq_ref, dk_ref, dv_ref, ds_ref):
        r = pl.program_id(1)

        @pl.when(r == 0)
        def _():
            ds_ref[...] = jnp.zeros_like(ds_ref)

        for h in range(hp):
            carry_k = carry_v = None
            for n in range(nb):
                cur = _blk(dil, r, n)
                q = q_ref[h, cur, :]
                dov = do_ref[h, cur, :]
                delta = jnp.sum(dov * o_ref[h, cur, :], axis=-1, keepdims=True)
                if n == 0:
                    own = pl.ds(ATT_BLK, ATT_BLK)
                    kk, vv = k_ref[h, cur, :], v_ref[h, cur, :]
                    s = _dot(q, kk, NT) * scale + bias_ref[h, :, own]
                    p = jnp.where(_att_valid_first(), jnp.exp(s - l_ref[h, cur, :]), 0.0)
                    ds = p * (_dot(dov, vv, NT) - delta + dl_ref[h, cur, :])
                    ds_ref[h, :, own] += ds
                    dq_ref[h, cur, :] = _dot(ds, kk, NN) * scale
                    carry_k, carry_v = _dot(ds, q, TN) * scale, _dot(p, dov, TN)
                    continue
                prev = _blk(dil, r, n - 1)
                kk = jnp.concatenate([k_ref[h, prev, :], k_ref[h, cur, :]], axis=0)
                vv = jnp.concatenate([v_ref[h, prev, :], v_ref[h, cur, :]], axis=0)
                s = _dot(q, kk, NT) * scale + bias_ref[h]
                p = jnp.where(_att_valid(n), jnp.exp(s - wide(l_ref[h, cur, :])), 0.0)
                dp = _dot(dov, vv, NT)
                ds = p * (dp - delta + wide(dl_ref[h, cur, :]))
                ds_ref[h] += ds
                dq_ref[h, cur, :] = _dot(ds, kk, NN) * scale
                dkk = _dot(ds, q, TN) * scale
                dvv = _dot(p, dov, TN)
                dk_ref[h, prev, :] = carry_k + dkk[:ATT_BLK]
                dv_ref[h, prev, :] = carry_v + dvv[:ATT_BLK]
                carry_k, carry_v = dkk[ATT_BLK:], dvv[ATT_BLK:]
            last = _blk(dil, r, nb - 1)
            dk_ref[h, last, :] = carry_k
            dv_ref[h, last, :] = carry_v

    osh = jax.ShapeDtypeStruct((ATT_HG, S, ATT_DH), F32)
    kw = dict(name=f"att_bwd{gi}", grid=(per, dil), in_specs=_slab_specs(gi) + [_bias_spec(gi)] + _head_specs(gi, 4),
              out_specs=(*_head_specs(gi, 3), pl.BlockSpec((hp, ATT_BLK, 2 * ATT_BLK), lambda g, r: (g, 0, 0))),
              out_shape=(osh, osh, osh, jax.ShapeDtypeStruct((ATT_HG, ATT_BLK, 2 * ATT_BLK), F32)))
    args = (slabs, slabs, slabs, bias, o, lse, do, dlse)
    if comm is not None:
        return _carry(body, comm, **kw)(*args)
    return _pcall(body, compiler_params=_params(("arbitrary", "arbitrary")), **kw)(*args)


AW = ATT_HG * ATT_DH


def _mix_weights(l0, l1, l2):
    mx = jnp.maximum(jnp.maximum(l0, l1), l2)
    e0, e1, e2 = jnp.exp(l0 - mx), jnp.exp(l1 - mx), jnp.exp(l2 - mx)
    den = e0 + e1 + e2
    return e0 / den, e1 / den, e2 / den


def _heads_spec():
    return pl.BlockSpec((ATT_HG, TR, ATT_DH), lambda i: (0, i, 0))


def _mix_fwd(os_, ls, comm=None):
    def body(o0, o1, o2, l0, l1, l2, att_ref):
        for h in range(ATT_HG):
            w0, w1, w2 = _mix_weights(l0[h], l1[h], l2[h])
            att_ref[:, h * ATT_DH:(h + 1) * ATT_DH] = (w0 * o0[h] + w1 * o1[h] + w2 * o2[h]).astype(BF16)

    kw = dict(name="mix_fwd", grid=(S // TR,), in_specs=[_heads_spec()] * 6, out_specs=_row_spec(AW),
              out_shape=jax.ShapeDtypeStruct((S, AW), BF16))
    if comm is not None:
        return _carry(body, comm, **kw)(*os_, *ls)
    return _pcall(body, compiler_params=_params(("parallel",)), **kw)(*os_, *ls)


def _mix_bwd(os_, ls, datt):
    def body(o0, o1, o2, l0, l1, l2, da_ref, d0, d1, d2, e0, e1, e2):
        for h in range(ATT_HG):
            ws = _mix_weights(l0[h], l1[h], l2[h])
            da = da_ref[:, h * ATT_DH:(h + 1) * ATT_DH]
            dws = []
            for o_ref, w, d_ref in zip((o0, o1, o2), ws, (d0, d1, d2)):
                d_ref[h] = w * da
                dws.append(jnp.broadcast_to(jnp.sum(da * o_ref[h], axis=-1, keepdims=True), (TR, ATT_DH)))
            tot = ws[0] * dws[0] + ws[1] * dws[1] + ws[2] * dws[2]
            for w, dw, e_ref in zip(ws, dws, (e0, e1, e2)):
                e_ref[h] = w * (dw - tot)

    o = pltpu.HBM((ATT_HG, S, ATT_DH), F32)
    return _pcall(body, name="mix_bwd", grid=(S // TR,), in_specs=[_heads_spec()] * 6 + [_row_spec(AW)],
                  out_specs=(_heads_spec(),) * 6, out_shape=(o,) * 6,
                  compiler_params=_params(("parallel",)))(*os_, *ls, datt)


def _ada_fwd(c_all, w_sh, b_sl):
    def body(c_ref, w_ref, b_ref, o_ref):
        cv = c_ref[...]
        o_ref[...] = _dot(cv * jax.nn.sigmoid(cv), w_ref[...], NN) + b_ref[...]

    return _pcall(body, name="ada_fwd", out_shape=jax.ShapeDtypeStruct((N_DEV, w_sh.shape[1]), F32),
                  compiler_params=_params())(c_all, w_sh, b_sl)


CAST_STEPS = 4


def _to_bf16(arrs, comm, name):
    n = len(arrs)

    def body(*refs):
        for src, dst in zip(refs[:n], refs[n:]):
            dst[...] = src[...].astype(BF16)

    blocks = [pl.BlockSpec((a.shape[0] // CAST_STEPS, a.shape[1]), lambda i: (i, 0)) for a in arrs]
    return _carry(body, comm, name=name, grid=(CAST_STEPS,), in_specs=blocks, out_specs=tuple(blocks),
                  out_shape=tuple(pltpu.HBM(a.shape, BF16) for a in arrs))(*arrs)


def _ada_bwd(c_all, dm_sl):
    def body(c_ref, d_ref, o_ref):
        cv = c_ref[...]
        o_ref[...] = _dot(cv * jax.nn.sigmoid(cv), d_ref[...], TN)

    return _pcall(body, name="ada_bwd", out_shape=jax.ShapeDtypeStruct((D, dm_sl.shape[1]), F32),
                  compiler_params=_params())(c_all, dm_sl)


N_MOD = 6


def _sum_small(gathered, widths):
    def body(g_ref, gb_ref, dm_ref, *outs):
        def total(cols):
            acc = g_ref[0, :, cols]
            for e in range(1, N_DEV):
                acc = acc + g_ref[e, :, cols]
            return acc

        gb_ref[...] = total(slice(0, N_MOD * D))
        for e in range(N_DEV):
            dm_ref[e:e + 1, :] = g_ref[e, :, :N_MOD * D]
        col = N_MOD * D
        for w, o_ref in zip(widths, outs):
            o_ref[...] = total(slice(col, col + w))
            col += w

    assert gathered.shape == (N_DEV, 1, N_MOD * D + sum(widths))
    shapes = (jax.ShapeDtypeStruct((1, N_MOD * D), F32), jax.ShapeDtypeStruct((N_DEV, N_MOD * D), F32),
              *[jax.ShapeDtypeStruct((1, w), F32) for w in widths])
    res = _pcall(body, name="sum_small", out_shape=shapes, compiler_params=_params())(gathered)
    return res[0], res[1], res[2:]


def _row_tile(m, n):
    t = max(8, min(m, (1 << 19) // n // 8 * 8))
    while m % t:
        t -= 8
    return t


def _pair_sum(full, recv, sel, name, col_block=0):
    _, m, n = recv.shape
    t = m

    def body(sel_ref, a_ref, b_ref, o_ref):
        o_ref[...] = (a_ref[...].astype(F32) + b_ref[...].astype(F32)).astype(o_ref.dtype)

    gs = pltpu.PrefetchScalarGridSpec(
        num_scalar_prefetch=1, grid=(4, m // t),
        in_specs=[pl.BlockSpec((None, None, t, n), lambda q, i, s: (q, s[0], i, col_block)),
                  pl.BlockSpec((None, t, n), lambda q, i, s: (q, i, 0))],
        out_specs=pl.BlockSpec((None, t, n), lambda q, i, s: (q, i, 0)))
    return _pcall(body, name=name, grid_spec=gs, out_shape=pltpu.HBM((4, m, n), full.dtype),
                  compiler_params=_params(("parallel", "parallel")))(sel, full, recv)


def _chip_sum(part, recv, sel, name):
    _, m, n = part.shape
    t = _row_tile(m, n)

    def body(sel_ref, a_ref, r_ref, o_ref):
        o_ref[...] = ((a_ref[...].astype(F32) + r_ref[0].astype(F32)) + r_ref[1].astype(F32)) + r_ref[2].astype(F32)

    gs = pltpu.PrefetchScalarGridSpec(
        num_scalar_prefetch=1, grid=(m // t,),
        in_specs=[pl.BlockSpec((None, t, n), lambda i, s: (s[0], i, 0)),
                  pl.BlockSpec((3, t, n), lambda i, s: (0, i, 0))],
        out_specs=pl.BlockSpec((t, n), lambda i, s: (i, 0)))
    return _pcall(body, name=name, grid_spec=gs, out_shape=jax.ShapeDtypeStruct((m, n), F32),
                  compiler_params=_params(("parallel",)))(sel, part, recv)


def _adamw_math(w, g, m, v):
    nm = ADAM_B1 * m + (1.0 - ADAM_B1) * g
    nv = ADAM_B2 * v + (1.0 - ADAM_B2) * (g * g)
    m_hat = nm / (1.0 - ADAM_B1 ** ADAM_STEP)
    v_hat = nv / (1.0 - ADAM_B2 ** ADAM_STEP)
    return -ADAM_LR * (m_hat / (jnp.sqrt(v_hat) + ADAM_EPS) + ADAM_WD * w), nm, nv


def _adamw(w, g, m, v, name):
    _, rows, cols = w.shape
    t = _row_tile(rows, cols)

    def body(w_ref, g_ref, m_ref, v_ref, d_ref, nm_ref, nv_ref):
        d_ref[...], nm_ref[...], nv_ref[...] = _adamw_math(w_ref[...], g_ref[...], m_ref[...], v_ref[...])

    spec3 = pl.BlockSpec((None, t, cols), lambda i: (0, i, 0))
    spec2 = pl.BlockSpec((t, cols), lambda i: (i, 0))
    o = jax.ShapeDtypeStruct(w.shape, F32)
    return _pcall(body, name=name, grid=(rows // t,), in_specs=[spec3, spec2, spec3, spec3], out_specs=(spec3,) * 3,
                  out_shape=(o, o, o), compiler_params=_params(("parallel",)))(w, g, m, v)


def _adamw_reduced1(w, m, v, part, recv, sel, name):
    _, rows, cols = w.shape
    t = _row_tile(rows, cols)

    def body(sel_ref, w_ref, m_ref, v_ref, p_ref, r_ref, g_ref, d_ref, nm_ref, nv_ref):
        g = ((p_ref[...].astype(F32) + r_ref[0].astype(F32)) + r_ref[1].astype(F32)) + r_ref[2].astype(F32)
        g_ref[...] = g
        d_ref[...], nm_ref[...], nv_ref[...] = _adamw_math(w_ref[...], g, m_ref[...], v_ref[...])

    wspec = pl.BlockSpec((None, t, cols), lambda i, s: (0, i, 0))
    gs = pltpu.PrefetchScalarGridSpec(
        num_scalar_prefetch=1, grid=(rows // t,),
        in_specs=[wspec, wspec, wspec, pl.BlockSpec((None, t, cols), lambda i, s: (s[0], i, 0)),
                  pl.BlockSpec((3, t, cols), lambda i, s: (0, i, 0))],
        out_specs=(wspec,) * 4)
    o = jax.ShapeDtypeStruct(w.shape, F32)
    return _pcall(body, name=name, grid_spec=gs, out_shape=(o, o, o, o),
                  compiler_params=_params(("parallel",)))(sel, w, m, v, part, recv)


def _adamw_reduced(w, m, v, parts, recvs, sel):
    _, rows, cols = w.shape
    half = cols // 2
    t = _row_tile(rows, half)

    def body(sel_ref, w_ref, m_ref, v_ref, pa_ref, pb_ref, ra_ref, rb_ref, g_ref, d_ref, nm_ref, nv_ref):
        total = lambda p_ref, r_ref: ((p_ref[...].astype(F32) + r_ref[0].astype(F32)) + r_ref[1].astype(F32)) \
            + r_ref[2].astype(F32)
        g = jnp.where(pl.program_id(1) == 0, total(pa_ref, ra_ref), total(pb_ref, rb_ref))
        g_ref[...] = g
        d_ref[...], nm_ref[...], nv_ref[...] = _adamw_math(w_ref[...], g, m_ref[...], v_ref[...])

    wspec = pl.BlockSpec((None, t, half), lambda i, j, s: (0, i, j))
    pspec = pl.BlockSpec((None, t, half), lambda i, j, s: (s[0], i, 0))
    rspec = pl.BlockSpec((3, t, half), lambda i, j, s: (0, i, 0))
    gs = pltpu.PrefetchScalarGridSpec(num_scalar_prefetch=1, grid=(rows // t, 2),
                                      in_specs=[wspec, wspec, wspec, pspec, pspec, rspec, rspec],
                                      out_specs=(wspec,) * 4)
    o = jax.ShapeDtypeStruct(w.shape, F32)
    return _pcall(body, name="adamw_w_in", grid_spec=gs, out_shape=(o, o, o, o),
                  compiler_params=_params(("parallel", "arbitrary")))(sel, w, m, v, *parts, *recvs)


def _adamw_small(ws, gs, ms, vs):
    n = len(ws)

    def body(*refs):
        for i in range(n):
            w_ref, g_ref, m_ref, v_ref = (refs[k * n + i] for k in range(4))
            d, nm, nv = _adamw_math(w_ref[...], g_ref[...], m_ref[...], v_ref[...])
            refs[4 * n + i][...] = d
            refs[5 * n + i][...] = nm
            refs[6 * n + i][...] = nv

    shapes = tuple(jax.ShapeDtypeStruct(w.shape, F32) for w in ws)
    res = _pcall(body, name="adamw_small", out_shape=shapes * 3, compiler_params=_params())(*ws, *gs, *ms, *vs)
    return res[:n], res[n:2 * n], res[2 * n:]


def _mesh_pos():
    return lax.axis_index("x"), lax.axis_index("y"), lax.axis_index("c")


class _GatherSmallDirect:
    ins_in_vmem = True

    def __init__(self, arrs):
        self.ins = list(arrs)
        self.out_shape = tuple(jax.ShapeDtypeStruct((N_DEV,) + a.shape, a.dtype) for a in arrs)
        n = len(arrs)
        self.sems = [pltpu.SemaphoreType.DMA((7 * n,)), pltpu.SemaphoreType.DMA((7 * n,)),
                     pltpu.SemaphoreType.DMA((n,))]

    def _copies(self, ins, outs, sems):
        send_sems, recv_sems, local_sems = sems
        x, y, c = _mesh_pos()
        me = _slot((x, y, c))
        mine = [pltpu.make_async_copy(ins[p], outs[p].at[me], local_sems.at[p]) for p in range(len(self.ins))]
        sends, arrivals = [], []
        for p in range(len(self.ins)):
            for j in range(1, N_DEV):
                peer = (me + j) % N_DEV
                to = (peer // 4, (peer // 2) % 2, peer % 2)
                sends.append(pltpu.make_async_remote_copy(
                    src_ref=ins[p], dst_ref=outs[p].at[me], send_sem=send_sems.at[7 * p + j - 1],
                    recv_sem=recv_sems.at[7 * p + (N_DEV - j) - 1], device_id=to, device_id_type=MESH))
                arrivals.append(pltpu.make_async_remote_copy(
                    src_ref=ins[p], dst_ref=outs[p].at[peer], send_sem=send_sems.at[7 * p + j - 1],
                    recv_sem=recv_sems.at[7 * p + j - 1], device_id=to, device_id_type=MESH))
        return mine, sends, arrivals

    def start(self, ins, outs, sems):
        mine, sends, _ = self._copies(ins, outs, sems)
        for cp in mine + sends:
            cp.start()

    def finish(self, ins, outs, sems):
        mine, sends, arrivals = self._copies(ins, outs, sems)
        for cp in arrivals:
            cp.wait_recv()
        for cp in sends:
            cp.wait_send()
        for cp in mine:
            cp.wait()


class _ExchangeCore:
    def __init__(self, fulls, cols=None):
        self.ins = list(fulls)
        self.cols = cols
        width = lambda f: f.shape[3] if cols is None else cols[1]
        self.out_shape = tuple(jax.ShapeDtypeStruct((4, f.shape[2], width(f)), f.dtype) for f in fulls)
        self.sems = [pltpu.SemaphoreType.DMA((4 * len(fulls),)), pltpu.SemaphoreType.DMA((4 * len(fulls),))]

    def _copies(self, ins, outs, sems):
        send_sems, recv_sems = sems
        x, y, c = _mesh_pos()

        def src(a, q):
            ref = ins[a].at[q, 1 - c]
            return ref if self.cols is None else ref.at[:, pl.ds(*self.cols)]

        return [pltpu.make_async_remote_copy(
            src_ref=src(a, q), dst_ref=outs[a].at[q], send_sem=send_sems.at[4 * a + q],
            recv_sem=recv_sems.at[4 * a + q], device_id=(x, y, 1 - c), device_id_type=MESH)
            for a in range(len(self.ins)) for q in range(4)]

    def start(self, ins, outs, sems):
        for cp in self._copies(ins, outs, sems):
            cp.start()

    def finish(self, ins, outs, sems):
        for cp in self._copies(ins, outs, sems):
            cp.wait()


class _ExchangeChip:
    def __init__(self, parts):
        self.ins = list(parts)
        self.out_shape = tuple(jax.ShapeDtypeStruct((3,) + p.shape[1:], p.dtype) for p in parts)
        self.sems = [pltpu.SemaphoreType.DMA((3 * len(parts),)), pltpu.SemaphoreType.DMA((3 * len(parts),))]

    def _copies(self, ins, outs, sems):
        send_sems, recv_sems = sems
        x, y, c = _mesh_pos()
        chips = [(1 - x, y), (x, 1 - y), (1 - x, 1 - y)]
        return [pltpu.make_async_remote_copy(
            src_ref=ins[a].at[2 * px + py], dst_ref=outs[a].at[j], send_sem=send_sems.at[3 * a + j],
            recv_sem=recv_sems.at[3 * a + j], device_id=(px, py, c), device_id_type=MESH)
            for a in range(len(self.ins)) for j, (px, py) in enumerate(chips)]

    def start(self, ins, outs, sems):
        for cp in self._copies(ins, outs, sems):
            cp.start()

    def finish(self, ins, outs, sems):
        for cp in self._copies(ins, outs, sems):
            cp.wait()


HBM_ONLY = pl.BlockSpec(memory_space=pltpu.HBM)
SEM_SPEC = pl.BlockSpec(memory_space=pltpu.SEMAPHORE)
SIDE_EFFECT = pltpu.SideEffectType.DATAFLOW_SIDE_EFFECTING


def _chip_copies(p_refs, land_refs, send_sems, recv_sems):
    x, y, c = _mesh_pos()
    return [pltpu.make_async_remote_copy(
        src_ref=p_refs[a].at[2 * px + py], dst_ref=land_refs[a].at[j], send_sem=send_sems.at[3 * a + j],
        recv_sem=recv_sems.at[3 * a + j], device_id=(px, py, c), device_id_type=MESH)
        for a in range(len(p_refs)) for j, (px, py) in enumerate([(1 - x, y), (x, 1 - y), (1 - x, 1 - y)])]


def _chip_exchange_start(parts, name):
    n = len(parts)
    lands = [lax.empty((3,) + p.shape[1:], p.dtype) for p in parts]

    def body(*refs):
        p_refs, land_refs, (send_sems, recv_sems) = refs[:n], refs[n:2 * n], refs[2 * n:2 * n + 2]
        for cp in _chip_copies(p_refs, land_refs, send_sems, recv_sems):
            cp.start()
        token = refs[-1]
        token[...] = jnp.zeros_like(token)

    hbm = lambda t: pltpu.HBM(t.shape, t.dtype)
    res = pl.pallas_call(
        body, name=name,
        out_shape=(pltpu.SemaphoreType.DMA((3 * n,)), pltpu.SemaphoreType.DMA((3 * n,)), *[hbm(t) for t in parts + lands],
                   jax.ShapeDtypeStruct((8, 128), F32)),
        in_specs=(HBM_ONLY,) * (2 * n),
        out_specs=(SEM_SPEC, SEM_SPEC, *[HBM_ONLY] * (2 * n), pl.BlockSpec(memory_space=pltpu.VMEM)),
        input_output_aliases={i: 2 + i for i in range(2 * n)},
        compiler_params=pltpu.CompilerParams(has_side_effects=SIDE_EFFECT))(
        *[pltpu.with_memory_space_constraint(t, pltpu.HBM) for t in parts + lands])
    return (res[0], res[1], list(res[2:2 + n]), list(res[2 + n:2 + 2 * n])), res[-1]


def _chip_exchange_wait(in_flight, after, name):
    send_sems, recv_sems, parts, lands = in_flight
    n = len(parts)

    def body(*refs):
        p_refs, land_refs, (send_sems, recv_sems) = refs[:n], refs[n:2 * n], refs[2 * n:2 * n + 2]
        for cp in _chip_copies(p_refs, land_refs, send_sems, recv_sems):
            cp.wait_send()
            cp.wait_recv()

    res = pl.pallas_call(
        body, name=name, out_shape=tuple(pltpu.HBM(t.shape, t.dtype) for t in parts + lands),
        in_specs=(*[HBM_ONLY] * (2 * n), SEM_SPEC, SEM_SPEC, pl.BlockSpec(memory_space=pl.ANY)),
        out_specs=(HBM_ONLY,) * (2 * n), input_output_aliases={i: i for i in range(2 * n)},
        compiler_params=pltpu.CompilerParams(has_side_effects=SIDE_EFFECT))(*parts, *lands, send_sems, recv_sems, after)
    return list(res[:n]), list(res[n:])


def _slot(p):
    return 4 * p[0] + 2 * p[1] + p[2]


def _gather_copies(src_refs, out_refs, send_sems, recv_sems):
    x, y, c = _mesh_pos()
    targets = [(x, y, 1 - c), (1 - x, y, c), (x, 1 - y, c), (1 - x, 1 - y, c)]
    return [pltpu.make_async_remote_copy(
        src_ref=src_refs[a], dst_ref=out_refs[a].at[_slot((x, y, c))], send_sem=send_sems.at[4 * a + k],
        recv_sem=recv_sems.at[4 * a + k], device_id=to, device_id_type=MESH)
        for a in range(len(src_refs)) for k, to in enumerate(targets)]


def _gather_start(shards, after, name):
    n = len(shards)
    outs = [lax.empty((N_DEV,) + s.shape, s.dtype) for s in shards]

    def body(*refs):
        for cp in _gather_copies(refs[:n], refs[n:2 * n], refs[2 * n + 1], refs[2 * n + 2]):
            cp.start()
        token = refs[-1]
        token[...] = jnp.zeros_like(token)

    res = pl.pallas_call(
        body, name=name,
        out_shape=(pltpu.SemaphoreType.DMA((4 * n,)), pltpu.SemaphoreType.DMA((4 * n,)),
                   *[pltpu.HBM(t.shape, t.dtype) for t in shards + outs], jax.ShapeDtypeStruct((8, 128), F32)),
        in_specs=(*[HBM_ONLY] * (2 * n), pl.BlockSpec(memory_space=pl.ANY)),
        out_specs=(SEM_SPEC, SEM_SPEC, *[HBM_ONLY] * (2 * n), pl.BlockSpec(memory_space=pltpu.VMEM)),
        input_output_aliases={i: 2 + i for i in range(2 * n)},
        compiler_params=pltpu.CompilerParams(has_side_effects=SIDE_EFFECT))(
        *[pltpu.with_memory_space_constraint(t, pltpu.HBM) for t in shards + outs], after)
    return (res[0], res[1], list(res[2:2 + n]), list(res[2 + n:2 + 2 * n])), res[-1]


def _gather_wait(in_flight, after, name):
    send_sems, recv_sems, shards, outs = in_flight
    n = len(shards)

    def body(*refs):
        for cp in _gather_copies(refs[:n], refs[n:2 * n], refs[2 * n], refs[2 * n + 1]):
            cp.wait_send()
            cp.wait_recv()

    res = pl.pallas_call(
        body, name=name, out_shape=tuple(pltpu.HBM(t.shape, t.dtype) for t in shards + outs),
        in_specs=(*[HBM_ONLY] * (2 * n), SEM_SPEC, SEM_SPEC, pl.BlockSpec(memory_space=pl.ANY)),
        out_specs=(HBM_ONLY,) * (2 * n), input_output_aliases={i: i for i in range(2 * n)},
        compiler_params=pltpu.CompilerParams(has_side_effects=SIDE_EFFECT))(*shards, *outs, send_sems, recv_sems, after)
    return list(res[:n]), list(res[n:])


class _PassToSibling:
    def __init__(self, shards, gathered):
        n = self.n = len(shards)
        self.ins = list(shards) + list(gathered)
        self.out_shape = tuple(jax.ShapeDtypeStruct(g.shape, g.dtype) for g in gathered)
        self.aliases = {n + a: a for a in range(n)}
        self.sems = [pltpu.SemaphoreType.DMA((3 * n,)), pltpu.SemaphoreType.DMA((3 * n,)),
                     pltpu.SemaphoreType.DMA((n,))]

    def _copies(self, ins, outs, sems):
        send_sems, recv_sems, local_sems = sems
        x, y, c = _mesh_pos()
        chips = [(1 - x, y), (x, 1 - y), (1 - x, 1 - y)]
        mine = [pltpu.make_async_copy(ins[a], outs[a].at[_slot((x, y, c))], local_sems.at[a]) for a in range(self.n)]
        passed, awaited = [], []
        for a in range(self.n):
            for j, chip in enumerate(chips):
                sems_j = dict(send_sem=send_sems.at[3 * a + j], recv_sem=recv_sems.at[3 * a + j],
                              device_id=(x, y, 1 - c), device_id_type=MESH)
                blk = outs[a].at[_slot((*chip, c))]
                passed.append(pltpu.make_async_remote_copy(src_ref=blk, dst_ref=blk, **sems_j))
                got = outs[a].at[_slot((*chip, 1 - c))]
                awaited.append(pltpu.make_async_remote_copy(src_ref=got, dst_ref=got, **sems_j))
        return mine, passed, awaited

    def start(self, ins, outs, sems):
        mine, passed, _ = self._copies(ins, outs, sems)
        for cp in mine + passed:
            cp.start()

    def finish(self, ins, outs, sems):
        mine, passed, awaited = self._copies(ins, outs, sems)
        for cp in passed:
            cp.wait_send()
        for cp in awaited:
            cp.wait_recv()
        for cp in mine:
            cp.wait()


def _reduce_sums(fulls, recv_core, core, tag):
    return [_pair_sum(f, r, core, f"rs_pair_{tag}{i}") for i, (f, r) in enumerate(zip(fulls, recv_core))]


def _local_step(x, tgt, mods, w_in_shard, order, shards, small, chip, core):
    sh1, sc1, g1, sh2, sc2, g2 = mods
    norm1_g, rel_bias, gn_g, gn_b, norm2_g, norm_f_g = small
    tables = _ret_tables()
    buckets = jnp.asarray(_bucket_tables())

    proj, slabs, w_in_t, h1 = _gather_proj(x, norm1_g, sh1, sc1, w_in_shard, order)
    flight_w1, token_w = _gather_start(list(shards[:3]), proj, "gather_w1_start")
    flight_w2, token_w = _gather_start(list(shards[3:]), token_w, "gather_w2_start")
    bias = _bias_build(rel_bias, buckets, token_w)
    outs, lses = [], []
    for gi in range(len(ATT_GROUPS)):
        o, l = _att_fwd(slabs, bias, gi)
        outs.append(o)
        lses.append(l)
    att, gathered = _mix_fwd(outs, lses, comm=_PassToSibling(*_gather_wait(flight_w1, lses[2], "gather_w1_wait")))
    w_ret_out, w_att_out, w_o = (_from_slots(g, ax) for g, ax in zip(gathered, BIG_AXES[1:4]))
    gated, ro, states = _ret_fwd(proj, tables, gn_g, gn_b, att)
    ret_out, gathered = _mm(gated, w_ret_out, 'nn', tm=S, tn=256, tk=2048, name="ret_out",
                            comm=_PassToSibling(*_gather_wait(flight_w2, gated, "gather_w2_wait")))
    w_ff1, w_ff2 = (_from_slots(g, ax) for g, ax in zip(gathered, BIG_AXES[4:]))
    att_out, merged = _att_out_merge(att, w_att_out, proj, ret_out)
    mixo, x1, h2 = _w_o_norm2(merged, w_o, x, g1, norm2_g, sh2, sc2)
    u, act = _mm(h2, w_ff1, 'nn', tm=S, tn=512, tk=D, name="ff1", relu2=True)
    loss, dx2, g_normf, df, dg2 = _ff2_final(act, w_ff2, x1, g2, tgt, norm_f_g)

    gw_ff2 = _mm(act, df, 'tn', tm=512, tn=D, tk=S, name="gw_ff2", out_dtype=BF16)
    du = _mm(df, w_ff2, 'nt', tm=S, tn=512, tk=D, name="d_act", out_dtype=BF16, relu2_of=u)
    gw_ff1 = _mm(h2, du, 'tn', tm=D, tn=512, tk=S, name="gw_ff1", out_dtype=BF16)
    fulls_a = [_to_slots(g, ax) for g, ax in zip((gw_ff1, gw_ff2), BIG_AXES[4:])]
    dh2, recv_core_a = _mm(du, w_ff1, 'nt', tm=1024, tn=1024, tk=2048, name="dh2", comm=_ExchangeCore(fulls_a))
    parts_a = _reduce_sums(fulls_a, recv_core_a, core, "a")
    flight_a, token_a = _chip_exchange_start(parts_a, "rs_a_start")
    dx1, dsc2, dsh2, g_norm2, dmixo, dg1 = _norm_mod_bwd(x1, norm2_g, sc2, dh2, dx2, "norm2_bwd", gate=(mixo, g1))

    gw_o = _mm(merged, dmixo, 'tn', tm=D, tn=512, tk=S, name="gw_o", out_dtype=BF16, after=token_a)
    d_ret_out, d_att_out, dga, dgb = _dmerged_split(dmixo, w_o, proj, ret_out, att_out)
    gw_ret_out = _mm(gated, d_ret_out, 'tn', tm=512, tn=D, tk=S, name="gw_ret_out", out_dtype=BF16)
    gw_att_out = _mm(att, d_att_out, 'tn', tm=AW, tn=D, tk=S, name="gw_att_out", out_dtype=BF16)
    fulls_b = [_to_slots(g, ax) for g, ax in zip((gw_ret_out, gw_att_out, gw_o), BIG_AXES[1:4])]
    dgated, recv_core_b = _mm(d_ret_out, w_ret_out, 'nt', tm=S, tn=512, tk=D, name="dgated",
                              comm=_ExchangeCore(fulls_b))
    parts_b = _reduce_sums(fulls_b, recv_core_b, core, "b")
    flight_b, token_b = _chip_exchange_start(parts_b, "rs_b_start")
    datt = _mm(d_att_out, w_att_out, 'nt', tm=S, tn=AW, tk=D, name="datt", after=token_b)
    mix_grads = _mix_bwd(outs, lses, datt)
    datt_parts, ds_sums = [], []
    for gi in range(len(ATT_GROUPS)):
        dq, dk, dv, ds_sum = _att_bwd(slabs, bias, outs[gi], lses[gi], mix_grads[gi], mix_grads[3 + gi], gi)
        datt_parts += [dq, dk, dv]
        ds_sums.append(ds_sum)
    g_bias = _bias_grad(jnp.concatenate(ds_sums, axis=0), buckets)[:, :, 0].T.reshape(1, -1)
    dproj, g_gn_g, g_gn_b = _ret_bwd(proj, tables, gn_g, gn_b, ro, states, dgated, datt_parts + [dga, dgb])
    parts_a, recv_chip_a = _chip_exchange_wait(flight_a, dproj, "rs_a_wait")
    parts_b, recv_chip_b = _chip_exchange_wait(flight_b, dproj, "rs_b_wait")
    reduced = list(zip(parts_b + parts_a, recv_chip_b + recv_chip_a))
    full_in = _to_slots(_mm(dproj, h1, 'tn', tm=512, tn=D, tk=S, name="gw_in", out_dtype=BF16), 0)
    halves = [(half * (D // 2), D // 2) for half in range(2)]
    (recv_core_in0,) = _run_comm(_ExchangeCore([full_in], cols=halves[0]), "rs_core_in0")
    flight0, token = _chip_exchange_start([_pair_sum(full_in, recv_core_in0, core, "rs_pair_c0", col_block=0)],
                                          "rs_in0_start")
    dh1, (recv_core_in1,) = _mm(dproj, w_in_t, 'nn', tm=1024, tn=1024, tk=2560, name="dh1",
                                comm=_ExchangeCore([full_in], cols=halves[1]), after=token)
    flight1, token = _chip_exchange_start([_pair_sum(full_in, recv_core_in1, core, "rs_pair_c1", col_block=1)],
                                          "rs_in1_start")
    in_flight = [flight0, flight1]
    gx, dsc1, dsh1, g_norm1 = _norm_mod_bwd(x, norm1_g, sc1, dh1, dx1, "norm1_bwd", after=token)

    dmod = [dsh1, dsc1, dg1, dsh2, dsc2, dg2]
    small_g = [g_norm1, g_bias, g_gn_g, g_gn_b, g_norm2, g_normf]
    return loss, gx, in_flight, reduced, small_g, dmod


def _to_slots(g, axis):
    if axis == 0:
        return g.reshape(4, 2, g.shape[0] // N_DEV, g.shape[1])
    return g.reshape(g.shape[0], N_DEV, g.shape[1] // N_DEV).transpose(1, 0, 2).reshape(4, 2, g.shape[0], -1)


def _from_slots(w8, axis):
    if axis == 0:
        return w8.reshape(-1, w8.shape[2])
    return w8.transpose(1, 0, 2).reshape(w8.shape[1], -1)


BIG_AXES = (1, 0, 1, 0, 1, 0)


def kernel(x, c, w_ada, b_ada, norm1_g, w_in, rel_bias, ret_gn_g, ret_gn_b, w_ret_out, w_att_out, w_o, norm2_g, w_ff1, w_ff2, norm_f_g, loss_target, m_w_ada, m_b_ada, m_norm1_g, m_w_in, m_rel_bias, m_ret_gn_g, m_ret_gn_b, m_w_ret_out, m_w_att_out, m_w_o, m_norm2_g, m_w_ff1, m_w_ff2, m_norm_f_g, v_w_ada, v_b_ada, v_norm1_g, v_w_in, v_rel_bias, v_ret_gn_g, v_ret_gn_b, v_w_ret_out, v_w_att_out, v_w_o, v_norm2_g, v_w_ff1, v_w_ff2, v_norm_f_g):
    mx, my, mc = _mesh_pos()
    dev = 4 * mx + 2 * my + mc
    chip = jnp.reshape(2 * mx + my, (1,)).astype(jnp.int32)
    core = jnp.reshape(mc, (1,)).astype(jnp.int32)
    ada_w = D * 6 // N_DEV

    w_in, m_w_in, v_w_in = (jnp.transpose(t, (0, 2, 1)) for t in (w_in, m_w_in, v_w_in))

    (w_in_shard,), (c_all,) = _to_bf16([w_in[0]], _GatherSmallDirect([c]), "gather_c")
    c_all = c_all.reshape(N_DEV, D)
    b_sl = lax.dynamic_slice(b_ada, (0, dev * ada_w), (1, ada_w))
    other_shards, (mod_all,) = _to_bf16([w[0] for w in (w_ret_out, w_att_out, w_o, w_ff1, w_ff2)],
                                        _GatherSmallDirect([_ada_fwd(c_all, w_ada[0], b_sl)]), "gather_mod")
    mod = lax.dynamic_index_in_dim(mod_all, dev, axis=1, keepdims=False).reshape(6, D)
    mods = tuple(mod[i:i + 1] for i in range(6))

    small = (norm1_g, rel_bias, ret_gn_g, ret_gn_b, norm2_g, norm_f_g.reshape(1, D))
    order = lax.dynamic_index_in_dim(jnp.asarray(_proj_order()), 2 * mx + my, axis=0, keepdims=False)
    loss, gx, in_flight, big_red, small_g, dmod = _local_step(x[0], loss_target[0], mods, w_in_shard, order,
                                                              list(other_shards), small, chip, core)

    names = ['w_ada', 'b_ada', 'norm1_g', 'w_in', 'rel_bias', 'ret_gn_g', 'ret_gn_b', 'w_ret_out', 'w_att_out',
             'w_o', 'norm2_g', 'w_ff1', 'w_ff2', 'norm_f_g']
    ws = dict(zip(names, (w_ada, b_ada, norm1_g, w_in, rel_bias, ret_gn_g, ret_gn_b, w_ret_out, w_att_out, w_o,
                          norm2_g, w_ff1, w_ff2, norm_f_g)))
    ms = dict(zip(names, (m_w_ada, m_b_ada, m_norm1_g, m_w_in, m_rel_bias, m_ret_gn_g, m_ret_gn_b, m_w_ret_out,
                          m_w_att_out, m_w_o, m_norm2_g, m_w_ff1, m_w_ff2, m_norm_f_g)))
    vs = dict(zip(names, (v_w_ada, v_b_ada, v_norm1_g, v_w_in, v_rel_bias, v_ret_gn_g, v_ret_gn_b, v_w_ret_out,
                          v_w_att_out, v_w_o, v_norm2_g, v_w_ff1, v_w_ff2, v_norm_f_g)))
    grads, delta, new_m, new_v = {}, {}, {}, {}
    big_names = ('w_ret_out', 'w_att_out', 'w_o', 'w_ff1', 'w_ff2')
    for n, (part, recv) in zip(big_names, big_red):
        grads[n], delta[n], new_m[n], new_v[n] = _adamw_reduced1(ws[n], ms[n], vs[n], part, recv, chip, "adamw_" + n)
    updated = lax.optimization_barrier((gx, tuple(delta[n] for n in big_names)))
    rows = dmod + small_g + [loss]
    (gathered,) = _run_comm(_GatherSmallDirect([jnp.concatenate(rows, axis=1)]), "gather_small", after=updated[0])
    g_b_ada, dmod_all, (g_norm1, g_bias, g_gn_g, g_gn_b, g_norm2, g_normf, loss_sum) = _sum_small(
        gathered, [r.shape[1] for r in rows[N_MOD:]])
    loss_out = loss_sum[0, 0]
    g_w_ada = _ada_bwd(c_all, lax.dynamic_slice(dmod_all, (0, dev * ada_w), (N_DEV, ada_w)))

    grads.update(w_ada=g_w_ada.reshape(w_ada.shape), b_ada=g_b_ada, norm1_g=g_norm1, rel_bias=g_bias,
                 ret_gn_g=g_gn_g, ret_gn_b=g_gn_b, norm2_g=g_norm2, norm_f_g=g_normf)
    delta['w_ada'], new_m['w_ada'], new_v['w_ada'] = _adamw(w_ada, g_w_ada, m_w_ada, v_w_ada, "adamw_w_ada")
    small_names = ('b_ada', 'norm1_g', 'rel_bias', 'ret_gn_g', 'ret_gn_b', 'norm2_g', 'norm_f_g')
    two_d = {n: (1, ws[n].size) if ws[n].ndim == 1 else ws[n].shape for n in small_names}
    d_, m_, v_ = _adamw_small(*[[src[n].reshape(two_d[n]) for n in small_names] for src in (ws, grads, ms, vs)])
    for i, n in enumerate(small_names):
        shp = ws[n].shape
        delta[n], new_m[n], new_v[n] = d_[i].reshape(shp), m_[i].reshape(shp), v_[i].reshape(shp)
        grads[n] = grads[n].reshape(shp)

    done = lax.optimization_barrier((gx, tuple(d_), tuple(delta[n] for n in ('w_ada', 'w_ret_out', 'w_att_out', 'w_o',
                                                                               'w_ff1', 'w_ff2'))))
    parts_in, recvs_in = [], []
    for half, flight in enumerate(in_flight):
        (part_in,), (recv_chip_in,) = _chip_exchange_wait(flight, done[0], f"rs_in{half}_wait")
        parts_in.append(part_in)
        recvs_in.append(recv_chip_in)
    grads['w_in'], delta['w_in'], new_m['w_in'], new_v['w_in'] = _adamw_reduced(w_in, m_w_in, v_w_in, parts_in,
                                                                               recvs_in, chip)
    for d in (grads, delta, new_m, new_v):
        d['w_in'] = jnp.transpose(d['w_in'], (0, 2, 1))
    return (loss_out, gx[None], *[grads[n] for n in names], *[delta[n] for n in names],
            *[new_m[n] for n in names], *[new_v[n] for n in names])
```

```python
import functools
import math

import numpy as np
import jax
import jax.numpy as jnp
from jax import lax
from jax.experimental import pallas as pl
from jax.experimental.pallas import tpu as pltpu

F32 = jnp.float32
BF16 = jnp.bfloat16
MESH = pl.DeviceIdType.MESH

N_DEV = 8
S = 2048
D = 1024
RET_HEADS = 4
RET_DK = 256
RET_DV = 512
CHUNK = 128
N_CHUNK = S // CHUNK
ATT_GROUPS = ((128, 1), (512, 4), (2048, 16))
ATT_HG = 4
ATT_DH = 128
ATT_BLK = 128
N_BUCKETS = 32
MAX_DIST = 2048
D_FF = 4096
IN_COLS = 12800
OFF_RQ, OFF_RK, OFF_RV, OFF_RG, OFF_ATT = 0, 1024, 2048, 4096, 6144
OFF_GA, OFF_GB = 6144, 7168
RMS_EPS = 1e-6
GN_EPS = 1e-5
ADAM_LR, ADAM_B1, ADAM_B2, ADAM_EPS, ADAM_WD, ADAM_STEP = 0.001, 0.9, 0.999, 1e-08, 0.01, 10
VMEM_LIMIT = 48 * 1024 * 1024


def _pcall(body, **kw):
    return pl.pallas_call(body, **kw)


def _params(sem=None):
    return pltpu.CompilerParams(dimension_semantics=sem, vmem_limit_bytes=VMEM_LIMIT)


HBM_SPEC = pl.BlockSpec(memory_space=pl.ANY)


def _carry(body, comm, *, name, grid, in_specs, out_specs, out_shape, scratch_shapes=()):
    single = not isinstance(out_specs, (tuple, list))
    o_specs = (out_specs,) if single else tuple(out_specs)
    o_shape = (out_shape,) if single else tuple(out_shape)
    n_in, n_out, n_scr = len(in_specs), len(o_specs), len(scratch_shapes)
    nci, nco = len(comm.ins), len(comm.out_shape)
    total = int(np.prod(grid))

    def wrapped(*refs):
        bounds = np.cumsum([0, n_in, nci, n_out, nco, n_scr])
        a, ci, o, co, scr = (refs[bounds[i]:bounds[i + 1]] for i in range(5))
        sems = refs[bounds[5]:]
        flat = 0
        for d, g in enumerate(grid):
            flat = flat * g + pl.program_id(d)

        @pl.when(flat == 0)
        def _():
            comm.start(ci, co, sems)

        body(*a, *o, *scr)

        @pl.when(flat == total - 1)
        def _():
            comm.finish(ci, co, sems)

    aliases = {n_in + i: n_out + o for i, o in getattr(comm, "aliases", {}).items()}
    call = _pcall(wrapped, name=name, grid=grid, in_specs=list(in_specs) + [HBM_SPEC] * nci,
                  out_specs=o_specs + (HBM_SPEC,) * nco, out_shape=o_shape + tuple(comm.out_shape),
                  scratch_shapes=list(scratch_shapes) + list(comm.sems), input_output_aliases=aliases,
                  compiler_params=_params(("arbitrary",) * len(grid)))

    def run(*args):
        res = call(*args, *comm.ins)
        own = res[0] if single else tuple(res[:n_out])
        return own, tuple(res[n_out:])

    return run


def _run_comm(comm, name, after=None):
    nci, nco = len(comm.ins), len(comm.out_shape)
    extra = [] if after is None else [after]

    def body(*refs):
        ci, co, sems = refs[:nci], refs[nci + len(extra):nci + len(extra) + nco], refs[nci + len(extra) + nco:]
        comm.start(ci, co, sems)
        comm.finish(ci, co, sems)

    in_spec = pl.BlockSpec(memory_space=pltpu.VMEM) if getattr(comm, "ins_in_vmem", False) else HBM_SPEC
    return _pcall(body, name=name, in_specs=[in_spec] * nci + [HBM_SPEC] * len(extra), out_specs=(HBM_SPEC,) * nco,
                  out_shape=tuple(comm.out_shape), scratch_shapes=list(comm.sems))(*comm.ins, *extra)


def _dot(a, b, dn):
    return lax.dot_general(a.astype(BF16), b.astype(BF16), (dn, ((), ())), preferred_element_type=F32)


NN = ((1,), (0,))
NT = ((1,), (1,))
TN = ((0,), (0,))


def _mm(a, b, mode, *, tm, tn, tk, name, out_dtype=F32, res=None, gvec=None, relu2=False, relu2_of=None, comm=None,
        after=None):
    if mode == 'nn':
        (M, K), (_, N) = a.shape, b.shape
        a_spec = pl.BlockSpec((tm, tk), lambda i, j, k: (i, k))
        b_spec = pl.BlockSpec((tk, tn), lambda i, j, k: (k, j))
        dn = NN
    elif mode == 'nt':
        (M, K), (N, _) = a.shape, b.shape
        a_spec = pl.BlockSpec((tm, tk), lambda i, j, k: (i, k))
        b_spec = pl.BlockSpec((tn, tk), lambda i, j, k: (j, k))
        dn = NT
    else:
        (K, M), (_, N) = a.shape, b.shape
        a_spec = pl.BlockSpec((tk, tm), lambda i, j, k: (k, i))
        b_spec = pl.BlockSpec((tk, tn), lambda i, j, k: (k, j))
        dn = TN
    assert M % tm == 0 and N % tn == 0 and K % tk == 0, (name, M, N, K)
    nk = K // tk
    fused = res is not None
    o_spec = pl.BlockSpec((tm, tn), lambda i, j, k: (i, j))

    def body(a_ref, b_ref, *rest):
        acc_ref = rest[-1] if nk > 1 else None
        if after is not None:
            rest = rest[1:]
        if fused:
            res_ref, g_ref, o_ref, x_ref = rest[:4]
        elif relu2_of is not None:
            u_ref, o_ref = rest[:2]
        elif relu2:
            o_ref, act_ref = rest[:2]
        else:
            o_ref = rest[0]

        def finish(acc):
            if relu2_of is not None:
                acc = acc * (2.0 * jnp.maximum(u_ref[...], 0.0))
            o_ref[...] = acc.astype(o_ref.dtype)
            if fused:
                x_ref[...] = res_ref[...] + g_ref[...] * acc
            if relu2:
                r = jnp.maximum(acc, 0.0)
                act_ref[...] = (r * r).astype(BF16)

        p = _dot(a_ref[...], b_ref[...], dn)
        if nk == 1:
            finish(p)
        else:
            k = pl.program_id(2)

            @pl.when(k == 0)
            def _():
                acc_ref[...] = p

            @pl.when(k > 0)
            def _():
                acc_ref[...] += p

            @pl.when(k == nk - 1)
            def _():
                finish(acc_ref[...])

    in_specs = [a_spec, b_spec]
    args = [a, b]
    if after is not None:
        in_specs.append(pl.BlockSpec(memory_space=pl.ANY))
        args.append(after)
    out_shape = jax.ShapeDtypeStruct((M, N), out_dtype)
    out_specs = o_spec
    if fused:
        in_specs += [pl.BlockSpec((tm, tn), lambda i, j, k: (i, j)), pl.BlockSpec((1, tn), lambda i, j, k: (0, j))]
        args += [res, gvec]
        out_shape = (out_shape, jax.ShapeDtypeStruct((M, N), F32))
        out_specs = (o_spec, pl.BlockSpec((tm, tn), lambda i, j, k: (i, j)))
    elif relu2_of is not None:
        in_specs.append(pl.BlockSpec((tm, tn), lambda i, j, k: (i, j)))
        args.append(relu2_of)
    elif relu2:
        out_shape = (out_shape, jax.ShapeDtypeStruct((M, N), BF16))
        out_specs = (o_spec, pl.BlockSpec((tm, tn), lambda i, j, k: (i, j)))
    kw = dict(name=name, grid=(M // tm, N // tn, nk), in_specs=in_specs, out_specs=out_specs,
              out_shape=out_shape, scratch_shapes=[pltpu.VMEM((tm, tn), F32)] if nk > 1 else [])
    if comm is not None:
        return _carry(body, comm, **kw)(*args)
    return _pcall(body, compiler_params=_params(("parallel", "parallel", "arbitrary")), **kw)(*args)


PROJ_TN = 512
ATT_T0, ATT_T1 = 6144 // PROJ_TN, 10752 // PROJ_TN
N_SLABS = (ATT_T1 - ATT_T0) * 4
MAIN_COLS = IN_COLS - (ATT_T1 - ATT_T0) * PROJ_TN


PROJ_TILES = IN_COLS // PROJ_TN
SHARD_ROWS = IN_COLS // N_DEV
W_CHUNKS = 4
N_OWN, N_NEAR = 5, 18


def _proj_order():
    out = np.zeros((4, 3, PROJ_TILES), np.int32)
    for q in range(4):
        def hops(t):
            owners = {col // (2 * SHARD_ROWS) for col in (t * PROJ_TN, (t + 1) * PROJ_TN - 1)}
            return max(bin(q ^ p).count("1") for p in owners)
        order = sorted(range(PROJ_TILES), key=lambda t: (hops(t), t))
        assert all(hops(t) == 0 for t in order[:N_OWN]) and all(hops(t) < 2 for t in order[:N_NEAR])
        is_att = [ATT_T0 <= t < ATT_T1 for t in order]
        for row, kind, index in ((1, False, lambda t: t if t < ATT_T0 else t - (ATT_T1 - ATT_T0)),
                                 (2, True, lambda t: t - ATT_T0)):
            own = [index(t) if a == kind else None for t, a in zip(order, is_att)]
            first = next(v for v in own if v is not None)
            last = first
            for j, v in enumerate(own):
                last = last if v is None else v
                out[q, row, j] = last
        out[q, 0] = order
    return out


def _gather_proj(x, g, sh, sc, shard, order):
    rows = SHARD_ROWS // W_CHUNKS

    def body(ord_ref, x_ref, g_ref, shift_ref, scale_ref, sh_ref, main_ref, slab_ref, full_ref, a_ref, wbuf, xbuf,
             fetch_sems, send_sems, recv_sems, local_sems, x_sem):
        j = pl.program_id(0)
        x, y, c = _mesh_pos()
        me, sibling = (x, y, c), (x, y, 1 - c)
        chips = [(1 - x, y), (x, 1 - y), (1 - x, 1 - y)]

        def block(p, owner):
            return full_ref.at[pl.ds(pl.multiple_of(_slot(owner) * SHARD_ROWS + p * rows, 16), rows)]

        def copy(p, k, owner, to, from_input=False):
            dst = block(p, owner)
            return pltpu.make_async_remote_copy(
                src_ref=sh_ref.at[pl.ds(p * rows, rows)] if from_input else dst, dst_ref=dst,
                send_sem=send_sems.at[7 * p + k], recv_sem=recv_sems.at[7 * p + k], device_id=to, device_id_type=MESH)

        pieces = range(W_CHUNKS)
        mine = [pltpu.make_async_copy(sh_ref.at[pl.ds(p * rows, rows)], block(p, me), local_sems.at[p]) for p in pieces]
        first = [copy(p, 0, me, sibling, from_input=True) for p in pieces]
        first += [copy(p, 1 + n, me, (*chips[n], c), from_input=True) for p in pieces for n in range(2)]
        near_pass = [copy(p, 4 + n, (*chips[n], c), sibling) for p in pieces for n in range(2)]
        relay = [copy(p, 3, ((x + 1 - c) % 2, (y + c) % 2, c), ((x + c) % 2, (y + 1 - c) % 2, c)) for p in pieces]
        far_pass = [copy(p, 6, (*chips[2], c), sibling) for p in pieces]

        def fetch(pos):
            slot = lax.rem(pos, 2)
            start = pl.multiple_of(ord_ref[0, pos] * PROJ_TN, PROJ_TN)
            return pltpu.make_async_copy(full_ref.at[pl.ds(start, PROJ_TN)], wbuf.at[slot], fetch_sems.at[slot])

        @pl.when(j == 0)
        def _():
            x_copy = pltpu.make_async_copy(x_ref, xbuf, x_sem.at[0])
            x_copy.start()
            for cp in mine + first:
                cp.start()
            x_copy.wait()
            for r in range(S // TR):
                rws = pl.ds(r * TR, TR)
                xv = xbuf[rws, :]
                rstd = lax.rsqrt(jnp.mean(xv * xv, axis=-1, keepdims=True) + RMS_EPS)
                n = xv * rstd * g_ref[...]
                a_ref[rws, :] = (n * (1.0 + scale_ref[...]) + shift_ref[...]).astype(BF16)
            for cp in mine:
                cp.wait()
            for p in pieces:
                copy(p, 0, sibling, me).wait_recv()
            fetch(j).start()

        @pl.when(j == N_OWN - 1)
        def _():
            for p in pieces:
                for n in range(2):
                    copy(p, 1 + n, (*chips[n], c), me).wait_recv()
                    near_pass[2 * p + n].start()
                relay[p].start()
            for p in pieces:
                for n in range(2):
                    copy(p, 4 + n, (*chips[n], 1 - c), me).wait_recv()

        @pl.when(j == N_NEAR - 1)
        def _():
            for p in pieces:
                copy(p, 3, (*chips[2], c), me).wait_recv()
                far_pass[p].start()
            for p in pieces:
                copy(p, 6, (*chips[2], 1 - c), me).wait_recv()

        @pl.when(j + 1 < PROJ_TILES)
        def _():
            fetch(j + 1).start()

        fetch(j).wait()
        w_ref = wbuf.at[lax.rem(j, 2)]
        tile = ord_ref[0, j]
        is_att = (tile >= ATT_T0) & (tile < ATT_T1)
        chunks = [pl.ds(r * 512, 512) for r in range(S // 512)]

        @pl.when(jnp.logical_not(is_att))
        def _():
            for rws in chunks:
                main_ref[rws, :] = _dot(a_ref[rws, :], w_ref[...], NT)

        @pl.when(is_att)
        def _():
            for rws in chunks:
                p = _dot(a_ref[rws, :], w_ref[...], NT)
                for h in range(4):
                    slab_ref[h, rws, :] = p[:, h * 128:(h + 1) * 128]

        @pl.when(j == PROJ_TILES - 1)
        def _():
            for cp in first + near_pass + relay + far_pass:
                cp.wait_send()

    vec = pl.BlockSpec((1, D), lambda j, o: (0, 0))
    gs = pltpu.PrefetchScalarGridSpec(
        num_scalar_prefetch=1, grid=(PROJ_TILES,),
        in_specs=[HBM_SPEC, vec, vec, vec, HBM_SPEC],
        out_specs=(pl.BlockSpec((S, PROJ_TN), lambda j, o: (0, o[1, j])),
                   pl.BlockSpec((4, S, 128), lambda j, o: (o[2, j], 0, 0)), HBM_SPEC,
                   pl.BlockSpec((S, D), lambda j, o: (0, 0))),
        scratch_shapes=[pltpu.VMEM((2, PROJ_TN, D), BF16), pltpu.VMEM((S, D), F32), pltpu.SemaphoreType.DMA((2,)),
                        pltpu.SemaphoreType.DMA((7 * W_CHUNKS,)), pltpu.SemaphoreType.DMA((7 * W_CHUNKS,)),
                        pltpu.SemaphoreType.DMA((W_CHUNKS,)), pltpu.SemaphoreType.DMA((1,))])
    return _pcall(body, name="gather_proj", grid_spec=gs,
                  out_shape=(jax.ShapeDtypeStruct((S, MAIN_COLS), F32), jax.ShapeDtypeStruct((N_SLABS, S, 128), F32),
                             jax.ShapeDtypeStruct((IN_COLS, D), BF16), jax.ShapeDtypeStruct((S, D), BF16)),
                  compiler_params=_params(("arbitrary",)))(order, x, g, sh, sc, shard)


TR = 256


def _row_spec(w=D):
    return pl.BlockSpec((TR, w), lambda i: (i, 0))


def _vec_spec(w=D):
    return pl.BlockSpec((1, w), lambda i: (0, 0))


def _norm_mod_bwd(x, g, sc, dh, dres, name, gate=None, after=None):
    gated = gate is not None

    def body(x_ref, g_ref, sc_ref, dh_ref, dres_ref, *rest):
        if after is not None:
            rest = rest[1:]
        if gated:
            f_ref, gv_ref, dx_ref, dsc_ref, dsh_ref, dg_ref, dz_ref, dgv_ref = rest
        else:
            dx_ref, dsc_ref, dsh_ref, dg_ref = rest
        i = pl.program_id(0)
        xv = x_ref[...]
        dh = dh_ref[...]
        rstd = lax.rsqrt(jnp.mean(xv * xv, axis=-1, keepdims=True) + RMS_EPS)
        xhat = xv * rstd
        gv = g_ref[...]
        dn = dh * (1.0 + sc_ref[...])
        dxhat = dn * gv
        dx = dres_ref[...] + rstd * (dxhat - xhat * jnp.mean(dxhat * xhat, axis=-1, keepdims=True))
        dx_ref[...] = dx
        sums = [(dsc_ref, jnp.sum(dh * (xhat * gv), axis=0, keepdims=True)),
                (dsh_ref, jnp.sum(dh, axis=0, keepdims=True)),
                (dg_ref, jnp.sum(dn * xhat, axis=0, keepdims=True))]
        if gated:
            dz_ref[...] = (dx * gv_ref[...]).astype(BF16)
            sums.append((dgv_ref, jnp.sum(dx * f_ref[...], axis=0, keepdims=True)))

        @pl.when(i == 0)
        def _():
            for ref, p in sums:
                ref[...] = p

        @pl.when(i > 0)
        def _():
            for ref, p in sums:
                ref[...] += p

    vec = jax.ShapeDtypeStruct((1, D), F32)
    in_specs = [_row_spec(), _vec_spec(), _vec_spec(), _row_spec(), _row_spec()]
    out_specs = [_row_spec(), _vec_spec(), _vec_spec(), _vec_spec()]
    out_shape = [jax.ShapeDtypeStruct((S, D), F32), vec, vec, vec]
    args = [x, g, sc, dh, dres]
    if after is not None:
        in_specs.append(HBM_SPEC)
        args.append(after)
    if gated:
        in_specs += [_row_spec(), _vec_spec()]
        out_specs += [_row_spec(), _vec_spec()]
        out_shape += [jax.ShapeDtypeStruct((S, D), BF16), vec]
        args += list(gate)
    return _pcall(body, name=name, grid=(S // TR,), in_specs=in_specs, out_specs=tuple(out_specs),
                  out_shape=tuple(out_shape), compiler_params=_params(("arbitrary",)))(*args)


def _w_o_norm2(merged, w_o, x, g1, g, sh, sc):
    def body(a_ref, b_ref, x_ref, g1_ref, g_ref, sh_ref, sc_ref, o_ref, x1_ref, h_ref):
        acc = _dot(a_ref[...], b_ref[...], NN)
        o_ref[...] = acc
        xv = x_ref[...] + g1_ref[...] * acc
        x1_ref[...] = xv
        rstd = lax.rsqrt(jnp.mean(xv * xv, axis=-1, keepdims=True) + RMS_EPS)
        h_ref[...] = (xv * rstd * g_ref[...] * (1.0 + sc_ref[...]) + sh_ref[...]).astype(BF16)

    rows = pl.BlockSpec((FF2_TM, D), lambda i: (i, 0))
    f32 = jax.ShapeDtypeStruct((S, D), F32)
    return _pcall(body, name="w_o_norm2", grid=(S // FF2_TM,),
                  in_specs=[rows, pl.BlockSpec((D, D), lambda i: (0, 0)), rows] + [_vec_spec()] * 4,
                  out_specs=(rows, rows, rows), out_shape=(f32, f32, jax.ShapeDtypeStruct((S, D), BF16)),
                  compiler_params=_params(("parallel",)))(merged, w_o, x, g1, g, sh, sc)


FF2_TM = 512


def _ff2_final(act, w_ff2, x1, g2, tgt, g):
    def body(a_ref, b_ref, x1_ref, g2_ref, t_ref, g_ref, loss_ref, dx_ref, dg_ref, df_ref, dg2_ref):
        i = pl.program_id(0)
        f = _dot(a_ref[...], b_ref[...], NN)
        g2v = g2_ref[...]
        xv = x1_ref[...] + g2v * f
        gv = g_ref[...]
        rstd = lax.rsqrt(jnp.mean(xv * xv, axis=-1, keepdims=True) + RMS_EPS)
        xhat = xv * rstd
        err = xhat * gv - t_ref[...]
        dy = err * (1.0 / D)
        dxhat = dy * gv
        dx = rstd * (dxhat - xhat * jnp.mean(dxhat * xhat, axis=-1, keepdims=True))
        dx_ref[...] = dx
        df_ref[...] = (dx * g2v).astype(BF16)
        p_g = jnp.sum(dy * xhat, axis=0, keepdims=True)
        p_g2 = jnp.sum(dx * f, axis=0, keepdims=True)
        p_l = jnp.zeros((1, 128), F32) + 0.5 * jnp.sum(jnp.mean(err * err, axis=-1, keepdims=True))

        @pl.when(i == 0)
        def _():
            dg_ref[...] = p_g
            dg2_ref[...] = p_g2
            loss_ref[...] = p_l

        @pl.when(i > 0)
        def _():
            dg_ref[...] += p_g
            dg2_ref[...] += p_g2
            loss_ref[...] += p_l

    vec = jax.ShapeDtypeStruct((1, D), F32)
    rows = lambda w: pl.BlockSpec((FF2_TM, w), lambda i: (i, 0))
    return _pcall(body, name="ff2_final", grid=(S // FF2_TM,),
                  in_specs=[rows(D_FF), pl.BlockSpec((D_FF, D), lambda i: (0, 0)), rows(D), _vec_spec(), rows(D),
                            _vec_spec()],
                  out_specs=(_vec_spec(128), rows(D), _vec_spec(), rows(D), _vec_spec()),
                  out_shape=(jax.ShapeDtypeStruct((1, 128), F32), jax.ShapeDtypeStruct((S, D), F32), vec,
                             jax.ShapeDtypeStruct((S, D), BF16), vec),
                  compiler_params=_params(("arbitrary",)))(act, w_ff2, x1, g2, tgt, g)


HALF = 512


MERGE_TM = 1024


def _merge_specs():
    blk = lambda off: pl.BlockSpec((MERGE_TM, HALF), lambda i, j: (i, off // HALF + j))
    return blk(OFF_GA), blk(OFF_GB), blk(0)


def _att_out_merge(att, w_att_out, proj, ret_out):
    def body(a_ref, b_ref, ga_ref, gb_ref, r_ref, o_ref, m_ref):
        acc = _dot(a_ref[...], b_ref[...], NN)
        o_ref[...] = acc
        m_ref[...] = (jax.nn.sigmoid(ga_ref[...]) * r_ref[...] + jax.nn.sigmoid(gb_ref[...]) * acc).astype(BF16)

    ga, gb, tile = _merge_specs()
    return _pcall(body, name="att_out", grid=(S // MERGE_TM, D // HALF),
                  in_specs=[pl.BlockSpec((MERGE_TM, AW), lambda i, j: (i, 0)), pl.BlockSpec((AW, HALF), lambda i, j: (0, j)),
                            ga, gb, tile],
                  out_specs=(tile, tile),
                  out_shape=(jax.ShapeDtypeStruct((S, D), F32), jax.ShapeDtypeStruct((S, D), BF16)),
                  compiler_params=_params(("parallel", "parallel")))(att, w_att_out, proj, proj, ret_out)


def _dmerged_split(dmixo, w_o, proj, ret_out, att_out):
    def body(a_ref, b_ref, ga_ref, gb_ref, r_ref, at_ref, dr_ref, da_ref, dga_ref, dgb_ref):
        dm = _dot(a_ref[...], b_ref[...], NT)
        sa = jax.nn.sigmoid(ga_ref[...])
        sb = jax.nn.sigmoid(gb_ref[...])
        dr_ref[...] = (dm * sa).astype(BF16)
        da_ref[...] = (dm * sb).astype(BF16)
        dga_ref[...] = (dm * r_ref[...] * (sa * (1.0 - sa))).astype(BF16)
        dgb_ref[...] = (dm * at_ref[...] * (sb * (1.0 - sb))).astype(BF16)

    ga, gb, tile = _merge_specs()
    o = jax.ShapeDtypeStruct((S, D), BF16)
    return _pcall(body, name="dmerged", grid=(S // MERGE_TM, D // HALF),
                  in_specs=[pl.BlockSpec((MERGE_TM, D), lambda i, j: (i, 0)), pl.BlockSpec((HALF, D), lambda i, j: (j, 0)),
                            ga, gb, tile, tile],
                  out_specs=(tile,) * 4, out_shape=(o, o, o, o),
                  compiler_params=_params(("parallel", "parallel")))(dmixo, w_o, proj, proj, ret_out, att_out)


def _ret_tables():
    H, C = RET_HEADS, CHUNK
    log_g = jnp.log1p(-(2.0 ** (-5.0 - jnp.arange(H, dtype=F32))))
    idx = jnp.arange(C, dtype=F32)
    rel = idx[:, None] - idx[None, :]
    inner = jnp.where(rel >= 0, jnp.exp(log_g[:, None, None] * jnp.maximum(rel, 0.0)), 0.0)
    qd = jnp.exp(log_g[:, None] * (idx + 1.0))[:, :, None]
    kd = jnp.exp(log_g[:, None] * (C - 1.0 - idx))[:, :, None]
    cd = jnp.broadcast_to(jnp.exp(log_g * C)[:, None, None], (H, 1, 128))
    half = RET_DK // 2
    inv = 10000.0 ** (-jnp.arange(half, dtype=F32) / half)
    ang = jnp.arange(S, dtype=F32)[:, None] * inv[None, :]
    return inner, qd, kd, cd, jnp.cos(ang), jnp.sin(ang)


def _rot(x, cos, sin):
    x1, x2 = x[:, :128], x[:, 128:]
    return jnp.concatenate([x1 * cos - x2 * sin, x1 * sin + x2 * cos], axis=1)


def _rot_t(d, cos, sin):
    d1, d2 = d[:, :128], d[:, 128:]
    return jnp.concatenate([d1 * cos + d2 * sin, d2 * cos - d1 * sin], axis=1)


RET_COLS = OFF_ATT
RET_VW = RET_HEADS * RET_DV


def _ret_specs(chunk_of):
    ci = chunk_of
    whole = lambda shape: pl.BlockSpec(shape, lambda t: (0,) * len(shape))
    return [
        pl.BlockSpec((CHUNK, RET_COLS), lambda t: (ci(t), 0)),
        pl.BlockSpec((CHUNK, 128), lambda t: (ci(t), 0)),
        pl.BlockSpec((CHUNK, 128), lambda t: (ci(t), 0)),
        whole((RET_HEADS, CHUNK, CHUNK)), whole((RET_HEADS, CHUNK, 1)), whole((RET_HEADS, CHUNK, 1)),
        whole((RET_HEADS, 1, 128)), whole((1, RET_VW)), whole((1, RET_VW)),
    ]


def _ret_cols(h):
    q = slice(OFF_RQ + h * RET_DK, OFF_RQ + (h + 1) * RET_DK)
    k = slice(OFF_RK + h * RET_DK, OFF_RK + (h + 1) * RET_DK)
    v = slice(OFF_RV + h * RET_DV, OFF_RV + (h + 1) * RET_DV)
    g = slice(OFF_RG + h * RET_DV, OFF_RG + (h + 1) * RET_DV)
    return q, k, v, g, slice(h * RET_DV, (h + 1) * RET_DV)


def _ret_fwd(proj, tables, gn_g, gn_b, after):
    inner, qd, kd, cd, cos, sin = tables

    def body(x_ref, cos_ref, sin_ref, in_ref, qd_ref, kd_ref, cd_ref, g_ref, b_ref, after_ref,
             gated_ref, ro_ref, st_ref, s_scr):
        i = pl.program_id(0)

        @pl.when(i == 0)
        def _():
            s_scr[...] = jnp.zeros_like(s_scr)

        cosv, sinv = cos_ref[...], sin_ref[...]
        for h in range(RET_HEADS):
            cq, ck, cv, cg, co = _ret_cols(h)
            q = _rot(x_ref[:, cq], cosv, sinv)
            k = _rot(x_ref[:, ck], cosv, sinv) * (RET_DK ** -0.5)
            v = x_ref[:, cv]
            st = s_scr[h]
            st_ref[h] = st.astype(BF16)
            s = _dot(q, k, NT) * in_ref[h]
            o = _dot(s, v, NN) + _dot(q, st, NN) * qd_ref[h]
            s_scr[h] = st * cd_ref[h, :, :1] + _dot(k * kd_ref[h], v, TN)
            ro_ref[:, co] = o
            mu = jnp.mean(o, axis=-1, keepdims=True)
            oc = o - mu
            var = jnp.mean(oc * oc, axis=-1, keepdims=True)
            rn = oc * lax.rsqrt(var + GN_EPS) * g_ref[:, co] + b_ref[:, co]
            rg = x_ref[:, cg]
            gated_ref[:, co] = (rg * jax.nn.sigmoid(rg) * rn).astype(BF16)

    ospec = pl.BlockSpec((CHUNK, RET_VW), lambda t: (t, 0))
    return _pcall(
        body, name="ret_fwd", grid=(N_CHUNK,), in_specs=_ret_specs(lambda t: t) + [HBM_SPEC],
        out_specs=(ospec, ospec, pl.BlockSpec((RET_HEADS, None, RET_DK, RET_DV), lambda t: (0, t, 0, 0))),
        out_shape=(jax.ShapeDtypeStruct((S, RET_VW), BF16), jax.ShapeDtypeStruct((S, RET_VW), F32),
                   jax.ShapeDtypeStruct((RET_HEADS, N_CHUNK, RET_DK, RET_DV), BF16)),
        scratch_shapes=[pltpu.VMEM((RET_HEADS, RET_DK, RET_DV), F32)],
        compiler_params=_params(("arbitrary",)))(proj, cos, sin, inner, qd, kd, cd, gn_g, gn_b, after)


def _ret_bwd(proj, tables, gn_g, gn_b, ro, states, dgated, others):
    inner, qd, kd, cd, cos, sin = tables
    last = N_CHUNK - 1
    pieces = lambda o: [(h, o.shape[2]) for h in range(o.shape[0])] if len(o.shape) == 3 else [(None, o.shape[1])]
    assert RET_COLS + sum(w for o in others for _, w in pieces(o)) == IN_COLS

    def body(x_ref, cos_ref, sin_ref, in_ref, qd_ref, kd_ref, cd_ref, g_ref, b_ref, ro_ref, st_ref, dg_ref, *rest):
        other_refs, (dx_ref, gg_ref, gb_ref, gs_scr) = rest[:len(others)], rest[len(others):]
        t = pl.program_id(0)
        col = RET_COLS
        for o_ref in other_refs:
            for h, w in pieces(o_ref):
                dx_ref[:, col:col + w] = (o_ref[...] if h is None else o_ref[h]).astype(BF16)
                col += w

        @pl.when(t == 0)
        def _():
            gs_scr[...] = jnp.zeros_like(gs_scr)
            gg_ref[...] = jnp.zeros_like(gg_ref)
            gb_ref[...] = jnp.zeros_like(gb_ref)

        cosv, sinv = cos_ref[...], sin_ref[...]
        for h in range(RET_HEADS):
            cq, ck, cv, cg, co = _ret_cols(h)
            q = _rot(x_ref[:, cq], cosv, sinv)
            k = _rot(x_ref[:, ck], cosv, sinv) * (RET_DK ** -0.5)
            v = x_ref[:, cv]
            qdv, kdv, dm = qd_ref[h], kd_ref[h], in_ref[h]
            st = st_ref[h]
            o = ro_ref[:, co]
            gv = g_ref[:, co]
            mu = jnp.mean(o, axis=-1, keepdims=True)
            oc = o - mu
            rstd = lax.rsqrt(jnp.mean(oc * oc, axis=-1, keepdims=True) + GN_EPS)
            ohat = oc * rstd
            rn = ohat * gv + b_ref[:, co]
            rg = x_ref[:, cg]
            sg = jax.nn.sigmoid(rg)
            dgt = dg_ref[:, co]
            drn = dgt * (rg * sg)
            dx_ref[:, cg] = (dgt * rn * (sg * (1.0 + rg * (1.0 - sg)))).astype(BF16)
            gg_ref[:, co] += jnp.sum(drn * ohat, axis=0, keepdims=True)
            gb_ref[:, co] += jnp.sum(drn, axis=0, keepdims=True)
            dohat = drn * gv
            do = rstd * (dohat - jnp.mean(dohat, axis=-1, keepdims=True)
                         - ohat * jnp.mean(dohat * ohat, axis=-1, keepdims=True))
            gs = gs_scr[h]
            s = _dot(q, k, NT) * dm
            dsr = _dot(do, v, NT) * dm
            dq = _dot(dsr, k, NN) + _dot(do, st, NT) * qdv
            dk = _dot(dsr, q, TN) + _dot(v, gs, NT) * kdv
            dv = _dot(s, do, TN) + _dot(k * kdv, gs, NN)
            gs_scr[h] = gs * cd_ref[h, :, :1] + _dot(q * qdv, do, TN)
            dx_ref[:, cq] = _rot_t(dq, cosv, sinv).astype(BF16)
            dx_ref[:, ck] = (_rot_t(dk, cosv, sinv) * (RET_DK ** -0.5)).astype(BF16)
            dx_ref[:, cv] = dv.astype(BF16)

    rev = lambda t: last - t
    vblk = pl.BlockSpec((CHUNK, RET_VW), lambda t: (rev(t), 0))
    vspec = pl.BlockSpec((1, RET_VW), lambda t: (0, 0))
    rows = lambda w: pl.BlockSpec((CHUNK, w), lambda t: (rev(t), 0))
    return _pcall(
        body, name="ret_bwd", grid=(N_CHUNK,),
        in_specs=_ret_specs(rev) + [vblk, pl.BlockSpec((RET_HEADS, None, RET_DK, RET_DV), lambda t: (0, rev(t), 0, 0)),
                                    vblk] + [rows(o.shape[1]) if o.ndim == 2 else
                                             pl.BlockSpec((o.shape[0], CHUNK, o.shape[2]), lambda t: (0, rev(t), 0))
                                             for o in others],
        out_specs=(rows(IN_COLS), vspec, vspec),
        out_shape=(jax.ShapeDtypeStruct((S, IN_COLS), BF16), jax.ShapeDtypeStruct((1, RET_VW), F32),
                   jax.ShapeDtypeStruct((1, RET_VW), F32)),
        scratch_shapes=[pltpu.VMEM((RET_HEADS, RET_DK, RET_DV), F32)],
        compiler_params=_params(("arbitrary",)))(proj, cos, sin, inner, qd, kd, cd, gn_g, gn_b, ro, states, dgated,
                                                 *others)


def _bucket_tables():
    qi = np.arange(ATT_BLK)[:, None]
    kj = np.arange(2 * ATT_BLK)[None, :]
    m = ATT_BLK + qi - kj
    out = []
    for win, dil in ATT_GROUPS:
        w = win // dil
        dist = (np.clip(m, 0, w) * dil).astype(np.int32)
        max_exact = N_BUCKETS // 2
        d_f = np.maximum(dist, 1).astype(np.float32)
        large = max_exact + (np.log(d_f / np.float32(max_exact)) / np.float32(math.log(MAX_DIST / max_exact))
                             * np.float32(N_BUCKETS - max_exact)).astype(np.int32)
        large = np.minimum(large, N_BUCKETS - 1)
        out.append(np.where(dist < max_exact, dist, large).astype(np.int32))
    return np.stack(out)


def _bias_build(rel_bias, buckets, after):
    def body(tab_ref, bk_ref, after_ref, o_ref):
        hh = pl.program_id(0)
        bk = bk_ref[...]
        acc = jnp.zeros((ATT_BLK, 2 * ATT_BLK), F32)
        for b in range(N_BUCKETS):
            acc = jnp.where(bk == b, tab_ref[b, hh], acc)
        o_ref[...] = acc

    nh = len(ATT_GROUPS) * ATT_HG
    return _pcall(body, name="bias_build", grid=(nh,),
                  in_specs=[pl.BlockSpec(memory_space=pltpu.SMEM),
                            pl.BlockSpec((None, ATT_BLK, 2 * ATT_BLK), lambda hh: (hh // ATT_HG, 0, 0)), HBM_SPEC],
                  out_specs=pl.BlockSpec((None, ATT_BLK, 2 * ATT_BLK), lambda hh: (hh, 0, 0)),
                  out_shape=jax.ShapeDtypeStruct((nh, ATT_BLK, 2 * ATT_BLK), F32),
                  compiler_params=_params(("parallel",)))(rel_bias, buckets, after)


def _bias_grad(ds_sum, buckets):
    def body(ds_ref, bk_ref, o_ref):
        bk = bk_ref[...]
        ds = ds_ref[...]
        rows = lax.broadcasted_iota(jnp.int32, (N_BUCKETS, 128), 0)
        acc = jnp.zeros((N_BUCKETS, 128), F32)
        for b in range(N_BUCKETS):
            acc = jnp.where(rows == b, jnp.sum(jnp.where(bk == b, ds, 0.0)), acc)
        o_ref[...] = acc

    nh = len(ATT_GROUPS) * ATT_HG
    return _pcall(body, name="bias_grad", grid=(nh,),
                  in_specs=[pl.BlockSpec((None, ATT_BLK, 2 * ATT_BLK), lambda hh: (hh, 0, 0)),
                            pl.BlockSpec((None, ATT_BLK, 2 * ATT_BLK), lambda hh: (hh // ATT_HG, 0, 0))],
                  out_specs=pl.BlockSpec((None, N_BUCKETS, 128), lambda hh: (hh, 0, 0)),
                  out_shape=jax.ShapeDtypeStruct((nh, N_BUCKETS, 128), F32),
                  compiler_params=_params(("parallel",)))(ds_sum, buckets)


def _att_valid(n):
    qi = lax.broadcasted_iota(jnp.int32, (ATT_BLK, 2 * ATT_BLK), 0)
    kj = lax.broadcasted_iota(jnp.int32, (ATT_BLK, 2 * ATT_BLK), 1)
    m = ATT_BLK + qi - kj
    first_key = jnp.where(n > 0, 0, ATT_BLK)
    return (m >= 0) & (m <= ATT_BLK) & (kj >= first_key)


ATT_HP = (2, 2, 2)


def _att_geometry(gi):
    _, dil = ATT_GROUPS[gi]
    return dil, S // dil // ATT_BLK, ATT_HP[gi]


def _blk(dil, r, n):
    if dil == 1:
        return pl.ds(n * ATT_BLK, ATT_BLK)
    return pl.ds(r + n * ATT_BLK * dil, ATT_BLK, stride=dil)


def _slab_specs(gi):
    _, _, hp = _att_geometry(gi)
    per = ATT_HG // hp
    return [pl.BlockSpec((hp, S, ATT_DH), lambda g, r, part=part: ((3 * gi + part) * per + g, 0, 0))
            for part in range(3)]


def _head_specs(gi, count):
    _, _, hp = _att_geometry(gi)
    return [pl.BlockSpec((hp, S, ATT_DH), lambda g, r: (g, 0, 0))] * count


def _bias_spec(gi):
    _, _, hp = _att_geometry(gi)
    return pl.BlockSpec((hp, ATT_BLK, 2 * ATT_BLK), lambda g, r: (gi * (ATT_HG // hp) + g, 0, 0))


def _att_valid_first():
    qi = lax.broadcasted_iota(jnp.int32, (ATT_BLK, ATT_BLK), 0)
    kj = lax.broadcasted_iota(jnp.int32, (ATT_BLK, ATT_BLK), 1)
    return kj <= qi


def _att_fwd(slabs, bias, gi, comm=None):
    dil, nb, hp = _att_geometry(gi)
    scale = ATT_DH ** -0.5

    def body(q_ref, k_ref, v_ref, bias_ref, o_ref, l_ref):
        r = pl.program_id(1)
        for n in range(nb):
            cur = _blk(dil, r, n)
            valid = _att_valid(n) if n > 0 else _att_valid_first()
            for h in range(hp):
                if n > 0:
                    prev = _blk(dil, r, n - 1)
                    kk = jnp.concatenate([k_ref[h, prev, :], k_ref[h, cur, :]], axis=0)
                    vv = jnp.concatenate([v_ref[h, prev, :], v_ref[h, cur, :]], axis=0)
                    bias = bias_ref[h]
                else:
                    kk, vv, bias = k_ref[h, cur, :], v_ref[h, cur, :], bias_ref[h, :, pl.ds(ATT_BLK, ATT_BLK)]
                s = _dot(q_ref[h, cur, :], kk, NT) * scale + bias
                s = jnp.where(valid, s, -1e30)
                mx = jnp.max(s, axis=-1, keepdims=True)
                e = jnp.exp(s - mx)
                den = jnp.sum(e, axis=-1, keepdims=True)
                o_ref[h, cur, :] = _dot(e / den, vv, NN)
                l_ref[h, cur, :] = jnp.broadcast_to(mx + jnp.log(den), (ATT_BLK, ATT_DH))

    osh = pltpu.HBM((ATT_HG, S, ATT_DH), F32)
    kw = dict(name=f"att_fwd{gi}", grid=(ATT_HG // hp, dil), in_specs=_slab_specs(gi) + [_bias_spec(gi)],
              out_specs=tuple(_head_specs(gi, 2)), out_shape=(osh, osh))
    if comm is not None:
        return _carry(body, comm, **kw)(slabs, slabs, slabs, bias)
    return _pcall(body, compiler_params=_params(("parallel", "arbitrary")), **kw)(slabs, slabs, slabs, bias)


def _att_bwd(slabs, bias, o, lse, do, dlse, gi, comm=None):
    dil, nb, hp = _att_geometry(gi)
    per = ATT_HG // hp
    scale = ATT_DH ** -0.5
    wide = lambda t: jnp.concatenate([t, t], axis=1)

    def body(q_ref, k_ref, v_ref, bias_ref, o_ref, l_ref, do_ref, dl_ref, dq_ref, dk_ref, dv_ref, ds_ref):
        r = pl.program_id(1)

        @pl.when(r == 0)
        def _():
            ds_ref[...] = jnp.zeros_like(ds_ref)

        for h in range(hp):
            carry_k = carry_v = None
            for n in range(nb):
                cur = _blk(dil, r, n)
                q = q_ref[h, cur, :]
                dov = do_ref[h, cur, :]
                delta = jnp.sum(dov * o_ref[h, cur, :], axis=-1, keepdims=True)
                if n == 0:
                    own = pl.ds(ATT_BLK, ATT_BLK)
                    kk, vv = k_ref[h, cur, :], v_ref[h, cur, :]
                    s = _dot(q, kk, NT) * scale + bias_ref[h, :, own]
                    p = jnp.where(_att_valid_first(), jnp.exp(s - l_ref[h, cur, :]), 0.0)
                    ds = p * (_dot(dov, vv, NT) - delta + dl_ref[h, cur, :])
                    ds_ref[h, :, own] += ds
                    dq_ref[h, cur, :] = _dot(ds, kk, NN) * scale
                    carry_k, carry_v = _dot(ds, q, TN) * scale, _dot(p, dov, TN)
                    continue
                prev = _blk(dil, r, n - 1)
                kk = jnp.concatenate([k_ref[h, prev, :], k_ref[h, cur, :]], axis=0)
                vv = jnp.concatenate([v_ref[h, prev, :], v_ref[h, cur, :]], axis=0)
                s = _dot(q, kk, NT) * scale + bias_ref[h]
                p = jnp.where(_att_valid(n), jnp.exp(s - wide(l_ref[h, cur, :])), 0.0)
                dp = _dot(dov, vv, NT)
                ds = p * (dp - delta + wide(dl_ref[h, cur, :]))
                ds_ref[h] += ds
                dq_ref[h, cur, :] = _dot(ds, kk, NN) * scale
                dkk = _dot(ds, q, TN) * scale
                dvv = _dot(p, dov, TN)
                dk_ref[h, prev, :] = carry_k + dkk[:ATT_BLK]
                dv_ref[h, prev, :] = carry_v + dvv[:ATT_BLK]
                carry_k, carry_v = dkk[ATT_BLK:], dvv[ATT_BLK:]
            last = _blk(dil, r, nb - 1)
            dk_ref[h, last, :] = carry_k
            dv_ref[h, last, :] = carry_v

    osh = jax.ShapeDtypeStruct((ATT_HG, S, ATT_DH), F32)
    kw = dict(name=f"att_bwd{gi}", grid=(per, dil), in_specs=_slab_specs(gi) + [_bias_spec(gi)] + _head_specs(gi, 4),
              out_specs=(*_head_specs(gi, 3), pl.BlockSpec((hp, ATT_BLK, 2 * ATT_BLK), lambda g, r: (g, 0, 0))),
              out_shape=(osh, osh, osh, jax.ShapeDtypeStruct((ATT_HG, ATT_BLK, 2 * ATT_BLK), F32)))
    args = (slabs, slabs, slabs, bias, o, lse, do, dlse)
    if comm is not None:
        return _carry(body, comm, **kw)(*args)
    return _pcall(body, compiler_params=_params(("arbitrary", "arbitrary")), **kw)(*args)


AW = ATT_HG * ATT_DH


def _mix_weights(l0, l1, l2):
    mx = jnp.maximum(jnp.maximum(l0, l1), l2)
    e0, e1, e2 = jnp.exp(l0 - mx), jnp.exp(l1 - mx), jnp.exp(l2 - mx)
    den = e0 + e1 + e2
    return e0 / den, e1 / den, e2 / den


def _heads_spec():
    return pl.BlockSpec((ATT_HG, TR, ATT_DH), lambda i: (0, i, 0))


def _mix_fwd(os_, ls, comm=None):
    def body(o0, o1, o2, l0, l1, l2, att_ref):
        for h in range(ATT_HG):
            w0, w1, w2 = _mix_weights(l0[h], l1[h], l2[h])
            att_ref[:, h * ATT_DH:(h + 1) * ATT_DH] = (w0 * o0[h] + w1 * o1[h] + w2 * o2[h]).astype(BF16)

    kw = dict(name="mix_fwd", grid=(S // TR,), in_specs=[_heads_spec()] * 6, out_specs=_row_spec(AW),
              out_shape=jax.ShapeDtypeStruct((S, AW), BF16))
    if comm is not None:
        return _carry(body, comm, **kw)(*os_, *ls)
    return _pcall(body, compiler_params=_params(("parallel",)), **kw)(*os_, *ls)


def _mix_bwd(os_, ls, datt):
    def body(o0, o1, o2, l0, l1, l2, da_ref, d0, d1, d2, e0, e1, e2):
        for h in range(ATT_HG):
            ws = _mix_weights(l0[h], l1[h], l2[h])
            da = da_ref[:, h * ATT_DH:(h + 1) * ATT_DH]
            dws = []
            for o_ref, w, d_ref in zip((o0, o1, o2), ws, (d0, d1, d2)):
                d_ref[h] = w * da
                dws.append(jnp.broadcast_to(jnp.sum(da * o_ref[h], axis=-1, keepdims=True), (TR, ATT_DH)))
            tot = ws[0] * dws[0] + ws[1] * dws[1] + ws[2] * dws[2]
            for w, dw, e_ref in zip(ws, dws, (e0, e1, e2)):
                e_ref[h] = w * (dw - tot)

    o = pltpu.HBM((ATT_HG, S, ATT_DH), F32)
    return _pcall(body, name="mix_bwd", grid=(S // TR,), in_specs=[_heads_spec()] * 6 + [_row_spec(AW)],
                  out_specs=(_heads_spec(),) * 6, out_shape=(o,) * 6,
                  compiler_params=_params(("parallel",)))(*os_, *ls, datt)


def _ada_fwd(c_all, w_sh, b_sl):
    def body(c_ref, w_ref, b_ref, o_ref):
        cv = c_ref[...]
        o_ref[...] = _dot(cv * jax.nn.sigmoid(cv), w_ref[...], NN) + b_ref[...]

    return _pcall(body, name="ada_fwd", out_shape=jax.ShapeDtypeStruct((N_DEV, w_sh.shape[1]), F32),
                  compiler_params=_params())(c_all, w_sh, b_sl)


CAST_STEPS = 4


def _to_bf16(arrs, comm, name):
    n = len(arrs)

    def body(*refs):
        for src, dst in zip(refs[:n], refs[n:]):
            dst[...] = src[...].astype(BF16)

    blocks = [pl.BlockSpec((a.shape[0] // CAST_STEPS, a.shape[1]), lambda i: (i, 0)) for a in arrs]
    return _carry(body, comm, name=name, grid=(CAST_STEPS,), in_specs=blocks, out_specs=tuple(blocks),
                  out_shape=tuple(pltpu.HBM(a.shape, BF16) for a in arrs))(*arrs)


def _ada_bwd(c_all, dm_sl):
    def body(c_ref, d_ref, o_ref):
        cv = c_ref[...]
        o_ref[...] = _dot(cv * jax.nn.sigmoid(cv), d_ref[...], TN)

    return _pcall(body, name="ada_bwd", out_shape=jax.ShapeDtypeStruct((D, dm_sl.shape[1]), F32),
                  compiler_params=_params())(c_all, dm_sl)


N_MOD = 6


def _sum_small(gathered, widths):
    def body(g_ref, gb_ref, dm_ref, *outs):
        def total(cols):
            acc = g_ref[0, :, cols]
            for e in range(1, N_DEV):
                acc = acc + g_ref[e, :, cols]
            return acc

        gb_ref[...] = total(slice(0, N_MOD * D))
        for e in range(N_DEV):
            dm_ref[e:e + 1, :] = g_ref[e, :, :N_MOD * D]
        col = N_MOD * D
        for w, o_ref in zip(widths, outs):
            o_ref[...] = total(slice(col, col + w))
            col += w

    assert gathered.shape == (N_DEV, 1, N_MOD * D + sum(widths))
    shapes = (jax.ShapeDtypeStruct((1, N_MOD * D), F32), jax.ShapeDtypeStruct((N_DEV, N_MOD * D), F32),
              *[jax.ShapeDtypeStruct((1, w), F32) for w in widths])
    res = _pcall(body, name="sum_small", out_shape=shapes, compiler_params=_params())(gathered)
    return res[0], res[1], res[2:]


def _row_tile(m, n):
    t = max(8, min(m, (1 << 19) // n // 8 * 8))
    while m % t:
        t -= 8
    return t


def _pair_sum(full, recv, sel, name, col_block=0):
    _, m, n = recv.shape
    t = m

    def body(sel_ref, a_ref, b_ref, o_ref):
        o_ref[...] = (a_ref[...].astype(F32) + b_ref[...].astype(F32)).astype(o_ref.dtype)

    gs = pltpu.PrefetchScalarGridSpec(
        num_scalar_prefetch=1, grid=(4, m // t),
        in_specs=[pl.BlockSpec((None, None, t, n), lambda q, i, s: (q, s[0], i, col_block)),
                  pl.BlockSpec((None, t, n), lambda q, i, s: (q, i, 0))],
        out_specs=pl.BlockSpec((None, t, n), lambda q, i, s: (q, i, 0)))
    return _pcall(body, name=name, grid_spec=gs, out_shape=pltpu.HBM((4, m, n), full.dtype),
                  compiler_params=_params(("parallel", "parallel")))(sel, full, recv)


def _chip_sum(part, recv, sel, name):
    _, m, n = part.shape
    t = _row_tile(m, n)

    def body(sel_ref, a_ref, r_ref, o_ref):
        o_ref[...] = ((a_ref[...].astype(F32) + r_ref[0].astype(F32)) + r_ref[1].astype(F32)) + r_ref[2].astype(F32)

    gs = pltpu.PrefetchScalarGridSpec(
        num_scalar_prefetch=1, grid=(m // t,),
        in_specs=[pl.BlockSpec((None, t, n), lambda i, s: (s[0], i, 0)),
                  pl.BlockSpec((3, t, n), lambda i, s: (0, i, 0))],
        out_specs=pl.BlockSpec((t, n), lambda i, s: (i, 0)))
    return _pcall(body, name=name, grid_spec=gs, out_shape=jax.ShapeDtypeStruct((m, n), F32),
                  compiler_params=_params(("parallel",)))(sel, part, recv)


def _adamw_math(w, g, m, v):
    nm = ADAM_B1 * m + (1.0 - ADAM_B1) * g
    nv = ADAM_B2 * v + (1.0 - ADAM_B2) * (g * g)
    m_hat = nm / (1.0 - ADAM_B1 ** ADAM_STEP)
    v_hat = nv / (1.0 - ADAM_B2 ** ADAM_STEP)
    return -ADAM_LR * (m_hat / (jnp.sqrt(v_hat) + ADAM_EPS) + ADAM_WD * w), nm, nv


def _adamw(w, g, m, v, name):
    _, rows, cols = w.shape
    t = _row_tile(rows, cols)

    def body(w_ref, g_ref, m_ref, v_ref, d_ref, nm_ref, nv_ref):
        d_ref[...], nm_ref[...], nv_ref[...] = _adamw_math(w_ref[...], g_ref[...], m_ref[...], v_ref[...])

    spec3 = pl.BlockSpec((None, t, cols), lambda i: (0, i, 0))
    spec2 = pl.BlockSpec((t, cols), lambda i: (i, 0))
    o = jax.ShapeDtypeStruct(w.shape, F32)
    return _pcall(body, name=name, grid=(rows // t,), in_specs=[spec3, spec2, spec3, spec3], out_specs=(spec3,) * 3,
                  out_shape=(o, o, o), compiler_params=_params(("parallel",)))(w, g, m, v)


def _adamw_reduced1(w, m, v, part, recv, sel, name):
    _, rows, cols = w.shape
    t = _row_tile(rows, cols)

    def body(sel_ref, w_ref, m_ref, v_ref, p_ref, r_ref, g_ref, d_ref, nm_ref, nv_ref):
        g = ((p_ref[...].astype(F32) + r_ref[0].astype(F32)) + r_ref[1].astype(F32)) + r_ref[2].astype(F32)
        g_ref[...] = g
        d_ref[...], nm_ref[...], nv_ref[...] = _adamw_math(w_ref[...], g, m_ref[...], v_ref[...])

    wspec = pl.BlockSpec((None, t, cols), lambda i, s: (0, i, 0))
    gs = pltpu.PrefetchScalarGridSpec(
        num_scalar_prefetch=1, grid=(rows // t,),
        in_specs=[wspec, wspec, wspec, pl.BlockSpec((None, t, cols), lambda i, s: (s[0], i, 0)),
                  pl.BlockSpec((3, t, cols), lambda i, s: (0, i, 0))],
        out_specs=(wspec,) * 4)
    o = jax.ShapeDtypeStruct(w.shape, F32)
    return _pcall(body, name=name, grid_spec=gs, out_shape=(o, o, o, o),
                  compiler_params=_params(("parallel",)))(sel, w, m, v, part, recv)


def _adamw_reduced(w, m, v, parts, recvs, sel):
    _, rows, cols = w.shape
    half = cols // 2
    t = _row_tile(rows, half)

    def body(sel_ref, w_ref, m_ref, v_ref, pa_ref, pb_ref, ra_ref, rb_ref, g_ref, d_ref, nm_ref, nv_ref):
        total = lambda p_ref, r_ref: ((p_ref[...].astype(F32) + r_ref[0].astype(F32)) + r_ref[1].astype(F32)) \
            + r_ref[2].astype(F32)
        g = jnp.where(pl.program_id(1) == 0, total(pa_ref, ra_ref), total(pb_ref, rb_ref))
        g_ref[...] = g
        d_ref[...], nm_ref[...], nv_ref[...] = _adamw_math(w_ref[...], g, m_ref[...], v_ref[...])

    wspec = pl.BlockSpec((None, t, half), lambda i, j, s: (0, i, j))
    pspec = pl.BlockSpec((None, t, half), lambda i, j, s: (s[0], i, 0))
    rspec = pl.BlockSpec((3, t, half), lambda i, j, s: (0, i, 0))
    gs = pltpu.PrefetchScalarGridSpec(num_scalar_prefetch=1, grid=(rows // t, 2),
                                      in_specs=[wspec, wspec, wspec, pspec, pspec, rspec, rspec],
                                      out_specs=(wspec,) * 4)
    o = jax.ShapeDtypeStruct(w.shape, F32)
    return _pcall(body, name="adamw_w_in", grid_spec=gs, out_shape=(o, o, o, o),
                  compiler_params=_params(("parallel", "arbitrary")))(sel, w, m, v, *parts, *recvs)


def _adamw_small(ws, gs, ms, vs):
    n = len(ws)

    def body(*refs):
        for i in range(n):
            w_ref, g_ref, m_ref, v_ref = (refs[k * n + i] for k in range(4))
            d, nm, nv = _adamw_math(w_ref[...], g_ref[...], m_ref[...], v_ref[...])
            refs[4 * n + i][...] = d
            refs[5 * n + i][...] = nm
            refs[6 * n + i][...] = nv

    shapes = tuple(jax.ShapeDtypeStruct(w.shape, F32) for w in ws)
    res = _pcall(body, name="adamw_small", out_shape=shapes * 3, compiler_params=_params())(*ws, *gs, *ms, *vs)
    return res[:n], res[n:2 * n], res[2 * n:]


def _mesh_pos():
    return lax.axis_index("x"), lax.axis_index("y"), lax.axis_index("c")


class _GatherSmallDirect:
    ins_in_vmem = True

    def __init__(self, arrs):
        self.ins = list(arrs)
        self.out_shape = tuple(jax.ShapeDtypeStruct((N_DEV,) + a.shape, a.dtype) for a in arrs)
        n = len(arrs)
        self.sems = [pltpu.SemaphoreType.DMA((7 * n,)), pltpu.SemaphoreType.DMA((7 * n,)),
                     pltpu.SemaphoreType.DMA((n,))]

    def _copies(self, ins, outs, sems):
        send_sems, recv_sems, local_sems = sems
        x, y, c = _mesh_pos()
        me = _slot((x, y, c))
        mine = [pltpu.make_async_copy(ins[p], outs[p].at[me], local_sems.at[p]) for p in range(len(self.ins))]
        sends, arrivals = [], []
        for p in range(len(self.ins)):
            for j in range(1, N_DEV):
                peer = (me + j) % N_DEV
                to = (peer // 4, (peer // 2) % 2, peer % 2)
                sends.append(pltpu.make_async_remote_copy(
                    src_ref=ins[p], dst_ref=outs[p].at[me], send_sem=send_sems.at[7 * p + j - 1],
                    recv_sem=recv_sems.at[7 * p + (N_DEV - j) - 1], device_id=to, device_id_type=MESH))
                arrivals.append(pltpu.make_async_remote_copy(
                    src_ref=ins[p], dst_ref=outs[p].at[peer], send_sem=send_sems.at[7 * p + j - 1],
                    recv_sem=recv_sems.at[7 * p + j - 1], device_id=to, device_id_type=MESH))
        return mine, sends, arrivals

    def start(self, ins, outs, sems):
        mine, sends, _ = self._copies(ins, outs, sems)
        for cp in mine + sends:
            cp.start()

    def finish(self, ins, outs, sems):
        mine, sends, arrivals = self._copies(ins, outs, sems)
        for cp in arrivals:
            cp.wait_recv()
        for cp in sends:
            cp.wait_send()
        for cp in mine:
            cp.wait()


class _ExchangeCore:
    def __init__(self, fulls, cols=None):
        self.ins = list(fulls)
        self.cols = cols
        width = lambda f: f.shape[3] if cols is None else cols[1]
        self.out_shape = tuple(jax.ShapeDtypeStruct((4, f.shape[2], width(f)), f.dtype) for f in fulls)
        self.sems = [pltpu.SemaphoreType.DMA((4 * len(fulls),)), pltpu.SemaphoreType.DMA((4 * len(fulls),))]

    def _copies(self, ins, outs, sems):
        send_sems, recv_sems = sems
        x, y, c = _mesh_pos()

        def src(a, q):
            ref = ins[a].at[q, 1 - c]
            return ref if self.cols is None else ref.at[:, pl.ds(*self.cols)]

        return [pltpu.make_async_remote_copy(
            src_ref=src(a, q), dst_ref=outs[a].at[q], send_sem=send_sems.at[4 * a + q],
            recv_sem=recv_sems.at[4 * a + q], device_id=(x, y, 1 - c), device_id_type=MESH)
            for a in range(len(self.ins)) for q in range(4)]

    def start(self, ins, outs, sems):
        for cp in self._copies(ins, outs, sems):
            cp.start()

    def finish(self, ins, outs, sems):
        for cp in self._copies(ins, outs, sems):
            cp.wait()


class _ExchangeChip:
    def __init__(self, parts):
        self.ins = list(parts)
        self.out_shape = tuple(jax.ShapeDtypeStruct((3,) + p.shape[1:], p.dtype) for p in parts)
        self.sems = [pltpu.SemaphoreType.DMA((3 * len(parts),)), pltpu.SemaphoreType.DMA((3 * len(parts),))]

    def _copies(self, ins, outs, sems):
        send_sems, recv_sems = sems
        x, y, c = _mesh_pos()
        chips = [(1 - x, y), (x, 1 - y), (1 - x, 1 - y)]
        return [pltpu.make_async_remote_copy(
            src_ref=ins[a].at[2 * px + py], dst_ref=outs[a].at[j], send_sem=send_sems.at[3 * a + j],
            recv_sem=recv_sems.at[3 * a + j], device_id=(px, py, c), device_id_type=MESH)
            for a in range(len(self.ins)) for j, (px, py) in enumerate(chips)]

    def start(self, ins, outs, sems):
        for cp in self._copies(ins, outs, sems):
            cp.start()

    def finish(self, ins, outs, sems):
        for cp in self._copies(ins, outs, sems):
            cp.wait()


HBM_ONLY = pl.BlockSpec(memory_space=pltpu.HBM)
SEM_SPEC = pl.BlockSpec(memory_space=pltpu.SEMAPHORE)
SIDE_EFFECT = pltpu.SideEffectType.DATAFLOW_SIDE_EFFECTING


def _chip_copies(p_refs, land_refs, send_sems, recv_sems):
    x, y, c = _mesh_pos()
    return [pltpu.make_async_remote_copy(
        src_ref=p_refs[a].at[2 * px + py], dst_ref=land_refs[a].at[j], send_sem=send_sems.at[3 * a + j],
        recv_sem=recv_sems.at[3 * a + j], device_id=(px, py, c), device_id_type=MESH)
        for a in range(len(p_refs)) for j, (px, py) in enumerate([(1 - x, y), (x, 1 - y), (1 - x, 1 - y)])]


def _chip_exchange_start(parts, name):
    n = len(parts)
    lands = [lax.empty((3,) + p.shape[1:], p.dtype) for p in parts]

    def body(*refs):
        p_refs, land_refs, (send_sems, recv_sems) = refs[:n], refs[n:2 * n], refs[2 * n:2 * n + 2]
        for cp in _chip_copies(p_refs, land_refs, send_sems, recv_sems):
            cp.start()
        token = refs[-1]
        token[...] = jnp.zeros_like(token)

    hbm = lambda t: pltpu.HBM(t.shape, t.dtype)
    res = pl.pallas_call(
        body, name=name,
        out_shape=(pltpu.SemaphoreType.DMA((3 * n,)), pltpu.SemaphoreType.DMA((3 * n,)), *[hbm(t) for t in parts + lands],
                   jax.ShapeDtypeStruct((8, 128), F32)),
        in_specs=(HBM_ONLY,) * (2 * n),
        out_specs=(SEM_SPEC, SEM_SPEC, *[HBM_ONLY] * (2 * n), pl.BlockSpec(memory_space=pltpu.VMEM)),
        input_output_aliases={i: 2 + i for i in range(2 * n)},
        compiler_params=pltpu.CompilerParams(has_side_effects=SIDE_EFFECT))(
        *[pltpu.with_memory_space_constraint(t, pltpu.HBM) for t in parts + lands])
    return (res[0], res[1], list(res[2:2 + n]), list(res[2 + n:2 + 2 * n])), res[-1]


def _chip_exchange_wait(in_flight, after, name):
    send_sems, recv_sems, parts, lands = in_flight
    n = len(parts)

    def body(*refs):
        p_refs, land_refs, (send_sems, recv_sems) = refs[:n], refs[n:2 * n], refs[2 * n:2 * n + 2]
        for cp in _chip_copies(p_refs, land_refs, send_sems, recv_sems):
            cp.wait_send()
            cp.wait_recv()

    res = pl.pallas_call(
        body, name=name, out_shape=tuple(pltpu.HBM(t.shape, t.dtype) for t in parts + lands),
        in_specs=(*[HBM_ONLY] * (2 * n), SEM_SPEC, SEM_SPEC, pl.BlockSpec(memory_space=pl.ANY)),
        out_specs=(HBM_ONLY,) * (2 * n), input_output_aliases={i: i for i in range(2 * n)},
        compiler_params=pltpu.CompilerParams(has_side_effects=SIDE_EFFECT))(*parts, *lands, send_sems, recv_sems, after)
    return list(res[:n]), list(res[n:])


def _slot(p):
    return 4 * p[0] + 2 * p[1] + p[2]


def _gather_copies(src_refs, out_refs, send_sems, recv_sems):
    x, y, c = _mesh_pos()
    targets = [(x, y, 1 - c), (1 - x, y, c), (x, 1 - y, c), (1 - x, 1 - y, c)]
    return [pltpu.make_async_remote_copy(
        src_ref=src_refs[a], dst_ref=out_refs[a].at[_slot((x, y, c))], send_sem=send_sems.at[4 * a + k],
        recv_sem=recv_sems.at[4 * a + k], device_id=to, device_id_type=MESH)
        for a in range(len(src_refs)) for k, to in enumerate(targets)]


def _gather_start(shards, after, name):
    n = len(shards)
    outs = [lax.empty((N_DEV,) + s.shape, s.dtype) for s in shards]

    def body(*refs):
        for cp in _gather_copies(refs[:n], refs[n:2 * n], refs[2 * n + 1], refs[2 * n + 2]):
            cp.start()
        token = refs[-1]
        token[...] = jnp.zeros_like(token)

    res = pl.pallas_call(
        body, name=name,
        out_shape=(pltpu.SemaphoreType.DMA((4 * n,)), pltpu.SemaphoreType.DMA((4 * n,)),
                   *[pltpu.HBM(t.shape, t.dtype) for t in shards + outs], jax.ShapeDtypeStruct((8, 128), F32)),
        in_specs=(*[HBM_ONLY] * (2 * n), pl.BlockSpec(memory_space=pl.ANY)),
        out_specs=(SEM_SPEC, SEM_SPEC, *[HBM_ONLY] * (2 * n), pl.BlockSpec(memory_space=pltpu.VMEM)),
        input_output_aliases={i: 2 + i for i in range(2 * n)},
        compiler_params=pltpu.CompilerParams(has_side_effects=SIDE_EFFECT))(
        *[pltpu.with_memory_space_constraint(t, pltpu.HBM) for t in shards + outs], after)
    return (res[0], res[1], list(res[2:2 + n]), list(res[2 + n:2 + 2 * n])), res[-1]


def _gather_wait(in_flight, after, name):
    send_sems, recv_sems, shards, outs = in_flight
    n = len(shards)

    def body(*refs):
        for cp in _gather_copies(refs[:n], refs[n:2 * n], refs[2 * n], refs[2 * n + 1]):
            cp.wait_send()
            cp.wait_recv()

    res = pl.pallas_call(
        body, name=name, out_shape=tuple(pltpu.HBM(t.shape, t.dtype) for t in shards + outs),
        in_specs=(*[HBM_ONLY] * (2 * n), SEM_SPEC, SEM_SPEC, pl.BlockSpec(memory_space=pl.ANY)),
        out_specs=(HBM_ONLY,) * (2 * n), input_output_aliases={i: i for i in range(2 * n)},
        compiler_params=pltpu.CompilerParams(has_side_effects=SIDE_EFFECT))(*shards, *outs, send_sems, recv_sems, after)
    return list(res[:n]), list(res[n:])


class _PassToSibling:
    def __init__(self, shards, gathered):
        n = self.n = len(shards)
        self.ins = list(shards) + list(gathered)
        self.out_shape = tuple(jax.ShapeDtypeStruct(g.shape, g.dtype) for g in gathered)
        self.aliases = {n + a: a for a in range(n)}
        self.sems = [pltpu.SemaphoreType.DMA((3 * n,)), pltpu.SemaphoreType.DMA((3 * n,)),
                     pltpu.SemaphoreType.DMA((n,))]

    def _copies(self, ins, outs, sems):
        send_sems, recv_sems, local_sems = sems
        x, y, c = _mesh_pos()
        chips = [(1 - x, y), (x, 1 - y), (1 - x, 1 - y)]
        mine = [pltpu.make_async_copy(ins[a], outs[a].at[_slot((x, y, c))], local_sems.at[a]) for a in range(self.n)]
        passed, awaited = [], []
        for a in range(self.n):
            for j, chip in enumerate(chips):
                sems_j = dict(send_sem=send_sems.at[3 * a + j], recv_sem=recv_sems.at[3 * a + j],
                              device_id=(x, y, 1 - c), device_id_type=MESH)
                blk = outs[a].at[_slot((*chip, c))]
                passed.append(pltpu.make_async_remote_copy(src_ref=blk, dst_ref=blk, **sems_j))
                got = outs[a].at[_slot((*chip, 1 - c))]
                awaited.append(pltpu.make_async_remote_copy(src_ref=got, dst_ref=got, **sems_j))
        return mine, passed, awaited

    def start(self, ins, outs, sems):
        mine, passed, _ = self._copies(ins, outs, sems)
        for cp in mine + passed:
            cp.start()

    def finish(self, ins, outs, sems):
        mine, passed, awaited = self._copies(ins, outs, sems)
        for cp in passed:
            cp.wait_send()
        for cp in awaited:
            cp.wait_recv()
        for cp in mine:
            cp.wait()


def _reduce_sums(fulls, recv_core, core, tag):
    return [_pair_sum(f, r, core, f"rs_pair_{tag}{i}") for i, (f, r) in enumerate(zip(fulls, recv_core))]


def _local_step(x, tgt, mods, w_in_shard, order, shards, small, chip, core):
    sh1, sc1, g1, sh2, sc2, g2 = mods
    norm1_g, rel_bias, gn_g, gn_b, norm2_g, norm_f_g = small
    tables = _ret_tables()
    buckets = jnp.asarray(_bucket_tables())

    proj, slabs, w_in_t, h1 = _gather_proj(x, norm1_g, sh1, sc1, w_in_shard, order)
    flight_w1, token_w = _gather_start(list(shards[:3]), proj, "gather_w1_start")
    flight_w2, token_w = _gather_start(list(shards[3:]), token_w, "gather_w2_start")
    bias = _bias_build(rel_bias, buckets, token_w)
    outs, lses = [], []
    for gi in range(len(ATT_GROUPS)):
        o, l = _att_fwd(slabs, bias, gi)
        outs.append(o)
        lses.append(l)
    att, gathered = _mix_fwd(outs, lses, comm=_PassToSibling(*_gather_wait(flight_w1, lses[2], "gather_w1_wait")))
    w_ret_out, w_att_out, w_o = (_from_slots(g, ax) for g, ax in zip(gathered, BIG_AXES[1:4]))
    gated, ro, states = _ret_fwd(proj, tables, gn_g, gn_b, att)
    ret_out, gathered = _mm(gated, w_ret_out, 'nn', tm=S, tn=256, tk=2048, name="ret_out",
                            comm=_PassToSibling(*_gather_wait(flight_w2, gated, "gather_w2_wait")))
    w_ff1, w_ff2 = (_from_slots(g, ax) for g, ax in zip(gathered, BIG_AXES[4:]))
    att_out, merged = _att_out_merge(att, w_att_out, proj, ret_out)
    mixo, x1, h2 = _w_o_norm2(merged, w_o, x, g1, norm2_g, sh2, sc2)
    u, act = _mm(h2, w_ff1, 'nn', tm=S, tn=512, tk=D, name="ff1", relu2=True)
    loss, dx2, g_normf, df, dg2 = _ff2_final(act, w_ff2, x1, g2, tgt, norm_f_g)

    gw_ff2 = _mm(act, df, 'tn', tm=512, tn=D, tk=S, name="gw_ff2", out_dtype=BF16)
    du = _mm(df, w_ff2, 'nt', tm=S, tn=512, tk=D, name="d_act", out_dtype=BF16, relu2_of=u)
    gw_ff1 = _mm(h2, du, 'tn', tm=D, tn=512, tk=S, name="gw_ff1", out_dtype=BF16)
    fulls_a = [_to_slots(g, ax) for g, ax in zip((gw_ff1, gw_ff2), BIG_AXES[4:])]
    dh2, recv_core_a = _mm(du, w_ff1, 'nt', tm=1024, tn=1024, tk=2048, name="dh2", comm=_ExchangeCore(fulls_a))
    parts_a = _reduce_sums(fulls_a, recv_core_a, core, "a")
    flight_a, token_a = _chip_exchange_start(parts_a, "rs_a_start")
    dx1, dsc2, dsh2, g_norm2, dmixo, dg1 = _norm_mod_bwd(x1, norm2_g, sc2, dh2, dx2, "norm2_bwd", gate=(mixo, g1))

    gw_o = _mm(merged, dmixo, 'tn', tm=D, tn=512, tk=S, name="gw_o", out_dtype=BF16, after=token_a)
    d_ret_out, d_att_out, dga, dgb = _dmerged_split(dmixo, w_o, proj, ret_out, att_out)
    gw_ret_out = _mm(gated, d_ret_out, 'tn', tm=512, tn=D, tk=S, name="gw_ret_out", out_dtype=BF16)
    gw_att_out = _mm(att, d_att_out, 'tn', tm=AW, tn=D, tk=S, name="gw_att_out", out_dtype=BF16)
    fulls_b = [_to_slots(g, ax) for g, ax in zip((gw_ret_out, gw_att_out, gw_o), BIG_AXES[1:4])]
    dgated, recv_core_b = _mm(d_ret_out, w_ret_out, 'nt', tm=S, tn=512, tk=D, name="dgated",
                              comm=_ExchangeCore(fulls_b))
    parts_b = _reduce_sums(fulls_b, recv_core_b, core, "b")
    flight_b, token_b = _chip_exchange_start(parts_b, "rs_b_start")
    datt = _mm(d_att_out, w_att_out, 'nt', tm=S, tn=AW, tk=D, name="datt", after=token_b)
    mix_grads = _mix_bwd(outs, lses, datt)
    datt_parts, ds_sums = [], []
    for gi in range(len(ATT_GROUPS)):
        dq, dk, dv, ds_sum = _att_bwd(slabs, bias, outs[gi], lses[gi], mix_grads[gi], mix_grads[3 + gi], gi)
        datt_parts += [dq, dk, dv]
        ds_sums.append(ds_sum)
    g_bias = _bias_grad(jnp.concatenate(ds_sums, axis=0), buckets)[:, :, 0].T.reshape(1, -1)
    dproj, g_gn_g, g_gn_b = _ret_bwd(proj, tables, gn_g, gn_b, ro, states, dgated, datt_parts + [dga, dgb])
    parts_a, recv_chip_a = _chip_exchange_wait(flight_a, dproj, "rs_a_wait")
    parts_b, recv_chip_b = _chip_exchange_wait(flight_b, dproj, "rs_b_wait")
    reduced = list(zip(parts_b + parts_a, recv_chip_b + recv_chip_a))
    full_in = _to_slots(_mm(dproj, h1, 'tn', tm=512, tn=D, tk=S, name="gw_in", out_dtype=BF16), 0)
    halves = [(half * (D // 2), D // 2) for half in range(2)]
    (recv_core_in0,) = _run_comm(_ExchangeCore([full_in], cols=halves[0]), "rs_core_in0")
    flight0, token = _chip_exchange_start([_pair_sum(full_in, recv_core_in0, core, "rs_pair_c0", col_block=0)],
                                          "rs_in0_start")
    dh1, (recv_core_in1,) = _mm(dproj, w_in_t, 'nn', tm=1024, tn=1024, tk=2560, name="dh1",
                                comm=_ExchangeCore([full_in], cols=halves[1]), after=token)
    flight1, token = _chip_exchange_start([_pair_sum(full_in, recv_core_in1, core, "rs_pair_c1", col_block=1)],
                                          "rs_in1_start")
    in_flight = [flight0, flight1]
    gx, dsc1, dsh1, g_norm1 = _norm_mod_bwd(x, norm1_g, sc1, dh1, dx1, "norm1_bwd", after=token)

    dmod = [dsh1, dsc1, dg1, dsh2, dsc2, dg2]
    small_g = [g_norm1, g_bias, g_gn_g, g_gn_b, g_norm2, g_normf]
    return loss, gx, in_flight, reduced, small_g, dmod


def _to_slots(g, axis):
    if axis == 0:
        return g.reshape(4, 2, g.shape[0] // N_DEV, g.shape[1])
    return g.reshape(g.shape[0], N_DEV, g.shape[1] // N_DEV).transpose(1, 0, 2).reshape(4, 2, g.shape[0], -1)


def _from_slots(w8, axis):
    if axis == 0:
        return w8.reshape(-1, w8.shape[2])
    return w8.transpose(1, 0, 2).reshape(w8.shape[1], -1)


BIG_AXES = (1, 0, 1, 0, 1, 0)


def kernel(x, c, w_ada, b_ada, norm1_g, w_in, rel_bias, ret_gn_g, ret_gn_b, w_ret_out, w_att_out, w_o, norm2_g, w_ff1, w_ff2, norm_f_g, loss_target, m_w_ada, m_b_ada, m_norm1_g, m_w_in, m_rel_bias, m_ret_gn_g, m_ret_gn_b, m_w_ret_out, m_w_att_out, m_w_o, m_norm2_g, m_w_ff1, m_w_ff2, m_norm_f_g, v_w_ada, v_b_ada, v_norm1_g, v_w_in, v_rel_bias, v_ret_gn_g, v_ret_gn_b, v_w_ret_out, v_w_att_out, v_w_o, v_norm2_g, v_w_ff1, v_w_ff2, v_norm_f_g):
    mx, my, mc = _mesh_pos()
    dev = 4 * mx + 2 * my + mc
    chip = jnp.reshape(2 * mx + my, (1,)).astype(jnp.int32)
    core = jnp.reshape(mc, (1,)).astype(jnp.int32)
    ada_w = D * 6 // N_DEV

    w_in, m_w_in, v_w_in = (jnp.transpose(t, (0, 2, 1)) for t in (w_in, m_w_in, v_w_in))

    (w_in_shard,), (c_all,) = _to_bf16([w_in[0]], _GatherSmallDirect([c]), "gather_c")
    c_all = c_all.reshape(N_DEV, D)
    b_sl = lax.dynamic_slice(b_ada, (0, dev * ada_w), (1, ada_w))
    other_shards, (mod_all,) = _to_bf16([w[0] for w in (w_ret_out, w_att_out, w_o, w_ff1, w_ff2)],
                                        _GatherSmallDirect([_ada_fwd(c_all, w_ada[0], b_sl)]), "gather_mod")
    mod = lax.dynamic_index_in_dim(mod_all, dev, axis=1, keepdims=False).reshape(6, D)
    mods = tuple(mod[i:i + 1] for i in range(6))

    small = (norm1_g, rel_bias, ret_gn_g, ret_gn_b, norm2_g, norm_f_g.reshape(1, D))
    order = lax.dynamic_index_in_dim(jnp.asarray(_proj_order()), 2 * mx + my, axis=0, keepdims=False)
    loss, gx, in_flight, big_red, small_g, dmod = _local_step(x[0], loss_target[0], mods, w_in_shard, order,
                                                              list(other_shards), small, chip, core)

    names = ['w_ada', 'b_ada', 'norm1_g', 'w_in', 'rel_bias', 'ret_gn_g', 'ret_gn_b', 'w_ret_out', 'w_att_out',
             'w_o', 'norm2_g', 'w_ff1', 'w_ff2', 'norm_f_g']
    ws = dict(zip(names, (w_ada, b_ada, norm1_g, w_in, rel_bias, ret_gn_g, ret_gn_b, w_ret_out, w_att_out, w_o,
                          norm2_g, w_ff1, w_ff2, norm_f_g)))
    ms = dict(zip(names, (m_w_ada, m_b_ada, m_norm1_g, m_w_in, m_rel_bias, m_ret_gn_g, m_ret_gn_b, m_w_ret_out,
                          m_w_att_out, m_w_o, m_norm2_g, m_w_ff1, m_w_ff2, m_norm_f_g)))
    vs = dict(zip(names, (v_w_ada, v_b_ada, v_norm1_g, v_w_in, v_rel_bias, v_ret_gn_g, v_ret_gn_b, v_w_ret_out,
                          v_w_att_out, v_w_o, v_norm2_g, v_w_ff1, v_w_ff2, v_norm_f_g)))
    grads, delta, new_m, new_v = {}, {}, {}, {}
    big_names = ('w_ret_out', 'w_att_out', 'w_o', 'w_ff1', 'w_ff2')
    for n, (part, recv) in zip(big_names, big_red):
        grads[n], delta[n], new_m[n], new_v[n] = _adamw_reduced1(ws[n], ms[n], vs[n], part, recv, chip, "adamw_" + n)
    updated = lax.optimization_barrier((gx, tuple(delta[n] for n in big_names)))
    rows = dmod + small_g + [loss]
    (gathered,) = _run_comm(_GatherSmallDirect([jnp.concatenate(rows, axis=1)]), "gather_small", after=updated[0])
    g_b_ada, dmod_all, (g_norm1, g_bias, g_gn_g, g_gn_b, g_norm2, g_normf, loss_sum) = _sum_small(
        gathered, [r.shape[1] for r in rows[N_MOD:]])
    loss_out = loss_sum[0, 0]
    g_w_ada = _ada_bwd(c_all, lax.dynamic_slice(dmod_all, (0, dev * ada_w), (N_DEV, ada_w)))

    grads.update(w_ada=g_w_ada.reshape(w_ada.shape), b_ada=g_b_ada, norm1_g=g_norm1, rel_bias=g_bias,
                 ret_gn_g=g_gn_g, ret_gn_b=g_gn_b, norm2_g=g_norm2, norm_f_g=g_normf)
    delta['w_ada'], new_m['w_ada'], new_v['w_ada'] = _adamw(w_ada, g_w_ada, m_w_ada, v_w_ada, "adamw_w_ada")
    small_names = ('b_ada', 'norm1_g', 'rel_bias', 'ret_gn_g', 'ret_gn_b', 'norm2_g', 'norm_f_g')
    two_d = {n: (1, ws[n].size) if ws[n].ndim == 1 else ws[n].shape for n in small_names}
    d_, m_, v_ = _adamw_small(*[[src[n].reshape(two_d[n]) for n in small_names] for src in (ws, grads, ms, vs)])
    for i, n in enumerate(small_names):
        shp = ws[n].shape
        delta[n], new_m[n], new_v[n] = d_[i].reshape(shp), m_[i].reshape(shp), v_[i].reshape(shp)
        grads[n] = grads[n].reshape(shp)

    done = lax.optimization_barrier((gx, tuple(d_), tuple(delta[n] for n in ('w_ada', 'w_ret_out', 'w_att_out', 'w_o',
                                                                               'w_ff1', 'w_ff2'))))
    parts_in, recvs_in = [], []
    for half, flight in enumerate(in_flight):
        (part_in,), (recv_chip_in,) = _chip_exchange_wait(flight, done[0], f"rs_in{half}_wait")
        parts_in.append(part_in)
        recvs_in.append(recv_chip_in)
    grads['w_in'], delta['w_in'], new_m['w_in'], new_v['w_in'] = _adamw_reduced(w_in, m_w_in, v_w_in, parts_in,
                                                                               recvs_in, chip)
    for d in (grads, delta, new_m, new_v):
        d['w_in'] = jnp.transpose(d['w_in'], (0, 2, 1))
    return (loss_out, gx[None], *[grads[n] for n in names], *[delta[n] for n in names],
            *[new_m[n] for n in names], *[new_v[n] for n in names])
```
